```python
import math
import jax, jax.numpy as jnp
from jax import lax
import numpy as np

D_MODEL = 1024
BATCH = 32
SEQ = 2048
DEPTH = 2

HEAD_DIM = 64
BLOCK = 128
LRU_WIDTH = D_MODEL // 2
LRU_BLOCKS = 8
LRU_BLOCK = LRU_WIDTH // LRU_BLOCKS
CONV_WIDTH = 4
LRU_C = 8.0
FOX_HEADS = (D_MODEL // 2) // HEAD_DIM
DIL_HEADS = (D_MODEL // 2) // HEAD_DIM
DIL_PATTERN = ((128, 1), (512, 4), (2048, 16))
SWA_Q_HEADS = (D_MODEL // 2) // HEAD_DIM
SWA_KV_HEADS = 2
SWA_WINDOW = 128
MEM_LEN = 256
XA_HEADS = 4
XA_HEAD_DIM = D_MODEL // XA_HEADS
D_FF = 4 * D_MODEL
ROPE_THETA = 10000.0
EPS = 1e-6

N_EVEN = (DEPTH + 1) // 2
N_ODD = DEPTH // 2
FOX_W = FOX_HEADS * HEAD_DIM
DIL_W = DIL_HEADS * HEAD_DIM
SWA_QW = SWA_Q_HEADS * HEAD_DIM
SWA_KW = SWA_KV_HEADS * HEAD_DIM
AB_IN = 2 * LRU_WIDTH + 3 * FOX_W + FOX_HEADS
AB_MIX = LRU_WIDTH + FOX_W
CD_IN = 3 * DIL_W + SWA_QW + 2 * SWA_KW
CD_MIX = DIL_W + SWA_QW

kernel_name = "hybrid_rglru_fox_dilated_swa_trunk"


def rmsnorm(x, g):
    xf = x.astype(jnp.float32)
    y = xf * lax.rsqrt(jnp.mean(xf * xf, axis=-1, keepdims=True) + EPS)
    return (y * g.astype(jnp.float32)).astype(x.dtype)


def rope(x, pos):
    half = x.shape[-1] // 2
    inv = ROPE_THETA ** (-jnp.arange(half, dtype=jnp.float32) / half)
    ang = pos.astype(jnp.float32)[:, None] * inv[None, :]
    cos = jnp.cos(ang)[None, :, None, :]
    sin = jnp.sin(ang)[None, :, None, :]
    xf = x.astype(jnp.float32)
    x1, x2 = xf[..., :half], xf[..., half:]
    return jnp.concatenate([x1 * cos - x2 * sin, x2 * cos + x1 * sin], axis=-1).astype(x.dtype)


def causal_depthwise_conv(u, w, b):
    out = lax.conv_general_dilated(
        u, w[:, None, :].astype(u.dtype), window_strides=(1,),
        padding=[(CONV_WIDTH - 1, 0)], dimension_numbers=("NWC", "WIO", "NWC"),
        feature_group_count=u.shape[-1])
    return out + b.astype(u.dtype)


def _linear_recurrence_combine(e1, e2):
    a1, b1 = e1
    a2, b2 = e2
    return a1 * a2, a2 * b1 + b2


def rg_lru(u, w_a, b_a, w_i, b_i, lam):
    bsz, s, c = u.shape
    uf = u.astype(jnp.float32)
    ub = uf.reshape(bsz, s, LRU_BLOCKS, LRU_BLOCK)
    r = jax.nn.sigmoid(jnp.einsum("bsgc,gcd->bsgd", ub, w_a.astype(jnp.float32)).reshape(bsz, s, c) + b_a.astype(jnp.float32))
    i = jax.nn.sigmoid(jnp.einsum("bsgc,gcd->bsgd", ub, w_i.astype(jnp.float32)).reshape(bsz, s, c) + b_i.astype(jnp.float32))
    log_a = -LRU_C * r * jax.nn.softplus(-lam.astype(jnp.float32))
    a = jnp.exp(log_a)
    x_in = jnp.sqrt(-jnp.expm1(2.0 * log_a)) * (i * uf)
    _, h = lax.associative_scan(_linear_recurrence_combine, (a, x_in), axis=1)
    return h.astype(u.dtype)


def forgetting_attention(q, k, v, log_f):
    s_len = q.shape[1]
    scale = q.shape[-1] ** -0.5
    c = jnp.cumsum(log_f, axis=1).transpose(0, 2, 1)
    outs = []
    for blk in range(s_len // BLOCK):
        q0, q1 = blk * BLOCK, (blk + 1) * BLOCK
        sc = jnp.einsum("bqhd,bkhd->bhqk", q[:, q0:q1], k[:, :q1]).astype(jnp.float32) * scale
        bias = c[:, :, q0:q1, None] - c[:, :, None, :q1]
        mask = jnp.arange(q0, q1)[:, None] >= jnp.arange(q1)[None, :]
        p = jax.nn.softmax(jnp.where(mask, sc + bias, -jnp.inf), axis=-1)
        outs.append(jnp.einsum("bhqk,bkhd->bqhd", p, v[:, :q1].astype(jnp.float32)))
    return jnp.concatenate(outs, axis=1).astype(q.dtype)


def banded_partial(q, k, v, max_dist):
    n, l, hk, g, dh = q.shape
    nb = -(-l // BLOCK)
    lp = nb * BLOCK
    pad = lp - l
    qb = jnp.pad(q, ((0, 0), (0, pad), (0, 0), (0, 0), (0, 0))).reshape(n, nb, BLOCK, hk, g, dh)
    kp = jnp.pad(k, ((0, 0), (BLOCK, pad), (0, 0), (0, 0))).reshape(n, nb + 1, BLOCK, hk, dh)
    vp = jnp.pad(v, ((0, 0), (BLOCK, pad), (0, 0), (0, 0))).reshape(n, nb + 1, BLOCK, hk, dh)
    kb = jnp.concatenate([kp[:, :-1], kp[:, 1:]], axis=2)
    vb = jnp.concatenate([vp[:, :-1], vp[:, 1:]], axis=2)
    sc = jnp.einsum("nbqhgd,nbkhd->nbhgqk", qb, kb).astype(jnp.float32) * dh ** -0.5
    qpos = jnp.arange(BLOCK)[:, None] + BLOCK
    kpos = jnp.arange(2 * BLOCK)[None, :]
    dist = qpos - kpos
    blk = jnp.arange(nb)[:, None, None]
    valid = (dist >= 0) & (dist <= max_dist) & (blk * BLOCK - BLOCK + kpos >= 0)
    sc = jnp.where(valid[None, :, None, None], sc, -jnp.inf)
    m = jnp.max(sc, axis=-1)
    p = jnp.exp(sc - m[..., None])
    den = jnp.sum(p, axis=-1)
    num = jnp.einsum("nbhgqk,nbkhd->nbqhgd", p, vb.astype(jnp.float32)).reshape(n, lp, hk, g, dh)[:, :l]
    m = m.transpose(0, 1, 4, 2, 3).reshape(n, lp, hk, g)[:, :l]
    den = den.transpose(0, 1, 4, 2, 3).reshape(n, lp, hk, g)[:, :l]
    return num, m, den


def dilated_attention(q, k, v):
    bsz, s, h, dh = q.shape
    nums, ms, dens = [], [], []
    for window, dil in DIL_PATTERN:
        l = s // dil

        def to_classes(t):
            return t.reshape(bsz, l, dil, h, dh).transpose(0, 2, 1, 3, 4).reshape(bsz * dil, l, h, dh)

        num, m, den = banded_partial(to_classes(q)[:, :, :, None, :], to_classes(k), to_classes(v), window // dil)
        nums.append(num[:, :, :, 0].reshape(bsz, dil, l, h, dh).transpose(0, 2, 1, 3, 4).reshape(bsz, s, h, dh))
        ms.append(m[..., 0].reshape(bsz, dil, l, h).transpose(0, 2, 1, 3).reshape(bsz, s, h))
        dens.append(den[..., 0].reshape(bsz, dil, l, h).transpose(0, 2, 1, 3).reshape(bsz, s, h))
    m_tot = ms[0]
    for m in ms[1:]:
        m_tot = jnp.maximum(m_tot, m)
    num_tot = sum(nm * jnp.exp(m - m_tot)[..., None] for nm, m in zip(nums, ms))
    den_tot = sum(dn * jnp.exp(m - m_tot) for dn, m in zip(dens, ms))
    return (num_tot / den_tot[..., None]).astype(q.dtype)


def sliding_window_sink_attention(q, k, v, sink):
    bsz, s, hq, dh = q.shape
    hk = k.shape[2]
    g = hq // hk
    num, m, den = banded_partial(q.reshape(bsz, s, hk, g, dh), k, v, SWA_WINDOW - 1)
    sink_g = sink.astype(jnp.float32).reshape(hk, g)
    m2 = jnp.maximum(m, sink_g)
    w = jnp.exp(m - m2)
    out = num * w[..., None] / (den * w + jnp.exp(sink_g - m2))[..., None]
    return out.reshape(bsz, s, hq, dh).astype(q.dtype)


def mixer_rglru_fox(h, w_in, conv_w, conv_b, w_a, b_a, w_i, b_i, lam, b_f, w_out):
    bsz, s, _ = h.shape
    z = h @ w_in
    u, gate, q, k, v, f_logit = jnp.split(
        z, [LRU_WIDTH, 2 * LRU_WIDTH, 2 * LRU_WIDTH + FOX_W, 2 * LRU_WIDTH + 2 * FOX_W, 2 * LRU_WIDTH + 3 * FOX_W], axis=-1)
    y_a = rg_lru(causal_depthwise_conv(u, conv_w, conv_b), w_a, b_a, w_i, b_i, lam) * jax.nn.gelu(gate)
    log_f = jax.nn.log_sigmoid(f_logit.astype(jnp.float32) + b_f.astype(jnp.float32))
    shp = (bsz, s, FOX_HEADS, HEAD_DIM)
    y_b = forgetting_attention(q.reshape(shp), k.reshape(shp), v.reshape(shp), log_f).reshape(bsz, s, FOX_W)
    return jnp.concatenate([y_a, y_b], axis=-1) @ w_out


def mixer_dilated_swa(h, w_in, sink, w_out):
    bsz, s, _ = h.shape
    pos = jnp.arange(s)
    z = h @ w_in
    qc, kc, vc, qd, kd, vd = jnp.split(
        z, [DIL_W, 2 * DIL_W, 3 * DIL_W, 3 * DIL_W + SWA_QW, 3 * DIL_W + SWA_QW + SWA_KW], axis=-1)
    dshp = (bsz, s, DIL_HEADS, HEAD_DIM)
    y_c = dilated_attention(rope(qc.reshape(dshp), pos), rope(kc.reshape(dshp), pos), vc.reshape(dshp))
    kshp = (bsz, s, SWA_KV_HEADS, HEAD_DIM)
    y_d = sliding_window_sink_attention(rope(qd.reshape(bsz, s, SWA_Q_HEADS, HEAD_DIM), pos),
                                        rope(kd.reshape(kshp), pos), vd.reshape(kshp), sink)
    return jnp.concatenate([y_c.reshape(bsz, s, DIL_W), y_d.reshape(bsz, s, SWA_QW)], axis=-1) @ w_out


def memory_cross_attention(h, mem, g_mem, w_q, w_kv, w_o):
    bsz, s, _ = h.shape
    mem_n = rmsnorm(mem, g_mem)
    q = (h @ w_q).reshape(bsz, s, XA_HEADS, XA_HEAD_DIM)
    k, v = jnp.split(mem_n @ w_kv, 2, axis=-1)
    k = k.reshape(bsz, -1, XA_HEADS, XA_HEAD_DIM)
    v = v.reshape(bsz, -1, XA_HEADS, XA_HEAD_DIM)
    sc = jnp.einsum("bqhd,bkhd->bhqk", q, k).astype(jnp.float32) * XA_HEAD_DIM ** -0.5
    p = jax.nn.softmax(sc, axis=-1)
    o = jnp.einsum("bhqk,bkhd->bqhd", p, v.astype(jnp.float32)).astype(h.dtype)
    return o.reshape(bsz, s, D_MODEL) @ w_o


def squared_relu_mlp(h, w_up, w_down):
    return jnp.square(jax.nn.relu(h @ w_up)) @ w_down


def _fwd_setup_inputs(seed: int = 0) -> dict:
    key = jax.random.key(seed)
    ks = iter(jax.random.split(key, 40))
    f32 = jnp.float32

    def nrm(shape, scale):
        return scale * jax.random.normal(next(ks), shape, f32)

    def gain(shape):
        return 1.0 + 0.05 * jax.random.normal(next(ks), shape, f32)

    a8 = jax.random.uniform(next(ks), (N_EVEN, LRU_WIDTH), f32, 0.9, 0.999)
    a0 = a8 ** (1.0 / LRU_C)
    lru_lambda = jnp.log(a0) - jnp.log1p(-a0)
    return {
        "x": nrm((BATCH, SEQ, D_MODEL), 1.0),
        "mem": nrm((BATCH, MEM_LEN, D_MODEL), 1.0),
        "ab_norm": gain((N_EVEN, D_MODEL)),
        "ab_w_in": nrm((N_EVEN, D_MODEL, AB_IN), D_MODEL ** -0.5),
        "ab_conv_w": nrm((N_EVEN, CONV_WIDTH, LRU_WIDTH), CONV_WIDTH ** -0.5),
        "ab_conv_b": nrm((N_EVEN, LRU_WIDTH), 0.05),
        "lru_w_a": nrm((N_EVEN, LRU_BLOCKS, LRU_BLOCK, LRU_BLOCK), LRU_BLOCK ** -0.5),
        "lru_b_a": nrm((N_EVEN, LRU_WIDTH), 0.05),
        "lru_w_i": nrm((N_EVEN, LRU_BLOCKS, LRU_BLOCK, LRU_BLOCK), LRU_BLOCK ** -0.5),
        "lru_b_i": nrm((N_EVEN, LRU_WIDTH), 0.05),
        "lru_lambda": lru_lambda,
        "fox_b_f": jax.random.uniform(next(ks), (N_EVEN, FOX_HEADS), f32, 1.0, 4.0),
        "ab_w_out": nrm((N_EVEN, AB_MIX, D_MODEL), AB_MIX ** -0.5),
        "cd_norm": gain((N_ODD, D_MODEL)),
        "cd_w_in": nrm((N_ODD, D_MODEL, CD_IN), D_MODEL ** -0.5),
        "cd_sink": nrm((N_ODD, SWA_Q_HEADS), 0.5),
        "cd_w_out": nrm((N_ODD, CD_MIX, D_MODEL), CD_MIX ** -0.5),
        "xa_norm": gain((DEPTH, D_MODEL)),
        "xa_mem_norm": gain((DEPTH, D_MODEL)),
        "xa_w_q": nrm((DEPTH, D_MODEL, D_MODEL), D_MODEL ** -0.5),
        "xa_w_kv": nrm((DEPTH, D_MODEL, 2 * D_MODEL), D_MODEL ** -0.5),
        "xa_w_o": nrm((DEPTH, D_MODEL, D_MODEL), D_MODEL ** -0.5),
        "mlp_norm": gain((DEPTH, D_MODEL)),
        "mlp_w_up": nrm((DEPTH, D_MODEL, D_FF), D_MODEL ** -0.5),
        "mlp_w_down": nrm((DEPTH, D_FF, D_MODEL), D_FF ** -0.5),
        "final_norm": gain((D_MODEL,)),
    }


def _fwd_reference(x, mem, ab_norm, ab_w_in, ab_conv_w, ab_conv_b, lru_w_a, lru_b_a, lru_w_i, lru_b_i,
              lru_lambda, fox_b_f, ab_w_out, cd_norm, cd_w_in, cd_sink, cd_w_out,
              xa_norm, xa_mem_norm, xa_w_q, xa_w_kv, xa_w_o, mlp_norm, mlp_w_up, mlp_w_down, final_norm):
    h = x
    for layer in range(DEPTH):
        j = layer // 2
        if layer % 2 == 0:
            h = h + mixer_rglru_fox(rmsnorm(h, ab_norm[j]), ab_w_in[j], ab_conv_w[j], ab_conv_b[j],
                                    lru_w_a[j], lru_b_a[j], lru_w_i[j], lru_b_i[j], lru_lambda[j],
                                    fox_b_f[j], ab_w_out[j])
        else:
            h = h + mixer_dilated_swa(rmsnorm(h, cd_norm[j]), cd_w_in[j], cd_sink[j], cd_w_out[j])
        h = h + memory_cross_attention(rmsnorm(h, xa_norm[layer]), mem, xa_mem_norm[layer],
                                       xa_w_q[layer], xa_w_kv[layer], xa_w_o[layer])
        h = h + squared_relu_mlp(rmsnorm(h, mlp_norm[layer]), mlp_w_up[layer], mlp_w_down[layer])
    return rmsnorm(h, final_norm)


import jax as _jax
import jax.numpy as _jnp

TWIN_FORMAT = 'train_step'
FWD_PARAMS = ['x', 'mem', 'ab_norm', 'ab_w_in', 'ab_conv_w', 'ab_conv_b', 'lru_w_a', 'lru_b_a', 'lru_w_i', 'lru_b_i', 'lru_lambda', 'fox_b_f', 'ab_w_out', 'cd_norm', 'cd_w_in', 'cd_sink', 'cd_w_out', 'xa_norm', 'xa_mem_norm', 'xa_w_q', 'xa_w_kv', 'xa_w_o', 'mlp_norm', 'mlp_w_up', 'mlp_w_down', 'final_norm']
TWIN_WEIGHTS = ['ab_norm', 'ab_w_in', 'ab_conv_w', 'ab_conv_b', 'lru_w_a', 'lru_b_a', 'lru_w_i', 'lru_b_i', 'lru_lambda', 'fox_b_f', 'ab_w_out', 'cd_norm', 'cd_w_in', 'cd_sink', 'cd_w_out', 'xa_norm', 'xa_mem_norm', 'xa_w_q', 'xa_w_kv', 'xa_w_o', 'mlp_norm', 'mlp_w_up', 'mlp_w_down', 'final_norm']
TWIN_DIFF_INPUT = 'x'
TWIN_INPUTS = ['x', 'mem', 'ab_norm', 'ab_w_in', 'ab_conv_w', 'ab_conv_b', 'lru_w_a', 'lru_b_a', 'lru_w_i', 'lru_b_i', 'lru_lambda', 'fox_b_f', 'ab_w_out', 'cd_norm', 'cd_w_in', 'cd_sink', 'cd_w_out', 'xa_norm', 'xa_mem_norm', 'xa_w_q', 'xa_w_kv', 'xa_w_o', 'mlp_norm', 'mlp_w_up', 'mlp_w_down', 'final_norm', 'loss_target', 'm_ab_norm', 'm_ab_w_in', 'm_ab_conv_w', 'm_ab_conv_b', 'm_lru_w_a', 'm_lru_b_a', 'm_lru_w_i', 'm_lru_b_i', 'm_lru_lambda', 'm_fox_b_f', 'm_ab_w_out', 'm_cd_norm', 'm_cd_w_in', 'm_cd_sink', 'm_cd_w_out', 'm_xa_norm', 'm_xa_mem_norm', 'm_xa_w_q', 'm_xa_w_kv', 'm_xa_w_o', 'm_mlp_norm', 'm_mlp_w_up', 'm_mlp_w_down', 'm_final_norm', 'v_ab_norm', 'v_ab_w_in', 'v_ab_conv_w', 'v_ab_conv_b', 'v_lru_w_a', 'v_lru_b_a', 'v_lru_w_i', 'v_lru_b_i', 'v_lru_lambda', 'v_fox_b_f', 'v_ab_w_out', 'v_cd_norm', 'v_cd_w_in', 'v_cd_sink', 'v_cd_w_out', 'v_xa_norm', 'v_xa_mem_norm', 'v_xa_w_q', 'v_xa_w_kv', 'v_xa_w_o', 'v_mlp_norm', 'v_mlp_w_up', 'v_mlp_w_down', 'v_final_norm']
TWIN_OUTPUTS = ['loss', 'grad_x', 'grad_ab_norm', 'grad_ab_w_in', 'grad_ab_conv_w', 'grad_ab_conv_b', 'grad_lru_w_a', 'grad_lru_b_a', 'grad_lru_w_i', 'grad_lru_b_i', 'grad_lru_lambda', 'grad_fox_b_f', 'grad_ab_w_out', 'grad_cd_norm', 'grad_cd_w_in', 'grad_cd_sink', 'grad_cd_w_out', 'grad_xa_norm', 'grad_xa_mem_norm', 'grad_xa_w_q', 'grad_xa_w_kv', 'grad_xa_w_o', 'grad_mlp_norm', 'grad_mlp_w_up', 'grad_mlp_w_down', 'grad_final_norm', 'delta_ab_norm', 'delta_ab_w_in', 'delta_ab_conv_w', 'delta_ab_conv_b', 'delta_lru_w_a', 'delta_lru_b_a', 'delta_lru_w_i', 'delta_lru_b_i', 'delta_lru_lambda', 'delta_fox_b_f', 'delta_ab_w_out', 'delta_cd_norm', 'delta_cd_w_in', 'delta_cd_sink', 'delta_cd_w_out', 'delta_xa_norm', 'delta_xa_mem_norm', 'delta_xa_w_q', 'delta_xa_w_kv', 'delta_xa_w_o', 'delta_mlp_norm', 'delta_mlp_w_up', 'delta_mlp_w_down', 'delta_final_norm', 'new_m_ab_norm', 'new_m_ab_w_in', 'new_m_ab_conv_w', 'new_m_ab_conv_b', 'new_m_lru_w_a', 'new_m_lru_b_a', 'new_m_lru_w_i', 'new_m_lru_b_i', 'new_m_lru_lambda', 'new_m_fox_b_f', 'new_m_ab_w_out', 'new_m_cd_norm', 'new_m_cd_w_in', 'new_m_cd_sink', 'new_m_cd_w_out', 'new_m_xa_norm', 'new_m_xa_mem_norm', 'new_m_xa_w_q', 'new_m_xa_w_kv', 'new_m_xa_w_o', 'new_m_mlp_norm', 'new_m_mlp_w_up', 'new_m_mlp_w_down', 'new_m_final_norm', 'new_v_ab_norm', 'new_v_ab_w_in', 'new_v_ab_conv_w', 'new_v_ab_conv_b', 'new_v_lru_w_a', 'new_v_lru_b_a', 'new_v_lru_w_i', 'new_v_lru_b_i', 'new_v_lru_lambda', 'new_v_fox_b_f', 'new_v_ab_w_out', 'new_v_cd_norm', 'new_v_cd_w_in', 'new_v_cd_sink', 'new_v_cd_w_out', 'new_v_xa_norm', 'new_v_xa_mem_norm', 'new_v_xa_w_q', 'new_v_xa_w_kv', 'new_v_xa_w_o', 'new_v_mlp_norm', 'new_v_mlp_w_up', 'new_v_mlp_w_down', 'new_v_final_norm']
TWIN_LEAF_KINDS = {'loss': 'loss', 'grad_x': 'grad_x', 'grad_ab_norm': 'grad_w', 'grad_ab_w_in': 'grad_w', 'grad_ab_conv_w': 'grad_w', 'grad_ab_conv_b': 'grad_w', 'grad_lru_w_a': 'grad_w', 'grad_lru_b_a': 'grad_w', 'grad_lru_w_i': 'grad_w', 'grad_lru_b_i': 'grad_w', 'grad_lru_lambda': 'grad_w', 'grad_fox_b_f': 'grad_w', 'grad_ab_w_out': 'grad_w', 'grad_cd_norm': 'grad_w', 'grad_cd_w_in': 'grad_w', 'grad_cd_sink': 'grad_w', 'grad_cd_w_out': 'grad_w', 'grad_xa_norm': 'grad_w', 'grad_xa_mem_norm': 'grad_w', 'grad_xa_w_q': 'grad_w', 'grad_xa_w_kv': 'grad_w', 'grad_xa_w_o': 'grad_w', 'grad_mlp_norm': 'grad_w', 'grad_mlp_w_up': 'grad_w', 'grad_mlp_w_down': 'grad_w', 'grad_final_norm': 'grad_w', 'delta_ab_norm': 'delta_w', 'delta_ab_w_in': 'delta_w', 'delta_ab_conv_w': 'delta_w', 'delta_ab_conv_b': 'delta_w', 'delta_lru_w_a': 'delta_w', 'delta_lru_b_a': 'delta_w', 'delta_lru_w_i': 'delta_w', 'delta_lru_b_i': 'delta_w', 'delta_lru_lambda': 'delta_w', 'delta_fox_b_f': 'delta_w', 'delta_ab_w_out': 'delta_w', 'delta_cd_norm': 'delta_w', 'delta_cd_w_in': 'delta_w', 'delta_cd_sink': 'delta_w', 'delta_cd_w_out': 'delta_w', 'delta_xa_norm': 'delta_w', 'delta_xa_mem_norm': 'delta_w', 'delta_xa_w_q': 'delta_w', 'delta_xa_w_kv': 'delta_w', 'delta_xa_w_o': 'delta_w', 'delta_mlp_norm': 'delta_w', 'delta_mlp_w_up': 'delta_w', 'delta_mlp_w_down': 'delta_w', 'delta_final_norm': 'delta_w', 'new_m_ab_norm': 'new_m', 'new_m_ab_w_in': 'new_m', 'new_m_ab_conv_w': 'new_m', 'new_m_ab_conv_b': 'new_m', 'new_m_lru_w_a': 'new_m', 'new_m_lru_b_a': 'new_m', 'new_m_lru_w_i': 'new_m', 'new_m_lru_b_i': 'new_m', 'new_m_lru_lambda': 'new_m', 'new_m_fox_b_f': 'new_m', 'new_m_ab_w_out': 'new_m', 'new_m_cd_norm': 'new_m', 'new_m_cd_w_in': 'new_m', 'new_m_cd_sink': 'new_m', 'new_m_cd_w_out': 'new_m', 'new_m_xa_norm': 'new_m', 'new_m_xa_mem_norm': 'new_m', 'new_m_xa_w_q': 'new_m', 'new_m_xa_w_kv': 'new_m', 'new_m_xa_w_o': 'new_m', 'new_m_mlp_norm': 'new_m', 'new_m_mlp_w_up': 'new_m', 'new_m_mlp_w_down': 'new_m', 'new_m_final_norm': 'new_m', 'new_v_ab_norm': 'new_v', 'new_v_ab_w_in': 'new_v', 'new_v_ab_conv_w': 'new_v', 'new_v_ab_conv_b': 'new_v', 'new_v_lru_w_a': 'new_v', 'new_v_lru_b_a': 'new_v', 'new_v_lru_w_i': 'new_v', 'new_v_lru_b_i': 'new_v', 'new_v_lru_lambda': 'new_v', 'new_v_fox_b_f': 'new_v', 'new_v_ab_w_out': 'new_v', 'new_v_cd_norm': 'new_v', 'new_v_cd_w_in': 'new_v', 'new_v_cd_sink': 'new_v', 'new_v_cd_w_out': 'new_v', 'new_v_xa_norm': 'new_v', 'new_v_xa_mem_norm': 'new_v', 'new_v_xa_w_q': 'new_v', 'new_v_xa_w_kv': 'new_v', 'new_v_xa_w_o': 'new_v', 'new_v_mlp_norm': 'new_v', 'new_v_mlp_w_up': 'new_v', 'new_v_mlp_w_down': 'new_v', 'new_v_final_norm': 'new_v'}


def _forward(args):
    return _fwd_reference(*[args[k] for k in FWD_PARAMS])


def _output_shape():
    out = _jax.eval_shape(lambda: _forward(_fwd_setup_inputs(0)))
    return out.shape, out.dtype

N_MICROBATCH = 1
ADAM_LR = 0.001
ADAM_B1 = 0.9
ADAM_B2 = 0.999
ADAM_EPS = 1e-08
ADAM_WD = 0.01
ADAM_STEP = 10
PER_EXAMPLE_BATCH_AXIS = {'x': 0, 'mem': 0, 'loss_target': 0}
SHARED_INPUTS = []
_WEIGHT_DTYPES = {'ab_norm': _jnp.float32, 'ab_w_in': _jnp.float32, 'ab_conv_w': _jnp.float32, 'ab_conv_b': _jnp.float32, 'lru_w_a': _jnp.float32, 'lru_b_a': _jnp.float32, 'lru_w_i': _jnp.float32, 'lru_b_i': _jnp.float32, 'lru_lambda': _jnp.float32, 'fox_b_f': _jnp.float32, 'ab_w_out': _jnp.float32, 'cd_norm': _jnp.float32, 'cd_w_in': _jnp.float32, 'cd_sink': _jnp.float32, 'cd_w_out': _jnp.float32, 'xa_norm': _jnp.float32, 'xa_mem_norm': _jnp.float32, 'xa_w_q': _jnp.float32, 'xa_w_kv': _jnp.float32, 'xa_w_o': _jnp.float32, 'mlp_norm': _jnp.float32, 'mlp_w_up': _jnp.float32, 'mlp_w_down': _jnp.float32, 'final_norm': _jnp.float32}
MOMENT_SCALE = {'ab_norm': 1.889009e-01, 'ab_w_in': 1.018163e-01, 'ab_conv_w': 1.621590e-01, 'ab_conv_b': 2.704279e+00, 'lru_w_a': 1.020647e-01, 'lru_b_a': 5.706384e-02, 'lru_w_i': 1.876598e-01, 'lru_b_i': 6.084307e-02, 'lru_lambda': 9.010497e-02, 'fox_b_f': 4.928274e-01, 'ab_w_out': 1.493186e-01, 'cd_norm': 2.521213e-01, 'cd_w_in': 1.955885e-01, 'cd_sink': 2.074261e-02, 'cd_w_out': 2.964398e-01, 'xa_norm': 2.313833e-02, 'xa_mem_norm': 4.432746e-02, 'xa_w_q': 2.384545e-02, 'xa_w_kv': 2.990126e-02, 'xa_w_o': 3.556391e-02, 'mlp_norm': 2.295338e-01, 'mlp_w_up': 1.193914e-01, 'mlp_w_down': 4.674470e-01, 'final_norm': 6.571107e+01}


def _to_microbatches(a, axis):
    t = _jnp.moveaxis(a, axis, 0)
    t = t.reshape((N_MICROBATCH, t.shape[0] // N_MICROBATCH) + t.shape[1:])
    return _jnp.moveaxis(t, 1, axis + 1)


def setup_inputs(seed: int = 0) -> dict:
    inp = _fwd_setup_inputs(seed)
    key = _jax.random.fold_in(_jax.random.key(seed), 7919)
    shape, _ = _output_shape()
    out = dict(inp)
    out["loss_target"] = _jax.random.normal(_jax.random.fold_in(key, 0), shape, _jnp.float32)
    for i, name in enumerate(TWIN_WEIGHTS):
        w = inp[name].astype(_jnp.float32)
        if MOMENT_SCALE is None:
            s = _jnp.sqrt(_jnp.mean(_jnp.square(w)) + 1e-30)
        else:
            s = MOMENT_SCALE[name]
        km, kv = _jax.random.split(_jax.random.fold_in(key, i + 1))
        out[name] = w
        out["m_" + name] = s * _jax.random.normal(km, w.shape, _jnp.float32)
        out["v_" + name] = (s * s) * _jax.random.uniform(kv, w.shape, _jnp.float32, 0.5, 1.5)
    if N_MICROBATCH > 1:
        for name, axis in PER_EXAMPLE_BATCH_AXIS.items():
            out[name] = _to_microbatches(out[name], axis)
    return {'x': out['x'], 'mem': out['mem'], 'ab_norm': out['ab_norm'], 'ab_w_in': out['ab_w_in'], 'ab_conv_w': out['ab_conv_w'], 'ab_conv_b': out['ab_conv_b'], 'lru_w_a': out['lru_w_a'], 'lru_b_a': out['lru_b_a'], 'lru_w_i': out['lru_w_i'], 'lru_b_i': out['lru_b_i'], 'lru_lambda': out['lru_lambda'], 'fox_b_f': out['fox_b_f'], 'ab_w_out': out['ab_w_out'], 'cd_norm': out['cd_norm'], 'cd_w_in': out['cd_w_in'], 'cd_sink': out['cd_sink'], 'cd_w_out': out['cd_w_out'], 'xa_norm': out['xa_norm'], 'xa_mem_norm': out['xa_mem_norm'], 'xa_w_q': out['xa_w_q'], 'xa_w_kv': out['xa_w_kv'], 'xa_w_o': out['xa_w_o'], 'mlp_norm': out['mlp_norm'], 'mlp_w_up': out['mlp_w_up'], 'mlp_w_down': out['mlp_w_down'], 'final_norm': out['final_norm'], 'loss_target': out['loss_target'], 'm_ab_norm': out['m_ab_norm'], 'm_ab_w_in': out['m_ab_w_in'], 'm_ab_conv_w': out['m_ab_conv_w'], 'm_ab_conv_b': out['m_ab_conv_b'], 'm_lru_w_a': out['m_lru_w_a'], 'm_lru_b_a': out['m_lru_b_a'], 'm_lru_w_i': out['m_lru_w_i'], 'm_lru_b_i': out['m_lru_b_i'], 'm_lru_lambda': out['m_lru_lambda'], 'm_fox_b_f': out['m_fox_b_f'], 'm_ab_w_out': out['m_ab_w_out'], 'm_cd_norm': out['m_cd_norm'], 'm_cd_w_in': out['m_cd_w_in'], 'm_cd_sink': out['m_cd_sink'], 'm_cd_w_out': out['m_cd_w_out'], 'm_xa_norm': out['m_xa_norm'], 'm_xa_mem_norm': out['m_xa_mem_norm'], 'm_xa_w_q': out['m_xa_w_q'], 'm_xa_w_kv': out['m_xa_w_kv'], 'm_xa_w_o': out['m_xa_w_o'], 'm_mlp_norm': out['m_mlp_norm'], 'm_mlp_w_up': out['m_mlp_w_up'], 'm_mlp_w_down': out['m_mlp_w_down'], 'm_final_norm': out['m_final_norm'], 'v_ab_norm': out['v_ab_norm'], 'v_ab_w_in': out['v_ab_w_in'], 'v_ab_conv_w': out['v_ab_conv_w'], 'v_ab_conv_b': out['v_ab_conv_b'], 'v_lru_w_a': out['v_lru_w_a'], 'v_lru_b_a': out['v_lru_b_a'], 'v_lru_w_i': out['v_lru_w_i'], 'v_lru_b_i': out['v_lru_b_i'], 'v_lru_lambda': out['v_lru_lambda'], 'v_fox_b_f': out['v_fox_b_f'], 'v_ab_w_out': out['v_ab_w_out'], 'v_cd_norm': out['v_cd_norm'], 'v_cd_w_in': out['v_cd_w_in'], 'v_cd_sink': out['v_cd_sink'], 'v_cd_w_out': out['v_cd_w_out'], 'v_xa_norm': out['v_xa_norm'], 'v_xa_mem_norm': out['v_xa_mem_norm'], 'v_xa_w_q': out['v_xa_w_q'], 'v_xa_w_kv': out['v_xa_w_kv'], 'v_xa_w_o': out['v_xa_w_o'], 'v_mlp_norm': out['v_mlp_norm'], 'v_mlp_w_up': out['v_mlp_w_up'], 'v_mlp_w_down': out['v_mlp_w_down'], 'v_final_norm': out['v_final_norm']}


def _loss(weights, diff, rest, loss_target):
    with _jax.named_scope("forward"):
        args = {**rest, TWIN_DIFF_INPUT: diff, **{k: w.astype(_WEIGHT_DTYPES[k]) for k, w in weights.items()}}
        y = _forward(args)
    with _jax.named_scope("loss_head"):
        err = _jnp.square(y.astype(_jnp.float32) - loss_target)
        return 0.5 * _jnp.sum(_jnp.mean(err, axis=-1)) if err.ndim else 0.5 * err


def _adamw(w, g, m, v):
    m = ADAM_B1 * m + (1.0 - ADAM_B1) * g
    v = ADAM_B2 * v + (1.0 - ADAM_B2) * _jnp.square(g)
    m_hat = m / (1.0 - ADAM_B1 ** ADAM_STEP)
    v_hat = v / (1.0 - ADAM_B2 ** ADAM_STEP)
    delta = -ADAM_LR * (m_hat / (_jnp.sqrt(v_hat) + ADAM_EPS) + ADAM_WD * w)
    return delta, m, v


def reference(x, mem, ab_norm, ab_w_in, ab_conv_w, ab_conv_b, lru_w_a, lru_b_a, lru_w_i, lru_b_i, lru_lambda, fox_b_f, ab_w_out, cd_norm, cd_w_in, cd_sink, cd_w_out, xa_norm, xa_mem_norm, xa_w_q, xa_w_kv, xa_w_o, mlp_norm, mlp_w_up, mlp_w_down, final_norm, loss_target, m_ab_norm, m_ab_w_in, m_ab_conv_w, m_ab_conv_b, m_lru_w_a, m_lru_b_a, m_lru_w_i, m_lru_b_i, m_lru_lambda, m_fox_b_f, m_ab_w_out, m_cd_norm, m_cd_w_in, m_cd_sink, m_cd_w_out, m_xa_norm, m_xa_mem_norm, m_xa_w_q, m_xa_w_kv, m_xa_w_o, m_mlp_norm, m_mlp_w_up, m_mlp_w_down, m_final_norm, v_ab_norm, v_ab_w_in, v_ab_conv_w, v_ab_conv_b, v_lru_w_a, v_lru_b_a, v_lru_w_i, v_lru_b_i, v_lru_lambda, v_fox_b_f, v_ab_w_out, v_cd_norm, v_cd_w_in, v_cd_sink, v_cd_w_out, v_xa_norm, v_xa_mem_norm, v_xa_w_q, v_xa_w_kv, v_xa_w_o, v_mlp_norm, v_mlp_w_up, v_mlp_w_down, v_final_norm):
    given = dict(x=x, mem=mem, ab_norm=ab_norm, ab_w_in=ab_w_in, ab_conv_w=ab_conv_w, ab_conv_b=ab_conv_b, lru_w_a=lru_w_a, lru_b_a=lru_b_a, lru_w_i=lru_w_i, lru_b_i=lru_b_i, lru_lambda=lru_lambda, fox_b_f=fox_b_f, ab_w_out=ab_w_out, cd_norm=cd_norm, cd_w_in=cd_w_in, cd_sink=cd_sink, cd_w_out=cd_w_out, xa_norm=xa_norm, xa_mem_norm=xa_mem_norm, xa_w_q=xa_w_q, xa_w_kv=xa_w_kv, xa_w_o=xa_w_o, mlp_norm=mlp_norm, mlp_w_up=mlp_w_up, mlp_w_down=mlp_w_down, final_norm=final_norm, loss_target=loss_target, m_ab_norm=m_ab_norm, m_ab_w_in=m_ab_w_in, m_ab_conv_w=m_ab_conv_w, m_ab_conv_b=m_ab_conv_b, m_lru_w_a=m_lru_w_a, m_lru_b_a=m_lru_b_a, m_lru_w_i=m_lru_w_i, m_lru_b_i=m_lru_b_i, m_lru_lambda=m_lru_lambda, m_fox_b_f=m_fox_b_f, m_ab_w_out=m_ab_w_out, m_cd_norm=m_cd_norm, m_cd_w_in=m_cd_w_in, m_cd_sink=m_cd_sink, m_cd_w_out=m_cd_w_out, m_xa_norm=m_xa_norm, m_xa_mem_norm=m_xa_mem_norm, m_xa_w_q=m_xa_w_q, m_xa_w_kv=m_xa_w_kv, m_xa_w_o=m_xa_w_o, m_mlp_norm=m_mlp_norm, m_mlp_w_up=m_mlp_w_up, m_mlp_w_down=m_mlp_w_down, m_final_norm=m_final_norm, v_ab_norm=v_ab_norm, v_ab_w_in=v_ab_w_in, v_ab_conv_w=v_ab_conv_w, v_ab_conv_b=v_ab_conv_b, v_lru_w_a=v_lru_w_a, v_lru_b_a=v_lru_b_a, v_lru_w_i=v_lru_w_i, v_lru_b_i=v_lru_b_i, v_lru_lambda=v_lru_lambda, v_fox_b_f=v_fox_b_f, v_ab_w_out=v_ab_w_out, v_cd_norm=v_cd_norm, v_cd_w_in=v_cd_w_in, v_cd_sink=v_cd_sink, v_cd_w_out=v_cd_w_out, v_xa_norm=v_xa_norm, v_xa_mem_norm=v_xa_mem_norm, v_xa_w_q=v_xa_w_q, v_xa_w_kv=v_xa_w_kv, v_xa_w_o=v_xa_w_o, v_mlp_norm=v_mlp_norm, v_mlp_w_up=v_mlp_w_up, v_mlp_w_down=v_mlp_w_down, v_final_norm=v_final_norm)
    weights = {n: given[n] for n in TWIN_WEIGHTS}
    shared = {n: given[n] for n in SHARED_INPUTS}
    per_example = {n: given[n] for n in ['x', 'mem']}
    grad_fn = _jax.value_and_grad(_loss, argnums=(0, 1))

    def one_microbatch(ex, loss_target):
        ex = dict(ex)
        diff = ex.pop(TWIN_DIFF_INPUT)
        return grad_fn(weights, diff, {**shared, **ex}, loss_target)

    if N_MICROBATCH == 1:
        loss, (grad_w, grad_x) = one_microbatch(per_example, given["loss_target"])
    else:
        def body(carry, xs):
            loss_sum, grad_sum = carry
            l_k, (gw_k, gx_k) = one_microbatch(xs[0], xs[1])
            with _jax.named_scope("update"):
                return (loss_sum + l_k, _jax.tree.map(_jnp.add, grad_sum, gw_k)), gx_k

        init = (_jnp.zeros((), _jnp.float32), _jax.tree.map(_jnp.zeros_like, weights))
        (loss, grad_w), grad_x = _jax.lax.scan(body, init, (per_example, given["loss_target"]))
    with _jax.named_scope("update"):
        delta_w, new_m, new_v = {}, {}, {}
        for n in TWIN_WEIGHTS:
            delta_w[n], new_m[n], new_v[n] = _adamw(weights[n], grad_w[n], given["m_" + n], given["v_" + n])
    return (loss, grad_x, *[grad_w[n] for n in TWIN_WEIGHTS], *[delta_w[n] for n in TWIN_WEIGHTS],
            *[new_m[n] for n in TWIN_WEIGHTS], *[new_v[n] for n in TWIN_WEIGHTS])
```

```python
import functools
import math

import jax
import jax.numpy as jnp
from jax import lax
from jax.experimental import pallas as pl
from jax.experimental.pallas import tpu as pltpu

F32 = jnp.float32
BF16 = jnp.bfloat16
MESH = pl.DeviceIdType.MESH

D_MODEL = 1024
SEQ = 2048
HEAD_DIM = 64
LRU_WIDTH = 512
LRU_BLOCKS = 8
LRU_C = 8.0
CONV_WIDTH = 4
ATT_W = 512
SWA_KW = 128
SWA_WINDOW = 128
DIL_PATTERN = ((128, 1), (512, 4), (2048, 16))
MEM_LEN = 256
XA_HEADS = 4
XA_HEAD_DIM = 256
D_FF = 4096
ROPE_THETA = 10000.0
EPS = 1e-6
N_CHIPS = 4
LANES = 128

ADAM_LR, ADAM_B1, ADAM_B2, ADAM_EPS, ADAM_WD, ADAM_STEP = 0.001, 0.9, 0.999, 1e-08, 0.01, 10

VMEM_LIMIT_BYTES = 56 * 1024 * 1024
MASKED = -1e30
ROW_TILE = 512
LRU_CHUNK = 512
ATT_BLOCK = 128


def _params(*sem):
    return pltpu.CompilerParams(dimension_semantics=sem, vmem_limit_bytes=VMEM_LIMIT_BYTES)


def _rowwise(fn, rows, consts=(), out_rows=(), out_accs=(), *, name, tile=ROW_TILE, row_maps=None):
    rows = [r if isinstance(r, tuple) else (r, r.shape[1], 0) for r in rows]
    consts = list(consts)
    nr, nc, no = len(rows), len(consts), len(out_rows)
    total = rows[0][0].shape[0]
    tile = min(tile, total)
    assert total % tile == 0
    n = total // tile

    def body(*refs):
        outs = fn(*[r[...] for r in refs[:nr + nc]])
        outs = tuple(outs) if isinstance(outs, (tuple, list)) else (outs,)
        for ref, val in zip(refs[nr + nc:nr + nc + no], outs[:no]):
            ref[...] = val.astype(ref.dtype)
        for ref, val in zip(refs[nr + nc + no:], outs[no:]):
            @pl.when(pl.program_id(0) == 0)
            def _(ref=ref):
                ref[...] = jnp.zeros(ref.shape, ref.dtype)
            ref[...] += val

    in_specs = []
    for idx, (arr, width, cb) in enumerate(rows):
        if row_maps is not None and row_maps[idx] is not None:
            in_specs.append(pl.BlockSpec((tile, width), row_maps[idx]))
        else:
            in_specs.append(pl.BlockSpec((tile, width), functools.partial(lambda i, cb: (i, cb), cb=cb)))
    in_specs += [pl.BlockSpec(c.shape, lambda i: (0, 0)) for c in consts]
    out_shape = [jax.ShapeDtypeStruct((total, w), dt) for w, dt in out_rows]
    out_shape += [jax.ShapeDtypeStruct(s, F32) for s in out_accs]
    out_specs = [pl.BlockSpec((tile, w), lambda i: (i, 0)) for w, _ in out_rows]
    out_specs += [pl.BlockSpec(s, lambda i: (0, 0)) for s in out_accs]
    return pl.pallas_call(
        body, grid=(n,), in_specs=in_specs, out_specs=out_specs, out_shape=out_shape, name=name,
        compiler_params=_params("arbitrary"),
    )(*[r[0] for r in rows], *consts)


def _matmul(a, b, *, ta=False, tb=False, outs=(F32,), epilogue=None, extras=(), tm=1024, tn=512, tk=1024, name):
    m, k = (a.shape[1], a.shape[0]) if ta else a.shape
    n = b.shape[0] if tb else b.shape[1]
    assert (b.shape[1] if tb else b.shape[0]) == k
    tm, tn, tk = min(tm, m), min(tn, n), min(tk, k)
    assert m % tm == 0 and n % tn == 0 and k % tk == 0, (name, m, n, k)
    nk = k // tk
    ne, no = len(extras), len(outs)
    dims = (((0 if ta else 1,), (1 if tb else 0,)), ((), ()))

    def body(*refs):
        a_ref, b_ref = refs[:2]
        e_refs, o_refs = refs[2:2 + ne], refs[2 + ne:2 + ne + no]

        def finish(acc):
            vals = (acc,) if epilogue is None else epilogue(acc, *[e[...] for e in e_refs])
            for ref, val in zip(o_refs, vals):
                ref[...] = val.astype(ref.dtype)

        prod = lax.dot_general(a_ref[...], b_ref[...], dims, preferred_element_type=F32)
        if nk == 1:
            finish(prod)
        else:
            acc_ref = refs[-1]
            step = pl.program_id(2)

            @pl.when(step == 0)
            def _():
                acc_ref[...] = prod

            @pl.when(step > 0)
            def _():
                acc_ref[...] += prod

            @pl.when(step == nk - 1)
            def _():
                finish(acc_ref[...])

    a_spec = pl.BlockSpec((tk, tm), lambda i, j, s: (s, i)) if ta else pl.BlockSpec((tm, tk), lambda i, j, s: (i, s))
    b_spec = pl.BlockSpec((tn, tk), lambda i, j, s: (j, s)) if tb else pl.BlockSpec((tk, tn), lambda i, j, s: (s, j))
    tile_spec = pl.BlockSpec((tm, tn), lambda i, j, s: (i, j))
    return pl.pallas_call(
        body, grid=(m // tm, n // tn, nk),
        in_specs=[a_spec, b_spec] + [tile_spec] * ne,
        out_specs=[tile_spec] * no,
        out_shape=[jax.ShapeDtypeStruct((m, n), dt) for dt in outs],
        scratch_shapes=[pltpu.VMEM((tm, tn), F32)] if nk > 1 else [],
        name=name, compiler_params=_params("parallel", "parallel", "arbitrary"),
    )(a, b, *extras)


def _split3(x):
    x1 = x.astype(BF16)
    r1 = x - x1.astype(F32)
    x2 = r1.astype(BF16)
    x3 = (r1 - x2.astype(F32)).astype(BF16)
    return x1, x2, x3


def _dot_exact(x, e):
    return sum(jnp.dot(p, e, preferred_element_type=F32) for p in _split3(x))


def _head_expand(heads, width):
    dh = width // heads
    rows = jnp.arange(LANES)[:, None]
    cols = jnp.arange(width)[None, :]
    return (cols // dh == rows).astype(BF16)


def _rstd(x):
    return lax.rsqrt(jnp.mean(x * x, axis=-1, keepdims=True) + EPS)


def _rms_fwd(x, gain, *, name):
    return _rowwise(lambda x, g: x * _rstd(x) * g, [x], [gain], [(x.shape[1], BF16)], name=name)[0]


def _rms_bwd_vals(x, dhn, gain):
    r = _rstd(x)
    xh = x * r
    dxh = dhn * gain
    dx = r * (dxh - xh * jnp.mean(dxh * xh, axis=-1, keepdims=True))
    return dx, jnp.sum(dhn * xh, axis=0, keepdims=True)


def _rms_bwd(x, dhn, dres, gain, *, name):
    def fn(x, dhn, dres, g):
        dx, dg = _rms_bwd_vals(x, dhn, g)
        dh = dres + dx
        return dh, dh, dg
    d = x.shape[1]
    return _rowwise(fn, [x, dhn, dres], [gain], [(d, F32), (d, BF16)], [(1, d)], name=name)


def _loss_and_grad(h, target, gain, *, name):
    def fn(h, tgt, g):
        r = _rstd(h)
        err = h * r * g - tgt
        loss = 0.5 * jnp.sum(jnp.mean(err * err, axis=-1, keepdims=True), axis=0, keepdims=True)
        dx, dg = _rms_bwd_vals(h, err * (1.0 / D_MODEL), g)
        return dx, dx, jnp.broadcast_to(loss, (1, LANES)), dg
    d = h.shape[1]
    return _rowwise(fn, [h, target], [gain], [(d, F32), (d, BF16)], [(1, LANES), (1, d)], name=name)


def _adamw(w, g, m, v, *, name):
    rows, cols = w.shape
    tile = rows
    for cand in (256, 128, 64, 32, 16, 8):
        if rows % cand == 0 and rows > cand:
            tile = cand
            break
    bc1 = 1.0 - ADAM_B1 ** ADAM_STEP
    bc2 = 1.0 - ADAM_B2 ** ADAM_STEP

    def fn(w, g, m, v):
        m2 = ADAM_B1 * m + (1.0 - ADAM_B1) * g
        v2 = ADAM_B2 * v + (1.0 - ADAM_B2) * (g * g)
        delta = -ADAM_LR * ((m2 / bc1) / (jnp.sqrt(v2 / bc2) + ADAM_EPS) + ADAM_WD * w)
        return delta, m2, v2
    return _rowwise(fn, [w, g, m, v], [], [(cols, F32)] * 3, name=name, tile=tile)


def _attn_specs(arr, width, cb, classes, rows_block, whole):
    ncols = arr.shape[2] // (classes * width)
    if whole:
        return pl.BlockSpec((1, arr.shape[1], width), lambda n, r, i: (n, 0, r * ncols + cb))
    return pl.BlockSpec((1, rows_block, width), lambda n, r, i: (n, i, r * ncols + cb))


def _scores(qh, kj, kb_row, i, j_pos, tq, tk, scale, mode, max_dist):
    s = lax.dot_general(qh, kj, (((1,), (1,)), ((), ())), preferred_element_type=F32) * scale
    if kb_row is not None:
        s = s - kb_row
    if mode != "full":
        qpos = i * tq + lax.broadcasted_iota(jnp.int32, (tq, tk), 0)
        kpos = j_pos * tk + lax.broadcasted_iota(jnp.int32, (tq, tk), 1)
        dist = qpos - kpos
        ok = (dist >= 0) & (kpos >= 0)
        if max_dist is not None:
            ok = ok & (dist <= max_dist)
        s = jnp.where(ok, s, MASKED)
    return s


def _key_blocks(mode, i):
    if mode == "full":
        return 0, 1
    if mode == "band":
        return i - 1, 2
    return 0, i + 1


def _attn_fwd(q, k, v, kb=None, *, classes=1, heads, kv_heads, dh, mode, max_dist=None, name):
    (qa, qc), (ka, kc), (va, vc) = q, k, v
    n, lq, lk = qa.shape[0], qa.shape[1], ka.shape[1]
    wq, wk = heads * dh, kv_heads * dh
    tq = min(lq, 256) if mode == "full" else ATT_BLOCK
    tk = lk if mode == "full" else ATT_BLOCK
    scale = dh ** -0.5
    group = heads // kv_heads
    has_kb = kb is not None

    def body(*refs):
        q_ref, k_ref, v_ref = refs[:3]
        kb_ref = refs[3] if has_kb else None
        o_ref, lse_ref = refs[-2:]
        i = pl.program_id(2)
        first, count = _key_blocks(mode, i)
        lane = lax.broadcasted_iota(jnp.int32, (tq, LANES), 1)
        lse_all = jnp.zeros((tq, LANES), F32)
        for h in range(heads):
            hk = h // group
            qh = q_ref[0, :, h * dh:(h + 1) * dh]

            def step(t, carry, h=h, hk=hk, qh=qh):
                m, l, acc = carry
                j_pos = first + t
                j = jnp.maximum(j_pos, 0)
                k0 = pl.multiple_of(j * tk, tk)
                kj = k_ref[0, pl.ds(k0, tk), hk * dh:(hk + 1) * dh]
                vj = v_ref[0, pl.ds(k0, tk), hk * dh:(hk + 1) * dh]
                kb_row = kb_ref[0, h, pl.ds(j, 1), :] if has_kb else None
                s = _scores(qh, kj, kb_row, i, j_pos, tq, tk, scale, mode, max_dist)
                m_new = jnp.maximum(m, jnp.max(s, axis=1, keepdims=True))
                alpha = jnp.exp(m - m_new)
                p = jnp.exp(s - m_new)
                l = alpha * l + jnp.sum(p, axis=1, keepdims=True)
                acc = alpha * acc + jnp.dot(p.astype(BF16), vj, preferred_element_type=F32)
                return m_new, l, acc

            init = (jnp.full((tq, 1), MASKED, F32), jnp.zeros((tq, 1), F32), jnp.zeros((tq, dh), F32))
            if mode == "causal":
                m, l, acc = lax.fori_loop(0, count, step, init)
            else:
                carry = init
                for t in reversed(range(count)):
                    carry = step(t, carry)
                m, l, acc = carry
            o_ref[0, :, h * dh:(h + 1) * dh] = acc / l
            lse_all = jnp.where(lane == h, m + jnp.log(l), lse_all)
        lse_ref[0] = lse_all

    in_specs = [_attn_specs(qa, wq, qc, classes, tq, False), _attn_specs(ka, wk, kc, classes, tk, True),
                _attn_specs(va, wk, vc, classes, tk, True)]
    args = [qa, ka, va]
    if has_kb:
        in_specs.append(pl.BlockSpec((1,) + kb.shape[1:], lambda n_, r, i: (n_, 0, 0, 0)))
        args.append(kb)
    out_shape = [jax.ShapeDtypeStruct((n, lq, classes * wq), F32), jax.ShapeDtypeStruct((n, lq, classes * LANES), F32)]
    out_specs = [pl.BlockSpec((1, tq, wq), lambda n_, r, i: (n_, i, r)), pl.BlockSpec((1, tq, LANES), lambda n_, r, i: (n_, i, r))]
    return pl.pallas_call(
        body, grid=(n, classes, lq // tq), in_specs=in_specs, out_specs=out_specs, out_shape=out_shape, name=name,
        compiler_params=_params("parallel", "parallel", "arbitrary"),
    )(*args)


def _attn_bwd(q, k, v, do, lse, delta, kb=None, *, classes=1, heads, kv_heads, dh, mode, max_dist=None, name):
    (qa, qc), (ka, kc), (va, vc), (da, dc) = q, k, v, do
    n, lq, lk = qa.shape[0], qa.shape[1], ka.shape[1]
    wq, wk = heads * dh, kv_heads * dh
    tq = min(lq, 256) if mode == "full" else ATT_BLOCK
    tk = lk if mode == "full" else ATT_BLOCK
    scale = dh ** -0.5
    group = heads // kv_heads
    has_kb = kb is not None
    tn = (((0,), (0,)), ((), ()))

    def body(*refs):
        q_ref, k_ref, v_ref, do_ref, lse_ref, dl_ref = refs[:6]
        kb_ref = refs[6] if has_kb else None
        n_in = 7 if has_kb else 6
        dq_ref, dk_ref, dv_ref = refs[n_in:n_in + 3]
        dkb_ref, drow_ref = refs[n_in + 3:n_in + 5] if has_kb else (None, None)
        i = pl.program_id(2)

        @pl.when(i == 0)
        def _():
            dk_ref[...] = jnp.zeros(dk_ref.shape, F32)
            dv_ref[...] = jnp.zeros(dv_ref.shape, F32)
            if has_kb:
                dkb_ref[...] = jnp.zeros(dkb_ref.shape, F32)

        first, count = _key_blocks(mode, i)
        lse_all = lse_ref[0]
        dl_all = dl_ref[0]
        lane = lax.broadcasted_iota(jnp.int32, (tq, LANES), 1)
        drow_all = jnp.zeros((tq, LANES), F32)
        for h in range(heads):
            hk = h // group
            qh = q_ref[0, :, h * dh:(h + 1) * dh]
            doh = do_ref[0, :, h * dh:(h + 1) * dh]
            lse_h = lse_all[:, h:h + 1]
            dl_h = dl_all[:, h:h + 1]

            def step(t, carry, h=h, hk=hk, qh=qh, doh=doh, lse_h=lse_h, dl_h=dl_h):
                dq, drow = carry
                j_pos = first + t
                j = jnp.maximum(j_pos, 0)
                k0 = pl.multiple_of(j * tk, tk)
                cols = slice(hk * dh, (hk + 1) * dh)
                kj = k_ref[0, pl.ds(k0, tk), cols]
                vj = v_ref[0, pl.ds(k0, tk), cols]
                kb_row = kb_ref[0, h, pl.ds(j, 1), :] if has_kb else None
                s = _scores(qh, kj, kb_row, i, j_pos, tq, tk, scale, mode, max_dist)
                p = jnp.exp(s - lse_h)
                dp = lax.dot_general(doh, vj, (((1,), (1,)), ((), ())), preferred_element_type=F32)
                ds = p * (dp - dl_h)
                dsb = ds.astype(BF16)
                dk_ref[0, pl.ds(k0, tk), cols] += lax.dot_general(dsb, qh, tn, preferred_element_type=F32) * scale
                dv_ref[0, pl.ds(k0, tk), cols] += lax.dot_general(p.astype(BF16), doh, tn, preferred_element_type=F32)
                if has_kb:
                    dkb_ref[0, h, pl.ds(j, 1), :] += -jnp.sum(ds, axis=0, keepdims=True)
                    drow = drow + jnp.sum(ds, axis=1, keepdims=True)
                return dq + jnp.dot(dsb, kj, preferred_element_type=F32), drow

            carry = (jnp.zeros((tq, dh), F32), jnp.zeros((tq, 1), F32))
            if mode == "causal":
                carry = lax.fori_loop(0, count, step, carry)
            else:
                for t in range(count):
                    carry = step(t, carry)
            dq_ref[0, :, h * dh:(h + 1) * dh] = carry[0] * scale
            drow_all = jnp.where(lane == h, carry[1], drow_all)
        if has_kb:
            drow_ref[0] = drow_all

    row_spec = functools.partial(_attn_specs, classes=classes, rows_block=tq, whole=False)
    in_specs = [row_spec(qa, wq, qc), _attn_specs(ka, wk, kc, classes, tk, True), _attn_specs(va, wk, vc, classes, tk, True),
                row_spec(da, wq, dc), row_spec(lse, LANES, 0), row_spec(delta, LANES, 0)]
    args = [qa, ka, va, da, lse, delta]
    out_shape = [jax.ShapeDtypeStruct((n, lq, classes * wq), F32), jax.ShapeDtypeStruct((n, lk, classes * wk), F32),
                 jax.ShapeDtypeStruct((n, lk, classes * wk), F32)]
    kv_spec = pl.BlockSpec((1, lk, wk), lambda n_, r, i: (n_, 0, r))
    out_specs = [pl.BlockSpec((1, tq, wq), lambda n_, r, i: (n_, i, r)), kv_spec, kv_spec]
    if has_kb:
        kb_spec = pl.BlockSpec((1,) + kb.shape[1:], lambda n_, r, i: (n_, 0, 0, 0))
        in_specs.append(kb_spec)
        args.append(kb)
        out_shape += [jax.ShapeDtypeStruct(kb.shape, F32), jax.ShapeDtypeStruct((n, lq, LANES), F32)]
        out_specs += [kb_spec, pl.BlockSpec((1, tq, LANES), lambda n_, r, i: (n_, i, 0))]
    return pl.pallas_call(
        body, grid=(n, classes, lq // tq), in_specs=in_specs, out_specs=out_specs, out_shape=out_shape, name=name,
        compiler_params=_params("parallel", "parallel", "arbitrary"),
    )(*args)


def _head_delta(do, out, heads, *, name, lse=None, sink=None):
    width = out.shape[1]
    gather = _head_expand(heads, width).T

    if sink is None:
        return _rowwise(lambda do, o, e: _dot_exact(do * o, e), [do, out], [gather], [(LANES, F32)], name=name)[0]

    def fn(do, o, lse, e, sink):
        delta = _dot_exact(do * o, e)
        return delta, -jnp.sum(jnp.exp(sink - lse) * delta, axis=0, keepdims=True)
    return _rowwise(fn, [do, out, lse], [gather, sink], [(LANES, F32)], [(1, LANES)], name=name)


def _rope_tables():
    half = HEAD_DIM // 2
    inv = ROPE_THETA ** (-jnp.arange(half, dtype=F32) / half)
    ang = jnp.arange(SEQ, dtype=F32)[:, None] * inv[None, :]
    cos = jnp.tile(jnp.cos(ang), (1, 2 * ATT_W // HEAD_DIM))
    sin = jnp.tile(jnp.concatenate([-jnp.sin(ang), jnp.sin(ang)], axis=1), (1, ATT_W // HEAD_DIM))
    return cos, sin


def _rotate(x, cos, sin):
    width = x.shape[1]
    lane = lax.broadcasted_iota(jnp.int32, x.shape, 1)
    first_half = (lane % HEAD_DIM) < (HEAD_DIM // 2)
    swapped = jnp.where(first_half, pltpu.roll(x, width - HEAD_DIM // 2, axis=1), pltpu.roll(x, HEAD_DIM // 2, axis=1))
    return x * cos[:, :width] + swapped * sin[:, :width]


def _gelu(x):
    k = math.sqrt(2.0 / math.pi)
    t = jnp.tanh(k * (x + 0.044715 * x * x * x))
    return 0.5 * x * (1.0 + t), t


def _gelu_grad(x, t):
    k = math.sqrt(2.0 / math.pi)
    return 0.5 * (1.0 + t) + 0.5 * x * (1.0 - t * t) * k * (1.0 + 3.0 * 0.044715 * x * x)


def _shift_down(x, halo, s):
    ext = jnp.concatenate([halo, x], axis=0)
    return pltpu.roll(ext, s, axis=0)[8:, :]


def _shift_up(x, halo, s):
    rows = x.shape[0]
    ext = jnp.concatenate([x, halo], axis=0)
    return pltpu.roll(ext, rows + 8 - s, axis=0)[:rows, :]


def _lru_gates(u, halo, cw, cb, wa, ba, wi, bi, lam):
    taps = [_shift_down(u, halo, CONV_WIDTH - 1 - k) for k in range(CONV_WIDTH - 1)] + [u]
    uc = cb + sum(cw[k:k + 1, :] * taps[k] for k in range(CONV_WIDTH))
    ucb = uc.astype(BF16)
    r = jax.nn.sigmoid(jnp.dot(ucb, wa, preferred_element_type=F32) + ba)
    gi = jax.nn.sigmoid(jnp.dot(ucb, wi, preferred_element_type=F32) + bi)
    sp = jnp.maximum(-lam, 0.0) + jnp.log(1.0 + jnp.exp(-jnp.abs(lam)))
    log_a = -LRU_C * r * sp
    a = jnp.exp(log_a)
    x2 = 2.0 * log_a
    one_minus_a2 = jnp.where(x2 > -0.01, -x2 * (1.0 + x2 * (0.5 + x2 * (1.0 / 6.0 + x2 / 24.0))), 1.0 - jnp.exp(x2))
    mult = jnp.sqrt(one_minus_a2)
    return taps, uc, ucb, r, gi, sp, a, mult


def _lru_specs(nchunk, reverse):
    def chunk(b, c):
        return b * nchunk + ((nchunk - 1 - c) if reverse else c)

    def halo_before(b, c):
        return jnp.maximum(chunk(b, c) * (LRU_CHUNK // 8) - 1, 0)

    return chunk, halo_before


def _lru_fwd(zug, cw, cb, wa, ba, wi, bi, lam, *, name):
    total = zug.shape[0]
    nchunk = SEQ // LRU_CHUNK
    w = LRU_WIDTH
    chunk, halo_before = _lru_specs(nchunk, False)

    def body(u_ref, uh_ref, g_ref, cw_ref, cb_ref, wa_ref, ba_ref, wi_ref, bi_ref, lam_ref, y_ref, h_ref, a_sc, x_sc, carry_sc):
        c = pl.program_id(1)
        u = u_ref[...]
        halo = jnp.where(c > 0, uh_ref[...], 0.0)
        _, uc, _, _, gi, _, a, mult = _lru_gates(u, halo, cw_ref[...], cb_ref[...], wa_ref[...], ba_ref[...],
                                                 wi_ref[...], bi_ref[...], lam_ref[...])
        a_sc[...] = a
        x_sc[...] = mult * (gi * uc)

        @pl.when(c == 0)
        def _():
            carry_sc[...] = jnp.zeros(carry_sc.shape, F32)

        def tile_step(t, h):
            r0 = pl.multiple_of(t * 8, 8)
            at = a_sc[pl.ds(r0, 8), :]
            xt = x_sc[pl.ds(r0, 8), :]
            rows = []
            for j in range(8):
                h = at[j:j + 1, :] * h + xt[j:j + 1, :]
                rows.append(h)
            h_ref[pl.ds(r0, 8), :] = jnp.concatenate(rows, axis=0)
            return h

        h_last = lax.fori_loop(0, LRU_CHUNK // 8, tile_step, carry_sc[0:1, :])
        carry_sc[0:1, :] = h_last
        gel, _ = _gelu(g_ref[...])
        y_ref[...] = (h_ref[...] * gel).astype(BF16)

    small = lambda arr: pl.BlockSpec(arr.shape, lambda b, c: (0, 0))
    return pl.pallas_call(
        body, grid=(total // SEQ, nchunk),
        in_specs=[pl.BlockSpec((LRU_CHUNK, w), lambda b, c: (chunk(b, c), 0)),
                  pl.BlockSpec((8, w), lambda b, c: (halo_before(b, c), 0)),
                  pl.BlockSpec((LRU_CHUNK, w), lambda b, c: (chunk(b, c), 1)),
                  small(cw), small(cb), small(wa), small(ba), small(wi), small(bi), small(lam)],
        out_specs=[pl.BlockSpec((LRU_CHUNK, w), lambda b, c: (chunk(b, c), 0))] * 2,
        out_shape=[jax.ShapeDtypeStruct((total, w), BF16), jax.ShapeDtypeStruct((total, w), F32)],
        scratch_shapes=[pltpu.VMEM((LRU_CHUNK, w), F32), pltpu.VMEM((LRU_CHUNK, w), F32), pltpu.VMEM((8, w), F32)],
        name=name, compiler_params=_params("arbitrary", "arbitrary"),
    )(zug, zug, zug, cw, cb, wa, ba, wi, bi, lam)


def _lru_bwd(zug, hl, dy, cw, cb, wa, ba, wi, bi, lam, *, name):
    total = zug.shape[0]
    nchunk = SEQ // LRU_CHUNK
    w = LRU_WIDTH
    chunk, halo_before = _lru_specs(nchunk, True)
    tn = (((0,), (0,)), ((), ()))
    nt = (((1,), (1,)), ((), ()))

    def body(u_ref, uh_ref, g_ref, hl_ref, hh_ref, dy_ref, cw_ref, cb_ref, wa_ref, ba_ref, wi_ref, bi_ref, lam_ref,
             dz_ref, dcw_ref, dcb_ref, dwa_ref, dba_ref, dwi_ref, dbi_ref, dlam_ref,
             a_sc, d_sc, g_sc, gcarry_sc, acarry_sc, duc_sc):
        b, c = pl.program_id(0), pl.program_id(1)
        first_chunk = c == nchunk - 1

        @pl.when((b == 0) & (c == 0))
        def _():
            for ref in (dcw_ref, dcb_ref, dwa_ref, dba_ref, dwi_ref, dbi_ref, dlam_ref):
                ref[...] = jnp.zeros(ref.shape, F32)

        @pl.when(c == 0)
        def _():
            gcarry_sc[...] = jnp.zeros(gcarry_sc.shape, F32)
            acarry_sc[...] = jnp.zeros(acarry_sc.shape, F32)
            duc_sc[...] = jnp.zeros(duc_sc.shape, F32)

        u = u_ref[...]
        halo = jnp.where(first_chunk, 0.0, uh_ref[...])
        cw, wa, wi, lam = cw_ref[...], wa_ref[...], wi_ref[...], lam_ref[...]
        taps, uc, ucb, r, gi, sp, a, mult = _lru_gates(u, halo, cw, cb_ref[...], wa, ba_ref[...], wi, bi_ref[...], lam)
        gate = g_ref[...]
        gel, th = _gelu(gate)
        dy = dy_ref[...]
        hl = hl_ref[...]
        a_sc[...] = a
        d_sc[...] = dy * gel
        dgate = dy * hl * _gelu_grad(gate, th)

        def tile_step(t, carry):
            g_next, a_next = carry
            r0 = pl.multiple_of((LRU_CHUNK // 8 - 1 - t) * 8, 8)
            dt = d_sc[pl.ds(r0, 8), :]
            at = a_sc[pl.ds(r0, 8), :]
            rows = [None] * 8
            for j in reversed(range(8)):
                g_next = dt[j:j + 1, :] + a_next * g_next
                a_next = at[j:j + 1, :]
                rows[j] = g_next
            g_sc[pl.ds(r0, 8), :] = jnp.concatenate(rows, axis=0)
            return g_next, a_next

        g_last, a_last = lax.fori_loop(0, LRU_CHUNK // 8, tile_step, (gcarry_sc[0:1, :], acarry_sc[0:1, :]))
        gcarry_sc[0:1, :] = g_last
        acarry_sc[0:1, :] = a_last

        gs = g_sc[...]
        hl_halo = jnp.where(first_chunk, 0.0, hh_ref[...])
        da = gs * _shift_down(hl, hl_halo, 1)
        dmult = gs * gi * uc
        dgi = gs * mult * uc
        duc = gs * mult * gi
        dlog_a = da * a - dmult * a * a / mult
        dr = dlog_a * (-LRU_C * sp)
        dsp = jnp.sum(dlog_a * (-LRU_C * r), axis=0, keepdims=True)
        dlam_ref[...] += -dsp * jax.nn.sigmoid(-lam)
        dpa = dr * r * (1.0 - r)
        dpi = dgi * gi * (1.0 - gi)
        dpab, dpib = dpa.astype(BF16), dpi.astype(BF16)
        dba_ref[...] += jnp.sum(dpa, axis=0, keepdims=True)
        dbi_ref[...] += jnp.sum(dpi, axis=0, keepdims=True)
        dwa_ref[...] += lax.dot_general(ucb, dpab, tn, preferred_element_type=F32)
        dwi_ref[...] += lax.dot_general(ucb, dpib, tn, preferred_element_type=F32)
        duc = duc + lax.dot_general(dpab, wa, nt, preferred_element_type=F32)
        duc = duc + lax.dot_general(dpib, wi, nt, preferred_element_type=F32)
        dcb_ref[...] += jnp.sum(duc, axis=0, keepdims=True)
        dcw_ref[...] += jnp.concatenate([jnp.sum(duc * taps[k], axis=0, keepdims=True) for k in range(CONV_WIDTH)], axis=0)
        after = duc_sc[...]
        du = cw[CONV_WIDTH - 1:CONV_WIDTH, :] * duc
        for k in range(CONV_WIDTH - 1):
            du = du + cw[k:k + 1, :] * _shift_up(duc, after, CONV_WIDTH - 1 - k)
        duc_sc[...] = duc[0:8, :]
        dz_ref[:, 0:w] = du.astype(BF16)
        dz_ref[:, w:2 * w] = dgate.astype(BF16)

    small = lambda arr: pl.BlockSpec(arr.shape, lambda b, c: (0, 0))
    acc = lambda shape: pl.BlockSpec(shape, lambda b, c: (0, 0))
    chunk_spec = lambda cb_: pl.BlockSpec((LRU_CHUNK, w), lambda b, c: (chunk(b, c), cb_))
    halo_spec = pl.BlockSpec((8, w), lambda b, c: (halo_before(b, c), 0))
    acc_shapes = [(CONV_WIDTH, w), (1, w), (w, w), (1, w), (w, w), (1, w), (1, w)]
    return pl.pallas_call(
        body, grid=(total // SEQ, nchunk),
        in_specs=[chunk_spec(0), halo_spec, chunk_spec(1), chunk_spec(0), halo_spec, chunk_spec(0),
                  small(cw), small(cb), small(wa), small(ba), small(wi), small(bi), small(lam)],
        out_specs=[pl.BlockSpec((LRU_CHUNK, 2 * w), lambda b, c: (chunk(b, c), 0))] + [acc(s) for s in acc_shapes],
        out_shape=[jax.ShapeDtypeStruct((total, 2 * w), BF16)] + [jax.ShapeDtypeStruct(s, F32) for s in acc_shapes],
        scratch_shapes=[pltpu.VMEM((LRU_CHUNK, w), F32)] * 3 + [pltpu.VMEM((8, w), F32)] * 3,
        name=name, compiler_params=_params("arbitrary", "arbitrary"),
    )(zug, zug, zug, hl, hl, dy, cw, cb, wa, ba, wi, bi, lam)


def _block_diag(wb):
    eye = jnp.eye(LRU_BLOCKS, dtype=wb.dtype)
    return jnp.einsum("gcd,gh->gchd", wb, eye).reshape(LRU_WIDTH, LRU_WIDTH).astype(BF16)


def _diag_blocks(wd):
    blk = LRU_WIDTH // LRU_BLOCKS
    return jnp.stack([wd[g * blk:(g + 1) * blk, g * blk:(g + 1) * blk] for g in range(LRU_BLOCKS)])


FOX_SCAN = 256


def _fox_bias(fl, bf, *, name):
    nb = fl.shape[0]

    def body(fl_ref, bf_ref, c_ref):
        x = fl_ref[0] + bf_ref[...]
        logf = jnp.minimum(x, 0.0) - jnp.log(1.0 + jnp.exp(-jnp.abs(x)))
        upper = (lax.broadcasted_iota(jnp.int32, (FOX_SCAN, FOX_SCAN), 0)
                 <= lax.broadcasted_iota(jnp.int32, (FOX_SCAN, FOX_SCAN), 1)).astype(BF16)
        carry = jnp.zeros((LRU_BLOCKS, 1), F32)
        for j in range(SEQ // FOX_SCAN):
            blk = logf[:, j * FOX_SCAN:(j + 1) * FOX_SCAN]
            c_ref[0, :, j * FOX_SCAN:(j + 1) * FOX_SCAN] = _dot_exact(blk, upper) + carry
            carry = carry + jnp.sum(blk, axis=1, keepdims=True)

    return pl.pallas_call(
        body, grid=(nb,),
        in_specs=[pl.BlockSpec((1, 8, SEQ), lambda b: (b, 0, 0)), pl.BlockSpec((8, 1), lambda b: (0, 0))],
        out_specs=pl.BlockSpec((1, 8, SEQ), lambda b: (b, 0, 0)),
        out_shape=jax.ShapeDtypeStruct((nb, 8, SEQ), F32), name=name, compiler_params=_params("arbitrary"),
    )(fl, bf)


def _fox_bias_bwd(fl, bf, dc, *, name):
    nb = fl.shape[0]

    def body(fl_ref, bf_ref, dc_ref, dfl_ref, dbf_ref):
        @pl.when(pl.program_id(0) == 0)
        def _():
            dbf_ref[...] = jnp.zeros(dbf_ref.shape, F32)

        x = fl_ref[0] + bf_ref[...]
        dc_all = dc_ref[0]
        lower = (lax.broadcasted_iota(jnp.int32, (FOX_SCAN, FOX_SCAN), 0)
                 >= lax.broadcasted_iota(jnp.int32, (FOX_SCAN, FOX_SCAN), 1)).astype(BF16)
        carry = jnp.zeros((LRU_BLOCKS, 1), F32)
        total = jnp.zeros((LRU_BLOCKS, 1), F32)
        for j in reversed(range(SEQ // FOX_SCAN)):
            cols = slice(j * FOX_SCAN, (j + 1) * FOX_SCAN)
            blk = dc_all[:, cols]
            dlogf = _dot_exact(blk, lower) + carry
            carry = carry + jnp.sum(blk, axis=1, keepdims=True)
            dx = dlogf * jax.nn.sigmoid(-x[:, cols])
            dfl_ref[0, :, cols] = dx
            total = total + jnp.sum(dx, axis=1, keepdims=True)
        dbf_ref[...] += total

    return pl.pallas_call(
        body, grid=(nb,),
        in_specs=[pl.BlockSpec((1, 8, SEQ), lambda b: (b, 0, 0)), pl.BlockSpec((8, 1), lambda b: (0, 0)),
                  pl.BlockSpec((1, 8, SEQ), lambda b: (b, 0, 0))],
        out_specs=[pl.BlockSpec((1, 8, SEQ), lambda b: (b, 0, 0)), pl.BlockSpec((8, 1), lambda b: (0, 0))],
        out_shape=[jax.ShapeDtypeStruct((nb, 8, SEQ), F32), jax.ShapeDtypeStruct((8, 1), F32)],
        name=name, compiler_params=_params("arbitrary"),
    )(fl, bf, dc)


def _seq3(a, nb):
    return a.reshape(nb, a.shape[0] // nb, a.shape[1])


def _flat2(a):
    return a.reshape(a.shape[0] * a.shape[1], a.shape[2])


def _mlp_fwd(h, p, tag):
    hn = _rms_fwd(h, p["norm"], name=f"{tag}_norm")
    up, act = _matmul(hn, p["w_up"], outs=(F32, BF16), epilogue=lambda acc: (acc, jnp.square(jnp.maximum(acc, 0.0))),
                      name=f"{tag}_up")
    out, = _matmul(act, p["w_down"], extras=(h,), epilogue=lambda acc, res: (acc + res,), name=f"{tag}_down")
    return out, (h, hn, up, act)


def _mlp_bwd(dh, dhb, p, saved, tag):
    h, hn, up, act = saved
    dup, = _matmul(dhb, p["w_down"], tb=True, outs=(BF16,), extras=(up,),
                   epilogue=lambda acc, up: (acc * (2.0 * jnp.maximum(up, 0.0)),), name=f"{tag}_bwd_dup")
    dw_down, = _matmul(act, dhb, ta=True, name=f"{tag}_bwd_dwdown")
    dw_up, = _matmul(hn, dup, ta=True, name=f"{tag}_bwd_dwup")
    dhn, = _matmul(dup, p["w_up"], tb=True, name=f"{tag}_bwd_dhn")
    dh2, dh2b, dg = _rms_bwd(h, dhn, dh, p["norm"], name=f"{tag}_bwd_norm")
    return dh2, dh2b, {"norm": dg, "w_up": dw_up, "w_down": dw_down}


def _xa_fwd(h, mem, p, tag, nb):
    hn = _rms_fwd(h, p["norm"], name=f"{tag}_norm")
    mem_n = _rms_fwd(mem, p["mem_norm"], name=f"{tag}_memnorm")
    q, = _matmul(hn, p["w_q"], outs=(BF16,), name=f"{tag}_q")
    kv, = _matmul(mem_n, p["w_kv"], outs=(BF16,), name=f"{tag}_kv")
    q3, kv3 = _seq3(q, nb), _seq3(kv, nb)
    o, lse = _attn_fwd((q3, 0), (kv3, 0), (kv3, 1), heads=XA_HEADS, kv_heads=XA_HEADS, dh=XA_HEAD_DIM, mode="full",
                       name=f"{tag}_attn")
    o = _flat2(o)
    ob, = _rowwise(lambda o: o, [o], [], [(D_MODEL, BF16)], name=f"{tag}_ocast")
    out, = _matmul(ob, p["w_o"], extras=(h,), epilogue=lambda acc, res: (acc + res,), name=f"{tag}_o")
    return out, (h, hn, mem_n, q3, kv3, o, ob, lse)


def _xa_bwd(dh, dhb, mem, p, saved, tag, nb):
    h, hn, mem_n, q3, kv3, o, ob, lse = saved
    do, dob = _matmul(dhb, p["w_o"], tb=True, outs=(F32, BF16), epilogue=lambda acc: (acc, acc), name=f"{tag}_bwd_do")
    dw_o, = _matmul(ob, dhb, ta=True, name=f"{tag}_bwd_dwo")
    delta = _head_delta(do, o, XA_HEADS, name=f"{tag}_bwd_delta")
    dq, dk, dv = _attn_bwd((q3, 0), (kv3, 0), (kv3, 1), (_seq3(dob, nb), 0), lse, _seq3(delta, nb), heads=XA_HEADS,
                           kv_heads=XA_HEADS, dh=XA_HEAD_DIM, mode="full", name=f"{tag}_bwd_attn")
    dqb, = _rowwise(lambda x: x, [_flat2(dq)], [], [(D_MODEL, BF16)], name=f"{tag}_bwd_dqcast")
    dkvb, = _rowwise(lambda a, b: jnp.concatenate([a, b], axis=1), [_flat2(dk), _flat2(dv)], [], [(2 * D_MODEL, BF16)],
                     name=f"{tag}_bwd_dkvcast")
    dw_q, = _matmul(hn, dqb, ta=True, name=f"{tag}_bwd_dwq")
    dw_kv, = _matmul(mem_n, dkvb, ta=True, name=f"{tag}_bwd_dwkv")
    dhn, = _matmul(dqb, p["w_q"], tb=True, name=f"{tag}_bwd_dhn")
    dmem_n, = _matmul(dkvb, p["w_kv"], tb=True, name=f"{tag}_bwd_dmemn")
    dg_mem, = _rowwise(lambda x, d, g: _rms_bwd_vals(x, d, g)[1], [mem, dmem_n], [p["mem_norm"]], [], [(1, D_MODEL)],
                       name=f"{tag}_bwd_memnorm")
    dh2, dh2b, dg = _rms_bwd(h, dhn, dh, p["norm"], name=f"{tag}_bwd_norm")
    return dh2, dh2b, {"norm": dg, "mem_norm": dg_mem, "w_q": dw_q, "w_kv": dw_kv, "w_o": dw_o}


AB_MAIN = 2 * LRU_WIDTH + 3 * ATT_W


def _ab_fwd(h, p, nb):
    hn = _rms_fwd(h, p["norm"], name="ab_norm")
    zug, = _matmul(hn, p["w_ug"], name="ab_in_ug")
    qkv, = _matmul(hn, p["w_qkv"], outs=(BF16,), name="ab_in_qkv")
    zf, = _matmul(hn, p["w_f"], name="ab_in_f")
    fl = zf[:, :8].reshape(nb, SEQ, 8).transpose(0, 2, 1)
    cbias = _fox_bias(fl, p["b_f"], name="ab_fox_bias")
    kb = cbias.reshape(nb, 8, SEQ // ATT_BLOCK, ATT_BLOCK)
    y_a, hl = _lru_fwd(zug, p["conv_w"], p["conv_b"], p["w_a"], p["b_a"], p["w_i"], p["b_i"], p["lam"], name="ab_lru")
    qkv3 = _seq3(qkv, nb)
    o, lse = _attn_fwd((qkv3, 0), (qkv3, 1), (qkv3, 2), kb, heads=8, kv_heads=8, dh=HEAD_DIM, mode="causal", name="ab_fox")
    o = _flat2(o)
    y, = _rowwise(lambda a, b: jnp.concatenate([a, b.astype(BF16)], axis=1), [y_a, o], [], [(D_MODEL, BF16)], name="ab_ycat")
    out, = _matmul(y, p["w_out"], extras=(h,), epilogue=lambda acc, res: (acc + res,), name="ab_out")
    return out, (h, hn, zug, qkv3, fl, kb, hl, o, lse, y)


def _ab_bwd(dh, dhb, p, saved, nb):
    h, hn, zug, qkv3, fl, kb, hl, o, lse, y = saved
    dy, dyb = _matmul(dhb, p["w_out"], tb=True, outs=(F32, BF16), epilogue=lambda acc: (acc, acc), name="ab_bwd_dy")
    dw_out, = _matmul(y, dhb, ta=True, name="ab_bwd_dwout")
    dzug, dcw, dcb, dwa, dba, dwi, dbi, dlam = _lru_bwd(zug, hl, dy, p["conv_w"], p["conv_b"], p["w_a"], p["b_a"],
                                                        p["w_i"], p["b_i"], p["lam"], name="ab_bwd_lru")
    delta = _head_delta((dy, ATT_W, 1), o, 8, name="ab_bwd_delta")
    dq, dk, dv, dkb, drow = _attn_bwd((qkv3, 0), (qkv3, 1), (qkv3, 2), (_seq3(dyb, nb), 1), lse, _seq3(delta, nb), kb,
                                      heads=8, kv_heads=8, dh=HEAD_DIM, mode="causal", name="ab_bwd_fox")
    dcum = dkb.reshape(nb, 8, SEQ) + drow[:, :, :8].transpose(0, 2, 1)
    dfl, dbf = _fox_bias_bwd(fl, p["b_f"], dcum, name="ab_bwd_fox_bias")
    dzf = jnp.pad(dfl.transpose(0, 2, 1).reshape(nb * SEQ, 8), ((0, 0), (0, LANES - 8)))
    dz, = _rowwise(lambda a, q, k, v, f: jnp.concatenate([a, q.astype(BF16), k.astype(BF16), v.astype(BF16), f.astype(BF16)], axis=1),
                   [dzug, _flat2(dq), _flat2(dk), _flat2(dv), dzf], [], [(AB_MAIN + LANES, BF16)], name="ab_bwd_dzcat")
    dw_in, = _matmul(hn, dz, ta=True, tn=384, name="ab_bwd_dwin")
    dhn, = _matmul(dz, p["w_in_pad"], tb=True, tk=AB_MAIN + LANES, name="ab_bwd_dhn")
    dh2, dh2b, dg = _rms_bwd(h, dhn, dh, p["norm"], name="ab_bwd_norm")
    grads = {"norm": dg, "w_in": dw_in[:, :AB_MAIN + 8], "conv_w": dcw, "conv_b": dcb, "w_a": _diag_blocks(dwa), "b_a": dba,
             "w_i": _diag_blocks(dwi), "b_i": dbi, "lam": dlam, "b_f": dbf.reshape(1, 8), "w_out": dw_out}
    return dh2, dh2b, grads


CD_IN = 3 * ATT_W + ATT_W + 2 * SWA_KW


def _class_view(a, dil):
    return a.reshape(a.shape[0], a.shape[1] // dil, dil * a.shape[2])


def _cd_fwd(h, p, nb):
    hn = _rms_fwd(h, p["norm"], name="cd_norm")
    z, = _matmul(hn, p["w_in"], tn=384, name="cd_in")
    cos, sin = _rope_tables()
    per_seq = SEQ // ROW_TILE
    table_map = lambda i: (i % per_seq, 0)

    def rope_fn(z, cos, sin):
        qc, kc, vc = z[:, 0:ATT_W], z[:, ATT_W:2 * ATT_W], z[:, 2 * ATT_W:3 * ATT_W]
        qd, kd, vd = z[:, 3 * ATT_W:4 * ATT_W], z[:, 4 * ATT_W:4 * ATT_W + SWA_KW], z[:, 4 * ATT_W + SWA_KW:]
        return (jnp.concatenate([_rotate(qc, cos, sin), _rotate(kc, cos, sin)], axis=1), vc, _rotate(qd, cos, sin),
                _rotate(kd, cos, sin), vd)
    qk, vc, qd, kd, vd = _rowwise(rope_fn, [z, cos, sin], [], [(2 * ATT_W, BF16), (ATT_W, BF16), (ATT_W, BF16), (SWA_KW, BF16),
                                                                (SWA_KW, BF16)], name="cd_rope", row_maps=[None, table_map, table_map])
    qk3, vc3 = _seq3(qk, nb), _seq3(vc, nb)
    branch = []
    for window, dil in DIL_PATTERN:
        o_i, lse_i = _attn_fwd((_class_view(qk3, dil), 0), (_class_view(qk3, dil), 1), (_class_view(vc3, dil), 0), classes=dil,
                               heads=8, kv_heads=8, dh=HEAD_DIM, mode="band", max_dist=window // dil, name=f"cd_dil{dil}")
        branch.append((o_i.reshape(nb * SEQ, ATT_W), lse_i.reshape(nb * SEQ, LANES)))
    expand = _head_expand(8, ATT_W)

    def combine(o1, o2, o3, l1, l2, l3, e):
        m = jnp.maximum(jnp.maximum(l1, l2), l3)
        lt = m + jnp.log(jnp.exp(l1 - m) + jnp.exp(l2 - m) + jnp.exp(l3 - m))
        o = sum(_dot_exact(jnp.exp(l - lt), e) * o_ for o_, l in ((o1, l1), (o2, l2), (o3, l3)))
        return o, lt
    y_c, lse_c = _rowwise(combine, [b[0] for b in branch] + [b[1] for b in branch], [expand], [(ATT_W, F32), (LANES, F32)],
                          name="cd_dil_combine")
    qd3, kd3, vd3 = _seq3(qd, nb), _seq3(kd, nb), _seq3(vd, nb)
    o_d, lse_band = _attn_fwd((qd3, 0), (kd3, 0), (vd3, 0), heads=8, kv_heads=2, dh=HEAD_DIM, mode="band",
                              max_dist=SWA_WINDOW - 1, name="cd_swa")

    def sink_combine(o, l, e, sink):
        lt = jnp.maximum(l, sink) + jnp.log(1.0 + jnp.exp(-jnp.abs(l - sink)))
        return o * _dot_exact(jnp.exp(l - lt), e), lt
    y_d, lse_d = _rowwise(sink_combine, [_flat2(o_d), _flat2(lse_band)], [expand, p["sink"]], [(ATT_W, F32), (LANES, F32)],
                          name="cd_swa_sink")
    y, = _rowwise(lambda a, b: jnp.concatenate([a, b], axis=1), [y_c, y_d], [], [(D_MODEL, BF16)], name="cd_ycat")
    out, = _matmul(y, p["w_out"], extras=(h,), epilogue=lambda acc, res: (acc + res,), name="cd_out")
    return out, (h, hn, qk3, vc3, qd3, kd3, vd3, y_c, lse_c, y_d, lse_d, y)


def _cd_bwd(dh, dhb, p, saved, nb):
    h, hn, qk3, vc3, qd3, kd3, vd3, y_c, lse_c, y_d, lse_d, y = saved
    dy, dyb = _matmul(dhb, p["w_out"], tb=True, outs=(F32, BF16), epilogue=lambda acc: (acc, acc), name="cd_bwd_dy")
    dw_out, = _matmul(y, dhb, ta=True, name="cd_bwd_dwout")
    dyb3 = _seq3(dyb, nb)
    delta_c = _head_delta((dy, ATT_W, 0), y_c, 8, name="cd_bwd_delta_dil")
    lse_c3, delta_c3 = _seq3(lse_c, nb), _seq3(delta_c, nb)
    parts = []
    for window, dil in DIL_PATTERN:
        cv = functools.partial(_class_view, dil=dil)
        dq_i, dk_i, dv_i = _attn_bwd((cv(qk3), 0), (cv(qk3), 1), (cv(vc3), 0), (cv(dyb3), 0), cv(lse_c3), cv(delta_c3),
                                     classes=dil, heads=8, kv_heads=8, dh=HEAD_DIM, mode="band", max_dist=window // dil,
                                     name=f"cd_bwd_dil{dil}")
        parts.append([a.reshape(nb * SEQ, ATT_W) for a in (dq_i, dk_i, dv_i)])
    delta_d, dsink = _head_delta((dy, ATT_W, 1), y_d, 8, lse=lse_d, sink=p["sink"], name="cd_bwd_delta_swa")
    dqd, dkd, dvd = _attn_bwd((qd3, 0), (kd3, 0), (vd3, 0), (dyb3, 1), _seq3(lse_d, nb), _seq3(delta_d, nb), heads=8, kv_heads=2,
                              dh=HEAD_DIM, mode="band", max_dist=SWA_WINDOW - 1, name="cd_bwd_swa")
    cos, sin = _rope_tables()
    per_seq = SEQ // ROW_TILE
    table_map = lambda i: (i % per_seq, 0)

    def unrope(q1, q2, q3, k1, k2, k3, v1, v2, v3, qd, kd, vd, cos, sin):
        back = lambda x: _rotate(x, cos, -sin)
        return jnp.concatenate([back(q1 + q2 + q3), back(k1 + k2 + k3), v1 + v2 + v3, back(qd), back(kd), vd], axis=1)
    ins = [parts[b][t] for t in range(3) for b in range(3)] + [_flat2(dqd), _flat2(dkd), _flat2(dvd), cos, sin]
    dz, = _rowwise(unrope, ins, [], [(CD_IN, BF16)], name="cd_bwd_unrope", row_maps=[None] * 12 + [table_map, table_map])
    dw_in, = _matmul(hn, dz, ta=True, tn=384, name="cd_bwd_dwin")
    dhn, = _matmul(dz, p["w_in"], tb=True, tk=CD_IN, name="cd_bwd_dhn")
    dh2, dh2b, dg = _rms_bwd(h, dhn, dh, p["norm"], name="cd_bwd_norm")
    return dh2, dh2b, {"norm": dg, "w_in": dw_in, "sink": dsink[:, :8], "w_out": dw_out}


def _local_step(x, mem, target, w):
    nb = x.shape[0]
    h = x.reshape(nb * SEQ, D_MODEL)
    mem2 = mem.reshape(nb * MEM_LEN, D_MODEL)
    tgt = target.reshape(nb * SEQ, D_MODEL)

    h, s_ab = _ab_fwd(h, w["ab"], nb)
    h, s_xa0 = _xa_fwd(h, mem2, w["xa0"], "xa0", nb)
    h, s_mlp0 = _mlp_fwd(h, w["mlp0"], "mlp0")
    h, s_cd = _cd_fwd(h, w["cd"], nb)
    h, s_xa1 = _xa_fwd(h, mem2, w["xa1"], "xa1", nb)
    h, s_mlp1 = _mlp_fwd(h, w["mlp1"], "mlp1")

    dh, dhb, loss, dg_final = _loss_and_grad(h, tgt, w["final_norm"], name="loss")
    grads = {"final_norm": dg_final}
    dh, dhb, grads["mlp1"] = _mlp_bwd(dh, dhb, w["mlp1"], s_mlp1, "mlp1")
    dh, dhb, grads["xa1"] = _xa_bwd(dh, dhb, mem2, w["xa1"], s_xa1, "xa1", nb)
    dh, dhb, grads["cd"] = _cd_bwd(dh, dhb, w["cd"], s_cd, nb)
    dh, dhb, grads["mlp0"] = _mlp_bwd(dh, dhb, w["mlp0"], s_mlp0, "mlp0")
    dh, dhb, grads["xa0"] = _xa_bwd(dh, dhb, mem2, w["xa0"], s_xa0, "xa0", nb)
    dh, dhb, grads["ab"] = _ab_bwd(dh, dhb, w["ab"], s_ab, nb)
    return loss, dh.reshape(nb, SEQ, D_MODEL), grads


ANY = pl.BlockSpec(memory_space=pl.ANY)


def _place():
    x, y, c = lax.axis_index("x"), lax.axis_index("y"), lax.axis_index("c")
    return x, y, c, [(1 - x, y), (x, 1 - y), (1 - x, 1 - y)]


def _gather_chips(shard):
    def body(src, out, send_sems, recv_sems, local_sem):
        x, y, c, chips = _place()
        mine = pltpu.make_async_copy(src, out.at[2 * x + y], local_sem)
        mine.start()
        sends = [pltpu.make_async_remote_copy(src_ref=src, dst_ref=out.at[2 * x + y], send_sem=send_sems.at[k],
                                              recv_sem=recv_sems.at[k], device_id=(px, py, c), device_id_type=MESH)
                 for k, (px, py) in enumerate(chips)]
        for cp in sends:
            cp.start()
        for k, (px, py) in enumerate(chips):
            pltpu.make_async_remote_copy(src_ref=src, dst_ref=out.at[2 * px + py], send_sem=send_sems.at[k],
                                         recv_sem=recv_sems.at[k], device_id=(px, py, c), device_id_type=MESH).wait_recv()
        for cp in sends:
            cp.wait_send()
        mine.wait()

    return pl.pallas_call(
        body, out_shape=jax.ShapeDtypeStruct((N_CHIPS,) + shard.shape, shard.dtype), in_specs=[ANY], out_specs=ANY,
        scratch_shapes=[pltpu.SemaphoreType.DMA((3,)), pltpu.SemaphoreType.DMA((3,)), pltpu.SemaphoreType.DMA(())],
        name="gather_chips",
    )(shard)


def _swap_cores(block, *, name):
    def body(src, out, send_sem, recv_sem):
        x, y, c, _ = _place()
        cp = pltpu.make_async_remote_copy(src_ref=src, dst_ref=out, send_sem=send_sem, recv_sem=recv_sem,
                                          device_id=(x, y, 1 - c), device_id_type=MESH)
        cp.start()
        cp.wait()

    return pl.pallas_call(
        body, out_shape=jax.ShapeDtypeStruct(block.shape, block.dtype), in_specs=[ANY], out_specs=ANY,
        scratch_shapes=[pltpu.SemaphoreType.DMA(()), pltpu.SemaphoreType.DMA(())], name=name,
    )(block)


def _scatter_chips(parts):
    def body(src, out, send_sems, recv_sems):
        x, y, c, chips = _place()
        sends = [pltpu.make_async_remote_copy(src_ref=src.at[2 * px + py], dst_ref=out.at[k], send_sem=send_sems.at[k],
                                              recv_sem=recv_sems.at[k], device_id=(px, py, c), device_id_type=MESH)
                 for k, (px, py) in enumerate(chips)]
        for cp in sends:
            cp.start()
        for cp in sends:
            cp.wait_recv()
        for cp in sends:
            cp.wait_send()

    return pl.pallas_call(
        body, out_shape=jax.ShapeDtypeStruct((3,) + parts.shape[1:], parts.dtype), in_specs=[ANY], out_specs=ANY,
        scratch_shapes=[pltpu.SemaphoreType.DMA((3,)), pltpu.SemaphoreType.DMA((3,))], name="scatter_chips",
    )(parts)


def _sum_all_devices(vec):
    rows = vec.shape[0]

    def body(x_ref, total_ref, all_ref, send_sems, recv_sems, local_sem):
        x, y, c, chips = _place()
        me, sibling = (x, y, c), (x, y, 1 - c)

        def slot(px, py, pc):
            return all_ref.at[4 * px + 2 * py + pc]

        def copy(k, block, to, src=None):
            return pltpu.make_async_remote_copy(src_ref=slot(*block) if src is None else src, dst_ref=slot(*block),
                                                send_sem=send_sems.at[k], recv_sem=recv_sems.at[k], device_id=to,
                                                device_id_type=MESH)

        mine = pltpu.make_async_copy(x_ref, slot(*me), local_sem)
        mine.start()
        first = [copy(0, me, sibling, src=x_ref)] + [copy(1 + j, me, (*chip, c), src=x_ref) for j, chip in enumerate(chips)]
        for cp in first:
            cp.start()
        passed = [copy(4 + j, (*chip, c), sibling) for j, chip in enumerate(chips)]
        for j, chip in enumerate(chips):
            copy(1 + j, (*chip, c), me).wait_recv()
            passed[j].start()
        copy(0, sibling, me).wait_recv()
        for j, chip in enumerate(chips):
            copy(4 + j, (*chip, 1 - c), me).wait_recv()
        for cp in first + passed:
            cp.wait_send()
        mine.wait()
        acc = all_ref[0]
        for d in range(1, 8):
            acc = acc + all_ref[d]
        total_ref[...] = acc

    vmem = pl.BlockSpec(memory_space=pltpu.VMEM)
    return pl.pallas_call(
        body, out_shape=[jax.ShapeDtypeStruct((rows, LANES), F32), jax.ShapeDtypeStruct((8, rows, LANES), F32)],
        in_specs=[vmem], out_specs=[vmem, vmem],
        scratch_shapes=[pltpu.SemaphoreType.DMA((7,)), pltpu.SemaphoreType.DMA((7,)), pltpu.SemaphoreType.DMA(())],
        name="sum_all_devices", compiler_params=pltpu.CompilerParams(vmem_limit_bytes=VMEM_LIMIT_BYTES),
    )(vec)[0]


def _share_halves(half):
    def body(src, out, send_sem, recv_sem, local_sem):
        x, y, c, _ = _place()
        mine = pltpu.make_async_copy(src, out.at[c], local_sem)
        mine.start()
        cp = pltpu.make_async_remote_copy(src_ref=src, dst_ref=out.at[c], send_sem=send_sem, recv_sem=recv_sem,
                                          device_id=(x, y, 1 - c), device_id_type=MESH)
        cp.start()
        pltpu.make_async_remote_copy(src_ref=src, dst_ref=out.at[1 - c], send_sem=send_sem, recv_sem=recv_sem,
                                     device_id=(x, y, 1 - c), device_id_type=MESH).wait_recv()
        cp.wait_send()
        mine.wait()

    return pl.pallas_call(
        body, out_shape=jax.ShapeDtypeStruct((2,) + half.shape, half.dtype), in_specs=[ANY], out_specs=ANY,
        scratch_shapes=[pltpu.SemaphoreType.DMA(()), pltpu.SemaphoreType.DMA(()), pltpu.SemaphoreType.DMA(())],
        name="share_halves",
    )(half)


PACK_COLS = 1024
BIG = (("ab_w_in", (1, 1024, 642), 2), ("ab_w_out", (1, 256, 1024), 1), ("cd_w_in", (1, 1024, 576), 2),
       ("cd_w_out", (1, 256, 1024), 1), ("xa_w_q", (2, 256, 1024), 1), ("xa_w_kv", (2, 1024, 512), 2),
       ("xa_w_o", (2, 256, 1024), 1), ("mlp_w_up", (2, 1024, 1024), 2), ("mlp_w_down", (2, 1024, 1024), 1))
BIG_ROWS = tuple(math.prod(shape) // PACK_COLS for _, shape, _ in BIG)
PACK_ROWS = -(-sum(BIG_ROWS) // 32) * 32
REDUCE_TILE = 208
assert (PACK_ROWS // 2) % REDUCE_TILE == 0

SMALL = (("ab_norm", (1, 1024)), ("ab_conv_w", (1, 4, 512)), ("ab_conv_b", (1, 512)), ("lru_w_a", (1, 8, 64, 64)),
         ("lru_b_a", (1, 512)), ("lru_w_i", (1, 8, 64, 64)), ("lru_b_i", (1, 512)), ("lru_lambda", (1, 512)),
         ("fox_b_f", (1, 8)), ("cd_norm", (1, 1024)), ("cd_sink", (1, 8)), ("xa_norm", (2, 1024)),
         ("xa_mem_norm", (2, 1024)), ("mlp_norm", (2, 1024)), ("final_norm", (1024,)), ("loss", (1,)))
SPLIT_SMALL = ("ab_conv_w", "cd_norm")


def _pack_rows(blocks, dtype):
    lead = blocks[0].shape[:-len(BIG[0][1])]
    flat = [b.astype(dtype).reshape(lead + (r, PACK_COLS)) for b, r in zip(blocks, BIG_ROWS)]
    pad = PACK_ROWS - sum(BIG_ROWS)
    return jnp.concatenate(flat + [jnp.zeros(lead + (pad, PACK_COLS), dtype)], axis=len(lead))


def _unpack_rows(packed):
    lead = packed.shape[:-2]
    out, r0 = [], 0
    for (_, shape, _), r in zip(BIG, BIG_ROWS):
        out.append(packed[..., r0:r0 + r, :].reshape(lead + shape))
        r0 += r
    return out


def _pack_small(vals):
    flat = jnp.concatenate([vals[name].astype(F32).reshape(-1) for name, _ in SMALL])
    size = -(-flat.shape[0] // (8 * LANES)) * (8 * LANES)
    return jnp.pad(flat, (0, size - flat.shape[0])).reshape(-1, LANES)


def _unpack_small(packed):
    flat, out, at = packed.reshape(-1), {}, 0
    for name, shape in SMALL:
        out[name] = flat[at:at + math.prod(shape)].reshape(shape)
        at += math.prod(shape)
    return out


def _chip_part(full, chip):
    width = full.shape[-1] // N_CHIPS
    return lax.dynamic_slice_in_dim(full, chip * width, width, axis=full.ndim - 1)


def _chip_embed(part, chip):
    full = jnp.zeros(part.shape[:-1] + (part.shape[-1] * N_CHIPS,), part.dtype)
    return lax.dynamic_update_slice_in_dim(full, part, chip * part.shape[-1], axis=part.ndim - 1)


def _reduce_to_owner(grads_by_chip, core):
    half = PACK_ROWS // 2
    keep = lax.dynamic_slice_in_dim(grads_by_chip, core * half, half, axis=1)
    give = lax.dynamic_slice_in_dim(grads_by_chip, (1 - core) * half, half, axis=1)
    got = _swap_cores(give, name="swap_cores")
    flat = lambda a: a.reshape(-1, PACK_COLS)
    pair, = _rowwise(lambda a, b: a + b, [flat(keep), flat(got)], [], [(PACK_COLS, F32)], name="sum_cores", tile=REDUCE_TILE)
    pair = pair.reshape(N_CHIPS, half, PACK_COLS)
    landed = _scatter_chips(pair)
    chip = 2 * lax.axis_index("x") + lax.axis_index("y")
    own = lax.dynamic_index_in_dim(pair, chip, axis=0, keepdims=False)
    done, = _rowwise(lambda a, b, c, d: a + b + c + d, [own, landed[0], landed[1], landed[2]], [], [(PACK_COLS, F32)],
                     name="sum_chips", tile=REDUCE_TILE)
    return _share_halves(done).reshape(PACK_ROWS, PACK_COLS)


def kernel(x, mem, ab_norm, ab_w_in, ab_conv_w, ab_conv_b, lru_w_a, lru_b_a, lru_w_i, lru_b_i, lru_lambda, fox_b_f, ab_w_out, cd_norm, cd_w_in, cd_sink, cd_w_out, xa_norm, xa_mem_norm, xa_w_q, xa_w_kv, xa_w_o, mlp_norm, mlp_w_up, mlp_w_down, final_norm, loss_target, m_ab_norm, m_ab_w_in, m_ab_conv_w, m_ab_conv_b, m_lru_w_a, m_lru_b_a, m_lru_w_i, m_lru_b_i, m_lru_lambda, m_fox_b_f, m_ab_w_out, m_cd_norm, m_cd_w_in, m_cd_sink, m_cd_w_out, m_xa_norm, m_xa_mem_norm, m_xa_w_q, m_xa_w_kv, m_xa_w_o, m_mlp_norm, m_mlp_w_up, m_mlp_w_down, m_final_norm, v_ab_norm, v_ab_w_in, v_ab_conv_w, v_ab_conv_b, v_lru_w_a, v_lru_b_a, v_lru_w_i, v_lru_b_i, v_lru_lambda, v_fox_b_f, v_ab_w_out, v_cd_norm, v_cd_w_in, v_cd_sink, v_cd_w_out, v_xa_norm, v_xa_mem_norm, v_xa_w_q, v_xa_w_kv, v_xa_w_o, v_mlp_norm, v_mlp_w_up, v_mlp_w_down, v_final_norm):
    given = dict(locals())
    weight_names = [name for name, _ in SMALL if name != "loss"] + [name for name, _, _ in BIG]
    w = {name: given[name] for name in weight_names}
    chip = 2 * lax.axis_index("x") + lax.axis_index("y")
    core = lax.axis_index("c")

    gathered = _gather_chips(_pack_rows([w[name] for name, _, _ in BIG], BF16))
    full = {}
    for (name, _, axis), blocks in zip(BIG, _unpack_rows(gathered)):
        full[name] = jnp.concatenate([blocks[s] for s in range(N_CHIPS)], axis=axis)
    owner = (core == 0).astype(F32)
    quarters = {name: _chip_embed(w[name], chip) * owner for name in SPLIT_SMALL}
    zeros = {name: jnp.zeros(shape, F32) for name, shape in SMALL}
    small_full = _unpack_small(_sum_all_devices(_pack_small({**zeros, **quarters})))
    conv_w_full, cd_norm_full = small_full["ab_conv_w"], small_full["cd_norm"]

    w_in = full["ab_w_in"][0]
    w_f = jnp.pad(w_in[:, AB_MAIN:], ((0, 0), (0, LANES - 8)))
    params = {
        "ab": dict(norm=w["ab_norm"], w_ug=w_in[:, :2 * LRU_WIDTH], w_qkv=w_in[:, 2 * LRU_WIDTH:AB_MAIN], w_f=w_f,
                   w_in_pad=jnp.concatenate([w_in[:, :AB_MAIN], w_f], axis=1), conv_w=conv_w_full[0], conv_b=w["ab_conv_b"],
                   w_a=_block_diag(w["lru_w_a"][0]), b_a=w["lru_b_a"], w_i=_block_diag(w["lru_w_i"][0]), b_i=w["lru_b_i"],
                   lam=w["lru_lambda"], b_f=w["fox_b_f"].reshape(8, 1), w_out=full["ab_w_out"][0]),
        "cd": dict(norm=cd_norm_full, w_in=full["cd_w_in"][0], sink=jnp.pad(w["cd_sink"], ((0, 0), (0, LANES - 8))),
                   w_out=full["cd_w_out"][0]),
        "final_norm": w["final_norm"].reshape(1, D_MODEL),
    }
    for layer in range(2):
        params[f"xa{layer}"] = dict(norm=w["xa_norm"][layer:layer + 1], mem_norm=w["xa_mem_norm"][layer:layer + 1],
                                    w_q=full["xa_w_q"][layer], w_kv=full["xa_w_kv"][layer], w_o=full["xa_w_o"][layer])
        params[f"mlp{layer}"] = dict(norm=w["mlp_norm"][layer:layer + 1], w_up=full["mlp_w_up"][layer],
                                     w_down=full["mlp_w_down"][layer])

    loss_part, grad_x, g = _local_step(x, mem, loss_target, params)

    pair = lambda key, leaf: jnp.stack([g[f"{key}0"][leaf], g[f"{key}1"][leaf]])
    big_full = {"ab_w_in": g["ab"]["w_in"][None], "ab_w_out": g["ab"]["w_out"][None], "cd_w_in": g["cd"]["w_in"][None],
                "cd_w_out": g["cd"]["w_out"][None], "xa_w_q": pair("xa", "w_q"), "xa_w_kv": pair("xa", "w_kv"),
                "xa_w_o": pair("xa", "w_o"), "mlp_w_up": pair("mlp", "w_up"), "mlp_w_down": pair("mlp", "w_down")}
    by_chip = _pack_rows([jnp.stack(jnp.split(big_full[name], N_CHIPS, axis=axis)) for name, _, axis in BIG], F32)
    grads = dict(zip([name for name, _, _ in BIG], _unpack_rows(_reduce_to_owner(by_chip, core))))

    small_local = {"ab_norm": g["ab"]["norm"], "ab_conv_w": g["ab"]["conv_w"], "ab_conv_b": g["ab"]["conv_b"],
                   "lru_w_a": g["ab"]["w_a"], "lru_b_a": g["ab"]["b_a"], "lru_w_i": g["ab"]["w_i"], "lru_b_i": g["ab"]["b_i"],
                   "lru_lambda": g["ab"]["lam"], "fox_b_f": g["ab"]["b_f"], "cd_norm": g["cd"]["norm"], "cd_sink": g["cd"]["sink"],
                   "xa_norm": pair("xa", "norm"), "xa_mem_norm": pair("xa", "mem_norm"), "mlp_norm": pair("mlp", "norm"),
                   "final_norm": g["final_norm"], "loss": loss_part[0, :1]}
    small_sum_packed = _sum_all_devices(_pack_small(small_local))
    small_sum = _unpack_small(small_sum_packed)
    loss = small_sum["loss"][0]

    delta, new_m, new_v = {}, {}, {}
    for name, shape, _ in BIG:
        flat = lambda a: a.reshape(-1, shape[-1])
        d, m2, v2 = _adamw(flat(w[name]), flat(grads[name]), flat(given["m_" + name]), flat(given["v_" + name]), name=f"adamw_{name}")
        delta[name], new_m[name], new_v[name] = d.reshape(shape), m2.reshape(shape), v2.reshape(shape)

    def small_state(prefix):
        vals = {name: (given[prefix + name] if name != "loss" else jnp.zeros((1,), F32)) for name, _ in SMALL}
        for name in SPLIT_SMALL:
            vals[name] = _chip_embed(vals[name], chip)
        return _pack_small(vals)
    packed = _adamw(small_state(""), small_sum_packed, small_state("m_"), small_state("v_"), name="adamw_small")
    for store, vals in zip((delta, new_m, new_v), packed):
        for name, val in _unpack_small(vals).items():
            store[name] = _chip_part(val, chip) if name in SPLIT_SMALL else val
    for name in SPLIT_SMALL:
        small_sum[name] = _chip_part(small_sum[name], chip)
    grads.update({name: small_sum[name] for name, _ in SMALL})

    order = ["ab_norm", "ab_w_in", "ab_conv_w", "ab_conv_b", "lru_w_a", "lru_b_a", "lru_w_i", "lru_b_i", "lru_lambda", "fox_b_f",
             "ab_w_out", "cd_norm", "cd_w_in", "cd_sink", "cd_w_out", "xa_norm", "xa_mem_norm", "xa_w_q", "xa_w_kv", "xa_w_o",
             "mlp_norm", "mlp_w_up", "mlp_w_down", "final_norm"]
    return (loss, grad_x, *[grads[n] for n in order], *[delta[n] for n in order], *[new_m[n] for n in order],
            *[new_v[n] for n in order])
```

```python
import functools
import math

import jax
import jax.numpy as jnp
from jax import lax
from jax.experimental import pallas as pl
from jax.experimental.pallas import tpu as pltpu

F32 = jnp.float32
BF16 = jnp.bfloat16
MESH = pl.DeviceIdType.MESH

D_MODEL = 1024
SEQ = 2048
HEAD_DIM = 64
LRU_WIDTH = 512
LRU_BLOCKS = 8
LRU_C = 8.0
CONV_WIDTH = 4
ATT_W = 512
SWA_KW = 128
SWA_WINDOW = 128
DIL_PATTERN = ((128, 1), (512, 4), (2048, 16))
MEM_LEN = 256
XA_HEADS = 4
XA_HEAD_DIM = 256
D_FF = 4096
ROPE_THETA = 10000.0
EPS = 1e-6
N_CHIPS = 4
LANES = 128

ADAM_LR, ADAM_B1, ADAM_B2, ADAM_EPS, ADAM_WD, ADAM_STEP = 0.001, 0.9, 0.999, 1e-08, 0.01, 10

VMEM_LIMIT_BYTES = 56 * 1024 * 1024
MASKED = -1e30
ROW_TILE = 512
LRU_CHUNK = 512
ATT_BLOCK = 128
ATT_CHUNK = 512
CAUSAL_ROWS = 256


def _params(*sem):
    return pltpu.CompilerParams(dimension_semantics=sem, vmem_limit_bytes=VMEM_LIMIT_BYTES)


def _rowwise(fn, rows, consts=(), out_rows=(), out_accs=(), *, name, tile=ROW_TILE, row_maps=None):
    rows = [r if isinstance(r, tuple) else (r, r.shape[1], 0) for r in rows]
    consts = list(consts)
    nr, nc, no = len(rows), len(consts), len(out_rows)
    total = rows[0][0].shape[0]
    tile = min(tile, total)
    assert total % tile == 0
    n = total // tile

    def body(*refs):
        outs = fn(*[r[...] for r in refs[:nr + nc]])
        outs = tuple(outs) if isinstance(outs, (tuple, list)) else (outs,)
        for ref, val in zip(refs[nr + nc:nr + nc + no], outs[:no]):
            ref[...] = val.astype(ref.dtype)
        for ref, val in zip(refs[nr + nc + no:], outs[no:]):
            @pl.when(pl.program_id(0) == 0)
            def _(ref=ref):
                ref[...] = jnp.zeros(ref.shape, ref.dtype)
            ref[...] += val

    in_specs = []
    for idx, (arr, width, cb) in enumerate(rows):
        if row_maps is not None and row_maps[idx] is not None:
            in_specs.append(pl.BlockSpec((tile, width), row_maps[idx]))
        else:
            in_specs.append(pl.BlockSpec((tile, width), functools.partial(lambda i, cb: (i, cb), cb=cb)))
    in_specs += [pl.BlockSpec(c.shape, lambda i: (0, 0)) for c in consts]
    out_shape = [jax.ShapeDtypeStruct((total, w), dt) for w, dt in out_rows]
    out_shape += [jax.ShapeDtypeStruct(s, F32) for s in out_accs]
    out_specs = [pl.BlockSpec((tile, w), lambda i: (i, 0)) for w, _ in out_rows]
    out_specs += [pl.BlockSpec(s, lambda i: (0, 0)) for s in out_accs]
    return pl.pallas_call(
        body, grid=(n,), in_specs=in_specs, out_specs=out_specs, out_shape=out_shape, name=name,
        compiler_params=_params("arbitrary"),
    )(*[r[0] for r in rows], *consts)


def _matmul(a, b, *, ta=False, tb=False, outs=(F32,), epilogue=None, extras=(), tm=1024, tn=512, tk=1024, name):
    m, k = (a.shape[1], a.shape[0]) if ta else a.shape
    n = b.shape[0] if tb else b.shape[1]
    assert (b.shape[1] if tb else b.shape[0]) == k
    tm, tn, tk = min(tm, m), min(tn, n), min(tk, k)
    assert m % tm == 0 and n % tn == 0 and k % tk == 0, (name, m, n, k)
    nk = k // tk
    ne, no = len(extras), len(outs)
    dims = (((0 if ta else 1,), (1 if tb else 0,)), ((), ()))

    def body(*refs):
        a_ref, b_ref = refs[:2]
        e_refs, o_refs = refs[2:2 + ne], refs[2 + ne:2 + ne + no]

        def finish(acc):
            vals = (acc,) if epilogue is None else epilogue(acc, *[e[...] for e in e_refs])
            for ref, val in zip(o_refs, vals):
                ref[...] = val.astype(ref.dtype)

        prod = lax.dot_general(a_ref[...], b_ref[...], dims, preferred_element_type=F32)
        if nk == 1:
            finish(prod)
        else:
            acc_ref = refs[-1]
            step = pl.program_id(2)

            @pl.when(step == 0)
            def _():
                acc_ref[...] = prod

            @pl.when(step > 0)
            def _():
                acc_ref[...] += prod

            @pl.when(step == nk - 1)
            def _():
                finish(acc_ref[...])

    a_spec = pl.BlockSpec((tk, tm), lambda i, j, s: (s, i)) if ta else pl.BlockSpec((tm, tk), lambda i, j, s: (i, s))
    b_spec = pl.BlockSpec((tn, tk), lambda i, j, s: (j, s)) if tb else pl.BlockSpec((tk, tn), lambda i, j, s: (s, j))
    tile_spec = pl.BlockSpec((tm, tn), lambda i, j, s: (i, j))
    return pl.pallas_call(
        body, grid=(m // tm, n // tn, nk),
        in_specs=[a_spec, b_spec] + [tile_spec] * ne,
        out_specs=[tile_spec] * no,
        out_shape=[jax.ShapeDtypeStruct((m, n), dt) for dt in outs],
        scratch_shapes=[pltpu.VMEM((tm, tn), F32)] if nk > 1 else [],
        name=name, compiler_params=_params("parallel", "parallel", "arbitrary"),
    )(a, b, *extras)


def _split3(x):
    x1 = x.astype(BF16)
    r1 = x - x1.astype(F32)
    x2 = r1.astype(BF16)
    x3 = (r1 - x2.astype(F32)).astype(BF16)
    return x1, x2, x3


def _dot_exact(x, e):
    return sum(jnp.dot(p, e, preferred_element_type=F32) for p in _split3(x))


def _head_expand(heads, width):
    dh = width // heads
    rows = jnp.arange(LANES)[:, None]
    cols = jnp.arange(width)[None, :]
    return (cols // dh == rows).astype(BF16)


def _rstd(x):
    return lax.rsqrt(jnp.mean(x * x, axis=-1, keepdims=True) + EPS)


def _rms_fwd(x, gain, *, name):
    return _rowwise(lambda x, g: x * _rstd(x) * g, [x], [gain], [(x.shape[1], BF16)], name=name)[0]


def _rms_bwd_vals(x, dhn, gain):
    r = _rstd(x)
    xh = x * r
    dxh = dhn * gain
    dx = r * (dxh - xh * jnp.mean(dxh * xh, axis=-1, keepdims=True))
    return dx, jnp.sum(dhn * xh, axis=0, keepdims=True)


def _rms_bwd(x, dhn, dres, gain, *, name):
    def fn(x, dhn, dres, g):
        dx, dg = _rms_bwd_vals(x, dhn, g)
        dh = dres + dx
        return dh, dh, dg
    d = x.shape[1]
    return _rowwise(fn, [x, dhn, dres], [gain], [(d, F32), (d, BF16)], [(1, d)], name=name)


def _loss_and_grad(h, target, gain, *, name):
    def fn(h, tgt, g):
        r = _rstd(h)
        err = h * r * g - tgt
        loss = 0.5 * jnp.sum(jnp.mean(err * err, axis=-1, keepdims=True), axis=0, keepdims=True)
        dx, dg = _rms_bwd_vals(h, err * (1.0 / D_MODEL), g)
        return dx, dx, jnp.broadcast_to(loss, (1, LANES)), dg
    d = h.shape[1]
    return _rowwise(fn, [h, target], [gain], [(d, F32), (d, BF16)], [(1, LANES), (1, d)], name=name)


def _adamw(w, g, m, v, *, name):
    rows, cols = w.shape
    tile = rows
    for cand in (256, 128, 64, 32, 16, 8):
        if rows % cand == 0 and rows > cand:
            tile = cand
            break
    bc1 = 1.0 - ADAM_B1 ** ADAM_STEP
    bc2 = 1.0 - ADAM_B2 ** ADAM_STEP

    def fn(w, g, m, v):
        m2 = ADAM_B1 * m + (1.0 - ADAM_B1) * g
        v2 = ADAM_B2 * v + (1.0 - ADAM_B2) * (g * g)
        delta = -ADAM_LR * ((m2 / bc1) / (jnp.sqrt(v2 / bc2) + ADAM_EPS) + ADAM_WD * w)
        return delta, m2, v2
    return _rowwise(fn, [w, g, m, v], [], [(cols, F32)] * 3, name=name, tile=tile)


def _attn_specs(arr, width, cb, classes, rows_block, whole):
    ncols = arr.shape[2] // (classes * width)
    if whole:
        return pl.BlockSpec((1, arr.shape[1], width), lambda n, r, i: (n, 0, r * ncols + cb))
    return pl.BlockSpec((1, rows_block, width), lambda n, r, i: (n, i, r * ncols + cb))


def _attn_tiles(mode, lq, lk):
    if mode == "full":
        return min(lq, 256), lk
    if mode == "band":
        return ATT_BLOCK, min(2 * ATT_BLOCK, lk)
    return CAUSAL_ROWS, ATT_CHUNK


def _window(mode, i, j, tq, win, lk):
    if mode == "causal":
        return pl.multiple_of(j * win, win), (i * tq) // win + 1
    if mode == "band":
        return pl.multiple_of(jnp.clip(i * tq + tq - win, 0, lk - win), ATT_BLOCK), 1
    return 0, 1


def _allowed(mode, i, start, tq, win, max_dist):
    if mode == "full":
        return None
    dist = (i * tq + lax.broadcasted_iota(jnp.int32, (tq, win), 0)) - (start + lax.broadcasted_iota(jnp.int32, (tq, win), 1))
    ok = dist >= 0
    if max_dist is not None:
        ok = ok & (dist <= max_dist)
    return ok


def _head_groups(heads, dh):
    gw = max(LANES, dh)
    return gw, gw // dh, heads // (gw // dh)


def _own_lanes(rows, gw, dh, u):
    return lax.broadcasted_iota(jnp.int32, (rows, gw), 1) // dh == u


def _only_head(x, own, per, u):
    return x if per == 1 else jnp.where(own[u], x, jnp.zeros_like(x))


def _step_scores(qz, kw, kb_row, ok, scale):
    s = lax.dot_general(qz, kw, (((1,), (1,)), ((), ())), preferred_element_type=F32) * scale
    if kb_row is not None:
        s = s - kb_row
    if ok is not None:
        s = jnp.where(ok, s, MASKED)
    return s


def _attn_fwd(q, k, v, kb=None, *, classes=1, heads, dh, mode, max_dist=None, name):
    (qa, qc), (ka, kc), (va, vc) = q, k, v
    n, lq, lk = qa.shape[0], qa.shape[1], ka.shape[1]
    width = heads * dh
    tq, win = _attn_tiles(mode, lq, lk)
    gw, per, groups = _head_groups(heads, dh)
    scale = dh ** -0.5
    has_kb = kb is not None
    single = mode != "causal"

    def body(*refs):
        q_ref, k_ref, v_ref = refs[:3]
        kb_ref = refs[3] if has_kb else None
        o_ref, lse_ref = refs[-2:]
        i = pl.program_id(2)
        lane = lax.broadcasted_iota(jnp.int32, (tq, LANES), 1)
        own = [_own_lanes(tq, gw, dh, u) for u in range(per)]

        def step(j, carry):
            m_all, l_all = carry
            start, _ = _window(mode, i, j, tq, win, lk)
            ok = _allowed(mode, i, start, tq, win, max_dist)
            for g in range(groups):
                cols = slice(g * gw, (g + 1) * gw)
                qg = q_ref[0, :, cols]
                kw = k_ref[0, pl.ds(start, win), cols]
                vw = v_ref[0, pl.ds(start, win), cols]
                alpha_g = new_g = None
                for u in range(per):
                    h = g * per + u
                    kb_row = kb_ref[0, h, pl.ds(j, 1), :] if has_kb else None
                    s = _step_scores(_only_head(qg, own, per, u), kw, kb_row, ok, scale)
                    m_old, l_old = m_all[:, h:h + 1], l_all[:, h:h + 1]
                    m_new = jnp.maximum(m_old, jnp.max(s, axis=1, keepdims=True))
                    alpha = jnp.exp(m_old - m_new)
                    p = jnp.exp(s - m_new)
                    l_new = alpha * l_old + jnp.sum(p, axis=1, keepdims=True)
                    r = jnp.dot(p.astype(BF16), vw, preferred_element_type=F32)
                    alpha_g = alpha if u == 0 else jnp.where(own[u], alpha, alpha_g)
                    new_g = r if u == 0 else jnp.where(own[u], r, new_g)
                    m_all = jnp.where(lane == h, m_new, m_all)
                    l_all = jnp.where(lane == h, l_new, l_all)
                o_ref[0, :, cols] = new_g if single else alpha_g * o_ref[0, :, cols] + new_g
            return m_all, l_all

        carry = (jnp.full((tq, LANES), MASKED, F32), jnp.zeros((tq, LANES), F32))
        if single:
            m_all, l_all = step(0, carry)
        else:
            o_ref[...] = jnp.zeros(o_ref.shape, F32)
            m_all, l_all = lax.fori_loop(0, _window(mode, i, 0, tq, win, lk)[1], step, carry)
        l_all = jnp.where(lane < heads, l_all, 1.0)
        inv = 1.0 / l_all
        for g in range(groups):
            cols = slice(g * gw, (g + 1) * gw)
            inv_g = inv[:, g * per:g * per + 1]
            for u in range(1, per):
                inv_g = jnp.where(own[u], inv[:, g * per + u:g * per + u + 1], inv_g)
            o_ref[0, :, cols] = o_ref[0, :, cols] * inv_g
        lse_ref[0] = jnp.where(lane < heads, m_all + jnp.log(l_all), 0.0)

    in_specs = [_attn_specs(qa, width, qc, classes, tq, False), _attn_specs(ka, width, kc, classes, win, True),
                _attn_specs(va, width, vc, classes, win, True)]
    args = [qa, ka, va]
    if has_kb:
        in_specs.append(pl.BlockSpec((1,) + kb.shape[1:], lambda n_, r, i: (n_, 0, 0, 0)))
        args.append(kb)
    out_shape = [jax.ShapeDtypeStruct((n, lq, classes * width), F32), jax.ShapeDtypeStruct((n, lq, classes * LANES), F32)]
    out_specs = [pl.BlockSpec((1, tq, width), lambda n_, r, i: (n_, i, r)), pl.BlockSpec((1, tq, LANES), lambda n_, r, i: (n_, i, r))]
    return pl.pallas_call(
        body, grid=(n, classes, lq // tq), in_specs=in_specs, out_specs=out_specs, out_shape=out_shape, name=name,
        compiler_params=_params("parallel", "parallel", "arbitrary"),
    )(*args)


def _attn_bwd(q, k, v, do, lse, delta, kb=None, *, classes=1, heads, dh, mode, max_dist=None, name):
    (qa, qc), (ka, kc), (va, vc), (da, dc) = q, k, v, do
    n, lq, lk = qa.shape[0], qa.shape[1], ka.shape[1]
    width = heads * dh
    tq, win = _attn_tiles(mode, lq, lk)
    gw, per, groups = _head_groups(heads, dh)
    scale = dh ** -0.5
    has_kb = kb is not None
    single = mode != "causal"
    nt = (((1,), (1,)), ((), ()))
    tn = (((0,), (0,)), ((), ()))

    def body(*refs):
        q_ref, k_ref, v_ref, do_ref, lse_ref, dl_ref = refs[:6]
        kb_ref = refs[6] if has_kb else None
        n_in = 7 if has_kb else 6
        dq_ref, dk_ref, dv_ref = refs[n_in:n_in + 3]
        dkb_ref, drow_ref = refs[n_in + 3:n_in + 5] if has_kb else (None, None)
        i = pl.program_id(2)

        @pl.when(i == 0)
        def _():
            dk_ref[...] = jnp.zeros(dk_ref.shape, F32)
            dv_ref[...] = jnp.zeros(dv_ref.shape, F32)
            if has_kb:
                dkb_ref[...] = jnp.zeros(dkb_ref.shape, F32)

        lse_all = lse_ref[0]
        dl_all = dl_ref[0]
        lane = lax.broadcasted_iota(jnp.int32, (tq, LANES), 1)
        own = [_own_lanes(tq, gw, dh, u) for u in range(per)]

        def step(j, drow_all):
            start, _ = _window(mode, i, j, tq, win, lk)
            ok = _allowed(mode, i, start, tq, win, max_dist)
            for g in range(groups):
                cols = slice(g * gw, (g + 1) * gw)
                qg = q_ref[0, :, cols]
                dog = do_ref[0, :, cols]
                kw = k_ref[0, pl.ds(start, win), cols]
                vw = v_ref[0, pl.ds(start, win), cols]
                dq_g = dk_w = dv_w = None
                for u in range(per):
                    h = g * per + u
                    qz, doz = _only_head(qg, own, per, u), _only_head(dog, own, per, u)
                    kb_row = kb_ref[0, h, pl.ds(j, 1), :] if has_kb else None
                    p = jnp.exp(_step_scores(qz, kw, kb_row, ok, scale) - lse_all[:, h:h + 1])
                    dp = lax.dot_general(doz, vw, nt, preferred_element_type=F32)
                    ds = p * (dp - dl_all[:, h:h + 1])
                    dsb = ds.astype(BF16)
                    r = jnp.dot(dsb, kw, preferred_element_type=F32)
                    dq_g = r if u == 0 else jnp.where(own[u], r, dq_g)
                    dk_u = lax.dot_general(dsb, qz, tn, preferred_element_type=F32)
                    dv_u = lax.dot_general(p.astype(BF16), doz, tn, preferred_element_type=F32)
                    dk_w = dk_u if u == 0 else dk_w + dk_u
                    dv_w = dv_u if u == 0 else dv_w + dv_u
                    if has_kb:
                        dkb_ref[0, h, pl.ds(j, 1), :] += -jnp.sum(ds, axis=0, keepdims=True)
                        drow_all = drow_all + jnp.where(lane == h, jnp.sum(ds, axis=1, keepdims=True), 0.0)
                dk_ref[0, pl.ds(start, win), cols] += dk_w * scale
                dv_ref[0, pl.ds(start, win), cols] += dv_w
                if single:
                    dq_ref[0, :, cols] = dq_g * scale
                else:
                    dq_ref[0, :, cols] += dq_g * scale
            return drow_all

        drow_all = jnp.zeros((tq, LANES), F32)
        if single:
            drow_all = step(0, drow_all)
        else:
            dq_ref[...] = jnp.zeros(dq_ref.shape, F32)
            drow_all = lax.fori_loop(0, _window(mode, i, 0, tq, win, lk)[1], step, drow_all)
        if has_kb:
            drow_ref[0] = drow_all

    row_spec = functools.partial(_attn_specs, classes=classes, rows_block=tq, whole=False)
    in_specs = [row_spec(qa, width, qc), _attn_specs(ka, width, kc, classes, win, True), _attn_specs(va, width, vc, classes, win, True),
                row_spec(da, width, dc), row_spec(lse, LANES, 0), row_spec(delta, LANES, 0)]
    args = [qa, ka, va, da, lse, delta]
    out_shape = [jax.ShapeDtypeStruct((n, lq, classes * width), F32), jax.ShapeDtypeStruct((n, lk, classes * width), F32),
                 jax.ShapeDtypeStruct((n, lk, classes * width), F32)]
    kv_spec = pl.BlockSpec((1, lk, width), lambda n_, r, i: (n_, 0, r))
    out_specs = [pl.BlockSpec((1, tq, width), lambda n_, r, i: (n_, i, r)), kv_spec, kv_spec]
    if has_kb:
        kb_spec = pl.BlockSpec((1,) + kb.shape[1:], lambda n_, r, i: (n_, 0, 0, 0))
        in_specs.append(kb_spec)
        args.append(kb)
        out_shape += [jax.ShapeDtypeStruct(kb.shape, F32), jax.ShapeDtypeStruct((n, lq, LANES), F32)]
        out_specs += [kb_spec, pl.BlockSpec((1, tq, LANES), lambda n_, r, i: (n_, i, 0))]
    return pl.pallas_call(
        body, grid=(n, classes, lq // tq), in_specs=in_specs, out_specs=out_specs, out_shape=out_shape, name=name,
        compiler_params=_params("parallel", "parallel", "arbitrary"),
    )(*args)


def _head_delta(do, out, heads, *, name, lse=None, sink=None):
    width = out.shape[1]
    gather = _head_expand(heads, width).T

    if sink is None:
        return _rowwise(lambda do, o, e: _dot_exact(do * o, e), [do, out], [gather], [(LANES, F32)], name=name)[0]

    def fn(do, o, lse, e, sink):
        delta = _dot_exact(do * o, e)
        return delta, -jnp.sum(jnp.exp(sink - lse) * delta, axis=0, keepdims=True)
    return _rowwise(fn, [do, out, lse], [gather, sink], [(LANES, F32)], [(1, LANES)], name=name)


def _rope_tables():
    half = HEAD_DIM // 2
    inv = ROPE_THETA ** (-jnp.arange(half, dtype=F32) / half)
    ang = jnp.arange(SEQ, dtype=F32)[:, None] * inv[None, :]
    cos = jnp.tile(jnp.cos(ang), (1, 2 * ATT_W // HEAD_DIM))
    sin = jnp.tile(jnp.concatenate([-jnp.sin(ang), jnp.sin(ang)], axis=1), (1, ATT_W // HEAD_DIM))
    return cos, sin


def _rotate(x, cos, sin):
    width = x.shape[1]
    lane = lax.broadcasted_iota(jnp.int32, x.shape, 1)
    first_half = (lane % HEAD_DIM) < (HEAD_DIM // 2)
    swapped = jnp.where(first_half, pltpu.roll(x, width - HEAD_DIM // 2, axis=1), pltpu.roll(x, HEAD_DIM // 2, axis=1))
    return x * cos[:, :width] + swapped * sin[:, :width]


def _gelu(x):
    k = math.sqrt(2.0 / math.pi)
    t = jnp.tanh(k * (x + 0.044715 * x * x * x))
    return 0.5 * x * (1.0 + t), t


def _gelu_grad(x, t):
    k = math.sqrt(2.0 / math.pi)
    return 0.5 * (1.0 + t) + 0.5 * x * (1.0 - t * t) * k * (1.0 + 3.0 * 0.044715 * x * x)


def _shift_down(x, halo, s):
    ext = jnp.concatenate([halo, x], axis=0)
    return pltpu.roll(ext, s, axis=0)[8:, :]


def _shift_up(x, halo, s):
    rows = x.shape[0]
    ext = jnp.concatenate([x, halo], axis=0)
    return pltpu.roll(ext, rows + 8 - s, axis=0)[:rows, :]


def _lru_gates(u, halo, cw, cb, wa, ba, wi, bi, lam):
    taps = [_shift_down(u, halo, CONV_WIDTH - 1 - k) for k in range(CONV_WIDTH - 1)] + [u]
    uc = cb + sum(cw[k:k + 1, :] * taps[k] for k in range(CONV_WIDTH))
    ucb = uc.astype(BF16)
    r = jax.nn.sigmoid(jnp.dot(ucb, wa, preferred_element_type=F32) + ba)
    gi = jax.nn.sigmoid(jnp.dot(ucb, wi, preferred_element_type=F32) + bi)
    sp = jnp.maximum(-lam, 0.0) + jnp.log(1.0 + jnp.exp(-jnp.abs(lam)))
    log_a = -LRU_C * r * sp
    a = jnp.exp(log_a)
    x2 = 2.0 * log_a
    one_minus_a2 = jnp.where(x2 > -0.01, -x2 * (1.0 + x2 * (0.5 + x2 * (1.0 / 6.0 + x2 / 24.0))), 1.0 - jnp.exp(x2))
    mult = jnp.sqrt(one_minus_a2)
    return taps, uc, ucb, r, gi, sp, a, mult


def _lru_specs(nchunk, reverse):
    def chunk(b, c):
        return b * nchunk + ((nchunk - 1 - c) if reverse else c)

    def halo_before(b, c):
        return jnp.maximum(chunk(b, c) * (LRU_CHUNK // 8) - 1, 0)

    return chunk, halo_before


def _lru_fwd(zug, cw, cb, wa, ba, wi, bi, lam, *, name):
    total = zug.shape[0]
    nchunk = SEQ // LRU_CHUNK
    w = LRU_WIDTH
    chunk, halo_before = _lru_specs(nchunk, False)

    def body(u_ref, uh_ref, g_ref, cw_ref, cb_ref, wa_ref, ba_ref, wi_ref, bi_ref, lam_ref, y_ref, h_ref, a_sc, x_sc, carry_sc):
        c = pl.program_id(1)
        u = u_ref[...]
        halo = jnp.where(c > 0, uh_ref[...], 0.0)
        _, uc, _, _, gi, _, a, mult = _lru_gates(u, halo, cw_ref[...], cb_ref[...], wa_ref[...], ba_ref[...],
                                                 wi_ref[...], bi_ref[...], lam_ref[...])
        a_sc[...] = a
        x_sc[...] = mult * (gi * uc)

        @pl.when(c == 0)
        def _():
            carry_sc[...] = jnp.zeros(carry_sc.shape, F32)

        def tile_step(t, h):
            r0 = pl.multiple_of(t * 8, 8)
            at = a_sc[pl.ds(r0, 8), :]
            xt = x_sc[pl.ds(r0, 8), :]
            rows = []
            for j in range(8):
                h = at[j:j + 1, :] * h + xt[j:j + 1, :]
                rows.append(h)
            h_ref[pl.ds(r0, 8), :] = jnp.concatenate(rows, axis=0)
            return h

        h_last = lax.fori_loop(0, LRU_CHUNK // 8, tile_step, carry_sc[0:1, :])
        carry_sc[0:1, :] = h_last
        gel, _ = _gelu(g_ref[...])
        y_ref[...] = (h_ref[...] * gel).astype(BF16)

    small = lambda arr: pl.BlockSpec(arr.shape, lambda b, c: (0, 0))
    return pl.pallas_call(
        body, grid=(total // SEQ, nchunk),
        in_specs=[pl.BlockSpec((LRU_CHUNK, w), lambda b, c: (chunk(b, c), 0)),
                  pl.BlockSpec((8, w), lambda b, c: (halo_before(b, c), 0)),
                  pl.BlockSpec((LRU_CHUNK, w), lambda b, c: (chunk(b, c), 1)),
                  small(cw), small(cb), small(wa), small(ba), small(wi), small(bi), small(lam)],
        out_specs=[pl.BlockSpec((LRU_CHUNK, w), lambda b, c: (chunk(b, c), 0))] * 2,
        out_shape=[jax.ShapeDtypeStruct((total, w), BF16), jax.ShapeDtypeStruct((total, w), F32)],
        scratch_shapes=[pltpu.VMEM((LRU_CHUNK, w), F32), pltpu.VMEM((LRU_CHUNK, w), F32), pltpu.VMEM((8, w), F32)],
        name=name, compiler_params=_params("arbitrary", "arbitrary"),
    )(zug, zug, zug, cw, cb, wa, ba, wi, bi, lam)


def _lru_bwd(zug, hl, dy, cw, cb, wa, ba, wi, bi, lam, *, name):
    total = zug.shape[0]
    nchunk = SEQ // LRU_CHUNK
    w = LRU_WIDTH
    chunk, halo_before = _lru_specs(nchunk, True)
    tn = (((0,), (0,)), ((), ()))
    nt = (((1,), (1,)), ((), ()))

    def body(u_ref, uh_ref, g_ref, hl_ref, hh_ref, dy_ref, cw_ref, cb_ref, wa_ref, ba_ref, wi_ref, bi_ref, lam_ref,
             dz_ref, dcw_ref, dcb_ref, dwa_ref, dba_ref, dwi_ref, dbi_ref, dlam_ref,
             a_sc, d_sc, g_sc, gcarry_sc, acarry_sc, duc_sc):
        b, c = pl.program_id(0), pl.program_id(1)
        first_chunk = c == nchunk - 1

        @pl.when((b == 0) & (c == 0))
        def _():
            for ref in (dcw_ref, dcb_ref, dwa_ref, dba_ref, dwi_ref, dbi_ref, dlam_ref):
                ref[...] = jnp.zeros(ref.shape, F32)

        @pl.when(c == 0)
        def _():
            gcarry_sc[...] = jnp.zeros(gcarry_sc.shape, F32)
            acarry_sc[...] = jnp.zeros(acarry_sc.shape, F32)
            duc_sc[...] = jnp.zeros(duc_sc.shape, F32)

        u = u_ref[...]
        halo = jnp.where(first_chunk, 0.0, uh_ref[...])
        cw, wa, wi, lam = cw_ref[...], wa_ref[...], wi_ref[...], lam_ref[...]
        taps, uc, ucb, r, gi, sp, a, mult = _lru_gates(u, halo, cw, cb_ref[...], wa, ba_ref[...], wi, bi_ref[...], lam)
        gate = g_ref[...]
        gel, th = _gelu(gate)
        dy = dy_ref[...]
        hl = hl_ref[...]
        a_sc[...] = a
        d_sc[...] = dy * gel
        dgate = dy * hl * _gelu_grad(gate, th)

        def tile_step(t, carry):
            g_next, a_next = carry
            r0 = pl.multiple_of((LRU_CHUNK // 8 - 1 - t) * 8, 8)
            dt = d_sc[pl.ds(r0, 8), :]
            at = a_sc[pl.ds(r0, 8), :]
            rows = [None] * 8
            for j in reversed(range(8)):
                g_next = dt[j:j + 1, :] + a_next * g_next
                a_next = at[j:j + 1, :]
                rows[j] = g_next
            g_sc[pl.ds(r0, 8), :] = jnp.concatenate(rows, axis=0)
            return g_next, a_next

        g_last, a_last = lax.fori_loop(0, LRU_CHUNK // 8, tile_step, (gcarry_sc[0:1, :], acarry_sc[0:1, :]))
        gcarry_sc[0:1, :] = g_last
        acarry_sc[0:1, :] = a_last

        gs = g_sc[...]
        hl_halo = jnp.where(first_chunk, 0.0, hh_ref[...])
        da = gs * _shift_down(hl, hl_halo, 1)
        dmult = gs * gi * uc
        dgi = gs * mult * uc
        duc = gs * mult * gi
        dlog_a = da * a - dmult * a * a / mult
        dr = dlog_a * (-LRU_C * sp)
        dsp = jnp.sum(dlog_a * (-LRU_C * r), axis=0, keepdims=True)
        dlam_ref[...] += -dsp * jax.nn.sigmoid(-lam)
        dpa = dr * r * (1.0 - r)
        dpi = dgi * gi * (1.0 - gi)
        dpab, dpib = dpa.astype(BF16), dpi.astype(BF16)
        dba_ref[...] += jnp.sum(dpa, axis=0, keepdims=True)
        dbi_ref[...] += jnp.sum(dpi, axis=0, keepdims=True)
        dwa_ref[...] += lax.dot_general(ucb, dpab, tn, preferred_element_type=F32)
        dwi_ref[...] += lax.dot_general(ucb, dpib, tn, preferred_element_type=F32)
        duc = duc + lax.dot_general(dpab, wa, nt, preferred_element_type=F32)
        duc = duc + lax.dot_general(dpib, wi, nt, preferred_element_type=F32)
        dcb_ref[...] += jnp.sum(duc, axis=0, keepdims=True)
        dcw_ref[...] += jnp.concatenate([jnp.sum(duc * taps[k], axis=0, keepdims=True) for k in range(CONV_WIDTH)], axis=0)
        after = duc_sc[...]
        du = cw[CONV_WIDTH - 1:CONV_WIDTH, :] * duc
        for k in range(CONV_WIDTH - 1):
            du = du + cw[k:k + 1, :] * _shift_up(duc, after, CONV_WIDTH - 1 - k)
        duc_sc[...] = duc[0:8, :]
        dz_ref[:, 0:w] = du.astype(BF16)
        dz_ref[:, w:2 * w] = dgate.astype(BF16)

    small = lambda arr: pl.BlockSpec(arr.shape, lambda b, c: (0, 0))
    acc = lambda shape: pl.BlockSpec(shape, lambda b, c: (0, 0))
    chunk_spec = lambda cb_: pl.BlockSpec((LRU_CHUNK, w), lambda b, c: (chunk(b, c), cb_))
    halo_spec = pl.BlockSpec((8, w), lambda b, c: (halo_before(b, c), 0))
    acc_shapes = [(CONV_WIDTH, w), (1, w), (w, w), (1, w), (w, w), (1, w), (1, w)]
    return pl.pallas_call(
        body, grid=(total // SEQ, nchunk),
        in_specs=[chunk_spec(0), halo_spec, chunk_spec(1), chunk_spec(0), halo_spec, chunk_spec(0),
                  small(cw), small(cb), small(wa), small(ba), small(wi), small(bi), small(lam)],
        out_specs=[pl.BlockSpec((LRU_CHUNK, 2 * w), lambda b, c: (chunk(b, c), 0))] + [acc(s) for s in acc_shapes],
        out_shape=[jax.ShapeDtypeStruct((total, 2 * w), BF16)] + [jax.ShapeDtypeStruct(s, F32) for s in acc_shapes],
        scratch_shapes=[pltpu.VMEM((LRU_CHUNK, w), F32)] * 3 + [pltpu.VMEM((8, w), F32)] * 3,
        name=name, compiler_params=_params("arbitrary", "arbitrary"),
    )(zug, zug, zug, hl, hl, dy, cw, cb, wa, ba, wi, bi, lam)


def _block_diag(wb):
    eye = jnp.eye(LRU_BLOCKS, dtype=wb.dtype)
    return jnp.einsum("gcd,gh->gchd", wb, eye).reshape(LRU_WIDTH, LRU_WIDTH).astype(BF16)


def _diag_blocks(wd):
    blk = LRU_WIDTH // LRU_BLOCKS
    return jnp.stack([wd[g * blk:(g + 1) * blk, g * blk:(g + 1) * blk] for g in range(LRU_BLOCKS)])


FOX_SCAN = 256


def _fox_bias(fl, bf, *, name):
    nb = fl.shape[0]

    def body(fl_ref, bf_ref, c_ref):
        x = fl_ref[0] + bf_ref[...]
        logf = jnp.minimum(x, 0.0) - jnp.log(1.0 + jnp.exp(-jnp.abs(x)))
        upper = (lax.broadcasted_iota(jnp.int32, (FOX_SCAN, FOX_SCAN), 0)
                 <= lax.broadcasted_iota(jnp.int32, (FOX_SCAN, FOX_SCAN), 1)).astype(BF16)
        carry = jnp.zeros((LRU_BLOCKS, 1), F32)
        for j in range(SEQ // FOX_SCAN):
            blk = logf[:, j * FOX_SCAN:(j + 1) * FOX_SCAN]
            c_ref[0, :, j * FOX_SCAN:(j + 1) * FOX_SCAN] = _dot_exact(blk, upper) + carry
            carry = carry + jnp.sum(blk, axis=1, keepdims=True)

    return pl.pallas_call(
        body, grid=(nb,),
        in_specs=[pl.BlockSpec((1, 8, SEQ), lambda b: (b, 0, 0)), pl.BlockSpec((8, 1), lambda b: (0, 0))],
        out_specs=pl.BlockSpec((1, 8, SEQ), lambda b: (b, 0, 0)),
        out_shape=jax.ShapeDtypeStruct((nb, 8, SEQ), F32), name=name, compiler_params=_params("arbitrary"),
    )(fl, bf)


def _fox_bias_bwd(fl, bf, dc, *, name):
    nb = fl.shape[0]

    def body(fl_ref, bf_ref, dc_ref, dfl_ref, dbf_ref):
        @pl.when(pl.program_id(0) == 0)
        def _():
            dbf_ref[...] = jnp.zeros(dbf_ref.shape, F32)

        x = fl_ref[0] + bf_ref[...]
        dc_all = dc_ref[0]
        lower = (lax.broadcasted_iota(jnp.int32, (FOX_SCAN, FOX_SCAN), 0)
                 >= lax.broadcasted_iota(jnp.int32, (FOX_SCAN, FOX_SCAN), 1)).astype(BF16)
        carry = jnp.zeros((LRU_BLOCKS, 1), F32)
        total = jnp.zeros((LRU_BLOCKS, 1), F32)
        for j in reversed(range(SEQ // FOX_SCAN)):
            cols = slice(j * FOX_SCAN, (j + 1) * FOX_SCAN)
            blk = dc_all[:, cols]
            dlogf = _dot_exact(blk, lower) + carry
            carry = carry + jnp.sum(blk, axis=1, keepdims=True)
            dx = dlogf * jax.nn.sigmoid(-x[:, cols])
            dfl_ref[0, :, cols] = dx
            total = total + jnp.sum(dx, axis=1, keepdims=True)
        dbf_ref[...] += total

    return pl.pallas_call(
        body, grid=(nb,),
        in_specs=[pl.BlockSpec((1, 8, SEQ), lambda b: (b, 0, 0)), pl.BlockSpec((8, 1), lambda b: (0, 0)),
                  pl.BlockSpec((1, 8, SEQ), lambda b: (b, 0, 0))],
        out_specs=[pl.BlockSpec((1, 8, SEQ), lambda b: (b, 0, 0)), pl.BlockSpec((8, 1), lambda b: (0, 0))],
        out_shape=[jax.ShapeDtypeStruct((nb, 8, SEQ), F32), jax.ShapeDtypeStruct((8, 1), F32)],
        name=name, compiler_params=_params("arbitrary"),
    )(fl, bf, dc)


def _seq3(a, nb):
    return a.reshape(nb, a.shape[0] // nb, a.shape[1])


def _flat2(a):
    return a.reshape(a.shape[0] * a.shape[1], a.shape[2])


def _mlp_fwd(h, p, tag):
    hn = _rms_fwd(h, p["norm"], name=f"{tag}_norm")
    up, act = _matmul(hn, p["w_up"], outs=(F32, BF16), epilogue=lambda acc: (acc, jnp.square(jnp.maximum(acc, 0.0))),
                      name=f"{tag}_up")
    out, = _matmul(act, p["w_down"], extras=(h,), epilogue=lambda acc, res: (acc + res,), name=f"{tag}_down")
    return out, (h, hn, up, act)


def _mlp_bwd(dh, dhb, p, saved, tag):
    h, hn, up, act = saved
    dup, = _matmul(dhb, p["w_down"], tb=True, outs=(BF16,), extras=(up,),
                   epilogue=lambda acc, up: (acc * (2.0 * jnp.maximum(up, 0.0)),), name=f"{tag}_bwd_dup")
    dw_down, = _matmul(act, dhb, ta=True, name=f"{tag}_bwd_dwdown")
    dw_up, = _matmul(hn, dup, ta=True, name=f"{tag}_bwd_dwup")
    dhn, = _matmul(dup, p["w_up"], tb=True, name=f"{tag}_bwd_dhn")
    dh2, dh2b, dg = _rms_bwd(h, dhn, dh, p["norm"], name=f"{tag}_bwd_norm")
    return dh2, dh2b, {"norm": dg, "w_up": dw_up, "w_down": dw_down}


def _xa_fwd(h, mem, p, tag, nb):
    hn = _rms_fwd(h, p["norm"], name=f"{tag}_norm")
    mem_n = _rms_fwd(mem, p["mem_norm"], name=f"{tag}_memnorm")
    q, = _matmul(hn, p["w_q"], outs=(BF16,), name=f"{tag}_q")
    kv, = _matmul(mem_n, p["w_kv"], outs=(BF16,), name=f"{tag}_kv")
    q3, kv3 = _seq3(q, nb), _seq3(kv, nb)
    o, lse = _attn_fwd((q3, 0), (kv3, 0), (kv3, 1), heads=XA_HEADS, dh=XA_HEAD_DIM, mode="full",
                       name=f"{tag}_attn")
    o = _flat2(o)
    ob, = _rowwise(lambda o: o, [o], [], [(D_MODEL, BF16)], name=f"{tag}_ocast")
    out, = _matmul(ob, p["w_o"], extras=(h,), epilogue=lambda acc, res: (acc + res,), name=f"{tag}_o")
    return out, (h, hn, mem_n, q3, kv3, o, ob, lse)


def _xa_bwd(dh, dhb, mem, p, saved, tag, nb):
    h, hn, mem_n, q3, kv3, o, ob, lse = saved
    do, dob = _matmul(dhb, p["w_o"], tb=True, outs=(F32, BF16), epilogue=lambda acc: (acc, acc), name=f"{tag}_bwd_do")
    dw_o, = _matmul(ob, dhb, ta=True, name=f"{tag}_bwd_dwo")
    delta = _head_delta(do, o, XA_HEADS, name=f"{tag}_bwd_delta")
    dq, dk, dv = _attn_bwd((q3, 0), (kv3, 0), (kv3, 1), (_seq3(dob, nb), 0), lse, _seq3(delta, nb), heads=XA_HEADS,
                           dh=XA_HEAD_DIM, mode="full", name=f"{tag}_bwd_attn")
    dqb, = _rowwise(lambda x: x, [_flat2(dq)], [], [(D_MODEL, BF16)], name=f"{tag}_bwd_dqcast")
    dkvb, = _rowwise(lambda a, b: jnp.concatenate([a, b], axis=1), [_flat2(dk), _flat2(dv)], [], [(2 * D_MODEL, BF16)],
                     name=f"{tag}_bwd_dkvcast")
    dw_q, = _matmul(hn, dqb, ta=True, name=f"{tag}_bwd_dwq")
    dw_kv, = _matmul(mem_n, dkvb, ta=True, name=f"{tag}_bwd_dwkv")
    dhn, = _matmul(dqb, p["w_q"], tb=True, name=f"{tag}_bwd_dhn")
    dmem_n, = _matmul(dkvb, p["w_kv"], tb=True, name=f"{tag}_bwd_dmemn")
    dg_mem, = _rowwise(lambda x, d, g: _rms_bwd_vals(x, d, g)[1], [mem, dmem_n], [p["mem_norm"]], [], [(1, D_MODEL)],
                       name=f"{tag}_bwd_memnorm")
    dh2, dh2b, dg = _rms_bwd(h, dhn, dh, p["norm"], name=f"{tag}_bwd_norm")
    return dh2, dh2b, {"norm": dg, "mem_norm": dg_mem, "w_q": dw_q, "w_kv": dw_kv, "w_o": dw_o}


AB_MAIN = 2 * LRU_WIDTH + 3 * ATT_W


def _ab_fwd(h, p, nb):
    hn = _rms_fwd(h, p["norm"], name="ab_norm")
    zug, = _matmul(hn, p["w_ug"], name="ab_in_ug")
    qkv, = _matmul(hn, p["w_qkv"], outs=(BF16,), name="ab_in_qkv")
    zf, = _matmul(hn, p["w_f"], name="ab_in_f")
    fl = zf[:, :8].reshape(nb, SEQ, 8).transpose(0, 2, 1)
    cbias = _fox_bias(fl, p["b_f"], name="ab_fox_bias")
    kb = cbias.reshape(nb, 8, SEQ // ATT_CHUNK, ATT_CHUNK)
    y_a, hl = _lru_fwd(zug, p["conv_w"], p["conv_b"], p["w_a"], p["b_a"], p["w_i"], p["b_i"], p["lam"], name="ab_lru")
    qkv3 = _seq3(qkv, nb)
    o, lse = _attn_fwd((qkv3, 0), (qkv3, 1), (qkv3, 2), kb, heads=8, dh=HEAD_DIM, mode="causal", name="ab_fox")
    o = _flat2(o)
    y, = _rowwise(lambda a, b: jnp.concatenate([a, b.astype(BF16)], axis=1), [y_a, o], [], [(D_MODEL, BF16)], name="ab_ycat")
    out, = _matmul(y, p["w_out"], extras=(h,), epilogue=lambda acc, res: (acc + res,), name="ab_out")
    return out, (h, hn, zug, qkv3, fl, kb, hl, o, lse, y)


def _ab_bwd(dh, dhb, p, saved, nb):
    h, hn, zug, qkv3, fl, kb, hl, o, lse, y = saved
    dy, dyb = _matmul(dhb, p["w_out"], tb=True, outs=(F32, BF16), epilogue=lambda acc: (acc, acc), name="ab_bwd_dy")
    dw_out, = _matmul(y, dhb, ta=True, name="ab_bwd_dwout")
    dzug, dcw, dcb, dwa, dba, dwi, dbi, dlam = _lru_bwd(zug, hl, dy, p["conv_w"], p["conv_b"], p["w_a"], p["b_a"],
                                                        p["w_i"], p["b_i"], p["lam"], name="ab_bwd_lru")
    delta = _head_delta((dy, ATT_W, 1), o, 8, name="ab_bwd_delta")
    dq, dk, dv, dkb, drow = _attn_bwd((qkv3, 0), (qkv3, 1), (qkv3, 2), (_seq3(dyb, nb), 1), lse, _seq3(delta, nb), kb,
                                      heads=8, dh=HEAD_DIM, mode="causal", name="ab_bwd_fox")
    dcum = dkb.reshape(nb, 8, SEQ) + drow[:, :, :8].transpose(0, 2, 1)
    dfl, dbf = _fox_bias_bwd(fl, p["b_f"], dcum, name="ab_bwd_fox_bias")
    dzf = jnp.pad(dfl.transpose(0, 2, 1).reshape(nb * SEQ, 8), ((0, 0), (0, LANES - 8)))
    dz, = _rowwise(lambda a, q, k, v, f: jnp.concatenate([a, q.astype(BF16), k.astype(BF16), v.astype(BF16), f.astype(BF16)], axis=1),
                   [dzug, _flat2(dq), _flat2(dk), _flat2(dv), dzf], [], [(AB_MAIN + LANES, BF16)], name="ab_bwd_dzcat")
    dw_in, = _matmul(hn, dz, ta=True, tn=384, name="ab_bwd_dwin")
    dhn, = _matmul(dz, p["w_in_pad"], tb=True, tk=AB_MAIN + LANES, name="ab_bwd_dhn")
    dh2, dh2b, dg = _rms_bwd(h, dhn, dh, p["norm"], name="ab_bwd_norm")
    grads = {"norm": dg, "w_in": dw_in[:, :AB_MAIN + 8], "conv_w": dcw, "conv_b": dcb, "w_a": _diag_blocks(dwa), "b_a": dba,
             "w_i": _diag_blocks(dwi), "b_i": dbi, "lam": dlam, "b_f": dbf.reshape(1, 8), "w_out": dw_out}
    return dh2, dh2b, grads


CD_IN = 3 * ATT_W + ATT_W + 2 * SWA_KW


def _class_view(a, dil):
    return a.reshape(a.shape[0], a.shape[1] // dil, dil * a.shape[2])


def _gqa_share():
    src = jnp.arange(SWA_KW)[:, None]
    dst = jnp.arange(ATT_W)[None, :]
    per_kv = ATT_W // (SWA_KW // HEAD_DIM)
    return ((dst // per_kv == src // HEAD_DIM) & (dst % HEAD_DIM == src % HEAD_DIM)).astype(BF16)


def _cd_fwd(h, p, nb):
    hn = _rms_fwd(h, p["norm"], name="cd_norm")
    z, = _matmul(hn, p["w_in"], tn=384, name="cd_in")
    cos, sin = _rope_tables()
    per_seq = SEQ // ROW_TILE
    table_map = lambda i: (i % per_seq, 0)

    def rope_fn(z, cos, sin, share):
        qc, kc, vc = z[:, 0:ATT_W], z[:, ATT_W:2 * ATT_W], z[:, 2 * ATT_W:3 * ATT_W]
        qd, kd, vd = z[:, 3 * ATT_W:4 * ATT_W], z[:, 4 * ATT_W:4 * ATT_W + SWA_KW], z[:, 4 * ATT_W + SWA_KW:]
        kd8 = jnp.dot(_rotate(kd, cos, sin).astype(BF16), share, preferred_element_type=F32)
        vd8 = jnp.dot(vd.astype(BF16), share, preferred_element_type=F32)
        return (jnp.concatenate([_rotate(qc, cos, sin), _rotate(kc, cos, sin)], axis=1), vc, _rotate(qd, cos, sin), kd8, vd8)
    qk, vc, qd, kd, vd = _rowwise(rope_fn, [z, cos, sin], [_gqa_share()], [(2 * ATT_W, BF16)] + [(ATT_W, BF16)] * 4,
                                  name="cd_rope", row_maps=[None, table_map, table_map])
    qk3, vc3 = _seq3(qk, nb), _seq3(vc, nb)
    branch = []
    for window, dil in DIL_PATTERN:
        o_i, lse_i = _attn_fwd((_class_view(qk3, dil), 0), (_class_view(qk3, dil), 1), (_class_view(vc3, dil), 0), classes=dil,
                               heads=8, dh=HEAD_DIM, mode="band", max_dist=window // dil, name=f"cd_dil{dil}")
        branch.append((o_i.reshape(nb * SEQ, ATT_W), lse_i.reshape(nb * SEQ, LANES)))
    expand = _head_expand(8, ATT_W)

    def combine(o1, o2, o3, l1, l2, l3, e):
        m = jnp.maximum(jnp.maximum(l1, l2), l3)
        lt = m + jnp.log(jnp.exp(l1 - m) + jnp.exp(l2 - m) + jnp.exp(l3 - m))
        o = sum(_dot_exact(jnp.exp(l - lt), e) * o_ for o_, l in ((o1, l1), (o2, l2), (o3, l3)))
        return o, lt
    y_c, lse_c = _rowwise(combine, [b[0] for b in branch] + [b[1] for b in branch], [expand], [(ATT_W, F32), (LANES, F32)],
                          name="cd_dil_combine")
    qd3, kd3, vd3 = _seq3(qd, nb), _seq3(kd, nb), _seq3(vd, nb)
    o_d, lse_band = _attn_fwd((qd3, 0), (kd3, 0), (vd3, 0), heads=8, dh=HEAD_DIM, mode="band", max_dist=SWA_WINDOW - 1,
                              name="cd_swa")

    def sink_combine(o, l, e, sink):
        lt = jnp.maximum(l, sink) + jnp.log(1.0 + jnp.exp(-jnp.abs(l - sink)))
        return o * _dot_exact(jnp.exp(l - lt), e), lt
    y_d, lse_d = _rowwise(sink_combine, [_flat2(o_d), _flat2(lse_band)], [expand, p["sink"]], [(ATT_W, F32), (LANES, F32)],
                          name="cd_swa_sink")
    y, = _rowwise(lambda a, b: jnp.concatenate([a, b], axis=1), [y_c, y_d], [], [(D_MODEL, BF16)], name="cd_ycat")
    out, = _matmul(y, p["w_out"], extras=(h,), epilogue=lambda acc, res: (acc + res,), name="cd_out")
    return out, (h, hn, qk3, vc3, qd3, kd3, vd3, y_c, lse_c, y_d, lse_d, y)


def _cd_bwd(dh, dhb, p, saved, nb):
    h, hn, qk3, vc3, qd3, kd3, vd3, y_c, lse_c, y_d, lse_d, y = saved
    dy, dyb = _matmul(dhb, p["w_out"], tb=True, outs=(F32, BF16), epilogue=lambda acc: (acc, acc), name="cd_bwd_dy")
    dw_out, = _matmul(y, dhb, ta=True, name="cd_bwd_dwout")
    dyb3 = _seq3(dyb, nb)
    delta_c = _head_delta((dy, ATT_W, 0), y_c, 8, name="cd_bwd_delta_dil")
    lse_c3, delta_c3 = _seq3(lse_c, nb), _seq3(delta_c, nb)
    parts = []
    for window, dil in DIL_PATTERN:
        cv = functools.partial(_class_view, dil=dil)
        dq_i, dk_i, dv_i = _attn_bwd((cv(qk3), 0), (cv(qk3), 1), (cv(vc3), 0), (cv(dyb3), 0), cv(lse_c3), cv(delta_c3),
                                     classes=dil, heads=8, dh=HEAD_DIM, mode="band", max_dist=window // dil,
                                     name=f"cd_bwd_dil{dil}")
        parts.append([a.reshape(nb * SEQ, ATT_W) for a in (dq_i, dk_i, dv_i)])
    delta_d, dsink = _head_delta((dy, ATT_W, 1), y_d, 8, lse=lse_d, sink=p["sink"], name="cd_bwd_delta_swa")
    dqd, dkd, dvd = _attn_bwd((qd3, 0), (kd3, 0), (vd3, 0), (dyb3, 1), _seq3(lse_d, nb), _seq3(delta_d, nb), heads=8,
                              dh=HEAD_DIM, mode="band", max_dist=SWA_WINDOW - 1, name="cd_bwd_swa")
    cos, sin = _rope_tables()
    per_seq = SEQ // ROW_TILE
    table_map = lambda i: (i % per_seq, 0)

    def unrope(q1, q2, q3, k1, k2, k3, v1, v2, v3, qd, kd8, vd8, cos, sin, gather):
        back = lambda x: _rotate(x, cos, -sin)
        kd, vd = _dot_exact(kd8, gather), _dot_exact(vd8, gather)
        return jnp.concatenate([back(q1 + q2 + q3), back(k1 + k2 + k3), v1 + v2 + v3, back(qd), back(kd), vd], axis=1)
    ins = [parts[b][t] for t in range(3) for b in range(3)] + [_flat2(dqd), _flat2(dkd), _flat2(dvd), cos, sin]
    dz, = _rowwise(unrope, ins, [_gqa_share().T], [(CD_IN, BF16)], name="cd_bwd_unrope",
                   row_maps=[None] * 12 + [table_map, table_map])
    dw_in, = _matmul(hn, dz, ta=True, tn=384, name="cd_bwd_dwin")
    dhn, = _matmul(dz, p["w_in"], tb=True, tk=CD_IN, name="cd_bwd_dhn")
    dh2, dh2b, dg = _rms_bwd(h, dhn, dh, p["norm"], name="cd_bwd_norm")
    return dh2, dh2b, {"norm": dg, "w_in": dw_in, "sink": dsink[:, :8], "w_out": dw_out}


def _local_step(x, mem, target, w):
    nb = x.shape[0]
    h = x.reshape(nb * SEQ, D_MODEL)
    mem2 = mem.reshape(nb * MEM_LEN, D_MODEL)
    tgt = target.reshape(nb * SEQ, D_MODEL)

    h, s_ab = _ab_fwd(h, w["ab"], nb)
    h, s_xa0 = _xa_fwd(h, mem2, w["xa0"], "xa0", nb)
    h, s_mlp0 = _mlp_fwd(h, w["mlp0"], "mlp0")
    h, s_cd = _cd_fwd(h, w["cd"], nb)
    h, s_xa1 = _xa_fwd(h, mem2, w["xa1"], "xa1", nb)
    h, s_mlp1 = _mlp_fwd(h, w["mlp1"], "mlp1")

    dh, dhb, loss, dg_final = _loss_and_grad(h, tgt, w["final_norm"], name="loss")
    grads = {"final_norm": dg_final}
    dh, dhb, grads["mlp1"] = _mlp_bwd(dh, dhb, w["mlp1"], s_mlp1, "mlp1")
    dh, dhb, grads["xa1"] = _xa_bwd(dh, dhb, mem2, w["xa1"], s_xa1, "xa1", nb)
    dh, dhb, grads["cd"] = _cd_bwd(dh, dhb, w["cd"], s_cd, nb)
    dh, dhb, grads["mlp0"] = _mlp_bwd(dh, dhb, w["mlp0"], s_mlp0, "mlp0")
    dh, dhb, grads["xa0"] = _xa_bwd(dh, dhb, mem2, w["xa0"], s_xa0, "xa0", nb)
    dh, dhb, grads["ab"] = _ab_bwd(dh, dhb, w["ab"], s_ab, nb)
    return loss, dh.reshape(nb, SEQ, D_MODEL), grads


ANY = pl.BlockSpec(memory_space=pl.ANY)


def _place():
    x, y, c = lax.axis_index("x"), lax.axis_index("y"), lax.axis_index("c")
    return x, y, c, [(1 - x, y), (x, 1 - y), (1 - x, 1 - y)]


def _gather_chips(shard):
    def body(src, out, send_sems, recv_sems, local_sem):
        x, y, c, chips = _place()
        mine = pltpu.make_async_copy(src, out.at[2 * x + y], local_sem)
        mine.start()
        sends = [pltpu.make_async_remote_copy(src_ref=src, dst_ref=out.at[2 * x + y], send_sem=send_sems.at[k],
                                              recv_sem=recv_sems.at[k], device_id=(px, py, c), device_id_type=MESH)
                 for k, (px, py) in enumerate(chips)]
        for cp in sends:
            cp.start()
        for k, (px, py) in enumerate(chips):
            pltpu.make_async_remote_copy(src_ref=src, dst_ref=out.at[2 * px + py], send_sem=send_sems.at[k],
                                         recv_sem=recv_sems.at[k], device_id=(px, py, c), device_id_type=MESH).wait_recv()
        for cp in sends:
            cp.wait_send()
        mine.wait()

    return pl.pallas_call(
        body, out_shape=jax.ShapeDtypeStruct((N_CHIPS,) + shard.shape, shard.dtype), in_specs=[ANY], out_specs=ANY,
        scratch_shapes=[pltpu.SemaphoreType.DMA((3,)), pltpu.SemaphoreType.DMA((3,)), pltpu.SemaphoreType.DMA(())],
        name="gather_chips",
    )(shard)


def _swap_cores(block, *, name):
    def body(src, out, send_sem, recv_sem):
        x, y, c, _ = _place()
        cp = pltpu.make_async_remote_copy(src_ref=src, dst_ref=out, send_sem=send_sem, recv_sem=recv_sem,
                                          device_id=(x, y, 1 - c), device_id_type=MESH)
        cp.start()
        cp.wait()

    return pl.pallas_call(
        body, out_shape=jax.ShapeDtypeStruct(block.shape, block.dtype), in_specs=[ANY], out_specs=ANY,
        scratch_shapes=[pltpu.SemaphoreType.DMA(()), pltpu.SemaphoreType.DMA(())], name=name,
    )(block)


def _scatter_chips(parts):
    def body(src, out, send_sems, recv_sems):
        x, y, c, chips = _place()
        sends = [pltpu.make_async_remote_copy(src_ref=src.at[2 * px + py], dst_ref=out.at[k], send_sem=send_sems.at[k],
                                              recv_sem=recv_sems.at[k], device_id=(px, py, c), device_id_type=MESH)
                 for k, (px, py) in enumerate(chips)]
        for cp in sends:
            cp.start()
        for cp in sends:
            cp.wait_recv()
        for cp in sends:
            cp.wait_send()

    return pl.pallas_call(
        body, out_shape=jax.ShapeDtypeStruct((3,) + parts.shape[1:], parts.dtype), in_specs=[ANY], out_specs=ANY,
        scratch_shapes=[pltpu.SemaphoreType.DMA((3,)), pltpu.SemaphoreType.DMA((3,))], name="scatter_chips",
    )(parts)


def _sum_all_devices(vec):
    rows = vec.shape[0]

    def body(x_ref, total_ref, all_ref, send_sems, recv_sems, local_sem):
        x, y, c, chips = _place()
        me, sibling = (x, y, c), (x, y, 1 - c)

        def slot(px, py, pc):
            return all_ref.at[4 * px + 2 * py + pc]

        def copy(k, block, to, src=None):
            return pltpu.make_async_remote_copy(src_ref=slot(*block) if src is None else src, dst_ref=slot(*block),
                                                send_sem=send_sems.at[k], recv_sem=recv_sems.at[k], device_id=to,
                                                device_id_type=MESH)

        mine = pltpu.make_async_copy(x_ref, slot(*me), local_sem)
        mine.start()
        first = [copy(0, me, sibling, src=x_ref)] + [copy(1 + j, me, (*chip, c), src=x_ref) for j, chip in enumerate(chips)]
        for cp in first:
            cp.start()
        passed = [copy(4 + j, (*chip, c), sibling) for j, chip in enumerate(chips)]
        for j, chip in enumerate(chips):
            copy(1 + j, (*chip, c), me).wait_recv()
            passed[j].start()
        copy(0, sibling, me).wait_recv()
        for j, chip in enumerate(chips):
            copy(4 + j, (*chip, 1 - c), me).wait_recv()
        for cp in first + passed:
            cp.wait_send()
        mine.wait()
        acc = all_ref[0]
        for d in range(1, 8):
            acc = acc + all_ref[d]
        total_ref[...] = acc

    vmem = pl.BlockSpec(memory_space=pltpu.VMEM)
    return pl.pallas_call(
        body, out_shape=[jax.ShapeDtypeStruct((rows, LANES), F32), jax.ShapeDtypeStruct((8, rows, LANES), F32)],
        in_specs=[vmem], out_specs=[vmem, vmem],
        scratch_shapes=[pltpu.SemaphoreType.DMA((7,)), pltpu.SemaphoreType.DMA((7,)), pltpu.SemaphoreType.DMA(())],
        name="sum_all_devices", compiler_params=pltpu.CompilerParams(vmem_limit_bytes=VMEM_LIMIT_BYTES),
    )(vec)[0]


def _share_halves(half):
    def body(src, out, send_sem, recv_sem, local_sem):
        x, y, c, _ = _place()
        mine = pltpu.make_async_copy(src, out.at[c], local_sem)
        mine.start()
        cp = pltpu.make_async_remote_copy(src_ref=src, dst_ref=out.at[c], send_sem=send_sem, recv_sem=recv_sem,
                                          device_id=(x, y, 1 - c), device_id_type=MESH)
        cp.start()
        pltpu.make_async_remote_copy(src_ref=src, dst_ref=out.at[1 - c], send_sem=send_sem, recv_sem=recv_sem,
                                     device_id=(x, y, 1 - c), device_id_type=MESH).wait_recv()
        cp.wait_send()
        mine.wait()

    return pl.pallas_call(
        body, out_shape=jax.ShapeDtypeStruct((2,) + half.shape, half.dtype), in_specs=[ANY], out_specs=ANY,
        scratch_shapes=[pltpu.SemaphoreType.DMA(()), pltpu.SemaphoreType.DMA(()), pltpu.SemaphoreType.DMA(())],
        name="share_halves",
    )(half)


PACK_COLS = 1024
BIG = (("ab_w_in", (1, 1024, 642), 2), ("ab_w_out", (1, 256, 1024), 1), ("cd_w_in", (1, 1024, 576), 2),
       ("cd_w_out", (1, 256, 1024), 1), ("xa_w_q", (2, 256, 1024), 1), ("xa_w_kv", (2, 1024, 512), 2),
       ("xa_w_o", (2, 256, 1024), 1), ("mlp_w_up", (2, 1024, 1024), 2), ("mlp_w_down", (2, 1024, 1024), 1))
BIG_ROWS = tuple(math.prod(shape) // PACK_COLS for _, shape, _ in BIG)
PACK_ROWS = -(-sum(BIG_ROWS) // 32) * 32
REDUCE_TILE = 208
assert (PACK_ROWS // 2) % REDUCE_TILE == 0

SMALL = (("ab_norm", (1, 1024)), ("ab_conv_w", (1, 4, 512)), ("ab_conv_b", (1, 512)), ("lru_w_a", (1, 8, 64, 64)),
         ("lru_b_a", (1, 512)), ("lru_w_i", (1, 8, 64, 64)), ("lru_b_i", (1, 512)), ("lru_lambda", (1, 512)),
         ("fox_b_f", (1, 8)), ("cd_norm", (1, 1024)), ("cd_sink", (1, 8)), ("xa_norm", (2, 1024)),
         ("xa_mem_norm", (2, 1024)), ("mlp_norm", (2, 1024)), ("final_norm", (1024,)), ("loss", (1,)))
SPLIT_SMALL = ("ab_conv_w", "cd_norm")


def _pack_rows(blocks, dtype):
    lead = blocks[0].shape[:-len(BIG[0][1])]
    flat = [b.astype(dtype).reshape(lead + (r, PACK_COLS)) for b, r in zip(blocks, BIG_ROWS)]
    pad = PACK_ROWS - sum(BIG_ROWS)
    return jnp.concatenate(flat + [jnp.zeros(lead + (pad, PACK_COLS), dtype)], axis=len(lead))


def _unpack_rows(packed):
    lead = packed.shape[:-2]
    out, r0 = [], 0
    for (_, shape, _), r in zip(BIG, BIG_ROWS):
        out.append(packed[..., r0:r0 + r, :].reshape(lead + shape))
        r0 += r
    return out


def _pack_small(vals):
    flat = jnp.concatenate([vals[name].astype(F32).reshape(-1) for name, _ in SMALL])
    size = -(-flat.shape[0] // (8 * LANES)) * (8 * LANES)
    return jnp.pad(flat, (0, size - flat.shape[0])).reshape(-1, LANES)


def _unpack_small(packed):
    flat, out, at = packed.reshape(-1), {}, 0
    for name, shape in SMALL:
        out[name] = flat[at:at + math.prod(shape)].reshape(shape)
        at += math.prod(shape)
    return out


def _chip_part(full, chip):
    width = full.shape[-1] // N_CHIPS
    return lax.dynamic_slice_in_dim(full, chip * width, width, axis=full.ndim - 1)


def _chip_embed(part, chip):
    full = jnp.zeros(part.shape[:-1] + (part.shape[-1] * N_CHIPS,), part.dtype)
    return lax.dynamic_update_slice_in_dim(full, part, chip * part.shape[-1], axis=part.ndim - 1)


def _reduce_to_owner(grads_by_chip, core):
    half = PACK_ROWS // 2
    keep = lax.dynamic_slice_in_dim(grads_by_chip, core * half, half, axis=1)
    give = lax.dynamic_slice_in_dim(grads_by_chip, (1 - core) * half, half, axis=1)
    got = _swap_cores(give, name="swap_cores")
    flat = lambda a: a.reshape(-1, PACK_COLS)
    pair, = _rowwise(lambda a, b: a + b, [flat(keep), flat(got)], [], [(PACK_COLS, F32)], name="sum_cores", tile=REDUCE_TILE)
    pair = pair.reshape(N_CHIPS, half, PACK_COLS)
    landed = _scatter_chips(pair)
    chip = 2 * lax.axis_index("x") + lax.axis_index("y")
    own = lax.dynamic_index_in_dim(pair, chip, axis=0, keepdims=False)
    done, = _rowwise(lambda a, b, c, d: a + b + c + d, [own, landed[0], landed[1], landed[2]], [], [(PACK_COLS, F32)],
                     name="sum_chips", tile=REDUCE_TILE)
    return _share_halves(done).reshape(PACK_ROWS, PACK_COLS)


def kernel(x, mem, ab_norm, ab_w_in, ab_conv_w, ab_conv_b, lru_w_a, lru_b_a, lru_w_i, lru_b_i, lru_lambda, fox_b_f, ab_w_out, cd_norm, cd_w_in, cd_sink, cd_w_out, xa_norm, xa_mem_norm, xa_w_q, xa_w_kv, xa_w_o, mlp_norm, mlp_w_up, mlp_w_down, final_norm, loss_target, m_ab_norm, m_ab_w_in, m_ab_conv_w, m_ab_conv_b, m_lru_w_a, m_lru_b_a, m_lru_w_i, m_lru_b_i, m_lru_lambda, m_fox_b_f, m_ab_w_out, m_cd_norm, m_cd_w_in, m_cd_sink, m_cd_w_out, m_xa_norm, m_xa_mem_norm, m_xa_w_q, m_xa_w_kv, m_xa_w_o, m_mlp_norm, m_mlp_w_up, m_mlp_w_down, m_final_norm, v_ab_norm, v_ab_w_in, v_ab_conv_w, v_ab_conv_b, v_lru_w_a, v_lru_b_a, v_lru_w_i, v_lru_b_i, v_lru_lambda, v_fox_b_f, v_ab_w_out, v_cd_norm, v_cd_w_in, v_cd_sink, v_cd_w_out, v_xa_norm, v_xa_mem_norm, v_xa_w_q, v_xa_w_kv, v_xa_w_o, v_mlp_norm, v_mlp_w_up, v_mlp_w_down, v_final_norm):
    given = dict(locals())
    weight_names = [name for name, _ in SMALL if name != "loss"] + [name for name, _, _ in BIG]
    w = {name: given[name] for name in weight_names}
    chip = 2 * lax.axis_index("x") + lax.axis_index("y")
    core = lax.axis_index("c")

    gathered = _gather_chips(_pack_rows([w[name] for name, _, _ in BIG], BF16))
    full = {}
    for (name, _, axis), blocks in zip(BIG, _unpack_rows(gathered)):
        full[name] = jnp.concatenate([blocks[s] for s in range(N_CHIPS)], axis=axis)
    owner = (core == 0).astype(F32)
    quarters = {name: _chip_embed(w[name], chip) * owner for name in SPLIT_SMALL}
    zeros = {name: jnp.zeros(shape, F32) for name, shape in SMALL}
    small_full = _unpack_small(_sum_all_devices(_pack_small({**zeros, **quarters})))
    conv_w_full, cd_norm_full = small_full["ab_conv_w"], small_full["cd_norm"]

    w_in = full["ab_w_in"][0]
    w_f = jnp.pad(w_in[:, AB_MAIN:], ((0, 0), (0, LANES - 8)))
    params = {
        "ab": dict(norm=w["ab_norm"], w_ug=w_in[:, :2 * LRU_WIDTH], w_qkv=w_in[:, 2 * LRU_WIDTH:AB_MAIN], w_f=w_f,
                   w_in_pad=jnp.concatenate([w_in[:, :AB_MAIN], w_f], axis=1), conv_w=conv_w_full[0], conv_b=w["ab_conv_b"],
                   w_a=_block_diag(w["lru_w_a"][0]), b_a=w["lru_b_a"], w_i=_block_diag(w["lru_w_i"][0]), b_i=w["lru_b_i"],
                   lam=w["lru_lambda"], b_f=w["fox_b_f"].reshape(8, 1), w_out=full["ab_w_out"][0]),
        "cd": dict(norm=cd_norm_full, w_in=full["cd_w_in"][0], sink=jnp.pad(w["cd_sink"], ((0, 0), (0, LANES - 8))),
                   w_out=full["cd_w_out"][0]),
        "final_norm": w["final_norm"].reshape(1, D_MODEL),
    }
    for layer in range(2):
        params[f"xa{layer}"] = dict(norm=w["xa_norm"][layer:layer + 1], mem_norm=w["xa_mem_norm"][layer:layer + 1],
                                    w_q=full["xa_w_q"][layer], w_kv=full["xa_w_kv"][layer], w_o=full["xa_w_o"][layer])
        params[f"mlp{layer}"] = dict(norm=w["mlp_norm"][layer:layer + 1], w_up=full["mlp_w_up"][layer],
                                     w_down=full["mlp_w_down"][layer])

    loss_part, grad_x, g = _local_step(x, mem, loss_target, params)

    pair = lambda key, leaf: jnp.stack([g[f"{key}0"][leaf], g[f"{key}1"][leaf]])
    big_full = {"ab_w_in": g["ab"]["w_in"][None], "ab_w_out": g["ab"]["w_out"][None], "cd_w_in": g["cd"]["w_in"][None],
                "cd_w_out": g["cd"]["w_out"][None], "xa_w_q": pair("xa", "w_q"), "xa_w_kv": pair("xa", "w_kv"),
                "xa_w_o": pair("xa", "w_o"), "mlp_w_up": pair("mlp", "w_up"), "mlp_w_down": pair("mlp", "w_down")}
    by_chip = _pack_rows([jnp.stack(jnp.split(big_full[name], N_CHIPS, axis=axis)) for name, _, axis in BIG], F32)
    grads = dict(zip([name for name, _, _ in BIG], _unpack_rows(_reduce_to_owner(by_chip, core))))

    small_local = {"ab_norm": g["ab"]["norm"], "ab_conv_w": g["ab"]["conv_w"], "ab_conv_b": g["ab"]["conv_b"],
                   "lru_w_a": g["ab"]["w_a"], "lru_b_a": g["ab"]["b_a"], "lru_w_i": g["ab"]["w_i"], "lru_b_i": g["ab"]["b_i"],
                   "lru_lambda": g["ab"]["lam"], "fox_b_f": g["ab"]["b_f"], "cd_norm": g["cd"]["norm"], "cd_sink": g["cd"]["sink"],
                   "xa_norm": pair("xa", "norm"), "xa_mem_norm": pair("xa", "mem_norm"), "mlp_norm": pair("mlp", "norm"),
                   "final_norm": g["final_norm"], "loss": loss_part[0, :1]}
    small_sum_packed = _sum_all_devices(_pack_small(small_local))
    small_sum = _unpack_small(small_sum_packed)
    loss = small_sum["loss"][0]

    delta, new_m, new_v = {}, {}, {}
    for name, shape, _ in BIG:
        flat = lambda a: a.reshape(-1, shape[-1])
        d, m2, v2 = _adamw(flat(w[name]), flat(grads[name]), flat(given["m_" + name]), flat(given["v_" + name]), name=f"adamw_{name}")
        delta[name], new_m[name], new_v[name] = d.reshape(shape), m2.reshape(shape), v2.reshape(shape)

    def small_state(prefix):
        vals = {name: (given[prefix + name] if name != "loss" else jnp.zeros((1,), F32)) for name, _ in SMALL}
        for name in SPLIT_SMALL:
            vals[name] = _chip_embed(vals[name], chip)
        return _pack_small(vals)
    packed = _adamw(small_state(""), small_sum_packed, small_state("m_"), small_state("v_"), name="adamw_small")
    for store, vals in zip((delta, new_m, new_v), packed):
        for name, val in _unpack_small(vals).items():
            store[name] = _chip_part(val, chip) if name in SPLIT_SMALL else val
    for name in SPLIT_SMALL:
        small_sum[name] = _chip_part(small_sum[name], chip)
    grads.update({name: small_sum[name] for name, _ in SMALL})

    order = ["ab_norm", "ab_w_in", "ab_conv_w", "ab_conv_b", "lru_w_a", "lru_b_a", "lru_w_i", "lru_b_i", "lru_lambda", "fox_b_f",
             "ab_w_out", "cd_norm", "cd_w_in", "cd_sink", "cd_w_out", "xa_norm", "xa_mem_norm", "xa_w_q", "xa_w_kv", "xa_w_o",
             "mlp_norm", "mlp_w_up", "mlp_w_down", "final_norm"]
    return (loss, grad_x, *[grads[n] for n in order], *[delta[n] for n in order], *[new_m[n] for n in order],
            *[new_v[n] for n in order])
```

```python
import functools
import math

import jax
import jax.numpy as jnp
from jax import lax
from jax.experimental import pallas as pl
from jax.experimental.pallas import tpu as pltpu

F32 = jnp.float32
BF16 = jnp.bfloat16
MESH = pl.DeviceIdType.MESH

D_MODEL = 1024
SEQ = 2048
HEAD_DIM = 64
LRU_WIDTH = 512
LRU_BLOCKS = 8
LRU_C = 8.0
CONV_WIDTH = 4
ATT_W = 512
SWA_KW = 128
SWA_WINDOW = 128
DIL_PATTERN = ((128, 1), (512, 4), (2048, 16))
MEM_LEN = 256
XA_HEADS = 4
XA_HEAD_DIM = 256
D_FF = 4096
ROPE_THETA = 10000.0
EPS = 1e-6
N_CHIPS = 4
LANES = 128

ADAM_LR, ADAM_B1, ADAM_B2, ADAM_EPS, ADAM_WD, ADAM_STEP = 0.001, 0.9, 0.999, 1e-08, 0.01, 10

VMEM_LIMIT_BYTES = 56 * 1024 * 1024
MASKED = -1e30
ROW_TILE = 512
LRU_CHUNK = 512
ATT_BLOCK = 128
ATT_CHUNK = 512
CAUSAL_ROWS = 256


def _params(*sem):
    return pltpu.CompilerParams(dimension_semantics=sem, vmem_limit_bytes=VMEM_LIMIT_BYTES)


def _rowwise(fn, rows, consts=(), out_rows=(), out_accs=(), *, name, tile=ROW_TILE, row_maps=None):
    rows = [r if isinstance(r, tuple) else (r, r.shape[1], 0) for r in rows]
    consts = list(consts)
    nr, nc, no = len(rows), len(consts), len(out_rows)
    total = rows[0][0].shape[0]
    tile = min(tile, total)
    assert total % tile == 0
    n = total // tile

    def body(*refs):
        outs = fn(*[r[...] for r in refs[:nr + nc]])
        outs = tuple(outs) if isinstance(outs, (tuple, list)) else (outs,)
        for ref, val in zip(refs[nr + nc:nr + nc + no], outs[:no]):
            ref[...] = val.astype(ref.dtype)
        for ref, val in zip(refs[nr + nc + no:], outs[no:]):
            @pl.when(pl.program_id(0) == 0)
            def _(ref=ref):
                ref[...] = jnp.zeros(ref.shape, ref.dtype)
            ref[...] += val

    in_specs = []
    for idx, (arr, width, cb) in enumerate(rows):
        if row_maps is not None and row_maps[idx] is not None:
            in_specs.append(pl.BlockSpec((tile, width), row_maps[idx]))
        else:
            in_specs.append(pl.BlockSpec((tile, width), functools.partial(lambda i, cb: (i, cb), cb=cb)))
    in_specs += [pl.BlockSpec(c.shape, lambda i: (0, 0)) for c in consts]
    out_shape = [jax.ShapeDtypeStruct((total, w), dt) for w, dt in out_rows]
    out_shape += [jax.ShapeDtypeStruct(s, F32) for s in out_accs]
    out_specs = [pl.BlockSpec((tile, w), lambda i: (i, 0)) for w, _ in out_rows]
    out_specs += [pl.BlockSpec(s, lambda i: (0, 0)) for s in out_accs]
    return pl.pallas_call(
        body, grid=(n,), in_specs=in_specs, out_specs=out_specs, out_shape=out_shape, name=name,
        compiler_params=_params("arbitrary"),
    )(*[r[0] for r in rows], *consts)


def _matmul(a, b, *, ta=False, tb=False, outs=(F32,), epilogue=None, extras=(), tm=1024, tn=512, tk=1024, name):
    m, k = (a.shape[1], a.shape[0]) if ta else a.shape
    n = b.shape[0] if tb else b.shape[1]
    assert (b.shape[1] if tb else b.shape[0]) == k
    tm, tn, tk = min(tm, m), min(tn, n), min(tk, k)
    assert m % tm == 0 and n % tn == 0 and k % tk == 0, (name, m, n, k)
    nk = k // tk
    ne, no = len(extras), len(outs)
    dims = (((0 if ta else 1,), (1 if tb else 0,)), ((), ()))

    def body(*refs):
        a_ref, b_ref = refs[:2]
        e_refs, o_refs = refs[2:2 + ne], refs[2 + ne:2 + ne + no]

        def finish(acc):
            vals = (acc,) if epilogue is None else epilogue(acc, *[e[...] for e in e_refs])
            for ref, val in zip(o_refs, vals):
                ref[...] = val.astype(ref.dtype)

        prod = lax.dot_general(a_ref[...], b_ref[...], dims, preferred_element_type=F32)
        if nk == 1:
            finish(prod)
        else:
            acc_ref = refs[-1]
            step = pl.program_id(2)

            @pl.when(step == 0)
            def _():
                acc_ref[...] = prod

            @pl.when(step > 0)
            def _():
                acc_ref[...] += prod

            @pl.when(step == nk - 1)
            def _():
                finish(acc_ref[...])

    a_spec = pl.BlockSpec((tk, tm), lambda i, j, s: (s, i)) if ta else pl.BlockSpec((tm, tk), lambda i, j, s: (i, s))
    b_spec = pl.BlockSpec((tn, tk), lambda i, j, s: (j, s)) if tb else pl.BlockSpec((tk, tn), lambda i, j, s: (s, j))
    tile_spec = pl.BlockSpec((tm, tn), lambda i, j, s: (i, j))
    return pl.pallas_call(
        body, grid=(m // tm, n // tn, nk),
        in_specs=[a_spec, b_spec] + [tile_spec] * ne,
        out_specs=[tile_spec] * no,
        out_shape=[jax.ShapeDtypeStruct((m, n), dt) for dt in outs],
        scratch_shapes=[pltpu.VMEM((tm, tn), F32)] if nk > 1 else [],
        name=name, compiler_params=_params("parallel", "parallel", "arbitrary"),
    )(a, b, *extras)


def _split3(x):
    x1 = x.astype(BF16)
    r1 = x - x1.astype(F32)
    x2 = r1.astype(BF16)
    x3 = (r1 - x2.astype(F32)).astype(BF16)
    return x1, x2, x3


def _dot_exact(x, e):
    return sum(jnp.dot(p, e, preferred_element_type=F32) for p in _split3(x))


def _head_expand(heads, width):
    dh = width // heads
    rows = jnp.arange(LANES)[:, None]
    cols = jnp.arange(width)[None, :]
    return (cols // dh == rows).astype(BF16)


def _rstd(x):
    return lax.rsqrt(jnp.mean(x * x, axis=-1, keepdims=True) + EPS)


def _rms_fwd(x, gain, *, name):
    return _rowwise(lambda x, g: x * _rstd(x) * g, [x], [gain], [(x.shape[1], BF16)], name=name)[0]


def _rms_bwd_vals(x, dhn, gain):
    r = _rstd(x)
    xh = x * r
    dxh = dhn * gain
    dx = r * (dxh - xh * jnp.mean(dxh * xh, axis=-1, keepdims=True))
    return dx, jnp.sum(dhn * xh, axis=0, keepdims=True)


def _rms_bwd(x, dhn, dres, gain, *, name):
    def fn(x, dhn, dres, g):
        dx, dg = _rms_bwd_vals(x, dhn, g)
        dh = dres + dx
        return dh, dh, dg
    d = x.shape[1]
    return _rowwise(fn, [x, dhn, dres], [gain], [(d, F32), (d, BF16)], [(1, d)], name=name)


def _loss_and_grad(h, target, gain, *, name):
    def fn(h, tgt, g):
        r = _rstd(h)
        err = h * r * g - tgt
        loss = 0.5 * jnp.sum(jnp.mean(err * err, axis=-1, keepdims=True), axis=0, keepdims=True)
        dx, dg = _rms_bwd_vals(h, err * (1.0 / D_MODEL), g)
        return dx, dx, jnp.broadcast_to(loss, (1, LANES)), dg
    d = h.shape[1]
    return _rowwise(fn, [h, target], [gain], [(d, F32), (d, BF16)], [(1, LANES), (1, d)], name=name)


def _adamw(w, g, m, v, *, name):
    rows, cols = w.shape
    tile = rows
    for cand in (256, 128, 64, 32, 16, 8):
        if rows % cand == 0 and rows > cand:
            tile = cand
            break
    bc1 = 1.0 - ADAM_B1 ** ADAM_STEP
    bc2 = 1.0 - ADAM_B2 ** ADAM_STEP

    def fn(w, g, m, v):
        m2 = ADAM_B1 * m + (1.0 - ADAM_B1) * g
        v2 = ADAM_B2 * v + (1.0 - ADAM_B2) * (g * g)
        delta = -ADAM_LR * ((m2 / bc1) / (jnp.sqrt(v2 / bc2) + ADAM_EPS) + ADAM_WD * w)
        return delta, m2, v2
    return _rowwise(fn, [w, g, m, v], [], [(cols, F32)] * 3, name=name, tile=tile)


def _attn_specs(arr, width, cb, classes, rows_block, whole):
    ncols = arr.shape[2] // (classes * width)
    if whole:
        return pl.BlockSpec((1, arr.shape[1], width), lambda n, r, i: (n, 0, r * ncols + cb))
    return pl.BlockSpec((1, rows_block, width), lambda n, r, i: (n, i, r * ncols + cb))


def _attn_tiles(mode, lq, lk):
    if mode == "full":
        return min(lq, 256), lk
    if mode == "band":
        return ATT_BLOCK, min(2 * ATT_BLOCK, lk)
    return CAUSAL_ROWS, ATT_CHUNK


def _window(mode, i, j, tq, win, lk):
    if mode == "causal":
        return pl.multiple_of(j * win, win), (i * tq) // win + 1
    if mode == "band":
        return pl.multiple_of(jnp.clip(i * tq + tq - win, 0, lk - win), ATT_BLOCK), 1
    return 0, 1


def _allowed(mode, i, start, tq, win, max_dist):
    if mode == "full":
        return None
    dist = (i * tq + lax.broadcasted_iota(jnp.int32, (tq, win), 0)) - (start + lax.broadcasted_iota(jnp.int32, (tq, win), 1))
    ok = dist >= 0
    if max_dist is not None:
        ok = ok & (dist <= max_dist)
    return ok


def _head_groups(heads, dh):
    gw = max(LANES, dh)
    return gw, gw // dh, heads // (gw // dh)


def _own_lanes(rows, gw, dh, u):
    return lax.broadcasted_iota(jnp.int32, (rows, gw), 1) // dh == u


def _only_head(x, own, per, u):
    return x if per == 1 else jnp.where(own[u], x, jnp.zeros_like(x))


def _step_scores(qz, kw, kb_row, ok, scale):
    s = lax.dot_general(qz, kw, (((1,), (1,)), ((), ())), preferred_element_type=F32) * scale
    if kb_row is not None:
        s = s - kb_row
    if ok is not None:
        s = jnp.where(ok, s, MASKED)
    return s


def _attn_fwd(q, k, v, kb=None, *, classes=1, heads, dh, mode, max_dist=None, name):
    (qa, qc), (ka, kc), (va, vc) = q, k, v
    n, lq, lk = qa.shape[0], qa.shape[1], ka.shape[1]
    width = heads * dh
    tq, win = _attn_tiles(mode, lq, lk)
    gw, per, groups = _head_groups(heads, dh)
    scale = dh ** -0.5
    has_kb = kb is not None
    single = mode != "causal"

    def body(*refs):
        q_ref, k_ref, v_ref = refs[:3]
        kb_ref = refs[3] if has_kb else None
        o_ref, lse_ref = refs[-2:]
        i = pl.program_id(2)
        lane = lax.broadcasted_iota(jnp.int32, (tq, LANES), 1)
        own = [_own_lanes(tq, gw, dh, u) for u in range(per)]

        def step(j, carry):
            m_all, l_all = carry
            start, _ = _window(mode, i, j, tq, win, lk)
            ok = _allowed(mode, i, start, tq, win, max_dist)
            for g in range(groups):
                cols = slice(g * gw, (g + 1) * gw)
                qg = q_ref[0, :, cols]
                kw = k_ref[0, pl.ds(start, win), cols]
                vw = v_ref[0, pl.ds(start, win), cols]
                alpha_g = new_g = None
                for u in range(per):
                    h = g * per + u
                    kb_row = kb_ref[0, h, pl.ds(j, 1), :] if has_kb else None
                    s = _step_scores(_only_head(qg, own, per, u), kw, kb_row, ok, scale)
                    m_old, l_old = m_all[:, h:h + 1], l_all[:, h:h + 1]
                    m_new = jnp.maximum(m_old, jnp.max(s, axis=1, keepdims=True))
                    alpha = jnp.exp(m_old - m_new)
                    p = jnp.exp(s - m_new)
                    l_new = alpha * l_old + jnp.sum(p, axis=1, keepdims=True)
                    r = jnp.dot(p.astype(BF16), vw, preferred_element_type=F32)
                    alpha_g = alpha if u == 0 else jnp.where(own[u], alpha, alpha_g)
                    new_g = r if u == 0 else jnp.where(own[u], r, new_g)
                    m_all = jnp.where(lane == h, m_new, m_all)
                    l_all = jnp.where(lane == h, l_new, l_all)
                o_ref[0, :, cols] = new_g if single else alpha_g * o_ref[0, :, cols] + new_g
            return m_all, l_all

        carry = (jnp.full((tq, LANES), MASKED, F32), jnp.zeros((tq, LANES), F32))
        if single:
            m_all, l_all = step(0, carry)
        else:
            o_ref[...] = jnp.zeros(o_ref.shape, F32)
            m_all, l_all = lax.fori_loop(0, _window(mode, i, 0, tq, win, lk)[1], step, carry)
        l_all = jnp.where(lane < heads, l_all, 1.0)
        inv = 1.0 / l_all
        for g in range(groups):
            cols = slice(g * gw, (g + 1) * gw)
            inv_g = inv[:, g * per:g * per + 1]
            for u in range(1, per):
                inv_g = jnp.where(own[u], inv[:, g * per + u:g * per + u + 1], inv_g)
            o_ref[0, :, cols] = o_ref[0, :, cols] * inv_g
        lse_ref[0] = jnp.where(lane < heads, m_all + jnp.log(l_all), 0.0)

    in_specs = [_attn_specs(qa, width, qc, classes, tq, False), _attn_specs(ka, width, kc, classes, win, True),
                _attn_specs(va, width, vc, classes, win, True)]
    args = [qa, ka, va]
    if has_kb:
        in_specs.append(pl.BlockSpec((1,) + kb.shape[1:], lambda n_, r, i: (n_, 0, 0, 0)))
        args.append(kb)
    out_shape = [jax.ShapeDtypeStruct((n, lq, classes * width), F32), jax.ShapeDtypeStruct((n, lq, classes * LANES), F32)]
    out_specs = [pl.BlockSpec((1, tq, width), lambda n_, r, i: (n_, i, r)), pl.BlockSpec((1, tq, LANES), lambda n_, r, i: (n_, i, r))]
    return pl.pallas_call(
        body, grid=(n, classes, lq // tq), in_specs=in_specs, out_specs=out_specs, out_shape=out_shape, name=name,
        compiler_params=_params("parallel", "parallel", "arbitrary"),
    )(*args)


def _attn_bwd(q, k, v, do, lse, delta, kb=None, *, classes=1, heads, dh, mode, max_dist=None, name):
    (qa, qc), (ka, kc), (va, vc), (da, dc) = q, k, v, do
    n, lq, lk = qa.shape[0], qa.shape[1], ka.shape[1]
    width = heads * dh
    tq, win = _attn_tiles(mode, lq, lk)
    gw, per, groups = _head_groups(heads, dh)
    scale = dh ** -0.5
    has_kb = kb is not None
    single = mode != "causal"
    nt = (((1,), (1,)), ((), ()))
    tn = (((0,), (0,)), ((), ()))

    def body(*refs):
        q_ref, k_ref, v_ref, do_ref, lse_ref, dl_ref = refs[:6]
        kb_ref = refs[6] if has_kb else None
        n_in = 7 if has_kb else 6
        dq_ref, dk_ref, dv_ref = refs[n_in:n_in + 3]
        dkb_ref, drow_ref = refs[n_in + 3:n_in + 5] if has_kb else (None, None)
        i = pl.program_id(2)

        @pl.when(i == 0)
        def _():
            dk_ref[...] = jnp.zeros(dk_ref.shape, F32)
            dv_ref[...] = jnp.zeros(dv_ref.shape, F32)
            if has_kb:
                dkb_ref[...] = jnp.zeros(dkb_ref.shape, F32)

        lse_all = lse_ref[0]
        dl_all = dl_ref[0]
        lane = lax.broadcasted_iota(jnp.int32, (tq, LANES), 1)
        own = [_own_lanes(tq, gw, dh, u) for u in range(per)]

        def step(j, drow_all):
            start, _ = _window(mode, i, j, tq, win, lk)
            ok = _allowed(mode, i, start, tq, win, max_dist)
            for g in range(groups):
                cols = slice(g * gw, (g + 1) * gw)
                qg = q_ref[0, :, cols]
                dog = do_ref[0, :, cols]
                kw = k_ref[0, pl.ds(start, win), cols]
                vw = v_ref[0, pl.ds(start, win), cols]
                dq_g = dk_w = dv_w = None
                for u in range(per):
                    h = g * per + u
                    qz, doz = _only_head(qg, own, per, u), _only_head(dog, own, per, u)
                    kb_row = kb_ref[0, h, pl.ds(j, 1), :] if has_kb else None
                    p = jnp.exp(_step_scores(qz, kw, kb_row, ok, scale) - lse_all[:, h:h + 1])
                    dp = lax.dot_general(doz, vw, nt, preferred_element_type=F32)
                    ds = p * (dp - dl_all[:, h:h + 1])
                    dsb = ds.astype(BF16)
                    r = jnp.dot(dsb, kw, preferred_element_type=F32)
                    dq_g = r if u == 0 else jnp.where(own[u], r, dq_g)
                    dk_u = lax.dot_general(dsb, qz, tn, preferred_element_type=F32)
                    dv_u = lax.dot_general(p.astype(BF16), doz, tn, preferred_element_type=F32)
                    dk_w = dk_u if u == 0 else dk_w + dk_u
                    dv_w = dv_u if u == 0 else dv_w + dv_u
                    if has_kb:
                        dkb_ref[0, h, pl.ds(j, 1), :] += -jnp.sum(ds, axis=0, keepdims=True)
                        drow_all = drow_all + jnp.where(lane == h, jnp.sum(ds, axis=1, keepdims=True), 0.0)
                dk_ref[0, pl.ds(start, win), cols] += dk_w * scale
                dv_ref[0, pl.ds(start, win), cols] += dv_w
                if single:
                    dq_ref[0, :, cols] = dq_g * scale
                else:
                    dq_ref[0, :, cols] += dq_g * scale
            return drow_all

        drow_all = jnp.zeros((tq, LANES), F32)
        if single:
            drow_all = step(0, drow_all)
        else:
            dq_ref[...] = jnp.zeros(dq_ref.shape, F32)
            drow_all = lax.fori_loop(0, _window(mode, i, 0, tq, win, lk)[1], step, drow_all)
        if has_kb:
            drow_ref[0] = drow_all

    row_spec = functools.partial(_attn_specs, classes=classes, rows_block=tq, whole=False)
    in_specs = [row_spec(qa, width, qc), _attn_specs(ka, width, kc, classes, win, True), _attn_specs(va, width, vc, classes, win, True),
                row_spec(da, width, dc), row_spec(lse, LANES, 0), row_spec(delta, LANES, 0)]
    args = [qa, ka, va, da, lse, delta]
    out_shape = [jax.ShapeDtypeStruct((n, lq, classes * width), F32), jax.ShapeDtypeStruct((n, lk, classes * width), F32),
                 jax.ShapeDtypeStruct((n, lk, classes * width), F32)]
    kv_spec = pl.BlockSpec((1, lk, width), lambda n_, r, i: (n_, 0, r))
    out_specs = [pl.BlockSpec((1, tq, width), lambda n_, r, i: (n_, i, r)), kv_spec, kv_spec]
    if has_kb:
        kb_spec = pl.BlockSpec((1,) + kb.shape[1:], lambda n_, r, i: (n_, 0, 0, 0))
        in_specs.append(kb_spec)
        args.append(kb)
        out_shape += [jax.ShapeDtypeStruct(kb.shape, F32), jax.ShapeDtypeStruct((n, lq, LANES), F32)]
        out_specs += [kb_spec, pl.BlockSpec((1, tq, LANES), lambda n_, r, i: (n_, i, 0))]
    return pl.pallas_call(
        body, grid=(n, classes, lq // tq), in_specs=in_specs, out_specs=out_specs, out_shape=out_shape, name=name,
        compiler_params=_params("parallel", "parallel", "arbitrary"),
    )(*args)


def _head_delta(do, out, heads, *, name, lse=None, sink=None):
    width = out.shape[1]
    gather = _head_expand(heads, width).T

    if sink is None:
        return _rowwise(lambda do, o, e: _dot_exact(do * o, e), [do, out], [gather], [(LANES, F32)], name=name)[0]

    def fn(do, o, lse, e, sink):
        delta = _dot_exact(do * o, e)
        return delta, -jnp.sum(jnp.exp(sink - lse) * delta, axis=0, keepdims=True)
    return _rowwise(fn, [do, out, lse], [gather, sink], [(LANES, F32)], [(1, LANES)], name=name)


def _rope_tables():
    half = HEAD_DIM // 2
    inv = ROPE_THETA ** (-jnp.arange(half, dtype=F32) / half)
    ang = jnp.arange(SEQ, dtype=F32)[:, None] * inv[None, :]
    cos = jnp.tile(jnp.cos(ang), (1, 2 * ATT_W // HEAD_DIM))
    sin = jnp.tile(jnp.concatenate([-jnp.sin(ang), jnp.sin(ang)], axis=1), (1, ATT_W // HEAD_DIM))
    return cos, sin


def _rotate(x, cos, sin):
    width = x.shape[1]
    lane = lax.broadcasted_iota(jnp.int32, x.shape, 1)
    first_half = (lane % HEAD_DIM) < (HEAD_DIM // 2)
    swapped = jnp.where(first_half, pltpu.roll(x, width - HEAD_DIM // 2, axis=1), pltpu.roll(x, HEAD_DIM // 2, axis=1))
    return x * cos[:, :width] + swapped * sin[:, :width]


def _gelu(x):
    k = math.sqrt(2.0 / math.pi)
    t = jnp.tanh(k * (x + 0.044715 * x * x * x))
    return 0.5 * x * (1.0 + t), t


def _gelu_grad(x, t):
    k = math.sqrt(2.0 / math.pi)
    return 0.5 * (1.0 + t) + 0.5 * x * (1.0 - t * t) * k * (1.0 + 3.0 * 0.044715 * x * x)


def _shift_down(x, halo, s):
    ext = jnp.concatenate([halo, x], axis=0)
    return pltpu.roll(ext, s, axis=0)[8:, :]


def _shift_up(x, halo, s):
    rows = x.shape[0]
    ext = jnp.concatenate([x, halo], axis=0)
    return pltpu.roll(ext, rows + 8 - s, axis=0)[:rows, :]


def _lru_gates(u, halo, cw, cb, wa, ba, wi, bi, lam):
    taps = [_shift_down(u, halo, CONV_WIDTH - 1 - k) for k in range(CONV_WIDTH - 1)] + [u]
    uc = cb + sum(cw[k:k + 1, :] * taps[k] for k in range(CONV_WIDTH))
    ucb = uc.astype(BF16)
    r = jax.nn.sigmoid(jnp.dot(ucb, wa, preferred_element_type=F32) + ba)
    gi = jax.nn.sigmoid(jnp.dot(ucb, wi, preferred_element_type=F32) + bi)
    sp = jnp.maximum(-lam, 0.0) + jnp.log(1.0 + jnp.exp(-jnp.abs(lam)))
    log_a = -LRU_C * r * sp
    a = jnp.exp(log_a)
    x2 = 2.0 * log_a
    one_minus_a2 = jnp.where(x2 > -0.01, -x2 * (1.0 + x2 * (0.5 + x2 * (1.0 / 6.0 + x2 / 24.0))), 1.0 - jnp.exp(x2))
    mult = jnp.sqrt(one_minus_a2)
    return taps, uc, ucb, r, gi, sp, a, mult


def _lru_specs(nchunk, reverse):
    def chunk(b, c):
        return b * nchunk + ((nchunk - 1 - c) if reverse else c)

    def halo_before(b, c):
        return jnp.maximum(chunk(b, c) * (LRU_CHUNK // 8) - 1, 0)

    return chunk, halo_before


def _lru_fwd(zug, cw, cb, wa, ba, wi, bi, lam, *, name):
    total = zug.shape[0]
    nchunk = SEQ // LRU_CHUNK
    w = LRU_WIDTH
    chunk, halo_before = _lru_specs(nchunk, False)

    def body(u_ref, uh_ref, g_ref, cw_ref, cb_ref, wa_ref, ba_ref, wi_ref, bi_ref, lam_ref, y_ref, h_ref, a_sc, x_sc, carry_sc):
        c = pl.program_id(1)
        u = u_ref[...]
        halo = jnp.where(c > 0, uh_ref[...], 0.0)
        _, uc, _, _, gi, _, a, mult = _lru_gates(u, halo, cw_ref[...], cb_ref[...], wa_ref[...], ba_ref[...],
                                                 wi_ref[...], bi_ref[...], lam_ref[...])
        a_sc[...] = a
        x_sc[...] = mult * (gi * uc)

        @pl.when(c == 0)
        def _():
            carry_sc[...] = jnp.zeros(carry_sc.shape, F32)

        def tile_step(t, h):
            r0 = pl.multiple_of(t * 8, 8)
            at = a_sc[pl.ds(r0, 8), :]
            xt = x_sc[pl.ds(r0, 8), :]
            rows = []
            for j in range(8):
                h = at[j:j + 1, :] * h + xt[j:j + 1, :]
                rows.append(h)
            h_ref[pl.ds(r0, 8), :] = jnp.concatenate(rows, axis=0)
            return h

        h_last = lax.fori_loop(0, LRU_CHUNK // 8, tile_step, carry_sc[0:1, :])
        carry_sc[0:1, :] = h_last
        gel, _ = _gelu(g_ref[...])
        y_ref[...] = (h_ref[...] * gel).astype(BF16)

    small = lambda arr: pl.BlockSpec(arr.shape, lambda b, c: (0, 0))
    return pl.pallas_call(
        body, grid=(total // SEQ, nchunk),
        in_specs=[pl.BlockSpec((LRU_CHUNK, w), lambda b, c: (chunk(b, c), 0)),
                  pl.BlockSpec((8, w), lambda b, c: (halo_before(b, c), 0)),
                  pl.BlockSpec((LRU_CHUNK, w), lambda b, c: (chunk(b, c), 1)),
                  small(cw), small(cb), small(wa), small(ba), small(wi), small(bi), small(lam)],
        out_specs=[pl.BlockSpec((LRU_CHUNK, w), lambda b, c: (chunk(b, c), 0))] * 2,
        out_shape=[jax.ShapeDtypeStruct((total, w), BF16), jax.ShapeDtypeStruct((total, w), F32)],
        scratch_shapes=[pltpu.VMEM((LRU_CHUNK, w), F32), pltpu.VMEM((LRU_CHUNK, w), F32), pltpu.VMEM((8, w), F32)],
        name=name, compiler_params=_params("arbitrary", "arbitrary"),
    )(zug, zug, zug, cw, cb, wa, ba, wi, bi, lam)


def _lru_bwd(zug, hl, dy, cw, cb, wa, ba, wi, bi, lam, *, name):
    total = zug.shape[0]
    nchunk = SEQ // LRU_CHUNK
    w = LRU_WIDTH
    chunk, halo_before = _lru_specs(nchunk, True)
    tn = (((0,), (0,)), ((), ()))
    nt = (((1,), (1,)), ((), ()))

    def body(u_ref, uh_ref, g_ref, hl_ref, hh_ref, dy_ref, cw_ref, cb_ref, wa_ref, ba_ref, wi_ref, bi_ref, lam_ref,
             dz_ref, dcw_ref, dcb_ref, dwa_ref, dba_ref, dwi_ref, dbi_ref, dlam_ref,
             a_sc, d_sc, g_sc, gcarry_sc, acarry_sc, duc_sc):
        b, c = pl.program_id(0), pl.program_id(1)
        first_chunk = c == nchunk - 1

        @pl.when((b == 0) & (c == 0))
        def _():
            for ref in (dcw_ref, dcb_ref, dwa_ref, dba_ref, dwi_ref, dbi_ref, dlam_ref):
                ref[...] = jnp.zeros(ref.shape, F32)

        @pl.when(c == 0)
        def _():
            gcarry_sc[...] = jnp.zeros(gcarry_sc.shape, F32)
            acarry_sc[...] = jnp.zeros(acarry_sc.shape, F32)
            duc_sc[...] = jnp.zeros(duc_sc.shape, F32)

        u = u_ref[...]
        halo = jnp.where(first_chunk, 0.0, uh_ref[...])
        cw, wa, wi, lam = cw_ref[...], wa_ref[...], wi_ref[...], lam_ref[...]
        taps, uc, ucb, r, gi, sp, a, mult = _lru_gates(u, halo, cw, cb_ref[...], wa, ba_ref[...], wi, bi_ref[...], lam)
        gate = g_ref[...]
        gel, th = _gelu(gate)
        dy = dy_ref[...]
        hl = hl_ref[...]
        a_sc[...] = a
        d_sc[...] = dy * gel
        dgate = dy * hl * _gelu_grad(gate, th)

        def tile_step(t, carry):
            g_next, a_next = carry
            r0 = pl.multiple_of((LRU_CHUNK // 8 - 1 - t) * 8, 8)
            dt = d_sc[pl.ds(r0, 8), :]
            at = a_sc[pl.ds(r0, 8), :]
            rows = [None] * 8
            for j in reversed(range(8)):
                g_next = dt[j:j + 1, :] + a_next * g_next
                a_next = at[j:j + 1, :]
                rows[j] = g_next
            g_sc[pl.ds(r0, 8), :] = jnp.concatenate(rows, axis=0)
            return g_next, a_next

        g_last, a_last = lax.fori_loop(0, LRU_CHUNK // 8, tile_step, (gcarry_sc[0:1, :], acarry_sc[0:1, :]))
        gcarry_sc[0:1, :] = g_last
        acarry_sc[0:1, :] = a_last

        gs = g_sc[...]
        hl_halo = jnp.where(first_chunk, 0.0, hh_ref[...])
        da = gs * _shift_down(hl, hl_halo, 1)
        dmult = gs * gi * uc
        dgi = gs * mult * uc
        duc = gs * mult * gi
        dlog_a = da * a - dmult * a * a / mult
        dr = dlog_a * (-LRU_C * sp)
        dsp = jnp.sum(dlog_a * (-LRU_C * r), axis=0, keepdims=True)
        dlam_ref[...] += -dsp * jax.nn.sigmoid(-lam)
        dpa = dr * r * (1.0 - r)
        dpi = dgi * gi * (1.0 - gi)
        dpab, dpib = dpa.astype(BF16), dpi.astype(BF16)
        dba_ref[...] += jnp.sum(dpa, axis=0, keepdims=True)
        dbi_ref[...] += jnp.sum(dpi, axis=0, keepdims=True)
        dwa_ref[...] += lax.dot_general(ucb, dpab, tn, preferred_element_type=F32)
        dwi_ref[...] += lax.dot_general(ucb, dpib, tn, preferred_element_type=F32)
        duc = duc + lax.dot_general(dpab, wa, nt, preferred_element_type=F32)
        duc = duc + lax.dot_general(dpib, wi, nt, preferred_element_type=F32)
        dcb_ref[...] += jnp.sum(duc, axis=0, keepdims=True)
        dcw_ref[...] += jnp.concatenate([jnp.sum(duc * taps[k], axis=0, keepdims=True) for k in range(CONV_WIDTH)], axis=0)
        after = duc_sc[...]
        du = cw[CONV_WIDTH - 1:CONV_WIDTH, :] * duc
        for k in range(CONV_WIDTH - 1):
            du = du + cw[k:k + 1, :] * _shift_up(duc, after, CONV_WIDTH - 1 - k)
        duc_sc[...] = duc[0:8, :]
        dz_ref[:, 0:w] = du.astype(BF16)
        dz_ref[:, w:2 * w] = dgate.astype(BF16)

    small = lambda arr: pl.BlockSpec(arr.shape, lambda b, c: (0, 0))
    acc = lambda shape: pl.BlockSpec(shape, lambda b, c: (0, 0))
    chunk_spec = lambda cb_: pl.BlockSpec((LRU_CHUNK, w), lambda b, c: (chunk(b, c), cb_))
    halo_spec = pl.BlockSpec((8, w), lambda b, c: (halo_before(b, c), 0))
    acc_shapes = [(CONV_WIDTH, w), (1, w), (w, w), (1, w), (w, w), (1, w), (1, w)]
    return pl.pallas_call(
        body, grid=(total // SEQ, nchunk),
        in_specs=[chunk_spec(0), halo_spec, chunk_spec(1), chunk_spec(0), halo_spec, chunk_spec(0),
                  small(cw), small(cb), small(wa), small(ba), small(wi), small(bi), small(lam)],
        out_specs=[pl.BlockSpec((LRU_CHUNK, 2 * w), lambda b, c: (chunk(b, c), 0))] + [acc(s) for s in acc_shapes],
        out_shape=[jax.ShapeDtypeStruct((total, 2 * w), BF16)] + [jax.ShapeDtypeStruct(s, F32) for s in acc_shapes],
        scratch_shapes=[pltpu.VMEM((LRU_CHUNK, w), F32)] * 3 + [pltpu.VMEM((8, w), F32)] * 3,
        name=name, compiler_params=_params("arbitrary", "arbitrary"),
    )(zug, zug, zug, hl, hl, dy, cw, cb, wa, ba, wi, bi, lam)


def _block_diag(wb):
    eye = jnp.eye(LRU_BLOCKS, dtype=wb.dtype)
    return jnp.einsum("gcd,gh->gchd", wb, eye).reshape(LRU_WIDTH, LRU_WIDTH).astype(BF16)


def _diag_blocks(wd):
    blk = LRU_WIDTH // LRU_BLOCKS
    return jnp.stack([wd[g * blk:(g + 1) * blk, g * blk:(g + 1) * blk] for g in range(LRU_BLOCKS)])


FOX_SCAN = 256


def _fox_bias(fl, bf, *, name):
    nb = fl.shape[0]

    def body(fl_ref, bf_ref, c_ref):
        x = fl_ref[0] + bf_ref[...]
        logf = jnp.minimum(x, 0.0) - jnp.log(1.0 + jnp.exp(-jnp.abs(x)))
        upper = (lax.broadcasted_iota(jnp.int32, (FOX_SCAN, FOX_SCAN), 0)
                 <= lax.broadcasted_iota(jnp.int32, (FOX_SCAN, FOX_SCAN), 1)).astype(BF16)
        carry = jnp.zeros((LRU_BLOCKS, 1), F32)
        for j in range(SEQ // FOX_SCAN):
            blk = logf[:, j * FOX_SCAN:(j + 1) * FOX_SCAN]
            c_ref[0, :, j * FOX_SCAN:(j + 1) * FOX_SCAN] = _dot_exact(blk, upper) + carry
            carry = carry + jnp.sum(blk, axis=1, keepdims=True)

    return pl.pallas_call(
        body, grid=(nb,),
        in_specs=[pl.BlockSpec((1, 8, SEQ), lambda b: (b, 0, 0)), pl.BlockSpec((8, 1), lambda b: (0, 0))],
        out_specs=pl.BlockSpec((1, 8, SEQ), lambda b: (b, 0, 0)),
        out_shape=jax.ShapeDtypeStruct((nb, 8, SEQ), F32), name=name, compiler_params=_params("arbitrary"),
    )(fl, bf)


def _fox_bias_bwd(fl, bf, dc, *, name):
    nb = fl.shape[0]

    def body(fl_ref, bf_ref, dc_ref, dfl_ref, dbf_ref):
        @pl.when(pl.program_id(0) == 0)
        def _():
            dbf_ref[...] = jnp.zeros(dbf_ref.shape, F32)

        x = fl_ref[0] + bf_ref[...]
        dc_all = dc_ref[0]
        lower = (lax.broadcasted_iota(jnp.int32, (FOX_SCAN, FOX_SCAN), 0)
                 >= lax.broadcasted_iota(jnp.int32, (FOX_SCAN, FOX_SCAN), 1)).astype(BF16)
        carry = jnp.zeros((LRU_BLOCKS, 1), F32)
        total = jnp.zeros((LRU_BLOCKS, 1), F32)
        for j in reversed(range(SEQ // FOX_SCAN)):
            cols = slice(j * FOX_SCAN, (j + 1) * FOX_SCAN)
            blk = dc_all[:, cols]
            dlogf = _dot_exact(blk, lower) + carry
            carry = carry + jnp.sum(blk, axis=1, keepdims=True)
            dx = dlogf * jax.nn.sigmoid(-x[:, cols])
            dfl_ref[0, :, cols] = dx
            total = total + jnp.sum(dx, axis=1, keepdims=True)
        dbf_ref[...] += total

    return pl.pallas_call(
        body, grid=(nb,),
        in_specs=[pl.BlockSpec((1, 8, SEQ), lambda b: (b, 0, 0)), pl.BlockSpec((8, 1), lambda b: (0, 0)),
                  pl.BlockSpec((1, 8, SEQ), lambda b: (b, 0, 0))],
        out_specs=[pl.BlockSpec((1, 8, SEQ), lambda b: (b, 0, 0)), pl.BlockSpec((8, 1), lambda b: (0, 0))],
        out_shape=[jax.ShapeDtypeStruct((nb, 8, SEQ), F32), jax.ShapeDtypeStruct((8, 1), F32)],
        name=name, compiler_params=_params("arbitrary"),
    )(fl, bf, dc)


def _seq3(a, nb):
    return a.reshape(nb, a.shape[0] // nb, a.shape[1])


def _flat2(a):
    return a.reshape(a.shape[0] * a.shape[1], a.shape[2])


def _mlp_fwd(h, p, tag):
    hn = _rms_fwd(h, p["norm"], name=f"{tag}_norm")
    up, act = _matmul(hn, p["w_up_t"], tb=True, outs=(F32, BF16),
                      epilogue=lambda acc: (acc, jnp.square(jnp.maximum(acc, 0.0))), name=f"{tag}_up")
    out, = _matmul(act, p["w_down"], extras=(h,), epilogue=lambda acc, res: (acc + res,), name=f"{tag}_down")
    return out, (h, hn, up, act)


def _mlp_bwd(dh, dhb, p, saved, tag):
    h, hn, up, act = saved
    dup, = _matmul(dhb, p["w_down"], tb=True, outs=(BF16,), extras=(up,),
                   epilogue=lambda acc, up: (acc * (2.0 * jnp.maximum(up, 0.0)),), name=f"{tag}_bwd_dup")
    dw_down, = _matmul(act, dhb, ta=True, name=f"{tag}_bwd_dwdown")
    dw_up_t, = _matmul(dup, hn, ta=True, name=f"{tag}_bwd_dwup")
    dhn, = _matmul(dup, p["w_up_t"], name=f"{tag}_bwd_dhn")
    dh2, dh2b, dg = _rms_bwd(h, dhn, dh, p["norm"], name=f"{tag}_bwd_norm")
    return dh2, dh2b, {"norm": dg, "w_up_t": dw_up_t, "w_down": dw_down}


def _xa_fwd(h, mem, p, tag, nb):
    hn = _rms_fwd(h, p["norm"], name=f"{tag}_norm")
    mem_n = _rms_fwd(mem, p["mem_norm"], name=f"{tag}_memnorm")
    q, = _matmul(hn, p["w_q"], outs=(BF16,), name=f"{tag}_q")
    kv, = _matmul(mem_n, p["w_kv_t"], tb=True, outs=(BF16,), name=f"{tag}_kv")
    q3, kv3 = _seq3(q, nb), _seq3(kv, nb)
    o, lse = _attn_fwd((q3, 0), (kv3, 0), (kv3, 1), heads=XA_HEADS, dh=XA_HEAD_DIM, mode="full",
                       name=f"{tag}_attn")
    o = _flat2(o)
    ob, = _rowwise(lambda o: o, [o], [], [(D_MODEL, BF16)], name=f"{tag}_ocast")
    out, = _matmul(ob, p["w_o"], extras=(h,), epilogue=lambda acc, res: (acc + res,), name=f"{tag}_o")
    return out, (h, hn, mem_n, q3, kv3, o, ob, lse)


def _xa_bwd(dh, dhb, mem, p, saved, tag, nb):
    h, hn, mem_n, q3, kv3, o, ob, lse = saved
    do, dob = _matmul(dhb, p["w_o"], tb=True, outs=(F32, BF16), epilogue=lambda acc: (acc, acc), name=f"{tag}_bwd_do")
    dw_o, = _matmul(ob, dhb, ta=True, name=f"{tag}_bwd_dwo")
    delta = _head_delta(do, o, XA_HEADS, name=f"{tag}_bwd_delta")
    dq, dk, dv = _attn_bwd((q3, 0), (kv3, 0), (kv3, 1), (_seq3(dob, nb), 0), lse, _seq3(delta, nb), heads=XA_HEADS,
                           dh=XA_HEAD_DIM, mode="full", name=f"{tag}_bwd_attn")
    dqb, = _rowwise(lambda x: x, [_flat2(dq)], [], [(D_MODEL, BF16)], name=f"{tag}_bwd_dqcast")
    dkvb, = _rowwise(lambda a, b: jnp.concatenate([a, b], axis=1), [_flat2(dk), _flat2(dv)], [], [(2 * D_MODEL, BF16)],
                     name=f"{tag}_bwd_dkvcast")
    dw_q, = _matmul(hn, dqb, ta=True, name=f"{tag}_bwd_dwq")
    dw_kv_t, = _matmul(dkvb, mem_n, ta=True, name=f"{tag}_bwd_dwkv")
    dhn, = _matmul(dqb, p["w_q"], tb=True, name=f"{tag}_bwd_dhn")
    dmem_n, = _matmul(dkvb, p["w_kv_t"], name=f"{tag}_bwd_dmemn")
    dg_mem, = _rowwise(lambda x, d, g: _rms_bwd_vals(x, d, g)[1], [mem, dmem_n], [p["mem_norm"]], [], [(1, D_MODEL)],
                       name=f"{tag}_bwd_memnorm")
    dh2, dh2b, dg = _rms_bwd(h, dhn, dh, p["norm"], name=f"{tag}_bwd_norm")
    return dh2, dh2b, {"norm": dg, "mem_norm": dg_mem, "w_q": dw_q, "w_kv_t": dw_kv_t, "w_o": dw_o}


AB_MAIN = 2 * LRU_WIDTH + 3 * ATT_W


def _ab_fwd(h, p, nb):
    hn = _rms_fwd(h, p["norm"], name="ab_norm")
    w_in_t = p["w_in_t"]
    zug, = _matmul(hn, w_in_t[:2 * LRU_WIDTH], tb=True, name="ab_in_ug")
    qkv, = _matmul(hn, w_in_t[2 * LRU_WIDTH:AB_MAIN], tb=True, outs=(BF16,), name="ab_in_qkv")
    zf, = _matmul(hn, w_in_t[AB_MAIN:], tb=True, name="ab_in_f")
    fl = zf[:, :8].reshape(nb, SEQ, 8).transpose(0, 2, 1)
    cbias = _fox_bias(fl, p["b_f"], name="ab_fox_bias")
    kb = cbias.reshape(nb, 8, SEQ // ATT_CHUNK, ATT_CHUNK)
    y_a, hl = _lru_fwd(zug, p["conv_w"], p["conv_b"], p["w_a"], p["b_a"], p["w_i"], p["b_i"], p["lam"], name="ab_lru")
    qkv3 = _seq3(qkv, nb)
    o, lse = _attn_fwd((qkv3, 0), (qkv3, 1), (qkv3, 2), kb, heads=8, dh=HEAD_DIM, mode="causal", name="ab_fox")
    o = _flat2(o)
    y, = _rowwise(lambda a, b: jnp.concatenate([a, b.astype(BF16)], axis=1), [y_a, o], [], [(D_MODEL, BF16)], name="ab_ycat")
    out, = _matmul(y, p["w_out"], extras=(h,), epilogue=lambda acc, res: (acc + res,), name="ab_out")
    return out, (h, hn, zug, qkv3, fl, kb, hl, o, lse, y)


def _ab_bwd(dh, dhb, p, saved, nb):
    h, hn, zug, qkv3, fl, kb, hl, o, lse, y = saved
    dy, dyb = _matmul(dhb, p["w_out"], tb=True, outs=(F32, BF16), epilogue=lambda acc: (acc, acc), name="ab_bwd_dy")
    dw_out, = _matmul(y, dhb, ta=True, name="ab_bwd_dwout")
    dzug, dcw, dcb, dwa, dba, dwi, dbi, dlam = _lru_bwd(zug, hl, dy, p["conv_w"], p["conv_b"], p["w_a"], p["b_a"],
                                                        p["w_i"], p["b_i"], p["lam"], name="ab_bwd_lru")
    delta = _head_delta((dy, ATT_W, 1), o, 8, name="ab_bwd_delta")
    dq, dk, dv, dkb, drow = _attn_bwd((qkv3, 0), (qkv3, 1), (qkv3, 2), (_seq3(dyb, nb), 1), lse, _seq3(delta, nb), kb,
                                      heads=8, dh=HEAD_DIM, mode="causal", name="ab_bwd_fox")
    dcum = dkb.reshape(nb, 8, SEQ) + drow[:, :, :8].transpose(0, 2, 1)
    dfl, dbf = _fox_bias_bwd(fl, p["b_f"], dcum, name="ab_bwd_fox_bias")
    dzf = jnp.pad(dfl.transpose(0, 2, 1).reshape(nb * SEQ, 8), ((0, 0), (0, LANES - 8)))
    dz, = _rowwise(lambda a, q, k, v, f: jnp.concatenate([a, q.astype(BF16), k.astype(BF16), v.astype(BF16), f.astype(BF16)], axis=1),
                   [dzug, _flat2(dq), _flat2(dk), _flat2(dv), dzf], [], [(AB_MAIN + LANES, BF16)], name="ab_bwd_dzcat")
    dw_in_t, = _matmul(dz, hn, ta=True, tm=384, name="ab_bwd_dwin")
    dhn, = _matmul(dz, p["w_in_t"], tk=AB_MAIN + LANES, name="ab_bwd_dhn")
    dh2, dh2b, dg = _rms_bwd(h, dhn, dh, p["norm"], name="ab_bwd_norm")
    grads = {"norm": dg, "w_in_t": dw_in_t[:AB_MAIN + 8], "conv_w": dcw, "conv_b": dcb, "w_a": _diag_blocks(dwa), "b_a": dba,
             "w_i": _diag_blocks(dwi), "b_i": dbi, "lam": dlam, "b_f": dbf.reshape(1, 8), "w_out": dw_out}
    return dh2, dh2b, grads


CD_IN = 3 * ATT_W + ATT_W + 2 * SWA_KW


def _class_view(a, dil):
    return a.reshape(a.shape[0], a.shape[1] // dil, dil * a.shape[2])


def _gqa_share():
    src = jnp.arange(SWA_KW)[:, None]
    dst = jnp.arange(ATT_W)[None, :]
    per_kv = ATT_W // (SWA_KW // HEAD_DIM)
    return ((dst // per_kv == src // HEAD_DIM) & (dst % HEAD_DIM == src % HEAD_DIM)).astype(BF16)


def _cd_fwd(h, p, nb):
    hn = _rms_fwd(h, p["norm"], name="cd_norm")
    z, = _matmul(hn, p["w_in_t"], tb=True, tn=384, name="cd_in")
    cos, sin = _rope_tables()
    per_seq = SEQ // ROW_TILE
    table_map = lambda i: (i % per_seq, 0)

    def rope_fn(z, cos, sin, share):
        qc, kc, vc = z[:, 0:ATT_W], z[:, ATT_W:2 * ATT_W], z[:, 2 * ATT_W:3 * ATT_W]
        qd, kd, vd = z[:, 3 * ATT_W:4 * ATT_W], z[:, 4 * ATT_W:4 * ATT_W + SWA_KW], z[:, 4 * ATT_W + SWA_KW:]
        kd8 = jnp.dot(_rotate(kd, cos, sin).astype(BF16), share, preferred_element_type=F32)
        vd8 = jnp.dot(vd.astype(BF16), share, preferred_element_type=F32)
        return (jnp.concatenate([_rotate(qc, cos, sin), _rotate(kc, cos, sin)], axis=1), vc, _rotate(qd, cos, sin), kd8, vd8)
    qk, vc, qd, kd, vd = _rowwise(rope_fn, [z, cos, sin], [_gqa_share()], [(2 * ATT_W, BF16)] + [(ATT_W, BF16)] * 4,
                                  name="cd_rope", row_maps=[None, table_map, table_map])
    qk3, vc3 = _seq3(qk, nb), _seq3(vc, nb)
    branch = []
    for window, dil in DIL_PATTERN:
        o_i, lse_i = _attn_fwd((_class_view(qk3, dil), 0), (_class_view(qk3, dil), 1), (_class_view(vc3, dil), 0), classes=dil,
                               heads=8, dh=HEAD_DIM, mode="band", max_dist=window // dil, name=f"cd_dil{dil}")
        branch.append((o_i.reshape(nb * SEQ, ATT_W), lse_i.reshape(nb * SEQ, LANES)))
    expand = _head_expand(8, ATT_W)

    def combine(o1, o2, o3, l1, l2, l3, e):
        m = jnp.maximum(jnp.maximum(l1, l2), l3)
        lt = m + jnp.log(jnp.exp(l1 - m) + jnp.exp(l2 - m) + jnp.exp(l3 - m))
        o = sum(_dot_exact(jnp.exp(l - lt), e) * o_ for o_, l in ((o1, l1), (o2, l2), (o3, l3)))
        return o, lt
    y_c, lse_c = _rowwise(combine, [b[0] for b in branch] + [b[1] for b in branch], [expand], [(ATT_W, F32), (LANES, F32)],
                          name="cd_dil_combine")
    qd3, kd3, vd3 = _seq3(qd, nb), _seq3(kd, nb), _seq3(vd, nb)
    o_d, lse_band = _attn_fwd((qd3, 0), (kd3, 0), (vd3, 0), heads=8, dh=HEAD_DIM, mode="band", max_dist=SWA_WINDOW - 1,
                              name="cd_swa")

    def sink_combine(o, l, e, sink):
        lt = jnp.maximum(l, sink) + jnp.log(1.0 + jnp.exp(-jnp.abs(l - sink)))
        return o * _dot_exact(jnp.exp(l - lt), e), lt
    y_d, lse_d = _rowwise(sink_combine, [_flat2(o_d), _flat2(lse_band)], [expand, p["sink"]], [(ATT_W, F32), (LANES, F32)],
                          name="cd_swa_sink")
    y, = _rowwise(lambda a, b: jnp.concatenate([a, b], axis=1), [y_c, y_d], [], [(D_MODEL, BF16)], name="cd_ycat")
    out, = _matmul(y, p["w_out"], extras=(h,), epilogue=lambda acc, res: (acc + res,), name="cd_out")
    return out, (h, hn, qk3, vc3, qd3, kd3, vd3, y_c, lse_c, y_d, lse_d, y)


def _cd_bwd(dh, dhb, p, saved, nb):
    h, hn, qk3, vc3, qd3, kd3, vd3, y_c, lse_c, y_d, lse_d, y = saved
    dy, dyb = _matmul(dhb, p["w_out"], tb=True, outs=(F32, BF16), epilogue=lambda acc: (acc, acc), name="cd_bwd_dy")
    dw_out, = _matmul(y, dhb, ta=True, name="cd_bwd_dwout")
    dyb3 = _seq3(dyb, nb)
    delta_c = _head_delta((dy, ATT_W, 0), y_c, 8, name="cd_bwd_delta_dil")
    lse_c3, delta_c3 = _seq3(lse_c, nb), _seq3(delta_c, nb)
    parts = []
    for window, dil in DIL_PATTERN:
        cv = functools.partial(_class_view, dil=dil)
        dq_i, dk_i, dv_i = _attn_bwd((cv(qk3), 0), (cv(qk3), 1), (cv(vc3), 0), (cv(dyb3), 0), cv(lse_c3), cv(delta_c3),
                                     classes=dil, heads=8, dh=HEAD_DIM, mode="band", max_dist=window // dil,
                                     name=f"cd_bwd_dil{dil}")
        parts.append([a.reshape(nb * SEQ, ATT_W) for a in (dq_i, dk_i, dv_i)])
    delta_d, dsink = _head_delta((dy, ATT_W, 1), y_d, 8, lse=lse_d, sink=p["sink"], name="cd_bwd_delta_swa")
    dqd, dkd, dvd = _attn_bwd((qd3, 0), (kd3, 0), (vd3, 0), (dyb3, 1), _seq3(lse_d, nb), _seq3(delta_d, nb), heads=8,
                              dh=HEAD_DIM, mode="band", max_dist=SWA_WINDOW - 1, name="cd_bwd_swa")
    cos, sin = _rope_tables()
    per_seq = SEQ // ROW_TILE
    table_map = lambda i: (i % per_seq, 0)

    def unrope(q1, q2, q3, k1, k2, k3, v1, v2, v3, qd, kd8, vd8, cos, sin, gather):
        back = lambda x: _rotate(x, cos, -sin)
        kd, vd = _dot_exact(kd8, gather), _dot_exact(vd8, gather)
        return jnp.concatenate([back(q1 + q2 + q3), back(k1 + k2 + k3), v1 + v2 + v3, back(qd), back(kd), vd], axis=1)
    ins = [parts[b][t] for t in range(3) for b in range(3)] + [_flat2(dqd), _flat2(dkd), _flat2(dvd), cos, sin]
    dz, = _rowwise(unrope, ins, [_gqa_share().T], [(CD_IN, BF16)], name="cd_bwd_unrope",
                   row_maps=[None] * 12 + [table_map, table_map])
    dw_in_t, = _matmul(dz, hn, ta=True, tm=384, name="cd_bwd_dwin")
    dhn, = _matmul(dz, p["w_in_t"], tk=CD_IN, name="cd_bwd_dhn")
    dh2, dh2b, dg = _rms_bwd(h, dhn, dh, p["norm"], name="cd_bwd_norm")
    return dh2, dh2b, {"norm": dg, "w_in_t": dw_in_t, "sink": dsink[:, :8], "w_out": dw_out}


def _local_step(x, mem, target, w):
    nb = x.shape[0]
    h = x.reshape(nb * SEQ, D_MODEL)
    mem2 = mem.reshape(nb * MEM_LEN, D_MODEL)
    tgt = target.reshape(nb * SEQ, D_MODEL)

    h, s_ab = _ab_fwd(h, w["ab"], nb)
    h, s_xa0 = _xa_fwd(h, mem2, w["xa0"], "xa0", nb)
    h, s_mlp0 = _mlp_fwd(h, w["mlp0"], "mlp0")
    h, s_cd = _cd_fwd(h, w["cd"], nb)
    h, s_xa1 = _xa_fwd(h, mem2, w["xa1"], "xa1", nb)
    h, s_mlp1 = _mlp_fwd(h, w["mlp1"], "mlp1")

    dh, dhb, loss, dg_final = _loss_and_grad(h, tgt, w["final_norm"], name="loss")
    grads = {"final_norm": dg_final}
    dh, dhb, grads["mlp1"] = _mlp_bwd(dh, dhb, w["mlp1"], s_mlp1, "mlp1")
    dh, dhb, grads["xa1"] = _xa_bwd(dh, dhb, mem2, w["xa1"], s_xa1, "xa1", nb)
    dh, dhb, grads["cd"] = _cd_bwd(dh, dhb, w["cd"], s_cd, nb)
    dh, dhb, grads["mlp0"] = _mlp_bwd(dh, dhb, w["mlp0"], s_mlp0, "mlp0")
    dh, dhb, grads["xa0"] = _xa_bwd(dh, dhb, mem2, w["xa0"], s_xa0, "xa0", nb)
    dh, dhb, grads["ab"] = _ab_bwd(dh, dhb, w["ab"], s_ab, nb)
    return loss, dh.reshape(nb, SEQ, D_MODEL), grads


ANY = pl.BlockSpec(memory_space=pl.ANY)


def _place():
    x, y, c = lax.axis_index("x"), lax.axis_index("y"), lax.axis_index("c")
    return x, y, c, [(1 - x, y), (x, 1 - y), (1 - x, 1 - y)]


def _gather_chips(shard):
    half = shard.shape[0] // 2

    def body(src, out, send_sems, recv_sems):
        x, y, c, chips = _place()
        sibling = (x, y, 1 - c)

        def rows(slot, core):
            return out.at[slot, pl.ds(core * half, half)]

        def copy(k, src_ref, dst_ref, to):
            return pltpu.make_async_remote_copy(src_ref=src_ref, dst_ref=dst_ref, send_sem=send_sems.at[k],
                                                recv_sem=recv_sems.at[k], device_id=to, device_id_type=MESH)

        first = [copy(k, src.at[pl.ds(c * half, half)], rows(2 * x + y, c), (px, py, c)) for k, (px, py) in enumerate(chips)]
        for cp in first:
            cp.start()
        passed = [copy(3 + k, rows(2 * px + py, c), rows(2 * px + py, c), sibling) for k, (px, py) in enumerate(chips)]
        for k, (px, py) in enumerate(chips):
            copy(k, src.at[pl.ds(c * half, half)], rows(2 * px + py, c), (px, py, c)).wait_recv()
            passed[k].start()
        for k, (px, py) in enumerate(chips):
            copy(3 + k, rows(2 * px + py, 1 - c), rows(2 * px + py, 1 - c), sibling).wait_recv()
        for cp in first + passed:
            cp.wait_send()

    return pl.pallas_call(
        body, out_shape=jax.ShapeDtypeStruct((N_CHIPS,) + shard.shape, shard.dtype), in_specs=[ANY], out_specs=ANY,
        scratch_shapes=[pltpu.SemaphoreType.DMA((6,)), pltpu.SemaphoreType.DMA((6,))], name="gather_chips",
    )(shard)


def _swap_cores(block, *, name, half_rows=None):
    shape = block.shape if half_rows is None else (block.shape[0], half_rows, block.shape[2])

    def body(src, out, send_sem, recv_sem):
        x, y, c, _ = _place()
        part = src if half_rows is None else src.at[:, pl.ds((1 - c) * half_rows, half_rows)]
        cp = pltpu.make_async_remote_copy(src_ref=part, dst_ref=out, send_sem=send_sem, recv_sem=recv_sem,
                                          device_id=(x, y, 1 - c), device_id_type=MESH)
        cp.start()
        cp.wait()

    return pl.pallas_call(
        body, out_shape=jax.ShapeDtypeStruct(shape, block.dtype), in_specs=[ANY], out_specs=ANY,
        scratch_shapes=[pltpu.SemaphoreType.DMA(()), pltpu.SemaphoreType.DMA(())], name=name,
    )(block)


def _scatter_chips(parts):
    def body(src, out, send_sems, recv_sems):
        x, y, c, chips = _place()
        sends = [pltpu.make_async_remote_copy(src_ref=src.at[2 * px + py], dst_ref=out.at[k], send_sem=send_sems.at[k],
                                              recv_sem=recv_sems.at[k], device_id=(px, py, c), device_id_type=MESH)
                 for k, (px, py) in enumerate(chips)]
        for cp in sends:
            cp.start()
        for cp in sends:
            cp.wait_recv()
        for cp in sends:
            cp.wait_send()

    return pl.pallas_call(
        body, out_shape=jax.ShapeDtypeStruct((3,) + parts.shape[1:], parts.dtype), in_specs=[ANY], out_specs=ANY,
        scratch_shapes=[pltpu.SemaphoreType.DMA((3,)), pltpu.SemaphoreType.DMA((3,))], name="scatter_chips",
    )(parts)


def _sum_all_devices(vec):
    rows = vec.shape[0]

    def body(x_ref, total_ref, all_ref, send_sems, recv_sems, local_sem):
        x, y, c, chips = _place()
        me, sibling = (x, y, c), (x, y, 1 - c)

        def slot(px, py, pc):
            return all_ref.at[4 * px + 2 * py + pc]

        def copy(k, block, to, src=None):
            return pltpu.make_async_remote_copy(src_ref=slot(*block) if src is None else src, dst_ref=slot(*block),
                                                send_sem=send_sems.at[k], recv_sem=recv_sems.at[k], device_id=to,
                                                device_id_type=MESH)

        mine = pltpu.make_async_copy(x_ref, slot(*me), local_sem)
        mine.start()
        first = [copy(0, me, sibling, src=x_ref)] + [copy(1 + j, me, (*chip, c), src=x_ref) for j, chip in enumerate(chips)]
        for cp in first:
            cp.start()
        passed = [copy(4 + j, (*chip, c), sibling) for j, chip in enumerate(chips)]
        for j, chip in enumerate(chips):
            copy(1 + j, (*chip, c), me).wait_recv()
            passed[j].start()
        copy(0, sibling, me).wait_recv()
        for j, chip in enumerate(chips):
            copy(4 + j, (*chip, 1 - c), me).wait_recv()
        for cp in first + passed:
            cp.wait_send()
        mine.wait()
        acc = all_ref[0]
        for d in range(1, 8):
            acc = acc + all_ref[d]
        total_ref[...] = acc

    vmem = pl.BlockSpec(memory_space=pltpu.VMEM)
    return pl.pallas_call(
        body, out_shape=[jax.ShapeDtypeStruct((rows, LANES), F32), jax.ShapeDtypeStruct((8, rows, LANES), F32)],
        in_specs=[vmem], out_specs=[vmem, vmem],
        scratch_shapes=[pltpu.SemaphoreType.DMA((7,)), pltpu.SemaphoreType.DMA((7,)), pltpu.SemaphoreType.DMA(())],
        name="sum_all_devices", compiler_params=pltpu.CompilerParams(vmem_limit_bytes=VMEM_LIMIT_BYTES),
    )(vec)[0]


PACK_COLS = 1024
BIG = (("ab_w_in", 1, 642, True), ("ab_w_out", 1, 256, False), ("cd_w_in", 1, 576, True), ("cd_w_out", 1, 256, False),
       ("xa_w_q", 2, 256, False), ("xa_w_kv", 2, 512, True), ("xa_w_o", 2, 256, False), ("mlp_w_up", 2, 1024, True),
       ("mlp_w_down", 2, 1024, False))
ENTRIES = tuple((name, layer, rows, transposed) for name, layers, rows, transposed in BIG for layer in range(layers))
PACK_ROWS = -(-sum(e[2] for e in ENTRIES) // 32) * 32
REDUCE_TILE = 208
assert (PACK_ROWS // 2) % REDUCE_TILE == 0

SMALL = (("ab_norm", (1, 1024)), ("ab_conv_w", (1, 4, 512)), ("ab_conv_b", (1, 512)), ("lru_w_a", (1, 8, 64, 64)),
         ("lru_b_a", (1, 512)), ("lru_w_i", (1, 8, 64, 64)), ("lru_b_i", (1, 512)), ("lru_lambda", (1, 512)),
         ("fox_b_f", (1, 8)), ("cd_norm", (1, 1024)), ("cd_sink", (1, 8)), ("xa_norm", (2, 1024)),
         ("xa_mem_norm", (2, 1024)), ("mlp_norm", (2, 1024)), ("final_norm", (1024,)), ("loss", (1,)))
SPLIT_SMALL = ("ab_conv_w", "cd_norm")


def _pack_rows(parts, dtype):
    lead = parts[0].shape[:-2]
    pad = jnp.zeros(lead + (PACK_ROWS - sum(e[2] for e in ENTRIES), PACK_COLS), dtype)
    return jnp.concatenate([p.astype(dtype) for p in parts] + [pad], axis=len(lead))


def _unpack_rows(packed):
    out, r0 = [], 0
    for _, _, rows, _ in ENTRIES:
        out.append(packed[..., r0:r0 + rows, :])
        r0 += rows
    return out


def _shard_rows(w):
    return [(w[name][layer].T if transposed else w[name][layer]) for name, layer, _, transposed in ENTRIES]


def _shard_blocks(rows):
    out = {}
    for (name, layer, _, transposed), r in zip(ENTRIES, rows):
        out.setdefault(name, []).append(r.T if transposed else r)
    return {name: jnp.stack(layers) for name, layers in out.items()}


def _pack_small(vals):
    flat = jnp.concatenate([vals[name].astype(F32).reshape(-1) for name, _ in SMALL])
    size = -(-flat.shape[0] // (8 * LANES)) * (8 * LANES)
    return jnp.pad(flat, (0, size - flat.shape[0])).reshape(-1, LANES)


def _unpack_small(packed):
    flat, out, at = packed.reshape(-1), {}, 0
    for name, shape in SMALL:
        out[name] = flat[at:at + math.prod(shape)].reshape(shape)
        at += math.prod(shape)
    return out


def _chip_part(full, chip):
    width = full.shape[-1] // N_CHIPS
    return lax.dynamic_slice_in_dim(full, chip * width, width, axis=full.ndim - 1)


def _chip_embed(part, chip):
    full = jnp.zeros(part.shape[:-1] + (part.shape[-1] * N_CHIPS,), part.dtype)
    return lax.dynamic_update_slice_in_dim(full, part, chip * part.shape[-1], axis=part.ndim - 1)


SUBLAYER_KEYS = {"ab_w_in": ("ab", "w_in_t"), "ab_w_out": ("ab", "w_out"), "cd_w_in": ("cd", "w_in_t"), "cd_w_out": ("cd", "w_out"),
                 "xa_w_q": ("xa", "w_q"), "xa_w_kv": ("xa", "w_kv_t"), "xa_w_o": ("xa", "w_o"), "mlp_w_up": ("mlp", "w_up_t"),
                 "mlp_w_down": ("mlp", "w_down")}


def _sublayer(name, layer):
    block, leaf = SUBLAYER_KEYS[name]
    return (block if block in ("ab", "cd") else f"{block}{layer}"), leaf


def _build_params(full_rows, w):
    pad = lambda a: jnp.pad(a, ((0, 0), (0, LANES - a.shape[1])))
    params = {
        "ab": dict(norm=w["ab_norm"], conv_w=w["ab_conv_w"][0], conv_b=w["ab_conv_b"], w_a=_block_diag(w["lru_w_a"][0]),
                   b_a=w["lru_b_a"], w_i=_block_diag(w["lru_w_i"][0]), b_i=w["lru_b_i"], lam=w["lru_lambda"],
                   b_f=w["fox_b_f"].reshape(8, 1)),
        "cd": dict(norm=w["cd_norm"], sink=pad(w["cd_sink"])),
        "final_norm": w["final_norm"].reshape(1, D_MODEL),
    }
    for layer in range(2):
        params[f"xa{layer}"] = dict(norm=w["xa_norm"][layer:layer + 1], mem_norm=w["xa_mem_norm"][layer:layer + 1])
        params[f"mlp{layer}"] = dict(norm=w["mlp_norm"][layer:layer + 1])
    for name, layer, _, _ in ENTRIES:
        block, leaf = _sublayer(name, layer)
        params[block][leaf] = full_rows[name, layer]
    w_in_t = params["ab"]["w_in_t"]
    params["ab"]["w_in_t"] = jnp.concatenate([w_in_t, jnp.zeros((LANES - 8, PACK_COLS), w_in_t.dtype)])
    return params


def _grad_rows(g):
    out = []
    for name, layer, _, _ in ENTRIES:
        block, leaf = _sublayer(name, layer)
        out.append(g[block][leaf])
    return out


def _reduce_to_owner(grads_by_chip, core):
    half = PACK_ROWS // 2
    steps = half // REDUCE_TILE
    chip = 2 * lax.axis_index("x") + lax.axis_index("y")
    place = jnp.stack([core, chip]).astype(jnp.int32)
    got = _swap_cores(grads_by_chip, name="swap_cores", half_rows=half)
    block = (1, REDUCE_TILE, PACK_COLS)

    def sum_cores(place_ref, mine_ref, got_ref, f32_ref, bf16_ref):
        total = mine_ref[...] + got_ref[...]
        f32_ref[...] = total
        bf16_ref[...] = total.astype(BF16)

    pair32, pair16 = pl.pallas_call(
        sum_cores, name="sum_cores",
        grid_spec=pltpu.PrefetchScalarGridSpec(
            num_scalar_prefetch=1, grid=(N_CHIPS, steps),
            in_specs=[pl.BlockSpec(block, lambda s, i, place_ref: (s, place_ref[0] * steps + i, 0)),
                      pl.BlockSpec(block, lambda s, i, place_ref: (s, i, 0))],
            out_specs=[pl.BlockSpec(block, lambda s, i, place_ref: (s, i, 0))] * 2),
        out_shape=[jax.ShapeDtypeStruct((N_CHIPS, half, PACK_COLS), F32), jax.ShapeDtypeStruct((N_CHIPS, half, PACK_COLS), BF16)],
        compiler_params=_params("arbitrary", "arbitrary"),
    )(place, grads_by_chip, got)
    landed = _scatter_chips(pair16)

    def sum_chips(place_ref, own_ref, landed_ref, out_ref):
        out_ref[...] = own_ref[0] + landed_ref[0].astype(F32) + landed_ref[1].astype(F32) + landed_ref[2].astype(F32)

    done = pl.pallas_call(
        sum_chips, name="sum_chips",
        grid_spec=pltpu.PrefetchScalarGridSpec(
            num_scalar_prefetch=1, grid=(steps,),
            in_specs=[pl.BlockSpec(block, lambda i, place_ref: (place_ref[1], i, 0)),
                      pl.BlockSpec((3, REDUCE_TILE, PACK_COLS), lambda i, place_ref: (0, i, 0))],
            out_specs=pl.BlockSpec(block[1:], lambda i, place_ref: (i, 0))),
        out_shape=jax.ShapeDtypeStruct((half, PACK_COLS), F32), compiler_params=_params("arbitrary"),
    )(place, pair32, landed)
    theirs = _swap_cores(done, name="share_halves")
    return jnp.where(core == 0, jnp.concatenate([done, theirs]), jnp.concatenate([theirs, done]))


def kernel(x, mem, ab_norm, ab_w_in, ab_conv_w, ab_conv_b, lru_w_a, lru_b_a, lru_w_i, lru_b_i, lru_lambda, fox_b_f, ab_w_out, cd_norm, cd_w_in, cd_sink, cd_w_out, xa_norm, xa_mem_norm, xa_w_q, xa_w_kv, xa_w_o, mlp_norm, mlp_w_up, mlp_w_down, final_norm, loss_target, m_ab_norm, m_ab_w_in, m_ab_conv_w, m_ab_conv_b, m_lru_w_a, m_lru_b_a, m_lru_w_i, m_lru_b_i, m_lru_lambda, m_fox_b_f, m_ab_w_out, m_cd_norm, m_cd_w_in, m_cd_sink, m_cd_w_out, m_xa_norm, m_xa_mem_norm, m_xa_w_q, m_xa_w_kv, m_xa_w_o, m_mlp_norm, m_mlp_w_up, m_mlp_w_down, m_final_norm, v_ab_norm, v_ab_w_in, v_ab_conv_w, v_ab_conv_b, v_lru_w_a, v_lru_b_a, v_lru_w_i, v_lru_b_i, v_lru_lambda, v_fox_b_f, v_ab_w_out, v_cd_norm, v_cd_w_in, v_cd_sink, v_cd_w_out, v_xa_norm, v_xa_mem_norm, v_xa_w_q, v_xa_w_kv, v_xa_w_o, v_mlp_norm, v_mlp_w_up, v_mlp_w_down, v_final_norm):
    given = dict(locals())
    weight_names = [name for name, _ in SMALL if name != "loss"] + [name for name, _, _, _ in BIG]
    w = {name: given[name] for name in weight_names}
    chip = 2 * lax.axis_index("x") + lax.axis_index("y")
    core = lax.axis_index("c")

    mine = _pack_rows(_shard_rows(w), BF16)
    gathered = _gather_chips(mine)
    slot = lax.broadcasted_iota(jnp.int32, (N_CHIPS, 1, 1), 0)
    full_rows = {}
    for (name, layer, rows, _), own, got in zip(ENTRIES, _unpack_rows(mine), _unpack_rows(gathered)):
        full_rows[name, layer] = jnp.where(slot == chip, own[None], got).reshape(N_CHIPS * rows, PACK_COLS)
    owner = (core == 0).astype(F32)
    quarters = {name: _chip_embed(w[name], chip) * owner for name in SPLIT_SMALL}
    zeros = {name: jnp.zeros(shape, F32) for name, shape in SMALL}
    small_full = _unpack_small(_sum_all_devices(_pack_small({**zeros, **quarters})))
    params = _build_params(full_rows, {**w, "ab_conv_w": small_full["ab_conv_w"], "cd_norm": small_full["cd_norm"]})

    loss_part, grad_x, g = _local_step(x, mem, loss_target, params)

    by_chip = _pack_rows([r.reshape(N_CHIPS, rows, PACK_COLS) for r, (_, _, rows, _) in zip(_grad_rows(g), ENTRIES)], F32)
    grads = _shard_blocks(_unpack_rows(_reduce_to_owner(by_chip, core)))
    pair = lambda key, leaf: jnp.stack([g[f"{key}0"][leaf], g[f"{key}1"][leaf]])

    small_local = {"ab_norm": g["ab"]["norm"], "ab_conv_w": g["ab"]["conv_w"], "ab_conv_b": g["ab"]["conv_b"],
                   "lru_w_a": g["ab"]["w_a"], "lru_b_a": g["ab"]["b_a"], "lru_w_i": g["ab"]["w_i"], "lru_b_i": g["ab"]["b_i"],
                   "lru_lambda": g["ab"]["lam"], "fox_b_f": g["ab"]["b_f"], "cd_norm": g["cd"]["norm"], "cd_sink": g["cd"]["sink"],
                   "xa_norm": pair("xa", "norm"), "xa_mem_norm": pair("xa", "mem_norm"), "mlp_norm": pair("mlp", "norm"),
                   "final_norm": g["final_norm"], "loss": loss_part[0, :1]}
    small_sum_packed = _sum_all_devices(_pack_small(small_local))
    small_sum = _unpack_small(small_sum_packed)
    loss = small_sum["loss"][0]

    delta, new_m, new_v = {}, {}, {}
    for name, _, _, _ in BIG:
        shape = w[name].shape
        flat = lambda a: a.reshape(-1, shape[-1])
        d, m2, v2 = _adamw(flat(w[name]), flat(grads[name]), flat(given["m_" + name]), flat(given["v_" + name]), name=f"adamw_{name}")
        delta[name], new_m[name], new_v[name] = d.reshape(shape), m2.reshape(shape), v2.reshape(shape)

    def small_state(prefix):
        vals = {name: (given[prefix + name] if name != "loss" else jnp.zeros((1,), F32)) for name, _ in SMALL}
        for name in SPLIT_SMALL:
            vals[name] = _chip_embed(vals[name], chip)
        return _pack_small(vals)
    packed = _adamw(small_state(""), small_sum_packed, small_state("m_"), small_state("v_"), name="adamw_small")
    for store, vals in zip((delta, new_m, new_v), packed):
        for name, val in _unpack_small(vals).items():
            store[name] = _chip_part(val, chip) if name in SPLIT_SMALL else val
    for name in SPLIT_SMALL:
        small_sum[name] = _chip_part(small_sum[name], chip)
    grads.update({name: small_sum[name] for name, _ in SMALL})

    order = ["ab_norm", "ab_w_in", "ab_conv_w", "ab_conv_b", "lru_w_a", "lru_b_a", "lru_w_i", "lru_b_i", "lru_lambda", "fox_b_f",
             "ab_w_out", "cd_norm", "cd_w_in", "cd_sink", "cd_w_out", "xa_norm", "xa_mem_norm", "xa_w_q", "xa_w_kv", "xa_w_o",
             "mlp_norm", "mlp_w_up", "mlp_w_down", "final_norm"]
    return (loss, grad_x, *[grads[n] for n in order], *[delta[n] for n in order], *[new_m[n] for n in order],
            *[new_v[n] for n in order])
```

```python
import functools
import math

import jax
import jax.numpy as jnp
from jax import lax
from jax.experimental import pallas as pl
from jax.experimental.pallas import tpu as pltpu

F32 = jnp.float32
BF16 = jnp.bfloat16
MESH = pl.DeviceIdType.MESH

D_MODEL = 1024
SEQ = 2048
HEAD_DIM = 64
LRU_WIDTH = 512
LRU_BLOCKS = 8
LRU_C = 8.0
CONV_WIDTH = 4
ATT_W = 512
SWA_KW = 128
SWA_WINDOW = 128
DIL_PATTERN = ((128, 1), (512, 4), (2048, 16))
MEM_LEN = 256
XA_HEADS = 4
XA_HEAD_DIM = 256
D_FF = 4096
ROPE_THETA = 10000.0
EPS = 1e-6
N_CHIPS = 4
LANES = 128

ADAM_LR, ADAM_B1, ADAM_B2, ADAM_EPS, ADAM_WD, ADAM_STEP = 0.001, 0.9, 0.999, 1e-08, 0.01, 10

VMEM_LIMIT_BYTES = 56 * 1024 * 1024
MASKED = -1e30
ROW_TILE = 512
LRU_CHUNK = 512
ATT_BLOCK = 128
BAND_ROWS = 256
ATT_CHUNK = 512
CAUSAL_ROWS = 256


def _params(*sem):
    return pltpu.CompilerParams(dimension_semantics=sem, vmem_limit_bytes=VMEM_LIMIT_BYTES)


def _rowwise(fn, rows, consts=(), out_rows=(), out_accs=(), *, name, tile=ROW_TILE, row_maps=None):
    rows = [r if isinstance(r, tuple) else (r, r.shape[1], 0) for r in rows]
    consts = list(consts)
    nr, nc, no = len(rows), len(consts), len(out_rows)
    total = rows[0][0].shape[0]
    tile = min(tile, total)
    assert total % tile == 0
    n = total // tile

    def body(*refs):
        outs = fn(*[r[...] for r in refs[:nr + nc]])
        outs = tuple(outs) if isinstance(outs, (tuple, list)) else (outs,)
        for ref, val in zip(refs[nr + nc:nr + nc + no], outs[:no]):
            ref[...] = val.astype(ref.dtype)
        for ref, val in zip(refs[nr + nc + no:], outs[no:]):
            @pl.when(pl.program_id(0) == 0)
            def _(ref=ref):
                ref[...] = jnp.zeros(ref.shape, ref.dtype)
            ref[...] += val

    in_specs = []
    for idx, (arr, width, cb) in enumerate(rows):
        if row_maps is not None and row_maps[idx] is not None:
            in_specs.append(pl.BlockSpec((tile, width), row_maps[idx]))
        else:
            in_specs.append(pl.BlockSpec((tile, width), functools.partial(lambda i, cb: (i, cb), cb=cb)))
    in_specs += [pl.BlockSpec(c.shape, lambda i: (0, 0)) for c in consts]
    out_shape = [jax.ShapeDtypeStruct((total, w), dt) for w, dt in out_rows]
    out_shape += [jax.ShapeDtypeStruct(s, F32) for s in out_accs]
    out_specs = [pl.BlockSpec((tile, w), lambda i: (i, 0)) for w, _ in out_rows]
    out_specs += [pl.BlockSpec(s, lambda i: (0, 0)) for s in out_accs]
    return pl.pallas_call(
        body, grid=(n,), in_specs=in_specs, out_specs=out_specs, out_shape=out_shape, name=name,
        compiler_params=_params("arbitrary"),
    )(*[r[0] for r in rows], *consts)


def _matmul(a, b, *, ta=False, tb=False, outs=(F32,), epilogue=None, extras=(), tm=1024, tn=512, tk=1024, name):
    m, k = (a.shape[1], a.shape[0]) if ta else a.shape
    n = b.shape[0] if tb else b.shape[1]
    assert (b.shape[1] if tb else b.shape[0]) == k
    tm, tn, tk = min(tm, m), min(tn, n), min(tk, k)
    assert m % tm == 0 and n % tn == 0 and k % tk == 0, (name, m, n, k)
    nk = k // tk
    ne, no = len(extras), len(outs)
    dims = (((0 if ta else 1,), (1 if tb else 0,)), ((), ()))

    def body(*refs):
        a_ref, b_ref = refs[:2]
        e_refs, o_refs = refs[2:2 + ne], refs[2 + ne:2 + ne + no]

        def finish(acc):
            vals = (acc,) if epilogue is None else epilogue(acc, *[e[...] for e in e_refs])
            for ref, val in zip(o_refs, vals):
                ref[...] = val.astype(ref.dtype)

        prod = lax.dot_general(a_ref[...], b_ref[...], dims, preferred_element_type=F32)
        if nk == 1:
            finish(prod)
        else:
            acc_ref = refs[-1]
            step = pl.program_id(2)

            @pl.when(step == 0)
            def _():
                acc_ref[...] = prod

            @pl.when(step > 0)
            def _():
                acc_ref[...] += prod

            @pl.when(step == nk - 1)
            def _():
                finish(acc_ref[...])

    a_spec = pl.BlockSpec((tk, tm), lambda i, j, s: (s, i)) if ta else pl.BlockSpec((tm, tk), lambda i, j, s: (i, s))
    b_spec = pl.BlockSpec((tn, tk), lambda i, j, s: (j, s)) if tb else pl.BlockSpec((tk, tn), lambda i, j, s: (s, j))
    tile_spec = pl.BlockSpec((tm, tn), lambda i, j, s: (i, j))
    return pl.pallas_call(
        body, grid=(m // tm, n // tn, nk),
        in_specs=[a_spec, b_spec] + [tile_spec] * ne,
        out_specs=[tile_spec] * no,
        out_shape=[jax.ShapeDtypeStruct((m, n), dt) for dt in outs],
        scratch_shapes=[pltpu.VMEM((tm, tn), F32)] if nk > 1 else [],
        name=name, compiler_params=_params("parallel", "parallel", "arbitrary"),
    )(a, b, *extras)


def _split3(x):
    x1 = x.astype(BF16)
    r1 = x - x1.astype(F32)
    x2 = r1.astype(BF16)
    x3 = (r1 - x2.astype(F32)).astype(BF16)
    return x1, x2, x3


def _dot_exact(x, e):
    return sum(jnp.dot(p, e, preferred_element_type=F32) for p in _split3(x))


def _head_expand(heads, width):
    dh = width // heads
    rows = jnp.arange(LANES)[:, None]
    cols = jnp.arange(width)[None, :]
    return (cols // dh == rows).astype(BF16)


def _rstd(x):
    return lax.rsqrt(jnp.mean(x * x, axis=-1, keepdims=True) + EPS)


def _rms_fwd(x, gain, *, name):
    return _rowwise(lambda x, g: x * _rstd(x) * g, [x], [gain], [(x.shape[1], BF16)], name=name)[0]


def _rms_bwd_vals(x, dhn, gain):
    r = _rstd(x)
    xh = x * r
    dxh = dhn * gain
    dx = r * (dxh - xh * jnp.mean(dxh * xh, axis=-1, keepdims=True))
    return dx, jnp.sum(dhn * xh, axis=0, keepdims=True)


def _rms_bwd(x, dhn, dres, gain, *, name):
    def fn(x, dhn, dres, g):
        dx, dg = _rms_bwd_vals(x, dhn, g)
        dh = dres + dx
        return dh, dh, dg
    d = x.shape[1]
    return _rowwise(fn, [x, dhn, dres], [gain], [(d, F32), (d, BF16)], [(1, d)], name=name)


def _loss_and_grad(h, target, gain, *, name):
    def fn(h, tgt, g):
        r = _rstd(h)
        err = h * r * g - tgt
        loss = 0.5 * jnp.sum(jnp.mean(err * err, axis=-1, keepdims=True), axis=0, keepdims=True)
        dx, dg = _rms_bwd_vals(h, err * (1.0 / D_MODEL), g)
        return dx, dx, jnp.broadcast_to(loss, (1, LANES)), dg
    d = h.shape[1]
    return _rowwise(fn, [h, target], [gain], [(d, F32), (d, BF16)], [(1, LANES), (1, d)], name=name)


def _adamw(w, g, m, v, *, name):
    rows, cols = w.shape
    tile = rows
    for cand in (256, 128, 64, 32, 16, 8):
        if rows % cand == 0 and rows > cand:
            tile = cand
            break
    bc1 = 1.0 - ADAM_B1 ** ADAM_STEP
    bc2 = 1.0 - ADAM_B2 ** ADAM_STEP

    def fn(w, g, m, v):
        m2 = ADAM_B1 * m + (1.0 - ADAM_B1) * g
        v2 = ADAM_B2 * v + (1.0 - ADAM_B2) * (g * g)
        delta = -ADAM_LR * ((m2 / bc1) / (jnp.sqrt(v2 / bc2) + ADAM_EPS) + ADAM_WD * w)
        return delta, m2, v2
    return _rowwise(fn, [w, g, m, v], [], [(cols, F32)] * 3, name=name, tile=tile)


def _attn_specs(arr, width, cb, classes, rows_block, whole):
    ncols = arr.shape[2] // (classes * width)
    if whole:
        return pl.BlockSpec((1, arr.shape[1], width), lambda n, r, i: (n, 0, r * ncols + cb))
    return pl.BlockSpec((1, rows_block, width), lambda n, r, i: (n, i, r * ncols + cb))


def _attn_tiles(mode, lq, lk):
    if mode == "full":
        return min(lq, 256), lk
    if mode == "band":
        tq = min(BAND_ROWS, lq)
        return tq, min(tq + ATT_BLOCK, lk)
    return CAUSAL_ROWS, ATT_CHUNK


def _window(mode, i, j, tq, win, lk):
    if mode == "causal":
        return pl.multiple_of(j * win, win), (i * tq) // win + 1
    if mode == "band":
        return pl.multiple_of(jnp.clip(i * tq + tq - win, 0, lk - win), ATT_BLOCK), 1
    return 0, 1


def _allowed(mode, i, start, tq, win, max_dist):
    if mode == "full":
        return None
    dist = (i * tq + lax.broadcasted_iota(jnp.int32, (tq, win), 0)) - (start + lax.broadcasted_iota(jnp.int32, (tq, win), 1))
    ok = dist >= 0
    if max_dist is not None:
        ok = ok & (dist <= max_dist)
    return ok


def _head_groups(heads, dh):
    gw = max(LANES, dh)
    return gw, gw // dh, heads // (gw // dh)


def _own_lanes(rows, gw, dh, u):
    return lax.broadcasted_iota(jnp.int32, (rows, gw), 1) // dh == u


def _only_head(x, own, per, u):
    return x if per == 1 else jnp.where(own[u], x, jnp.zeros_like(x))


def _step_scores(qz, kw, kb_row, ok, scale):
    s = lax.dot_general(qz, kw, (((1,), (1,)), ((), ())), preferred_element_type=F32) * scale
    if kb_row is not None:
        s = s - kb_row
    if ok is not None:
        s = jnp.where(ok, s, MASKED)
    return s


def _attn_fwd(q, k, v, kb=None, *, classes=1, heads, dh, mode, max_dist=None, name):
    (qa, qc), (ka, kc), (va, vc) = q, k, v
    n, lq, lk = qa.shape[0], qa.shape[1], ka.shape[1]
    width = heads * dh
    tq, win = _attn_tiles(mode, lq, lk)
    gw, per, groups = _head_groups(heads, dh)
    scale = dh ** -0.5
    has_kb = kb is not None
    single = mode != "causal"

    def body(*refs):
        q_ref, k_ref, v_ref = refs[:3]
        kb_ref = refs[3] if has_kb else None
        o_ref, lse_ref = refs[-2:]
        i = pl.program_id(2)
        lane = lax.broadcasted_iota(jnp.int32, (tq, LANES), 1)
        own = [_own_lanes(tq, gw, dh, u) for u in range(per)]

        def step(j, carry):
            m_all, l_all = carry
            start, _ = _window(mode, i, j, tq, win, lk)
            ok = _allowed(mode, i, start, tq, win, max_dist)
            for g in range(groups):
                cols = slice(g * gw, (g + 1) * gw)
                qg = q_ref[0, :, cols]
                kw = k_ref[0, pl.ds(start, win), cols]
                vw = v_ref[0, pl.ds(start, win), cols]
                alpha_g = new_g = None
                for u in range(per):
                    h = g * per + u
                    kb_row = kb_ref[0, h, pl.ds(j, 1), :] if has_kb else None
                    s = _step_scores(_only_head(qg, own, per, u), kw, kb_row, ok, scale)
                    m_old, l_old = m_all[:, h:h + 1], l_all[:, h:h + 1]
                    m_new = jnp.maximum(m_old, jnp.max(s, axis=1, keepdims=True))
                    alpha = jnp.exp(m_old - m_new)
                    p = jnp.exp(s - m_new)
                    l_new = alpha * l_old + jnp.sum(p, axis=1, keepdims=True)
                    r = jnp.dot(p.astype(BF16), vw, preferred_element_type=F32)
                    alpha_g = alpha if u == 0 else jnp.where(own[u], alpha, alpha_g)
                    new_g = r if u == 0 else jnp.where(own[u], r, new_g)
                    m_all = jnp.where(lane == h, m_new, m_all)
                    l_all = jnp.where(lane == h, l_new, l_all)
                o_ref[0, :, cols] = new_g if single else alpha_g * o_ref[0, :, cols] + new_g
            return m_all, l_all

        carry = (jnp.full((tq, LANES), MASKED, F32), jnp.zeros((tq, LANES), F32))
        if single:
            m_all, l_all = step(0, carry)
        else:
            o_ref[...] = jnp.zeros(o_ref.shape, F32)
            m_all, l_all = lax.fori_loop(0, _window(mode, i, 0, tq, win, lk)[1], step, carry)
        l_all = jnp.where(lane < heads, l_all, 1.0)
        inv = 1.0 / l_all
        for g in range(groups):
            cols = slice(g * gw, (g + 1) * gw)
            inv_g = inv[:, g * per:g * per + 1]
            for u in range(1, per):
                inv_g = jnp.where(own[u], inv[:, g * per + u:g * per + u + 1], inv_g)
            o_ref[0, :, cols] = o_ref[0, :, cols] * inv_g
        lse_ref[0] = jnp.where(lane < heads, m_all + jnp.log(l_all), 0.0)

    in_specs = [_attn_specs(qa, width, qc, classes, tq, False), _attn_specs(ka, width, kc, classes, win, True),
                _attn_specs(va, width, vc, classes, win, True)]
    args = [qa, ka, va]
    if has_kb:
        in_specs.append(pl.BlockSpec((1,) + kb.shape[1:], lambda n_, r, i: (n_, 0, 0, 0)))
        args.append(kb)
    out_shape = [jax.ShapeDtypeStruct((n, lq, classes * width), F32), jax.ShapeDtypeStruct((n, lq, classes * LANES), F32)]
    out_specs = [pl.BlockSpec((1, tq, width), lambda n_, r, i: (n_, i, r)), pl.BlockSpec((1, tq, LANES), lambda n_, r, i: (n_, i, r))]
    return pl.pallas_call(
        body, grid=(n, classes, lq // tq), in_specs=in_specs, out_specs=out_specs, out_shape=out_shape, name=name,
        compiler_params=_params("parallel", "parallel", "arbitrary"),
    )(*args)


def _attn_bwd(q, k, v, do, lse, delta, kb=None, *, classes=1, heads, dh, mode, max_dist=None, name):
    (qa, qc), (ka, kc), (va, vc), (da, dc) = q, k, v, do
    n, lq, lk = qa.shape[0], qa.shape[1], ka.shape[1]
    width = heads * dh
    tq, win = _attn_tiles(mode, lq, lk)
    gw, per, groups = _head_groups(heads, dh)
    scale = dh ** -0.5
    has_kb = kb is not None
    single = mode != "causal"
    nt = (((1,), (1,)), ((), ()))
    tn = (((0,), (0,)), ((), ()))

    def body(*refs):
        q_ref, k_ref, v_ref, do_ref, lse_ref, dl_ref = refs[:6]
        kb_ref = refs[6] if has_kb else None
        n_in = 7 if has_kb else 6
        dq_ref, dk_ref, dv_ref = refs[n_in:n_in + 3]
        dkb_ref, drow_ref = refs[n_in + 3:n_in + 5] if has_kb else (None, None)
        i = pl.program_id(2)

        @pl.when(i == 0)
        def _():
            dk_ref[...] = jnp.zeros(dk_ref.shape, F32)
            dv_ref[...] = jnp.zeros(dv_ref.shape, F32)
            if has_kb:
                dkb_ref[...] = jnp.zeros(dkb_ref.shape, F32)

        lse_all = lse_ref[0]
        dl_all = dl_ref[0]
        lane = lax.broadcasted_iota(jnp.int32, (tq, LANES), 1)
        own = [_own_lanes(tq, gw, dh, u) for u in range(per)]

        def step(j, drow_all):
            start, _ = _window(mode, i, j, tq, win, lk)
            ok = _allowed(mode, i, start, tq, win, max_dist)
            for g in range(groups):
                cols = slice(g * gw, (g + 1) * gw)
                qg = q_ref[0, :, cols]
                dog = do_ref[0, :, cols]
                kw = k_ref[0, pl.ds(start, win), cols]
                vw = v_ref[0, pl.ds(start, win), cols]
                dq_g = dk_w = dv_w = None
                for u in range(per):
                    h = g * per + u
                    qz, doz = _only_head(qg, own, per, u), _only_head(dog, own, per, u)
                    kb_row = kb_ref[0, h, pl.ds(j, 1), :] if has_kb else None
                    p = jnp.exp(_step_scores(qz, kw, kb_row, ok, scale) - lse_all[:, h:h + 1])
                    dp = lax.dot_general(doz, vw, nt, preferred_element_type=F32)
                    ds = p * (dp - dl_all[:, h:h + 1])
                    dsb = ds.astype(BF16)
                    r = jnp.dot(dsb, kw, preferred_element_type=F32)
                    dq_g = r if u == 0 else jnp.where(own[u], r, dq_g)
                    dk_u = lax.dot_general(dsb, qz, tn, preferred_element_type=F32)
                    dv_u = lax.dot_general(p.astype(BF16), doz, tn, preferred_element_type=F32)
                    dk_w = dk_u if u == 0 else dk_w + dk_u
                    dv_w = dv_u if u == 0 else dv_w + dv_u
                    if has_kb:
                        dkb_ref[0, h, pl.ds(j, 1), :] += -jnp.sum(ds, axis=0, keepdims=True)
                        drow_all = drow_all + jnp.where(lane == h, jnp.sum(ds, axis=1, keepdims=True), 0.0)
                dk_ref[0, pl.ds(start, win), cols] += dk_w * scale
                dv_ref[0, pl.ds(start, win), cols] += dv_w
                if single:
                    dq_ref[0, :, cols] = dq_g * scale
                else:
                    dq_ref[0, :, cols] += dq_g * scale
            return drow_all

        drow_all = jnp.zeros((tq, LANES), F32)
        if single:
            drow_all = step(0, drow_all)
        else:
            dq_ref[...] = jnp.zeros(dq_ref.shape, F32)
            drow_all = lax.fori_loop(0, _window(mode, i, 0, tq, win, lk)[1], step, drow_all)
        if has_kb:
            drow_ref[0] = drow_all

    row_spec = functools.partial(_attn_specs, classes=classes, rows_block=tq, whole=False)
    in_specs = [row_spec(qa, width, qc), _attn_specs(ka, width, kc, classes, win, True), _attn_specs(va, width, vc, classes, win, True),
                row_spec(da, width, dc), row_spec(lse, LANES, 0), row_spec(delta, LANES, 0)]
    args = [qa, ka, va, da, lse, delta]
    out_shape = [jax.ShapeDtypeStruct((n, lq, classes * width), F32), jax.ShapeDtypeStruct((n, lk, classes * width), F32),
                 jax.ShapeDtypeStruct((n, lk, classes * width), F32)]
    kv_spec = pl.BlockSpec((1, lk, width), lambda n_, r, i: (n_, 0, r))
    out_specs = [pl.BlockSpec((1, tq, width), lambda n_, r, i: (n_, i, r)), kv_spec, kv_spec]
    if has_kb:
        kb_spec = pl.BlockSpec((1,) + kb.shape[1:], lambda n_, r, i: (n_, 0, 0, 0))
        in_specs.append(kb_spec)
        args.append(kb)
        out_shape += [jax.ShapeDtypeStruct(kb.shape, F32), jax.ShapeDtypeStruct((n, lq, LANES), F32)]
        out_specs += [kb_spec, pl.BlockSpec((1, tq, LANES), lambda n_, r, i: (n_, i, 0))]
    return pl.pallas_call(
        body, grid=(n, classes, lq // tq), in_specs=in_specs, out_specs=out_specs, out_shape=out_shape, name=name,
        compiler_params=_params("parallel", "parallel", "arbitrary"),
    )(*args)


def _head_delta(do, out, heads, *, name, lse=None, sink=None):
    width = out.shape[1]
    gather = _head_expand(heads, width).T

    if sink is None:
        return _rowwise(lambda do, o, e: _dot_exact(do * o, e), [do, out], [gather], [(LANES, F32)], name=name)[0]

    def fn(do, o, lse, e, sink):
        delta = _dot_exact(do * o, e)
        return delta, -jnp.sum(jnp.exp(sink - lse) * delta, axis=0, keepdims=True)
    return _rowwise(fn, [do, out, lse], [gather, sink], [(LANES, F32)], [(1, LANES)], name=name)


def _rope_tables():
    half = HEAD_DIM // 2
    inv = ROPE_THETA ** (-jnp.arange(half, dtype=F32) / half)
    ang = jnp.arange(SEQ, dtype=F32)[:, None] * inv[None, :]
    cos = jnp.tile(jnp.cos(ang), (1, 2 * ATT_W // HEAD_DIM))
    sin = jnp.tile(jnp.concatenate([-jnp.sin(ang), jnp.sin(ang)], axis=1), (1, ATT_W // HEAD_DIM))
    return cos, sin


def _rotate(x, cos, sin):
    width = x.shape[1]
    lane = lax.broadcasted_iota(jnp.int32, x.shape, 1)
    first_half = (lane % HEAD_DIM) < (HEAD_DIM // 2)
    swapped = jnp.where(first_half, pltpu.roll(x, width - HEAD_DIM // 2, axis=1), pltpu.roll(x, HEAD_DIM // 2, axis=1))
    return x * cos[:, :width] + swapped * sin[:, :width]


def _gelu(x):
    k = math.sqrt(2.0 / math.pi)
    t = jnp.tanh(k * (x + 0.044715 * x * x * x))
    return 0.5 * x * (1.0 + t), t


def _gelu_grad(x, t):
    k = math.sqrt(2.0 / math.pi)
    return 0.5 * (1.0 + t) + 0.5 * x * (1.0 - t * t) * k * (1.0 + 3.0 * 0.044715 * x * x)


def _shift_down(x, halo, s):
    ext = jnp.concatenate([halo, x], axis=0)
    return pltpu.roll(ext, s, axis=0)[8:, :]


def _shift_up(x, halo, s):
    rows = x.shape[0]
    ext = jnp.concatenate([x, halo], axis=0)
    return pltpu.roll(ext, rows + 8 - s, axis=0)[:rows, :]


def _lru_gates(u, halo, cw, cb, wa, ba, wi, bi, lam):
    taps = [_shift_down(u, halo, CONV_WIDTH - 1 - k) for k in range(CONV_WIDTH - 1)] + [u]
    uc = cb + sum(cw[k:k + 1, :] * taps[k] for k in range(CONV_WIDTH))
    ucb = uc.astype(BF16)
    r = jax.nn.sigmoid(jnp.dot(ucb, wa, preferred_element_type=F32) + ba)
    gi = jax.nn.sigmoid(jnp.dot(ucb, wi, preferred_element_type=F32) + bi)
    sp = jnp.maximum(-lam, 0.0) + jnp.log(1.0 + jnp.exp(-jnp.abs(lam)))
    log_a = -LRU_C * r * sp
    a = jnp.exp(log_a)
    x2 = 2.0 * log_a
    one_minus_a2 = jnp.where(x2 > -0.01, -x2 * (1.0 + x2 * (0.5 + x2 * (1.0 / 6.0 + x2 / 24.0))), 1.0 - jnp.exp(x2))
    mult = jnp.sqrt(one_minus_a2)
    return taps, uc, ucb, r, gi, sp, a, mult


def _lru_specs(nchunk, reverse):
    def chunk(b, c):
        return b * nchunk + ((nchunk - 1 - c) if reverse else c)

    def halo_before(b, c):
        return jnp.maximum(chunk(b, c) * (LRU_CHUNK // 8) - 1, 0)

    return chunk, halo_before


def _lru_fwd(zug, cw, cb, wa, ba, wi, bi, lam, *, name):
    total = zug.shape[0]
    nchunk = SEQ // LRU_CHUNK
    w = LRU_WIDTH
    chunk, halo_before = _lru_specs(nchunk, False)

    def body(u_ref, uh_ref, g_ref, cw_ref, cb_ref, wa_ref, ba_ref, wi_ref, bi_ref, lam_ref, y_ref, h_ref, a_sc, x_sc, carry_sc):
        c = pl.program_id(1)
        u = u_ref[...]
        halo = jnp.where(c > 0, uh_ref[...], 0.0)
        _, uc, _, _, gi, _, a, mult = _lru_gates(u, halo, cw_ref[...], cb_ref[...], wa_ref[...], ba_ref[...],
                                                 wi_ref[...], bi_ref[...], lam_ref[...])
        a_sc[...] = a
        x_sc[...] = mult * (gi * uc)

        @pl.when(c == 0)
        def _():
            carry_sc[...] = jnp.zeros(carry_sc.shape, F32)

        def tile_step(t, h):
            r0 = pl.multiple_of(t * 8, 8)
            at = a_sc[pl.ds(r0, 8), :]
            xt = x_sc[pl.ds(r0, 8), :]
            rows = []
            for j in range(8):
                h = at[j:j + 1, :] * h + xt[j:j + 1, :]
                rows.append(h)
            h_ref[pl.ds(r0, 8), :] = jnp.concatenate(rows, axis=0)
            return h

        h_last = lax.fori_loop(0, LRU_CHUNK // 8, tile_step, carry_sc[0:1, :])
        carry_sc[0:1, :] = h_last
        gel, _ = _gelu(g_ref[...])
        y_ref[...] = (h_ref[...] * gel).astype(BF16)

    small = lambda arr: pl.BlockSpec(arr.shape, lambda b, c: (0, 0))
    return pl.pallas_call(
        body, grid=(total // SEQ, nchunk),
        in_specs=[pl.BlockSpec((LRU_CHUNK, w), lambda b, c: (chunk(b, c), 0)),
                  pl.BlockSpec((8, w), lambda b, c: (halo_before(b, c), 0)),
                  pl.BlockSpec((LRU_CHUNK, w), lambda b, c: (chunk(b, c), 1)),
                  small(cw), small(cb), small(wa), small(ba), small(wi), small(bi), small(lam)],
        out_specs=[pl.BlockSpec((LRU_CHUNK, w), lambda b, c: (chunk(b, c), 0))] * 2,
        out_shape=[jax.ShapeDtypeStruct((total, w), BF16), jax.ShapeDtypeStruct((total, w), F32)],
        scratch_shapes=[pltpu.VMEM((LRU_CHUNK, w), F32), pltpu.VMEM((LRU_CHUNK, w), F32), pltpu.VMEM((8, w), F32)],
        name=name, compiler_params=_params("arbitrary", "arbitrary"),
    )(zug, zug, zug, cw, cb, wa, ba, wi, bi, lam)


def _lru_bwd(zug, hl, dy, cw, cb, wa, ba, wi, bi, lam, *, name):
    total = zug.shape[0]
    nchunk = SEQ // LRU_CHUNK
    w = LRU_WIDTH
    chunk, halo_before = _lru_specs(nchunk, True)
    tn = (((0,), (0,)), ((), ()))
    nt = (((1,), (1,)), ((), ()))

    def body(u_ref, uh_ref, g_ref, hl_ref, hh_ref, dy_ref, cw_ref, cb_ref, wa_ref, ba_ref, wi_ref, bi_ref, lam_ref,
             dz_ref, dcw_ref, dcb_ref, dwa_ref, dba_ref, dwi_ref, dbi_ref, dlam_ref,
             a_sc, d_sc, g_sc, gcarry_sc, acarry_sc, duc_sc):
        b, c = pl.program_id(0), pl.program_id(1)
        first_chunk = c == nchunk - 1

        @pl.when((b == 0) & (c == 0))
        def _():
            for ref in (dcw_ref, dcb_ref, dwa_ref, dba_ref, dwi_ref, dbi_ref, dlam_ref):
                ref[...] = jnp.zeros(ref.shape, F32)

        @pl.when(c == 0)
        def _():
            gcarry_sc[...] = jnp.zeros(gcarry_sc.shape, F32)
            acarry_sc[...] = jnp.zeros(acarry_sc.shape, F32)
            duc_sc[...] = jnp.zeros(duc_sc.shape, F32)

        u = u_ref[...]
        halo = jnp.where(first_chunk, 0.0, uh_ref[...])
        cw, wa, wi, lam = cw_ref[...], wa_ref[...], wi_ref[...], lam_ref[...]
        taps, uc, ucb, r, gi, sp, a, mult = _lru_gates(u, halo, cw, cb_ref[...], wa, ba_ref[...], wi, bi_ref[...], lam)
        gate = g_ref[...]
        gel, th = _gelu(gate)
        dy = dy_ref[...]
        hl = hl_ref[...]
        a_sc[...] = a
        d_sc[...] = dy * gel
        dgate = dy * hl * _gelu_grad(gate, th)

        def tile_step(t, carry):
            g_next, a_next = carry
            r0 = pl.multiple_of((LRU_CHUNK // 8 - 1 - t) * 8, 8)
            dt = d_sc[pl.ds(r0, 8), :]
            at = a_sc[pl.ds(r0, 8), :]
            rows = [None] * 8
            for j in reversed(range(8)):
                g_next = dt[j:j + 1, :] + a_next * g_next
                a_next = at[j:j + 1, :]
                rows[j] = g_next
            g_sc[pl.ds(r0, 8), :] = jnp.concatenate(rows, axis=0)
            return g_next, a_next

        g_last, a_last = lax.fori_loop(0, LRU_CHUNK // 8, tile_step, (gcarry_sc[0:1, :], acarry_sc[0:1, :]))
        gcarry_sc[0:1, :] = g_last
        acarry_sc[0:1, :] = a_last

        gs = g_sc[...]
        hl_halo = jnp.where(first_chunk, 0.0, hh_ref[...])
        da = gs * _shift_down(hl, hl_halo, 1)
        dmult = gs * gi * uc
        dgi = gs * mult * uc
        duc = gs * mult * gi
        dlog_a = da * a - dmult * a * a / mult
        dr = dlog_a * (-LRU_C * sp)
        dsp = jnp.sum(dlog_a * (-LRU_C * r), axis=0, keepdims=True)
        dlam_ref[...] += -dsp * jax.nn.sigmoid(-lam)
        dpa = dr * r * (1.0 - r)
        dpi = dgi * gi * (1.0 - gi)
        dpab, dpib = dpa.astype(BF16), dpi.astype(BF16)
        dba_ref[...] += jnp.sum(dpa, axis=0, keepdims=True)
        dbi_ref[...] += jnp.sum(dpi, axis=0, keepdims=True)
        dwa_ref[...] += lax.dot_general(ucb, dpab, tn, preferred_element_type=F32)
        dwi_ref[...] += lax.dot_general(ucb, dpib, tn, preferred_element_type=F32)
        duc = duc + lax.dot_general(dpab, wa, nt, preferred_element_type=F32)
        duc = duc + lax.dot_general(dpib, wi, nt, preferred_element_type=F32)
        dcb_ref[...] += jnp.sum(duc, axis=0, keepdims=True)
        dcw_ref[...] += jnp.concatenate([jnp.sum(duc * taps[k], axis=0, keepdims=True) for k in range(CONV_WIDTH)], axis=0)
        after = duc_sc[...]
        du = cw[CONV_WIDTH - 1:CONV_WIDTH, :] * duc
        for k in range(CONV_WIDTH - 1):
            du = du + cw[k:k + 1, :] * _shift_up(duc, after, CONV_WIDTH - 1 - k)
        duc_sc[...] = duc[0:8, :]
        dz_ref[:, 0:w] = du.astype(BF16)
        dz_ref[:, w:2 * w] = dgate.astype(BF16)

    small = lambda arr: pl.BlockSpec(arr.shape, lambda b, c: (0, 0))
    acc = lambda shape: pl.BlockSpec(shape, lambda b, c: (0, 0))
    chunk_spec = lambda cb_: pl.BlockSpec((LRU_CHUNK, w), lambda b, c: (chunk(b, c), cb_))
    halo_spec = pl.BlockSpec((8, w), lambda b, c: (halo_before(b, c), 0))
    acc_shapes = [(CONV_WIDTH, w), (1, w), (w, w), (1, w), (w, w), (1, w), (1, w)]
    return pl.pallas_call(
        body, grid=(total // SEQ, nchunk),
        in_specs=[chunk_spec(0), halo_spec, chunk_spec(1), chunk_spec(0), halo_spec, chunk_spec(0),
                  small(cw), small(cb), small(wa), small(ba), small(wi), small(bi), small(lam)],
        out_specs=[pl.BlockSpec((LRU_CHUNK, 2 * w), lambda b, c: (chunk(b, c), 0))] + [acc(s) for s in acc_shapes],
        out_shape=[jax.ShapeDtypeStruct((total, 2 * w), BF16)] + [jax.ShapeDtypeStruct(s, F32) for s in acc_shapes],
        scratch_shapes=[pltpu.VMEM((LRU_CHUNK, w), F32)] * 3 + [pltpu.VMEM((8, w), F32)] * 3,
        name=name, compiler_params=_params("arbitrary", "arbitrary"),
    )(zug, zug, zug, hl, hl, dy, cw, cb, wa, ba, wi, bi, lam)


def _block_diag(wb):
    eye = jnp.eye(LRU_BLOCKS, dtype=wb.dtype)
    return jnp.einsum("gcd,gh->gchd", wb, eye).reshape(LRU_WIDTH, LRU_WIDTH).astype(BF16)


def _diag_blocks(wd):
    blk = LRU_WIDTH // LRU_BLOCKS
    return jnp.stack([wd[g * blk:(g + 1) * blk, g * blk:(g + 1) * blk] for g in range(LRU_BLOCKS)])


FOX_SCAN = 256


def _fox_bias(fl, bf, *, name):
    nb = fl.shape[0]

    def body(fl_ref, bf_ref, c_ref):
        x = fl_ref[0] + bf_ref[...]
        logf = jnp.minimum(x, 0.0) - jnp.log(1.0 + jnp.exp(-jnp.abs(x)))
        upper = (lax.broadcasted_iota(jnp.int32, (FOX_SCAN, FOX_SCAN), 0)
                 <= lax.broadcasted_iota(jnp.int32, (FOX_SCAN, FOX_SCAN), 1)).astype(BF16)
        carry = jnp.zeros((LRU_BLOCKS, 1), F32)
        for j in range(SEQ // FOX_SCAN):
            blk = logf[:, j * FOX_SCAN:(j + 1) * FOX_SCAN]
            c_ref[0, :, j * FOX_SCAN:(j + 1) * FOX_SCAN] = _dot_exact(blk, upper) + carry
            carry = carry + jnp.sum(blk, axis=1, keepdims=True)

    return pl.pallas_call(
        body, grid=(nb,),
        in_specs=[pl.BlockSpec((1, 8, SEQ), lambda b: (b, 0, 0)), pl.BlockSpec((8, 1), lambda b: (0, 0))],
        out_specs=pl.BlockSpec((1, 8, SEQ), lambda b: (b, 0, 0)),
        out_shape=jax.ShapeDtypeStruct((nb, 8, SEQ), F32), name=name, compiler_params=_params("arbitrary"),
    )(fl, bf)


def _fox_bias_bwd(fl, bf, dc, *, name):
    nb = fl.shape[0]

    def body(fl_ref, bf_ref, dc_ref, dfl_ref, dbf_ref):
        @pl.when(pl.program_id(0) == 0)
        def _():
            dbf_ref[...] = jnp.zeros(dbf_ref.shape, F32)

        x = fl_ref[0] + bf_ref[...]
        dc_all = dc_ref[0]
        lower = (lax.broadcasted_iota(jnp.int32, (FOX_SCAN, FOX_SCAN), 0)
                 >= lax.broadcasted_iota(jnp.int32, (FOX_SCAN, FOX_SCAN), 1)).astype(BF16)
        carry = jnp.zeros((LRU_BLOCKS, 1), F32)
        total = jnp.zeros((LRU_BLOCKS, 1), F32)
        for j in reversed(range(SEQ // FOX_SCAN)):
            cols = slice(j * FOX_SCAN, (j + 1) * FOX_SCAN)
            blk = dc_all[:, cols]
            dlogf = _dot_exact(blk, lower) + carry
            carry = carry + jnp.sum(blk, axis=1, keepdims=True)
            dx = dlogf * jax.nn.sigmoid(-x[:, cols])
            dfl_ref[0, :, cols] = dx
            total = total + jnp.sum(dx, axis=1, keepdims=True)
        dbf_ref[...] += total

    return pl.pallas_call(
        body, grid=(nb,),
        in_specs=[pl.BlockSpec((1, 8, SEQ), lambda b: (b, 0, 0)), pl.BlockSpec((8, 1), lambda b: (0, 0)),
                  pl.BlockSpec((1, 8, SEQ), lambda b: (b, 0, 0))],
        out_specs=[pl.BlockSpec((1, 8, SEQ), lambda b: (b, 0, 0)), pl.BlockSpec((8, 1), lambda b: (0, 0))],
        out_shape=[jax.ShapeDtypeStruct((nb, 8, SEQ), F32), jax.ShapeDtypeStruct((8, 1), F32)],
        name=name, compiler_params=_params("arbitrary"),
    )(fl, bf, dc)


def _seq3(a, nb):
    return a.reshape(nb, a.shape[0] // nb, a.shape[1])


def _flat2(a):
    return a.reshape(a.shape[0] * a.shape[1], a.shape[2])


def _mlp_fwd(h, p, tag):
    hn = _rms_fwd(h, p["norm"], name=f"{tag}_norm")
    act, = _matmul(hn, p["w_up_t"], tb=True, outs=(BF16,), epilogue=lambda acc: (jnp.square(jnp.maximum(acc, 0.0)),),
                   name=f"{tag}_up")
    out, = _matmul(act, p["w_down"], extras=(h,), epilogue=lambda acc, res: (acc + res,), name=f"{tag}_down")
    return out, (h, hn, act)


def _mlp_bwd(dh, dhb, p, saved, tag):
    h, hn, act = saved
    dup, = _matmul(dhb, p["w_down"], tb=True, outs=(BF16,), extras=(act,),
                   epilogue=lambda acc, act: (acc * (2.0 * jnp.sqrt(act.astype(F32))),), name=f"{tag}_bwd_dup")
    dw_down, = _matmul(act, dhb, ta=True, name=f"{tag}_bwd_dwdown")
    dw_up_t, = _matmul(dup, hn, ta=True, name=f"{tag}_bwd_dwup")
    dhn, = _matmul(dup, p["w_up_t"], name=f"{tag}_bwd_dhn")
    dh2, dh2b, dg = _rms_bwd(h, dhn, dh, p["norm"], name=f"{tag}_bwd_norm")
    return dh2, dh2b, {"norm": dg, "w_up_t": dw_up_t, "w_down": dw_down}


def _xa_fwd(h, mem, p, tag, nb):
    hn = _rms_fwd(h, p["norm"], name=f"{tag}_norm")
    mem_n = _rms_fwd(mem, p["mem_norm"], name=f"{tag}_memnorm")
    q, = _matmul(hn, p["w_q"], outs=(BF16,), name=f"{tag}_q")
    kv, = _matmul(mem_n, p["w_kv_t"], tb=True, outs=(BF16,), name=f"{tag}_kv")
    q3, kv3 = _seq3(q, nb), _seq3(kv, nb)
    o, lse = _attn_fwd((q3, 0), (kv3, 0), (kv3, 1), heads=XA_HEADS, dh=XA_HEAD_DIM, mode="full",
                       name=f"{tag}_attn")
    o = _flat2(o)
    ob, = _rowwise(lambda o: o, [o], [], [(D_MODEL, BF16)], name=f"{tag}_ocast")
    out, = _matmul(ob, p["w_o"], extras=(h,), epilogue=lambda acc, res: (acc + res,), name=f"{tag}_o")
    return out, (h, hn, mem_n, q3, kv3, o, ob, lse)


def _xa_bwd(dh, dhb, mem, p, saved, tag, nb):
    h, hn, mem_n, q3, kv3, o, ob, lse = saved
    do, dob = _matmul(dhb, p["w_o"], tb=True, outs=(F32, BF16), epilogue=lambda acc: (acc, acc), name=f"{tag}_bwd_do")
    dw_o, = _matmul(ob, dhb, ta=True, name=f"{tag}_bwd_dwo")
    delta = _head_delta(do, o, XA_HEADS, name=f"{tag}_bwd_delta")
    dq, dk, dv = _attn_bwd((q3, 0), (kv3, 0), (kv3, 1), (_seq3(dob, nb), 0), lse, _seq3(delta, nb), heads=XA_HEADS,
                           dh=XA_HEAD_DIM, mode="full", name=f"{tag}_bwd_attn")
    dqb, = _rowwise(lambda x: x, [_flat2(dq)], [], [(D_MODEL, BF16)], name=f"{tag}_bwd_dqcast")
    dkvb, = _rowwise(lambda a, b: jnp.concatenate([a, b], axis=1), [_flat2(dk), _flat2(dv)], [], [(2 * D_MODEL, BF16)],
                     name=f"{tag}_bwd_dkvcast")
    dw_q, = _matmul(hn, dqb, ta=True, name=f"{tag}_bwd_dwq")
    dw_kv_t, = _matmul(dkvb, mem_n, ta=True, name=f"{tag}_bwd_dwkv")
    dhn, = _matmul(dqb, p["w_q"], tb=True, name=f"{tag}_bwd_dhn")
    dmem_n, = _matmul(dkvb, p["w_kv_t"], name=f"{tag}_bwd_dmemn")
    dg_mem, = _rowwise(lambda x, d, g: _rms_bwd_vals(x, d, g)[1], [mem, dmem_n], [p["mem_norm"]], [], [(1, D_MODEL)],
                       name=f"{tag}_bwd_memnorm")
    dh2, dh2b, dg = _rms_bwd(h, dhn, dh, p["norm"], name=f"{tag}_bwd_norm")
    return dh2, dh2b, {"norm": dg, "mem_norm": dg_mem, "w_q": dw_q, "w_kv_t": dw_kv_t, "w_o": dw_o}


AB_MAIN = 2 * LRU_WIDTH + 3 * ATT_W


def _ab_fwd(h, p, nb):
    hn = _rms_fwd(h, p["norm"], name="ab_norm")
    w_in_t = p["w_in_t"]
    zug, = _matmul(hn, w_in_t[:2 * LRU_WIDTH], tb=True, name="ab_in_ug")
    qkv, = _matmul(hn, w_in_t[2 * LRU_WIDTH:AB_MAIN], tb=True, outs=(BF16,), name="ab_in_qkv")
    zf, = _matmul(hn, w_in_t[AB_MAIN:], tb=True, name="ab_in_f")
    fl = zf[:, :8].reshape(nb, SEQ, 8).transpose(0, 2, 1)
    cbias = _fox_bias(fl, p["b_f"], name="ab_fox_bias")
    kb = cbias.reshape(nb, 8, SEQ // ATT_CHUNK, ATT_CHUNK)
    y_a, hl = _lru_fwd(zug, p["conv_w"], p["conv_b"], p["w_a"], p["b_a"], p["w_i"], p["b_i"], p["lam"], name="ab_lru")
    qkv3 = _seq3(qkv, nb)
    o, lse = _attn_fwd((qkv3, 0), (qkv3, 1), (qkv3, 2), kb, heads=8, dh=HEAD_DIM, mode="causal", name="ab_fox")
    o = _flat2(o)
    y, = _rowwise(lambda a, b: jnp.concatenate([a, b.astype(BF16)], axis=1), [y_a, o], [], [(D_MODEL, BF16)], name="ab_ycat")
    out, = _matmul(y, p["w_out"], extras=(h,), epilogue=lambda acc, res: (acc + res,), name="ab_out")
    return out, (h, hn, zug, qkv3, fl, kb, hl, o, lse, y)


def _ab_bwd(dh, dhb, p, saved, nb):
    h, hn, zug, qkv3, fl, kb, hl, o, lse, y = saved
    dy, dyb = _matmul(dhb, p["w_out"], tb=True, outs=(F32, BF16), epilogue=lambda acc: (acc, acc), name="ab_bwd_dy")
    dw_out, = _matmul(y, dhb, ta=True, name="ab_bwd_dwout")
    dzug, dcw, dcb, dwa, dba, dwi, dbi, dlam = _lru_bwd(zug, hl, dy, p["conv_w"], p["conv_b"], p["w_a"], p["b_a"],
                                                        p["w_i"], p["b_i"], p["lam"], name="ab_bwd_lru")
    delta = _head_delta((dy, ATT_W, 1), o, 8, name="ab_bwd_delta")
    dq, dk, dv, dkb, drow = _attn_bwd((qkv3, 0), (qkv3, 1), (qkv3, 2), (_seq3(dyb, nb), 1), lse, _seq3(delta, nb), kb,
                                      heads=8, dh=HEAD_DIM, mode="causal", name="ab_bwd_fox")
    dcum = dkb.reshape(nb, 8, SEQ) + drow[:, :, :8].transpose(0, 2, 1)
    dfl, dbf = _fox_bias_bwd(fl, p["b_f"], dcum, name="ab_bwd_fox_bias")
    dzf = jnp.pad(dfl.transpose(0, 2, 1).reshape(nb * SEQ, 8), ((0, 0), (0, LANES - 8)))
    dz, = _rowwise(lambda a, q, k, v, f: jnp.concatenate([a, q.astype(BF16), k.astype(BF16), v.astype(BF16), f.astype(BF16)], axis=1),
                   [dzug, _flat2(dq), _flat2(dk), _flat2(dv), dzf], [], [(AB_MAIN + LANES, BF16)], name="ab_bwd_dzcat")
    dw_in_t, = _matmul(dz, hn, ta=True, tm=384, name="ab_bwd_dwin")
    dhn, = _matmul(dz, p["w_in_t"], tk=AB_MAIN + LANES, name="ab_bwd_dhn")
    dh2, dh2b, dg = _rms_bwd(h, dhn, dh, p["norm"], name="ab_bwd_norm")
    grads = {"norm": dg, "w_in_t": dw_in_t[:AB_MAIN + 8], "conv_w": dcw, "conv_b": dcb, "w_a": _diag_blocks(dwa), "b_a": dba,
             "w_i": _diag_blocks(dwi), "b_i": dbi, "lam": dlam, "b_f": dbf.reshape(1, 8), "w_out": dw_out}
    return dh2, dh2b, grads


CD_IN = 3 * ATT_W + ATT_W + 2 * SWA_KW


def _class_view(a, dil):
    return a.reshape(a.shape[0], a.shape[1] // dil, dil * a.shape[2])


def _gqa_share():
    src = jnp.arange(SWA_KW)[:, None]
    dst = jnp.arange(ATT_W)[None, :]
    per_kv = ATT_W // (SWA_KW // HEAD_DIM)
    return ((dst // per_kv == src // HEAD_DIM) & (dst % HEAD_DIM == src % HEAD_DIM)).astype(BF16)


def _cd_fwd(h, p, nb):
    hn = _rms_fwd(h, p["norm"], name="cd_norm")
    z, = _matmul(hn, p["w_in_t"], tb=True, tn=384, name="cd_in")
    cos, sin = _rope_tables()
    per_seq = SEQ // ROW_TILE
    table_map = lambda i: (i % per_seq, 0)

    def rope_fn(z, cos, sin, share):
        qc, kc, vc = z[:, 0:ATT_W], z[:, ATT_W:2 * ATT_W], z[:, 2 * ATT_W:3 * ATT_W]
        qd, kd, vd = z[:, 3 * ATT_W:4 * ATT_W], z[:, 4 * ATT_W:4 * ATT_W + SWA_KW], z[:, 4 * ATT_W + SWA_KW:]
        kd8 = jnp.dot(_rotate(kd, cos, sin).astype(BF16), share, preferred_element_type=F32)
        vd8 = jnp.dot(vd.astype(BF16), share, preferred_element_type=F32)
        return (jnp.concatenate([_rotate(qc, cos, sin), _rotate(kc, cos, sin)], axis=1), vc, _rotate(qd, cos, sin), kd8, vd8)
    qk, vc, qd, kd, vd = _rowwise(rope_fn, [z, cos, sin], [_gqa_share()], [(2 * ATT_W, BF16)] + [(ATT_W, BF16)] * 4,
                                  name="cd_rope", row_maps=[None, table_map, table_map])
    qk3, vc3 = _seq3(qk, nb), _seq3(vc, nb)
    branch = []
    for window, dil in DIL_PATTERN:
        o_i, lse_i = _attn_fwd((_class_view(qk3, dil), 0), (_class_view(qk3, dil), 1), (_class_view(vc3, dil), 0), classes=dil,
                               heads=8, dh=HEAD_DIM, mode="band", max_dist=window // dil, name=f"cd_dil{dil}")
        branch.append((o_i.reshape(nb * SEQ, ATT_W), lse_i.reshape(nb * SEQ, LANES)))
    expand = _head_expand(8, ATT_W)

    def combine(o1, o2, o3, l1, l2, l3, e):
        m = jnp.maximum(jnp.maximum(l1, l2), l3)
        lt = m + jnp.log(jnp.exp(l1 - m) + jnp.exp(l2 - m) + jnp.exp(l3 - m))
        o = sum(_dot_exact(jnp.exp(l - lt), e) * o_ for o_, l in ((o1, l1), (o2, l2), (o3, l3)))
        return o, lt
    y_c, lse_c = _rowwise(combine, [b[0] for b in branch] + [b[1] for b in branch], [expand], [(ATT_W, F32), (LANES, F32)],
                          name="cd_dil_combine")
    qd3, kd3, vd3 = _seq3(qd, nb), _seq3(kd, nb), _seq3(vd, nb)
    o_d, lse_band = _attn_fwd((qd3, 0), (kd3, 0), (vd3, 0), heads=8, dh=HEAD_DIM, mode="band", max_dist=SWA_WINDOW - 1,
                              name="cd_swa")

    def sink_combine(o, l, e, sink):
        lt = jnp.maximum(l, sink) + jnp.log(1.0 + jnp.exp(-jnp.abs(l - sink)))
        return o * _dot_exact(jnp.exp(l - lt), e), lt
    y_d, lse_d = _rowwise(sink_combine, [_flat2(o_d), _flat2(lse_band)], [expand, p["sink"]], [(ATT_W, F32), (LANES, F32)],
                          name="cd_swa_sink")
    y, = _rowwise(lambda a, b: jnp.concatenate([a, b], axis=1), [y_c, y_d], [], [(D_MODEL, BF16)], name="cd_ycat")
    out, = _matmul(y, p["w_out"], extras=(h,), epilogue=lambda acc, res: (acc + res,), name="cd_out")
    return out, (h, hn, qk3, vc3, qd3, kd3, vd3, y_c, lse_c, y_d, lse_d, y)


def _cd_bwd(dh, dhb, p, saved, nb):
    h, hn, qk3, vc3, qd3, kd3, vd3, y_c, lse_c, y_d, lse_d, y = saved
    dy, dyb = _matmul(dhb, p["w_out"], tb=True, outs=(F32, BF16), epilogue=lambda acc: (acc, acc), name="cd_bwd_dy")
    dw_out, = _matmul(y, dhb, ta=True, name="cd_bwd_dwout")
    dyb3 = _seq3(dyb, nb)
    delta_c = _head_delta((dy, ATT_W, 0), y_c, 8, name="cd_bwd_delta_dil")
    lse_c3, delta_c3 = _seq3(lse_c, nb), _seq3(delta_c, nb)
    parts = []
    for window, dil in DIL_PATTERN:
        cv = functools.partial(_class_view, dil=dil)
        dq_i, dk_i, dv_i = _attn_bwd((cv(qk3), 0), (cv(qk3), 1), (cv(vc3), 0), (cv(dyb3), 0), cv(lse_c3), cv(delta_c3),
                                     classes=dil, heads=8, dh=HEAD_DIM, mode="band", max_dist=window // dil,
                                     name=f"cd_bwd_dil{dil}")
        parts.append([a.reshape(nb * SEQ, ATT_W) for a in (dq_i, dk_i, dv_i)])
    delta_d, dsink = _head_delta((dy, ATT_W, 1), y_d, 8, lse=lse_d, sink=p["sink"], name="cd_bwd_delta_swa")
    dqd, dkd, dvd = _attn_bwd((qd3, 0), (kd3, 0), (vd3, 0), (dyb3, 1), _seq3(lse_d, nb), _seq3(delta_d, nb), heads=8,
                              dh=HEAD_DIM, mode="band", max_dist=SWA_WINDOW - 1, name="cd_bwd_swa")
    cos, sin = _rope_tables()
    per_seq = SEQ // ROW_TILE
    table_map = lambda i: (i % per_seq, 0)

    def unrope(q1, q2, q3, k1, k2, k3, v1, v2, v3, qd, kd8, vd8, cos, sin, gather):
        back = lambda x: _rotate(x, cos, -sin)
        kd, vd = _dot_exact(kd8, gather), _dot_exact(vd8, gather)
        return jnp.concatenate([back(q1 + q2 + q3), back(k1 + k2 + k3), v1 + v2 + v3, back(qd), back(kd), vd], axis=1)
    ins = [parts[b][t] for t in range(3) for b in range(3)] + [_flat2(dqd), _flat2(dkd), _flat2(dvd), cos, sin]
    dz, = _rowwise(unrope, ins, [_gqa_share().T], [(CD_IN, BF16)], name="cd_bwd_unrope",
                   row_maps=[None] * 12 + [table_map, table_map])
    dw_in_t, = _matmul(dz, hn, ta=True, tm=384, name="cd_bwd_dwin")
    dhn, = _matmul(dz, p["w_in_t"], tk=CD_IN, name="cd_bwd_dhn")
    dh2, dh2b, dg = _rms_bwd(h, dhn, dh, p["norm"], name="cd_bwd_norm")
    return dh2, dh2b, {"norm": dg, "w_in_t": dw_in_t, "sink": dsink[:, :8], "w_out": dw_out}


def _local_step(x, mem, target, w):
    nb = x.shape[0]
    h = x.reshape(nb * SEQ, D_MODEL)
    mem2 = mem.reshape(nb * MEM_LEN, D_MODEL)
    tgt = target.reshape(nb * SEQ, D_MODEL)

    h, s_ab = _ab_fwd(h, w["ab"], nb)
    h, s_xa0 = _xa_fwd(h, mem2, w["xa0"], "xa0", nb)
    h, s_mlp0 = _mlp_fwd(h, w["mlp0"], "mlp0")
    h, s_cd = _cd_fwd(h, w["cd"], nb)
    h, s_xa1 = _xa_fwd(h, mem2, w["xa1"], "xa1", nb)
    h, s_mlp1 = _mlp_fwd(h, w["mlp1"], "mlp1")

    dh, dhb, loss, dg_final = _loss_and_grad(h, tgt, w["final_norm"], name="loss")
    grads = {"final_norm": dg_final}
    dh, dhb, grads["mlp1"] = _mlp_bwd(dh, dhb, w["mlp1"], s_mlp1, "mlp1")
    dh, dhb, grads["xa1"] = _xa_bwd(dh, dhb, mem2, w["xa1"], s_xa1, "xa1", nb)
    dh, dhb, grads["cd"] = _cd_bwd(dh, dhb, w["cd"], s_cd, nb)
    dh, dhb, grads["mlp0"] = _mlp_bwd(dh, dhb, w["mlp0"], s_mlp0, "mlp0")
    dh, dhb, grads["xa0"] = _xa_bwd(dh, dhb, mem2, w["xa0"], s_xa0, "xa0", nb)
    dh, dhb, grads["ab"] = _ab_bwd(dh, dhb, w["ab"], s_ab, nb)
    return loss, dh.reshape(nb, SEQ, D_MODEL), grads


ANY = pl.BlockSpec(memory_space=pl.ANY)


def _place():
    x, y, c = lax.axis_index("x"), lax.axis_index("y"), lax.axis_index("c")
    return x, y, c, [(1 - x, y), (x, 1 - y), (1 - x, 1 - y)]


def _gather_chips(shard):
    half = shard.shape[0] // 2

    def body(src, out, send_sems, recv_sems):
        x, y, c, chips = _place()
        sibling = (x, y, 1 - c)

        def rows(slot, core):
            return out.at[slot, pl.ds(core * half, half)]

        def copy(k, src_ref, dst_ref, to):
            return pltpu.make_async_remote_copy(src_ref=src_ref, dst_ref=dst_ref, send_sem=send_sems.at[k],
                                                recv_sem=recv_sems.at[k], device_id=to, device_id_type=MESH)

        first = [copy(k, src.at[pl.ds(c * half, half)], rows(2 * x + y, c), (px, py, c)) for k, (px, py) in enumerate(chips)]
        for cp in first:
            cp.start()
        passed = [copy(3 + k, rows(2 * px + py, c), rows(2 * px + py, c), sibling) for k, (px, py) in enumerate(chips)]
        for k, (px, py) in enumerate(chips):
            copy(k, src.at[pl.ds(c * half, half)], rows(2 * px + py, c), (px, py, c)).wait_recv()
            passed[k].start()
        for k, (px, py) in enumerate(chips):
            copy(3 + k, rows(2 * px + py, 1 - c), rows(2 * px + py, 1 - c), sibling).wait_recv()
        for cp in first + passed:
            cp.wait_send()

    return pl.pallas_call(
        body, out_shape=jax.ShapeDtypeStruct((N_CHIPS,) + shard.shape, shard.dtype), in_specs=[ANY], out_specs=ANY,
        scratch_shapes=[pltpu.SemaphoreType.DMA((6,)), pltpu.SemaphoreType.DMA((6,))], name="gather_chips",
    )(shard)


def _swap_cores(block, *, name, half_rows=None):
    shape = block.shape if half_rows is None else (block.shape[0], half_rows, block.shape[2])

    def body(src, out, send_sem, recv_sem):
        x, y, c, _ = _place()
        part = src if half_rows is None else src.at[:, pl.ds((1 - c) * half_rows, half_rows)]
        cp = pltpu.make_async_remote_copy(src_ref=part, dst_ref=out, send_sem=send_sem, recv_sem=recv_sem,
                                          device_id=(x, y, 1 - c), device_id_type=MESH)
        cp.start()
        cp.wait()

    return pl.pallas_call(
        body, out_shape=jax.ShapeDtypeStruct(shape, block.dtype), in_specs=[ANY], out_specs=ANY,
        scratch_shapes=[pltpu.SemaphoreType.DMA(()), pltpu.SemaphoreType.DMA(())], name=name,
    )(block)


def _scatter_chips(parts):
    def body(src, out, send_sems, recv_sems):
        x, y, c, chips = _place()
        sends = [pltpu.make_async_remote_copy(src_ref=src.at[2 * px + py], dst_ref=out.at[k], send_sem=send_sems.at[k],
                                              recv_sem=recv_sems.at[k], device_id=(px, py, c), device_id_type=MESH)
                 for k, (px, py) in enumerate(chips)]
        for cp in sends:
            cp.start()
        for cp in sends:
            cp.wait_recv()
        for cp in sends:
            cp.wait_send()

    return pl.pallas_call(
        body, out_shape=jax.ShapeDtypeStruct((3,) + parts.shape[1:], parts.dtype), in_specs=[ANY], out_specs=ANY,
        scratch_shapes=[pltpu.SemaphoreType.DMA((3,)), pltpu.SemaphoreType.DMA((3,))], name="scatter_chips",
    )(parts)


def _sum_all_devices(vec):
    rows = vec.shape[0]

    def body(x_ref, total_ref, all_ref, send_sems, recv_sems, local_sem):
        x, y, c, chips = _place()
        me, sibling = (x, y, c), (x, y, 1 - c)

        def slot(px, py, pc):
            return all_ref.at[4 * px + 2 * py + pc]

        def copy(k, block, to, src=None):
            return pltpu.make_async_remote_copy(src_ref=slot(*block) if src is None else src, dst_ref=slot(*block),
                                                send_sem=send_sems.at[k], recv_sem=recv_sems.at[k], device_id=to,
                                                device_id_type=MESH)

        mine = pltpu.make_async_copy(x_ref, slot(*me), local_sem)
        mine.start()
        first = [copy(0, me, sibling, src=x_ref)] + [copy(1 + j, me, (*chip, c), src=x_ref) for j, chip in enumerate(chips)]
        for cp in first:
            cp.start()
        passed = [copy(4 + j, (*chip, c), sibling) for j, chip in enumerate(chips)]
        for j, chip in enumerate(chips):
            copy(1 + j, (*chip, c), me).wait_recv()
            passed[j].start()
        copy(0, sibling, me).wait_recv()
        for j, chip in enumerate(chips):
            copy(4 + j, (*chip, 1 - c), me).wait_recv()
        for cp in first + passed:
            cp.wait_send()
        mine.wait()
        acc = all_ref[0]
        for d in range(1, 8):
            acc = acc + all_ref[d]
        total_ref[...] = acc

    vmem = pl.BlockSpec(memory_space=pltpu.VMEM)
    return pl.pallas_call(
        body, out_shape=[jax.ShapeDtypeStruct((rows, LANES), F32), jax.ShapeDtypeStruct((8, rows, LANES), F32)],
        in_specs=[vmem], out_specs=[vmem, vmem],
        scratch_shapes=[pltpu.SemaphoreType.DMA((7,)), pltpu.SemaphoreType.DMA((7,)), pltpu.SemaphoreType.DMA(())],
        name="sum_all_devices", compiler_params=pltpu.CompilerParams(vmem_limit_bytes=VMEM_LIMIT_BYTES),
    )(vec)[0]


PACK_COLS = 1024
BIG = (("mlp_w_up", 2, 1024, True), ("mlp_w_down", 2, 1024, False), ("xa_w_kv", 2, 512, True), ("xa_w_q", 2, 256, False),
       ("xa_w_o", 2, 256, False), ("ab_w_out", 1, 256, False), ("cd_w_out", 1, 256, False), ("cd_w_in", 1, 576, True),
       ("ab_w_in", 1, 642, True))
ENTRIES = tuple((name, layer, rows, transposed) for name, layers, rows, transposed in BIG for layer in range(layers))
PACK_ROWS = -(-sum(e[2] for e in ENTRIES) // 32) * 32
REDUCE_TILE = 208
assert (PACK_ROWS // 2) % REDUCE_TILE == 0

SMALL = (("ab_norm", (1, 1024)), ("ab_conv_w", (1, 4, 512)), ("ab_conv_b", (1, 512)), ("lru_w_a", (1, 8, 64, 64)),
         ("lru_b_a", (1, 512)), ("lru_w_i", (1, 8, 64, 64)), ("lru_b_i", (1, 512)), ("lru_lambda", (1, 512)),
         ("fox_b_f", (1, 8)), ("cd_norm", (1, 1024)), ("cd_sink", (1, 8)), ("xa_norm", (2, 1024)),
         ("xa_mem_norm", (2, 1024)), ("mlp_norm", (2, 1024)), ("final_norm", (1024,)), ("loss", (1,)))
SPLIT_SMALL = ("ab_conv_w", "cd_norm")


def _pack_rows(parts, dtype):
    lead = parts[0].shape[:-2]
    pad = jnp.zeros(lead + (PACK_ROWS - sum(e[2] for e in ENTRIES), PACK_COLS), dtype)
    return jnp.concatenate([p.astype(dtype) for p in parts] + [pad], axis=len(lead))


def _unpack_rows(packed):
    out, r0 = [], 0
    for _, _, rows, _ in ENTRIES:
        out.append(packed[..., r0:r0 + rows, :])
        r0 += rows
    return out


def _shard_rows(w):
    return [(w[name][layer].T if transposed else w[name][layer]) for name, layer, _, transposed in ENTRIES]


def _shard_blocks(rows):
    out = {}
    for (name, layer, _, transposed), r in zip(ENTRIES, rows):
        out.setdefault(name, []).append(r.T if transposed else r)
    return {name: jnp.stack(layers) for name, layers in out.items()}


def _pack_small(vals):
    flat = jnp.concatenate([vals[name].astype(F32).reshape(-1) for name, _ in SMALL])
    size = -(-flat.shape[0] // (8 * LANES)) * (8 * LANES)
    return jnp.pad(flat, (0, size - flat.shape[0])).reshape(-1, LANES)


def _unpack_small(packed):
    flat, out, at = packed.reshape(-1), {}, 0
    for name, shape in SMALL:
        out[name] = flat[at:at + math.prod(shape)].reshape(shape)
        at += math.prod(shape)
    return out


def _chip_part(full, chip):
    width = full.shape[-1] // N_CHIPS
    return lax.dynamic_slice_in_dim(full, chip * width, width, axis=full.ndim - 1)


def _chip_embed(part, chip):
    full = jnp.zeros(part.shape[:-1] + (part.shape[-1] * N_CHIPS,), part.dtype)
    return lax.dynamic_update_slice_in_dim(full, part, chip * part.shape[-1], axis=part.ndim - 1)


SUBLAYER_KEYS = {"ab_w_in": ("ab", "w_in_t"), "ab_w_out": ("ab", "w_out"), "cd_w_in": ("cd", "w_in_t"), "cd_w_out": ("cd", "w_out"),
                 "xa_w_q": ("xa", "w_q"), "xa_w_kv": ("xa", "w_kv_t"), "xa_w_o": ("xa", "w_o"), "mlp_w_up": ("mlp", "w_up_t"),
                 "mlp_w_down": ("mlp", "w_down")}


def _sublayer(name, layer):
    block, leaf = SUBLAYER_KEYS[name]
    return (block if block in ("ab", "cd") else f"{block}{layer}"), leaf


def _build_params(full_rows, w):
    pad = lambda a: jnp.pad(a, ((0, 0), (0, LANES - a.shape[1])))
    params = {
        "ab": dict(norm=w["ab_norm"], conv_w=w["ab_conv_w"][0], conv_b=w["ab_conv_b"], w_a=_block_diag(w["lru_w_a"][0]),
                   b_a=w["lru_b_a"], w_i=_block_diag(w["lru_w_i"][0]), b_i=w["lru_b_i"], lam=w["lru_lambda"],
                   b_f=w["fox_b_f"].reshape(8, 1)),
        "cd": dict(norm=w["cd_norm"], sink=pad(w["cd_sink"])),
        "final_norm": w["final_norm"].reshape(1, D_MODEL),
    }
    for layer in range(2):
        params[f"xa{layer}"] = dict(norm=w["xa_norm"][layer:layer + 1], mem_norm=w["xa_mem_norm"][layer:layer + 1])
        params[f"mlp{layer}"] = dict(norm=w["mlp_norm"][layer:layer + 1])
    for name, layer, _, _ in ENTRIES:
        block, leaf = _sublayer(name, layer)
        params[block][leaf] = full_rows[name, layer]
    w_in_t = params["ab"]["w_in_t"]
    params["ab"]["w_in_t"] = jnp.concatenate([w_in_t, jnp.zeros((LANES - 8, PACK_COLS), w_in_t.dtype)])
    return params


def _grad_rows(g):
    out = []
    for name, layer, _, _ in ENTRIES:
        block, leaf = _sublayer(name, layer)
        out.append(g[block][leaf])
    return out


def _reduce_to_owner(grads_by_chip, core):
    half = PACK_ROWS // 2
    steps = half // REDUCE_TILE
    chip = 2 * lax.axis_index("x") + lax.axis_index("y")
    place = jnp.stack([core, chip]).astype(jnp.int32)
    got = _swap_cores(grads_by_chip, name="swap_cores", half_rows=half)
    block = (1, REDUCE_TILE, PACK_COLS)

    def sum_cores(place_ref, mine_ref, got_ref, f32_ref, bf16_ref):
        total = mine_ref[...] + got_ref[...]
        f32_ref[...] = total
        bf16_ref[...] = total.astype(BF16)

    pair32, pair16 = pl.pallas_call(
        sum_cores, name="sum_cores",
        grid_spec=pltpu.PrefetchScalarGridSpec(
            num_scalar_prefetch=1, grid=(N_CHIPS, steps),
            in_specs=[pl.BlockSpec(block, lambda s, i, place_ref: (s, place_ref[0] * steps + i, 0)),
                      pl.BlockSpec(block, lambda s, i, place_ref: (s, i, 0))],
            out_specs=[pl.BlockSpec(block, lambda s, i, place_ref: (s, i, 0))] * 2),
        out_shape=[jax.ShapeDtypeStruct((N_CHIPS, half, PACK_COLS), F32), jax.ShapeDtypeStruct((N_CHIPS, half, PACK_COLS), BF16)],
        compiler_params=_params("arbitrary", "arbitrary"),
    )(place, grads_by_chip, got)
    landed = _scatter_chips(pair16)

    def sum_chips(place_ref, own_ref, landed_ref, out_ref):
        out_ref[...] = own_ref[0] + landed_ref[0].astype(F32) + landed_ref[1].astype(F32) + landed_ref[2].astype(F32)

    done = pl.pallas_call(
        sum_chips, name="sum_chips",
        grid_spec=pltpu.PrefetchScalarGridSpec(
            num_scalar_prefetch=1, grid=(steps,),
            in_specs=[pl.BlockSpec(block, lambda i, place_ref: (place_ref[1], i, 0)),
                      pl.BlockSpec((3, REDUCE_TILE, PACK_COLS), lambda i, place_ref: (0, i, 0))],
            out_specs=pl.BlockSpec(block[1:], lambda i, place_ref: (i, 0))),
        out_shape=jax.ShapeDtypeStruct((half, PACK_COLS), F32), compiler_params=_params("arbitrary"),
    )(place, pair32, landed)
    theirs = _swap_cores(done, name="share_halves")
    return jnp.where(core == 0, jnp.concatenate([done, theirs]), jnp.concatenate([theirs, done]))


def kernel(x, mem, ab_norm, ab_w_in, ab_conv_w, ab_conv_b, lru_w_a, lru_b_a, lru_w_i, lru_b_i, lru_lambda, fox_b_f, ab_w_out, cd_norm, cd_w_in, cd_sink, cd_w_out, xa_norm, xa_mem_norm, xa_w_q, xa_w_kv, xa_w_o, mlp_norm, mlp_w_up, mlp_w_down, final_norm, loss_target, m_ab_norm, m_ab_w_in, m_ab_conv_w, m_ab_conv_b, m_lru_w_a, m_lru_b_a, m_lru_w_i, m_lru_b_i, m_lru_lambda, m_fox_b_f, m_ab_w_out, m_cd_norm, m_cd_w_in, m_cd_sink, m_cd_w_out, m_xa_norm, m_xa_mem_norm, m_xa_w_q, m_xa_w_kv, m_xa_w_o, m_mlp_norm, m_mlp_w_up, m_mlp_w_down, m_final_norm, v_ab_norm, v_ab_w_in, v_ab_conv_w, v_ab_conv_b, v_lru_w_a, v_lru_b_a, v_lru_w_i, v_lru_b_i, v_lru_lambda, v_fox_b_f, v_ab_w_out, v_cd_norm, v_cd_w_in, v_cd_sink, v_cd_w_out, v_xa_norm, v_xa_mem_norm, v_xa_w_q, v_xa_w_kv, v_xa_w_o, v_mlp_norm, v_mlp_w_up, v_mlp_w_down, v_final_norm):
    given = dict(locals())
    weight_names = [name for name, _ in SMALL if name != "loss"] + [name for name, _, _, _ in BIG]
    w = {name: given[name] for name in weight_names}
    chip = 2 * lax.axis_index("x") + lax.axis_index("y")
    core = lax.axis_index("c")

    mine = _pack_rows(_shard_rows(w), BF16)
    gathered = _gather_chips(mine)
    slot = lax.broadcasted_iota(jnp.int32, (N_CHIPS, 1, 1), 0)
    full_rows = {}
    for (name, layer, rows, _), own, got in zip(ENTRIES, _unpack_rows(mine), _unpack_rows(gathered)):
        full_rows[name, layer] = jnp.where(slot == chip, own[None], got).reshape(N_CHIPS * rows, PACK_COLS)
    owner = (core == 0).astype(F32)
    quarters = {name: _chip_embed(w[name], chip) * owner for name in SPLIT_SMALL}
    zeros = {name: jnp.zeros(shape, F32) for name, shape in SMALL}
    small_full = _unpack_small(_sum_all_devices(_pack_small({**zeros, **quarters})))
    params = _build_params(full_rows, {**w, "ab_conv_w": small_full["ab_conv_w"], "cd_norm": small_full["cd_norm"]})

    loss_part, grad_x, g = _local_step(x, mem, loss_target, params)

    by_chip = _pack_rows([r.reshape(N_CHIPS, rows, PACK_COLS) for r, (_, _, rows, _) in zip(_grad_rows(g), ENTRIES)], F32)
    grads = _shard_blocks(_unpack_rows(_reduce_to_owner(by_chip, core)))
    pair = lambda key, leaf: jnp.stack([g[f"{key}0"][leaf], g[f"{key}1"][leaf]])

    small_local = {"ab_norm": g["ab"]["norm"], "ab_conv_w": g["ab"]["conv_w"], "ab_conv_b": g["ab"]["conv_b"],
                   "lru_w_a": g["ab"]["w_a"], "lru_b_a": g["ab"]["b_a"], "lru_w_i": g["ab"]["w_i"], "lru_b_i": g["ab"]["b_i"],
                   "lru_lambda": g["ab"]["lam"], "fox_b_f": g["ab"]["b_f"], "cd_norm": g["cd"]["norm"], "cd_sink": g["cd"]["sink"],
                   "xa_norm": pair("xa", "norm"), "xa_mem_norm": pair("xa", "mem_norm"), "mlp_norm": pair("mlp", "norm"),
                   "final_norm": g["final_norm"], "loss": loss_part[0, :1]}
    small_sum_packed = _sum_all_devices(_pack_small(small_local))
    small_sum = _unpack_small(small_sum_packed)
    loss = small_sum["loss"][0]

    delta, new_m, new_v = {}, {}, {}
    for name, _, _, _ in BIG:
        shape = w[name].shape
        flat = lambda a: a.reshape(-1, shape[-1])
        d, m2, v2 = _adamw(flat(w[name]), flat(grads[name]), flat(given["m_" + name]), flat(given["v_" + name]), name=f"adamw_{name}")
        delta[name], new_m[name], new_v[name] = d.reshape(shape), m2.reshape(shape), v2.reshape(shape)

    def small_state(prefix):
        vals = {name: (given[prefix + name] if name != "loss" else jnp.zeros((1,), F32)) for name, _ in SMALL}
        for name in SPLIT_SMALL:
            vals[name] = _chip_embed(vals[name], chip)
        return _pack_small(vals)
    packed = _adamw(small_state(""), small_sum_packed, small_state("m_"), small_state("v_"), name="adamw_small")
    for store, vals in zip((delta, new_m, new_v), packed):
        for name, val in _unpack_small(vals).items():
            store[name] = _chip_part(val, chip) if name in SPLIT_SMALL else val
    for name in SPLIT_SMALL:
        small_sum[name] = _chip_part(small_sum[name], chip)
    grads.update({name: small_sum[name] for name, _ in SMALL})

    order = ["ab_norm", "ab_w_in", "ab_conv_w", "ab_conv_b", "lru_w_a", "lru_b_a", "lru_w_i", "lru_b_i", "lru_lambda", "fox_b_f",
             "ab_w_out", "cd_norm", "cd_w_in", "cd_sink", "cd_w_out", "xa_norm", "xa_mem_norm", "xa_w_q", "xa_w_kv", "xa_w_o",
             "mlp_norm", "mlp_w_up", "mlp_w_down", "final_norm"]
    return (loss, grad_x, *[grads[n] for n in order], *[delta[n] for n in order], *[new_m[n] for n in order],
            *[new_v[n] for n in order])
```

```python
import functools
import math

import jax
import jax.numpy as jnp
from jax import lax
from jax.experimental import pallas as pl
from jax.experimental.pallas import tpu as pltpu

F32 = jnp.float32
BF16 = jnp.bfloat16
MESH = pl.DeviceIdType.MESH

D_MODEL = 1024
SEQ = 2048
HEAD_DIM = 64
LRU_WIDTH = 512
LRU_BLOCKS = 8
LRU_C = 8.0
CONV_WIDTH = 4
ATT_W = 512
SWA_KW = 128
SWA_WINDOW = 128
DIL_PATTERN = ((128, 1), (512, 4), (2048, 16))
MEM_LEN = 256
XA_HEADS = 4
XA_HEAD_DIM = 256
D_FF = 4096
ROPE_THETA = 10000.0
EPS = 1e-6
N_CHIPS = 4
LANES = 128

ADAM_LR, ADAM_B1, ADAM_B2, ADAM_EPS, ADAM_WD, ADAM_STEP = 0.001, 0.9, 0.999, 1e-08, 0.01, 10

VMEM_LIMIT_BYTES = 56 * 1024 * 1024
MASKED = -1e30
ROW_TILE = 512
LRU_CHUNK = 512
ATT_BLOCK = 128
BAND_ROWS = 256
ATT_CHUNK = 512
CAUSAL_ROWS = 256


def _params(*sem):
    return pltpu.CompilerParams(dimension_semantics=sem, vmem_limit_bytes=VMEM_LIMIT_BYTES)


def _rowwise(fn, rows, consts=(), out_rows=(), out_accs=(), *, name, tile=ROW_TILE, row_maps=None):
    rows = [r if isinstance(r, tuple) else (r, r.shape[1], 0) for r in rows]
    consts = list(consts)
    nr, nc, no = len(rows), len(consts), len(out_rows)
    total = rows[0][0].shape[0]
    tile = min(tile, total)
    assert total % tile == 0
    n = total // tile

    def body(*refs):
        outs = fn(*[r[...] for r in refs[:nr + nc]])
        outs = tuple(outs) if isinstance(outs, (tuple, list)) else (outs,)
        for ref, val in zip(refs[nr + nc:nr + nc + no], outs[:no]):
            ref[...] = val.astype(ref.dtype)
        for ref, val in zip(refs[nr + nc + no:], outs[no:]):
            @pl.when(pl.program_id(0) == 0)
            def _(ref=ref):
                ref[...] = jnp.zeros(ref.shape, ref.dtype)
            ref[...] += val

    in_specs = []
    for idx, (arr, width, cb) in enumerate(rows):
        if row_maps is not None and row_maps[idx] is not None:
            in_specs.append(pl.BlockSpec((tile, width), row_maps[idx]))
        else:
            in_specs.append(pl.BlockSpec((tile, width), functools.partial(lambda i, cb: (i, cb), cb=cb)))
    in_specs += [pl.BlockSpec(c.shape, lambda i: (0, 0)) for c in consts]
    out_shape = [jax.ShapeDtypeStruct((total, w), dt) for w, dt in out_rows]
    out_shape += [jax.ShapeDtypeStruct(s, F32) for s in out_accs]
    out_specs = [pl.BlockSpec((tile, w), lambda i: (i, 0)) for w, _ in out_rows]
    out_specs += [pl.BlockSpec(s, lambda i: (0, 0)) for s in out_accs]
    return pl.pallas_call(
        body, grid=(n,), in_specs=in_specs, out_specs=out_specs, out_shape=out_shape, name=name,
        compiler_params=_params("arbitrary"),
    )(*[r[0] for r in rows], *consts)


MATMUL_VMEM_BYTES = 40 * 1024 * 1024


def _matmul_tiles(m, n, k, tm, tn, out_bytes):
    tm, tn, tk = min(tm, m), min(tn, n), k

    def need():
        return 2 * (2 * tk * (tm + tn) + tm * tn * out_bytes) + (0 if tk == k else 4 * tm * tn)

    while need() > MATMUL_VMEM_BYTES:
        if tm >= tn and tm % 256 == 0:
            tm //= 2
        elif tn % 256 == 0:
            tn //= 2
        else:
            tk //= 2
    return tm, tn, tk


def _matmul(a, b, *, ta=False, tb=False, outs=(F32,), epilogue=None, extras=(), tm=1024, tn=1024, name):
    m, k = (a.shape[1], a.shape[0]) if ta else a.shape
    n = b.shape[0] if tb else b.shape[1]
    assert (b.shape[1] if tb else b.shape[0]) == k
    out_bytes = sum(jnp.dtype(dt).itemsize for dt in outs) + sum(e.dtype.itemsize for e in extras)
    tm, tn, tk = _matmul_tiles(m, n, k, tm, tn, out_bytes)
    assert m % tm == 0 and n % tn == 0 and k % tk == 0, (name, m, n, k)
    nk = k // tk
    ne, no = len(extras), len(outs)
    dims = (((0 if ta else 1,), (1 if tb else 0,)), ((), ()))

    def body(*refs):
        a_ref, b_ref = refs[:2]
        e_refs, o_refs = refs[2:2 + ne], refs[2 + ne:2 + ne + no]

        def finish(acc):
            vals = (acc,) if epilogue is None else epilogue(acc, *[e[...] for e in e_refs])
            for ref, val in zip(o_refs, vals):
                ref[...] = val.astype(ref.dtype)

        prod = lax.dot_general(a_ref[...], b_ref[...], dims, preferred_element_type=F32)
        if nk == 1:
            finish(prod)
        else:
            acc_ref = refs[-1]
            step = pl.program_id(2)

            @pl.when(step == 0)
            def _():
                acc_ref[...] = prod

            @pl.when(step > 0)
            def _():
                acc_ref[...] += prod

            @pl.when(step == nk - 1)
            def _():
                finish(acc_ref[...])

    a_spec = pl.BlockSpec((tk, tm), lambda i, j, s: (s, i)) if ta else pl.BlockSpec((tm, tk), lambda i, j, s: (i, s))
    b_spec = pl.BlockSpec((tn, tk), lambda i, j, s: (j, s)) if tb else pl.BlockSpec((tk, tn), lambda i, j, s: (s, j))
    tile_spec = pl.BlockSpec((tm, tn), lambda i, j, s: (i, j))
    return pl.pallas_call(
        body, grid=(m // tm, n // tn, nk),
        in_specs=[a_spec, b_spec] + [tile_spec] * ne,
        out_specs=[tile_spec] * no,
        out_shape=[jax.ShapeDtypeStruct((m, n), dt) for dt in outs],
        scratch_shapes=[pltpu.VMEM((tm, tn), F32)] if nk > 1 else [],
        name=name, compiler_params=_params("parallel", "parallel", "arbitrary"),
    )(a, b, *extras)


def _split3(x):
    x1 = x.astype(BF16)
    r1 = x - x1.astype(F32)
    x2 = r1.astype(BF16)
    x3 = (r1 - x2.astype(F32)).astype(BF16)
    return x1, x2, x3


def _dot_exact(x, e):
    return sum(jnp.dot(p, e, preferred_element_type=F32) for p in _split3(x))


def _head_expand(heads, width):
    dh = width // heads
    rows = jnp.arange(LANES)[:, None]
    cols = jnp.arange(width)[None, :]
    return (cols // dh == rows).astype(BF16)


def _rstd(x):
    return lax.rsqrt(jnp.mean(x * x, axis=-1, keepdims=True) + EPS)


def _rms_fwd(x, gain, *, name):
    return _rowwise(lambda x, g: x * _rstd(x) * g, [x], [gain], [(x.shape[1], BF16)], name=name)[0]


def _rms_bwd_vals(x, dhn, gain):
    r = _rstd(x)
    xh = x * r
    dxh = dhn * gain
    dx = r * (dxh - xh * jnp.mean(dxh * xh, axis=-1, keepdims=True))
    return dx, jnp.sum(dhn * xh, axis=0, keepdims=True)


def _rms_bwd(x, dhn, dres, gain, *, name):
    def fn(x, dhn, dres, g):
        dx, dg = _rms_bwd_vals(x, dhn, g)
        dh = dres + dx
        return dh, dh, dg
    d = x.shape[1]
    return _rowwise(fn, [x, dhn, dres], [gain], [(d, F32), (d, BF16)], [(1, d)], name=name)


def _loss_and_grad(h, target, gain, *, name):
    def fn(h, tgt, g):
        r = _rstd(h)
        err = h * r * g - tgt
        loss = 0.5 * jnp.sum(jnp.mean(err * err, axis=-1, keepdims=True), axis=0, keepdims=True)
        dx, dg = _rms_bwd_vals(h, err * (1.0 / D_MODEL), g)
        return dx, dx, jnp.broadcast_to(loss, (1, LANES)), dg
    d = h.shape[1]
    return _rowwise(fn, [h, target], [gain], [(d, F32), (d, BF16)], [(1, LANES), (1, d)], name=name)


def _adamw(w, g, m, v, *, name):
    rows, cols = w.shape
    tile = rows
    for cand in (256, 128, 64, 32, 16, 8):
        if rows % cand == 0 and rows > cand:
            tile = cand
            break
    bc1 = 1.0 - ADAM_B1 ** ADAM_STEP
    bc2 = 1.0 - ADAM_B2 ** ADAM_STEP

    def fn(w, g, m, v):
        m2 = ADAM_B1 * m + (1.0 - ADAM_B1) * g
        v2 = ADAM_B2 * v + (1.0 - ADAM_B2) * (g * g)
        delta = -ADAM_LR * ((m2 / bc1) / (jnp.sqrt(v2 / bc2) + ADAM_EPS) + ADAM_WD * w)
        return delta, m2, v2
    return _rowwise(fn, [w, g, m, v], [], [(cols, F32)] * 3, name=name, tile=tile)


def _attn_specs(arr, width, cb, classes, rows_block, whole):
    ncols = arr.shape[2] // (classes * width)
    if whole:
        return pl.BlockSpec((1, arr.shape[1], width), lambda n, r, i: (n, 0, r * ncols + cb))
    return pl.BlockSpec((1, rows_block, width), lambda n, r, i: (n, i, r * ncols + cb))


def _attn_tiles(mode, lq, lk):
    if mode == "full":
        return min(lq, 256), lk
    if mode == "band":
        tq = min(BAND_ROWS, lq)
        return tq, min(tq + ATT_BLOCK, lk)
    return CAUSAL_ROWS, ATT_CHUNK


def _window(mode, i, j, tq, win, lk):
    if mode == "causal":
        return pl.multiple_of(j * win, win), (i * tq) // win + 1
    if mode == "band":
        return pl.multiple_of(jnp.clip(i * tq + tq - win, 0, lk - win), ATT_BLOCK), 1
    return 0, 1


def _allowed(mode, i, start, tq, win, max_dist):
    if mode == "full":
        return None
    dist = (i * tq + lax.broadcasted_iota(jnp.int32, (tq, win), 0)) - (start + lax.broadcasted_iota(jnp.int32, (tq, win), 1))
    ok = dist >= 0
    if max_dist is not None:
        ok = ok & (dist <= max_dist)
    return ok


def _head_groups(heads, dh):
    gw = max(LANES, dh)
    return gw, gw // dh, heads // (gw // dh)


def _own_lanes(rows, gw, dh, u):
    return lax.broadcasted_iota(jnp.int32, (rows, gw), 1) // dh == u


def _only_head(x, own, per, u):
    return x if per == 1 else jnp.where(own[u], x, jnp.zeros_like(x))


def _step_scores(qz, kw, kb_row, ok, scale):
    s = lax.dot_general(qz, kw, (((1,), (1,)), ((), ())), preferred_element_type=F32) * scale
    if kb_row is not None:
        s = s - kb_row
    if ok is not None:
        s = jnp.where(ok, s, MASKED)
    return s


def _attn_fwd(q, k, v, kb=None, *, classes=1, heads, dh, mode, max_dist=None, name):
    (qa, qc), (ka, kc), (va, vc) = q, k, v
    n, lq, lk = qa.shape[0], qa.shape[1], ka.shape[1]
    width = heads * dh
    tq, win = _attn_tiles(mode, lq, lk)
    gw, per, groups = _head_groups(heads, dh)
    scale = dh ** -0.5
    has_kb = kb is not None
    single = mode != "causal"

    def body(*refs):
        q_ref, k_ref, v_ref = refs[:3]
        kb_ref = refs[3] if has_kb else None
        o_ref, lse_ref = refs[-2:]
        i = pl.program_id(2)
        lane = lax.broadcasted_iota(jnp.int32, (tq, LANES), 1)
        own = [_own_lanes(tq, gw, dh, u) for u in range(per)]

        def step(j, carry):
            m_all, l_all = carry
            start, _ = _window(mode, i, j, tq, win, lk)
            ok = _allowed(mode, i, start, tq, win, max_dist)
            for g in range(groups):
                cols = slice(g * gw, (g + 1) * gw)
                qg = q_ref[0, :, cols]
                kw = k_ref[0, pl.ds(start, win), cols]
                vw = v_ref[0, pl.ds(start, win), cols]
                alpha_g = new_g = None
                for u in range(per):
                    h = g * per + u
                    kb_row = kb_ref[0, h, pl.ds(j, 1), :] if has_kb else None
                    s = _step_scores(_only_head(qg, own, per, u), kw, kb_row, ok, scale)
                    m_old, l_old = m_all[:, h:h + 1], l_all[:, h:h + 1]
                    m_new = jnp.maximum(m_old, jnp.max(s, axis=1, keepdims=True))
                    alpha = jnp.exp(m_old - m_new)
                    p = jnp.exp(s - m_new)
                    l_new = alpha * l_old + jnp.sum(p, axis=1, keepdims=True)
                    r = jnp.dot(p.astype(BF16), vw, preferred_element_type=F32)
                    alpha_g = alpha if u == 0 else jnp.where(own[u], alpha, alpha_g)
                    new_g = r if u == 0 else jnp.where(own[u], r, new_g)
                    m_all = jnp.where(lane == h, m_new, m_all)
                    l_all = jnp.where(lane == h, l_new, l_all)
                o_ref[0, :, cols] = new_g if single else alpha_g * o_ref[0, :, cols] + new_g
            return m_all, l_all

        carry = (jnp.full((tq, LANES), MASKED, F32), jnp.zeros((tq, LANES), F32))
        if single:
            m_all, l_all = step(0, carry)
        else:
            o_ref[...] = jnp.zeros(o_ref.shape, F32)
            m_all, l_all = lax.fori_loop(0, _window(mode, i, 0, tq, win, lk)[1], step, carry)
        l_all = jnp.where(lane < heads, l_all, 1.0)
        inv = 1.0 / l_all
        for g in range(groups):
            cols = slice(g * gw, (g + 1) * gw)
            inv_g = inv[:, g * per:g * per + 1]
            for u in range(1, per):
                inv_g = jnp.where(own[u], inv[:, g * per + u:g * per + u + 1], inv_g)
            o_ref[0, :, cols] = o_ref[0, :, cols] * inv_g
        lse_ref[0] = jnp.where(lane < heads, m_all + jnp.log(l_all), 0.0)

    in_specs = [_attn_specs(qa, width, qc, classes, tq, False), _attn_specs(ka, width, kc, classes, win, True),
                _attn_specs(va, width, vc, classes, win, True)]
    args = [qa, ka, va]
    if has_kb:
        in_specs.append(pl.BlockSpec((1,) + kb.shape[1:], lambda n_, r, i: (n_, 0, 0, 0)))
        args.append(kb)
    out_shape = [jax.ShapeDtypeStruct((n, lq, classes * width), F32), jax.ShapeDtypeStruct((n, lq, classes * LANES), F32)]
    out_specs = [pl.BlockSpec((1, tq, width), lambda n_, r, i: (n_, i, r)), pl.BlockSpec((1, tq, LANES), lambda n_, r, i: (n_, i, r))]
    return pl.pallas_call(
        body, grid=(n, classes, lq // tq), in_specs=in_specs, out_specs=out_specs, out_shape=out_shape, name=name,
        compiler_params=_params("parallel", "parallel", "arbitrary"),
    )(*args)


def _attn_bwd(q, k, v, do, lse, delta, kb=None, *, classes=1, heads, dh, mode, max_dist=None, name):
    (qa, qc), (ka, kc), (va, vc), (da, dc) = q, k, v, do
    n, lq, lk = qa.shape[0], qa.shape[1], ka.shape[1]
    width = heads * dh
    tq, win = _attn_tiles(mode, lq, lk)
    gw, per, groups = _head_groups(heads, dh)
    scale = dh ** -0.5
    has_kb = kb is not None
    single = mode != "causal"
    nt = (((1,), (1,)), ((), ()))
    tn = (((0,), (0,)), ((), ()))

    def body(*refs):
        q_ref, k_ref, v_ref, do_ref, lse_ref, dl_ref = refs[:6]
        kb_ref = refs[6] if has_kb else None
        n_in = 7 if has_kb else 6
        dq_ref, dk_ref, dv_ref = refs[n_in:n_in + 3]
        dkb_ref, drow_ref = refs[n_in + 3:n_in + 5] if has_kb else (None, None)
        i = pl.program_id(2)

        @pl.when(i == 0)
        def _():
            dk_ref[...] = jnp.zeros(dk_ref.shape, F32)
            dv_ref[...] = jnp.zeros(dv_ref.shape, F32)
            if has_kb:
                dkb_ref[...] = jnp.zeros(dkb_ref.shape, F32)

        lse_all = lse_ref[0]
        dl_all = dl_ref[0]
        lane = lax.broadcasted_iota(jnp.int32, (tq, LANES), 1)
        own = [_own_lanes(tq, gw, dh, u) for u in range(per)]

        def step(j, drow_all):
            start, _ = _window(mode, i, j, tq, win, lk)
            ok = _allowed(mode, i, start, tq, win, max_dist)
            for g in range(groups):
                cols = slice(g * gw, (g + 1) * gw)
                qg = q_ref[0, :, cols]
                dog = do_ref[0, :, cols]
                kw = k_ref[0, pl.ds(start, win), cols]
                vw = v_ref[0, pl.ds(start, win), cols]
                dq_g = dk_w = dv_w = None
                for u in range(per):
                    h = g * per + u
                    qz, doz = _only_head(qg, own, per, u), _only_head(dog, own, per, u)
                    kb_row = kb_ref[0, h, pl.ds(j, 1), :] if has_kb else None
                    p = jnp.exp(_step_scores(qz, kw, kb_row, ok, scale) - lse_all[:, h:h + 1])
                    dp = lax.dot_general(doz, vw, nt, preferred_element_type=F32)
                    ds = p * (dp - dl_all[:, h:h + 1])
                    dsb = ds.astype(BF16)
                    r = jnp.dot(dsb, kw, preferred_element_type=F32)
                    dq_g = r if u == 0 else jnp.where(own[u], r, dq_g)
                    dk_u = lax.dot_general(dsb, qz, tn, preferred_element_type=F32)
                    dv_u = lax.dot_general(p.astype(BF16), doz, tn, preferred_element_type=F32)
                    dk_w = dk_u if u == 0 else dk_w + dk_u
                    dv_w = dv_u if u == 0 else dv_w + dv_u
                    if has_kb:
                        dkb_ref[0, h, pl.ds(j, 1), :] += -jnp.sum(ds, axis=0, keepdims=True)
                        drow_all = drow_all + jnp.where(lane == h, jnp.sum(ds, axis=1, keepdims=True), 0.0)
                dk_ref[0, pl.ds(start, win), cols] += dk_w * scale
                dv_ref[0, pl.ds(start, win), cols] += dv_w
                if single:
                    dq_ref[0, :, cols] = dq_g * scale
                else:
                    dq_ref[0, :, cols] += dq_g * scale
            return drow_all

        drow_all = jnp.zeros((tq, LANES), F32)
        if single:
            drow_all = step(0, drow_all)
        else:
            dq_ref[...] = jnp.zeros(dq_ref.shape, F32)
            drow_all = lax.fori_loop(0, _window(mode, i, 0, tq, win, lk)[1], step, drow_all)
        if has_kb:
            drow_ref[0] = drow_all

    row_spec = functools.partial(_attn_specs, classes=classes, rows_block=tq, whole=False)
    in_specs = [row_spec(qa, width, qc), _attn_specs(ka, width, kc, classes, win, True), _attn_specs(va, width, vc, classes, win, True),
                row_spec(da, width, dc), row_spec(lse, LANES, 0), row_spec(delta, LANES, 0)]
    args = [qa, ka, va, da, lse, delta]
    out_shape = [jax.ShapeDtypeStruct((n, lq, classes * width), F32), jax.ShapeDtypeStruct((n, lk, classes * width), F32),
                 jax.ShapeDtypeStruct((n, lk, classes * width), F32)]
    kv_spec = pl.BlockSpec((1, lk, width), lambda n_, r, i: (n_, 0, r))
    out_specs = [pl.BlockSpec((1, tq, width), lambda n_, r, i: (n_, i, r)), kv_spec, kv_spec]
    if has_kb:
        kb_spec = pl.BlockSpec((1,) + kb.shape[1:], lambda n_, r, i: (n_, 0, 0, 0))
        in_specs.append(kb_spec)
        args.append(kb)
        out_shape += [jax.ShapeDtypeStruct(kb.shape, F32), jax.ShapeDtypeStruct((n, lq, LANES), F32)]
        out_specs += [kb_spec, pl.BlockSpec((1, tq, LANES), lambda n_, r, i: (n_, i, 0))]
    return pl.pallas_call(
        body, grid=(n, classes, lq // tq), in_specs=in_specs, out_specs=out_specs, out_shape=out_shape, name=name,
        compiler_params=_params("parallel", "parallel", "arbitrary"),
    )(*args)


def _head_delta(do, out, heads, *, name, lse=None, sink=None):
    width = out.shape[1]
    gather = _head_expand(heads, width).T

    if sink is None:
        return _rowwise(lambda do, o, e: _dot_exact(do * o, e), [do, out], [gather], [(LANES, F32)], name=name)[0]

    def fn(do, o, lse, e, sink):
        delta = _dot_exact(do * o, e)
        return delta, -jnp.sum(jnp.exp(sink - lse) * delta, axis=0, keepdims=True)
    return _rowwise(fn, [do, out, lse], [gather, sink], [(LANES, F32)], [(1, LANES)], name=name)


def _rope_tables():
    half = HEAD_DIM // 2
    inv = ROPE_THETA ** (-jnp.arange(half, dtype=F32) / half)
    ang = jnp.arange(SEQ, dtype=F32)[:, None] * inv[None, :]
    cos = jnp.tile(jnp.cos(ang), (1, 2 * ATT_W // HEAD_DIM))
    sin = jnp.tile(jnp.concatenate([-jnp.sin(ang), jnp.sin(ang)], axis=1), (1, ATT_W // HEAD_DIM))
    return cos, sin


def _rotate(x, cos, sin):
    width = x.shape[1]
    lane = lax.broadcasted_iota(jnp.int32, x.shape, 1)
    first_half = (lane % HEAD_DIM) < (HEAD_DIM // 2)
    swapped = jnp.where(first_half, pltpu.roll(x, width - HEAD_DIM // 2, axis=1), pltpu.roll(x, HEAD_DIM // 2, axis=1))
    return x * cos[:, :width] + swapped * sin[:, :width]


def _gelu(x):
    k = math.sqrt(2.0 / math.pi)
    t = jnp.tanh(k * (x + 0.044715 * x * x * x))
    return 0.5 * x * (1.0 + t), t


def _gelu_grad(x, t):
    k = math.sqrt(2.0 / math.pi)
    return 0.5 * (1.0 + t) + 0.5 * x * (1.0 - t * t) * k * (1.0 + 3.0 * 0.044715 * x * x)


def _shift_down(x, halo, s):
    ext = jnp.concatenate([halo, x], axis=0)
    return pltpu.roll(ext, s, axis=0)[8:, :]


def _shift_up(x, halo, s):
    rows = x.shape[0]
    ext = jnp.concatenate([x, halo], axis=0)
    return pltpu.roll(ext, rows + 8 - s, axis=0)[:rows, :]


def _lru_gates(u, halo, cw, cb, wa, ba, wi, bi, lam):
    taps = [_shift_down(u, halo, CONV_WIDTH - 1 - k) for k in range(CONV_WIDTH - 1)] + [u]
    uc = cb + sum(cw[k:k + 1, :] * taps[k] for k in range(CONV_WIDTH))
    ucb = uc.astype(BF16)
    r = jax.nn.sigmoid(jnp.dot(ucb, wa, preferred_element_type=F32) + ba)
    gi = jax.nn.sigmoid(jnp.dot(ucb, wi, preferred_element_type=F32) + bi)
    sp = jnp.maximum(-lam, 0.0) + jnp.log(1.0 + jnp.exp(-jnp.abs(lam)))
    log_a = -LRU_C * r * sp
    a = jnp.exp(log_a)
    x2 = 2.0 * log_a
    one_minus_a2 = jnp.where(x2 > -0.01, -x2 * (1.0 + x2 * (0.5 + x2 * (1.0 / 6.0 + x2 / 24.0))), 1.0 - jnp.exp(x2))
    mult = jnp.sqrt(one_minus_a2)
    return taps, uc, ucb, r, gi, sp, a, mult


def _lru_specs(nchunk, reverse):
    def chunk(b, c):
        return b * nchunk + ((nchunk - 1 - c) if reverse else c)

    def halo_before(b, c):
        return jnp.maximum(chunk(b, c) * (LRU_CHUNK // 8) - 1, 0)

    return chunk, halo_before


def _lru_fwd(zug, cw, cb, wa, ba, wi, bi, lam, *, name):
    total = zug.shape[0]
    nchunk = SEQ // LRU_CHUNK
    w = LRU_WIDTH
    chunk, halo_before = _lru_specs(nchunk, False)

    def body(u_ref, uh_ref, g_ref, cw_ref, cb_ref, wa_ref, ba_ref, wi_ref, bi_ref, lam_ref, y_ref, h_ref, a_sc, x_sc, carry_sc):
        c = pl.program_id(1)
        u = u_ref[...]
        halo = jnp.where(c > 0, uh_ref[...], 0.0)
        _, uc, _, _, gi, _, a, mult = _lru_gates(u, halo, cw_ref[...], cb_ref[...], wa_ref[...], ba_ref[...],
                                                 wi_ref[...], bi_ref[...], lam_ref[...])
        a_sc[...] = a
        x_sc[...] = mult * (gi * uc)

        @pl.when(c == 0)
        def _():
            carry_sc[...] = jnp.zeros(carry_sc.shape, F32)

        def tile_step(t, h):
            r0 = pl.multiple_of(t * 8, 8)
            at = a_sc[pl.ds(r0, 8), :]
            xt = x_sc[pl.ds(r0, 8), :]
            rows = []
            for j in range(8):
                h = at[j:j + 1, :] * h + xt[j:j + 1, :]
                rows.append(h)
            h_ref[pl.ds(r0, 8), :] = jnp.concatenate(rows, axis=0)
            return h

        h_last = lax.fori_loop(0, LRU_CHUNK // 8, tile_step, carry_sc[0:1, :])
        carry_sc[0:1, :] = h_last
        gel, _ = _gelu(g_ref[...])
        y_ref[...] = (h_ref[...] * gel).astype(BF16)

    small = lambda arr: pl.BlockSpec(arr.shape, lambda b, c: (0, 0))
    return pl.pallas_call(
        body, grid=(total // SEQ, nchunk),
        in_specs=[pl.BlockSpec((LRU_CHUNK, w), lambda b, c: (chunk(b, c), 0)),
                  pl.BlockSpec((8, w), lambda b, c: (halo_before(b, c), 0)),
                  pl.BlockSpec((LRU_CHUNK, w), lambda b, c: (chunk(b, c), 1)),
                  small(cw), small(cb), small(wa), small(ba), small(wi), small(bi), small(lam)],
        out_specs=[pl.BlockSpec((LRU_CHUNK, w), lambda b, c: (chunk(b, c), 0))] * 2,
        out_shape=[jax.ShapeDtypeStruct((total, w), BF16), jax.ShapeDtypeStruct((total, w), F32)],
        scratch_shapes=[pltpu.VMEM((LRU_CHUNK, w), F32), pltpu.VMEM((LRU_CHUNK, w), F32), pltpu.VMEM((8, w), F32)],
        name=name, compiler_params=_params("arbitrary", "arbitrary"),
    )(zug, zug, zug, cw, cb, wa, ba, wi, bi, lam)


def _lru_bwd(zug, hl, dy, cw, cb, wa, ba, wi, bi, lam, *, name):
    total = zug.shape[0]
    nchunk = SEQ // LRU_CHUNK
    w = LRU_WIDTH
    chunk, halo_before = _lru_specs(nchunk, True)
    tn = (((0,), (0,)), ((), ()))
    nt = (((1,), (1,)), ((), ()))

    def body(u_ref, uh_ref, g_ref, hl_ref, hh_ref, dy_ref, cw_ref, cb_ref, wa_ref, ba_ref, wi_ref, bi_ref, lam_ref,
             dz_ref, dcw_ref, dcb_ref, dwa_ref, dba_ref, dwi_ref, dbi_ref, dlam_ref,
             a_sc, d_sc, g_sc, gcarry_sc, acarry_sc, duc_sc):
        b, c = pl.program_id(0), pl.program_id(1)
        first_chunk = c == nchunk - 1

        @pl.when((b == 0) & (c == 0))
        def _():
            for ref in (dcw_ref, dcb_ref, dwa_ref, dba_ref, dwi_ref, dbi_ref, dlam_ref):
                ref[...] = jnp.zeros(ref.shape, F32)

        @pl.when(c == 0)
        def _():
            gcarry_sc[...] = jnp.zeros(gcarry_sc.shape, F32)
            acarry_sc[...] = jnp.zeros(acarry_sc.shape, F32)
            duc_sc[...] = jnp.zeros(duc_sc.shape, F32)

        u = u_ref[...]
        halo = jnp.where(first_chunk, 0.0, uh_ref[...])
        cw, wa, wi, lam = cw_ref[...], wa_ref[...], wi_ref[...], lam_ref[...]
        taps, uc, ucb, r, gi, sp, a, mult = _lru_gates(u, halo, cw, cb_ref[...], wa, ba_ref[...], wi, bi_ref[...], lam)
        gate = g_ref[...]
        gel, th = _gelu(gate)
        dy = dy_ref[...]
        hl = hl_ref[...]
        a_sc[...] = a
        d_sc[...] = dy * gel
        dgate = dy * hl * _gelu_grad(gate, th)

        def tile_step(t, carry):
            g_next, a_next = carry
            r0 = pl.multiple_of((LRU_CHUNK // 8 - 1 - t) * 8, 8)
            dt = d_sc[pl.ds(r0, 8), :]
            at = a_sc[pl.ds(r0, 8), :]
            rows = [None] * 8
            for j in reversed(range(8)):
                g_next = dt[j:j + 1, :] + a_next * g_next
                a_next = at[j:j + 1, :]
                rows[j] = g_next
            g_sc[pl.ds(r0, 8), :] = jnp.concatenate(rows, axis=0)
            return g_next, a_next

        g_last, a_last = lax.fori_loop(0, LRU_CHUNK // 8, tile_step, (gcarry_sc[0:1, :], acarry_sc[0:1, :]))
        gcarry_sc[0:1, :] = g_last
        acarry_sc[0:1, :] = a_last

        gs = g_sc[...]
        hl_halo = jnp.where(first_chunk, 0.0, hh_ref[...])
        da = gs * _shift_down(hl, hl_halo, 1)
        dmult = gs * gi * uc
        dgi = gs * mult * uc
        duc = gs * mult * gi
        dlog_a = da * a - dmult * a * a / mult
        dr = dlog_a * (-LRU_C * sp)
        dsp = jnp.sum(dlog_a * (-LRU_C * r), axis=0, keepdims=True)
        dlam_ref[...] += -dsp * jax.nn.sigmoid(-lam)
        dpa = dr * r * (1.0 - r)
        dpi = dgi * gi * (1.0 - gi)
        dpab, dpib = dpa.astype(BF16), dpi.astype(BF16)
        dba_ref[...] += jnp.sum(dpa, axis=0, keepdims=True)
        dbi_ref[...] += jnp.sum(dpi, axis=0, keepdims=True)
        dwa_ref[...] += lax.dot_general(ucb, dpab, tn, preferred_element_type=F32)
        dwi_ref[...] += lax.dot_general(ucb, dpib, tn, preferred_element_type=F32)
        duc = duc + lax.dot_general(dpab, wa, nt, preferred_element_type=F32)
        duc = duc + lax.dot_general(dpib, wi, nt, preferred_element_type=F32)
        dcb_ref[...] += jnp.sum(duc, axis=0, keepdims=True)
        dcw_ref[...] += jnp.concatenate([jnp.sum(duc * taps[k], axis=0, keepdims=True) for k in range(CONV_WIDTH)], axis=0)
        after = duc_sc[...]
        du = cw[CONV_WIDTH - 1:CONV_WIDTH, :] * duc
        for k in range(CONV_WIDTH - 1):
            du = du + cw[k:k + 1, :] * _shift_up(duc, after, CONV_WIDTH - 1 - k)
        duc_sc[...] = duc[0:8, :]
        dz_ref[:, 0:w] = du.astype(BF16)
        dz_ref[:, w:2 * w] = dgate.astype(BF16)

    small = lambda arr: pl.BlockSpec(arr.shape, lambda b, c: (0, 0))
    acc = lambda shape: pl.BlockSpec(shape, lambda b, c: (0, 0))
    chunk_spec = lambda cb_: pl.BlockSpec((LRU_CHUNK, w), lambda b, c: (chunk(b, c), cb_))
    halo_spec = pl.BlockSpec((8, w), lambda b, c: (halo_before(b, c), 0))
    acc_shapes = [(CONV_WIDTH, w), (1, w), (w, w), (1, w), (w, w), (1, w), (1, w)]
    return pl.pallas_call(
        body, grid=(total // SEQ, nchunk),
        in_specs=[chunk_spec(0), halo_spec, chunk_spec(1), chunk_spec(0), halo_spec, chunk_spec(0),
                  small(cw), small(cb), small(wa), small(ba), small(wi), small(bi), small(lam)],
        out_specs=[pl.BlockSpec((LRU_CHUNK, 2 * w), lambda b, c: (chunk(b, c), 0))] + [acc(s) for s in acc_shapes],
        out_shape=[jax.ShapeDtypeStruct((total, 2 * w), BF16)] + [jax.ShapeDtypeStruct(s, F32) for s in acc_shapes],
        scratch_shapes=[pltpu.VMEM((LRU_CHUNK, w), F32)] * 3 + [pltpu.VMEM((8, w), F32)] * 3,
        name=name, compiler_params=_params("arbitrary", "arbitrary"),
    )(zug, zug, zug, hl, hl, dy, cw, cb, wa, ba, wi, bi, lam)


def _block_diag(wb):
    eye = jnp.eye(LRU_BLOCKS, dtype=wb.dtype)
    return jnp.einsum("gcd,gh->gchd", wb, eye).reshape(LRU_WIDTH, LRU_WIDTH).astype(BF16)


def _diag_blocks(wd):
    blk = LRU_WIDTH // LRU_BLOCKS
    return jnp.stack([wd[g * blk:(g + 1) * blk, g * blk:(g + 1) * blk] for g in range(LRU_BLOCKS)])


FOX_SCAN = 256


def _fox_bias(fl, bf, *, name):
    nb = fl.shape[0]

    def body(fl_ref, bf_ref, c_ref):
        x = fl_ref[0] + bf_ref[...]
        logf = jnp.minimum(x, 0.0) - jnp.log(1.0 + jnp.exp(-jnp.abs(x)))
        upper = (lax.broadcasted_iota(jnp.int32, (FOX_SCAN, FOX_SCAN), 0)
                 <= lax.broadcasted_iota(jnp.int32, (FOX_SCAN, FOX_SCAN), 1)).astype(BF16)
        carry = jnp.zeros((LRU_BLOCKS, 1), F32)
        for j in range(SEQ // FOX_SCAN):
            blk = logf[:, j * FOX_SCAN:(j + 1) * FOX_SCAN]
            c_ref[0, :, j * FOX_SCAN:(j + 1) * FOX_SCAN] = _dot_exact(blk, upper) + carry
            carry = carry + jnp.sum(blk, axis=1, keepdims=True)

    return pl.pallas_call(
        body, grid=(nb,),
        in_specs=[pl.BlockSpec((1, 8, SEQ), lambda b: (b, 0, 0)), pl.BlockSpec((8, 1), lambda b: (0, 0))],
        out_specs=pl.BlockSpec((1, 8, SEQ), lambda b: (b, 0, 0)),
        out_shape=jax.ShapeDtypeStruct((nb, 8, SEQ), F32), name=name, compiler_params=_params("arbitrary"),
    )(fl, bf)


def _fox_bias_bwd(fl, bf, dc, *, name):
    nb = fl.shape[0]

    def body(fl_ref, bf_ref, dc_ref, dfl_ref, dbf_ref):
        @pl.when(pl.program_id(0) == 0)
        def _():
            dbf_ref[...] = jnp.zeros(dbf_ref.shape, F32)

        x = fl_ref[0] + bf_ref[...]
        dc_all = dc_ref[0]
        lower = (lax.broadcasted_iota(jnp.int32, (FOX_SCAN, FOX_SCAN), 0)
                 >= lax.broadcasted_iota(jnp.int32, (FOX_SCAN, FOX_SCAN), 1)).astype(BF16)
        carry = jnp.zeros((LRU_BLOCKS, 1), F32)
        total = jnp.zeros((LRU_BLOCKS, 1), F32)
        for j in reversed(range(SEQ // FOX_SCAN)):
            cols = slice(j * FOX_SCAN, (j + 1) * FOX_SCAN)
            blk = dc_all[:, cols]
            dlogf = _dot_exact(blk, lower) + carry
            carry = carry + jnp.sum(blk, axis=1, keepdims=True)
            dx = dlogf * jax.nn.sigmoid(-x[:, cols])
            dfl_ref[0, :, cols] = dx
            total = total + jnp.sum(dx, axis=1, keepdims=True)
        dbf_ref[...] += total

    return pl.pallas_call(
        body, grid=(nb,),
        in_specs=[pl.BlockSpec((1, 8, SEQ), lambda b: (b, 0, 0)), pl.BlockSpec((8, 1), lambda b: (0, 0)),
                  pl.BlockSpec((1, 8, SEQ), lambda b: (b, 0, 0))],
        out_specs=[pl.BlockSpec((1, 8, SEQ), lambda b: (b, 0, 0)), pl.BlockSpec((8, 1), lambda b: (0, 0))],
        out_shape=[jax.ShapeDtypeStruct((nb, 8, SEQ), F32), jax.ShapeDtypeStruct((8, 1), F32)],
        name=name, compiler_params=_params("arbitrary"),
    )(fl, bf, dc)


def _seq3(a, nb):
    return a.reshape(nb, a.shape[0] // nb, a.shape[1])


def _flat2(a):
    return a.reshape(a.shape[0] * a.shape[1], a.shape[2])


def _mlp_fwd(h, p, tag):
    hn = _rms_fwd(h, p["norm"], name=f"{tag}_norm")
    act, = _matmul(hn, p["w_up_t"], tb=True, outs=(BF16,), epilogue=lambda acc: (jnp.square(jnp.maximum(acc, 0.0)),),
                   name=f"{tag}_up")
    out, = _matmul(act, p["w_down"], extras=(h,), epilogue=lambda acc, res: (acc + res,), name=f"{tag}_down")
    return out, (h, hn, act)


def _mlp_bwd(dh, dhb, p, saved, tag):
    h, hn, act = saved
    dup, = _matmul(dhb, p["w_down"], tb=True, outs=(BF16,), extras=(act,),
                   epilogue=lambda acc, act: (acc * (2.0 * jnp.sqrt(act.astype(F32))),), name=f"{tag}_bwd_dup")
    dw_down, = _matmul(act, dhb, ta=True, name=f"{tag}_bwd_dwdown")
    dw_up_t, = _matmul(dup, hn, ta=True, name=f"{tag}_bwd_dwup")
    dhn, = _matmul(dup, p["w_up_t"], name=f"{tag}_bwd_dhn")
    dh2, dh2b, dg = _rms_bwd(h, dhn, dh, p["norm"], name=f"{tag}_bwd_norm")
    return dh2, dh2b, {"norm": dg, "w_up_t": dw_up_t, "w_down": dw_down}


def _xa_fwd(h, mem, p, tag, nb):
    hn = _rms_fwd(h, p["norm"], name=f"{tag}_norm")
    mem_n = _rms_fwd(mem, p["mem_norm"], name=f"{tag}_memnorm")
    q, = _matmul(hn, p["w_q"], outs=(BF16,), name=f"{tag}_q")
    kv, = _matmul(mem_n, p["w_kv_t"], tb=True, outs=(BF16,), name=f"{tag}_kv")
    q3, kv3 = _seq3(q, nb), _seq3(kv, nb)
    o, lse = _attn_fwd((q3, 0), (kv3, 0), (kv3, 1), heads=XA_HEADS, dh=XA_HEAD_DIM, mode="full",
                       name=f"{tag}_attn")
    o = _flat2(o)
    ob, = _rowwise(lambda o: o, [o], [], [(D_MODEL, BF16)], name=f"{tag}_ocast")
    out, = _matmul(ob, p["w_o"], extras=(h,), epilogue=lambda acc, res: (acc + res,), name=f"{tag}_o")
    return out, (h, hn, mem_n, q3, kv3, o, ob, lse)


def _xa_bwd(dh, dhb, mem, p, saved, tag, nb):
    h, hn, mem_n, q3, kv3, o, ob, lse = saved
    do, dob = _matmul(dhb, p["w_o"], tb=True, outs=(F32, BF16), epilogue=lambda acc: (acc, acc), name=f"{tag}_bwd_do")
    dw_o, = _matmul(ob, dhb, ta=True, name=f"{tag}_bwd_dwo")
    delta = _head_delta(do, o, XA_HEADS, name=f"{tag}_bwd_delta")
    dq, dk, dv = _attn_bwd((q3, 0), (kv3, 0), (kv3, 1), (_seq3(dob, nb), 0), lse, _seq3(delta, nb), heads=XA_HEADS,
                           dh=XA_HEAD_DIM, mode="full", name=f"{tag}_bwd_attn")
    dqb, = _rowwise(lambda x: x, [_flat2(dq)], [], [(D_MODEL, BF16)], name=f"{tag}_bwd_dqcast")
    dkvb, = _rowwise(lambda a, b: jnp.concatenate([a, b], axis=1), [_flat2(dk), _flat2(dv)], [], [(2 * D_MODEL, BF16)],
                     name=f"{tag}_bwd_dkvcast")
    dw_q, = _matmul(hn, dqb, ta=True, name=f"{tag}_bwd_dwq")
    dw_kv_t, = _matmul(dkvb, mem_n, ta=True, name=f"{tag}_bwd_dwkv")
    dhn, = _matmul(dqb, p["w_q"], tb=True, name=f"{tag}_bwd_dhn")
    dmem_n, = _matmul(dkvb, p["w_kv_t"], name=f"{tag}_bwd_dmemn")
    dg_mem, = _rowwise(lambda x, d, g: _rms_bwd_vals(x, d, g)[1], [mem, dmem_n], [p["mem_norm"]], [], [(1, D_MODEL)],
                       name=f"{tag}_bwd_memnorm")
    dh2, dh2b, dg = _rms_bwd(h, dhn, dh, p["norm"], name=f"{tag}_bwd_norm")
    return dh2, dh2b, {"norm": dg, "mem_norm": dg_mem, "w_q": dw_q, "w_kv_t": dw_kv_t, "w_o": dw_o}


AB_MAIN = 2 * LRU_WIDTH + 3 * ATT_W


def _ab_fwd(h, p, nb):
    hn = _rms_fwd(h, p["norm"], name="ab_norm")
    w_in_t = p["w_in_t"]
    zug, = _matmul(hn, w_in_t[:2 * LRU_WIDTH], tb=True, name="ab_in_ug")
    qkv, = _matmul(hn, w_in_t[2 * LRU_WIDTH:AB_MAIN], tb=True, outs=(BF16,), tn=768, name="ab_in_qkv")
    zf, = _matmul(hn, w_in_t[AB_MAIN:], tb=True, name="ab_in_f")
    fl = zf[:, :8].reshape(nb, SEQ, 8).transpose(0, 2, 1)
    cbias = _fox_bias(fl, p["b_f"], name="ab_fox_bias")
    kb = cbias.reshape(nb, 8, SEQ // ATT_CHUNK, ATT_CHUNK)
    y_a, hl = _lru_fwd(zug, p["conv_w"], p["conv_b"], p["w_a"], p["b_a"], p["w_i"], p["b_i"], p["lam"], name="ab_lru")
    qkv3 = _seq3(qkv, nb)
    o, lse = _attn_fwd((qkv3, 0), (qkv3, 1), (qkv3, 2), kb, heads=8, dh=HEAD_DIM, mode="causal", name="ab_fox")
    o = _flat2(o)
    y, = _rowwise(lambda a, b: jnp.concatenate([a, b.astype(BF16)], axis=1), [y_a, o], [], [(D_MODEL, BF16)], name="ab_ycat")
    out, = _matmul(y, p["w_out"], extras=(h,), epilogue=lambda acc, res: (acc + res,), name="ab_out")
    return out, (h, hn, zug, qkv3, fl, kb, hl, o, lse, y)


def _ab_bwd(dh, dhb, p, saved, nb):
    h, hn, zug, qkv3, fl, kb, hl, o, lse, y = saved
    dy, dyb = _matmul(dhb, p["w_out"], tb=True, outs=(F32, BF16), epilogue=lambda acc: (acc, acc), name="ab_bwd_dy")
    dw_out, = _matmul(y, dhb, ta=True, name="ab_bwd_dwout")
    dzug, dcw, dcb, dwa, dba, dwi, dbi, dlam = _lru_bwd(zug, hl, dy, p["conv_w"], p["conv_b"], p["w_a"], p["b_a"],
                                                        p["w_i"], p["b_i"], p["lam"], name="ab_bwd_lru")
    delta = _head_delta((dy, ATT_W, 1), o, 8, name="ab_bwd_delta")
    dq, dk, dv, dkb, drow = _attn_bwd((qkv3, 0), (qkv3, 1), (qkv3, 2), (_seq3(dyb, nb), 1), lse, _seq3(delta, nb), kb,
                                      heads=8, dh=HEAD_DIM, mode="causal", name="ab_bwd_fox")
    dcum = dkb.reshape(nb, 8, SEQ) + drow[:, :, :8].transpose(0, 2, 1)
    dfl, dbf = _fox_bias_bwd(fl, p["b_f"], dcum, name="ab_bwd_fox_bias")
    dzf = jnp.pad(dfl.transpose(0, 2, 1).reshape(nb * SEQ, 8), ((0, 0), (0, LANES - 8)))
    dz, = _rowwise(lambda a, q, k, v, f: jnp.concatenate([a, q.astype(BF16), k.astype(BF16), v.astype(BF16), f.astype(BF16)], axis=1),
                   [dzug, _flat2(dq), _flat2(dk), _flat2(dv), dzf], [], [(AB_MAIN + LANES, BF16)], name="ab_bwd_dzcat")
    dw_in_t, = _matmul(dz, hn, ta=True, tm=384, name="ab_bwd_dwin")
    dhn, = _matmul(dz, p["w_in_t"], name="ab_bwd_dhn")
    dh2, dh2b, dg = _rms_bwd(h, dhn, dh, p["norm"], name="ab_bwd_norm")
    grads = {"norm": dg, "w_in_t": dw_in_t[:AB_MAIN + 8], "conv_w": dcw, "conv_b": dcb, "w_a": _diag_blocks(dwa), "b_a": dba,
             "w_i": _diag_blocks(dwi), "b_i": dbi, "lam": dlam, "b_f": dbf.reshape(1, 8), "w_out": dw_out}
    return dh2, dh2b, grads


CD_IN = 3 * ATT_W + ATT_W + 2 * SWA_KW


def _class_view(a, dil):
    return a.reshape(a.shape[0], a.shape[1] // dil, dil * a.shape[2])


def _gqa_share():
    src = jnp.arange(SWA_KW)[:, None]
    dst = jnp.arange(ATT_W)[None, :]
    per_kv = ATT_W // (SWA_KW // HEAD_DIM)
    return ((dst // per_kv == src // HEAD_DIM) & (dst % HEAD_DIM == src % HEAD_DIM)).astype(BF16)


def _cd_fwd(h, p, nb):
    hn = _rms_fwd(h, p["norm"], name="cd_norm")
    z, = _matmul(hn, p["w_in_t"], tb=True, tn=384, name="cd_in")
    cos, sin = _rope_tables()
    per_seq = SEQ // ROW_TILE
    table_map = lambda i: (i % per_seq, 0)

    def rope_fn(z, cos, sin, share):
        qc, kc, vc = z[:, 0:ATT_W], z[:, ATT_W:2 * ATT_W], z[:, 2 * ATT_W:3 * ATT_W]
        qd, kd, vd = z[:, 3 * ATT_W:4 * ATT_W], z[:, 4 * ATT_W:4 * ATT_W + SWA_KW], z[:, 4 * ATT_W + SWA_KW:]
        kd8 = jnp.dot(_rotate(kd, cos, sin).astype(BF16), share, preferred_element_type=F32)
        vd8 = jnp.dot(vd.astype(BF16), share, preferred_element_type=F32)
        return (jnp.concatenate([_rotate(qc, cos, sin), _rotate(kc, cos, sin)], axis=1), vc, _rotate(qd, cos, sin), kd8, vd8)
    qk, vc, qd, kd, vd = _rowwise(rope_fn, [z, cos, sin], [_gqa_share()], [(2 * ATT_W, BF16)] + [(ATT_W, BF16)] * 4,
                                  name="cd_rope", row_maps=[None, table_map, table_map])
    qk3, vc3 = _seq3(qk, nb), _seq3(vc, nb)
    branch = []
    for window, dil in DIL_PATTERN:
        o_i, lse_i = _attn_fwd((_class_view(qk3, dil), 0), (_class_view(qk3, dil), 1), (_class_view(vc3, dil), 0), classes=dil,
                               heads=8, dh=HEAD_DIM, mode="band", max_dist=window // dil, name=f"cd_dil{dil}")
        branch.append((o_i.reshape(nb * SEQ, ATT_W), lse_i.reshape(nb * SEQ, LANES)))
    expand = _head_expand(8, ATT_W)

    def combine(o1, o2, o3, l1, l2, l3, e):
        m = jnp.maximum(jnp.maximum(l1, l2), l3)
        lt = m + jnp.log(jnp.exp(l1 - m) + jnp.exp(l2 - m) + jnp.exp(l3 - m))
        o = sum(_dot_exact(jnp.exp(l - lt), e) * o_ for o_, l in ((o1, l1), (o2, l2), (o3, l3)))
        return o, lt
    y_c, lse_c = _rowwise(combine, [b[0] for b in branch] + [b[1] for b in branch], [expand], [(ATT_W, F32), (LANES, F32)],
                          name="cd_dil_combine")
    qd3, kd3, vd3 = _seq3(qd, nb), _seq3(kd, nb), _seq3(vd, nb)
    o_d, lse_band = _attn_fwd((qd3, 0), (kd3, 0), (vd3, 0), heads=8, dh=HEAD_DIM, mode="band", max_dist=SWA_WINDOW - 1,
                              name="cd_swa")

    def sink_combine(o, l, e, sink):
        lt = jnp.maximum(l, sink) + jnp.log(1.0 + jnp.exp(-jnp.abs(l - sink)))
        return o * _dot_exact(jnp.exp(l - lt), e), lt
    y_d, lse_d = _rowwise(sink_combine, [_flat2(o_d), _flat2(lse_band)], [expand, p["sink"]], [(ATT_W, F32), (LANES, F32)],
                          name="cd_swa_sink")
    y, = _rowwise(lambda a, b: jnp.concatenate([a, b], axis=1), [y_c, y_d], [], [(D_MODEL, BF16)], name="cd_ycat")
    out, = _matmul(y, p["w_out"], extras=(h,), epilogue=lambda acc, res: (acc + res,), name="cd_out")
    return out, (h, hn, qk3, vc3, qd3, kd3, vd3, y_c, lse_c, y_d, lse_d, y)


def _cd_bwd(dh, dhb, p, saved, nb):
    h, hn, qk3, vc3, qd3, kd3, vd3, y_c, lse_c, y_d, lse_d, y = saved
    dy, dyb = _matmul(dhb, p["w_out"], tb=True, outs=(F32, BF16), epilogue=lambda acc: (acc, acc), name="cd_bwd_dy")
    dw_out, = _matmul(y, dhb, ta=True, name="cd_bwd_dwout")
    dyb3 = _seq3(dyb, nb)
    delta_c = _head_delta((dy, ATT_W, 0), y_c, 8, name="cd_bwd_delta_dil")
    lse_c3, delta_c3 = _seq3(lse_c, nb), _seq3(delta_c, nb)
    parts = []
    for window, dil in DIL_PATTERN:
        cv = functools.partial(_class_view, dil=dil)
        dq_i, dk_i, dv_i = _attn_bwd((cv(qk3), 0), (cv(qk3), 1), (cv(vc3), 0), (cv(dyb3), 0), cv(lse_c3), cv(delta_c3),
                                     classes=dil, heads=8, dh=HEAD_DIM, mode="band", max_dist=window // dil,
                                     name=f"cd_bwd_dil{dil}")
        parts.append([a.reshape(nb * SEQ, ATT_W) for a in (dq_i, dk_i, dv_i)])
    delta_d, dsink = _head_delta((dy, ATT_W, 1), y_d, 8, lse=lse_d, sink=p["sink"], name="cd_bwd_delta_swa")
    dqd, dkd, dvd = _attn_bwd((qd3, 0), (kd3, 0), (vd3, 0), (dyb3, 1), _seq3(lse_d, nb), _seq3(delta_d, nb), heads=8,
                              dh=HEAD_DIM, mode="band", max_dist=SWA_WINDOW - 1, name="cd_bwd_swa")
    cos, sin = _rope_tables()
    per_seq = SEQ // ROW_TILE
    table_map = lambda i: (i % per_seq, 0)

    def unrope(q1, q2, q3, k1, k2, k3, v1, v2, v3, qd, kd8, vd8, cos, sin, gather):
        back = lambda x: _rotate(x, cos, -sin)
        kd, vd = _dot_exact(kd8, gather), _dot_exact(vd8, gather)
        return jnp.concatenate([back(q1 + q2 + q3), back(k1 + k2 + k3), v1 + v2 + v3, back(qd), back(kd), vd], axis=1)
    ins = [parts[b][t] for t in range(3) for b in range(3)] + [_flat2(dqd), _flat2(dkd), _flat2(dvd), cos, sin]
    dz, = _rowwise(unrope, ins, [_gqa_share().T], [(CD_IN, BF16)], name="cd_bwd_unrope",
                   row_maps=[None] * 12 + [table_map, table_map])
    dw_in_t, = _matmul(dz, hn, ta=True, tm=384, name="cd_bwd_dwin")
    dhn, = _matmul(dz, p["w_in_t"], name="cd_bwd_dhn")
    dh2, dh2b, dg = _rms_bwd(h, dhn, dh, p["norm"], name="cd_bwd_norm")
    return dh2, dh2b, {"norm": dg, "w_in_t": dw_in_t, "sink": dsink[:, :8], "w_out": dw_out}


def _local_step(x, mem, target, w):
    nb = x.shape[0]
    h = x.reshape(nb * SEQ, D_MODEL)
    mem2 = mem.reshape(nb * MEM_LEN, D_MODEL)
    tgt = target.reshape(nb * SEQ, D_MODEL)

    h, s_ab = _ab_fwd(h, w["ab"], nb)
    h, s_xa0 = _xa_fwd(h, mem2, w["xa0"], "xa0", nb)
    h, s_mlp0 = _mlp_fwd(h, w["mlp0"], "mlp0")
    h, s_cd = _cd_fwd(h, w["cd"], nb)
    h, s_xa1 = _xa_fwd(h, mem2, w["xa1"], "xa1", nb)
    h, s_mlp1 = _mlp_fwd(h, w["mlp1"], "mlp1")

    dh, dhb, loss, dg_final = _loss_and_grad(h, tgt, w["final_norm"], name="loss")
    grads = {"final_norm": dg_final}
    dh, dhb, grads["mlp1"] = _mlp_bwd(dh, dhb, w["mlp1"], s_mlp1, "mlp1")
    dh, dhb, grads["xa1"] = _xa_bwd(dh, dhb, mem2, w["xa1"], s_xa1, "xa1", nb)
    dh, dhb, grads["cd"] = _cd_bwd(dh, dhb, w["cd"], s_cd, nb)
    dh, dhb, grads["mlp0"] = _mlp_bwd(dh, dhb, w["mlp0"], s_mlp0, "mlp0")
    dh, dhb, grads["xa0"] = _xa_bwd(dh, dhb, mem2, w["xa0"], s_xa0, "xa0", nb)
    dh, dhb, grads["ab"] = _ab_bwd(dh, dhb, w["ab"], s_ab, nb)
    return loss, dh.reshape(nb, SEQ, D_MODEL), grads


ANY = pl.BlockSpec(memory_space=pl.ANY)


def _place():
    x, y, c = lax.axis_index("x"), lax.axis_index("y"), lax.axis_index("c")
    return x, y, c, [(1 - x, y), (x, 1 - y), (1 - x, 1 - y)]


def _gather_chips(shard):
    half = shard.shape[0] // 2

    def body(src, out, send_sems, recv_sems):
        x, y, c, chips = _place()
        sibling = (x, y, 1 - c)

        def rows(slot, core):
            return out.at[slot, pl.ds(core * half, half)]

        def copy(k, src_ref, dst_ref, to):
            return pltpu.make_async_remote_copy(src_ref=src_ref, dst_ref=dst_ref, send_sem=send_sems.at[k],
                                                recv_sem=recv_sems.at[k], device_id=to, device_id_type=MESH)

        first = [copy(k, src.at[pl.ds(c * half, half)], rows(2 * x + y, c), (px, py, c)) for k, (px, py) in enumerate(chips)]
        for cp in first:
            cp.start()
        passed = [copy(3 + k, rows(2 * px + py, c), rows(2 * px + py, c), sibling) for k, (px, py) in enumerate(chips)]
        for k, (px, py) in enumerate(chips):
            copy(k, src.at[pl.ds(c * half, half)], rows(2 * px + py, c), (px, py, c)).wait_recv()
            passed[k].start()
        for k, (px, py) in enumerate(chips):
            copy(3 + k, rows(2 * px + py, 1 - c), rows(2 * px + py, 1 - c), sibling).wait_recv()
        for cp in first + passed:
            cp.wait_send()

    return pl.pallas_call(
        body, out_shape=jax.ShapeDtypeStruct((N_CHIPS,) + shard.shape, shard.dtype), in_specs=[ANY], out_specs=ANY,
        scratch_shapes=[pltpu.SemaphoreType.DMA((6,)), pltpu.SemaphoreType.DMA((6,))], name="gather_chips",
    )(shard)


def _swap_cores(block, *, name, half_rows=None):
    shape = block.shape if half_rows is None else (block.shape[0], half_rows, block.shape[2])

    def body(src, out, send_sem, recv_sem):
        x, y, c, _ = _place()
        part = src if half_rows is None else src.at[:, pl.ds((1 - c) * half_rows, half_rows)]
        cp = pltpu.make_async_remote_copy(src_ref=part, dst_ref=out, send_sem=send_sem, recv_sem=recv_sem,
                                          device_id=(x, y, 1 - c), device_id_type=MESH)
        cp.start()
        cp.wait()

    return pl.pallas_call(
        body, out_shape=jax.ShapeDtypeStruct(shape, block.dtype), in_specs=[ANY], out_specs=ANY,
        scratch_shapes=[pltpu.SemaphoreType.DMA(()), pltpu.SemaphoreType.DMA(())], name=name,
    )(block)


def _scatter_chips(parts):
    def body(src, out, send_sems, recv_sems):
        x, y, c, chips = _place()
        sends = [pltpu.make_async_remote_copy(src_ref=src.at[2 * px + py], dst_ref=out.at[k], send_sem=send_sems.at[k],
                                              recv_sem=recv_sems.at[k], device_id=(px, py, c), device_id_type=MESH)
                 for k, (px, py) in enumerate(chips)]
        for cp in sends:
            cp.start()
        for cp in sends:
            cp.wait_recv()
        for cp in sends:
            cp.wait_send()

    return pl.pallas_call(
        body, out_shape=jax.ShapeDtypeStruct((3,) + parts.shape[1:], parts.dtype), in_specs=[ANY], out_specs=ANY,
        scratch_shapes=[pltpu.SemaphoreType.DMA((3,)), pltpu.SemaphoreType.DMA((3,))], name="scatter_chips",
    )(parts)


def _sum_all_devices(vec):
    rows = vec.shape[0]

    def body(x_ref, total_ref, all_ref, send_sems, recv_sems, local_sem):
        x, y, c, chips = _place()
        me, sibling = (x, y, c), (x, y, 1 - c)

        def slot(px, py, pc):
            return all_ref.at[4 * px + 2 * py + pc]

        def copy(k, block, to, src=None):
            return pltpu.make_async_remote_copy(src_ref=slot(*block) if src is None else src, dst_ref=slot(*block),
                                                send_sem=send_sems.at[k], recv_sem=recv_sems.at[k], device_id=to,
                                                device_id_type=MESH)

        mine = pltpu.make_async_copy(x_ref, slot(*me), local_sem)
        mine.start()
        first = [copy(0, me, sibling, src=x_ref)] + [copy(1 + j, me, (*chip, c), src=x_ref) for j, chip in enumerate(chips)]
        for cp in first:
            cp.start()
        passed = [copy(4 + j, (*chip, c), sibling) for j, chip in enumerate(chips)]
        for j, chip in enumerate(chips):
            copy(1 + j, (*chip, c), me).wait_recv()
            passed[j].start()
        copy(0, sibling, me).wait_recv()
        for j, chip in enumerate(chips):
            copy(4 + j, (*chip, 1 - c), me).wait_recv()
        for cp in first + passed:
            cp.wait_send()
        mine.wait()
        acc = all_ref[0]
        for d in range(1, 8):
            acc = acc + all_ref[d]
        total_ref[...] = acc

    vmem = pl.BlockSpec(memory_space=pltpu.VMEM)
    return pl.pallas_call(
        body, out_shape=[jax.ShapeDtypeStruct((rows, LANES), F32), jax.ShapeDtypeStruct((8, rows, LANES), F32)],
        in_specs=[vmem], out_specs=[vmem, vmem],
        scratch_shapes=[pltpu.SemaphoreType.DMA((7,)), pltpu.SemaphoreType.DMA((7,)), pltpu.SemaphoreType.DMA(())],
        name="sum_all_devices", compiler_params=pltpu.CompilerParams(vmem_limit_bytes=VMEM_LIMIT_BYTES),
    )(vec)[0]


PACK_COLS = 1024
BIG = (("mlp_w_up", 2, 1024, True), ("mlp_w_down", 2, 1024, False), ("xa_w_kv", 2, 512, True), ("xa_w_q", 2, 256, False),
       ("xa_w_o", 2, 256, False), ("ab_w_out", 1, 256, False), ("cd_w_out", 1, 256, False), ("cd_w_in", 1, 576, True),
       ("ab_w_in", 1, 642, True))
ENTRIES = tuple((name, layer, rows, transposed) for name, layers, rows, transposed in BIG for layer in range(layers))
PACK_ROWS = -(-sum(e[2] for e in ENTRIES) // 32) * 32
REDUCE_TILE = 208
assert (PACK_ROWS // 2) % REDUCE_TILE == 0

SMALL = (("ab_norm", (1, 1024)), ("ab_conv_w", (1, 4, 512)), ("ab_conv_b", (1, 512)), ("lru_w_a", (1, 8, 64, 64)),
         ("lru_b_a", (1, 512)), ("lru_w_i", (1, 8, 64, 64)), ("lru_b_i", (1, 512)), ("lru_lambda", (1, 512)),
         ("fox_b_f", (1, 8)), ("cd_norm", (1, 1024)), ("cd_sink", (1, 8)), ("xa_norm", (2, 1024)),
         ("xa_mem_norm", (2, 1024)), ("mlp_norm", (2, 1024)), ("final_norm", (1024,)), ("loss", (1,)))
SPLIT_SMALL = ("ab_conv_w", "cd_norm")


def _pack_rows(parts, dtype):
    lead = parts[0].shape[:-2]
    pad = jnp.zeros(lead + (PACK_ROWS - sum(e[2] for e in ENTRIES), PACK_COLS), dtype)
    return jnp.concatenate([p.astype(dtype) for p in parts] + [pad], axis=len(lead))


def _unpack_rows(packed):
    out, r0 = [], 0
    for _, _, rows, _ in ENTRIES:
        out.append(packed[..., r0:r0 + rows, :])
        r0 += rows
    return out


def _shard_rows(w):
    return [(w[name][layer].T if transposed else w[name][layer]) for name, layer, _, transposed in ENTRIES]


def _shard_blocks(rows):
    out = {}
    for (name, layer, _, transposed), r in zip(ENTRIES, rows):
        out.setdefault(name, []).append(r.T if transposed else r)
    return {name: jnp.stack(layers) for name, layers in out.items()}


def _pack_small(vals):
    flat = jnp.concatenate([vals[name].astype(F32).reshape(-1) for name, _ in SMALL])
    size = -(-flat.shape[0] // (8 * LANES)) * (8 * LANES)
    return jnp.pad(flat, (0, size - flat.shape[0])).reshape(-1, LANES)


def _unpack_small(packed):
    flat, out, at = packed.reshape(-1), {}, 0
    for name, shape in SMALL:
        out[name] = flat[at:at + math.prod(shape)].reshape(shape)
        at += math.prod(shape)
    return out


def _chip_part(full, chip):
    width = full.shape[-1] // N_CHIPS
    return lax.dynamic_slice_in_dim(full, chip * width, width, axis=full.ndim - 1)


def _chip_embed(part, chip):
    full = jnp.zeros(part.shape[:-1] + (part.shape[-1] * N_CHIPS,), part.dtype)
    return lax.dynamic_update_slice_in_dim(full, part, chip * part.shape[-1], axis=part.ndim - 1)


SUBLAYER_KEYS = {"ab_w_in": ("ab", "w_in_t"), "ab_w_out": ("ab", "w_out"), "cd_w_in": ("cd", "w_in_t"), "cd_w_out": ("cd", "w_out"),
                 "xa_w_q": ("xa", "w_q"), "xa_w_kv": ("xa", "w_kv_t"), "xa_w_o": ("xa", "w_o"), "mlp_w_up": ("mlp", "w_up_t"),
                 "mlp_w_down": ("mlp", "w_down")}


def _sublayer(name, layer):
    block, leaf = SUBLAYER_KEYS[name]
    return (block if block in ("ab", "cd") else f"{block}{layer}"), leaf


def _build_params(full_rows, w):
    pad = lambda a: jnp.pad(a, ((0, 0), (0, LANES - a.shape[1])))
    params = {
        "ab": dict(norm=w["ab_norm"], conv_w=w["ab_conv_w"][0], conv_b=w["ab_conv_b"], w_a=_block_diag(w["lru_w_a"][0]),
                   b_a=w["lru_b_a"], w_i=_block_diag(w["lru_w_i"][0]), b_i=w["lru_b_i"], lam=w["lru_lambda"],
                   b_f=w["fox_b_f"].reshape(8, 1)),
        "cd": dict(norm=w["cd_norm"], sink=pad(w["cd_sink"])),
        "final_norm": w["final_norm"].reshape(1, D_MODEL),
    }
    for layer in range(2):
        params[f"xa{layer}"] = dict(norm=w["xa_norm"][layer:layer + 1], mem_norm=w["xa_mem_norm"][layer:layer + 1])
        params[f"mlp{layer}"] = dict(norm=w["mlp_norm"][layer:layer + 1])
    for name, layer, _, _ in ENTRIES:
        block, leaf = _sublayer(name, layer)
        params[block][leaf] = full_rows[name, layer]
    w_in_t = params["ab"]["w_in_t"]
    params["ab"]["w_in_t"] = jnp.concatenate([w_in_t, jnp.zeros((LANES - 8, PACK_COLS), w_in_t.dtype)])
    return params


def _grad_rows(g):
    out = []
    for name, layer, _, _ in ENTRIES:
        block, leaf = _sublayer(name, layer)
        out.append(g[block][leaf])
    return out


def _reduce_to_owner(grads_by_chip, core):
    half = PACK_ROWS // 2
    steps = half // REDUCE_TILE
    chip = 2 * lax.axis_index("x") + lax.axis_index("y")
    place = jnp.stack([core, chip]).astype(jnp.int32)
    got = _swap_cores(grads_by_chip, name="swap_cores", half_rows=half)
    block = (1, REDUCE_TILE, PACK_COLS)

    def sum_cores(place_ref, mine_ref, got_ref, f32_ref, bf16_ref):
        total = mine_ref[...] + got_ref[...]
        f32_ref[...] = total
        bf16_ref[...] = total.astype(BF16)

    pair32, pair16 = pl.pallas_call(
        sum_cores, name="sum_cores",
        grid_spec=pltpu.PrefetchScalarGridSpec(
            num_scalar_prefetch=1, grid=(N_CHIPS, steps),
            in_specs=[pl.BlockSpec(block, lambda s, i, place_ref: (s, place_ref[0] * steps + i, 0)),
                      pl.BlockSpec(block, lambda s, i, place_ref: (s, i, 0))],
            out_specs=[pl.BlockSpec(block, lambda s, i, place_ref: (s, i, 0))] * 2),
        out_shape=[jax.ShapeDtypeStruct((N_CHIPS, half, PACK_COLS), F32), jax.ShapeDtypeStruct((N_CHIPS, half, PACK_COLS), BF16)],
        compiler_params=_params("arbitrary", "arbitrary"),
    )(place, grads_by_chip, got)
    landed = _scatter_chips(pair16)

    def sum_chips(place_ref, own_ref, landed_ref, out_ref):
        out_ref[...] = own_ref[0] + landed_ref[0].astype(F32) + landed_ref[1].astype(F32) + landed_ref[2].astype(F32)

    done = pl.pallas_call(
        sum_chips, name="sum_chips",
        grid_spec=pltpu.PrefetchScalarGridSpec(
            num_scalar_prefetch=1, grid=(steps,),
            in_specs=[pl.BlockSpec(block, lambda i, place_ref: (place_ref[1], i, 0)),
                      pl.BlockSpec((3, REDUCE_TILE, PACK_COLS), lambda i, place_ref: (0, i, 0))],
            out_specs=pl.BlockSpec(block[1:], lambda i, place_ref: (i, 0))),
        out_shape=jax.ShapeDtypeStruct((half, PACK_COLS), F32), compiler_params=_params("arbitrary"),
    )(place, pair32, landed)
    theirs = _swap_cores(done, name="share_halves")
    return jnp.where(core == 0, jnp.concatenate([done, theirs]), jnp.concatenate([theirs, done]))


def kernel(x, mem, ab_norm, ab_w_in, ab_conv_w, ab_conv_b, lru_w_a, lru_b_a, lru_w_i, lru_b_i, lru_lambda, fox_b_f, ab_w_out, cd_norm, cd_w_in, cd_sink, cd_w_out, xa_norm, xa_mem_norm, xa_w_q, xa_w_kv, xa_w_o, mlp_norm, mlp_w_up, mlp_w_down, final_norm, loss_target, m_ab_norm, m_ab_w_in, m_ab_conv_w, m_ab_conv_b, m_lru_w_a, m_lru_b_a, m_lru_w_i, m_lru_b_i, m_lru_lambda, m_fox_b_f, m_ab_w_out, m_cd_norm, m_cd_w_in, m_cd_sink, m_cd_w_out, m_xa_norm, m_xa_mem_norm, m_xa_w_q, m_xa_w_kv, m_xa_w_o, m_mlp_norm, m_mlp_w_up, m_mlp_w_down, m_final_norm, v_ab_norm, v_ab_w_in, v_ab_conv_w, v_ab_conv_b, v_lru_w_a, v_lru_b_a, v_lru_w_i, v_lru_b_i, v_lru_lambda, v_fox_b_f, v_ab_w_out, v_cd_norm, v_cd_w_in, v_cd_sink, v_cd_w_out, v_xa_norm, v_xa_mem_norm, v_xa_w_q, v_xa_w_kv, v_xa_w_o, v_mlp_norm, v_mlp_w_up, v_mlp_w_down, v_final_norm):
    given = dict(locals())
    weight_names = [name for name, _ in SMALL if name != "loss"] + [name for name, _, _, _ in BIG]
    w = {name: given[name] for name in weight_names}
    chip = 2 * lax.axis_index("x") + lax.axis_index("y")
    core = lax.axis_index("c")

    mine = _pack_rows(_shard_rows(w), BF16)
    gathered = _gather_chips(mine)
    slot = lax.broadcasted_iota(jnp.int32, (N_CHIPS, 1, 1), 0)
    full_rows = {}
    for (name, layer, rows, _), own, got in zip(ENTRIES, _unpack_rows(mine), _unpack_rows(gathered)):
        full_rows[name, layer] = jnp.where(slot == chip, own[None], got).reshape(N_CHIPS * rows, PACK_COLS)
    owner = (core == 0).astype(F32)
    quarters = {name: _chip_embed(w[name], chip) * owner for name in SPLIT_SMALL}
    zeros = {name: jnp.zeros(shape, F32) for name, shape in SMALL}
    small_full = _unpack_small(_sum_all_devices(_pack_small({**zeros, **quarters})))
    params = _build_params(full_rows, {**w, "ab_conv_w": small_full["ab_conv_w"], "cd_norm": small_full["cd_norm"]})

    loss_part, grad_x, g = _local_step(x, mem, loss_target, params)

    by_chip = _pack_rows([r.reshape(N_CHIPS, rows, PACK_COLS) for r, (_, _, rows, _) in zip(_grad_rows(g), ENTRIES)], F32)
    grads = _shard_blocks(_unpack_rows(_reduce_to_owner(by_chip, core)))
    pair = lambda key, leaf: jnp.stack([g[f"{key}0"][leaf], g[f"{key}1"][leaf]])

    small_local = {"ab_norm": g["ab"]["norm"], "ab_conv_w": g["ab"]["conv_w"], "ab_conv_b": g["ab"]["conv_b"],
                   "lru_w_a": g["ab"]["w_a"], "lru_b_a": g["ab"]["b_a"], "lru_w_i": g["ab"]["w_i"], "lru_b_i": g["ab"]["b_i"],
                   "lru_lambda": g["ab"]["lam"], "fox_b_f": g["ab"]["b_f"], "cd_norm": g["cd"]["norm"], "cd_sink": g["cd"]["sink"],
                   "xa_norm": pair("xa", "norm"), "xa_mem_norm": pair("xa", "mem_norm"), "mlp_norm": pair("mlp", "norm"),
                   "final_norm": g["final_norm"], "loss": loss_part[0, :1]}
    small_sum_packed = _sum_all_devices(_pack_small(small_local))
    small_sum = _unpack_small(small_sum_packed)
    loss = small_sum["loss"][0]

    delta, new_m, new_v = {}, {}, {}
    for name, _, _, _ in BIG:
        shape = w[name].shape
        flat = lambda a: a.reshape(-1, shape[-1])
        d, m2, v2 = _adamw(flat(w[name]), flat(grads[name]), flat(given["m_" + name]), flat(given["v_" + name]), name=f"adamw_{name}")
        delta[name], new_m[name], new_v[name] = d.reshape(shape), m2.reshape(shape), v2.reshape(shape)

    def small_state(prefix):
        vals = {name: (given[prefix + name] if name != "loss" else jnp.zeros((1,), F32)) for name, _ in SMALL}
        for name in SPLIT_SMALL:
            vals[name] = _chip_embed(vals[name], chip)
        return _pack_small(vals)
    packed = _adamw(small_state(""), small_sum_packed, small_state("m_"), small_state("v_"), name="adamw_small")
    for store, vals in zip((delta, new_m, new_v), packed):
        for name, val in _unpack_small(vals).items():
            store[name] = _chip_part(val, chip) if name in SPLIT_SMALL else val
    for name in SPLIT_SMALL:
        small_sum[name] = _chip_part(small_sum[name], chip)
    grads.update({name: small_sum[name] for name, _ in SMALL})

    order = ["ab_norm", "ab_w_in", "ab_conv_w", "ab_conv_b", "lru_w_a", "lru_b_a", "lru_w_i", "lru_b_i", "lru_lambda", "fox_b_f",
             "ab_w_out", "cd_norm", "cd_w_in", "cd_sink", "cd_w_out", "xa_norm", "xa_mem_norm", "xa_w_q", "xa_w_kv", "xa_w_o",
             "mlp_norm", "mlp_w_up", "mlp_w_down", "final_norm"]
    return (loss, grad_x, *[grads[n] for n in order], *[delta[n] for n in order], *[new_m[n] for n in order],
            *[new_v[n] for n in order])
```

```python
import functools
import math

import jax
import jax.numpy as jnp
from jax import lax
from jax.experimental import pallas as pl
from jax.experimental.pallas import tpu as pltpu

F32 = jnp.float32
BF16 = jnp.bfloat16
MESH = pl.DeviceIdType.MESH

D_MODEL = 1024
SEQ = 2048
HEAD_DIM = 64
LRU_WIDTH = 512
LRU_BLOCKS = 8
LRU_C = 8.0
CONV_WIDTH = 4
ATT_W = 512
SWA_KW = 128
SWA_WINDOW = 128
DIL_PATTERN = ((128, 1), (512, 4), (2048, 16))
MEM_LEN = 256
XA_HEADS = 4
XA_HEAD_DIM = 256
D_FF = 4096
ROPE_THETA = 10000.0
EPS = 1e-6
N_CHIPS = 4
LANES = 128

ADAM_LR, ADAM_B1, ADAM_B2, ADAM_EPS, ADAM_WD, ADAM_STEP = 0.001, 0.9, 0.999, 1e-08, 0.01, 10

VMEM_LIMIT_BYTES = 56 * 1024 * 1024
MASKED = -1e30
ROW_TILE = 512
LRU_CHUNK = 512
ATT_BLOCK = 128
BAND_ROWS = 256
ATT_CHUNK = 512
CAUSAL_ROWS = 256


def _params(*sem):
    return pltpu.CompilerParams(dimension_semantics=sem, vmem_limit_bytes=VMEM_LIMIT_BYTES)


def _rowwise(fn, rows, consts=(), out_rows=(), out_accs=(), *, name, tile=ROW_TILE, row_maps=None):
    rows = [r if isinstance(r, tuple) else (r, r.shape[1], 0) for r in rows]
    consts = list(consts)
    nr, nc, no = len(rows), len(consts), len(out_rows)
    total = rows[0][0].shape[0]
    tile = min(tile, total)
    assert total % tile == 0
    n = total // tile

    def body(*refs):
        outs = fn(*[r[...] for r in refs[:nr + nc]])
        outs = tuple(outs) if isinstance(outs, (tuple, list)) else (outs,)
        for ref, val in zip(refs[nr + nc:nr + nc + no], outs[:no]):
            ref[...] = val.astype(ref.dtype)
        for ref, val in zip(refs[nr + nc + no:], outs[no:]):
            @pl.when(pl.program_id(0) == 0)
            def _(ref=ref):
                ref[...] = jnp.zeros(ref.shape, ref.dtype)
            ref[...] += val

    in_specs = []
    for idx, (arr, width, cb) in enumerate(rows):
        if row_maps is not None and row_maps[idx] is not None:
            in_specs.append(pl.BlockSpec((tile, width), row_maps[idx]))
        else:
            in_specs.append(pl.BlockSpec((tile, width), functools.partial(lambda i, cb: (i, cb), cb=cb)))
    in_specs += [pl.BlockSpec(c.shape, lambda i: (0, 0)) for c in consts]
    out_shape = [jax.ShapeDtypeStruct((total, w), dt) for w, dt in out_rows]
    out_shape += [jax.ShapeDtypeStruct(s, F32) for s in out_accs]
    out_specs = [pl.BlockSpec((tile, w), lambda i: (i, 0)) for w, _ in out_rows]
    out_specs += [pl.BlockSpec(s, lambda i: (0, 0)) for s in out_accs]
    return pl.pallas_call(
        body, grid=(n,), in_specs=in_specs, out_specs=out_specs, out_shape=out_shape, name=name,
        compiler_params=_params("arbitrary"),
    )(*[r[0] for r in rows], *consts)


MATMUL_VMEM_BYTES = 40 * 1024 * 1024


def _matmul_tiles(m, n, k, tm, tn, out_bytes):
    tm, tn, tk = min(tm, m), min(tn, n), k

    def need():
        return 2 * (2 * tk * (tm + tn) + tm * tn * out_bytes) + (0 if tk == k else 4 * tm * tn)

    while need() > MATMUL_VMEM_BYTES:
        if tm >= tn and tm % 256 == 0:
            tm //= 2
        elif tn % 256 == 0:
            tn //= 2
        else:
            tk //= 2
    return tm, tn, tk


def _matmul(a, b, *, ta=False, tb=False, outs=(F32,), epilogue=None, extras=(), tm=1024, tn=1024, name):
    m, k = (a.shape[1], a.shape[0]) if ta else a.shape
    n = b.shape[0] if tb else b.shape[1]
    assert (b.shape[1] if tb else b.shape[0]) == k
    out_bytes = sum(jnp.dtype(dt).itemsize for dt in outs) + sum(e.dtype.itemsize for e in extras)
    tm, tn, tk = _matmul_tiles(m, n, k, tm, tn, out_bytes)
    assert m % tm == 0 and n % tn == 0 and k % tk == 0, (name, m, n, k)
    nk = k // tk
    ne, no = len(extras), len(outs)
    dims = (((0 if ta else 1,), (1 if tb else 0,)), ((), ()))

    def body(*refs):
        a_ref, b_ref = refs[:2]
        e_refs, o_refs = refs[2:2 + ne], refs[2 + ne:2 + ne + no]

        def finish(acc):
            vals = (acc,) if epilogue is None else epilogue(acc, *[e[...] for e in e_refs])
            for ref, val in zip(o_refs, vals):
                ref[...] = val.astype(ref.dtype)

        prod = lax.dot_general(a_ref[...], b_ref[...], dims, preferred_element_type=F32)
        if nk == 1:
            finish(prod)
        else:
            acc_ref = refs[-1]
            step = pl.program_id(2)

            @pl.when(step == 0)
            def _():
                acc_ref[...] = prod

            @pl.when(step > 0)
            def _():
                acc_ref[...] += prod

            @pl.when(step == nk - 1)
            def _():
                finish(acc_ref[...])

    a_spec = pl.BlockSpec((tk, tm), lambda i, j, s: (s, i)) if ta else pl.BlockSpec((tm, tk), lambda i, j, s: (i, s))
    b_spec = pl.BlockSpec((tn, tk), lambda i, j, s: (j, s)) if tb else pl.BlockSpec((tk, tn), lambda i, j, s: (s, j))
    tile_spec = pl.BlockSpec((tm, tn), lambda i, j, s: (i, j))
    return pl.pallas_call(
        body, grid=(m // tm, n // tn, nk),
        in_specs=[a_spec, b_spec] + [tile_spec] * ne,
        out_specs=[tile_spec] * no,
        out_shape=[jax.ShapeDtypeStruct((m, n), dt) for dt in outs],
        scratch_shapes=[pltpu.VMEM((tm, tn), F32)] if nk > 1 else [],
        name=name, compiler_params=_params("parallel", "parallel", "arbitrary"),
    )(a, b, *extras)


def _split3(x):
    x1 = x.astype(BF16)
    r1 = x - x1.astype(F32)
    x2 = r1.astype(BF16)
    x3 = (r1 - x2.astype(F32)).astype(BF16)
    return x1, x2, x3


def _dot_exact(x, e):
    return sum(jnp.dot(p, e, preferred_element_type=F32) for p in _split3(x))


def _head_expand(heads, width):
    dh = width // heads
    rows = jnp.arange(LANES)[:, None]
    cols = jnp.arange(width)[None, :]
    return (cols // dh == rows).astype(BF16)


def _rstd(x):
    return lax.rsqrt(jnp.mean(x * x, axis=-1, keepdims=True) + EPS)


def _rms_fwd(x, gain, *, name):
    return _rowwise(lambda x, g: x * _rstd(x) * g, [x], [gain], [(x.shape[1], BF16)], name=name)[0]


def _rms_bwd_vals(x, dhn, gain):
    r = _rstd(x)
    xh = x * r
    dxh = dhn * gain
    dx = r * (dxh - xh * jnp.mean(dxh * xh, axis=-1, keepdims=True))
    return dx, jnp.sum(dhn * xh, axis=0, keepdims=True)


def _rms_bwd(x, dhn, dres, gain, *, name):
    def fn(x, dhn, dres, g):
        dx, dg = _rms_bwd_vals(x, dhn, g)
        dh = dres + dx
        return dh, dh, dg
    d = x.shape[1]
    return _rowwise(fn, [x, dhn, dres], [gain], [(d, F32), (d, BF16)], [(1, d)], name=name)


def _loss_and_grad(h, target, gain, *, name):
    def fn(h, tgt, g):
        r = _rstd(h)
        err = h * r * g - tgt
        loss = 0.5 * jnp.sum(jnp.mean(err * err, axis=-1, keepdims=True), axis=0, keepdims=True)
        dx, dg = _rms_bwd_vals(h, err * (1.0 / D_MODEL), g)
        return dx, dx, jnp.broadcast_to(loss, (1, LANES)), dg
    d = h.shape[1]
    return _rowwise(fn, [h, target], [gain], [(d, F32), (d, BF16)], [(1, LANES), (1, d)], name=name)


def _adamw(w, g, m, v, *, name):
    rows, cols = w.shape
    tile = rows
    for cand in (256, 128, 64, 32, 16, 8):
        if rows % cand == 0 and rows > cand:
            tile = cand
            break
    bc1 = 1.0 - ADAM_B1 ** ADAM_STEP
    bc2 = 1.0 - ADAM_B2 ** ADAM_STEP

    def fn(w, g, m, v):
        m2 = ADAM_B1 * m + (1.0 - ADAM_B1) * g
        v2 = ADAM_B2 * v + (1.0 - ADAM_B2) * (g * g)
        delta = -ADAM_LR * ((m2 / bc1) / (jnp.sqrt(v2 / bc2) + ADAM_EPS) + ADAM_WD * w)
        return delta, m2, v2
    return _rowwise(fn, [w, g, m, v], [], [(cols, F32)] * 3, name=name, tile=tile)


def _attn_specs(arr, width, cb, classes, rows_block, whole):
    ncols = arr.shape[2] // (classes * width)
    if whole:
        return pl.BlockSpec((1, arr.shape[1], width), lambda n, r, i: (n, 0, r * ncols + cb))
    return pl.BlockSpec((1, rows_block, width), lambda n, r, i: (n, i, r * ncols + cb))


def _attn_tiles(mode, lq, lk):
    if mode == "full":
        return min(lq, 256), lk
    if mode == "band":
        tq = min(BAND_ROWS, lq)
        return tq, min(tq + ATT_BLOCK, lk)
    return CAUSAL_ROWS, ATT_CHUNK


def _window(mode, i, j, tq, win, lk):
    if mode == "causal":
        return pl.multiple_of(j * win, win), (i * tq) // win + 1
    if mode == "band":
        return pl.multiple_of(jnp.clip(i * tq + tq - win, 0, lk - win), ATT_BLOCK), 1
    return 0, 1


def _allowed(mode, i, start, tq, win, max_dist):
    if mode == "full":
        return None
    dist = (i * tq + lax.broadcasted_iota(jnp.int32, (tq, win), 0)) - (start + lax.broadcasted_iota(jnp.int32, (tq, win), 1))
    ok = dist >= 0
    if max_dist is not None:
        ok = ok & (dist <= max_dist)
    return ok


def _head_groups(heads, dh):
    gw = max(LANES, dh)
    return gw, gw // dh, heads // (gw // dh)


def _own_lanes(rows, gw, dh, u):
    return lax.broadcasted_iota(jnp.int32, (rows, gw), 1) // dh == u


def _only_head(x, own, per, u):
    return x if per == 1 else jnp.where(own[u], x, jnp.zeros_like(x))


def _step_scores(qz, kw, kb_row, ok, scale):
    s = lax.dot_general(qz, kw, (((1,), (1,)), ((), ())), preferred_element_type=F32) * scale
    if kb_row is not None:
        s = s - kb_row
    if ok is not None:
        s = jnp.where(ok, s, MASKED)
    return s


def _attn_fwd(q, k, v, kb=None, *, classes=1, heads, dh, mode, max_dist=None, name):
    (qa, qc), (ka, kc), (va, vc) = q, k, v
    n, lq, lk = qa.shape[0], qa.shape[1], ka.shape[1]
    width = heads * dh
    tq, win = _attn_tiles(mode, lq, lk)
    gw, per, groups = _head_groups(heads, dh)
    scale = dh ** -0.5
    has_kb = kb is not None
    single = mode != "causal"

    def body(*refs):
        q_ref, k_ref, v_ref = refs[:3]
        kb_ref = refs[3] if has_kb else None
        o_ref, lse_ref = refs[-2:]
        i = pl.program_id(2)
        lane = lax.broadcasted_iota(jnp.int32, (tq, LANES), 1)
        own = [_own_lanes(tq, gw, dh, u) for u in range(per)]

        def step(j, carry):
            m_in, l_in = carry
            m_out, l_out = m_in, l_in
            start, _ = _window(mode, i, j, tq, win, lk)
            ok = _allowed(mode, i, start, tq, win, max_dist)
            for g in range(groups):
                cols = slice(g * gw, (g + 1) * gw)
                qg = q_ref[0, :, cols]
                kw = k_ref[0, pl.ds(start, win), cols]
                vw = v_ref[0, pl.ds(start, win), cols]
                alpha_g = new_g = None
                for u in range(per):
                    h = g * per + u
                    kb_row = kb_ref[0, h, pl.ds(j, 1), :] if has_kb else None
                    s = _step_scores(_only_head(qg, own, per, u), kw, kb_row, ok, scale)
                    m_new = jnp.max(s, axis=1, keepdims=True)
                    if not single:
                        m_old = m_in[:, h:h + 1]
                        m_new = jnp.maximum(m_old, m_new)
                        alpha = jnp.exp(m_old - m_new)
                        alpha_g = alpha if u == 0 else jnp.where(own[u], alpha, alpha_g)
                    p = jnp.exp(s - m_new)
                    l_new = jnp.sum(p, axis=1, keepdims=True)
                    r = jnp.dot(p.astype(BF16), vw, preferred_element_type=F32)
                    if single:
                        r = r * (1.0 / l_new)
                    else:
                        l_new = alpha * l_in[:, h:h + 1] + l_new
                    new_g = r if u == 0 else jnp.where(own[u], r, new_g)
                    m_out = jnp.where(lane == h, m_new, m_out)
                    l_out = jnp.where(lane == h, l_new, l_out)
                o_ref[0, :, cols] = new_g if single else alpha_g * o_ref[0, :, cols] + new_g
            return m_out, l_out

        carry = (jnp.full((tq, LANES), MASKED, F32), jnp.zeros((tq, LANES), F32))
        if single:
            m_all, l_all = step(0, carry)
            l_all = jnp.where(lane < heads, l_all, 1.0)
        else:
            o_ref[...] = jnp.zeros(o_ref.shape, F32)
            m_all, l_all = lax.fori_loop(0, _window(mode, i, 0, tq, win, lk)[1], step, carry)
            l_all = jnp.where(lane < heads, l_all, 1.0)
            inv = 1.0 / l_all
            for g in range(groups):
                cols = slice(g * gw, (g + 1) * gw)
                inv_g = inv[:, g * per:g * per + 1]
                for u in range(1, per):
                    inv_g = jnp.where(own[u], inv[:, g * per + u:g * per + u + 1], inv_g)
                o_ref[0, :, cols] = o_ref[0, :, cols] * inv_g
        lse_ref[0] = jnp.where(lane < heads, m_all + jnp.log(l_all), 0.0)

    in_specs = [_attn_specs(qa, width, qc, classes, tq, False), _attn_specs(ka, width, kc, classes, win, True),
                _attn_specs(va, width, vc, classes, win, True)]
    args = [qa, ka, va]
    if has_kb:
        in_specs.append(pl.BlockSpec((1,) + kb.shape[1:], lambda n_, r, i: (n_, 0, 0, 0)))
        args.append(kb)
    out_shape = [jax.ShapeDtypeStruct((n, lq, classes * width), F32), jax.ShapeDtypeStruct((n, lq, classes * LANES), F32)]
    out_specs = [pl.BlockSpec((1, tq, width), lambda n_, r, i: (n_, i, r)), pl.BlockSpec((1, tq, LANES), lambda n_, r, i: (n_, i, r))]
    return pl.pallas_call(
        body, grid=(n, classes, lq // tq), in_specs=in_specs, out_specs=out_specs, out_shape=out_shape, name=name,
        compiler_params=_params("parallel", "parallel", "arbitrary"),
    )(*args)


def _attn_bwd(q, k, v, do, lse, delta, kb=None, *, classes=1, heads, dh, mode, max_dist=None, name):
    (qa, qc), (ka, kc), (va, vc), (da, dc) = q, k, v, do
    n, lq, lk = qa.shape[0], qa.shape[1], ka.shape[1]
    width = heads * dh
    tq, win = _attn_tiles(mode, lq, lk)
    gw, per, groups = _head_groups(heads, dh)
    scale = dh ** -0.5
    has_kb = kb is not None
    single = mode != "causal"
    nt = (((1,), (1,)), ((), ()))
    tn = (((0,), (0,)), ((), ()))

    def body(*refs):
        q_ref, k_ref, v_ref, do_ref, lse_ref, dl_ref = refs[:6]
        kb_ref = refs[6] if has_kb else None
        n_in = 7 if has_kb else 6
        dq_ref, dk_ref, dv_ref = refs[n_in:n_in + 3]
        dkb_ref, drow_ref = refs[n_in + 3:n_in + 5] if has_kb else (None, None)
        i = pl.program_id(2)

        @pl.when(i == 0)
        def _():
            dk_ref[...] = jnp.zeros(dk_ref.shape, F32)
            dv_ref[...] = jnp.zeros(dv_ref.shape, F32)
            if has_kb:
                dkb_ref[...] = jnp.zeros(dkb_ref.shape, F32)

        lse_all = lse_ref[0]
        dl_all = dl_ref[0]
        lane = lax.broadcasted_iota(jnp.int32, (tq, LANES), 1)
        own = [_own_lanes(tq, gw, dh, u) for u in range(per)]

        def step(j, drow_all):
            start, _ = _window(mode, i, j, tq, win, lk)
            ok = _allowed(mode, i, start, tq, win, max_dist)
            for g in range(groups):
                cols = slice(g * gw, (g + 1) * gw)
                qg = q_ref[0, :, cols]
                dog = do_ref[0, :, cols]
                kw = k_ref[0, pl.ds(start, win), cols]
                vw = v_ref[0, pl.ds(start, win), cols]
                dq_g = dk_w = dv_w = None
                for u in range(per):
                    h = g * per + u
                    qz, doz = _only_head(qg, own, per, u), _only_head(dog, own, per, u)
                    kb_row = kb_ref[0, h, pl.ds(j, 1), :] if has_kb else None
                    p = jnp.exp(_step_scores(qz, kw, kb_row, ok, scale) - lse_all[:, h:h + 1])
                    dp = lax.dot_general(doz, vw, nt, preferred_element_type=F32)
                    ds = p * (dp - dl_all[:, h:h + 1])
                    dsb = ds.astype(BF16)
                    r = jnp.dot(dsb, kw, preferred_element_type=F32)
                    dq_g = r if u == 0 else jnp.where(own[u], r, dq_g)
                    dk_u = lax.dot_general(dsb, qz, tn, preferred_element_type=F32)
                    dv_u = lax.dot_general(p.astype(BF16), doz, tn, preferred_element_type=F32)
                    dk_w = dk_u if u == 0 else dk_w + dk_u
                    dv_w = dv_u if u == 0 else dv_w + dv_u
                    if has_kb:
                        dkb_ref[0, h, pl.ds(j, 1), :] += -jnp.sum(ds, axis=0, keepdims=True)
                        drow_all = drow_all + jnp.where(lane == h, jnp.sum(ds, axis=1, keepdims=True), 0.0)
                dk_ref[0, pl.ds(start, win), cols] += dk_w * scale
                dv_ref[0, pl.ds(start, win), cols] += dv_w
                if single:
                    dq_ref[0, :, cols] = dq_g * scale
                else:
                    dq_ref[0, :, cols] += dq_g * scale
            return drow_all

        drow_all = jnp.zeros((tq, LANES), F32)
        if single:
            drow_all = step(0, drow_all)
        else:
            dq_ref[...] = jnp.zeros(dq_ref.shape, F32)
            drow_all = lax.fori_loop(0, _window(mode, i, 0, tq, win, lk)[1], step, drow_all)
        if has_kb:
            drow_ref[0] = drow_all

    row_spec = functools.partial(_attn_specs, classes=classes, rows_block=tq, whole=False)
    in_specs = [row_spec(qa, width, qc), _attn_specs(ka, width, kc, classes, win, True), _attn_specs(va, width, vc, classes, win, True),
                row_spec(da, width, dc), row_spec(lse, LANES, 0), row_spec(delta, LANES, 0)]
    args = [qa, ka, va, da, lse, delta]
    out_shape = [jax.ShapeDtypeStruct((n, lq, classes * width), F32), jax.ShapeDtypeStruct((n, lk, classes * width), F32),
                 jax.ShapeDtypeStruct((n, lk, classes * width), F32)]
    kv_spec = pl.BlockSpec((1, lk, width), lambda n_, r, i: (n_, 0, r))
    out_specs = [pl.BlockSpec((1, tq, width), lambda n_, r, i: (n_, i, r)), kv_spec, kv_spec]
    if has_kb:
        kb_spec = pl.BlockSpec((1,) + kb.shape[1:], lambda n_, r, i: (n_, 0, 0, 0))
        in_specs.append(kb_spec)
        args.append(kb)
        out_shape += [jax.ShapeDtypeStruct(kb.shape, F32), jax.ShapeDtypeStruct((n, lq, LANES), F32)]
        out_specs += [kb_spec, pl.BlockSpec((1, tq, LANES), lambda n_, r, i: (n_, i, 0))]
    return pl.pallas_call(
        body, grid=(n, classes, lq // tq), in_specs=in_specs, out_specs=out_specs, out_shape=out_shape, name=name,
        compiler_params=_params("parallel", "parallel", "arbitrary"),
    )(*args)


def _head_delta(do, out, heads, *, name, lse=None, sink=None):
    width = out.shape[1]
    gather = _head_expand(heads, width).T

    if sink is None:
        return _rowwise(lambda do, o, e: _dot_exact(do * o, e), [do, out], [gather], [(LANES, F32)], name=name)[0]

    def fn(do, o, lse, e, sink):
        delta = _dot_exact(do * o, e)
        return delta, -jnp.sum(jnp.exp(sink - lse) * delta, axis=0, keepdims=True)
    return _rowwise(fn, [do, out, lse], [gather, sink], [(LANES, F32)], [(1, LANES)], name=name)


def _rope_tables():
    half = HEAD_DIM // 2
    inv = ROPE_THETA ** (-jnp.arange(half, dtype=F32) / half)
    ang = jnp.arange(SEQ, dtype=F32)[:, None] * inv[None, :]
    cos = jnp.tile(jnp.cos(ang), (1, 2 * ATT_W // HEAD_DIM))
    sin = jnp.tile(jnp.concatenate([-jnp.sin(ang), jnp.sin(ang)], axis=1), (1, ATT_W // HEAD_DIM))
    return cos, sin


def _rotate(x, cos, sin):
    width = x.shape[1]
    lane = lax.broadcasted_iota(jnp.int32, x.shape, 1)
    first_half = (lane % HEAD_DIM) < (HEAD_DIM // 2)
    swapped = jnp.where(first_half, pltpu.roll(x, width - HEAD_DIM // 2, axis=1), pltpu.roll(x, HEAD_DIM // 2, axis=1))
    return x * cos[:, :width] + swapped * sin[:, :width]


def _gelu(x):
    k = math.sqrt(2.0 / math.pi)
    t = jnp.tanh(k * (x + 0.044715 * x * x * x))
    return 0.5 * x * (1.0 + t), t


def _gelu_grad(x, t):
    k = math.sqrt(2.0 / math.pi)
    return 0.5 * (1.0 + t) + 0.5 * x * (1.0 - t * t) * k * (1.0 + 3.0 * 0.044715 * x * x)


def _shift_down(x, halo, s):
    ext = jnp.concatenate([halo, x], axis=0)
    return pltpu.roll(ext, s, axis=0)[8:, :]


def _shift_up(x, halo, s):
    rows = x.shape[0]
    ext = jnp.concatenate([x, halo], axis=0)
    return pltpu.roll(ext, rows + 8 - s, axis=0)[:rows, :]


def _lru_gates(u, halo, cw, cb, wa, ba, wi, bi, lam):
    taps = [_shift_down(u, halo, CONV_WIDTH - 1 - k) for k in range(CONV_WIDTH - 1)] + [u]
    uc = cb + sum(cw[k:k + 1, :] * taps[k] for k in range(CONV_WIDTH))
    ucb = uc.astype(BF16)
    r = jax.nn.sigmoid(jnp.dot(ucb, wa, preferred_element_type=F32) + ba)
    gi = jax.nn.sigmoid(jnp.dot(ucb, wi, preferred_element_type=F32) + bi)
    sp = jnp.maximum(-lam, 0.0) + jnp.log(1.0 + jnp.exp(-jnp.abs(lam)))
    log_a = -LRU_C * r * sp
    a = jnp.exp(log_a)
    x2 = 2.0 * log_a
    one_minus_a2 = jnp.where(x2 > -0.01, -x2 * (1.0 + x2 * (0.5 + x2 * (1.0 / 6.0 + x2 / 24.0))), 1.0 - jnp.exp(x2))
    mult = jnp.sqrt(one_minus_a2)
    return taps, uc, ucb, r, gi, sp, a, mult


def _lru_specs(nchunk, reverse):
    def chunk(b, c):
        return b * nchunk + ((nchunk - 1 - c) if reverse else c)

    def halo_before(b, c):
        return jnp.maximum(chunk(b, c) * (LRU_CHUNK // 8) - 1, 0)

    return chunk, halo_before


def _lru_fwd(zug, cw, cb, wa, ba, wi, bi, lam, *, name):
    total = zug.shape[0]
    nchunk = SEQ // LRU_CHUNK
    w = LRU_WIDTH
    chunk, halo_before = _lru_specs(nchunk, False)

    def body(u_ref, uh_ref, g_ref, cw_ref, cb_ref, wa_ref, ba_ref, wi_ref, bi_ref, lam_ref, y_ref, h_ref, a_sc, x_sc, carry_sc):
        c = pl.program_id(1)
        u = u_ref[...]
        halo = jnp.where(c > 0, uh_ref[...], 0.0)
        _, uc, _, _, gi, _, a, mult = _lru_gates(u, halo, cw_ref[...], cb_ref[...], wa_ref[...], ba_ref[...],
                                                 wi_ref[...], bi_ref[...], lam_ref[...])
        a_sc[...] = a
        x_sc[...] = mult * (gi * uc)

        @pl.when(c == 0)
        def _():
            carry_sc[...] = jnp.zeros(carry_sc.shape, F32)

        def tile_step(t, h):
            r0 = pl.multiple_of(t * 8, 8)
            at = a_sc[pl.ds(r0, 8), :]
            xt = x_sc[pl.ds(r0, 8), :]
            rows = []
            for j in range(8):
                h = at[j:j + 1, :] * h + xt[j:j + 1, :]
                rows.append(h)
            h_ref[pl.ds(r0, 8), :] = jnp.concatenate(rows, axis=0)
            return h

        h_last = lax.fori_loop(0, LRU_CHUNK // 8, tile_step, carry_sc[0:1, :])
        carry_sc[0:1, :] = h_last
        gel, _ = _gelu(g_ref[...])
        y_ref[...] = (h_ref[...] * gel).astype(BF16)

    small = lambda arr: pl.BlockSpec(arr.shape, lambda b, c: (0, 0))
    return pl.pallas_call(
        body, grid=(total // SEQ, nchunk),
        in_specs=[pl.BlockSpec((LRU_CHUNK, w), lambda b, c: (chunk(b, c), 0)),
                  pl.BlockSpec((8, w), lambda b, c: (halo_before(b, c), 0)),
                  pl.BlockSpec((LRU_CHUNK, w), lambda b, c: (chunk(b, c), 1)),
                  small(cw), small(cb), small(wa), small(ba), small(wi), small(bi), small(lam)],
        out_specs=[pl.BlockSpec((LRU_CHUNK, w), lambda b, c: (chunk(b, c), 0))] * 2,
        out_shape=[jax.ShapeDtypeStruct((total, w), BF16), jax.ShapeDtypeStruct((total, w), F32)],
        scratch_shapes=[pltpu.VMEM((LRU_CHUNK, w), F32), pltpu.VMEM((LRU_CHUNK, w), F32), pltpu.VMEM((8, w), F32)],
        name=name, compiler_params=_params("arbitrary", "arbitrary"),
    )(zug, zug, zug, cw, cb, wa, ba, wi, bi, lam)


def _lru_bwd(zug, hl, dy, cw, cb, wa, ba, wi, bi, lam, *, name):
    total = zug.shape[0]
    nchunk = SEQ // LRU_CHUNK
    w = LRU_WIDTH
    chunk, halo_before = _lru_specs(nchunk, True)
    tn = (((0,), (0,)), ((), ()))
    nt = (((1,), (1,)), ((), ()))

    def body(u_ref, uh_ref, g_ref, hl_ref, hh_ref, dy_ref, cw_ref, cb_ref, wa_ref, ba_ref, wi_ref, bi_ref, lam_ref,
             dz_ref, dcw_ref, dcb_ref, dwa_ref, dba_ref, dwi_ref, dbi_ref, dlam_ref,
             a_sc, d_sc, g_sc, gcarry_sc, acarry_sc, duc_sc):
        b, c = pl.program_id(0), pl.program_id(1)
        first_chunk = c == nchunk - 1

        @pl.when((b == 0) & (c == 0))
        def _():
            for ref in (dcw_ref, dcb_ref, dwa_ref, dba_ref, dwi_ref, dbi_ref, dlam_ref):
                ref[...] = jnp.zeros(ref.shape, F32)

        @pl.when(c == 0)
        def _():
            gcarry_sc[...] = jnp.zeros(gcarry_sc.shape, F32)
            acarry_sc[...] = jnp.zeros(acarry_sc.shape, F32)
            duc_sc[...] = jnp.zeros(duc_sc.shape, F32)

        u = u_ref[...]
        halo = jnp.where(first_chunk, 0.0, uh_ref[...])
        cw, wa, wi, lam = cw_ref[...], wa_ref[...], wi_ref[...], lam_ref[...]
        taps, uc, ucb, r, gi, sp, a, mult = _lru_gates(u, halo, cw, cb_ref[...], wa, ba_ref[...], wi, bi_ref[...], lam)
        gate = g_ref[...]
        gel, th = _gelu(gate)
        dy = dy_ref[...]
        hl = hl_ref[...]
        a_sc[...] = a
        d_sc[...] = dy * gel
        dgate = dy * hl * _gelu_grad(gate, th)

        def tile_step(t, carry):
            g_next, a_next = carry
            r0 = pl.multiple_of((LRU_CHUNK // 8 - 1 - t) * 8, 8)
            dt = d_sc[pl.ds(r0, 8), :]
            at = a_sc[pl.ds(r0, 8), :]
            rows = [None] * 8
            for j in reversed(range(8)):
                g_next = dt[j:j + 1, :] + a_next * g_next
                a_next = at[j:j + 1, :]
                rows[j] = g_next
            g_sc[pl.ds(r0, 8), :] = jnp.concatenate(rows, axis=0)
            return g_next, a_next

        g_last, a_last = lax.fori_loop(0, LRU_CHUNK // 8, tile_step, (gcarry_sc[0:1, :], acarry_sc[0:1, :]))
        gcarry_sc[0:1, :] = g_last
        acarry_sc[0:1, :] = a_last

        gs = g_sc[...]
        hl_halo = jnp.where(first_chunk, 0.0, hh_ref[...])
        da = gs * _shift_down(hl, hl_halo, 1)
        dmult = gs * gi * uc
        dgi = gs * mult * uc
        duc = gs * mult * gi
        dlog_a = da * a - dmult * a * a / mult
        dr = dlog_a * (-LRU_C * sp)
        dsp = jnp.sum(dlog_a * (-LRU_C * r), axis=0, keepdims=True)
        dlam_ref[...] += -dsp * jax.nn.sigmoid(-lam)
        dpa = dr * r * (1.0 - r)
        dpi = dgi * gi * (1.0 - gi)
        dpab, dpib = dpa.astype(BF16), dpi.astype(BF16)
        dba_ref[...] += jnp.sum(dpa, axis=0, keepdims=True)
        dbi_ref[...] += jnp.sum(dpi, axis=0, keepdims=True)
        dwa_ref[...] += lax.dot_general(ucb, dpab, tn, preferred_element_type=F32)
        dwi_ref[...] += lax.dot_general(ucb, dpib, tn, preferred_element_type=F32)
        duc = duc + lax.dot_general(dpab, wa, nt, preferred_element_type=F32)
        duc = duc + lax.dot_general(dpib, wi, nt, preferred_element_type=F32)
        dcb_ref[...] += jnp.sum(duc, axis=0, keepdims=True)
        dcw_ref[...] += jnp.concatenate([jnp.sum(duc * taps[k], axis=0, keepdims=True) for k in range(CONV_WIDTH)], axis=0)
        after = duc_sc[...]
        du = cw[CONV_WIDTH - 1:CONV_WIDTH, :] * duc
        for k in range(CONV_WIDTH - 1):
            du = du + cw[k:k + 1, :] * _shift_up(duc, after, CONV_WIDTH - 1 - k)
        duc_sc[...] = duc[0:8, :]
        dz_ref[:, 0:w] = du.astype(BF16)
        dz_ref[:, w:2 * w] = dgate.astype(BF16)

    small = lambda arr: pl.BlockSpec(arr.shape, lambda b, c: (0, 0))
    acc = lambda shape: pl.BlockSpec(shape, lambda b, c: (0, 0))
    chunk_spec = lambda cb_: pl.BlockSpec((LRU_CHUNK, w), lambda b, c: (chunk(b, c), cb_))
    halo_spec = pl.BlockSpec((8, w), lambda b, c: (halo_before(b, c), 0))
    acc_shapes = [(CONV_WIDTH, w), (1, w), (w, w), (1, w), (w, w), (1, w), (1, w)]
    return pl.pallas_call(
        body, grid=(total // SEQ, nchunk),
        in_specs=[chunk_spec(0), halo_spec, chunk_spec(1), chunk_spec(0), halo_spec, chunk_spec(0),
                  small(cw), small(cb), small(wa), small(ba), small(wi), small(bi), small(lam)],
        out_specs=[pl.BlockSpec((LRU_CHUNK, 2 * w), lambda b, c: (chunk(b, c), 0))] + [acc(s) for s in acc_shapes],
        out_shape=[jax.ShapeDtypeStruct((total, 2 * w), BF16)] + [jax.ShapeDtypeStruct(s, F32) for s in acc_shapes],
        scratch_shapes=[pltpu.VMEM((LRU_CHUNK, w), F32)] * 3 + [pltpu.VMEM((8, w), F32)] * 3,
        name=name, compiler_params=_params("arbitrary", "arbitrary"),
    )(zug, zug, zug, hl, hl, dy, cw, cb, wa, ba, wi, bi, lam)


def _block_diag(wb):
    eye = jnp.eye(LRU_BLOCKS, dtype=wb.dtype)
    return jnp.einsum("gcd,gh->gchd", wb, eye).reshape(LRU_WIDTH, LRU_WIDTH).astype(BF16)


def _diag_blocks(wd):
    blk = LRU_WIDTH // LRU_BLOCKS
    return jnp.stack([wd[g * blk:(g + 1) * blk, g * blk:(g + 1) * blk] for g in range(LRU_BLOCKS)])


FOX_SCAN = 256


def _fox_bias(fl, bf, *, name):
    nb = fl.shape[0]

    def body(fl_ref, bf_ref, c_ref):
        x = fl_ref[0] + bf_ref[...]
        logf = jnp.minimum(x, 0.0) - jnp.log(1.0 + jnp.exp(-jnp.abs(x)))
        upper = (lax.broadcasted_iota(jnp.int32, (FOX_SCAN, FOX_SCAN), 0)
                 <= lax.broadcasted_iota(jnp.int32, (FOX_SCAN, FOX_SCAN), 1)).astype(BF16)
        carry = jnp.zeros((LRU_BLOCKS, 1), F32)
        for j in range(SEQ // FOX_SCAN):
            blk = logf[:, j * FOX_SCAN:(j + 1) * FOX_SCAN]
            c_ref[0, :, j * FOX_SCAN:(j + 1) * FOX_SCAN] = _dot_exact(blk, upper) + carry
            carry = carry + jnp.sum(blk, axis=1, keepdims=True)

    return pl.pallas_call(
        body, grid=(nb,),
        in_specs=[pl.BlockSpec((1, 8, SEQ), lambda b: (b, 0, 0)), pl.BlockSpec((8, 1), lambda b: (0, 0))],
        out_specs=pl.BlockSpec((1, 8, SEQ), lambda b: (b, 0, 0)),
        out_shape=jax.ShapeDtypeStruct((nb, 8, SEQ), F32), name=name, compiler_params=_params("arbitrary"),
    )(fl, bf)


def _fox_bias_bwd(fl, bf, dc, *, name):
    nb = fl.shape[0]

    def body(fl_ref, bf_ref, dc_ref, dfl_ref, dbf_ref):
        @pl.when(pl.program_id(0) == 0)
        def _():
            dbf_ref[...] = jnp.zeros(dbf_ref.shape, F32)

        x = fl_ref[0] + bf_ref[...]
        dc_all = dc_ref[0]
        lower = (lax.broadcasted_iota(jnp.int32, (FOX_SCAN, FOX_SCAN), 0)
                 >= lax.broadcasted_iota(jnp.int32, (FOX_SCAN, FOX_SCAN), 1)).astype(BF16)
        carry = jnp.zeros((LRU_BLOCKS, 1), F32)
        total = jnp.zeros((LRU_BLOCKS, 1), F32)
        for j in reversed(range(SEQ // FOX_SCAN)):
            cols = slice(j * FOX_SCAN, (j + 1) * FOX_SCAN)
            blk = dc_all[:, cols]
            dlogf = _dot_exact(blk, lower) + carry
            carry = carry + jnp.sum(blk, axis=1, keepdims=True)
            dx = dlogf * jax.nn.sigmoid(-x[:, cols])
            dfl_ref[0, :, cols] = dx
            total = total + jnp.sum(dx, axis=1, keepdims=True)
        dbf_ref[...] += total

    return pl.pallas_call(
        body, grid=(nb,),
        in_specs=[pl.BlockSpec((1, 8, SEQ), lambda b: (b, 0, 0)), pl.BlockSpec((8, 1), lambda b: (0, 0)),
                  pl.BlockSpec((1, 8, SEQ), lambda b: (b, 0, 0))],
        out_specs=[pl.BlockSpec((1, 8, SEQ), lambda b: (b, 0, 0)), pl.BlockSpec((8, 1), lambda b: (0, 0))],
        out_shape=[jax.ShapeDtypeStruct((nb, 8, SEQ), F32), jax.ShapeDtypeStruct((8, 1), F32)],
        name=name, compiler_params=_params("arbitrary"),
    )(fl, bf, dc)


def _seq3(a, nb):
    return a.reshape(nb, a.shape[0] // nb, a.shape[1])


def _flat2(a):
    return a.reshape(a.shape[0] * a.shape[1], a.shape[2])


def _mlp_fwd(h, p, tag):
    hn = _rms_fwd(h, p["norm"], name=f"{tag}_norm")
    act, = _matmul(hn, p["w_up_t"], tb=True, outs=(BF16,), epilogue=lambda acc: (jnp.square(jnp.maximum(acc, 0.0)),),
                   name=f"{tag}_up")
    out, = _matmul(act, p["w_down"], extras=(h,), epilogue=lambda acc, res: (acc + res,), name=f"{tag}_down")
    return out, (h, hn, act)


def _mlp_bwd(dh, dhb, p, saved, tag):
    h, hn, act = saved
    dup, = _matmul(dhb, p["w_down"], tb=True, outs=(BF16,), extras=(act,),
                   epilogue=lambda acc, act: (acc * (2.0 * jnp.sqrt(act.astype(F32))),), name=f"{tag}_bwd_dup")
    dw_down, = _matmul(act, dhb, ta=True, outs=(BF16,),name=f"{tag}_bwd_dwdown")
    dw_up_t, = _matmul(dup, hn, ta=True, outs=(BF16,),name=f"{tag}_bwd_dwup")
    dhn, = _matmul(dup, p["w_up_t"], name=f"{tag}_bwd_dhn")
    dh2, dh2b, dg = _rms_bwd(h, dhn, dh, p["norm"], name=f"{tag}_bwd_norm")
    return dh2, dh2b, {"norm": dg, "w_up_t": dw_up_t, "w_down": dw_down}


def _xa_fwd(h, mem, p, tag, nb):
    hn = _rms_fwd(h, p["norm"], name=f"{tag}_norm")
    mem_n = _rms_fwd(mem, p["mem_norm"], name=f"{tag}_memnorm")
    q, = _matmul(hn, p["w_q"], outs=(BF16,), name=f"{tag}_q")
    kv, = _matmul(mem_n, p["w_kv_t"], tb=True, outs=(BF16,), name=f"{tag}_kv")
    q3, kv3 = _seq3(q, nb), _seq3(kv, nb)
    o, lse = _attn_fwd((q3, 0), (kv3, 0), (kv3, 1), heads=XA_HEADS, dh=XA_HEAD_DIM, mode="full",
                       name=f"{tag}_attn")
    o = _flat2(o)
    ob, = _rowwise(lambda o: o, [o], [], [(D_MODEL, BF16)], name=f"{tag}_ocast")
    out, = _matmul(ob, p["w_o"], extras=(h,), epilogue=lambda acc, res: (acc + res,), name=f"{tag}_o")
    return out, (h, hn, mem_n, q3, kv3, o, ob, lse)


def _xa_bwd(dh, dhb, mem, p, saved, tag, nb):
    h, hn, mem_n, q3, kv3, o, ob, lse = saved
    do, dob = _matmul(dhb, p["w_o"], tb=True, outs=(F32, BF16), epilogue=lambda acc: (acc, acc), name=f"{tag}_bwd_do")
    dw_o, = _matmul(ob, dhb, ta=True, outs=(BF16,),name=f"{tag}_bwd_dwo")
    delta = _head_delta(do, o, XA_HEADS, name=f"{tag}_bwd_delta")
    dq, dk, dv = _attn_bwd((q3, 0), (kv3, 0), (kv3, 1), (_seq3(dob, nb), 0), lse, _seq3(delta, nb), heads=XA_HEADS,
                           dh=XA_HEAD_DIM, mode="full", name=f"{tag}_bwd_attn")
    dqb, = _rowwise(lambda x: x, [_flat2(dq)], [], [(D_MODEL, BF16)], name=f"{tag}_bwd_dqcast")
    dkvb, = _rowwise(lambda a, b: jnp.concatenate([a, b], axis=1), [_flat2(dk), _flat2(dv)], [], [(2 * D_MODEL, BF16)],
                     name=f"{tag}_bwd_dkvcast")
    dw_q, = _matmul(hn, dqb, ta=True, outs=(BF16,),name=f"{tag}_bwd_dwq")
    dw_kv_t, = _matmul(dkvb, mem_n, ta=True, outs=(BF16,),name=f"{tag}_bwd_dwkv")
    dhn, = _matmul(dqb, p["w_q"], tb=True, name=f"{tag}_bwd_dhn")
    dmem_n, = _matmul(dkvb, p["w_kv_t"], name=f"{tag}_bwd_dmemn")
    dg_mem, = _rowwise(lambda x, d, g: _rms_bwd_vals(x, d, g)[1], [mem, dmem_n], [p["mem_norm"]], [], [(1, D_MODEL)],
                       name=f"{tag}_bwd_memnorm")
    dh2, dh2b, dg = _rms_bwd(h, dhn, dh, p["norm"], name=f"{tag}_bwd_norm")
    return dh2, dh2b, {"norm": dg, "mem_norm": dg_mem, "w_q": dw_q, "w_kv_t": dw_kv_t, "w_o": dw_o}


AB_MAIN = 2 * LRU_WIDTH + 3 * ATT_W


def _ab_fwd(h, p, nb):
    hn = _rms_fwd(h, p["norm"], name="ab_norm")
    w_in_t = p["w_in_t"]
    zug, = _matmul(hn, w_in_t[:2 * LRU_WIDTH], tb=True, name="ab_in_ug")
    qkv, = _matmul(hn, w_in_t[2 * LRU_WIDTH:AB_MAIN], tb=True, outs=(BF16,), tn=768, name="ab_in_qkv")
    zf, = _matmul(hn, w_in_t[AB_MAIN:], tb=True, name="ab_in_f")
    fl = zf[:, :8].reshape(nb, SEQ, 8).transpose(0, 2, 1)
    cbias = _fox_bias(fl, p["b_f"], name="ab_fox_bias")
    kb = cbias.reshape(nb, 8, SEQ // ATT_CHUNK, ATT_CHUNK)
    y_a, hl = _lru_fwd(zug, p["conv_w"], p["conv_b"], p["w_a"], p["b_a"], p["w_i"], p["b_i"], p["lam"], name="ab_lru")
    qkv3 = _seq3(qkv, nb)
    o, lse = _attn_fwd((qkv3, 0), (qkv3, 1), (qkv3, 2), kb, heads=8, dh=HEAD_DIM, mode="causal", name="ab_fox")
    o = _flat2(o)
    y, = _rowwise(lambda a, b: jnp.concatenate([a, b.astype(BF16)], axis=1), [y_a, o], [], [(D_MODEL, BF16)], name="ab_ycat")
    out, = _matmul(y, p["w_out"], extras=(h,), epilogue=lambda acc, res: (acc + res,), name="ab_out")
    return out, (h, hn, zug, qkv3, fl, kb, hl, o, lse, y)


def _ab_bwd(dh, dhb, p, saved, nb):
    h, hn, zug, qkv3, fl, kb, hl, o, lse, y = saved
    dy, dyb = _matmul(dhb, p["w_out"], tb=True, outs=(F32, BF16), epilogue=lambda acc: (acc, acc), name="ab_bwd_dy")
    dw_out, = _matmul(y, dhb, ta=True, outs=(BF16,),name="ab_bwd_dwout")
    dzug, dcw, dcb, dwa, dba, dwi, dbi, dlam = _lru_bwd(zug, hl, dy, p["conv_w"], p["conv_b"], p["w_a"], p["b_a"],
                                                        p["w_i"], p["b_i"], p["lam"], name="ab_bwd_lru")
    delta = _head_delta((dy, ATT_W, 1), o, 8, name="ab_bwd_delta")
    dq, dk, dv, dkb, drow = _attn_bwd((qkv3, 0), (qkv3, 1), (qkv3, 2), (_seq3(dyb, nb), 1), lse, _seq3(delta, nb), kb,
                                      heads=8, dh=HEAD_DIM, mode="causal", name="ab_bwd_fox")
    dcum = dkb.reshape(nb, 8, SEQ) + drow[:, :, :8].transpose(0, 2, 1)
    dfl, dbf = _fox_bias_bwd(fl, p["b_f"], dcum, name="ab_bwd_fox_bias")
    dzf = jnp.pad(dfl.transpose(0, 2, 1).reshape(nb * SEQ, 8), ((0, 0), (0, LANES - 8)))
    dz, = _rowwise(lambda a, q, k, v, f: jnp.concatenate([a, q.astype(BF16), k.astype(BF16), v.astype(BF16), f.astype(BF16)], axis=1),
                   [dzug, _flat2(dq), _flat2(dk), _flat2(dv), dzf], [], [(AB_MAIN + LANES, BF16)], name="ab_bwd_dzcat")
    dw_in_t, = _matmul(dz, hn, ta=True, outs=(BF16,),tm=384, name="ab_bwd_dwin")
    dhn, = _matmul(dz, p["w_in_t"], name="ab_bwd_dhn")
    dh2, dh2b, dg = _rms_bwd(h, dhn, dh, p["norm"], name="ab_bwd_norm")
    grads = {"norm": dg, "w_in_t": dw_in_t[:AB_MAIN + 8], "conv_w": dcw, "conv_b": dcb, "w_a": _diag_blocks(dwa), "b_a": dba,
             "w_i": _diag_blocks(dwi), "b_i": dbi, "lam": dlam, "b_f": dbf.reshape(1, 8), "w_out": dw_out}
    return dh2, dh2b, grads


CD_IN = 3 * ATT_W + ATT_W + 2 * SWA_KW


def _class_view(a, dil):
    return a.reshape(a.shape[0], a.shape[1] // dil, dil * a.shape[2])


def _gqa_share():
    src = jnp.arange(SWA_KW)[:, None]
    dst = jnp.arange(ATT_W)[None, :]
    per_kv = ATT_W // (SWA_KW // HEAD_DIM)
    return ((dst // per_kv == src // HEAD_DIM) & (dst % HEAD_DIM == src % HEAD_DIM)).astype(BF16)


def _cd_fwd(h, p, nb):
    hn = _rms_fwd(h, p["norm"], name="cd_norm")
    z, = _matmul(hn, p["w_in_t"], tb=True, tn=384, name="cd_in")
    cos, sin = _rope_tables()
    per_seq = SEQ // ROW_TILE
    table_map = lambda i: (i % per_seq, 0)

    def rope_fn(z, cos, sin, share):
        qc, kc, vc = z[:, 0:ATT_W], z[:, ATT_W:2 * ATT_W], z[:, 2 * ATT_W:3 * ATT_W]
        qd, kd, vd = z[:, 3 * ATT_W:4 * ATT_W], z[:, 4 * ATT_W:4 * ATT_W + SWA_KW], z[:, 4 * ATT_W + SWA_KW:]
        kd8 = jnp.dot(_rotate(kd, cos, sin).astype(BF16), share, preferred_element_type=F32)
        vd8 = jnp.dot(vd.astype(BF16), share, preferred_element_type=F32)
        return (jnp.concatenate([_rotate(qc, cos, sin), _rotate(kc, cos, sin)], axis=1), vc, _rotate(qd, cos, sin), kd8, vd8)
    qk, vc, qd, kd, vd = _rowwise(rope_fn, [z, cos, sin], [_gqa_share()], [(2 * ATT_W, BF16)] + [(ATT_W, BF16)] * 4,
                                  name="cd_rope", row_maps=[None, table_map, table_map])
    qk3, vc3 = _seq3(qk, nb), _seq3(vc, nb)
    branch = []
    for window, dil in DIL_PATTERN:
        o_i, lse_i = _attn_fwd((_class_view(qk3, dil), 0), (_class_view(qk3, dil), 1), (_class_view(vc3, dil), 0), classes=dil,
                               heads=8, dh=HEAD_DIM, mode="band", max_dist=window // dil, name=f"cd_dil{dil}")
        branch.append((o_i.reshape(nb * SEQ, ATT_W), lse_i.reshape(nb * SEQ, LANES)))
    expand = _head_expand(8, ATT_W)

    def combine(o1, o2, o3, l1, l2, l3, e):
        m = jnp.maximum(jnp.maximum(l1, l2), l3)
        lt = m + jnp.log(jnp.exp(l1 - m) + jnp.exp(l2 - m) + jnp.exp(l3 - m))
        o = sum(_dot_exact(jnp.exp(l - lt), e) * o_ for o_, l in ((o1, l1), (o2, l2), (o3, l3)))
        return o, lt
    y_c, lse_c = _rowwise(combine, [b[0] for b in branch] + [b[1] for b in branch], [expand], [(ATT_W, F32), (LANES, F32)],
                          name="cd_dil_combine")
    qd3, kd3, vd3 = _seq3(qd, nb), _seq3(kd, nb), _seq3(vd, nb)
    o_d, lse_band = _attn_fwd((qd3, 0), (kd3, 0), (vd3, 0), heads=8, dh=HEAD_DIM, mode="band", max_dist=SWA_WINDOW - 1,
                              name="cd_swa")

    def sink_combine(o, l, e, sink):
        lt = jnp.maximum(l, sink) + jnp.log(1.0 + jnp.exp(-jnp.abs(l - sink)))
        return o * _dot_exact(jnp.exp(l - lt), e), lt
    y_d, lse_d = _rowwise(sink_combine, [_flat2(o_d), _flat2(lse_band)], [expand, p["sink"]], [(ATT_W, F32), (LANES, F32)],
                          name="cd_swa_sink")
    y, = _rowwise(lambda a, b: jnp.concatenate([a, b], axis=1), [y_c, y_d], [], [(D_MODEL, BF16)], name="cd_ycat")
    out, = _matmul(y, p["w_out"], extras=(h,), epilogue=lambda acc, res: (acc + res,), name="cd_out")
    return out, (h, hn, qk3, vc3, qd3, kd3, vd3, y_c, lse_c, y_d, lse_d, y)


def _cd_bwd(dh, dhb, p, saved, nb):
    h, hn, qk3, vc3, qd3, kd3, vd3, y_c, lse_c, y_d, lse_d, y = saved
    dy, dyb = _matmul(dhb, p["w_out"], tb=True, outs=(F32, BF16), epilogue=lambda acc: (acc, acc), name="cd_bwd_dy")
    dw_out, = _matmul(y, dhb, ta=True, outs=(BF16,),name="cd_bwd_dwout")
    dyb3 = _seq3(dyb, nb)
    delta_c = _head_delta((dy, ATT_W, 0), y_c, 8, name="cd_bwd_delta_dil")
    lse_c3, delta_c3 = _seq3(lse_c, nb), _seq3(delta_c, nb)
    parts = []
    for window, dil in DIL_PATTERN:
        cv = functools.partial(_class_view, dil=dil)
        dq_i, dk_i, dv_i = _attn_bwd((cv(qk3), 0), (cv(qk3), 1), (cv(vc3), 0), (cv(dyb3), 0), cv(lse_c3), cv(delta_c3),
                                     classes=dil, heads=8, dh=HEAD_DIM, mode="band", max_dist=window // dil,
                                     name=f"cd_bwd_dil{dil}")
        parts.append([a.reshape(nb * SEQ, ATT_W) for a in (dq_i, dk_i, dv_i)])
    delta_d, dsink = _head_delta((dy, ATT_W, 1), y_d, 8, lse=lse_d, sink=p["sink"], name="cd_bwd_delta_swa")
    dqd, dkd, dvd = _attn_bwd((qd3, 0), (kd3, 0), (vd3, 0), (dyb3, 1), _seq3(lse_d, nb), _seq3(delta_d, nb), heads=8,
                              dh=HEAD_DIM, mode="band", max_dist=SWA_WINDOW - 1, name="cd_bwd_swa")
    cos, sin = _rope_tables()
    per_seq = SEQ // ROW_TILE
    table_map = lambda i: (i % per_seq, 0)

    def unrope(q1, q2, q3, k1, k2, k3, v1, v2, v3, qd, kd8, vd8, cos, sin, gather):
        back = lambda x: _rotate(x, cos, -sin)
        kd, vd = _dot_exact(kd8, gather), _dot_exact(vd8, gather)
        return jnp.concatenate([back(q1 + q2 + q3), back(k1 + k2 + k3), v1 + v2 + v3, back(qd), back(kd), vd], axis=1)
    ins = [parts[b][t] for t in range(3) for b in range(3)] + [_flat2(dqd), _flat2(dkd), _flat2(dvd), cos, sin]
    dz, = _rowwise(unrope, ins, [_gqa_share().T], [(CD_IN, BF16)], name="cd_bwd_unrope",
                   row_maps=[None] * 12 + [table_map, table_map])
    dw_in_t, = _matmul(dz, hn, ta=True, outs=(BF16,),tm=384, name="cd_bwd_dwin")
    dhn, = _matmul(dz, p["w_in_t"], name="cd_bwd_dhn")
    dh2, dh2b, dg = _rms_bwd(h, dhn, dh, p["norm"], name="cd_bwd_norm")
    return dh2, dh2b, {"norm": dg, "w_in_t": dw_in_t, "sink": dsink[:, :8], "w_out": dw_out}


def _local_step(x, mem, target, w, complete=None):
    nb = x.shape[0]
    h = x.reshape(nb * SEQ, D_MODEL)
    mem2 = mem.reshape(nb * MEM_LEN, D_MODEL)
    tgt = target.reshape(nb * SEQ, D_MODEL)

    h, s_ab = _ab_fwd(h, w["ab"], nb)
    if complete is not None:
        complete(h)
    h, s_xa0 = _xa_fwd(h, mem2, w["xa0"], "xa0", nb)
    h, s_mlp0 = _mlp_fwd(h, w["mlp0"], "mlp0")
    h, s_cd = _cd_fwd(h, w["cd"], nb)
    h, s_xa1 = _xa_fwd(h, mem2, w["xa1"], "xa1", nb)
    h, s_mlp1 = _mlp_fwd(h, w["mlp1"], "mlp1")

    dh, dhb, loss, dg_final = _loss_and_grad(h, tgt, w["final_norm"], name="loss")
    grads = {"final_norm": dg_final}
    dh, dhb, grads["mlp1"] = _mlp_bwd(dh, dhb, w["mlp1"], s_mlp1, "mlp1")
    dh, dhb, grads["xa1"] = _xa_bwd(dh, dhb, mem2, w["xa1"], s_xa1, "xa1", nb)
    dh, dhb, grads["cd"] = _cd_bwd(dh, dhb, w["cd"], s_cd, nb)
    dh, dhb, grads["mlp0"] = _mlp_bwd(dh, dhb, w["mlp0"], s_mlp0, "mlp0")
    dh, dhb, grads["xa0"] = _xa_bwd(dh, dhb, mem2, w["xa0"], s_xa0, "xa0", nb)
    dh, dhb, grads["ab"] = _ab_bwd(dh, dhb, w["ab"], s_ab, nb)
    return loss, dh.reshape(nb, SEQ, D_MODEL), grads


ANY = pl.BlockSpec(memory_space=pl.ANY)


def _place():
    x, y, c = lax.axis_index("x"), lax.axis_index("y"), lax.axis_index("c")
    return x, y, c, [(1 - x, y), (x, 1 - y), (1 - x, 1 - y)]


def _gather_chips(shard):
    half = shard.shape[0] // 2

    def body(src, out, send_sems, recv_sems):
        x, y, c, chips = _place()
        sibling = (x, y, 1 - c)

        def rows(slot, core):
            return out.at[slot, pl.ds(core * half, half)]

        def copy(k, src_ref, dst_ref, to):
            return pltpu.make_async_remote_copy(src_ref=src_ref, dst_ref=dst_ref, send_sem=send_sems.at[k],
                                                recv_sem=recv_sems.at[k], device_id=to, device_id_type=MESH)

        first = [copy(k, src.at[pl.ds(c * half, half)], rows(2 * x + y, c), (px, py, c)) for k, (px, py) in enumerate(chips)]
        for cp in first:
            cp.start()
        passed = [copy(3 + k, rows(2 * px + py, c), rows(2 * px + py, c), sibling) for k, (px, py) in enumerate(chips)]
        for k, (px, py) in enumerate(chips):
            copy(k, src.at[pl.ds(c * half, half)], rows(2 * px + py, c), (px, py, c)).wait_recv()
            passed[k].start()
        for k, (px, py) in enumerate(chips):
            copy(3 + k, rows(2 * px + py, 1 - c), rows(2 * px + py, 1 - c), sibling).wait_recv()
        for cp in first + passed:
            cp.wait_send()

    return pl.pallas_call(
        body, out_shape=jax.ShapeDtypeStruct((N_CHIPS,) + shard.shape, shard.dtype), in_specs=[ANY], out_specs=ANY,
        scratch_shapes=[pltpu.SemaphoreType.DMA((6,)), pltpu.SemaphoreType.DMA((6,))], name="gather_chips",
    )(shard)


HBM = pl.BlockSpec(memory_space=pltpu.HBM)
SEM = pl.BlockSpec(memory_space=pltpu.SEMAPHORE)
SIDE_EFFECT = pltpu.SideEffectType.DATAFLOW_SIDE_EFFECTING


def _chip_copies(src, land, send_sems, recv_sems):
    half = src.shape[0] // 2
    x, y, c, chips = _place()

    def copy(k, slot, to):
        return pltpu.make_async_remote_copy(src_ref=src.at[pl.ds(c * half, half)], dst_ref=land.at[slot, pl.ds(c * half, half)],
                                            send_sem=send_sems[k], recv_sem=recv_sems[k], device_id=to, device_id_type=MESH)
    out = [copy(k, 2 * x + y, (px, py, c)) for k, (px, py) in enumerate(chips)]
    back = [copy(k, 2 * px + py, (px, py, c)) for k, (px, py) in enumerate(chips)]
    return out, back


def _gather_start(shard):
    def body(src, land, *rest):
        sems, token = rest[:6], rest[8]
        out, _ = _chip_copies(src, land, sems[:3], sems[3:])
        for cp in out:
            cp.start()
        token[...] = jnp.zeros(token.shape, F32)

    sem = pltpu.SemaphoreType.DMA(())
    land_shape = (N_CHIPS,) + shard.shape
    return pl.pallas_call(
        body, name="gather_start",
        out_shape=(sem,) * 6 + (pltpu.HBM(shard.shape, shard.dtype), pltpu.HBM(land_shape, shard.dtype),
                                jax.ShapeDtypeStruct((8, LANES), F32)),
        in_specs=(HBM, HBM), out_specs=(SEM,) * 6 + (HBM, HBM, pl.BlockSpec(memory_space=pltpu.VMEM)),
        input_output_aliases={0: 6, 1: 7}, compiler_params=pltpu.CompilerParams(has_side_effects=SIDE_EFFECT),
    )(pltpu.with_memory_space_constraint(shard, pltpu.HBM),
      pltpu.with_memory_space_constraint(lax.empty(land_shape, shard.dtype), pltpu.HBM))


def _gather_wait(started, after):
    sems, src_thru, land_thru = started[:6], started[6], started[7]

    def body(src, land, *rest):
        out, back = _chip_copies(src, land, rest[:3], rest[3:6])
        for cp in out:
            cp.wait_send()
        for cp in back:
            cp.wait_recv()

    return pl.pallas_call(
        body, name="gather_wait",
        out_shape=(pltpu.HBM(src_thru.shape, src_thru.dtype), pltpu.HBM(land_thru.shape, land_thru.dtype)),
        in_specs=(HBM, HBM) + (SEM,) * 6 + (pl.BlockSpec(memory_space=pl.ANY),), out_specs=(HBM, HBM),
        input_output_aliases={0: 0, 1: 1}, compiler_params=pltpu.CompilerParams(has_side_effects=SIDE_EFFECT),
    )(src_thru, land_thru, *sems, after)[1]


def _gather_pass(land):
    half = land.shape[1] // 2

    def body(land_in, land_out, send_sems, recv_sems):
        x, y, c, chips = _place()

        def copy(k, slot, core):
            rows = land_out.at[slot, pl.ds(core * half, half)]
            return pltpu.make_async_remote_copy(src_ref=rows, dst_ref=rows, send_sem=send_sems.at[k], recv_sem=recv_sems.at[k],
                                                device_id=(x, y, 1 - c), device_id_type=MESH)
        sends = [copy(k, 2 * px + py, c) for k, (px, py) in enumerate(chips)]
        for cp in sends:
            cp.start()
        for k, (px, py) in enumerate(chips):
            copy(k, 2 * px + py, 1 - c).wait_recv()
        for cp in sends:
            cp.wait_send()

    return pl.pallas_call(
        body, out_shape=jax.ShapeDtypeStruct(land.shape, land.dtype), in_specs=[ANY], out_specs=ANY,
        scratch_shapes=[pltpu.SemaphoreType.DMA((3,)), pltpu.SemaphoreType.DMA((3,))], input_output_aliases={0: 0},
        name="gather_pass",
    )(land)


def _swap_cores(block, *, name, half_rows=None):
    shape = block.shape if half_rows is None else (block.shape[0], half_rows, block.shape[2])

    def body(src, out, send_sem, recv_sem):
        x, y, c, _ = _place()
        part = src if half_rows is None else src.at[:, pl.ds((1 - c) * half_rows, half_rows)]
        cp = pltpu.make_async_remote_copy(src_ref=part, dst_ref=out, send_sem=send_sem, recv_sem=recv_sem,
                                          device_id=(x, y, 1 - c), device_id_type=MESH)
        cp.start()
        cp.wait()

    return pl.pallas_call(
        body, out_shape=jax.ShapeDtypeStruct(shape, block.dtype), in_specs=[ANY], out_specs=ANY,
        scratch_shapes=[pltpu.SemaphoreType.DMA(()), pltpu.SemaphoreType.DMA(())], name=name,
    )(block)


def _scatter_chips(parts):
    def body(src, out, send_sems, recv_sems):
        x, y, c, chips = _place()
        sends = [pltpu.make_async_remote_copy(src_ref=src.at[2 * px + py], dst_ref=out.at[k], send_sem=send_sems.at[k],
                                              recv_sem=recv_sems.at[k], device_id=(px, py, c), device_id_type=MESH)
                 for k, (px, py) in enumerate(chips)]
        for cp in sends:
            cp.start()
        for cp in sends:
            cp.wait_recv()
        for cp in sends:
            cp.wait_send()

    return pl.pallas_call(
        body, out_shape=jax.ShapeDtypeStruct((3,) + parts.shape[1:], parts.dtype), in_specs=[ANY], out_specs=ANY,
        scratch_shapes=[pltpu.SemaphoreType.DMA((3,)), pltpu.SemaphoreType.DMA((3,))], name="scatter_chips",
    )(parts)


def _sum_all_devices(vec):
    rows = vec.shape[0]

    def body(x_ref, total_ref, all_ref, send_sems, recv_sems, local_sem):
        x, y, c, chips = _place()
        me, sibling = (x, y, c), (x, y, 1 - c)

        def slot(px, py, pc):
            return all_ref.at[4 * px + 2 * py + pc]

        def copy(k, block, to, src=None):
            return pltpu.make_async_remote_copy(src_ref=slot(*block) if src is None else src, dst_ref=slot(*block),
                                                send_sem=send_sems.at[k], recv_sem=recv_sems.at[k], device_id=to,
                                                device_id_type=MESH)

        mine = pltpu.make_async_copy(x_ref, slot(*me), local_sem)
        mine.start()
        first = [copy(0, me, sibling, src=x_ref)] + [copy(1 + j, me, (*chip, c), src=x_ref) for j, chip in enumerate(chips)]
        for cp in first:
            cp.start()
        passed = [copy(4 + j, (*chip, c), sibling) for j, chip in enumerate(chips)]
        for j, chip in enumerate(chips):
            copy(1 + j, (*chip, c), me).wait_recv()
            passed[j].start()
        copy(0, sibling, me).wait_recv()
        for j, chip in enumerate(chips):
            copy(4 + j, (*chip, 1 - c), me).wait_recv()
        for cp in first + passed:
            cp.wait_send()
        mine.wait()
        acc = all_ref[0]
        for d in range(1, 8):
            acc = acc + all_ref[d]
        total_ref[...] = acc

    vmem = pl.BlockSpec(memory_space=pltpu.VMEM)
    return pl.pallas_call(
        body, out_shape=[jax.ShapeDtypeStruct((rows, LANES), F32), jax.ShapeDtypeStruct((8, rows, LANES), F32)],
        in_specs=[vmem], out_specs=[vmem, vmem],
        scratch_shapes=[pltpu.SemaphoreType.DMA((7,)), pltpu.SemaphoreType.DMA((7,)), pltpu.SemaphoreType.DMA(())],
        name="sum_all_devices", compiler_params=pltpu.CompilerParams(vmem_limit_bytes=VMEM_LIMIT_BYTES),
    )(vec)[0]


PACK_COLS = 1024
BIG = (("mlp_w_up", 2, 1024, True), ("mlp_w_down", 2, 1024, False), ("xa_w_kv", 2, 512, True), ("xa_w_q", 2, 256, False),
       ("xa_w_o", 2, 256, False), ("ab_w_out", 1, 256, False), ("cd_w_out", 1, 256, False), ("cd_w_in", 1, 576, True),
       ("ab_w_in", 1, 642, True))
ENTRIES = tuple((name, layer, rows, transposed) for name, layers, rows, transposed in BIG for layer in range(layers))
FIRST_ENTRIES = tuple(e for e in ENTRIES if e[0].startswith("ab_"))
LATER_ENTRIES = tuple(e for e in ENTRIES if not e[0].startswith("ab_"))


def _tile_rows(entry):
    return -(-entry[2] // 16) * 16


def _padded_rows(entries):
    return -(-sum(_tile_rows(e) for e in entries) // 32) * 32


PACK_ROWS = _padded_rows(ENTRIES)
REDUCE_TILE = 208
assert (PACK_ROWS // 2) % REDUCE_TILE == 0

SMALL = (("ab_norm", (1, 1024)), ("ab_conv_w", (1, 4, 512)), ("ab_conv_b", (1, 512)), ("lru_w_a", (1, 8, 64, 64)),
         ("lru_b_a", (1, 512)), ("lru_w_i", (1, 8, 64, 64)), ("lru_b_i", (1, 512)), ("lru_lambda", (1, 512)),
         ("fox_b_f", (1, 8)), ("cd_norm", (1, 1024)), ("cd_sink", (1, 8)), ("xa_norm", (2, 1024)),
         ("xa_mem_norm", (2, 1024)), ("mlp_norm", (2, 1024)), ("final_norm", (1024,)), ("loss", (1,)))
SPLIT_SMALL = ("ab_conv_w", "cd_norm")


def _pack_rows(parts, entries, dtype):
    lead = parts[0].shape[:-2]
    zeros = lambda rows: jnp.zeros(lead + (rows, PACK_COLS), dtype)
    pieces = [jnp.pad(p.astype(dtype), ((0, 0),) * len(lead) + ((0, _tile_rows(e) - e[2]), (0, 0))) for p, e in zip(parts, entries)]
    tail = _padded_rows(entries) - sum(_tile_rows(e) for e in entries)
    return jnp.concatenate(pieces + ([zeros(tail)] if tail else []), axis=len(lead))


def _unpack_rows(packed, entries):
    out, r0 = [], 0
    for e in entries:
        out.append(packed[..., r0:r0 + e[2], :])
        r0 += _tile_rows(e)
    return out


def _shard_rows(w, entries):
    return [(w[name][layer].T if transposed else w[name][layer]) for name, layer, _, transposed in entries]


def _shard_blocks(rows):
    out = {}
    for (name, layer, _, transposed), r in zip(ENTRIES, rows):
        out.setdefault(name, []).append(r.T if transposed else r)
    return {name: jnp.stack(layers) for name, layers in out.items()}


def _pack_small(vals):
    flat = jnp.concatenate([vals[name].astype(F32).reshape(-1) for name, _ in SMALL])
    size = -(-flat.shape[0] // (8 * LANES)) * (8 * LANES)
    return jnp.pad(flat, (0, size - flat.shape[0])).reshape(-1, LANES)


def _unpack_small(packed):
    flat, out, at = packed.reshape(-1), {}, 0
    for name, shape in SMALL:
        out[name] = flat[at:at + math.prod(shape)].reshape(shape)
        at += math.prod(shape)
    return out


def _chip_part(full, chip):
    width = full.shape[-1] // N_CHIPS
    return lax.dynamic_slice_in_dim(full, chip * width, width, axis=full.ndim - 1)


def _chip_embed(part, chip):
    full = jnp.zeros(part.shape[:-1] + (part.shape[-1] * N_CHIPS,), part.dtype)
    return lax.dynamic_update_slice_in_dim(full, part, chip * part.shape[-1], axis=part.ndim - 1)


SUBLAYER_KEYS = {"ab_w_in": ("ab", "w_in_t"), "ab_w_out": ("ab", "w_out"), "cd_w_in": ("cd", "w_in_t"), "cd_w_out": ("cd", "w_out"),
                 "xa_w_q": ("xa", "w_q"), "xa_w_kv": ("xa", "w_kv_t"), "xa_w_o": ("xa", "w_o"), "mlp_w_up": ("mlp", "w_up_t"),
                 "mlp_w_down": ("mlp", "w_down")}


def _sublayer(name, layer):
    block, leaf = SUBLAYER_KEYS[name]
    return (block if block in ("ab", "cd") else f"{block}{layer}"), leaf


def _set_big(params, full_rows):
    for (name, layer), rows in full_rows.items():
        block, leaf = _sublayer(name, layer)
        if name == "ab_w_in":
            rows = jnp.concatenate([rows, jnp.zeros((LANES - 8, PACK_COLS), rows.dtype)])
        params[block][leaf] = rows


def _build_params(full_rows, w):
    pad = lambda a: jnp.pad(a, ((0, 0), (0, LANES - a.shape[1])))
    params = {
        "ab": dict(norm=w["ab_norm"], conv_w=w["ab_conv_w"][0], conv_b=w["ab_conv_b"], w_a=_block_diag(w["lru_w_a"][0]),
                   b_a=w["lru_b_a"], w_i=_block_diag(w["lru_w_i"][0]), b_i=w["lru_b_i"], lam=w["lru_lambda"],
                   b_f=w["fox_b_f"].reshape(8, 1)),
        "cd": dict(norm=w["cd_norm"], sink=pad(w["cd_sink"])),
        "final_norm": w["final_norm"].reshape(1, D_MODEL),
    }
    for layer in range(2):
        params[f"xa{layer}"] = dict(norm=w["xa_norm"][layer:layer + 1], mem_norm=w["xa_mem_norm"][layer:layer + 1])
        params[f"mlp{layer}"] = dict(norm=w["mlp_norm"][layer:layer + 1])
    _set_big(params, full_rows)
    return params


def _grad_rows(g):
    out = []
    for name, layer, _, _ in ENTRIES:
        block, leaf = _sublayer(name, layer)
        out.append(g[block][leaf])
    return out


def _reduce_to_owner(grads_by_chip, core):
    half = PACK_ROWS // 2
    steps = half // REDUCE_TILE
    chip = 2 * lax.axis_index("x") + lax.axis_index("y")
    place = jnp.stack([core, chip]).astype(jnp.int32)
    got = _swap_cores(grads_by_chip, name="swap_cores", half_rows=half)
    block = (1, REDUCE_TILE, PACK_COLS)

    def sum_cores(place_ref, mine_ref, got_ref, f32_ref, bf16_ref):
        total = mine_ref[...].astype(F32) + got_ref[...].astype(F32)
        f32_ref[...] = total
        bf16_ref[...] = total.astype(BF16)

    pair32, pair16 = pl.pallas_call(
        sum_cores, name="sum_cores",
        grid_spec=pltpu.PrefetchScalarGridSpec(
            num_scalar_prefetch=1, grid=(N_CHIPS, steps),
            in_specs=[pl.BlockSpec(block, lambda s, i, place_ref: (s, place_ref[0] * steps + i, 0)),
                      pl.BlockSpec(block, lambda s, i, place_ref: (s, i, 0))],
            out_specs=[pl.BlockSpec(block, lambda s, i, place_ref: (s, i, 0))] * 2),
        out_shape=[jax.ShapeDtypeStruct((N_CHIPS, half, PACK_COLS), F32), jax.ShapeDtypeStruct((N_CHIPS, half, PACK_COLS), BF16)],
        compiler_params=_params("arbitrary", "arbitrary"),
    )(place, grads_by_chip, got)
    landed = _scatter_chips(pair16)

    def sum_chips(place_ref, own_ref, landed_ref, out_ref):
        out_ref[...] = own_ref[0] + landed_ref[0].astype(F32) + landed_ref[1].astype(F32) + landed_ref[2].astype(F32)

    done = pl.pallas_call(
        sum_chips, name="sum_chips",
        grid_spec=pltpu.PrefetchScalarGridSpec(
            num_scalar_prefetch=1, grid=(steps,),
            in_specs=[pl.BlockSpec(block, lambda i, place_ref: (place_ref[1], i, 0)),
                      pl.BlockSpec((3, REDUCE_TILE, PACK_COLS), lambda i, place_ref: (0, i, 0))],
            out_specs=pl.BlockSpec(block[1:], lambda i, place_ref: (i, 0))),
        out_shape=jax.ShapeDtypeStruct((half, PACK_COLS), F32), compiler_params=_params("arbitrary"),
    )(place, pair32, landed)
    theirs = _swap_cores(done, name="share_halves")
    return jnp.where(core == 0, jnp.concatenate([done, theirs]), jnp.concatenate([theirs, done]))


def kernel(x, mem, ab_norm, ab_w_in, ab_conv_w, ab_conv_b, lru_w_a, lru_b_a, lru_w_i, lru_b_i, lru_lambda, fox_b_f, ab_w_out, cd_norm, cd_w_in, cd_sink, cd_w_out, xa_norm, xa_mem_norm, xa_w_q, xa_w_kv, xa_w_o, mlp_norm, mlp_w_up, mlp_w_down, final_norm, loss_target, m_ab_norm, m_ab_w_in, m_ab_conv_w, m_ab_conv_b, m_lru_w_a, m_lru_b_a, m_lru_w_i, m_lru_b_i, m_lru_lambda, m_fox_b_f, m_ab_w_out, m_cd_norm, m_cd_w_in, m_cd_sink, m_cd_w_out, m_xa_norm, m_xa_mem_norm, m_xa_w_q, m_xa_w_kv, m_xa_w_o, m_mlp_norm, m_mlp_w_up, m_mlp_w_down, m_final_norm, v_ab_norm, v_ab_w_in, v_ab_conv_w, v_ab_conv_b, v_lru_w_a, v_lru_b_a, v_lru_w_i, v_lru_b_i, v_lru_lambda, v_fox_b_f, v_ab_w_out, v_cd_norm, v_cd_w_in, v_cd_sink, v_cd_w_out, v_xa_norm, v_xa_mem_norm, v_xa_w_q, v_xa_w_kv, v_xa_w_o, v_mlp_norm, v_mlp_w_up, v_mlp_w_down, v_final_norm):
    given = dict(locals())
    weight_names = [name for name, _ in SMALL if name != "loss"] + [name for name, _, _, _ in BIG]
    w = {name: given[name] for name in weight_names}
    chip = 2 * lax.axis_index("x") + lax.axis_index("y")
    core = lax.axis_index("c")

    slot = lax.broadcasted_iota(jnp.int32, (N_CHIPS, 1, 1), 0)

    def whole(entries, mine, gathered):
        return {(name, layer): jnp.where(slot == chip, own[None], got).reshape(N_CHIPS * rows, PACK_COLS)
                for (name, layer, rows, _), own, got in zip(entries, _unpack_rows(mine, entries), _unpack_rows(gathered, entries))}

    mine_first = _pack_rows(_shard_rows(w, FIRST_ENTRIES), FIRST_ENTRIES, BF16)
    mine_later = _pack_rows(_shard_rows(w, LATER_ENTRIES), LATER_ENTRIES, BF16)
    first = _gather_chips(mine_first)
    started = _gather_start(mine_later)
    owner = (core == 0).astype(F32)
    quarters = {name: _chip_embed(w[name], chip) * owner for name in SPLIT_SMALL}
    zeros = {name: jnp.zeros(shape, F32) for name, shape in SMALL}
    small_full = _unpack_small(_sum_all_devices(_pack_small({**zeros, **quarters})))
    params = _build_params(whole(FIRST_ENTRIES, mine_first, first),
                           {**w, "ab_conv_w": small_full["ab_conv_w"], "cd_norm": small_full["cd_norm"]})
    params["ab"]["norm"] = params["ab"]["norm"] + started[8][0, 0]

    def complete(h):
        _set_big(params, whole(LATER_ENTRIES, mine_later, _gather_pass(_gather_wait(started, after=h))))

    loss_part, grad_x, g = _local_step(x, mem, loss_target, params, complete)

    by_chip = _pack_rows([r.reshape(N_CHIPS, rows, PACK_COLS) for r, (_, _, rows, _) in zip(_grad_rows(g), ENTRIES)],
                         ENTRIES, BF16)
    grads = _shard_blocks(_unpack_rows(_reduce_to_owner(by_chip, core), ENTRIES))
    pair = lambda key, leaf: jnp.stack([g[f"{key}0"][leaf], g[f"{key}1"][leaf]])

    small_local = {"ab_norm": g["ab"]["norm"], "ab_conv_w": g["ab"]["conv_w"], "ab_conv_b": g["ab"]["conv_b"],
                   "lru_w_a": g["ab"]["w_a"], "lru_b_a": g["ab"]["b_a"], "lru_w_i": g["ab"]["w_i"], "lru_b_i": g["ab"]["b_i"],
                   "lru_lambda": g["ab"]["lam"], "fox_b_f": g["ab"]["b_f"], "cd_norm": g["cd"]["norm"], "cd_sink": g["cd"]["sink"],
                   "xa_norm": pair("xa", "norm"), "xa_mem_norm": pair("xa", "mem_norm"), "mlp_norm": pair("mlp", "norm"),
                   "final_norm": g["final_norm"], "loss": loss_part[0, :1]}
    small_sum_packed = _sum_all_devices(_pack_small(small_local))
    small_sum = _unpack_small(small_sum_packed)
    loss = small_sum["loss"][0]

    delta, new_m, new_v = {}, {}, {}
    for name, _, _, _ in BIG:
        shape = w[name].shape
        flat = lambda a: a.reshape(-1, shape[-1])
        d, m2, v2 = _adamw(flat(w[name]), flat(grads[name]), flat(given["m_" + name]), flat(given["v_" + name]), name=f"adamw_{name}")
        delta[name], new_m[name], new_v[name] = d.reshape(shape), m2.reshape(shape), v2.reshape(shape)

    def small_state(prefix):
        vals = {name: (given[prefix + name] if name != "loss" else jnp.zeros((1,), F32)) for name, _ in SMALL}
        for name in SPLIT_SMALL:
            vals[name] = _chip_embed(vals[name], chip)
        return _pack_small(vals)
    packed = _adamw(small_state(""), small_sum_packed, small_state("m_"), small_state("v_"), name="adamw_small")
    for store, vals in zip((delta, new_m, new_v), packed):
        for name, val in _unpack_small(vals).items():
            store[name] = _chip_part(val, chip) if name in SPLIT_SMALL else val
    for name in SPLIT_SMALL:
        small_sum[name] = _chip_part(small_sum[name], chip)
    grads.update({name: small_sum[name] for name, _ in SMALL})

    order = ["ab_norm", "ab_w_in", "ab_conv_w", "ab_conv_b", "lru_w_a", "lru_b_a", "lru_w_i", "lru_b_i", "lru_lambda", "fox_b_f",
             "ab_w_out", "cd_norm", "cd_w_in", "cd_sink", "cd_w_out", "xa_norm", "xa_mem_norm", "xa_w_q", "xa_w_kv", "xa_w_o",
             "mlp_norm", "mlp_w_up", "mlp_w_down", "final_norm"]
    return (loss, grad_x, *[grads[n] for n in order], *[delta[n] for n in order], *[new_m[n] for n in order],
            *[new_v[n] for n in order])
```

```python
import functools
import math

import jax
import jax.numpy as jnp
from jax import lax
from jax.experimental import pallas as pl
from jax.experimental.pallas import tpu as pltpu

F32 = jnp.float32
BF16 = jnp.bfloat16
MESH = pl.DeviceIdType.MESH

D_MODEL = 1024
SEQ = 2048
HEAD_DIM = 64
LRU_WIDTH = 512
LRU_BLOCKS = 8
LRU_C = 8.0
CONV_WIDTH = 4
ATT_W = 512
SWA_KW = 128
SWA_WINDOW = 128
DIL_PATTERN = ((128, 1), (512, 4), (2048, 16))
MEM_LEN = 256
XA_HEADS = 4
XA_HEAD_DIM = 256
D_FF = 4096
ROPE_THETA = 10000.0
EPS = 1e-6
N_CHIPS = 4
LANES = 128

ADAM_LR, ADAM_B1, ADAM_B2, ADAM_EPS, ADAM_WD, ADAM_STEP = 0.001, 0.9, 0.999, 1e-08, 0.01, 10

VMEM_LIMIT_BYTES = 56 * 1024 * 1024
MASKED = -1e30
ROW_TILE = 512
LRU_CHUNK = 512
ATT_BLOCK = 128
BAND_ROWS = 256
ATT_CHUNK = 512
CAUSAL_ROWS = 256


def _params(*sem):
    return pltpu.CompilerParams(dimension_semantics=sem, vmem_limit_bytes=VMEM_LIMIT_BYTES)


def _rowwise(fn, rows, consts=(), out_rows=(), out_accs=(), *, name, tile=ROW_TILE, row_maps=None):
    rows = [r if isinstance(r, tuple) else (r, r.shape[1], 0) for r in rows]
    consts = list(consts)
    nr, nc, no = len(rows), len(consts), len(out_rows)
    total = rows[0][0].shape[0]
    tile = min(tile, total)
    assert total % tile == 0
    n = total // tile

    def body(*refs):
        outs = fn(*[r[...] for r in refs[:nr + nc]])
        outs = tuple(outs) if isinstance(outs, (tuple, list)) else (outs,)
        for ref, val in zip(refs[nr + nc:nr + nc + no], outs[:no]):
            ref[...] = val.astype(ref.dtype)
        for ref, val in zip(refs[nr + nc + no:], outs[no:]):
            @pl.when(pl.program_id(0) == 0)
            def _(ref=ref):
                ref[...] = jnp.zeros(ref.shape, ref.dtype)
            ref[...] += val

    in_specs = []
    for idx, (arr, width, cb) in enumerate(rows):
        if row_maps is not None and row_maps[idx] is not None:
            in_specs.append(pl.BlockSpec((tile, width), row_maps[idx]))
        else:
            in_specs.append(pl.BlockSpec((tile, width), functools.partial(lambda i, cb: (i, cb), cb=cb)))
    in_specs += [pl.BlockSpec(c.shape, lambda i: (0, 0)) for c in consts]
    out_shape = [jax.ShapeDtypeStruct((total, w), dt) for w, dt in out_rows]
    out_shape += [jax.ShapeDtypeStruct(s, F32) for s in out_accs]
    out_specs = [pl.BlockSpec((tile, w), lambda i: (i, 0)) for w, _ in out_rows]
    out_specs += [pl.BlockSpec(s, lambda i: (0, 0)) for s in out_accs]
    return pl.pallas_call(
        body, grid=(n,), in_specs=in_specs, out_specs=out_specs, out_shape=out_shape, name=name,
        compiler_params=_params("arbitrary"),
    )(*[r[0] for r in rows], *consts)


MATMUL_VMEM_BYTES = 40 * 1024 * 1024


def _matmul_tiles(m, n, k, tm, tn, out_bytes):
    tm, tn, tk = min(tm, m), min(tn, n), k

    def need():
        return 2 * (2 * tk * (tm + tn) + tm * tn * out_bytes) + (0 if tk == k else 4 * tm * tn)

    while need() > MATMUL_VMEM_BYTES:
        if tm >= tn and tm % 256 == 0:
            tm //= 2
        elif tn % 256 == 0:
            tn //= 2
        else:
            tk //= 2
    return tm, tn, tk


def _matmul(a, b, *, ta=False, tb=False, outs=(F32,), epilogue=None, extras=(), consts=(), sums=(), tm=1024, tn=1024, name):
    m, k = (a.shape[1], a.shape[0]) if ta else a.shape
    n = b.shape[0] if tb else b.shape[1]
    assert (b.shape[1] if tb else b.shape[0]) == k
    out_bytes = sum(jnp.dtype(dt).itemsize for dt in outs) + sum(e.dtype.itemsize for e in extras)
    tm, tn, tk = _matmul_tiles(m, n, k, tm, tn, out_bytes)
    assert m % tm == 0 and n % tn == 0 and k % tk == 0, (name, m, n, k)
    nk = k // tk
    ne, nc, no = len(extras), len(consts), len(outs)
    assert not sums or tn == n, name
    dims = (((0 if ta else 1,), (1 if tb else 0,)), ((), ()))

    def body(*refs):
        a_ref, b_ref = refs[:2]
        e_refs, o_refs = refs[2:2 + ne + nc], refs[2 + ne + nc:2 + ne + nc + no]
        s_refs = refs[2 + ne + nc + no:2 + ne + nc + no + len(sums)]

        def finish(acc):
            vals = (acc,) if epilogue is None else epilogue(acc, *[e[...] for e in e_refs])
            for ref, val in zip(o_refs, vals[:no]):
                ref[...] = val.astype(ref.dtype)
            for ref, val in zip(s_refs, vals[no:]):
                @pl.when(pl.program_id(0) == 0)
                def _(ref=ref, val=val):
                    ref[...] = val

                @pl.when(pl.program_id(0) > 0)
                def _(ref=ref, val=val):
                    ref[...] += val

        prod = lax.dot_general(a_ref[...], b_ref[...], dims, preferred_element_type=F32)
        if nk == 1:
            finish(prod)
        else:
            acc_ref = refs[-1]
            step = pl.program_id(2)

            @pl.when(step == 0)
            def _():
                acc_ref[...] = prod

            @pl.when(step > 0)
            def _():
                acc_ref[...] += prod

            @pl.when(step == nk - 1)
            def _():
                finish(acc_ref[...])

    a_spec = pl.BlockSpec((tk, tm), lambda i, j, s: (s, i)) if ta else pl.BlockSpec((tm, tk), lambda i, j, s: (i, s))
    b_spec = pl.BlockSpec((tn, tk), lambda i, j, s: (j, s)) if tb else pl.BlockSpec((tk, tn), lambda i, j, s: (s, j))
    tile_spec = pl.BlockSpec((tm, tn), lambda i, j, s: (i, j))
    whole = lambda shape: pl.BlockSpec(shape, lambda i, j, s: (0, 0))
    return pl.pallas_call(
        body, grid=(m // tm, n // tn, nk),
        in_specs=[a_spec, b_spec] + [tile_spec] * ne + [whole(c.shape) for c in consts],
        out_specs=[tile_spec] * no + [whole(shape) for shape in sums],
        out_shape=[jax.ShapeDtypeStruct((m, n), dt) for dt in outs] + [jax.ShapeDtypeStruct(shape, F32) for shape in sums],
        scratch_shapes=[pltpu.VMEM((tm, tn), F32)] if nk > 1 else [],
        name=name, compiler_params=_params("arbitrary" if sums else "parallel", "parallel", "arbitrary"),
    )(a, b, *extras, *consts)


def _split3(x):
    x1 = x.astype(BF16)
    r1 = x - x1.astype(F32)
    x2 = r1.astype(BF16)
    x3 = (r1 - x2.astype(F32)).astype(BF16)
    return x1, x2, x3


def _dot_exact(x, e):
    return sum(jnp.dot(p, e, preferred_element_type=F32) for p in _split3(x))


def _head_expand(heads, width):
    dh = width // heads
    rows = jnp.arange(LANES)[:, None]
    cols = jnp.arange(width)[None, :]
    return (cols // dh == rows).astype(BF16)


def _rstd(x):
    return lax.rsqrt(jnp.mean(x * x, axis=-1, keepdims=True) + EPS)


def _rms_fwd(x, gain, *, name):
    return _rowwise(lambda x, g: x * _rstd(x) * g, [x], [gain], [(x.shape[1], BF16)], name=name)[0]


def _rms_bwd_vals(x, dhn, gain):
    r = _rstd(x)
    xh = x * r
    dxh = dhn * gain
    dx = r * (dxh - xh * jnp.mean(dxh * xh, axis=-1, keepdims=True))
    return dx, jnp.sum(dhn * xh, axis=0, keepdims=True)


def _norm_input_grad(dz, w, x, dres, gain, *, tb=False, name):
    def epilogue(dhn, x, dres, g):
        dx, dg = _rms_bwd_vals(x, dhn, g)
        dh = dres + dx
        return dh, dh, dg
    return _matmul(dz, w, tb=tb, outs=(F32, BF16), extras=(x, dres), consts=(gain,), sums=((1, x.shape[1]),), epilogue=epilogue,
                   name=name)


def _loss_and_grad(h, target, gain, *, name):
    def fn(h, tgt, g):
        r = _rstd(h)
        err = h * r * g - tgt
        loss = 0.5 * jnp.sum(jnp.mean(err * err, axis=-1, keepdims=True), axis=0, keepdims=True)
        dx, dg = _rms_bwd_vals(h, err * (1.0 / D_MODEL), g)
        return dx, dx, jnp.broadcast_to(loss, (1, LANES)), dg
    d = h.shape[1]
    return _rowwise(fn, [h, target], [gain], [(d, F32), (d, BF16)], [(1, LANES), (1, d)], name=name)


def _adamw(w, g, m, v, *, name):
    rows, cols = w.shape
    tile = rows
    for cand in (256, 128, 64, 32, 16, 8):
        if rows % cand == 0 and rows > cand:
            tile = cand
            break
    bc1 = 1.0 - ADAM_B1 ** ADAM_STEP
    bc2 = 1.0 - ADAM_B2 ** ADAM_STEP

    def fn(w, g, m, v):
        m2 = ADAM_B1 * m + (1.0 - ADAM_B1) * g
        v2 = ADAM_B2 * v + (1.0 - ADAM_B2) * (g * g)
        delta = -ADAM_LR * ((m2 / bc1) / (jnp.sqrt(v2 / bc2) + ADAM_EPS) + ADAM_WD * w)
        return delta, m2, v2
    return _rowwise(fn, [w, g, m, v], [], [(cols, F32)] * 3, name=name, tile=tile)


def _attn_specs(arr, width, cb, classes, rows_block, whole):
    ncols = arr.shape[2] // (classes * width)
    if whole:
        return pl.BlockSpec((1, arr.shape[1], width), lambda n, r, i: (n, 0, r * ncols + cb))
    return pl.BlockSpec((1, rows_block, width), lambda n, r, i: (n, i, r * ncols + cb))


def _attn_tiles(mode, lq, lk):
    if mode == "full":
        return min(lq, 256), lk
    if mode == "band":
        tq = min(BAND_ROWS, lq)
        return tq, min(tq + ATT_BLOCK, lk)
    return CAUSAL_ROWS, ATT_CHUNK


def _window(mode, i, j, tq, win, lk):
    if mode == "causal":
        return pl.multiple_of(j * win, win), (i * tq) // win + 1
    if mode == "band":
        return pl.multiple_of(jnp.clip(i * tq + tq - win, 0, lk - win), ATT_BLOCK), 1
    return 0, 1


def _allowed(mode, i, start, tq, win, max_dist):
    if mode == "full":
        return None
    dist = (i * tq + lax.broadcasted_iota(jnp.int32, (tq, win), 0)) - (start + lax.broadcasted_iota(jnp.int32, (tq, win), 1))
    ok = dist >= 0
    if max_dist is not None:
        ok = ok & (dist <= max_dist)
    return ok


def _head_groups(heads, dh):
    gw = max(LANES, dh)
    return gw, gw // dh, heads // (gw // dh)


def _own_lanes(rows, gw, dh, u):
    return lax.broadcasted_iota(jnp.int32, (rows, gw), 1) // dh == u


def _only_head(x, own, per, u):
    return x if per == 1 else jnp.where(own[u], x, jnp.zeros_like(x))


def _step_scores(qz, kw, kb_row, ok, scale):
    s = lax.dot_general(qz, kw, (((1,), (1,)), ((), ())), preferred_element_type=F32) * scale
    if kb_row is not None:
        s = s - kb_row
    if ok is not None:
        s = jnp.where(ok, s, MASKED)
    return s


def _attn_fwd(q, k, v, kb=None, *, classes=1, heads, dh, mode, max_dist=None, name):
    (qa, qc), (ka, kc), (va, vc) = q, k, v
    n, lq, lk = qa.shape[0], qa.shape[1], ka.shape[1]
    width = heads * dh
    tq, win = _attn_tiles(mode, lq, lk)
    gw, per, groups = _head_groups(heads, dh)
    scale = dh ** -0.5
    has_kb = kb is not None
    single = mode != "causal"

    def body(*refs):
        q_ref, k_ref, v_ref = refs[:3]
        kb_ref = refs[3] if has_kb else None
        o_ref, lse_ref = refs[-2:]
        i = pl.program_id(2)
        lane = lax.broadcasted_iota(jnp.int32, (tq, LANES), 1)
        own = [_own_lanes(tq, gw, dh, u) for u in range(per)]

        def step(j, carry):
            m_in, l_in = carry
            m_out, l_out = m_in, l_in
            start, _ = _window(mode, i, j, tq, win, lk)
            ok = _allowed(mode, i, start, tq, win, max_dist)
            for g in range(groups):
                cols = slice(g * gw, (g + 1) * gw)
                qg = q_ref[0, :, cols]
                kw = k_ref[0, pl.ds(start, win), cols]
                vw = v_ref[0, pl.ds(start, win), cols]
                alpha_g = new_g = None
                for u in range(per):
                    h = g * per + u
                    kb_row = kb_ref[0, h, pl.ds(j, 1), :] if has_kb else None
                    s = _step_scores(_only_head(qg, own, per, u), kw, kb_row, ok, scale)
                    m_new = jnp.max(s, axis=1, keepdims=True)
                    if not single:
                        m_old = m_in[:, h:h + 1]
                        m_new = jnp.maximum(m_old, m_new)
                        alpha = jnp.exp(m_old - m_new)
                        alpha_g = alpha if u == 0 else jnp.where(own[u], alpha, alpha_g)
                    p = jnp.exp(s - m_new)
                    l_new = jnp.sum(p, axis=1, keepdims=True)
                    r = jnp.dot(p.astype(BF16), vw, preferred_element_type=F32)
                    if single:
                        r = r * (1.0 / l_new)
                    else:
                        l_new = alpha * l_in[:, h:h + 1] + l_new
                    new_g = r if u == 0 else jnp.where(own[u], r, new_g)
                    m_out = jnp.where(lane == h, m_new, m_out)
                    l_out = jnp.where(lane == h, l_new, l_out)
                o_ref[0, :, cols] = new_g if single else alpha_g * o_ref[0, :, cols] + new_g
            return m_out, l_out

        carry = (jnp.full((tq, LANES), MASKED, F32), jnp.zeros((tq, LANES), F32))
        if single:
            m_all, l_all = step(0, carry)
            l_all = jnp.where(lane < heads, l_all, 1.0)
        else:
            o_ref[...] = jnp.zeros(o_ref.shape, F32)
            m_all, l_all = lax.fori_loop(0, _window(mode, i, 0, tq, win, lk)[1], step, carry)
            l_all = jnp.where(lane < heads, l_all, 1.0)
            inv = 1.0 / l_all
            for g in range(groups):
                cols = slice(g * gw, (g + 1) * gw)
                inv_g = inv[:, g * per:g * per + 1]
                for u in range(1, per):
                    inv_g = jnp.where(own[u], inv[:, g * per + u:g * per + u + 1], inv_g)
                o_ref[0, :, cols] = o_ref[0, :, cols] * inv_g
        lse_ref[0] = jnp.where(lane < heads, m_all + jnp.log(l_all), 0.0)

    in_specs = [_attn_specs(qa, width, qc, classes, tq, False), _attn_specs(ka, width, kc, classes, win, True),
                _attn_specs(va, width, vc, classes, win, True)]
    args = [qa, ka, va]
    if has_kb:
        in_specs.append(pl.BlockSpec((1,) + kb.shape[1:], lambda n_, r, i: (n_, 0, 0, 0)))
        args.append(kb)
    out_shape = [jax.ShapeDtypeStruct((n, lq, classes * width), F32), jax.ShapeDtypeStruct((n, lq, classes * LANES), F32)]
    out_specs = [pl.BlockSpec((1, tq, width), lambda n_, r, i: (n_, i, r)), pl.BlockSpec((1, tq, LANES), lambda n_, r, i: (n_, i, r))]
    return pl.pallas_call(
        body, grid=(n, classes, lq // tq), in_specs=in_specs, out_specs=out_specs, out_shape=out_shape, name=name,
        compiler_params=_params("parallel", "parallel", "arbitrary"),
    )(*args)


def _attn_bwd(q, k, v, do, lse, delta, kb=None, *, classes=1, heads, dh, mode, max_dist=None, dq_dtype=F32, name):
    (qa, qc), (ka, kc), (va, vc), (da, dc) = q, k, v, do
    n, lq, lk = qa.shape[0], qa.shape[1], ka.shape[1]
    width = heads * dh
    tq, win = _attn_tiles(mode, lq, lk)
    gw, per, groups = _head_groups(heads, dh)
    scale = dh ** -0.5
    has_kb = kb is not None
    single = mode != "causal"
    nt = (((1,), (1,)), ((), ()))
    tn = (((0,), (0,)), ((), ()))

    def body(*refs):
        q_ref, k_ref, v_ref, do_ref, lse_ref, dl_ref = refs[:6]
        kb_ref = refs[6] if has_kb else None
        n_in = 7 if has_kb else 6
        dq_ref, dk_ref, dv_ref = refs[n_in:n_in + 3]
        dkb_ref, drow_ref = refs[n_in + 3:n_in + 5] if has_kb else (None, None)
        i = pl.program_id(2)

        @pl.when(i == 0)
        def _():
            dk_ref[...] = jnp.zeros(dk_ref.shape, F32)
            dv_ref[...] = jnp.zeros(dv_ref.shape, F32)
            if has_kb:
                dkb_ref[...] = jnp.zeros(dkb_ref.shape, F32)

        lse_all = lse_ref[0]
        dl_all = dl_ref[0]
        lane = lax.broadcasted_iota(jnp.int32, (tq, LANES), 1)
        own = [_own_lanes(tq, gw, dh, u) for u in range(per)]

        def step(j, drow_all):
            start, _ = _window(mode, i, j, tq, win, lk)
            ok = _allowed(mode, i, start, tq, win, max_dist)
            for g in range(groups):
                cols = slice(g * gw, (g + 1) * gw)
                qg = q_ref[0, :, cols]
                dog = do_ref[0, :, cols]
                kw = k_ref[0, pl.ds(start, win), cols]
                vw = v_ref[0, pl.ds(start, win), cols]
                dq_g = dk_w = dv_w = None
                for u in range(per):
                    h = g * per + u
                    qz, doz = _only_head(qg, own, per, u), _only_head(dog, own, per, u)
                    kb_row = kb_ref[0, h, pl.ds(j, 1), :] if has_kb else None
                    p = jnp.exp(_step_scores(qz, kw, kb_row, ok, scale) - lse_all[:, h:h + 1])
                    dp = lax.dot_general(doz, vw, nt, preferred_element_type=F32)
                    ds = p * (dp - dl_all[:, h:h + 1])
                    dsb = ds.astype(BF16)
                    r = jnp.dot(dsb, kw, preferred_element_type=F32)
                    dq_g = r if u == 0 else jnp.where(own[u], r, dq_g)
                    dk_u = lax.dot_general(dsb, qz, tn, preferred_element_type=F32)
                    dv_u = lax.dot_general(p.astype(BF16), doz, tn, preferred_element_type=F32)
                    dk_w = dk_u if u == 0 else dk_w + dk_u
                    dv_w = dv_u if u == 0 else dv_w + dv_u
                    if has_kb:
                        dkb_ref[0, h, pl.ds(j, 1), :] += -jnp.sum(ds, axis=0, keepdims=True)
                        drow_all = drow_all + jnp.where(lane == h, jnp.sum(ds, axis=1, keepdims=True), 0.0)
                dk_ref[0, pl.ds(start, win), cols] += dk_w * scale
                dv_ref[0, pl.ds(start, win), cols] += dv_w
                if single:
                    dq_ref[0, :, cols] = (dq_g * scale).astype(dq_ref.dtype)
                else:
                    dq_ref[0, :, cols] += dq_g * scale
            return drow_all

        drow_all = jnp.zeros((tq, LANES), F32)
        if single:
            drow_all = step(0, drow_all)
        else:
            dq_ref[...] = jnp.zeros(dq_ref.shape, F32)
            drow_all = lax.fori_loop(0, _window(mode, i, 0, tq, win, lk)[1], step, drow_all)
        if has_kb:
            drow_ref[0] = drow_all

    row_spec = functools.partial(_attn_specs, classes=classes, rows_block=tq, whole=False)
    in_specs = [row_spec(qa, width, qc), _attn_specs(ka, width, kc, classes, win, True), _attn_specs(va, width, vc, classes, win, True),
                row_spec(da, width, dc), row_spec(lse, LANES, 0), row_spec(delta, LANES, 0)]
    args = [qa, ka, va, da, lse, delta]
    assert single or dq_dtype == F32
    out_shape = [jax.ShapeDtypeStruct((n, lq, classes * width), dq_dtype), jax.ShapeDtypeStruct((n, lk, classes * width), F32),
                 jax.ShapeDtypeStruct((n, lk, classes * width), F32)]
    kv_spec = pl.BlockSpec((1, lk, width), lambda n_, r, i: (n_, 0, r))
    out_specs = [pl.BlockSpec((1, tq, width), lambda n_, r, i: (n_, i, r)), kv_spec, kv_spec]
    if has_kb:
        kb_spec = pl.BlockSpec((1,) + kb.shape[1:], lambda n_, r, i: (n_, 0, 0, 0))
        in_specs.append(kb_spec)
        args.append(kb)
        out_shape += [jax.ShapeDtypeStruct(kb.shape, F32), jax.ShapeDtypeStruct((n, lq, LANES), F32)]
        out_specs += [kb_spec, pl.BlockSpec((1, tq, LANES), lambda n_, r, i: (n_, i, 0))]
    return pl.pallas_call(
        body, grid=(n, classes, lq // tq), in_specs=in_specs, out_specs=out_specs, out_shape=out_shape, name=name,
        compiler_params=_params("parallel", "parallel", "arbitrary"),
    )(*args)


def _head_delta(do, out, heads, *, name, lse=None, sink=None):
    width = out.shape[1]
    gather = _head_expand(heads, width).T

    if sink is None:
        return _rowwise(lambda do, o, e: _dot_exact(do * o, e), [do, out], [gather], [(LANES, F32)], name=name)[0]

    def fn(do, o, lse, e, sink):
        delta = _dot_exact(do * o, e)
        return delta, -jnp.sum(jnp.exp(sink - lse) * delta, axis=0, keepdims=True)
    return _rowwise(fn, [do, out, lse], [gather, sink], [(LANES, F32)], [(1, LANES)], name=name)


def _rope_tables():
    half = HEAD_DIM // 2
    inv = ROPE_THETA ** (-jnp.arange(half, dtype=F32) / half)
    ang = jnp.arange(SEQ, dtype=F32)[:, None] * inv[None, :]
    cos = jnp.tile(jnp.cos(ang), (1, 2 * ATT_W // HEAD_DIM))
    sin = jnp.tile(jnp.concatenate([-jnp.sin(ang), jnp.sin(ang)], axis=1), (1, ATT_W // HEAD_DIM))
    return cos, sin


def _rotate(x, cos, sin):
    width = x.shape[1]
    lane = lax.broadcasted_iota(jnp.int32, x.shape, 1)
    first_half = (lane % HEAD_DIM) < (HEAD_DIM // 2)
    swapped = jnp.where(first_half, pltpu.roll(x, width - HEAD_DIM // 2, axis=1), pltpu.roll(x, HEAD_DIM // 2, axis=1))
    return x * cos[:, :width] + swapped * sin[:, :width]


def _gelu(x):
    k = math.sqrt(2.0 / math.pi)
    t = jnp.tanh(k * (x + 0.044715 * x * x * x))
    return 0.5 * x * (1.0 + t), t


def _gelu_grad(x, t):
    k = math.sqrt(2.0 / math.pi)
    return 0.5 * (1.0 + t) + 0.5 * x * (1.0 - t * t) * k * (1.0 + 3.0 * 0.044715 * x * x)


def _shift_down(x, halo, s):
    ext = jnp.concatenate([halo, x], axis=0)
    return pltpu.roll(ext, s, axis=0)[8:, :]


def _shift_up(x, halo, s):
    rows = x.shape[0]
    ext = jnp.concatenate([x, halo], axis=0)
    return pltpu.roll(ext, rows + 8 - s, axis=0)[:rows, :]


def _lru_gates(u, halo, cw, cb, wa, ba, wi, bi, lam):
    taps = [_shift_down(u, halo, CONV_WIDTH - 1 - k) for k in range(CONV_WIDTH - 1)] + [u]
    uc = cb + sum(cw[k:k + 1, :] * taps[k] for k in range(CONV_WIDTH))
    ucb = uc.astype(BF16)
    r = jax.nn.sigmoid(jnp.dot(ucb, wa, preferred_element_type=F32) + ba)
    gi = jax.nn.sigmoid(jnp.dot(ucb, wi, preferred_element_type=F32) + bi)
    sp = jnp.maximum(-lam, 0.0) + jnp.log(1.0 + jnp.exp(-jnp.abs(lam)))
    log_a = -LRU_C * r * sp
    a = jnp.exp(log_a)
    x2 = 2.0 * log_a
    one_minus_a2 = jnp.where(x2 > -0.01, -x2 * (1.0 + x2 * (0.5 + x2 * (1.0 / 6.0 + x2 / 24.0))), 1.0 - jnp.exp(x2))
    mult = jnp.sqrt(one_minus_a2)
    return taps, uc, ucb, r, gi, sp, a, mult


def _lru_specs(nchunk, reverse):
    def chunk(b, c):
        return b * nchunk + ((nchunk - 1 - c) if reverse else c)

    def halo_before(b, c):
        return jnp.maximum(chunk(b, c) * (LRU_CHUNK // 8) - 1, 0)

    return chunk, halo_before


def _lru_fwd(zug, cw, cb, wa, ba, wi, bi, lam, *, name):
    total = zug.shape[0]
    nchunk = SEQ // LRU_CHUNK
    w = LRU_WIDTH
    chunk, halo_before = _lru_specs(nchunk, False)

    def body(u_ref, uh_ref, g_ref, cw_ref, cb_ref, wa_ref, ba_ref, wi_ref, bi_ref, lam_ref, y_ref, h_ref, a_sc, x_sc, carry_sc):
        c = pl.program_id(1)
        u = u_ref[...]
        halo = jnp.where(c > 0, uh_ref[...], 0.0)
        _, uc, _, _, gi, _, a, mult = _lru_gates(u, halo, cw_ref[...], cb_ref[...], wa_ref[...], ba_ref[...],
                                                 wi_ref[...], bi_ref[...], lam_ref[...])
        a_sc[...] = a
        x_sc[...] = mult * (gi * uc)

        @pl.when(c == 0)
        def _():
            carry_sc[...] = jnp.zeros(carry_sc.shape, F32)

        def tile_step(t, h):
            r0 = pl.multiple_of(t * 8, 8)
            at = a_sc[pl.ds(r0, 8), :]
            xt = x_sc[pl.ds(r0, 8), :]
            rows = []
            for j in range(8):
                h = at[j:j + 1, :] * h + xt[j:j + 1, :]
                rows.append(h)
            h_ref[pl.ds(r0, 8), :] = jnp.concatenate(rows, axis=0)
            return h

        h_last = lax.fori_loop(0, LRU_CHUNK // 8, tile_step, carry_sc[0:1, :])
        carry_sc[0:1, :] = h_last
        gel, _ = _gelu(g_ref[...])
        y_ref[...] = (h_ref[...] * gel).astype(BF16)

    small = lambda arr: pl.BlockSpec(arr.shape, lambda b, c: (0, 0))
    return pl.pallas_call(
        body, grid=(total // SEQ, nchunk),
        in_specs=[pl.BlockSpec((LRU_CHUNK, w), lambda b, c: (chunk(b, c), 0)),
                  pl.BlockSpec((8, w), lambda b, c: (halo_before(b, c), 0)),
                  pl.BlockSpec((LRU_CHUNK, w), lambda b, c: (chunk(b, c), 1)),
                  small(cw), small(cb), small(wa), small(ba), small(wi), small(bi), small(lam)],
        out_specs=[pl.BlockSpec((LRU_CHUNK, w), lambda b, c: (chunk(b, c), 0))] * 2,
        out_shape=[jax.ShapeDtypeStruct((total, w), BF16), jax.ShapeDtypeStruct((total, w), F32)],
        scratch_shapes=[pltpu.VMEM((LRU_CHUNK, w), F32), pltpu.VMEM((LRU_CHUNK, w), F32), pltpu.VMEM((8, w), F32)],
        name=name, compiler_params=_params("arbitrary", "arbitrary"),
    )(zug, zug, zug, cw, cb, wa, ba, wi, bi, lam)


def _lru_bwd(zug, hl, dy, cw, cb, wa, ba, wi, bi, lam, *, name):
    total = zug.shape[0]
    nchunk = SEQ // LRU_CHUNK
    w = LRU_WIDTH
    chunk, halo_before = _lru_specs(nchunk, True)
    tn = (((0,), (0,)), ((), ()))
    nt = (((1,), (1,)), ((), ()))

    def body(u_ref, uh_ref, g_ref, hl_ref, hh_ref, dy_ref, cw_ref, cb_ref, wa_ref, ba_ref, wi_ref, bi_ref, lam_ref,
             dz_ref, dcw_ref, dcb_ref, dwa_ref, dba_ref, dwi_ref, dbi_ref, dlam_ref,
             a_sc, d_sc, g_sc, gcarry_sc, acarry_sc, duc_sc):
        b, c = pl.program_id(0), pl.program_id(1)
        first_chunk = c == nchunk - 1

        @pl.when((b == 0) & (c == 0))
        def _():
            for ref in (dcw_ref, dcb_ref, dwa_ref, dba_ref, dwi_ref, dbi_ref, dlam_ref):
                ref[...] = jnp.zeros(ref.shape, F32)

        @pl.when(c == 0)
        def _():
            gcarry_sc[...] = jnp.zeros(gcarry_sc.shape, F32)
            acarry_sc[...] = jnp.zeros(acarry_sc.shape, F32)
            duc_sc[...] = jnp.zeros(duc_sc.shape, F32)

        u = u_ref[...]
        halo = jnp.where(first_chunk, 0.0, uh_ref[...])
        cw, wa, wi, lam = cw_ref[...], wa_ref[...], wi_ref[...], lam_ref[...]
        taps, uc, ucb, r, gi, sp, a, mult = _lru_gates(u, halo, cw, cb_ref[...], wa, ba_ref[...], wi, bi_ref[...], lam)
        gate = g_ref[...]
        gel, th = _gelu(gate)
        dy = dy_ref[...]
        hl = hl_ref[...]
        a_sc[...] = a
        d_sc[...] = dy * gel
        dgate = dy * hl * _gelu_grad(gate, th)

        def tile_step(t, carry):
            g_next, a_next = carry
            r0 = pl.multiple_of((LRU_CHUNK // 8 - 1 - t) * 8, 8)
            dt = d_sc[pl.ds(r0, 8), :]
            at = a_sc[pl.ds(r0, 8), :]
            rows = [None] * 8
            for j in reversed(range(8)):
                g_next = dt[j:j + 1, :] + a_next * g_next
                a_next = at[j:j + 1, :]
                rows[j] = g_next
            g_sc[pl.ds(r0, 8), :] = jnp.concatenate(rows, axis=0)
            return g_next, a_next

        g_last, a_last = lax.fori_loop(0, LRU_CHUNK // 8, tile_step, (gcarry_sc[0:1, :], acarry_sc[0:1, :]))
        gcarry_sc[0:1, :] = g_last
        acarry_sc[0:1, :] = a_last

        gs = g_sc[...]
        hl_halo = jnp.where(first_chunk, 0.0, hh_ref[...])
        da = gs * _shift_down(hl, hl_halo, 1)
        dmult = gs * gi * uc
        dgi = gs * mult * uc
        duc = gs * mult * gi
        dlog_a = da * a - dmult * a * a / mult
        dr = dlog_a * (-LRU_C * sp)
        dsp = jnp.sum(dlog_a * (-LRU_C * r), axis=0, keepdims=True)
        dlam_ref[...] += -dsp * jax.nn.sigmoid(-lam)
        dpa = dr * r * (1.0 - r)
        dpi = dgi * gi * (1.0 - gi)
        dpab, dpib = dpa.astype(BF16), dpi.astype(BF16)
        dba_ref[...] += jnp.sum(dpa, axis=0, keepdims=True)
        dbi_ref[...] += jnp.sum(dpi, axis=0, keepdims=True)
        dwa_ref[...] += lax.dot_general(ucb, dpab, tn, preferred_element_type=F32)
        dwi_ref[...] += lax.dot_general(ucb, dpib, tn, preferred_element_type=F32)
        duc = duc + lax.dot_general(dpab, wa, nt, preferred_element_type=F32)
        duc = duc + lax.dot_general(dpib, wi, nt, preferred_element_type=F32)
        dcb_ref[...] += jnp.sum(duc, axis=0, keepdims=True)
        dcw_ref[...] += jnp.concatenate([jnp.sum(duc * taps[k], axis=0, keepdims=True) for k in range(CONV_WIDTH)], axis=0)
        after = duc_sc[...]
        du = cw[CONV_WIDTH - 1:CONV_WIDTH, :] * duc
        for k in range(CONV_WIDTH - 1):
            du = du + cw[k:k + 1, :] * _shift_up(duc, after, CONV_WIDTH - 1 - k)
        duc_sc[...] = duc[0:8, :]
        dz_ref[:, 0:w] = du.astype(BF16)
        dz_ref[:, w:2 * w] = dgate.astype(BF16)

    small = lambda arr: pl.BlockSpec(arr.shape, lambda b, c: (0, 0))
    acc = lambda shape: pl.BlockSpec(shape, lambda b, c: (0, 0))
    chunk_spec = lambda cb_: pl.BlockSpec((LRU_CHUNK, w), lambda b, c: (chunk(b, c), cb_))
    halo_spec = pl.BlockSpec((8, w), lambda b, c: (halo_before(b, c), 0))
    acc_shapes = [(CONV_WIDTH, w), (1, w), (w, w), (1, w), (w, w), (1, w), (1, w)]
    return pl.pallas_call(
        body, grid=(total // SEQ, nchunk),
        in_specs=[chunk_spec(0), halo_spec, chunk_spec(1), chunk_spec(0), halo_spec, chunk_spec(0),
                  small(cw), small(cb), small(wa), small(ba), small(wi), small(bi), small(lam)],
        out_specs=[pl.BlockSpec((LRU_CHUNK, 2 * w), lambda b, c: (chunk(b, c), 0))] + [acc(s) for s in acc_shapes],
        out_shape=[jax.ShapeDtypeStruct((total, 2 * w), BF16)] + [jax.ShapeDtypeStruct(s, F32) for s in acc_shapes],
        scratch_shapes=[pltpu.VMEM((LRU_CHUNK, w), F32)] * 3 + [pltpu.VMEM((8, w), F32)] * 3,
        name=name, compiler_params=_params("arbitrary", "arbitrary"),
    )(zug, zug, zug, hl, hl, dy, cw, cb, wa, ba, wi, bi, lam)


def _block_diag(wb):
    eye = jnp.eye(LRU_BLOCKS, dtype=wb.dtype)
    return jnp.einsum("gcd,gh->gchd", wb, eye).reshape(LRU_WIDTH, LRU_WIDTH).astype(BF16)


def _diag_blocks(wd):
    blk = LRU_WIDTH // LRU_BLOCKS
    return jnp.stack([wd[g * blk:(g + 1) * blk, g * blk:(g + 1) * blk] for g in range(LRU_BLOCKS)])


FOX_SCAN = 256


def _fox_bias(fl, bf, *, name):
    nb = fl.shape[0]

    def body(fl_ref, bf_ref, c_ref):
        x = fl_ref[0] + bf_ref[...]
        logf = jnp.minimum(x, 0.0) - jnp.log(1.0 + jnp.exp(-jnp.abs(x)))
        upper = (lax.broadcasted_iota(jnp.int32, (FOX_SCAN, FOX_SCAN), 0)
                 <= lax.broadcasted_iota(jnp.int32, (FOX_SCAN, FOX_SCAN), 1)).astype(BF16)
        carry = jnp.zeros((LRU_BLOCKS, 1), F32)
        for j in range(SEQ // FOX_SCAN):
            blk = logf[:, j * FOX_SCAN:(j + 1) * FOX_SCAN]
            c_ref[0, :, j * FOX_SCAN:(j + 1) * FOX_SCAN] = _dot_exact(blk, upper) + carry
            carry = carry + jnp.sum(blk, axis=1, keepdims=True)

    return pl.pallas_call(
        body, grid=(nb,),
        in_specs=[pl.BlockSpec((1, 8, SEQ), lambda b: (b, 0, 0)), pl.BlockSpec((8, 1), lambda b: (0, 0))],
        out_specs=pl.BlockSpec((1, 8, SEQ), lambda b: (b, 0, 0)),
        out_shape=jax.ShapeDtypeStruct((nb, 8, SEQ), F32), name=name, compiler_params=_params("arbitrary"),
    )(fl, bf)


def _fox_bias_bwd(fl, bf, dc, *, name):
    nb = fl.shape[0]

    def body(fl_ref, bf_ref, dc_ref, dfl_ref, dbf_ref):
        @pl.when(pl.program_id(0) == 0)
        def _():
            dbf_ref[...] = jnp.zeros(dbf_ref.shape, F32)

        x = fl_ref[0] + bf_ref[...]
        dc_all = dc_ref[0]
        lower = (lax.broadcasted_iota(jnp.int32, (FOX_SCAN, FOX_SCAN), 0)
                 >= lax.broadcasted_iota(jnp.int32, (FOX_SCAN, FOX_SCAN), 1)).astype(BF16)
        carry = jnp.zeros((LRU_BLOCKS, 1), F32)
        total = jnp.zeros((LRU_BLOCKS, 1), F32)
        for j in reversed(range(SEQ // FOX_SCAN)):
            cols = slice(j * FOX_SCAN, (j + 1) * FOX_SCAN)
            blk = dc_all[:, cols]
            dlogf = _dot_exact(blk, lower) + carry
            carry = carry + jnp.sum(blk, axis=1, keepdims=True)
            dx = dlogf * jax.nn.sigmoid(-x[:, cols])
            dfl_ref[0, :, cols] = dx
            total = total + jnp.sum(dx, axis=1, keepdims=True)
        dbf_ref[...] += total

    return pl.pallas_call(
        body, grid=(nb,),
        in_specs=[pl.BlockSpec((1, 8, SEQ), lambda b: (b, 0, 0)), pl.BlockSpec((8, 1), lambda b: (0, 0)),
                  pl.BlockSpec((1, 8, SEQ), lambda b: (b, 0, 0))],
        out_specs=[pl.BlockSpec((1, 8, SEQ), lambda b: (b, 0, 0)), pl.BlockSpec((8, 1), lambda b: (0, 0))],
        out_shape=[jax.ShapeDtypeStruct((nb, 8, SEQ), F32), jax.ShapeDtypeStruct((8, 1), F32)],
        name=name, compiler_params=_params("arbitrary"),
    )(fl, bf, dc)


def _seq3(a, nb):
    return a.reshape(nb, a.shape[0] // nb, a.shape[1])


def _flat2(a):
    return a.reshape(a.shape[0] * a.shape[1], a.shape[2])


def _mlp_fwd(h, p, tag):
    hn = _rms_fwd(h, p["norm"], name=f"{tag}_norm")
    act, = _matmul(hn, p["w_up_t"], tb=True, outs=(BF16,), epilogue=lambda acc: (jnp.square(jnp.maximum(acc, 0.0)),),
                   name=f"{tag}_up")
    out, = _matmul(act, p["w_down"], extras=(h,), epilogue=lambda acc, res: (acc + res,), name=f"{tag}_down")
    return out, (h, hn, act)


def _mlp_bwd(dh, dhb, p, saved, tag):
    h, hn, act = saved
    dup, = _matmul(dhb, p["w_down"], tb=True, outs=(BF16,), extras=(act,),
                   epilogue=lambda acc, act: (acc * (2.0 * jnp.sqrt(act.astype(F32))),), name=f"{tag}_bwd_dup")
    dw_down, = _matmul(act, dhb, ta=True, outs=(BF16,),name=f"{tag}_bwd_dwdown")
    dw_up_t, = _matmul(dup, hn, ta=True, outs=(BF16,),name=f"{tag}_bwd_dwup")
    dh2, dh2b, dg = _norm_input_grad(dup, p["w_up_t"], h, dh, p["norm"], name=f"{tag}_bwd_dh")
    return dh2, dh2b, {"norm": dg, "w_up_t": dw_up_t, "w_down": dw_down}


def _xa_fwd(h, mem, p, tag, nb):
    hn = _rms_fwd(h, p["norm"], name=f"{tag}_norm")
    mem_n = _rms_fwd(mem, p["mem_norm"], name=f"{tag}_memnorm")
    q, = _matmul(hn, p["w_q"], outs=(BF16,), name=f"{tag}_q")
    kv, = _matmul(mem_n, p["w_kv_t"], tb=True, outs=(BF16,), name=f"{tag}_kv")
    q3, kv3 = _seq3(q, nb), _seq3(kv, nb)
    o, lse = _attn_fwd((q3, 0), (kv3, 0), (kv3, 1), heads=XA_HEADS, dh=XA_HEAD_DIM, mode="full",
                       name=f"{tag}_attn")
    o = _flat2(o)
    ob, = _rowwise(lambda o: o, [o], [], [(D_MODEL, BF16)], name=f"{tag}_ocast")
    out, = _matmul(ob, p["w_o"], extras=(h,), epilogue=lambda acc, res: (acc + res,), name=f"{tag}_o")
    return out, (h, hn, mem_n, q3, kv3, o, ob, lse)


def _xa_bwd(dh, dhb, mem, p, saved, tag, nb):
    h, hn, mem_n, q3, kv3, o, ob, lse = saved
    do, dob = _matmul(dhb, p["w_o"], tb=True, outs=(F32, BF16), epilogue=lambda acc: (acc, acc), name=f"{tag}_bwd_do")
    dw_o, = _matmul(ob, dhb, ta=True, outs=(BF16,),name=f"{tag}_bwd_dwo")
    delta = _head_delta(do, o, XA_HEADS, name=f"{tag}_bwd_delta")
    dq, dk, dv = _attn_bwd((q3, 0), (kv3, 0), (kv3, 1), (_seq3(dob, nb), 0), lse, _seq3(delta, nb), heads=XA_HEADS,
                           dh=XA_HEAD_DIM, mode="full", dq_dtype=BF16, name=f"{tag}_bwd_attn")
    dqb = _flat2(dq)
    dkvb, = _rowwise(lambda a, b: jnp.concatenate([a, b], axis=1), [_flat2(dk), _flat2(dv)], [], [(2 * D_MODEL, BF16)],
                     name=f"{tag}_bwd_dkvcast")
    dw_q, = _matmul(hn, dqb, ta=True, outs=(BF16,),name=f"{tag}_bwd_dwq")
    dw_kv_t, = _matmul(dkvb, mem_n, ta=True, outs=(BF16,),name=f"{tag}_bwd_dwkv")
    dmem_n, = _matmul(dkvb, p["w_kv_t"], name=f"{tag}_bwd_dmemn")
    dg_mem, = _rowwise(lambda x, d, g: _rms_bwd_vals(x, d, g)[1], [mem, dmem_n], [p["mem_norm"]], [], [(1, D_MODEL)],
                       name=f"{tag}_bwd_memnorm")
    dh2, dh2b, dg = _norm_input_grad(dqb, p["w_q"], h, dh, p["norm"], tb=True, name=f"{tag}_bwd_dh")
    return dh2, dh2b, {"norm": dg, "mem_norm": dg_mem, "w_q": dw_q, "w_kv_t": dw_kv_t, "w_o": dw_o}


AB_MAIN = 2 * LRU_WIDTH + 3 * ATT_W


def _ab_fwd(h, p, nb):
    hn = _rms_fwd(h, p["norm"], name="ab_norm")
    w_in_t = p["w_in_t"]
    zug, = _matmul(hn, w_in_t[:2 * LRU_WIDTH], tb=True, name="ab_in_ug")
    qkv, = _matmul(hn, w_in_t[2 * LRU_WIDTH:AB_MAIN], tb=True, outs=(BF16,), tn=768, name="ab_in_qkv")
    zf, = _matmul(hn, w_in_t[AB_MAIN:], tb=True, name="ab_in_f")
    fl = zf[:, :8].reshape(nb, SEQ, 8).transpose(0, 2, 1)
    cbias = _fox_bias(fl, p["b_f"], name="ab_fox_bias")
    kb = cbias.reshape(nb, 8, SEQ // ATT_CHUNK, ATT_CHUNK)
    y_a, hl = _lru_fwd(zug, p["conv_w"], p["conv_b"], p["w_a"], p["b_a"], p["w_i"], p["b_i"], p["lam"], name="ab_lru")
    qkv3 = _seq3(qkv, nb)
    o, lse = _attn_fwd((qkv3, 0), (qkv3, 1), (qkv3, 2), kb, heads=8, dh=HEAD_DIM, mode="causal", name="ab_fox")
    o = _flat2(o)
    y, = _rowwise(lambda a, b: jnp.concatenate([a, b.astype(BF16)], axis=1), [y_a, o], [], [(D_MODEL, BF16)], name="ab_ycat")
    out, = _matmul(y, p["w_out"], extras=(h,), epilogue=lambda acc, res: (acc + res,), name="ab_out")
    return out, (h, hn, zug, qkv3, fl, kb, hl, o, lse, y)


def _ab_bwd(dh, dhb, p, saved, nb):
    h, hn, zug, qkv3, fl, kb, hl, o, lse, y = saved
    dy, dyb = _matmul(dhb, p["w_out"], tb=True, outs=(F32, BF16), epilogue=lambda acc: (acc, acc), name="ab_bwd_dy")
    dw_out, = _matmul(y, dhb, ta=True, outs=(BF16,),name="ab_bwd_dwout")
    dzug, dcw, dcb, dwa, dba, dwi, dbi, dlam = _lru_bwd(zug, hl, dy, p["conv_w"], p["conv_b"], p["w_a"], p["b_a"],
                                                        p["w_i"], p["b_i"], p["lam"], name="ab_bwd_lru")
    delta = _head_delta((dy, ATT_W, 1), o, 8, name="ab_bwd_delta")
    dq, dk, dv, dkb, drow = _attn_bwd((qkv3, 0), (qkv3, 1), (qkv3, 2), (_seq3(dyb, nb), 1), lse, _seq3(delta, nb), kb,
                                      heads=8, dh=HEAD_DIM, mode="causal", name="ab_bwd_fox")
    dcum = dkb.reshape(nb, 8, SEQ) + drow[:, :, :8].transpose(0, 2, 1)
    dfl, dbf = _fox_bias_bwd(fl, p["b_f"], dcum, name="ab_bwd_fox_bias")
    dzf = jnp.pad(dfl.transpose(0, 2, 1).reshape(nb * SEQ, 8), ((0, 0), (0, LANES - 8)))
    dz, = _rowwise(lambda a, q, k, v, f: jnp.concatenate([a, q.astype(BF16), k.astype(BF16), v.astype(BF16), f.astype(BF16)], axis=1),
                   [dzug, _flat2(dq), _flat2(dk), _flat2(dv), dzf], [], [(AB_MAIN + LANES, BF16)], name="ab_bwd_dzcat")
    dw_in_t, = _matmul(dz, hn, ta=True, outs=(BF16,),tm=384, name="ab_bwd_dwin")
    dh2, dh2b, dg = _norm_input_grad(dz, p["w_in_t"], h, dh, p["norm"], name="ab_bwd_dh")
    grads = {"norm": dg, "w_in_t": dw_in_t[:AB_MAIN + 8], "conv_w": dcw, "conv_b": dcb, "w_a": _diag_blocks(dwa), "b_a": dba,
             "w_i": _diag_blocks(dwi), "b_i": dbi, "lam": dlam, "b_f": dbf.reshape(1, 8), "w_out": dw_out}
    return dh2, dh2b, grads


CD_IN = 3 * ATT_W + ATT_W + 2 * SWA_KW


def _class_view(a, dil):
    return a.reshape(a.shape[0], a.shape[1] // dil, dil * a.shape[2])


def _gqa_share():
    src = jnp.arange(SWA_KW)[:, None]
    dst = jnp.arange(ATT_W)[None, :]
    per_kv = ATT_W // (SWA_KW // HEAD_DIM)
    return ((dst // per_kv == src // HEAD_DIM) & (dst % HEAD_DIM == src % HEAD_DIM)).astype(BF16)


def _cd_fwd(h, p, nb):
    hn = _rms_fwd(h, p["norm"], name="cd_norm")
    z, = _matmul(hn, p["w_in_t"], tb=True, tn=384, name="cd_in")
    cos, sin = _rope_tables()
    per_seq = SEQ // ROW_TILE
    table_map = lambda i: (i % per_seq, 0)

    def rope_fn(z, cos, sin, share):
        qc, kc, vc = z[:, 0:ATT_W], z[:, ATT_W:2 * ATT_W], z[:, 2 * ATT_W:3 * ATT_W]
        qd, kd, vd = z[:, 3 * ATT_W:4 * ATT_W], z[:, 4 * ATT_W:4 * ATT_W + SWA_KW], z[:, 4 * ATT_W + SWA_KW:]
        kd8 = jnp.dot(_rotate(kd, cos, sin).astype(BF16), share, preferred_element_type=F32)
        vd8 = jnp.dot(vd.astype(BF16), share, preferred_element_type=F32)
        return (jnp.concatenate([_rotate(qc, cos, sin), _rotate(kc, cos, sin)], axis=1), vc, _rotate(qd, cos, sin), kd8, vd8)
    qk, vc, qd, kd, vd = _rowwise(rope_fn, [z, cos, sin], [_gqa_share()], [(2 * ATT_W, BF16)] + [(ATT_W, BF16)] * 4,
                                  name="cd_rope", row_maps=[None, table_map, table_map])
    qk3, vc3 = _seq3(qk, nb), _seq3(vc, nb)
    branch = []
    for window, dil in DIL_PATTERN:
        o_i, lse_i = _attn_fwd((_class_view(qk3, dil), 0), (_class_view(qk3, dil), 1), (_class_view(vc3, dil), 0), classes=dil,
                               heads=8, dh=HEAD_DIM, mode="band", max_dist=window // dil, name=f"cd_dil{dil}")
        branch.append((o_i.reshape(nb * SEQ, ATT_W), lse_i.reshape(nb * SEQ, LANES)))
    expand = _head_expand(8, ATT_W)

    def combine(o1, o2, o3, l1, l2, l3, e):
        m = jnp.maximum(jnp.maximum(l1, l2), l3)
        lt = m + jnp.log(jnp.exp(l1 - m) + jnp.exp(l2 - m) + jnp.exp(l3 - m))
        o = sum(_dot_exact(jnp.exp(l - lt), e) * o_ for o_, l in ((o1, l1), (o2, l2), (o3, l3)))
        return o, lt
    y_c, lse_c = _rowwise(combine, [b[0] for b in branch] + [b[1] for b in branch], [expand], [(ATT_W, F32), (LANES, F32)],
                          name="cd_dil_combine")
    qd3, kd3, vd3 = _seq3(qd, nb), _seq3(kd, nb), _seq3(vd, nb)
    o_d, lse_band = _attn_fwd((qd3, 0), (kd3, 0), (vd3, 0), heads=8, dh=HEAD_DIM, mode="band", max_dist=SWA_WINDOW - 1,
                              name="cd_swa")

    def sink_combine(o, l, e, sink):
        lt = jnp.maximum(l, sink) + jnp.log(1.0 + jnp.exp(-jnp.abs(l - sink)))
        return o * _dot_exact(jnp.exp(l - lt), e), lt
    y_d, lse_d = _rowwise(sink_combine, [_flat2(o_d), _flat2(lse_band)], [expand, p["sink"]], [(ATT_W, F32), (LANES, F32)],
                          name="cd_swa_sink")
    y, = _rowwise(lambda a, b: jnp.concatenate([a, b], axis=1), [y_c, y_d], [], [(D_MODEL, BF16)], name="cd_ycat")
    out, = _matmul(y, p["w_out"], extras=(h,), epilogue=lambda acc, res: (acc + res,), name="cd_out")
    return out, (h, hn, qk3, vc3, qd3, kd3, vd3, y_c, lse_c, y_d, lse_d, y)


def _cd_bwd(dh, dhb, p, saved, nb):
    h, hn, qk3, vc3, qd3, kd3, vd3, y_c, lse_c, y_d, lse_d, y = saved
    dy, dyb = _matmul(dhb, p["w_out"], tb=True, outs=(F32, BF16), epilogue=lambda acc: (acc, acc), name="cd_bwd_dy")
    dw_out, = _matmul(y, dhb, ta=True, outs=(BF16,),name="cd_bwd_dwout")
    dyb3 = _seq3(dyb, nb)
    delta_c = _head_delta((dy, ATT_W, 0), y_c, 8, name="cd_bwd_delta_dil")
    lse_c3, delta_c3 = _seq3(lse_c, nb), _seq3(delta_c, nb)
    parts = []
    for window, dil in DIL_PATTERN:
        cv = functools.partial(_class_view, dil=dil)
        dq_i, dk_i, dv_i = _attn_bwd((cv(qk3), 0), (cv(qk3), 1), (cv(vc3), 0), (cv(dyb3), 0), cv(lse_c3), cv(delta_c3),
                                     classes=dil, heads=8, dh=HEAD_DIM, mode="band", max_dist=window // dil,
                                     name=f"cd_bwd_dil{dil}")
        parts.append([a.reshape(nb * SEQ, ATT_W) for a in (dq_i, dk_i, dv_i)])
    delta_d, dsink = _head_delta((dy, ATT_W, 1), y_d, 8, lse=lse_d, sink=p["sink"], name="cd_bwd_delta_swa")
    dqd, dkd, dvd = _attn_bwd((qd3, 0), (kd3, 0), (vd3, 0), (dyb3, 1), _seq3(lse_d, nb), _seq3(delta_d, nb), heads=8,
                              dh=HEAD_DIM, mode="band", max_dist=SWA_WINDOW - 1, name="cd_bwd_swa")
    cos, sin = _rope_tables()
    per_seq = SEQ // ROW_TILE
    table_map = lambda i: (i % per_seq, 0)

    def unrope(q1, q2, q3, k1, k2, k3, v1, v2, v3, qd, kd8, vd8, cos, sin, gather):
        back = lambda x: _rotate(x, cos, -sin)
        kd, vd = _dot_exact(kd8, gather), _dot_exact(vd8, gather)
        return jnp.concatenate([back(q1 + q2 + q3), back(k1 + k2 + k3), v1 + v2 + v3, back(qd), back(kd), vd], axis=1)
    ins = [parts[b][t] for t in range(3) for b in range(3)] + [_flat2(dqd), _flat2(dkd), _flat2(dvd), cos, sin]
    dz, = _rowwise(unrope, ins, [_gqa_share().T], [(CD_IN, BF16)], name="cd_bwd_unrope",
                   row_maps=[None] * 12 + [table_map, table_map])
    dw_in_t, = _matmul(dz, hn, ta=True, outs=(BF16,),tm=384, name="cd_bwd_dwin")
    dh2, dh2b, dg = _norm_input_grad(dz, p["w_in_t"], h, dh, p["norm"], name="cd_bwd_dh")
    return dh2, dh2b, {"norm": dg, "w_in_t": dw_in_t, "sink": dsink[:, :8], "w_out": dw_out}


def _local_step(x, mem, target, w, complete=None, grads_ready=None):
    nb = x.shape[0]
    h = x.reshape(nb * SEQ, D_MODEL)
    mem2 = mem.reshape(nb * MEM_LEN, D_MODEL)
    tgt = target.reshape(nb * SEQ, D_MODEL)

    h, s_ab = _ab_fwd(h, w["ab"], nb)
    if complete is not None:
        complete(h)
    h, s_xa0 = _xa_fwd(h, mem2, w["xa0"], "xa0", nb)
    h, s_mlp0 = _mlp_fwd(h, w["mlp0"], "mlp0")
    h, s_cd = _cd_fwd(h, w["cd"], nb)
    h, s_xa1 = _xa_fwd(h, mem2, w["xa1"], "xa1", nb)
    h, s_mlp1 = _mlp_fwd(h, w["mlp1"], "mlp1")

    dh, dhb, loss, dg_final = _loss_and_grad(h, tgt, w["final_norm"], name="loss")
    grads = {"final_norm": dg_final}
    dh, dhb, grads["mlp1"] = _mlp_bwd(dh, dhb, w["mlp1"], s_mlp1, "mlp1")
    dh, dhb, grads["xa1"] = _xa_bwd(dh, dhb, mem2, w["xa1"], s_xa1, "xa1", nb)
    dh, dhb, grads["cd"] = _cd_bwd(dh, dhb, w["cd"], s_cd, nb)
    if grads_ready is not None:
        grads_ready(0, grads)
    dh, dhb, grads["mlp0"] = _mlp_bwd(dh, dhb, w["mlp0"], s_mlp0, "mlp0")
    dh, dhb, grads["xa0"] = _xa_bwd(dh, dhb, mem2, w["xa0"], s_xa0, "xa0", nb)
    if grads_ready is not None:
        grads_ready(1, grads)
    dh, dhb, grads["ab"] = _ab_bwd(dh, dhb, w["ab"], s_ab, nb)
    return loss, dh.reshape(nb, SEQ, D_MODEL), grads


ANY = pl.BlockSpec(memory_space=pl.ANY)


def _place():
    x, y, c = lax.axis_index("x"), lax.axis_index("y"), lax.axis_index("c")
    return x, y, c, [(1 - x, y), (x, 1 - y), (1 - x, 1 - y)]


def _gather_chips(shard):
    half = shard.shape[0] // 2

    def body(src, out, send_sems, recv_sems):
        x, y, c, chips = _place()
        sibling = (x, y, 1 - c)

        def rows(slot, core):
            return out.at[slot, pl.ds(core * half, half)]

        def copy(k, src_ref, dst_ref, to):
            return pltpu.make_async_remote_copy(src_ref=src_ref, dst_ref=dst_ref, send_sem=send_sems.at[k],
                                                recv_sem=recv_sems.at[k], device_id=to, device_id_type=MESH)

        first = [copy(k, src.at[pl.ds(c * half, half)], rows(2 * x + y, c), (px, py, c)) for k, (px, py) in enumerate(chips)]
        for cp in first:
            cp.start()
        passed = [copy(3 + k, rows(2 * px + py, c), rows(2 * px + py, c), sibling) for k, (px, py) in enumerate(chips)]
        for k, (px, py) in enumerate(chips):
            copy(k, src.at[pl.ds(c * half, half)], rows(2 * px + py, c), (px, py, c)).wait_recv()
            passed[k].start()
        for k, (px, py) in enumerate(chips):
            copy(3 + k, rows(2 * px + py, 1 - c), rows(2 * px + py, 1 - c), sibling).wait_recv()
        for cp in first + passed:
            cp.wait_send()

    return pl.pallas_call(
        body, out_shape=jax.ShapeDtypeStruct((N_CHIPS,) + shard.shape, shard.dtype), in_specs=[ANY], out_specs=ANY,
        scratch_shapes=[pltpu.SemaphoreType.DMA((6,)), pltpu.SemaphoreType.DMA((6,))], name="gather_chips",
    )(shard)


HBM = pl.BlockSpec(memory_space=pltpu.HBM)
SEM = pl.BlockSpec(memory_space=pltpu.SEMAPHORE)
SIDE_EFFECT = pltpu.SideEffectType.DATAFLOW_SIDE_EFFECTING


def _chip_copies(src, land, send_sems, recv_sems):
    half = src.shape[0] // 2
    x, y, c, chips = _place()

    def copy(k, slot, to):
        return pltpu.make_async_remote_copy(src_ref=src.at[pl.ds(c * half, half)], dst_ref=land.at[slot, pl.ds(c * half, half)],
                                            send_sem=send_sems[k], recv_sem=recv_sems[k], device_id=to, device_id_type=MESH)
    out = [copy(k, 2 * x + y, (px, py, c)) for k, (px, py) in enumerate(chips)]
    back = [copy(k, 2 * px + py, (px, py, c)) for k, (px, py) in enumerate(chips)]
    return out, back


def _gather_start(shard):
    def body(src, land, *rest):
        sems, token = rest[:6], rest[8]
        out, _ = _chip_copies(src, land, sems[:3], sems[3:])
        for cp in out:
            cp.start()
        token[...] = jnp.zeros(token.shape, F32)

    sem = pltpu.SemaphoreType.DMA(())
    land_shape = (N_CHIPS,) + shard.shape
    return pl.pallas_call(
        body, name="gather_start",
        out_shape=(sem,) * 6 + (pltpu.HBM(shard.shape, shard.dtype), pltpu.HBM(land_shape, shard.dtype),
                                jax.ShapeDtypeStruct((8, LANES), F32)),
        in_specs=(HBM, HBM), out_specs=(SEM,) * 6 + (HBM, HBM, pl.BlockSpec(memory_space=pltpu.VMEM)),
        input_output_aliases={0: 6, 1: 7}, compiler_params=pltpu.CompilerParams(has_side_effects=SIDE_EFFECT),
    )(pltpu.with_memory_space_constraint(shard, pltpu.HBM),
      pltpu.with_memory_space_constraint(lax.empty(land_shape, shard.dtype), pltpu.HBM))


def _gather_wait(started, after):
    sems, src_thru, land_thru = started[:6], started[6], started[7]

    def body(src, land, *rest):
        out, back = _chip_copies(src, land, rest[:3], rest[3:6])
        for cp in out:
            cp.wait_send()
        for cp in back:
            cp.wait_recv()

    return pl.pallas_call(
        body, name="gather_wait",
        out_shape=(pltpu.HBM(src_thru.shape, src_thru.dtype), pltpu.HBM(land_thru.shape, land_thru.dtype)),
        in_specs=(HBM, HBM) + (SEM,) * 6 + (pl.BlockSpec(memory_space=pl.ANY),), out_specs=(HBM, HBM),
        input_output_aliases={0: 0, 1: 1}, compiler_params=pltpu.CompilerParams(has_side_effects=SIDE_EFFECT),
    )(src_thru, land_thru, *sems, after)[1]


def _gather_pass(land):
    half = land.shape[1] // 2

    def body(land_in, land_out, send_sems, recv_sems):
        x, y, c, chips = _place()

        def copy(k, slot, core):
            rows = land_out.at[slot, pl.ds(core * half, half)]
            return pltpu.make_async_remote_copy(src_ref=rows, dst_ref=rows, send_sem=send_sems.at[k], recv_sem=recv_sems.at[k],
                                                device_id=(x, y, 1 - c), device_id_type=MESH)
        sends = [copy(k, 2 * px + py, c) for k, (px, py) in enumerate(chips)]
        for cp in sends:
            cp.start()
        for k, (px, py) in enumerate(chips):
            copy(k, 2 * px + py, 1 - c).wait_recv()
        for cp in sends:
            cp.wait_send()

    return pl.pallas_call(
        body, out_shape=jax.ShapeDtypeStruct(land.shape, land.dtype), in_specs=[ANY], out_specs=ANY,
        scratch_shapes=[pltpu.SemaphoreType.DMA((3,)), pltpu.SemaphoreType.DMA((3,))], input_output_aliases={0: 0},
        name="gather_pass",
    )(land)


def _swap_cores(block, *, name, half_rows=None):
    shape = block.shape if half_rows is None else (block.shape[0], half_rows, block.shape[2])

    def body(src, out, send_sem, recv_sem):
        x, y, c, _ = _place()
        part = src if half_rows is None else src.at[:, pl.ds((1 - c) * half_rows, half_rows)]
        cp = pltpu.make_async_remote_copy(src_ref=part, dst_ref=out, send_sem=send_sem, recv_sem=recv_sem,
                                          device_id=(x, y, 1 - c), device_id_type=MESH)
        cp.start()
        cp.wait()

    return pl.pallas_call(
        body, out_shape=jax.ShapeDtypeStruct(shape, block.dtype), in_specs=[ANY], out_specs=ANY,
        scratch_shapes=[pltpu.SemaphoreType.DMA(()), pltpu.SemaphoreType.DMA(())], name=name,
    )(block)


def _scatter_copies(parts, land, send_sems, recv_sems):
    x, y, c, chips = _place()
    return [pltpu.make_async_remote_copy(src_ref=parts.at[2 * px + py], dst_ref=land.at[k], send_sem=send_sems[k],
                                         recv_sem=recv_sems[k], device_id=(px, py, c), device_id_type=MESH)
            for k, (px, py) in enumerate(chips)]


def _scatter_start(parts, tag):
    def body(src, land, *rest):
        for cp in _scatter_copies(src, land, rest[:3], rest[3:6]):
            cp.start()
        rest[8][...] = jnp.zeros(rest[8].shape, F32)

    sem = pltpu.SemaphoreType.DMA(())
    land_shape = (3,) + parts.shape[1:]
    return pl.pallas_call(
        body, name=f"scatter_start_{tag}",
        out_shape=(sem,) * 6 + (pltpu.HBM(parts.shape, parts.dtype), pltpu.HBM(land_shape, parts.dtype),
                                jax.ShapeDtypeStruct((8, LANES), F32)),
        in_specs=(HBM, HBM), out_specs=(SEM,) * 6 + (HBM, HBM, pl.BlockSpec(memory_space=pltpu.VMEM)),
        input_output_aliases={0: 6, 1: 7}, compiler_params=pltpu.CompilerParams(has_side_effects=SIDE_EFFECT),
    )(pltpu.with_memory_space_constraint(parts, pltpu.HBM),
      pltpu.with_memory_space_constraint(lax.empty(land_shape, parts.dtype), pltpu.HBM))


def _scatter_wait(started, after, tag):
    def body(src, land, *rest):
        for cp in _scatter_copies(src, land, rest[:3], rest[3:6]):
            cp.wait_send()
            cp.wait_recv()

    src_thru, land_thru = started[6], started[7]
    return pl.pallas_call(
        body, name=f"scatter_wait_{tag}",
        out_shape=(pltpu.HBM(src_thru.shape, src_thru.dtype), pltpu.HBM(land_thru.shape, land_thru.dtype)),
        in_specs=(HBM, HBM) + (SEM,) * 6 + (pl.BlockSpec(memory_space=pl.ANY),), out_specs=(HBM, HBM),
        input_output_aliases={0: 0, 1: 1}, compiler_params=pltpu.CompilerParams(has_side_effects=SIDE_EFFECT),
    )(src_thru, land_thru, *started[:6], after)[1]


def _sum_all_devices(vec):
    rows = vec.shape[0]

    def body(x_ref, total_ref, all_ref, send_sems, recv_sems, local_sem):
        x, y, c, chips = _place()
        me, sibling = (x, y, c), (x, y, 1 - c)

        def slot(px, py, pc):
            return all_ref.at[4 * px + 2 * py + pc]

        def copy(k, block, to, src=None):
            return pltpu.make_async_remote_copy(src_ref=slot(*block) if src is None else src, dst_ref=slot(*block),
                                                send_sem=send_sems.at[k], recv_sem=recv_sems.at[k], device_id=to,
                                                device_id_type=MESH)

        mine = pltpu.make_async_copy(x_ref, slot(*me), local_sem)
        mine.start()
        first = [copy(0, me, sibling, src=x_ref)] + [copy(1 + j, me, (*chip, c), src=x_ref) for j, chip in enumerate(chips)]
        for cp in first:
            cp.start()
        passed = [copy(4 + j, (*chip, c), sibling) for j, chip in enumerate(chips)]
        for j, chip in enumerate(chips):
            copy(1 + j, (*chip, c), me).wait_recv()
            passed[j].start()
        copy(0, sibling, me).wait_recv()
        for j, chip in enumerate(chips):
            copy(4 + j, (*chip, 1 - c), me).wait_recv()
        for cp in first + passed:
            cp.wait_send()
        mine.wait()
        acc = all_ref[0]
        for d in range(1, 8):
            acc = acc + all_ref[d]
        total_ref[...] = acc

    vmem = pl.BlockSpec(memory_space=pltpu.VMEM)
    return pl.pallas_call(
        body, out_shape=[jax.ShapeDtypeStruct((rows, LANES), F32), jax.ShapeDtypeStruct((8, rows, LANES), F32)],
        in_specs=[vmem], out_specs=[vmem, vmem],
        scratch_shapes=[pltpu.SemaphoreType.DMA((7,)), pltpu.SemaphoreType.DMA((7,)), pltpu.SemaphoreType.DMA(())],
        name="sum_all_devices", compiler_params=pltpu.CompilerParams(vmem_limit_bytes=VMEM_LIMIT_BYTES),
    )(vec)[0]


PACK_COLS = 1024
BIG = (("mlp_w_up", 2, 1024, True), ("mlp_w_down", 2, 1024, False), ("xa_w_kv", 2, 512, True), ("xa_w_q", 2, 256, False),
       ("xa_w_o", 2, 256, False), ("ab_w_out", 1, 256, False), ("cd_w_out", 1, 256, False), ("cd_w_in", 1, 576, True),
       ("ab_w_in", 1, 642, True))
ENTRIES = tuple((name, layer, rows, transposed) for name, layers, rows, transposed in BIG for layer in range(layers))
FIRST_ENTRIES = tuple(e for e in ENTRIES if e[0].startswith("ab_"))
LATER_ENTRIES = tuple(e for e in ENTRIES if not e[0].startswith("ab_"))


def _tile_rows(entry):
    return -(-entry[2] // 16) * 16


def _padded_rows(entries):
    return -(-sum(_tile_rows(e) for e in entries) // 32) * 32


SMALL = (("ab_norm", (1, 1024)), ("ab_conv_w", (1, 4, 512)), ("ab_conv_b", (1, 512)), ("lru_w_a", (1, 8, 64, 64)),
         ("lru_b_a", (1, 512)), ("lru_w_i", (1, 8, 64, 64)), ("lru_b_i", (1, 512)), ("lru_lambda", (1, 512)),
         ("fox_b_f", (1, 8)), ("cd_norm", (1, 1024)), ("cd_sink", (1, 8)), ("xa_norm", (2, 1024)),
         ("xa_mem_norm", (2, 1024)), ("mlp_norm", (2, 1024)), ("final_norm", (1024,)), ("loss", (1,)))
SPLIT_SMALL = ("ab_conv_w", "cd_norm")


def _pack_rows(parts, entries, dtype):
    lead = parts[0].shape[:-2]
    zeros = lambda rows: jnp.zeros(lead + (rows, PACK_COLS), dtype)
    pieces = [jnp.pad(p.astype(dtype), ((0, 0),) * len(lead) + ((0, _tile_rows(e) - e[2]), (0, 0))) for p, e in zip(parts, entries)]
    tail = _padded_rows(entries) - sum(_tile_rows(e) for e in entries)
    return jnp.concatenate(pieces + ([zeros(tail)] if tail else []), axis=len(lead))


def _unpack_rows(packed, entries):
    out, r0 = [], 0
    for e in entries:
        out.append(packed[..., r0:r0 + e[2], :])
        r0 += _tile_rows(e)
    return out


def _shard_rows(w, entries):
    return [(w[name][layer].T if transposed else w[name][layer]) for name, layer, _, transposed in entries]


def _shard_blocks(rows):
    out = {}
    for (name, layer, _, transposed), r in zip(ENTRIES, rows):
        out.setdefault(name, []).append(r.T if transposed else r)
    return {name: jnp.stack(layers) for name, layers in out.items()}


def _pack_small(vals):
    flat = jnp.concatenate([vals[name].astype(F32).reshape(-1) for name, _ in SMALL])
    size = -(-flat.shape[0] // (8 * LANES)) * (8 * LANES)
    return jnp.pad(flat, (0, size - flat.shape[0])).reshape(-1, LANES)


def _unpack_small(packed):
    flat, out, at = packed.reshape(-1), {}, 0
    for name, shape in SMALL:
        out[name] = flat[at:at + math.prod(shape)].reshape(shape)
        at += math.prod(shape)
    return out


def _chip_part(full, chip):
    width = full.shape[-1] // N_CHIPS
    return lax.dynamic_slice_in_dim(full, chip * width, width, axis=full.ndim - 1)


def _chip_embed(part, chip):
    full = jnp.zeros(part.shape[:-1] + (part.shape[-1] * N_CHIPS,), part.dtype)
    return lax.dynamic_update_slice_in_dim(full, part, chip * part.shape[-1], axis=part.ndim - 1)


SUBLAYER_KEYS = {"ab_w_in": ("ab", "w_in_t"), "ab_w_out": ("ab", "w_out"), "cd_w_in": ("cd", "w_in_t"), "cd_w_out": ("cd", "w_out"),
                 "xa_w_q": ("xa", "w_q"), "xa_w_kv": ("xa", "w_kv_t"), "xa_w_o": ("xa", "w_o"), "mlp_w_up": ("mlp", "w_up_t"),
                 "mlp_w_down": ("mlp", "w_down")}


def _sublayer(name, layer):
    block, leaf = SUBLAYER_KEYS[name]
    return (block if block in ("ab", "cd") else f"{block}{layer}"), leaf


def _set_big(params, full_rows):
    for (name, layer), rows in full_rows.items():
        block, leaf = _sublayer(name, layer)
        if name == "ab_w_in":
            rows = jnp.concatenate([rows, jnp.zeros((LANES - 8, PACK_COLS), rows.dtype)])
        params[block][leaf] = rows


def _build_params(full_rows, w):
    pad = lambda a: jnp.pad(a, ((0, 0), (0, LANES - a.shape[1])))
    params = {
        "ab": dict(norm=w["ab_norm"], conv_w=w["ab_conv_w"][0], conv_b=w["ab_conv_b"], w_a=_block_diag(w["lru_w_a"][0]),
                   b_a=w["lru_b_a"], w_i=_block_diag(w["lru_w_i"][0]), b_i=w["lru_b_i"], lam=w["lru_lambda"],
                   b_f=w["fox_b_f"].reshape(8, 1)),
        "cd": dict(norm=w["cd_norm"], sink=pad(w["cd_sink"])),
        "final_norm": w["final_norm"].reshape(1, D_MODEL),
    }
    for layer in range(2):
        params[f"xa{layer}"] = dict(norm=w["xa_norm"][layer:layer + 1], mem_norm=w["xa_mem_norm"][layer:layer + 1])
        params[f"mlp{layer}"] = dict(norm=w["mlp_norm"][layer:layer + 1])
    _set_big(params, full_rows)
    return params


def _grad_rows(g, entries=ENTRIES):
    out = []
    for name, layer, _, _ in entries:
        block, leaf = _sublayer(name, layer)
        out.append(g[block][leaf])
    return out


def _reduce_group(entry):
    name, layer = entry[0], entry[1]
    if name.startswith("ab_"):
        return 2
    return 0 if layer == 1 or name.startswith("cd_") else 1


REDUCE_GROUPS = tuple(tuple(e for e in ENTRIES if _reduce_group(e) == group) for group in range(3))


def _reduce_tile(half):
    return max(t for t in range(16, 1025, 16) if half % t == 0)


def _reduce_start(grads_by_chip, core, chip, tag):
    half = grads_by_chip.shape[1] // 2
    tile = _reduce_tile(half)
    steps = half // tile
    place = jnp.stack([core, chip]).astype(jnp.int32)
    got = _swap_cores(grads_by_chip, name=f"swap_cores_{tag}", half_rows=half)
    block = (1, tile, PACK_COLS)

    def sum_cores(place_ref, mine_ref, got_ref, f32_ref, bf16_ref):
        total = mine_ref[...].astype(F32) + got_ref[...].astype(F32)
        f32_ref[...] = total
        bf16_ref[...] = total.astype(BF16)

    pair32, pair16 = pl.pallas_call(
        sum_cores, name=f"sum_cores_{tag}",
        grid_spec=pltpu.PrefetchScalarGridSpec(
            num_scalar_prefetch=1, grid=(N_CHIPS, steps),
            in_specs=[pl.BlockSpec(block, lambda s, i, place_ref: (s, place_ref[0] * steps + i, 0)),
                      pl.BlockSpec(block, lambda s, i, place_ref: (s, i, 0))],
            out_specs=[pl.BlockSpec(block, lambda s, i, place_ref: (s, i, 0))] * 2),
        out_shape=[jax.ShapeDtypeStruct((N_CHIPS, half, PACK_COLS), F32), jax.ShapeDtypeStruct((N_CHIPS, half, PACK_COLS), BF16)],
        compiler_params=_params("arbitrary", "arbitrary"),
    )(place, grads_by_chip, got)
    return pair32, _scatter_start(pair16, tag), place


def _reduce_finish(state, core, tag, after):
    pair32, started, place = state
    half = pair32.shape[1]
    tile = _reduce_tile(half)
    landed = _scatter_wait(started, after, tag)

    def sum_chips(place_ref, own_ref, landed_ref, out_ref):
        out_ref[...] = own_ref[0] + landed_ref[0].astype(F32) + landed_ref[1].astype(F32) + landed_ref[2].astype(F32)

    done = pl.pallas_call(
        sum_chips, name=f"sum_chips_{tag}",
        grid_spec=pltpu.PrefetchScalarGridSpec(
            num_scalar_prefetch=1, grid=(half // tile,),
            in_specs=[pl.BlockSpec((1, tile, PACK_COLS), lambda i, place_ref: (place_ref[1], i, 0)),
                      pl.BlockSpec((3, tile, PACK_COLS), lambda i, place_ref: (0, i, 0))],
            out_specs=pl.BlockSpec((tile, PACK_COLS), lambda i, place_ref: (i, 0))),
        out_shape=jax.ShapeDtypeStruct((half, PACK_COLS), F32), compiler_params=_params("arbitrary"),
    )(place, pair32, landed)
    theirs = _swap_cores(done, name=f"share_halves_{tag}")
    return jnp.where(core == 0, jnp.concatenate([done, theirs]), jnp.concatenate([theirs, done]))


def kernel(x, mem, ab_norm, ab_w_in, ab_conv_w, ab_conv_b, lru_w_a, lru_b_a, lru_w_i, lru_b_i, lru_lambda, fox_b_f, ab_w_out, cd_norm, cd_w_in, cd_sink, cd_w_out, xa_norm, xa_mem_norm, xa_w_q, xa_w_kv, xa_w_o, mlp_norm, mlp_w_up, mlp_w_down, final_norm, loss_target, m_ab_norm, m_ab_w_in, m_ab_conv_w, m_ab_conv_b, m_lru_w_a, m_lru_b_a, m_lru_w_i, m_lru_b_i, m_lru_lambda, m_fox_b_f, m_ab_w_out, m_cd_norm, m_cd_w_in, m_cd_sink, m_cd_w_out, m_xa_norm, m_xa_mem_norm, m_xa_w_q, m_xa_w_kv, m_xa_w_o, m_mlp_norm, m_mlp_w_up, m_mlp_w_down, m_final_norm, v_ab_norm, v_ab_w_in, v_ab_conv_w, v_ab_conv_b, v_lru_w_a, v_lru_b_a, v_lru_w_i, v_lru_b_i, v_lru_lambda, v_fox_b_f, v_ab_w_out, v_cd_norm, v_cd_w_in, v_cd_sink, v_cd_w_out, v_xa_norm, v_xa_mem_norm, v_xa_w_q, v_xa_w_kv, v_xa_w_o, v_mlp_norm, v_mlp_w_up, v_mlp_w_down, v_final_norm):
    given = dict(locals())
    weight_names = [name for name, _ in SMALL if name != "loss"] + [name for name, _, _, _ in BIG]
    w = {name: given[name] for name in weight_names}
    chip = 2 * lax.axis_index("x") + lax.axis_index("y")
    core = lax.axis_index("c")

    slot = lax.broadcasted_iota(jnp.int32, (N_CHIPS, 1, 1), 0)

    def whole(entries, mine, gathered):
        return {(name, layer): jnp.where(slot == chip, own[None], got).reshape(N_CHIPS * rows, PACK_COLS)
                for (name, layer, rows, _), own, got in zip(entries, _unpack_rows(mine, entries), _unpack_rows(gathered, entries))}

    mine_first = _pack_rows(_shard_rows(w, FIRST_ENTRIES), FIRST_ENTRIES, BF16)
    mine_later = _pack_rows(_shard_rows(w, LATER_ENTRIES), LATER_ENTRIES, BF16)
    first = _gather_chips(mine_first)
    started = _gather_start(mine_later)
    owner = (core == 0).astype(F32)
    quarters = {name: _chip_embed(w[name], chip) * owner for name in SPLIT_SMALL}
    zeros = {name: jnp.zeros(shape, F32) for name, shape in SMALL}
    small_full = _unpack_small(_sum_all_devices(_pack_small({**zeros, **quarters})))
    params = _build_params(whole(FIRST_ENTRIES, mine_first, first),
                           {**w, "ab_conv_w": small_full["ab_conv_w"], "cd_norm": small_full["cd_norm"]})
    params["ab"]["norm"] = params["ab"]["norm"] + started[8][0, 0]

    def complete(h):
        _set_big(params, whole(LATER_ENTRIES, mine_later, _gather_pass(_gather_wait(started, after=h))))

    reducing = {}

    def grads_ready(group, g):
        entries = REDUCE_GROUPS[group]
        by_chip = _pack_rows([r.reshape(N_CHIPS, e[2], PACK_COLS) for r, e in zip(_grad_rows(g, entries), entries)], entries, BF16)
        reducing[group] = _reduce_start(by_chip, core, chip, f"g{group}")
        if group < 2:
            block, leaf = ("mlp0", "w_down") if group == 0 else ("ab", "w_out")
            params[block][leaf] = params[block][leaf] + reducing[group][1][8][0, 0].astype(BF16)

    loss_part, grad_x, g = _local_step(x, mem, loss_target, params, complete, grads_ready)
    grads_ready(2, g)
    reduced = {}
    for group, entries in enumerate(REDUCE_GROUPS):
        rows = _unpack_rows(_reduce_finish(reducing[group], core, f"g{group}", after=grad_x), entries)
        reduced.update({(e[0], e[1]): r for e, r in zip(entries, rows)})
    grads = _shard_blocks([reduced[e[0], e[1]] for e in ENTRIES])
    pair = lambda key, leaf: jnp.stack([g[f"{key}0"][leaf], g[f"{key}1"][leaf]])

    small_local = {"ab_norm": g["ab"]["norm"], "ab_conv_w": g["ab"]["conv_w"], "ab_conv_b": g["ab"]["conv_b"],
                   "lru_w_a": g["ab"]["w_a"], "lru_b_a": g["ab"]["b_a"], "lru_w_i": g["ab"]["w_i"], "lru_b_i": g["ab"]["b_i"],
                   "lru_lambda": g["ab"]["lam"], "fox_b_f": g["ab"]["b_f"], "cd_norm": g["cd"]["norm"], "cd_sink": g["cd"]["sink"],
                   "xa_norm": pair("xa", "norm"), "xa_mem_norm": pair("xa", "mem_norm"), "mlp_norm": pair("mlp", "norm"),
                   "final_norm": g["final_norm"], "loss": loss_part[0, :1]}
    small_sum_packed = _sum_all_devices(_pack_small(small_local))
    small_sum = _unpack_small(small_sum_packed)
    loss = small_sum["loss"][0]

    delta, new_m, new_v = {}, {}, {}
    for name, _, _, _ in BIG:
        shape = w[name].shape
        flat = lambda a: a.reshape(-1, shape[-1])
        d, m2, v2 = _adamw(flat(w[name]), flat(grads[name]), flat(given["m_" + name]), flat(given["v_" + name]), name=f"adamw_{name}")
        delta[name], new_m[name], new_v[name] = d.reshape(shape), m2.reshape(shape), v2.reshape(shape)

    def small_state(prefix):
        vals = {name: (given[prefix + name] if name != "loss" else jnp.zeros((1,), F32)) for name, _ in SMALL}
        for name in SPLIT_SMALL:
            vals[name] = _chip_embed(vals[name], chip)
        return _pack_small(vals)
    packed = _adamw(small_state(""), small_sum_packed, small_state("m_"), small_state("v_"), name="adamw_small")
    for store, vals in zip((delta, new_m, new_v), packed):
        for name, val in _unpack_small(vals).items():
            store[name] = _chip_part(val, chip) if name in SPLIT_SMALL else val
    for name in SPLIT_SMALL:
        small_sum[name] = _chip_part(small_sum[name], chip)
    grads.update({name: small_sum[name] for name, _ in SMALL})

    order = ["ab_norm", "ab_w_in", "ab_conv_w", "ab_conv_b", "lru_w_a", "lru_b_a", "lru_w_i", "lru_b_i", "lru_lambda", "fox_b_f",
             "ab_w_out", "cd_norm", "cd_w_in", "cd_sink", "cd_w_out", "xa_norm", "xa_mem_norm", "xa_w_q", "xa_w_kv", "xa_w_o",
             "mlp_norm", "mlp_w_up", "mlp_w_down", "final_norm"]
    return (loss, grad_x, *[grads[n] for n in order], *[delta[n] for n in order], *[new_m[n] for n in order],
            *[new_v[n] for n in order])
```

```python
import functools
import math

import jax
import jax.numpy as jnp
from jax import lax
from jax.experimental import pallas as pl
from jax.experimental.pallas import tpu as pltpu

F32 = jnp.float32
BF16 = jnp.bfloat16
MESH = pl.DeviceIdType.MESH

D_MODEL = 1024
SEQ = 2048
HEAD_DIM = 64
LRU_WIDTH = 512
LRU_BLOCKS = 8
LRU_C = 8.0
CONV_WIDTH = 4
ATT_W = 512
SWA_KW = 128
SWA_WINDOW = 128
DIL_PATTERN = ((128, 1), (512, 4), (2048, 16))
MEM_LEN = 256
XA_HEADS = 4
XA_HEAD_DIM = 256
D_FF = 4096
ROPE_THETA = 10000.0
EPS = 1e-6
N_CHIPS = 4
LANES = 128

ADAM_LR, ADAM_B1, ADAM_B2, ADAM_EPS, ADAM_WD, ADAM_STEP = 0.001, 0.9, 0.999, 1e-08, 0.01, 10

VMEM_LIMIT_BYTES = 56 * 1024 * 1024
MASKED = -1e30
ROW_TILE = 512
LRU_CHUNK = 512
ATT_BLOCK = 128
BAND_ROWS = 256
ATT_CHUNK = 512
CAUSAL_ROWS = 256


def _params(*sem):
    return pltpu.CompilerParams(dimension_semantics=sem, vmem_limit_bytes=VMEM_LIMIT_BYTES)


def _rowwise(fn, rows, consts=(), out_rows=(), out_accs=(), *, name, tile=ROW_TILE, row_maps=None):
    rows = [r if isinstance(r, tuple) else (r, r.shape[1], 0) for r in rows]
    consts = list(consts)
    nr, nc, no = len(rows), len(consts), len(out_rows)
    dealt_in = [r[3] if isinstance(r[0], str) else None for r in rows]
    dealt_out = [o[2] if len(o) == 3 else None for o in out_rows]
    total = next(r[0].shape[0] for r in rows if not isinstance(r[0], str))
    tile = min(tile, total)
    assert total % tile == 0
    n = total // tile
    n_acc = len(out_accs)

    def body(*refs):
        scratch = list(refs[nr + nc + no + n_acc:])
        vals = []
        for ref, d, row in zip(refs[:nr], dealt_in, rows):
            if d is None:
                vals.append(ref[...])
                continue
            sc, width = scratch.pop(0), row[2]
            for r in range(d):
                for c in range(width // LANES):
                    lanes = slice(r * width + c * LANES, r * width + (c + 1) * LANES)
                    sc.at[c][pl.ds(r, tile // d, stride=d), :] = ref[:, lanes].astype(F32)
            vals.append(jnp.concatenate([sc[c] for c in range(width // LANES)], axis=1))
        outs = fn(*vals, *[r[...] for r in refs[nr:nr + nc]])
        outs = tuple(outs) if isinstance(outs, (tuple, list)) else (outs,)
        for ref, val, d, spec in zip(refs[nr + nc:nr + nc + no], outs[:no], dealt_out, out_rows):
            if d is None:
                ref[...] = val.astype(ref.dtype)
                continue
            sc, width = scratch.pop(0), spec[0]
            for c in range(width // LANES):
                sc[c] = val[:, c * LANES:(c + 1) * LANES].astype(F32)
            for r in range(d):
                for c in range(width // LANES):
                    lanes = slice(r * width + c * LANES, r * width + (c + 1) * LANES)
                    ref[:, lanes] = sc.at[c][pl.ds(r, tile // d, stride=d), :].astype(ref.dtype)
        for ref, val in zip(refs[nr + nc + no:nr + nc + no + n_acc], outs[no:]):
            @pl.when(pl.program_id(0) == 0)
            def _(ref=ref):
                ref[...] = jnp.zeros(ref.shape, ref.dtype)
            ref[...] += val

    in_specs, args, scratch_shapes = [], [], []
    for idx, row in enumerate(rows):
        if isinstance(row[0], str):
            _, arr, width, d = row
            in_specs.append(pl.BlockSpec((tile // d, d * width), lambda i: (i, 0)))
            scratch_shapes.append(pltpu.VMEM((width // LANES, tile, LANES), F32))
        elif row_maps is not None and row_maps[idx] is not None:
            arr, width, _ = row
            in_specs.append(pl.BlockSpec((tile, width), row_maps[idx]))
        else:
            arr, width, cb = row
            in_specs.append(pl.BlockSpec((tile, width), functools.partial(lambda i, cb: (i, cb), cb=cb)))
        args.append(arr)
    in_specs += [pl.BlockSpec(c.shape, lambda i: (0, 0)) for c in consts]
    out_shape, out_specs = [], []
    for spec, d in zip(out_rows, dealt_out):
        w, dt = spec[0], spec[1]
        if d is None:
            out_shape.append(jax.ShapeDtypeStruct((total, w), dt))
            out_specs.append(pl.BlockSpec((tile, w), lambda i: (i, 0)))
        else:
            out_shape.append(jax.ShapeDtypeStruct((total // d, d * w), dt))
            out_specs.append(pl.BlockSpec((tile // d, d * w), lambda i: (i, 0)))
            scratch_shapes.append(pltpu.VMEM((w // LANES, tile, LANES), F32))
    out_shape += [jax.ShapeDtypeStruct(s, F32) for s in out_accs]
    out_specs += [pl.BlockSpec(s, lambda i: (0, 0)) for s in out_accs]
    return pl.pallas_call(
        body, grid=(n,), in_specs=in_specs, out_specs=out_specs, out_shape=out_shape, scratch_shapes=scratch_shapes, name=name,
        compiler_params=_params("arbitrary"),
    )(*args, *consts)


MATMUL_VMEM_BYTES = 40 * 1024 * 1024


def _matmul_tiles(m, n, k, tm, tn, out_bytes):
    tm, tn, tk = min(tm, m), min(tn, n), k

    def need():
        return 2 * (2 * tk * (tm + tn) + tm * tn * out_bytes) + (0 if tk == k else 4 * tm * tn)

    while need() > MATMUL_VMEM_BYTES:
        if tm >= tn and tm % 256 == 0:
            tm //= 2
        elif tn % 256 == 0:
            tn //= 2
        else:
            tk //= 2
    return tm, tn, tk


def _matmul(a, b, *, ta=False, tb=False, outs=(F32,), epilogue=None, extras=(), consts=(), sums=(), tm=1024, tn=1024, name):
    m, k = (a.shape[1], a.shape[0]) if ta else a.shape
    n = b.shape[0] if tb else b.shape[1]
    assert (b.shape[1] if tb else b.shape[0]) == k
    out_bytes = sum(jnp.dtype(dt).itemsize for dt in outs) + sum(e.dtype.itemsize for e in extras)
    tm, tn, tk = _matmul_tiles(m, n, k, tm, tn, out_bytes)
    assert m % tm == 0 and n % tn == 0 and k % tk == 0, (name, m, n, k)
    nk = k // tk
    ne, nc, no = len(extras), len(consts), len(outs)
    assert not sums or tn == n, name
    dims = (((0 if ta else 1,), (1 if tb else 0,)), ((), ()))

    def body(*refs):
        a_ref, b_ref = refs[:2]
        e_refs, o_refs = refs[2:2 + ne + nc], refs[2 + ne + nc:2 + ne + nc + no]
        s_refs = refs[2 + ne + nc + no:2 + ne + nc + no + len(sums)]

        def finish(acc):
            vals = (acc,) if epilogue is None else epilogue(acc, *[e[...] for e in e_refs])
            for ref, val in zip(o_refs, vals[:no]):
                ref[...] = val.astype(ref.dtype)
            for ref, val in zip(s_refs, vals[no:]):
                @pl.when(pl.program_id(0) == 0)
                def _(ref=ref, val=val):
                    ref[...] = val

                @pl.when(pl.program_id(0) > 0)
                def _(ref=ref, val=val):
                    ref[...] += val

        prod = lax.dot_general(a_ref[...], b_ref[...], dims, preferred_element_type=F32)
        if nk == 1:
            finish(prod)
        else:
            acc_ref = refs[-1]
            step = pl.program_id(2)

            @pl.when(step == 0)
            def _():
                acc_ref[...] = prod

            @pl.when(step > 0)
            def _():
                acc_ref[...] += prod

            @pl.when(step == nk - 1)
            def _():
                finish(acc_ref[...])

    a_spec = pl.BlockSpec((tk, tm), lambda i, j, s: (s, i)) if ta else pl.BlockSpec((tm, tk), lambda i, j, s: (i, s))
    b_spec = pl.BlockSpec((tn, tk), lambda i, j, s: (j, s)) if tb else pl.BlockSpec((tk, tn), lambda i, j, s: (s, j))
    tile_spec = pl.BlockSpec((tm, tn), lambda i, j, s: (i, j))
    whole = lambda shape: pl.BlockSpec(shape, lambda i, j, s: (0, 0))
    return pl.pallas_call(
        body, grid=(m // tm, n // tn, nk),
        in_specs=[a_spec, b_spec] + [tile_spec] * ne + [whole(c.shape) for c in consts],
        out_specs=[tile_spec] * no + [whole(shape) for shape in sums],
        out_shape=[jax.ShapeDtypeStruct((m, n), dt) for dt in outs] + [jax.ShapeDtypeStruct(shape, F32) for shape in sums],
        scratch_shapes=[pltpu.VMEM((tm, tn), F32)] if nk > 1 else [],
        name=name, compiler_params=_params("arbitrary" if sums else "parallel", "parallel", "arbitrary"),
    )(a, b, *extras, *consts)


def _split3(x):
    x1 = x.astype(BF16)
    r1 = x - x1.astype(F32)
    x2 = r1.astype(BF16)
    x3 = (r1 - x2.astype(F32)).astype(BF16)
    return x1, x2, x3


def _dot_exact(x, e):
    return sum(jnp.dot(p, e, preferred_element_type=F32) for p in _split3(x))


def _head_expand(heads, width):
    dh = width // heads
    rows = jnp.arange(LANES)[:, None]
    cols = jnp.arange(width)[None, :]
    return (cols // dh == rows).astype(BF16)


def _rstd(x):
    return lax.rsqrt(jnp.mean(x * x, axis=-1, keepdims=True) + EPS)


def _rms_fwd(x, gain, *, name):
    return _rowwise(lambda x, g: x * _rstd(x) * g, [x], [gain], [(x.shape[1], BF16)], name=name)[0]


def _rms_bwd_vals(x, dhn, gain):
    r = _rstd(x)
    xh = x * r
    dxh = dhn * gain
    dx = r * (dxh - xh * jnp.mean(dxh * xh, axis=-1, keepdims=True))
    return dx, jnp.sum(dhn * xh, axis=0, keepdims=True)


def _norm_input_grad(dz, w, x, dres, gain, *, tb=False, name):
    def epilogue(dhn, x, dres, g):
        dx, dg = _rms_bwd_vals(x, dhn, g)
        dh = dres + dx
        return dh, dh, dg
    return _matmul(dz, w, tb=tb, outs=(F32, BF16), extras=(x, dres), consts=(gain,), sums=((1, x.shape[1]),), epilogue=epilogue,
                   name=name)


def _loss_and_grad(h, target, gain, *, name):
    def fn(h, tgt, g):
        r = _rstd(h)
        err = h * r * g - tgt
        loss = 0.5 * jnp.sum(jnp.mean(err * err, axis=-1, keepdims=True), axis=0, keepdims=True)
        dx, dg = _rms_bwd_vals(h, err * (1.0 / D_MODEL), g)
        return dx, dx, jnp.broadcast_to(loss, (1, LANES)), dg
    d = h.shape[1]
    return _rowwise(fn, [h, target], [gain], [(d, F32), (d, BF16)], [(1, LANES), (1, d)], name=name)


def _adamw(w, g, m, v, *, name):
    rows, cols = w.shape
    tile = rows
    for cand in (256, 128, 64, 32, 16, 8):
        if rows % cand == 0 and rows > cand:
            tile = cand
            break
    bc1 = 1.0 - ADAM_B1 ** ADAM_STEP
    bc2 = 1.0 - ADAM_B2 ** ADAM_STEP

    def fn(w, g, m, v):
        m2 = ADAM_B1 * m + (1.0 - ADAM_B1) * g
        v2 = ADAM_B2 * v + (1.0 - ADAM_B2) * (g * g)
        delta = -ADAM_LR * ((m2 / bc1) / (jnp.sqrt(v2 / bc2) + ADAM_EPS) + ADAM_WD * w)
        return delta, m2, v2
    return _rowwise(fn, [w, g, m, v], [], [(cols, F32)] * 3, name=name, tile=tile)


def _attn_specs(arr, width, cb, classes, rows_block, whole):
    ncols = arr.shape[2] // (classes * width)
    if whole:
        return pl.BlockSpec((1, arr.shape[1], width), lambda n, r, i: (n, 0, r * ncols + cb))
    return pl.BlockSpec((1, rows_block, width), lambda n, r, i: (n, i, r * ncols + cb))


def _attn_tiles(mode, lq, lk):
    if mode == "full":
        return min(lq, 256), lk
    if mode == "band":
        tq = min(BAND_ROWS, lq)
        return tq, min(tq + ATT_BLOCK, lk)
    return CAUSAL_ROWS, ATT_CHUNK


def _window(mode, i, j, tq, win, lk):
    if mode == "causal":
        return pl.multiple_of(j * win, win), (i * tq) // win + 1
    if mode == "band":
        return pl.multiple_of(jnp.clip(i * tq + tq - win, 0, lk - win), ATT_BLOCK), 1
    return 0, 1


def _allowed(mode, i, start, tq, win, max_dist):
    if mode == "full":
        return None
    dist = (i * tq + lax.broadcasted_iota(jnp.int32, (tq, win), 0)) - (start + lax.broadcasted_iota(jnp.int32, (tq, win), 1))
    ok = dist >= 0
    if max_dist is not None:
        ok = ok & (dist <= max_dist)
    return ok


def _head_groups(heads, dh):
    gw = max(LANES, dh)
    return gw, gw // dh, heads // (gw // dh)


def _own_lanes(rows, gw, dh, u):
    return lax.broadcasted_iota(jnp.int32, (rows, gw), 1) // dh == u


def _only_head(x, own, per, u):
    return x if per == 1 else jnp.where(own[u], x, jnp.zeros_like(x))


def _step_scores(qz, kw, kb_row, ok, scale):
    s = lax.dot_general(qz, kw, (((1,), (1,)), ((), ())), preferred_element_type=F32) * scale
    if kb_row is not None:
        s = s - kb_row
    if ok is not None:
        s = jnp.where(ok, s, MASKED)
    return s


def _attn_fwd(q, k, v, kb=None, *, classes=1, heads, dh, mode, max_dist=None, name):
    (qa, qc), (ka, kc), (va, vc) = q, k, v
    n, lq, lk = qa.shape[0], qa.shape[1], ka.shape[1]
    width = heads * dh
    tq, win = _attn_tiles(mode, lq, lk)
    gw, per, groups = _head_groups(heads, dh)
    scale = dh ** -0.5
    has_kb = kb is not None
    single = mode != "causal"

    def body(*refs):
        q_ref, k_ref, v_ref = refs[:3]
        kb_ref = refs[3] if has_kb else None
        o_ref, lse_ref = refs[-2:]
        i = pl.program_id(2)
        lane = lax.broadcasted_iota(jnp.int32, (tq, LANES), 1)
        own = [_own_lanes(tq, gw, dh, u) for u in range(per)]

        def step(j, carry):
            m_in, l_in = carry
            m_out, l_out = m_in, l_in
            start, _ = _window(mode, i, j, tq, win, lk)
            ok = _allowed(mode, i, start, tq, win, max_dist)
            for g in range(groups):
                cols = slice(g * gw, (g + 1) * gw)
                qg = q_ref[0, :, cols]
                kw = k_ref[0, pl.ds(start, win), cols]
                vw = v_ref[0, pl.ds(start, win), cols]
                alpha_g = new_g = None
                for u in range(per):
                    h = g * per + u
                    kb_row = kb_ref[0, h, pl.ds(j, 1), :] if has_kb else None
                    s = _step_scores(_only_head(qg, own, per, u), kw, kb_row, ok, scale)
                    m_new = jnp.max(s, axis=1, keepdims=True)
                    if not single:
                        m_old = m_in[:, h:h + 1]
                        m_new = jnp.maximum(m_old, m_new)
                        alpha = jnp.exp(m_old - m_new)
                        alpha_g = alpha if u == 0 else jnp.where(own[u], alpha, alpha_g)
                    p = jnp.exp(s - m_new)
                    l_new = jnp.sum(p, axis=1, keepdims=True)
                    r = jnp.dot(p.astype(BF16), vw, preferred_element_type=F32)
                    if single:
                        r = r * (1.0 / l_new)
                    else:
                        l_new = alpha * l_in[:, h:h + 1] + l_new
                    new_g = r if u == 0 else jnp.where(own[u], r, new_g)
                    m_out = jnp.where(lane == h, m_new, m_out)
                    l_out = jnp.where(lane == h, l_new, l_out)
                o_ref[0, :, cols] = new_g if single else alpha_g * o_ref[0, :, cols] + new_g
            return m_out, l_out

        carry = (jnp.full((tq, LANES), MASKED, F32), jnp.zeros((tq, LANES), F32))
        if single:
            m_all, l_all = step(0, carry)
            l_all = jnp.where(lane < heads, l_all, 1.0)
        else:
            o_ref[...] = jnp.zeros(o_ref.shape, F32)
            m_all, l_all = lax.fori_loop(0, _window(mode, i, 0, tq, win, lk)[1], step, carry)
            l_all = jnp.where(lane < heads, l_all, 1.0)
            inv = 1.0 / l_all
            for g in range(groups):
                cols = slice(g * gw, (g + 1) * gw)
                inv_g = inv[:, g * per:g * per + 1]
                for u in range(1, per):
                    inv_g = jnp.where(own[u], inv[:, g * per + u:g * per + u + 1], inv_g)
                o_ref[0, :, cols] = o_ref[0, :, cols] * inv_g
        lse_ref[0] = jnp.where(lane < heads, m_all + jnp.log(l_all), 0.0)

    in_specs = [_attn_specs(qa, width, qc, classes, tq, False), _attn_specs(ka, width, kc, classes, win, True),
                _attn_specs(va, width, vc, classes, win, True)]
    args = [qa, ka, va]
    if has_kb:
        in_specs.append(pl.BlockSpec((1,) + kb.shape[1:], lambda n_, r, i: (n_, 0, 0, 0)))
        args.append(kb)
    out_shape = [jax.ShapeDtypeStruct((n, lq, classes * width), F32), jax.ShapeDtypeStruct((n, lq, classes * LANES), F32)]
    out_specs = [pl.BlockSpec((1, tq, width), lambda n_, r, i: (n_, i, r)), pl.BlockSpec((1, tq, LANES), lambda n_, r, i: (n_, i, r))]
    return pl.pallas_call(
        body, grid=(n, classes, lq // tq), in_specs=in_specs, out_specs=out_specs, out_shape=out_shape, name=name,
        compiler_params=_params("parallel", "parallel", "arbitrary"),
    )(*args)


def _attn_bwd(q, k, v, do, lse, delta, kb=None, *, classes=1, heads, dh, mode, max_dist=None, dq_dtype=F32, name):
    (qa, qc), (ka, kc), (va, vc), (da, dc) = q, k, v, do
    n, lq, lk = qa.shape[0], qa.shape[1], ka.shape[1]
    width = heads * dh
    tq, win = _attn_tiles(mode, lq, lk)
    gw, per, groups = _head_groups(heads, dh)
    scale = dh ** -0.5
    has_kb = kb is not None
    single = mode != "causal"
    nt = (((1,), (1,)), ((), ()))
    tn = (((0,), (0,)), ((), ()))

    def body(*refs):
        q_ref, k_ref, v_ref, do_ref, lse_ref, dl_ref = refs[:6]
        kb_ref = refs[6] if has_kb else None
        n_in = 7 if has_kb else 6
        dq_ref, dk_ref, dv_ref = refs[n_in:n_in + 3]
        dkb_ref, drow_ref = refs[n_in + 3:n_in + 5] if has_kb else (None, None)
        i = pl.program_id(2)

        @pl.when(i == 0)
        def _():
            dk_ref[...] = jnp.zeros(dk_ref.shape, F32)
            dv_ref[...] = jnp.zeros(dv_ref.shape, F32)
            if has_kb:
                dkb_ref[...] = jnp.zeros(dkb_ref.shape, F32)

        lse_all = lse_ref[0]
        dl_all = dl_ref[0]
        lane = lax.broadcasted_iota(jnp.int32, (tq, LANES), 1)
        own = [_own_lanes(tq, gw, dh, u) for u in range(per)]

        def step(j, drow_all):
            start, _ = _window(mode, i, j, tq, win, lk)
            ok = _allowed(mode, i, start, tq, win, max_dist)
            for g in range(groups):
                cols = slice(g * gw, (g + 1) * gw)
                qg = q_ref[0, :, cols]
                dog = do_ref[0, :, cols]
                kw = k_ref[0, pl.ds(start, win), cols]
                vw = v_ref[0, pl.ds(start, win), cols]
                dq_g = dk_w = dv_w = None
                for u in range(per):
                    h = g * per + u
                    qz, doz = _only_head(qg, own, per, u), _only_head(dog, own, per, u)
                    kb_row = kb_ref[0, h, pl.ds(j, 1), :] if has_kb else None
                    p = jnp.exp(_step_scores(qz, kw, kb_row, ok, scale) - lse_all[:, h:h + 1])
                    dp = lax.dot_general(doz, vw, nt, preferred_element_type=F32)
                    ds = p * (dp - dl_all[:, h:h + 1])
                    dsb = ds.astype(BF16)
                    r = jnp.dot(dsb, kw, preferred_element_type=F32)
                    dq_g = r if u == 0 else jnp.where(own[u], r, dq_g)
                    dk_u = lax.dot_general(dsb, qz, tn, preferred_element_type=F32)
                    dv_u = lax.dot_general(p.astype(BF16), doz, tn, preferred_element_type=F32)
                    dk_w = dk_u if u == 0 else dk_w + dk_u
                    dv_w = dv_u if u == 0 else dv_w + dv_u
                    if has_kb:
                        dkb_ref[0, h, pl.ds(j, 1), :] += -jnp.sum(ds, axis=0, keepdims=True)
                        drow_all = drow_all + jnp.where(lane == h, jnp.sum(ds, axis=1, keepdims=True), 0.0)
                dk_ref[0, pl.ds(start, win), cols] += dk_w * scale
                dv_ref[0, pl.ds(start, win), cols] += dv_w
                if single:
                    dq_ref[0, :, cols] = (dq_g * scale).astype(dq_ref.dtype)
                else:
                    dq_ref[0, :, cols] += dq_g * scale
            return drow_all

        drow_all = jnp.zeros((tq, LANES), F32)
        if single:
            drow_all = step(0, drow_all)
        else:
            dq_ref[...] = jnp.zeros(dq_ref.shape, F32)
            drow_all = lax.fori_loop(0, _window(mode, i, 0, tq, win, lk)[1], step, drow_all)
        if has_kb:
            drow_ref[0] = drow_all

    row_spec = functools.partial(_attn_specs, classes=classes, rows_block=tq, whole=False)
    in_specs = [row_spec(qa, width, qc), _attn_specs(ka, width, kc, classes, win, True), _attn_specs(va, width, vc, classes, win, True),
                row_spec(da, width, dc), row_spec(lse, LANES, 0), row_spec(delta, LANES, 0)]
    args = [qa, ka, va, da, lse, delta]
    assert single or dq_dtype == F32
    out_shape = [jax.ShapeDtypeStruct((n, lq, classes * width), dq_dtype), jax.ShapeDtypeStruct((n, lk, classes * width), F32),
                 jax.ShapeDtypeStruct((n, lk, classes * width), F32)]
    kv_spec = pl.BlockSpec((1, lk, width), lambda n_, r, i: (n_, 0, r))
    out_specs = [pl.BlockSpec((1, tq, width), lambda n_, r, i: (n_, i, r)), kv_spec, kv_spec]
    if has_kb:
        kb_spec = pl.BlockSpec((1,) + kb.shape[1:], lambda n_, r, i: (n_, 0, 0, 0))
        in_specs.append(kb_spec)
        args.append(kb)
        out_shape += [jax.ShapeDtypeStruct(kb.shape, F32), jax.ShapeDtypeStruct((n, lq, LANES), F32)]
        out_specs += [kb_spec, pl.BlockSpec((1, tq, LANES), lambda n_, r, i: (n_, i, 0))]
    return pl.pallas_call(
        body, grid=(n, classes, lq // tq), in_specs=in_specs, out_specs=out_specs, out_shape=out_shape, name=name,
        compiler_params=_params("parallel", "parallel", "arbitrary"),
    )(*args)


def _head_delta(do, out, heads, *, name, lse=None, sink=None):
    width = out.shape[1]
    gather = _head_expand(heads, width).T

    if sink is None:
        return _rowwise(lambda do, o, e: _dot_exact(do * o, e), [do, out], [gather], [(LANES, F32)], name=name)[0]

    def fn(do, o, lse, e, sink):
        delta = _dot_exact(do * o, e)
        return delta, -jnp.sum(jnp.exp(sink - lse) * delta, axis=0, keepdims=True)
    return _rowwise(fn, [do, out, lse], [gather, sink], [(LANES, F32)], [(1, LANES)], name=name)


def _rope_tables():
    half = HEAD_DIM // 2
    inv = ROPE_THETA ** (-jnp.arange(half, dtype=F32) / half)
    ang = jnp.arange(SEQ, dtype=F32)[:, None] * inv[None, :]
    cos = jnp.tile(jnp.cos(ang), (1, 2 * ATT_W // HEAD_DIM))
    sin = jnp.tile(jnp.concatenate([-jnp.sin(ang), jnp.sin(ang)], axis=1), (1, ATT_W // HEAD_DIM))
    return cos, sin


def _rotate(x, cos, sin):
    width = x.shape[1]
    lane = lax.broadcasted_iota(jnp.int32, x.shape, 1)
    first_half = (lane % HEAD_DIM) < (HEAD_DIM // 2)
    swapped = jnp.where(first_half, pltpu.roll(x, width - HEAD_DIM // 2, axis=1), pltpu.roll(x, HEAD_DIM // 2, axis=1))
    return x * cos[:, :width] + swapped * sin[:, :width]


def _gelu(x):
    k = math.sqrt(2.0 / math.pi)
    t = jnp.tanh(k * (x + 0.044715 * x * x * x))
    return 0.5 * x * (1.0 + t), t


def _gelu_grad(x, t):
    k = math.sqrt(2.0 / math.pi)
    return 0.5 * (1.0 + t) + 0.5 * x * (1.0 - t * t) * k * (1.0 + 3.0 * 0.044715 * x * x)


def _shift_down(x, halo, s):
    ext = jnp.concatenate([halo, x], axis=0)
    return pltpu.roll(ext, s, axis=0)[8:, :]


def _shift_up(x, halo, s):
    rows = x.shape[0]
    ext = jnp.concatenate([x, halo], axis=0)
    return pltpu.roll(ext, rows + 8 - s, axis=0)[:rows, :]


def _lru_gates(u, halo, cw, cb, wa, ba, wi, bi, lam):
    taps = [_shift_down(u, halo, CONV_WIDTH - 1 - k) for k in range(CONV_WIDTH - 1)] + [u]
    uc = cb + sum(cw[k:k + 1, :] * taps[k] for k in range(CONV_WIDTH))
    ucb = uc.astype(BF16)
    r = jax.nn.sigmoid(jnp.dot(ucb, wa, preferred_element_type=F32) + ba)
    gi = jax.nn.sigmoid(jnp.dot(ucb, wi, preferred_element_type=F32) + bi)
    sp = jnp.maximum(-lam, 0.0) + jnp.log(1.0 + jnp.exp(-jnp.abs(lam)))
    log_a = -LRU_C * r * sp
    a = jnp.exp(log_a)
    x2 = 2.0 * log_a
    one_minus_a2 = jnp.where(x2 > -0.01, -x2 * (1.0 + x2 * (0.5 + x2 * (1.0 / 6.0 + x2 / 24.0))), 1.0 - jnp.exp(x2))
    mult = jnp.sqrt(one_minus_a2)
    return taps, uc, ucb, r, gi, sp, a, mult


def _lru_specs(nchunk, reverse):
    def chunk(b, c):
        return b * nchunk + ((nchunk - 1 - c) if reverse else c)

    def halo_before(b, c):
        return jnp.maximum(chunk(b, c) * (LRU_CHUNK // 8) - 1, 0)

    return chunk, halo_before


def _lru_fwd(zug, cw, cb, wa, ba, wi, bi, lam, *, name):
    total = zug.shape[0]
    nchunk = SEQ // LRU_CHUNK
    w = LRU_WIDTH
    chunk, halo_before = _lru_specs(nchunk, False)

    def body(u_ref, uh_ref, g_ref, cw_ref, cb_ref, wa_ref, ba_ref, wi_ref, bi_ref, lam_ref, y_ref, h_ref, a_sc, x_sc, carry_sc):
        c = pl.program_id(1)
        u = u_ref[...]
        halo = jnp.where(c > 0, uh_ref[...], 0.0)
        _, uc, _, _, gi, _, a, mult = _lru_gates(u, halo, cw_ref[...], cb_ref[...], wa_ref[...], ba_ref[...],
                                                 wi_ref[...], bi_ref[...], lam_ref[...])
        a_sc[...] = a
        x_sc[...] = mult * (gi * uc)

        @pl.when(c == 0)
        def _():
            carry_sc[...] = jnp.zeros(carry_sc.shape, F32)

        def tile_step(t, h):
            r0 = pl.multiple_of(t * 8, 8)
            at = a_sc[pl.ds(r0, 8), :]
            xt = x_sc[pl.ds(r0, 8), :]
            rows = []
            for j in range(8):
                h = at[j:j + 1, :] * h + xt[j:j + 1, :]
                rows.append(h)
            h_ref[pl.ds(r0, 8), :] = jnp.concatenate(rows, axis=0)
            return h

        h_last = lax.fori_loop(0, LRU_CHUNK // 8, tile_step, carry_sc[0:1, :])
        carry_sc[0:1, :] = h_last
        gel, _ = _gelu(g_ref[...])
        y_ref[...] = (h_ref[...] * gel).astype(BF16)

    small = lambda arr: pl.BlockSpec(arr.shape, lambda b, c: (0, 0))
    return pl.pallas_call(
        body, grid=(total // SEQ, nchunk),
        in_specs=[pl.BlockSpec((LRU_CHUNK, w), lambda b, c: (chunk(b, c), 0)),
                  pl.BlockSpec((8, w), lambda b, c: (halo_before(b, c), 0)),
                  pl.BlockSpec((LRU_CHUNK, w), lambda b, c: (chunk(b, c), 1)),
                  small(cw), small(cb), small(wa), small(ba), small(wi), small(bi), small(lam)],
        out_specs=[pl.BlockSpec((LRU_CHUNK, w), lambda b, c: (chunk(b, c), 0))] * 2,
        out_shape=[jax.ShapeDtypeStruct((total, w), BF16), jax.ShapeDtypeStruct((total, w), F32)],
        scratch_shapes=[pltpu.VMEM((LRU_CHUNK, w), F32), pltpu.VMEM((LRU_CHUNK, w), F32), pltpu.VMEM((8, w), F32)],
        name=name, compiler_params=_params("arbitrary", "arbitrary"),
    )(zug, zug, zug, cw, cb, wa, ba, wi, bi, lam)


def _lru_bwd(zug, hl, dy, cw, cb, wa, ba, wi, bi, lam, *, name):
    total = zug.shape[0]
    nchunk = SEQ // LRU_CHUNK
    w = LRU_WIDTH
    chunk, halo_before = _lru_specs(nchunk, True)
    tn = (((0,), (0,)), ((), ()))
    nt = (((1,), (1,)), ((), ()))

    def body(u_ref, uh_ref, g_ref, hl_ref, hh_ref, dy_ref, cw_ref, cb_ref, wa_ref, ba_ref, wi_ref, bi_ref, lam_ref,
             dz_ref, dcw_ref, dcb_ref, dwa_ref, dba_ref, dwi_ref, dbi_ref, dlam_ref,
             a_sc, d_sc, g_sc, gcarry_sc, acarry_sc, duc_sc):
        b, c = pl.program_id(0), pl.program_id(1)
        first_chunk = c == nchunk - 1

        @pl.when((b == 0) & (c == 0))
        def _():
            for ref in (dcw_ref, dcb_ref, dwa_ref, dba_ref, dwi_ref, dbi_ref, dlam_ref):
                ref[...] = jnp.zeros(ref.shape, F32)

        @pl.when(c == 0)
        def _():
            gcarry_sc[...] = jnp.zeros(gcarry_sc.shape, F32)
            acarry_sc[...] = jnp.zeros(acarry_sc.shape, F32)
            duc_sc[...] = jnp.zeros(duc_sc.shape, F32)

        u = u_ref[...]
        halo = jnp.where(first_chunk, 0.0, uh_ref[...])
        cw, wa, wi, lam = cw_ref[...], wa_ref[...], wi_ref[...], lam_ref[...]
        taps, uc, ucb, r, gi, sp, a, mult = _lru_gates(u, halo, cw, cb_ref[...], wa, ba_ref[...], wi, bi_ref[...], lam)
        gate = g_ref[...]
        gel, th = _gelu(gate)
        dy = dy_ref[...]
        hl = hl_ref[...]
        a_sc[...] = a
        d_sc[...] = dy * gel
        dgate = dy * hl * _gelu_grad(gate, th)

        def tile_step(t, carry):
            g_next, a_next = carry
            r0 = pl.multiple_of((LRU_CHUNK // 8 - 1 - t) * 8, 8)
            dt = d_sc[pl.ds(r0, 8), :]
            at = a_sc[pl.ds(r0, 8), :]
            rows = [None] * 8
            for j in reversed(range(8)):
                g_next = dt[j:j + 1, :] + a_next * g_next
                a_next = at[j:j + 1, :]
                rows[j] = g_next
            g_sc[pl.ds(r0, 8), :] = jnp.concatenate(rows, axis=0)
            return g_next, a_next

        g_last, a_last = lax.fori_loop(0, LRU_CHUNK // 8, tile_step, (gcarry_sc[0:1, :], acarry_sc[0:1, :]))
        gcarry_sc[0:1, :] = g_last
        acarry_sc[0:1, :] = a_last

        gs = g_sc[...]
        hl_halo = jnp.where(first_chunk, 0.0, hh_ref[...])
        da = gs * _shift_down(hl, hl_halo, 1)
        dmult = gs * gi * uc
        dgi = gs * mult * uc
        duc = gs * mult * gi
        dlog_a = da * a - dmult * a * a / mult
        dr = dlog_a * (-LRU_C * sp)
        dsp = jnp.sum(dlog_a * (-LRU_C * r), axis=0, keepdims=True)
        dlam_ref[...] += -dsp * jax.nn.sigmoid(-lam)
        dpa = dr * r * (1.0 - r)
        dpi = dgi * gi * (1.0 - gi)
        dpab, dpib = dpa.astype(BF16), dpi.astype(BF16)
        dba_ref[...] += jnp.sum(dpa, axis=0, keepdims=True)
        dbi_ref[...] += jnp.sum(dpi, axis=0, keepdims=True)
        dwa_ref[...] += lax.dot_general(ucb, dpab, tn, preferred_element_type=F32)
        dwi_ref[...] += lax.dot_general(ucb, dpib, tn, preferred_element_type=F32)
        duc = duc + lax.dot_general(dpab, wa, nt, preferred_element_type=F32)
        duc = duc + lax.dot_general(dpib, wi, nt, preferred_element_type=F32)
        dcb_ref[...] += jnp.sum(duc, axis=0, keepdims=True)
        dcw_ref[...] += jnp.concatenate([jnp.sum(duc * taps[k], axis=0, keepdims=True) for k in range(CONV_WIDTH)], axis=0)
        after = duc_sc[...]
        du = cw[CONV_WIDTH - 1:CONV_WIDTH, :] * duc
        for k in range(CONV_WIDTH - 1):
            du = du + cw[k:k + 1, :] * _shift_up(duc, after, CONV_WIDTH - 1 - k)
        duc_sc[...] = duc[0:8, :]
        dz_ref[:, 0:w] = du.astype(BF16)
        dz_ref[:, w:2 * w] = dgate.astype(BF16)

    small = lambda arr: pl.BlockSpec(arr.shape, lambda b, c: (0, 0))
    acc = lambda shape: pl.BlockSpec(shape, lambda b, c: (0, 0))
    chunk_spec = lambda cb_: pl.BlockSpec((LRU_CHUNK, w), lambda b, c: (chunk(b, c), cb_))
    halo_spec = pl.BlockSpec((8, w), lambda b, c: (halo_before(b, c), 0))
    acc_shapes = [(CONV_WIDTH, w), (1, w), (w, w), (1, w), (w, w), (1, w), (1, w)]
    return pl.pallas_call(
        body, grid=(total // SEQ, nchunk),
        in_specs=[chunk_spec(0), halo_spec, chunk_spec(1), chunk_spec(0), halo_spec, chunk_spec(0),
                  small(cw), small(cb), small(wa), small(ba), small(wi), small(bi), small(lam)],
        out_specs=[pl.BlockSpec((LRU_CHUNK, 2 * w), lambda b, c: (chunk(b, c), 0))] + [acc(s) for s in acc_shapes],
        out_shape=[jax.ShapeDtypeStruct((total, 2 * w), BF16)] + [jax.ShapeDtypeStruct(s, F32) for s in acc_shapes],
        scratch_shapes=[pltpu.VMEM((LRU_CHUNK, w), F32)] * 3 + [pltpu.VMEM((8, w), F32)] * 3,
        name=name, compiler_params=_params("arbitrary", "arbitrary"),
    )(zug, zug, zug, hl, hl, dy, cw, cb, wa, ba, wi, bi, lam)


def _block_diag(wb):
    eye = jnp.eye(LRU_BLOCKS, dtype=wb.dtype)
    return jnp.einsum("gcd,gh->gchd", wb, eye).reshape(LRU_WIDTH, LRU_WIDTH).astype(BF16)


def _diag_blocks(wd):
    blk = LRU_WIDTH // LRU_BLOCKS
    return jnp.stack([wd[g * blk:(g + 1) * blk, g * blk:(g + 1) * blk] for g in range(LRU_BLOCKS)])


FOX_SCAN = 256


def _fox_bias(fl, bf, *, name):
    nb = fl.shape[0]

    def body(fl_ref, bf_ref, c_ref):
        x = fl_ref[0] + bf_ref[...]
        logf = jnp.minimum(x, 0.0) - jnp.log(1.0 + jnp.exp(-jnp.abs(x)))
        upper = (lax.broadcasted_iota(jnp.int32, (FOX_SCAN, FOX_SCAN), 0)
                 <= lax.broadcasted_iota(jnp.int32, (FOX_SCAN, FOX_SCAN), 1)).astype(BF16)
        carry = jnp.zeros((LRU_BLOCKS, 1), F32)
        for j in range(SEQ // FOX_SCAN):
            blk = logf[:, j * FOX_SCAN:(j + 1) * FOX_SCAN]
            c_ref[0, :, j * FOX_SCAN:(j + 1) * FOX_SCAN] = _dot_exact(blk, upper) + carry
            carry = carry + jnp.sum(blk, axis=1, keepdims=True)

    return pl.pallas_call(
        body, grid=(nb,),
        in_specs=[pl.BlockSpec((1, 8, SEQ), lambda b: (b, 0, 0)), pl.BlockSpec((8, 1), lambda b: (0, 0))],
        out_specs=pl.BlockSpec((1, 8, SEQ), lambda b: (b, 0, 0)),
        out_shape=jax.ShapeDtypeStruct((nb, 8, SEQ), F32), name=name, compiler_params=_params("arbitrary"),
    )(fl, bf)


def _fox_bias_bwd(fl, bf, dc, *, name):
    nb = fl.shape[0]

    def body(fl_ref, bf_ref, dc_ref, dfl_ref, dbf_ref):
        @pl.when(pl.program_id(0) == 0)
        def _():
            dbf_ref[...] = jnp.zeros(dbf_ref.shape, F32)

        x = fl_ref[0] + bf_ref[...]
        dc_all = dc_ref[0]
        lower = (lax.broadcasted_iota(jnp.int32, (FOX_SCAN, FOX_SCAN), 0)
                 >= lax.broadcasted_iota(jnp.int32, (FOX_SCAN, FOX_SCAN), 1)).astype(BF16)
        carry = jnp.zeros((LRU_BLOCKS, 1), F32)
        total = jnp.zeros((LRU_BLOCKS, 1), F32)
        for j in reversed(range(SEQ // FOX_SCAN)):
            cols = slice(j * FOX_SCAN, (j + 1) * FOX_SCAN)
            blk = dc_all[:, cols]
            dlogf = _dot_exact(blk, lower) + carry
            carry = carry + jnp.sum(blk, axis=1, keepdims=True)
            dx = dlogf * jax.nn.sigmoid(-x[:, cols])
            dfl_ref[0, :, cols] = dx
            total = total + jnp.sum(dx, axis=1, keepdims=True)
        dbf_ref[...] += total

    return pl.pallas_call(
        body, grid=(nb,),
        in_specs=[pl.BlockSpec((1, 8, SEQ), lambda b: (b, 0, 0)), pl.BlockSpec((8, 1), lambda b: (0, 0)),
                  pl.BlockSpec((1, 8, SEQ), lambda b: (b, 0, 0))],
        out_specs=[pl.BlockSpec((1, 8, SEQ), lambda b: (b, 0, 0)), pl.BlockSpec((8, 1), lambda b: (0, 0))],
        out_shape=[jax.ShapeDtypeStruct((nb, 8, SEQ), F32), jax.ShapeDtypeStruct((8, 1), F32)],
        name=name, compiler_params=_params("arbitrary"),
    )(fl, bf, dc)


def _seq3(a, nb):
    return a.reshape(nb, a.shape[0] // nb, a.shape[1])


def _flat2(a):
    return a.reshape(a.shape[0] * a.shape[1], a.shape[2])


def _mlp_fwd(h, p, tag):
    hn = _rms_fwd(h, p["norm"], name=f"{tag}_norm")
    act, = _matmul(hn, p["w_up_t"], tb=True, outs=(BF16,), epilogue=lambda acc: (jnp.square(jnp.maximum(acc, 0.0)),),
                   name=f"{tag}_up")
    out, = _matmul(act, p["w_down"], extras=(h,), epilogue=lambda acc, res: (acc + res,), name=f"{tag}_down")
    return out, (h, hn, act)


def _mlp_bwd(dh, dhb, p, saved, tag):
    h, hn, act = saved
    dup, = _matmul(dhb, p["w_down"], tb=True, outs=(BF16,), extras=(act,),
                   epilogue=lambda acc, act: (acc * (2.0 * jnp.sqrt(act.astype(F32))),), name=f"{tag}_bwd_dup")
    dw_down, = _matmul(act, dhb, ta=True, outs=(BF16,),name=f"{tag}_bwd_dwdown")
    dw_up_t, = _matmul(dup, hn, ta=True, outs=(BF16,),name=f"{tag}_bwd_dwup")
    dh2, dh2b, dg = _norm_input_grad(dup, p["w_up_t"], h, dh, p["norm"], name=f"{tag}_bwd_dh")
    return dh2, dh2b, {"norm": dg, "w_up_t": dw_up_t, "w_down": dw_down}


def _xa_fwd(h, mem, p, tag, nb):
    hn = _rms_fwd(h, p["norm"], name=f"{tag}_norm")
    mem_n = _rms_fwd(mem, p["mem_norm"], name=f"{tag}_memnorm")
    q, = _matmul(hn, p["w_q"], outs=(BF16,), name=f"{tag}_q")
    kv, = _matmul(mem_n, p["w_kv_t"], tb=True, outs=(BF16,), name=f"{tag}_kv")
    q3, kv3 = _seq3(q, nb), _seq3(kv, nb)
    o, lse = _attn_fwd((q3, 0), (kv3, 0), (kv3, 1), heads=XA_HEADS, dh=XA_HEAD_DIM, mode="full",
                       name=f"{tag}_attn")
    o = _flat2(o)
    ob, = _rowwise(lambda o: o, [o], [], [(D_MODEL, BF16)], name=f"{tag}_ocast")
    out, = _matmul(ob, p["w_o"], extras=(h,), epilogue=lambda acc, res: (acc + res,), name=f"{tag}_o")
    return out, (h, hn, mem_n, q3, kv3, o, ob, lse)


def _xa_bwd(dh, dhb, mem, p, saved, tag, nb):
    h, hn, mem_n, q3, kv3, o, ob, lse = saved
    do, dob = _matmul(dhb, p["w_o"], tb=True, outs=(F32, BF16), epilogue=lambda acc: (acc, acc), name=f"{tag}_bwd_do")
    dw_o, = _matmul(ob, dhb, ta=True, outs=(BF16,),name=f"{tag}_bwd_dwo")
    delta = _head_delta(do, o, XA_HEADS, name=f"{tag}_bwd_delta")
    dq, dk, dv = _attn_bwd((q3, 0), (kv3, 0), (kv3, 1), (_seq3(dob, nb), 0), lse, _seq3(delta, nb), heads=XA_HEADS,
                           dh=XA_HEAD_DIM, mode="full", dq_dtype=BF16, name=f"{tag}_bwd_attn")
    dqb = _flat2(dq)
    dkvb, = _rowwise(lambda a, b: jnp.concatenate([a, b], axis=1), [_flat2(dk), _flat2(dv)], [], [(2 * D_MODEL, BF16)],
                     name=f"{tag}_bwd_dkvcast")
    dw_q, = _matmul(hn, dqb, ta=True, outs=(BF16,),name=f"{tag}_bwd_dwq")
    dw_kv_t, = _matmul(dkvb, mem_n, ta=True, outs=(BF16,),name=f"{tag}_bwd_dwkv")
    dmem_n, = _matmul(dkvb, p["w_kv_t"], name=f"{tag}_bwd_dmemn")
    dg_mem, = _rowwise(lambda x, d, g: _rms_bwd_vals(x, d, g)[1], [mem, dmem_n], [p["mem_norm"]], [], [(1, D_MODEL)],
                       name=f"{tag}_bwd_memnorm")
    dh2, dh2b, dg = _norm_input_grad(dqb, p["w_q"], h, dh, p["norm"], tb=True, name=f"{tag}_bwd_dh")
    return dh2, dh2b, {"norm": dg, "mem_norm": dg_mem, "w_q": dw_q, "w_kv_t": dw_kv_t, "w_o": dw_o}


AB_MAIN = 2 * LRU_WIDTH + 3 * ATT_W


def _ab_fwd(h, p, nb):
    hn = _rms_fwd(h, p["norm"], name="ab_norm")
    w_in_t = p["w_in_t"]
    zug, = _matmul(hn, w_in_t[:2 * LRU_WIDTH], tb=True, name="ab_in_ug")
    qkv, = _matmul(hn, w_in_t[2 * LRU_WIDTH:AB_MAIN], tb=True, outs=(BF16,), tn=768, name="ab_in_qkv")
    zf, = _matmul(hn, w_in_t[AB_MAIN:], tb=True, name="ab_in_f")
    fl = zf[:, :8].reshape(nb, SEQ, 8).transpose(0, 2, 1)
    cbias = _fox_bias(fl, p["b_f"], name="ab_fox_bias")
    kb = cbias.reshape(nb, 8, SEQ // ATT_CHUNK, ATT_CHUNK)
    y_a, hl = _lru_fwd(zug, p["conv_w"], p["conv_b"], p["w_a"], p["b_a"], p["w_i"], p["b_i"], p["lam"], name="ab_lru")
    qkv3 = _seq3(qkv, nb)
    o, lse = _attn_fwd((qkv3, 0), (qkv3, 1), (qkv3, 2), kb, heads=8, dh=HEAD_DIM, mode="causal", name="ab_fox")
    o = _flat2(o)
    y, = _rowwise(lambda a, b: jnp.concatenate([a, b.astype(BF16)], axis=1), [y_a, o], [], [(D_MODEL, BF16)], name="ab_ycat")
    out, = _matmul(y, p["w_out"], extras=(h,), epilogue=lambda acc, res: (acc + res,), name="ab_out")
    return out, (h, hn, zug, qkv3, fl, kb, hl, o, lse, y)


def _ab_bwd(dh, dhb, p, saved, nb):
    h, hn, zug, qkv3, fl, kb, hl, o, lse, y = saved
    dy, dyb = _matmul(dhb, p["w_out"], tb=True, outs=(F32, BF16), epilogue=lambda acc: (acc, acc), name="ab_bwd_dy")
    dw_out, = _matmul(y, dhb, ta=True, outs=(BF16,),name="ab_bwd_dwout")
    dzug, dcw, dcb, dwa, dba, dwi, dbi, dlam = _lru_bwd(zug, hl, dy, p["conv_w"], p["conv_b"], p["w_a"], p["b_a"],
                                                        p["w_i"], p["b_i"], p["lam"], name="ab_bwd_lru")
    delta = _head_delta((dy, ATT_W, 1), o, 8, name="ab_bwd_delta")
    dq, dk, dv, dkb, drow = _attn_bwd((qkv3, 0), (qkv3, 1), (qkv3, 2), (_seq3(dyb, nb), 1), lse, _seq3(delta, nb), kb,
                                      heads=8, dh=HEAD_DIM, mode="causal", name="ab_bwd_fox")
    dcum = dkb.reshape(nb, 8, SEQ) + drow[:, :, :8].transpose(0, 2, 1)
    dfl, dbf = _fox_bias_bwd(fl, p["b_f"], dcum, name="ab_bwd_fox_bias")
    dzf = jnp.pad(dfl.transpose(0, 2, 1).reshape(nb * SEQ, 8), ((0, 0), (0, LANES - 8)))
    dz, = _rowwise(lambda a, q, k, v, f: jnp.concatenate([a, q.astype(BF16), k.astype(BF16), v.astype(BF16), f.astype(BF16)], axis=1),
                   [dzug, _flat2(dq), _flat2(dk), _flat2(dv), dzf], [], [(AB_MAIN + LANES, BF16)], name="ab_bwd_dzcat")
    dw_in_t, = _matmul(dz, hn, ta=True, outs=(BF16,),tm=384, name="ab_bwd_dwin")
    dh2, dh2b, dg = _norm_input_grad(dz, p["w_in_t"], h, dh, p["norm"], name="ab_bwd_dh")
    grads = {"norm": dg, "w_in_t": dw_in_t[:AB_MAIN + 8], "conv_w": dcw, "conv_b": dcb, "w_a": _diag_blocks(dwa), "b_a": dba,
             "w_i": _diag_blocks(dwi), "b_i": dbi, "lam": dlam, "b_f": dbf.reshape(1, 8), "w_out": dw_out}
    return dh2, dh2b, grads


CD_IN = 3 * ATT_W + ATT_W + 2 * SWA_KW


def _gqa_share():
    src = jnp.arange(SWA_KW)[:, None]
    dst = jnp.arange(ATT_W)[None, :]
    per_kv = ATT_W // (SWA_KW // HEAD_DIM)
    return ((dst // per_kv == src // HEAD_DIM) & (dst % HEAD_DIM == src % HEAD_DIM)).astype(BF16)


def _cd_fwd(h, p, nb):
    hn = _rms_fwd(h, p["norm"], name="cd_norm")
    z, = _matmul(hn, p["w_in_t"], tb=True, tn=384, name="cd_in")
    cos, sin = _rope_tables()
    per_seq = SEQ // ROW_TILE
    table_map = lambda i: (i % per_seq, 0)

    def rope_fn(z, cos, sin, share):
        qc, kc, vc = z[:, 0:ATT_W], z[:, ATT_W:2 * ATT_W], z[:, 2 * ATT_W:3 * ATT_W]
        qd, kd, vd = z[:, 3 * ATT_W:4 * ATT_W], z[:, 4 * ATT_W:4 * ATT_W + SWA_KW], z[:, 4 * ATT_W + SWA_KW:]
        kd8 = jnp.dot(_rotate(kd, cos, sin).astype(BF16), share, preferred_element_type=F32)
        vd8 = jnp.dot(vd.astype(BF16), share, preferred_element_type=F32)
        qk = jnp.concatenate([_rotate(qc, cos, sin), _rotate(kc, cos, sin)], axis=1)
        return qk, vc, _rotate(qd, cos, sin), kd8, vd8, qk, qk, vc, vc
    dils = [dil for _, dil in DIL_PATTERN if dil > 1]
    outs = _rowwise(rope_fn, [z, cos, sin], [_gqa_share()],
                    [(2 * ATT_W, BF16)] + [(ATT_W, BF16)] * 4 + [(2 * ATT_W, BF16, d) for d in dils] + [(ATT_W, BF16, d) for d in dils],
                    name="cd_rope", row_maps=[None, table_map, table_map])
    qk, vc, qd, kd, vd = outs[:5]
    views = {1: (_seq3(qk, nb), _seq3(vc, nb))}
    for d, qk_d, vc_d in zip(dils, outs[5:5 + len(dils)], outs[5 + len(dils):]):
        views[d] = (_seq3(qk_d, nb), _seq3(vc_d, nb))
    in_classes = lambda a, width, d: _flat2(a) if d == 1 else ("classes", _flat2(a), width, d)
    branch = []
    for window, dil in DIL_PATTERN:
        qk_v, vc_v = views[dil]
        o_i, lse_i = _attn_fwd((qk_v, 0), (qk_v, 1), (vc_v, 0), classes=dil, heads=8, dh=HEAD_DIM, mode="band",
                               max_dist=window // dil, name=f"cd_dil{dil}")
        branch.append((in_classes(o_i, ATT_W, dil), in_classes(lse_i, LANES, dil)))
    expand = _head_expand(8, ATT_W)

    def combine(o1, o2, o3, l1, l2, l3, e):
        m = jnp.maximum(jnp.maximum(l1, l2), l3)
        lt = m + jnp.log(jnp.exp(l1 - m) + jnp.exp(l2 - m) + jnp.exp(l3 - m))
        o = sum(_dot_exact(jnp.exp(l - lt), e) * o_ for o_, l in ((o1, l1), (o2, l2), (o3, l3)))
        return o, lt
    y_c, lse_c = _rowwise(combine, [b[0] for b in branch] + [b[1] for b in branch], [expand], [(ATT_W, F32), (LANES, F32)],
                          name="cd_dil_combine")
    qd3, kd3, vd3 = _seq3(qd, nb), _seq3(kd, nb), _seq3(vd, nb)
    o_d, lse_band = _attn_fwd((qd3, 0), (kd3, 0), (vd3, 0), heads=8, dh=HEAD_DIM, mode="band", max_dist=SWA_WINDOW - 1,
                              name="cd_swa")

    def sink_combine(o, l, e, sink):
        lt = jnp.maximum(l, sink) + jnp.log(1.0 + jnp.exp(-jnp.abs(l - sink)))
        return o * _dot_exact(jnp.exp(l - lt), e), lt
    y_d, lse_d = _rowwise(sink_combine, [_flat2(o_d), _flat2(lse_band)], [expand, p["sink"]], [(ATT_W, F32), (LANES, F32)],
                          name="cd_swa_sink")
    y, = _rowwise(lambda a, b: jnp.concatenate([a, b], axis=1), [y_c, y_d], [], [(D_MODEL, BF16)], name="cd_ycat")
    out, = _matmul(y, p["w_out"], extras=(h,), epilogue=lambda acc, res: (acc + res,), name="cd_out")
    return out, (h, hn, views, qd3, kd3, vd3, y_c, lse_c, y_d, lse_d, y)


def _cd_bwd(dh, dhb, p, saved, nb):
    h, hn, views, qd3, kd3, vd3, y_c, lse_c, y_d, lse_d, y = saved
    dy, dyb = _matmul(dhb, p["w_out"], tb=True, outs=(F32, BF16), epilogue=lambda acc: (acc, acc), name="cd_bwd_dy")
    dw_out, = _matmul(y, dhb, ta=True, outs=(BF16,),name="cd_bwd_dwout")
    dyb3 = _seq3(dyb, nb)
    dils = [dil for _, dil in DIL_PATTERN if dil > 1]
    gather = _head_expand(8, ATT_W).T

    def prepare(do, o, lse, e):
        delta = _dot_exact(do * o, e)
        return (delta,) + (do,) * len(dils) + (lse,) * len(dils) + (delta,) * len(dils)
    outs = _rowwise(prepare, [(dy, ATT_W, 0), y_c, lse_c], [gather],
                    [(LANES, F32)] + [(ATT_W, BF16, d) for d in dils] + [(LANES, F32, d) for d in dils] * 2, name="cd_bwd_delta_dil")
    seen = {1: ((dyb3, 0), _seq3(lse_c, nb), _seq3(outs[0], nb))}
    for j, d in enumerate(dils):
        seen[d] = ((_seq3(outs[1 + j], nb), 0), _seq3(outs[1 + len(dils) + j], nb), _seq3(outs[1 + 2 * len(dils) + j], nb))
    in_classes = lambda a, d: _flat2(a) if d == 1 else ("classes", _flat2(a), ATT_W, d)
    parts = []
    for window, dil in DIL_PATTERN:
        (qk_v, vc_v), (do_v, lse_v, delta_v) = views[dil], seen[dil]
        grads = _attn_bwd((qk_v, 0), (qk_v, 1), (vc_v, 0), do_v, lse_v, delta_v, classes=dil, heads=8, dh=HEAD_DIM, mode="band",
                          max_dist=window // dil, name=f"cd_bwd_dil{dil}")
        parts.append([in_classes(a, dil) for a in grads])
    delta_d, dsink = _head_delta((dy, ATT_W, 1), y_d, 8, lse=lse_d, sink=p["sink"], name="cd_bwd_delta_swa")
    dqd, dkd, dvd = _attn_bwd((qd3, 0), (kd3, 0), (vd3, 0), (dyb3, 1), _seq3(lse_d, nb), _seq3(delta_d, nb), heads=8,
                              dh=HEAD_DIM, mode="band", max_dist=SWA_WINDOW - 1, name="cd_bwd_swa")
    cos, sin = _rope_tables()
    per_seq = SEQ // ROW_TILE
    table_map = lambda i: (i % per_seq, 0)

    def unrope(q1, q2, q3, k1, k2, k3, v1, v2, v3, qd, kd8, vd8, cos, sin, gather):
        back = lambda x: _rotate(x, cos, -sin)
        kd, vd = _dot_exact(kd8, gather), _dot_exact(vd8, gather)
        return jnp.concatenate([back(q1 + q2 + q3), back(k1 + k2 + k3), v1 + v2 + v3, back(qd), back(kd), vd], axis=1)
    ins = [parts[b][t] for t in range(3) for b in range(3)] + [_flat2(dqd), _flat2(dkd), _flat2(dvd), cos, sin]
    dz, = _rowwise(unrope, ins, [_gqa_share().T], [(CD_IN, BF16)], name="cd_bwd_unrope",
                   row_maps=[None] * 12 + [table_map, table_map])
    dw_in_t, = _matmul(dz, hn, ta=True, outs=(BF16,),tm=384, name="cd_bwd_dwin")
    dh2, dh2b, dg = _norm_input_grad(dz, p["w_in_t"], h, dh, p["norm"], name="cd_bwd_dh")
    return dh2, dh2b, {"norm": dg, "w_in_t": dw_in_t, "sink": dsink[:, :8], "w_out": dw_out}


def _local_step(x, mem, target, w, complete=None, grads_ready=None):
    nb = x.shape[0]
    h = x.reshape(nb * SEQ, D_MODEL)
    mem2 = mem.reshape(nb * MEM_LEN, D_MODEL)
    tgt = target.reshape(nb * SEQ, D_MODEL)

    h, s_ab = _ab_fwd(h, w["ab"], nb)
    if complete is not None:
        complete(h)
    h, s_xa0 = _xa_fwd(h, mem2, w["xa0"], "xa0", nb)
    h, s_mlp0 = _mlp_fwd(h, w["mlp0"], "mlp0")
    h, s_cd = _cd_fwd(h, w["cd"], nb)
    h, s_xa1 = _xa_fwd(h, mem2, w["xa1"], "xa1", nb)
    h, s_mlp1 = _mlp_fwd(h, w["mlp1"], "mlp1")

    dh, dhb, loss, dg_final = _loss_and_grad(h, tgt, w["final_norm"], name="loss")
    grads = {"final_norm": dg_final}
    dh, dhb, grads["mlp1"] = _mlp_bwd(dh, dhb, w["mlp1"], s_mlp1, "mlp1")
    dh, dhb, grads["xa1"] = _xa_bwd(dh, dhb, mem2, w["xa1"], s_xa1, "xa1", nb)
    dh, dhb, grads["cd"] = _cd_bwd(dh, dhb, w["cd"], s_cd, nb)
    if grads_ready is not None:
        grads_ready(0, grads)
    dh, dhb, grads["mlp0"] = _mlp_bwd(dh, dhb, w["mlp0"], s_mlp0, "mlp0")
    dh, dhb, grads["xa0"] = _xa_bwd(dh, dhb, mem2, w["xa0"], s_xa0, "xa0", nb)
    if grads_ready is not None:
        grads_ready(1, grads)
    dh, dhb, grads["ab"] = _ab_bwd(dh, dhb, w["ab"], s_ab, nb)
    return loss, dh.reshape(nb, SEQ, D_MODEL), grads


ANY = pl.BlockSpec(memory_space=pl.ANY)


def _place():
    x, y, c = lax.axis_index("x"), lax.axis_index("y"), lax.axis_index("c")
    return x, y, c, [(1 - x, y), (x, 1 - y), (1 - x, 1 - y)]


def _gather_chips(shard):
    half = shard.shape[0] // 2

    def body(src, out, send_sems, recv_sems):
        x, y, c, chips = _place()
        sibling = (x, y, 1 - c)

        def rows(slot, core):
            return out.at[slot, pl.ds(core * half, half)]

        def copy(k, src_ref, dst_ref, to):
            return pltpu.make_async_remote_copy(src_ref=src_ref, dst_ref=dst_ref, send_sem=send_sems.at[k],
                                                recv_sem=recv_sems.at[k], device_id=to, device_id_type=MESH)

        first = [copy(k, src.at[pl.ds(c * half, half)], rows(2 * x + y, c), (px, py, c)) for k, (px, py) in enumerate(chips)]
        for cp in first:
            cp.start()
        passed = [copy(3 + k, rows(2 * px + py, c), rows(2 * px + py, c), sibling) for k, (px, py) in enumerate(chips)]
        for k, (px, py) in enumerate(chips):
            copy(k, src.at[pl.ds(c * half, half)], rows(2 * px + py, c), (px, py, c)).wait_recv()
            passed[k].start()
        for k, (px, py) in enumerate(chips):
            copy(3 + k, rows(2 * px + py, 1 - c), rows(2 * px + py, 1 - c), sibling).wait_recv()
        for cp in first + passed:
            cp.wait_send()

    return pl.pallas_call(
        body, out_shape=jax.ShapeDtypeStruct((N_CHIPS,) + shard.shape, shard.dtype), in_specs=[ANY], out_specs=ANY,
        scratch_shapes=[pltpu.SemaphoreType.DMA((6,)), pltpu.SemaphoreType.DMA((6,))], name="gather_chips",
    )(shard)


HBM = pl.BlockSpec(memory_space=pltpu.HBM)
SEM = pl.BlockSpec(memory_space=pltpu.SEMAPHORE)
SIDE_EFFECT = pltpu.SideEffectType.DATAFLOW_SIDE_EFFECTING


def _chip_copies(src, land, send_sems, recv_sems):
    half = src.shape[0] // 2
    x, y, c, chips = _place()

    def copy(k, slot, to):
        return pltpu.make_async_remote_copy(src_ref=src.at[pl.ds(c * half, half)], dst_ref=land.at[slot, pl.ds(c * half, half)],
                                            send_sem=send_sems[k], recv_sem=recv_sems[k], device_id=to, device_id_type=MESH)
    out = [copy(k, 2 * x + y, (px, py, c)) for k, (px, py) in enumerate(chips)]
    back = [copy(k, 2 * px + py, (px, py, c)) for k, (px, py) in enumerate(chips)]
    return out, back


def _gather_start(shard):
    def body(src, land, *rest):
        sems, token = rest[:6], rest[8]
        out, _ = _chip_copies(src, land, sems[:3], sems[3:])
        for cp in out:
            cp.start()
        token[...] = jnp.zeros(token.shape, F32)

    sem = pltpu.SemaphoreType.DMA(())
    land_shape = (N_CHIPS,) + shard.shape
    return pl.pallas_call(
        body, name="gather_start",
        out_shape=(sem,) * 6 + (pltpu.HBM(shard.shape, shard.dtype), pltpu.HBM(land_shape, shard.dtype),
                                jax.ShapeDtypeStruct((8, LANES), F32)),
        in_specs=(HBM, HBM), out_specs=(SEM,) * 6 + (HBM, HBM, pl.BlockSpec(memory_space=pltpu.VMEM)),
        input_output_aliases={0: 6, 1: 7}, compiler_params=pltpu.CompilerParams(has_side_effects=SIDE_EFFECT),
    )(pltpu.with_memory_space_constraint(shard, pltpu.HBM),
      pltpu.with_memory_space_constraint(lax.empty(land_shape, shard.dtype), pltpu.HBM))


def _gather_wait(started, after):
    sems, src_thru, land_thru = started[:6], started[6], started[7]

    def body(src, land, *rest):
        out, back = _chip_copies(src, land, rest[:3], rest[3:6])
        for cp in out:
            cp.wait_send()
        for cp in back:
            cp.wait_recv()

    return pl.pallas_call(
        body, name="gather_wait",
        out_shape=(pltpu.HBM(src_thru.shape, src_thru.dtype), pltpu.HBM(land_thru.shape, land_thru.dtype)),
        in_specs=(HBM, HBM) + (SEM,) * 6 + (pl.BlockSpec(memory_space=pl.ANY),), out_specs=(HBM, HBM),
        input_output_aliases={0: 0, 1: 1}, compiler_params=pltpu.CompilerParams(has_side_effects=SIDE_EFFECT),
    )(src_thru, land_thru, *sems, after)[1]


def _gather_pass(land):
    half = land.shape[1] // 2

    def body(land_in, land_out, send_sems, recv_sems):
        x, y, c, chips = _place()

        def copy(k, slot, core):
            rows = land_out.at[slot, pl.ds(core * half, half)]
            return pltpu.make_async_remote_copy(src_ref=rows, dst_ref=rows, send_sem=send_sems.at[k], recv_sem=recv_sems.at[k],
                                                device_id=(x, y, 1 - c), device_id_type=MESH)
        sends = [copy(k, 2 * px + py, c) for k, (px, py) in enumerate(chips)]
        for cp in sends:
            cp.start()
        for k, (px, py) in enumerate(chips):
            copy(k, 2 * px + py, 1 - c).wait_recv()
        for cp in sends:
            cp.wait_send()

    return pl.pallas_call(
        body, out_shape=jax.ShapeDtypeStruct(land.shape, land.dtype), in_specs=[ANY], out_specs=ANY,
        scratch_shapes=[pltpu.SemaphoreType.DMA((3,)), pltpu.SemaphoreType.DMA((3,))], input_output_aliases={0: 0},
        name="gather_pass",
    )(land)


def _swap_cores(block, *, name, half_rows=None):
    shape = block.shape if half_rows is None else (block.shape[0], half_rows, block.shape[2])

    def body(src, out, send_sem, recv_sem):
        x, y, c, _ = _place()
        part = src if half_rows is None else src.at[:, pl.ds((1 - c) * half_rows, half_rows)]
        cp = pltpu.make_async_remote_copy(src_ref=part, dst_ref=out, send_sem=send_sem, recv_sem=recv_sem,
                                          device_id=(x, y, 1 - c), device_id_type=MESH)
        cp.start()
        cp.wait()

    return pl.pallas_call(
        body, out_shape=jax.ShapeDtypeStruct(shape, block.dtype), in_specs=[ANY], out_specs=ANY,
        scratch_shapes=[pltpu.SemaphoreType.DMA(()), pltpu.SemaphoreType.DMA(())], name=name,
    )(block)


def _scatter_copies(parts, land, send_sems, recv_sems):
    x, y, c, chips = _place()
    return [pltpu.make_async_remote_copy(src_ref=parts.at[2 * px + py], dst_ref=land.at[k], send_sem=send_sems[k],
                                         recv_sem=recv_sems[k], device_id=(px, py, c), device_id_type=MESH)
            for k, (px, py) in enumerate(chips)]


def _scatter_start(parts, tag):
    def body(src, land, *rest):
        for cp in _scatter_copies(src, land, rest[:3], rest[3:6]):
            cp.start()
        rest[8][...] = jnp.zeros(rest[8].shape, F32)

    sem = pltpu.SemaphoreType.DMA(())
    land_shape = (3,) + parts.shape[1:]
    return pl.pallas_call(
        body, name=f"scatter_start_{tag}",
        out_shape=(sem,) * 6 + (pltpu.HBM(parts.shape, parts.dtype), pltpu.HBM(land_shape, parts.dtype),
                                jax.ShapeDtypeStruct((8, LANES), F32)),
        in_specs=(HBM, HBM), out_specs=(SEM,) * 6 + (HBM, HBM, pl.BlockSpec(memory_space=pltpu.VMEM)),
        input_output_aliases={0: 6, 1: 7}, compiler_params=pltpu.CompilerParams(has_side_effects=SIDE_EFFECT),
    )(pltpu.with_memory_space_constraint(parts, pltpu.HBM),
      pltpu.with_memory_space_constraint(lax.empty(land_shape, parts.dtype), pltpu.HBM))


def _scatter_wait(started, after, tag):
    def body(src, land, *rest):
        for cp in _scatter_copies(src, land, rest[:3], rest[3:6]):
            cp.wait_send()
            cp.wait_recv()

    src_thru, land_thru = started[6], started[7]
    return pl.pallas_call(
        body, name=f"scatter_wait_{tag}",
        out_shape=(pltpu.HBM(src_thru.shape, src_thru.dtype), pltpu.HBM(land_thru.shape, land_thru.dtype)),
        in_specs=(HBM, HBM) + (SEM,) * 6 + (pl.BlockSpec(memory_space=pl.ANY),), out_specs=(HBM, HBM),
        input_output_aliases={0: 0, 1: 1}, compiler_params=pltpu.CompilerParams(has_side_effects=SIDE_EFFECT),
    )(src_thru, land_thru, *started[:6], after)[1]


def _sum_all_devices(vec):
    rows = vec.shape[0]

    def body(x_ref, total_ref, all_ref, send_sems, recv_sems, local_sem):
        x, y, c, chips = _place()
        me, sibling = (x, y, c), (x, y, 1 - c)

        def slot(px, py, pc):
            return all_ref.at[4 * px + 2 * py + pc]

        def copy(k, block, to, src=None):
            return pltpu.make_async_remote_copy(src_ref=slot(*block) if src is None else src, dst_ref=slot(*block),
                                                send_sem=send_sems.at[k], recv_sem=recv_sems.at[k], device_id=to,
                                                device_id_type=MESH)

        mine = pltpu.make_async_copy(x_ref, slot(*me), local_sem)
        mine.start()
        first = [copy(0, me, sibling, src=x_ref)] + [copy(1 + j, me, (*chip, c), src=x_ref) for j, chip in enumerate(chips)]
        for cp in first:
            cp.start()
        passed = [copy(4 + j, (*chip, c), sibling) for j, chip in enumerate(chips)]
        for j, chip in enumerate(chips):
            copy(1 + j, (*chip, c), me).wait_recv()
            passed[j].start()
        copy(0, sibling, me).wait_recv()
        for j, chip in enumerate(chips):
            copy(4 + j, (*chip, 1 - c), me).wait_recv()
        for cp in first + passed:
            cp.wait_send()
        mine.wait()
        acc = all_ref[0]
        for d in range(1, 8):
            acc = acc + all_ref[d]
        total_ref[...] = acc

    vmem = pl.BlockSpec(memory_space=pltpu.VMEM)
    return pl.pallas_call(
        body, out_shape=[jax.ShapeDtypeStruct((rows, LANES), F32), jax.ShapeDtypeStruct((8, rows, LANES), F32)],
        in_specs=[vmem], out_specs=[vmem, vmem],
        scratch_shapes=[pltpu.SemaphoreType.DMA((7,)), pltpu.SemaphoreType.DMA((7,)), pltpu.SemaphoreType.DMA(())],
        name="sum_all_devices", compiler_params=pltpu.CompilerParams(vmem_limit_bytes=VMEM_LIMIT_BYTES),
    )(vec)[0]


PACK_COLS = 1024
BIG = (("mlp_w_up", 2, 1024, True), ("mlp_w_down", 2, 1024, False), ("xa_w_kv", 2, 512, True), ("xa_w_q", 2, 256, False),
       ("xa_w_o", 2, 256, False), ("ab_w_out", 1, 256, False), ("cd_w_out", 1, 256, False), ("cd_w_in", 1, 576, True),
       ("ab_w_in", 1, 642, True))
ENTRIES = tuple((name, layer, rows, transposed) for name, layers, rows, transposed in BIG for layer in range(layers))
FIRST_ENTRIES = tuple(e for e in ENTRIES if e[0].startswith("ab_"))
LATER_ENTRIES = tuple(e for e in ENTRIES if not e[0].startswith("ab_"))


def _tile_rows(entry):
    return -(-entry[2] // 16) * 16


def _padded_rows(entries):
    return -(-sum(_tile_rows(e) for e in entries) // 32) * 32


SMALL = (("ab_norm", (1, 1024)), ("ab_conv_w", (1, 4, 512)), ("ab_conv_b", (1, 512)), ("lru_w_a", (1, 8, 64, 64)),
         ("lru_b_a", (1, 512)), ("lru_w_i", (1, 8, 64, 64)), ("lru_b_i", (1, 512)), ("lru_lambda", (1, 512)),
         ("fox_b_f", (1, 8)), ("cd_norm", (1, 1024)), ("cd_sink", (1, 8)), ("xa_norm", (2, 1024)),
         ("xa_mem_norm", (2, 1024)), ("mlp_norm", (2, 1024)), ("final_norm", (1024,)), ("loss", (1,)))
SPLIT_SMALL = ("ab_conv_w", "cd_norm")


def _pack_rows(parts, entries, dtype):
    lead = parts[0].shape[:-2]
    zeros = lambda rows: jnp.zeros(lead + (rows, PACK_COLS), dtype)
    pieces = [jnp.pad(p.astype(dtype), ((0, 0),) * len(lead) + ((0, _tile_rows(e) - e[2]), (0, 0))) for p, e in zip(parts, entries)]
    tail = _padded_rows(entries) - sum(_tile_rows(e) for e in entries)
    return jnp.concatenate(pieces + ([zeros(tail)] if tail else []), axis=len(lead))


def _unpack_rows(packed, entries):
    out, r0 = [], 0
    for e in entries:
        out.append(packed[..., r0:r0 + e[2], :])
        r0 += _tile_rows(e)
    return out


def _shard_rows(w, entries):
    return [(w[name][layer].T if transposed else w[name][layer]) for name, layer, _, transposed in entries]


def _shard_blocks(rows):
    out = {}
    for (name, layer, _, transposed), r in zip(ENTRIES, rows):
        out.setdefault(name, []).append(r.T if transposed else r)
    return {name: jnp.stack(layers) for name, layers in out.items()}


def _pack_small(vals):
    flat = jnp.concatenate([vals[name].astype(F32).reshape(-1) for name, _ in SMALL])
    size = -(-flat.shape[0] // (8 * LANES)) * (8 * LANES)
    return jnp.pad(flat, (0, size - flat.shape[0])).reshape(-1, LANES)


def _unpack_small(packed):
    flat, out, at = packed.reshape(-1), {}, 0
    for name, shape in SMALL:
        out[name] = flat[at:at + math.prod(shape)].reshape(shape)
        at += math.prod(shape)
    return out


def _chip_part(full, chip):
    width = full.shape[-1] // N_CHIPS
    return lax.dynamic_slice_in_dim(full, chip * width, width, axis=full.ndim - 1)


def _chip_embed(part, chip):
    full = jnp.zeros(part.shape[:-1] + (part.shape[-1] * N_CHIPS,), part.dtype)
    return lax.dynamic_update_slice_in_dim(full, part, chip * part.shape[-1], axis=part.ndim - 1)


SUBLAYER_KEYS = {"ab_w_in": ("ab", "w_in_t"), "ab_w_out": ("ab", "w_out"), "cd_w_in": ("cd", "w_in_t"), "cd_w_out": ("cd", "w_out"),
                 "xa_w_q": ("xa", "w_q"), "xa_w_kv": ("xa", "w_kv_t"), "xa_w_o": ("xa", "w_o"), "mlp_w_up": ("mlp", "w_up_t"),
                 "mlp_w_down": ("mlp", "w_down")}


def _sublayer(name, layer):
    block, leaf = SUBLAYER_KEYS[name]
    return (block if block in ("ab", "cd") else f"{block}{layer}"), leaf


def _set_big(params, full_rows):
    for (name, layer), rows in full_rows.items():
        block, leaf = _sublayer(name, layer)
        if name == "ab_w_in":
            rows = jnp.concatenate([rows, jnp.zeros((LANES - 8, PACK_COLS), rows.dtype)])
        params[block][leaf] = rows


def _build_params(full_rows, w):
    pad = lambda a: jnp.pad(a, ((0, 0), (0, LANES - a.shape[1])))
    params = {
        "ab": dict(norm=w["ab_norm"], conv_w=w["ab_conv_w"][0], conv_b=w["ab_conv_b"], w_a=_block_diag(w["lru_w_a"][0]),
                   b_a=w["lru_b_a"], w_i=_block_diag(w["lru_w_i"][0]), b_i=w["lru_b_i"], lam=w["lru_lambda"],
                   b_f=w["fox_b_f"].reshape(8, 1)),
        "cd": dict(norm=w["cd_norm"], sink=pad(w["cd_sink"])),
        "final_norm": w["final_norm"].reshape(1, D_MODEL),
    }
    for layer in range(2):
        params[f"xa{layer}"] = dict(norm=w["xa_norm"][layer:layer + 1], mem_norm=w["xa_mem_norm"][layer:layer + 1])
        params[f"mlp{layer}"] = dict(norm=w["mlp_norm"][layer:layer + 1])
    _set_big(params, full_rows)
    return params


def _grad_rows(g, entries=ENTRIES):
    out = []
    for name, layer, _, _ in entries:
        block, leaf = _sublayer(name, layer)
        out.append(g[block][leaf])
    return out


def _reduce_group(entry):
    name, layer = entry[0], entry[1]
    if name.startswith("ab_"):
        return 2
    return 0 if layer == 1 or name.startswith("cd_") else 1


REDUCE_GROUPS = tuple(tuple(e for e in ENTRIES if _reduce_group(e) == group) for group in range(3))


def _reduce_tile(half):
    return max(t for t in range(16, 1025, 16) if half % t == 0)


def _reduce_start(grads_by_chip, core, chip, tag):
    half = grads_by_chip.shape[1] // 2
    tile = _reduce_tile(half)
    steps = half // tile
    place = jnp.stack([core, chip]).astype(jnp.int32)
    got = _swap_cores(grads_by_chip, name=f"swap_cores_{tag}", half_rows=half)
    block = (1, tile, PACK_COLS)

    def sum_cores(place_ref, mine_ref, got_ref, f32_ref, bf16_ref):
        total = mine_ref[...].astype(F32) + got_ref[...].astype(F32)
        f32_ref[...] = total
        bf16_ref[...] = total.astype(BF16)

    pair32, pair16 = pl.pallas_call(
        sum_cores, name=f"sum_cores_{tag}",
        grid_spec=pltpu.PrefetchScalarGridSpec(
            num_scalar_prefetch=1, grid=(N_CHIPS, steps),
            in_specs=[pl.BlockSpec(block, lambda s, i, place_ref: (s, place_ref[0] * steps + i, 0)),
                      pl.BlockSpec(block, lambda s, i, place_ref: (s, i, 0))],
            out_specs=[pl.BlockSpec(block, lambda s, i, place_ref: (s, i, 0))] * 2),
        out_shape=[jax.ShapeDtypeStruct((N_CHIPS, half, PACK_COLS), F32), jax.ShapeDtypeStruct((N_CHIPS, half, PACK_COLS), BF16)],
        compiler_params=_params("arbitrary", "arbitrary"),
    )(place, grads_by_chip, got)
    return pair32, _scatter_start(pair16, tag), place


def _reduce_finish(state, core, tag, after):
    pair32, started, place = state
    half = pair32.shape[1]
    tile = _reduce_tile(half)
    landed = _scatter_wait(started, after, tag)

    def sum_chips(place_ref, own_ref, landed_ref, out_ref):
        out_ref[...] = own_ref[0] + landed_ref[0].astype(F32) + landed_ref[1].astype(F32) + landed_ref[2].astype(F32)

    done = pl.pallas_call(
        sum_chips, name=f"sum_chips_{tag}",
        grid_spec=pltpu.PrefetchScalarGridSpec(
            num_scalar_prefetch=1, grid=(half // tile,),
            in_specs=[pl.BlockSpec((1, tile, PACK_COLS), lambda i, place_ref: (place_ref[1], i, 0)),
                      pl.BlockSpec((3, tile, PACK_COLS), lambda i, place_ref: (0, i, 0))],
            out_specs=pl.BlockSpec((tile, PACK_COLS), lambda i, place_ref: (i, 0))),
        out_shape=jax.ShapeDtypeStruct((half, PACK_COLS), F32), compiler_params=_params("arbitrary"),
    )(place, pair32, landed)
    theirs = _swap_cores(done, name=f"share_halves_{tag}")
    return jnp.where(core == 0, jnp.concatenate([done, theirs]), jnp.concatenate([theirs, done]))


def kernel(x, mem, ab_norm, ab_w_in, ab_conv_w, ab_conv_b, lru_w_a, lru_b_a, lru_w_i, lru_b_i, lru_lambda, fox_b_f, ab_w_out, cd_norm, cd_w_in, cd_sink, cd_w_out, xa_norm, xa_mem_norm, xa_w_q, xa_w_kv, xa_w_o, mlp_norm, mlp_w_up, mlp_w_down, final_norm, loss_target, m_ab_norm, m_ab_w_in, m_ab_conv_w, m_ab_conv_b, m_lru_w_a, m_lru_b_a, m_lru_w_i, m_lru_b_i, m_lru_lambda, m_fox_b_f, m_ab_w_out, m_cd_norm, m_cd_w_in, m_cd_sink, m_cd_w_out, m_xa_norm, m_xa_mem_norm, m_xa_w_q, m_xa_w_kv, m_xa_w_o, m_mlp_norm, m_mlp_w_up, m_mlp_w_down, m_final_norm, v_ab_norm, v_ab_w_in, v_ab_conv_w, v_ab_conv_b, v_lru_w_a, v_lru_b_a, v_lru_w_i, v_lru_b_i, v_lru_lambda, v_fox_b_f, v_ab_w_out, v_cd_norm, v_cd_w_in, v_cd_sink, v_cd_w_out, v_xa_norm, v_xa_mem_norm, v_xa_w_q, v_xa_w_kv, v_xa_w_o, v_mlp_norm, v_mlp_w_up, v_mlp_w_down, v_final_norm):
    given = dict(locals())
    weight_names = [name for name, _ in SMALL if name != "loss"] + [name for name, _, _, _ in BIG]
    w = {name: given[name] for name in weight_names}
    chip = 2 * lax.axis_index("x") + lax.axis_index("y")
    core = lax.axis_index("c")

    slot = lax.broadcasted_iota(jnp.int32, (N_CHIPS, 1, 1), 0)

    def whole(entries, mine, gathered):
        return {(name, layer): jnp.where(slot == chip, own[None], got).reshape(N_CHIPS * rows, PACK_COLS)
                for (name, layer, rows, _), own, got in zip(entries, _unpack_rows(mine, entries), _unpack_rows(gathered, entries))}

    mine_first = _pack_rows(_shard_rows(w, FIRST_ENTRIES), FIRST_ENTRIES, BF16)
    mine_later = _pack_rows(_shard_rows(w, LATER_ENTRIES), LATER_ENTRIES, BF16)
    first = _gather_chips(mine_first)
    started = _gather_start(mine_later)
    owner = (core == 0).astype(F32)
    quarters = {name: _chip_embed(w[name], chip) * owner for name in SPLIT_SMALL}
    zeros = {name: jnp.zeros(shape, F32) for name, shape in SMALL}
    small_full = _unpack_small(_sum_all_devices(_pack_small({**zeros, **quarters})))
    params = _build_params(whole(FIRST_ENTRIES, mine_first, first),
                           {**w, "ab_conv_w": small_full["ab_conv_w"], "cd_norm": small_full["cd_norm"]})
    params["ab"]["norm"] = params["ab"]["norm"] + started[8][0, 0]

    def complete(h):
        _set_big(params, whole(LATER_ENTRIES, mine_later, _gather_pass(_gather_wait(started, after=h))))

    reducing = {}

    def grads_ready(group, g):
        entries = REDUCE_GROUPS[group]
        by_chip = _pack_rows([r.reshape(N_CHIPS, e[2], PACK_COLS) for r, e in zip(_grad_rows(g, entries), entries)], entries, BF16)
        reducing[group] = _reduce_start(by_chip, core, chip, f"g{group}")
        if group < 2:
            block, leaf = ("mlp0", "w_down") if group == 0 else ("ab", "w_out")
            params[block][leaf] = params[block][leaf] + reducing[group][1][8][0, 0].astype(BF16)

    loss_part, grad_x, g = _local_step(x, mem, loss_target, params, complete, grads_ready)
    grads_ready(2, g)
    reduced = {}
    for group, entries in enumerate(REDUCE_GROUPS):
        rows = _unpack_rows(_reduce_finish(reducing[group], core, f"g{group}", after=grad_x), entries)
        reduced.update({(e[0], e[1]): r for e, r in zip(entries, rows)})
    grads = _shard_blocks([reduced[e[0], e[1]] for e in ENTRIES])
    pair = lambda key, leaf: jnp.stack([g[f"{key}0"][leaf], g[f"{key}1"][leaf]])

    small_local = {"ab_norm": g["ab"]["norm"], "ab_conv_w": g["ab"]["conv_w"], "ab_conv_b": g["ab"]["conv_b"],
                   "lru_w_a": g["ab"]["w_a"], "lru_b_a": g["ab"]["b_a"], "lru_w_i": g["ab"]["w_i"], "lru_b_i": g["ab"]["b_i"],
                   "lru_lambda": g["ab"]["lam"], "fox_b_f": g["ab"]["b_f"], "cd_norm": g["cd"]["norm"], "cd_sink": g["cd"]["sink"],
                   "xa_norm": pair("xa", "norm"), "xa_mem_norm": pair("xa", "mem_norm"), "mlp_norm": pair("mlp", "norm"),
                   "final_norm": g["final_norm"], "loss": loss_part[0, :1]}
    small_sum_packed = _sum_all_devices(_pack_small(small_local))
    small_sum = _unpack_small(small_sum_packed)
    loss = small_sum["loss"][0]

    delta, new_m, new_v = {}, {}, {}
    for name, _, _, _ in BIG:
        shape = w[name].shape
        flat = lambda a: a.reshape(-1, shape[-1])
        d, m2, v2 = _adamw(flat(w[name]), flat(grads[name]), flat(given["m_" + name]), flat(given["v_" + name]), name=f"adamw_{name}")
        delta[name], new_m[name], new_v[name] = d.reshape(shape), m2.reshape(shape), v2.reshape(shape)

    def small_state(prefix):
        vals = {name: (given[prefix + name] if name != "loss" else jnp.zeros((1,), F32)) for name, _ in SMALL}
        for name in SPLIT_SMALL:
            vals[name] = _chip_embed(vals[name], chip)
        return _pack_small(vals)
    packed = _adamw(small_state(""), small_sum_packed, small_state("m_"), small_state("v_"), name="adamw_small")
    for store, vals in zip((delta, new_m, new_v), packed):
        for name, val in _unpack_small(vals).items():
            store[name] = _chip_part(val, chip) if name in SPLIT_SMALL else val
    for name in SPLIT_SMALL:
        small_sum[name] = _chip_part(small_sum[name], chip)
    grads.update({name: small_sum[name] for name, _ in SMALL})

    order = ["ab_norm", "ab_w_in", "ab_conv_w", "ab_conv_b", "lru_w_a", "lru_b_a", "lru_w_i", "lru_b_i", "lru_lambda", "fox_b_f",
             "ab_w_out", "cd_norm", "cd_w_in", "cd_sink", "cd_w_out", "xa_norm", "xa_mem_norm", "xa_w_q", "xa_w_kv", "xa_w_o",
             "mlp_norm", "mlp_w_up", "mlp_w_down", "final_norm"]
    return (loss, grad_x, *[grads[n] for n in order], *[delta[n] for n in order], *[new_m[n] for n in order],
            *[new_v[n] for n in order])
```

```python
import functools
import math

import jax
import jax.numpy as jnp
from jax import lax
from jax.experimental import pallas as pl
from jax.experimental.pallas import tpu as pltpu

F32 = jnp.float32
BF16 = jnp.bfloat16
MESH = pl.DeviceIdType.MESH

D_MODEL = 1024
SEQ = 2048
HEAD_DIM = 64
LRU_WIDTH = 512
LRU_BLOCKS = 8
LRU_C = 8.0
CONV_WIDTH = 4
ATT_W = 512
SWA_KW = 128
SWA_WINDOW = 128
DIL_PATTERN = ((128, 1), (512, 4), (2048, 16))
MEM_LEN = 256
XA_HEADS = 4
XA_HEAD_DIM = 256
D_FF = 4096
ROPE_THETA = 10000.0
EPS = 1e-6
N_CHIPS = 4
LANES = 128

ADAM_LR, ADAM_B1, ADAM_B2, ADAM_EPS, ADAM_WD, ADAM_STEP = 0.001, 0.9, 0.999, 1e-08, 0.01, 10

VMEM_LIMIT_BYTES = 56 * 1024 * 1024
MASKED = -1e30
ROW_TILE = 512
LRU_CHUNK = 512
ATT_BLOCK = 128
BAND_ROWS = 256
ATT_CHUNK = 512
CAUSAL_ROWS = 256


def _params(*sem):
    return pltpu.CompilerParams(dimension_semantics=sem, vmem_limit_bytes=VMEM_LIMIT_BYTES)


def _rowwise(fn, rows, consts=(), out_rows=(), out_accs=(), *, name, tile=ROW_TILE, row_maps=None):
    rows = [r if isinstance(r, tuple) else (r, r.shape[1], 0) for r in rows]
    consts = list(consts)
    nr, nc, no = len(rows), len(consts), len(out_rows)
    dealt_in = [r[3] if isinstance(r[0], str) else None for r in rows]
    dealt_out = [o[2] if len(o) == 3 else None for o in out_rows]
    total = next(r[0].shape[0] for r in rows if not isinstance(r[0], str))
    tile = min(tile, total)
    assert total % tile == 0
    n = total // tile
    n_acc = len(out_accs)

    def body(*refs):
        scratch = list(refs[nr + nc + no + n_acc:])
        vals = []
        for ref, d, row in zip(refs[:nr], dealt_in, rows):
            if d is None:
                vals.append(ref[...])
                continue
            sc, width = scratch.pop(0), row[2]
            for r in range(d):
                for c in range(width // LANES):
                    lanes = slice(r * width + c * LANES, r * width + (c + 1) * LANES)
                    sc.at[c][pl.ds(r, tile // d, stride=d), :] = ref[:, lanes].astype(F32)
            vals.append(jnp.concatenate([sc[c] for c in range(width // LANES)], axis=1))
        outs = fn(*vals, *[r[...] for r in refs[nr:nr + nc]])
        outs = tuple(outs) if isinstance(outs, (tuple, list)) else (outs,)
        for ref, val, d, spec in zip(refs[nr + nc:nr + nc + no], outs[:no], dealt_out, out_rows):
            if d is None:
                ref[...] = val.astype(ref.dtype)
                continue
            sc, width = scratch.pop(0), spec[0]
            for c in range(width // LANES):
                sc[c] = val[:, c * LANES:(c + 1) * LANES].astype(F32)
            for r in range(d):
                for c in range(width // LANES):
                    lanes = slice(r * width + c * LANES, r * width + (c + 1) * LANES)
                    ref[:, lanes] = sc.at[c][pl.ds(r, tile // d, stride=d), :].astype(ref.dtype)
        for ref, val in zip(refs[nr + nc + no:nr + nc + no + n_acc], outs[no:]):
            @pl.when(pl.program_id(0) == 0)
            def _(ref=ref):
                ref[...] = jnp.zeros(ref.shape, ref.dtype)
            ref[...] += val

    in_specs, args, scratch_shapes = [], [], []
    for idx, row in enumerate(rows):
        if isinstance(row[0], str):
            _, arr, width, d = row
            in_specs.append(pl.BlockSpec((tile // d, d * width), lambda i: (i, 0)))
            scratch_shapes.append(pltpu.VMEM((width // LANES, tile, LANES), F32))
        elif row_maps is not None and row_maps[idx] is not None:
            arr, width, _ = row
            in_specs.append(pl.BlockSpec((tile, width), row_maps[idx]))
        else:
            arr, width, cb = row
            in_specs.append(pl.BlockSpec((tile, width), functools.partial(lambda i, cb: (i, cb), cb=cb)))
        args.append(arr)
    in_specs += [pl.BlockSpec(c.shape, lambda i: (0, 0)) for c in consts]
    out_shape, out_specs = [], []
    for spec, d in zip(out_rows, dealt_out):
        w, dt = spec[0], spec[1]
        if d is None:
            out_shape.append(jax.ShapeDtypeStruct((total, w), dt))
            out_specs.append(pl.BlockSpec((tile, w), lambda i: (i, 0)))
        else:
            out_shape.append(jax.ShapeDtypeStruct((total // d, d * w), dt))
            out_specs.append(pl.BlockSpec((tile // d, d * w), lambda i: (i, 0)))
            scratch_shapes.append(pltpu.VMEM((w // LANES, tile, LANES), F32))
    out_shape += [jax.ShapeDtypeStruct(s, F32) for s in out_accs]
    out_specs += [pl.BlockSpec(s, lambda i: (0, 0)) for s in out_accs]
    return pl.pallas_call(
        body, grid=(n,), in_specs=in_specs, out_specs=out_specs, out_shape=out_shape, scratch_shapes=scratch_shapes, name=name,
        compiler_params=_params("arbitrary"),
    )(*args, *consts)


MATMUL_VMEM_BYTES = 40 * 1024 * 1024


def _matmul_tiles(m, n, k, tm, tn, out_bytes):
    tm, tn, tk = min(tm, m), min(tn, n), k

    def need():
        return 2 * (2 * tk * (tm + tn) + tm * tn * out_bytes) + (0 if tk == k else 4 * tm * tn)

    while need() > MATMUL_VMEM_BYTES:
        if tm >= tn and tm % 256 == 0:
            tm //= 2
        elif tn % 256 == 0:
            tn //= 2
        else:
            tk //= 2
    return tm, tn, tk


def _matmul(a, b, *, ta=False, tb=False, outs=(F32,), epilogue=None, extras=(), consts=(), sums=(), whole_rows=False,
            tm=1024, tn=1024, name):
    m, k = (a.shape[1], a.shape[0]) if ta else a.shape
    n = b.shape[0] if tb else b.shape[1]
    assert (b.shape[1] if tb else b.shape[0]) == k
    outs = [o if isinstance(o, tuple) else (o, None) for o in outs]
    out_bytes = sum(jnp.dtype(dt).itemsize for dt, w in outs if w is None) + sum(e.dtype.itemsize for e in extras)
    tm, tn, tk = _matmul_tiles(m, n, k, tm, tn, out_bytes)
    assert m % tm == 0 and n % tn == 0 and k % tk == 0, (name, m, n, k)
    nk = k // tk
    ne, nc, no = len(extras), len(consts), len(outs)
    assert tn == n or not (sums or whole_rows or any(w is not None for _, w in outs)), name
    dims = (((0 if ta else 1,), (1 if tb else 0,)), ((), ()))

    def body(*refs):
        a_ref, b_ref = refs[:2]
        e_refs, o_refs = refs[2:2 + ne + nc], refs[2 + ne + nc:2 + ne + nc + no]
        s_refs = refs[2 + ne + nc + no:2 + ne + nc + no + len(sums)]

        def finish(acc):
            vals = (acc,) if epilogue is None else epilogue(acc, *[e[...] for e in e_refs])
            for ref, val in zip(o_refs, vals[:no]):
                ref[...] = val.astype(ref.dtype)
            for ref, val in zip(s_refs, vals[no:]):
                @pl.when(pl.program_id(0) == 0)
                def _(ref=ref, val=val):
                    ref[...] = val

                @pl.when(pl.program_id(0) > 0)
                def _(ref=ref, val=val):
                    ref[...] += val

        prod = lax.dot_general(a_ref[...], b_ref[...], dims, preferred_element_type=F32)
        if nk == 1:
            finish(prod)
        else:
            acc_ref = refs[-1]
            step = pl.program_id(2)

            @pl.when(step == 0)
            def _():
                acc_ref[...] = prod

            @pl.when(step > 0)
            def _():
                acc_ref[...] += prod

            @pl.when(step == nk - 1)
            def _():
                finish(acc_ref[...])

    a_spec = pl.BlockSpec((tk, tm), lambda i, j, s: (s, i)) if ta else pl.BlockSpec((tm, tk), lambda i, j, s: (i, s))
    b_spec = pl.BlockSpec((tn, tk), lambda i, j, s: (j, s)) if tb else pl.BlockSpec((tk, tn), lambda i, j, s: (s, j))
    tile_spec = pl.BlockSpec((tm, tn), lambda i, j, s: (i, j))
    whole = lambda shape: pl.BlockSpec(shape, lambda i, j, s: (0, 0))
    return pl.pallas_call(
        body, grid=(m // tm, n // tn, nk),
        in_specs=[a_spec, b_spec] + [tile_spec] * ne + [whole(c.shape) for c in consts],
        out_specs=[tile_spec if w is None else pl.BlockSpec((tm, w), lambda i, j, s: (i, 0)) for _, w in outs]
        + [whole(shape) for shape in sums],
        out_shape=[jax.ShapeDtypeStruct((m, n if w is None else w), dt) for dt, w in outs]
        + [jax.ShapeDtypeStruct(shape, F32) for shape in sums],
        scratch_shapes=[pltpu.VMEM((tm, tn), F32)] if nk > 1 else [],
        name=name, compiler_params=_params("arbitrary" if sums else "parallel", "parallel", "arbitrary"),
    )(a, b, *extras, *consts)


def _split3(x):
    x1 = x.astype(BF16)
    r1 = x - x1.astype(F32)
    x2 = r1.astype(BF16)
    x3 = (r1 - x2.astype(F32)).astype(BF16)
    return x1, x2, x3


def _dot_exact(x, e):
    return sum(jnp.dot(p, e, preferred_element_type=F32) for p in _split3(x))


def _head_expand(heads, width):
    dh = width // heads
    rows = jnp.arange(LANES)[:, None]
    cols = jnp.arange(width)[None, :]
    return (cols // dh == rows).astype(BF16)


def _rstd(x):
    return lax.rsqrt(jnp.mean(x * x, axis=-1, keepdims=True) + EPS)


def _rms_fwd(x, gain, *, name):
    return _rowwise(lambda x, g: x * _rstd(x) * g, [x], [gain], [(x.shape[1], BF16)], name=name)[0]


def _rms_bwd_vals(x, dhn, gain):
    r = _rstd(x)
    xh = x * r
    dxh = dhn * gain
    dx = r * (dxh - xh * jnp.mean(dxh * xh, axis=-1, keepdims=True))
    return dx, jnp.sum(dhn * xh, axis=0, keepdims=True)


def _norm_input_grad(dz, w, x, dres, gain, *, tb=False, name):
    def epilogue(dhn, x, dres, g):
        dx, dg = _rms_bwd_vals(x, dhn, g)
        dh = dres + dx
        return dh, dh, dg
    return _matmul(dz, w, tb=tb, outs=(F32, BF16), extras=(x, dres), consts=(gain,), sums=((1, x.shape[1]),), epilogue=epilogue,
                   name=name)


def _loss_and_grad(h, target, gain, *, name):
    def fn(h, tgt, g):
        r = _rstd(h)
        err = h * r * g - tgt
        loss = 0.5 * jnp.sum(jnp.mean(err * err, axis=-1, keepdims=True), axis=0, keepdims=True)
        dx, dg = _rms_bwd_vals(h, err * (1.0 / D_MODEL), g)
        return dx, dx, jnp.broadcast_to(loss, (1, LANES)), dg
    d = h.shape[1]
    return _rowwise(fn, [h, target], [gain], [(d, F32), (d, BF16)], [(1, LANES), (1, d)], name=name)


def _adamw(w, g, m, v, *, name):
    rows, cols = w.shape
    tile = rows
    for cand in (256, 128, 64, 32, 16, 8):
        if rows % cand == 0 and rows > cand:
            tile = cand
            break
    bc1 = 1.0 - ADAM_B1 ** ADAM_STEP
    bc2 = 1.0 - ADAM_B2 ** ADAM_STEP

    def fn(w, g, m, v):
        m2 = ADAM_B1 * m + (1.0 - ADAM_B1) * g
        v2 = ADAM_B2 * v + (1.0 - ADAM_B2) * (g * g)
        delta = -ADAM_LR * ((m2 / bc1) / (jnp.sqrt(v2 / bc2) + ADAM_EPS) + ADAM_WD * w)
        return delta, m2, v2
    return _rowwise(fn, [w, g, m, v], [], [(cols, F32)] * 3, name=name, tile=tile)


def _attn_specs(arr, width, cb, classes, rows_block, whole):
    ncols = arr.shape[2] // (classes * width)
    if whole:
        return pl.BlockSpec((1, arr.shape[1], width), lambda n, r, i: (n, 0, r * ncols + cb))
    return pl.BlockSpec((1, rows_block, width), lambda n, r, i: (n, i, r * ncols + cb))


def _attn_tiles(mode, lq, lk):
    if mode == "full":
        return min(lq, 256), lk
    if mode == "band":
        tq = min(BAND_ROWS, lq)
        return tq, min(tq + ATT_BLOCK, lk)
    return CAUSAL_ROWS, ATT_CHUNK


def _window(mode, i, j, tq, win, lk):
    if mode == "causal":
        return pl.multiple_of(j * win, win), (i * tq) // win + 1
    if mode == "band":
        return pl.multiple_of(jnp.clip(i * tq + tq - win, 0, lk - win), ATT_BLOCK), 1
    return 0, 1


def _allowed(mode, i, start, tq, win, max_dist):
    if mode == "full":
        return None
    dist = (i * tq + lax.broadcasted_iota(jnp.int32, (tq, win), 0)) - (start + lax.broadcasted_iota(jnp.int32, (tq, win), 1))
    ok = dist >= 0
    if max_dist is not None:
        ok = ok & (dist <= max_dist)
    return ok


def _head_groups(heads, dh):
    gw = max(LANES, dh)
    return gw, gw // dh, heads // (gw // dh)


def _own_lanes(rows, gw, dh, u):
    return lax.broadcasted_iota(jnp.int32, (rows, gw), 1) // dh == u


def _only_head(x, own, per, u):
    return x if per == 1 else jnp.where(own[u], x, jnp.zeros_like(x))


def _step_scores(qz, kw, kb_row, ok, scale):
    s = lax.dot_general(qz, kw, (((1,), (1,)), ((), ())), preferred_element_type=F32) * scale
    if kb_row is not None:
        s = s - kb_row
    if ok is not None:
        s = jnp.where(ok, s, MASKED)
    return s


def _attn_fwd(q, k, v, kb=None, *, classes=1, heads, dh, mode, max_dist=None, name):
    (qa, qc), (ka, kc), (va, vc) = q, k, v
    n, lq, lk = qa.shape[0], qa.shape[1], ka.shape[1]
    width = heads * dh
    tq, win = _attn_tiles(mode, lq, lk)
    gw, per, groups = _head_groups(heads, dh)
    scale = dh ** -0.5
    has_kb = kb is not None
    single = mode != "causal"

    def body(*refs):
        q_ref, k_ref, v_ref = refs[:3]
        kb_ref = refs[3] if has_kb else None
        o_ref, lse_ref = refs[-2:]
        i = pl.program_id(2)
        lane = lax.broadcasted_iota(jnp.int32, (tq, LANES), 1)
        own = [_own_lanes(tq, gw, dh, u) for u in range(per)]

        def step(j, carry):
            m_in, l_in = carry
            m_out, l_out = m_in, l_in
            start, _ = _window(mode, i, j, tq, win, lk)
            ok = _allowed(mode, i, start, tq, win, max_dist)
            for g in range(groups):
                cols = slice(g * gw, (g + 1) * gw)
                qg = q_ref[0, :, cols]
                kw = k_ref[0, pl.ds(start, win), cols]
                vw = v_ref[0, pl.ds(start, win), cols]
                alpha_g = new_g = None
                for u in range(per):
                    h = g * per + u
                    kb_row = kb_ref[0, h, pl.ds(j, 1), :] if has_kb else None
                    s = _step_scores(_only_head(qg, own, per, u), kw, kb_row, ok, scale)
                    m_new = jnp.max(s, axis=1, keepdims=True)
                    if not single:
                        m_old = m_in[:, h:h + 1]
                        m_new = jnp.maximum(m_old, m_new)
                        alpha = jnp.exp(m_old - m_new)
                        alpha_g = alpha if u == 0 else jnp.where(own[u], alpha, alpha_g)
                    p = jnp.exp(s - m_new)
                    l_new = jnp.sum(p, axis=1, keepdims=True)
                    r = jnp.dot(p.astype(BF16), vw, preferred_element_type=F32)
                    if single:
                        r = r * (1.0 / l_new)
                    else:
                        l_new = alpha * l_in[:, h:h + 1] + l_new
                    new_g = r if u == 0 else jnp.where(own[u], r, new_g)
                    m_out = jnp.where(lane == h, m_new, m_out)
                    l_out = jnp.where(lane == h, l_new, l_out)
                o_ref[0, :, cols] = new_g if single else alpha_g * o_ref[0, :, cols] + new_g
            return m_out, l_out

        carry = (jnp.full((tq, LANES), MASKED, F32), jnp.zeros((tq, LANES), F32))
        if single:
            m_all, l_all = step(0, carry)
            l_all = jnp.where(lane < heads, l_all, 1.0)
        else:
            o_ref[...] = jnp.zeros(o_ref.shape, F32)
            m_all, l_all = lax.fori_loop(0, _window(mode, i, 0, tq, win, lk)[1], step, carry)
            l_all = jnp.where(lane < heads, l_all, 1.0)
            inv = 1.0 / l_all
            for g in range(groups):
                cols = slice(g * gw, (g + 1) * gw)
                inv_g = inv[:, g * per:g * per + 1]
                for u in range(1, per):
                    inv_g = jnp.where(own[u], inv[:, g * per + u:g * per + u + 1], inv_g)
                o_ref[0, :, cols] = o_ref[0, :, cols] * inv_g
        lse_ref[0] = jnp.where(lane < heads, m_all + jnp.log(l_all), 0.0)

    in_specs = [_attn_specs(qa, width, qc, classes, tq, False), _attn_specs(ka, width, kc, classes, win, True),
                _attn_specs(va, width, vc, classes, win, True)]
    args = [qa, ka, va]
    if has_kb:
        in_specs.append(pl.BlockSpec((1,) + kb.shape[1:], lambda n_, r, i: (n_, 0, 0, 0)))
        args.append(kb)
    out_shape = [jax.ShapeDtypeStruct((n, lq, classes * width), F32), jax.ShapeDtypeStruct((n, lq, classes * LANES), F32)]
    out_specs = [pl.BlockSpec((1, tq, width), lambda n_, r, i: (n_, i, r)), pl.BlockSpec((1, tq, LANES), lambda n_, r, i: (n_, i, r))]
    return pl.pallas_call(
        body, grid=(n, classes, lq // tq), in_specs=in_specs, out_specs=out_specs, out_shape=out_shape, name=name,
        compiler_params=_params("parallel", "parallel", "arbitrary"),
    )(*args)


def _attn_bwd(q, k, v, do, lse, delta, kb=None, *, classes=1, heads, dh, mode, max_dist=None, dq_dtype=F32, name):
    (qa, qc), (ka, kc), (va, vc), (da, dc) = q, k, v, do
    n, lq, lk = qa.shape[0], qa.shape[1], ka.shape[1]
    width = heads * dh
    tq, win = _attn_tiles(mode, lq, lk)
    gw, per, groups = _head_groups(heads, dh)
    scale = dh ** -0.5
    has_kb = kb is not None
    single = mode != "causal"
    nt = (((1,), (1,)), ((), ()))
    tn = (((0,), (0,)), ((), ()))

    def body(*refs):
        q_ref, k_ref, v_ref, do_ref, lse_ref, dl_ref = refs[:6]
        kb_ref = refs[6] if has_kb else None
        n_in = 7 if has_kb else 6
        dq_ref, dk_ref, dv_ref = refs[n_in:n_in + 3]
        dkb_ref, drow_ref = refs[n_in + 3:n_in + 5] if has_kb else (None, None)
        i = pl.program_id(2)

        @pl.when(i == 0)
        def _():
            dk_ref[...] = jnp.zeros(dk_ref.shape, F32)
            dv_ref[...] = jnp.zeros(dv_ref.shape, F32)
            if has_kb:
                dkb_ref[...] = jnp.zeros(dkb_ref.shape, F32)

        lse_all = lse_ref[0]
        dl_all = dl_ref[0]
        lane = lax.broadcasted_iota(jnp.int32, (tq, LANES), 1)
        own = [_own_lanes(tq, gw, dh, u) for u in range(per)]

        def step(j, drow_all):
            start, _ = _window(mode, i, j, tq, win, lk)
            ok = _allowed(mode, i, start, tq, win, max_dist)
            for g in range(groups):
                cols = slice(g * gw, (g + 1) * gw)
                qg = q_ref[0, :, cols]
                dog = do_ref[0, :, cols]
                kw = k_ref[0, pl.ds(start, win), cols]
                vw = v_ref[0, pl.ds(start, win), cols]
                dq_g = dk_w = dv_w = None
                for u in range(per):
                    h = g * per + u
                    qz, doz = _only_head(qg, own, per, u), _only_head(dog, own, per, u)
                    kb_row = kb_ref[0, h, pl.ds(j, 1), :] if has_kb else None
                    p = jnp.exp(_step_scores(qz, kw, kb_row, ok, scale) - lse_all[:, h:h + 1])
                    dp = lax.dot_general(doz, vw, nt, preferred_element_type=F32)
                    ds = p * (dp - dl_all[:, h:h + 1])
                    dsb = ds.astype(BF16)
                    r = jnp.dot(dsb, kw, preferred_element_type=F32)
                    dq_g = r if u == 0 else jnp.where(own[u], r, dq_g)
                    dk_u = lax.dot_general(dsb, qz, tn, preferred_element_type=F32)
                    dv_u = lax.dot_general(p.astype(BF16), doz, tn, preferred_element_type=F32)
                    dk_w = dk_u if u == 0 else dk_w + dk_u
                    dv_w = dv_u if u == 0 else dv_w + dv_u
                    if has_kb:
                        dkb_ref[0, h, pl.ds(j, 1), :] += -jnp.sum(ds, axis=0, keepdims=True)
                        drow_all = drow_all + jnp.where(lane == h, jnp.sum(ds, axis=1, keepdims=True), 0.0)
                dk_ref[0, pl.ds(start, win), cols] += dk_w * scale
                dv_ref[0, pl.ds(start, win), cols] += dv_w
                if single:
                    dq_ref[0, :, cols] = (dq_g * scale).astype(dq_ref.dtype)
                else:
                    dq_ref[0, :, cols] += dq_g * scale
            return drow_all

        drow_all = jnp.zeros((tq, LANES), F32)
        if single:
            drow_all = step(0, drow_all)
        else:
            dq_ref[...] = jnp.zeros(dq_ref.shape, F32)
            drow_all = lax.fori_loop(0, _window(mode, i, 0, tq, win, lk)[1], step, drow_all)
        if has_kb:
            drow_ref[0] = drow_all

    row_spec = functools.partial(_attn_specs, classes=classes, rows_block=tq, whole=False)
    in_specs = [row_spec(qa, width, qc), _attn_specs(ka, width, kc, classes, win, True), _attn_specs(va, width, vc, classes, win, True),
                row_spec(da, width, dc), row_spec(lse, LANES, 0), row_spec(delta, LANES, 0)]
    args = [qa, ka, va, da, lse, delta]
    assert single or dq_dtype == F32
    out_shape = [jax.ShapeDtypeStruct((n, lq, classes * width), dq_dtype), jax.ShapeDtypeStruct((n, lk, classes * width), F32),
                 jax.ShapeDtypeStruct((n, lk, classes * width), F32)]
    kv_spec = pl.BlockSpec((1, lk, width), lambda n_, r, i: (n_, 0, r))
    out_specs = [pl.BlockSpec((1, tq, width), lambda n_, r, i: (n_, i, r)), kv_spec, kv_spec]
    if has_kb:
        kb_spec = pl.BlockSpec((1,) + kb.shape[1:], lambda n_, r, i: (n_, 0, 0, 0))
        in_specs.append(kb_spec)
        args.append(kb)
        out_shape += [jax.ShapeDtypeStruct(kb.shape, F32), jax.ShapeDtypeStruct((n, lq, LANES), F32)]
        out_specs += [kb_spec, pl.BlockSpec((1, tq, LANES), lambda n_, r, i: (n_, i, 0))]
    return pl.pallas_call(
        body, grid=(n, classes, lq // tq), in_specs=in_specs, out_specs=out_specs, out_shape=out_shape, name=name,
        compiler_params=_params("parallel", "parallel", "arbitrary"),
    )(*args)


def _head_delta(do, out, heads, *, name, lse=None, sink=None):
    width = out.shape[1]
    gather = _head_expand(heads, width).T

    if sink is None:
        return _rowwise(lambda do, o, e: _dot_exact(do * o, e), [do, out], [gather], [(LANES, F32)], name=name)[0]

    def fn(do, o, lse, e, sink):
        delta = _dot_exact(do * o, e)
        return delta, -jnp.sum(jnp.exp(sink - lse) * delta, axis=0, keepdims=True)
    return _rowwise(fn, [do, out, lse], [gather, sink], [(LANES, F32)], [(1, LANES)], name=name)


def _rope_tables():
    half = HEAD_DIM // 2
    inv = ROPE_THETA ** (-jnp.arange(half, dtype=F32) / half)
    ang = jnp.arange(SEQ, dtype=F32)[:, None] * inv[None, :]
    cos = jnp.tile(jnp.cos(ang), (1, 2 * ATT_W // HEAD_DIM))
    sin = jnp.tile(jnp.concatenate([-jnp.sin(ang), jnp.sin(ang)], axis=1), (1, ATT_W // HEAD_DIM))
    return cos, sin


def _rotate(x, cos, sin):
    width = x.shape[1]
    lane = lax.broadcasted_iota(jnp.int32, x.shape, 1)
    first_half = (lane % HEAD_DIM) < (HEAD_DIM // 2)
    swapped = jnp.where(first_half, pltpu.roll(x, width - HEAD_DIM // 2, axis=1), pltpu.roll(x, HEAD_DIM // 2, axis=1))
    return x * cos[:, :width] + swapped * sin[:, :width]


def _gelu(x):
    k = math.sqrt(2.0 / math.pi)
    t = jnp.tanh(k * (x + 0.044715 * x * x * x))
    return 0.5 * x * (1.0 + t), t


def _gelu_grad(x, t):
    k = math.sqrt(2.0 / math.pi)
    return 0.5 * (1.0 + t) + 0.5 * x * (1.0 - t * t) * k * (1.0 + 3.0 * 0.044715 * x * x)


def _shift_down(x, halo, s):
    ext = jnp.concatenate([halo, x], axis=0)
    return pltpu.roll(ext, s, axis=0)[8:, :]


def _shift_up(x, halo, s):
    rows = x.shape[0]
    ext = jnp.concatenate([x, halo], axis=0)
    return pltpu.roll(ext, rows + 8 - s, axis=0)[:rows, :]


def _lru_gates(u, halo, cw, cb, wa, ba, wi, bi, lam):
    taps = [_shift_down(u, halo, CONV_WIDTH - 1 - k) for k in range(CONV_WIDTH - 1)] + [u]
    uc = cb + sum(cw[k:k + 1, :] * taps[k] for k in range(CONV_WIDTH))
    ucb = uc.astype(BF16)
    r = jax.nn.sigmoid(jnp.dot(ucb, wa, preferred_element_type=F32) + ba)
    gi = jax.nn.sigmoid(jnp.dot(ucb, wi, preferred_element_type=F32) + bi)
    sp = jnp.maximum(-lam, 0.0) + jnp.log(1.0 + jnp.exp(-jnp.abs(lam)))
    log_a = -LRU_C * r * sp
    a = jnp.exp(log_a)
    x2 = 2.0 * log_a
    one_minus_a2 = jnp.where(x2 > -0.01, -x2 * (1.0 + x2 * (0.5 + x2 * (1.0 / 6.0 + x2 / 24.0))), 1.0 - jnp.exp(x2))
    mult = jnp.sqrt(one_minus_a2)
    return taps, uc, ucb, r, gi, sp, a, mult


def _lru_specs(nchunk, reverse):
    def chunk(b, c):
        return b * nchunk + ((nchunk - 1 - c) if reverse else c)

    def halo_before(b, c):
        return jnp.maximum(chunk(b, c) * (LRU_CHUNK // 8) - 1, 0)

    return chunk, halo_before


def _lru_fwd(zug, cw, cb, wa, ba, wi, bi, lam, *, name):
    total = zug.shape[0]
    nchunk = SEQ // LRU_CHUNK
    w = LRU_WIDTH
    chunk, halo_before = _lru_specs(nchunk, False)

    def body(u_ref, uh_ref, g_ref, cw_ref, cb_ref, wa_ref, ba_ref, wi_ref, bi_ref, lam_ref, y_ref, h_ref, a_sc, x_sc, carry_sc):
        c = pl.program_id(1)
        u = u_ref[...]
        halo = jnp.where(c > 0, uh_ref[...], 0.0)
        _, uc, _, _, gi, _, a, mult = _lru_gates(u, halo, cw_ref[...], cb_ref[...], wa_ref[...], ba_ref[...],
                                                 wi_ref[...], bi_ref[...], lam_ref[...])
        a_sc[...] = a
        x_sc[...] = mult * (gi * uc)

        @pl.when(c == 0)
        def _():
            carry_sc[...] = jnp.zeros(carry_sc.shape, F32)

        def tile_step(t, h):
            r0 = pl.multiple_of(t * 8, 8)
            at = a_sc[pl.ds(r0, 8), :]
            xt = x_sc[pl.ds(r0, 8), :]
            rows = []
            for j in range(8):
                h = at[j:j + 1, :] * h + xt[j:j + 1, :]
                rows.append(h)
            h_ref[pl.ds(r0, 8), :] = jnp.concatenate(rows, axis=0)
            return h

        h_last = lax.fori_loop(0, LRU_CHUNK // 8, tile_step, carry_sc[0:1, :])
        carry_sc[0:1, :] = h_last
        gel, _ = _gelu(g_ref[...])
        y_ref[...] = (h_ref[...] * gel).astype(BF16)

    small = lambda arr: pl.BlockSpec(arr.shape, lambda b, c: (0, 0))
    return pl.pallas_call(
        body, grid=(total // SEQ, nchunk),
        in_specs=[pl.BlockSpec((LRU_CHUNK, w), lambda b, c: (chunk(b, c), 0)),
                  pl.BlockSpec((8, w), lambda b, c: (halo_before(b, c), 0)),
                  pl.BlockSpec((LRU_CHUNK, w), lambda b, c: (chunk(b, c), 1)),
                  small(cw), small(cb), small(wa), small(ba), small(wi), small(bi), small(lam)],
        out_specs=[pl.BlockSpec((LRU_CHUNK, w), lambda b, c: (chunk(b, c), 0))] * 2,
        out_shape=[jax.ShapeDtypeStruct((total, w), BF16), jax.ShapeDtypeStruct((total, w), F32)],
        scratch_shapes=[pltpu.VMEM((LRU_CHUNK, w), F32), pltpu.VMEM((LRU_CHUNK, w), F32), pltpu.VMEM((8, w), F32)],
        name=name, compiler_params=_params("arbitrary", "arbitrary"),
    )(zug, zug, zug, cw, cb, wa, ba, wi, bi, lam)


def _lru_bwd(zug, hl, dy, cw, cb, wa, ba, wi, bi, lam, *, name):
    total = zug.shape[0]
    nchunk = SEQ // LRU_CHUNK
    w = LRU_WIDTH
    chunk, halo_before = _lru_specs(nchunk, True)
    tn = (((0,), (0,)), ((), ()))
    nt = (((1,), (1,)), ((), ()))

    def body(u_ref, uh_ref, g_ref, hl_ref, hh_ref, dy_ref, cw_ref, cb_ref, wa_ref, ba_ref, wi_ref, bi_ref, lam_ref,
             dz_ref, dcw_ref, dcb_ref, dwa_ref, dba_ref, dwi_ref, dbi_ref, dlam_ref,
             a_sc, d_sc, g_sc, gcarry_sc, acarry_sc, duc_sc):
        b, c = pl.program_id(0), pl.program_id(1)
        first_chunk = c == nchunk - 1

        @pl.when((b == 0) & (c == 0))
        def _():
            for ref in (dcw_ref, dcb_ref, dwa_ref, dba_ref, dwi_ref, dbi_ref, dlam_ref):
                ref[...] = jnp.zeros(ref.shape, F32)

        @pl.when(c == 0)
        def _():
            gcarry_sc[...] = jnp.zeros(gcarry_sc.shape, F32)
            acarry_sc[...] = jnp.zeros(acarry_sc.shape, F32)
            duc_sc[...] = jnp.zeros(duc_sc.shape, F32)

        u = u_ref[...]
        halo = jnp.where(first_chunk, 0.0, uh_ref[...])
        cw, wa, wi, lam = cw_ref[...], wa_ref[...], wi_ref[...], lam_ref[...]
        taps, uc, ucb, r, gi, sp, a, mult = _lru_gates(u, halo, cw, cb_ref[...], wa, ba_ref[...], wi, bi_ref[...], lam)
        gate = g_ref[...]
        gel, th = _gelu(gate)
        dy = dy_ref[...]
        hl = hl_ref[...]
        a_sc[...] = a
        d_sc[...] = dy * gel
        dgate = dy * hl * _gelu_grad(gate, th)

        def tile_step(t, carry):
            g_next, a_next = carry
            r0 = pl.multiple_of((LRU_CHUNK // 8 - 1 - t) * 8, 8)
            dt = d_sc[pl.ds(r0, 8), :]
            at = a_sc[pl.ds(r0, 8), :]
            rows = [None] * 8
            for j in reversed(range(8)):
                g_next = dt[j:j + 1, :] + a_next * g_next
                a_next = at[j:j + 1, :]
                rows[j] = g_next
            g_sc[pl.ds(r0, 8), :] = jnp.concatenate(rows, axis=0)
            return g_next, a_next

        g_last, a_last = lax.fori_loop(0, LRU_CHUNK // 8, tile_step, (gcarry_sc[0:1, :], acarry_sc[0:1, :]))
        gcarry_sc[0:1, :] = g_last
        acarry_sc[0:1, :] = a_last

        gs = g_sc[...]
        hl_halo = jnp.where(first_chunk, 0.0, hh_ref[...])
        da = gs * _shift_down(hl, hl_halo, 1)
        dmult = gs * gi * uc
        dgi = gs * mult * uc
        duc = gs * mult * gi
        dlog_a = da * a - dmult * a * a / mult
        dr = dlog_a * (-LRU_C * sp)
        dsp = jnp.sum(dlog_a * (-LRU_C * r), axis=0, keepdims=True)
        dlam_ref[...] += -dsp * jax.nn.sigmoid(-lam)
        dpa = dr * r * (1.0 - r)
        dpi = dgi * gi * (1.0 - gi)
        dpab, dpib = dpa.astype(BF16), dpi.astype(BF16)
        dba_ref[...] += jnp.sum(dpa, axis=0, keepdims=True)
        dbi_ref[...] += jnp.sum(dpi, axis=0, keepdims=True)
        dwa_ref[...] += lax.dot_general(ucb, dpab, tn, preferred_element_type=F32)
        dwi_ref[...] += lax.dot_general(ucb, dpib, tn, preferred_element_type=F32)
        duc = duc + lax.dot_general(dpab, wa, nt, preferred_element_type=F32)
        duc = duc + lax.dot_general(dpib, wi, nt, preferred_element_type=F32)
        dcb_ref[...] += jnp.sum(duc, axis=0, keepdims=True)
        dcw_ref[...] += jnp.concatenate([jnp.sum(duc * taps[k], axis=0, keepdims=True) for k in range(CONV_WIDTH)], axis=0)
        after = duc_sc[...]
        du = cw[CONV_WIDTH - 1:CONV_WIDTH, :] * duc
        for k in range(CONV_WIDTH - 1):
            du = du + cw[k:k + 1, :] * _shift_up(duc, after, CONV_WIDTH - 1 - k)
        duc_sc[...] = duc[0:8, :]
        dz_ref[:, 0:w] = du.astype(BF16)
        dz_ref[:, w:2 * w] = dgate.astype(BF16)

    small = lambda arr: pl.BlockSpec(arr.shape, lambda b, c: (0, 0))
    acc = lambda shape: pl.BlockSpec(shape, lambda b, c: (0, 0))
    chunk_spec = lambda cb_: pl.BlockSpec((LRU_CHUNK, w), lambda b, c: (chunk(b, c), cb_))
    halo_spec = pl.BlockSpec((8, w), lambda b, c: (halo_before(b, c), 0))
    acc_shapes = [(CONV_WIDTH, w), (1, w), (w, w), (1, w), (w, w), (1, w), (1, w)]
    return pl.pallas_call(
        body, grid=(total // SEQ, nchunk),
        in_specs=[chunk_spec(0), halo_spec, chunk_spec(1), chunk_spec(0), halo_spec, chunk_spec(0),
                  small(cw), small(cb), small(wa), small(ba), small(wi), small(bi), small(lam)],
        out_specs=[pl.BlockSpec((LRU_CHUNK, 2 * w), lambda b, c: (chunk(b, c), 0))] + [acc(s) for s in acc_shapes],
        out_shape=[jax.ShapeDtypeStruct((total, 2 * w), BF16)] + [jax.ShapeDtypeStruct(s, F32) for s in acc_shapes],
        scratch_shapes=[pltpu.VMEM((LRU_CHUNK, w), F32)] * 3 + [pltpu.VMEM((8, w), F32)] * 3,
        name=name, compiler_params=_params("arbitrary", "arbitrary"),
    )(zug, zug, zug, hl, hl, dy, cw, cb, wa, ba, wi, bi, lam)


def _block_diag(wb):
    eye = jnp.eye(LRU_BLOCKS, dtype=wb.dtype)
    return jnp.einsum("gcd,gh->gchd", wb, eye).reshape(LRU_WIDTH, LRU_WIDTH).astype(BF16)


def _diag_blocks(wd):
    blk = LRU_WIDTH // LRU_BLOCKS
    return jnp.stack([wd[g * blk:(g + 1) * blk, g * blk:(g + 1) * blk] for g in range(LRU_BLOCKS)])


FOX_SCAN = 256


def _fox_bias(fl, bf, *, name):
    nb = fl.shape[0]

    def body(fl_ref, bf_ref, c_ref):
        x = fl_ref[0] + bf_ref[...]
        logf = jnp.minimum(x, 0.0) - jnp.log(1.0 + jnp.exp(-jnp.abs(x)))
        upper = (lax.broadcasted_iota(jnp.int32, (FOX_SCAN, FOX_SCAN), 0)
                 <= lax.broadcasted_iota(jnp.int32, (FOX_SCAN, FOX_SCAN), 1)).astype(BF16)
        carry = jnp.zeros((LRU_BLOCKS, 1), F32)
        for j in range(SEQ // FOX_SCAN):
            blk = logf[:, j * FOX_SCAN:(j + 1) * FOX_SCAN]
            c_ref[0, :, j * FOX_SCAN:(j + 1) * FOX_SCAN] = _dot_exact(blk, upper) + carry
            carry = carry + jnp.sum(blk, axis=1, keepdims=True)

    return pl.pallas_call(
        body, grid=(nb,),
        in_specs=[pl.BlockSpec((1, 8, SEQ), lambda b: (b, 0, 0)), pl.BlockSpec((8, 1), lambda b: (0, 0))],
        out_specs=pl.BlockSpec((1, 8, SEQ), lambda b: (b, 0, 0)),
        out_shape=jax.ShapeDtypeStruct((nb, 8, SEQ), F32), name=name, compiler_params=_params("arbitrary"),
    )(fl, bf)


def _fox_bias_bwd(fl, bf, dc, *, name):
    nb = fl.shape[0]

    def body(fl_ref, bf_ref, dc_ref, dfl_ref, dbf_ref):
        @pl.when(pl.program_id(0) == 0)
        def _():
            dbf_ref[...] = jnp.zeros(dbf_ref.shape, F32)

        x = fl_ref[0] + bf_ref[...]
        dc_all = dc_ref[0]
        lower = (lax.broadcasted_iota(jnp.int32, (FOX_SCAN, FOX_SCAN), 0)
                 >= lax.broadcasted_iota(jnp.int32, (FOX_SCAN, FOX_SCAN), 1)).astype(BF16)
        carry = jnp.zeros((LRU_BLOCKS, 1), F32)
        total = jnp.zeros((LRU_BLOCKS, 1), F32)
        for j in reversed(range(SEQ // FOX_SCAN)):
            cols = slice(j * FOX_SCAN, (j + 1) * FOX_SCAN)
            blk = dc_all[:, cols]
            dlogf = _dot_exact(blk, lower) + carry
            carry = carry + jnp.sum(blk, axis=1, keepdims=True)
            dx = dlogf * jax.nn.sigmoid(-x[:, cols])
            dfl_ref[0, :, cols] = dx
            total = total + jnp.sum(dx, axis=1, keepdims=True)
        dbf_ref[...] += total

    return pl.pallas_call(
        body, grid=(nb,),
        in_specs=[pl.BlockSpec((1, 8, SEQ), lambda b: (b, 0, 0)), pl.BlockSpec((8, 1), lambda b: (0, 0)),
                  pl.BlockSpec((1, 8, SEQ), lambda b: (b, 0, 0))],
        out_specs=[pl.BlockSpec((1, 8, SEQ), lambda b: (b, 0, 0)), pl.BlockSpec((8, 1), lambda b: (0, 0))],
        out_shape=[jax.ShapeDtypeStruct((nb, 8, SEQ), F32), jax.ShapeDtypeStruct((8, 1), F32)],
        name=name, compiler_params=_params("arbitrary"),
    )(fl, bf, dc)


def _seq3(a, nb):
    return a.reshape(nb, a.shape[0] // nb, a.shape[1])


def _flat2(a):
    return a.reshape(a.shape[0] * a.shape[1], a.shape[2])


def _residual_out(y, w, h, next_gain, *, name):
    if next_gain is None:
        out, = _matmul(y, w, extras=(h,), epilogue=lambda acc, res: (acc + res,), name=name)
        return out, None

    def epilogue(acc, res, g):
        out = acc + res
        return out, out * _rstd(out) * g
    return _matmul(y, w, outs=(F32, BF16), extras=(h,), consts=(next_gain,), epilogue=epilogue, whole_rows=True, name=name)


def _mlp_fwd(h, hn, p, tag, next_gain):
    act, = _matmul(hn, p["w_up_t"], tb=True, outs=(BF16,), epilogue=lambda acc: (jnp.square(jnp.maximum(acc, 0.0)),),
                   name=f"{tag}_up")
    out, hn_next = _residual_out(act, p["w_down"], h, next_gain, name=f"{tag}_down")
    return out, hn_next, (h, hn, act)


def _mlp_bwd(dh, dhb, p, saved, tag):
    h, hn, act = saved
    dup, = _matmul(dhb, p["w_down"], tb=True, outs=(BF16,), extras=(act,),
                   epilogue=lambda acc, act: (acc * (2.0 * jnp.sqrt(act.astype(F32))),), name=f"{tag}_bwd_dup")
    dw_down, = _matmul(act, dhb, ta=True, outs=(BF16,),name=f"{tag}_bwd_dwdown")
    dw_up_t, = _matmul(dup, hn, ta=True, outs=(BF16,),name=f"{tag}_bwd_dwup")
    dh2, dh2b, dg = _norm_input_grad(dup, p["w_up_t"], h, dh, p["norm"], name=f"{tag}_bwd_dh")
    return dh2, dh2b, {"norm": dg, "w_up_t": dw_up_t, "w_down": dw_down}


def _xa_fwd(h, hn, mem, p, tag, nb, next_gain):
    mem_n = _rms_fwd(mem, p["mem_norm"], name=f"{tag}_memnorm")
    q, = _matmul(hn, p["w_q"], outs=(BF16,), name=f"{tag}_q")
    kv, = _matmul(mem_n, p["w_kv_t"], tb=True, outs=(BF16,), name=f"{tag}_kv")
    q3, kv3 = _seq3(q, nb), _seq3(kv, nb)
    o, lse = _attn_fwd((q3, 0), (kv3, 0), (kv3, 1), heads=XA_HEADS, dh=XA_HEAD_DIM, mode="full",
                       name=f"{tag}_attn")
    o = _flat2(o)
    ob, = _rowwise(lambda o: o, [o], [], [(D_MODEL, BF16)], name=f"{tag}_ocast")
    out, hn_next = _residual_out(ob, p["w_o"], h, next_gain, name=f"{tag}_o")
    return out, hn_next, (h, hn, mem_n, q3, kv3, o, ob, lse)


def _xa_bwd(dh, dhb, mem, p, saved, tag, nb):
    h, hn, mem_n, q3, kv3, o, ob, lse = saved
    dob, delta = _matmul(dhb, p["w_o"], tb=True, outs=(BF16, (F32, LANES)), extras=(o,), consts=(_head_expand(XA_HEADS, D_MODEL).T,),
                         epilogue=lambda acc, o, e: (acc, _dot_exact(acc * o, e)), name=f"{tag}_bwd_do")
    dw_o, = _matmul(ob, dhb, ta=True, outs=(BF16,),name=f"{tag}_bwd_dwo")
    dq, dk, dv = _attn_bwd((q3, 0), (kv3, 0), (kv3, 1), (_seq3(dob, nb), 0), lse, _seq3(delta, nb), heads=XA_HEADS,
                           dh=XA_HEAD_DIM, mode="full", dq_dtype=BF16, name=f"{tag}_bwd_attn")
    dqb = _flat2(dq)
    dkvb, = _rowwise(lambda a, b: jnp.concatenate([a, b], axis=1), [_flat2(dk), _flat2(dv)], [], [(2 * D_MODEL, BF16)],
                     name=f"{tag}_bwd_dkvcast")
    dw_q, = _matmul(hn, dqb, ta=True, outs=(BF16,),name=f"{tag}_bwd_dwq")
    dw_kv_t, = _matmul(dkvb, mem_n, ta=True, outs=(BF16,),name=f"{tag}_bwd_dwkv")
    dmem_n, = _matmul(dkvb, p["w_kv_t"], name=f"{tag}_bwd_dmemn")
    dg_mem, = _rowwise(lambda x, d, g: _rms_bwd_vals(x, d, g)[1], [mem, dmem_n], [p["mem_norm"]], [], [(1, D_MODEL)],
                       name=f"{tag}_bwd_memnorm")
    dh2, dh2b, dg = _norm_input_grad(dqb, p["w_q"], h, dh, p["norm"], tb=True, name=f"{tag}_bwd_dh")
    return dh2, dh2b, {"norm": dg, "mem_norm": dg_mem, "w_q": dw_q, "w_kv_t": dw_kv_t, "w_o": dw_o}


AB_MAIN = 2 * LRU_WIDTH + 3 * ATT_W


def _ab_fwd(h, hn, p, nb, next_gain):
    w_in_t = p["w_in_t"]
    zug, = _matmul(hn, w_in_t[:2 * LRU_WIDTH], tb=True, name="ab_in_ug")
    qkv, = _matmul(hn, w_in_t[2 * LRU_WIDTH:AB_MAIN], tb=True, outs=(BF16,), tn=768, name="ab_in_qkv")
    zf, = _matmul(hn, w_in_t[AB_MAIN:], tb=True, name="ab_in_f")
    fl = zf[:, :8].reshape(nb, SEQ, 8).transpose(0, 2, 1)
    cbias = _fox_bias(fl, p["b_f"], name="ab_fox_bias")
    kb = cbias.reshape(nb, 8, SEQ // ATT_CHUNK, ATT_CHUNK)
    y_a, hl = _lru_fwd(zug, p["conv_w"], p["conv_b"], p["w_a"], p["b_a"], p["w_i"], p["b_i"], p["lam"], name="ab_lru")
    qkv3 = _seq3(qkv, nb)
    o, lse = _attn_fwd((qkv3, 0), (qkv3, 1), (qkv3, 2), kb, heads=8, dh=HEAD_DIM, mode="causal", name="ab_fox")
    o = _flat2(o)
    y, = _rowwise(lambda a, b: jnp.concatenate([a, b.astype(BF16)], axis=1), [y_a, o], [], [(D_MODEL, BF16)], name="ab_ycat")
    out, hn_next = _residual_out(y, p["w_out"], h, next_gain, name="ab_out")
    return out, hn_next, (h, hn, zug, qkv3, fl, kb, hl, o, lse, y)


def _ab_bwd(dh, dhb, p, saved, nb):
    h, hn, zug, qkv3, fl, kb, hl, o, lse, y = saved
    dy, dyb, delta = _matmul(dhb, p["w_out"], tb=True, outs=(F32, BF16, (F32, LANES)), extras=(y,), consts=(_head_expand(8, ATT_W).T,),
                             epilogue=lambda acc, y, e: (acc, acc, _dot_exact(acc[:, ATT_W:] * y[:, ATT_W:].astype(F32), e)),
                             name="ab_bwd_dy")
    dw_out, = _matmul(y, dhb, ta=True, outs=(BF16,),name="ab_bwd_dwout")
    dzug, dcw, dcb, dwa, dba, dwi, dbi, dlam = _lru_bwd(zug, hl, dy, p["conv_w"], p["conv_b"], p["w_a"], p["b_a"],
                                                        p["w_i"], p["b_i"], p["lam"], name="ab_bwd_lru")
    dq, dk, dv, dkb, drow = _attn_bwd((qkv3, 0), (qkv3, 1), (qkv3, 2), (_seq3(dyb, nb), 1), lse, _seq3(delta, nb), kb,
                                      heads=8, dh=HEAD_DIM, mode="causal", name="ab_bwd_fox")
    dcum = dkb.reshape(nb, 8, SEQ) + drow[:, :, :8].transpose(0, 2, 1)
    dfl, dbf = _fox_bias_bwd(fl, p["b_f"], dcum, name="ab_bwd_fox_bias")
    dzf = jnp.pad(dfl.transpose(0, 2, 1).reshape(nb * SEQ, 8), ((0, 0), (0, LANES - 8)))
    dz, = _rowwise(lambda a, q, k, v, f: jnp.concatenate([a, q.astype(BF16), k.astype(BF16), v.astype(BF16), f.astype(BF16)], axis=1),
                   [dzug, _flat2(dq), _flat2(dk), _flat2(dv), dzf], [], [(AB_MAIN + LANES, BF16)], name="ab_bwd_dzcat")
    dw_in_t, = _matmul(dz, hn, ta=True, outs=(BF16,),tm=384, name="ab_bwd_dwin")
    dh2, dh2b, dg = _norm_input_grad(dz, p["w_in_t"], h, dh, p["norm"], name="ab_bwd_dh")
    grads = {"norm": dg, "w_in_t": dw_in_t[:AB_MAIN + 8], "conv_w": dcw, "conv_b": dcb, "w_a": _diag_blocks(dwa), "b_a": dba,
             "w_i": _diag_blocks(dwi), "b_i": dbi, "lam": dlam, "b_f": dbf.reshape(1, 8), "w_out": dw_out}
    return dh2, dh2b, grads


CD_IN = 3 * ATT_W + ATT_W + 2 * SWA_KW


def _gqa_share():
    src = jnp.arange(SWA_KW)[:, None]
    dst = jnp.arange(ATT_W)[None, :]
    per_kv = ATT_W // (SWA_KW // HEAD_DIM)
    return ((dst // per_kv == src // HEAD_DIM) & (dst % HEAD_DIM == src % HEAD_DIM)).astype(BF16)


def _cd_fwd(h, hn, p, nb, next_gain):
    z, = _matmul(hn, p["w_in_t"], tb=True, tn=384, name="cd_in")
    cos, sin = _rope_tables()
    per_seq = SEQ // ROW_TILE
    table_map = lambda i: (i % per_seq, 0)

    def rope_fn(z, cos, sin, share):
        qc, kc, vc = z[:, 0:ATT_W], z[:, ATT_W:2 * ATT_W], z[:, 2 * ATT_W:3 * ATT_W]
        qd, kd, vd = z[:, 3 * ATT_W:4 * ATT_W], z[:, 4 * ATT_W:4 * ATT_W + SWA_KW], z[:, 4 * ATT_W + SWA_KW:]
        kd8 = jnp.dot(_rotate(kd, cos, sin).astype(BF16), share, preferred_element_type=F32)
        vd8 = jnp.dot(vd.astype(BF16), share, preferred_element_type=F32)
        qk = jnp.concatenate([_rotate(qc, cos, sin), _rotate(kc, cos, sin)], axis=1)
        return qk, vc, _rotate(qd, cos, sin), kd8, vd8, qk, qk, vc, vc
    dils = [dil for _, dil in DIL_PATTERN if dil > 1]
    outs = _rowwise(rope_fn, [z, cos, sin], [_gqa_share()],
                    [(2 * ATT_W, BF16)] + [(ATT_W, BF16)] * 4 + [(2 * ATT_W, BF16, d) for d in dils] + [(ATT_W, BF16, d) for d in dils],
                    name="cd_rope", row_maps=[None, table_map, table_map])
    qk, vc, qd, kd, vd = outs[:5]
    views = {1: (_seq3(qk, nb), _seq3(vc, nb))}
    for d, qk_d, vc_d in zip(dils, outs[5:5 + len(dils)], outs[5 + len(dils):]):
        views[d] = (_seq3(qk_d, nb), _seq3(vc_d, nb))
    in_classes = lambda a, width, d: _flat2(a) if d == 1 else ("classes", _flat2(a), width, d)
    branch = []
    for window, dil in DIL_PATTERN:
        qk_v, vc_v = views[dil]
        o_i, lse_i = _attn_fwd((qk_v, 0), (qk_v, 1), (vc_v, 0), classes=dil, heads=8, dh=HEAD_DIM, mode="band",
                               max_dist=window // dil, name=f"cd_dil{dil}")
        branch.append((in_classes(o_i, ATT_W, dil), in_classes(lse_i, LANES, dil)))
    expand = _head_expand(8, ATT_W)

    def combine(o1, o2, o3, l1, l2, l3, e):
        m = jnp.maximum(jnp.maximum(l1, l2), l3)
        lt = m + jnp.log(jnp.exp(l1 - m) + jnp.exp(l2 - m) + jnp.exp(l3 - m))
        o = sum(_dot_exact(jnp.exp(l - lt), e) * o_ for o_, l in ((o1, l1), (o2, l2), (o3, l3)))
        return o, lt
    y_c, lse_c = _rowwise(combine, [b[0] for b in branch] + [b[1] for b in branch], [expand], [(ATT_W, F32), (LANES, F32)],
                          name="cd_dil_combine")
    qd3, kd3, vd3 = _seq3(qd, nb), _seq3(kd, nb), _seq3(vd, nb)
    o_d, lse_band = _attn_fwd((qd3, 0), (kd3, 0), (vd3, 0), heads=8, dh=HEAD_DIM, mode="band", max_dist=SWA_WINDOW - 1,
                              name="cd_swa")

    def sink_combine(o, l, e, sink):
        lt = jnp.maximum(l, sink) + jnp.log(1.0 + jnp.exp(-jnp.abs(l - sink)))
        return o * _dot_exact(jnp.exp(l - lt), e), lt
    y_d, lse_d = _rowwise(sink_combine, [_flat2(o_d), _flat2(lse_band)], [expand, p["sink"]], [(ATT_W, F32), (LANES, F32)],
                          name="cd_swa_sink")
    y, = _rowwise(lambda a, b: jnp.concatenate([a, b], axis=1), [y_c, y_d], [], [(D_MODEL, BF16)], name="cd_ycat")
    out, hn_next = _residual_out(y, p["w_out"], h, next_gain, name="cd_out")
    return out, hn_next, (h, hn, views, qd3, kd3, vd3, y_c, lse_c, y_d, lse_d, y)


def _cd_bwd(dh, dhb, p, saved, nb):
    h, hn, views, qd3, kd3, vd3, y_c, lse_c, y_d, lse_d, y = saved
    dy, dyb = _matmul(dhb, p["w_out"], tb=True, outs=(F32, BF16), epilogue=lambda acc: (acc, acc), name="cd_bwd_dy")
    dw_out, = _matmul(y, dhb, ta=True, outs=(BF16,),name="cd_bwd_dwout")
    dyb3 = _seq3(dyb, nb)
    dils = [dil for _, dil in DIL_PATTERN if dil > 1]
    gather = _head_expand(8, ATT_W).T

    def prepare(do, o, lse, e):
        delta = _dot_exact(do * o, e)
        return (delta,) + (do,) * len(dils) + (lse,) * len(dils) + (delta,) * len(dils)
    outs = _rowwise(prepare, [(dy, ATT_W, 0), y_c, lse_c], [gather],
                    [(LANES, F32)] + [(ATT_W, BF16, d) for d in dils] + [(LANES, F32, d) for d in dils] * 2, name="cd_bwd_delta_dil")
    seen = {1: ((dyb3, 0), _seq3(lse_c, nb), _seq3(outs[0], nb))}
    for j, d in enumerate(dils):
        seen[d] = ((_seq3(outs[1 + j], nb), 0), _seq3(outs[1 + len(dils) + j], nb), _seq3(outs[1 + 2 * len(dils) + j], nb))
    in_classes = lambda a, d: _flat2(a) if d == 1 else ("classes", _flat2(a), ATT_W, d)
    parts = []
    for window, dil in DIL_PATTERN:
        (qk_v, vc_v), (do_v, lse_v, delta_v) = views[dil], seen[dil]
        grads = _attn_bwd((qk_v, 0), (qk_v, 1), (vc_v, 0), do_v, lse_v, delta_v, classes=dil, heads=8, dh=HEAD_DIM, mode="band",
                          max_dist=window // dil, name=f"cd_bwd_dil{dil}")
        parts.append([in_classes(a, dil) for a in grads])
    delta_d, dsink = _head_delta((dy, ATT_W, 1), y_d, 8, lse=lse_d, sink=p["sink"], name="cd_bwd_delta_swa")
    dqd, dkd, dvd = _attn_bwd((qd3, 0), (kd3, 0), (vd3, 0), (dyb3, 1), _seq3(lse_d, nb), _seq3(delta_d, nb), heads=8,
                              dh=HEAD_DIM, mode="band", max_dist=SWA_WINDOW - 1, name="cd_bwd_swa")
    cos, sin = _rope_tables()
    per_seq = SEQ // ROW_TILE
    table_map = lambda i: (i % per_seq, 0)

    def unrope(q1, q2, q3, k1, k2, k3, v1, v2, v3, qd, kd8, vd8, cos, sin, gather):
        back = lambda x: _rotate(x, cos, -sin)
        kd, vd = _dot_exact(kd8, gather), _dot_exact(vd8, gather)
        return jnp.concatenate([back(q1 + q2 + q3), back(k1 + k2 + k3), v1 + v2 + v3, back(qd), back(kd), vd], axis=1)
    ins = [parts[b][t] for t in range(3) for b in range(3)] + [_flat2(dqd), _flat2(dkd), _flat2(dvd), cos, sin]
    dz, = _rowwise(unrope, ins, [_gqa_share().T], [(CD_IN, BF16)], name="cd_bwd_unrope",
                   row_maps=[None] * 12 + [table_map, table_map])
    dw_in_t, = _matmul(dz, hn, ta=True, outs=(BF16,),tm=384, name="cd_bwd_dwin")
    dh2, dh2b, dg = _norm_input_grad(dz, p["w_in_t"], h, dh, p["norm"], name="cd_bwd_dh")
    return dh2, dh2b, {"norm": dg, "w_in_t": dw_in_t, "sink": dsink[:, :8], "w_out": dw_out}


def _local_step(x, mem, target, w, complete=None, grads_ready=None):
    nb = x.shape[0]
    h = x.reshape(nb * SEQ, D_MODEL)
    mem2 = mem.reshape(nb * MEM_LEN, D_MODEL)
    tgt = target.reshape(nb * SEQ, D_MODEL)

    hn = _rms_fwd(h, w["ab"]["norm"], name="ab_norm")
    h, hn, s_ab = _ab_fwd(h, hn, w["ab"], nb, w["xa0"]["norm"])
    if complete is not None:
        complete(h)
    h, hn, s_xa0 = _xa_fwd(h, hn, mem2, w["xa0"], "xa0", nb, w["mlp0"]["norm"])
    h, hn, s_mlp0 = _mlp_fwd(h, hn, w["mlp0"], "mlp0", w["cd"]["norm"])
    h, hn, s_cd = _cd_fwd(h, hn, w["cd"], nb, w["xa1"]["norm"])
    h, hn, s_xa1 = _xa_fwd(h, hn, mem2, w["xa1"], "xa1", nb, w["mlp1"]["norm"])
    h, _, s_mlp1 = _mlp_fwd(h, hn, w["mlp1"], "mlp1", None)

    dh, dhb, loss, dg_final = _loss_and_grad(h, tgt, w["final_norm"], name="loss")
    grads = {"final_norm": dg_final}
    dh, dhb, grads["mlp1"] = _mlp_bwd(dh, dhb, w["mlp1"], s_mlp1, "mlp1")
    dh, dhb, grads["xa1"] = _xa_bwd(dh, dhb, mem2, w["xa1"], s_xa1, "xa1", nb)
    dh, dhb, grads["cd"] = _cd_bwd(dh, dhb, w["cd"], s_cd, nb)
    if grads_ready is not None:
        grads_ready(0, grads)
    dh, dhb, grads["mlp0"] = _mlp_bwd(dh, dhb, w["mlp0"], s_mlp0, "mlp0")
    dh, dhb, grads["xa0"] = _xa_bwd(dh, dhb, mem2, w["xa0"], s_xa0, "xa0", nb)
    if grads_ready is not None:
        grads_ready(1, grads)
    dh, dhb, grads["ab"] = _ab_bwd(dh, dhb, w["ab"], s_ab, nb)
    return loss, dh.reshape(nb, SEQ, D_MODEL), grads


ANY = pl.BlockSpec(memory_space=pl.ANY)


def _place():
    x, y, c = lax.axis_index("x"), lax.axis_index("y"), lax.axis_index("c")
    return x, y, c, [(1 - x, y), (x, 1 - y), (1 - x, 1 - y)]


def _gather_chips(shard):
    half = shard.shape[0] // 2

    def body(src, out, send_sems, recv_sems):
        x, y, c, chips = _place()
        sibling = (x, y, 1 - c)

        def rows(slot, core):
            return out.at[slot, pl.ds(core * half, half)]

        def copy(k, src_ref, dst_ref, to):
            return pltpu.make_async_remote_copy(src_ref=src_ref, dst_ref=dst_ref, send_sem=send_sems.at[k],
                                                recv_sem=recv_sems.at[k], device_id=to, device_id_type=MESH)

        first = [copy(k, src.at[pl.ds(c * half, half)], rows(2 * x + y, c), (px, py, c)) for k, (px, py) in enumerate(chips)]
        for cp in first:
            cp.start()
        passed = [copy(3 + k, rows(2 * px + py, c), rows(2 * px + py, c), sibling) for k, (px, py) in enumerate(chips)]
        for k, (px, py) in enumerate(chips):
            copy(k, src.at[pl.ds(c * half, half)], rows(2 * px + py, c), (px, py, c)).wait_recv()
            passed[k].start()
        for k, (px, py) in enumerate(chips):
            copy(3 + k, rows(2 * px + py, 1 - c), rows(2 * px + py, 1 - c), sibling).wait_recv()
        for cp in first + passed:
            cp.wait_send()

    return pl.pallas_call(
        body, out_shape=jax.ShapeDtypeStruct((N_CHIPS,) + shard.shape, shard.dtype), in_specs=[ANY], out_specs=ANY,
        scratch_shapes=[pltpu.SemaphoreType.DMA((6,)), pltpu.SemaphoreType.DMA((6,))], name="gather_chips",
    )(shard)


HBM = pl.BlockSpec(memory_space=pltpu.HBM)
SEM = pl.BlockSpec(memory_space=pltpu.SEMAPHORE)
SIDE_EFFECT = pltpu.SideEffectType.DATAFLOW_SIDE_EFFECTING


def _chip_copies(src, land, send_sems, recv_sems):
    half = src.shape[0] // 2
    x, y, c, chips = _place()

    def copy(k, slot, to):
        return pltpu.make_async_remote_copy(src_ref=src.at[pl.ds(c * half, half)], dst_ref=land.at[slot, pl.ds(c * half, half)],
                                            send_sem=send_sems[k], recv_sem=recv_sems[k], device_id=to, device_id_type=MESH)
    out = [copy(k, 2 * x + y, (px, py, c)) for k, (px, py) in enumerate(chips)]
    back = [copy(k, 2 * px + py, (px, py, c)) for k, (px, py) in enumerate(chips)]
    return out, back


def _gather_start(shard):
    def body(src, land, *rest):
        sems, token = rest[:6], rest[8]
        out, _ = _chip_copies(src, land, sems[:3], sems[3:])
        for cp in out:
            cp.start()
        token[...] = jnp.zeros(token.shape, F32)

    sem = pltpu.SemaphoreType.DMA(())
    land_shape = (N_CHIPS,) + shard.shape
    return pl.pallas_call(
        body, name="gather_start",
        out_shape=(sem,) * 6 + (pltpu.HBM(shard.shape, shard.dtype), pltpu.HBM(land_shape, shard.dtype),
                                jax.ShapeDtypeStruct((8, LANES), F32)),
        in_specs=(HBM, HBM), out_specs=(SEM,) * 6 + (HBM, HBM, pl.BlockSpec(memory_space=pltpu.VMEM)),
        input_output_aliases={0: 6, 1: 7}, compiler_params=pltpu.CompilerParams(has_side_effects=SIDE_EFFECT),
    )(pltpu.with_memory_space_constraint(shard, pltpu.HBM),
      pltpu.with_memory_space_constraint(lax.empty(land_shape, shard.dtype), pltpu.HBM))


def _gather_wait(started, after):
    sems, src_thru, land_thru = started[:6], started[6], started[7]

    def body(src, land, *rest):
        out, back = _chip_copies(src, land, rest[:3], rest[3:6])
        for cp in out:
            cp.wait_send()
        for cp in back:
            cp.wait_recv()

    return pl.pallas_call(
        body, name="gather_wait",
        out_shape=(pltpu.HBM(src_thru.shape, src_thru.dtype), pltpu.HBM(land_thru.shape, land_thru.dtype)),
        in_specs=(HBM, HBM) + (SEM,) * 6 + (pl.BlockSpec(memory_space=pl.ANY),), out_specs=(HBM, HBM),
        input_output_aliases={0: 0, 1: 1}, compiler_params=pltpu.CompilerParams(has_side_effects=SIDE_EFFECT),
    )(src_thru, land_thru, *sems, after)[1]


def _gather_pass(land):
    half = land.shape[1] // 2

    def body(land_in, land_out, send_sems, recv_sems):
        x, y, c, chips = _place()

        def copy(k, slot, core):
            rows = land_out.at[slot, pl.ds(core * half, half)]
            return pltpu.make_async_remote_copy(src_ref=rows, dst_ref=rows, send_sem=send_sems.at[k], recv_sem=recv_sems.at[k],
                                                device_id=(x, y, 1 - c), device_id_type=MESH)
        sends = [copy(k, 2 * px + py, c) for k, (px, py) in enumerate(chips)]
        for cp in sends:
            cp.start()
        for k, (px, py) in enumerate(chips):
            copy(k, 2 * px + py, 1 - c).wait_recv()
        for cp in sends:
            cp.wait_send()

    return pl.pallas_call(
        body, out_shape=jax.ShapeDtypeStruct(land.shape, land.dtype), in_specs=[ANY], out_specs=ANY,
        scratch_shapes=[pltpu.SemaphoreType.DMA((3,)), pltpu.SemaphoreType.DMA((3,))], input_output_aliases={0: 0},
        name="gather_pass",
    )(land)


def _swap_cores(block, *, name, half_rows=None):
    shape = block.shape if half_rows is None else (block.shape[0], half_rows, block.shape[2])

    def body(src, out, send_sem, recv_sem):
        x, y, c, _ = _place()
        part = src if half_rows is None else src.at[:, pl.ds((1 - c) * half_rows, half_rows)]
        cp = pltpu.make_async_remote_copy(src_ref=part, dst_ref=out, send_sem=send_sem, recv_sem=recv_sem,
                                          device_id=(x, y, 1 - c), device_id_type=MESH)
        cp.start()
        cp.wait()

    return pl.pallas_call(
        body, out_shape=jax.ShapeDtypeStruct(shape, block.dtype), in_specs=[ANY], out_specs=ANY,
        scratch_shapes=[pltpu.SemaphoreType.DMA(()), pltpu.SemaphoreType.DMA(())], name=name,
    )(block)


def _scatter_copies(parts, land, send_sems, recv_sems):
    x, y, c, chips = _place()
    return [pltpu.make_async_remote_copy(src_ref=parts.at[2 * px + py], dst_ref=land.at[k], send_sem=send_sems[k],
                                         recv_sem=recv_sems[k], device_id=(px, py, c), device_id_type=MESH)
            for k, (px, py) in enumerate(chips)]


def _scatter_start(parts, tag):
    def body(src, land, *rest):
        for cp in _scatter_copies(src, land, rest[:3], rest[3:6]):
            cp.start()
        rest[8][...] = jnp.zeros(rest[8].shape, F32)

    sem = pltpu.SemaphoreType.DMA(())
    land_shape = (3,) + parts.shape[1:]
    return pl.pallas_call(
        body, name=f"scatter_start_{tag}",
        out_shape=(sem,) * 6 + (pltpu.HBM(parts.shape, parts.dtype), pltpu.HBM(land_shape, parts.dtype),
                                jax.ShapeDtypeStruct((8, LANES), F32)),
        in_specs=(HBM, HBM), out_specs=(SEM,) * 6 + (HBM, HBM, pl.BlockSpec(memory_space=pltpu.VMEM)),
        input_output_aliases={0: 6, 1: 7}, compiler_params=pltpu.CompilerParams(has_side_effects=SIDE_EFFECT),
    )(pltpu.with_memory_space_constraint(parts, pltpu.HBM),
      pltpu.with_memory_space_constraint(lax.empty(land_shape, parts.dtype), pltpu.HBM))


def _scatter_wait(started, after, tag):
    def body(src, land, *rest):
        for cp in _scatter_copies(src, land, rest[:3], rest[3:6]):
            cp.wait_send()
            cp.wait_recv()

    src_thru, land_thru = started[6], started[7]
    return pl.pallas_call(
        body, name=f"scatter_wait_{tag}",
        out_shape=(pltpu.HBM(src_thru.shape, src_thru.dtype), pltpu.HBM(land_thru.shape, land_thru.dtype)),
        in_specs=(HBM, HBM) + (SEM,) * 6 + (pl.BlockSpec(memory_space=pl.ANY),), out_specs=(HBM, HBM),
        input_output_aliases={0: 0, 1: 1}, compiler_params=pltpu.CompilerParams(has_side_effects=SIDE_EFFECT),
    )(src_thru, land_thru, *started[:6], after)[1]


def _sum_all_devices(vec):
    rows = vec.shape[0]

    def body(x_ref, total_ref, all_ref, send_sems, recv_sems, local_sem):
        x, y, c, chips = _place()
        me, sibling = (x, y, c), (x, y, 1 - c)

        def slot(px, py, pc):
            return all_ref.at[4 * px + 2 * py + pc]

        def copy(k, block, to, src=None):
            return pltpu.make_async_remote_copy(src_ref=slot(*block) if src is None else src, dst_ref=slot(*block),
                                                send_sem=send_sems.at[k], recv_sem=recv_sems.at[k], device_id=to,
                                                device_id_type=MESH)

        mine = pltpu.make_async_copy(x_ref, slot(*me), local_sem)
        mine.start()
        first = [copy(0, me, sibling, src=x_ref)] + [copy(1 + j, me, (*chip, c), src=x_ref) for j, chip in enumerate(chips)]
        for cp in first:
            cp.start()
        passed = [copy(4 + j, (*chip, c), sibling) for j, chip in enumerate(chips)]
        for j, chip in enumerate(chips):
            copy(1 + j, (*chip, c), me).wait_recv()
            passed[j].start()
        copy(0, sibling, me).wait_recv()
        for j, chip in enumerate(chips):
            copy(4 + j, (*chip, 1 - c), me).wait_recv()
        for cp in first + passed:
            cp.wait_send()
        mine.wait()
        acc = all_ref[0]
        for d in range(1, 8):
            acc = acc + all_ref[d]
        total_ref[...] = acc

    vmem = pl.BlockSpec(memory_space=pltpu.VMEM)
    return pl.pallas_call(
        body, out_shape=[jax.ShapeDtypeStruct((rows, LANES), F32), jax.ShapeDtypeStruct((8, rows, LANES), F32)],
        in_specs=[vmem], out_specs=[vmem, vmem],
        scratch_shapes=[pltpu.SemaphoreType.DMA((7,)), pltpu.SemaphoreType.DMA((7,)), pltpu.SemaphoreType.DMA(())],
        name="sum_all_devices", compiler_params=pltpu.CompilerParams(vmem_limit_bytes=VMEM_LIMIT_BYTES),
    )(vec)[0]


PACK_COLS = 1024
BIG = (("mlp_w_up", 2, 1024, True), ("mlp_w_down", 2, 1024, False), ("xa_w_kv", 2, 512, True), ("xa_w_q", 2, 256, False),
       ("xa_w_o", 2, 256, False), ("ab_w_out", 1, 256, False), ("cd_w_out", 1, 256, False), ("cd_w_in", 1, 576, True),
       ("ab_w_in", 1, 642, True))
ENTRIES = tuple((name, layer, rows, transposed) for name, layers, rows, transposed in BIG for layer in range(layers))
FIRST_ENTRIES = tuple(e for e in ENTRIES if e[0].startswith("ab_"))
LATER_ENTRIES = tuple(e for e in ENTRIES if not e[0].startswith("ab_"))


def _tile_rows(entry):
    return -(-entry[2] // 16) * 16


def _padded_rows(entries):
    return -(-sum(_tile_rows(e) for e in entries) // 32) * 32


SMALL = (("ab_norm", (1, 1024)), ("ab_conv_w", (1, 4, 512)), ("ab_conv_b", (1, 512)), ("lru_w_a", (1, 8, 64, 64)),
         ("lru_b_a", (1, 512)), ("lru_w_i", (1, 8, 64, 64)), ("lru_b_i", (1, 512)), ("lru_lambda", (1, 512)),
         ("fox_b_f", (1, 8)), ("cd_norm", (1, 1024)), ("cd_sink", (1, 8)), ("xa_norm", (2, 1024)),
         ("xa_mem_norm", (2, 1024)), ("mlp_norm", (2, 1024)), ("final_norm", (1024,)), ("loss", (1,)))
SPLIT_SMALL = ("ab_conv_w", "cd_norm")


def _pack_rows(parts, entries, dtype):
    lead = parts[0].shape[:-2]
    zeros = lambda rows: jnp.zeros(lead + (rows, PACK_COLS), dtype)
    pieces = [jnp.pad(p.astype(dtype), ((0, 0),) * len(lead) + ((0, _tile_rows(e) - e[2]), (0, 0))) for p, e in zip(parts, entries)]
    tail = _padded_rows(entries) - sum(_tile_rows(e) for e in entries)
    return jnp.concatenate(pieces + ([zeros(tail)] if tail else []), axis=len(lead))


def _unpack_rows(packed, entries):
    out, r0 = [], 0
    for e in entries:
        out.append(packed[..., r0:r0 + e[2], :])
        r0 += _tile_rows(e)
    return out


def _shard_rows(w, entries):
    return [(w[name][layer].T if transposed else w[name][layer]) for name, layer, _, transposed in entries]


def _shard_blocks(rows):
    out = {}
    for (name, layer, _, transposed), r in zip(ENTRIES, rows):
        out.setdefault(name, []).append(r.T if transposed else r)
    return {name: jnp.stack(layers) for name, layers in out.items()}


def _pack_small(vals):
    flat = jnp.concatenate([vals[name].astype(F32).reshape(-1) for name, _ in SMALL])
    size = -(-flat.shape[0] // (8 * LANES)) * (8 * LANES)
    return jnp.pad(flat, (0, size - flat.shape[0])).reshape(-1, LANES)


def _unpack_small(packed):
    flat, out, at = packed.reshape(-1), {}, 0
    for name, shape in SMALL:
        out[name] = flat[at:at + math.prod(shape)].reshape(shape)
        at += math.prod(shape)
    return out


def _chip_part(full, chip):
    width = full.shape[-1] // N_CHIPS
    return lax.dynamic_slice_in_dim(full, chip * width, width, axis=full.ndim - 1)


def _chip_embed(part, chip):
    full = jnp.zeros(part.shape[:-1] + (part.shape[-1] * N_CHIPS,), part.dtype)
    return lax.dynamic_update_slice_in_dim(full, part, chip * part.shape[-1], axis=part.ndim - 1)


SUBLAYER_KEYS = {"ab_w_in": ("ab", "w_in_t"), "ab_w_out": ("ab", "w_out"), "cd_w_in": ("cd", "w_in_t"), "cd_w_out": ("cd", "w_out"),
                 "xa_w_q": ("xa", "w_q"), "xa_w_kv": ("xa", "w_kv_t"), "xa_w_o": ("xa", "w_o"), "mlp_w_up": ("mlp", "w_up_t"),
                 "mlp_w_down": ("mlp", "w_down")}


def _sublayer(name, layer):
    block, leaf = SUBLAYER_KEYS[name]
    return (block if block in ("ab", "cd") else f"{block}{layer}"), leaf


def _set_big(params, full_rows):
    for (name, layer), rows in full_rows.items():
        block, leaf = _sublayer(name, layer)
        if name == "ab_w_in":
            rows = jnp.concatenate([rows, jnp.zeros((LANES - 8, PACK_COLS), rows.dtype)])
        params[block][leaf] = rows


def _build_params(full_rows, w):
    pad = lambda a: jnp.pad(a, ((0, 0), (0, LANES - a.shape[1])))
    params = {
        "ab": dict(norm=w["ab_norm"], conv_w=w["ab_conv_w"][0], conv_b=w["ab_conv_b"], w_a=_block_diag(w["lru_w_a"][0]),
                   b_a=w["lru_b_a"], w_i=_block_diag(w["lru_w_i"][0]), b_i=w["lru_b_i"], lam=w["lru_lambda"],
                   b_f=w["fox_b_f"].reshape(8, 1)),
        "cd": dict(norm=w["cd_norm"], sink=pad(w["cd_sink"])),
        "final_norm": w["final_norm"].reshape(1, D_MODEL),
    }
    for layer in range(2):
        params[f"xa{layer}"] = dict(norm=w["xa_norm"][layer:layer + 1], mem_norm=w["xa_mem_norm"][layer:layer + 1])
        params[f"mlp{layer}"] = dict(norm=w["mlp_norm"][layer:layer + 1])
    _set_big(params, full_rows)
    return params


def _grad_rows(g, entries=ENTRIES):
    out = []
    for name, layer, _, _ in entries:
        block, leaf = _sublayer(name, layer)
        out.append(g[block][leaf])
    return out


def _reduce_group(entry):
    name, layer = entry[0], entry[1]
    if name.startswith("ab_"):
        return 2
    return 0 if layer == 1 or name.startswith("cd_") else 1


REDUCE_GROUPS = tuple(tuple(e for e in ENTRIES if _reduce_group(e) == group) for group in range(3))


def _reduce_tile(half):
    return max(t for t in range(16, 1025, 16) if half % t == 0)


def _reduce_start(grads_by_chip, core, chip, tag):
    half = grads_by_chip.shape[1] // 2
    tile = _reduce_tile(half)
    steps = half // tile
    place = jnp.stack([core, chip]).astype(jnp.int32)
    got = _swap_cores(grads_by_chip, name=f"swap_cores_{tag}", half_rows=half)
    block = (1, tile, PACK_COLS)

    def sum_cores(place_ref, mine_ref, got_ref, f32_ref, bf16_ref):
        total = mine_ref[...].astype(F32) + got_ref[...].astype(F32)
        f32_ref[...] = total
        bf16_ref[...] = total.astype(BF16)

    pair32, pair16 = pl.pallas_call(
        sum_cores, name=f"sum_cores_{tag}",
        grid_spec=pltpu.PrefetchScalarGridSpec(
            num_scalar_prefetch=1, grid=(N_CHIPS, steps),
            in_specs=[pl.BlockSpec(block, lambda s, i, place_ref: (s, place_ref[0] * steps + i, 0)),
                      pl.BlockSpec(block, lambda s, i, place_ref: (s, i, 0))],
            out_specs=[pl.BlockSpec(block, lambda s, i, place_ref: (s, i, 0))] * 2),
        out_shape=[jax.ShapeDtypeStruct((N_CHIPS, half, PACK_COLS), F32), jax.ShapeDtypeStruct((N_CHIPS, half, PACK_COLS), BF16)],
        compiler_params=_params("arbitrary", "arbitrary"),
    )(place, grads_by_chip, got)
    return pair32, _scatter_start(pair16, tag), place


def _reduce_finish(state, core, tag, after):
    pair32, started, place = state
    half = pair32.shape[1]
    tile = _reduce_tile(half)
    landed = _scatter_wait(started, after, tag)

    def sum_chips(place_ref, own_ref, landed_ref, out_ref):
        out_ref[...] = own_ref[0] + landed_ref[0].astype(F32) + landed_ref[1].astype(F32) + landed_ref[2].astype(F32)

    done = pl.pallas_call(
        sum_chips, name=f"sum_chips_{tag}",
        grid_spec=pltpu.PrefetchScalarGridSpec(
            num_scalar_prefetch=1, grid=(half // tile,),
            in_specs=[pl.BlockSpec((1, tile, PACK_COLS), lambda i, place_ref: (place_ref[1], i, 0)),
                      pl.BlockSpec((3, tile, PACK_COLS), lambda i, place_ref: (0, i, 0))],
            out_specs=pl.BlockSpec((tile, PACK_COLS), lambda i, place_ref: (i, 0))),
        out_shape=jax.ShapeDtypeStruct((half, PACK_COLS), F32), compiler_params=_params("arbitrary"),
    )(place, pair32, landed)
    theirs = _swap_cores(done, name=f"share_halves_{tag}")
    return jnp.where(core == 0, jnp.concatenate([done, theirs]), jnp.concatenate([theirs, done]))


def kernel(x, mem, ab_norm, ab_w_in, ab_conv_w, ab_conv_b, lru_w_a, lru_b_a, lru_w_i, lru_b_i, lru_lambda, fox_b_f, ab_w_out, cd_norm, cd_w_in, cd_sink, cd_w_out, xa_norm, xa_mem_norm, xa_w_q, xa_w_kv, xa_w_o, mlp_norm, mlp_w_up, mlp_w_down, final_norm, loss_target, m_ab_norm, m_ab_w_in, m_ab_conv_w, m_ab_conv_b, m_lru_w_a, m_lru_b_a, m_lru_w_i, m_lru_b_i, m_lru_lambda, m_fox_b_f, m_ab_w_out, m_cd_norm, m_cd_w_in, m_cd_sink, m_cd_w_out, m_xa_norm, m_xa_mem_norm, m_xa_w_q, m_xa_w_kv, m_xa_w_o, m_mlp_norm, m_mlp_w_up, m_mlp_w_down, m_final_norm, v_ab_norm, v_ab_w_in, v_ab_conv_w, v_ab_conv_b, v_lru_w_a, v_lru_b_a, v_lru_w_i, v_lru_b_i, v_lru_lambda, v_fox_b_f, v_ab_w_out, v_cd_norm, v_cd_w_in, v_cd_sink, v_cd_w_out, v_xa_norm, v_xa_mem_norm, v_xa_w_q, v_xa_w_kv, v_xa_w_o, v_mlp_norm, v_mlp_w_up, v_mlp_w_down, v_final_norm):
    given = dict(locals())
    weight_names = [name for name, _ in SMALL if name != "loss"] + [name for name, _, _, _ in BIG]
    w = {name: given[name] for name in weight_names}
    chip = 2 * lax.axis_index("x") + lax.axis_index("y")
    core = lax.axis_index("c")

    slot = lax.broadcasted_iota(jnp.int32, (N_CHIPS, 1, 1), 0)

    def whole(entries, mine, gathered):
        return {(name, layer): jnp.where(slot == chip, own[None], got).reshape(N_CHIPS * rows, PACK_COLS)
                for (name, layer, rows, _), own, got in zip(entries, _unpack_rows(mine, entries), _unpack_rows(gathered, entries))}

    mine_first = _pack_rows(_shard_rows(w, FIRST_ENTRIES), FIRST_ENTRIES, BF16)
    mine_later = _pack_rows(_shard_rows(w, LATER_ENTRIES), LATER_ENTRIES, BF16)
    first = _gather_chips(mine_first)
    started = _gather_start(mine_later)
    owner = (core == 0).astype(F32)
    quarters = {name: _chip_embed(w[name], chip) * owner for name in SPLIT_SMALL}
    zeros = {name: jnp.zeros(shape, F32) for name, shape in SMALL}
    small_full = _unpack_small(_sum_all_devices(_pack_small({**zeros, **quarters})))
    params = _build_params(whole(FIRST_ENTRIES, mine_first, first),
                           {**w, "ab_conv_w": small_full["ab_conv_w"], "cd_norm": small_full["cd_norm"]})
    params["ab"]["norm"] = params["ab"]["norm"] + started[8][0, 0]

    def complete(h):
        _set_big(params, whole(LATER_ENTRIES, mine_later, _gather_pass(_gather_wait(started, after=h))))

    reducing = {}

    def grads_ready(group, g):
        entries = REDUCE_GROUPS[group]
        by_chip = _pack_rows([r.reshape(N_CHIPS, e[2], PACK_COLS) for r, e in zip(_grad_rows(g, entries), entries)], entries, BF16)
        reducing[group] = _reduce_start(by_chip, core, chip, f"g{group}")
        if group < 2:
            block, leaf = ("mlp0", "w_down") if group == 0 else ("ab", "w_out")
            params[block][leaf] = params[block][leaf] + reducing[group][1][8][0, 0].astype(BF16)

    loss_part, grad_x, g = _local_step(x, mem, loss_target, params, complete, grads_ready)
    grads_ready(2, g)
    reduced = {}
    for group, entries in enumerate(REDUCE_GROUPS):
        rows = _unpack_rows(_reduce_finish(reducing[group], core, f"g{group}", after=grad_x), entries)
        reduced.update({(e[0], e[1]): r for e, r in zip(entries, rows)})
    grads = _shard_blocks([reduced[e[0], e[1]] for e in ENTRIES])
    pair = lambda key, leaf: jnp.stack([g[f"{key}0"][leaf], g[f"{key}1"][leaf]])

    small_local = {"ab_norm": g["ab"]["norm"], "ab_conv_w": g["ab"]["conv_w"], "ab_conv_b": g["ab"]["conv_b"],
                   "lru_w_a": g["ab"]["w_a"], "lru_b_a": g["ab"]["b_a"], "lru_w_i": g["ab"]["w_i"], "lru_b_i": g["ab"]["b_i"],
                   "lru_lambda": g["ab"]["lam"], "fox_b_f": g["ab"]["b_f"], "cd_norm": g["cd"]["norm"], "cd_sink": g["cd"]["sink"],
                   "xa_norm": pair("xa", "norm"), "xa_mem_norm": pair("xa", "mem_norm"), "mlp_norm": pair("mlp", "norm"),
                   "final_norm": g["final_norm"], "loss": loss_part[0, :1]}
    small_sum_packed = _sum_all_devices(_pack_small(small_local))
    small_sum = _unpack_small(small_sum_packed)
    loss = small_sum["loss"][0]

    delta, new_m, new_v = {}, {}, {}
    for name, _, _, _ in BIG:
        shape = w[name].shape
        flat = lambda a: a.reshape(-1, shape[-1])
        d, m2, v2 = _adamw(flat(w[name]), flat(grads[name]), flat(given["m_" + name]), flat(given["v_" + name]), name=f"adamw_{name}")
        delta[name], new_m[name], new_v[name] = d.reshape(shape), m2.reshape(shape), v2.reshape(shape)

    def small_state(prefix):
        vals = {name: (given[prefix + name] if name != "loss" else jnp.zeros((1,), F32)) for name, _ in SMALL}
        for name in SPLIT_SMALL:
            vals[name] = _chip_embed(vals[name], chip)
        return _pack_small(vals)
    packed = _adamw(small_state(""), small_sum_packed, small_state("m_"), small_state("v_"), name="adamw_small")
    for store, vals in zip((delta, new_m, new_v), packed):
        for name, val in _unpack_small(vals).items():
            store[name] = _chip_part(val, chip) if name in SPLIT_SMALL else val
    for name in SPLIT_SMALL:
        small_sum[name] = _chip_part(small_sum[name], chip)
    grads.update({name: small_sum[name] for name, _ in SMALL})

    order = ["ab_norm", "ab_w_in", "ab_conv_w", "ab_conv_b", "lru_w_a", "lru_b_a", "lru_w_i", "lru_b_i", "lru_lambda", "fox_b_f",
             "ab_w_out", "cd_norm", "cd_w_in", "cd_sink", "cd_w_out", "xa_norm", "xa_mem_norm", "xa_w_q", "xa_w_kv", "xa_w_o",
             "mlp_norm", "mlp_w_up", "mlp_w_down", "final_norm"]
    return (loss, grad_x, *[grads[n] for n in order], *[delta[n] for n in order], *[new_m[n] for n in order],
            *[new_v[n] for n in order])
```

```python
import functools
import math

import jax
import jax.numpy as jnp
from jax import lax
from jax.experimental import pallas as pl
from jax.experimental.pallas import tpu as pltpu

F32 = jnp.float32
BF16 = jnp.bfloat16
MESH = pl.DeviceIdType.MESH

D_MODEL = 1024
SEQ = 2048
HEAD_DIM = 64
LRU_WIDTH = 512
LRU_BLOCKS = 8
LRU_C = 8.0
CONV_WIDTH = 4
ATT_W = 512
SWA_KW = 128
SWA_WINDOW = 128
DIL_PATTERN = ((128, 1), (512, 4), (2048, 16))
MEM_LEN = 256
XA_HEADS = 4
XA_HEAD_DIM = 256
D_FF = 4096
ROPE_THETA = 10000.0
EPS = 1e-6
N_CHIPS = 4
LANES = 128

ADAM_LR, ADAM_B1, ADAM_B2, ADAM_EPS, ADAM_WD, ADAM_STEP = 0.001, 0.9, 0.999, 1e-08, 0.01, 10

VMEM_LIMIT_BYTES = 56 * 1024 * 1024
MASKED = -1e30
ROW_TILE = 512
LRU_CHUNK = 512
ATT_BLOCK = 128
BAND_ROWS = 256
ATT_CHUNK = 512
CAUSAL_ROWS = 256


def _params(*sem):
    return pltpu.CompilerParams(dimension_semantics=sem, vmem_limit_bytes=VMEM_LIMIT_BYTES)


def _rowwise(fn, rows, consts=(), out_rows=(), out_accs=(), *, name, tile=ROW_TILE, row_maps=None):
    rows = [r if isinstance(r, tuple) else (r, r.shape[1], 0) for r in rows]
    consts = list(consts)
    nr, nc, no = len(rows), len(consts), len(out_rows)
    dealt_in = [r[3] if isinstance(r[0], str) else None for r in rows]
    dealt_out = [o[2] if len(o) == 3 else None for o in out_rows]
    total = next(r[0].shape[0] for r in rows if not isinstance(r[0], str))
    tile = min(tile, total)
    assert total % tile == 0
    n = total // tile
    n_acc = len(out_accs)

    def body(*refs):
        scratch = list(refs[nr + nc + no + n_acc:])
        vals = []
        for ref, d, row in zip(refs[:nr], dealt_in, rows):
            if d is None:
                vals.append(ref[...])
                continue
            sc, width = scratch.pop(0), row[2]
            for r in range(d):
                for c in range(width // LANES):
                    lanes = slice(r * width + c * LANES, r * width + (c + 1) * LANES)
                    sc.at[c][pl.ds(r, tile // d, stride=d), :] = ref[:, lanes].astype(F32)
            vals.append(jnp.concatenate([sc[c] for c in range(width // LANES)], axis=1))
        outs = fn(*vals, *[r[...] for r in refs[nr:nr + nc]])
        outs = tuple(outs) if isinstance(outs, (tuple, list)) else (outs,)
        for ref, val, d, spec in zip(refs[nr + nc:nr + nc + no], outs[:no], dealt_out, out_rows):
            if d is None:
                ref[...] = val.astype(ref.dtype)
                continue
            sc, width = scratch.pop(0), spec[0]
            for c in range(width // LANES):
                sc[c] = val[:, c * LANES:(c + 1) * LANES].astype(F32)
            for r in range(d):
                for c in range(width // LANES):
                    lanes = slice(r * width + c * LANES, r * width + (c + 1) * LANES)
                    ref[:, lanes] = sc.at[c][pl.ds(r, tile // d, stride=d), :].astype(ref.dtype)
        for ref, val in zip(refs[nr + nc + no:nr + nc + no + n_acc], outs[no:]):
            @pl.when(pl.program_id(0) == 0)
            def _(ref=ref):
                ref[...] = jnp.zeros(ref.shape, ref.dtype)
            ref[...] += val

    in_specs, args, scratch_shapes = [], [], []
    for idx, row in enumerate(rows):
        if isinstance(row[0], str):
            _, arr, width, d = row
            in_specs.append(pl.BlockSpec((tile // d, d * width), lambda i: (i, 0)))
            scratch_shapes.append(pltpu.VMEM((width // LANES, tile, LANES), F32))
        elif row_maps is not None and row_maps[idx] is not None:
            arr, width, _ = row
            in_specs.append(pl.BlockSpec((tile, width), row_maps[idx]))
        else:
            arr, width, cb = row
            in_specs.append(pl.BlockSpec((tile, width), functools.partial(lambda i, cb: (i, cb), cb=cb)))
        args.append(arr)
    in_specs += [pl.BlockSpec(c.shape, lambda i: (0, 0)) for c in consts]
    out_shape, out_specs = [], []
    for spec, d in zip(out_rows, dealt_out):
        w, dt = spec[0], spec[1]
        if d is None:
            out_shape.append(jax.ShapeDtypeStruct((total, w), dt))
            out_specs.append(pl.BlockSpec((tile, w), lambda i: (i, 0)))
        else:
            out_shape.append(jax.ShapeDtypeStruct((total // d, d * w), dt))
            out_specs.append(pl.BlockSpec((tile // d, d * w), lambda i: (i, 0)))
            scratch_shapes.append(pltpu.VMEM((w // LANES, tile, LANES), F32))
    out_shape += [jax.ShapeDtypeStruct(s, F32) for s in out_accs]
    out_specs += [pl.BlockSpec(s, lambda i: (0, 0)) for s in out_accs]
    return pl.pallas_call(
        body, grid=(n,), in_specs=in_specs, out_specs=out_specs, out_shape=out_shape, scratch_shapes=scratch_shapes, name=name,
        compiler_params=_params("arbitrary"),
    )(*args, *consts)


MATMUL_VMEM_BYTES = 40 * 1024 * 1024


def _matmul_tiles(m, n, k, tm, tn, out_bytes):
    tm, tn, tk = min(tm, m), min(tn, n), k

    def need():
        return 2 * (2 * tk * (tm + tn) + tm * tn * out_bytes) + (0 if tk == k else 4 * tm * tn)

    while need() > MATMUL_VMEM_BYTES:
        if tm >= tn and tm % 256 == 0:
            tm //= 2
        elif tn % 256 == 0:
            tn //= 2
        else:
            tk //= 2
    return tm, tn, tk


def _matmul(a, b, *, ta=False, tb=False, outs=(F32,), epilogue=None, extras=(), consts=(), sums=(), whole_rows=False,
            tm=1024, tn=1024, name):
    m, k = (a.shape[1], a.shape[0]) if ta else a.shape
    n = b.shape[0] if tb else b.shape[1]
    assert (b.shape[1] if tb else b.shape[0]) == k
    outs = [o if isinstance(o, tuple) else (o, None) for o in outs]
    out_bytes = sum(jnp.dtype(dt).itemsize for dt, w in outs if w is None) + sum(e.dtype.itemsize for e in extras)
    tm, tn, tk = _matmul_tiles(m, n, k, tm, tn, out_bytes)
    assert m % tm == 0 and n % tn == 0 and k % tk == 0, (name, m, n, k)
    nk = k // tk
    ne, nc, no = len(extras), len(consts), len(outs)
    assert tn == n or not (sums or whole_rows or any(w is not None for _, w in outs)), name
    dims = (((0 if ta else 1,), (1 if tb else 0,)), ((), ()))

    def body(*refs):
        a_ref, b_ref = refs[:2]
        e_refs, o_refs = refs[2:2 + ne + nc], refs[2 + ne + nc:2 + ne + nc + no]
        s_refs = refs[2 + ne + nc + no:2 + ne + nc + no + len(sums)]

        def finish(acc):
            vals = (acc,) if epilogue is None else epilogue(acc, *[e[...] for e in e_refs])
            for ref, val in zip(o_refs, vals[:no]):
                ref[...] = val.astype(ref.dtype)
            for ref, val in zip(s_refs, vals[no:]):
                @pl.when(pl.program_id(0) == 0)
                def _(ref=ref, val=val):
                    ref[...] = val

                @pl.when(pl.program_id(0) > 0)
                def _(ref=ref, val=val):
                    ref[...] += val

        prod = lax.dot_general(a_ref[...], b_ref[...], dims, preferred_element_type=F32)
        if nk == 1:
            finish(prod)
        else:
            acc_ref = refs[-1]
            step = pl.program_id(2)

            @pl.when(step == 0)
            def _():
                acc_ref[...] = prod

            @pl.when(step > 0)
            def _():
                acc_ref[...] += prod

            @pl.when(step == nk - 1)
            def _():
                finish(acc_ref[...])

    a_spec = pl.BlockSpec((tk, tm), lambda i, j, s: (s, i)) if ta else pl.BlockSpec((tm, tk), lambda i, j, s: (i, s))
    b_spec = pl.BlockSpec((tn, tk), lambda i, j, s: (j, s)) if tb else pl.BlockSpec((tk, tn), lambda i, j, s: (s, j))
    tile_spec = pl.BlockSpec((tm, tn), lambda i, j, s: (i, j))
    whole = lambda shape: pl.BlockSpec(shape, lambda i, j, s: (0, 0))
    return pl.pallas_call(
        body, grid=(m // tm, n // tn, nk),
        in_specs=[a_spec, b_spec] + [tile_spec] * ne + [whole(c.shape) for c in consts],
        out_specs=[tile_spec if w is None else pl.BlockSpec((tm, w), lambda i, j, s: (i, 0)) for _, w in outs]
        + [whole(shape) for shape in sums],
        out_shape=[jax.ShapeDtypeStruct((m, n if w is None else w), dt) for dt, w in outs]
        + [jax.ShapeDtypeStruct(shape, F32) for shape in sums],
        scratch_shapes=[pltpu.VMEM((tm, tn), F32)] if nk > 1 else [],
        name=name, compiler_params=_params("arbitrary" if sums else "parallel", "parallel", "arbitrary"),
    )(a, b, *extras, *consts)


def _split3(x):
    x1 = x.astype(BF16)
    r1 = x - x1.astype(F32)
    x2 = r1.astype(BF16)
    x3 = (r1 - x2.astype(F32)).astype(BF16)
    return x1, x2, x3


def _dot_exact(x, e):
    return sum(jnp.dot(p, e, preferred_element_type=F32) for p in _split3(x))


def _head_expand(heads, width):
    dh = width // heads
    rows = jnp.arange(LANES)[:, None]
    cols = jnp.arange(width)[None, :]
    return (cols // dh == rows).astype(BF16)


def _rstd(x):
    return lax.rsqrt(jnp.mean(x * x, axis=-1, keepdims=True) + EPS)


def _rms_fwd(x, gain, *, name):
    return _rowwise(lambda x, g: x * _rstd(x) * g, [x], [gain], [(x.shape[1], BF16)], name=name)[0]


def _rms_bwd_vals(x, dhn, gain):
    r = _rstd(x)
    xh = x * r
    dxh = dhn * gain
    dx = r * (dxh - xh * jnp.mean(dxh * xh, axis=-1, keepdims=True))
    return dx, jnp.sum(dhn * xh, axis=0, keepdims=True)


def _norm_input_grad(dz, w, x, dres, gain, *, tb=False, name):
    def epilogue(dhn, x, dres, g):
        dx, dg = _rms_bwd_vals(x, dhn, g)
        dh = dres + dx
        return dh, dh, dg
    return _matmul(dz, w, tb=tb, outs=(F32, BF16), extras=(x, dres), consts=(gain,), sums=((1, x.shape[1]),), epilogue=epilogue,
                   name=name)


def _loss_and_grad(h, target, gain, *, name):
    def fn(h, tgt, g):
        r = _rstd(h)
        err = h * r * g - tgt
        loss = 0.5 * jnp.sum(jnp.mean(err * err, axis=-1, keepdims=True), axis=0, keepdims=True)
        dx, dg = _rms_bwd_vals(h, err * (1.0 / D_MODEL), g)
        return dx, dx, jnp.broadcast_to(loss, (1, LANES)), dg
    d = h.shape[1]
    return _rowwise(fn, [h, target], [gain], [(d, F32), (d, BF16)], [(1, LANES), (1, d)], name=name)


def _adamw(w, g, m, v, *, name):
    rows, cols = w.shape
    tile = rows
    for cand in (256, 128, 64, 32, 16, 8):
        if rows % cand == 0 and rows > cand:
            tile = cand
            break
    bc1 = 1.0 - ADAM_B1 ** ADAM_STEP
    bc2 = 1.0 - ADAM_B2 ** ADAM_STEP

    def fn(w, g, m, v):
        m2 = ADAM_B1 * m + (1.0 - ADAM_B1) * g
        v2 = ADAM_B2 * v + (1.0 - ADAM_B2) * (g * g)
        delta = -ADAM_LR * ((m2 / bc1) / (jnp.sqrt(v2 / bc2) + ADAM_EPS) + ADAM_WD * w)
        return delta, m2, v2
    return _rowwise(fn, [w, g, m, v], [], [(cols, F32)] * 3, name=name, tile=tile)


def _attn_specs(arr, width, cb, classes, rows_block, whole):
    ncols = arr.shape[2] // (classes * width)
    if whole:
        return pl.BlockSpec((1, arr.shape[1], width), lambda n, r, i: (n, 0, r * ncols + cb))
    return pl.BlockSpec((1, rows_block, width), lambda n, r, i: (n, i, r * ncols + cb))


def _attn_tiles(mode, lq, lk):
    if mode == "full":
        return min(lq, 256), min(lq, 256), lk
    if mode == "band":
        return min(BAND_ROWS, lq), ATT_BLOCK, min(2 * ATT_BLOCK, lk)
    return CAUSAL_ROWS, CAUSAL_ROWS, ATT_CHUNK


def _window(mode, row0, j, rows, win, lk):
    if mode == "causal":
        return pl.multiple_of(j * win, win), row0 // win + 1
    if mode == "band":
        return pl.multiple_of(jnp.clip(row0 + rows - win, 0, lk - win), ATT_BLOCK), 1
    return 0, 1


def _allowed(mode, row0, start, rows, win, max_dist):
    if mode == "full":
        return None
    dist = (row0 + lax.broadcasted_iota(jnp.int32, (rows, win), 0)) - (start + lax.broadcasted_iota(jnp.int32, (rows, win), 1))
    ok = dist >= 0
    if max_dist is not None:
        ok = ok & (dist <= max_dist)
    return ok


def _head_groups(heads, dh):
    gw = max(LANES, dh)
    return gw, gw // dh, heads // (gw // dh)


def _own_lanes(rows, gw, dh, u):
    return lax.broadcasted_iota(jnp.int32, (rows, gw), 1) // dh == u


def _only_head(x, own, per, u):
    return x if per == 1 else jnp.where(own[u], x, jnp.zeros_like(x))


def _step_scores(qz, kw, kb_row, ok, scale):
    s = lax.dot_general(qz, kw, (((1,), (1,)), ((), ())), preferred_element_type=F32) * scale
    if kb_row is not None:
        s = s - kb_row
    if ok is not None:
        s = jnp.where(ok, s, MASKED)
    return s


def _attn_fwd(q, k, v, kb=None, *, classes=1, heads, dh, mode, max_dist=None, name):
    (qa, qc), (ka, kc), (va, vc) = q, k, v
    n, lq, lk = qa.shape[0], qa.shape[1], ka.shape[1]
    width = heads * dh
    tq, rows, win = _attn_tiles(mode, lq, lk)
    gw, per, groups = _head_groups(heads, dh)
    scale = dh ** -0.5
    has_kb = kb is not None
    single = mode != "causal"

    def body(*refs):
        for sub in range(tq // rows):
            part(refs, pl.program_id(2) * tq + sub * rows, slice(sub * rows, (sub + 1) * rows))

    def part(refs, row0, rs):
        q_ref, k_ref, v_ref = refs[:3]
        kb_ref = refs[3] if has_kb else None
        o_ref, lse_ref = refs[-2:]
        lane = lax.broadcasted_iota(jnp.int32, (rows, LANES), 1)
        own = [_own_lanes(rows, gw, dh, u) for u in range(per)]

        def step(j, carry):
            m_in, l_in = carry
            m_out, l_out = m_in, l_in
            start, _ = _window(mode, row0, j, rows, win, lk)
            ok = _allowed(mode, row0, start, rows, win, max_dist)
            for g in range(groups):
                cols = slice(g * gw, (g + 1) * gw)
                qg = q_ref[0, rs, cols]
                kw = k_ref[0, pl.ds(start, win), cols]
                vw = v_ref[0, pl.ds(start, win), cols]
                alpha_g = new_g = None
                for u in range(per):
                    h = g * per + u
                    kb_row = kb_ref[0, h, pl.ds(j, 1), :] if has_kb else None
                    s = _step_scores(_only_head(qg, own, per, u), kw, kb_row, ok, scale)
                    m_new = jnp.max(s, axis=1, keepdims=True)
                    if not single:
                        m_old = m_in[:, h:h + 1]
                        m_new = jnp.maximum(m_old, m_new)
                        alpha = jnp.exp(m_old - m_new)
                        alpha_g = alpha if u == 0 else jnp.where(own[u], alpha, alpha_g)
                    p = jnp.exp(s - m_new)
                    l_new = jnp.sum(p, axis=1, keepdims=True)
                    r = jnp.dot(p.astype(BF16), vw, preferred_element_type=F32)
                    if single:
                        r = r * (1.0 / l_new)
                    else:
                        l_new = alpha * l_in[:, h:h + 1] + l_new
                    new_g = r if u == 0 else jnp.where(own[u], r, new_g)
                    m_out = jnp.where(lane == h, m_new, m_out)
                    l_out = jnp.where(lane == h, l_new, l_out)
                o_ref[0, rs, cols] = new_g if single else alpha_g * o_ref[0, rs, cols] + new_g
            return m_out, l_out

        carry = (jnp.full((rows, LANES), MASKED, F32), jnp.zeros((rows, LANES), F32))
        if single:
            m_all, l_all = step(0, carry)
            l_all = jnp.where(lane < heads, l_all, 1.0)
        else:
            o_ref[...] = jnp.zeros(o_ref.shape, F32)
            m_all, l_all = lax.fori_loop(0, _window(mode, row0, 0, rows, win, lk)[1], step, carry)
            l_all = jnp.where(lane < heads, l_all, 1.0)
            inv = 1.0 / l_all
            for g in range(groups):
                cols = slice(g * gw, (g + 1) * gw)
                inv_g = inv[:, g * per:g * per + 1]
                for u in range(1, per):
                    inv_g = jnp.where(own[u], inv[:, g * per + u:g * per + u + 1], inv_g)
                o_ref[0, rs, cols] = o_ref[0, rs, cols] * inv_g
        lse_ref[0, rs, :] = jnp.where(lane < heads, m_all + jnp.log(l_all), 0.0)

    in_specs = [_attn_specs(qa, width, qc, classes, tq, False), _attn_specs(ka, width, kc, classes, win, True),
                _attn_specs(va, width, vc, classes, win, True)]
    args = [qa, ka, va]
    if has_kb:
        in_specs.append(pl.BlockSpec((1,) + kb.shape[1:], lambda n_, r, i: (n_, 0, 0, 0)))
        args.append(kb)
    out_shape = [jax.ShapeDtypeStruct((n, lq, classes * width), F32), jax.ShapeDtypeStruct((n, lq, classes * LANES), F32)]
    out_specs = [pl.BlockSpec((1, tq, width), lambda n_, r, i: (n_, i, r)), pl.BlockSpec((1, tq, LANES), lambda n_, r, i: (n_, i, r))]
    return pl.pallas_call(
        body, grid=(n, classes, lq // tq), in_specs=in_specs, out_specs=out_specs, out_shape=out_shape, name=name,
        compiler_params=_params("parallel", "parallel", "arbitrary"),
    )(*args)


def _attn_bwd(q, k, v, do, lse, delta, kb=None, *, classes=1, heads, dh, mode, max_dist=None, dq_dtype=F32, name):
    (qa, qc), (ka, kc), (va, vc), (da, dc) = q, k, v, do
    n, lq, lk = qa.shape[0], qa.shape[1], ka.shape[1]
    width = heads * dh
    tq, rows, win = _attn_tiles(mode, lq, lk)
    gw, per, groups = _head_groups(heads, dh)
    scale = dh ** -0.5
    has_kb = kb is not None
    single = mode != "causal"
    nt = (((1,), (1,)), ((), ()))
    tn = (((0,), (0,)), ((), ()))

    def body(*refs):
        n_in = 7 if has_kb else 6
        dk_ref, dv_ref = refs[n_in + 1:n_in + 3]

        @pl.when(pl.program_id(2) == 0)
        def _():
            dk_ref[...] = jnp.zeros(dk_ref.shape, F32)
            dv_ref[...] = jnp.zeros(dv_ref.shape, F32)
            if has_kb:
                refs[n_in + 3][...] = jnp.zeros(refs[n_in + 3].shape, F32)

        for sub in range(tq // rows):
            part(refs, pl.program_id(2) * tq + sub * rows, slice(sub * rows, (sub + 1) * rows))

    def part(refs, row0, rs):
        q_ref, k_ref, v_ref, do_ref, lse_ref, dl_ref = refs[:6]
        kb_ref = refs[6] if has_kb else None
        n_in = 7 if has_kb else 6
        dq_ref, dk_ref, dv_ref = refs[n_in:n_in + 3]
        dkb_ref, drow_ref = refs[n_in + 3:n_in + 5] if has_kb else (None, None)
        lse_all = lse_ref[0, rs, :]
        dl_all = dl_ref[0, rs, :]
        lane = lax.broadcasted_iota(jnp.int32, (rows, LANES), 1)
        own = [_own_lanes(rows, gw, dh, u) for u in range(per)]

        def step(j, drow_all):
            start, _ = _window(mode, row0, j, rows, win, lk)
            ok = _allowed(mode, row0, start, rows, win, max_dist)
            for g in range(groups):
                cols = slice(g * gw, (g + 1) * gw)
                qg = q_ref[0, rs, cols]
                dog = do_ref[0, rs, cols]
                kw = k_ref[0, pl.ds(start, win), cols]
                vw = v_ref[0, pl.ds(start, win), cols]
                dq_g = dk_w = dv_w = None
                for u in range(per):
                    h = g * per + u
                    qz, doz = _only_head(qg, own, per, u), _only_head(dog, own, per, u)
                    kb_row = kb_ref[0, h, pl.ds(j, 1), :] if has_kb else None
                    p = jnp.exp(_step_scores(qz, kw, kb_row, ok, scale) - lse_all[:, h:h + 1])
                    dp = lax.dot_general(doz, vw, nt, preferred_element_type=F32)
                    ds = p * (dp - dl_all[:, h:h + 1])
                    dsb = ds.astype(BF16)
                    r = jnp.dot(dsb, kw, preferred_element_type=F32)
                    dq_g = r if u == 0 else jnp.where(own[u], r, dq_g)
                    dk_u = lax.dot_general(dsb, qz, tn, preferred_element_type=F32)
                    dv_u = lax.dot_general(p.astype(BF16), doz, tn, preferred_element_type=F32)
                    dk_w = dk_u if u == 0 else dk_w + dk_u
                    dv_w = dv_u if u == 0 else dv_w + dv_u
                    if has_kb:
                        dkb_ref[0, h, pl.ds(j, 1), :] += -jnp.sum(ds, axis=0, keepdims=True)
                        drow_all = drow_all + jnp.where(lane == h, jnp.sum(ds, axis=1, keepdims=True), 0.0)
                dk_ref[0, pl.ds(start, win), cols] += dk_w * scale
                dv_ref[0, pl.ds(start, win), cols] += dv_w
                if single:
                    dq_ref[0, rs, cols] = (dq_g * scale).astype(dq_ref.dtype)
                else:
                    dq_ref[0, rs, cols] += dq_g * scale
            return drow_all

        drow_all = jnp.zeros((rows, LANES), F32)
        if single:
            drow_all = step(0, drow_all)
        else:
            dq_ref[...] = jnp.zeros(dq_ref.shape, F32)
            drow_all = lax.fori_loop(0, _window(mode, row0, 0, rows, win, lk)[1], step, drow_all)
        if has_kb:
            drow_ref[0, rs, :] = drow_all

    row_spec = functools.partial(_attn_specs, classes=classes, rows_block=tq, whole=False)
    in_specs = [row_spec(qa, width, qc), _attn_specs(ka, width, kc, classes, win, True), _attn_specs(va, width, vc, classes, win, True),
                row_spec(da, width, dc), row_spec(lse, LANES, 0), row_spec(delta, LANES, 0)]
    args = [qa, ka, va, da, lse, delta]
    assert single or dq_dtype == F32
    out_shape = [jax.ShapeDtypeStruct((n, lq, classes * width), dq_dtype), jax.ShapeDtypeStruct((n, lk, classes * width), F32),
                 jax.ShapeDtypeStruct((n, lk, classes * width), F32)]
    kv_spec = pl.BlockSpec((1, lk, width), lambda n_, r, i: (n_, 0, r))
    out_specs = [pl.BlockSpec((1, tq, width), lambda n_, r, i: (n_, i, r)), kv_spec, kv_spec]
    if has_kb:
        kb_spec = pl.BlockSpec((1,) + kb.shape[1:], lambda n_, r, i: (n_, 0, 0, 0))
        in_specs.append(kb_spec)
        args.append(kb)
        out_shape += [jax.ShapeDtypeStruct(kb.shape, F32), jax.ShapeDtypeStruct((n, lq, LANES), F32)]
        out_specs += [kb_spec, pl.BlockSpec((1, tq, LANES), lambda n_, r, i: (n_, i, 0))]
    return pl.pallas_call(
        body, grid=(n, classes, lq // tq), in_specs=in_specs, out_specs=out_specs, out_shape=out_shape, name=name,
        compiler_params=_params("parallel", "parallel", "arbitrary"),
    )(*args)


def _head_delta(do, out, heads, *, name, lse=None, sink=None):
    width = out.shape[1]
    gather = _head_expand(heads, width).T

    if sink is None:
        return _rowwise(lambda do, o, e: _dot_exact(do * o, e), [do, out], [gather], [(LANES, F32)], name=name)[0]

    def fn(do, o, lse, e, sink):
        delta = _dot_exact(do * o, e)
        return delta, -jnp.sum(jnp.exp(sink - lse) * delta, axis=0, keepdims=True)
    return _rowwise(fn, [do, out, lse], [gather, sink], [(LANES, F32)], [(1, LANES)], name=name)


def _rope_tables():
    half = HEAD_DIM // 2
    inv = ROPE_THETA ** (-jnp.arange(half, dtype=F32) / half)
    ang = jnp.arange(SEQ, dtype=F32)[:, None] * inv[None, :]
    cos = jnp.tile(jnp.cos(ang), (1, 2 * ATT_W // HEAD_DIM))
    sin = jnp.tile(jnp.concatenate([-jnp.sin(ang), jnp.sin(ang)], axis=1), (1, ATT_W // HEAD_DIM))
    return cos, sin


def _rotate(x, cos, sin):
    width = x.shape[1]
    lane = lax.broadcasted_iota(jnp.int32, x.shape, 1)
    first_half = (lane % HEAD_DIM) < (HEAD_DIM // 2)
    swapped = jnp.where(first_half, pltpu.roll(x, width - HEAD_DIM // 2, axis=1), pltpu.roll(x, HEAD_DIM // 2, axis=1))
    return x * cos[:, :width] + swapped * sin[:, :width]


def _gelu(x):
    k = math.sqrt(2.0 / math.pi)
    t = jnp.tanh(k * (x + 0.044715 * x * x * x))
    return 0.5 * x * (1.0 + t), t


def _gelu_grad(x, t):
    k = math.sqrt(2.0 / math.pi)
    return 0.5 * (1.0 + t) + 0.5 * x * (1.0 - t * t) * k * (1.0 + 3.0 * 0.044715 * x * x)


def _shift_down(x, halo, s):
    ext = jnp.concatenate([halo, x], axis=0)
    return pltpu.roll(ext, s, axis=0)[8:, :]


def _shift_up(x, halo, s):
    rows = x.shape[0]
    ext = jnp.concatenate([x, halo], axis=0)
    return pltpu.roll(ext, rows + 8 - s, axis=0)[:rows, :]


def _lru_gates(u, halo, cw, cb, wa, ba, wi, bi, lam):
    taps = [_shift_down(u, halo, CONV_WIDTH - 1 - k) for k in range(CONV_WIDTH - 1)] + [u]
    uc = cb + sum(cw[k:k + 1, :] * taps[k] for k in range(CONV_WIDTH))
    ucb = uc.astype(BF16)
    r = jax.nn.sigmoid(jnp.dot(ucb, wa, preferred_element_type=F32) + ba)
    gi = jax.nn.sigmoid(jnp.dot(ucb, wi, preferred_element_type=F32) + bi)
    sp = jnp.maximum(-lam, 0.0) + jnp.log(1.0 + jnp.exp(-jnp.abs(lam)))
    log_a = -LRU_C * r * sp
    a = jnp.exp(log_a)
    x2 = 2.0 * log_a
    one_minus_a2 = jnp.where(x2 > -0.01, -x2 * (1.0 + x2 * (0.5 + x2 * (1.0 / 6.0 + x2 / 24.0))), 1.0 - jnp.exp(x2))
    mult = jnp.sqrt(one_minus_a2)
    return taps, uc, ucb, r, gi, sp, a, mult


def _lru_specs(nchunk, reverse):
    def chunk(b, c):
        return b * nchunk + ((nchunk - 1 - c) if reverse else c)

    def halo_before(b, c):
        return jnp.maximum(chunk(b, c) * (LRU_CHUNK // 8) - 1, 0)

    return chunk, halo_before


def _lru_fwd(zug, cw, cb, wa, ba, wi, bi, lam, *, name):
    total = zug.shape[0]
    nchunk = SEQ // LRU_CHUNK
    w = LRU_WIDTH
    chunk, halo_before = _lru_specs(nchunk, False)

    def body(u_ref, uh_ref, g_ref, cw_ref, cb_ref, wa_ref, ba_ref, wi_ref, bi_ref, lam_ref, y_ref, h_ref, a_sc, x_sc, carry_sc):
        c = pl.program_id(1)
        u = u_ref[...]
        halo = jnp.where(c > 0, uh_ref[...], 0.0)
        _, uc, _, _, gi, _, a, mult = _lru_gates(u, halo, cw_ref[...], cb_ref[...], wa_ref[...], ba_ref[...],
                                                 wi_ref[...], bi_ref[...], lam_ref[...])
        a_sc[...] = a
        x_sc[...] = mult * (gi * uc)

        @pl.when(c == 0)
        def _():
            carry_sc[...] = jnp.zeros(carry_sc.shape, F32)

        def tile_step(t, h):
            r0 = pl.multiple_of(t * 8, 8)
            at = a_sc[pl.ds(r0, 8), :]
            xt = x_sc[pl.ds(r0, 8), :]
            rows = []
            for j in range(8):
                h = at[j:j + 1, :] * h + xt[j:j + 1, :]
                rows.append(h)
            h_ref[pl.ds(r0, 8), :] = jnp.concatenate(rows, axis=0)
            return h

        h_last = lax.fori_loop(0, LRU_CHUNK // 8, tile_step, carry_sc[0:1, :])
        carry_sc[0:1, :] = h_last
        gel, _ = _gelu(g_ref[...])
        y_ref[...] = (h_ref[...] * gel).astype(BF16)

    small = lambda arr: pl.BlockSpec(arr.shape, lambda b, c: (0, 0))
    return pl.pallas_call(
        body, grid=(total // SEQ, nchunk),
        in_specs=[pl.BlockSpec((LRU_CHUNK, w), lambda b, c: (chunk(b, c), 0)),
                  pl.BlockSpec((8, w), lambda b, c: (halo_before(b, c), 0)),
                  pl.BlockSpec((LRU_CHUNK, w), lambda b, c: (chunk(b, c), 1)),
                  small(cw), small(cb), small(wa), small(ba), small(wi), small(bi), small(lam)],
        out_specs=[pl.BlockSpec((LRU_CHUNK, w), lambda b, c: (chunk(b, c), 0))] * 2,
        out_shape=[jax.ShapeDtypeStruct((total, w), BF16), jax.ShapeDtypeStruct((total, w), F32)],
        scratch_shapes=[pltpu.VMEM((LRU_CHUNK, w), F32), pltpu.VMEM((LRU_CHUNK, w), F32), pltpu.VMEM((8, w), F32)],
        name=name, compiler_params=_params("arbitrary", "arbitrary"),
    )(zug, zug, zug, cw, cb, wa, ba, wi, bi, lam)


def _lru_bwd(zug, hl, dy, cw, cb, wa, ba, wi, bi, lam, *, name):
    total = zug.shape[0]
    nchunk = SEQ // LRU_CHUNK
    w = LRU_WIDTH
    chunk, halo_before = _lru_specs(nchunk, True)
    tn = (((0,), (0,)), ((), ()))
    nt = (((1,), (1,)), ((), ()))

    def body(u_ref, uh_ref, g_ref, hl_ref, hh_ref, dy_ref, cw_ref, cb_ref, wa_ref, ba_ref, wi_ref, bi_ref, lam_ref,
             dz_ref, dcw_ref, dcb_ref, dwa_ref, dba_ref, dwi_ref, dbi_ref, dlam_ref,
             a_sc, d_sc, g_sc, gcarry_sc, acarry_sc, duc_sc):
        b, c = pl.program_id(0), pl.program_id(1)
        first_chunk = c == nchunk - 1

        @pl.when((b == 0) & (c == 0))
        def _():
            for ref in (dcw_ref, dcb_ref, dwa_ref, dba_ref, dwi_ref, dbi_ref, dlam_ref):
                ref[...] = jnp.zeros(ref.shape, F32)

        @pl.when(c == 0)
        def _():
            gcarry_sc[...] = jnp.zeros(gcarry_sc.shape, F32)
            acarry_sc[...] = jnp.zeros(acarry_sc.shape, F32)
            duc_sc[...] = jnp.zeros(duc_sc.shape, F32)

        u = u_ref[...]
        halo = jnp.where(first_chunk, 0.0, uh_ref[...])
        cw, wa, wi, lam = cw_ref[...], wa_ref[...], wi_ref[...], lam_ref[...]
        taps, uc, ucb, r, gi, sp, a, mult = _lru_gates(u, halo, cw, cb_ref[...], wa, ba_ref[...], wi, bi_ref[...], lam)
        gate = g_ref[...]
        gel, th = _gelu(gate)
        dy = dy_ref[...]
        hl = hl_ref[...]
        a_sc[...] = a
        d_sc[...] = dy * gel
        dgate = dy * hl * _gelu_grad(gate, th)

        def tile_step(t, carry):
            g_next, a_next = carry
            r0 = pl.multiple_of((LRU_CHUNK // 8 - 1 - t) * 8, 8)
            dt = d_sc[pl.ds(r0, 8), :]
            at = a_sc[pl.ds(r0, 8), :]
            rows = [None] * 8
            for j in reversed(range(8)):
                g_next = dt[j:j + 1, :] + a_next * g_next
                a_next = at[j:j + 1, :]
                rows[j] = g_next
            g_sc[pl.ds(r0, 8), :] = jnp.concatenate(rows, axis=0)
            return g_next, a_next

        g_last, a_last = lax.fori_loop(0, LRU_CHUNK // 8, tile_step, (gcarry_sc[0:1, :], acarry_sc[0:1, :]))
        gcarry_sc[0:1, :] = g_last
        acarry_sc[0:1, :] = a_last

        gs = g_sc[...]
        hl_halo = jnp.where(first_chunk, 0.0, hh_ref[...])
        da = gs * _shift_down(hl, hl_halo, 1)
        dmult = gs * gi * uc
        dgi = gs * mult * uc
        duc = gs * mult * gi
        dlog_a = da * a - dmult * a * a / mult
        dr = dlog_a * (-LRU_C * sp)
        dsp = jnp.sum(dlog_a * (-LRU_C * r), axis=0, keepdims=True)
        dlam_ref[...] += -dsp * jax.nn.sigmoid(-lam)
        dpa = dr * r * (1.0 - r)
        dpi = dgi * gi * (1.0 - gi)
        dpab, dpib = dpa.astype(BF16), dpi.astype(BF16)
        dba_ref[...] += jnp.sum(dpa, axis=0, keepdims=True)
        dbi_ref[...] += jnp.sum(dpi, axis=0, keepdims=True)
        dwa_ref[...] += lax.dot_general(ucb, dpab, tn, preferred_element_type=F32)
        dwi_ref[...] += lax.dot_general(ucb, dpib, tn, preferred_element_type=F32)
        duc = duc + lax.dot_general(dpab, wa, nt, preferred_element_type=F32)
        duc = duc + lax.dot_general(dpib, wi, nt, preferred_element_type=F32)
        dcb_ref[...] += jnp.sum(duc, axis=0, keepdims=True)
        dcw_ref[...] += jnp.concatenate([jnp.sum(duc * taps[k], axis=0, keepdims=True) for k in range(CONV_WIDTH)], axis=0)
        after = duc_sc[...]
        du = cw[CONV_WIDTH - 1:CONV_WIDTH, :] * duc
        for k in range(CONV_WIDTH - 1):
            du = du + cw[k:k + 1, :] * _shift_up(duc, after, CONV_WIDTH - 1 - k)
        duc_sc[...] = duc[0:8, :]
        dz_ref[:, 0:w] = du.astype(BF16)
        dz_ref[:, w:2 * w] = dgate.astype(BF16)

    small = lambda arr: pl.BlockSpec(arr.shape, lambda b, c: (0, 0))
    acc = lambda shape: pl.BlockSpec(shape, lambda b, c: (0, 0))
    chunk_spec = lambda cb_: pl.BlockSpec((LRU_CHUNK, w), lambda b, c: (chunk(b, c), cb_))
    halo_spec = pl.BlockSpec((8, w), lambda b, c: (halo_before(b, c), 0))
    acc_shapes = [(CONV_WIDTH, w), (1, w), (w, w), (1, w), (w, w), (1, w), (1, w)]
    return pl.pallas_call(
        body, grid=(total // SEQ, nchunk),
        in_specs=[chunk_spec(0), halo_spec, chunk_spec(1), chunk_spec(0), halo_spec, chunk_spec(0),
                  small(cw), small(cb), small(wa), small(ba), small(wi), small(bi), small(lam)],
        out_specs=[pl.BlockSpec((LRU_CHUNK, 2 * w), lambda b, c: (chunk(b, c), 0))] + [acc(s) for s in acc_shapes],
        out_shape=[jax.ShapeDtypeStruct((total, 2 * w), BF16)] + [jax.ShapeDtypeStruct(s, F32) for s in acc_shapes],
        scratch_shapes=[pltpu.VMEM((LRU_CHUNK, w), F32)] * 3 + [pltpu.VMEM((8, w), F32)] * 3,
        name=name, compiler_params=_params("arbitrary", "arbitrary"),
    )(zug, zug, zug, hl, hl, dy, cw, cb, wa, ba, wi, bi, lam)


def _block_diag(wb):
    eye = jnp.eye(LRU_BLOCKS, dtype=wb.dtype)
    return jnp.einsum("gcd,gh->gchd", wb, eye).reshape(LRU_WIDTH, LRU_WIDTH).astype(BF16)


def _diag_blocks(wd):
    blk = LRU_WIDTH // LRU_BLOCKS
    return jnp.stack([wd[g * blk:(g + 1) * blk, g * blk:(g + 1) * blk] for g in range(LRU_BLOCKS)])


FOX_SCAN = 256


def _fox_bias(fl, bf, *, name):
    nb = fl.shape[0]

    def body(fl_ref, bf_ref, c_ref):
        x = fl_ref[0] + bf_ref[...]
        logf = jnp.minimum(x, 0.0) - jnp.log(1.0 + jnp.exp(-jnp.abs(x)))
        upper = (lax.broadcasted_iota(jnp.int32, (FOX_SCAN, FOX_SCAN), 0)
                 <= lax.broadcasted_iota(jnp.int32, (FOX_SCAN, FOX_SCAN), 1)).astype(BF16)
        carry = jnp.zeros((LRU_BLOCKS, 1), F32)
        for j in range(SEQ // FOX_SCAN):
            blk = logf[:, j * FOX_SCAN:(j + 1) * FOX_SCAN]
            c_ref[0, :, j * FOX_SCAN:(j + 1) * FOX_SCAN] = _dot_exact(blk, upper) + carry
            carry = carry + jnp.sum(blk, axis=1, keepdims=True)

    return pl.pallas_call(
        body, grid=(nb,),
        in_specs=[pl.BlockSpec((1, 8, SEQ), lambda b: (b, 0, 0)), pl.BlockSpec((8, 1), lambda b: (0, 0))],
        out_specs=pl.BlockSpec((1, 8, SEQ), lambda b: (b, 0, 0)),
        out_shape=jax.ShapeDtypeStruct((nb, 8, SEQ), F32), name=name, compiler_params=_params("arbitrary"),
    )(fl, bf)


def _fox_bias_bwd(fl, bf, dc, *, name):
    nb = fl.shape[0]

    def body(fl_ref, bf_ref, dc_ref, dfl_ref, dbf_ref):
        @pl.when(pl.program_id(0) == 0)
        def _():
            dbf_ref[...] = jnp.zeros(dbf_ref.shape, F32)

        x = fl_ref[0] + bf_ref[...]
        dc_all = dc_ref[0]
        lower = (lax.broadcasted_iota(jnp.int32, (FOX_SCAN, FOX_SCAN), 0)
                 >= lax.broadcasted_iota(jnp.int32, (FOX_SCAN, FOX_SCAN), 1)).astype(BF16)
        carry = jnp.zeros((LRU_BLOCKS, 1), F32)
        total = jnp.zeros((LRU_BLOCKS, 1), F32)
        for j in reversed(range(SEQ // FOX_SCAN)):
            cols = slice(j * FOX_SCAN, (j + 1) * FOX_SCAN)
            blk = dc_all[:, cols]
            dlogf = _dot_exact(blk, lower) + carry
            carry = carry + jnp.sum(blk, axis=1, keepdims=True)
            dx = dlogf * jax.nn.sigmoid(-x[:, cols])
            dfl_ref[0, :, cols] = dx
            total = total + jnp.sum(dx, axis=1, keepdims=True)
        dbf_ref[...] += total

    return pl.pallas_call(
        body, grid=(nb,),
        in_specs=[pl.BlockSpec((1, 8, SEQ), lambda b: (b, 0, 0)), pl.BlockSpec((8, 1), lambda b: (0, 0)),
                  pl.BlockSpec((1, 8, SEQ), lambda b: (b, 0, 0))],
        out_specs=[pl.BlockSpec((1, 8, SEQ), lambda b: (b, 0, 0)), pl.BlockSpec((8, 1), lambda b: (0, 0))],
        out_shape=[jax.ShapeDtypeStruct((nb, 8, SEQ), F32), jax.ShapeDtypeStruct((8, 1), F32)],
        name=name, compiler_params=_params("arbitrary"),
    )(fl, bf, dc)


def _seq3(a, nb):
    return a.reshape(nb, a.shape[0] // nb, a.shape[1])


def _flat2(a):
    return a.reshape(a.shape[0] * a.shape[1], a.shape[2])


def _residual_out(y, w, h, next_gain, *, name):
    if next_gain is None:
        out, = _matmul(y, w, extras=(h,), epilogue=lambda acc, res: (acc + res,), name=name)
        return out, None

    def epilogue(acc, res, g):
        out = acc + res
        return out, out * _rstd(out) * g
    return _matmul(y, w, outs=(F32, BF16), extras=(h,), consts=(next_gain,), epilogue=epilogue, whole_rows=True, name=name)


def _mlp_fwd(h, hn, p, tag, next_gain):
    act, = _matmul(hn, p["w_up_t"], tb=True, outs=(BF16,), epilogue=lambda acc: (jnp.square(jnp.maximum(acc, 0.0)),),
                   name=f"{tag}_up")
    out, hn_next = _residual_out(act, p["w_down"], h, next_gain, name=f"{tag}_down")
    return out, hn_next, (h, hn, act)


def _mlp_bwd(dh, dhb, p, saved, tag):
    h, hn, act = saved
    dup, = _matmul(dhb, p["w_down"], tb=True, outs=(BF16,), extras=(act,),
                   epilogue=lambda acc, act: (acc * (2.0 * jnp.sqrt(act.astype(F32))),), name=f"{tag}_bwd_dup")
    dw_down, = _matmul(act, dhb, ta=True, outs=(BF16,),name=f"{tag}_bwd_dwdown")
    dw_up_t, = _matmul(dup, hn, ta=True, outs=(BF16,),name=f"{tag}_bwd_dwup")
    dh2, dh2b, dg = _norm_input_grad(dup, p["w_up_t"], h, dh, p["norm"], name=f"{tag}_bwd_dh")
    return dh2, dh2b, {"norm": dg, "w_up_t": dw_up_t, "w_down": dw_down}


def _xa_fwd(h, hn, mem, p, tag, nb, next_gain):
    mem_n = _rms_fwd(mem, p["mem_norm"], name=f"{tag}_memnorm")
    q, = _matmul(hn, p["w_q"], outs=(BF16,), name=f"{tag}_q")
    kv, = _matmul(mem_n, p["w_kv_t"], tb=True, outs=(BF16,), name=f"{tag}_kv")
    q3, kv3 = _seq3(q, nb), _seq3(kv, nb)
    o, lse = _attn_fwd((q3, 0), (kv3, 0), (kv3, 1), heads=XA_HEADS, dh=XA_HEAD_DIM, mode="full",
                       name=f"{tag}_attn")
    o = _flat2(o)
    ob, = _rowwise(lambda o: o, [o], [], [(D_MODEL, BF16)], name=f"{tag}_ocast")
    out, hn_next = _residual_out(ob, p["w_o"], h, next_gain, name=f"{tag}_o")
    return out, hn_next, (h, hn, mem_n, q3, kv3, o, ob, lse)


def _xa_bwd(dh, dhb, mem, p, saved, tag, nb):
    h, hn, mem_n, q3, kv3, o, ob, lse = saved
    dob, delta = _matmul(dhb, p["w_o"], tb=True, outs=(BF16, (F32, LANES)), extras=(o,), consts=(_head_expand(XA_HEADS, D_MODEL).T,),
                         epilogue=lambda acc, o, e: (acc, _dot_exact(acc * o, e)), name=f"{tag}_bwd_do")
    dw_o, = _matmul(ob, dhb, ta=True, outs=(BF16,),name=f"{tag}_bwd_dwo")
    dq, dk, dv = _attn_bwd((q3, 0), (kv3, 0), (kv3, 1), (_seq3(dob, nb), 0), lse, _seq3(delta, nb), heads=XA_HEADS,
                           dh=XA_HEAD_DIM, mode="full", dq_dtype=BF16, name=f"{tag}_bwd_attn")
    dqb = _flat2(dq)
    dkvb, = _rowwise(lambda a, b: jnp.concatenate([a, b], axis=1), [_flat2(dk), _flat2(dv)], [], [(2 * D_MODEL, BF16)],
                     name=f"{tag}_bwd_dkvcast")
    dw_q, = _matmul(hn, dqb, ta=True, outs=(BF16,),name=f"{tag}_bwd_dwq")
    dw_kv_t, = _matmul(dkvb, mem_n, ta=True, outs=(BF16,),name=f"{tag}_bwd_dwkv")
    dmem_n, = _matmul(dkvb, p["w_kv_t"], name=f"{tag}_bwd_dmemn")
    dg_mem, = _rowwise(lambda x, d, g: _rms_bwd_vals(x, d, g)[1], [mem, dmem_n], [p["mem_norm"]], [], [(1, D_MODEL)],
                       name=f"{tag}_bwd_memnorm")
    dh2, dh2b, dg = _norm_input_grad(dqb, p["w_q"], h, dh, p["norm"], tb=True, name=f"{tag}_bwd_dh")
    return dh2, dh2b, {"norm": dg, "mem_norm": dg_mem, "w_q": dw_q, "w_kv_t": dw_kv_t, "w_o": dw_o}


AB_MAIN = 2 * LRU_WIDTH + 3 * ATT_W


def _ab_fwd(h, hn, p, nb, next_gain):
    w_in_t = p["w_in_t"]
    zug, = _matmul(hn, w_in_t[:2 * LRU_WIDTH], tb=True, name="ab_in_ug")
    qkv, = _matmul(hn, w_in_t[2 * LRU_WIDTH:AB_MAIN], tb=True, outs=(BF16,), tn=768, name="ab_in_qkv")
    zf, = _matmul(hn, w_in_t[AB_MAIN:], tb=True, name="ab_in_f")
    fl = zf[:, :8].reshape(nb, SEQ, 8).transpose(0, 2, 1)
    cbias = _fox_bias(fl, p["b_f"], name="ab_fox_bias")
    kb = cbias.reshape(nb, 8, SEQ // ATT_CHUNK, ATT_CHUNK)
    y_a, hl = _lru_fwd(zug, p["conv_w"], p["conv_b"], p["w_a"], p["b_a"], p["w_i"], p["b_i"], p["lam"], name="ab_lru")
    qkv3 = _seq3(qkv, nb)
    o, lse = _attn_fwd((qkv3, 0), (qkv3, 1), (qkv3, 2), kb, heads=8, dh=HEAD_DIM, mode="causal", name="ab_fox")
    o = _flat2(o)
    y, = _rowwise(lambda a, b: jnp.concatenate([a, b.astype(BF16)], axis=1), [y_a, o], [], [(D_MODEL, BF16)], name="ab_ycat")
    out, hn_next = _residual_out(y, p["w_out"], h, next_gain, name="ab_out")
    return out, hn_next, (h, hn, zug, qkv3, fl, kb, hl, o, lse, y)


def _ab_bwd(dh, dhb, p, saved, nb):
    h, hn, zug, qkv3, fl, kb, hl, o, lse, y = saved
    dy, dyb, delta = _matmul(dhb, p["w_out"], tb=True, outs=(F32, BF16, (F32, LANES)), extras=(y,), consts=(_head_expand(8, ATT_W).T,),
                             epilogue=lambda acc, y, e: (acc, acc, _dot_exact(acc[:, ATT_W:] * y[:, ATT_W:].astype(F32), e)),
                             name="ab_bwd_dy")
    dw_out, = _matmul(y, dhb, ta=True, outs=(BF16,),name="ab_bwd_dwout")
    dzug, dcw, dcb, dwa, dba, dwi, dbi, dlam = _lru_bwd(zug, hl, dy, p["conv_w"], p["conv_b"], p["w_a"], p["b_a"],
                                                        p["w_i"], p["b_i"], p["lam"], name="ab_bwd_lru")
    dq, dk, dv, dkb, drow = _attn_bwd((qkv3, 0), (qkv3, 1), (qkv3, 2), (_seq3(dyb, nb), 1), lse, _seq3(delta, nb), kb,
                                      heads=8, dh=HEAD_DIM, mode="causal", name="ab_bwd_fox")
    dcum = dkb.reshape(nb, 8, SEQ) + drow[:, :, :8].transpose(0, 2, 1)
    dfl, dbf = _fox_bias_bwd(fl, p["b_f"], dcum, name="ab_bwd_fox_bias")
    dzf = jnp.pad(dfl.transpose(0, 2, 1).reshape(nb * SEQ, 8), ((0, 0), (0, LANES - 8)))
    dz, = _rowwise(lambda a, q, k, v, f: jnp.concatenate([a, q.astype(BF16), k.astype(BF16), v.astype(BF16), f.astype(BF16)], axis=1),
                   [dzug, _flat2(dq), _flat2(dk), _flat2(dv), dzf], [], [(AB_MAIN + LANES, BF16)], name="ab_bwd_dzcat")
    dw_in_t, = _matmul(dz, hn, ta=True, outs=(BF16,),tm=384, name="ab_bwd_dwin")
    dh2, dh2b, dg = _norm_input_grad(dz, p["w_in_t"], h, dh, p["norm"], name="ab_bwd_dh")
    grads = {"norm": dg, "w_in_t": dw_in_t[:AB_MAIN + 8], "conv_w": dcw, "conv_b": dcb, "w_a": _diag_blocks(dwa), "b_a": dba,
             "w_i": _diag_blocks(dwi), "b_i": dbi, "lam": dlam, "b_f": dbf.reshape(1, 8), "w_out": dw_out}
    return dh2, dh2b, grads


CD_IN = 3 * ATT_W + ATT_W + 2 * SWA_KW


def _gqa_share():
    src = jnp.arange(SWA_KW)[:, None]
    dst = jnp.arange(ATT_W)[None, :]
    per_kv = ATT_W // (SWA_KW // HEAD_DIM)
    return ((dst // per_kv == src // HEAD_DIM) & (dst % HEAD_DIM == src % HEAD_DIM)).astype(BF16)


def _cd_fwd(h, hn, p, nb, next_gain):
    z, = _matmul(hn, p["w_in_t"], tb=True, tn=384, name="cd_in")
    cos, sin = _rope_tables()
    per_seq = SEQ // ROW_TILE
    table_map = lambda i: (i % per_seq, 0)

    def rope_fn(z, cos, sin, share):
        qc, kc, vc = z[:, 0:ATT_W], z[:, ATT_W:2 * ATT_W], z[:, 2 * ATT_W:3 * ATT_W]
        qd, kd, vd = z[:, 3 * ATT_W:4 * ATT_W], z[:, 4 * ATT_W:4 * ATT_W + SWA_KW], z[:, 4 * ATT_W + SWA_KW:]
        kd8 = jnp.dot(_rotate(kd, cos, sin).astype(BF16), share, preferred_element_type=F32)
        vd8 = jnp.dot(vd.astype(BF16), share, preferred_element_type=F32)
        qk = jnp.concatenate([_rotate(qc, cos, sin), _rotate(kc, cos, sin)], axis=1)
        return qk, vc, _rotate(qd, cos, sin), kd8, vd8, qk, qk, vc, vc
    dils = [dil for _, dil in DIL_PATTERN if dil > 1]
    outs = _rowwise(rope_fn, [z, cos, sin], [_gqa_share()],
                    [(2 * ATT_W, BF16)] + [(ATT_W, BF16)] * 4 + [(2 * ATT_W, BF16, d) for d in dils] + [(ATT_W, BF16, d) for d in dils],
                    name="cd_rope", row_maps=[None, table_map, table_map])
    qk, vc, qd, kd, vd = outs[:5]
    views = {1: (_seq3(qk, nb), _seq3(vc, nb))}
    for d, qk_d, vc_d in zip(dils, outs[5:5 + len(dils)], outs[5 + len(dils):]):
        views[d] = (_seq3(qk_d, nb), _seq3(vc_d, nb))
    in_classes = lambda a, width, d: _flat2(a) if d == 1 else ("classes", _flat2(a), width, d)
    branch = []
    for window, dil in DIL_PATTERN:
        qk_v, vc_v = views[dil]
        o_i, lse_i = _attn_fwd((qk_v, 0), (qk_v, 1), (vc_v, 0), classes=dil, heads=8, dh=HEAD_DIM, mode="band",
                               max_dist=window // dil, name=f"cd_dil{dil}")
        branch.append((in_classes(o_i, ATT_W, dil), in_classes(lse_i, LANES, dil)))
    expand = _head_expand(8, ATT_W)

    def combine(o1, o2, o3, l1, l2, l3, e):
        m = jnp.maximum(jnp.maximum(l1, l2), l3)
        lt = m + jnp.log(jnp.exp(l1 - m) + jnp.exp(l2 - m) + jnp.exp(l3 - m))
        o = sum(_dot_exact(jnp.exp(l - lt), e) * o_ for o_, l in ((o1, l1), (o2, l2), (o3, l3)))
        return o, lt
    y_c, lse_c = _rowwise(combine, [b[0] for b in branch] + [b[1] for b in branch], [expand], [(ATT_W, F32), (LANES, F32)],
                          name="cd_dil_combine")
    qd3, kd3, vd3 = _seq3(qd, nb), _seq3(kd, nb), _seq3(vd, nb)
    o_d, lse_band = _attn_fwd((qd3, 0), (kd3, 0), (vd3, 0), heads=8, dh=HEAD_DIM, mode="band", max_dist=SWA_WINDOW - 1,
                              name="cd_swa")

    def sink_combine(o, l, e, sink):
        lt = jnp.maximum(l, sink) + jnp.log(1.0 + jnp.exp(-jnp.abs(l - sink)))
        return o * _dot_exact(jnp.exp(l - lt), e), lt
    y_d, lse_d = _rowwise(sink_combine, [_flat2(o_d), _flat2(lse_band)], [expand, p["sink"]], [(ATT_W, F32), (LANES, F32)],
                          name="cd_swa_sink")
    y, = _rowwise(lambda a, b: jnp.concatenate([a, b], axis=1), [y_c, y_d], [], [(D_MODEL, BF16)], name="cd_ycat")
    out, hn_next = _residual_out(y, p["w_out"], h, next_gain, name="cd_out")
    return out, hn_next, (h, hn, views, qd3, kd3, vd3, y_c, lse_c, y_d, lse_d, y)


def _cd_bwd(dh, dhb, p, saved, nb):
    h, hn, views, qd3, kd3, vd3, y_c, lse_c, y_d, lse_d, y = saved
    dy, dyb = _matmul(dhb, p["w_out"], tb=True, outs=(F32, BF16), epilogue=lambda acc: (acc, acc), name="cd_bwd_dy")
    dw_out, = _matmul(y, dhb, ta=True, outs=(BF16,),name="cd_bwd_dwout")
    dyb3 = _seq3(dyb, nb)
    dils = [dil for _, dil in DIL_PATTERN if dil > 1]
    gather = _head_expand(8, ATT_W).T

    def prepare(do, o, lse, e):
        delta = _dot_exact(do * o, e)
        return (delta,) + (do,) * len(dils) + (lse,) * len(dils) + (delta,) * len(dils)
    outs = _rowwise(prepare, [(dy, ATT_W, 0), y_c, lse_c], [gather],
                    [(LANES, F32)] + [(ATT_W, BF16, d) for d in dils] + [(LANES, F32, d) for d in dils] * 2, name="cd_bwd_delta_dil")
    seen = {1: ((dyb3, 0), _seq3(lse_c, nb), _seq3(outs[0], nb))}
    for j, d in enumerate(dils):
        seen[d] = ((_seq3(outs[1 + j], nb), 0), _seq3(outs[1 + len(dils) + j], nb), _seq3(outs[1 + 2 * len(dils) + j], nb))
    in_classes = lambda a, d: _flat2(a) if d == 1 else ("classes", _flat2(a), ATT_W, d)
    parts = []
    for window, dil in DIL_PATTERN:
        (qk_v, vc_v), (do_v, lse_v, delta_v) = views[dil], seen[dil]
        grads = _attn_bwd((qk_v, 0), (qk_v, 1), (vc_v, 0), do_v, lse_v, delta_v, classes=dil, heads=8, dh=HEAD_DIM, mode="band",
                          max_dist=window // dil, name=f"cd_bwd_dil{dil}")
        parts.append([in_classes(a, dil) for a in grads])
    delta_d, dsink = _head_delta((dy, ATT_W, 1), y_d, 8, lse=lse_d, sink=p["sink"], name="cd_bwd_delta_swa")
    dqd, dkd, dvd = _attn_bwd((qd3, 0), (kd3, 0), (vd3, 0), (dyb3, 1), _seq3(lse_d, nb), _seq3(delta_d, nb), heads=8,
                              dh=HEAD_DIM, mode="band", max_dist=SWA_WINDOW - 1, name="cd_bwd_swa")
    cos, sin = _rope_tables()
    per_seq = SEQ // ROW_TILE
    table_map = lambda i: (i % per_seq, 0)

    def unrope(q1, q2, q3, k1, k2, k3, v1, v2, v3, qd, kd8, vd8, cos, sin, gather):
        back = lambda x: _rotate(x, cos, -sin)
        kd, vd = _dot_exact(kd8, gather), _dot_exact(vd8, gather)
        return jnp.concatenate([back(q1 + q2 + q3), back(k1 + k2 + k3), v1 + v2 + v3, back(qd), back(kd), vd], axis=1)
    ins = [parts[b][t] for t in range(3) for b in range(3)] + [_flat2(dqd), _flat2(dkd), _flat2(dvd), cos, sin]
    dz, = _rowwise(unrope, ins, [_gqa_share().T], [(CD_IN, BF16)], name="cd_bwd_unrope",
                   row_maps=[None] * 12 + [table_map, table_map])
    dw_in_t, = _matmul(dz, hn, ta=True, outs=(BF16,),tm=384, name="cd_bwd_dwin")
    dh2, dh2b, dg = _norm_input_grad(dz, p["w_in_t"], h, dh, p["norm"], name="cd_bwd_dh")
    return dh2, dh2b, {"norm": dg, "w_in_t": dw_in_t, "sink": dsink[:, :8], "w_out": dw_out}


def _local_step(x, mem, target, w, complete=None, grads_ready=None):
    nb = x.shape[0]
    h = x.reshape(nb * SEQ, D_MODEL)
    mem2 = mem.reshape(nb * MEM_LEN, D_MODEL)
    tgt = target.reshape(nb * SEQ, D_MODEL)

    hn = _rms_fwd(h, w["ab"]["norm"], name="ab_norm")
    h, hn, s_ab = _ab_fwd(h, hn, w["ab"], nb, w["xa0"]["norm"])
    if complete is not None:
        complete(h)
    h, hn, s_xa0 = _xa_fwd(h, hn, mem2, w["xa0"], "xa0", nb, w["mlp0"]["norm"])
    h, hn, s_mlp0 = _mlp_fwd(h, hn, w["mlp0"], "mlp0", w["cd"]["norm"])
    h, hn, s_cd = _cd_fwd(h, hn, w["cd"], nb, w["xa1"]["norm"])
    h, hn, s_xa1 = _xa_fwd(h, hn, mem2, w["xa1"], "xa1", nb, w["mlp1"]["norm"])
    h, _, s_mlp1 = _mlp_fwd(h, hn, w["mlp1"], "mlp1", None)

    dh, dhb, loss, dg_final = _loss_and_grad(h, tgt, w["final_norm"], name="loss")
    grads = {"final_norm": dg_final}
    dh, dhb, grads["mlp1"] = _mlp_bwd(dh, dhb, w["mlp1"], s_mlp1, "mlp1")
    dh, dhb, grads["xa1"] = _xa_bwd(dh, dhb, mem2, w["xa1"], s_xa1, "xa1", nb)
    dh, dhb, grads["cd"] = _cd_bwd(dh, dhb, w["cd"], s_cd, nb)
    if grads_ready is not None:
        grads_ready(0, grads)
    dh, dhb, grads["mlp0"] = _mlp_bwd(dh, dhb, w["mlp0"], s_mlp0, "mlp0")
    dh, dhb, grads["xa0"] = _xa_bwd(dh, dhb, mem2, w["xa0"], s_xa0, "xa0", nb)
    if grads_ready is not None:
        grads_ready(1, grads)
    dh, dhb, grads["ab"] = _ab_bwd(dh, dhb, w["ab"], s_ab, nb)
    return loss, dh.reshape(nb, SEQ, D_MODEL), grads


ANY = pl.BlockSpec(memory_space=pl.ANY)


def _place():
    x, y, c = lax.axis_index("x"), lax.axis_index("y"), lax.axis_index("c")
    return x, y, c, [(1 - x, y), (x, 1 - y), (1 - x, 1 - y)]


def _gather_chips(shard):
    half = shard.shape[0] // 2

    def body(src, out, send_sems, recv_sems):
        x, y, c, chips = _place()
        sibling = (x, y, 1 - c)

        def rows(slot, core):
            return out.at[slot, pl.ds(core * half, half)]

        def copy(k, src_ref, dst_ref, to):
            return pltpu.make_async_remote_copy(src_ref=src_ref, dst_ref=dst_ref, send_sem=send_sems.at[k],
                                                recv_sem=recv_sems.at[k], device_id=to, device_id_type=MESH)

        first = [copy(k, src.at[pl.ds(c * half, half)], rows(2 * x + y, c), (px, py, c)) for k, (px, py) in enumerate(chips)]
        for cp in first:
            cp.start()
        passed = [copy(3 + k, rows(2 * px + py, c), rows(2 * px + py, c), sibling) for k, (px, py) in enumerate(chips)]
        for k, (px, py) in enumerate(chips):
            copy(k, src.at[pl.ds(c * half, half)], rows(2 * px + py, c), (px, py, c)).wait_recv()
            passed[k].start()
        for k, (px, py) in enumerate(chips):
            copy(3 + k, rows(2 * px + py, 1 - c), rows(2 * px + py, 1 - c), sibling).wait_recv()
        for cp in first + passed:
            cp.wait_send()

    return pl.pallas_call(
        body, out_shape=jax.ShapeDtypeStruct((N_CHIPS,) + shard.shape, shard.dtype), in_specs=[ANY], out_specs=ANY,
        scratch_shapes=[pltpu.SemaphoreType.DMA((6,)), pltpu.SemaphoreType.DMA((6,))], name="gather_chips",
    )(shard)


HBM = pl.BlockSpec(memory_space=pltpu.HBM)
SEM = pl.BlockSpec(memory_space=pltpu.SEMAPHORE)
SIDE_EFFECT = pltpu.SideEffectType.DATAFLOW_SIDE_EFFECTING


def _chip_copies(src, land, send_sems, recv_sems):
    half = src.shape[0] // 2
    x, y, c, chips = _place()

    def copy(k, slot, to):
        return pltpu.make_async_remote_copy(src_ref=src.at[pl.ds(c * half, half)], dst_ref=land.at[slot, pl.ds(c * half, half)],
                                            send_sem=send_sems[k], recv_sem=recv_sems[k], device_id=to, device_id_type=MESH)
    out = [copy(k, 2 * x + y, (px, py, c)) for k, (px, py) in enumerate(chips)]
    back = [copy(k, 2 * px + py, (px, py, c)) for k, (px, py) in enumerate(chips)]
    return out, back


def _gather_start(shard):
    def body(src, land, *rest):
        sems, token = rest[:6], rest[8]
        out, _ = _chip_copies(src, land, sems[:3], sems[3:])
        for cp in out:
            cp.start()
        token[...] = jnp.zeros(token.shape, F32)

    sem = pltpu.SemaphoreType.DMA(())
    land_shape = (N_CHIPS,) + shard.shape
    return pl.pallas_call(
        body, name="gather_start",
        out_shape=(sem,) * 6 + (pltpu.HBM(shard.shape, shard.dtype), pltpu.HBM(land_shape, shard.dtype),
                                jax.ShapeDtypeStruct((8, LANES), F32)),
        in_specs=(HBM, HBM), out_specs=(SEM,) * 6 + (HBM, HBM, pl.BlockSpec(memory_space=pltpu.VMEM)),
        input_output_aliases={0: 6, 1: 7}, compiler_params=pltpu.CompilerParams(has_side_effects=SIDE_EFFECT),
    )(pltpu.with_memory_space_constraint(shard, pltpu.HBM),
      pltpu.with_memory_space_constraint(lax.empty(land_shape, shard.dtype), pltpu.HBM))


def _gather_wait(started, after):
    sems, src_thru, land_thru = started[:6], started[6], started[7]

    def body(src, land, *rest):
        out, back = _chip_copies(src, land, rest[:3], rest[3:6])
        for cp in out:
            cp.wait_send()
        for cp in back:
            cp.wait_recv()

    return pl.pallas_call(
        body, name="gather_wait",
        out_shape=(pltpu.HBM(src_thru.shape, src_thru.dtype), pltpu.HBM(land_thru.shape, land_thru.dtype)),
        in_specs=(HBM, HBM) + (SEM,) * 6 + (pl.BlockSpec(memory_space=pl.ANY),), out_specs=(HBM, HBM),
        input_output_aliases={0: 0, 1: 1}, compiler_params=pltpu.CompilerParams(has_side_effects=SIDE_EFFECT),
    )(src_thru, land_thru, *sems, after)[1]


def _gather_pass(land):
    half = land.shape[1] // 2

    def body(land_in, land_out, send_sems, recv_sems):
        x, y, c, chips = _place()

        def copy(k, slot, core):
            rows = land_out.at[slot, pl.ds(core * half, half)]
            return pltpu.make_async_remote_copy(src_ref=rows, dst_ref=rows, send_sem=send_sems.at[k], recv_sem=recv_sems.at[k],
                                                device_id=(x, y, 1 - c), device_id_type=MESH)
        sends = [copy(k, 2 * px + py, c) for k, (px, py) in enumerate(chips)]
        for cp in sends:
            cp.start()
        for k, (px, py) in enumerate(chips):
            copy(k, 2 * px + py, 1 - c).wait_recv()
        for cp in sends:
            cp.wait_send()

    return pl.pallas_call(
        body, out_shape=jax.ShapeDtypeStruct(land.shape, land.dtype), in_specs=[ANY], out_specs=ANY,
        scratch_shapes=[pltpu.SemaphoreType.DMA((3,)), pltpu.SemaphoreType.DMA((3,))], input_output_aliases={0: 0},
        name="gather_pass",
    )(land)


def _swap_cores(block, *, name, half_rows=None):
    shape = block.shape if half_rows is None else (block.shape[0], half_rows, block.shape[2])

    def body(src, out, send_sem, recv_sem):
        x, y, c, _ = _place()
        part = src if half_rows is None else src.at[:, pl.ds((1 - c) * half_rows, half_rows)]
        cp = pltpu.make_async_remote_copy(src_ref=part, dst_ref=out, send_sem=send_sem, recv_sem=recv_sem,
                                          device_id=(x, y, 1 - c), device_id_type=MESH)
        cp.start()
        cp.wait()

    return pl.pallas_call(
        body, out_shape=jax.ShapeDtypeStruct(shape, block.dtype), in_specs=[ANY], out_specs=ANY,
        scratch_shapes=[pltpu.SemaphoreType.DMA(()), pltpu.SemaphoreType.DMA(())], name=name,
    )(block)


def _scatter_copies(parts, land, send_sems, recv_sems):
    x, y, c, chips = _place()
    return [pltpu.make_async_remote_copy(src_ref=parts.at[2 * px + py], dst_ref=land.at[k], send_sem=send_sems[k],
                                         recv_sem=recv_sems[k], device_id=(px, py, c), device_id_type=MESH)
            for k, (px, py) in enumerate(chips)]


def _scatter_start(parts, tag):
    def body(src, land, *rest):
        for cp in _scatter_copies(src, land, rest[:3], rest[3:6]):
            cp.start()
        rest[8][...] = jnp.zeros(rest[8].shape, F32)

    sem = pltpu.SemaphoreType.DMA(())
    land_shape = (3,) + parts.shape[1:]
    return pl.pallas_call(
        body, name=f"scatter_start_{tag}",
        out_shape=(sem,) * 6 + (pltpu.HBM(parts.shape, parts.dtype), pltpu.HBM(land_shape, parts.dtype),
                                jax.ShapeDtypeStruct((8, LANES), F32)),
        in_specs=(HBM, HBM), out_specs=(SEM,) * 6 + (HBM, HBM, pl.BlockSpec(memory_space=pltpu.VMEM)),
        input_output_aliases={0: 6, 1: 7}, compiler_params=pltpu.CompilerParams(has_side_effects=SIDE_EFFECT),
    )(pltpu.with_memory_space_constraint(parts, pltpu.HBM),
      pltpu.with_memory_space_constraint(lax.empty(land_shape, parts.dtype), pltpu.HBM))


def _scatter_wait(started, after, tag):
    def body(src, land, *rest):
        for cp in _scatter_copies(src, land, rest[:3], rest[3:6]):
            cp.wait_send()
            cp.wait_recv()

    src_thru, land_thru = started[6], started[7]
    return pl.pallas_call(
        body, name=f"scatter_wait_{tag}",
        out_shape=(pltpu.HBM(src_thru.shape, src_thru.dtype), pltpu.HBM(land_thru.shape, land_thru.dtype)),
        in_specs=(HBM, HBM) + (SEM,) * 6 + (pl.BlockSpec(memory_space=pl.ANY),), out_specs=(HBM, HBM),
        input_output_aliases={0: 0, 1: 1}, compiler_params=pltpu.CompilerParams(has_side_effects=SIDE_EFFECT),
    )(src_thru, land_thru, *started[:6], after)[1]


def _sum_all_devices(vec):
    rows = vec.shape[0]

    def body(x_ref, total_ref, all_ref, send_sems, recv_sems, local_sem):
        x, y, c, chips = _place()
        me, sibling = (x, y, c), (x, y, 1 - c)

        def slot(px, py, pc):
            return all_ref.at[4 * px + 2 * py + pc]

        def copy(k, block, to, src=None):
            return pltpu.make_async_remote_copy(src_ref=slot(*block) if src is None else src, dst_ref=slot(*block),
                                                send_sem=send_sems.at[k], recv_sem=recv_sems.at[k], device_id=to,
                                                device_id_type=MESH)

        mine = pltpu.make_async_copy(x_ref, slot(*me), local_sem)
        mine.start()
        first = [copy(0, me, sibling, src=x_ref)] + [copy(1 + j, me, (*chip, c), src=x_ref) for j, chip in enumerate(chips)]
        for cp in first:
            cp.start()
        passed = [copy(4 + j, (*chip, c), sibling) for j, chip in enumerate(chips)]
        for j, chip in enumerate(chips):
            copy(1 + j, (*chip, c), me).wait_recv()
            passed[j].start()
        copy(0, sibling, me).wait_recv()
        for j, chip in enumerate(chips):
            copy(4 + j, (*chip, 1 - c), me).wait_recv()
        for cp in first + passed:
            cp.wait_send()
        mine.wait()
        acc = all_ref[0]
        for d in range(1, 8):
            acc = acc + all_ref[d]
        total_ref[...] = acc

    vmem = pl.BlockSpec(memory_space=pltpu.VMEM)
    return pl.pallas_call(
        body, out_shape=[jax.ShapeDtypeStruct((rows, LANES), F32), jax.ShapeDtypeStruct((8, rows, LANES), F32)],
        in_specs=[vmem], out_specs=[vmem, vmem],
        scratch_shapes=[pltpu.SemaphoreType.DMA((7,)), pltpu.SemaphoreType.DMA((7,)), pltpu.SemaphoreType.DMA(())],
        name="sum_all_devices", compiler_params=pltpu.CompilerParams(vmem_limit_bytes=VMEM_LIMIT_BYTES),
    )(vec)[0]


PACK_COLS = 1024
BIG = (("mlp_w_up", 2, 1024, True), ("mlp_w_down", 2, 1024, False), ("xa_w_kv", 2, 512, True), ("xa_w_q", 2, 256, False),
       ("xa_w_o", 2, 256, False), ("ab_w_out", 1, 256, False), ("cd_w_out", 1, 256, False), ("cd_w_in", 1, 576, True),
       ("ab_w_in", 1, 642, True))
ENTRIES = tuple((name, layer, rows, transposed) for name, layers, rows, transposed in BIG for layer in range(layers))
FIRST_ENTRIES = tuple(e for e in ENTRIES if e[0].startswith("ab_"))
LATER_ENTRIES = tuple(e for e in ENTRIES if not e[0].startswith("ab_"))


def _tile_rows(entry):
    return -(-entry[2] // 16) * 16


def _padded_rows(entries):
    return -(-sum(_tile_rows(e) for e in entries) // 32) * 32


SMALL = (("ab_norm", (1, 1024)), ("ab_conv_w", (1, 4, 512)), ("ab_conv_b", (1, 512)), ("lru_w_a", (1, 8, 64, 64)),
         ("lru_b_a", (1, 512)), ("lru_w_i", (1, 8, 64, 64)), ("lru_b_i", (1, 512)), ("lru_lambda", (1, 512)),
         ("fox_b_f", (1, 8)), ("cd_norm", (1, 1024)), ("cd_sink", (1, 8)), ("xa_norm", (2, 1024)),
         ("xa_mem_norm", (2, 1024)), ("mlp_norm", (2, 1024)), ("final_norm", (1024,)), ("loss", (1,)))
SPLIT_SMALL = ("ab_conv_w", "cd_norm")


def _pack_rows(parts, entries, dtype):
    lead = parts[0].shape[:-2]
    zeros = lambda rows: jnp.zeros(lead + (rows, PACK_COLS), dtype)
    pieces = [jnp.pad(p.astype(dtype), ((0, 0),) * len(lead) + ((0, _tile_rows(e) - e[2]), (0, 0))) for p, e in zip(parts, entries)]
    tail = _padded_rows(entries) - sum(_tile_rows(e) for e in entries)
    return jnp.concatenate(pieces + ([zeros(tail)] if tail else []), axis=len(lead))


def _unpack_rows(packed, entries):
    out, r0 = [], 0
    for e in entries:
        out.append(packed[..., r0:r0 + e[2], :])
        r0 += _tile_rows(e)
    return out


def _shard_rows(w, entries):
    return [(w[name][layer].T if transposed else w[name][layer]) for name, layer, _, transposed in entries]


def _shard_blocks(rows):
    out = {}
    for (name, layer, _, transposed), r in zip(ENTRIES, rows):
        out.setdefault(name, []).append(r.T if transposed else r)
    return {name: jnp.stack(layers) for name, layers in out.items()}


def _pack_small(vals):
    flat = jnp.concatenate([vals[name].astype(F32).reshape(-1) for name, _ in SMALL])
    size = -(-flat.shape[0] // (8 * LANES)) * (8 * LANES)
    return jnp.pad(flat, (0, size - flat.shape[0])).reshape(-1, LANES)


def _unpack_small(packed):
    flat, out, at = packed.reshape(-1), {}, 0
    for name, shape in SMALL:
        out[name] = flat[at:at + math.prod(shape)].reshape(shape)
        at += math.prod(shape)
    return out


def _chip_part(full, chip):
    width = full.shape[-1] // N_CHIPS
    return lax.dynamic_slice_in_dim(full, chip * width, width, axis=full.ndim - 1)


def _chip_embed(part, chip):
    full = jnp.zeros(part.shape[:-1] + (part.shape[-1] * N_CHIPS,), part.dtype)
    return lax.dynamic_update_slice_in_dim(full, part, chip * part.shape[-1], axis=part.ndim - 1)


SUBLAYER_KEYS = {"ab_w_in": ("ab", "w_in_t"), "ab_w_out": ("ab", "w_out"), "cd_w_in": ("cd", "w_in_t"), "cd_w_out": ("cd", "w_out"),
                 "xa_w_q": ("xa", "w_q"), "xa_w_kv": ("xa", "w_kv_t"), "xa_w_o": ("xa", "w_o"), "mlp_w_up": ("mlp", "w_up_t"),
                 "mlp_w_down": ("mlp", "w_down")}


def _sublayer(name, layer):
    block, leaf = SUBLAYER_KEYS[name]
    return (block if block in ("ab", "cd") else f"{block}{layer}"), leaf


def _set_big(params, full_rows):
    for (name, layer), rows in full_rows.items():
        block, leaf = _sublayer(name, layer)
        if name == "ab_w_in":
            rows = jnp.concatenate([rows, jnp.zeros((LANES - 8, PACK_COLS), rows.dtype)])
        params[block][leaf] = rows


def _build_params(full_rows, w):
    pad = lambda a: jnp.pad(a, ((0, 0), (0, LANES - a.shape[1])))
    params = {
        "ab": dict(norm=w["ab_norm"], conv_w=w["ab_conv_w"][0], conv_b=w["ab_conv_b"], w_a=_block_diag(w["lru_w_a"][0]),
                   b_a=w["lru_b_a"], w_i=_block_diag(w["lru_w_i"][0]), b_i=w["lru_b_i"], lam=w["lru_lambda"],
                   b_f=w["fox_b_f"].reshape(8, 1)),
        "cd": dict(norm=w["cd_norm"], sink=pad(w["cd_sink"])),
        "final_norm": w["final_norm"].reshape(1, D_MODEL),
    }
    for layer in range(2):
        params[f"xa{layer}"] = dict(norm=w["xa_norm"][layer:layer + 1], mem_norm=w["xa_mem_norm"][layer:layer + 1])
        params[f"mlp{layer}"] = dict(norm=w["mlp_norm"][layer:layer + 1])
    _set_big(params, full_rows)
    return params


def _grad_rows(g, entries=ENTRIES):
    out = []
    for name, layer, _, _ in entries:
        block, leaf = _sublayer(name, layer)
        out.append(g[block][leaf])
    return out


def _reduce_group(entry):
    name, layer = entry[0], entry[1]
    if name.startswith("ab_"):
        return 2
    return 0 if layer == 1 or name.startswith("cd_") else 1


REDUCE_GROUPS = tuple(tuple(e for e in ENTRIES if _reduce_group(e) == group) for group in range(3))


def _reduce_tile(half):
    return max(t for t in range(16, 1025, 16) if half % t == 0)


def _reduce_start(grads_by_chip, core, chip, tag):
    half = grads_by_chip.shape[1] // 2
    tile = _reduce_tile(half)
    steps = half // tile
    place = jnp.stack([core, chip]).astype(jnp.int32)
    got = _swap_cores(grads_by_chip, name=f"swap_cores_{tag}", half_rows=half)
    block = (1, tile, PACK_COLS)

    def sum_cores(place_ref, mine_ref, got_ref, f32_ref, bf16_ref):
        total = mine_ref[...].astype(F32) + got_ref[...].astype(F32)
        f32_ref[...] = total
        bf16_ref[...] = total.astype(BF16)

    pair32, pair16 = pl.pallas_call(
        sum_cores, name=f"sum_cores_{tag}",
        grid_spec=pltpu.PrefetchScalarGridSpec(
            num_scalar_prefetch=1, grid=(N_CHIPS, steps),
            in_specs=[pl.BlockSpec(block, lambda s, i, place_ref: (s, place_ref[0] * steps + i, 0)),
                      pl.BlockSpec(block, lambda s, i, place_ref: (s, i, 0))],
            out_specs=[pl.BlockSpec(block, lambda s, i, place_ref: (s, i, 0))] * 2),
        out_shape=[jax.ShapeDtypeStruct((N_CHIPS, half, PACK_COLS), F32), jax.ShapeDtypeStruct((N_CHIPS, half, PACK_COLS), BF16)],
        compiler_params=_params("arbitrary", "arbitrary"),
    )(place, grads_by_chip, got)
    return pair32, _scatter_start(pair16, tag), place


def _reduce_finish(state, core, tag, after):
    pair32, started, place = state
    half = pair32.shape[1]
    tile = _reduce_tile(half)
    landed = _scatter_wait(started, after, tag)

    def sum_chips(place_ref, own_ref, landed_ref, out_ref):
        out_ref[...] = own_ref[0] + landed_ref[0].astype(F32) + landed_ref[1].astype(F32) + landed_ref[2].astype(F32)

    done = pl.pallas_call(
        sum_chips, name=f"sum_chips_{tag}",
        grid_spec=pltpu.PrefetchScalarGridSpec(
            num_scalar_prefetch=1, grid=(half // tile,),
            in_specs=[pl.BlockSpec((1, tile, PACK_COLS), lambda i, place_ref: (place_ref[1], i, 0)),
                      pl.BlockSpec((3, tile, PACK_COLS), lambda i, place_ref: (0, i, 0))],
            out_specs=pl.BlockSpec((tile, PACK_COLS), lambda i, place_ref: (i, 0))),
        out_shape=jax.ShapeDtypeStruct((half, PACK_COLS), F32), compiler_params=_params("arbitrary"),
    )(place, pair32, landed)
    theirs = _swap_cores(done, name=f"share_halves_{tag}")
    return jnp.where(core == 0, jnp.concatenate([done, theirs]), jnp.concatenate([theirs, done]))


def kernel(x, mem, ab_norm, ab_w_in, ab_conv_w, ab_conv_b, lru_w_a, lru_b_a, lru_w_i, lru_b_i, lru_lambda, fox_b_f, ab_w_out, cd_norm, cd_w_in, cd_sink, cd_w_out, xa_norm, xa_mem_norm, xa_w_q, xa_w_kv, xa_w_o, mlp_norm, mlp_w_up, mlp_w_down, final_norm, loss_target, m_ab_norm, m_ab_w_in, m_ab_conv_w, m_ab_conv_b, m_lru_w_a, m_lru_b_a, m_lru_w_i, m_lru_b_i, m_lru_lambda, m_fox_b_f, m_ab_w_out, m_cd_norm, m_cd_w_in, m_cd_sink, m_cd_w_out, m_xa_norm, m_xa_mem_norm, m_xa_w_q, m_xa_w_kv, m_xa_w_o, m_mlp_norm, m_mlp_w_up, m_mlp_w_down, m_final_norm, v_ab_norm, v_ab_w_in, v_ab_conv_w, v_ab_conv_b, v_lru_w_a, v_lru_b_a, v_lru_w_i, v_lru_b_i, v_lru_lambda, v_fox_b_f, v_ab_w_out, v_cd_norm, v_cd_w_in, v_cd_sink, v_cd_w_out, v_xa_norm, v_xa_mem_norm, v_xa_w_q, v_xa_w_kv, v_xa_w_o, v_mlp_norm, v_mlp_w_up, v_mlp_w_down, v_final_norm):
    given = dict(locals())
    weight_names = [name for name, _ in SMALL if name != "loss"] + [name for name, _, _, _ in BIG]
    w = {name: given[name] for name in weight_names}
    chip = 2 * lax.axis_index("x") + lax.axis_index("y")
    core = lax.axis_index("c")

    slot = lax.broadcasted_iota(jnp.int32, (N_CHIPS, 1, 1), 0)

    def whole(entries, mine, gathered):
        return {(name, layer): jnp.where(slot == chip, own[None], got).reshape(N_CHIPS * rows, PACK_COLS)
                for (name, layer, rows, _), own, got in zip(entries, _unpack_rows(mine, entries), _unpack_rows(gathered, entries))}

    mine_first = _pack_rows(_shard_rows(w, FIRST_ENTRIES), FIRST_ENTRIES, BF16)
    mine_later = _pack_rows(_shard_rows(w, LATER_ENTRIES), LATER_ENTRIES, BF16)
    first = _gather_chips(mine_first)
    started = _gather_start(mine_later)
    owner = (core == 0).astype(F32)
    quarters = {name: _chip_embed(w[name], chip) * owner for name in SPLIT_SMALL}
    zeros = {name: jnp.zeros(shape, F32) for name, shape in SMALL}
    small_full = _unpack_small(_sum_all_devices(_pack_small({**zeros, **quarters})))
    params = _build_params(whole(FIRST_ENTRIES, mine_first, first),
                           {**w, "ab_conv_w": small_full["ab_conv_w"], "cd_norm": small_full["cd_norm"]})
    params["ab"]["norm"] = params["ab"]["norm"] + started[8][0, 0]

    def complete(h):
        _set_big(params, whole(LATER_ENTRIES, mine_later, _gather_pass(_gather_wait(started, after=h))))

    reducing = {}

    def grads_ready(group, g):
        entries = REDUCE_GROUPS[group]
        by_chip = _pack_rows([r.reshape(N_CHIPS, e[2], PACK_COLS) for r, e in zip(_grad_rows(g, entries), entries)], entries, BF16)
        reducing[group] = _reduce_start(by_chip, core, chip, f"g{group}")
        if group < 2:
            block, leaf = ("mlp0", "w_down") if group == 0 else ("ab", "w_out")
            params[block][leaf] = params[block][leaf] + reducing[group][1][8][0, 0].astype(BF16)

    loss_part, grad_x, g = _local_step(x, mem, loss_target, params, complete, grads_ready)
    grads_ready(2, g)
    reduced = {}
    for group, entries in enumerate(REDUCE_GROUPS):
        rows = _unpack_rows(_reduce_finish(reducing[group], core, f"g{group}", after=grad_x), entries)
        reduced.update({(e[0], e[1]): r for e, r in zip(entries, rows)})
    grads = _shard_blocks([reduced[e[0], e[1]] for e in ENTRIES])
    pair = lambda key, leaf: jnp.stack([g[f"{key}0"][leaf], g[f"{key}1"][leaf]])

    small_local = {"ab_norm": g["ab"]["norm"], "ab_conv_w": g["ab"]["conv_w"], "ab_conv_b": g["ab"]["conv_b"],
                   "lru_w_a": g["ab"]["w_a"], "lru_b_a": g["ab"]["b_a"], "lru_w_i": g["ab"]["w_i"], "lru_b_i": g["ab"]["b_i"],
                   "lru_lambda": g["ab"]["lam"], "fox_b_f": g["ab"]["b_f"], "cd_norm": g["cd"]["norm"], "cd_sink": g["cd"]["sink"],
                   "xa_norm": pair("xa", "norm"), "xa_mem_norm": pair("xa", "mem_norm"), "mlp_norm": pair("mlp", "norm"),
                   "final_norm": g["final_norm"], "loss": loss_part[0, :1]}
    small_sum_packed = _sum_all_devices(_pack_small(small_local))
    small_sum = _unpack_small(small_sum_packed)
    loss = small_sum["loss"][0]

    delta, new_m, new_v = {}, {}, {}
    for name, _, _, _ in BIG:
        shape = w[name].shape
        flat = lambda a: a.reshape(-1, shape[-1])
        d, m2, v2 = _adamw(flat(w[name]), flat(grads[name]), flat(given["m_" + name]), flat(given["v_" + name]), name=f"adamw_{name}")
        delta[name], new_m[name], new_v[name] = d.reshape(shape), m2.reshape(shape), v2.reshape(shape)

    def small_state(prefix):
        vals = {name: (given[prefix + name] if name != "loss" else jnp.zeros((1,), F32)) for name, _ in SMALL}
        for name in SPLIT_SMALL:
            vals[name] = _chip_embed(vals[name], chip)
        return _pack_small(vals)
    packed = _adamw(small_state(""), small_sum_packed, small_state("m_"), small_state("v_"), name="adamw_small")
    for store, vals in zip((delta, new_m, new_v), packed):
        for name, val in _unpack_small(vals).items():
            store[name] = _chip_part(val, chip) if name in SPLIT_SMALL else val
    for name in SPLIT_SMALL:
        small_sum[name] = _chip_part(small_sum[name], chip)
    grads.update({name: small_sum[name] for name, _ in SMALL})

    order = ["ab_norm", "ab_w_in", "ab_conv_w", "ab_conv_b", "lru_w_a", "lru_b_a", "lru_w_i", "lru_b_i", "lru_lambda", "fox_b_f",
             "ab_w_out", "cd_norm", "cd_w_in", "cd_sink", "cd_w_out", "xa_norm", "xa_mem_norm", "xa_w_q", "xa_w_kv", "xa_w_o",
             "mlp_norm", "mlp_w_up", "mlp_w_down", "final_norm"]
    return (loss, grad_x, *[grads[n] for n in order], *[delta[n] for n in order], *[new_m[n] for n in order],
            *[new_v[n] for n in order])
```

```python
import functools
import math

import jax
import jax.numpy as jnp
from jax import lax
from jax.experimental import pallas as pl
from jax.experimental.pallas import tpu as pltpu

F32 = jnp.float32
BF16 = jnp.bfloat16
MESH = pl.DeviceIdType.MESH

D_MODEL = 1024
SEQ = 2048
HEAD_DIM = 64
LRU_WIDTH = 512
LRU_BLOCKS = 8
LRU_C = 8.0
CONV_WIDTH = 4
ATT_W = 512
SWA_KW = 128
SWA_WINDOW = 128
DIL_PATTERN = ((128, 1), (512, 4), (2048, 16))
MEM_LEN = 256
XA_HEADS = 4
XA_HEAD_DIM = 256
D_FF = 4096
ROPE_THETA = 10000.0
EPS = 1e-6
N_CHIPS = 4
LANES = 128

ADAM_LR, ADAM_B1, ADAM_B2, ADAM_EPS, ADAM_WD, ADAM_STEP = 0.001, 0.9, 0.999, 1e-08, 0.01, 10

VMEM_LIMIT_BYTES = 56 * 1024 * 1024
MASKED = -1e30
ROW_TILE = 512
LRU_CHUNK = 512
ATT_BLOCK = 128
BAND_ROWS = 256
ATT_CHUNK = 512
CAUSAL_ROWS = 256


def _params(*sem):
    return pltpu.CompilerParams(dimension_semantics=sem, vmem_limit_bytes=VMEM_LIMIT_BYTES)


def _rowwise(fn, rows, consts=(), out_rows=(), out_accs=(), *, name, tile=ROW_TILE, row_maps=None):
    rows = [r if isinstance(r, tuple) else (r, r.shape[1], 0) for r in rows]
    consts = list(consts)
    nr, nc, no = len(rows), len(consts), len(out_rows)
    dealt_in = [r[3] if isinstance(r[0], str) else None for r in rows]
    dealt_out = [o[2] if len(o) == 3 else None for o in out_rows]
    total = next(r[0].shape[0] for r in rows if not isinstance(r[0], str))
    tile = min(tile, total)
    assert total % tile == 0
    n = total // tile
    n_acc = len(out_accs)

    def body(*refs):
        scratch = list(refs[nr + nc + no + n_acc:])
        vals = []
        for ref, d, row in zip(refs[:nr], dealt_in, rows):
            if d is None:
                vals.append(ref[...])
                continue
            sc, width = scratch.pop(0), row[2]
            for r in range(d):
                for c in range(width // LANES):
                    lanes = slice(r * width + c * LANES, r * width + (c + 1) * LANES)
                    sc.at[c][pl.ds(r, tile // d, stride=d), :] = ref[:, lanes].astype(F32)
            vals.append(jnp.concatenate([sc[c] for c in range(width // LANES)], axis=1))
        outs = fn(*vals, *[r[...] for r in refs[nr:nr + nc]])
        outs = tuple(outs) if isinstance(outs, (tuple, list)) else (outs,)
        for ref, val, d, spec in zip(refs[nr + nc:nr + nc + no], outs[:no], dealt_out, out_rows):
            if d is None:
                ref[...] = val.astype(ref.dtype)
                continue
            sc, width = scratch.pop(0), spec[0]
            for c in range(width // LANES):
                sc[c] = val[:, c * LANES:(c + 1) * LANES].astype(F32)
            for r in range(d):
                for c in range(width // LANES):
                    lanes = slice(r * width + c * LANES, r * width + (c + 1) * LANES)
                    ref[:, lanes] = sc.at[c][pl.ds(r, tile // d, stride=d), :].astype(ref.dtype)
        for ref, val in zip(refs[nr + nc + no:nr + nc + no + n_acc], outs[no:]):
            @pl.when(pl.program_id(0) == 0)
            def _(ref=ref):
                ref[...] = jnp.zeros(ref.shape, ref.dtype)
            ref[...] += val

    in_specs, args, scratch_shapes = [], [], []
    for idx, row in enumerate(rows):
        if isinstance(row[0], str):
            _, arr, width, d = row
            in_specs.append(pl.BlockSpec((tile // d, d * width), lambda i: (i, 0)))
            scratch_shapes.append(pltpu.VMEM((width // LANES, tile, LANES), F32))
        elif row_maps is not None and row_maps[idx] is not None:
            arr, width, _ = row
            in_specs.append(pl.BlockSpec((tile, width), row_maps[idx]))
        else:
            arr, width, cb = row
            in_specs.append(pl.BlockSpec((tile, width), functools.partial(lambda i, cb: (i, cb), cb=cb)))
        args.append(arr)
    in_specs += [pl.BlockSpec(c.shape, lambda i: (0, 0)) for c in consts]
    out_shape, out_specs = [], []
    for spec, d in zip(out_rows, dealt_out):
        w, dt = spec[0], spec[1]
        if d is None:
            out_shape.append(jax.ShapeDtypeStruct((total, w), dt))
            out_specs.append(pl.BlockSpec((tile, w), lambda i: (i, 0)))
        else:
            out_shape.append(jax.ShapeDtypeStruct((total // d, d * w), dt))
            out_specs.append(pl.BlockSpec((tile // d, d * w), lambda i: (i, 0)))
            scratch_shapes.append(pltpu.VMEM((w // LANES, tile, LANES), F32))
    out_shape += [jax.ShapeDtypeStruct(s, F32) for s in out_accs]
    out_specs += [pl.BlockSpec(s, lambda i: (0, 0)) for s in out_accs]
    return pl.pallas_call(
        body, grid=(n,), in_specs=in_specs, out_specs=out_specs, out_shape=out_shape, scratch_shapes=scratch_shapes, name=name,
        compiler_params=_params("arbitrary"),
    )(*args, *consts)


MATMUL_VMEM_BYTES = 40 * 1024 * 1024


def _matmul_tiles(m, n, k, tm, tn, out_bytes):
    tm, tn, tk = min(tm, m), min(tn, n), k

    def need():
        return 2 * (2 * tk * (tm + tn) + tm * tn * out_bytes) + (0 if tk == k else 4 * tm * tn)

    while need() > MATMUL_VMEM_BYTES:
        if tm >= tn and tm % 256 == 0:
            tm //= 2
        elif tn % 256 == 0:
            tn //= 2
        else:
            tk //= 2
    return tm, tn, tk


def _matmul(a, b, *, ta=False, tb=False, outs=(F32,), epilogue=None, extras=(), consts=(), sums=(), whole_rows=False,
            tm=1024, tn=1024, name):
    m, k = (a.shape[1], a.shape[0]) if ta else a.shape
    n = b.shape[0] if tb else b.shape[1]
    assert (b.shape[1] if tb else b.shape[0]) == k
    outs = [o if isinstance(o, tuple) else (o, None) for o in outs]
    out_bytes = sum(jnp.dtype(dt).itemsize for dt, w in outs if w is None) + sum(e.dtype.itemsize for e in extras)
    tm, tn, tk = _matmul_tiles(m, n, k, tm, tn, out_bytes)
    assert m % tm == 0 and n % tn == 0 and k % tk == 0, (name, m, n, k)
    nk = k // tk
    ne, nc, no = len(extras), len(consts), len(outs)
    assert tn == n or not (sums or whole_rows or any(w is not None for _, w in outs)), name
    dims = (((0 if ta else 1,), (1 if tb else 0,)), ((), ()))

    def body(*refs):
        a_ref, b_ref = refs[:2]
        e_refs, o_refs = refs[2:2 + ne + nc], refs[2 + ne + nc:2 + ne + nc + no]
        s_refs = refs[2 + ne + nc + no:2 + ne + nc + no + len(sums)]

        def finish(acc):
            vals = (acc,) if epilogue is None else epilogue(acc, *[e[...] for e in e_refs])
            for ref, val in zip(o_refs, vals[:no]):
                ref[...] = val.astype(ref.dtype)
            for ref, val in zip(s_refs, vals[no:]):
                @pl.when(pl.program_id(0) == 0)
                def _(ref=ref, val=val):
                    ref[...] = val

                @pl.when(pl.program_id(0) > 0)
                def _(ref=ref, val=val):
                    ref[...] += val

        prod = lax.dot_general(a_ref[...], b_ref[...], dims, preferred_element_type=F32)
        if nk == 1:
            finish(prod)
        else:
            acc_ref = refs[-1]
            step = pl.program_id(2)

            @pl.when(step == 0)
            def _():
                acc_ref[...] = prod

            @pl.when(step > 0)
            def _():
                acc_ref[...] += prod

            @pl.when(step == nk - 1)
            def _():
                finish(acc_ref[...])

    a_spec = pl.BlockSpec((tk, tm), lambda i, j, s: (s, i)) if ta else pl.BlockSpec((tm, tk), lambda i, j, s: (i, s))
    b_spec = pl.BlockSpec((tn, tk), lambda i, j, s: (j, s)) if tb else pl.BlockSpec((tk, tn), lambda i, j, s: (s, j))
    tile_spec = pl.BlockSpec((tm, tn), lambda i, j, s: (i, j))
    whole = lambda shape: pl.BlockSpec(shape, lambda i, j, s: (0, 0))
    return pl.pallas_call(
        body, grid=(m // tm, n // tn, nk),
        in_specs=[a_spec, b_spec] + [tile_spec] * ne + [whole(c.shape) for c in consts],
        out_specs=[tile_spec if w is None else pl.BlockSpec((tm, w), lambda i, j, s: (i, 0)) for _, w in outs]
        + [whole(shape) for shape in sums],
        out_shape=[jax.ShapeDtypeStruct((m, n if w is None else w), dt) for dt, w in outs]
        + [jax.ShapeDtypeStruct(shape, F32) for shape in sums],
        scratch_shapes=[pltpu.VMEM((tm, tn), F32)] if nk > 1 else [],
        name=name, compiler_params=_params("arbitrary" if sums else "parallel", "parallel", "arbitrary"),
    )(a, b, *extras, *consts)


def _split3(x):
    x1 = x.astype(BF16)
    r1 = x - x1.astype(F32)
    x2 = r1.astype(BF16)
    x3 = (r1 - x2.astype(F32)).astype(BF16)
    return x1, x2, x3


def _dot_exact(x, e):
    return sum(jnp.dot(p, e, preferred_element_type=F32) for p in _split3(x))


def _head_expand(heads, width):
    dh = width // heads
    rows = jnp.arange(LANES)[:, None]
    cols = jnp.arange(width)[None, :]
    return (cols // dh == rows).astype(BF16)


def _rstd(x):
    return lax.rsqrt(jnp.mean(x * x, axis=-1, keepdims=True) + EPS)


def _rms_fwd(x, gain, *, name):
    return _rowwise(lambda x, g: x * _rstd(x) * g, [x], [gain], [(x.shape[1], BF16)], name=name)[0]


def _rms_bwd_vals(x, dhn, gain):
    r = _rstd(x)
    xh = x * r
    dxh = dhn * gain
    dx = r * (dxh - xh * jnp.mean(dxh * xh, axis=-1, keepdims=True))
    return dx, jnp.sum(dhn * xh, axis=0, keepdims=True)


def _norm_input_grad(dz, w, x, dres, gain, *, tb=False, name):
    def epilogue(dhn, x, dres, g):
        dx, dg = _rms_bwd_vals(x, dhn, g)
        dh = dres + dx
        return dh, dh, dg
    return _matmul(dz, w, tb=tb, outs=(F32, BF16), extras=(x, dres), consts=(gain,), sums=((1, x.shape[1]),), epilogue=epilogue,
                   name=name)


def _loss_and_grad(h, target, gain, *, name):
    def fn(h, tgt, g):
        r = _rstd(h)
        err = h * r * g - tgt
        loss = 0.5 * jnp.sum(jnp.mean(err * err, axis=-1, keepdims=True), axis=0, keepdims=True)
        dx, dg = _rms_bwd_vals(h, err * (1.0 / D_MODEL), g)
        return dx, dx, jnp.broadcast_to(loss, (1, LANES)), dg
    d = h.shape[1]
    return _rowwise(fn, [h, target], [gain], [(d, F32), (d, BF16)], [(1, LANES), (1, d)], name=name)


def _adamw(w, g, m, v, *, name):
    rows, cols = w.shape
    tile = rows
    for cand in (256, 128, 64, 32, 16, 8):
        if rows % cand == 0 and rows > cand:
            tile = cand
            break
    bc1 = 1.0 - ADAM_B1 ** ADAM_STEP
    bc2 = 1.0 - ADAM_B2 ** ADAM_STEP

    def fn(w, g, m, v):
        m2 = ADAM_B1 * m + (1.0 - ADAM_B1) * g
        v2 = ADAM_B2 * v + (1.0 - ADAM_B2) * (g * g)
        delta = -ADAM_LR * ((m2 / bc1) / (jnp.sqrt(v2 / bc2) + ADAM_EPS) + ADAM_WD * w)
        return delta, m2, v2
    return _rowwise(fn, [w, g, m, v], [], [(cols, F32)] * 3, name=name, tile=tile)


def _attn_specs(arr, width, cb, classes, rows_block, whole):
    ncols = arr.shape[2] // (classes * width)
    if whole:
        return pl.BlockSpec((1, arr.shape[1], width), lambda n, r, i: (n, 0, r * ncols + cb))
    return pl.BlockSpec((1, rows_block, width), lambda n, r, i: (n, i, r * ncols + cb))


def _attn_tiles(mode, lq, lk, backward=False):
    if mode == "full":
        return min(lq, 256), min(lq, 256), lk
    if mode == "band":
        tq = min(BAND_ROWS, lq)
        rows = tq if backward else ATT_BLOCK
        return tq, rows, min(rows + ATT_BLOCK, lk)
    return CAUSAL_ROWS, CAUSAL_ROWS, ATT_CHUNK


def _window(mode, row0, j, rows, win, lk):
    if mode == "causal":
        return pl.multiple_of(j * win, win), row0 // win + 1
    if mode == "band":
        return pl.multiple_of(jnp.clip(row0 + rows - win, 0, lk - win), ATT_BLOCK), 1
    return 0, 1


def _allowed(mode, row0, start, rows, win, max_dist):
    if mode == "full":
        return None
    dist = (row0 + lax.broadcasted_iota(jnp.int32, (rows, win), 0)) - (start + lax.broadcasted_iota(jnp.int32, (rows, win), 1))
    ok = dist >= 0
    if max_dist is not None:
        ok = ok & (dist <= max_dist)
    return ok


def _head_groups(heads, dh):
    gw = max(LANES, dh)
    return gw, gw // dh, heads // (gw // dh)


def _own_lanes(rows, gw, dh, u):
    return lax.broadcasted_iota(jnp.int32, (rows, gw), 1) // dh == u


def _only_head(x, own, per, u):
    return x if per == 1 else jnp.where(own[u], x, jnp.zeros_like(x))


def _step_scores(qz, kw, kb_row, ok, scale):
    s = lax.dot_general(qz, kw, (((1,), (1,)), ((), ())), preferred_element_type=F32) * scale
    if kb_row is not None:
        s = s - kb_row
    if ok is not None:
        s = jnp.where(ok, s, MASKED)
    return s


def _attn_fwd(q, k, v, kb=None, *, classes=1, heads, dh, mode, max_dist=None, name):
    (qa, qc), (ka, kc), (va, vc) = q, k, v
    n, lq, lk = qa.shape[0], qa.shape[1], ka.shape[1]
    width = heads * dh
    tq, rows, win = _attn_tiles(mode, lq, lk)
    gw, per, groups = _head_groups(heads, dh)
    scale = dh ** -0.5
    has_kb = kb is not None
    single = mode != "causal"

    def body(*refs):
        for sub in range(tq // rows):
            part(refs, pl.program_id(2) * tq + sub * rows, slice(sub * rows, (sub + 1) * rows))

    def part(refs, row0, rs):
        q_ref, k_ref, v_ref = refs[:3]
        kb_ref = refs[3] if has_kb else None
        o_ref, lse_ref = refs[-2:]
        lane = lax.broadcasted_iota(jnp.int32, (rows, LANES), 1)
        own = [_own_lanes(rows, gw, dh, u) for u in range(per)]

        def step(j, carry):
            m_in, l_in = carry
            m_out, l_out = m_in, l_in
            start, _ = _window(mode, row0, j, rows, win, lk)
            ok = _allowed(mode, row0, start, rows, win, max_dist)
            for g in range(groups):
                cols = slice(g * gw, (g + 1) * gw)
                qg = q_ref[0, rs, cols]
                kw = k_ref[0, pl.ds(start, win), cols]
                vw = v_ref[0, pl.ds(start, win), cols]
                alpha_g = new_g = None
                for u in range(per):
                    h = g * per + u
                    kb_row = kb_ref[0, h, pl.ds(j, 1), :] if has_kb else None
                    s = _step_scores(_only_head(qg, own, per, u), kw, kb_row, ok, scale)
                    m_new = jnp.max(s, axis=1, keepdims=True)
                    if not single:
                        m_old = m_in[:, h:h + 1]
                        m_new = jnp.maximum(m_old, m_new)
                        alpha = jnp.exp(m_old - m_new)
                        alpha_g = alpha if u == 0 else jnp.where(own[u], alpha, alpha_g)
                    p = jnp.exp(s - m_new)
                    l_new = jnp.sum(p, axis=1, keepdims=True)
                    r = jnp.dot(p.astype(BF16), vw, preferred_element_type=F32)
                    if single:
                        r = r * (1.0 / l_new)
                    else:
                        l_new = alpha * l_in[:, h:h + 1] + l_new
                    new_g = r if u == 0 else jnp.where(own[u], r, new_g)
                    m_out = jnp.where(lane == h, m_new, m_out)
                    l_out = jnp.where(lane == h, l_new, l_out)
                o_ref[0, rs, cols] = new_g if single else alpha_g * o_ref[0, rs, cols] + new_g
            return m_out, l_out

        carry = (jnp.full((rows, LANES), MASKED, F32), jnp.zeros((rows, LANES), F32))
        if single:
            m_all, l_all = step(0, carry)
            l_all = jnp.where(lane < heads, l_all, 1.0)
        else:
            o_ref[...] = jnp.zeros(o_ref.shape, F32)
            m_all, l_all = lax.fori_loop(0, _window(mode, row0, 0, rows, win, lk)[1], step, carry)
            l_all = jnp.where(lane < heads, l_all, 1.0)
            inv = 1.0 / l_all
            for g in range(groups):
                cols = slice(g * gw, (g + 1) * gw)
                inv_g = inv[:, g * per:g * per + 1]
                for u in range(1, per):
                    inv_g = jnp.where(own[u], inv[:, g * per + u:g * per + u + 1], inv_g)
                o_ref[0, rs, cols] = o_ref[0, rs, cols] * inv_g
        lse_ref[0, rs, :] = jnp.where(lane < heads, m_all + jnp.log(l_all), 0.0)

    in_specs = [_attn_specs(qa, width, qc, classes, tq, False), _attn_specs(ka, width, kc, classes, win, True),
                _attn_specs(va, width, vc, classes, win, True)]
    args = [qa, ka, va]
    if has_kb:
        in_specs.append(pl.BlockSpec((1,) + kb.shape[1:], lambda n_, r, i: (n_, 0, 0, 0)))
        args.append(kb)
    out_shape = [jax.ShapeDtypeStruct((n, lq, classes * width), F32), jax.ShapeDtypeStruct((n, lq, classes * LANES), F32)]
    out_specs = [pl.BlockSpec((1, tq, width), lambda n_, r, i: (n_, i, r)), pl.BlockSpec((1, tq, LANES), lambda n_, r, i: (n_, i, r))]
    return pl.pallas_call(
        body, grid=(n, classes, lq // tq), in_specs=in_specs, out_specs=out_specs, out_shape=out_shape, name=name,
        compiler_params=_params("parallel", "parallel", "arbitrary"),
    )(*args)


def _attn_bwd(q, k, v, do, lse, delta, kb=None, *, classes=1, heads, dh, mode, max_dist=None, dq_dtype=F32, name):
    (qa, qc), (ka, kc), (va, vc), (da, dc) = q, k, v, do
    n, lq, lk = qa.shape[0], qa.shape[1], ka.shape[1]
    width = heads * dh
    tq, rows, win = _attn_tiles(mode, lq, lk, backward=True)
    gw, per, groups = _head_groups(heads, dh)
    scale = dh ** -0.5
    has_kb = kb is not None
    single = mode != "causal"
    nt =(((1,), (1,)), ((), ()))
    tn = (((0,), (0,)), ((), ()))

    def body(*refs):
        n_in = 7 if has_kb else 6
        dk_ref, dv_ref = refs[n_in + 1:n_in + 3]

        @pl.when(pl.program_id(2) == 0)
        def _():
            dk_ref[...] = jnp.zeros(dk_ref.shape, F32)
            dv_ref[...] = jnp.zeros(dv_ref.shape, F32)
            if has_kb:
                refs[n_in + 3][...] = jnp.zeros(refs[n_in + 3].shape, F32)

        for sub in range(tq // rows):
            part(refs, pl.program_id(2) * tq + sub * rows, slice(sub * rows, (sub + 1) * rows))

    def part(refs, row0, rs):
        q_ref, k_ref, v_ref, do_ref, lse_ref, dl_ref = refs[:6]
        kb_ref = refs[6] if has_kb else None
        n_in = 7 if has_kb else 6
        dq_ref, dk_ref, dv_ref = refs[n_in:n_in + 3]
        dkb_ref, drow_ref = refs[n_in + 3:n_in + 5] if has_kb else (None, None)
        lse_all = lse_ref[0, rs, :]
        dl_all = dl_ref[0, rs, :]
        lane = lax.broadcasted_iota(jnp.int32, (rows, LANES), 1)
        own = [_own_lanes(rows, gw, dh, u) for u in range(per)]

        def step(j, drow_all):
            start, _ = _window(mode, row0, j, rows, win, lk)
            ok = _allowed(mode, row0, start, rows, win, max_dist)
            for g in range(groups):
                cols = slice(g * gw, (g + 1) * gw)
                qg = q_ref[0, rs, cols]
                dog = do_ref[0, rs, cols]
                kw = k_ref[0, pl.ds(start, win), cols]
                vw = v_ref[0, pl.ds(start, win), cols]
                dq_g = dk_w = dv_w = None
                for u in range(per):
                    h = g * per + u
                    qz, doz = _only_head(qg, own, per, u), _only_head(dog, own, per, u)
                    kb_row = kb_ref[0, h, pl.ds(j, 1), :] if has_kb else None
                    p = jnp.exp(_step_scores(qz, kw, kb_row, ok, scale) - lse_all[:, h:h + 1])
                    dp = lax.dot_general(doz, vw, nt, preferred_element_type=F32)
                    ds = p * (dp - dl_all[:, h:h + 1])
                    dsb = ds.astype(BF16)
                    r = jnp.dot(dsb, kw, preferred_element_type=F32)
                    dq_g = r if u == 0 else jnp.where(own[u], r, dq_g)
                    dk_u = lax.dot_general(dsb, qz, tn, preferred_element_type=F32)
                    dv_u = lax.dot_general(p.astype(BF16), doz, tn, preferred_element_type=F32)
                    dk_w = dk_u if u == 0 else dk_w + dk_u
                    dv_w = dv_u if u == 0 else dv_w + dv_u
                    if has_kb:
                        dkb_ref[0, h, pl.ds(j, 1), :] += -jnp.sum(ds, axis=0, keepdims=True)
                        drow_all = drow_all + jnp.where(lane == h, jnp.sum(ds, axis=1, keepdims=True), 0.0)
                dk_ref[0, pl.ds(start, win), cols] += dk_w * scale
                dv_ref[0, pl.ds(start, win), cols] += dv_w
                if single:
                    dq_ref[0, rs, cols] = (dq_g * scale).astype(dq_ref.dtype)
                else:
                    dq_ref[0, rs, cols] += dq_g * scale
            return drow_all

        drow_all = jnp.zeros((rows, LANES), F32)
        if single:
            drow_all = step(0, drow_all)
        else:
            dq_ref[...] = jnp.zeros(dq_ref.shape, F32)
            drow_all = lax.fori_loop(0, _window(mode, row0, 0, rows, win, lk)[1], step, drow_all)
        if has_kb:
            drow_ref[0, rs, :] = drow_all

    row_spec = functools.partial(_attn_specs, classes=classes, rows_block=tq, whole=False)
    in_specs = [row_spec(qa, width, qc), _attn_specs(ka, width, kc, classes, win, True), _attn_specs(va, width, vc, classes, win, True),
                row_spec(da, width, dc), row_spec(lse, LANES, 0), row_spec(delta, LANES, 0)]
    args = [qa, ka, va, da, lse, delta]
    assert single or dq_dtype == F32
    out_shape = [jax.ShapeDtypeStruct((n, lq, classes * width), dq_dtype), jax.ShapeDtypeStruct((n, lk, classes * width), F32),
                 jax.ShapeDtypeStruct((n, lk, classes * width), F32)]
    kv_spec = pl.BlockSpec((1, lk, width), lambda n_, r, i: (n_, 0, r))
    out_specs = [pl.BlockSpec((1, tq, width), lambda n_, r, i: (n_, i, r)), kv_spec, kv_spec]
    if has_kb:
        kb_spec = pl.BlockSpec((1,) + kb.shape[1:], lambda n_, r, i: (n_, 0, 0, 0))
        in_specs.append(kb_spec)
        args.append(kb)
        out_shape += [jax.ShapeDtypeStruct(kb.shape, F32), jax.ShapeDtypeStruct((n, lq, LANES), F32)]
        out_specs += [kb_spec, pl.BlockSpec((1, tq, LANES), lambda n_, r, i: (n_, i, 0))]
    return pl.pallas_call(
        body, grid=(n, classes, lq // tq), in_specs=in_specs, out_specs=out_specs, out_shape=out_shape, name=name,
        compiler_params=_params("parallel", "parallel", "arbitrary"),
    )(*args)


def _head_delta(do, out, heads, *, name, lse=None, sink=None):
    width = out.shape[1]
    gather = _head_expand(heads, width).T

    if sink is None:
        return _rowwise(lambda do, o, e: _dot_exact(do * o, e), [do, out], [gather], [(LANES, F32)], name=name)[0]

    def fn(do, o, lse, e, sink):
        delta = _dot_exact(do * o, e)
        return delta, -jnp.sum(jnp.exp(sink - lse) * delta, axis=0, keepdims=True)
    return _rowwise(fn, [do, out, lse], [gather, sink], [(LANES, F32)], [(1, LANES)], name=name)


def _rope_tables():
    half = HEAD_DIM // 2
    inv = ROPE_THETA ** (-jnp.arange(half, dtype=F32) / half)
    ang = jnp.arange(SEQ, dtype=F32)[:, None] * inv[None, :]
    cos = jnp.tile(jnp.cos(ang), (1, 2 * ATT_W // HEAD_DIM))
    sin = jnp.tile(jnp.concatenate([-jnp.sin(ang), jnp.sin(ang)], axis=1), (1, ATT_W // HEAD_DIM))
    return cos, sin


def _rotate(x, cos, sin):
    width = x.shape[1]
    lane = lax.broadcasted_iota(jnp.int32, x.shape, 1)
    first_half = (lane % HEAD_DIM) < (HEAD_DIM // 2)
    swapped = jnp.where(first_half, pltpu.roll(x, width - HEAD_DIM // 2, axis=1), pltpu.roll(x, HEAD_DIM // 2, axis=1))
    return x * cos[:, :width] + swapped * sin[:, :width]


def _gelu(x):
    k = math.sqrt(2.0 / math.pi)
    t = jnp.tanh(k * (x + 0.044715 * x * x * x))
    return 0.5 * x * (1.0 + t), t


def _gelu_grad(x, t):
    k = math.sqrt(2.0 / math.pi)
    return 0.5 * (1.0 + t) + 0.5 * x * (1.0 - t * t) * k * (1.0 + 3.0 * 0.044715 * x * x)


def _shift_down(x, halo, s):
    ext = jnp.concatenate([halo, x], axis=0)
    return pltpu.roll(ext, s, axis=0)[8:, :]


def _shift_up(x, halo, s):
    rows = x.shape[0]
    ext = jnp.concatenate([x, halo], axis=0)
    return pltpu.roll(ext, rows + 8 - s, axis=0)[:rows, :]


def _lru_gates(u, halo, cw, cb, wa, ba, wi, bi, lam):
    taps = [_shift_down(u, halo, CONV_WIDTH - 1 - k) for k in range(CONV_WIDTH - 1)] + [u]
    uc = cb + sum(cw[k:k + 1, :] * taps[k] for k in range(CONV_WIDTH))
    ucb = uc.astype(BF16)
    r = jax.nn.sigmoid(jnp.dot(ucb, wa, preferred_element_type=F32) + ba)
    gi = jax.nn.sigmoid(jnp.dot(ucb, wi, preferred_element_type=F32) + bi)
    sp = jnp.maximum(-lam, 0.0) + jnp.log(1.0 + jnp.exp(-jnp.abs(lam)))
    log_a = -LRU_C * r * sp
    a = jnp.exp(log_a)
    x2 = 2.0 * log_a
    one_minus_a2 = jnp.where(x2 > -0.01, -x2 * (1.0 + x2 * (0.5 + x2 * (1.0 / 6.0 + x2 / 24.0))), 1.0 - jnp.exp(x2))
    mult = jnp.sqrt(one_minus_a2)
    return taps, uc, ucb, r, gi, sp, a, mult


def _lru_specs(nchunk, reverse):
    def chunk(b, c):
        return b * nchunk + ((nchunk - 1 - c) if reverse else c)

    def halo_before(b, c):
        return jnp.maximum(chunk(b, c) * (LRU_CHUNK // 8) - 1, 0)

    return chunk, halo_before


def _lru_fwd(zug, cw, cb, wa, ba, wi, bi, lam, *, name):
    total = zug.shape[0]
    nchunk = SEQ // LRU_CHUNK
    w = LRU_WIDTH
    chunk, halo_before = _lru_specs(nchunk, False)

    def body(u_ref, uh_ref, g_ref, cw_ref, cb_ref, wa_ref, ba_ref, wi_ref, bi_ref, lam_ref, y_ref, h_ref, a_sc, x_sc, carry_sc):
        c = pl.program_id(1)
        u = u_ref[...]
        halo = jnp.where(c > 0, uh_ref[...], 0.0)
        _, uc, _, _, gi, _, a, mult = _lru_gates(u, halo, cw_ref[...], cb_ref[...], wa_ref[...], ba_ref[...],
                                                 wi_ref[...], bi_ref[...], lam_ref[...])
        a_sc[...] = a
        x_sc[...] = mult * (gi * uc)

        @pl.when(c == 0)
        def _():
            carry_sc[...] = jnp.zeros(carry_sc.shape, F32)

        def tile_step(t, h):
            r0 = pl.multiple_of(t * 8, 8)
            at = a_sc[pl.ds(r0, 8), :]
            xt = x_sc[pl.ds(r0, 8), :]
            rows = []
            for j in range(8):
                h = at[j:j + 1, :] * h + xt[j:j + 1, :]
                rows.append(h)
            h_ref[pl.ds(r0, 8), :] = jnp.concatenate(rows, axis=0)
            return h

        h_last = lax.fori_loop(0, LRU_CHUNK // 8, tile_step, carry_sc[0:1, :])
        carry_sc[0:1, :] = h_last
        gel, _ = _gelu(g_ref[...])
        y_ref[...] = (h_ref[...] * gel).astype(BF16)

    small = lambda arr: pl.BlockSpec(arr.shape, lambda b, c: (0, 0))
    return pl.pallas_call(
        body, grid=(total // SEQ, nchunk),
        in_specs=[pl.BlockSpec((LRU_CHUNK, w), lambda b, c: (chunk(b, c), 0)),
                  pl.BlockSpec((8, w), lambda b, c: (halo_before(b, c), 0)),
                  pl.BlockSpec((LRU_CHUNK, w), lambda b, c: (chunk(b, c), 1)),
                  small(cw), small(cb), small(wa), small(ba), small(wi), small(bi), small(lam)],
        out_specs=[pl.BlockSpec((LRU_CHUNK, w), lambda b, c: (chunk(b, c), 0))] * 2,
        out_shape=[jax.ShapeDtypeStruct((total, w), BF16), jax.ShapeDtypeStruct((total, w), F32)],
        scratch_shapes=[pltpu.VMEM((LRU_CHUNK, w), F32), pltpu.VMEM((LRU_CHUNK, w), F32), pltpu.VMEM((8, w), F32)],
        name=name, compiler_params=_params("arbitrary", "arbitrary"),
    )(zug, zug, zug, cw, cb, wa, ba, wi, bi, lam)


def _lru_bwd(zug, hl, dy, cw, cb, wa, ba, wi, bi, lam, *, name):
    total = zug.shape[0]
    nchunk = SEQ // LRU_CHUNK
    w = LRU_WIDTH
    chunk, halo_before = _lru_specs(nchunk, True)
    tn = (((0,), (0,)), ((), ()))
    nt = (((1,), (1,)), ((), ()))

    def body(u_ref, uh_ref, g_ref, hl_ref, hh_ref, dy_ref, cw_ref, cb_ref, wa_ref, ba_ref, wi_ref, bi_ref, lam_ref,
             dz_ref, dcw_ref, dcb_ref, dwa_ref, dba_ref, dwi_ref, dbi_ref, dlam_ref,
             a_sc, d_sc, g_sc, gcarry_sc, acarry_sc, duc_sc):
        b, c = pl.program_id(0), pl.program_id(1)
        first_chunk = c == nchunk - 1

        @pl.when((b == 0) & (c == 0))
        def _():
            for ref in (dcw_ref, dcb_ref, dwa_ref, dba_ref, dwi_ref, dbi_ref, dlam_ref):
                ref[...] = jnp.zeros(ref.shape, F32)

        @pl.when(c == 0)
        def _():
            gcarry_sc[...] = jnp.zeros(gcarry_sc.shape, F32)
            acarry_sc[...] = jnp.zeros(acarry_sc.shape, F32)
            duc_sc[...] = jnp.zeros(duc_sc.shape, F32)

        u = u_ref[...]
        halo = jnp.where(first_chunk, 0.0, uh_ref[...])
        cw, wa, wi, lam = cw_ref[...], wa_ref[...], wi_ref[...], lam_ref[...]
        taps, uc, ucb, r, gi, sp, a, mult = _lru_gates(u, halo, cw, cb_ref[...], wa, ba_ref[...], wi, bi_ref[...], lam)
        gate = g_ref[...]
        gel, th = _gelu(gate)
        dy = dy_ref[...]
        hl = hl_ref[...]
        a_sc[...] = a
        d_sc[...] = dy * gel
        dgate = dy * hl * _gelu_grad(gate, th)

        def tile_step(t, carry):
            g_next, a_next = carry
            r0 = pl.multiple_of((LRU_CHUNK // 8 - 1 - t) * 8, 8)
            dt = d_sc[pl.ds(r0, 8), :]
            at = a_sc[pl.ds(r0, 8), :]
            rows = [None] * 8
            for j in reversed(range(8)):
                g_next = dt[j:j + 1, :] + a_next * g_next
                a_next = at[j:j + 1, :]
                rows[j] = g_next
            g_sc[pl.ds(r0, 8), :] = jnp.concatenate(rows, axis=0)
            return g_next, a_next

        g_last, a_last = lax.fori_loop(0, LRU_CHUNK // 8, tile_step, (gcarry_sc[0:1, :], acarry_sc[0:1, :]))
        gcarry_sc[0:1, :] = g_last
        acarry_sc[0:1, :] = a_last

        gs = g_sc[...]
        hl_halo = jnp.where(first_chunk, 0.0, hh_ref[...])
        da = gs * _shift_down(hl, hl_halo, 1)
        dmult = gs * gi * uc
        dgi = gs * mult * uc
        duc = gs * mult * gi
        dlog_a = da * a - dmult * a * a / mult
        dr = dlog_a * (-LRU_C * sp)
        dsp = jnp.sum(dlog_a * (-LRU_C * r), axis=0, keepdims=True)
        dlam_ref[...] += -dsp * jax.nn.sigmoid(-lam)
        dpa = dr * r * (1.0 - r)
        dpi = dgi * gi * (1.0 - gi)
        dpab, dpib = dpa.astype(BF16), dpi.astype(BF16)
        dba_ref[...] += jnp.sum(dpa, axis=0, keepdims=True)
        dbi_ref[...] += jnp.sum(dpi, axis=0, keepdims=True)
        dwa_ref[...] += lax.dot_general(ucb, dpab, tn, preferred_element_type=F32)
        dwi_ref[...] += lax.dot_general(ucb, dpib, tn, preferred_element_type=F32)
        duc = duc + lax.dot_general(dpab, wa, nt, preferred_element_type=F32)
        duc = duc + lax.dot_general(dpib, wi, nt, preferred_element_type=F32)
        dcb_ref[...] += jnp.sum(duc, axis=0, keepdims=True)
        dcw_ref[...] += jnp.concatenate([jnp.sum(duc * taps[k], axis=0, keepdims=True) for k in range(CONV_WIDTH)], axis=0)
        after = duc_sc[...]
        du = cw[CONV_WIDTH - 1:CONV_WIDTH, :] * duc
        for k in range(CONV_WIDTH - 1):
            du = du + cw[k:k + 1, :] * _shift_up(duc, after, CONV_WIDTH - 1 - k)
        duc_sc[...] = duc[0:8, :]
        dz_ref[:, 0:w] = du.astype(BF16)
        dz_ref[:, w:2 * w] = dgate.astype(BF16)

    small = lambda arr: pl.BlockSpec(arr.shape, lambda b, c: (0, 0))
    acc = lambda shape: pl.BlockSpec(shape, lambda b, c: (0, 0))
    chunk_spec = lambda cb_: pl.BlockSpec((LRU_CHUNK, w), lambda b, c: (chunk(b, c), cb_))
    halo_spec = pl.BlockSpec((8, w), lambda b, c: (halo_before(b, c), 0))
    acc_shapes = [(CONV_WIDTH, w), (1, w), (w, w), (1, w), (w, w), (1, w), (1, w)]
    return pl.pallas_call(
        body, grid=(total // SEQ, nchunk),
        in_specs=[chunk_spec(0), halo_spec, chunk_spec(1), chunk_spec(0), halo_spec, chunk_spec(0),
                  small(cw), small(cb), small(wa), small(ba), small(wi), small(bi), small(lam)],
        out_specs=[pl.BlockSpec((LRU_CHUNK, 2 * w), lambda b, c: (chunk(b, c), 0))] + [acc(s) for s in acc_shapes],
        out_shape=[jax.ShapeDtypeStruct((total, 2 * w), BF16)] + [jax.ShapeDtypeStruct(s, F32) for s in acc_shapes],
        scratch_shapes=[pltpu.VMEM((LRU_CHUNK, w), F32)] * 3 + [pltpu.VMEM((8, w), F32)] * 3,
        name=name, compiler_params=_params("arbitrary", "arbitrary"),
    )(zug, zug, zug, hl, hl, dy, cw, cb, wa, ba, wi, bi, lam)


def _block_diag(wb):
    eye = jnp.eye(LRU_BLOCKS, dtype=wb.dtype)
    return jnp.einsum("gcd,gh->gchd", wb, eye).reshape(LRU_WIDTH, LRU_WIDTH).astype(BF16)


def _diag_blocks(wd):
    blk = LRU_WIDTH // LRU_BLOCKS
    return jnp.stack([wd[g * blk:(g + 1) * blk, g * blk:(g + 1) * blk] for g in range(LRU_BLOCKS)])


FOX_SCAN = 256


def _fox_bias(fl, bf, *, name):
    nb = fl.shape[0]

    def body(fl_ref, bf_ref, c_ref):
        x = fl_ref[0] + bf_ref[...]
        logf = jnp.minimum(x, 0.0) - jnp.log(1.0 + jnp.exp(-jnp.abs(x)))
        upper = (lax.broadcasted_iota(jnp.int32, (FOX_SCAN, FOX_SCAN), 0)
                 <= lax.broadcasted_iota(jnp.int32, (FOX_SCAN, FOX_SCAN), 1)).astype(BF16)
        carry = jnp.zeros((LRU_BLOCKS, 1), F32)
        for j in range(SEQ // FOX_SCAN):
            blk = logf[:, j * FOX_SCAN:(j + 1) * FOX_SCAN]
            c_ref[0, :, j * FOX_SCAN:(j + 1) * FOX_SCAN] = _dot_exact(blk, upper) + carry
            carry = carry + jnp.sum(blk, axis=1, keepdims=True)

    return pl.pallas_call(
        body, grid=(nb,),
        in_specs=[pl.BlockSpec((1, 8, SEQ), lambda b: (b, 0, 0)), pl.BlockSpec((8, 1), lambda b: (0, 0))],
        out_specs=pl.BlockSpec((1, 8, SEQ), lambda b: (b, 0, 0)),
        out_shape=jax.ShapeDtypeStruct((nb, 8, SEQ), F32), name=name, compiler_params=_params("arbitrary"),
    )(fl, bf)


def _fox_bias_bwd(fl, bf, dc, *, name):
    nb = fl.shape[0]

    def body(fl_ref, bf_ref, dc_ref, dfl_ref, dbf_ref):
        @pl.when(pl.program_id(0) == 0)
        def _():
            dbf_ref[...] = jnp.zeros(dbf_ref.shape, F32)

        x = fl_ref[0] + bf_ref[...]
        dc_all = dc_ref[0]
        lower = (lax.broadcasted_iota(jnp.int32, (FOX_SCAN, FOX_SCAN), 0)
                 >= lax.broadcasted_iota(jnp.int32, (FOX_SCAN, FOX_SCAN), 1)).astype(BF16)
        carry = jnp.zeros((LRU_BLOCKS, 1), F32)
        total = jnp.zeros((LRU_BLOCKS, 1), F32)
        for j in reversed(range(SEQ // FOX_SCAN)):
            cols = slice(j * FOX_SCAN, (j + 1) * FOX_SCAN)
            blk = dc_all[:, cols]
            dlogf = _dot_exact(blk, lower) + carry
            carry = carry + jnp.sum(blk, axis=1, keepdims=True)
            dx = dlogf * jax.nn.sigmoid(-x[:, cols])
            dfl_ref[0, :, cols] = dx
            total = total + jnp.sum(dx, axis=1, keepdims=True)
        dbf_ref[...] += total

    return pl.pallas_call(
        body, grid=(nb,),
        in_specs=[pl.BlockSpec((1, 8, SEQ), lambda b: (b, 0, 0)), pl.BlockSpec((8, 1), lambda b: (0, 0)),
                  pl.BlockSpec((1, 8, SEQ), lambda b: (b, 0, 0))],
        out_specs=[pl.BlockSpec((1, 8, SEQ), lambda b: (b, 0, 0)), pl.BlockSpec((8, 1), lambda b: (0, 0))],
        out_shape=[jax.ShapeDtypeStruct((nb, 8, SEQ), F32), jax.ShapeDtypeStruct((8, 1), F32)],
        name=name, compiler_params=_params("arbitrary"),
    )(fl, bf, dc)


def _seq3(a, nb):
    return a.reshape(nb, a.shape[0] // nb, a.shape[1])


def _flat2(a):
    return a.reshape(a.shape[0] * a.shape[1], a.shape[2])


def _residual_out(y, w, h, next_gain, *, name):
    if next_gain is None:
        out, = _matmul(y, w, extras=(h,), epilogue=lambda acc, res: (acc + res,), name=name)
        return out, None

    def epilogue(acc, res, g):
        out = acc + res
        return out, out * _rstd(out) * g
    return _matmul(y, w, outs=(F32, BF16), extras=(h,), consts=(next_gain,), epilogue=epilogue, whole_rows=True, name=name)


def _mlp_fwd(h, hn, p, tag, next_gain):
    act, = _matmul(hn, p["w_up_t"], tb=True, outs=(BF16,), epilogue=lambda acc: (jnp.square(jnp.maximum(acc, 0.0)),),
                   name=f"{tag}_up")
    out, hn_next = _residual_out(act, p["w_down"], h, next_gain, name=f"{tag}_down")
    return out, hn_next, (h, hn, act)


def _mlp_bwd(dh, dhb, p, saved, tag):
    h, hn, act = saved
    dup, = _matmul(dhb, p["w_down"], tb=True, outs=(BF16,), extras=(act,),
                   epilogue=lambda acc, act: (acc * (2.0 * jnp.sqrt(act.astype(F32))),), name=f"{tag}_bwd_dup")
    dw_down, = _matmul(act, dhb, ta=True, outs=(BF16,),name=f"{tag}_bwd_dwdown")
    dw_up_t, = _matmul(dup, hn, ta=True, outs=(BF16,),name=f"{tag}_bwd_dwup")
    dh2, dh2b, dg = _norm_input_grad(dup, p["w_up_t"], h, dh, p["norm"], name=f"{tag}_bwd_dh")
    return dh2, dh2b, {"norm": dg, "w_up_t": dw_up_t, "w_down": dw_down}


def _xa_fwd(h, hn, mem, p, tag, nb, next_gain):
    mem_n = _rms_fwd(mem, p["mem_norm"], name=f"{tag}_memnorm")
    q, = _matmul(hn, p["w_q"], outs=(BF16,), name=f"{tag}_q")
    kv, = _matmul(mem_n, p["w_kv_t"], tb=True, outs=(BF16,), name=f"{tag}_kv")
    q3, kv3 = _seq3(q, nb), _seq3(kv, nb)
    o, lse = _attn_fwd((q3, 0), (kv3, 0), (kv3, 1), heads=XA_HEADS, dh=XA_HEAD_DIM, mode="full",
                       name=f"{tag}_attn")
    o = _flat2(o)
    ob, = _rowwise(lambda o: o, [o], [], [(D_MODEL, BF16)], name=f"{tag}_ocast")
    out, hn_next = _residual_out(ob, p["w_o"], h, next_gain, name=f"{tag}_o")
    return out, hn_next, (h, hn, mem_n, q3, kv3, o, ob, lse)


def _xa_bwd(dh, dhb, mem, p, saved, tag, nb):
    h, hn, mem_n, q3, kv3, o, ob, lse = saved
    dob, delta = _matmul(dhb, p["w_o"], tb=True, outs=(BF16, (F32, LANES)), extras=(o,), consts=(_head_expand(XA_HEADS, D_MODEL).T,),
                         epilogue=lambda acc, o, e: (acc, _dot_exact(acc * o, e)), name=f"{tag}_bwd_do")
    dw_o, = _matmul(ob, dhb, ta=True, outs=(BF16,),name=f"{tag}_bwd_dwo")
    dq, dk, dv = _attn_bwd((q3, 0), (kv3, 0), (kv3, 1), (_seq3(dob, nb), 0), lse, _seq3(delta, nb), heads=XA_HEADS,
                           dh=XA_HEAD_DIM, mode="full", dq_dtype=BF16, name=f"{tag}_bwd_attn")
    dqb = _flat2(dq)
    dkvb, = _rowwise(lambda a, b: jnp.concatenate([a, b], axis=1), [_flat2(dk), _flat2(dv)], [], [(2 * D_MODEL, BF16)],
                     name=f"{tag}_bwd_dkvcast")
    dw_q, = _matmul(hn, dqb, ta=True, outs=(BF16,),name=f"{tag}_bwd_dwq")
    dw_kv_t, = _matmul(dkvb, mem_n, ta=True, outs=(BF16,),name=f"{tag}_bwd_dwkv")
    dmem_n, = _matmul(dkvb, p["w_kv_t"], name=f"{tag}_bwd_dmemn")
    dg_mem, = _rowwise(lambda x, d, g: _rms_bwd_vals(x, d, g)[1], [mem, dmem_n], [p["mem_norm"]], [], [(1, D_MODEL)],
                       name=f"{tag}_bwd_memnorm")
    dh2, dh2b, dg = _norm_input_grad(dqb, p["w_q"], h, dh, p["norm"], tb=True, name=f"{tag}_bwd_dh")
    return dh2, dh2b, {"norm": dg, "mem_norm": dg_mem, "w_q": dw_q, "w_kv_t": dw_kv_t, "w_o": dw_o}


AB_MAIN = 2 * LRU_WIDTH + 3 * ATT_W


def _ab_fwd(h, hn, p, nb, next_gain):
    w_in_t = p["w_in_t"]
    zug, = _matmul(hn, w_in_t[:2 * LRU_WIDTH], tb=True, name="ab_in_ug")
    qkv, = _matmul(hn, w_in_t[2 * LRU_WIDTH:AB_MAIN], tb=True, outs=(BF16,), tn=768, name="ab_in_qkv")
    zf, = _matmul(hn, w_in_t[AB_MAIN:], tb=True, name="ab_in_f")
    fl = zf[:, :8].reshape(nb, SEQ, 8).transpose(0, 2, 1)
    cbias = _fox_bias(fl, p["b_f"], name="ab_fox_bias")
    kb = cbias.reshape(nb, 8, SEQ // ATT_CHUNK, ATT_CHUNK)
    y_a, hl = _lru_fwd(zug, p["conv_w"], p["conv_b"], p["w_a"], p["b_a"], p["w_i"], p["b_i"], p["lam"], name="ab_lru")
    qkv3 = _seq3(qkv, nb)
    o, lse = _attn_fwd((qkv3, 0), (qkv3, 1), (qkv3, 2), kb, heads=8, dh=HEAD_DIM, mode="causal", name="ab_fox")
    o = _flat2(o)
    y, = _rowwise(lambda a, b: jnp.concatenate([a, b.astype(BF16)], axis=1), [y_a, o], [], [(D_MODEL, BF16)], name="ab_ycat")
    out, hn_next = _residual_out(y, p["w_out"], h, next_gain, name="ab_out")
    return out, hn_next, (h, hn, zug, qkv3, fl, kb, hl, o, lse, y)


def _ab_bwd(dh, dhb, p, saved, nb):
    h, hn, zug, qkv3, fl, kb, hl, o, lse, y = saved
    dy, dyb, delta = _matmul(dhb, p["w_out"], tb=True, outs=(F32, BF16, (F32, LANES)), extras=(y,), consts=(_head_expand(8, ATT_W).T,),
                             epilogue=lambda acc, y, e: (acc, acc, _dot_exact(acc[:, ATT_W:] * y[:, ATT_W:].astype(F32), e)),
                             name="ab_bwd_dy")
    dw_out, = _matmul(y, dhb, ta=True, outs=(BF16,),name="ab_bwd_dwout")
    dzug, dcw, dcb, dwa, dba, dwi, dbi, dlam = _lru_bwd(zug, hl, dy, p["conv_w"], p["conv_b"], p["w_a"], p["b_a"],
                                                        p["w_i"], p["b_i"], p["lam"], name="ab_bwd_lru")
    dq, dk, dv, dkb, drow = _attn_bwd((qkv3, 0), (qkv3, 1), (qkv3, 2), (_seq3(dyb, nb), 1), lse, _seq3(delta, nb), kb,
                                      heads=8, dh=HEAD_DIM, mode="causal", name="ab_bwd_fox")
    dcum = dkb.reshape(nb, 8, SEQ) + drow[:, :, :8].transpose(0, 2, 1)
    dfl, dbf = _fox_bias_bwd(fl, p["b_f"], dcum, name="ab_bwd_fox_bias")
    dzf = jnp.pad(dfl.transpose(0, 2, 1).reshape(nb * SEQ, 8), ((0, 0), (0, LANES - 8)))
    dz, = _rowwise(lambda a, q, k, v, f: jnp.concatenate([a, q.astype(BF16), k.astype(BF16), v.astype(BF16), f.astype(BF16)], axis=1),
                   [dzug, _flat2(dq), _flat2(dk), _flat2(dv), dzf], [], [(AB_MAIN + LANES, BF16)], name="ab_bwd_dzcat")
    dw_in_t, = _matmul(dz, hn, ta=True, outs=(BF16,),tm=384, name="ab_bwd_dwin")
    dh2, dh2b, dg = _norm_input_grad(dz, p["w_in_t"], h, dh, p["norm"], name="ab_bwd_dh")
    grads = {"norm": dg, "w_in_t": dw_in_t[:AB_MAIN + 8], "conv_w": dcw, "conv_b": dcb, "w_a": _diag_blocks(dwa), "b_a": dba,
             "w_i": _diag_blocks(dwi), "b_i": dbi, "lam": dlam, "b_f": dbf.reshape(1, 8), "w_out": dw_out}
    return dh2, dh2b, grads


CD_IN = 3 * ATT_W + ATT_W + 2 * SWA_KW


def _gqa_share():
    src = jnp.arange(SWA_KW)[:, None]
    dst = jnp.arange(ATT_W)[None, :]
    per_kv = ATT_W // (SWA_KW // HEAD_DIM)
    return ((dst // per_kv == src // HEAD_DIM) & (dst % HEAD_DIM == src % HEAD_DIM)).astype(BF16)


def _cd_fwd(h, hn, p, nb, next_gain):
    z, = _matmul(hn, p["w_in_t"], tb=True, tn=384, name="cd_in")
    cos, sin = _rope_tables()
    per_seq = SEQ // ROW_TILE
    table_map = lambda i: (i % per_seq, 0)

    def rope_fn(z, cos, sin, share):
        qc, kc, vc = z[:, 0:ATT_W], z[:, ATT_W:2 * ATT_W], z[:, 2 * ATT_W:3 * ATT_W]
        qd, kd, vd = z[:, 3 * ATT_W:4 * ATT_W], z[:, 4 * ATT_W:4 * ATT_W + SWA_KW], z[:, 4 * ATT_W + SWA_KW:]
        kd8 = jnp.dot(_rotate(kd, cos, sin).astype(BF16), share, preferred_element_type=F32)
        vd8 = jnp.dot(vd.astype(BF16), share, preferred_element_type=F32)
        qk = jnp.concatenate([_rotate(qc, cos, sin), _rotate(kc, cos, sin)], axis=1)
        return qk, vc, _rotate(qd, cos, sin), kd8, vd8, qk, qk, vc, vc
    dils = [dil for _, dil in DIL_PATTERN if dil > 1]
    outs = _rowwise(rope_fn, [z, cos, sin], [_gqa_share()],
                    [(2 * ATT_W, BF16)] + [(ATT_W, BF16)] * 4 + [(2 * ATT_W, BF16, d) for d in dils] + [(ATT_W, BF16, d) for d in dils],
                    name="cd_rope", row_maps=[None, table_map, table_map])
    qk, vc, qd, kd, vd = outs[:5]
    views = {1: (_seq3(qk, nb), _seq3(vc, nb))}
    for d, qk_d, vc_d in zip(dils, outs[5:5 + len(dils)], outs[5 + len(dils):]):
        views[d] = (_seq3(qk_d, nb), _seq3(vc_d, nb))
    in_classes = lambda a, width, d: _flat2(a) if d == 1 else ("classes", _flat2(a), width, d)
    branch = []
    for window, dil in DIL_PATTERN:
        qk_v, vc_v = views[dil]
        o_i, lse_i = _attn_fwd((qk_v, 0), (qk_v, 1), (vc_v, 0), classes=dil, heads=8, dh=HEAD_DIM, mode="band",
                               max_dist=window // dil, name=f"cd_dil{dil}")
        branch.append((in_classes(o_i, ATT_W, dil), in_classes(lse_i, LANES, dil)))
    expand = _head_expand(8, ATT_W)

    def combine(o1, o2, o3, l1, l2, l3, e):
        m = jnp.maximum(jnp.maximum(l1, l2), l3)
        lt = m + jnp.log(jnp.exp(l1 - m) + jnp.exp(l2 - m) + jnp.exp(l3 - m))
        o = sum(_dot_exact(jnp.exp(l - lt), e) * o_ for o_, l in ((o1, l1), (o2, l2), (o3, l3)))
        return o, lt
    y_c, lse_c = _rowwise(combine, [b[0] for b in branch] + [b[1] for b in branch], [expand], [(ATT_W, F32), (LANES, F32)],
                          name="cd_dil_combine")
    qd3, kd3, vd3 = _seq3(qd, nb), _seq3(kd, nb), _seq3(vd, nb)
    o_d, lse_band = _attn_fwd((qd3, 0), (kd3, 0), (vd3, 0), heads=8, dh=HEAD_DIM, mode="band", max_dist=SWA_WINDOW - 1,
                              name="cd_swa")

    def sink_combine(o, l, e, sink):
        lt = jnp.maximum(l, sink) + jnp.log(1.0 + jnp.exp(-jnp.abs(l - sink)))
        return o * _dot_exact(jnp.exp(l - lt), e), lt
    y_d, lse_d = _rowwise(sink_combine, [_flat2(o_d), _flat2(lse_band)], [expand, p["sink"]], [(ATT_W, F32), (LANES, F32)],
                          name="cd_swa_sink")
    y, = _rowwise(lambda a, b: jnp.concatenate([a, b], axis=1), [y_c, y_d], [], [(D_MODEL, BF16)], name="cd_ycat")
    out, hn_next = _residual_out(y, p["w_out"], h, next_gain, name="cd_out")
    return out, hn_next, (h, hn, views, qd3, kd3, vd3, y_c, lse_c, y_d, lse_d, y)


def _cd_bwd(dh, dhb, p, saved, nb):
    h, hn, views, qd3, kd3, vd3, y_c, lse_c, y_d, lse_d, y = saved
    dy, dyb = _matmul(dhb, p["w_out"], tb=True, outs=(F32, BF16), epilogue=lambda acc: (acc, acc), name="cd_bwd_dy")
    dw_out, = _matmul(y, dhb, ta=True, outs=(BF16,),name="cd_bwd_dwout")
    dyb3 = _seq3(dyb, nb)
    dils = [dil for _, dil in DIL_PATTERN if dil > 1]
    gather = _head_expand(8, ATT_W).T

    def prepare(do, o, lse, e):
        delta = _dot_exact(do * o, e)
        return (delta,) + (do,) * len(dils) + (lse,) * len(dils) + (delta,) * len(dils)
    outs = _rowwise(prepare, [(dy, ATT_W, 0), y_c, lse_c], [gather],
                    [(LANES, F32)] + [(ATT_W, BF16, d) for d in dils] + [(LANES, F32, d) for d in dils] * 2, name="cd_bwd_delta_dil")
    seen = {1: ((dyb3, 0), _seq3(lse_c, nb), _seq3(outs[0], nb))}
    for j, d in enumerate(dils):
        seen[d] = ((_seq3(outs[1 + j], nb), 0), _seq3(outs[1 + len(dils) + j], nb), _seq3(outs[1 + 2 * len(dils) + j], nb))
    in_classes = lambda a, d: _flat2(a) if d == 1 else ("classes", _flat2(a), ATT_W, d)
    parts = []
    for window, dil in DIL_PATTERN:
        (qk_v, vc_v), (do_v, lse_v, delta_v) = views[dil], seen[dil]
        grads = _attn_bwd((qk_v, 0), (qk_v, 1), (vc_v, 0), do_v, lse_v, delta_v, classes=dil, heads=8, dh=HEAD_DIM, mode="band",
                          max_dist=window // dil, name=f"cd_bwd_dil{dil}")
        parts.append([in_classes(a, dil) for a in grads])
    delta_d, dsink = _head_delta((dy, ATT_W, 1), y_d, 8, lse=lse_d, sink=p["sink"], name="cd_bwd_delta_swa")
    dqd, dkd, dvd = _attn_bwd((qd3, 0), (kd3, 0), (vd3, 0), (dyb3, 1), _seq3(lse_d, nb), _seq3(delta_d, nb), heads=8,
                              dh=HEAD_DIM, mode="band", max_dist=SWA_WINDOW - 1, name="cd_bwd_swa")
    cos, sin = _rope_tables()
    per_seq = SEQ // ROW_TILE
    table_map = lambda i: (i % per_seq, 0)

    def unrope(q1, q2, q3, k1, k2, k3, v1, v2, v3, qd, kd8, vd8, cos, sin, gather):
        back = lambda x: _rotate(x, cos, -sin)
        kd, vd = _dot_exact(kd8, gather), _dot_exact(vd8, gather)
        return jnp.concatenate([back(q1 + q2 + q3), back(k1 + k2 + k3), v1 + v2 + v3, back(qd), back(kd), vd], axis=1)
    ins = [parts[b][t] for t in range(3) for b in range(3)] + [_flat2(dqd), _flat2(dkd), _flat2(dvd), cos, sin]
    dz, = _rowwise(unrope, ins, [_gqa_share().T], [(CD_IN, BF16)], name="cd_bwd_unrope",
                   row_maps=[None] * 12 + [table_map, table_map])
    dw_in_t, = _matmul(dz, hn, ta=True, outs=(BF16,),tm=384, name="cd_bwd_dwin")
    dh2, dh2b, dg = _norm_input_grad(dz, p["w_in_t"], h, dh, p["norm"], name="cd_bwd_dh")
    return dh2, dh2b, {"norm": dg, "w_in_t": dw_in_t, "sink": dsink[:, :8], "w_out": dw_out}


def _local_step(x, mem, target, w, complete=None, grads_ready=None):
    nb = x.shape[0]
    h = x.reshape(nb * SEQ, D_MODEL)
    mem2 = mem.reshape(nb * MEM_LEN, D_MODEL)
    tgt = target.reshape(nb * SEQ, D_MODEL)

    hn = _rms_fwd(h, w["ab"]["norm"], name="ab_norm")
    h, hn, s_ab = _ab_fwd(h, hn, w["ab"], nb, w["xa0"]["norm"])
    if complete is not None:
        complete(h)
    h, hn, s_xa0 = _xa_fwd(h, hn, mem2, w["xa0"], "xa0", nb, w["mlp0"]["norm"])
    h, hn, s_mlp0 = _mlp_fwd(h, hn, w["mlp0"], "mlp0", w["cd"]["norm"])
    h, hn, s_cd = _cd_fwd(h, hn, w["cd"], nb, w["xa1"]["norm"])
    h, hn, s_xa1 = _xa_fwd(h, hn, mem2, w["xa1"], "xa1", nb, w["mlp1"]["norm"])
    h, _, s_mlp1 = _mlp_fwd(h, hn, w["mlp1"], "mlp1", None)

    dh, dhb, loss, dg_final = _loss_and_grad(h, tgt, w["final_norm"], name="loss")
    grads = {"final_norm": dg_final}
    dh, dhb, grads["mlp1"] = _mlp_bwd(dh, dhb, w["mlp1"], s_mlp1, "mlp1")
    dh, dhb, grads["xa1"] = _xa_bwd(dh, dhb, mem2, w["xa1"], s_xa1, "xa1", nb)
    dh, dhb, grads["cd"] = _cd_bwd(dh, dhb, w["cd"], s_cd, nb)
    if grads_ready is not None:
        grads_ready(0, grads)
    dh, dhb, grads["mlp0"] = _mlp_bwd(dh, dhb, w["mlp0"], s_mlp0, "mlp0")
    dh, dhb, grads["xa0"] = _xa_bwd(dh, dhb, mem2, w["xa0"], s_xa0, "xa0", nb)
    if grads_ready is not None:
        grads_ready(1, grads)
    dh, dhb, grads["ab"] = _ab_bwd(dh, dhb, w["ab"], s_ab, nb)
    return loss, dh.reshape(nb, SEQ, D_MODEL), grads


ANY = pl.BlockSpec(memory_space=pl.ANY)


def _place():
    x, y, c = lax.axis_index("x"), lax.axis_index("y"), lax.axis_index("c")
    return x, y, c, [(1 - x, y), (x, 1 - y), (1 - x, 1 - y)]


def _gather_chips(shard):
    half = shard.shape[0] // 2

    def body(src, out, send_sems, recv_sems):
        x, y, c, chips = _place()
        sibling = (x, y, 1 - c)

        def rows(slot, core):
            return out.at[slot, pl.ds(core * half, half)]

        def copy(k, src_ref, dst_ref, to):
            return pltpu.make_async_remote_copy(src_ref=src_ref, dst_ref=dst_ref, send_sem=send_sems.at[k],
                                                recv_sem=recv_sems.at[k], device_id=to, device_id_type=MESH)

        first = [copy(k, src.at[pl.ds(c * half, half)], rows(2 * x + y, c), (px, py, c)) for k, (px, py) in enumerate(chips)]
        for cp in first:
            cp.start()
        passed = [copy(3 + k, rows(2 * px + py, c), rows(2 * px + py, c), sibling) for k, (px, py) in enumerate(chips)]
        for k, (px, py) in enumerate(chips):
            copy(k, src.at[pl.ds(c * half, half)], rows(2 * px + py, c), (px, py, c)).wait_recv()
            passed[k].start()
        for k, (px, py) in enumerate(chips):
            copy(3 + k, rows(2 * px + py, 1 - c), rows(2 * px + py, 1 - c), sibling).wait_recv()
        for cp in first + passed:
            cp.wait_send()

    return pl.pallas_call(
        body, out_shape=jax.ShapeDtypeStruct((N_CHIPS,) + shard.shape, shard.dtype), in_specs=[ANY], out_specs=ANY,
        scratch_shapes=[pltpu.SemaphoreType.DMA((6,)), pltpu.SemaphoreType.DMA((6,))], name="gather_chips",
    )(shard)


HBM = pl.BlockSpec(memory_space=pltpu.HBM)
SEM = pl.BlockSpec(memory_space=pltpu.SEMAPHORE)
SIDE_EFFECT = pltpu.SideEffectType.DATAFLOW_SIDE_EFFECTING


def _chip_copies(src, land, send_sems, recv_sems):
    half = src.shape[0] // 2
    x, y, c, chips = _place()

    def copy(k, slot, to):
        return pltpu.make_async_remote_copy(src_ref=src.at[pl.ds(c * half, half)], dst_ref=land.at[slot, pl.ds(c * half, half)],
                                            send_sem=send_sems[k], recv_sem=recv_sems[k], device_id=to, device_id_type=MESH)
    out = [copy(k, 2 * x + y, (px, py, c)) for k, (px, py) in enumerate(chips)]
    back = [copy(k, 2 * px + py, (px, py, c)) for k, (px, py) in enumerate(chips)]
    return out, back


def _gather_start(shard):
    def body(src, land, *rest):
        sems, token = rest[:6], rest[8]
        out, _ = _chip_copies(src, land, sems[:3], sems[3:])
        for cp in out:
            cp.start()
        token[...] = jnp.zeros(token.shape, F32)

    sem = pltpu.SemaphoreType.DMA(())
    land_shape = (N_CHIPS,) + shard.shape
    return pl.pallas_call(
        body, name="gather_start",
        out_shape=(sem,) * 6 + (pltpu.HBM(shard.shape, shard.dtype), pltpu.HBM(land_shape, shard.dtype),
                                jax.ShapeDtypeStruct((8, LANES), F32)),
        in_specs=(HBM, HBM), out_specs=(SEM,) * 6 + (HBM, HBM, pl.BlockSpec(memory_space=pltpu.VMEM)),
        input_output_aliases={0: 6, 1: 7}, compiler_params=pltpu.CompilerParams(has_side_effects=SIDE_EFFECT),
    )(pltpu.with_memory_space_constraint(shard, pltpu.HBM),
      pltpu.with_memory_space_constraint(lax.empty(land_shape, shard.dtype), pltpu.HBM))


def _gather_wait(started, after):
    sems, src_thru, land_thru = started[:6], started[6], started[7]

    def body(src, land, *rest):
        out, back = _chip_copies(src, land, rest[:3], rest[3:6])
        for cp in out:
            cp.wait_send()
        for cp in back:
            cp.wait_recv()

    return pl.pallas_call(
        body, name="gather_wait",
        out_shape=(pltpu.HBM(src_thru.shape, src_thru.dtype), pltpu.HBM(land_thru.shape, land_thru.dtype)),
        in_specs=(HBM, HBM) + (SEM,) * 6 + (pl.BlockSpec(memory_space=pl.ANY),), out_specs=(HBM, HBM),
        input_output_aliases={0: 0, 1: 1}, compiler_params=pltpu.CompilerParams(has_side_effects=SIDE_EFFECT),
    )(src_thru, land_thru, *sems, after)[1]


def _gather_pass(land):
    half = land.shape[1] // 2

    def body(land_in, land_out, send_sems, recv_sems):
        x, y, c, chips = _place()

        def copy(k, slot, core):
            rows = land_out.at[slot, pl.ds(core * half, half)]
            return pltpu.make_async_remote_copy(src_ref=rows, dst_ref=rows, send_sem=send_sems.at[k], recv_sem=recv_sems.at[k],
                                                device_id=(x, y, 1 - c), device_id_type=MESH)
        sends = [copy(k, 2 * px + py, c) for k, (px, py) in enumerate(chips)]
        for cp in sends:
            cp.start()
        for k, (px, py) in enumerate(chips):
            copy(k, 2 * px + py, 1 - c).wait_recv()
        for cp in sends:
            cp.wait_send()

    return pl.pallas_call(
        body, out_shape=jax.ShapeDtypeStruct(land.shape, land.dtype), in_specs=[ANY], out_specs=ANY,
        scratch_shapes=[pltpu.SemaphoreType.DMA((3,)), pltpu.SemaphoreType.DMA((3,))], input_output_aliases={0: 0},
        name="gather_pass",
    )(land)


def _swap_cores(block, *, name, half_rows=None):
    shape = block.shape if half_rows is None else (block.shape[0], half_rows, block.shape[2])

    def body(src, out, send_sem, recv_sem):
        x, y, c, _ = _place()
        part = src if half_rows is None else src.at[:, pl.ds((1 - c) * half_rows, half_rows)]
        cp = pltpu.make_async_remote_copy(src_ref=part, dst_ref=out, send_sem=send_sem, recv_sem=recv_sem,
                                          device_id=(x, y, 1 - c), device_id_type=MESH)
        cp.start()
        cp.wait()

    return pl.pallas_call(
        body, out_shape=jax.ShapeDtypeStruct(shape, block.dtype), in_specs=[ANY], out_specs=ANY,
        scratch_shapes=[pltpu.SemaphoreType.DMA(()), pltpu.SemaphoreType.DMA(())], name=name,
    )(block)


def _scatter_copies(parts, land, send_sems, recv_sems):
    x, y, c, chips = _place()
    return [pltpu.make_async_remote_copy(src_ref=parts.at[2 * px + py], dst_ref=land.at[k], send_sem=send_sems[k],
                                         recv_sem=recv_sems[k], device_id=(px, py, c), device_id_type=MESH)
            for k, (px, py) in enumerate(chips)]


def _scatter_start(parts, tag):
    def body(src, land, *rest):
        for cp in _scatter_copies(src, land, rest[:3], rest[3:6]):
            cp.start()
        rest[8][...] = jnp.zeros(rest[8].shape, F32)

    sem = pltpu.SemaphoreType.DMA(())
    land_shape = (3,) + parts.shape[1:]
    return pl.pallas_call(
        body, name=f"scatter_start_{tag}",
        out_shape=(sem,) * 6 + (pltpu.HBM(parts.shape, parts.dtype), pltpu.HBM(land_shape, parts.dtype),
                                jax.ShapeDtypeStruct((8, LANES), F32)),
        in_specs=(HBM, HBM), out_specs=(SEM,) * 6 + (HBM, HBM, pl.BlockSpec(memory_space=pltpu.VMEM)),
        input_output_aliases={0: 6, 1: 7}, compiler_params=pltpu.CompilerParams(has_side_effects=SIDE_EFFECT),
    )(pltpu.with_memory_space_constraint(parts, pltpu.HBM),
      pltpu.with_memory_space_constraint(lax.empty(land_shape, parts.dtype), pltpu.HBM))


def _scatter_wait(started, after, tag):
    def body(src, land, *rest):
        for cp in _scatter_copies(src, land, rest[:3], rest[3:6]):
            cp.wait_send()
            cp.wait_recv()

    src_thru, land_thru = started[6], started[7]
    return pl.pallas_call(
        body, name=f"scatter_wait_{tag}",
        out_shape=(pltpu.HBM(src_thru.shape, src_thru.dtype), pltpu.HBM(land_thru.shape, land_thru.dtype)),
        in_specs=(HBM, HBM) + (SEM,) * 6 + (pl.BlockSpec(memory_space=pl.ANY),), out_specs=(HBM, HBM),
        input_output_aliases={0: 0, 1: 1}, compiler_params=pltpu.CompilerParams(has_side_effects=SIDE_EFFECT),
    )(src_thru, land_thru, *started[:6], after)[1]


def _sum_all_devices(vec):
    rows = vec.shape[0]

    def body(x_ref, total_ref, all_ref, send_sems, recv_sems, local_sem):
        x, y, c, chips = _place()
        me, sibling = (x, y, c), (x, y, 1 - c)

        def slot(px, py, pc):
            return all_ref.at[4 * px + 2 * py + pc]

        def copy(k, block, to, src=None):
            return pltpu.make_async_remote_copy(src_ref=slot(*block) if src is None else src, dst_ref=slot(*block),
                                                send_sem=send_sems.at[k], recv_sem=recv_sems.at[k], device_id=to,
                                                device_id_type=MESH)

        mine = pltpu.make_async_copy(x_ref, slot(*me), local_sem)
        mine.start()
        first = [copy(0, me, sibling, src=x_ref)] + [copy(1 + j, me, (*chip, c), src=x_ref) for j, chip in enumerate(chips)]
        for cp in first:
            cp.start()
        passed = [copy(4 + j, (*chip, c), sibling) for j, chip in enumerate(chips)]
        for j, chip in enumerate(chips):
            copy(1 + j, (*chip, c), me).wait_recv()
            passed[j].start()
        copy(0, sibling, me).wait_recv()
        for j, chip in enumerate(chips):
            copy(4 + j, (*chip, 1 - c), me).wait_recv()
        for cp in first + passed:
            cp.wait_send()
        mine.wait()
        acc = all_ref[0]
        for d in range(1, 8):
            acc = acc + all_ref[d]
        total_ref[...] = acc

    vmem = pl.BlockSpec(memory_space=pltpu.VMEM)
    return pl.pallas_call(
        body, out_shape=[jax.ShapeDtypeStruct((rows, LANES), F32), jax.ShapeDtypeStruct((8, rows, LANES), F32)],
        in_specs=[vmem], out_specs=[vmem, vmem],
        scratch_shapes=[pltpu.SemaphoreType.DMA((7,)), pltpu.SemaphoreType.DMA((7,)), pltpu.SemaphoreType.DMA(())],
        name="sum_all_devices", compiler_params=pltpu.CompilerParams(vmem_limit_bytes=VMEM_LIMIT_BYTES),
    )(vec)[0]


PACK_COLS = 1024
BIG = (("mlp_w_up", 2, 1024, True), ("mlp_w_down", 2, 1024, False), ("xa_w_kv", 2, 512, True), ("xa_w_q", 2, 256, False),
       ("xa_w_o", 2, 256, False), ("ab_w_out", 1, 256, False), ("cd_w_out", 1, 256, False), ("cd_w_in", 1, 576, True),
       ("ab_w_in", 1, 642, True))
ENTRIES = tuple((name, layer, rows, transposed) for name, layers, rows, transposed in BIG for layer in range(layers))
FIRST_ENTRIES = tuple(e for e in ENTRIES if e[0].startswith("ab_"))
LATER_ENTRIES = tuple(e for e in ENTRIES if not e[0].startswith("ab_"))


def _tile_rows(entry):
    return -(-entry[2] // 16) * 16


def _padded_rows(entries):
    return -(-sum(_tile_rows(e) for e in entries) // 32) * 32


SMALL = (("ab_norm", (1, 1024)), ("ab_conv_w", (1, 4, 512)), ("ab_conv_b", (1, 512)), ("lru_w_a", (1, 8, 64, 64)),
         ("lru_b_a", (1, 512)), ("lru_w_i", (1, 8, 64, 64)), ("lru_b_i", (1, 512)), ("lru_lambda", (1, 512)),
         ("fox_b_f", (1, 8)), ("cd_norm", (1, 1024)), ("cd_sink", (1, 8)), ("xa_norm", (2, 1024)),
         ("xa_mem_norm", (2, 1024)), ("mlp_norm", (2, 1024)), ("final_norm", (1024,)), ("loss", (1,)))
SPLIT_SMALL = ("ab_conv_w", "cd_norm")


def _pack_rows(parts, entries, dtype):
    lead = parts[0].shape[:-2]
    zeros = lambda rows: jnp.zeros(lead + (rows, PACK_COLS), dtype)
    pieces = [jnp.pad(p.astype(dtype), ((0, 0),) * len(lead) + ((0, _tile_rows(e) - e[2]), (0, 0))) for p, e in zip(parts, entries)]
    tail = _padded_rows(entries) - sum(_tile_rows(e) for e in entries)
    return jnp.concatenate(pieces + ([zeros(tail)] if tail else []), axis=len(lead))


def _unpack_rows(packed, entries):
    out, r0 = [], 0
    for e in entries:
        out.append(packed[..., r0:r0 + e[2], :])
        r0 += _tile_rows(e)
    return out


def _shard_rows(w, entries):
    return [(w[name][layer].T if transposed else w[name][layer]) for name, layer, _, transposed in entries]


def _shard_blocks(rows):
    out = {}
    for (name, layer, _, transposed), r in zip(ENTRIES, rows):
        out.setdefault(name, []).append(r.T if transposed else r)
    return {name: jnp.stack(layers) for name, layers in out.items()}


def _pack_small(vals):
    flat = jnp.concatenate([vals[name].astype(F32).reshape(-1) for name, _ in SMALL])
    size = -(-flat.shape[0] // (8 * LANES)) * (8 * LANES)
    return jnp.pad(flat, (0, size - flat.shape[0])).reshape(-1, LANES)


def _unpack_small(packed):
    flat, out, at = packed.reshape(-1), {}, 0
    for name, shape in SMALL:
        out[name] = flat[at:at + math.prod(shape)].reshape(shape)
        at += math.prod(shape)
    return out


def _chip_part(full, chip):
    width = full.shape[-1] // N_CHIPS
    return lax.dynamic_slice_in_dim(full, chip * width, width, axis=full.ndim - 1)


def _chip_embed(part, chip):
    full = jnp.zeros(part.shape[:-1] + (part.shape[-1] * N_CHIPS,), part.dtype)
    return lax.dynamic_update_slice_in_dim(full, part, chip * part.shape[-1], axis=part.ndim - 1)


SUBLAYER_KEYS = {"ab_w_in": ("ab", "w_in_t"), "ab_w_out": ("ab", "w_out"), "cd_w_in": ("cd", "w_in_t"), "cd_w_out": ("cd", "w_out"),
                 "xa_w_q": ("xa", "w_q"), "xa_w_kv": ("xa", "w_kv_t"), "xa_w_o": ("xa", "w_o"), "mlp_w_up": ("mlp", "w_up_t"),
                 "mlp_w_down": ("mlp", "w_down")}


def _sublayer(name, layer):
    block, leaf = SUBLAYER_KEYS[name]
    return (block if block in ("ab", "cd") else f"{block}{layer}"), leaf


def _set_big(params, full_rows):
    for (name, layer), rows in full_rows.items():
        block, leaf = _sublayer(name, layer)
        if name == "ab_w_in":
            rows = jnp.concatenate([rows, jnp.zeros((LANES - 8, PACK_COLS), rows.dtype)])
        params[block][leaf] = rows


def _build_params(full_rows, w):
    pad = lambda a: jnp.pad(a, ((0, 0), (0, LANES - a.shape[1])))
    params = {
        "ab": dict(norm=w["ab_norm"], conv_w=w["ab_conv_w"][0], conv_b=w["ab_conv_b"], w_a=_block_diag(w["lru_w_a"][0]),
                   b_a=w["lru_b_a"], w_i=_block_diag(w["lru_w_i"][0]), b_i=w["lru_b_i"], lam=w["lru_lambda"],
                   b_f=w["fox_b_f"].reshape(8, 1)),
        "cd": dict(norm=w["cd_norm"], sink=pad(w["cd_sink"])),
        "final_norm": w["final_norm"].reshape(1, D_MODEL),
    }
    for layer in range(2):
        params[f"xa{layer}"] = dict(norm=w["xa_norm"][layer:layer + 1], mem_norm=w["xa_mem_norm"][layer:layer + 1])
        params[f"mlp{layer}"] = dict(norm=w["mlp_norm"][layer:layer + 1])
    _set_big(params, full_rows)
    return params


def _grad_rows(g, entries=ENTRIES):
    out = []
    for name, layer, _, _ in entries:
        block, leaf = _sublayer(name, layer)
        out.append(g[block][leaf])
    return out


def _reduce_group(entry):
    name, layer = entry[0], entry[1]
    if name.startswith("ab_"):
        return 2
    return 0 if layer == 1 or name.startswith("cd_") else 1


REDUCE_GROUPS = tuple(tuple(e for e in ENTRIES if _reduce_group(e) == group) for group in range(3))


def _reduce_tile(half):
    return max(t for t in range(16, 1025, 16) if half % t == 0)


def _reduce_start(grads_by_chip, core, chip, tag):
    half = grads_by_chip.shape[1] // 2
    tile = _reduce_tile(half)
    steps = half // tile
    place = jnp.stack([core, chip]).astype(jnp.int32)
    got = _swap_cores(grads_by_chip, name=f"swap_cores_{tag}", half_rows=half)
    block = (1, tile, PACK_COLS)

    def sum_cores(place_ref, mine_ref, got_ref, f32_ref, bf16_ref):
        total = mine_ref[...].astype(F32) + got_ref[...].astype(F32)
        f32_ref[...] = total
        bf16_ref[...] = total.astype(BF16)

    pair32, pair16 = pl.pallas_call(
        sum_cores, name=f"sum_cores_{tag}",
        grid_spec=pltpu.PrefetchScalarGridSpec(
            num_scalar_prefetch=1, grid=(N_CHIPS, steps),
            in_specs=[pl.BlockSpec(block, lambda s, i, place_ref: (s, place_ref[0] * steps + i, 0)),
                      pl.BlockSpec(block, lambda s, i, place_ref: (s, i, 0))],
            out_specs=[pl.BlockSpec(block, lambda s, i, place_ref: (s, i, 0))] * 2),
        out_shape=[jax.ShapeDtypeStruct((N_CHIPS, half, PACK_COLS), F32), jax.ShapeDtypeStruct((N_CHIPS, half, PACK_COLS), BF16)],
        compiler_params=_params("arbitrary", "arbitrary"),
    )(place, grads_by_chip, got)
    return pair32, _scatter_start(pair16, tag), place


def _reduce_finish(state, core, tag, after):
    pair32, started, place = state
    half = pair32.shape[1]
    tile = _reduce_tile(half)
    landed = _scatter_wait(started, after, tag)

    def sum_chips(place_ref, own_ref, landed_ref, out_ref):
        out_ref[...] = own_ref[0] + landed_ref[0].astype(F32) + landed_ref[1].astype(F32) + landed_ref[2].astype(F32)

    done = pl.pallas_call(
        sum_chips, name=f"sum_chips_{tag}",
        grid_spec=pltpu.PrefetchScalarGridSpec(
            num_scalar_prefetch=1, grid=(half // tile,),
            in_specs=[pl.BlockSpec((1, tile, PACK_COLS), lambda i, place_ref: (place_ref[1], i, 0)),
                      pl.BlockSpec((3, tile, PACK_COLS), lambda i, place_ref: (0, i, 0))],
            out_specs=pl.BlockSpec((tile, PACK_COLS), lambda i, place_ref: (i, 0))),
        out_shape=jax.ShapeDtypeStruct((half, PACK_COLS), F32), compiler_params=_params("arbitrary"),
    )(place, pair32, landed)
    theirs = _swap_cores(done, name=f"share_halves_{tag}")
    return jnp.where(core == 0, jnp.concatenate([done, theirs]), jnp.concatenate([theirs, done]))


def kernel(x, mem, ab_norm, ab_w_in, ab_conv_w, ab_conv_b, lru_w_a, lru_b_a, lru_w_i, lru_b_i, lru_lambda, fox_b_f, ab_w_out, cd_norm, cd_w_in, cd_sink, cd_w_out, xa_norm, xa_mem_norm, xa_w_q, xa_w_kv, xa_w_o, mlp_norm, mlp_w_up, mlp_w_down, final_norm, loss_target, m_ab_norm, m_ab_w_in, m_ab_conv_w, m_ab_conv_b, m_lru_w_a, m_lru_b_a, m_lru_w_i, m_lru_b_i, m_lru_lambda, m_fox_b_f, m_ab_w_out, m_cd_norm, m_cd_w_in, m_cd_sink, m_cd_w_out, m_xa_norm, m_xa_mem_norm, m_xa_w_q, m_xa_w_kv, m_xa_w_o, m_mlp_norm, m_mlp_w_up, m_mlp_w_down, m_final_norm, v_ab_norm, v_ab_w_in, v_ab_conv_w, v_ab_conv_b, v_lru_w_a, v_lru_b_a, v_lru_w_i, v_lru_b_i, v_lru_lambda, v_fox_b_f, v_ab_w_out, v_cd_norm, v_cd_w_in, v_cd_sink, v_cd_w_out, v_xa_norm, v_xa_mem_norm, v_xa_w_q, v_xa_w_kv, v_xa_w_o, v_mlp_norm, v_mlp_w_up, v_mlp_w_down, v_final_norm):
    given = dict(locals())
    weight_names = [name for name, _ in SMALL if name != "loss"] + [name for name, _, _, _ in BIG]
    w = {name: given[name] for name in weight_names}
    chip = 2 * lax.axis_index("x") + lax.axis_index("y")
    core = lax.axis_index("c")

    slot = lax.broadcasted_iota(jnp.int32, (N_CHIPS, 1, 1), 0)

    def whole(entries, mine, gathered):
        return {(name, layer): jnp.where(slot == chip, own[None], got).reshape(N_CHIPS * rows, PACK_COLS)
                for (name, layer, rows, _), own, got in zip(entries, _unpack_rows(mine, entries), _unpack_rows(gathered, entries))}

    mine_first = _pack_rows(_shard_rows(w, FIRST_ENTRIES), FIRST_ENTRIES, BF16)
    mine_later = _pack_rows(_shard_rows(w, LATER_ENTRIES), LATER_ENTRIES, BF16)
    first = _gather_chips(mine_first)
    started = _gather_start(mine_later)
    owner = (core == 0).astype(F32)
    quarters = {name: _chip_embed(w[name], chip) * owner for name in SPLIT_SMALL}
    zeros = {name: jnp.zeros(shape, F32) for name, shape in SMALL}
    small_full = _unpack_small(_sum_all_devices(_pack_small({**zeros, **quarters})))
    params = _build_params(whole(FIRST_ENTRIES, mine_first, first),
                           {**w, "ab_conv_w": small_full["ab_conv_w"], "cd_norm": small_full["cd_norm"]})
    params["ab"]["norm"] = params["ab"]["norm"] + started[8][0, 0]

    def complete(h):
        _set_big(params, whole(LATER_ENTRIES, mine_later, _gather_pass(_gather_wait(started, after=h))))

    reducing = {}

    def grads_ready(group, g):
        entries = REDUCE_GROUPS[group]
        by_chip = _pack_rows([r.reshape(N_CHIPS, e[2], PACK_COLS) for r, e in zip(_grad_rows(g, entries), entries)], entries, BF16)
        reducing[group] = _reduce_start(by_chip, core, chip, f"g{group}")
        if group < 2:
            block, leaf = ("mlp0", "w_down") if group == 0 else ("ab", "w_out")
            params[block][leaf] = params[block][leaf] + reducing[group][1][8][0, 0].astype(BF16)

    loss_part, grad_x, g = _local_step(x, mem, loss_target, params, complete, grads_ready)
    grads_ready(2, g)
    per_layer = {name: {} for name, _, _, _ in BIG}
    after = grad_x
    for group, entries in enumerate(REDUCE_GROUPS):
        rows = _unpack_rows(_reduce_finish(reducing[group], core, f"g{group}", after=after), entries)
        for (name, layer, _, transposed), r in zip(entries, rows):
            grad = r.T if transposed else r
            state = [given[prefix + name][layer] for prefix in ("", "m_", "v_")]
            per_layer[name][layer] = (grad,) + tuple(_adamw(state[0], grad, state[1], state[2], name=f"adamw_{name}{layer}"))
            after = per_layer[name][layer][1]
    stacked = lambda name, idx: jnp.stack([per_layer[name][layer][idx] for layer in sorted(per_layer[name])])
    grads, delta, new_m, new_v = ({name: stacked(name, idx) for name, _, _, _ in BIG} for idx in range(4))
    pair = lambda key, leaf: jnp.stack([g[f"{key}0"][leaf], g[f"{key}1"][leaf]])

    small_local = {"ab_norm": g["ab"]["norm"], "ab_conv_w": g["ab"]["conv_w"], "ab_conv_b": g["ab"]["conv_b"],
                   "lru_w_a": g["ab"]["w_a"], "lru_b_a": g["ab"]["b_a"], "lru_w_i": g["ab"]["w_i"], "lru_b_i": g["ab"]["b_i"],
                   "lru_lambda": g["ab"]["lam"], "fox_b_f": g["ab"]["b_f"], "cd_norm": g["cd"]["norm"], "cd_sink": g["cd"]["sink"],
                   "xa_norm": pair("xa", "norm"), "xa_mem_norm": pair("xa", "mem_norm"), "mlp_norm": pair("mlp", "norm"),
                   "final_norm": g["final_norm"], "loss": loss_part[0, :1]}
    small_sum_packed = _sum_all_devices(_pack_small(small_local))
    small_sum = _unpack_small(small_sum_packed)
    loss = small_sum["loss"][0]

    def small_state(prefix):
        vals = {name: (given[prefix + name] if name != "loss" else jnp.zeros((1,), F32)) for name, _ in SMALL}
        for name in SPLIT_SMALL:
            vals[name] = _chip_embed(vals[name], chip)
        return _pack_small(vals)
    packed = _adamw(small_state(""), small_sum_packed, small_state("m_"), small_state("v_"), name="adamw_small")
    for store, vals in zip((delta, new_m, new_v), packed):
        for name, val in _unpack_small(vals).items():
            store[name] = _chip_part(val, chip) if name in SPLIT_SMALL else val
    for name in SPLIT_SMALL:
        small_sum[name] = _chip_part(small_sum[name], chip)
    grads.update({name: small_sum[name] for name, _ in SMALL})

    order = ["ab_norm", "ab_w_in", "ab_conv_w", "ab_conv_b", "lru_w_a", "lru_b_a", "lru_w_i", "lru_b_i", "lru_lambda", "fox_b_f",
             "ab_w_out", "cd_norm", "cd_w_in", "cd_sink", "cd_w_out", "xa_norm", "xa_mem_norm", "xa_w_q", "xa_w_kv", "xa_w_o",
             "mlp_norm", "mlp_w_up", "mlp_w_down", "final_norm"]
    return (loss, grad_x, *[grads[n] for n in order], *[delta[n] for n in order], *[new_m[n] for n in order],
            *[new_v[n] for n in order])
```

```python
import functools
import math

import jax
import jax.numpy as jnp
from jax import lax
from jax.experimental import pallas as pl
from jax.experimental.pallas import tpu as pltpu

F32 = jnp.float32
BF16 = jnp.bfloat16
MESH = pl.DeviceIdType.MESH

D_MODEL = 1024
SEQ = 2048
HEAD_DIM = 64
LRU_WIDTH = 512
LRU_BLOCKS = 8
LRU_C = 8.0
CONV_WIDTH = 4
ATT_W = 512
SWA_KW = 128
SWA_WINDOW = 128
DIL_PATTERN = ((128, 1), (512, 4), (2048, 16))
MEM_LEN = 256
XA_HEADS = 4
XA_HEAD_DIM = 256
D_FF = 4096
ROPE_THETA = 10000.0
EPS = 1e-6
N_CHIPS = 4
LANES = 128

ADAM_LR, ADAM_B1, ADAM_B2, ADAM_EPS, ADAM_WD, ADAM_STEP = 0.001, 0.9, 0.999, 1e-08, 0.01, 10

VMEM_LIMIT_BYTES = 56 * 1024 * 1024
MASKED = -1e30
ROW_TILE = 512
LRU_CHUNK = 512
ATT_BLOCK = 128
BAND_ROWS = 256
ATT_CHUNK = 512
CAUSAL_ROWS = 256


def _params(*sem):
    return pltpu.CompilerParams(dimension_semantics=sem, vmem_limit_bytes=VMEM_LIMIT_BYTES)


def _rowwise(fn, rows, consts=(), out_rows=(), out_accs=(), *, name, tile=ROW_TILE, row_maps=None):
    rows = [r if isinstance(r, tuple) else (r, r.shape[1], 0) for r in rows]
    consts = list(consts)
    nr, nc, no = len(rows), len(consts), len(out_rows)
    dealt_in = [r[3] if isinstance(r[0], str) else None for r in rows]
    dealt_out = [o[2] if len(o) == 3 else None for o in out_rows]
    total = next(r[0].shape[0] for r in rows if not isinstance(r[0], str))
    tile = min(tile, total)
    assert total % tile == 0
    n = total // tile
    n_acc = len(out_accs)

    def body(*refs):
        scratch = list(refs[nr + nc + no + n_acc:])
        vals = []
        for ref, d, row in zip(refs[:nr], dealt_in, rows):
            if d is None:
                vals.append(ref[...])
                continue
            sc, width = scratch.pop(0), row[2]
            for r in range(d):
                for c in range(width // LANES):
                    lanes = slice(r * width + c * LANES, r * width + (c + 1) * LANES)
                    sc.at[c][pl.ds(r, tile // d, stride=d), :] = ref[:, lanes].astype(F32)
            vals.append(jnp.concatenate([sc[c] for c in range(width // LANES)], axis=1))
        outs = fn(*vals, *[r[...] for r in refs[nr:nr + nc]])
        outs = tuple(outs) if isinstance(outs, (tuple, list)) else (outs,)
        for ref, val, d, spec in zip(refs[nr + nc:nr + nc + no], outs[:no], dealt_out, out_rows):
            if d is None:
                ref[...] = val.astype(ref.dtype)
                continue
            sc, width = scratch.pop(0), spec[0]
            for c in range(width // LANES):
                sc[c] = val[:, c * LANES:(c + 1) * LANES].astype(F32)
            for r in range(d):
                for c in range(width // LANES):
                    lanes = slice(r * width + c * LANES, r * width + (c + 1) * LANES)
                    ref[:, lanes] = sc.at[c][pl.ds(r, tile // d, stride=d), :].astype(ref.dtype)
        for ref, val in zip(refs[nr + nc + no:nr + nc + no + n_acc], outs[no:]):
            @pl.when(pl.program_id(0) == 0)
            def _(ref=ref):
                ref[...] = jnp.zeros(ref.shape, ref.dtype)
            ref[...] += val

    in_specs, args, scratch_shapes = [], [], []
    for idx, row in enumerate(rows):
        if isinstance(row[0], str):
            _, arr, width, d = row
            in_specs.append(pl.BlockSpec((tile // d, d * width), lambda i: (i, 0)))
            scratch_shapes.append(pltpu.VMEM((width // LANES, tile, LANES), F32))
        elif row_maps is not None and row_maps[idx] is not None:
            arr, width, _ = row
            in_specs.append(pl.BlockSpec((tile, width), row_maps[idx]))
        else:
            arr, width, cb = row
            in_specs.append(pl.BlockSpec((tile, width), functools.partial(lambda i, cb: (i, cb), cb=cb)))
        args.append(arr)
    in_specs += [pl.BlockSpec(c.shape, lambda i: (0, 0)) for c in consts]
    out_shape, out_specs = [], []
    for spec, d in zip(out_rows, dealt_out):
        w, dt = spec[0], spec[1]
        if d is None:
            out_shape.append(jax.ShapeDtypeStruct((total, w), dt))
            out_specs.append(pl.BlockSpec((tile, w), lambda i: (i, 0)))
        else:
            out_shape.append(jax.ShapeDtypeStruct((total // d, d * w), dt))
            out_specs.append(pl.BlockSpec((tile // d, d * w), lambda i: (i, 0)))
            scratch_shapes.append(pltpu.VMEM((w // LANES, tile, LANES), F32))
    out_shape += [jax.ShapeDtypeStruct(s, F32) for s in out_accs]
    out_specs += [pl.BlockSpec(s, lambda i: (0, 0)) for s in out_accs]
    return pl.pallas_call(
        body, grid=(n,), in_specs=in_specs, out_specs=out_specs, out_shape=out_shape, scratch_shapes=scratch_shapes, name=name,
        compiler_params=_params("arbitrary"),
    )(*args, *consts)


MATMUL_VMEM_BYTES = 40 * 1024 * 1024


def _matmul_tiles(m, n, k, tm, tn, out_bytes):
    tm, tn, tk = min(tm, m), min(tn, n), k

    def need():
        return 2 * (2 * tk * (tm + tn) + tm * tn * out_bytes) + (0 if tk == k else 4 * tm * tn)

    while need() > MATMUL_VMEM_BYTES:
        if tm >= tn and tm % 256 == 0:
            tm //= 2
        elif tn % 256 == 0:
            tn //= 2
        else:
            tk //= 2
    return tm, tn, tk


def _matmul(a, b, *, ta=False, tb=False, outs=(F32,), epilogue=None, extras=(), consts=(), sums=(), whole_rows=False,
            tm=1024, tn=1024, name):
    m, k = (a.shape[1], a.shape[0]) if ta else a.shape
    n = b.shape[0] if tb else b.shape[1]
    assert (b.shape[1] if tb else b.shape[0]) == k
    outs = [o if isinstance(o, tuple) else (o, None) for o in outs]
    out_bytes = sum(jnp.dtype(dt).itemsize for dt, w in outs if w is None) + sum(e.dtype.itemsize for e in extras)
    tm, tn, tk = _matmul_tiles(m, n, k, tm, tn, out_bytes)
    assert m % tm == 0 and n % tn == 0 and k % tk == 0, (name, m, n, k)
    nk = k // tk
    ne, nc, no = len(extras), len(consts), len(outs)
    assert tn == n or not (sums or whole_rows or any(w is not None for _, w in outs)), name
    dims = (((0 if ta else 1,), (1 if tb else 0,)), ((), ()))

    def body(*refs):
        a_ref, b_ref = refs[:2]
        e_refs, o_refs = refs[2:2 + ne + nc], refs[2 + ne + nc:2 + ne + nc + no]
        s_refs = refs[2 + ne + nc + no:2 + ne + nc + no + len(sums)]

        def finish(acc):
            vals = (acc,) if epilogue is None else epilogue(acc, *[e[...] for e in e_refs])
            for ref, val in zip(o_refs, vals[:no]):
                ref[...] = val.astype(ref.dtype)
            for ref, val in zip(s_refs, vals[no:]):
                @pl.when(pl.program_id(0) == 0)
                def _(ref=ref, val=val):
                    ref[...] = val

                @pl.when(pl.program_id(0) > 0)
                def _(ref=ref, val=val):
                    ref[...] += val

        prod = lax.dot_general(a_ref[...], b_ref[...], dims, preferred_element_type=F32)
        if nk == 1:
            finish(prod)
        else:
            acc_ref = refs[-1]
            step = pl.program_id(2)

            @pl.when(step == 0)
            def _():
                acc_ref[...] = prod

            @pl.when(step > 0)
            def _():
                acc_ref[...] += prod

            @pl.when(step == nk - 1)
            def _():
                finish(acc_ref[...])

    a_spec = pl.BlockSpec((tk, tm), lambda i, j, s: (s, i)) if ta else pl.BlockSpec((tm, tk), lambda i, j, s: (i, s))
    b_spec = pl.BlockSpec((tn, tk), lambda i, j, s: (j, s)) if tb else pl.BlockSpec((tk, tn), lambda i, j, s: (s, j))
    tile_spec = pl.BlockSpec((tm, tn), lambda i, j, s: (i, j))
    whole = lambda shape: pl.BlockSpec(shape, lambda i, j, s: (0, 0))
    return pl.pallas_call(
        body, grid=(m // tm, n // tn, nk),
        in_specs=[a_spec, b_spec] + [tile_spec] * ne + [whole(c.shape) for c in consts],
        out_specs=[tile_spec if w is None else pl.BlockSpec((tm, w), lambda i, j, s: (i, 0)) for _, w in outs]
        + [whole(shape) for shape in sums],
        out_shape=[jax.ShapeDtypeStruct((m, n if w is None else w), dt) for dt, w in outs]
        + [jax.ShapeDtypeStruct(shape, F32) for shape in sums],
        scratch_shapes=[pltpu.VMEM((tm, tn), F32)] if nk > 1 else [],
        name=name, compiler_params=_params("arbitrary" if sums else "parallel", "parallel", "arbitrary"),
    )(a, b, *extras, *consts)


def _split3(x):
    x1 = x.astype(BF16)
    r1 = x - x1.astype(F32)
    x2 = r1.astype(BF16)
    x3 = (r1 - x2.astype(F32)).astype(BF16)
    return x1, x2, x3


def _dot_exact(x, e):
    return sum(jnp.dot(p, e, preferred_element_type=F32) for p in _split3(x))


def _head_expand(heads, width):
    dh = width // heads
    rows = jnp.arange(LANES)[:, None]
    cols = jnp.arange(width)[None, :]
    return (cols // dh == rows).astype(BF16)


def _rstd(x):
    return lax.rsqrt(jnp.mean(x * x, axis=-1, keepdims=True) + EPS)


def _rms_fwd(x, gain, *, name):
    return _rowwise(lambda x, g: x * _rstd(x) * g, [x], [gain], [(x.shape[1], BF16)], name=name)[0]


def _rms_bwd_vals(x, dhn, gain):
    r = _rstd(x)
    xh = x * r
    dxh = dhn * gain
    dx = r * (dxh - xh * jnp.mean(dxh * xh, axis=-1, keepdims=True))
    return dx, jnp.sum(dhn * xh, axis=0, keepdims=True)


def _norm_input_grad(dz, w, x, dres, gain, *, tb=False, name):
    def epilogue(dhn, x, dres, g):
        dx, dg = _rms_bwd_vals(x, dhn, g)
        dh = dres + dx
        return dh, dh, dg
    return _matmul(dz, w, tb=tb, outs=(F32, BF16), extras=(x, dres), consts=(gain,), sums=((1, x.shape[1]),), epilogue=epilogue,
                   name=name)


def _loss_and_grad(h, target, gain, *, name):
    def fn(h, tgt, g):
        r = _rstd(h)
        err = h * r * g - tgt
        loss = 0.5 * jnp.sum(jnp.mean(err * err, axis=-1, keepdims=True), axis=0, keepdims=True)
        dx, dg = _rms_bwd_vals(h, err * (1.0 / D_MODEL), g)
        return dx, dx, jnp.broadcast_to(loss, (1, LANES)), dg
    d = h.shape[1]
    return _rowwise(fn, [h, target], [gain], [(d, F32), (d, BF16)], [(1, LANES), (1, d)], name=name)


def _adamw(w, g, m, v, *, name):
    rows, cols = w.shape
    tile = rows
    for cand in (256, 128, 64, 32, 16, 8):
        if rows % cand == 0 and rows > cand:
            tile = cand
            break
    bc1 = 1.0 - ADAM_B1 ** ADAM_STEP
    bc2 = 1.0 - ADAM_B2 ** ADAM_STEP

    def fn(w, g, m, v):
        m2 = ADAM_B1 * m + (1.0 - ADAM_B1) * g
        v2 = ADAM_B2 * v + (1.0 - ADAM_B2) * (g * g)
        delta = -ADAM_LR * ((m2 / bc1) / (jnp.sqrt(v2 / bc2) + ADAM_EPS) + ADAM_WD * w)
        return delta, m2, v2
    return _rowwise(fn, [w, g, m, v], [], [(cols, F32)] * 3, name=name, tile=tile)


def _attn_specs(arr, width, cb, classes, rows_block, whole):
    ncols = arr.shape[2] // (classes * width)
    if whole:
        return pl.BlockSpec((1, arr.shape[1], width), lambda n, r, i: (n, 0, r * ncols + cb))
    return pl.BlockSpec((1, rows_block, width), lambda n, r, i: (n, i, r * ncols + cb))


def _attn_tiles(mode, lq, lk, backward=False):
    if mode == "full":
        return min(lq, 256), min(lq, 256), lk
    if mode == "band":
        tq = min(BAND_ROWS, lq)
        rows = tq if backward else ATT_BLOCK
        return tq, rows, min(rows + ATT_BLOCK, lk)
    return CAUSAL_ROWS, CAUSAL_ROWS, ATT_CHUNK


def _window(mode, row0, j, rows, win, lk):
    if mode == "causal":
        return pl.multiple_of(j * win, win), row0 // win + 1
    if mode == "band":
        return pl.multiple_of(jnp.clip(row0 + rows - win, 0, lk - win), ATT_BLOCK), 1
    return 0, 1


def _allowed(mode, row0, start, rows, win, max_dist):
    if mode == "full":
        return None
    dist = (row0 + lax.broadcasted_iota(jnp.int32, (rows, win), 0)) - (start + lax.broadcasted_iota(jnp.int32, (rows, win), 1))
    ok = dist >= 0
    if max_dist is not None:
        ok = ok & (dist <= max_dist)
    return ok


def _head_groups(heads, dh):
    gw = max(LANES, dh)
    return gw, gw // dh, heads // (gw // dh)


def _own_lanes(rows, gw, dh, u):
    return lax.broadcasted_iota(jnp.int32, (rows, gw), 1) // dh == u


def _only_head(x, own, per, u):
    return x if per == 1 else jnp.where(own[u], x, jnp.zeros_like(x))


def _step_scores(qz, kw, kb_row, ok, scale):
    s = lax.dot_general(qz, kw, (((1,), (1,)), ((), ())), preferred_element_type=F32) * scale
    if kb_row is not None:
        s = s - kb_row
    if ok is not None:
        s = jnp.where(ok, s, MASKED)
    return s


def _attn_fwd(q, k, v, kb=None, *, classes=1, heads, dh, mode, max_dist=None, bf16_copy=False, name):
    (qa, qc), (ka, kc), (va, vc) = q, k, v
    n, lq, lk = qa.shape[0], qa.shape[1], ka.shape[1]
    width = heads * dh
    tq, rows, win = _attn_tiles(mode, lq, lk)
    gw, per, groups = _head_groups(heads, dh)
    scale = dh ** -0.5
    has_kb = kb is not None
    single = mode != "causal"

    def body(*refs):
        for sub in range(tq // rows):
            part(refs, pl.program_id(2) * tq + sub * rows, slice(sub * rows, (sub + 1) * rows))

    def part(refs, row0, rs):
        q_ref, k_ref, v_ref = refs[:3]
        kb_ref = refs[3] if has_kb else None
        n_in = 4 if has_kb else 3
        o_ref, lse_ref = refs[n_in:n_in + 2]
        o16_ref = refs[n_in + 2] if bf16_copy else None
        lane = lax.broadcasted_iota(jnp.int32, (rows, LANES), 1)
        own = [_own_lanes(rows, gw, dh, u) for u in range(per)]

        def step(j, carry):
            m_in, l_in = carry
            m_out, l_out = m_in, l_in
            start, _ = _window(mode, row0, j, rows, win, lk)
            ok = _allowed(mode, row0, start, rows, win, max_dist)
            for g in range(groups):
                cols = slice(g * gw, (g + 1) * gw)
                qg = q_ref[0, rs, cols]
                kw = k_ref[0, pl.ds(start, win), cols]
                vw = v_ref[0, pl.ds(start, win), cols]
                alpha_g = new_g = None
                for u in range(per):
                    h = g * per + u
                    kb_row = kb_ref[0, h, pl.ds(j, 1), :] if has_kb else None
                    s = _step_scores(_only_head(qg, own, per, u), kw, kb_row, ok, scale)
                    m_new = jnp.max(s, axis=1, keepdims=True)
                    if not single:
                        m_old = m_in[:, h:h + 1]
                        m_new = jnp.maximum(m_old, m_new)
                        alpha = jnp.exp(m_old - m_new)
                        alpha_g = alpha if u == 0 else jnp.where(own[u], alpha, alpha_g)
                    p = jnp.exp(s - m_new)
                    l_new = jnp.sum(p, axis=1, keepdims=True)
                    r = jnp.dot(p.astype(BF16), vw, preferred_element_type=F32)
                    if single:
                        r = r * (1.0 / l_new)
                    else:
                        l_new = alpha * l_in[:, h:h + 1] + l_new
                    new_g = r if u == 0 else jnp.where(own[u], r, new_g)
                    m_out = jnp.where(lane == h, m_new, m_out)
                    l_out = jnp.where(lane == h, l_new, l_out)
                o_ref[0, rs, cols] = new_g if single else alpha_g * o_ref[0, rs, cols] + new_g
                if bf16_copy:
                    o16_ref[0, rs, cols] = new_g.astype(BF16)
            return m_out, l_out

        carry = (jnp.full((rows, LANES), MASKED, F32), jnp.zeros((rows, LANES), F32))
        if single:
            m_all, l_all = step(0, carry)
            l_all = jnp.where(lane < heads, l_all, 1.0)
        else:
            o_ref[...] = jnp.zeros(o_ref.shape, F32)
            m_all, l_all = lax.fori_loop(0, _window(mode, row0, 0, rows, win, lk)[1], step, carry)
            l_all = jnp.where(lane < heads, l_all, 1.0)
            inv = 1.0 / l_all
            for g in range(groups):
                cols = slice(g * gw, (g + 1) * gw)
                inv_g = inv[:, g * per:g * per + 1]
                for u in range(1, per):
                    inv_g = jnp.where(own[u], inv[:, g * per + u:g * per + u + 1], inv_g)
                o_ref[0, rs, cols] = o_ref[0, rs, cols] * inv_g
        lse_ref[0, rs, :] = jnp.where(lane < heads, m_all + jnp.log(l_all), 0.0)

    in_specs = [_attn_specs(qa, width, qc, classes, tq, False), _attn_specs(ka, width, kc, classes, win, True),
                _attn_specs(va, width, vc, classes, win, True)]
    args = [qa, ka, va]
    if has_kb:
        in_specs.append(pl.BlockSpec((1,) + kb.shape[1:], lambda n_, r, i: (n_, 0, 0, 0)))
        args.append(kb)
    out_shape = [jax.ShapeDtypeStruct((n, lq, classes * width), F32), jax.ShapeDtypeStruct((n, lq, classes * LANES), F32)]
    out_specs = [pl.BlockSpec((1, tq, width), lambda n_, r, i: (n_, i, r)), pl.BlockSpec((1, tq, LANES), lambda n_, r, i: (n_, i, r))]
    if bf16_copy:
        assert single
        out_shape.append(jax.ShapeDtypeStruct((n, lq, classes * width), BF16))
        out_specs.append(out_specs[0])
    return pl.pallas_call(
        body, grid=(n, classes, lq // tq), in_specs=in_specs, out_specs=out_specs, out_shape=out_shape, name=name,
        compiler_params=_params("parallel", "parallel", "arbitrary"),
    )(*args)


def _attn_bwd(q, k, v, do, lse, delta, kb=None, *, classes=1, heads, dh, mode, max_dist=None, dq_dtype=F32, name):
    (qa, qc), (ka, kc), (va, vc), (da, dc) = q, k, v, do
    n, lq, lk = qa.shape[0], qa.shape[1], ka.shape[1]
    width = heads * dh
    tq, rows, win = _attn_tiles(mode, lq, lk, backward=True)
    gw, per, groups = _head_groups(heads, dh)
    scale = dh ** -0.5
    has_kb = kb is not None
    single = mode != "causal"
    nt =(((1,), (1,)), ((), ()))
    tn = (((0,), (0,)), ((), ()))

    def body(*refs):
        n_in = 7 if has_kb else 6
        dk_ref, dv_ref = refs[n_in + 1:n_in + 3]

        @pl.when(pl.program_id(2) == 0)
        def _():
            dk_ref[...] = jnp.zeros(dk_ref.shape, F32)
            dv_ref[...] = jnp.zeros(dv_ref.shape, F32)
            if has_kb:
                refs[n_in + 3][...] = jnp.zeros(refs[n_in + 3].shape, F32)

        for sub in range(tq // rows):
            part(refs, pl.program_id(2) * tq + sub * rows, slice(sub * rows, (sub + 1) * rows))

    def part(refs, row0, rs):
        q_ref, k_ref, v_ref, do_ref, lse_ref, dl_ref = refs[:6]
        kb_ref = refs[6] if has_kb else None
        n_in = 7 if has_kb else 6
        dq_ref, dk_ref, dv_ref = refs[n_in:n_in + 3]
        dkb_ref, drow_ref = refs[n_in + 3:n_in + 5] if has_kb else (None, None)
        lse_all = lse_ref[0, rs, :]
        dl_all = dl_ref[0, rs, :]
        lane = lax.broadcasted_iota(jnp.int32, (rows, LANES), 1)
        own = [_own_lanes(rows, gw, dh, u) for u in range(per)]

        def step(j, drow_all):
            start, _ = _window(mode, row0, j, rows, win, lk)
            ok = _allowed(mode, row0, start, rows, win, max_dist)
            for g in range(groups):
                cols = slice(g * gw, (g + 1) * gw)
                qg = q_ref[0, rs, cols]
                dog = do_ref[0, rs, cols]
                kw = k_ref[0, pl.ds(start, win), cols]
                vw = v_ref[0, pl.ds(start, win), cols]
                dq_g = dk_w = dv_w = None
                for u in range(per):
                    h = g * per + u
                    qz, doz = _only_head(qg, own, per, u), _only_head(dog, own, per, u)
                    kb_row = kb_ref[0, h, pl.ds(j, 1), :] if has_kb else None
                    p = jnp.exp(_step_scores(qz, kw, kb_row, ok, scale) - lse_all[:, h:h + 1])
                    dp = lax.dot_general(doz, vw, nt, preferred_element_type=F32)
                    ds = p * (dp - dl_all[:, h:h + 1])
                    dsb = ds.astype(BF16)
                    r = jnp.dot(dsb, kw, preferred_element_type=F32)
                    dq_g = r if u == 0 else jnp.where(own[u], r, dq_g)
                    dk_u = lax.dot_general(dsb, qz, tn, preferred_element_type=F32)
                    dv_u = lax.dot_general(p.astype(BF16), doz, tn, preferred_element_type=F32)
                    dk_w = dk_u if u == 0 else dk_w + dk_u
                    dv_w = dv_u if u == 0 else dv_w + dv_u
                    if has_kb:
                        dkb_ref[0, h, pl.ds(j, 1), :] += -jnp.sum(ds, axis=0, keepdims=True)
                        drow_all = drow_all + jnp.where(lane == h, jnp.sum(ds, axis=1, keepdims=True), 0.0)
                dk_ref[0, pl.ds(start, win), cols] += dk_w * scale
                dv_ref[0, pl.ds(start, win), cols] += dv_w
                if single:
                    dq_ref[0, rs, cols] = (dq_g * scale).astype(dq_ref.dtype)
                else:
                    dq_ref[0, rs, cols] += dq_g * scale
            return drow_all

        drow_all = jnp.zeros((rows, LANES), F32)
        if single:
            drow_all = step(0, drow_all)
        else:
            dq_ref[...] = jnp.zeros(dq_ref.shape, F32)
            drow_all = lax.fori_loop(0, _window(mode, row0, 0, rows, win, lk)[1], step, drow_all)
        if has_kb:
            drow_ref[0, rs, :] = drow_all

    row_spec = functools.partial(_attn_specs, classes=classes, rows_block=tq, whole=False)
    in_specs = [row_spec(qa, width, qc), _attn_specs(ka, width, kc, classes, win, True), _attn_specs(va, width, vc, classes, win, True),
                row_spec(da, width, dc), row_spec(lse, LANES, 0), row_spec(delta, LANES, 0)]
    args = [qa, ka, va, da, lse, delta]
    assert single or dq_dtype == F32
    out_shape = [jax.ShapeDtypeStruct((n, lq, classes * width), dq_dtype), jax.ShapeDtypeStruct((n, lk, classes * width), F32),
                 jax.ShapeDtypeStruct((n, lk, classes * width), F32)]
    kv_spec = pl.BlockSpec((1, lk, width), lambda n_, r, i: (n_, 0, r))
    out_specs = [pl.BlockSpec((1, tq, width), lambda n_, r, i: (n_, i, r)), kv_spec, kv_spec]
    if has_kb:
        kb_spec = pl.BlockSpec((1,) + kb.shape[1:], lambda n_, r, i: (n_, 0, 0, 0))
        in_specs.append(kb_spec)
        args.append(kb)
        out_shape += [jax.ShapeDtypeStruct(kb.shape, F32), jax.ShapeDtypeStruct((n, lq, LANES), F32)]
        out_specs += [kb_spec, pl.BlockSpec((1, tq, LANES), lambda n_, r, i: (n_, i, 0))]
    return pl.pallas_call(
        body, grid=(n, classes, lq // tq), in_specs=in_specs, out_specs=out_specs, out_shape=out_shape, name=name,
        compiler_params=_params("parallel", "parallel", "arbitrary"),
    )(*args)


def _head_delta(do, out, heads, *, name, lse=None, sink=None):
    width = out.shape[1]
    gather = _head_expand(heads, width).T

    if sink is None:
        return _rowwise(lambda do, o, e: _dot_exact(do * o, e), [do, out], [gather], [(LANES, F32)], name=name)[0]

    def fn(do, o, lse, e, sink):
        delta = _dot_exact(do * o, e)
        return delta, -jnp.sum(jnp.exp(sink - lse) * delta, axis=0, keepdims=True)
    return _rowwise(fn, [do, out, lse], [gather, sink], [(LANES, F32)], [(1, LANES)], name=name)


def _rope_tables():
    half = HEAD_DIM // 2
    inv = ROPE_THETA ** (-jnp.arange(half, dtype=F32) / half)
    ang = jnp.arange(SEQ, dtype=F32)[:, None] * inv[None, :]
    cos = jnp.tile(jnp.cos(ang), (1, 2 * ATT_W // HEAD_DIM))
    sin = jnp.tile(jnp.concatenate([-jnp.sin(ang), jnp.sin(ang)], axis=1), (1, ATT_W // HEAD_DIM))
    return cos, sin


def _rotate(x, cos, sin):
    width = x.shape[1]
    lane = lax.broadcasted_iota(jnp.int32, x.shape, 1)
    first_half = (lane % HEAD_DIM) < (HEAD_DIM // 2)
    swapped = jnp.where(first_half, pltpu.roll(x, width - HEAD_DIM // 2, axis=1), pltpu.roll(x, HEAD_DIM // 2, axis=1))
    return x * cos[:, :width] + swapped * sin[:, :width]


def _gelu(x):
    k = math.sqrt(2.0 / math.pi)
    t = jnp.tanh(k * (x + 0.044715 * x * x * x))
    return 0.5 * x * (1.0 + t), t


def _gelu_grad(x, t):
    k = math.sqrt(2.0 / math.pi)
    return 0.5 * (1.0 + t) + 0.5 * x * (1.0 - t * t) * k * (1.0 + 3.0 * 0.044715 * x * x)


def _shift_down(x, halo, s):
    ext = jnp.concatenate([halo, x], axis=0)
    return pltpu.roll(ext, s, axis=0)[8:, :]


def _shift_up(x, halo, s):
    rows = x.shape[0]
    ext = jnp.concatenate([x, halo], axis=0)
    return pltpu.roll(ext, rows + 8 - s, axis=0)[:rows, :]


def _lru_gates(u, halo, cw, cb, wa, ba, wi, bi, lam):
    taps = [_shift_down(u, halo, CONV_WIDTH - 1 - k) for k in range(CONV_WIDTH - 1)] + [u]
    uc = cb + sum(cw[k:k + 1, :] * taps[k] for k in range(CONV_WIDTH))
    ucb = uc.astype(BF16)
    r = jax.nn.sigmoid(jnp.dot(ucb, wa, preferred_element_type=F32) + ba)
    gi = jax.nn.sigmoid(jnp.dot(ucb, wi, preferred_element_type=F32) + bi)
    sp = jnp.maximum(-lam, 0.0) + jnp.log(1.0 + jnp.exp(-jnp.abs(lam)))
    log_a = -LRU_C * r * sp
    a = jnp.exp(log_a)
    x2 = 2.0 * log_a
    one_minus_a2 = jnp.where(x2 > -0.01, -x2 * (1.0 + x2 * (0.5 + x2 * (1.0 / 6.0 + x2 / 24.0))), 1.0 - jnp.exp(x2))
    mult = jnp.sqrt(one_minus_a2)
    return taps, uc, ucb, r, gi, sp, a, mult


def _lru_specs(nchunk, reverse):
    def chunk(b, c):
        return b * nchunk + ((nchunk - 1 - c) if reverse else c)

    def halo_before(b, c):
        return jnp.maximum(chunk(b, c) * (LRU_CHUNK // 8) - 1, 0)

    return chunk, halo_before


def _lru_fwd(zug, cw, cb, wa, ba, wi, bi, lam, *, name):
    total = zug.shape[0]
    nchunk = SEQ // LRU_CHUNK
    w = LRU_WIDTH
    chunk, halo_before = _lru_specs(nchunk, False)

    def body(u_ref, uh_ref, g_ref, cw_ref, cb_ref, wa_ref, ba_ref, wi_ref, bi_ref, lam_ref, y_ref, h_ref, a_sc, x_sc, carry_sc):
        c = pl.program_id(1)
        u = u_ref[...]
        halo = jnp.where(c > 0, uh_ref[...], 0.0)
        _, uc, _, _, gi, _, a, mult = _lru_gates(u, halo, cw_ref[...], cb_ref[...], wa_ref[...], ba_ref[...],
                                                 wi_ref[...], bi_ref[...], lam_ref[...])
        a_sc[...] = a
        x_sc[...] = mult * (gi * uc)

        @pl.when(c == 0)
        def _():
            carry_sc[...] = jnp.zeros(carry_sc.shape, F32)

        def tile_step(t, h):
            r0 = pl.multiple_of(t * 8, 8)
            at = a_sc[pl.ds(r0, 8), :]
            xt = x_sc[pl.ds(r0, 8), :]
            rows = []
            for j in range(8):
                h = at[j:j + 1, :] * h + xt[j:j + 1, :]
                rows.append(h)
            h_ref[pl.ds(r0, 8), :] = jnp.concatenate(rows, axis=0)
            return h

        h_last = lax.fori_loop(0, LRU_CHUNK // 8, tile_step, carry_sc[0:1, :])
        carry_sc[0:1, :] = h_last
        gel, _ = _gelu(g_ref[...])
        y_ref[...] = (h_ref[...] * gel).astype(BF16)

    small = lambda arr: pl.BlockSpec(arr.shape, lambda b, c: (0, 0))
    return pl.pallas_call(
        body, grid=(total // SEQ, nchunk),
        in_specs=[pl.BlockSpec((LRU_CHUNK, w), lambda b, c: (chunk(b, c), 0)),
                  pl.BlockSpec((8, w), lambda b, c: (halo_before(b, c), 0)),
                  pl.BlockSpec((LRU_CHUNK, w), lambda b, c: (chunk(b, c), 1)),
                  small(cw), small(cb), small(wa), small(ba), small(wi), small(bi), small(lam)],
        out_specs=[pl.BlockSpec((LRU_CHUNK, w), lambda b, c: (chunk(b, c), 0))] * 2,
        out_shape=[jax.ShapeDtypeStruct((total, w), BF16), jax.ShapeDtypeStruct((total, w), F32)],
        scratch_shapes=[pltpu.VMEM((LRU_CHUNK, w), F32), pltpu.VMEM((LRU_CHUNK, w), F32), pltpu.VMEM((8, w), F32)],
        name=name, compiler_params=_params("arbitrary", "arbitrary"),
    )(zug, zug, zug, cw, cb, wa, ba, wi, bi, lam)


def _lru_bwd(zug, hl, dy, cw, cb, wa, ba, wi, bi, lam, *, name):
    total = zug.shape[0]
    nchunk = SEQ // LRU_CHUNK
    w = LRU_WIDTH
    chunk, halo_before = _lru_specs(nchunk, True)
    tn = (((0,), (0,)), ((), ()))
    nt = (((1,), (1,)), ((), ()))

    def body(u_ref, uh_ref, g_ref, hl_ref, hh_ref, dy_ref, cw_ref, cb_ref, wa_ref, ba_ref, wi_ref, bi_ref, lam_ref,
             dz_ref, dcw_ref, dcb_ref, dwa_ref, dba_ref, dwi_ref, dbi_ref, dlam_ref,
             a_sc, d_sc, g_sc, gcarry_sc, acarry_sc, duc_sc):
        b, c = pl.program_id(0), pl.program_id(1)
        first_chunk = c == nchunk - 1

        @pl.when((b == 0) & (c == 0))
        def _():
            for ref in (dcw_ref, dcb_ref, dwa_ref, dba_ref, dwi_ref, dbi_ref, dlam_ref):
                ref[...] = jnp.zeros(ref.shape, F32)

        @pl.when(c == 0)
        def _():
            gcarry_sc[...] = jnp.zeros(gcarry_sc.shape, F32)
            acarry_sc[...] = jnp.zeros(acarry_sc.shape, F32)
            duc_sc[...] = jnp.zeros(duc_sc.shape, F32)

        u = u_ref[...]
        halo = jnp.where(first_chunk, 0.0, uh_ref[...])
        cw, wa, wi, lam = cw_ref[...], wa_ref[...], wi_ref[...], lam_ref[...]
        taps, uc, ucb, r, gi, sp, a, mult = _lru_gates(u, halo, cw, cb_ref[...], wa, ba_ref[...], wi, bi_ref[...], lam)
        gate = g_ref[...]
        gel, th = _gelu(gate)
        dy = dy_ref[...]
        hl = hl_ref[...]
        a_sc[...] = a
        d_sc[...] = dy * gel
        dgate = dy * hl * _gelu_grad(gate, th)

        def tile_step(t, carry):
            g_next, a_next = carry
            r0 = pl.multiple_of((LRU_CHUNK // 8 - 1 - t) * 8, 8)
            dt = d_sc[pl.ds(r0, 8), :]
            at = a_sc[pl.ds(r0, 8), :]
            rows = [None] * 8
            for j in reversed(range(8)):
                g_next = dt[j:j + 1, :] + a_next * g_next
                a_next = at[j:j + 1, :]
                rows[j] = g_next
            g_sc[pl.ds(r0, 8), :] = jnp.concatenate(rows, axis=0)
            return g_next, a_next

        g_last, a_last = lax.fori_loop(0, LRU_CHUNK // 8, tile_step, (gcarry_sc[0:1, :], acarry_sc[0:1, :]))
        gcarry_sc[0:1, :] = g_last
        acarry_sc[0:1, :] = a_last

        gs = g_sc[...]
        hl_halo = jnp.where(first_chunk, 0.0, hh_ref[...])
        da = gs * _shift_down(hl, hl_halo, 1)
        dmult = gs * gi * uc
        dgi = gs * mult * uc
        duc = gs * mult * gi
        dlog_a = da * a - dmult * a * a / mult
        dr = dlog_a * (-LRU_C * sp)
        dsp = jnp.sum(dlog_a * (-LRU_C * r), axis=0, keepdims=True)
        dlam_ref[...] += -dsp * jax.nn.sigmoid(-lam)
        dpa = dr * r * (1.0 - r)
        dpi = dgi * gi * (1.0 - gi)
        dpab, dpib = dpa.astype(BF16), dpi.astype(BF16)
        dba_ref[...] += jnp.sum(dpa, axis=0, keepdims=True)
        dbi_ref[...] += jnp.sum(dpi, axis=0, keepdims=True)
        dwa_ref[...] += lax.dot_general(ucb, dpab, tn, preferred_element_type=F32)
        dwi_ref[...] += lax.dot_general(ucb, dpib, tn, preferred_element_type=F32)
        duc = duc + lax.dot_general(dpab, wa, nt, preferred_element_type=F32)
        duc = duc + lax.dot_general(dpib, wi, nt, preferred_element_type=F32)
        dcb_ref[...] += jnp.sum(duc, axis=0, keepdims=True)
        dcw_ref[...] += jnp.concatenate([jnp.sum(duc * taps[k], axis=0, keepdims=True) for k in range(CONV_WIDTH)], axis=0)
        after = duc_sc[...]
        du = cw[CONV_WIDTH - 1:CONV_WIDTH, :] * duc
        for k in range(CONV_WIDTH - 1):
            du = du + cw[k:k + 1, :] * _shift_up(duc, after, CONV_WIDTH - 1 - k)
        duc_sc[...] = duc[0:8, :]
        dz_ref[:, 0:w] = du.astype(BF16)
        dz_ref[:, w:2 * w] = dgate.astype(BF16)

    small = lambda arr: pl.BlockSpec(arr.shape, lambda b, c: (0, 0))
    acc = lambda shape: pl.BlockSpec(shape, lambda b, c: (0, 0))
    chunk_spec = lambda cb_: pl.BlockSpec((LRU_CHUNK, w), lambda b, c: (chunk(b, c), cb_))
    halo_spec = pl.BlockSpec((8, w), lambda b, c: (halo_before(b, c), 0))
    acc_shapes = [(CONV_WIDTH, w), (1, w), (w, w), (1, w), (w, w), (1, w), (1, w)]
    return pl.pallas_call(
        body, grid=(total // SEQ, nchunk),
        in_specs=[chunk_spec(0), halo_spec, chunk_spec(1), chunk_spec(0), halo_spec, chunk_spec(0),
                  small(cw), small(cb), small(wa), small(ba), small(wi), small(bi), small(lam)],
        out_specs=[pl.BlockSpec((LRU_CHUNK, 2 * w), lambda b, c: (chunk(b, c), 0))] + [acc(s) for s in acc_shapes],
        out_shape=[jax.ShapeDtypeStruct((total, 2 * w), BF16)] + [jax.ShapeDtypeStruct(s, F32) for s in acc_shapes],
        scratch_shapes=[pltpu.VMEM((LRU_CHUNK, w), F32)] * 3 + [pltpu.VMEM((8, w), F32)] * 3,
        name=name, compiler_params=_params("arbitrary", "arbitrary"),
    )(zug, zug, zug, hl, hl, dy, cw, cb, wa, ba, wi, bi, lam)


def _block_diag(wb):
    eye = jnp.eye(LRU_BLOCKS, dtype=wb.dtype)
    return jnp.einsum("gcd,gh->gchd", wb, eye).reshape(LRU_WIDTH, LRU_WIDTH).astype(BF16)


def _diag_blocks(wd):
    blk = LRU_WIDTH // LRU_BLOCKS
    return jnp.stack([wd[g * blk:(g + 1) * blk, g * blk:(g + 1) * blk] for g in range(LRU_BLOCKS)])


FOX_SCAN = 256


def _fox_bias(fl, bf, *, name):
    nb = fl.shape[0]

    def body(fl_ref, bf_ref, c_ref):
        x = fl_ref[0] + bf_ref[...]
        logf = jnp.minimum(x, 0.0) - jnp.log(1.0 + jnp.exp(-jnp.abs(x)))
        upper = (lax.broadcasted_iota(jnp.int32, (FOX_SCAN, FOX_SCAN), 0)
                 <= lax.broadcasted_iota(jnp.int32, (FOX_SCAN, FOX_SCAN), 1)).astype(BF16)
        carry = jnp.zeros((LRU_BLOCKS, 1), F32)
        for j in range(SEQ // FOX_SCAN):
            blk = logf[:, j * FOX_SCAN:(j + 1) * FOX_SCAN]
            c_ref[0, :, j * FOX_SCAN:(j + 1) * FOX_SCAN] = _dot_exact(blk, upper) + carry
            carry = carry + jnp.sum(blk, axis=1, keepdims=True)

    return pl.pallas_call(
        body, grid=(nb,),
        in_specs=[pl.BlockSpec((1, 8, SEQ), lambda b: (b, 0, 0)), pl.BlockSpec((8, 1), lambda b: (0, 0))],
        out_specs=pl.BlockSpec((1, 8, SEQ), lambda b: (b, 0, 0)),
        out_shape=jax.ShapeDtypeStruct((nb, 8, SEQ), F32), name=name, compiler_params=_params("arbitrary"),
    )(fl, bf)


def _fox_bias_bwd(fl, bf, dc, *, name):
    nb = fl.shape[0]

    def body(fl_ref, bf_ref, dc_ref, dfl_ref, dbf_ref):
        @pl.when(pl.program_id(0) == 0)
        def _():
            dbf_ref[...] = jnp.zeros(dbf_ref.shape, F32)

        x = fl_ref[0] + bf_ref[...]
        dc_all = dc_ref[0]
        lower = (lax.broadcasted_iota(jnp.int32, (FOX_SCAN, FOX_SCAN), 0)
                 >= lax.broadcasted_iota(jnp.int32, (FOX_SCAN, FOX_SCAN), 1)).astype(BF16)
        carry = jnp.zeros((LRU_BLOCKS, 1), F32)
        total = jnp.zeros((LRU_BLOCKS, 1), F32)
        for j in reversed(range(SEQ // FOX_SCAN)):
            cols = slice(j * FOX_SCAN, (j + 1) * FOX_SCAN)
            blk = dc_all[:, cols]
            dlogf = _dot_exact(blk, lower) + carry
            carry = carry + jnp.sum(blk, axis=1, keepdims=True)
            dx = dlogf * jax.nn.sigmoid(-x[:, cols])
            dfl_ref[0, :, cols] = dx
            total = total + jnp.sum(dx, axis=1, keepdims=True)
        dbf_ref[...] += total

    return pl.pallas_call(
        body, grid=(nb,),
        in_specs=[pl.BlockSpec((1, 8, SEQ), lambda b: (b, 0, 0)), pl.BlockSpec((8, 1), lambda b: (0, 0)),
                  pl.BlockSpec((1, 8, SEQ), lambda b: (b, 0, 0))],
        out_specs=[pl.BlockSpec((1, 8, SEQ), lambda b: (b, 0, 0)), pl.BlockSpec((8, 1), lambda b: (0, 0))],
        out_shape=[jax.ShapeDtypeStruct((nb, 8, SEQ), F32), jax.ShapeDtypeStruct((8, 1), F32)],
        name=name, compiler_params=_params("arbitrary"),
    )(fl, bf, dc)


def _seq3(a, nb):
    return a.reshape(nb, a.shape[0] // nb, a.shape[1])


def _flat2(a):
    return a.reshape(a.shape[0] * a.shape[1], a.shape[2])


def _residual_out(y, w, h, next_gain, *, name):
    if next_gain is None:
        out, = _matmul(y, w, extras=(h,), epilogue=lambda acc, res: (acc + res,), name=name)
        return out, None

    def epilogue(acc, res, g):
        out = acc + res
        return out, out * _rstd(out) * g
    return _matmul(y, w, outs=(F32, BF16), extras=(h,), consts=(next_gain,), epilogue=epilogue, whole_rows=True, name=name)


def _mlp_fwd(h, hn, p, tag, next_gain):
    act, = _matmul(hn, p["w_up_t"], tb=True, outs=(BF16,), epilogue=lambda acc: (jnp.square(jnp.maximum(acc, 0.0)),),
                   name=f"{tag}_up")
    out, hn_next = _residual_out(act, p["w_down"], h, next_gain, name=f"{tag}_down")
    return out, hn_next, (h, hn, act)


def _mlp_bwd(dh, dhb, p, saved, tag):
    h, hn, act = saved
    dup, = _matmul(dhb, p["w_down"], tb=True, outs=(BF16,), extras=(act,),
                   epilogue=lambda acc, act: (acc * (2.0 * jnp.sqrt(act.astype(F32))),), name=f"{tag}_bwd_dup")
    dw_down, = _matmul(act, dhb, ta=True, outs=(BF16,),name=f"{tag}_bwd_dwdown")
    dw_up_t, = _matmul(dup, hn, ta=True, outs=(BF16,),name=f"{tag}_bwd_dwup")
    dh2, dh2b, dg = _norm_input_grad(dup, p["w_up_t"], h, dh, p["norm"], name=f"{tag}_bwd_dh")
    return dh2, dh2b, {"norm": dg, "w_up_t": dw_up_t, "w_down": dw_down}


def _xa_fwd(h, hn, mem, p, tag, nb, next_gain):
    mem_n = _rms_fwd(mem, p["mem_norm"], name=f"{tag}_memnorm")
    q, = _matmul(hn, p["w_q"], outs=(BF16,), name=f"{tag}_q")
    kv, = _matmul(mem_n, p["w_kv_t"], tb=True, outs=(BF16,), name=f"{tag}_kv")
    q3, kv3 = _seq3(q, nb), _seq3(kv, nb)
    o, lse, ob = _attn_fwd((q3, 0), (kv3, 0), (kv3, 1), heads=XA_HEADS, dh=XA_HEAD_DIM, mode="full", bf16_copy=True,
                           name=f"{tag}_attn")
    o, ob = _flat2(o), _flat2(ob)
    out, hn_next = _residual_out(ob, p["w_o"], h, next_gain, name=f"{tag}_o")
    return out, hn_next, (h, hn, mem_n, q3, kv3, o, ob, lse)


def _xa_bwd(dh, dhb, mem, p, saved, tag, nb):
    h, hn, mem_n, q3, kv3, o, ob, lse = saved
    dob, delta = _matmul(dhb, p["w_o"], tb=True, outs=(BF16, (F32, LANES)), extras=(o,), consts=(_head_expand(XA_HEADS, D_MODEL).T,),
                         epilogue=lambda acc, o, e: (acc, _dot_exact(acc * o, e)), name=f"{tag}_bwd_do")
    dw_o, = _matmul(ob, dhb, ta=True, outs=(BF16,),name=f"{tag}_bwd_dwo")
    dq, dk, dv = _attn_bwd((q3, 0), (kv3, 0), (kv3, 1), (_seq3(dob, nb), 0), lse, _seq3(delta, nb), heads=XA_HEADS,
                           dh=XA_HEAD_DIM, mode="full", dq_dtype=BF16, name=f"{tag}_bwd_attn")
    dqb = _flat2(dq)
    dkvb, = _rowwise(lambda a, b: jnp.concatenate([a, b], axis=1), [_flat2(dk), _flat2(dv)], [], [(2 * D_MODEL, BF16)],
                     name=f"{tag}_bwd_dkvcast")
    dw_q, = _matmul(hn, dqb, ta=True, outs=(BF16,),name=f"{tag}_bwd_dwq")
    dw_kv_t, = _matmul(dkvb, mem_n, ta=True, outs=(BF16,),name=f"{tag}_bwd_dwkv")
    dmem_n, = _matmul(dkvb, p["w_kv_t"], name=f"{tag}_bwd_dmemn")
    dg_mem, = _rowwise(lambda x, d, g: _rms_bwd_vals(x, d, g)[1], [mem, dmem_n], [p["mem_norm"]], [], [(1, D_MODEL)],
                       name=f"{tag}_bwd_memnorm")
    dh2, dh2b, dg = _norm_input_grad(dqb, p["w_q"], h, dh, p["norm"], tb=True, name=f"{tag}_bwd_dh")
    return dh2, dh2b, {"norm": dg, "mem_norm": dg_mem, "w_q": dw_q, "w_kv_t": dw_kv_t, "w_o": dw_o}


AB_MAIN = 2 * LRU_WIDTH + 3 * ATT_W


def _ab_fwd(h, hn, p, nb, next_gain):
    w_in_t = p["w_in_t"]
    zug, = _matmul(hn, w_in_t[:2 * LRU_WIDTH], tb=True, name="ab_in_ug")
    qkv, = _matmul(hn, w_in_t[2 * LRU_WIDTH:AB_MAIN], tb=True, outs=(BF16,), tn=768, name="ab_in_qkv")
    zf, = _matmul(hn, w_in_t[AB_MAIN:], tb=True, name="ab_in_f")
    fl = zf[:, :8].reshape(nb, SEQ, 8).transpose(0, 2, 1)
    cbias = _fox_bias(fl, p["b_f"], name="ab_fox_bias")
    kb = cbias.reshape(nb, 8, SEQ // ATT_CHUNK, ATT_CHUNK)
    y_a, hl = _lru_fwd(zug, p["conv_w"], p["conv_b"], p["w_a"], p["b_a"], p["w_i"], p["b_i"], p["lam"], name="ab_lru")
    qkv3 = _seq3(qkv, nb)
    o, lse = _attn_fwd((qkv3, 0), (qkv3, 1), (qkv3, 2), kb, heads=8, dh=HEAD_DIM, mode="causal", name="ab_fox")
    o = _flat2(o)
    y, = _rowwise(lambda a, b: jnp.concatenate([a, b.astype(BF16)], axis=1), [y_a, o], [], [(D_MODEL, BF16)], name="ab_ycat")
    out, hn_next = _residual_out(y, p["w_out"], h, next_gain, name="ab_out")
    return out, hn_next, (h, hn, zug, qkv3, fl, kb, hl, o, lse, y)


def _ab_bwd(dh, dhb, p, saved, nb):
    h, hn, zug, qkv3, fl, kb, hl, o, lse, y = saved
    dy, dyb, delta = _matmul(dhb, p["w_out"], tb=True, outs=(F32, BF16, (F32, LANES)), extras=(y,), consts=(_head_expand(8, ATT_W).T,),
                             epilogue=lambda acc, y, e: (acc, acc, _dot_exact(acc[:, ATT_W:] * y[:, ATT_W:].astype(F32), e)),
                             name="ab_bwd_dy")
    dw_out, = _matmul(y, dhb, ta=True, outs=(BF16,),name="ab_bwd_dwout")
    dzug, dcw, dcb, dwa, dba, dwi, dbi, dlam = _lru_bwd(zug, hl, dy, p["conv_w"], p["conv_b"], p["w_a"], p["b_a"],
                                                        p["w_i"], p["b_i"], p["lam"], name="ab_bwd_lru")
    dq, dk, dv, dkb, drow = _attn_bwd((qkv3, 0), (qkv3, 1), (qkv3, 2), (_seq3(dyb, nb), 1), lse, _seq3(delta, nb), kb,
                                      heads=8, dh=HEAD_DIM, mode="causal", name="ab_bwd_fox")
    dcum = dkb.reshape(nb, 8, SEQ) + drow[:, :, :8].transpose(0, 2, 1)
    dfl, dbf = _fox_bias_bwd(fl, p["b_f"], dcum, name="ab_bwd_fox_bias")
    dzf = jnp.pad(dfl.transpose(0, 2, 1).reshape(nb * SEQ, 8), ((0, 0), (0, LANES - 8)))
    dz, = _rowwise(lambda a, q, k, v, f: jnp.concatenate([a, q.astype(BF16), k.astype(BF16), v.astype(BF16), f.astype(BF16)], axis=1),
                   [dzug, _flat2(dq), _flat2(dk), _flat2(dv), dzf], [], [(AB_MAIN + LANES, BF16)], name="ab_bwd_dzcat")
    dw_in_t, = _matmul(dz, hn, ta=True, outs=(BF16,),tm=384, name="ab_bwd_dwin")
    dh2, dh2b, dg = _norm_input_grad(dz, p["w_in_t"], h, dh, p["norm"], name="ab_bwd_dh")
    grads = {"norm": dg, "w_in_t": dw_in_t[:AB_MAIN + 8], "conv_w": dcw, "conv_b": dcb, "w_a": _diag_blocks(dwa), "b_a": dba,
             "w_i": _diag_blocks(dwi), "b_i": dbi, "lam": dlam, "b_f": dbf.reshape(1, 8), "w_out": dw_out}
    return dh2, dh2b, grads


CD_IN = 3 * ATT_W + ATT_W + 2 * SWA_KW


def _gqa_share():
    src = jnp.arange(SWA_KW)[:, None]
    dst = jnp.arange(ATT_W)[None, :]
    per_kv = ATT_W // (SWA_KW // HEAD_DIM)
    return ((dst // per_kv == src // HEAD_DIM) & (dst % HEAD_DIM == src % HEAD_DIM)).astype(BF16)


def _cd_fwd(h, hn, p, nb, next_gain):
    z, = _matmul(hn, p["w_in_t"], tb=True, tn=384, name="cd_in")
    cos, sin = _rope_tables()
    per_seq = SEQ // ROW_TILE
    table_map = lambda i: (i % per_seq, 0)

    def rope_fn(z, cos, sin, share):
        qc, kc, vc = z[:, 0:ATT_W], z[:, ATT_W:2 * ATT_W], z[:, 2 * ATT_W:3 * ATT_W]
        qd, kd, vd = z[:, 3 * ATT_W:4 * ATT_W], z[:, 4 * ATT_W:4 * ATT_W + SWA_KW], z[:, 4 * ATT_W + SWA_KW:]
        kd8 = jnp.dot(_rotate(kd, cos, sin).astype(BF16), share, preferred_element_type=F32)
        vd8 = jnp.dot(vd.astype(BF16), share, preferred_element_type=F32)
        qk = jnp.concatenate([_rotate(qc, cos, sin), _rotate(kc, cos, sin)], axis=1)
        return qk, vc, _rotate(qd, cos, sin), kd8, vd8, qk, qk, vc, vc
    dils = [dil for _, dil in DIL_PATTERN if dil > 1]
    outs = _rowwise(rope_fn, [z, cos, sin], [_gqa_share()],
                    [(2 * ATT_W, BF16)] + [(ATT_W, BF16)] * 4 + [(2 * ATT_W, BF16, d) for d in dils] + [(ATT_W, BF16, d) for d in dils],
                    name="cd_rope", row_maps=[None, table_map, table_map])
    qk, vc, qd, kd, vd = outs[:5]
    views = {1: (_seq3(qk, nb), _seq3(vc, nb))}
    for d, qk_d, vc_d in zip(dils, outs[5:5 + len(dils)], outs[5 + len(dils):]):
        views[d] = (_seq3(qk_d, nb), _seq3(vc_d, nb))
    in_classes = lambda a, width, d: _flat2(a) if d == 1 else ("classes", _flat2(a), width, d)
    branch = []
    for window, dil in DIL_PATTERN:
        qk_v, vc_v = views[dil]
        o_i, lse_i = _attn_fwd((qk_v, 0), (qk_v, 1), (vc_v, 0), classes=dil, heads=8, dh=HEAD_DIM, mode="band",
                               max_dist=window // dil, name=f"cd_dil{dil}")
        branch.append((in_classes(o_i, ATT_W, dil), in_classes(lse_i, LANES, dil)))
    expand = _head_expand(8, ATT_W)

    def combine(o1, o2, o3, l1, l2, l3, e):
        m = jnp.maximum(jnp.maximum(l1, l2), l3)
        lt = m + jnp.log(jnp.exp(l1 - m) + jnp.exp(l2 - m) + jnp.exp(l3 - m))
        o = sum(_dot_exact(jnp.exp(l - lt), e) * o_ for o_, l in ((o1, l1), (o2, l2), (o3, l3)))
        return o, lt
    y_c, lse_c = _rowwise(combine, [b[0] for b in branch] + [b[1] for b in branch], [expand], [(ATT_W, F32), (LANES, F32)],
                          name="cd_dil_combine")
    qd3, kd3, vd3 = _seq3(qd, nb), _seq3(kd, nb), _seq3(vd, nb)
    o_d, lse_band = _attn_fwd((qd3, 0), (kd3, 0), (vd3, 0), heads=8, dh=HEAD_DIM, mode="band", max_dist=SWA_WINDOW - 1,
                              name="cd_swa")

    def sink_combine(o, l, e, sink):
        lt = jnp.maximum(l, sink) + jnp.log(1.0 + jnp.exp(-jnp.abs(l - sink)))
        return o * _dot_exact(jnp.exp(l - lt), e), lt
    y_d, lse_d = _rowwise(sink_combine, [_flat2(o_d), _flat2(lse_band)], [expand, p["sink"]], [(ATT_W, F32), (LANES, F32)],
                          name="cd_swa_sink")
    y, = _rowwise(lambda a, b: jnp.concatenate([a, b], axis=1), [y_c, y_d], [], [(D_MODEL, BF16)], name="cd_ycat")
    out, hn_next = _residual_out(y, p["w_out"], h, next_gain, name="cd_out")
    return out, hn_next, (h, hn, views, qd3, kd3, vd3, y_c, lse_c, y_d, lse_d, y)


def _cd_bwd(dh, dhb, p, saved, nb):
    h, hn, views, qd3, kd3, vd3, y_c, lse_c, y_d, lse_d, y = saved
    dy, dyb = _matmul(dhb, p["w_out"], tb=True, outs=(F32, BF16), epilogue=lambda acc: (acc, acc), name="cd_bwd_dy")
    dw_out, = _matmul(y, dhb, ta=True, outs=(BF16,),name="cd_bwd_dwout")
    dyb3 = _seq3(dyb, nb)
    dils = [dil for _, dil in DIL_PATTERN if dil > 1]
    gather = _head_expand(8, ATT_W).T

    def prepare(do, o, lse, e):
        delta = _dot_exact(do * o, e)
        return (delta,) + (do,) * len(dils) + (lse,) * len(dils) + (delta,) * len(dils)
    outs = _rowwise(prepare, [(dy, ATT_W, 0), y_c, lse_c], [gather],
                    [(LANES, F32)] + [(ATT_W, BF16, d) for d in dils] + [(LANES, F32, d) for d in dils] * 2, name="cd_bwd_delta_dil")
    seen = {1: ((dyb3, 0), _seq3(lse_c, nb), _seq3(outs[0], nb))}
    for j, d in enumerate(dils):
        seen[d] = ((_seq3(outs[1 + j], nb), 0), _seq3(outs[1 + len(dils) + j], nb), _seq3(outs[1 + 2 * len(dils) + j], nb))
    in_classes = lambda a, d: _flat2(a) if d == 1 else ("classes", _flat2(a), ATT_W, d)
    parts = []
    for window, dil in DIL_PATTERN:
        (qk_v, vc_v), (do_v, lse_v, delta_v) = views[dil], seen[dil]
        grads = _attn_bwd((qk_v, 0), (qk_v, 1), (vc_v, 0), do_v, lse_v, delta_v, classes=dil, heads=8, dh=HEAD_DIM, mode="band",
                          max_dist=window // dil, dq_dtype=BF16, name=f"cd_bwd_dil{dil}")
        parts.append([in_classes(a, dil) for a in grads])
    delta_d, dsink = _head_delta((dy, ATT_W, 1), y_d, 8, lse=lse_d, sink=p["sink"], name="cd_bwd_delta_swa")
    dqd, dkd, dvd = _attn_bwd((qd3, 0), (kd3, 0), (vd3, 0), (dyb3, 1), _seq3(lse_d, nb), _seq3(delta_d, nb), heads=8,
                              dh=HEAD_DIM, mode="band", max_dist=SWA_WINDOW - 1, dq_dtype=BF16, name="cd_bwd_swa")
    cos, sin = _rope_tables()
    per_seq = SEQ // ROW_TILE
    table_map = lambda i: (i % per_seq, 0)

    def unrope(q1, q2, q3, k1, k2, k3, v1, v2, v3, qd, kd8, vd8, cos, sin, gather):
        back = lambda x: _rotate(x.astype(F32), cos, -sin)
        kd, vd = _dot_exact(kd8, gather), _dot_exact(vd8, gather)
        return jnp.concatenate([back(q1 + q2 + q3), back(k1 + k2 + k3), v1 + v2 + v3, back(qd), back(kd), vd], axis=1)
    ins = [parts[b][t] for t in range(3) for b in range(3)] + [_flat2(dqd), _flat2(dkd), _flat2(dvd), cos, sin]
    dz, = _rowwise(unrope, ins, [_gqa_share().T], [(CD_IN, BF16)], name="cd_bwd_unrope",
                   row_maps=[None] * 12 + [table_map, table_map])
    dw_in_t, = _matmul(dz, hn, ta=True, outs=(BF16,),tm=384, name="cd_bwd_dwin")
    dh2, dh2b, dg = _norm_input_grad(dz, p["w_in_t"], h, dh, p["norm"], name="cd_bwd_dh")
    return dh2, dh2b, {"norm": dg, "w_in_t": dw_in_t, "sink": dsink[:, :8], "w_out": dw_out}


def _local_step(x, mem, target, w, complete=None, grads_ready=None):
    nb = x.shape[0]
    h = x.reshape(nb * SEQ, D_MODEL)
    mem2 = mem.reshape(nb * MEM_LEN, D_MODEL)
    tgt = target.reshape(nb * SEQ, D_MODEL)

    hn = _rms_fwd(h, w["ab"]["norm"], name="ab_norm")
    h, hn, s_ab = _ab_fwd(h, hn, w["ab"], nb, w["xa0"]["norm"])
    if complete is not None:
        complete(h)
    h, hn, s_xa0 = _xa_fwd(h, hn, mem2, w["xa0"], "xa0", nb, w["mlp0"]["norm"])
    h, hn, s_mlp0 = _mlp_fwd(h, hn, w["mlp0"], "mlp0", w["cd"]["norm"])
    h, hn, s_cd = _cd_fwd(h, hn, w["cd"], nb, w["xa1"]["norm"])
    h, hn, s_xa1 = _xa_fwd(h, hn, mem2, w["xa1"], "xa1", nb, w["mlp1"]["norm"])
    h, _, s_mlp1 = _mlp_fwd(h, hn, w["mlp1"], "mlp1", None)

    dh, dhb, loss, dg_final = _loss_and_grad(h, tgt, w["final_norm"], name="loss")
    grads = {"final_norm": dg_final}
    dh, dhb, grads["mlp1"] = _mlp_bwd(dh, dhb, w["mlp1"], s_mlp1, "mlp1")
    dh, dhb, grads["xa1"] = _xa_bwd(dh, dhb, mem2, w["xa1"], s_xa1, "xa1", nb)
    dh, dhb, grads["cd"] = _cd_bwd(dh, dhb, w["cd"], s_cd, nb)
    if grads_ready is not None:
        grads_ready(0, grads)
    dh, dhb, grads["mlp0"] = _mlp_bwd(dh, dhb, w["mlp0"], s_mlp0, "mlp0")
    dh, dhb, grads["xa0"] = _xa_bwd(dh, dhb, mem2, w["xa0"], s_xa0, "xa0", nb)
    if grads_ready is not None:
        grads_ready(1, grads)
    dh, dhb, grads["ab"] = _ab_bwd(dh, dhb, w["ab"], s_ab, nb)
    return loss, dh.reshape(nb, SEQ, D_MODEL), grads


ANY = pl.BlockSpec(memory_space=pl.ANY)


def _place():
    x, y, c = lax.axis_index("x"), lax.axis_index("y"), lax.axis_index("c")
    return x, y, c, [(1 - x, y), (x, 1 - y), (1 - x, 1 - y)]


def _gather_chips(shard):
    half = shard.shape[0] // 2

    def body(src, out, send_sems, recv_sems):
        x, y, c, chips = _place()
        sibling = (x, y, 1 - c)

        def rows(slot, core):
            return out.at[slot, pl.ds(core * half, half)]

        def copy(k, src_ref, dst_ref, to):
            return pltpu.make_async_remote_copy(src_ref=src_ref, dst_ref=dst_ref, send_sem=send_sems.at[k],
                                                recv_sem=recv_sems.at[k], device_id=to, device_id_type=MESH)

        first = [copy(k, src.at[pl.ds(c * half, half)], rows(2 * x + y, c), (px, py, c)) for k, (px, py) in enumerate(chips)]
        for cp in first:
            cp.start()
        passed = [copy(3 + k, rows(2 * px + py, c), rows(2 * px + py, c), sibling) for k, (px, py) in enumerate(chips)]
        for k, (px, py) in enumerate(chips):
            copy(k, src.at[pl.ds(c * half, half)], rows(2 * px + py, c), (px, py, c)).wait_recv()
            passed[k].start()
        for k, (px, py) in enumerate(chips):
            copy(3 + k, rows(2 * px + py, 1 - c), rows(2 * px + py, 1 - c), sibling).wait_recv()
        for cp in first + passed:
            cp.wait_send()

    return pl.pallas_call(
        body, out_shape=jax.ShapeDtypeStruct((N_CHIPS,) + shard.shape, shard.dtype), in_specs=[ANY], out_specs=ANY,
        scratch_shapes=[pltpu.SemaphoreType.DMA((6,)), pltpu.SemaphoreType.DMA((6,))], name="gather_chips",
    )(shard)


HBM = pl.BlockSpec(memory_space=pltpu.HBM)
SEM = pl.BlockSpec(memory_space=pltpu.SEMAPHORE)
SIDE_EFFECT = pltpu.SideEffectType.DATAFLOW_SIDE_EFFECTING


def _chip_copies(src, land, send_sems, recv_sems):
    half = src.shape[0] // 2
    x, y, c, chips = _place()

    def copy(k, slot, to):
        return pltpu.make_async_remote_copy(src_ref=src.at[pl.ds(c * half, half)], dst_ref=land.at[slot, pl.ds(c * half, half)],
                                            send_sem=send_sems[k], recv_sem=recv_sems[k], device_id=to, device_id_type=MESH)
    out = [copy(k, 2 * x + y, (px, py, c)) for k, (px, py) in enumerate(chips)]
    back = [copy(k, 2 * px + py, (px, py, c)) for k, (px, py) in enumerate(chips)]
    return out, back


def _gather_start(shard):
    def body(src, land, *rest):
        sems, token = rest[:6], rest[8]
        out, _ = _chip_copies(src, land, sems[:3], sems[3:])
        for cp in out:
            cp.start()
        token[...] = jnp.zeros(token.shape, F32)

    sem = pltpu.SemaphoreType.DMA(())
    land_shape = (N_CHIPS,) + shard.shape
    return pl.pallas_call(
        body, name="gather_start",
        out_shape=(sem,) * 6 + (pltpu.HBM(shard.shape, shard.dtype), pltpu.HBM(land_shape, shard.dtype),
                                jax.ShapeDtypeStruct((8, LANES), F32)),
        in_specs=(HBM, HBM), out_specs=(SEM,) * 6 + (HBM, HBM, pl.BlockSpec(memory_space=pltpu.VMEM)),
        input_output_aliases={0: 6, 1: 7}, compiler_params=pltpu.CompilerParams(has_side_effects=SIDE_EFFECT),
    )(pltpu.with_memory_space_constraint(shard, pltpu.HBM),
      pltpu.with_memory_space_constraint(lax.empty(land_shape, shard.dtype), pltpu.HBM))


def _gather_wait(started, after):
    sems, src_thru, land_thru = started[:6], started[6], started[7]

    def body(src, land, *rest):
        out, back = _chip_copies(src, land, rest[:3], rest[3:6])
        for cp in out:
            cp.wait_send()
        for cp in back:
            cp.wait_recv()

    return pl.pallas_call(
        body, name="gather_wait",
        out_shape=(pltpu.HBM(src_thru.shape, src_thru.dtype), pltpu.HBM(land_thru.shape, land_thru.dtype)),
        in_specs=(HBM, HBM) + (SEM,) * 6 + (pl.BlockSpec(memory_space=pl.ANY),), out_specs=(HBM, HBM),
        input_output_aliases={0: 0, 1: 1}, compiler_params=pltpu.CompilerParams(has_side_effects=SIDE_EFFECT),
    )(src_thru, land_thru, *sems, after)[1]


def _gather_pass(land):
    half = land.shape[1] // 2

    def body(land_in, land_out, send_sems, recv_sems):
        x, y, c, chips = _place()

        def copy(k, slot, core):
            rows = land_out.at[slot, pl.ds(core * half, half)]
            return pltpu.make_async_remote_copy(src_ref=rows, dst_ref=rows, send_sem=send_sems.at[k], recv_sem=recv_sems.at[k],
                                                device_id=(x, y, 1 - c), device_id_type=MESH)
        sends = [copy(k, 2 * px + py, c) for k, (px, py) in enumerate(chips)]
        for cp in sends:
            cp.start()
        for k, (px, py) in enumerate(chips):
            copy(k, 2 * px + py, 1 - c).wait_recv()
        for cp in sends:
            cp.wait_send()

    return pl.pallas_call(
        body, out_shape=jax.ShapeDtypeStruct(land.shape, land.dtype), in_specs=[ANY], out_specs=ANY,
        scratch_shapes=[pltpu.SemaphoreType.DMA((3,)), pltpu.SemaphoreType.DMA((3,))], input_output_aliases={0: 0},
        name="gather_pass",
    )(land)


def _swap_cores(block, *, name, half_rows=None):
    shape = block.shape if half_rows is None else (block.shape[0], half_rows, block.shape[2])

    def body(src, out, send_sem, recv_sem):
        x, y, c, _ = _place()
        part = src if half_rows is None else src.at[:, pl.ds((1 - c) * half_rows, half_rows)]
        cp = pltpu.make_async_remote_copy(src_ref=part, dst_ref=out, send_sem=send_sem, recv_sem=recv_sem,
                                          device_id=(x, y, 1 - c), device_id_type=MESH)
        cp.start()
        cp.wait()

    return pl.pallas_call(
        body, out_shape=jax.ShapeDtypeStruct(shape, block.dtype), in_specs=[ANY], out_specs=ANY,
        scratch_shapes=[pltpu.SemaphoreType.DMA(()), pltpu.SemaphoreType.DMA(())], name=name,
    )(block)


def _scatter_copies(parts, land, send_sems, recv_sems):
    x, y, c, chips = _place()
    return [pltpu.make_async_remote_copy(src_ref=parts.at[2 * px + py], dst_ref=land.at[k], send_sem=send_sems[k],
                                         recv_sem=recv_sems[k], device_id=(px, py, c), device_id_type=MESH)
            for k, (px, py) in enumerate(chips)]


def _scatter_start(parts, tag):
    def body(src, land, *rest):
        for cp in _scatter_copies(src, land, rest[:3], rest[3:6]):
            cp.start()
        rest[8][...] = jnp.zeros(rest[8].shape, F32)

    sem = pltpu.SemaphoreType.DMA(())
    land_shape = (3,) + parts.shape[1:]
    return pl.pallas_call(
        body, name=f"scatter_start_{tag}",
        out_shape=(sem,) * 6 + (pltpu.HBM(parts.shape, parts.dtype), pltpu.HBM(land_shape, parts.dtype),
                                jax.ShapeDtypeStruct((8, LANES), F32)),
        in_specs=(HBM, HBM), out_specs=(SEM,) * 6 + (HBM, HBM, pl.BlockSpec(memory_space=pltpu.VMEM)),
        input_output_aliases={0: 6, 1: 7}, compiler_params=pltpu.CompilerParams(has_side_effects=SIDE_EFFECT),
    )(pltpu.with_memory_space_constraint(parts, pltpu.HBM),
      pltpu.with_memory_space_constraint(lax.empty(land_shape, parts.dtype), pltpu.HBM))


def _scatter_wait(started, after, tag):
    def body(src, land, *rest):
        for cp in _scatter_copies(src, land, rest[:3], rest[3:6]):
            cp.wait_send()
            cp.wait_recv()

    src_thru, land_thru = started[6], started[7]
    return pl.pallas_call(
        body, name=f"scatter_wait_{tag}",
        out_shape=(pltpu.HBM(src_thru.shape, src_thru.dtype), pltpu.HBM(land_thru.shape, land_thru.dtype)),
        in_specs=(HBM, HBM) + (SEM,) * 6 + (pl.BlockSpec(memory_space=pl.ANY),), out_specs=(HBM, HBM),
        input_output_aliases={0: 0, 1: 1}, compiler_params=pltpu.CompilerParams(has_side_effects=SIDE_EFFECT),
    )(src_thru, land_thru, *started[:6], after)[1]


def _sum_all_devices(vec):
    rows = vec.shape[0]

    def body(x_ref, total_ref, all_ref, send_sems, recv_sems, local_sem):
        x, y, c, chips = _place()
        me, sibling = (x, y, c), (x, y, 1 - c)

        def slot(px, py, pc):
            return all_ref.at[4 * px + 2 * py + pc]

        def copy(k, block, to, src=None):
            return pltpu.make_async_remote_copy(src_ref=slot(*block) if src is None else src, dst_ref=slot(*block),
                                                send_sem=send_sems.at[k], recv_sem=recv_sems.at[k], device_id=to,
                                                device_id_type=MESH)

        mine = pltpu.make_async_copy(x_ref, slot(*me), local_sem)
        mine.start()
        first = [copy(0, me, sibling, src=x_ref)] + [copy(1 + j, me, (*chip, c), src=x_ref) for j, chip in enumerate(chips)]
        for cp in first:
            cp.start()
        passed = [copy(4 + j, (*chip, c), sibling) for j, chip in enumerate(chips)]
        for j, chip in enumerate(chips):
            copy(1 + j, (*chip, c), me).wait_recv()
            passed[j].start()
        copy(0, sibling, me).wait_recv()
        for j, chip in enumerate(chips):
            copy(4 + j, (*chip, 1 - c), me).wait_recv()
        for cp in first + passed:
            cp.wait_send()
        mine.wait()
        acc = all_ref[0]
        for d in range(1, 8):
            acc = acc + all_ref[d]
        total_ref[...] = acc

    vmem = pl.BlockSpec(memory_space=pltpu.VMEM)
    return pl.pallas_call(
        body, out_shape=[jax.ShapeDtypeStruct((rows, LANES), F32), jax.ShapeDtypeStruct((8, rows, LANES), F32)],
        in_specs=[vmem], out_specs=[vmem, vmem],
        scratch_shapes=[pltpu.SemaphoreType.DMA((7,)), pltpu.SemaphoreType.DMA((7,)), pltpu.SemaphoreType.DMA(())],
        name="sum_all_devices", compiler_params=pltpu.CompilerParams(vmem_limit_bytes=VMEM_LIMIT_BYTES),
    )(vec)[0]


PACK_COLS = 1024
BIG = (("mlp_w_up", 2, 1024, True), ("mlp_w_down", 2, 1024, False), ("xa_w_kv", 2, 512, True), ("xa_w_q", 2, 256, False),
       ("xa_w_o", 2, 256, False), ("ab_w_out", 1, 256, False), ("cd_w_out", 1, 256, False), ("cd_w_in", 1, 576, True),
       ("ab_w_in", 1, 642, True))
ENTRIES = tuple((name, layer, rows, transposed) for name, layers, rows, transposed in BIG for layer in range(layers))
FIRST_ENTRIES = tuple(e for e in ENTRIES if e[0].startswith("ab_"))
LATER_ENTRIES = tuple(e for e in ENTRIES if not e[0].startswith("ab_"))


def _tile_rows(entry):
    return -(-entry[2] // 16) * 16


def _padded_rows(entries):
    return -(-sum(_tile_rows(e) for e in entries) // 32) * 32


SMALL = (("ab_norm", (1, 1024)), ("ab_conv_w", (1, 4, 512)), ("ab_conv_b", (1, 512)), ("lru_w_a", (1, 8, 64, 64)),
         ("lru_b_a", (1, 512)), ("lru_w_i", (1, 8, 64, 64)), ("lru_b_i", (1, 512)), ("lru_lambda", (1, 512)),
         ("fox_b_f", (1, 8)), ("cd_norm", (1, 1024)), ("cd_sink", (1, 8)), ("xa_norm", (2, 1024)),
         ("xa_mem_norm", (2, 1024)), ("mlp_norm", (2, 1024)), ("final_norm", (1024,)), ("loss", (1,)))
SPLIT_SMALL = ("ab_conv_w", "cd_norm")


def _pack_rows(parts, entries, dtype):
    lead = parts[0].shape[:-2]
    zeros = lambda rows: jnp.zeros(lead + (rows, PACK_COLS), dtype)
    pieces = [jnp.pad(p.astype(dtype), ((0, 0),) * len(lead) + ((0, _tile_rows(e) - e[2]), (0, 0))) for p, e in zip(parts, entries)]
    tail = _padded_rows(entries) - sum(_tile_rows(e) for e in entries)
    return jnp.concatenate(pieces + ([zeros(tail)] if tail else []), axis=len(lead))


def _unpack_rows(packed, entries):
    out, r0 = [], 0
    for e in entries:
        out.append(packed[..., r0:r0 + e[2], :])
        r0 += _tile_rows(e)
    return out


def _shard_rows(w, entries):
    return [(w[name][layer].T if transposed else w[name][layer]) for name, layer, _, transposed in entries]


def _shard_blocks(rows):
    out = {}
    for (name, layer, _, transposed), r in zip(ENTRIES, rows):
        out.setdefault(name, []).append(r.T if transposed else r)
    return {name: jnp.stack(layers) for name, layers in out.items()}


def _pack_small(vals):
    flat = jnp.concatenate([vals[name].astype(F32).reshape(-1) for name, _ in SMALL])
    size = -(-flat.shape[0] // (8 * LANES)) * (8 * LANES)
    return jnp.pad(flat, (0, size - flat.shape[0])).reshape(-1, LANES)


def _unpack_small(packed):
    flat, out, at = packed.reshape(-1), {}, 0
    for name, shape in SMALL:
        out[name] = flat[at:at + math.prod(shape)].reshape(shape)
        at += math.prod(shape)
    return out


def _chip_part(full, chip):
    width = full.shape[-1] // N_CHIPS
    return lax.dynamic_slice_in_dim(full, chip * width, width, axis=full.ndim - 1)


def _chip_embed(part, chip):
    full = jnp.zeros(part.shape[:-1] + (part.shape[-1] * N_CHIPS,), part.dtype)
    return lax.dynamic_update_slice_in_dim(full, part, chip * part.shape[-1], axis=part.ndim - 1)


SUBLAYER_KEYS = {"ab_w_in": ("ab", "w_in_t"), "ab_w_out": ("ab", "w_out"), "cd_w_in": ("cd", "w_in_t"), "cd_w_out": ("cd", "w_out"),
                 "xa_w_q": ("xa", "w_q"), "xa_w_kv": ("xa", "w_kv_t"), "xa_w_o": ("xa", "w_o"), "mlp_w_up": ("mlp", "w_up_t"),
                 "mlp_w_down": ("mlp", "w_down")}


def _sublayer(name, layer):
    block, leaf = SUBLAYER_KEYS[name]
    return (block if block in ("ab", "cd") else f"{block}{layer}"), leaf


def _set_big(params, full_rows):
    for (name, layer), rows in full_rows.items():
        block, leaf = _sublayer(name, layer)
        if name == "ab_w_in":
            rows = jnp.concatenate([rows, jnp.zeros((LANES - 8, PACK_COLS), rows.dtype)])
        params[block][leaf] = rows


def _build_params(full_rows, w):
    pad = lambda a: jnp.pad(a, ((0, 0), (0, LANES - a.shape[1])))
    params = {
        "ab": dict(norm=w["ab_norm"], conv_w=w["ab_conv_w"][0], conv_b=w["ab_conv_b"], w_a=_block_diag(w["lru_w_a"][0]),
                   b_a=w["lru_b_a"], w_i=_block_diag(w["lru_w_i"][0]), b_i=w["lru_b_i"], lam=w["lru_lambda"],
                   b_f=w["fox_b_f"].reshape(8, 1)),
        "cd": dict(norm=w["cd_norm"], sink=pad(w["cd_sink"])),
        "final_norm": w["final_norm"].reshape(1, D_MODEL),
    }
    for layer in range(2):
        params[f"xa{layer}"] = dict(norm=w["xa_norm"][layer:layer + 1], mem_norm=w["xa_mem_norm"][layer:layer + 1])
        params[f"mlp{layer}"] = dict(norm=w["mlp_norm"][layer:layer + 1])
    _set_big(params, full_rows)
    return params


def _grad_rows(g, entries=ENTRIES):
    out = []
    for name, layer, _, _ in entries:
        block, leaf = _sublayer(name, layer)
        out.append(g[block][leaf])
    return out


def _reduce_group(entry):
    name, layer = entry[0], entry[1]
    if name.startswith("ab_"):
        return 2
    return 0 if layer == 1 or name.startswith("cd_") else 1


REDUCE_GROUPS = tuple(tuple(e for e in ENTRIES if _reduce_group(e) == group) for group in range(3))


def _reduce_tile(half):
    return max(t for t in range(16, 1025, 16) if half % t == 0)


def _reduce_start(grads_by_chip, core, chip, tag):
    half = grads_by_chip.shape[1] // 2
    tile = _reduce_tile(half)
    steps = half // tile
    place = jnp.stack([core, chip]).astype(jnp.int32)
    got = _swap_cores(grads_by_chip, name=f"swap_cores_{tag}", half_rows=half)
    block = (1, tile, PACK_COLS)

    def sum_cores(place_ref, mine_ref, got_ref, f32_ref, bf16_ref):
        total = mine_ref[...].astype(F32) + got_ref[...].astype(F32)
        f32_ref[...] = total
        bf16_ref[...] = total.astype(BF16)

    pair32, pair16 = pl.pallas_call(
        sum_cores, name=f"sum_cores_{tag}",
        grid_spec=pltpu.PrefetchScalarGridSpec(
            num_scalar_prefetch=1, grid=(N_CHIPS, steps),
            in_specs=[pl.BlockSpec(block, lambda s, i, place_ref: (s, place_ref[0] * steps + i, 0)),
                      pl.BlockSpec(block, lambda s, i, place_ref: (s, i, 0))],
            out_specs=[pl.BlockSpec(block, lambda s, i, place_ref: (s, i, 0))] * 2),
        out_shape=[jax.ShapeDtypeStruct((N_CHIPS, half, PACK_COLS), F32), jax.ShapeDtypeStruct((N_CHIPS, half, PACK_COLS), BF16)],
        compiler_params=_params("arbitrary", "arbitrary"),
    )(place, grads_by_chip, got)
    return pair32, _scatter_start(pair16, tag), place


def _reduce_finish(state, core, tag, after):
    pair32, started, place = state
    half = pair32.shape[1]
    tile = _reduce_tile(half)
    landed = _scatter_wait(started, after, tag)

    def sum_chips(place_ref, own_ref, landed_ref, out_ref):
        out_ref[...] = own_ref[0] + landed_ref[0].astype(F32) + landed_ref[1].astype(F32) + landed_ref[2].astype(F32)

    done = pl.pallas_call(
        sum_chips, name=f"sum_chips_{tag}",
        grid_spec=pltpu.PrefetchScalarGridSpec(
            num_scalar_prefetch=1, grid=(half // tile,),
            in_specs=[pl.BlockSpec((1, tile, PACK_COLS), lambda i, place_ref: (place_ref[1], i, 0)),
                      pl.BlockSpec((3, tile, PACK_COLS), lambda i, place_ref: (0, i, 0))],
            out_specs=pl.BlockSpec((tile, PACK_COLS), lambda i, place_ref: (i, 0))),
        out_shape=jax.ShapeDtypeStruct((half, PACK_COLS), F32), compiler_params=_params("arbitrary"),
    )(place, pair32, landed)
    theirs = _swap_cores(done, name=f"share_halves_{tag}")
    return jnp.where(core == 0, jnp.concatenate([done, theirs]), jnp.concatenate([theirs, done]))


def kernel(x, mem, ab_norm, ab_w_in, ab_conv_w, ab_conv_b, lru_w_a, lru_b_a, lru_w_i, lru_b_i, lru_lambda, fox_b_f, ab_w_out, cd_norm, cd_w_in, cd_sink, cd_w_out, xa_norm, xa_mem_norm, xa_w_q, xa_w_kv, xa_w_o, mlp_norm, mlp_w_up, mlp_w_down, final_norm, loss_target, m_ab_norm, m_ab_w_in, m_ab_conv_w, m_ab_conv_b, m_lru_w_a, m_lru_b_a, m_lru_w_i, m_lru_b_i, m_lru_lambda, m_fox_b_f, m_ab_w_out, m_cd_norm, m_cd_w_in, m_cd_sink, m_cd_w_out, m_xa_norm, m_xa_mem_norm, m_xa_w_q, m_xa_w_kv, m_xa_w_o, m_mlp_norm, m_mlp_w_up, m_mlp_w_down, m_final_norm, v_ab_norm, v_ab_w_in, v_ab_conv_w, v_ab_conv_b, v_lru_w_a, v_lru_b_a, v_lru_w_i, v_lru_b_i, v_lru_lambda, v_fox_b_f, v_ab_w_out, v_cd_norm, v_cd_w_in, v_cd_sink, v_cd_w_out, v_xa_norm, v_xa_mem_norm, v_xa_w_q, v_xa_w_kv, v_xa_w_o, v_mlp_norm, v_mlp_w_up, v_mlp_w_down, v_final_norm):
    given = dict(locals())
    weight_names = [name for name, _ in SMALL if name != "loss"] + [name for name, _, _, _ in BIG]
    w = {name: given[name] for name in weight_names}
    chip = 2 * lax.axis_index("x") + lax.axis_index("y")
    core = lax.axis_index("c")

    slot = lax.broadcasted_iota(jnp.int32, (N_CHIPS, 1, 1), 0)

    def whole(entries, mine, gathered):
        return {(name, layer): jnp.where(slot == chip, own[None], got).reshape(N_CHIPS * rows, PACK_COLS)
                for (name, layer, rows, _), own, got in zip(entries, _unpack_rows(mine, entries), _unpack_rows(gathered, entries))}

    mine_first = _pack_rows(_shard_rows(w, FIRST_ENTRIES), FIRST_ENTRIES, BF16)
    mine_later = _pack_rows(_shard_rows(w, LATER_ENTRIES), LATER_ENTRIES, BF16)
    first = _gather_chips(mine_first)
    started = _gather_start(mine_later)
    owner = (core == 0).astype(F32)
    quarters = {name: _chip_embed(w[name], chip) * owner for name in SPLIT_SMALL}
    zeros = {name: jnp.zeros(shape, F32) for name, shape in SMALL}
    small_full = _unpack_small(_sum_all_devices(_pack_small({**zeros, **quarters})))
    params = _build_params(whole(FIRST_ENTRIES, mine_first, first),
                           {**w, "ab_conv_w": small_full["ab_conv_w"], "cd_norm": small_full["cd_norm"]})
    params["ab"]["norm"] = params["ab"]["norm"] + started[8][0, 0]

    def complete(h):
        _set_big(params, whole(LATER_ENTRIES, mine_later, _gather_pass(_gather_wait(started, after=h))))

    reducing = {}

    def grads_ready(group, g):
        entries = REDUCE_GROUPS[group]
        by_chip = _pack_rows([r.reshape(N_CHIPS, e[2], PACK_COLS) for r, e in zip(_grad_rows(g, entries), entries)], entries, BF16)
        reducing[group] = _reduce_start(by_chip, core, chip, f"g{group}")
        if group < 2:
            block, leaf = ("mlp0", "w_down") if group == 0 else ("ab", "w_out")
            params[block][leaf] = params[block][leaf] + reducing[group][1][8][0, 0].astype(BF16)

    loss_part, grad_x, g = _local_step(x, mem, loss_target, params, complete, grads_ready)
    grads_ready(2, g)
    reduced = {}
    for group, entries in enumerate(REDUCE_GROUPS):
        rows = _unpack_rows(_reduce_finish(reducing[group], core, f"g{group}", after=grad_x), entries)
        reduced.update({(e[0], e[1]): r for e, r in zip(entries, rows)})
    grads = _shard_blocks([reduced[e[0], e[1]] for e in ENTRIES])
    pair = lambda key, leaf: jnp.stack([g[f"{key}0"][leaf], g[f"{key}1"][leaf]])

    small_local = {"ab_norm": g["ab"]["norm"], "ab_conv_w": g["ab"]["conv_w"], "ab_conv_b": g["ab"]["conv_b"],
                   "lru_w_a": g["ab"]["w_a"], "lru_b_a": g["ab"]["b_a"], "lru_w_i": g["ab"]["w_i"], "lru_b_i": g["ab"]["b_i"],
                   "lru_lambda": g["ab"]["lam"], "fox_b_f": g["ab"]["b_f"], "cd_norm": g["cd"]["norm"], "cd_sink": g["cd"]["sink"],
                   "xa_norm": pair("xa", "norm"), "xa_mem_norm": pair("xa", "mem_norm"), "mlp_norm": pair("mlp", "norm"),
                   "final_norm": g["final_norm"], "loss": loss_part[0, :1]}
    small_sum_packed = _sum_all_devices(_pack_small(small_local))
    small_sum = _unpack_small(small_sum_packed)
    loss = small_sum["loss"][0]

    delta, new_m, new_v = {}, {}, {}
    for name, _, _, _ in BIG:
        shape = w[name].shape
        flat = lambda a: a.reshape(-1, shape[-1])
        d, m2, v2 = _adamw(flat(w[name]), flat(grads[name]), flat(given["m_" + name]), flat(given["v_" + name]), name=f"adamw_{name}")
        delta[name], new_m[name], new_v[name] = d.reshape(shape), m2.reshape(shape), v2.reshape(shape)

    def small_state(prefix):
        vals = {name: (given[prefix + name] if name != "loss" else jnp.zeros((1,), F32)) for name, _ in SMALL}
        for name in SPLIT_SMALL:
            vals[name] = _chip_embed(vals[name], chip)
        return _pack_small(vals)
    packed = _adamw(small_state(""), small_sum_packed, small_state("m_"), small_state("v_"), name="adamw_small")
    for store, vals in zip((delta, new_m, new_v), packed):
        for name, val in _unpack_small(vals).items():
            store[name] = _chip_part(val, chip) if name in SPLIT_SMALL else val
    for name in SPLIT_SMALL:
        small_sum[name] = _chip_part(small_sum[name], chip)
    grads.update({name: small_sum[name] for name, _ in SMALL})

    order = ["ab_norm", "ab_w_in", "ab_conv_w", "ab_conv_b", "lru_w_a", "lru_b_a", "lru_w_i", "lru_b_i", "lru_lambda", "fox_b_f",
             "ab_w_out", "cd_norm", "cd_w_in", "cd_sink", "cd_w_out", "xa_norm", "xa_mem_norm", "xa_w_q", "xa_w_kv", "xa_w_o",
             "mlp_norm", "mlp_w_up", "mlp_w_down", "final_norm"]
    return (loss, grad_x, *[grads[n] for n in order], *[delta[n] for n in order], *[new_m[n] for n in order],
            *[new_v[n] for n in order])
```

```python
import functools
import math

import jax
import jax.numpy as jnp
from jax import lax
from jax.experimental import pallas as pl
from jax.experimental.pallas import tpu as pltpu

F32 = jnp.float32
BF16 = jnp.bfloat16
MESH = pl.DeviceIdType.MESH

D_MODEL = 1024
SEQ = 2048
HEAD_DIM = 64
LRU_WIDTH = 512
LRU_BLOCKS = 8
LRU_C = 8.0
CONV_WIDTH = 4
ATT_W = 512
SWA_KW = 128
SWA_WINDOW = 128
DIL_PATTERN = ((128, 1), (512, 4), (2048, 16))
MEM_LEN = 256
XA_HEADS = 4
XA_HEAD_DIM = 256
D_FF = 4096
ROPE_THETA = 10000.0
EPS = 1e-6
N_CHIPS = 4
LANES = 128

ADAM_LR, ADAM_B1, ADAM_B2, ADAM_EPS, ADAM_WD, ADAM_STEP = 0.001, 0.9, 0.999, 1e-08, 0.01, 10

VMEM_LIMIT_BYTES = 56 * 1024 * 1024
MASKED = -1e30
ROW_TILE = 512
LRU_CHUNK = 512
ATT_BLOCK = 128
BAND_ROWS = 256
ATT_CHUNK = 512
CAUSAL_ROWS = 256


def _params(*sem):
    return pltpu.CompilerParams(dimension_semantics=sem, vmem_limit_bytes=VMEM_LIMIT_BYTES)


def _rowwise(fn, rows, consts=(), out_rows=(), out_accs=(), *, name, tile=ROW_TILE, row_maps=None):
    rows = [r if isinstance(r, tuple) else (r, r.shape[1], 0) for r in rows]
    consts = list(consts)
    nr, nc, no = len(rows), len(consts), len(out_rows)
    dealt_in = [r[3] if isinstance(r[0], str) else None for r in rows]
    dealt_out = [o[2] if len(o) == 3 else None for o in out_rows]
    total = next(r[0].shape[0] for r in rows if not isinstance(r[0], str))
    tile = min(tile, total)
    assert total % tile == 0
    n = total // tile
    n_acc = len(out_accs)

    def body(*refs):
        scratch = list(refs[nr + nc + no + n_acc:])
        vals = []
        for ref, d, row in zip(refs[:nr], dealt_in, rows):
            if d is None:
                vals.append(ref[...])
                continue
            sc, width = scratch.pop(0), row[2]
            for r in range(d):
                for c in range(width // LANES):
                    lanes = slice(r * width + c * LANES, r * width + (c + 1) * LANES)
                    sc.at[c][pl.ds(r, tile // d, stride=d), :] = ref[:, lanes].astype(F32)
            vals.append(jnp.concatenate([sc[c] for c in range(width // LANES)], axis=1))
        outs = fn(*vals, *[r[...] for r in refs[nr:nr + nc]])
        outs = tuple(outs) if isinstance(outs, (tuple, list)) else (outs,)
        for ref, val, d, spec in zip(refs[nr + nc:nr + nc + no], outs[:no], dealt_out, out_rows):
            if d is None:
                ref[...] = val.astype(ref.dtype)
                continue
            sc, width = scratch.pop(0), spec[0]
            for c in range(width // LANES):
                sc[c] = val[:, c * LANES:(c + 1) * LANES].astype(F32)
            for r in range(d):
                for c in range(width // LANES):
                    lanes = slice(r * width + c * LANES, r * width + (c + 1) * LANES)
                    ref[:, lanes] = sc.at[c][pl.ds(r, tile // d, stride=d), :].astype(ref.dtype)
        for ref, val in zip(refs[nr + nc + no:nr + nc + no + n_acc], outs[no:]):
            @pl.when(pl.program_id(0) == 0)
            def _(ref=ref):
                ref[...] = jnp.zeros(ref.shape, ref.dtype)
            ref[...] += val

    in_specs, args, scratch_shapes = [], [], []
    for idx, row in enumerate(rows):
        if isinstance(row[0], str):
            _, arr, width, d = row
            in_specs.append(pl.BlockSpec((tile // d, d * width), lambda i: (i, 0)))
            scratch_shapes.append(pltpu.VMEM((width // LANES, tile, LANES), F32))
        elif row_maps is not None and row_maps[idx] is not None:
            arr, width, _ = row
            in_specs.append(pl.BlockSpec((tile, width), row_maps[idx]))
        else:
            arr, width, cb = row
            in_specs.append(pl.BlockSpec((tile, width), functools.partial(lambda i, cb: (i, cb), cb=cb)))
        args.append(arr)
    in_specs += [pl.BlockSpec(c.shape, lambda i: (0, 0)) for c in consts]
    out_shape, out_specs = [], []
    for spec, d in zip(out_rows, dealt_out):
        w, dt = spec[0], spec[1]
        if d is None:
            out_shape.append(jax.ShapeDtypeStruct((total, w), dt))
            out_specs.append(pl.BlockSpec((tile, w), lambda i: (i, 0)))
        else:
            out_shape.append(jax.ShapeDtypeStruct((total // d, d * w), dt))
            out_specs.append(pl.BlockSpec((tile // d, d * w), lambda i: (i, 0)))
            scratch_shapes.append(pltpu.VMEM((w // LANES, tile, LANES), F32))
    out_shape += [jax.ShapeDtypeStruct(s, F32) for s in out_accs]
    out_specs += [pl.BlockSpec(s, lambda i: (0, 0)) for s in out_accs]
    return pl.pallas_call(
        body, grid=(n,), in_specs=in_specs, out_specs=out_specs, out_shape=out_shape, scratch_shapes=scratch_shapes, name=name,
        compiler_params=_params("arbitrary"),
    )(*args, *consts)


MATMUL_VMEM_BYTES = 40 * 1024 * 1024


def _matmul_tiles(m, n, k, tm, tn, out_bytes):
    tm, tn, tk = min(tm, m), min(tn, n), k

    def need():
        return 2 * (2 * tk * (tm + tn) + tm * tn * out_bytes) + (0 if tk == k else 4 * tm * tn)

    while need() > MATMUL_VMEM_BYTES:
        if tm >= tn and tm % 256 == 0:
            tm //= 2
        elif tn % 256 == 0:
            tn //= 2
        else:
            tk //= 2
    return tm, tn, tk


def _matmul(a, b, *, ta=False, tb=False, outs=(F32,), epilogue=None, extras=(), consts=(), sums=(), whole_rows=False,
            tm=1024, tn=1024, name):
    m, k = (a.shape[1], a.shape[0]) if ta else a.shape
    n = b.shape[0] if tb else b.shape[1]
    assert (b.shape[1] if tb else b.shape[0]) == k
    outs = [o if isinstance(o, tuple) else (o, None) for o in outs]
    out_bytes = sum(jnp.dtype(dt).itemsize for dt, w in outs if w is None) + sum(e.dtype.itemsize for e in extras)
    tm, tn, tk = _matmul_tiles(m, n, k, tm, tn, out_bytes)
    assert m % tm == 0 and n % tn == 0 and k % tk == 0, (name, m, n, k)
    nk = k // tk
    ne, nc, no = len(extras), len(consts), len(outs)
    assert tn == n or not (sums or whole_rows or any(w is not None for _, w in outs)), name
    dims = (((0 if ta else 1,), (1 if tb else 0,)), ((), ()))

    def body(*refs):
        a_ref, b_ref = refs[:2]
        e_refs, o_refs = refs[2:2 + ne + nc], refs[2 + ne + nc:2 + ne + nc + no]
        s_refs = refs[2 + ne + nc + no:2 + ne + nc + no + len(sums)]

        def finish(acc):
            vals = (acc,) if epilogue is None else epilogue(acc, *[e[...] for e in e_refs])
            for ref, val in zip(o_refs, vals[:no]):
                ref[...] = val.astype(ref.dtype)
            for ref, val in zip(s_refs, vals[no:]):
                @pl.when(pl.program_id(0) == 0)
                def _(ref=ref, val=val):
                    ref[...] = val

                @pl.when(pl.program_id(0) > 0)
                def _(ref=ref, val=val):
                    ref[...] += val

        prod = lax.dot_general(a_ref[...], b_ref[...], dims, preferred_element_type=F32)
        if nk == 1:
            finish(prod)
        else:
            acc_ref = refs[-1]
            step = pl.program_id(2)

            @pl.when(step == 0)
            def _():
                acc_ref[...] = prod

            @pl.when(step > 0)
            def _():
                acc_ref[...] += prod

            @pl.when(step == nk - 1)
            def _():
                finish(acc_ref[...])

    a_spec = pl.BlockSpec((tk, tm), lambda i, j, s: (s, i)) if ta else pl.BlockSpec((tm, tk), lambda i, j, s: (i, s))
    b_spec = pl.BlockSpec((tn, tk), lambda i, j, s: (j, s)) if tb else pl.BlockSpec((tk, tn), lambda i, j, s: (s, j))
    tile_spec = pl.BlockSpec((tm, tn), lambda i, j, s: (i, j))
    whole = lambda shape: pl.BlockSpec(shape, lambda i, j, s: (0, 0))
    return pl.pallas_call(
        body, grid=(m // tm, n // tn, nk),
        in_specs=[a_spec, b_spec] + [tile_spec] * ne + [whole(c.shape) for c in consts],
        out_specs=[tile_spec if w is None else pl.BlockSpec((tm, w), lambda i, j, s: (i, 0)) for _, w in outs]
        + [whole(shape) for shape in sums],
        out_shape=[jax.ShapeDtypeStruct((m, n if w is None else w), dt) for dt, w in outs]
        + [jax.ShapeDtypeStruct(shape, F32) for shape in sums],
        scratch_shapes=[pltpu.VMEM((tm, tn), F32)] if nk > 1 else [],
        name=name, compiler_params=_params("arbitrary" if sums else "parallel", "parallel", "arbitrary"),
    )(a, b, *extras, *consts)


def _split3(x):
    x1 = x.astype(BF16)
    r1 = x - x1.astype(F32)
    x2 = r1.astype(BF16)
    x3 = (r1 - x2.astype(F32)).astype(BF16)
    return x1, x2, x3


def _dot_exact(x, e):
    return sum(jnp.dot(p, e, preferred_element_type=F32) for p in _split3(x))


def _head_expand(heads, width):
    dh = width // heads
    rows = jnp.arange(LANES)[:, None]
    cols = jnp.arange(width)[None, :]
    return (cols // dh == rows).astype(BF16)


def _rstd(x):
    return lax.rsqrt(jnp.mean(x * x, axis=-1, keepdims=True) + EPS)


def _rms_fwd(x, gain, *, name):
    return _rowwise(lambda x, g: x * _rstd(x) * g, [x], [gain], [(x.shape[1], BF16)], name=name)[0]


def _rms_bwd_vals(x, dhn, gain):
    r = _rstd(x)
    xh = x * r
    dxh = dhn * gain
    dx = r * (dxh - xh * jnp.mean(dxh * xh, axis=-1, keepdims=True))
    return dx, jnp.sum(dhn * xh, axis=0, keepdims=True)


def _norm_input_grad(dz, w, x, dres, gain, *, tb=False, name):
    def epilogue(dhn, x, dres, g):
        dx, dg = _rms_bwd_vals(x, dhn, g)
        dh = dres + dx
        return dh, dh, dg
    return _matmul(dz, w, tb=tb, outs=(F32, BF16), extras=(x, dres), consts=(gain,), sums=((1, x.shape[1]),), epilogue=epilogue,
                   name=name)


def _loss_and_grad(h, target, gain, *, name):
    def fn(h, tgt, g):
        r = _rstd(h)
        err = h * r * g - tgt
        loss = 0.5 * jnp.sum(jnp.mean(err * err, axis=-1, keepdims=True), axis=0, keepdims=True)
        dx, dg = _rms_bwd_vals(h, err * (1.0 / D_MODEL), g)
        return dx, dx, jnp.broadcast_to(loss, (1, LANES)), dg
    d = h.shape[1]
    return _rowwise(fn, [h, target], [gain], [(d, F32), (d, BF16)], [(1, LANES), (1, d)], name=name)


def _adamw(w, g, m, v, *, name):
    rows, cols = w.shape
    tile = rows
    for cand in (256, 128, 64, 32, 16, 8):
        if rows % cand == 0 and rows > cand:
            tile = cand
            break
    bc1 = 1.0 - ADAM_B1 ** ADAM_STEP
    bc2 = 1.0 - ADAM_B2 ** ADAM_STEP

    def fn(w, g, m, v):
        m2 = ADAM_B1 * m + (1.0 - ADAM_B1) * g
        v2 = ADAM_B2 * v + (1.0 - ADAM_B2) * (g * g)
        delta = -ADAM_LR * ((m2 / bc1) / (jnp.sqrt(v2 / bc2) + ADAM_EPS) + ADAM_WD * w)
        return delta, m2, v2
    return _rowwise(fn, [w, g, m, v], [], [(cols, F32)] * 3, name=name, tile=tile)


def _attn_specs(arr, width, cb, classes, rows_block, whole):
    ncols = arr.shape[2] // (classes * width)
    if whole:
        return pl.BlockSpec((1, arr.shape[1], width), lambda n, r, i: (n, 0, r * ncols + cb))
    return pl.BlockSpec((1, rows_block, width), lambda n, r, i: (n, i, r * ncols + cb))


def _attn_tiles(mode, lq, lk, backward=False):
    if mode == "full":
        return min(lq, 256), min(lq, 256), lk
    if mode == "band":
        tq = min(BAND_ROWS, lq)
        rows = tq if backward else ATT_BLOCK
        return tq, rows, min(rows + ATT_BLOCK, lk)
    return CAUSAL_ROWS, CAUSAL_ROWS, ATT_CHUNK


def _window(mode, row0, j, rows, win, lk):
    if mode == "causal":
        return pl.multiple_of(j * win, win), row0 // win + 1
    if mode == "band":
        return pl.multiple_of(jnp.clip(row0 + rows - win, 0, lk - win), ATT_BLOCK), 1
    return 0, 1


def _allowed(mode, row0, start, rows, win, max_dist):
    if mode == "full":
        return None
    dist = (row0 + lax.broadcasted_iota(jnp.int32, (rows, win), 0)) - (start + lax.broadcasted_iota(jnp.int32, (rows, win), 1))
    ok = dist >= 0
    if max_dist is not None:
        ok = ok & (dist <= max_dist)
    return ok


def _head_groups(heads, dh):
    gw = max(LANES, dh)
    return gw, gw // dh, heads // (gw // dh)


def _own_lanes(rows, gw, dh, u):
    return lax.broadcasted_iota(jnp.int32, (rows, gw), 1) // dh == u


def _only_head(x, own, per, u):
    return x if per == 1 else jnp.where(own[u], x, jnp.zeros_like(x))


def _step_scores(qz, kw, kb_row, ok, scale):
    s = lax.dot_general(qz, kw, (((1,), (1,)), ((), ())), preferred_element_type=F32) * scale
    if kb_row is not None:
        s = s - kb_row
    if ok is not None:
        s = jnp.where(ok, s, MASKED)
    return s


def _attn_fwd(q, k, v, kb=None, *, classes=1, heads, dh, mode, max_dist=None, bf16_copy=False, name):
    (qa, qc), (ka, kc), (va, vc) = q, k, v
    n, lq, lk = qa.shape[0], qa.shape[1], ka.shape[1]
    width = heads * dh
    tq, rows, win = _attn_tiles(mode, lq, lk)
    gw, per, groups = _head_groups(heads, dh)
    scale = dh ** -0.5
    has_kb = kb is not None
    single = mode != "causal"

    def body(*refs):
        for sub in range(tq // rows):
            part(refs, pl.program_id(2) * tq + sub * rows, slice(sub * rows, (sub + 1) * rows))

    def part(refs, row0, rs):
        q_ref, k_ref, v_ref = refs[:3]
        kb_ref = refs[3] if has_kb else None
        n_in = 4 if has_kb else 3
        o_ref, lse_ref = refs[n_in:n_in + 2]
        o16_ref = refs[n_in + 2] if bf16_copy else None
        lane = lax.broadcasted_iota(jnp.int32, (rows, LANES), 1)
        own = [_own_lanes(rows, gw, dh, u) for u in range(per)]

        def step(j, carry):
            m_in, l_in = carry
            m_out, l_out = m_in, l_in
            start, _ = _window(mode, row0, j, rows, win, lk)
            ok = _allowed(mode, row0, start, rows, win, max_dist)
            for g in range(groups):
                cols = slice(g * gw, (g + 1) * gw)
                qg = q_ref[0, rs, cols]
                kw = k_ref[0, pl.ds(start, win), cols]
                vw = v_ref[0, pl.ds(start, win), cols]
                alpha_g = new_g = None
                for u in range(per):
                    h = g * per + u
                    kb_row = kb_ref[0, h, pl.ds(j, 1), :] if has_kb else None
                    s = _step_scores(_only_head(qg, own, per, u), kw, kb_row, ok, scale)
                    m_new = jnp.max(s, axis=1, keepdims=True)
                    if not single:
                        m_old = m_in[:, h:h + 1]
                        m_new = jnp.maximum(m_old, m_new)
                        alpha = jnp.exp(m_old - m_new)
                        alpha_g = alpha if u == 0 else jnp.where(own[u], alpha, alpha_g)
                    p = jnp.exp(s - m_new)
                    l_new = jnp.sum(p, axis=1, keepdims=True)
                    r = jnp.dot(p.astype(BF16), vw, preferred_element_type=F32)
                    if single:
                        r = r * (1.0 / l_new)
                    else:
                        l_new = alpha * l_in[:, h:h + 1] + l_new
                    new_g = r if u == 0 else jnp.where(own[u], r, new_g)
                    m_out = jnp.where(lane == h, m_new, m_out)
                    l_out = jnp.where(lane == h, l_new, l_out)
                o_ref[0, rs, cols] = new_g if single else alpha_g * o_ref[0, rs, cols] + new_g
                if bf16_copy:
                    o16_ref[0, rs, cols] = new_g.astype(BF16)
            return m_out, l_out

        carry = (jnp.full((rows, LANES), MASKED, F32), jnp.zeros((rows, LANES), F32))
        if single:
            m_all, l_all = step(0, carry)
            l_all = jnp.where(lane < heads, l_all, 1.0)
        else:
            o_ref[...] = jnp.zeros(o_ref.shape, F32)
            m_all, l_all = lax.fori_loop(0, _window(mode, row0, 0, rows, win, lk)[1], step, carry)
            l_all = jnp.where(lane < heads, l_all, 1.0)
            inv = 1.0 / l_all
            for g in range(groups):
                cols = slice(g * gw, (g + 1) * gw)
                inv_g = inv[:, g * per:g * per + 1]
                for u in range(1, per):
                    inv_g = jnp.where(own[u], inv[:, g * per + u:g * per + u + 1], inv_g)
                o_ref[0, rs, cols] = o_ref[0, rs, cols] * inv_g
        lse_ref[0, rs, :] = jnp.where(lane < heads, m_all + jnp.log(l_all), 0.0)

    in_specs = [_attn_specs(qa, width, qc, classes, tq, False), _attn_specs(ka, width, kc, classes, win, True),
                _attn_specs(va, width, vc, classes, win, True)]
    args = [qa, ka, va]
    if has_kb:
        in_specs.append(pl.BlockSpec((1,) + kb.shape[1:], lambda n_, r, i: (n_, 0, 0, 0)))
        args.append(kb)
    out_shape = [jax.ShapeDtypeStruct((n, lq, classes * width), F32), jax.ShapeDtypeStruct((n, lq, classes * LANES), F32)]
    out_specs = [pl.BlockSpec((1, tq, width), lambda n_, r, i: (n_, i, r)), pl.BlockSpec((1, tq, LANES), lambda n_, r, i: (n_, i, r))]
    if bf16_copy:
        assert single
        out_shape.append(jax.ShapeDtypeStruct((n, lq, classes * width), BF16))
        out_specs.append(out_specs[0])
    return pl.pallas_call(
        body, grid=(n, classes, lq // tq), in_specs=in_specs, out_specs=out_specs, out_shape=out_shape, name=name,
        compiler_params=_params("parallel", "parallel", "arbitrary"),
    )(*args)


def _attn_bwd(q, k, v, do, lse, delta, kb=None, *, classes=1, heads, dh, mode, max_dist=None, dq_dtype=F32, name):
    (qa, qc), (ka, kc), (va, vc), (da, dc) = q, k, v, do
    n, lq, lk = qa.shape[0], qa.shape[1], ka.shape[1]
    width = heads * dh
    tq, rows, win = _attn_tiles(mode, lq, lk, backward=True)
    gw, per, groups = _head_groups(heads, dh)
    scale = dh ** -0.5
    has_kb = kb is not None
    single = mode != "causal"
    nt =(((1,), (1,)), ((), ()))
    tn = (((0,), (0,)), ((), ()))

    def body(*refs):
        n_in = 7 if has_kb else 6
        dk_ref, dv_ref = refs[n_in + 1:n_in + 3]

        @pl.when(pl.program_id(2) == 0)
        def _():
            dk_ref[...] = jnp.zeros(dk_ref.shape, F32)
            dv_ref[...] = jnp.zeros(dv_ref.shape, F32)
            if has_kb:
                refs[n_in + 3][...] = jnp.zeros(refs[n_in + 3].shape, F32)

        for sub in range(tq // rows):
            part(refs, pl.program_id(2) * tq + sub * rows, slice(sub * rows, (sub + 1) * rows))

    def part(refs, row0, rs):
        q_ref, k_ref, v_ref, do_ref, lse_ref, dl_ref = refs[:6]
        kb_ref = refs[6] if has_kb else None
        n_in = 7 if has_kb else 6
        dq_ref, dk_ref, dv_ref = refs[n_in:n_in + 3]
        dkb_ref, drow_ref = refs[n_in + 3:n_in + 5] if has_kb else (None, None)
        lse_all = lse_ref[0, rs, :]
        dl_all = dl_ref[0, rs, :]
        lane = lax.broadcasted_iota(jnp.int32, (rows, LANES), 1)
        own = [_own_lanes(rows, gw, dh, u) for u in range(per)]

        def step(j, drow_all):
            start, _ = _window(mode, row0, j, rows, win, lk)
            ok = _allowed(mode, row0, start, rows, win, max_dist)
            for g in range(groups):
                cols = slice(g * gw, (g + 1) * gw)
                qg = q_ref[0, rs, cols]
                dog = do_ref[0, rs, cols]
                kw = k_ref[0, pl.ds(start, win), cols]
                vw = v_ref[0, pl.ds(start, win), cols]
                dq_g = dk_w = dv_w = None
                for u in range(per):
                    h = g * per + u
                    qz, doz = _only_head(qg, own, per, u), _only_head(dog, own, per, u)
                    kb_row = kb_ref[0, h, pl.ds(j, 1), :] if has_kb else None
                    p = jnp.exp(_step_scores(qz, kw, kb_row, ok, scale) - lse_all[:, h:h + 1])
                    dp = lax.dot_general(doz, vw, nt, preferred_element_type=F32)
                    ds = p * (dp - dl_all[:, h:h + 1])
                    dsb = ds.astype(BF16)
                    r = jnp.dot(dsb, kw, preferred_element_type=F32)
                    dq_g = r if u == 0 else jnp.where(own[u], r, dq_g)
                    dk_u = lax.dot_general(dsb, qz, tn, preferred_element_type=F32)
                    dv_u = lax.dot_general(p.astype(BF16), doz, tn, preferred_element_type=F32)
                    dk_w = dk_u if u == 0 else dk_w + dk_u
                    dv_w = dv_u if u == 0 else dv_w + dv_u
                    if has_kb:
                        dkb_ref[0, h, pl.ds(j, 1), :] += -jnp.sum(ds, axis=0, keepdims=True)
                        drow_all = drow_all + jnp.where(lane == h, jnp.sum(ds, axis=1, keepdims=True), 0.0)
                dk_ref[0, pl.ds(start, win), cols] += dk_w * scale
                dv_ref[0, pl.ds(start, win), cols] += dv_w
                if single:
                    dq_ref[0, rs, cols] = (dq_g * scale).astype(dq_ref.dtype)
                else:
                    dq_ref[0, rs, cols] += dq_g * scale
            return drow_all

        drow_all = jnp.zeros((rows, LANES), F32)
        if single:
            drow_all = step(0, drow_all)
        else:
            dq_ref[...] = jnp.zeros(dq_ref.shape, F32)
            drow_all = lax.fori_loop(0, _window(mode, row0, 0, rows, win, lk)[1], step, drow_all)
        if has_kb:
            drow_ref[0, rs, :] = drow_all

    row_spec = functools.partial(_attn_specs, classes=classes, rows_block=tq, whole=False)
    in_specs = [row_spec(qa, width, qc), _attn_specs(ka, width, kc, classes, win, True), _attn_specs(va, width, vc, classes, win, True),
                row_spec(da, width, dc), row_spec(lse, LANES, 0), row_spec(delta, LANES, 0)]
    args = [qa, ka, va, da, lse, delta]
    assert single or dq_dtype == F32
    out_shape = [jax.ShapeDtypeStruct((n, lq, classes * width), dq_dtype), jax.ShapeDtypeStruct((n, lk, classes * width), F32),
                 jax.ShapeDtypeStruct((n, lk, classes * width), F32)]
    kv_spec = pl.BlockSpec((1, lk, width), lambda n_, r, i: (n_, 0, r))
    out_specs = [pl.BlockSpec((1, tq, width), lambda n_, r, i: (n_, i, r)), kv_spec, kv_spec]
    if has_kb:
        kb_spec = pl.BlockSpec((1,) + kb.shape[1:], lambda n_, r, i: (n_, 0, 0, 0))
        in_specs.append(kb_spec)
        args.append(kb)
        out_shape += [jax.ShapeDtypeStruct(kb.shape, F32), jax.ShapeDtypeStruct((n, lq, LANES), F32)]
        out_specs += [kb_spec, pl.BlockSpec((1, tq, LANES), lambda n_, r, i: (n_, i, 0))]
    return pl.pallas_call(
        body, grid=(n, classes, lq // tq), in_specs=in_specs, out_specs=out_specs, out_shape=out_shape, name=name,
        compiler_params=_params("parallel", "parallel", "arbitrary"),
    )(*args)


def _head_delta(do, out, heads, *, name, lse=None, sink=None):
    width = out.shape[1]
    gather = _head_expand(heads, width).T

    if sink is None:
        return _rowwise(lambda do, o, e: _dot_exact(do * o, e), [do, out], [gather], [(LANES, F32)], name=name)[0]

    def fn(do, o, lse, e, sink):
        delta = _dot_exact(do * o, e)
        return delta, -jnp.sum(jnp.exp(sink - lse) * delta, axis=0, keepdims=True)
    return _rowwise(fn, [do, out, lse], [gather, sink], [(LANES, F32)], [(1, LANES)], name=name)


def _rope_tables():
    half = HEAD_DIM // 2
    inv = ROPE_THETA ** (-jnp.arange(half, dtype=F32) / half)
    ang = jnp.arange(SEQ, dtype=F32)[:, None] * inv[None, :]
    cos = jnp.tile(jnp.cos(ang), (1, 2 * ATT_W // HEAD_DIM))
    sin = jnp.tile(jnp.concatenate([-jnp.sin(ang), jnp.sin(ang)], axis=1), (1, ATT_W // HEAD_DIM))
    return cos, sin


def _rotate(x, cos, sin):
    width = x.shape[1]
    lane = lax.broadcasted_iota(jnp.int32, x.shape, 1)
    first_half = (lane % HEAD_DIM) < (HEAD_DIM // 2)
    swapped = jnp.where(first_half, pltpu.roll(x, width - HEAD_DIM // 2, axis=1), pltpu.roll(x, HEAD_DIM // 2, axis=1))
    return x * cos[:, :width] + swapped * sin[:, :width]


def _gelu(x):
    k = math.sqrt(2.0 / math.pi)
    t = jnp.tanh(k * (x + 0.044715 * x * x * x))
    return 0.5 * x * (1.0 + t), t


def _gelu_grad(x, t):
    k = math.sqrt(2.0 / math.pi)
    return 0.5 * (1.0 + t) + 0.5 * x * (1.0 - t * t) * k * (1.0 + 3.0 * 0.044715 * x * x)


def _shift_down(x, halo, s):
    ext = jnp.concatenate([halo, x], axis=0)
    return pltpu.roll(ext, s, axis=0)[8:, :]


def _shift_up(x, halo, s):
    rows = x.shape[0]
    ext = jnp.concatenate([x, halo], axis=0)
    return pltpu.roll(ext, rows + 8 - s, axis=0)[:rows, :]


def _lru_gates(u, halo, cw, cb, wa, ba, wi, bi, lam):
    taps = [_shift_down(u, halo, CONV_WIDTH - 1 - k) for k in range(CONV_WIDTH - 1)] + [u]
    uc = cb + sum(cw[k:k + 1, :] * taps[k] for k in range(CONV_WIDTH))
    ucb = uc.astype(BF16)
    r = jax.nn.sigmoid(jnp.dot(ucb, wa, preferred_element_type=F32) + ba)
    gi = jax.nn.sigmoid(jnp.dot(ucb, wi, preferred_element_type=F32) + bi)
    sp = jnp.maximum(-lam, 0.0) + jnp.log(1.0 + jnp.exp(-jnp.abs(lam)))
    log_a = -LRU_C * r * sp
    a = jnp.exp(log_a)
    x2 = 2.0 * log_a
    one_minus_a2 = jnp.where(x2 > -0.01, -x2 * (1.0 + x2 * (0.5 + x2 * (1.0 / 6.0 + x2 / 24.0))), 1.0 - jnp.exp(x2))
    mult = jnp.sqrt(one_minus_a2)
    return taps, uc, ucb, r, gi, sp, a, mult


def _lru_specs(nchunk, reverse):
    def chunk(b, c):
        return b * nchunk + ((nchunk - 1 - c) if reverse else c)

    def halo_before(b, c):
        return jnp.maximum(chunk(b, c) * (LRU_CHUNK // 8) - 1, 0)

    return chunk, halo_before


def _lru_fwd(zug, cw, cb, wa, ba, wi, bi, lam, *, name):
    total = zug.shape[0]
    nchunk = SEQ // LRU_CHUNK
    w = LRU_WIDTH
    chunk, halo_before = _lru_specs(nchunk, False)

    def body(u_ref, uh_ref, g_ref, cw_ref, cb_ref, wa_ref, ba_ref, wi_ref, bi_ref, lam_ref, y_ref, h_ref, a_sc, x_sc, carry_sc):
        c = pl.program_id(1)
        u = u_ref[...]
        halo = jnp.where(c > 0, uh_ref[...], 0.0)
        _, uc, _, _, gi, _, a, mult = _lru_gates(u, halo, cw_ref[...], cb_ref[...], wa_ref[...], ba_ref[...],
                                                 wi_ref[...], bi_ref[...], lam_ref[...])
        a_sc[...] = a
        x_sc[...] = mult * (gi * uc)

        @pl.when(c == 0)
        def _():
            carry_sc[...] = jnp.zeros(carry_sc.shape, F32)

        def tile_step(t, h):
            r0 = pl.multiple_of(t * 8, 8)
            at = a_sc[pl.ds(r0, 8), :]
            xt = x_sc[pl.ds(r0, 8), :]
            rows = []
            for j in range(8):
                h = at[j:j + 1, :] * h + xt[j:j + 1, :]
                rows.append(h)
            h_ref[pl.ds(r0, 8), :] = jnp.concatenate(rows, axis=0)
            return h

        h_last = lax.fori_loop(0, LRU_CHUNK // 8, tile_step, carry_sc[0:1, :])
        carry_sc[0:1, :] = h_last
        gel, _ = _gelu(g_ref[...])
        y_ref[...] = (h_ref[...] * gel).astype(BF16)

    small = lambda arr: pl.BlockSpec(arr.shape, lambda b, c: (0, 0))
    return pl.pallas_call(
        body, grid=(total // SEQ, nchunk),
        in_specs=[pl.BlockSpec((LRU_CHUNK, w), lambda b, c: (chunk(b, c), 0)),
                  pl.BlockSpec((8, w), lambda b, c: (halo_before(b, c), 0)),
                  pl.BlockSpec((LRU_CHUNK, w), lambda b, c: (chunk(b, c), 1)),
                  small(cw), small(cb), small(wa), small(ba), small(wi), small(bi), small(lam)],
        out_specs=[pl.BlockSpec((LRU_CHUNK, w), lambda b, c: (chunk(b, c), 0))] * 2,
        out_shape=[jax.ShapeDtypeStruct((total, w), BF16), jax.ShapeDtypeStruct((total, w), F32)],
        scratch_shapes=[pltpu.VMEM((LRU_CHUNK, w), F32), pltpu.VMEM((LRU_CHUNK, w), F32), pltpu.VMEM((8, w), F32)],
        name=name, compiler_params=_params("arbitrary", "arbitrary"),
    )(zug, zug, zug, cw, cb, wa, ba, wi, bi, lam)


def _lru_bwd(zug, hl, dy, cw, cb, wa, ba, wi, bi, lam, *, name):
    total = zug.shape[0]
    nchunk = SEQ // LRU_CHUNK
    w = LRU_WIDTH
    chunk, halo_before = _lru_specs(nchunk, True)
    tn = (((0,), (0,)), ((), ()))
    nt = (((1,), (1,)), ((), ()))

    def body(u_ref, uh_ref, g_ref, hl_ref, hh_ref, dy_ref, cw_ref, cb_ref, wa_ref, ba_ref, wi_ref, bi_ref, lam_ref,
             dz_ref, dcw_ref, dcb_ref, dwa_ref, dba_ref, dwi_ref, dbi_ref, dlam_ref,
             a_sc, d_sc, g_sc, gcarry_sc, acarry_sc, duc_sc):
        b, c = pl.program_id(0), pl.program_id(1)
        first_chunk = c == nchunk - 1

        @pl.when((b == 0) & (c == 0))
        def _():
            for ref in (dcw_ref, dcb_ref, dwa_ref, dba_ref, dwi_ref, dbi_ref, dlam_ref):
                ref[...] = jnp.zeros(ref.shape, F32)

        @pl.when(c == 0)
        def _():
            gcarry_sc[...] = jnp.zeros(gcarry_sc.shape, F32)
            acarry_sc[...] = jnp.zeros(acarry_sc.shape, F32)
            duc_sc[...] = jnp.zeros(duc_sc.shape, F32)

        u = u_ref[...]
        halo = jnp.where(first_chunk, 0.0, uh_ref[...])
        cw, wa, wi, lam = cw_ref[...], wa_ref[...], wi_ref[...], lam_ref[...]
        taps, uc, ucb, r, gi, sp, a, mult = _lru_gates(u, halo, cw, cb_ref[...], wa, ba_ref[...], wi, bi_ref[...], lam)
        gate = g_ref[...]
        gel, th = _gelu(gate)
        dy = dy_ref[...]
        hl = hl_ref[...]
        a_sc[...] = a
        d_sc[...] = dy * gel
        dgate = dy * hl * _gelu_grad(gate, th)

        def tile_step(t, carry):
            g_next, a_next = carry
            r0 = pl.multiple_of((LRU_CHUNK // 8 - 1 - t) * 8, 8)
            dt = d_sc[pl.ds(r0, 8), :]
            at = a_sc[pl.ds(r0, 8), :]
            rows = [None] * 8
            for j in reversed(range(8)):
                g_next = dt[j:j + 1, :] + a_next * g_next
                a_next = at[j:j + 1, :]
                rows[j] = g_next
            g_sc[pl.ds(r0, 8), :] = jnp.concatenate(rows, axis=0)
            return g_next, a_next

        g_last, a_last = lax.fori_loop(0, LRU_CHUNK // 8, tile_step, (gcarry_sc[0:1, :], acarry_sc[0:1, :]))
        gcarry_sc[0:1, :] = g_last
        acarry_sc[0:1, :] = a_last

        gs = g_sc[...]
        hl_halo = jnp.where(first_chunk, 0.0, hh_ref[...])
        da = gs * _shift_down(hl, hl_halo, 1)
        dmult = gs * gi * uc
        dgi = gs * mult * uc
        duc = gs * mult * gi
        dlog_a = da * a - dmult * a * a / mult
        dr = dlog_a * (-LRU_C * sp)
        dsp = jnp.sum(dlog_a * (-LRU_C * r), axis=0, keepdims=True)
        dlam_ref[...] += -dsp * jax.nn.sigmoid(-lam)
        dpa = dr * r * (1.0 - r)
        dpi = dgi * gi * (1.0 - gi)
        dpab, dpib = dpa.astype(BF16), dpi.astype(BF16)
        dba_ref[...] += jnp.sum(dpa, axis=0, keepdims=True)
        dbi_ref[...] += jnp.sum(dpi, axis=0, keepdims=True)
        dwa_ref[...] += lax.dot_general(ucb, dpab, tn, preferred_element_type=F32)
        dwi_ref[...] += lax.dot_general(ucb, dpib, tn, preferred_element_type=F32)
        duc = duc + lax.dot_general(dpab, wa, nt, preferred_element_type=F32)
        duc = duc + lax.dot_general(dpib, wi, nt, preferred_element_type=F32)
        dcb_ref[...] += jnp.sum(duc, axis=0, keepdims=True)
        dcw_ref[...] += jnp.concatenate([jnp.sum(duc * taps[k], axis=0, keepdims=True) for k in range(CONV_WIDTH)], axis=0)
        after = duc_sc[...]
        du = cw[CONV_WIDTH - 1:CONV_WIDTH, :] * duc
        for k in range(CONV_WIDTH - 1):
            du = du + cw[k:k + 1, :] * _shift_up(duc, after, CONV_WIDTH - 1 - k)
        duc_sc[...] = duc[0:8, :]
        dz_ref[:, 0:w] = du.astype(BF16)
        dz_ref[:, w:2 * w] = dgate.astype(BF16)

    small = lambda arr: pl.BlockSpec(arr.shape, lambda b, c: (0, 0))
    acc = lambda shape: pl.BlockSpec(shape, lambda b, c: (0, 0))
    chunk_spec = lambda cb_: pl.BlockSpec((LRU_CHUNK, w), lambda b, c: (chunk(b, c), cb_))
    halo_spec = pl.BlockSpec((8, w), lambda b, c: (halo_before(b, c), 0))
    acc_shapes = [(CONV_WIDTH, w), (1, w), (w, w), (1, w), (w, w), (1, w), (1, w)]
    return pl.pallas_call(
        body, grid=(total // SEQ, nchunk),
        in_specs=[chunk_spec(0), halo_spec, chunk_spec(1), chunk_spec(0), halo_spec, chunk_spec(0),
                  small(cw), small(cb), small(wa), small(ba), small(wi), small(bi), small(lam)],
        out_specs=[pl.BlockSpec((LRU_CHUNK, 2 * w), lambda b, c: (chunk(b, c), 0))] + [acc(s) for s in acc_shapes],
        out_shape=[jax.ShapeDtypeStruct((total, 2 * w), BF16)] + [jax.ShapeDtypeStruct(s, F32) for s in acc_shapes],
        scratch_shapes=[pltpu.VMEM((LRU_CHUNK, w), F32)] * 3 + [pltpu.VMEM((8, w), F32)] * 3,
        name=name, compiler_params=_params("arbitrary", "arbitrary"),
    )(zug, zug, zug, hl, hl, dy, cw, cb, wa, ba, wi, bi, lam)


def _block_diag(wb):
    eye = jnp.eye(LRU_BLOCKS, dtype=wb.dtype)
    return jnp.einsum("gcd,gh->gchd", wb, eye).reshape(LRU_WIDTH, LRU_WIDTH).astype(BF16)


def _diag_blocks(wd):
    blk = LRU_WIDTH // LRU_BLOCKS
    return jnp.stack([wd[g * blk:(g + 1) * blk, g * blk:(g + 1) * blk] for g in range(LRU_BLOCKS)])


FOX_SCAN = 256


def _fox_bias(fl, bf, *, name):
    nb = fl.shape[0]

    def body(fl_ref, bf_ref, c_ref):
        x = fl_ref[0] + bf_ref[...]
        logf = jnp.minimum(x, 0.0) - jnp.log(1.0 + jnp.exp(-jnp.abs(x)))
        upper = (lax.broadcasted_iota(jnp.int32, (FOX_SCAN, FOX_SCAN), 0)
                 <= lax.broadcasted_iota(jnp.int32, (FOX_SCAN, FOX_SCAN), 1)).astype(BF16)
        carry = jnp.zeros((LRU_BLOCKS, 1), F32)
        for j in range(SEQ // FOX_SCAN):
            blk = logf[:, j * FOX_SCAN:(j + 1) * FOX_SCAN]
            c_ref[0, :, j * FOX_SCAN:(j + 1) * FOX_SCAN] = _dot_exact(blk, upper) + carry
            carry = carry + jnp.sum(blk, axis=1, keepdims=True)

    return pl.pallas_call(
        body, grid=(nb,),
        in_specs=[pl.BlockSpec((1, 8, SEQ), lambda b: (b, 0, 0)), pl.BlockSpec((8, 1), lambda b: (0, 0))],
        out_specs=pl.BlockSpec((1, 8, SEQ), lambda b: (b, 0, 0)),
        out_shape=jax.ShapeDtypeStruct((nb, 8, SEQ), F32), name=name, compiler_params=_params("arbitrary"),
    )(fl, bf)


def _fox_bias_bwd(fl, bf, dc, *, name):
    nb = fl.shape[0]

    def body(fl_ref, bf_ref, dc_ref, dfl_ref, dbf_ref):
        @pl.when(pl.program_id(0) == 0)
        def _():
            dbf_ref[...] = jnp.zeros(dbf_ref.shape, F32)

        x = fl_ref[0] + bf_ref[...]
        dc_all = dc_ref[0]
        lower = (lax.broadcasted_iota(jnp.int32, (FOX_SCAN, FOX_SCAN), 0)
                 >= lax.broadcasted_iota(jnp.int32, (FOX_SCAN, FOX_SCAN), 1)).astype(BF16)
        carry = jnp.zeros((LRU_BLOCKS, 1), F32)
        total = jnp.zeros((LRU_BLOCKS, 1), F32)
        for j in reversed(range(SEQ // FOX_SCAN)):
            cols = slice(j * FOX_SCAN, (j + 1) * FOX_SCAN)
            blk = dc_all[:, cols]
            dlogf = _dot_exact(blk, lower) + carry
            carry = carry + jnp.sum(blk, axis=1, keepdims=True)
            dx = dlogf * jax.nn.sigmoid(-x[:, cols])
            dfl_ref[0, :, cols] = dx
            total = total + jnp.sum(dx, axis=1, keepdims=True)
        dbf_ref[...] += total

    return pl.pallas_call(
        body, grid=(nb,),
        in_specs=[pl.BlockSpec((1, 8, SEQ), lambda b: (b, 0, 0)), pl.BlockSpec((8, 1), lambda b: (0, 0)),
                  pl.BlockSpec((1, 8, SEQ), lambda b: (b, 0, 0))],
        out_specs=[pl.BlockSpec((1, 8, SEQ), lambda b: (b, 0, 0)), pl.BlockSpec((8, 1), lambda b: (0, 0))],
        out_shape=[jax.ShapeDtypeStruct((nb, 8, SEQ), F32), jax.ShapeDtypeStruct((8, 1), F32)],
        name=name, compiler_params=_params("arbitrary"),
    )(fl, bf, dc)


def _seq3(a, nb):
    return a.reshape(nb, a.shape[0] // nb, a.shape[1])


def _flat2(a):
    return a.reshape(a.shape[0] * a.shape[1], a.shape[2])


def _residual_out(y, w, h, next_gain, *, name):
    if next_gain is None:
        out, = _matmul(y, w, extras=(h,), epilogue=lambda acc, res: (acc + res,), name=name)
        return out, None

    def epilogue(acc, res, g):
        out = acc + res
        return out, out * _rstd(out) * g
    return _matmul(y, w, outs=(F32, BF16), extras=(h,), consts=(next_gain,), epilogue=epilogue, whole_rows=True, name=name)


def _mlp_fwd(h, hn, p, tag, next_gain):
    act, = _matmul(hn, p["w_up_t"], tb=True, outs=(BF16,), epilogue=lambda acc: (jnp.square(jnp.maximum(acc, 0.0)),),
                   name=f"{tag}_up")
    out, hn_next = _residual_out(act, p["w_down"], h, next_gain, name=f"{tag}_down")
    return out, hn_next, (h, hn, act)


def _mlp_bwd(dh, dhb, p, saved, tag):
    h, hn, act = saved
    dup, = _matmul(dhb, p["w_down"], tb=True, outs=(BF16,), extras=(act,),
                   epilogue=lambda acc, act: (acc * (2.0 * jnp.sqrt(act.astype(F32))),), name=f"{tag}_bwd_dup")
    dw_down, = _matmul(act, dhb, ta=True, outs=(BF16,),name=f"{tag}_bwd_dwdown")
    dw_up_t, = _matmul(dup, hn, ta=True, outs=(BF16,),name=f"{tag}_bwd_dwup")
    dh2, dh2b, dg = _norm_input_grad(dup, p["w_up_t"], h, dh, p["norm"], name=f"{tag}_bwd_dh")
    return dh2, dh2b, {"norm": dg, "w_up_t": dw_up_t, "w_down": dw_down}


def _xa_fwd(h, hn, mem, p, tag, nb, next_gain):
    mem_n = _rms_fwd(mem, p["mem_norm"], name=f"{tag}_memnorm")
    q, = _matmul(hn, p["w_q"], outs=(BF16,), name=f"{tag}_q")
    kv, = _matmul(mem_n, p["w_kv_t"], tb=True, outs=(BF16,), name=f"{tag}_kv")
    q3, kv3 = _seq3(q, nb), _seq3(kv, nb)
    o, lse, ob = _attn_fwd((q3, 0), (kv3, 0), (kv3, 1), heads=XA_HEADS, dh=XA_HEAD_DIM, mode="full", bf16_copy=True,
                           name=f"{tag}_attn")
    o, ob = _flat2(o), _flat2(ob)
    out, hn_next = _residual_out(ob, p["w_o"], h, next_gain, name=f"{tag}_o")
    return out, hn_next, (h, hn, mem_n, q3, kv3, o, ob, lse)


def _xa_bwd(dh, dhb, mem, p, saved, tag, nb):
    h, hn, mem_n, q3, kv3, o, ob, lse = saved
    dob, delta = _matmul(dhb, p["w_o"], tb=True, outs=(BF16, (F32, LANES)), extras=(o,), consts=(_head_expand(XA_HEADS, D_MODEL).T,),
                         epilogue=lambda acc, o, e: (acc, _dot_exact(acc * o, e)), name=f"{tag}_bwd_do")
    dw_o, = _matmul(ob, dhb, ta=True, outs=(BF16,),name=f"{tag}_bwd_dwo")
    dq, dk, dv = _attn_bwd((q3, 0), (kv3, 0), (kv3, 1), (_seq3(dob, nb), 0), lse, _seq3(delta, nb), heads=XA_HEADS,
                           dh=XA_HEAD_DIM, mode="full", dq_dtype=BF16, name=f"{tag}_bwd_attn")
    dqb = _flat2(dq)
    dkvb, = _rowwise(lambda a, b: jnp.concatenate([a, b], axis=1), [_flat2(dk), _flat2(dv)], [], [(2 * D_MODEL, BF16)],
                     name=f"{tag}_bwd_dkvcast")
    dw_q, = _matmul(hn, dqb, ta=True, outs=(BF16,),name=f"{tag}_bwd_dwq")
    dw_kv_t, = _matmul(dkvb, mem_n, ta=True, outs=(BF16,),name=f"{tag}_bwd_dwkv")
    dmem_n, = _matmul(dkvb, p["w_kv_t"], name=f"{tag}_bwd_dmemn")
    dg_mem, = _rowwise(lambda x, d, g: _rms_bwd_vals(x, d, g)[1], [mem, dmem_n], [p["mem_norm"]], [], [(1, D_MODEL)],
                       name=f"{tag}_bwd_memnorm")
    dh2, dh2b, dg = _norm_input_grad(dqb, p["w_q"], h, dh, p["norm"], tb=True, name=f"{tag}_bwd_dh")
    return dh2, dh2b, {"norm": dg, "mem_norm": dg_mem, "w_q": dw_q, "w_kv_t": dw_kv_t, "w_o": dw_o}


AB_MAIN = 2 * LRU_WIDTH + 3 * ATT_W


def _ab_fwd(h, hn, p, nb, next_gain):
    w_in_t = p["w_in_t"]
    zug, = _matmul(hn, w_in_t[:2 * LRU_WIDTH], tb=True, name="ab_in_ug")
    qkv, = _matmul(hn, w_in_t[2 * LRU_WIDTH:AB_MAIN], tb=True, outs=(BF16,), tn=768, name="ab_in_qkv")
    zf, = _matmul(hn, w_in_t[AB_MAIN:], tb=True, name="ab_in_f")
    fl = zf[:, :8].reshape(nb, SEQ, 8).transpose(0, 2, 1)
    cbias = _fox_bias(fl, p["b_f"], name="ab_fox_bias")
    kb = cbias.reshape(nb, 8, SEQ // ATT_CHUNK, ATT_CHUNK)
    y_a, hl = _lru_fwd(zug, p["conv_w"], p["conv_b"], p["w_a"], p["b_a"], p["w_i"], p["b_i"], p["lam"], name="ab_lru")
    qkv3 = _seq3(qkv, nb)
    o, lse = _attn_fwd((qkv3, 0), (qkv3, 1), (qkv3, 2), kb, heads=8, dh=HEAD_DIM, mode="causal", name="ab_fox")
    o = _flat2(o)
    y, = _rowwise(lambda a, b: jnp.concatenate([a, b.astype(BF16)], axis=1), [y_a, o], [], [(D_MODEL, BF16)], name="ab_ycat")
    out, hn_next = _residual_out(y, p["w_out"], h, next_gain, name="ab_out")
    return out, hn_next, (h, hn, zug, qkv3, fl, kb, hl, o, lse, y)


def _ab_bwd(dh, dhb, p, saved, nb):
    h, hn, zug, qkv3, fl, kb, hl, o, lse, y = saved
    dy, dyb, delta = _matmul(dhb, p["w_out"], tb=True, outs=(F32, BF16, (F32, LANES)), extras=(y,), consts=(_head_expand(8, ATT_W).T,),
                             epilogue=lambda acc, y, e: (acc, acc, _dot_exact(acc[:, ATT_W:] * y[:, ATT_W:].astype(F32), e)),
                             name="ab_bwd_dy")
    dw_out, = _matmul(y, dhb, ta=True, outs=(BF16,),name="ab_bwd_dwout")
    dzug, dcw, dcb, dwa, dba, dwi, dbi, dlam = _lru_bwd(zug, hl, dy, p["conv_w"], p["conv_b"], p["w_a"], p["b_a"],
                                                        p["w_i"], p["b_i"], p["lam"], name="ab_bwd_lru")
    dq, dk, dv, dkb, drow = _attn_bwd((qkv3, 0), (qkv3, 1), (qkv3, 2), (_seq3(dyb, nb), 1), lse, _seq3(delta, nb), kb,
                                      heads=8, dh=HEAD_DIM, mode="causal", name="ab_bwd_fox")
    dcum = dkb.reshape(nb, 8, SEQ) + drow[:, :, :8].transpose(0, 2, 1)
    dfl, dbf = _fox_bias_bwd(fl, p["b_f"], dcum, name="ab_bwd_fox_bias")
    dzf = jnp.pad(dfl.transpose(0, 2, 1).reshape(nb * SEQ, 8), ((0, 0), (0, LANES - 8)))
    dz, = _rowwise(lambda a, q, k, v, f: jnp.concatenate([a, q.astype(BF16), k.astype(BF16), v.astype(BF16), f.astype(BF16)], axis=1),
                   [dzug, _flat2(dq), _flat2(dk), _flat2(dv), dzf], [], [(AB_MAIN + LANES, BF16)], name="ab_bwd_dzcat")
    dw_in_t, = _matmul(dz, hn, ta=True, outs=(BF16,),tm=384, name="ab_bwd_dwin")
    dh2, dh2b, dg = _norm_input_grad(dz, p["w_in_t"], h, dh, p["norm"], name="ab_bwd_dh")
    grads = {"norm": dg, "w_in_t": dw_in_t[:AB_MAIN + 8], "conv_w": dcw, "conv_b": dcb, "w_a": _diag_blocks(dwa), "b_a": dba,
             "w_i": _diag_blocks(dwi), "b_i": dbi, "lam": dlam, "b_f": dbf.reshape(1, 8), "w_out": dw_out}
    return dh2, dh2b, grads


CD_IN = 3 * ATT_W + ATT_W + 2 * SWA_KW


def _gqa_share():
    src = jnp.arange(SWA_KW)[:, None]
    dst = jnp.arange(ATT_W)[None, :]
    per_kv = ATT_W // (SWA_KW // HEAD_DIM)
    return ((dst // per_kv == src // HEAD_DIM) & (dst % HEAD_DIM == src % HEAD_DIM)).astype(BF16)


def _cd_fwd(h, hn, p, nb, next_gain):
    z, = _matmul(hn, p["w_in_t"], tb=True, tn=384, name="cd_in")
    cos, sin = _rope_tables()
    per_seq = SEQ // ROW_TILE
    table_map = lambda i: (i % per_seq, 0)

    def rope_fn(z, cos, sin, share):
        qc, kc, vc = z[:, 0:ATT_W], z[:, ATT_W:2 * ATT_W], z[:, 2 * ATT_W:3 * ATT_W]
        qd, kd, vd = z[:, 3 * ATT_W:4 * ATT_W], z[:, 4 * ATT_W:4 * ATT_W + SWA_KW], z[:, 4 * ATT_W + SWA_KW:]
        kd8 = jnp.dot(_rotate(kd, cos, sin).astype(BF16), share, preferred_element_type=F32)
        vd8 = jnp.dot(vd.astype(BF16), share, preferred_element_type=F32)
        qk = jnp.concatenate([_rotate(qc, cos, sin), _rotate(kc, cos, sin)], axis=1)
        return qk, vc, _rotate(qd, cos, sin), kd8, vd8, qk, qk, vc, vc
    dils = [dil for _, dil in DIL_PATTERN if dil > 1]
    outs = _rowwise(rope_fn, [z, cos, sin], [_gqa_share()],
                    [(2 * ATT_W, BF16)] + [(ATT_W, BF16)] * 4 + [(2 * ATT_W, BF16, d) for d in dils] + [(ATT_W, BF16, d) for d in dils],
                    name="cd_rope", row_maps=[None, table_map, table_map])
    qk, vc, qd, kd, vd = outs[:5]
    views = {1: (_seq3(qk, nb), _seq3(vc, nb))}
    for d, qk_d, vc_d in zip(dils, outs[5:5 + len(dils)], outs[5 + len(dils):]):
        views[d] = (_seq3(qk_d, nb), _seq3(vc_d, nb))
    in_classes = lambda a, width, d: _flat2(a) if d == 1 else ("classes", _flat2(a), width, d)
    branch = []
    for window, dil in DIL_PATTERN:
        qk_v, vc_v = views[dil]
        o_i, lse_i = _attn_fwd((qk_v, 0), (qk_v, 1), (vc_v, 0), classes=dil, heads=8, dh=HEAD_DIM, mode="band",
                               max_dist=window // dil, name=f"cd_dil{dil}")
        branch.append((in_classes(o_i, ATT_W, dil), in_classes(lse_i, LANES, dil)))
    expand = _head_expand(8, ATT_W)

    def combine(o1, o2, o3, l1, l2, l3, e):
        m = jnp.maximum(jnp.maximum(l1, l2), l3)
        lt = m + jnp.log(jnp.exp(l1 - m) + jnp.exp(l2 - m) + jnp.exp(l3 - m))
        o = sum(_dot_exact(jnp.exp(l - lt), e) * o_ for o_, l in ((o1, l1), (o2, l2), (o3, l3)))
        return o, lt
    y_c, lse_c = _rowwise(combine, [b[0] for b in branch] + [b[1] for b in branch], [expand], [(ATT_W, F32), (LANES, F32)],
                          name="cd_dil_combine")
    qd3, kd3, vd3 = _seq3(qd, nb), _seq3(kd, nb), _seq3(vd, nb)
    o_d, lse_band = _attn_fwd((qd3, 0), (kd3, 0), (vd3, 0), heads=8, dh=HEAD_DIM, mode="band", max_dist=SWA_WINDOW - 1,
                              name="cd_swa")

    def sink_combine(o, l, e, sink):
        lt = jnp.maximum(l, sink) + jnp.log(1.0 + jnp.exp(-jnp.abs(l - sink)))
        return o * _dot_exact(jnp.exp(l - lt), e), lt
    y_d, lse_d = _rowwise(sink_combine, [_flat2(o_d), _flat2(lse_band)], [expand, p["sink"]], [(ATT_W, F32), (LANES, F32)],
                          name="cd_swa_sink")
    y, = _rowwise(lambda a, b: jnp.concatenate([a, b], axis=1), [y_c, y_d], [], [(D_MODEL, BF16)], name="cd_ycat")
    out, hn_next = _residual_out(y, p["w_out"], h, next_gain, name="cd_out")
    return out, hn_next, (h, hn, views, qd3, kd3, vd3, y_c, lse_c, y_d, lse_d, y)


def _cd_bwd(dh, dhb, p, saved, nb):
    h, hn, views, qd3, kd3, vd3, y_c, lse_c, y_d, lse_d, y = saved
    dy, dyb = _matmul(dhb, p["w_out"], tb=True, outs=(F32, BF16), epilogue=lambda acc: (acc, acc), name="cd_bwd_dy")
    dw_out, = _matmul(y, dhb, ta=True, outs=(BF16,),name="cd_bwd_dwout")
    dyb3 = _seq3(dyb, nb)
    dils = [dil for _, dil in DIL_PATTERN if dil > 1]
    gather = _head_expand(8, ATT_W).T

    def prepare(do, o, lse, e):
        delta = _dot_exact(do * o, e)
        return (delta,) + (do,) * len(dils) + (lse,) * len(dils) + (delta,) * len(dils)
    outs = _rowwise(prepare, [(dy, ATT_W, 0), y_c, lse_c], [gather],
                    [(LANES, F32)] + [(ATT_W, BF16, d) for d in dils] + [(LANES, F32, d) for d in dils] * 2, name="cd_bwd_delta_dil")
    seen = {1: ((dyb3, 0), _seq3(lse_c, nb), _seq3(outs[0], nb))}
    for j, d in enumerate(dils):
        seen[d] = ((_seq3(outs[1 + j], nb), 0), _seq3(outs[1 + len(dils) + j], nb), _seq3(outs[1 + 2 * len(dils) + j], nb))
    in_classes = lambda a, d: _flat2(a) if d == 1 else ("classes", _flat2(a), ATT_W, d)
    parts = []
    for window, dil in DIL_PATTERN:
        (qk_v, vc_v), (do_v, lse_v, delta_v) = views[dil], seen[dil]
        grads = _attn_bwd((qk_v, 0), (qk_v, 1), (vc_v, 0), do_v, lse_v, delta_v, classes=dil, heads=8, dh=HEAD_DIM, mode="band",
                          max_dist=window // dil, dq_dtype=BF16, name=f"cd_bwd_dil{dil}")
        parts.append([in_classes(a, dil) for a in grads])
    delta_d, dsink = _head_delta((dy, ATT_W, 1), y_d, 8, lse=lse_d, sink=p["sink"], name="cd_bwd_delta_swa")
    dqd, dkd, dvd = _attn_bwd((qd3, 0), (kd3, 0), (vd3, 0), (dyb3, 1), _seq3(lse_d, nb), _seq3(delta_d, nb), heads=8,
                              dh=HEAD_DIM, mode="band", max_dist=SWA_WINDOW - 1, dq_dtype=BF16, name="cd_bwd_swa")
    cos, sin = _rope_tables()
    per_seq = SEQ // ROW_TILE
    table_map = lambda i: (i % per_seq, 0)

    def unrope(q1, q2, q3, k1, k2, k3, v1, v2, v3, qd, kd8, vd8, cos, sin, gather):
        back = lambda x: _rotate(x.astype(F32), cos, -sin)
        kd, vd = _dot_exact(kd8, gather), _dot_exact(vd8, gather)
        return jnp.concatenate([back(q1 + q2 + q3), back(k1 + k2 + k3), v1 + v2 + v3, back(qd), back(kd), vd], axis=1)
    ins = [parts[b][t] for t in range(3) for b in range(3)] + [_flat2(dqd), _flat2(dkd), _flat2(dvd), cos, sin]
    dz, = _rowwise(unrope, ins, [_gqa_share().T], [(CD_IN, BF16)], name="cd_bwd_unrope",
                   row_maps=[None] * 12 + [table_map, table_map])
    dw_in_t, = _matmul(dz, hn, ta=True, outs=(BF16,),tm=384, name="cd_bwd_dwin")
    dh2, dh2b, dg = _norm_input_grad(dz, p["w_in_t"], h, dh, p["norm"], name="cd_bwd_dh")
    return dh2, dh2b, {"norm": dg, "w_in_t": dw_in_t, "sink": dsink[:, :8], "w_out": dw_out}


def _local_step(x, mem, target, w, complete=None, grads_ready=None):
    nb = x.shape[0]
    h = x.reshape(nb * SEQ, D_MODEL)
    mem2 = mem.reshape(nb * MEM_LEN, D_MODEL)
    tgt = target.reshape(nb * SEQ, D_MODEL)

    hn = _rms_fwd(h, w["ab"]["norm"], name="ab_norm")
    h, hn, s_ab = _ab_fwd(h, hn, w["ab"], nb, w["xa0"]["norm"])
    if complete is not None:
        complete(h)
    h, hn, s_xa0 = _xa_fwd(h, hn, mem2, w["xa0"], "xa0", nb, w["mlp0"]["norm"])
    h, hn, s_mlp0 = _mlp_fwd(h, hn, w["mlp0"], "mlp0", w["cd"]["norm"])
    h, hn, s_cd = _cd_fwd(h, hn, w["cd"], nb, w["xa1"]["norm"])
    h, hn, s_xa1 = _xa_fwd(h, hn, mem2, w["xa1"], "xa1", nb, w["mlp1"]["norm"])
    h, _, s_mlp1 = _mlp_fwd(h, hn, w["mlp1"], "mlp1", None)

    dh, dhb, loss, dg_final = _loss_and_grad(h, tgt, w["final_norm"], name="loss")
    grads = {"final_norm": dg_final}
    dh, dhb, grads["mlp1"] = _mlp_bwd(dh, dhb, w["mlp1"], s_mlp1, "mlp1")
    dh, dhb, grads["xa1"] = _xa_bwd(dh, dhb, mem2, w["xa1"], s_xa1, "xa1", nb)
    dh, dhb, grads["cd"] = _cd_bwd(dh, dhb, w["cd"], s_cd, nb)
    if grads_ready is not None:
        grads_ready(0, grads)
    dh, dhb, grads["mlp0"] = _mlp_bwd(dh, dhb, w["mlp0"], s_mlp0, "mlp0")
    dh, dhb, grads["xa0"] = _xa_bwd(dh, dhb, mem2, w["xa0"], s_xa0, "xa0", nb)
    if grads_ready is not None:
        grads_ready(1, grads)
    dh, dhb, grads["ab"] = _ab_bwd(dh, dhb, w["ab"], s_ab, nb)
    return loss, dh.reshape(nb, SEQ, D_MODEL), grads


ANY = pl.BlockSpec(memory_space=pl.ANY)


def _place():
    x, y, c = lax.axis_index("x"), lax.axis_index("y"), lax.axis_index("c")
    return x, y, c, [(1 - x, y), (x, 1 - y), (1 - x, 1 - y)]


def _gather_chips(shard):
    half = shard.shape[0] // 2

    def body(src, out, send_sems, recv_sems):
        x, y, c, chips = _place()
        sibling = (x, y, 1 - c)

        def rows(slot, core):
            return out.at[slot, pl.ds(core * half, half)]

        def copy(k, src_ref, dst_ref, to):
            return pltpu.make_async_remote_copy(src_ref=src_ref, dst_ref=dst_ref, send_sem=send_sems.at[k],
                                                recv_sem=recv_sems.at[k], device_id=to, device_id_type=MESH)

        first = [copy(k, src.at[pl.ds(c * half, half)], rows(2 * x + y, c), (px, py, c)) for k, (px, py) in enumerate(chips)]
        for cp in first:
            cp.start()
        passed = [copy(3 + k, rows(2 * px + py, c), rows(2 * px + py, c), sibling) for k, (px, py) in enumerate(chips)]
        for k, (px, py) in enumerate(chips):
            copy(k, src.at[pl.ds(c * half, half)], rows(2 * px + py, c), (px, py, c)).wait_recv()
            passed[k].start()
        for k, (px, py) in enumerate(chips):
            copy(3 + k, rows(2 * px + py, 1 - c), rows(2 * px + py, 1 - c), sibling).wait_recv()
        for cp in first + passed:
            cp.wait_send()

    return pl.pallas_call(
        body, out_shape=jax.ShapeDtypeStruct((N_CHIPS,) + shard.shape, shard.dtype), in_specs=[ANY], out_specs=ANY,
        scratch_shapes=[pltpu.SemaphoreType.DMA((6,)), pltpu.SemaphoreType.DMA((6,))], name="gather_chips",
    )(shard)


HBM = pl.BlockSpec(memory_space=pltpu.HBM)
SEM = pl.BlockSpec(memory_space=pltpu.SEMAPHORE)
SIDE_EFFECT = pltpu.SideEffectType.DATAFLOW_SIDE_EFFECTING


def _chip_copies(src, land, send_sems, recv_sems):
    half = src.shape[0] // 2
    x, y, c, chips = _place()

    def copy(k, slot, to):
        return pltpu.make_async_remote_copy(src_ref=src.at[pl.ds(c * half, half)], dst_ref=land.at[slot, pl.ds(c * half, half)],
                                            send_sem=send_sems[k], recv_sem=recv_sems[k], device_id=to, device_id_type=MESH)
    out = [copy(k, 2 * x + y, (px, py, c)) for k, (px, py) in enumerate(chips)]
    back = [copy(k, 2 * px + py, (px, py, c)) for k, (px, py) in enumerate(chips)]
    return out, back


def _gather_start(shard, after):
    def body(src, land, *rest):
        rest = rest[len(after):]
        sems, token = rest[:6], rest[8]
        out, _ = _chip_copies(src, land, sems[:3], sems[3:])
        for cp in out:
            cp.start()
        token[...] = jnp.zeros(token.shape, F32)

    sem = pltpu.SemaphoreType.DMA(())
    land_shape = (N_CHIPS,) + shard.shape
    return pl.pallas_call(
        body, name="gather_start",
        out_shape=(sem,) * 6 + (pltpu.HBM(shard.shape, shard.dtype), pltpu.HBM(land_shape, shard.dtype),
                                jax.ShapeDtypeStruct((8, LANES), F32)),
        in_specs=(HBM, HBM) + (pl.BlockSpec(memory_space=pl.ANY),) * len(after),
        out_specs=(SEM,) * 6 + (HBM, HBM, pl.BlockSpec(memory_space=pltpu.VMEM)),
        input_output_aliases={0: 6, 1: 7}, compiler_params=pltpu.CompilerParams(has_side_effects=SIDE_EFFECT),
    )(pltpu.with_memory_space_constraint(shard, pltpu.HBM),
      pltpu.with_memory_space_constraint(lax.empty(land_shape, shard.dtype), pltpu.HBM), *after)


def _gather_wait(started, after):
    sems, src_thru, land_thru = started[:6], started[6], started[7]

    def body(src, land, *rest):
        out, back = _chip_copies(src, land, rest[:3], rest[3:6])
        for cp in out:
            cp.wait_send()
        for cp in back:
            cp.wait_recv()

    return pl.pallas_call(
        body, name="gather_wait",
        out_shape=(pltpu.HBM(src_thru.shape, src_thru.dtype), pltpu.HBM(land_thru.shape, land_thru.dtype)),
        in_specs=(HBM, HBM) + (SEM,) * 6 + (pl.BlockSpec(memory_space=pl.ANY),), out_specs=(HBM, HBM),
        input_output_aliases={0: 0, 1: 1}, compiler_params=pltpu.CompilerParams(has_side_effects=SIDE_EFFECT),
    )(src_thru, land_thru, *sems, after)[1]


def _gather_pass(land):
    half = land.shape[1] // 2

    def body(land_in, land_out, send_sems, recv_sems):
        x, y, c, chips = _place()

        def copy(k, slot, core):
            rows = land_out.at[slot, pl.ds(core * half, half)]
            return pltpu.make_async_remote_copy(src_ref=rows, dst_ref=rows, send_sem=send_sems.at[k], recv_sem=recv_sems.at[k],
                                                device_id=(x, y, 1 - c), device_id_type=MESH)
        sends = [copy(k, 2 * px + py, c) for k, (px, py) in enumerate(chips)]
        for cp in sends:
            cp.start()
        for k, (px, py) in enumerate(chips):
            copy(k, 2 * px + py, 1 - c).wait_recv()
        for cp in sends:
            cp.wait_send()

    return pl.pallas_call(
        body, out_shape=jax.ShapeDtypeStruct(land.shape, land.dtype), in_specs=[ANY], out_specs=ANY,
        scratch_shapes=[pltpu.SemaphoreType.DMA((3,)), pltpu.SemaphoreType.DMA((3,))], input_output_aliases={0: 0},
        name="gather_pass",
    )(land)


def _swap_cores(block, *, name, half_rows=None):
    shape = block.shape if half_rows is None else (block.shape[0], half_rows, block.shape[2])

    def body(src, out, send_sem, recv_sem):
        x, y, c, _ = _place()
        part = src if half_rows is None else src.at[:, pl.ds((1 - c) * half_rows, half_rows)]
        cp = pltpu.make_async_remote_copy(src_ref=part, dst_ref=out, send_sem=send_sem, recv_sem=recv_sem,
                                          device_id=(x, y, 1 - c), device_id_type=MESH)
        cp.start()
        cp.wait()

    return pl.pallas_call(
        body, out_shape=jax.ShapeDtypeStruct(shape, block.dtype), in_specs=[ANY], out_specs=ANY,
        scratch_shapes=[pltpu.SemaphoreType.DMA(()), pltpu.SemaphoreType.DMA(())], name=name,
    )(block)


def _scatter_copies(parts, land, send_sems, recv_sems):
    x, y, c, chips = _place()
    return [pltpu.make_async_remote_copy(src_ref=parts.at[2 * px + py], dst_ref=land.at[k], send_sem=send_sems[k],
                                         recv_sem=recv_sems[k], device_id=(px, py, c), device_id_type=MESH)
            for k, (px, py) in enumerate(chips)]


def _scatter_start(parts, tag):
    def body(src, land, *rest):
        for cp in _scatter_copies(src, land, rest[:3], rest[3:6]):
            cp.start()
        rest[8][...] = jnp.zeros(rest[8].shape, F32)

    sem = pltpu.SemaphoreType.DMA(())
    land_shape = (3,) + parts.shape[1:]
    return pl.pallas_call(
        body, name=f"scatter_start_{tag}",
        out_shape=(sem,) * 6 + (pltpu.HBM(parts.shape, parts.dtype), pltpu.HBM(land_shape, parts.dtype),
                                jax.ShapeDtypeStruct((8, LANES), F32)),
        in_specs=(HBM, HBM), out_specs=(SEM,) * 6 + (HBM, HBM, pl.BlockSpec(memory_space=pltpu.VMEM)),
        input_output_aliases={0: 6, 1: 7}, compiler_params=pltpu.CompilerParams(has_side_effects=SIDE_EFFECT),
    )(pltpu.with_memory_space_constraint(parts, pltpu.HBM),
      pltpu.with_memory_space_constraint(lax.empty(land_shape, parts.dtype), pltpu.HBM))


def _scatter_wait(started, after, tag):
    def body(src, land, *rest):
        for cp in _scatter_copies(src, land, rest[:3], rest[3:6]):
            cp.wait_send()
            cp.wait_recv()

    src_thru, land_thru = started[6], started[7]
    return pl.pallas_call(
        body, name=f"scatter_wait_{tag}",
        out_shape=(pltpu.HBM(src_thru.shape, src_thru.dtype), pltpu.HBM(land_thru.shape, land_thru.dtype)),
        in_specs=(HBM, HBM) + (SEM,) * 6 + (pl.BlockSpec(memory_space=pl.ANY),), out_specs=(HBM, HBM),
        input_output_aliases={0: 0, 1: 1}, compiler_params=pltpu.CompilerParams(has_side_effects=SIDE_EFFECT),
    )(src_thru, land_thru, *started[:6], after)[1]


def _sum_all_devices(vec):
    rows = vec.shape[0]

    def body(x_ref, total_ref, all_ref, send_sems, recv_sems, local_sem):
        x, y, c, chips = _place()
        me, sibling = (x, y, c), (x, y, 1 - c)

        def slot(px, py, pc):
            return all_ref.at[4 * px + 2 * py + pc]

        def copy(k, block, to, src=None):
            return pltpu.make_async_remote_copy(src_ref=slot(*block) if src is None else src, dst_ref=slot(*block),
                                                send_sem=send_sems.at[k], recv_sem=recv_sems.at[k], device_id=to,
                                                device_id_type=MESH)

        mine = pltpu.make_async_copy(x_ref, slot(*me), local_sem)
        mine.start()
        first = [copy(0, me, sibling, src=x_ref)] + [copy(1 + j, me, (*chip, c), src=x_ref) for j, chip in enumerate(chips)]
        for cp in first:
            cp.start()
        passed = [copy(4 + j, (*chip, c), sibling) for j, chip in enumerate(chips)]
        for j, chip in enumerate(chips):
            copy(1 + j, (*chip, c), me).wait_recv()
            passed[j].start()
        copy(0, sibling, me).wait_recv()
        for j, chip in enumerate(chips):
            copy(4 + j, (*chip, 1 - c), me).wait_recv()
        for cp in first + passed:
            cp.wait_send()
        mine.wait()
        acc = all_ref[0]
        for d in range(1, 8):
            acc = acc + all_ref[d]
        total_ref[...] = acc

    vmem = pl.BlockSpec(memory_space=pltpu.VMEM)
    return pl.pallas_call(
        body, out_shape=[jax.ShapeDtypeStruct((rows, LANES), F32), jax.ShapeDtypeStruct((8, rows, LANES), F32)],
        in_specs=[vmem], out_specs=[vmem, vmem],
        scratch_shapes=[pltpu.SemaphoreType.DMA((7,)), pltpu.SemaphoreType.DMA((7,)), pltpu.SemaphoreType.DMA(())],
        name="sum_all_devices", compiler_params=pltpu.CompilerParams(vmem_limit_bytes=VMEM_LIMIT_BYTES),
    )(vec)[0]


PACK_COLS = 1024
BIG = (("mlp_w_up", 2, 1024, True), ("mlp_w_down", 2, 1024, False), ("xa_w_kv", 2, 512, True), ("xa_w_q", 2, 256, False),
       ("xa_w_o", 2, 256, False), ("ab_w_out", 1, 256, False), ("cd_w_out", 1, 256, False), ("cd_w_in", 1, 576, True),
       ("ab_w_in", 1, 642, True))
ENTRIES = tuple((name, layer, rows, transposed) for name, layers, rows, transposed in BIG for layer in range(layers))
FIRST_ENTRIES = tuple(e for e in ENTRIES if e[0].startswith("ab_"))
LATER_ENTRIES = tuple(e for e in ENTRIES if not e[0].startswith("ab_"))


def _tile_rows(entry):
    return -(-entry[2] // 16) * 16


def _padded_rows(entries):
    return -(-sum(_tile_rows(e) for e in entries) // 32) * 32


SMALL = (("ab_norm", (1, 1024)), ("ab_conv_w", (1, 4, 512)), ("ab_conv_b", (1, 512)), ("lru_w_a", (1, 8, 64, 64)),
         ("lru_b_a", (1, 512)), ("lru_w_i", (1, 8, 64, 64)), ("lru_b_i", (1, 512)), ("lru_lambda", (1, 512)),
         ("fox_b_f", (1, 8)), ("cd_norm", (1, 1024)), ("cd_sink", (1, 8)), ("xa_norm", (2, 1024)),
         ("xa_mem_norm", (2, 1024)), ("mlp_norm", (2, 1024)), ("final_norm", (1024,)), ("loss", (1,)))
SPLIT_SMALL = ("ab_conv_w", "cd_norm")


def _pack_rows(parts, entries, dtype):
    lead = parts[0].shape[:-2]
    zeros = lambda rows: jnp.zeros(lead + (rows, PACK_COLS), dtype)
    pieces = [jnp.pad(p.astype(dtype), ((0, 0),) * len(lead) + ((0, _tile_rows(e) - e[2]), (0, 0))) for p, e in zip(parts, entries)]
    tail = _padded_rows(entries) - sum(_tile_rows(e) for e in entries)
    return jnp.concatenate(pieces + ([zeros(tail)] if tail else []), axis=len(lead))


def _unpack_rows(packed, entries):
    out, r0 = [], 0
    for e in entries:
        out.append(packed[..., r0:r0 + e[2], :])
        r0 += _tile_rows(e)
    return out


def _shard_rows(w, entries):
    return [(w[name][layer].T if transposed else w[name][layer]) for name, layer, _, transposed in entries]


def _shard_blocks(rows):
    out = {}
    for (name, layer, _, transposed), r in zip(ENTRIES, rows):
        out.setdefault(name, []).append(r.T if transposed else r)
    return {name: jnp.stack(layers) for name, layers in out.items()}


def _pack_small(vals):
    flat = jnp.concatenate([vals[name].astype(F32).reshape(-1) for name, _ in SMALL])
    size = -(-flat.shape[0] // (8 * LANES)) * (8 * LANES)
    return jnp.pad(flat, (0, size - flat.shape[0])).reshape(-1, LANES)


def _unpack_small(packed):
    flat, out, at = packed.reshape(-1), {}, 0
    for name, shape in SMALL:
        out[name] = flat[at:at + math.prod(shape)].reshape(shape)
        at += math.prod(shape)
    return out


def _chip_part(full, chip):
    width = full.shape[-1] // N_CHIPS
    return lax.dynamic_slice_in_dim(full, chip * width, width, axis=full.ndim - 1)


def _chip_embed(part, chip):
    full = jnp.zeros(part.shape[:-1] + (part.shape[-1] * N_CHIPS,), part.dtype)
    return lax.dynamic_update_slice_in_dim(full, part, chip * part.shape[-1], axis=part.ndim - 1)


SUBLAYER_KEYS = {"ab_w_in": ("ab", "w_in_t"), "ab_w_out": ("ab", "w_out"), "cd_w_in": ("cd", "w_in_t"), "cd_w_out": ("cd", "w_out"),
                 "xa_w_q": ("xa", "w_q"), "xa_w_kv": ("xa", "w_kv_t"), "xa_w_o": ("xa", "w_o"), "mlp_w_up": ("mlp", "w_up_t"),
                 "mlp_w_down": ("mlp", "w_down")}


def _sublayer(name, layer):
    block, leaf = SUBLAYER_KEYS[name]
    return (block if block in ("ab", "cd") else f"{block}{layer}"), leaf


def _set_big(params, full_rows):
    for (name, layer), rows in full_rows.items():
        block, leaf = _sublayer(name, layer)
        if name == "ab_w_in":
            rows = jnp.concatenate([rows, jnp.zeros((LANES - 8, PACK_COLS), rows.dtype)])
        params[block][leaf] = rows


def _build_params(full_rows, w):
    pad = lambda a: jnp.pad(a, ((0, 0), (0, LANES - a.shape[1])))
    params = {
        "ab": dict(norm=w["ab_norm"], conv_w=w["ab_conv_w"][0], conv_b=w["ab_conv_b"], w_a=_block_diag(w["lru_w_a"][0]),
                   b_a=w["lru_b_a"], w_i=_block_diag(w["lru_w_i"][0]), b_i=w["lru_b_i"], lam=w["lru_lambda"],
                   b_f=w["fox_b_f"].reshape(8, 1)),
        "cd": dict(norm=w["cd_norm"], sink=pad(w["cd_sink"])),
        "final_norm": w["final_norm"].reshape(1, D_MODEL),
    }
    for layer in range(2):
        params[f"xa{layer}"] = dict(norm=w["xa_norm"][layer:layer + 1], mem_norm=w["xa_mem_norm"][layer:layer + 1])
        params[f"mlp{layer}"] = dict(norm=w["mlp_norm"][layer:layer + 1])
    _set_big(params, full_rows)
    return params


def _grad_rows(g, entries=ENTRIES):
    out = []
    for name, layer, _, _ in entries:
        block, leaf = _sublayer(name, layer)
        out.append(g[block][leaf])
    return out


def _reduce_group(entry):
    name, layer = entry[0], entry[1]
    if name.startswith("ab_"):
        return 2
    return 0 if layer == 1 or name.startswith("cd_") else 1


REDUCE_GROUPS = tuple(tuple(e for e in ENTRIES if _reduce_group(e) == group) for group in range(3))


def _reduce_tile(half):
    return max(t for t in range(16, 1025, 16) if half % t == 0)


def _reduce_start(grads_by_chip, core, chip, tag):
    half = grads_by_chip.shape[1] // 2
    tile = _reduce_tile(half)
    steps = half // tile
    place = jnp.stack([core, chip]).astype(jnp.int32)
    got = _swap_cores(grads_by_chip, name=f"swap_cores_{tag}", half_rows=half)
    block = (1, tile, PACK_COLS)

    def sum_cores(place_ref, mine_ref, got_ref, f32_ref, bf16_ref):
        total = mine_ref[...].astype(F32) + got_ref[...].astype(F32)
        f32_ref[...] = total
        bf16_ref[...] = total.astype(BF16)

    pair32, pair16 = pl.pallas_call(
        sum_cores, name=f"sum_cores_{tag}",
        grid_spec=pltpu.PrefetchScalarGridSpec(
            num_scalar_prefetch=1, grid=(N_CHIPS, steps),
            in_specs=[pl.BlockSpec(block, lambda s, i, place_ref: (s, place_ref[0] * steps + i, 0)),
                      pl.BlockSpec(block, lambda s, i, place_ref: (s, i, 0))],
            out_specs=[pl.BlockSpec(block, lambda s, i, place_ref: (s, i, 0))] * 2),
        out_shape=[jax.ShapeDtypeStruct((N_CHIPS, half, PACK_COLS), F32), jax.ShapeDtypeStruct((N_CHIPS, half, PACK_COLS), BF16)],
        compiler_params=_params("arbitrary", "arbitrary"),
    )(place, grads_by_chip, got)
    return pair32, _scatter_start(pair16, tag), place


def _reduce_finish(state, core, tag, after):
    pair32, started, place = state
    half = pair32.shape[1]
    tile = _reduce_tile(half)
    landed = _scatter_wait(started, after, tag)

    def sum_chips(place_ref, own_ref, landed_ref, out_ref):
        out_ref[...] = own_ref[0] + landed_ref[0].astype(F32) + landed_ref[1].astype(F32) + landed_ref[2].astype(F32)

    done = pl.pallas_call(
        sum_chips, name=f"sum_chips_{tag}",
        grid_spec=pltpu.PrefetchScalarGridSpec(
            num_scalar_prefetch=1, grid=(half // tile,),
            in_specs=[pl.BlockSpec((1, tile, PACK_COLS), lambda i, place_ref: (place_ref[1], i, 0)),
                      pl.BlockSpec((3, tile, PACK_COLS), lambda i, place_ref: (0, i, 0))],
            out_specs=pl.BlockSpec((tile, PACK_COLS), lambda i, place_ref: (i, 0))),
        out_shape=jax.ShapeDtypeStruct((half, PACK_COLS), F32), compiler_params=_params("arbitrary"),
    )(place, pair32, landed)
    theirs = _swap_cores(done, name=f"share_halves_{tag}")
    return jnp.where(core == 0, jnp.concatenate([done, theirs]), jnp.concatenate([theirs, done]))


def kernel(x, mem, ab_norm, ab_w_in, ab_conv_w, ab_conv_b, lru_w_a, lru_b_a, lru_w_i, lru_b_i, lru_lambda, fox_b_f, ab_w_out, cd_norm, cd_w_in, cd_sink, cd_w_out, xa_norm, xa_mem_norm, xa_w_q, xa_w_kv, xa_w_o, mlp_norm, mlp_w_up, mlp_w_down, final_norm, loss_target, m_ab_norm, m_ab_w_in, m_ab_conv_w, m_ab_conv_b, m_lru_w_a, m_lru_b_a, m_lru_w_i, m_lru_b_i, m_lru_lambda, m_fox_b_f, m_ab_w_out, m_cd_norm, m_cd_w_in, m_cd_sink, m_cd_w_out, m_xa_norm, m_xa_mem_norm, m_xa_w_q, m_xa_w_kv, m_xa_w_o, m_mlp_norm, m_mlp_w_up, m_mlp_w_down, m_final_norm, v_ab_norm, v_ab_w_in, v_ab_conv_w, v_ab_conv_b, v_lru_w_a, v_lru_b_a, v_lru_w_i, v_lru_b_i, v_lru_lambda, v_fox_b_f, v_ab_w_out, v_cd_norm, v_cd_w_in, v_cd_sink, v_cd_w_out, v_xa_norm, v_xa_mem_norm, v_xa_w_q, v_xa_w_kv, v_xa_w_o, v_mlp_norm, v_mlp_w_up, v_mlp_w_down, v_final_norm):
    given = dict(locals())
    weight_names = [name for name, _ in SMALL if name != "loss"] + [name for name, _, _, _ in BIG]
    w = {name: given[name] for name in weight_names}
    chip = 2 * lax.axis_index("x") + lax.axis_index("y")
    core = lax.axis_index("c")

    slot = lax.broadcasted_iota(jnp.int32, (N_CHIPS, 1, 1), 0)

    def whole(entries, mine, gathered):
        return {(name, layer): jnp.where(slot == chip, own[None], got).reshape(N_CHIPS * rows, PACK_COLS)
                for (name, layer, rows, _), own, got in zip(entries, _unpack_rows(mine, entries), _unpack_rows(gathered, entries))}

    mine_first = _pack_rows(_shard_rows(w, FIRST_ENTRIES), FIRST_ENTRIES, BF16)
    mine_later = _pack_rows(_shard_rows(w, LATER_ENTRIES), LATER_ENTRIES, BF16)
    first = _gather_chips(mine_first)
    owner = (core == 0).astype(F32)
    quarters = {name: _chip_embed(w[name], chip) * owner for name in SPLIT_SMALL}
    zeros = {name: jnp.zeros(shape, F32) for name, shape in SMALL}
    small_packed = _sum_all_devices(_pack_small({**zeros, **quarters}))
    small_full = _unpack_small(small_packed)
    started = _gather_start(mine_later, after=(small_packed, first))
    params = _build_params(whole(FIRST_ENTRIES, mine_first, first),
                           {**w, "ab_conv_w": small_full["ab_conv_w"], "cd_norm": small_full["cd_norm"]})
    params["ab"]["norm"] = params["ab"]["norm"] + started[8][0, 0]

    def complete(h):
        _set_big(params, whole(LATER_ENTRIES, mine_later, _gather_pass(_gather_wait(started, after=h))))

    reducing = {}

    def grads_ready(group, g):
        entries = REDUCE_GROUPS[group]
        by_chip = _pack_rows([r.reshape(N_CHIPS, e[2], PACK_COLS) for r, e in zip(_grad_rows(g, entries), entries)], entries, BF16)
        reducing[group] = _reduce_start(by_chip, core, chip, f"g{group}")
        if group < 2:
            block, leaf = ("mlp0", "w_down") if group == 0 else ("ab", "w_out")
            params[block][leaf] = params[block][leaf] + reducing[group][1][8][0, 0].astype(BF16)

    loss_part, grad_x, g = _local_step(x, mem, loss_target, params, complete, grads_ready)
    grads_ready(2, g)
    reduced = {}
    for group, entries in enumerate(REDUCE_GROUPS):
        rows = _unpack_rows(_reduce_finish(reducing[group], core, f"g{group}", after=grad_x), entries)
        reduced.update({(e[0], e[1]): r for e, r in zip(entries, rows)})
    grads = _shard_blocks([reduced[e[0], e[1]] for e in ENTRIES])
    pair = lambda key, leaf: jnp.stack([g[f"{key}0"][leaf], g[f"{key}1"][leaf]])

    small_local = {"ab_norm": g["ab"]["norm"], "ab_conv_w": g["ab"]["conv_w"], "ab_conv_b": g["ab"]["conv_b"],
                   "lru_w_a": g["ab"]["w_a"], "lru_b_a": g["ab"]["b_a"], "lru_w_i": g["ab"]["w_i"], "lru_b_i": g["ab"]["b_i"],
                   "lru_lambda": g["ab"]["lam"], "fox_b_f": g["ab"]["b_f"], "cd_norm": g["cd"]["norm"], "cd_sink": g["cd"]["sink"],
                   "xa_norm": pair("xa", "norm"), "xa_mem_norm": pair("xa", "mem_norm"), "mlp_norm": pair("mlp", "norm"),
                   "final_norm": g["final_norm"], "loss": loss_part[0, :1]}
    small_sum_packed = _sum_all_devices(_pack_small(small_local))
    small_sum = _unpack_small(small_sum_packed)
    loss = small_sum["loss"][0]

    delta, new_m, new_v = {}, {}, {}
    for name, _, _, _ in BIG:
        shape = w[name].shape
        flat = lambda a: a.reshape(-1, shape[-1])
        d, m2, v2 = _adamw(flat(w[name]), flat(grads[name]), flat(given["m_" + name]), flat(given["v_" + name]), name=f"adamw_{name}")
        delta[name], new_m[name], new_v[name] = d.reshape(shape), m2.reshape(shape), v2.reshape(shape)

    def small_state(prefix):
        vals = {name: (given[prefix + name] if name != "loss" else jnp.zeros((1,), F32)) for name, _ in SMALL}
        for name in SPLIT_SMALL:
            vals[name] = _chip_embed(vals[name], chip)
        return _pack_small(vals)
    packed = _adamw(small_state(""), small_sum_packed, small_state("m_"), small_state("v_"), name="adamw_small")
    for store, vals in zip((delta, new_m, new_v), packed):
        for name, val in _unpack_small(vals).items():
            store[name] = _chip_part(val, chip) if name in SPLIT_SMALL else val
    for name in SPLIT_SMALL:
        small_sum[name] = _chip_part(small_sum[name], chip)
    grads.update({name: small_sum[name] for name, _ in SMALL})

    order = ["ab_norm", "ab_w_in", "ab_conv_w", "ab_conv_b", "lru_w_a", "lru_b_a", "lru_w_i", "lru_b_i", "lru_lambda", "fox_b_f",
             "ab_w_out", "cd_norm", "cd_w_in", "cd_sink", "cd_w_out", "xa_norm", "xa_mem_norm", "xa_w_q", "xa_w_kv", "xa_w_o",
             "mlp_norm", "mlp_w_up", "mlp_w_down", "final_norm"]
    return (loss, grad_x, *[grads[n] for n in order], *[delta[n] for n in order], *[new_m[n] for n in order],
            *[new_v[n] for n in order])
```

```python
import functools
import math

import jax
import jax.numpy as jnp
from jax import lax
from jax.experimental import pallas as pl
from jax.experimental.pallas import tpu as pltpu

F32 = jnp.float32
BF16 = jnp.bfloat16
MESH = pl.DeviceIdType.MESH

D_MODEL = 1024
SEQ = 2048
HEAD_DIM = 64
LRU_WIDTH = 512
LRU_BLOCKS = 8
LRU_C = 8.0
CONV_WIDTH = 4
ATT_W = 512
SWA_KW = 128
SWA_WINDOW = 128
DIL_PATTERN = ((128, 1), (512, 4), (2048, 16))
MEM_LEN = 256
XA_HEADS = 4
XA_HEAD_DIM = 256
D_FF = 4096
ROPE_THETA = 10000.0
EPS = 1e-6
N_CHIPS = 4
LANES = 128

ADAM_LR, ADAM_B1, ADAM_B2, ADAM_EPS, ADAM_WD, ADAM_STEP = 0.001, 0.9, 0.999, 1e-08, 0.01, 10

VMEM_LIMIT_BYTES = 56 * 1024 * 1024
MASKED = -1e30
ROW_TILE = 512
LRU_CHUNK = 512
ATT_BLOCK = 128
BAND_ROWS = 256
ATT_CHUNK = 512
CAUSAL_ROWS = 256
CLASS_ROWS = 512


def _params(*sem):
    return pltpu.CompilerParams(dimension_semantics=sem, vmem_limit_bytes=VMEM_LIMIT_BYTES)


def _rowwise(fn, rows, consts=(), out_rows=(), out_accs=(), *, name, tile=ROW_TILE, row_maps=None):
    rows = [r if isinstance(r, tuple) else (r, r.shape[1], 0) for r in rows]
    consts = list(consts)
    nr, nc, no = len(rows), len(consts), len(out_rows)
    dealt_in = [r[3] if isinstance(r[0], str) else None for r in rows]
    dealt_out = [o[2] if len(o) == 3 else None for o in out_rows]
    total = next(r[0].shape[0] for r in rows if not isinstance(r[0], str))
    tile = min(tile, total)
    assert total % tile == 0
    n = total // tile
    n_acc = len(out_accs)

    def body(*refs):
        scratch = list(refs[nr + nc + no + n_acc:])
        vals = []
        for ref, d, row in zip(refs[:nr], dealt_in, rows):
            if d is None:
                vals.append(ref[...])
                continue
            sc, width = scratch.pop(0), row[2]
            for r in range(d):
                for c in range(width // LANES):
                    lanes = slice(r * width + c * LANES, r * width + (c + 1) * LANES)
                    sc.at[c][pl.ds(r, tile // d, stride=d), :] = ref[:, lanes].astype(F32)
            vals.append(jnp.concatenate([sc[c] for c in range(width // LANES)], axis=1))
        outs = fn(*vals, *[r[...] for r in refs[nr:nr + nc]])
        outs = tuple(outs) if isinstance(outs, (tuple, list)) else (outs,)
        for ref, val, d, spec in zip(refs[nr + nc:nr + nc + no], outs[:no], dealt_out, out_rows):
            if d is None:
                ref[...] = val.astype(ref.dtype)
                continue
            sc, width = scratch.pop(0), spec[0]
            for c in range(width // LANES):
                sc[c] = val[:, c * LANES:(c + 1) * LANES].astype(F32)
            for r in range(d):
                for c in range(width // LANES):
                    lanes = slice(r * width + c * LANES, r * width + (c + 1) * LANES)
                    ref[:, lanes] = sc.at[c][pl.ds(r, tile // d, stride=d), :].astype(ref.dtype)
        for ref, val in zip(refs[nr + nc + no:nr + nc + no + n_acc], outs[no:]):
            @pl.when(pl.program_id(0) == 0)
            def _(ref=ref):
                ref[...] = jnp.zeros(ref.shape, ref.dtype)
            ref[...] += val

    in_specs, args, scratch_shapes = [], [], []
    for idx, row in enumerate(rows):
        if isinstance(row[0], str):
            _, arr, width, d = row
            in_specs.append(pl.BlockSpec((tile // d, d * width), lambda i: (i, 0)))
            scratch_shapes.append(pltpu.VMEM((width // LANES, tile, LANES), F32))
        elif row_maps is not None and row_maps[idx] is not None:
            arr, width, _ = row
            in_specs.append(pl.BlockSpec((tile, width), row_maps[idx]))
        else:
            arr, width, cb = row
            in_specs.append(pl.BlockSpec((tile, width), functools.partial(lambda i, cb: (i, cb), cb=cb)))
        args.append(arr)
    in_specs += [pl.BlockSpec(c.shape, lambda i: (0, 0)) for c in consts]
    out_shape, out_specs = [], []
    for spec, d in zip(out_rows, dealt_out):
        w, dt = spec[0], spec[1]
        if d is None:
            out_shape.append(jax.ShapeDtypeStruct((total, w), dt))
            out_specs.append(pl.BlockSpec((tile, w), lambda i: (i, 0)))
        else:
            out_shape.append(jax.ShapeDtypeStruct((total // d, d * w), dt))
            out_specs.append(pl.BlockSpec((tile // d, d * w), lambda i: (i, 0)))
            scratch_shapes.append(pltpu.VMEM((w // LANES, tile, LANES), F32))
    out_shape += [jax.ShapeDtypeStruct(s, F32) for s in out_accs]
    out_specs += [pl.BlockSpec(s, lambda i: (0, 0)) for s in out_accs]
    return pl.pallas_call(
        body, grid=(n,), in_specs=in_specs, out_specs=out_specs, out_shape=out_shape, scratch_shapes=scratch_shapes, name=name,
        compiler_params=_params("arbitrary"),
    )(*args, *consts)


MATMUL_VMEM_BYTES = 40 * 1024 * 1024


def _matmul_tiles(m, n, k, tm, tn, out_bytes):
    tm, tn, tk = min(tm, m), min(tn, n), k

    def need():
        return 2 * (2 * tk * (tm + tn) + tm * tn * out_bytes) + (0 if tk == k else 4 * tm * tn)

    while need() > MATMUL_VMEM_BYTES:
        if tm >= tn and tm % 256 == 0:
            tm //= 2
        elif tn % 256 == 0:
            tn //= 2
        else:
            tk //= 2
    return tm, tn, tk


def _matmul(a, b, *, ta=False, tb=False, outs=(F32,), epilogue=None, extras=(), consts=(), sums=(), whole_rows=False,
            tm=1024, tn=1024, name):
    m, k = (a.shape[1], a.shape[0]) if ta else a.shape
    n = b.shape[0] if tb else b.shape[1]
    assert (b.shape[1] if tb else b.shape[0]) == k
    outs = [o if isinstance(o, tuple) else (o, None) for o in outs]
    out_bytes = sum(jnp.dtype(dt).itemsize for dt, w in outs if w is None) + sum(e.dtype.itemsize for e in extras)
    tm, tn, tk = _matmul_tiles(m, n, k, tm, tn, out_bytes)
    assert m % tm == 0 and n % tn == 0 and k % tk == 0, (name, m, n, k)
    nk = k // tk
    ne, nc, no = len(extras), len(consts), len(outs)
    assert tn == n or not (sums or whole_rows or any(w is not None for _, w in outs)), name
    dims = (((0 if ta else 1,), (1 if tb else 0,)), ((), ()))

    def body(*refs):
        a_ref, b_ref = refs[:2]
        e_refs, o_refs = refs[2:2 + ne + nc], refs[2 + ne + nc:2 + ne + nc + no]
        s_refs = refs[2 + ne + nc + no:2 + ne + nc + no + len(sums)]

        def finish(acc):
            vals = (acc,) if epilogue is None else epilogue(acc, *[e[...] for e in e_refs])
            for ref, val in zip(o_refs, vals[:no]):
                ref[...] = val.astype(ref.dtype)
            for ref, val in zip(s_refs, vals[no:]):
                @pl.when(pl.program_id(0) == 0)
                def _(ref=ref, val=val):
                    ref[...] = val

                @pl.when(pl.program_id(0) > 0)
                def _(ref=ref, val=val):
                    ref[...] += val

        prod = lax.dot_general(a_ref[...], b_ref[...], dims, preferred_element_type=F32)
        if nk == 1:
            finish(prod)
        else:
            acc_ref = refs[-1]
            step = pl.program_id(2)

            @pl.when(step == 0)
            def _():
                acc_ref[...] = prod

            @pl.when(step > 0)
            def _():
                acc_ref[...] += prod

            @pl.when(step == nk - 1)
            def _():
                finish(acc_ref[...])

    a_spec = pl.BlockSpec((tk, tm), lambda i, j, s: (s, i)) if ta else pl.BlockSpec((tm, tk), lambda i, j, s: (i, s))
    b_spec = pl.BlockSpec((tn, tk), lambda i, j, s: (j, s)) if tb else pl.BlockSpec((tk, tn), lambda i, j, s: (s, j))
    tile_spec = pl.BlockSpec((tm, tn), lambda i, j, s: (i, j))
    whole = lambda shape: pl.BlockSpec(shape, lambda i, j, s: (0, 0))
    return pl.pallas_call(
        body, grid=(m // tm, n // tn, nk),
        in_specs=[a_spec, b_spec] + [tile_spec] * ne + [whole(c.shape) for c in consts],
        out_specs=[tile_spec if w is None else pl.BlockSpec((tm, w), lambda i, j, s: (i, 0)) for _, w in outs]
        + [whole(shape) for shape in sums],
        out_shape=[jax.ShapeDtypeStruct((m, n if w is None else w), dt) for dt, w in outs]
        + [jax.ShapeDtypeStruct(shape, F32) for shape in sums],
        scratch_shapes=[pltpu.VMEM((tm, tn), F32)] if nk > 1 else [],
        name=name, compiler_params=_params("arbitrary" if sums else "parallel", "parallel", "arbitrary"),
    )(a, b, *extras, *consts)


def _split3(x):
    x1 = x.astype(BF16)
    r1 = x - x1.astype(F32)
    x2 = r1.astype(BF16)
    x3 = (r1 - x2.astype(F32)).astype(BF16)
    return x1, x2, x3


def _dot_exact(x, e):
    return sum(jnp.dot(p, e, preferred_element_type=F32) for p in _split3(x))


def _head_expand(heads, width):
    dh = width // heads
    rows = jnp.arange(LANES)[:, None]
    cols = jnp.arange(width)[None, :]
    return (cols // dh == rows).astype(BF16)


def _rstd(x):
    return lax.rsqrt(jnp.mean(x * x, axis=-1, keepdims=True) + EPS)


def _rms_fwd(x, gain, *, name):
    return _rowwise(lambda x, g: x * _rstd(x) * g, [x], [gain], [(x.shape[1], BF16)], name=name)[0]


def _rms_bwd_vals(x, dhn, gain):
    r = _rstd(x)
    xh = x * r
    dxh = dhn * gain
    dx = r * (dxh - xh * jnp.mean(dxh * xh, axis=-1, keepdims=True))
    return dx, jnp.sum(dhn * xh, axis=0, keepdims=True)


def _norm_input_grad(dz, w, x, dres, gain, *, tb=False, name):
    def epilogue(dhn, x, dres, g):
        dx, dg = _rms_bwd_vals(x, dhn, g)
        dh = dres + dx
        return dh, dh, dg
    return _matmul(dz, w, tb=tb, outs=(F32, BF16), extras=(x, dres), consts=(gain,), sums=((1, x.shape[1]),), epilogue=epilogue,
                   name=name)


def _loss_and_grad(h, target, gain, *, name):
    def fn(h, tgt, g):
        r = _rstd(h)
        err = h * r * g - tgt
        loss = 0.5 * jnp.sum(jnp.mean(err * err, axis=-1, keepdims=True), axis=0, keepdims=True)
        dx, dg = _rms_bwd_vals(h, err * (1.0 / D_MODEL), g)
        return dx, dx, jnp.broadcast_to(loss, (1, LANES)), dg
    d = h.shape[1]
    return _rowwise(fn, [h, target], [gain], [(d, F32), (d, BF16)], [(1, LANES), (1, d)], name=name)


def _adamw(w, g, m, v, *, name):
    rows, cols = w.shape
    tile = rows
    for cand in (256, 128, 64, 32, 16, 8):
        if rows % cand == 0 and rows > cand:
            tile = cand
            break
    bc1 = 1.0 - ADAM_B1 ** ADAM_STEP
    bc2 = 1.0 - ADAM_B2 ** ADAM_STEP

    def fn(w, g, m, v):
        m2 = ADAM_B1 * m + (1.0 - ADAM_B1) * g
        v2 = ADAM_B2 * v + (1.0 - ADAM_B2) * (g * g)
        delta = -ADAM_LR * ((m2 / bc1) / (jnp.sqrt(v2 / bc2) + ADAM_EPS) + ADAM_WD * w)
        return delta, m2, v2
    return _rowwise(fn, [w, g, m, v], [], [(cols, F32)] * 3, name=name, tile=tile)


def _attn_specs(arr, width, cb, classes, rows_block, whole, together=1):
    ncols = arr.shape[2] // (classes * width)
    lanes, at = (width, lambda r: r * ncols + cb) if together == 1 else (together * ncols * width, lambda r: r)
    if whole:
        return pl.BlockSpec((1, arr.shape[1], lanes), lambda n, r, i: (n, 0, at(r)))
    return pl.BlockSpec((1, rows_block, lanes), lambda n, r, i: (n, i, at(r)))


def _class_window(refs, spans, together, cl):
    if together == 1:
        return refs
    return [ref.at[:, :, pl.ds((cl * ncols + cb) * width, width)] for ref, (ncols, cb, width) in zip(refs, spans)]


def _attn_tiles(mode, lq, lk, backward=False):
    if mode == "full":
        return min(lq, 256), min(lq, 256), lk
    if mode == "band":
        tq = min(BAND_ROWS, lq)
        rows = tq if backward else ATT_BLOCK
        return tq, rows, min(rows + ATT_BLOCK, lk)
    return CAUSAL_ROWS, CAUSAL_ROWS, ATT_CHUNK


def _window(mode, row0, j, rows, win, lk):
    if mode == "causal":
        return pl.multiple_of(j * win, win), row0 // win + 1
    if mode == "band":
        return pl.multiple_of(jnp.clip(row0 + rows - win, 0, lk - win), ATT_BLOCK), 1
    return 0, 1


def _allowed(mode, row0, start, rows, win, max_dist):
    if mode == "full":
        return None
    dist = (row0 + lax.broadcasted_iota(jnp.int32, (rows, win), 0)) - (start + lax.broadcasted_iota(jnp.int32, (rows, win), 1))
    ok = dist >= 0
    if max_dist is not None:
        ok = ok & (dist <= max_dist)
    return ok


def _head_groups(heads, dh):
    gw = max(LANES, dh)
    return gw, gw // dh, heads // (gw // dh)


def _own_lanes(rows, gw, dh, u):
    return lax.broadcasted_iota(jnp.int32, (rows, gw), 1) // dh == u


def _only_head(x, own, per, u):
    return x if per == 1 else jnp.where(own[u], x, jnp.zeros_like(x))


def _step_scores(qz, kw, kb_row, ok, scale):
    s = lax.dot_general(qz, kw, (((1,), (1,)), ((), ())), preferred_element_type=F32) * scale
    if kb_row is not None:
        s = s - kb_row
    if ok is not None:
        s = jnp.where(ok, s, MASKED)
    return s


def _attn_fwd(q, k, v, kb=None, *, classes=1, heads, dh, mode, max_dist=None, bf16_copy=False, name):
    (qa, qc), (ka, kc), (va, vc) = q, k, v
    n, lq, lk = qa.shape[0], qa.shape[1], ka.shape[1]
    width = heads * dh
    tq, rows, win = _attn_tiles(mode, lq, lk)
    gw, per, groups = _head_groups(heads, dh)
    scale = dh ** -0.5
    has_kb = kb is not None
    single = mode != "causal"
    together = min(classes, max(1, CLASS_ROWS // lq))
    ncols = lambda arr: arr.shape[2] // (classes * width)
    spans = [(ncols(qa), qc, width), (ncols(ka), kc, width), (ncols(va), vc, width), (1, 0, width), (1, 0, LANES), (1, 0, width)]

    def body(*refs):
        for cl in range(together):
            for sub in range(tq // rows):
                part(_class_window(refs, spans, together, cl), pl.program_id(2) * tq + sub * rows, slice(sub * rows, (sub + 1) * rows))

    def part(refs, row0, rs):
        q_ref, k_ref, v_ref = refs[:3]
        kb_ref = refs[3] if has_kb else None
        n_in = 4 if has_kb else 3
        o_ref, lse_ref = refs[n_in:n_in + 2]
        o16_ref = refs[n_in + 2] if bf16_copy else None
        lane = lax.broadcasted_iota(jnp.int32, (rows, LANES), 1)
        own = [_own_lanes(rows, gw, dh, u) for u in range(per)]

        def step(j, carry):
            m_in, l_in = carry
            m_out, l_out = m_in, l_in
            start, _ = _window(mode, row0, j, rows, win, lk)
            ok = _allowed(mode, row0, start, rows, win, max_dist)
            for g in range(groups):
                cols = slice(g * gw, (g + 1) * gw)
                qg = q_ref[0, rs, cols]
                kw = k_ref[0, pl.ds(start, win), cols]
                vw = v_ref[0, pl.ds(start, win), cols]
                alpha_g = new_g = None
                for u in range(per):
                    h = g * per + u
                    kb_row = kb_ref[0, h, pl.ds(j, 1), :] if has_kb else None
                    s = _step_scores(_only_head(qg, own, per, u), kw, kb_row, ok, scale)
                    m_new = jnp.max(s, axis=1, keepdims=True)
                    if not single:
                        m_old = m_in[:, h:h + 1]
                        m_new = jnp.maximum(m_old, m_new)
                        alpha = jnp.exp(m_old - m_new)
                        alpha_g = alpha if u == 0 else jnp.where(own[u], alpha, alpha_g)
                    p = jnp.exp(s - m_new)
                    l_new = jnp.sum(p, axis=1, keepdims=True)
                    r = jnp.dot(p.astype(BF16), vw, preferred_element_type=F32)
                    if single:
                        r = r * (1.0 / l_new)
                    else:
                        l_new = alpha * l_in[:, h:h + 1] + l_new
                    new_g = r if u == 0 else jnp.where(own[u], r, new_g)
                    m_out = jnp.where(lane == h, m_new, m_out)
                    l_out = jnp.where(lane == h, l_new, l_out)
                o_ref[0, rs, cols] = new_g if single else alpha_g * o_ref[0, rs, cols] + new_g
                if bf16_copy:
                    o16_ref[0, rs, cols] = new_g.astype(BF16)
            return m_out, l_out

        carry = (jnp.full((rows, LANES), MASKED, F32), jnp.zeros((rows, LANES), F32))
        if single:
            m_all, l_all = step(0, carry)
            l_all = jnp.where(lane < heads, l_all, 1.0)
        else:
            o_ref[...] = jnp.zeros(o_ref.shape, F32)
            m_all, l_all = lax.fori_loop(0, _window(mode, row0, 0, rows, win, lk)[1], step, carry)
            l_all = jnp.where(lane < heads, l_all, 1.0)
            inv = 1.0 / l_all
            for g in range(groups):
                cols = slice(g * gw, (g + 1) * gw)
                inv_g = inv[:, g * per:g * per + 1]
                for u in range(1, per):
                    inv_g = jnp.where(own[u], inv[:, g * per + u:g * per + u + 1], inv_g)
                o_ref[0, rs, cols] = o_ref[0, rs, cols] * inv_g
        lse_ref[0, rs, :] = jnp.where(lane < heads, m_all + jnp.log(l_all), 0.0)

    in_specs = [_attn_specs(qa, width, qc, classes, tq, False, together), _attn_specs(ka, width, kc, classes, win, True, together),
                _attn_specs(va, width, vc, classes, win, True, together)]
    args = [qa, ka, va]
    if has_kb:
        assert together == 1
        in_specs.append(pl.BlockSpec((1,) + kb.shape[1:], lambda n_, r, i: (n_, 0, 0, 0)))
        args.append(kb)
    out_shape = [jax.ShapeDtypeStruct((n, lq, classes * width), F32), jax.ShapeDtypeStruct((n, lq, classes * LANES), F32)]
    out_specs = [pl.BlockSpec((1, tq, together * width), lambda n_, r, i: (n_, i, r)),
                 pl.BlockSpec((1, tq, together * LANES), lambda n_, r, i: (n_, i, r))]
    if bf16_copy:
        assert single
        out_shape.append(jax.ShapeDtypeStruct((n, lq, classes * width), BF16))
        out_specs.append(out_specs[0])
    return pl.pallas_call(
        body, grid=(n, classes // together, lq // tq), in_specs=in_specs, out_specs=out_specs, out_shape=out_shape, name=name,
        compiler_params=_params("parallel", "parallel", "arbitrary"),
    )(*args)


def _attn_bwd(q, k, v, do, lse, delta, kb=None, *, classes=1, heads, dh, mode, max_dist=None, dq_dtype=F32, name):
    (qa, qc), (ka, kc), (va, vc), (da, dc) = q, k, v, do
    n, lq, lk = qa.shape[0], qa.shape[1], ka.shape[1]
    width = heads * dh
    tq, rows, win = _attn_tiles(mode, lq, lk, backward=True)
    gw, per, groups = _head_groups(heads, dh)
    scale = dh ** -0.5
    has_kb = kb is not None
    single = mode != "causal"
    nt = (((1,), (1,)), ((), ()))
    tn = (((0,), (0,)), ((), ()))
    together = min(classes, max(1, CLASS_ROWS // lq))
    ncols = lambda arr: arr.shape[2] // (classes * width)
    spans = [(ncols(qa), qc, width), (ncols(ka), kc, width), (ncols(va), vc, width), (ncols(da), dc, width), (1, 0, LANES),
             (1, 0, LANES), (1, 0, width), (1, 0, width), (1, 0, width)]

    def body(*refs):
        n_in = 7 if has_kb else 6
        dk_ref, dv_ref = refs[n_in + 1:n_in + 3]

        @pl.when(pl.program_id(2) == 0)
        def _():
            dk_ref[...] = jnp.zeros(dk_ref.shape, F32)
            dv_ref[...] = jnp.zeros(dv_ref.shape, F32)
            if has_kb:
                refs[n_in + 3][...] = jnp.zeros(refs[n_in + 3].shape, F32)

        for cl in range(together):
            for sub in range(tq // rows):
                part(_class_window(refs, spans, together, cl), pl.program_id(2) * tq + sub * rows, slice(sub * rows, (sub + 1) * rows))

    def part(refs, row0, rs):
        q_ref, k_ref, v_ref, do_ref, lse_ref, dl_ref = refs[:6]
        kb_ref = refs[6] if has_kb else None
        n_in = 7 if has_kb else 6
        dq_ref, dk_ref, dv_ref = refs[n_in:n_in + 3]
        dkb_ref, drow_ref = refs[n_in + 3:n_in + 5] if has_kb else (None, None)
        lse_all = lse_ref[0, rs, :]
        dl_all = dl_ref[0, rs, :]
        lane = lax.broadcasted_iota(jnp.int32, (rows, LANES), 1)
        own = [_own_lanes(rows, gw, dh, u) for u in range(per)]

        def step(j, drow_all):
            start, _ = _window(mode, row0, j, rows, win, lk)
            ok = _allowed(mode, row0, start, rows, win, max_dist)
            for g in range(groups):
                cols = slice(g * gw, (g + 1) * gw)
                qg = q_ref[0, rs, cols]
                dog = do_ref[0, rs, cols]
                kw = k_ref[0, pl.ds(start, win), cols]
                vw = v_ref[0, pl.ds(start, win), cols]
                dq_g = dk_w = dv_w = None
                for u in range(per):
                    h = g * per + u
                    qz, doz = _only_head(qg, own, per, u), _only_head(dog, own, per, u)
                    kb_row = kb_ref[0, h, pl.ds(j, 1), :] if has_kb else None
                    p = jnp.exp(_step_scores(qz, kw, kb_row, ok, scale) - lse_all[:, h:h + 1])
                    dp = lax.dot_general(doz, vw, nt, preferred_element_type=F32)
                    ds = p * (dp - dl_all[:, h:h + 1])
                    dsb = ds.astype(BF16)
                    r = jnp.dot(dsb, kw, preferred_element_type=F32)
                    dq_g = r if u == 0 else jnp.where(own[u], r, dq_g)
                    dk_u = lax.dot_general(dsb, qz, tn, preferred_element_type=F32)
                    dv_u = lax.dot_general(p.astype(BF16), doz, tn, preferred_element_type=F32)
                    dk_w = dk_u if u == 0 else dk_w + dk_u
                    dv_w = dv_u if u == 0 else dv_w + dv_u
                    if has_kb:
                        dkb_ref[0, h, pl.ds(j, 1), :] += -jnp.sum(ds, axis=0, keepdims=True)
                        drow_all = drow_all + jnp.where(lane == h, jnp.sum(ds, axis=1, keepdims=True), 0.0)
                dk_ref[0, pl.ds(start, win), cols] += dk_w * scale
                dv_ref[0, pl.ds(start, win), cols] += dv_w
                if single:
                    dq_ref[0, rs, cols] = (dq_g * scale).astype(dq_ref.dtype)
                else:
                    dq_ref[0, rs, cols] += dq_g * scale
            return drow_all

        drow_all = jnp.zeros((rows, LANES), F32)
        if single:
            drow_all = step(0, drow_all)
        else:
            dq_ref[...] = jnp.zeros(dq_ref.shape, F32)
            drow_all = lax.fori_loop(0, _window(mode, row0, 0, rows, win, lk)[1], step, drow_all)
        if has_kb:
            drow_ref[0, rs, :] = drow_all

    row_spec = functools.partial(_attn_specs, classes=classes, rows_block=tq, whole=False, together=together)
    in_specs = [row_spec(qa, width, qc), _attn_specs(ka, width, kc, classes, win, True, together),
                _attn_specs(va, width, vc, classes, win, True, together), row_spec(da, width, dc), row_spec(lse, LANES, 0),
                row_spec(delta, LANES, 0)]
    args = [qa, ka, va, da, lse, delta]
    assert single or dq_dtype == F32
    out_shape = [jax.ShapeDtypeStruct((n, lq, classes * width), dq_dtype), jax.ShapeDtypeStruct((n, lk, classes * width), F32),
                 jax.ShapeDtypeStruct((n, lk, classes * width), F32)]
    kv_spec = pl.BlockSpec((1, lk, together * width), lambda n_, r, i: (n_, 0, r))
    out_specs = [pl.BlockSpec((1, tq, together * width), lambda n_, r, i: (n_, i, r)), kv_spec, kv_spec]
    if has_kb:
        assert together == 1
        kb_spec = pl.BlockSpec((1,) + kb.shape[1:], lambda n_, r, i: (n_, 0, 0, 0))
        in_specs.append(kb_spec)
        args.append(kb)
        out_shape += [jax.ShapeDtypeStruct(kb.shape, F32), jax.ShapeDtypeStruct((n, lq, LANES), F32)]
        out_specs += [kb_spec, pl.BlockSpec((1, tq, LANES), lambda n_, r, i: (n_, i, 0))]
    return pl.pallas_call(
        body, grid=(n, classes // together, lq // tq), in_specs=in_specs, out_specs=out_specs, out_shape=out_shape, name=name,
        compiler_params=_params("parallel", "parallel", "arbitrary"),
    )(*args)


def _head_delta(do, out, heads, *, name, lse=None, sink=None):
    width = out.shape[1]
    gather = _head_expand(heads, width).T

    if sink is None:
        return _rowwise(lambda do, o, e: _dot_exact(do * o, e), [do, out], [gather], [(LANES, F32)], name=name)[0]

    def fn(do, o, lse, e, sink):
        delta = _dot_exact(do * o, e)
        return delta, -jnp.sum(jnp.exp(sink - lse) * delta, axis=0, keepdims=True)
    return _rowwise(fn, [do, out, lse], [gather, sink], [(LANES, F32)], [(1, LANES)], name=name)


def _rope_tables():
    half = HEAD_DIM // 2
    inv = ROPE_THETA ** (-jnp.arange(half, dtype=F32) / half)
    ang = jnp.arange(SEQ, dtype=F32)[:, None] * inv[None, :]
    cos = jnp.tile(jnp.cos(ang), (1, 2 * ATT_W // HEAD_DIM))
    sin = jnp.tile(jnp.concatenate([-jnp.sin(ang), jnp.sin(ang)], axis=1), (1, ATT_W // HEAD_DIM))
    return cos, sin


def _rotate(x, cos, sin):
    width = x.shape[1]
    lane = lax.broadcasted_iota(jnp.int32, x.shape, 1)
    first_half = (lane % HEAD_DIM) < (HEAD_DIM // 2)
    swapped = jnp.where(first_half, pltpu.roll(x, width - HEAD_DIM // 2, axis=1), pltpu.roll(x, HEAD_DIM // 2, axis=1))
    return x * cos[:, :width] + swapped * sin[:, :width]


def _gelu(x):
    k = math.sqrt(2.0 / math.pi)
    t = jnp.tanh(k * (x + 0.044715 * x * x * x))
    return 0.5 * x * (1.0 + t), t


def _gelu_grad(x, t):
    k = math.sqrt(2.0 / math.pi)
    return 0.5 * (1.0 + t) + 0.5 * x * (1.0 - t * t) * k * (1.0 + 3.0 * 0.044715 * x * x)


def _shift_down(x, halo, s):
    ext = jnp.concatenate([halo, x], axis=0)
    return pltpu.roll(ext, s, axis=0)[8:, :]


def _shift_up(x, halo, s):
    rows = x.shape[0]
    ext = jnp.concatenate([x, halo], axis=0)
    return pltpu.roll(ext, rows + 8 - s, axis=0)[:rows, :]


def _lru_gates(u, halo, cw, cb, wa, ba, wi, bi, lam):
    taps = [_shift_down(u, halo, CONV_WIDTH - 1 - k) for k in range(CONV_WIDTH - 1)] + [u]
    uc = cb + sum(cw[k:k + 1, :] * taps[k] for k in range(CONV_WIDTH))
    ucb = uc.astype(BF16)
    r = jax.nn.sigmoid(jnp.dot(ucb, wa, preferred_element_type=F32) + ba)
    gi = jax.nn.sigmoid(jnp.dot(ucb, wi, preferred_element_type=F32) + bi)
    sp = jnp.maximum(-lam, 0.0) + jnp.log(1.0 + jnp.exp(-jnp.abs(lam)))
    log_a = -LRU_C * r * sp
    a = jnp.exp(log_a)
    x2 = 2.0 * log_a
    one_minus_a2 = jnp.where(x2 > -0.01, -x2 * (1.0 + x2 * (0.5 + x2 * (1.0 / 6.0 + x2 / 24.0))), 1.0 - jnp.exp(x2))
    mult = jnp.sqrt(one_minus_a2)
    return taps, uc, ucb, r, gi, sp, a, mult


def _lru_specs(nchunk, reverse):
    def chunk(b, c):
        return b * nchunk + ((nchunk - 1 - c) if reverse else c)

    def halo_before(b, c):
        return jnp.maximum(chunk(b, c) * (LRU_CHUNK // 8) - 1, 0)

    return chunk, halo_before


def _lru_fwd(zug, cw, cb, wa, ba, wi, bi, lam, *, name):
    total = zug.shape[0]
    nchunk = SEQ // LRU_CHUNK
    w = LRU_WIDTH
    chunk, halo_before = _lru_specs(nchunk, False)

    def body(u_ref, uh_ref, g_ref, cw_ref, cb_ref, wa_ref, ba_ref, wi_ref, bi_ref, lam_ref, y_ref, h_ref, a_sc, x_sc, carry_sc):
        c = pl.program_id(1)
        u = u_ref[...]
        halo = jnp.where(c > 0, uh_ref[...], 0.0)
        _, uc, _, _, gi, _, a, mult = _lru_gates(u, halo, cw_ref[...], cb_ref[...], wa_ref[...], ba_ref[...],
                                                 wi_ref[...], bi_ref[...], lam_ref[...])
        a_sc[...] = a
        x_sc[...] = mult * (gi * uc)

        @pl.when(c == 0)
        def _():
            carry_sc[...] = jnp.zeros(carry_sc.shape, F32)

        def tile_step(t, h):
            r0 = pl.multiple_of(t * 8, 8)
            at = a_sc[pl.ds(r0, 8), :]
            xt = x_sc[pl.ds(r0, 8), :]
            rows = []
            for j in range(8):
                h = at[j:j + 1, :] * h + xt[j:j + 1, :]
                rows.append(h)
            h_ref[pl.ds(r0, 8), :] = jnp.concatenate(rows, axis=0)
            return h

        h_last = lax.fori_loop(0, LRU_CHUNK // 8, tile_step, carry_sc[0:1, :])
        carry_sc[0:1, :] = h_last
        gel, _ = _gelu(g_ref[...])
        y_ref[...] = (h_ref[...] * gel).astype(BF16)

    small = lambda arr: pl.BlockSpec(arr.shape, lambda b, c: (0, 0))
    return pl.pallas_call(
        body, grid=(total // SEQ, nchunk),
        in_specs=[pl.BlockSpec((LRU_CHUNK, w), lambda b, c: (chunk(b, c), 0)),
                  pl.BlockSpec((8, w), lambda b, c: (halo_before(b, c), 0)),
                  pl.BlockSpec((LRU_CHUNK, w), lambda b, c: (chunk(b, c), 1)),
                  small(cw), small(cb), small(wa), small(ba), small(wi), small(bi), small(lam)],
        out_specs=[pl.BlockSpec((LRU_CHUNK, w), lambda b, c: (chunk(b, c), 0))] * 2,
        out_shape=[jax.ShapeDtypeStruct((total, w), BF16), jax.ShapeDtypeStruct((total, w), F32)],
        scratch_shapes=[pltpu.VMEM((LRU_CHUNK, w), F32), pltpu.VMEM((LRU_CHUNK, w), F32), pltpu.VMEM((8, w), F32)],
        name=name, compiler_params=_params("arbitrary", "arbitrary"),
    )(zug, zug, zug, cw, cb, wa, ba, wi, bi, lam)


def _lru_bwd(zug, hl, dy, cw, cb, wa, ba, wi, bi, lam, *, name):
    total = zug.shape[0]
    nchunk = SEQ // LRU_CHUNK
    w = LRU_WIDTH
    chunk, halo_before = _lru_specs(nchunk, True)
    tn = (((0,), (0,)), ((), ()))
    nt = (((1,), (1,)), ((), ()))

    def body(u_ref, uh_ref, g_ref, hl_ref, hh_ref, dy_ref, cw_ref, cb_ref, wa_ref, ba_ref, wi_ref, bi_ref, lam_ref,
             dz_ref, dcw_ref, dcb_ref, dwa_ref, dba_ref, dwi_ref, dbi_ref, dlam_ref,
             a_sc, d_sc, g_sc, gcarry_sc, acarry_sc, duc_sc):
        b, c = pl.program_id(0), pl.program_id(1)
        first_chunk = c == nchunk - 1

        @pl.when((b == 0) & (c == 0))
        def _():
            for ref in (dcw_ref, dcb_ref, dwa_ref, dba_ref, dwi_ref, dbi_ref, dlam_ref):
                ref[...] = jnp.zeros(ref.shape, F32)

        @pl.when(c == 0)
        def _():
            gcarry_sc[...] = jnp.zeros(gcarry_sc.shape, F32)
            acarry_sc[...] = jnp.zeros(acarry_sc.shape, F32)
            duc_sc[...] = jnp.zeros(duc_sc.shape, F32)

        u = u_ref[...]
        halo = jnp.where(first_chunk, 0.0, uh_ref[...])
        cw, wa, wi, lam = cw_ref[...], wa_ref[...], wi_ref[...], lam_ref[...]
        taps, uc, ucb, r, gi, sp, a, mult = _lru_gates(u, halo, cw, cb_ref[...], wa, ba_ref[...], wi, bi_ref[...], lam)
        gate = g_ref[...]
        gel, th = _gelu(gate)
        dy = dy_ref[...]
        hl = hl_ref[...]
        a_sc[...] = a
        d_sc[...] = dy * gel
        dgate = dy * hl * _gelu_grad(gate, th)

        def tile_step(t, carry):
            g_next, a_next = carry
            r0 = pl.multiple_of((LRU_CHUNK // 8 - 1 - t) * 8, 8)
            dt = d_sc[pl.ds(r0, 8), :]
            at = a_sc[pl.ds(r0, 8), :]
            rows = [None] * 8
            for j in reversed(range(8)):
                g_next = dt[j:j + 1, :] + a_next * g_next
                a_next = at[j:j + 1, :]
                rows[j] = g_next
            g_sc[pl.ds(r0, 8), :] = jnp.concatenate(rows, axis=0)
            return g_next, a_next

        g_last, a_last = lax.fori_loop(0, LRU_CHUNK // 8, tile_step, (gcarry_sc[0:1, :], acarry_sc[0:1, :]))
        gcarry_sc[0:1, :] = g_last
        acarry_sc[0:1, :] = a_last

        gs = g_sc[...]
        hl_halo = jnp.where(first_chunk, 0.0, hh_ref[...])
        da = gs * _shift_down(hl, hl_halo, 1)
        dmult = gs * gi * uc
        dgi = gs * mult * uc
        duc = gs * mult * gi
        dlog_a = da * a - dmult * a * a / mult
        dr = dlog_a * (-LRU_C * sp)
        dsp = jnp.sum(dlog_a * (-LRU_C * r), axis=0, keepdims=True)
        dlam_ref[...] += -dsp * jax.nn.sigmoid(-lam)
        dpa = dr * r * (1.0 - r)
        dpi = dgi * gi * (1.0 - gi)
        dpab, dpib = dpa.astype(BF16), dpi.astype(BF16)
        dba_ref[...] += jnp.sum(dpa, axis=0, keepdims=True)
        dbi_ref[...] += jnp.sum(dpi, axis=0, keepdims=True)
        dwa_ref[...] += lax.dot_general(ucb, dpab, tn, preferred_element_type=F32)
        dwi_ref[...] += lax.dot_general(ucb, dpib, tn, preferred_element_type=F32)
        duc = duc + lax.dot_general(dpab, wa, nt, preferred_element_type=F32)
        duc = duc + lax.dot_general(dpib, wi, nt, preferred_element_type=F32)
        dcb_ref[...] += jnp.sum(duc, axis=0, keepdims=True)
        dcw_ref[...] += jnp.concatenate([jnp.sum(duc * taps[k], axis=0, keepdims=True) for k in range(CONV_WIDTH)], axis=0)
        after = duc_sc[...]
        du = cw[CONV_WIDTH - 1:CONV_WIDTH, :] * duc
        for k in range(CONV_WIDTH - 1):
            du = du + cw[k:k + 1, :] * _shift_up(duc, after, CONV_WIDTH - 1 - k)
        duc_sc[...] = duc[0:8, :]
        dz_ref[:, 0:w] = du.astype(BF16)
        dz_ref[:, w:2 * w] = dgate.astype(BF16)

    small = lambda arr: pl.BlockSpec(arr.shape, lambda b, c: (0, 0))
    acc = lambda shape: pl.BlockSpec(shape, lambda b, c: (0, 0))
    chunk_spec = lambda cb_: pl.BlockSpec((LRU_CHUNK, w), lambda b, c: (chunk(b, c), cb_))
    halo_spec = pl.BlockSpec((8, w), lambda b, c: (halo_before(b, c), 0))
    acc_shapes = [(CONV_WIDTH, w), (1, w), (w, w), (1, w), (w, w), (1, w), (1, w)]
    return pl.pallas_call(
        body, grid=(total // SEQ, nchunk),
        in_specs=[chunk_spec(0), halo_spec, chunk_spec(1), chunk_spec(0), halo_spec, chunk_spec(0),
                  small(cw), small(cb), small(wa), small(ba), small(wi), small(bi), small(lam)],
        out_specs=[pl.BlockSpec((LRU_CHUNK, 2 * w), lambda b, c: (chunk(b, c), 0))] + [acc(s) for s in acc_shapes],
        out_shape=[jax.ShapeDtypeStruct((total, 2 * w), BF16)] + [jax.ShapeDtypeStruct(s, F32) for s in acc_shapes],
        scratch_shapes=[pltpu.VMEM((LRU_CHUNK, w), F32)] * 3 + [pltpu.VMEM((8, w), F32)] * 3,
        name=name, compiler_params=_params("arbitrary", "arbitrary"),
    )(zug, zug, zug, hl, hl, dy, cw, cb, wa, ba, wi, bi, lam)


def _block_diag(wb):
    eye = jnp.eye(LRU_BLOCKS, dtype=wb.dtype)
    return jnp.einsum("gcd,gh->gchd", wb, eye).reshape(LRU_WIDTH, LRU_WIDTH).astype(BF16)


def _diag_blocks(wd):
    blk = LRU_WIDTH // LRU_BLOCKS
    return jnp.stack([wd[g * blk:(g + 1) * blk, g * blk:(g + 1) * blk] for g in range(LRU_BLOCKS)])


FOX_SCAN = 256


def _fox_bias(fl, bf, *, name):
    nb = fl.shape[0]

    def body(fl_ref, bf_ref, c_ref):
        x = fl_ref[0] + bf_ref[...]
        logf = jnp.minimum(x, 0.0) - jnp.log(1.0 + jnp.exp(-jnp.abs(x)))
        upper = (lax.broadcasted_iota(jnp.int32, (FOX_SCAN, FOX_SCAN), 0)
                 <= lax.broadcasted_iota(jnp.int32, (FOX_SCAN, FOX_SCAN), 1)).astype(BF16)
        carry = jnp.zeros((LRU_BLOCKS, 1), F32)
        for j in range(SEQ // FOX_SCAN):
            blk = logf[:, j * FOX_SCAN:(j + 1) * FOX_SCAN]
            c_ref[0, :, j * FOX_SCAN:(j + 1) * FOX_SCAN] = _dot_exact(blk, upper) + carry
            carry = carry + jnp.sum(blk, axis=1, keepdims=True)

    return pl.pallas_call(
        body, grid=(nb,),
        in_specs=[pl.BlockSpec((1, 8, SEQ), lambda b: (b, 0, 0)), pl.BlockSpec((8, 1), lambda b: (0, 0))],
        out_specs=pl.BlockSpec((1, 8, SEQ), lambda b: (b, 0, 0)),
        out_shape=jax.ShapeDtypeStruct((nb, 8, SEQ), F32), name=name, compiler_params=_params("arbitrary"),
    )(fl, bf)


def _fox_bias_bwd(fl, bf, dc, *, name):
    nb = fl.shape[0]

    def body(fl_ref, bf_ref, dc_ref, dfl_ref, dbf_ref):
        @pl.when(pl.program_id(0) == 0)
        def _():
            dbf_ref[...] = jnp.zeros(dbf_ref.shape, F32)

        x = fl_ref[0] + bf_ref[...]
        dc_all = dc_ref[0]
        lower = (lax.broadcasted_iota(jnp.int32, (FOX_SCAN, FOX_SCAN), 0)
                 >= lax.broadcasted_iota(jnp.int32, (FOX_SCAN, FOX_SCAN), 1)).astype(BF16)
        carry = jnp.zeros((LRU_BLOCKS, 1), F32)
        total = jnp.zeros((LRU_BLOCKS, 1), F32)
        for j in reversed(range(SEQ // FOX_SCAN)):
            cols = slice(j * FOX_SCAN, (j + 1) * FOX_SCAN)
            blk = dc_all[:, cols]
            dlogf = _dot_exact(blk, lower) + carry
            carry = carry + jnp.sum(blk, axis=1, keepdims=True)
            dx = dlogf * jax.nn.sigmoid(-x[:, cols])
            dfl_ref[0, :, cols] = dx
            total = total + jnp.sum(dx, axis=1, keepdims=True)
        dbf_ref[...] += total

    return pl.pallas_call(
        body, grid=(nb,),
        in_specs=[pl.BlockSpec((1, 8, SEQ), lambda b: (b, 0, 0)), pl.BlockSpec((8, 1), lambda b: (0, 0)),
                  pl.BlockSpec((1, 8, SEQ), lambda b: (b, 0, 0))],
        out_specs=[pl.BlockSpec((1, 8, SEQ), lambda b: (b, 0, 0)), pl.BlockSpec((8, 1), lambda b: (0, 0))],
        out_shape=[jax.ShapeDtypeStruct((nb, 8, SEQ), F32), jax.ShapeDtypeStruct((8, 1), F32)],
        name=name, compiler_params=_params("arbitrary"),
    )(fl, bf, dc)


def _seq3(a, nb):
    return a.reshape(nb, a.shape[0] // nb, a.shape[1])


def _flat2(a):
    return a.reshape(a.shape[0] * a.shape[1], a.shape[2])


def _residual_out(y, w, h, next_gain, *, name):
    if next_gain is None:
        out, = _matmul(y, w, extras=(h,), epilogue=lambda acc, res: (acc + res,), name=name)
        return out, None

    def epilogue(acc, res, g):
        out = acc + res
        return out, out * _rstd(out) * g
    return _matmul(y, w, outs=(F32, BF16), extras=(h,), consts=(next_gain,), epilogue=epilogue, whole_rows=True, name=name)


def _mlp_fwd(h, hn, p, tag, next_gain):
    act, = _matmul(hn, p["w_up_t"], tb=True, outs=(BF16,), epilogue=lambda acc: (jnp.square(jnp.maximum(acc, 0.0)),),
                   name=f"{tag}_up")
    out, hn_next = _residual_out(act, p["w_down"], h, next_gain, name=f"{tag}_down")
    return out, hn_next, (h, hn, act)


def _mlp_bwd(dh, dhb, p, saved, tag):
    h, hn, act = saved
    dup, = _matmul(dhb, p["w_down"], tb=True, outs=(BF16,), extras=(act,),
                   epilogue=lambda acc, act: (acc * (2.0 * jnp.sqrt(act.astype(F32))),), name=f"{tag}_bwd_dup")
    dw_down, = _matmul(act, dhb, ta=True, outs=(BF16,),name=f"{tag}_bwd_dwdown")
    dw_up_t, = _matmul(dup, hn, ta=True, outs=(BF16,),name=f"{tag}_bwd_dwup")
    dh2, dh2b, dg = _norm_input_grad(dup, p["w_up_t"], h, dh, p["norm"], name=f"{tag}_bwd_dh")
    return dh2, dh2b, {"norm": dg, "w_up_t": dw_up_t, "w_down": dw_down}


def _xa_fwd(h, hn, mem, p, tag, nb, next_gain):
    mem_n = _rms_fwd(mem, p["mem_norm"], name=f"{tag}_memnorm")
    q, = _matmul(hn, p["w_q"], outs=(BF16,), name=f"{tag}_q")
    kv, = _matmul(mem_n, p["w_kv_t"], tb=True, outs=(BF16,), name=f"{tag}_kv")
    q3, kv3 = _seq3(q, nb), _seq3(kv, nb)
    o, lse, ob = _attn_fwd((q3, 0), (kv3, 0), (kv3, 1), heads=XA_HEADS, dh=XA_HEAD_DIM, mode="full", bf16_copy=True,
                           name=f"{tag}_attn")
    o, ob = _flat2(o), _flat2(ob)
    out, hn_next = _residual_out(ob, p["w_o"], h, next_gain, name=f"{tag}_o")
    return out, hn_next, (h, hn, mem_n, q3, kv3, o, ob, lse)


def _xa_bwd(dh, dhb, mem, p, saved, tag, nb):
    h, hn, mem_n, q3, kv3, o, ob, lse = saved
    dob, delta = _matmul(dhb, p["w_o"], tb=True, outs=(BF16, (F32, LANES)), extras=(o,), consts=(_head_expand(XA_HEADS, D_MODEL).T,),
                         epilogue=lambda acc, o, e: (acc, _dot_exact(acc * o, e)), name=f"{tag}_bwd_do")
    dw_o, = _matmul(ob, dhb, ta=True, outs=(BF16,),name=f"{tag}_bwd_dwo")
    dq, dk, dv = _attn_bwd((q3, 0), (kv3, 0), (kv3, 1), (_seq3(dob, nb), 0), lse, _seq3(delta, nb), heads=XA_HEADS,
                           dh=XA_HEAD_DIM, mode="full", dq_dtype=BF16, name=f"{tag}_bwd_attn")
    dqb = _flat2(dq)
    dkvb, = _rowwise(lambda a, b: jnp.concatenate([a, b], axis=1), [_flat2(dk), _flat2(dv)], [], [(2 * D_MODEL, BF16)],
                     name=f"{tag}_bwd_dkvcast")
    dw_q, = _matmul(hn, dqb, ta=True, outs=(BF16,),name=f"{tag}_bwd_dwq")
    dw_kv_t, = _matmul(dkvb, mem_n, ta=True, outs=(BF16,),name=f"{tag}_bwd_dwkv")
    dmem_n, = _matmul(dkvb, p["w_kv_t"], name=f"{tag}_bwd_dmemn")
    dg_mem, = _rowwise(lambda x, d, g: _rms_bwd_vals(x, d, g)[1], [mem, dmem_n], [p["mem_norm"]], [], [(1, D_MODEL)],
                       name=f"{tag}_bwd_memnorm")
    dh2, dh2b, dg = _norm_input_grad(dqb, p["w_q"], h, dh, p["norm"], tb=True, name=f"{tag}_bwd_dh")
    return dh2, dh2b, {"norm": dg, "mem_norm": dg_mem, "w_q": dw_q, "w_kv_t": dw_kv_t, "w_o": dw_o}


AB_MAIN = 2 * LRU_WIDTH + 3 * ATT_W


def _ab_fwd(h, hn, p, nb, next_gain):
    w_in_t = p["w_in_t"]
    zug, = _matmul(hn, w_in_t[:2 * LRU_WIDTH], tb=True, name="ab_in_ug")
    qkv, = _matmul(hn, w_in_t[2 * LRU_WIDTH:AB_MAIN], tb=True, outs=(BF16,), tn=768, name="ab_in_qkv")
    zf, = _matmul(hn, w_in_t[AB_MAIN:], tb=True, name="ab_in_f")
    fl = zf[:, :8].reshape(nb, SEQ, 8).transpose(0, 2, 1)
    cbias = _fox_bias(fl, p["b_f"], name="ab_fox_bias")
    kb = cbias.reshape(nb, 8, SEQ // ATT_CHUNK, ATT_CHUNK)
    y_a, hl = _lru_fwd(zug, p["conv_w"], p["conv_b"], p["w_a"], p["b_a"], p["w_i"], p["b_i"], p["lam"], name="ab_lru")
    qkv3 = _seq3(qkv, nb)
    o, lse = _attn_fwd((qkv3, 0), (qkv3, 1), (qkv3, 2), kb, heads=8, dh=HEAD_DIM, mode="causal", name="ab_fox")
    o = _flat2(o)
    y, = _rowwise(lambda a, b: jnp.concatenate([a, b.astype(BF16)], axis=1), [y_a, o], [], [(D_MODEL, BF16)], name="ab_ycat")
    out, hn_next = _residual_out(y, p["w_out"], h, next_gain, name="ab_out")
    return out, hn_next, (h, hn, zug, qkv3, fl, kb, hl, o, lse, y)


def _ab_bwd(dh, dhb, p, saved, nb):
    h, hn, zug, qkv3, fl, kb, hl, o, lse, y = saved
    dy, dyb, delta = _matmul(dhb, p["w_out"], tb=True, outs=(F32, BF16, (F32, LANES)), extras=(y,), consts=(_head_expand(8, ATT_W).T,),
                             epilogue=lambda acc, y, e: (acc, acc, _dot_exact(acc[:, ATT_W:] * y[:, ATT_W:].astype(F32), e)),
                             name="ab_bwd_dy")
    dw_out, = _matmul(y, dhb, ta=True, outs=(BF16,),name="ab_bwd_dwout")
    dzug, dcw, dcb, dwa, dba, dwi, dbi, dlam = _lru_bwd(zug, hl, dy, p["conv_w"], p["conv_b"], p["w_a"], p["b_a"],
                                                        p["w_i"], p["b_i"], p["lam"], name="ab_bwd_lru")
    dq, dk, dv, dkb, drow = _attn_bwd((qkv3, 0), (qkv3, 1), (qkv3, 2), (_seq3(dyb, nb), 1), lse, _seq3(delta, nb), kb,
                                      heads=8, dh=HEAD_DIM, mode="causal", name="ab_bwd_fox")
    dcum = dkb.reshape(nb, 8, SEQ) + drow[:, :, :8].transpose(0, 2, 1)
    dfl, dbf = _fox_bias_bwd(fl, p["b_f"], dcum, name="ab_bwd_fox_bias")
    dzf = jnp.pad(dfl.transpose(0, 2, 1).reshape(nb * SEQ, 8), ((0, 0), (0, LANES - 8)))
    dz, = _rowwise(lambda a, q, k, v, f: jnp.concatenate([a, q.astype(BF16), k.astype(BF16), v.astype(BF16), f.astype(BF16)], axis=1),
                   [dzug, _flat2(dq), _flat2(dk), _flat2(dv), dzf], [], [(AB_MAIN + LANES, BF16)], name="ab_bwd_dzcat")
    dw_in_t, = _matmul(dz, hn, ta=True, outs=(BF16,),tm=384, name="ab_bwd_dwin")
    dh2, dh2b, dg = _norm_input_grad(dz, p["w_in_t"], h, dh, p["norm"], name="ab_bwd_dh")
    grads = {"norm": dg, "w_in_t": dw_in_t[:AB_MAIN + 8], "conv_w": dcw, "conv_b": dcb, "w_a": _diag_blocks(dwa), "b_a": dba,
             "w_i": _diag_blocks(dwi), "b_i": dbi, "lam": dlam, "b_f": dbf.reshape(1, 8), "w_out": dw_out}
    return dh2, dh2b, grads


CD_IN = 3 * ATT_W + ATT_W + 2 * SWA_KW


def _gqa_share():
    src = jnp.arange(SWA_KW)[:, None]
    dst = jnp.arange(ATT_W)[None, :]
    per_kv = ATT_W // (SWA_KW // HEAD_DIM)
    return ((dst // per_kv == src // HEAD_DIM) & (dst % HEAD_DIM == src % HEAD_DIM)).astype(BF16)


def _cd_fwd(h, hn, p, nb, next_gain):
    z, = _matmul(hn, p["w_in_t"], tb=True, tn=1152, name="cd_in")
    cos, sin = _rope_tables()
    per_seq = SEQ // ROW_TILE
    table_map = lambda i: (i % per_seq, 0)

    def rope_fn(z, cos, sin, share):
        qc, kc, vc = z[:, 0:ATT_W], z[:, ATT_W:2 * ATT_W], z[:, 2 * ATT_W:3 * ATT_W]
        qd, kd, vd = z[:, 3 * ATT_W:4 * ATT_W], z[:, 4 * ATT_W:4 * ATT_W + SWA_KW], z[:, 4 * ATT_W + SWA_KW:]
        kd8 = jnp.dot(_rotate(kd, cos, sin).astype(BF16), share, preferred_element_type=F32)
        vd8 = jnp.dot(vd.astype(BF16), share, preferred_element_type=F32)
        qk = jnp.concatenate([_rotate(qc, cos, sin), _rotate(kc, cos, sin)], axis=1)
        return qk, vc, _rotate(qd, cos, sin), kd8, vd8, qk, qk, vc, vc
    dils = [dil for _, dil in DIL_PATTERN if dil > 1]
    outs = _rowwise(rope_fn, [z, cos, sin], [_gqa_share()],
                    [(2 * ATT_W, BF16)] + [(ATT_W, BF16)] * 4 + [(2 * ATT_W, BF16, d) for d in dils] + [(ATT_W, BF16, d) for d in dils],
                    name="cd_rope", row_maps=[None, table_map, table_map])
    qk, vc, qd, kd, vd = outs[:5]
    views = {1: (_seq3(qk, nb), _seq3(vc, nb))}
    for d, qk_d, vc_d in zip(dils, outs[5:5 + len(dils)], outs[5 + len(dils):]):
        views[d] = (_seq3(qk_d, nb), _seq3(vc_d, nb))
    in_classes = lambda a, width, d: _flat2(a) if d == 1 else ("classes", _flat2(a), width, d)
    branch = []
    for window, dil in DIL_PATTERN:
        qk_v, vc_v = views[dil]
        o_i, lse_i = _attn_fwd((qk_v, 0), (qk_v, 1), (vc_v, 0), classes=dil, heads=8, dh=HEAD_DIM, mode="band",
                               max_dist=window // dil, name=f"cd_dil{dil}")
        branch.append((in_classes(o_i, ATT_W, dil), in_classes(lse_i, LANES, dil)))
    expand = _head_expand(8, ATT_W)

    def combine(o1, o2, o3, l1, l2, l3, e):
        m = jnp.maximum(jnp.maximum(l1, l2), l3)
        lt = m + jnp.log(jnp.exp(l1 - m) + jnp.exp(l2 - m) + jnp.exp(l3 - m))
        o = sum(_dot_exact(jnp.exp(l - lt), e) * o_ for o_, l in ((o1, l1), (o2, l2), (o3, l3)))
        return o, lt
    y_c, lse_c = _rowwise(combine, [b[0] for b in branch] + [b[1] for b in branch], [expand], [(ATT_W, F32), (LANES, F32)],
                          name="cd_dil_combine")
    qd3, kd3, vd3 = _seq3(qd, nb), _seq3(kd, nb), _seq3(vd, nb)
    o_d, lse_band = _attn_fwd((qd3, 0), (kd3, 0), (vd3, 0), heads=8, dh=HEAD_DIM, mode="band", max_dist=SWA_WINDOW - 1,
                              name="cd_swa")

    def sink_combine(o, l, e, sink):
        lt = jnp.maximum(l, sink) + jnp.log(1.0 + jnp.exp(-jnp.abs(l - sink)))
        return o * _dot_exact(jnp.exp(l - lt), e), lt
    y_d, lse_d = _rowwise(sink_combine, [_flat2(o_d), _flat2(lse_band)], [expand, p["sink"]], [(ATT_W, F32), (LANES, F32)],
                          name="cd_swa_sink")
    y, = _rowwise(lambda a, b: jnp.concatenate([a, b], axis=1), [y_c, y_d], [], [(D_MODEL, BF16)], name="cd_ycat")
    out, hn_next = _residual_out(y, p["w_out"], h, next_gain, name="cd_out")
    return out, hn_next, (h, hn, views, qd3, kd3, vd3, y_c, lse_c, y_d, lse_d, y)


def _cd_bwd(dh, dhb, p, saved, nb):
    h, hn, views, qd3, kd3, vd3, y_c, lse_c, y_d, lse_d, y = saved
    dy, dyb = _matmul(dhb, p["w_out"], tb=True, outs=(F32, BF16), epilogue=lambda acc: (acc, acc), name="cd_bwd_dy")
    dw_out, = _matmul(y, dhb, ta=True, outs=(BF16,),name="cd_bwd_dwout")
    dyb3 = _seq3(dyb, nb)
    dils = [dil for _, dil in DIL_PATTERN if dil > 1]
    gather = _head_expand(8, ATT_W).T

    def prepare(do, o, lse, e):
        delta = _dot_exact(do * o, e)
        return (delta,) + (do,) * len(dils) + (lse,) * len(dils) + (delta,) * len(dils)
    outs = _rowwise(prepare, [(dy, ATT_W, 0), y_c, lse_c], [gather],
                    [(LANES, F32)] + [(ATT_W, BF16, d) for d in dils] + [(LANES, F32, d) for d in dils] * 2, name="cd_bwd_delta_dil")
    seen = {1: ((dyb3, 0), _seq3(lse_c, nb), _seq3(outs[0], nb))}
    for j, d in enumerate(dils):
        seen[d] = ((_seq3(outs[1 + j], nb), 0), _seq3(outs[1 + len(dils) + j], nb), _seq3(outs[1 + 2 * len(dils) + j], nb))
    in_classes = lambda a, d: _flat2(a) if d == 1 else ("classes", _flat2(a), ATT_W, d)
    parts = []
    for window, dil in DIL_PATTERN:
        (qk_v, vc_v), (do_v, lse_v, delta_v) = views[dil], seen[dil]
        grads = _attn_bwd((qk_v, 0), (qk_v, 1), (vc_v, 0), do_v, lse_v, delta_v, classes=dil, heads=8, dh=HEAD_DIM, mode="band",
                          max_dist=window // dil, dq_dtype=BF16, name=f"cd_bwd_dil{dil}")
        parts.append([in_classes(a, dil) for a in grads])
    delta_d, dsink = _head_delta((dy, ATT_W, 1), y_d, 8, lse=lse_d, sink=p["sink"], name="cd_bwd_delta_swa")
    dqd, dkd, dvd = _attn_bwd((qd3, 0), (kd3, 0), (vd3, 0), (dyb3, 1), _seq3(lse_d, nb), _seq3(delta_d, nb), heads=8,
                              dh=HEAD_DIM, mode="band", max_dist=SWA_WINDOW - 1, dq_dtype=BF16, name="cd_bwd_swa")
    cos, sin = _rope_tables()
    per_seq = SEQ // ROW_TILE
    table_map = lambda i: (i % per_seq, 0)

    def unrope(q1, q2, q3, k1, k2, k3, v1, v2, v3, qd, kd8, vd8, cos, sin, gather):
        back = lambda x: _rotate(x.astype(F32), cos, -sin)
        kd, vd = _dot_exact(kd8, gather), _dot_exact(vd8, gather)
        return jnp.concatenate([back(q1 + q2 + q3), back(k1 + k2 + k3), v1 + v2 + v3, back(qd), back(kd), vd], axis=1)
    ins = [parts[b][t] for t in range(3) for b in range(3)] + [_flat2(dqd), _flat2(dkd), _flat2(dvd), cos, sin]
    dz, = _rowwise(unrope, ins, [_gqa_share().T], [(CD_IN, BF16)], name="cd_bwd_unrope",
                   row_maps=[None] * 12 + [table_map, table_map])
    dw_in_t, = _matmul(dz, hn, ta=True, outs=(BF16,),tm=384, name="cd_bwd_dwin")
    dh2, dh2b, dg = _norm_input_grad(dz, p["w_in_t"], h, dh, p["norm"], name="cd_bwd_dh")
    return dh2, dh2b, {"norm": dg, "w_in_t": dw_in_t, "sink": dsink[:, :8], "w_out": dw_out}


def _local_step(x, mem, target, w, complete=None, grads_ready=None):
    nb = x.shape[0]
    h = x.reshape(nb * SEQ, D_MODEL)
    mem2 = mem.reshape(nb * MEM_LEN, D_MODEL)
    tgt = target.reshape(nb * SEQ, D_MODEL)

    hn = _rms_fwd(h, w["ab"]["norm"], name="ab_norm")
    h, hn, s_ab = _ab_fwd(h, hn, w["ab"], nb, w["xa0"]["norm"])
    if complete is not None:
        complete(h)
    h, hn, s_xa0 = _xa_fwd(h, hn, mem2, w["xa0"], "xa0", nb, w["mlp0"]["norm"])
    h, hn, s_mlp0 = _mlp_fwd(h, hn, w["mlp0"], "mlp0", w["cd"]["norm"])
    h, hn, s_cd = _cd_fwd(h, hn, w["cd"], nb, w["xa1"]["norm"])
    h, hn, s_xa1 = _xa_fwd(h, hn, mem2, w["xa1"], "xa1", nb, w["mlp1"]["norm"])
    h, _, s_mlp1 = _mlp_fwd(h, hn, w["mlp1"], "mlp1", None)

    dh, dhb, loss, dg_final = _loss_and_grad(h, tgt, w["final_norm"], name="loss")
    grads = {"final_norm": dg_final}
    dh, dhb, grads["mlp1"] = _mlp_bwd(dh, dhb, w["mlp1"], s_mlp1, "mlp1")
    dh, dhb, grads["xa1"] = _xa_bwd(dh, dhb, mem2, w["xa1"], s_xa1, "xa1", nb)
    dh, dhb, grads["cd"] = _cd_bwd(dh, dhb, w["cd"], s_cd, nb)
    if grads_ready is not None:
        grads_ready(0, grads)
    dh, dhb, grads["mlp0"] = _mlp_bwd(dh, dhb, w["mlp0"], s_mlp0, "mlp0")
    dh, dhb, grads["xa0"] = _xa_bwd(dh, dhb, mem2, w["xa0"], s_xa0, "xa0", nb)
    if grads_ready is not None:
        grads_ready(1, grads)
    dh, dhb, grads["ab"] = _ab_bwd(dh, dhb, w["ab"], s_ab, nb)
    return loss, dh.reshape(nb, SEQ, D_MODEL), grads


ANY = pl.BlockSpec(memory_space=pl.ANY)


def _place():
    x, y, c = lax.axis_index("x"), lax.axis_index("y"), lax.axis_index("c")
    return x, y, c, [(1 - x, y), (x, 1 - y), (1 - x, 1 - y)]


def _gather_chips(shard):
    half = shard.shape[0] // 2

    def body(src, out, send_sems, recv_sems):
        x, y, c, chips = _place()
        sibling = (x, y, 1 - c)

        def rows(slot, core):
            return out.at[slot, pl.ds(core * half, half)]

        def copy(k, src_ref, dst_ref, to):
            return pltpu.make_async_remote_copy(src_ref=src_ref, dst_ref=dst_ref, send_sem=send_sems.at[k],
                                                recv_sem=recv_sems.at[k], device_id=to, device_id_type=MESH)

        first = [copy(k, src.at[pl.ds(c * half, half)], rows(2 * x + y, c), (px, py, c)) for k, (px, py) in enumerate(chips)]
        for cp in first:
            cp.start()
        passed = [copy(3 + k, rows(2 * px + py, c), rows(2 * px + py, c), sibling) for k, (px, py) in enumerate(chips)]
        for k, (px, py) in enumerate(chips):
            copy(k, src.at[pl.ds(c * half, half)], rows(2 * px + py, c), (px, py, c)).wait_recv()
            passed[k].start()
        for k, (px, py) in enumerate(chips):
            copy(3 + k, rows(2 * px + py, 1 - c), rows(2 * px + py, 1 - c), sibling).wait_recv()
        for cp in first + passed:
            cp.wait_send()

    return pl.pallas_call(
        body, out_shape=jax.ShapeDtypeStruct((N_CHIPS,) + shard.shape, shard.dtype), in_specs=[ANY], out_specs=ANY,
        scratch_shapes=[pltpu.SemaphoreType.DMA((6,)), pltpu.SemaphoreType.DMA((6,))], name="gather_chips",
    )(shard)


HBM = pl.BlockSpec(memory_space=pltpu.HBM)
SEM = pl.BlockSpec(memory_space=pltpu.SEMAPHORE)
SIDE_EFFECT = pltpu.SideEffectType.DATAFLOW_SIDE_EFFECTING


def _chip_copies(src, land, send_sems, recv_sems):
    half = src.shape[0] // 2
    x, y, c, chips = _place()

    def copy(k, slot, to):
        return pltpu.make_async_remote_copy(src_ref=src.at[pl.ds(c * half, half)], dst_ref=land.at[slot, pl.ds(c * half, half)],
                                            send_sem=send_sems[k], recv_sem=recv_sems[k], device_id=to, device_id_type=MESH)
    out = [copy(k, 2 * x + y, (px, py, c)) for k, (px, py) in enumerate(chips)]
    back = [copy(k, 2 * px + py, (px, py, c)) for k, (px, py) in enumerate(chips)]
    return out, back


def _gather_start(shard, after):
    def body(src, land, *rest):
        rest = rest[len(after):]
        sems, token = rest[:6], rest[8]
        out, _ = _chip_copies(src, land, sems[:3], sems[3:])
        for cp in out:
            cp.start()
        token[...] = jnp.zeros(token.shape, F32)

    sem = pltpu.SemaphoreType.DMA(())
    land_shape = (N_CHIPS,) + shard.shape
    return pl.pallas_call(
        body, name="gather_start",
        out_shape=(sem,) * 6 + (pltpu.HBM(shard.shape, shard.dtype), pltpu.HBM(land_shape, shard.dtype),
                                jax.ShapeDtypeStruct((8, LANES), F32)),
        in_specs=(HBM, HBM) + (pl.BlockSpec(memory_space=pl.ANY),) * len(after),
        out_specs=(SEM,) * 6 + (HBM, HBM, pl.BlockSpec(memory_space=pltpu.VMEM)),
        input_output_aliases={0: 6, 1: 7}, compiler_params=pltpu.CompilerParams(has_side_effects=SIDE_EFFECT),
    )(pltpu.with_memory_space_constraint(shard, pltpu.HBM),
      pltpu.with_memory_space_constraint(lax.empty(land_shape, shard.dtype), pltpu.HBM), *after)


def _gather_wait(started, after):
    sems, src_thru, land_thru = started[:6], started[6], started[7]

    def body(src, land, *rest):
        out, back = _chip_copies(src, land, rest[:3], rest[3:6])
        for cp in out:
            cp.wait_send()
        for cp in back:
            cp.wait_recv()

    return pl.pallas_call(
        body, name="gather_wait",
        out_shape=(pltpu.HBM(src_thru.shape, src_thru.dtype), pltpu.HBM(land_thru.shape, land_thru.dtype)),
        in_specs=(HBM, HBM) + (SEM,) * 6 + (pl.BlockSpec(memory_space=pl.ANY),), out_specs=(HBM, HBM),
        input_output_aliases={0: 0, 1: 1}, compiler_params=pltpu.CompilerParams(has_side_effects=SIDE_EFFECT),
    )(src_thru, land_thru, *sems, after)[1]


def _gather_pass(land):
    half = land.shape[1] // 2

    def body(land_in, land_out, send_sems, recv_sems):
        x, y, c, chips = _place()

        def copy(k, slot, core):
            rows = land_out.at[slot, pl.ds(core * half, half)]
            return pltpu.make_async_remote_copy(src_ref=rows, dst_ref=rows, send_sem=send_sems.at[k], recv_sem=recv_sems.at[k],
                                                device_id=(x, y, 1 - c), device_id_type=MESH)
        sends = [copy(k, 2 * px + py, c) for k, (px, py) in enumerate(chips)]
        for cp in sends:
            cp.start()
        for k, (px, py) in enumerate(chips):
            copy(k, 2 * px + py, 1 - c).wait_recv()
        for cp in sends:
            cp.wait_send()

    return pl.pallas_call(
        body, out_shape=jax.ShapeDtypeStruct(land.shape, land.dtype), in_specs=[ANY], out_specs=ANY,
        scratch_shapes=[pltpu.SemaphoreType.DMA((3,)), pltpu.SemaphoreType.DMA((3,))], input_output_aliases={0: 0},
        name="gather_pass",
    )(land)


def _swap_cores(block, *, name, half_rows=None):
    shape = block.shape if half_rows is None else (block.shape[0], half_rows, block.shape[2])

    def body(src, out, send_sem, recv_sem):
        x, y, c, _ = _place()
        part = src if half_rows is None else src.at[:, pl.ds((1 - c) * half_rows, half_rows)]
        cp = pltpu.make_async_remote_copy(src_ref=part, dst_ref=out, send_sem=send_sem, recv_sem=recv_sem,
                                          device_id=(x, y, 1 - c), device_id_type=MESH)
        cp.start()
        cp.wait()

    return pl.pallas_call(
        body, out_shape=jax.ShapeDtypeStruct(shape, block.dtype), in_specs=[ANY], out_specs=ANY,
        scratch_shapes=[pltpu.SemaphoreType.DMA(()), pltpu.SemaphoreType.DMA(())], name=name,
    )(block)


def _scatter_copies(parts, land, send_sems, recv_sems):
    x, y, c, chips = _place()
    return [pltpu.make_async_remote_copy(src_ref=parts.at[2 * px + py], dst_ref=land.at[k], send_sem=send_sems[k],
                                         recv_sem=recv_sems[k], device_id=(px, py, c), device_id_type=MESH)
            for k, (px, py) in enumerate(chips)]


def _scatter_start(parts, tag):
    def body(src, land, *rest):
        for cp in _scatter_copies(src, land, rest[:3], rest[3:6]):
            cp.start()
        rest[8][...] = jnp.zeros(rest[8].shape, F32)

    sem = pltpu.SemaphoreType.DMA(())
    land_shape = (3,) + parts.shape[1:]
    return pl.pallas_call(
        body, name=f"scatter_start_{tag}",
        out_shape=(sem,) * 6 + (pltpu.HBM(parts.shape, parts.dtype), pltpu.HBM(land_shape, parts.dtype),
                                jax.ShapeDtypeStruct((8, LANES), F32)),
        in_specs=(HBM, HBM), out_specs=(SEM,) * 6 + (HBM, HBM, pl.BlockSpec(memory_space=pltpu.VMEM)),
        input_output_aliases={0: 6, 1: 7}, compiler_params=pltpu.CompilerParams(has_side_effects=SIDE_EFFECT),
    )(pltpu.with_memory_space_constraint(parts, pltpu.HBM),
      pltpu.with_memory_space_constraint(lax.empty(land_shape, parts.dtype), pltpu.HBM))


def _scatter_wait(started, after, tag):
    def body(src, land, *rest):
        for cp in _scatter_copies(src, land, rest[:3], rest[3:6]):
            cp.wait_send()
            cp.wait_recv()

    src_thru, land_thru = started[6], started[7]
    return pl.pallas_call(
        body, name=f"scatter_wait_{tag}",
        out_shape=(pltpu.HBM(src_thru.shape, src_thru.dtype), pltpu.HBM(land_thru.shape, land_thru.dtype)),
        in_specs=(HBM, HBM) + (SEM,) * 6 + (pl.BlockSpec(memory_space=pl.ANY),), out_specs=(HBM, HBM),
        input_output_aliases={0: 0, 1: 1}, compiler_params=pltpu.CompilerParams(has_side_effects=SIDE_EFFECT),
    )(src_thru, land_thru, *started[:6], after)[1]


def _sum_all_devices(vec):
    rows = vec.shape[0]

    def body(x_ref, total_ref, all_ref, send_sems, recv_sems, local_sem):
        x, y, c, chips = _place()
        me, sibling = (x, y, c), (x, y, 1 - c)

        def slot(px, py, pc):
            return all_ref.at[4 * px + 2 * py + pc]

        def copy(k, block, to, src=None):
            return pltpu.make_async_remote_copy(src_ref=slot(*block) if src is None else src, dst_ref=slot(*block),
                                                send_sem=send_sems.at[k], recv_sem=recv_sems.at[k], device_id=to,
                                                device_id_type=MESH)

        mine = pltpu.make_async_copy(x_ref, slot(*me), local_sem)
        mine.start()
        first = [copy(0, me, sibling, src=x_ref)] + [copy(1 + j, me, (*chip, c), src=x_ref) for j, chip in enumerate(chips)]
        for cp in first:
            cp.start()
        passed = [copy(4 + j, (*chip, c), sibling) for j, chip in enumerate(chips)]
        for j, chip in enumerate(chips):
            copy(1 + j, (*chip, c), me).wait_recv()
            passed[j].start()
        copy(0, sibling, me).wait_recv()
        for j, chip in enumerate(chips):
            copy(4 + j, (*chip, 1 - c), me).wait_recv()
        for cp in first + passed:
            cp.wait_send()
        mine.wait()
        acc = all_ref[0]
        for d in range(1, 8):
            acc = acc + all_ref[d]
        total_ref[...] = acc

    vmem = pl.BlockSpec(memory_space=pltpu.VMEM)
    return pl.pallas_call(
        body, out_shape=[jax.ShapeDtypeStruct((rows, LANES), F32), jax.ShapeDtypeStruct((8, rows, LANES), F32)],
        in_specs=[vmem], out_specs=[vmem, vmem],
        scratch_shapes=[pltpu.SemaphoreType.DMA((7,)), pltpu.SemaphoreType.DMA((7,)), pltpu.SemaphoreType.DMA(())],
        name="sum_all_devices", compiler_params=pltpu.CompilerParams(vmem_limit_bytes=VMEM_LIMIT_BYTES),
    )(vec)[0]


PACK_COLS = 1024
BIG = (("mlp_w_up", 2, 1024, True), ("mlp_w_down", 2, 1024, False), ("xa_w_kv", 2, 512, True), ("xa_w_q", 2, 256, False),
       ("xa_w_o", 2, 256, False), ("ab_w_out", 1, 256, False), ("cd_w_out", 1, 256, False), ("cd_w_in", 1, 576, True),
       ("ab_w_in", 1, 642, True))
ENTRIES = tuple((name, layer, rows, transposed) for name, layers, rows, transposed in BIG for layer in range(layers))
FIRST_ENTRIES = tuple(e for e in ENTRIES if e[0].startswith("ab_"))
LATER_ENTRIES = tuple(e for e in ENTRIES if not e[0].startswith("ab_"))


def _tile_rows(entry):
    return -(-entry[2] // 16) * 16


def _padded_rows(entries):
    return -(-sum(_tile_rows(e) for e in entries) // 32) * 32


SMALL = (("ab_norm", (1, 1024)), ("ab_conv_w", (1, 4, 512)), ("ab_conv_b", (1, 512)), ("lru_w_a", (1, 8, 64, 64)),
         ("lru_b_a", (1, 512)), ("lru_w_i", (1, 8, 64, 64)), ("lru_b_i", (1, 512)), ("lru_lambda", (1, 512)),
         ("fox_b_f", (1, 8)), ("cd_norm", (1, 1024)), ("cd_sink", (1, 8)), ("xa_norm", (2, 1024)),
         ("xa_mem_norm", (2, 1024)), ("mlp_norm", (2, 1024)), ("final_norm", (1024,)), ("loss", (1,)))
SPLIT_SMALL = ("ab_conv_w", "cd_norm")


def _pack_rows(parts, entries, dtype):
    lead = parts[0].shape[:-2]
    zeros = lambda rows: jnp.zeros(lead + (rows, PACK_COLS), dtype)
    pieces = [jnp.pad(p.astype(dtype), ((0, 0),) * len(lead) + ((0, _tile_rows(e) - e[2]), (0, 0))) for p, e in zip(parts, entries)]
    tail = _padded_rows(entries) - sum(_tile_rows(e) for e in entries)
    return jnp.concatenate(pieces + ([zeros(tail)] if tail else []), axis=len(lead))


def _unpack_rows(packed, entries):
    out, r0 = [], 0
    for e in entries:
        out.append(packed[..., r0:r0 + e[2], :])
        r0 += _tile_rows(e)
    return out


def _shard_rows(w, entries):
    return [(w[name][layer].T if transposed else w[name][layer]) for name, layer, _, transposed in entries]


def _shard_blocks(rows):
    out = {}
    for (name, layer, _, transposed), r in zip(ENTRIES, rows):
        out.setdefault(name, []).append(r.T if transposed else r)
    return {name: jnp.stack(layers) for name, layers in out.items()}


def _pack_small(vals):
    flat = jnp.concatenate([vals[name].astype(F32).reshape(-1) for name, _ in SMALL])
    size = -(-flat.shape[0] // (8 * LANES)) * (8 * LANES)
    return jnp.pad(flat, (0, size - flat.shape[0])).reshape(-1, LANES)


def _unpack_small(packed):
    flat, out, at = packed.reshape(-1), {}, 0
    for name, shape in SMALL:
        out[name] = flat[at:at + math.prod(shape)].reshape(shape)
        at += math.prod(shape)
    return out


def _chip_part(full, chip):
    width = full.shape[-1] // N_CHIPS
    return lax.dynamic_slice_in_dim(full, chip * width, width, axis=full.ndim - 1)


def _chip_embed(part, chip):
    full = jnp.zeros(part.shape[:-1] + (part.shape[-1] * N_CHIPS,), part.dtype)
    return lax.dynamic_update_slice_in_dim(full, part, chip * part.shape[-1], axis=part.ndim - 1)


SUBLAYER_KEYS = {"ab_w_in": ("ab", "w_in_t"), "ab_w_out": ("ab", "w_out"), "cd_w_in": ("cd", "w_in_t"), "cd_w_out": ("cd", "w_out"),
                 "xa_w_q": ("xa", "w_q"), "xa_w_kv": ("xa", "w_kv_t"), "xa_w_o": ("xa", "w_o"), "mlp_w_up": ("mlp", "w_up_t"),
                 "mlp_w_down": ("mlp", "w_down")}


def _sublayer(name, layer):
    block, leaf = SUBLAYER_KEYS[name]
    return (block if block in ("ab", "cd") else f"{block}{layer}"), leaf


def _set_big(params, full_rows):
    for (name, layer), rows in full_rows.items():
        block, leaf = _sublayer(name, layer)
        if name == "ab_w_in":
            rows = jnp.concatenate([rows, jnp.zeros((LANES - 8, PACK_COLS), rows.dtype)])
        params[block][leaf] = rows


def _build_params(full_rows, w):
    pad = lambda a: jnp.pad(a, ((0, 0), (0, LANES - a.shape[1])))
    params = {
        "ab": dict(norm=w["ab_norm"], conv_w=w["ab_conv_w"][0], conv_b=w["ab_conv_b"], w_a=_block_diag(w["lru_w_a"][0]),
                   b_a=w["lru_b_a"], w_i=_block_diag(w["lru_w_i"][0]), b_i=w["lru_b_i"], lam=w["lru_lambda"],
                   b_f=w["fox_b_f"].reshape(8, 1)),
        "cd": dict(norm=w["cd_norm"], sink=pad(w["cd_sink"])),
        "final_norm": w["final_norm"].reshape(1, D_MODEL),
    }
    for layer in range(2):
        params[f"xa{layer}"] = dict(norm=w["xa_norm"][layer:layer + 1], mem_norm=w["xa_mem_norm"][layer:layer + 1])
        params[f"mlp{layer}"] = dict(norm=w["mlp_norm"][layer:layer + 1])
    _set_big(params, full_rows)
    return params


def _grad_rows(g, entries=ENTRIES):
    out = []
    for name, layer, _, _ in entries:
        block, leaf = _sublayer(name, layer)
        out.append(g[block][leaf])
    return out


def _reduce_group(entry):
    name, layer = entry[0], entry[1]
    if name.startswith("ab_"):
        return 2
    return 0 if layer == 1 or name.startswith("cd_") else 1


REDUCE_GROUPS = tuple(tuple(e for e in ENTRIES if _reduce_group(e) == group) for group in range(3))


def _reduce_tile(half):
    return max(t for t in range(16, 1025, 16) if half % t == 0)


def _reduce_start(grads_by_chip, core, chip, tag):
    half = grads_by_chip.shape[1] // 2
    tile = _reduce_tile(half)
    steps = half // tile
    place = jnp.stack([core, chip]).astype(jnp.int32)
    got = _swap_cores(grads_by_chip, name=f"swap_cores_{tag}", half_rows=half)
    block = (1, tile, PACK_COLS)

    def sum_cores(place_ref, mine_ref, got_ref, f32_ref, bf16_ref):
        total = mine_ref[...].astype(F32) + got_ref[...].astype(F32)
        f32_ref[...] = total
        bf16_ref[...] = total.astype(BF16)

    pair32, pair16 = pl.pallas_call(
        sum_cores, name=f"sum_cores_{tag}",
        grid_spec=pltpu.PrefetchScalarGridSpec(
            num_scalar_prefetch=1, grid=(N_CHIPS, steps),
            in_specs=[pl.BlockSpec(block, lambda s, i, place_ref: (s, place_ref[0] * steps + i, 0)),
                      pl.BlockSpec(block, lambda s, i, place_ref: (s, i, 0))],
            out_specs=[pl.BlockSpec(block, lambda s, i, place_ref: (s, i, 0))] * 2),
        out_shape=[jax.ShapeDtypeStruct((N_CHIPS, half, PACK_COLS), F32), jax.ShapeDtypeStruct((N_CHIPS, half, PACK_COLS), BF16)],
        compiler_params=_params("arbitrary", "arbitrary"),
    )(place, grads_by_chip, got)
    return pair32, _scatter_start(pair16, tag), place


def _reduce_finish(state, core, tag, after):
    pair32, started, place = state
    half = pair32.shape[1]
    tile = _reduce_tile(half)
    landed = _scatter_wait(started, after, tag)

    def sum_chips(place_ref, own_ref, landed_ref, out_ref):
        out_ref[...] = own_ref[0] + landed_ref[0].astype(F32) + landed_ref[1].astype(F32) + landed_ref[2].astype(F32)

    done = pl.pallas_call(
        sum_chips, name=f"sum_chips_{tag}",
        grid_spec=pltpu.PrefetchScalarGridSpec(
            num_scalar_prefetch=1, grid=(half // tile,),
            in_specs=[pl.BlockSpec((1, tile, PACK_COLS), lambda i, place_ref: (place_ref[1], i, 0)),
                      pl.BlockSpec((3, tile, PACK_COLS), lambda i, place_ref: (0, i, 0))],
            out_specs=pl.BlockSpec((tile, PACK_COLS), lambda i, place_ref: (i, 0))),
        out_shape=jax.ShapeDtypeStruct((half, PACK_COLS), F32), compiler_params=_params("arbitrary"),
    )(place, pair32, landed)
    theirs = _swap_cores(done, name=f"share_halves_{tag}")
    return jnp.where(core == 0, jnp.concatenate([done, theirs]), jnp.concatenate([theirs, done]))


def kernel(x, mem, ab_norm, ab_w_in, ab_conv_w, ab_conv_b, lru_w_a, lru_b_a, lru_w_i, lru_b_i, lru_lambda, fox_b_f, ab_w_out, cd_norm, cd_w_in, cd_sink, cd_w_out, xa_norm, xa_mem_norm, xa_w_q, xa_w_kv, xa_w_o, mlp_norm, mlp_w_up, mlp_w_down, final_norm, loss_target, m_ab_norm, m_ab_w_in, m_ab_conv_w, m_ab_conv_b, m_lru_w_a, m_lru_b_a, m_lru_w_i, m_lru_b_i, m_lru_lambda, m_fox_b_f, m_ab_w_out, m_cd_norm, m_cd_w_in, m_cd_sink, m_cd_w_out, m_xa_norm, m_xa_mem_norm, m_xa_w_q, m_xa_w_kv, m_xa_w_o, m_mlp_norm, m_mlp_w_up, m_mlp_w_down, m_final_norm, v_ab_norm, v_ab_w_in, v_ab_conv_w, v_ab_conv_b, v_lru_w_a, v_lru_b_a, v_lru_w_i, v_lru_b_i, v_lru_lambda, v_fox_b_f, v_ab_w_out, v_cd_norm, v_cd_w_in, v_cd_sink, v_cd_w_out, v_xa_norm, v_xa_mem_norm, v_xa_w_q, v_xa_w_kv, v_xa_w_o, v_mlp_norm, v_mlp_w_up, v_mlp_w_down, v_final_norm):
    given = dict(locals())
    weight_names = [name for name, _ in SMALL if name != "loss"] + [name for name, _, _, _ in BIG]
    w = {name: given[name] for name in weight_names}
    chip = 2 * lax.axis_index("x") + lax.axis_index("y")
    core = lax.axis_index("c")

    slot = lax.broadcasted_iota(jnp.int32, (N_CHIPS, 1, 1), 0)

    def whole(entries, mine, gathered):
        return {(name, layer): jnp.where(slot == chip, own[None], got).reshape(N_CHIPS * rows, PACK_COLS)
                for (name, layer, rows, _), own, got in zip(entries, _unpack_rows(mine, entries), _unpack_rows(gathered, entries))}

    mine_first = _pack_rows(_shard_rows(w, FIRST_ENTRIES), FIRST_ENTRIES, BF16)
    mine_later = _pack_rows(_shard_rows(w, LATER_ENTRIES), LATER_ENTRIES, BF16)
    first = _gather_chips(mine_first)
    owner = (core == 0).astype(F32)
    quarters = {name: _chip_embed(w[name], chip) * owner for name in SPLIT_SMALL}
    zeros = {name: jnp.zeros(shape, F32) for name, shape in SMALL}
    small_packed = _sum_all_devices(_pack_small({**zeros, **quarters}))
    small_full = _unpack_small(small_packed)
    started = _gather_start(mine_later, after=(small_packed, first))
    params = _build_params(whole(FIRST_ENTRIES, mine_first, first),
                           {**w, "ab_conv_w": small_full["ab_conv_w"], "cd_norm": small_full["cd_norm"]})
    params["ab"]["norm"] = params["ab"]["norm"] + started[8][0, 0]

    def complete(h):
        _set_big(params, whole(LATER_ENTRIES, mine_later, _gather_pass(_gather_wait(started, after=h))))

    reducing = {}

    def grads_ready(group, g):
        entries = REDUCE_GROUPS[group]
        by_chip = _pack_rows([r.reshape(N_CHIPS, e[2], PACK_COLS) for r, e in zip(_grad_rows(g, entries), entries)], entries, BF16)
        reducing[group] = _reduce_start(by_chip, core, chip, f"g{group}")
        if group < 2:
            block, leaf = ("mlp0", "w_down") if group == 0 else ("ab", "w_out")
            params[block][leaf] = params[block][leaf] + reducing[group][1][8][0, 0].astype(BF16)

    loss_part, grad_x, g = _local_step(x, mem, loss_target, params, complete, grads_ready)
    grads_ready(2, g)
    reduced = {}
    for group, entries in enumerate(REDUCE_GROUPS):
        rows = _unpack_rows(_reduce_finish(reducing[group], core, f"g{group}", after=grad_x), entries)
        reduced.update({(e[0], e[1]): r for e, r in zip(entries, rows)})
    grads = _shard_blocks([reduced[e[0], e[1]] for e in ENTRIES])
    pair = lambda key, leaf: jnp.stack([g[f"{key}0"][leaf], g[f"{key}1"][leaf]])

    small_local = {"ab_norm": g["ab"]["norm"], "ab_conv_w": g["ab"]["conv_w"], "ab_conv_b": g["ab"]["conv_b"],
                   "lru_w_a": g["ab"]["w_a"], "lru_b_a": g["ab"]["b_a"], "lru_w_i": g["ab"]["w_i"], "lru_b_i": g["ab"]["b_i"],
                   "lru_lambda": g["ab"]["lam"], "fox_b_f": g["ab"]["b_f"], "cd_norm": g["cd"]["norm"], "cd_sink": g["cd"]["sink"],
                   "xa_norm": pair("xa", "norm"), "xa_mem_norm": pair("xa", "mem_norm"), "mlp_norm": pair("mlp", "norm"),
                   "final_norm": g["final_norm"], "loss": loss_part[0, :1]}
    small_sum_packed = _sum_all_devices(_pack_small(small_local))
    small_sum = _unpack_small(small_sum_packed)
    loss = small_sum["loss"][0]

    delta, new_m, new_v = {}, {}, {}
    for name, _, _, _ in BIG:
        shape = w[name].shape
        flat = lambda a: a.reshape(-1, shape[-1])
        d, m2, v2 = _adamw(flat(w[name]), flat(grads[name]), flat(given["m_" + name]), flat(given["v_" + name]), name=f"adamw_{name}")
        delta[name], new_m[name], new_v[name] = d.reshape(shape), m2.reshape(shape), v2.reshape(shape)

    def small_state(prefix):
        vals = {name: (given[prefix + name] if name != "loss" else jnp.zeros((1,), F32)) for name, _ in SMALL}
        for name in SPLIT_SMALL:
            vals[name] = _chip_embed(vals[name], chip)
        return _pack_small(vals)
    packed = _adamw(small_state(""), small_sum_packed, small_state("m_"), small_state("v_"), name="adamw_small")
    for store, vals in zip((delta, new_m, new_v), packed):
        for name, val in _unpack_small(vals).items():
            store[name] = _chip_part(val, chip) if name in SPLIT_SMALL else val
    for name in SPLIT_SMALL:
        small_sum[name] = _chip_part(small_sum[name], chip)
    grads.update({name: small_sum[name] for name, _ in SMALL})

    order = ["ab_norm", "ab_w_in", "ab_conv_w", "ab_conv_b", "lru_w_a", "lru_b_a", "lru_w_i", "lru_b_i", "lru_lambda", "fox_b_f",
             "ab_w_out", "cd_norm", "cd_w_in", "cd_sink", "cd_w_out", "xa_norm", "xa_mem_norm", "xa_w_q", "xa_w_kv", "xa_w_o",
             "mlp_norm", "mlp_w_up", "mlp_w_down", "final_norm"]
    return (loss, grad_x, *[grads[n] for n in order], *[delta[n] for n in order], *[new_m[n] for n in order],
            *[new_v[n] for n in order])
```

```python
import functools
import math

import jax
import jax.numpy as jnp
from jax import lax
from jax.experimental import pallas as pl
from jax.experimental.pallas import tpu as pltpu

F32 = jnp.float32
BF16 = jnp.bfloat16
MESH = pl.DeviceIdType.MESH

D_MODEL = 1024
SEQ = 2048
HEAD_DIM = 64
LRU_WIDTH = 512
LRU_BLOCKS = 8
LRU_C = 8.0
CONV_WIDTH = 4
ATT_W = 512
SWA_KW = 128
SWA_WINDOW = 128
DIL_PATTERN = ((128, 1), (512, 4), (2048, 16))
MEM_LEN = 256
XA_HEADS = 4
XA_HEAD_DIM = 256
D_FF = 4096
ROPE_THETA = 10000.0
EPS = 1e-6
N_CHIPS = 4
LANES = 128

ADAM_LR, ADAM_B1, ADAM_B2, ADAM_EPS, ADAM_WD, ADAM_STEP = 0.001, 0.9, 0.999, 1e-08, 0.01, 10

VMEM_LIMIT_BYTES = 56 * 1024 * 1024
MASKED = -1e30
ROW_TILE = 512
LRU_CHUNK = 512
ATT_BLOCK = 128
BAND_ROWS = 256
ATT_CHUNK = 512
CAUSAL_ROWS = 256
CLASS_ROWS = 512


def _params(*sem):
    return pltpu.CompilerParams(dimension_semantics=sem, vmem_limit_bytes=VMEM_LIMIT_BYTES)


def _rowwise(fn, rows, consts=(), out_rows=(), out_accs=(), *, name, tile=ROW_TILE, row_maps=None):
    rows = [r if isinstance(r, tuple) else (r, r.shape[1], 0) for r in rows]
    consts = list(consts)
    nr, nc, no = len(rows), len(consts), len(out_rows)
    dealt_in = [r[3] if isinstance(r[0], str) else None for r in rows]
    dealt_out = [o[2] if len(o) == 3 else None for o in out_rows]
    total = next(r[0].shape[0] for r in rows if not isinstance(r[0], str))
    tile = min(tile, total)
    assert total % tile == 0
    n = total // tile
    n_acc = len(out_accs)

    def body(*refs):
        scratch = list(refs[nr + nc + no + n_acc:])
        vals = []
        for ref, d, row in zip(refs[:nr], dealt_in, rows):
            if d is None:
                vals.append(ref[...])
                continue
            sc, width = scratch.pop(0), row[2]
            for r in range(d):
                for c in range(width // LANES):
                    lanes = slice(r * width + c * LANES, r * width + (c + 1) * LANES)
                    sc.at[c][pl.ds(r, tile // d, stride=d), :] = ref[:, lanes].astype(F32)
            vals.append(jnp.concatenate([sc[c] for c in range(width // LANES)], axis=1))
        outs = fn(*vals, *[r[...] for r in refs[nr:nr + nc]])
        outs = tuple(outs) if isinstance(outs, (tuple, list)) else (outs,)
        for ref, val, d, spec in zip(refs[nr + nc:nr + nc + no], outs[:no], dealt_out, out_rows):
            if d is None:
                ref[...] = val.astype(ref.dtype)
                continue
            sc, width = scratch.pop(0), spec[0]
            for c in range(width // LANES):
                sc[c] = val[:, c * LANES:(c + 1) * LANES].astype(F32)
            for r in range(d):
                for c in range(width // LANES):
                    lanes = slice(r * width + c * LANES, r * width + (c + 1) * LANES)
                    ref[:, lanes] = sc.at[c][pl.ds(r, tile // d, stride=d), :].astype(ref.dtype)
        for ref, val in zip(refs[nr + nc + no:nr + nc + no + n_acc], outs[no:]):
            @pl.when(pl.program_id(0) == 0)
            def _(ref=ref):
                ref[...] = jnp.zeros(ref.shape, ref.dtype)
            ref[...] += val

    in_specs, args, scratch_shapes = [], [], []
    for idx, row in enumerate(rows):
        if isinstance(row[0], str):
            _, arr, width, d = row
            in_specs.append(pl.BlockSpec((tile // d, d * width), lambda i: (i, 0)))
            scratch_shapes.append(pltpu.VMEM((width // LANES, tile, LANES), F32))
        elif row_maps is not None and row_maps[idx] is not None:
            arr, width, _ = row
            in_specs.append(pl.BlockSpec((tile, width), row_maps[idx]))
        else:
            arr, width, cb = row
            in_specs.append(pl.BlockSpec((tile, width), functools.partial(lambda i, cb: (i, cb), cb=cb)))
        args.append(arr)
    in_specs += [pl.BlockSpec(c.shape, lambda i: (0, 0)) for c in consts]
    out_shape, out_specs = [], []
    for spec, d in zip(out_rows, dealt_out):
        w, dt = spec[0], spec[1]
        if d is None:
            out_shape.append(jax.ShapeDtypeStruct((total, w), dt))
            out_specs.append(pl.BlockSpec((tile, w), lambda i: (i, 0)))
        else:
            out_shape.append(jax.ShapeDtypeStruct((total // d, d * w), dt))
            out_specs.append(pl.BlockSpec((tile // d, d * w), lambda i: (i, 0)))
            scratch_shapes.append(pltpu.VMEM((w // LANES, tile, LANES), F32))
    out_shape += [jax.ShapeDtypeStruct(s, F32) for s in out_accs]
    out_specs += [pl.BlockSpec(s, lambda i: (0, 0)) for s in out_accs]
    return pl.pallas_call(
        body, grid=(n,), in_specs=in_specs, out_specs=out_specs, out_shape=out_shape, scratch_shapes=scratch_shapes, name=name,
        compiler_params=_params("arbitrary"),
    )(*args, *consts)


MATMUL_VMEM_BYTES = 40 * 1024 * 1024


def _matmul_tiles(m, n, k, tm, tn, out_bytes):
    tm, tn, tk = min(tm, m), min(tn, n), k

    def need():
        return 2 * (2 * tk * (tm + tn) + tm * tn * out_bytes) + (0 if tk == k else 4 * tm * tn)

    while need() > MATMUL_VMEM_BYTES:
        if tm >= tn and tm % 256 == 0:
            tm //= 2
        elif tn % 256 == 0:
            tn //= 2
        else:
            tk //= 2
    return tm, tn, tk


def _matmul(a, b, *, ta=False, tb=False, outs=(F32,), epilogue=None, extras=(), consts=(), sums=(), whole_rows=False,
            tm=1024, tn=1024, name):
    m, k = (a.shape[1], a.shape[0]) if ta else a.shape
    n = b.shape[0] if tb else b.shape[1]
    assert (b.shape[1] if tb else b.shape[0]) == k
    outs = [o if isinstance(o, tuple) else (o, None) for o in outs]
    out_bytes = sum(jnp.dtype(dt).itemsize for dt, w in outs if w is None) + sum(e.dtype.itemsize for e in extras)
    tm, tn, tk = _matmul_tiles(m, n, k, tm, tn, out_bytes)
    assert m % tm == 0 and n % tn == 0 and k % tk == 0, (name, m, n, k)
    nk = k // tk
    ne, nc, no = len(extras), len(consts), len(outs)
    assert tn == n or not (sums or whole_rows or any(w is not None for _, w in outs)), name
    dims = (((0 if ta else 1,), (1 if tb else 0,)), ((), ()))

    def body(*refs):
        a_ref, b_ref = refs[:2]
        e_refs, o_refs = refs[2:2 + ne + nc], refs[2 + ne + nc:2 + ne + nc + no]
        s_refs = refs[2 + ne + nc + no:2 + ne + nc + no + len(sums)]

        def finish(acc):
            vals = (acc,) if epilogue is None else epilogue(acc, *[e[...] for e in e_refs])
            for ref, val in zip(o_refs, vals[:no]):
                ref[...] = val.astype(ref.dtype)
            for ref, val in zip(s_refs, vals[no:]):
                @pl.when(pl.program_id(0) == 0)
                def _(ref=ref, val=val):
                    ref[...] = val

                @pl.when(pl.program_id(0) > 0)
                def _(ref=ref, val=val):
                    ref[...] += val

        prod = lax.dot_general(a_ref[...], b_ref[...], dims, preferred_element_type=F32)
        if nk == 1:
            finish(prod)
        else:
            acc_ref = refs[-1]
            step = pl.program_id(2)

            @pl.when(step == 0)
            def _():
                acc_ref[...] = prod

            @pl.when(step > 0)
            def _():
                acc_ref[...] += prod

            @pl.when(step == nk - 1)
            def _():
                finish(acc_ref[...])

    a_spec = pl.BlockSpec((tk, tm), lambda i, j, s: (s, i)) if ta else pl.BlockSpec((tm, tk), lambda i, j, s: (i, s))
    b_spec = pl.BlockSpec((tn, tk), lambda i, j, s: (j, s)) if tb else pl.BlockSpec((tk, tn), lambda i, j, s: (s, j))
    tile_spec = pl.BlockSpec((tm, tn), lambda i, j, s: (i, j))
    whole = lambda shape: pl.BlockSpec(shape, lambda i, j, s: (0, 0))
    return pl.pallas_call(
        body, grid=(m // tm, n // tn, nk),
        in_specs=[a_spec, b_spec] + [tile_spec] * ne + [whole(c.shape) for c in consts],
        out_specs=[tile_spec if w is None else pl.BlockSpec((tm, w), lambda i, j, s: (i, 0)) for _, w in outs]
        + [whole(shape) for shape in sums],
        out_shape=[jax.ShapeDtypeStruct((m, n if w is None else w), dt) for dt, w in outs]
        + [jax.ShapeDtypeStruct(shape, F32) for shape in sums],
        scratch_shapes=[pltpu.VMEM((tm, tn), F32)] if nk > 1 else [],
        name=name, compiler_params=_params("arbitrary" if sums else "parallel", "parallel", "arbitrary"),
    )(a, b, *extras, *consts)


def _split3(x):
    x1 = x.astype(BF16)
    r1 = x - x1.astype(F32)
    x2 = r1.astype(BF16)
    x3 = (r1 - x2.astype(F32)).astype(BF16)
    return x1, x2, x3


def _dot_exact(x, e):
    return sum(jnp.dot(p, e, preferred_element_type=F32) for p in _split3(x))


def _head_expand(heads, width):
    dh = width // heads
    rows = jnp.arange(LANES)[:, None]
    cols = jnp.arange(width)[None, :]
    return (cols // dh == rows).astype(BF16)


def _rstd(x):
    return lax.rsqrt(jnp.mean(x * x, axis=-1, keepdims=True) + EPS)


def _rms_fwd(x, gain, *, name):
    return _rowwise(lambda x, g: x * _rstd(x) * g, [x], [gain], [(x.shape[1], BF16)], name=name)[0]


def _rms_bwd_vals(x, dhn, gain):
    r = _rstd(x)
    xh = x * r
    dxh = dhn * gain
    dx = r * (dxh - xh * jnp.mean(dxh * xh, axis=-1, keepdims=True))
    return dx, jnp.sum(dhn * xh, axis=0, keepdims=True)


def _norm_input_grad(dz, w, x, dres, gain, *, tb=False, name):
    def epilogue(dhn, x, dres, g):
        dx, dg = _rms_bwd_vals(x, dhn, g)
        dh = dres + dx
        return dh, dh, dg
    return _matmul(dz, w, tb=tb, outs=(F32, BF16), extras=(x, dres), consts=(gain,), sums=((1, x.shape[1]),), epilogue=epilogue,
                   name=name)


def _loss_and_grad(h, target, gain, *, name):
    def fn(h, tgt, g):
        r = _rstd(h)
        err = h * r * g - tgt
        loss = 0.5 * jnp.sum(jnp.mean(err * err, axis=-1, keepdims=True), axis=0, keepdims=True)
        dx, dg = _rms_bwd_vals(h, err * (1.0 / D_MODEL), g)
        return dx, dx, jnp.broadcast_to(loss, (1, LANES)), dg
    d = h.shape[1]
    return _rowwise(fn, [h, target], [gain], [(d, F32), (d, BF16)], [(1, LANES), (1, d)], name=name)


def _adamw(w, g, m, v, *, name):
    rows, cols = w.shape
    tile = rows
    for cand in (512, 256, 128, 64, 32, 16, 8):
        if rows % cand == 0 and rows > cand:
            tile = cand
            break
    bc1 = 1.0 - ADAM_B1 ** ADAM_STEP
    bc2 = 1.0 - ADAM_B2 ** ADAM_STEP

    def fn(w, g, m, v):
        m2 = ADAM_B1 * m + (1.0 - ADAM_B1) * g
        v2 = ADAM_B2 * v + (1.0 - ADAM_B2) * (g * g)
        delta = -ADAM_LR * ((m2 / bc1) / (jnp.sqrt(v2 / bc2) + ADAM_EPS) + ADAM_WD * w)
        return delta, m2, v2
    return _rowwise(fn, [w, g, m, v], [], [(cols, F32)] * 3, name=name, tile=tile)


def _attn_specs(arr, width, cb, classes, rows_block, whole, together=1):
    ncols = arr.shape[2] // (classes * width)
    lanes, at = (width, lambda r: r * ncols + cb) if together == 1 else (together * ncols * width, lambda r: r)
    if whole:
        return pl.BlockSpec((1, arr.shape[1], lanes), lambda n, r, i: (n, 0, at(r)))
    return pl.BlockSpec((1, rows_block, lanes), lambda n, r, i: (n, i, at(r)))


def _class_window(refs, spans, together, cl):
    if together == 1:
        return refs
    return [ref.at[:, :, pl.ds((cl * ncols + cb) * width, width)] for ref, (ncols, cb, width) in zip(refs, spans)]


def _attn_tiles(mode, lq, lk, backward=False):
    if mode == "full":
        return min(lq, 256), min(lq, 256), lk
    if mode == "band":
        tq = min(BAND_ROWS, lq)
        rows = tq if backward else ATT_BLOCK
        return tq, rows, min(rows + ATT_BLOCK, lk)
    return CAUSAL_ROWS, CAUSAL_ROWS, ATT_CHUNK


def _window(mode, row0, j, rows, win, lk):
    if mode == "causal":
        return pl.multiple_of(j * win, win), row0 // win + 1
    if mode == "band":
        return pl.multiple_of(jnp.clip(row0 + rows - win, 0, lk - win), ATT_BLOCK), 1
    return 0, 1


def _allowed(mode, row0, start, rows, win, max_dist):
    if mode == "full":
        return None
    dist = (row0 + lax.broadcasted_iota(jnp.int32, (rows, win), 0)) - (start + lax.broadcasted_iota(jnp.int32, (rows, win), 1))
    ok = dist >= 0
    if max_dist is not None:
        ok = ok & (dist <= max_dist)
    return ok


def _head_groups(heads, dh):
    gw = max(LANES, dh)
    return gw, gw // dh, heads // (gw // dh)


def _own_lanes(rows, gw, dh, u):
    return lax.broadcasted_iota(jnp.int32, (rows, gw), 1) // dh == u


def _only_head(x, own, per, u):
    return x if per == 1 else jnp.where(own[u], x, jnp.zeros_like(x))


def _step_scores(qz, kw, kb_row, ok):
    s = lax.dot_general(qz, kw, (((1,), (1,)), ((), ())), preferred_element_type=F32)
    if kb_row is not None:
        s = s - kb_row
    if ok is not None:
        s = jnp.where(ok, s, MASKED)
    return s


def _attn_fwd(q, k, v, kb=None, *, classes=1, heads, dh, mode, max_dist=None, bf16_copy=False, name):
    (qa, qc), (ka, kc), (va, vc) = q, k, v
    n, lq, lk = qa.shape[0], qa.shape[1], ka.shape[1]
    width = heads * dh
    tq, rows, win = _attn_tiles(mode, lq, lk)
    gw, per, groups = _head_groups(heads, dh)
    scale = dh ** -0.5
    has_kb = kb is not None
    single = mode != "causal"
    together = min(classes, max(1, CLASS_ROWS // lq))
    ncols = lambda arr: arr.shape[2] // (classes * width)
    spans = [(ncols(qa), qc, width), (ncols(ka), kc, width), (ncols(va), vc, width), (1, 0, width), (1, 0, LANES), (1, 0, width)]

    def body(*refs):
        for cl in range(together):
            for sub in range(tq // rows):
                part(_class_window(refs, spans, together, cl), pl.program_id(2) * tq + sub * rows, slice(sub * rows, (sub + 1) * rows))

    def part(refs, row0, rs):
        q_ref, k_ref, v_ref = refs[:3]
        kb_ref = refs[3] if has_kb else None
        n_in = 4 if has_kb else 3
        o_ref, lse_ref = refs[n_in:n_in + 2]
        o16_ref = refs[n_in + 2] if bf16_copy else None
        lane = lax.broadcasted_iota(jnp.int32, (rows, LANES), 1)
        own = [_own_lanes(rows, gw, dh, u) for u in range(per)]

        def step(j, carry):
            m_in, l_in = carry
            m_out, l_out = m_in, l_in
            start, _ = _window(mode, row0, j, rows, win, lk)
            ok = _allowed(mode, row0, start, rows, win, max_dist)
            for g in range(groups):
                cols = slice(g * gw, (g + 1) * gw)
                qg = q_ref[0, rs, cols] * scale
                kw = k_ref[0, pl.ds(start, win), cols]
                vw = v_ref[0, pl.ds(start, win), cols]
                alpha_g = new_g = None
                for u in range(per):
                    h = g * per + u
                    kb_row = kb_ref[0, h, pl.ds(j, 1), :] if has_kb else None
                    s = _step_scores(_only_head(qg, own, per, u), kw, kb_row, ok)
                    m_new = jnp.max(s, axis=1, keepdims=True)
                    if not single:
                        m_old = m_in[:, h:h + 1]
                        m_new = jnp.maximum(m_old, m_new)
                        alpha = jnp.exp(m_old - m_new)
                        alpha_g = alpha if u == 0 else jnp.where(own[u], alpha, alpha_g)
                    p = jnp.exp(s - m_new)
                    l_new = jnp.sum(p, axis=1, keepdims=True)
                    r = jnp.dot(p.astype(BF16), vw, preferred_element_type=F32)
                    if single:
                        r = r * (1.0 / l_new)
                    else:
                        l_new = alpha * l_in[:, h:h + 1] + l_new
                    new_g = r if u == 0 else jnp.where(own[u], r, new_g)
                    m_out = jnp.where(lane == h, m_new, m_out)
                    l_out = jnp.where(lane == h, l_new, l_out)
                o_ref[0, rs, cols] = new_g if single else alpha_g * o_ref[0, rs, cols] + new_g
                if bf16_copy:
                    o16_ref[0, rs, cols] = new_g.astype(BF16)
            return m_out, l_out

        carry = (jnp.full((rows, LANES), MASKED, F32), jnp.zeros((rows, LANES), F32))
        if single:
            m_all, l_all = step(0, carry)
            l_all = jnp.where(lane < heads, l_all, 1.0)
        else:
            o_ref[...] = jnp.zeros(o_ref.shape, F32)
            m_all, l_all = lax.fori_loop(0, _window(mode, row0, 0, rows, win, lk)[1], step, carry)
            l_all = jnp.where(lane < heads, l_all, 1.0)
            inv = 1.0 / l_all
            for g in range(groups):
                cols = slice(g * gw, (g + 1) * gw)
                inv_g = inv[:, g * per:g * per + 1]
                for u in range(1, per):
                    inv_g = jnp.where(own[u], inv[:, g * per + u:g * per + u + 1], inv_g)
                o_ref[0, rs, cols] = o_ref[0, rs, cols] * inv_g
        lse_ref[0, rs, :] = jnp.where(lane < heads, m_all + jnp.log(l_all), 0.0)

    in_specs = [_attn_specs(qa, width, qc, classes, tq, False, together), _attn_specs(ka, width, kc, classes, win, True, together),
                _attn_specs(va, width, vc, classes, win, True, together)]
    args = [qa, ka, va]
    if has_kb:
        assert together == 1
        in_specs.append(pl.BlockSpec((1,) + kb.shape[1:], lambda n_, r, i: (n_, 0, 0, 0)))
        args.append(kb)
    out_shape = [jax.ShapeDtypeStruct((n, lq, classes * width), F32), jax.ShapeDtypeStruct((n, lq, classes * LANES), F32)]
    out_specs = [pl.BlockSpec((1, tq, together * width), lambda n_, r, i: (n_, i, r)),
                 pl.BlockSpec((1, tq, together * LANES), lambda n_, r, i: (n_, i, r))]
    if bf16_copy:
        assert single
        out_shape.append(jax.ShapeDtypeStruct((n, lq, classes * width), BF16))
        out_specs.append(out_specs[0])
    return pl.pallas_call(
        body, grid=(n, classes // together, lq // tq), in_specs=in_specs, out_specs=out_specs, out_shape=out_shape, name=name,
        compiler_params=_params("parallel", "parallel", "arbitrary"),
    )(*args)


def _attn_bwd(q, k, v, do, lse, delta, kb=None, *, classes=1, heads, dh, mode, max_dist=None, dq_dtype=F32, name):
    (qa, qc), (ka, kc), (va, vc), (da, dc) = q, k, v, do
    n, lq, lk = qa.shape[0], qa.shape[1], ka.shape[1]
    width = heads * dh
    tq, rows, win = _attn_tiles(mode, lq, lk, backward=True)
    gw, per, groups = _head_groups(heads, dh)
    scale = dh ** -0.5
    has_kb = kb is not None
    single = mode != "causal"
    nt = (((1,), (1,)), ((), ()))
    tn = (((0,), (0,)), ((), ()))
    together = min(classes, max(1, CLASS_ROWS // lq))
    ncols = lambda arr: arr.shape[2] // (classes * width)
    spans = [(ncols(qa), qc, width), (ncols(ka), kc, width), (ncols(va), vc, width), (ncols(da), dc, width), (1, 0, LANES),
             (1, 0, LANES), (1, 0, width), (1, 0, width), (1, 0, width)]

    def body(*refs):
        n_in = 7 if has_kb else 6
        dk_ref, dv_ref = refs[n_in + 1:n_in + 3]

        @pl.when(pl.program_id(2) == 0)
        def _():
            dk_ref[...] = jnp.zeros(dk_ref.shape, F32)
            dv_ref[...] = jnp.zeros(dv_ref.shape, F32)
            if has_kb:
                refs[n_in + 3][...] = jnp.zeros(refs[n_in + 3].shape, F32)

        for cl in range(together):
            for sub in range(tq // rows):
                part(_class_window(refs, spans, together, cl), pl.program_id(2) * tq + sub * rows, slice(sub * rows, (sub + 1) * rows))

    def part(refs, row0, rs):
        q_ref, k_ref, v_ref, do_ref, lse_ref, dl_ref = refs[:6]
        kb_ref = refs[6] if has_kb else None
        n_in = 7 if has_kb else 6
        dq_ref, dk_ref, dv_ref = refs[n_in:n_in + 3]
        dkb_ref, drow_ref = refs[n_in + 3:n_in + 5] if has_kb else (None, None)
        lse_all = lse_ref[0, rs, :]
        dl_all = dl_ref[0, rs, :]
        lane = lax.broadcasted_iota(jnp.int32, (rows, LANES), 1)
        own = [_own_lanes(rows, gw, dh, u) for u in range(per)]

        def step(j, drow_all):
            start, _ = _window(mode, row0, j, rows, win, lk)
            ok = _allowed(mode, row0, start, rows, win, max_dist)
            for g in range(groups):
                cols = slice(g * gw, (g + 1) * gw)
                qg = q_ref[0, rs, cols] * scale
                dog = do_ref[0, rs, cols]
                kw = k_ref[0, pl.ds(start, win), cols]
                vw = v_ref[0, pl.ds(start, win), cols]
                dq_g = dk_w = dv_w = None
                for u in range(per):
                    h = g * per + u
                    qz, doz = _only_head(qg, own, per, u), _only_head(dog, own, per, u)
                    kb_row = kb_ref[0, h, pl.ds(j, 1), :] if has_kb else None
                    p = jnp.exp(_step_scores(qz, kw, kb_row, ok) - lse_all[:, h:h + 1])
                    dp = lax.dot_general(doz, vw, nt, preferred_element_type=F32)
                    ds = p * (dp - dl_all[:, h:h + 1])
                    dsb = ds.astype(BF16)
                    r = jnp.dot(dsb, kw, preferred_element_type=F32)
                    dq_g = r if u == 0 else jnp.where(own[u], r, dq_g)
                    dk_u = lax.dot_general(dsb, qz, tn, preferred_element_type=F32)
                    dv_u = lax.dot_general(p.astype(BF16), doz, tn, preferred_element_type=F32)
                    dk_w = dk_u if u == 0 else dk_w + dk_u
                    dv_w = dv_u if u == 0 else dv_w + dv_u
                    if has_kb:
                        dkb_ref[0, h, pl.ds(j, 1), :] += -jnp.sum(ds, axis=0, keepdims=True)
                        drow_all = drow_all + jnp.where(lane == h, jnp.sum(ds, axis=1, keepdims=True), 0.0)
                dk_ref[0, pl.ds(start, win), cols] += dk_w
                dv_ref[0, pl.ds(start, win), cols] += dv_w
                if single:
                    dq_ref[0, rs, cols] = (dq_g * scale).astype(dq_ref.dtype)
                else:
                    dq_ref[0, rs, cols] += dq_g * scale
            return drow_all

        drow_all = jnp.zeros((rows, LANES), F32)
        if single:
            drow_all = step(0, drow_all)
        else:
            dq_ref[...] = jnp.zeros(dq_ref.shape, F32)
            drow_all = lax.fori_loop(0, _window(mode, row0, 0, rows, win, lk)[1], step, drow_all)
        if has_kb:
            drow_ref[0, rs, :] = drow_all

    row_spec = functools.partial(_attn_specs, classes=classes, rows_block=tq, whole=False, together=together)
    in_specs = [row_spec(qa, width, qc), _attn_specs(ka, width, kc, classes, win, True, together),
                _attn_specs(va, width, vc, classes, win, True, together), row_spec(da, width, dc), row_spec(lse, LANES, 0),
                row_spec(delta, LANES, 0)]
    args = [qa, ka, va, da, lse, delta]
    assert single or dq_dtype == F32
    out_shape = [jax.ShapeDtypeStruct((n, lq, classes * width), dq_dtype), jax.ShapeDtypeStruct((n, lk, classes * width), F32),
                 jax.ShapeDtypeStruct((n, lk, classes * width), F32)]
    kv_spec = pl.BlockSpec((1, lk, together * width), lambda n_, r, i: (n_, 0, r))
    out_specs = [pl.BlockSpec((1, tq, together * width), lambda n_, r, i: (n_, i, r)), kv_spec, kv_spec]
    if has_kb:
        assert together == 1
        kb_spec = pl.BlockSpec((1,) + kb.shape[1:], lambda n_, r, i: (n_, 0, 0, 0))
        in_specs.append(kb_spec)
        args.append(kb)
        out_shape += [jax.ShapeDtypeStruct(kb.shape, F32), jax.ShapeDtypeStruct((n, lq, LANES), F32)]
        out_specs += [kb_spec, pl.BlockSpec((1, tq, LANES), lambda n_, r, i: (n_, i, 0))]
    return pl.pallas_call(
        body, grid=(n, classes // together, lq // tq), in_specs=in_specs, out_specs=out_specs, out_shape=out_shape, name=name,
        compiler_params=_params("parallel", "parallel", "arbitrary"),
    )(*args)


def _head_delta(do, out, heads, *, name, lse=None, sink=None):
    width = out.shape[1]
    gather = _head_expand(heads, width).T

    if sink is None:
        return _rowwise(lambda do, o, e: _dot_exact(do * o, e), [do, out], [gather], [(LANES, F32)], name=name)[0]

    def fn(do, o, lse, e, sink):
        delta = _dot_exact(do * o, e)
        return delta, -jnp.sum(jnp.exp(sink - lse) * delta, axis=0, keepdims=True)
    return _rowwise(fn, [do, out, lse], [gather, sink], [(LANES, F32)], [(1, LANES)], name=name)


def _rope_tables():
    half = HEAD_DIM // 2
    inv = ROPE_THETA ** (-jnp.arange(half, dtype=F32) / half)
    ang = jnp.arange(SEQ, dtype=F32)[:, None] * inv[None, :]
    cos = jnp.tile(jnp.cos(ang), (1, 2 * ATT_W // HEAD_DIM))
    sin = jnp.tile(jnp.concatenate([-jnp.sin(ang), jnp.sin(ang)], axis=1), (1, ATT_W // HEAD_DIM))
    return cos, sin


def _rotate(x, cos, sin):
    width = x.shape[1]
    lane = lax.broadcasted_iota(jnp.int32, x.shape, 1)
    first_half = (lane % HEAD_DIM) < (HEAD_DIM // 2)
    swapped = jnp.where(first_half, pltpu.roll(x, width - HEAD_DIM // 2, axis=1), pltpu.roll(x, HEAD_DIM // 2, axis=1))
    return x * cos[:, :width] + swapped * sin[:, :width]


def _gelu(x):
    k = math.sqrt(2.0 / math.pi)
    t = jnp.tanh(k * (x + 0.044715 * x * x * x))
    return 0.5 * x * (1.0 + t), t


def _gelu_grad(x, t):
    k = math.sqrt(2.0 / math.pi)
    return 0.5 * (1.0 + t) + 0.5 * x * (1.0 - t * t) * k * (1.0 + 3.0 * 0.044715 * x * x)


def _shift_down(x, halo, s):
    ext = jnp.concatenate([halo, x], axis=0)
    return pltpu.roll(ext, s, axis=0)[8:, :]


def _shift_up(x, halo, s):
    rows = x.shape[0]
    ext = jnp.concatenate([x, halo], axis=0)
    return pltpu.roll(ext, rows + 8 - s, axis=0)[:rows, :]


def _lru_gates(u, halo, cw, cb, wa, ba, wi, bi, lam):
    taps = [_shift_down(u, halo, CONV_WIDTH - 1 - k) for k in range(CONV_WIDTH - 1)] + [u]
    uc = cb + sum(cw[k:k + 1, :] * taps[k] for k in range(CONV_WIDTH))
    ucb = uc.astype(BF16)
    r = jax.nn.sigmoid(jnp.dot(ucb, wa, preferred_element_type=F32) + ba)
    gi = jax.nn.sigmoid(jnp.dot(ucb, wi, preferred_element_type=F32) + bi)
    sp = jnp.maximum(-lam, 0.0) + jnp.log(1.0 + jnp.exp(-jnp.abs(lam)))
    log_a = -LRU_C * r * sp
    a = jnp.exp(log_a)
    x2 = 2.0 * log_a
    one_minus_a2 = jnp.where(x2 > -0.01, -x2 * (1.0 + x2 * (0.5 + x2 * (1.0 / 6.0 + x2 / 24.0))), 1.0 - jnp.exp(x2))
    mult = jnp.sqrt(one_minus_a2)
    return taps, uc, ucb, r, gi, sp, a, mult


def _lru_specs(nchunk, reverse):
    def chunk(b, c):
        return b * nchunk + ((nchunk - 1 - c) if reverse else c)

    def halo_before(b, c):
        return jnp.maximum(chunk(b, c) * (LRU_CHUNK // 8) - 1, 0)

    return chunk, halo_before


def _lru_fwd(zug, cw, cb, wa, ba, wi, bi, lam, *, name):
    total = zug.shape[0]
    nchunk = SEQ // LRU_CHUNK
    w = LRU_WIDTH
    chunk, halo_before = _lru_specs(nchunk, False)

    def body(u_ref, uh_ref, g_ref, cw_ref, cb_ref, wa_ref, ba_ref, wi_ref, bi_ref, lam_ref, y_ref, h_ref, a_sc, x_sc, carry_sc):
        c = pl.program_id(1)
        u = u_ref[...]
        halo = jnp.where(c > 0, uh_ref[...], 0.0)
        _, uc, _, _, gi, _, a, mult = _lru_gates(u, halo, cw_ref[...], cb_ref[...], wa_ref[...], ba_ref[...],
                                                 wi_ref[...], bi_ref[...], lam_ref[...])
        a_sc[...] = a
        x_sc[...] = mult * (gi * uc)

        @pl.when(c == 0)
        def _():
            carry_sc[...] = jnp.zeros(carry_sc.shape, F32)

        def tile_step(t, h):
            r0 = pl.multiple_of(t * 8, 8)
            at = a_sc[pl.ds(r0, 8), :]
            xt = x_sc[pl.ds(r0, 8), :]
            rows = []
            for j in range(8):
                h = at[j:j + 1, :] * h + xt[j:j + 1, :]
                rows.append(h)
            h_ref[pl.ds(r0, 8), :] = jnp.concatenate(rows, axis=0)
            return h

        h_last = lax.fori_loop(0, LRU_CHUNK // 8, tile_step, carry_sc[0:1, :])
        carry_sc[0:1, :] = h_last
        gel, _ = _gelu(g_ref[...])
        y_ref[...] = (h_ref[...] * gel).astype(BF16)

    small = lambda arr: pl.BlockSpec(arr.shape, lambda b, c: (0, 0))
    return pl.pallas_call(
        body, grid=(total // SEQ, nchunk),
        in_specs=[pl.BlockSpec((LRU_CHUNK, w), lambda b, c: (chunk(b, c), 0)),
                  pl.BlockSpec((8, w), lambda b, c: (halo_before(b, c), 0)),
                  pl.BlockSpec((LRU_CHUNK, w), lambda b, c: (chunk(b, c), 1)),
                  small(cw), small(cb), small(wa), small(ba), small(wi), small(bi), small(lam)],
        out_specs=[pl.BlockSpec((LRU_CHUNK, w), lambda b, c: (chunk(b, c), 0))] * 2,
        out_shape=[jax.ShapeDtypeStruct((total, w), BF16), jax.ShapeDtypeStruct((total, w), F32)],
        scratch_shapes=[pltpu.VMEM((LRU_CHUNK, w), F32), pltpu.VMEM((LRU_CHUNK, w), F32), pltpu.VMEM((8, w), F32)],
        name=name, compiler_params=_params("arbitrary", "arbitrary"),
    )(zug, zug, zug, cw, cb, wa, ba, wi, bi, lam)


def _lru_bwd(zug, hl, dy, cw, cb, wa, ba, wi, bi, lam, *, name):
    total = zug.shape[0]
    nchunk = SEQ // LRU_CHUNK
    w = LRU_WIDTH
    chunk, halo_before = _lru_specs(nchunk, True)
    tn = (((0,), (0,)), ((), ()))
    nt = (((1,), (1,)), ((), ()))

    def body(u_ref, uh_ref, g_ref, hl_ref, hh_ref, dy_ref, cw_ref, cb_ref, wa_ref, ba_ref, wi_ref, bi_ref, lam_ref,
             dz_ref, dcw_ref, dcb_ref, dwa_ref, dba_ref, dwi_ref, dbi_ref, dlam_ref,
             a_sc, d_sc, g_sc, gcarry_sc, acarry_sc, duc_sc):
        b, c = pl.program_id(0), pl.program_id(1)
        first_chunk = c == nchunk - 1

        @pl.when((b == 0) & (c == 0))
        def _():
            for ref in (dcw_ref, dcb_ref, dwa_ref, dba_ref, dwi_ref, dbi_ref, dlam_ref):
                ref[...] = jnp.zeros(ref.shape, F32)

        @pl.when(c == 0)
        def _():
            gcarry_sc[...] = jnp.zeros(gcarry_sc.shape, F32)
            acarry_sc[...] = jnp.zeros(acarry_sc.shape, F32)
            duc_sc[...] = jnp.zeros(duc_sc.shape, F32)

        u = u_ref[...]
        halo = jnp.where(first_chunk, 0.0, uh_ref[...])
        cw, wa, wi, lam = cw_ref[...], wa_ref[...], wi_ref[...], lam_ref[...]
        taps, uc, ucb, r, gi, sp, a, mult = _lru_gates(u, halo, cw, cb_ref[...], wa, ba_ref[...], wi, bi_ref[...], lam)
        gate = g_ref[...]
        gel, th = _gelu(gate)
        dy = dy_ref[...]
        hl = hl_ref[...]
        a_sc[...] = a
        d_sc[...] = dy * gel
        dgate = dy * hl * _gelu_grad(gate, th)

        def tile_step(t, carry):
            g_next, a_next = carry
            r0 = pl.multiple_of((LRU_CHUNK // 8 - 1 - t) * 8, 8)
            dt = d_sc[pl.ds(r0, 8), :]
            at = a_sc[pl.ds(r0, 8), :]
            rows = [None] * 8
            for j in reversed(range(8)):
                g_next = dt[j:j + 1, :] + a_next * g_next
                a_next = at[j:j + 1, :]
                rows[j] = g_next
            g_sc[pl.ds(r0, 8), :] = jnp.concatenate(rows, axis=0)
            return g_next, a_next

        g_last, a_last = lax.fori_loop(0, LRU_CHUNK // 8, tile_step, (gcarry_sc[0:1, :], acarry_sc[0:1, :]))
        gcarry_sc[0:1, :] = g_last
        acarry_sc[0:1, :] = a_last

        gs = g_sc[...]
        hl_halo = jnp.where(first_chunk, 0.0, hh_ref[...])
        da = gs * _shift_down(hl, hl_halo, 1)
        dmult = gs * gi * uc
        dgi = gs * mult * uc
        duc = gs * mult * gi
        dlog_a = da * a - dmult * a * a / mult
        dr = dlog_a * (-LRU_C * sp)
        dsp = jnp.sum(dlog_a * (-LRU_C * r), axis=0, keepdims=True)
        dlam_ref[...] += -dsp * jax.nn.sigmoid(-lam)
        dpa = dr * r * (1.0 - r)
        dpi = dgi * gi * (1.0 - gi)
        dpab, dpib = dpa.astype(BF16), dpi.astype(BF16)
        dba_ref[...] += jnp.sum(dpa, axis=0, keepdims=True)
        dbi_ref[...] += jnp.sum(dpi, axis=0, keepdims=True)
        dwa_ref[...] += lax.dot_general(ucb, dpab, tn, preferred_element_type=F32)
        dwi_ref[...] += lax.dot_general(ucb, dpib, tn, preferred_element_type=F32)
        duc = duc + lax.dot_general(dpab, wa, nt, preferred_element_type=F32)
        duc = duc + lax.dot_general(dpib, wi, nt, preferred_element_type=F32)
        dcb_ref[...] += jnp.sum(duc, axis=0, keepdims=True)
        dcw_ref[...] += jnp.concatenate([jnp.sum(duc * taps[k], axis=0, keepdims=True) for k in range(CONV_WIDTH)], axis=0)
        after = duc_sc[...]
        du = cw[CONV_WIDTH - 1:CONV_WIDTH, :] * duc
        for k in range(CONV_WIDTH - 1):
            du = du + cw[k:k + 1, :] * _shift_up(duc, after, CONV_WIDTH - 1 - k)
        duc_sc[...] = duc[0:8, :]
        dz_ref[:, 0:w] = du.astype(BF16)
        dz_ref[:, w:2 * w] = dgate.astype(BF16)

    small = lambda arr: pl.BlockSpec(arr.shape, lambda b, c: (0, 0))
    acc = lambda shape: pl.BlockSpec(shape, lambda b, c: (0, 0))
    chunk_spec = lambda cb_: pl.BlockSpec((LRU_CHUNK, w), lambda b, c: (chunk(b, c), cb_))
    halo_spec = pl.BlockSpec((8, w), lambda b, c: (halo_before(b, c), 0))
    acc_shapes = [(CONV_WIDTH, w), (1, w), (w, w), (1, w), (w, w), (1, w), (1, w)]
    return pl.pallas_call(
        body, grid=(total // SEQ, nchunk),
        in_specs=[chunk_spec(0), halo_spec, chunk_spec(1), chunk_spec(0), halo_spec, chunk_spec(0),
                  small(cw), small(cb), small(wa), small(ba), small(wi), small(bi), small(lam)],
        out_specs=[pl.BlockSpec((LRU_CHUNK, 2 * w), lambda b, c: (chunk(b, c), 0))] + [acc(s) for s in acc_shapes],
        out_shape=[jax.ShapeDtypeStruct((total, 2 * w), BF16)] + [jax.ShapeDtypeStruct(s, F32) for s in acc_shapes],
        scratch_shapes=[pltpu.VMEM((LRU_CHUNK, w), F32)] * 3 + [pltpu.VMEM((8, w), F32)] * 3,
        name=name, compiler_params=_params("arbitrary", "arbitrary"),
    )(zug, zug, zug, hl, hl, dy, cw, cb, wa, ba, wi, bi, lam)


def _block_diag(wb):
    eye = jnp.eye(LRU_BLOCKS, dtype=wb.dtype)
    return jnp.einsum("gcd,gh->gchd", wb, eye).reshape(LRU_WIDTH, LRU_WIDTH).astype(BF16)


def _diag_blocks(wd):
    blk = LRU_WIDTH // LRU_BLOCKS
    return jnp.stack([wd[g * blk:(g + 1) * blk, g * blk:(g + 1) * blk] for g in range(LRU_BLOCKS)])


FOX_SCAN = 256


def _fox_bias(fl, bf, *, name):
    nb = fl.shape[0]

    def body(fl_ref, bf_ref, c_ref):
        x = fl_ref[0] + bf_ref[...]
        logf = jnp.minimum(x, 0.0) - jnp.log(1.0 + jnp.exp(-jnp.abs(x)))
        upper = (lax.broadcasted_iota(jnp.int32, (FOX_SCAN, FOX_SCAN), 0)
                 <= lax.broadcasted_iota(jnp.int32, (FOX_SCAN, FOX_SCAN), 1)).astype(BF16)
        carry = jnp.zeros((LRU_BLOCKS, 1), F32)
        for j in range(SEQ // FOX_SCAN):
            blk = logf[:, j * FOX_SCAN:(j + 1) * FOX_SCAN]
            c_ref[0, :, j * FOX_SCAN:(j + 1) * FOX_SCAN] = _dot_exact(blk, upper) + carry
            carry = carry + jnp.sum(blk, axis=1, keepdims=True)

    return pl.pallas_call(
        body, grid=(nb,),
        in_specs=[pl.BlockSpec((1, 8, SEQ), lambda b: (b, 0, 0)), pl.BlockSpec((8, 1), lambda b: (0, 0))],
        out_specs=pl.BlockSpec((1, 8, SEQ), lambda b: (b, 0, 0)),
        out_shape=jax.ShapeDtypeStruct((nb, 8, SEQ), F32), name=name, compiler_params=_params("arbitrary"),
    )(fl, bf)


def _fox_bias_bwd(fl, bf, dc, *, name):
    nb = fl.shape[0]

    def body(fl_ref, bf_ref, dc_ref, dfl_ref, dbf_ref):
        @pl.when(pl.program_id(0) == 0)
        def _():
            dbf_ref[...] = jnp.zeros(dbf_ref.shape, F32)

        x = fl_ref[0] + bf_ref[...]
        dc_all = dc_ref[0]
        lower = (lax.broadcasted_iota(jnp.int32, (FOX_SCAN, FOX_SCAN), 0)
                 >= lax.broadcasted_iota(jnp.int32, (FOX_SCAN, FOX_SCAN), 1)).astype(BF16)
        carry = jnp.zeros((LRU_BLOCKS, 1), F32)
        total = jnp.zeros((LRU_BLOCKS, 1), F32)
        for j in reversed(range(SEQ // FOX_SCAN)):
            cols = slice(j * FOX_SCAN, (j + 1) * FOX_SCAN)
            blk = dc_all[:, cols]
            dlogf = _dot_exact(blk, lower) + carry
            carry = carry + jnp.sum(blk, axis=1, keepdims=True)
            dx = dlogf * jax.nn.sigmoid(-x[:, cols])
            dfl_ref[0, :, cols] = dx
            total = total + jnp.sum(dx, axis=1, keepdims=True)
        dbf_ref[...] += total

    return pl.pallas_call(
        body, grid=(nb,),
        in_specs=[pl.BlockSpec((1, 8, SEQ), lambda b: (b, 0, 0)), pl.BlockSpec((8, 1), lambda b: (0, 0)),
                  pl.BlockSpec((1, 8, SEQ), lambda b: (b, 0, 0))],
        out_specs=[pl.BlockSpec((1, 8, SEQ), lambda b: (b, 0, 0)), pl.BlockSpec((8, 1), lambda b: (0, 0))],
        out_shape=[jax.ShapeDtypeStruct((nb, 8, SEQ), F32), jax.ShapeDtypeStruct((8, 1), F32)],
        name=name, compiler_params=_params("arbitrary"),
    )(fl, bf, dc)


def _seq3(a, nb):
    return a.reshape(nb, a.shape[0] // nb, a.shape[1])


def _flat2(a):
    return a.reshape(a.shape[0] * a.shape[1], a.shape[2])


def _residual_out(y, w, h, next_gain, *, name):
    if next_gain is None:
        out, = _matmul(y, w, extras=(h,), epilogue=lambda acc, res: (acc + res,), name=name)
        return out, None

    def epilogue(acc, res, g):
        out = acc + res
        return out, out * _rstd(out) * g
    return _matmul(y, w, outs=(F32, BF16), extras=(h,), consts=(next_gain,), epilogue=epilogue, whole_rows=True, name=name)


def _mlp_fwd(h, hn, p, tag, next_gain):
    act, = _matmul(hn, p["w_up_t"], tb=True, outs=(BF16,), epilogue=lambda acc: (jnp.square(jnp.maximum(acc, 0.0)),),
                   name=f"{tag}_up")
    out, hn_next = _residual_out(act, p["w_down"], h, next_gain, name=f"{tag}_down")
    return out, hn_next, (h, hn, act)


def _mlp_bwd(dh, dhb, p, saved, tag):
    h, hn, act = saved
    dup, = _matmul(dhb, p["w_down"], tb=True, outs=(BF16,), extras=(act,),
                   epilogue=lambda acc, act: (acc * (2.0 * jnp.sqrt(act).astype(F32)),), name=f"{tag}_bwd_dup")
    dw_down, = _matmul(act, dhb, ta=True, outs=(BF16,),name=f"{tag}_bwd_dwdown")
    dw_up_t, = _matmul(dup, hn, ta=True, outs=(BF16,),name=f"{tag}_bwd_dwup")
    dh2, dh2b, dg = _norm_input_grad(dup, p["w_up_t"], h, dh, p["norm"], name=f"{tag}_bwd_dh")
    return dh2, dh2b, {"norm": dg, "w_up_t": dw_up_t, "w_down": dw_down}


def _xa_fwd(h, hn, mem, p, tag, nb, next_gain):
    mem_n = _rms_fwd(mem, p["mem_norm"], name=f"{tag}_memnorm")
    q, = _matmul(hn, p["w_q"], outs=(BF16,), name=f"{tag}_q")
    kv, = _matmul(mem_n, p["w_kv_t"], tb=True, outs=(BF16,), name=f"{tag}_kv")
    q3, kv3 = _seq3(q, nb), _seq3(kv, nb)
    o, lse, ob = _attn_fwd((q3, 0), (kv3, 0), (kv3, 1), heads=XA_HEADS, dh=XA_HEAD_DIM, mode="full", bf16_copy=True,
                           name=f"{tag}_attn")
    o, ob = _flat2(o), _flat2(ob)
    out, hn_next = _residual_out(ob, p["w_o"], h, next_gain, name=f"{tag}_o")
    return out, hn_next, (h, hn, mem_n, q3, kv3, o, ob, lse)


def _xa_bwd(dh, dhb, mem, p, saved, tag, nb):
    h, hn, mem_n, q3, kv3, o, ob, lse = saved
    dob, delta = _matmul(dhb, p["w_o"], tb=True, outs=(BF16, (F32, LANES)), extras=(o,), consts=(_head_expand(XA_HEADS, D_MODEL).T,),
                         epilogue=lambda acc, o, e: (acc, _dot_exact(acc * o, e)), name=f"{tag}_bwd_do")
    dw_o, = _matmul(ob, dhb, ta=True, outs=(BF16,),name=f"{tag}_bwd_dwo")
    dq, dk, dv = _attn_bwd((q3, 0), (kv3, 0), (kv3, 1), (_seq3(dob, nb), 0), lse, _seq3(delta, nb), heads=XA_HEADS,
                           dh=XA_HEAD_DIM, mode="full", dq_dtype=BF16, name=f"{tag}_bwd_attn")
    dqb = _flat2(dq)
    dkvb, = _rowwise(lambda a, b: jnp.concatenate([a, b], axis=1), [_flat2(dk), _flat2(dv)], [], [(2 * D_MODEL, BF16)],
                     name=f"{tag}_bwd_dkvcast")
    dw_q, = _matmul(hn, dqb, ta=True, outs=(BF16,),name=f"{tag}_bwd_dwq")
    dw_kv_t, = _matmul(dkvb, mem_n, ta=True, outs=(BF16,),name=f"{tag}_bwd_dwkv")
    dmem_n, = _matmul(dkvb, p["w_kv_t"], name=f"{tag}_bwd_dmemn")
    dg_mem, = _rowwise(lambda x, d, g: _rms_bwd_vals(x, d, g)[1], [mem, dmem_n], [p["mem_norm"]], [], [(1, D_MODEL)],
                       name=f"{tag}_bwd_memnorm")
    dh2, dh2b, dg = _norm_input_grad(dqb, p["w_q"], h, dh, p["norm"], tb=True, name=f"{tag}_bwd_dh")
    return dh2, dh2b, {"norm": dg, "mem_norm": dg_mem, "w_q": dw_q, "w_kv_t": dw_kv_t, "w_o": dw_o}


AB_MAIN = 2 * LRU_WIDTH + 3 * ATT_W


def _ab_fwd(h, hn, p, nb, next_gain):
    w_in_t = p["w_in_t"]
    zug, = _matmul(hn, w_in_t[:2 * LRU_WIDTH], tb=True, name="ab_in_ug")
    qkv, = _matmul(hn, w_in_t[2 * LRU_WIDTH:AB_MAIN], tb=True, outs=(BF16,), tn=768, name="ab_in_qkv")
    zf, = _matmul(hn, w_in_t[AB_MAIN:], tb=True, name="ab_in_f")
    fl = zf[:, :8].reshape(nb, SEQ, 8).transpose(0, 2, 1)
    cbias = _fox_bias(fl, p["b_f"], name="ab_fox_bias")
    kb = cbias.reshape(nb, 8, SEQ // ATT_CHUNK, ATT_CHUNK)
    y_a, hl = _lru_fwd(zug, p["conv_w"], p["conv_b"], p["w_a"], p["b_a"], p["w_i"], p["b_i"], p["lam"], name="ab_lru")
    qkv3 = _seq3(qkv, nb)
    o, lse = _attn_fwd((qkv3, 0), (qkv3, 1), (qkv3, 2), kb, heads=8, dh=HEAD_DIM, mode="causal", name="ab_fox")
    o = _flat2(o)
    y, = _rowwise(lambda a, b: jnp.concatenate([a, b.astype(BF16)], axis=1), [y_a, o], [], [(D_MODEL, BF16)], name="ab_ycat")
    out, hn_next = _residual_out(y, p["w_out"], h, next_gain, name="ab_out")
    return out, hn_next, (h, hn, zug, qkv3, fl, kb, hl, o, lse, y)


def _ab_bwd(dh, dhb, p, saved, nb):
    h, hn, zug, qkv3, fl, kb, hl, o, lse, y = saved
    dy, dyb, delta = _matmul(dhb, p["w_out"], tb=True, outs=(F32, BF16, (F32, LANES)), extras=(y,), consts=(_head_expand(8, ATT_W).T,),
                             epilogue=lambda acc, y, e: (acc, acc, _dot_exact(acc[:, ATT_W:] * y[:, ATT_W:].astype(F32), e)),
                             name="ab_bwd_dy")
    dw_out, = _matmul(y, dhb, ta=True, outs=(BF16,),name="ab_bwd_dwout")
    dzug, dcw, dcb, dwa, dba, dwi, dbi, dlam = _lru_bwd(zug, hl, dy, p["conv_w"], p["conv_b"], p["w_a"], p["b_a"],
                                                        p["w_i"], p["b_i"], p["lam"], name="ab_bwd_lru")
    dq, dk, dv, dkb, drow = _attn_bwd((qkv3, 0), (qkv3, 1), (qkv3, 2), (_seq3(dyb, nb), 1), lse, _seq3(delta, nb), kb,
                                      heads=8, dh=HEAD_DIM, mode="causal", name="ab_bwd_fox")
    dcum = dkb.reshape(nb, 8, SEQ) + drow[:, :, :8].transpose(0, 2, 1)
    dfl, dbf = _fox_bias_bwd(fl, p["b_f"], dcum, name="ab_bwd_fox_bias")
    dzf = jnp.pad(dfl.transpose(0, 2, 1).reshape(nb * SEQ, 8), ((0, 0), (0, LANES - 8)))
    dz, = _rowwise(lambda a, q, k, v, f: jnp.concatenate([a, q.astype(BF16), k.astype(BF16), v.astype(BF16), f.astype(BF16)], axis=1),
                   [dzug, _flat2(dq), _flat2(dk), _flat2(dv), dzf], [], [(AB_MAIN + LANES, BF16)], name="ab_bwd_dzcat")
    dw_in_t, = _matmul(dz, hn, ta=True, outs=(BF16,),tm=384, name="ab_bwd_dwin")
    dh2, dh2b, dg = _norm_input_grad(dz, p["w_in_t"], h, dh, p["norm"], name="ab_bwd_dh")
    grads = {"norm": dg, "w_in_t": dw_in_t[:AB_MAIN + 8], "conv_w": dcw, "conv_b": dcb, "w_a": _diag_blocks(dwa), "b_a": dba,
             "w_i": _diag_blocks(dwi), "b_i": dbi, "lam": dlam, "b_f": dbf.reshape(1, 8), "w_out": dw_out}
    return dh2, dh2b, grads


CD_IN = 3 * ATT_W + ATT_W + 2 * SWA_KW


def _gqa_share():
    src = jnp.arange(SWA_KW)[:, None]
    dst = jnp.arange(ATT_W)[None, :]
    per_kv = ATT_W // (SWA_KW // HEAD_DIM)
    return ((dst // per_kv == src // HEAD_DIM) & (dst % HEAD_DIM == src % HEAD_DIM)).astype(BF16)


def _cd_fwd(h, hn, p, nb, next_gain):
    z, = _matmul(hn, p["w_in_t"], tb=True, tn=1152, name="cd_in")
    cos, sin = _rope_tables()
    per_seq = SEQ // ROW_TILE
    table_map = lambda i: (i % per_seq, 0)

    def rope_fn(z, cos, sin, share):
        qc, kc, vc = z[:, 0:ATT_W], z[:, ATT_W:2 * ATT_W], z[:, 2 * ATT_W:3 * ATT_W]
        qd, kd, vd = z[:, 3 * ATT_W:4 * ATT_W], z[:, 4 * ATT_W:4 * ATT_W + SWA_KW], z[:, 4 * ATT_W + SWA_KW:]
        kd8 = jnp.dot(_rotate(kd, cos, sin).astype(BF16), share, preferred_element_type=F32)
        vd8 = jnp.dot(vd.astype(BF16), share, preferred_element_type=F32)
        qk = jnp.concatenate([_rotate(qc, cos, sin), _rotate(kc, cos, sin)], axis=1)
        return qk, vc, _rotate(qd, cos, sin), kd8, vd8, qk, qk, vc, vc
    dils = [dil for _, dil in DIL_PATTERN if dil > 1]
    outs = _rowwise(rope_fn, [z, cos, sin], [_gqa_share()],
                    [(2 * ATT_W, BF16)] + [(ATT_W, BF16)] * 4 + [(2 * ATT_W, BF16, d) for d in dils] + [(ATT_W, BF16, d) for d in dils],
                    name="cd_rope", row_maps=[None, table_map, table_map])
    qk, vc, qd, kd, vd = outs[:5]
    views = {1: (_seq3(qk, nb), _seq3(vc, nb))}
    for d, qk_d, vc_d in zip(dils, outs[5:5 + len(dils)], outs[5 + len(dils):]):
        views[d] = (_seq3(qk_d, nb), _seq3(vc_d, nb))
    in_classes = lambda a, width, d: _flat2(a) if d == 1 else ("classes", _flat2(a), width, d)
    branch = []
    for window, dil in DIL_PATTERN:
        qk_v, vc_v = views[dil]
        o_i, lse_i = _attn_fwd((qk_v, 0), (qk_v, 1), (vc_v, 0), classes=dil, heads=8, dh=HEAD_DIM, mode="band",
                               max_dist=window // dil, name=f"cd_dil{dil}")
        branch.append((in_classes(o_i, ATT_W, dil), in_classes(lse_i, LANES, dil)))
    expand = _head_expand(8, ATT_W)

    def combine(o1, o2, o3, l1, l2, l3, e):
        m = jnp.maximum(jnp.maximum(l1, l2), l3)
        lt = m + jnp.log(jnp.exp(l1 - m) + jnp.exp(l2 - m) + jnp.exp(l3 - m))
        o = sum(_dot_exact(jnp.exp(l - lt), e) * o_ for o_, l in ((o1, l1), (o2, l2), (o3, l3)))
        return o, lt
    y_c, lse_c = _rowwise(combine, [b[0] for b in branch] + [b[1] for b in branch], [expand], [(ATT_W, F32), (LANES, F32)],
                          name="cd_dil_combine")
    qd3, kd3, vd3 = _seq3(qd, nb), _seq3(kd, nb), _seq3(vd, nb)
    o_d, lse_band = _attn_fwd((qd3, 0), (kd3, 0), (vd3, 0), heads=8, dh=HEAD_DIM, mode="band", max_dist=SWA_WINDOW - 1,
                              name="cd_swa")

    def sink_combine(o, l, e, sink):
        lt = jnp.maximum(l, sink) + jnp.log(1.0 + jnp.exp(-jnp.abs(l - sink)))
        return o * _dot_exact(jnp.exp(l - lt), e), lt
    y_d, lse_d = _rowwise(sink_combine, [_flat2(o_d), _flat2(lse_band)], [expand, p["sink"]], [(ATT_W, F32), (LANES, F32)],
                          name="cd_swa_sink")
    y, = _rowwise(lambda a, b: jnp.concatenate([a, b], axis=1), [y_c, y_d], [], [(D_MODEL, BF16)], name="cd_ycat")
    out, hn_next = _residual_out(y, p["w_out"], h, next_gain, name="cd_out")
    return out, hn_next, (h, hn, views, qd3, kd3, vd3, y_c, lse_c, y_d, lse_d, y)


def _cd_bwd(dh, dhb, p, saved, nb):
    h, hn, views, qd3, kd3, vd3, y_c, lse_c, y_d, lse_d, y = saved
    dy, dyb = _matmul(dhb, p["w_out"], tb=True, outs=(F32, BF16), epilogue=lambda acc: (acc, acc), name="cd_bwd_dy")
    dw_out, = _matmul(y, dhb, ta=True, outs=(BF16,),name="cd_bwd_dwout")
    dyb3 = _seq3(dyb, nb)
    dils = [dil for _, dil in DIL_PATTERN if dil > 1]
    gather = _head_expand(8, ATT_W).T

    def prepare(do, o, lse, e):
        delta = _dot_exact(do * o, e)
        return (delta,) + (do,) * len(dils) + (lse,) * len(dils) + (delta,) * len(dils)
    outs = _rowwise(prepare, [(dy, ATT_W, 0), y_c, lse_c], [gather],
                    [(LANES, F32)] + [(ATT_W, BF16, d) for d in dils] + [(LANES, F32, d) for d in dils] * 2, name="cd_bwd_delta_dil")
    seen = {1: ((dyb3, 0), _seq3(lse_c, nb), _seq3(outs[0], nb))}
    for j, d in enumerate(dils):
        seen[d] = ((_seq3(outs[1 + j], nb), 0), _seq3(outs[1 + len(dils) + j], nb), _seq3(outs[1 + 2 * len(dils) + j], nb))
    in_classes = lambda a, d: _flat2(a) if d == 1 else ("classes", _flat2(a), ATT_W, d)
    parts = []
    for window, dil in DIL_PATTERN:
        (qk_v, vc_v), (do_v, lse_v, delta_v) = views[dil], seen[dil]
        grads = _attn_bwd((qk_v, 0), (qk_v, 1), (vc_v, 0), do_v, lse_v, delta_v, classes=dil, heads=8, dh=HEAD_DIM, mode="band",
                          max_dist=window // dil, dq_dtype=BF16, name=f"cd_bwd_dil{dil}")
        parts.append([in_classes(a, dil) for a in grads])
    delta_d, dsink = _head_delta((dy, ATT_W, 1), y_d, 8, lse=lse_d, sink=p["sink"], name="cd_bwd_delta_swa")
    dqd, dkd, dvd = _attn_bwd((qd3, 0), (kd3, 0), (vd3, 0), (dyb3, 1), _seq3(lse_d, nb), _seq3(delta_d, nb), heads=8,
                              dh=HEAD_DIM, mode="band", max_dist=SWA_WINDOW - 1, dq_dtype=BF16, name="cd_bwd_swa")
    cos, sin = _rope_tables()
    per_seq = SEQ // ROW_TILE
    table_map = lambda i: (i % per_seq, 0)

    def unrope(q1, q2, q3, k1, k2, k3, v1, v2, v3, qd, kd8, vd8, cos, sin, gather):
        back = lambda x: _rotate(x.astype(F32), cos, -sin)
        kd, vd = _dot_exact(kd8, gather), _dot_exact(vd8, gather)
        return jnp.concatenate([back(q1 + q2 + q3), back(k1 + k2 + k3), v1 + v2 + v3, back(qd), back(kd), vd], axis=1)
    ins = [parts[b][t] for t in range(3) for b in range(3)] + [_flat2(dqd), _flat2(dkd), _flat2(dvd), cos, sin]
    dz, = _rowwise(unrope, ins, [_gqa_share().T], [(CD_IN, BF16)], name="cd_bwd_unrope",
                   row_maps=[None] * 12 + [table_map, table_map])
    dw_in_t, = _matmul(dz, hn, ta=True, outs=(BF16,),tm=384, name="cd_bwd_dwin")
    dh2, dh2b, dg = _norm_input_grad(dz, p["w_in_t"], h, dh, p["norm"], name="cd_bwd_dh")
    return dh2, dh2b, {"norm": dg, "w_in_t": dw_in_t, "sink": dsink[:, :8], "w_out": dw_out}


def _local_step(x, mem, target, w, complete=None, grads_ready=None):
    nb = x.shape[0]
    h = x.reshape(nb * SEQ, D_MODEL)
    mem2 = mem.reshape(nb * MEM_LEN, D_MODEL)
    tgt = target.reshape(nb * SEQ, D_MODEL)

    hn = _rms_fwd(h, w["ab"]["norm"], name="ab_norm")
    h, hn, s_ab = _ab_fwd(h, hn, w["ab"], nb, w["xa0"]["norm"])
    if complete is not None:
        complete(h)
    h, hn, s_xa0 = _xa_fwd(h, hn, mem2, w["xa0"], "xa0", nb, w["mlp0"]["norm"])
    h, hn, s_mlp0 = _mlp_fwd(h, hn, w["mlp0"], "mlp0", w["cd"]["norm"])
    h, hn, s_cd = _cd_fwd(h, hn, w["cd"], nb, w["xa1"]["norm"])
    h, hn, s_xa1 = _xa_fwd(h, hn, mem2, w["xa1"], "xa1", nb, w["mlp1"]["norm"])
    h, _, s_mlp1 = _mlp_fwd(h, hn, w["mlp1"], "mlp1", None)

    dh, dhb, loss, dg_final = _loss_and_grad(h, tgt, w["final_norm"], name="loss")
    grads = {"final_norm": dg_final}
    dh, dhb, grads["mlp1"] = _mlp_bwd(dh, dhb, w["mlp1"], s_mlp1, "mlp1")
    dh, dhb, grads["xa1"] = _xa_bwd(dh, dhb, mem2, w["xa1"], s_xa1, "xa1", nb)
    dh, dhb, grads["cd"] = _cd_bwd(dh, dhb, w["cd"], s_cd, nb)
    if grads_ready is not None:
        grads_ready(0, grads)
    dh, dhb, grads["mlp0"] = _mlp_bwd(dh, dhb, w["mlp0"], s_mlp0, "mlp0")
    dh, dhb, grads["xa0"] = _xa_bwd(dh, dhb, mem2, w["xa0"], s_xa0, "xa0", nb)
    if grads_ready is not None:
        grads_ready(1, grads)
    dh, dhb, grads["ab"] = _ab_bwd(dh, dhb, w["ab"], s_ab, nb)
    return loss, dh.reshape(nb, SEQ, D_MODEL), grads


ANY = pl.BlockSpec(memory_space=pl.ANY)


def _place():
    x, y, c = lax.axis_index("x"), lax.axis_index("y"), lax.axis_index("c")
    return x, y, c, [(1 - x, y), (x, 1 - y), (1 - x, 1 - y)]


def _gather_chips(shard):
    half = shard.shape[0] // 2

    def body(src, out, send_sems, recv_sems):
        x, y, c, chips = _place()
        sibling = (x, y, 1 - c)

        def rows(slot, core):
            return out.at[slot, pl.ds(core * half, half)]

        def copy(k, src_ref, dst_ref, to):
            return pltpu.make_async_remote_copy(src_ref=src_ref, dst_ref=dst_ref, send_sem=send_sems.at[k],
                                                recv_sem=recv_sems.at[k], device_id=to, device_id_type=MESH)

        first = [copy(k, src.at[pl.ds(c * half, half)], rows(2 * x + y, c), (px, py, c)) for k, (px, py) in enumerate(chips)]
        for cp in first:
            cp.start()
        passed = [copy(3 + k, rows(2 * px + py, c), rows(2 * px + py, c), sibling) for k, (px, py) in enumerate(chips)]
        for k, (px, py) in enumerate(chips):
            copy(k, src.at[pl.ds(c * half, half)], rows(2 * px + py, c), (px, py, c)).wait_recv()
            passed[k].start()
        for k, (px, py) in enumerate(chips):
            copy(3 + k, rows(2 * px + py, 1 - c), rows(2 * px + py, 1 - c), sibling).wait_recv()
        for cp in first + passed:
            cp.wait_send()

    return pl.pallas_call(
        body, out_shape=jax.ShapeDtypeStruct((N_CHIPS,) + shard.shape, shard.dtype), in_specs=[ANY], out_specs=ANY,
        scratch_shapes=[pltpu.SemaphoreType.DMA((6,)), pltpu.SemaphoreType.DMA((6,))], name="gather_chips",
    )(shard)


HBM = pl.BlockSpec(memory_space=pltpu.HBM)
SEM = pl.BlockSpec(memory_space=pltpu.SEMAPHORE)
SIDE_EFFECT = pltpu.SideEffectType.DATAFLOW_SIDE_EFFECTING


def _chip_copies(src, land, send_sems, recv_sems):
    half = src.shape[0] // 2
    x, y, c, chips = _place()

    def copy(k, slot, to):
        return pltpu.make_async_remote_copy(src_ref=src.at[pl.ds(c * half, half)], dst_ref=land.at[slot, pl.ds(c * half, half)],
                                            send_sem=send_sems[k], recv_sem=recv_sems[k], device_id=to, device_id_type=MESH)
    out = [copy(k, 2 * x + y, (px, py, c)) for k, (px, py) in enumerate(chips)]
    back = [copy(k, 2 * px + py, (px, py, c)) for k, (px, py) in enumerate(chips)]
    return out, back


def _gather_start(shard, after):
    def body(src, land, *rest):
        rest = rest[len(after):]
        sems, token = rest[:6], rest[8]
        out, _ = _chip_copies(src, land, sems[:3], sems[3:])
        for cp in out:
            cp.start()
        token[...] = jnp.zeros(token.shape, F32)

    sem = pltpu.SemaphoreType.DMA(())
    land_shape = (N_CHIPS,) + shard.shape
    return pl.pallas_call(
        body, name="gather_start",
        out_shape=(sem,) * 6 + (pltpu.HBM(shard.shape, shard.dtype), pltpu.HBM(land_shape, shard.dtype),
                                jax.ShapeDtypeStruct((8, LANES), F32)),
        in_specs=(HBM, HBM) + (pl.BlockSpec(memory_space=pl.ANY),) * len(after),
        out_specs=(SEM,) * 6 + (HBM, HBM, pl.BlockSpec(memory_space=pltpu.VMEM)),
        input_output_aliases={0: 6, 1: 7}, compiler_params=pltpu.CompilerParams(has_side_effects=SIDE_EFFECT),
    )(pltpu.with_memory_space_constraint(shard, pltpu.HBM),
      pltpu.with_memory_space_constraint(lax.empty(land_shape, shard.dtype), pltpu.HBM), *after)


def _gather_wait(started, after):
    sems, src_thru, land_thru = started[:6], started[6], started[7]

    def body(src, land, *rest):
        out, back = _chip_copies(src, land, rest[:3], rest[3:6])
        for cp in out:
            cp.wait_send()
        for cp in back:
            cp.wait_recv()

    return pl.pallas_call(
        body, name="gather_wait",
        out_shape=(pltpu.HBM(src_thru.shape, src_thru.dtype), pltpu.HBM(land_thru.shape, land_thru.dtype)),
        in_specs=(HBM, HBM) + (SEM,) * 6 + (pl.BlockSpec(memory_space=pl.ANY),), out_specs=(HBM, HBM),
        input_output_aliases={0: 0, 1: 1}, compiler_params=pltpu.CompilerParams(has_side_effects=SIDE_EFFECT),
    )(src_thru, land_thru, *sems, after)


def _gather_pass(land):
    half = land.shape[1] // 2

    def body(land_in, land_out, send_sems, recv_sems):
        x, y, c, chips = _place()

        def copy(k, slot, core):
            rows = land_out.at[slot, pl.ds(core * half, half)]
            return pltpu.make_async_remote_copy(src_ref=rows, dst_ref=rows, send_sem=send_sems.at[k], recv_sem=recv_sems.at[k],
                                                device_id=(x, y, 1 - c), device_id_type=MESH)
        sends = [copy(k, 2 * px + py, c) for k, (px, py) in enumerate(chips)]
        for cp in sends:
            cp.start()
        for k, (px, py) in enumerate(chips):
            copy(k, 2 * px + py, 1 - c).wait_recv()
        for cp in sends:
            cp.wait_send()

    return pl.pallas_call(
        body, out_shape=jax.ShapeDtypeStruct(land.shape, land.dtype), in_specs=[ANY], out_specs=ANY,
        scratch_shapes=[pltpu.SemaphoreType.DMA((3,)), pltpu.SemaphoreType.DMA((3,))], input_output_aliases={0: 0},
        name="gather_pass",
    )(land)


def _swap_cores(block, *, name, half_rows=None):
    shape = block.shape if half_rows is None else (block.shape[0], half_rows, block.shape[2])

    def body(src, out, send_sem, recv_sem):
        x, y, c, _ = _place()
        part = src if half_rows is None else src.at[:, pl.ds((1 - c) * half_rows, half_rows)]
        cp = pltpu.make_async_remote_copy(src_ref=part, dst_ref=out, send_sem=send_sem, recv_sem=recv_sem,
                                          device_id=(x, y, 1 - c), device_id_type=MESH)
        cp.start()
        cp.wait()

    return pl.pallas_call(
        body, out_shape=jax.ShapeDtypeStruct(shape, block.dtype), in_specs=[ANY], out_specs=ANY,
        scratch_shapes=[pltpu.SemaphoreType.DMA(()), pltpu.SemaphoreType.DMA(())], name=name,
    )(block)


def _scatter_copies(parts, land, send_sems, recv_sems):
    x, y, c, chips = _place()
    return [pltpu.make_async_remote_copy(src_ref=parts.at[2 * px + py], dst_ref=land.at[k], send_sem=send_sems[k],
                                         recv_sem=recv_sems[k], device_id=(px, py, c), device_id_type=MESH)
            for k, (px, py) in enumerate(chips)]


def _scatter_start(parts, tag):
    def body(src, land, *rest):
        for cp in _scatter_copies(src, land, rest[:3], rest[3:6]):
            cp.start()
        rest[8][...] = jnp.zeros(rest[8].shape, F32)

    sem = pltpu.SemaphoreType.DMA(())
    land_shape = (3,) + parts.shape[1:]
    return pl.pallas_call(
        body, name=f"scatter_start_{tag}",
        out_shape=(sem,) * 6 + (pltpu.HBM(parts.shape, parts.dtype), pltpu.HBM(land_shape, parts.dtype),
                                jax.ShapeDtypeStruct((8, LANES), F32)),
        in_specs=(HBM, HBM), out_specs=(SEM,) * 6 + (HBM, HBM, pl.BlockSpec(memory_space=pltpu.VMEM)),
        input_output_aliases={0: 6, 1: 7}, compiler_params=pltpu.CompilerParams(has_side_effects=SIDE_EFFECT),
    )(pltpu.with_memory_space_constraint(parts, pltpu.HBM),
      pltpu.with_memory_space_constraint(lax.empty(land_shape, parts.dtype), pltpu.HBM))


def _scatter_wait(started, after, tag):
    def body(src, land, *rest):
        for cp in _scatter_copies(src, land, rest[:3], rest[3:6]):
            cp.wait_send()
            cp.wait_recv()

    src_thru, land_thru = started[6], started[7]
    return pl.pallas_call(
        body, name=f"scatter_wait_{tag}",
        out_shape=(pltpu.HBM(src_thru.shape, src_thru.dtype), pltpu.HBM(land_thru.shape, land_thru.dtype)),
        in_specs=(HBM, HBM) + (SEM,) * 6 + (pl.BlockSpec(memory_space=pl.ANY),), out_specs=(HBM, HBM),
        input_output_aliases={0: 0, 1: 1}, compiler_params=pltpu.CompilerParams(has_side_effects=SIDE_EFFECT),
    )(src_thru, land_thru, *started[:6], after)[1]


def _sum_all_devices(vec):
    rows = vec.shape[0]

    def body(x_ref, total_ref, all_ref, send_sems, recv_sems, local_sem):
        x, y, c, chips = _place()
        me, sibling = (x, y, c), (x, y, 1 - c)

        def slot(px, py, pc):
            return all_ref.at[4 * px + 2 * py + pc]

        def copy(k, block, to, src=None):
            return pltpu.make_async_remote_copy(src_ref=slot(*block) if src is None else src, dst_ref=slot(*block),
                                                send_sem=send_sems.at[k], recv_sem=recv_sems.at[k], device_id=to,
                                                device_id_type=MESH)

        mine = pltpu.make_async_copy(x_ref, slot(*me), local_sem)
        mine.start()
        first = [copy(0, me, sibling, src=x_ref)] + [copy(1 + j, me, (*chip, c), src=x_ref) for j, chip in enumerate(chips)]
        for cp in first:
            cp.start()
        passed = [copy(4 + j, (*chip, c), sibling) for j, chip in enumerate(chips)]
        for j, chip in enumerate(chips):
            copy(1 + j, (*chip, c), me).wait_recv()
            passed[j].start()
        copy(0, sibling, me).wait_recv()
        for j, chip in enumerate(chips):
            copy(4 + j, (*chip, 1 - c), me).wait_recv()
        for cp in first + passed:
            cp.wait_send()
        mine.wait()
        acc = all_ref[0]
        for d in range(1, 8):
            acc = acc + all_ref[d]
        total_ref[...] = acc

    vmem = pl.BlockSpec(memory_space=pltpu.VMEM)
    return pl.pallas_call(
        body, out_shape=[jax.ShapeDtypeStruct((rows, LANES), F32), jax.ShapeDtypeStruct((8, rows, LANES), F32)],
        in_specs=[vmem], out_specs=[vmem, vmem],
        scratch_shapes=[pltpu.SemaphoreType.DMA((7,)), pltpu.SemaphoreType.DMA((7,)), pltpu.SemaphoreType.DMA(())],
        name="sum_all_devices", compiler_params=pltpu.CompilerParams(vmem_limit_bytes=VMEM_LIMIT_BYTES),
    )(vec)[0]


PACK_COLS = 1024
BIG = (("mlp_w_up", 2, 1024, True), ("mlp_w_down", 2, 1024, False), ("xa_w_kv", 2, 512, True), ("xa_w_q", 2, 256, False),
       ("xa_w_o", 2, 256, False), ("ab_w_out", 1, 256, False), ("cd_w_out", 1, 256, False), ("cd_w_in", 1, 576, True),
       ("ab_w_in", 1, 642, True))
ENTRIES = tuple((name, layer, rows, transposed) for name, layers, rows, transposed in BIG for layer in range(layers))
FIRST_ENTRIES = tuple(e for e in ENTRIES if e[0].startswith("ab_"))
LATER_ENTRIES = tuple(e for e in ENTRIES if not e[0].startswith("ab_"))


def _tile_rows(entry):
    return -(-entry[2] // 16) * 16


def _padded_rows(entries):
    return -(-sum(_tile_rows(e) for e in entries) // 32) * 32


SMALL = (("ab_norm", (1, 1024)), ("ab_conv_w", (1, 4, 512)), ("ab_conv_b", (1, 512)), ("lru_w_a", (1, 8, 64, 64)),
         ("lru_b_a", (1, 512)), ("lru_w_i", (1, 8, 64, 64)), ("lru_b_i", (1, 512)), ("lru_lambda", (1, 512)),
         ("fox_b_f", (1, 8)), ("cd_norm", (1, 1024)), ("cd_sink", (1, 8)), ("xa_norm", (2, 1024)),
         ("xa_mem_norm", (2, 1024)), ("mlp_norm", (2, 1024)), ("final_norm", (1024,)), ("loss", (1,)))
SPLIT_SMALL = ("ab_conv_w", "cd_norm")


def _pack_rows(parts, entries, dtype):
    lead = parts[0].shape[:-2]
    zeros = lambda rows: jnp.zeros(lead + (rows, PACK_COLS), dtype)
    pieces = [jnp.pad(p.astype(dtype), ((0, 0),) * len(lead) + ((0, _tile_rows(e) - e[2]), (0, 0))) for p, e in zip(parts, entries)]
    tail = _padded_rows(entries) - sum(_tile_rows(e) for e in entries)
    return jnp.concatenate(pieces + ([zeros(tail)] if tail else []), axis=len(lead))


def _unpack_rows(packed, entries):
    out, r0 = [], 0
    for e in entries:
        out.append(packed[..., r0:r0 + e[2], :])
        r0 += _tile_rows(e)
    return out


def _shard_rows(w, entries):
    return [(w[name][layer].T if transposed else w[name][layer]) for name, layer, _, transposed in entries]


def _shard_blocks(rows):
    out = {}
    for (name, layer, _, transposed), r in zip(ENTRIES, rows):
        out.setdefault(name, []).append(r.T if transposed else r)
    return {name: jnp.stack(layers) for name, layers in out.items()}


def _pack_small(vals):
    flat = jnp.concatenate([vals[name].astype(F32).reshape(-1) for name, _ in SMALL])
    size = -(-flat.shape[0] // (8 * LANES)) * (8 * LANES)
    return jnp.pad(flat, (0, size - flat.shape[0])).reshape(-1, LANES)


def _unpack_small(packed):
    flat, out, at = packed.reshape(-1), {}, 0
    for name, shape in SMALL:
        out[name] = flat[at:at + math.prod(shape)].reshape(shape)
        at += math.prod(shape)
    return out


def _chip_part(full, chip):
    width = full.shape[-1] // N_CHIPS
    return lax.dynamic_slice_in_dim(full, chip * width, width, axis=full.ndim - 1)


def _chip_embed(part, chip):
    full = jnp.zeros(part.shape[:-1] + (part.shape[-1] * N_CHIPS,), part.dtype)
    return lax.dynamic_update_slice_in_dim(full, part, chip * part.shape[-1], axis=part.ndim - 1)


SUBLAYER_KEYS = {"ab_w_in": ("ab", "w_in_t"), "ab_w_out": ("ab", "w_out"), "cd_w_in": ("cd", "w_in_t"), "cd_w_out": ("cd", "w_out"),
                 "xa_w_q": ("xa", "w_q"), "xa_w_kv": ("xa", "w_kv_t"), "xa_w_o": ("xa", "w_o"), "mlp_w_up": ("mlp", "w_up_t"),
                 "mlp_w_down": ("mlp", "w_down")}


def _sublayer(name, layer):
    block, leaf = SUBLAYER_KEYS[name]
    return (block if block in ("ab", "cd") else f"{block}{layer}"), leaf


def _set_big(params, full_rows):
    for (name, layer), rows in full_rows.items():
        block, leaf = _sublayer(name, layer)
        if name == "ab_w_in":
            rows = jnp.concatenate([rows, jnp.zeros((LANES - 8, PACK_COLS), rows.dtype)])
        params[block][leaf] = rows


def _build_params(full_rows, w):
    pad = lambda a: jnp.pad(a, ((0, 0), (0, LANES - a.shape[1])))
    params = {
        "ab": dict(norm=w["ab_norm"], conv_w=w["ab_conv_w"][0], conv_b=w["ab_conv_b"], w_a=_block_diag(w["lru_w_a"][0]),
                   b_a=w["lru_b_a"], w_i=_block_diag(w["lru_w_i"][0]), b_i=w["lru_b_i"], lam=w["lru_lambda"],
                   b_f=w["fox_b_f"].reshape(8, 1)),
        "cd": dict(norm=w["cd_norm"], sink=pad(w["cd_sink"])),
        "final_norm": w["final_norm"].reshape(1, D_MODEL),
    }
    for layer in range(2):
        params[f"xa{layer}"] = dict(norm=w["xa_norm"][layer:layer + 1], mem_norm=w["xa_mem_norm"][layer:layer + 1])
        params[f"mlp{layer}"] = dict(norm=w["mlp_norm"][layer:layer + 1])
    _set_big(params, full_rows)
    return params


def _grad_rows(g, entries=ENTRIES):
    out = []
    for name, layer, _, _ in entries:
        block, leaf = _sublayer(name, layer)
        out.append(g[block][leaf])
    return out


def _reduce_group(entry):
    name, layer = entry[0], entry[1]
    if name.startswith("ab_"):
        return 2
    return 0 if layer == 1 or name.startswith("cd_") else 1


REDUCE_GROUPS = tuple(tuple(e for e in ENTRIES if _reduce_group(e) == group) for group in range(3))


def _reduce_tile(half):
    return max(t for t in range(16, 1025, 16) if half % t == 0)


def _reduce_start(grads_by_chip, core, chip, tag):
    half = grads_by_chip.shape[1] // 2
    tile = _reduce_tile(half)
    steps = half // tile
    place = jnp.stack([core, chip]).astype(jnp.int32)
    got = _swap_cores(grads_by_chip, name=f"swap_cores_{tag}", half_rows=half)
    block = (1, tile, PACK_COLS)

    def sum_cores(place_ref, mine_ref, got_ref, f32_ref, bf16_ref):
        total = mine_ref[...].astype(F32) + got_ref[...].astype(F32)
        f32_ref[...] = total
        bf16_ref[...] = total.astype(BF16)

    pair32, pair16 = pl.pallas_call(
        sum_cores, name=f"sum_cores_{tag}",
        grid_spec=pltpu.PrefetchScalarGridSpec(
            num_scalar_prefetch=1, grid=(N_CHIPS, steps),
            in_specs=[pl.BlockSpec(block, lambda s, i, place_ref: (s, place_ref[0] * steps + i, 0)),
                      pl.BlockSpec(block, lambda s, i, place_ref: (s, i, 0))],
            out_specs=[pl.BlockSpec(block, lambda s, i, place_ref: (s, i, 0))] * 2),
        out_shape=[jax.ShapeDtypeStruct((N_CHIPS, half, PACK_COLS), F32), jax.ShapeDtypeStruct((N_CHIPS, half, PACK_COLS), BF16)],
        compiler_params=_params("arbitrary", "arbitrary"),
    )(place, grads_by_chip, got)
    return pair32, _scatter_start(pair16, tag), place


def _reduce_finish(state, core, tag, after):
    pair32, started, place = state
    half = pair32.shape[1]
    tile = _reduce_tile(half)
    landed = _scatter_wait(started, after, tag)

    def sum_chips(place_ref, own_ref, landed_ref, out_ref):
        out_ref[...] = own_ref[0] + landed_ref[0].astype(F32) + landed_ref[1].astype(F32) + landed_ref[2].astype(F32)

    done = pl.pallas_call(
        sum_chips, name=f"sum_chips_{tag}",
        grid_spec=pltpu.PrefetchScalarGridSpec(
            num_scalar_prefetch=1, grid=(half // tile,),
            in_specs=[pl.BlockSpec((1, tile, PACK_COLS), lambda i, place_ref: (place_ref[1], i, 0)),
                      pl.BlockSpec((3, tile, PACK_COLS), lambda i, place_ref: (0, i, 0))],
            out_specs=pl.BlockSpec((tile, PACK_COLS), lambda i, place_ref: (i, 0))),
        out_shape=jax.ShapeDtypeStruct((half, PACK_COLS), F32), compiler_params=_params("arbitrary"),
    )(place, pair32, landed)
    theirs = _swap_cores(done, name=f"share_halves_{tag}")
    return jnp.where(core == 0, jnp.concatenate([done, theirs]), jnp.concatenate([theirs, done]))


def kernel(x, mem, ab_norm, ab_w_in, ab_conv_w, ab_conv_b, lru_w_a, lru_b_a, lru_w_i, lru_b_i, lru_lambda, fox_b_f, ab_w_out, cd_norm, cd_w_in, cd_sink, cd_w_out, xa_norm, xa_mem_norm, xa_w_q, xa_w_kv, xa_w_o, mlp_norm, mlp_w_up, mlp_w_down, final_norm, loss_target, m_ab_norm, m_ab_w_in, m_ab_conv_w, m_ab_conv_b, m_lru_w_a, m_lru_b_a, m_lru_w_i, m_lru_b_i, m_lru_lambda, m_fox_b_f, m_ab_w_out, m_cd_norm, m_cd_w_in, m_cd_sink, m_cd_w_out, m_xa_norm, m_xa_mem_norm, m_xa_w_q, m_xa_w_kv, m_xa_w_o, m_mlp_norm, m_mlp_w_up, m_mlp_w_down, m_final_norm, v_ab_norm, v_ab_w_in, v_ab_conv_w, v_ab_conv_b, v_lru_w_a, v_lru_b_a, v_lru_w_i, v_lru_b_i, v_lru_lambda, v_fox_b_f, v_ab_w_out, v_cd_norm, v_cd_w_in, v_cd_sink, v_cd_w_out, v_xa_norm, v_xa_mem_norm, v_xa_w_q, v_xa_w_kv, v_xa_w_o, v_mlp_norm, v_mlp_w_up, v_mlp_w_down, v_final_norm):
    given = dict(locals())
    weight_names = [name for name, _ in SMALL if name != "loss"] + [name for name, _, _, _ in BIG]
    w = {name: given[name] for name in weight_names}
    chip = 2 * lax.axis_index("x") + lax.axis_index("y")
    core = lax.axis_index("c")

    slot = lax.broadcasted_iota(jnp.int32, (N_CHIPS, 1, 1), 0)

    def whole(entries, mine, gathered):
        return {(name, layer): jnp.where(slot == chip, own[None], got).reshape(N_CHIPS * rows, PACK_COLS)
                for (name, layer, rows, _), own, got in zip(entries, _unpack_rows(mine, entries), _unpack_rows(gathered, entries))}

    mine_first = _pack_rows(_shard_rows(w, FIRST_ENTRIES), FIRST_ENTRIES, BF16)
    mine_later = _pack_rows(_shard_rows(w, LATER_ENTRIES), LATER_ENTRIES, BF16)
    first = _gather_chips(mine_first)
    owner = (core == 0).astype(F32)
    quarters = {name: _chip_embed(w[name], chip) * owner for name in SPLIT_SMALL}
    zeros = {name: jnp.zeros(shape, F32) for name, shape in SMALL}
    small_packed = _sum_all_devices(_pack_small({**zeros, **quarters}))
    small_full = _unpack_small(small_packed)
    started = _gather_start(mine_later, after=(small_packed, first))
    params = _build_params(whole(FIRST_ENTRIES, mine_first, first),
                           {**w, "ab_conv_w": small_full["ab_conv_w"], "cd_norm": small_full["cd_norm"]})
    params["ab"]["norm"] = params["ab"]["norm"] + started[8][0, 0]

    def complete(h):
        mine, landed = _gather_wait(started, after=h)
        _set_big(params, whole(LATER_ENTRIES, mine, _gather_pass(landed)))

    reducing = {}

    def grads_ready(group, g):
        entries = REDUCE_GROUPS[group]
        by_chip = _pack_rows([r.reshape(N_CHIPS, e[2], PACK_COLS) for r, e in zip(_grad_rows(g, entries), entries)], entries, BF16)
        reducing[group] = _reduce_start(by_chip, core, chip, f"g{group}")
        if group < 2:
            block, leaf = ("mlp0", "w_down") if group == 0 else ("ab", "w_out")
            params[block][leaf] = params[block][leaf] + reducing[group][1][8][0, 0].astype(BF16)

    loss_part, grad_x, g = _local_step(x, mem, loss_target, params, complete, grads_ready)
    grads_ready(2, g)
    reduced = {}
    for group, entries in enumerate(REDUCE_GROUPS):
        rows = _unpack_rows(_reduce_finish(reducing[group], core, f"g{group}", after=grad_x), entries)
        reduced.update({(e[0], e[1]): r for e, r in zip(entries, rows)})
    grads = _shard_blocks([reduced[e[0], e[1]] for e in ENTRIES])
    pair = lambda key, leaf: jnp.stack([g[f"{key}0"][leaf], g[f"{key}1"][leaf]])

    small_local = {"ab_norm": g["ab"]["norm"], "ab_conv_w": g["ab"]["conv_w"], "ab_conv_b": g["ab"]["conv_b"],
                   "lru_w_a": g["ab"]["w_a"], "lru_b_a": g["ab"]["b_a"], "lru_w_i": g["ab"]["w_i"], "lru_b_i": g["ab"]["b_i"],
                   "lru_lambda": g["ab"]["lam"], "fox_b_f": g["ab"]["b_f"], "cd_norm": g["cd"]["norm"], "cd_sink": g["cd"]["sink"],
                   "xa_norm": pair("xa", "norm"), "xa_mem_norm": pair("xa", "mem_norm"), "mlp_norm": pair("mlp", "norm"),
                   "final_norm": g["final_norm"], "loss": loss_part[0, :1]}
    small_sum_packed = _sum_all_devices(_pack_small(small_local))
    small_sum = _unpack_small(small_sum_packed)
    loss = small_sum["loss"][0]

    delta, new_m, new_v = {}, {}, {}
    for name, _, _, _ in BIG:
        shape = w[name].shape
        flat = lambda a: a.reshape(-1, shape[-1])
        d, m2, v2 = _adamw(flat(w[name]), flat(grads[name]), flat(given["m_" + name]), flat(given["v_" + name]), name=f"adamw_{name}")
        delta[name], new_m[name], new_v[name] = d.reshape(shape), m2.reshape(shape), v2.reshape(shape)

    def small_state(prefix):
        vals = {name: (given[prefix + name] if name != "loss" else jnp.zeros((1,), F32)) for name, _ in SMALL}
        for name in SPLIT_SMALL:
            vals[name] = _chip_embed(vals[name], chip)
        return _pack_small(vals)
    packed = _adamw(small_state(""), small_sum_packed, small_state("m_"), small_state("v_"), name="adamw_small")
    for store, vals in zip((delta, new_m, new_v), packed):
        for name, val in _unpack_small(vals).items():
            store[name] = _chip_part(val, chip) if name in SPLIT_SMALL else val
    for name in SPLIT_SMALL:
        small_sum[name] = _chip_part(small_sum[name], chip)
    grads.update({name: small_sum[name] for name, _ in SMALL})

    order = ["ab_norm", "ab_w_in", "ab_conv_w", "ab_conv_b", "lru_w_a", "lru_b_a", "lru_w_i", "lru_b_i", "lru_lambda", "fox_b_f",
             "ab_w_out", "cd_norm", "cd_w_in", "cd_sink", "cd_w_out", "xa_norm", "xa_mem_norm", "xa_w_q", "xa_w_kv", "xa_w_o",
             "mlp_norm", "mlp_w_up", "mlp_w_down", "final_norm"]
    return (loss, grad_x, *[grads[n] for n in order], *[delta[n] for n in order], *[new_m[n] for n in order],
            *[new_v[n] for n in order])
```

```python
import functools
import math

import jax
import jax.numpy as jnp
from jax import lax
from jax.experimental import pallas as pl
from jax.experimental.pallas import tpu as pltpu

F32 = jnp.float32
BF16 = jnp.bfloat16
MESH = pl.DeviceIdType.MESH

D_MODEL = 1024
SEQ = 2048
HEAD_DIM = 64
LRU_WIDTH = 512
LRU_BLOCKS = 8
LRU_C = 8.0
CONV_WIDTH = 4
ATT_W = 512
SWA_KW = 128
SWA_WINDOW = 128
DIL_PATTERN = ((128, 1), (512, 4), (2048, 16))
MEM_LEN = 256
XA_HEADS = 4
XA_HEAD_DIM = 256
D_FF = 4096
ROPE_THETA = 10000.0
EPS = 1e-6
N_CHIPS = 4
LANES = 128

ADAM_LR, ADAM_B1, ADAM_B2, ADAM_EPS, ADAM_WD, ADAM_STEP = 0.001, 0.9, 0.999, 1e-08, 0.01, 10

VMEM_LIMIT_BYTES = 56 * 1024 * 1024
MASKED = -1e30
ROW_TILE = 512
LRU_CHUNK = 512
ATT_BLOCK = 128
BAND_ROWS = 256
ATT_CHUNK = 512
CAUSAL_ROWS = 256
CLASS_ROWS = 512


def _params(*sem):
    return pltpu.CompilerParams(dimension_semantics=sem, vmem_limit_bytes=VMEM_LIMIT_BYTES)


def _rowwise(fn, rows, consts=(), out_rows=(), out_accs=(), *, name, tile=ROW_TILE, row_maps=None):
    rows = [r if isinstance(r, tuple) else (r, r.shape[1], 0) for r in rows]
    consts = list(consts)
    nr, nc, no = len(rows), len(consts), len(out_rows)
    dealt_in = [r[3] if isinstance(r[0], str) else None for r in rows]
    dealt_out = [o[2] if len(o) == 3 else None for o in out_rows]
    total = next(r[0].shape[0] for r in rows if not isinstance(r[0], str))
    tile = min(tile, total)
    assert total % tile == 0
    n = total // tile
    n_acc = len(out_accs)

    def body(*refs):
        scratch = list(refs[nr + nc + no + n_acc:])
        vals = []
        for ref, d, row in zip(refs[:nr], dealt_in, rows):
            if d is None:
                vals.append(ref[...])
                continue
            sc, width = scratch.pop(0), row[2]
            for r in range(d):
                for c in range(width // LANES):
                    lanes = slice(r * width + c * LANES, r * width + (c + 1) * LANES)
                    sc.at[c][pl.ds(r, tile // d, stride=d), :] = ref[:, lanes].astype(F32)
            vals.append(jnp.concatenate([sc[c] for c in range(width // LANES)], axis=1))
        outs = fn(*vals, *[r[...] for r in refs[nr:nr + nc]])
        outs = tuple(outs) if isinstance(outs, (tuple, list)) else (outs,)
        for ref, val, d, spec in zip(refs[nr + nc:nr + nc + no], outs[:no], dealt_out, out_rows):
            if d is None:
                ref[...] = val.astype(ref.dtype)
                continue
            sc, width = scratch.pop(0), spec[0]
            for c in range(width // LANES):
                sc[c] = val[:, c * LANES:(c + 1) * LANES].astype(F32)
            for r in range(d):
                for c in range(width // LANES):
                    lanes = slice(r * width + c * LANES, r * width + (c + 1) * LANES)
                    ref[:, lanes] = sc.at[c][pl.ds(r, tile // d, stride=d), :].astype(ref.dtype)
        for ref, val in zip(refs[nr + nc + no:nr + nc + no + n_acc], outs[no:]):
            @pl.when(pl.program_id(0) == 0)
            def _(ref=ref):
                ref[...] = jnp.zeros(ref.shape, ref.dtype)
            ref[...] += val

    in_specs, args, scratch_shapes = [], [], []
    for idx, row in enumerate(rows):
        if isinstance(row[0], str):
            _, arr, width, d = row
            in_specs.append(pl.BlockSpec((tile // d, d * width), lambda i: (i, 0)))
            scratch_shapes.append(pltpu.VMEM((width // LANES, tile, LANES), F32))
        elif row_maps is not None and row_maps[idx] is not None:
            arr, width, _ = row
            in_specs.append(pl.BlockSpec((tile, width), row_maps[idx]))
        else:
            arr, width, cb = row
            in_specs.append(pl.BlockSpec((tile, width), functools.partial(lambda i, cb: (i, cb), cb=cb)))
        args.append(arr)
    in_specs += [pl.BlockSpec(c.shape, lambda i: (0, 0)) for c in consts]
    out_shape, out_specs = [], []
    for spec, d in zip(out_rows, dealt_out):
        w, dt = spec[0], spec[1]
        if d is None:
            out_shape.append(jax.ShapeDtypeStruct((total, w), dt))
            out_specs.append(pl.BlockSpec((tile, w), lambda i: (i, 0)))
        else:
            out_shape.append(jax.ShapeDtypeStruct((total // d, d * w), dt))
            out_specs.append(pl.BlockSpec((tile // d, d * w), lambda i: (i, 0)))
            scratch_shapes.append(pltpu.VMEM((w // LANES, tile, LANES), F32))
    out_shape += [jax.ShapeDtypeStruct(s, F32) for s in out_accs]
    out_specs += [pl.BlockSpec(s, lambda i: (0, 0)) for s in out_accs]
    return pl.pallas_call(
        body, grid=(n,), in_specs=in_specs, out_specs=out_specs, out_shape=out_shape, scratch_shapes=scratch_shapes, name=name,
        compiler_params=_params("arbitrary"),
    )(*args, *consts)


MATMUL_VMEM_BYTES = 40 * 1024 * 1024


def _matmul_tiles(m, n, k, tm, tn, out_bytes):
    tm, tn, tk = min(tm, m), min(tn, n), k

    def need():
        return 2 * (2 * tk * (tm + tn) + tm * tn * out_bytes) + (0 if tk == k else 4 * tm * tn)

    while need() > MATMUL_VMEM_BYTES:
        if tm >= tn and tm % 256 == 0:
            tm //= 2
        elif tn % 256 == 0:
            tn //= 2
        else:
            tk //= 2
    return tm, tn, tk


def _matmul(a, b, *, ta=False, tb=False, outs=(F32,), epilogue=None, extras=(), consts=(), sums=(), whole_rows=False,
            tm=1024, tn=1024, name):
    m, k = (a.shape[1], a.shape[0]) if ta else a.shape
    n = b.shape[0] if tb else b.shape[1]
    assert (b.shape[1] if tb else b.shape[0]) == k
    outs = [o if isinstance(o, tuple) else (o, None) for o in outs]
    out_bytes = sum(jnp.dtype(dt).itemsize for dt, w in outs if w is None) + sum(e.dtype.itemsize for e in extras)
    tm, tn, tk = _matmul_tiles(m, n, k, tm, tn, out_bytes)
    assert m % tm == 0 and n % tn == 0 and k % tk == 0, (name, m, n, k)
    nk = k // tk
    ne, nc, no = len(extras), len(consts), len(outs)
    assert tn == n or not (sums or whole_rows or any(w is not None for _, w in outs)), name
    dims = (((0 if ta else 1,), (1 if tb else 0,)), ((), ()))

    def body(*refs):
        a_ref, b_ref = refs[:2]
        e_refs, o_refs = refs[2:2 + ne + nc], refs[2 + ne + nc:2 + ne + nc + no]
        s_refs = refs[2 + ne + nc + no:2 + ne + nc + no + len(sums)]

        def finish(acc):
            vals = (acc,) if epilogue is None else epilogue(acc, *[e[...] for e in e_refs])
            for ref, val in zip(o_refs, vals[:no]):
                ref[...] = val.astype(ref.dtype)
            for ref, val in zip(s_refs, vals[no:]):
                @pl.when(pl.program_id(0) == 0)
                def _(ref=ref, val=val):
                    ref[...] = val

                @pl.when(pl.program_id(0) > 0)
                def _(ref=ref, val=val):
                    ref[...] += val

        prod = lax.dot_general(a_ref[...], b_ref[...], dims, preferred_element_type=F32)
        if nk == 1:
            finish(prod)
        else:
            acc_ref = refs[-1]
            step = pl.program_id(2)

            @pl.when(step == 0)
            def _():
                acc_ref[...] = prod

            @pl.when(step > 0)
            def _():
                acc_ref[...] += prod

            @pl.when(step == nk - 1)
            def _():
                finish(acc_ref[...])

    a_spec = pl.BlockSpec((tk, tm), lambda i, j, s: (s, i)) if ta else pl.BlockSpec((tm, tk), lambda i, j, s: (i, s))
    b_spec = pl.BlockSpec((tn, tk), lambda i, j, s: (j, s)) if tb else pl.BlockSpec((tk, tn), lambda i, j, s: (s, j))
    tile_spec = pl.BlockSpec((tm, tn), lambda i, j, s: (i, j))
    whole = lambda shape: pl.BlockSpec(shape, lambda i, j, s: (0, 0))
    return pl.pallas_call(
        body, grid=(m // tm, n // tn, nk),
        in_specs=[a_spec, b_spec] + [tile_spec] * ne + [whole(c.shape) for c in consts],
        out_specs=[tile_spec if w is None else pl.BlockSpec((tm, w), lambda i, j, s: (i, 0)) for _, w in outs]
        + [whole(shape) for shape in sums],
        out_shape=[jax.ShapeDtypeStruct((m, n if w is None else w), dt) for dt, w in outs]
        + [jax.ShapeDtypeStruct(shape, F32) for shape in sums],
        scratch_shapes=[pltpu.VMEM((tm, tn), F32)] if nk > 1 else [],
        name=name, compiler_params=_params("arbitrary" if sums else "parallel", "parallel", "arbitrary"),
    )(a, b, *extras, *consts)


def _split3(x):
    x1 = x.astype(BF16)
    r1 = x - x1.astype(F32)
    x2 = r1.astype(BF16)
    x3 = (r1 - x2.astype(F32)).astype(BF16)
    return x1, x2, x3


def _dot_exact(x, e):
    return sum(jnp.dot(p, e, preferred_element_type=F32) for p in _split3(x))


def _head_expand(heads, width):
    dh = width // heads
    rows = jnp.arange(LANES)[:, None]
    cols = jnp.arange(width)[None, :]
    return (cols // dh == rows).astype(BF16)


def _rstd(x):
    return lax.rsqrt(jnp.mean(x * x, axis=-1, keepdims=True) + EPS)


def _rms_fwd(x, gain, *, name):
    return _rowwise(lambda x, g: x * _rstd(x) * g, [x], [gain], [(x.shape[1], BF16)], name=name)[0]


def _rms_bwd_vals(x, dhn, gain):
    r = _rstd(x)
    xh = x * r
    dxh = dhn * gain
    dx = r * (dxh - xh * jnp.mean(dxh * xh, axis=-1, keepdims=True))
    return dx, jnp.sum(dhn * xh, axis=0, keepdims=True)


def _norm_input_grad(dz, w, x, dres, gain, *, tb=False, name):
    def epilogue(dhn, x, dres, g):
        dx, dg = _rms_bwd_vals(x, dhn, g)
        dh = dres + dx
        return dh, dh, dg
    return _matmul(dz, w, tb=tb, outs=(F32, BF16), extras=(x, dres), consts=(gain,), sums=((1, x.shape[1]),), epilogue=epilogue,
                   name=name)


def _loss_and_grad(h, target, gain, *, name):
    def fn(h, tgt, g):
        r = _rstd(h)
        err = h * r * g - tgt
        loss = 0.5 * jnp.sum(jnp.mean(err * err, axis=-1, keepdims=True), axis=0, keepdims=True)
        dx, dg = _rms_bwd_vals(h, err * (1.0 / D_MODEL), g)
        return dx, dx, jnp.broadcast_to(loss, (1, LANES)), dg
    d = h.shape[1]
    return _rowwise(fn, [h, target], [gain], [(d, F32), (d, BF16)], [(1, LANES), (1, d)], name=name)


def _adamw(w, g, m, v, *, name):
    rows, cols = w.shape
    tile = rows
    for cand in (512, 256, 128, 64, 32, 16, 8):
        if rows % cand == 0 and rows > cand:
            tile = cand
            break
    bc1 = 1.0 - ADAM_B1 ** ADAM_STEP
    bc2 = 1.0 - ADAM_B2 ** ADAM_STEP

    def fn(w, g, m, v):
        m2 = ADAM_B1 * m + (1.0 - ADAM_B1) * g
        v2 = ADAM_B2 * v + (1.0 - ADAM_B2) * (g * g)
        delta = -ADAM_LR * ((m2 / bc1) / (jnp.sqrt(v2 / bc2) + ADAM_EPS) + ADAM_WD * w)
        return delta, m2, v2
    return _rowwise(fn, [w, g, m, v], [], [(cols, F32)] * 3, name=name, tile=tile)


def _attn_specs(arr, width, cb, classes, rows_block, whole, together=1):
    ncols = arr.shape[2] // (classes * width)
    lanes, at = (width, lambda r: r * ncols + cb) if together == 1 else (together * ncols * width, lambda r: r)
    if whole:
        return pl.BlockSpec((1, arr.shape[1], lanes), lambda n, r, i: (n, 0, at(r)))
    return pl.BlockSpec((1, rows_block, lanes), lambda n, r, i: (n, i, at(r)))


def _class_window(refs, spans, together, cl):
    if together == 1:
        return refs
    return [ref.at[:, :, pl.ds((cl * ncols + cb) * width, width)] for ref, (ncols, cb, width) in zip(refs, spans)]


def _attn_tiles(mode, lq, lk, backward=False):
    if mode == "full":
        return min(lq, 256), min(lq, 256), lk
    if mode == "band":
        tq = min(BAND_ROWS, lq)
        rows = tq if backward else ATT_BLOCK
        return tq, rows, min(rows + ATT_BLOCK, lk)
    return CAUSAL_ROWS, CAUSAL_ROWS, ATT_CHUNK


def _window(mode, row0, j, rows, win, lk):
    if mode == "causal":
        return pl.multiple_of(j * win, win), row0 // win + 1
    if mode == "band":
        return pl.multiple_of(jnp.clip(row0 + rows - win, 0, lk - win), ATT_BLOCK), 1
    return 0, 1


def _allowed(mode, row0, start, rows, win, max_dist):
    if mode == "full":
        return None
    dist = (row0 + lax.broadcasted_iota(jnp.int32, (rows, win), 0)) - (start + lax.broadcasted_iota(jnp.int32, (rows, win), 1))
    ok = dist >= 0
    if max_dist is not None:
        ok = ok & (dist <= max_dist)
    return ok


def _head_groups(heads, dh):
    gw = max(LANES, dh)
    return gw, gw // dh, heads // (gw // dh)


def _own_lanes(rows, gw, dh, u):
    return lax.broadcasted_iota(jnp.int32, (rows, gw), 1) // dh == u


def _only_head(x, own, per, u):
    return x if per == 1 else jnp.where(own[u], x, jnp.zeros_like(x))


def _step_scores(qz, kw, kb_row, ok):
    s = lax.dot_general(qz, kw, (((1,), (1,)), ((), ())), preferred_element_type=F32)
    if kb_row is not None:
        s = s - kb_row
    if ok is not None:
        s = jnp.where(ok, s, MASKED)
    return s


def _attn_fwd(q, k, v, kb=None, *, classes=1, heads, dh, mode, max_dist=None, bf16_copy=False, name):
    (qa, qc), (ka, kc), (va, vc) = q, k, v
    n, lq, lk = qa.shape[0], qa.shape[1], ka.shape[1]
    width = heads * dh
    tq, rows, win = _attn_tiles(mode, lq, lk)
    gw, per, groups = _head_groups(heads, dh)
    scale = dh ** -0.5
    has_kb = kb is not None
    single = mode != "causal"
    together = min(classes, max(1, CLASS_ROWS // lq))
    ncols = lambda arr: arr.shape[2] // (classes * width)
    spans = [(ncols(qa), qc, width), (ncols(ka), kc, width), (ncols(va), vc, width), (1, 0, width), (1, 0, LANES), (1, 0, width)]

    def body(*refs):
        for cl in range(together):
            for sub in range(tq // rows):
                part(_class_window(refs, spans, together, cl), pl.program_id(2) * tq + sub * rows, slice(sub * rows, (sub + 1) * rows))

    def part(refs, row0, rs):
        q_ref, k_ref, v_ref = refs[:3]
        kb_ref = refs[3] if has_kb else None
        n_in = 4 if has_kb else 3
        o_ref, lse_ref = refs[n_in:n_in + 2]
        o16_ref = refs[n_in + 2] if bf16_copy else None
        lane = lax.broadcasted_iota(jnp.int32, (rows, LANES), 1)
        own = [_own_lanes(rows, gw, dh, u) for u in range(per)]

        def step(j, carry):
            m_in, l_in = carry
            m_out, l_out = m_in, l_in
            start, _ = _window(mode, row0, j, rows, win, lk)
            ok = _allowed(mode, row0, start, rows, win, max_dist)
            for g in range(groups):
                cols = slice(g * gw, (g + 1) * gw)
                qg = q_ref[0, rs, cols] * scale
                kw = k_ref[0, pl.ds(start, win), cols]
                vw = v_ref[0, pl.ds(start, win), cols]
                alpha_g = new_g = None
                for u in range(per):
                    h = g * per + u
                    kb_row = kb_ref[0, h, pl.ds(j, 1), :] if has_kb else None
                    s = _step_scores(_only_head(qg, own, per, u), kw, kb_row, ok)
                    m_new = jnp.max(s, axis=1, keepdims=True)
                    if not single:
                        m_old = m_in[:, h:h + 1]
                        m_new = jnp.maximum(m_old, m_new)
                        alpha = jnp.exp(m_old - m_new)
                        alpha_g = alpha if u == 0 else jnp.where(own[u], alpha, alpha_g)
                    p = jnp.exp(s - m_new)
                    l_new = jnp.sum(p, axis=1, keepdims=True)
                    r = jnp.dot(p.astype(BF16), vw, preferred_element_type=F32)
                    if single:
                        r = r * (1.0 / l_new)
                    else:
                        l_new = alpha * l_in[:, h:h + 1] + l_new
                    new_g = r if u == 0 else jnp.where(own[u], r, new_g)
                    m_out = jnp.where(lane == h, m_new, m_out)
                    l_out = jnp.where(lane == h, l_new, l_out)
                o_ref[0, rs, cols] = new_g if single else alpha_g * o_ref[0, rs, cols] + new_g
                if bf16_copy:
                    o16_ref[0, rs, cols] = new_g.astype(BF16)
            return m_out, l_out

        carry = (jnp.full((rows, LANES), MASKED, F32), jnp.zeros((rows, LANES), F32))
        if single:
            m_all, l_all = step(0, carry)
            l_all = jnp.where(lane < heads, l_all, 1.0)
        else:
            o_ref[...] = jnp.zeros(o_ref.shape, F32)
            m_all, l_all = lax.fori_loop(0, _window(mode, row0, 0, rows, win, lk)[1], step, carry)
            l_all = jnp.where(lane < heads, l_all, 1.0)
            inv = 1.0 / l_all
            for g in range(groups):
                cols = slice(g * gw, (g + 1) * gw)
                inv_g = inv[:, g * per:g * per + 1]
                for u in range(1, per):
                    inv_g = jnp.where(own[u], inv[:, g * per + u:g * per + u + 1], inv_g)
                o_ref[0, rs, cols] = o_ref[0, rs, cols] * inv_g
        lse_ref[0, rs, :] = jnp.where(lane < heads, m_all + jnp.log(l_all), 0.0)

    in_specs = [_attn_specs(qa, width, qc, classes, tq, False, together), _attn_specs(ka, width, kc, classes, win, True, together),
                _attn_specs(va, width, vc, classes, win, True, together)]
    args = [qa, ka, va]
    if has_kb:
        assert together == 1
        in_specs.append(pl.BlockSpec((1,) + kb.shape[1:], lambda n_, r, i: (n_, 0, 0, 0)))
        args.append(kb)
    out_shape = [jax.ShapeDtypeStruct((n, lq, classes * width), F32), jax.ShapeDtypeStruct((n, lq, classes * LANES), F32)]
    out_specs = [pl.BlockSpec((1, tq, together * width), lambda n_, r, i: (n_, i, r)),
                 pl.BlockSpec((1, tq, together * LANES), lambda n_, r, i: (n_, i, r))]
    if bf16_copy:
        assert single
        out_shape.append(jax.ShapeDtypeStruct((n, lq, classes * width), BF16))
        out_specs.append(out_specs[0])
    return pl.pallas_call(
        body, grid=(n, classes // together, lq // tq), in_specs=in_specs, out_specs=out_specs, out_shape=out_shape, name=name,
        compiler_params=_params("parallel", "parallel", "arbitrary"),
    )(*args)


def _attn_bwd(q, k, v, do, lse, delta, kb=None, *, classes=1, heads, dh, mode, max_dist=None, dq_dtype=F32, name):
    (qa, qc), (ka, kc), (va, vc), (da, dc) = q, k, v, do
    n, lq, lk = qa.shape[0], qa.shape[1], ka.shape[1]
    width = heads * dh
    tq, rows, win = _attn_tiles(mode, lq, lk, backward=True)
    gw, per, groups = _head_groups(heads, dh)
    scale = dh ** -0.5
    has_kb = kb is not None
    single = mode != "causal"
    nt = (((1,), (1,)), ((), ()))
    tn = (((0,), (0,)), ((), ()))
    together = min(classes, max(1, CLASS_ROWS // lq))
    ncols = lambda arr: arr.shape[2] // (classes * width)
    spans = [(ncols(qa), qc, width), (ncols(ka), kc, width), (ncols(va), vc, width), (ncols(da), dc, width), (1, 0, LANES),
             (1, 0, LANES), (1, 0, width), (1, 0, width), (1, 0, width)]

    def body(*refs):
        n_in = 7 if has_kb else 6
        dk_ref, dv_ref = refs[n_in + 1:n_in + 3]

        @pl.when(pl.program_id(2) == 0)
        def _():
            dk_ref[...] = jnp.zeros(dk_ref.shape, F32)
            dv_ref[...] = jnp.zeros(dv_ref.shape, F32)
            if has_kb:
                refs[n_in + 3][...] = jnp.zeros(refs[n_in + 3].shape, F32)

        for cl in range(together):
            for sub in range(tq // rows):
                part(_class_window(refs, spans, together, cl), pl.program_id(2) * tq + sub * rows, slice(sub * rows, (sub + 1) * rows))

    def part(refs, row0, rs):
        q_ref, k_ref, v_ref, do_ref, lse_ref, dl_ref = refs[:6]
        kb_ref = refs[6] if has_kb else None
        n_in = 7 if has_kb else 6
        dq_ref, dk_ref, dv_ref = refs[n_in:n_in + 3]
        dkb_ref, drow_ref = refs[n_in + 3:n_in + 5] if has_kb else (None, None)
        lse_all = lse_ref[0, rs, :]
        dl_all = dl_ref[0, rs, :]
        lane = lax.broadcasted_iota(jnp.int32, (rows, LANES), 1)
        own = [_own_lanes(rows, gw, dh, u) for u in range(per)]

        def step(j, drow_all):
            start, _ = _window(mode, row0, j, rows, win, lk)
            ok = _allowed(mode, row0, start, rows, win, max_dist)
            for g in range(groups):
                cols = slice(g * gw, (g + 1) * gw)
                qg = q_ref[0, rs, cols] * scale
                dog = do_ref[0, rs, cols]
                kw = k_ref[0, pl.ds(start, win), cols]
                vw = v_ref[0, pl.ds(start, win), cols]
                dq_g = dk_w = dv_w = None
                for u in range(per):
                    h = g * per + u
                    qz, doz = _only_head(qg, own, per, u), _only_head(dog, own, per, u)
                    kb_row = kb_ref[0, h, pl.ds(j, 1), :] if has_kb else None
                    p = jnp.exp(_step_scores(qz, kw, kb_row, ok) - lse_all[:, h:h + 1])
                    dp = lax.dot_general(doz, vw, nt, preferred_element_type=F32)
                    ds = p * (dp - dl_all[:, h:h + 1])
                    dsb = ds.astype(BF16)
                    r = jnp.dot(dsb, kw, preferred_element_type=F32)
                    dq_g = r if u == 0 else jnp.where(own[u], r, dq_g)
                    dk_u = lax.dot_general(dsb, qz, tn, preferred_element_type=F32)
                    dv_u = lax.dot_general(p.astype(BF16), doz, tn, preferred_element_type=F32)
                    dk_w = dk_u if u == 0 else dk_w + dk_u
                    dv_w = dv_u if u == 0 else dv_w + dv_u
                    if has_kb:
                        dkb_ref[0, h, pl.ds(j, 1), :] += -jnp.sum(ds, axis=0, keepdims=True)
                        drow_all = drow_all + jnp.where(lane == h, jnp.sum(ds, axis=1, keepdims=True), 0.0)
                dk_ref[0, pl.ds(start, win), cols] += dk_w
                dv_ref[0, pl.ds(start, win), cols] += dv_w
                if single:
                    dq_ref[0, rs, cols] = (dq_g * scale).astype(dq_ref.dtype)
                else:
                    dq_ref[0, rs, cols] += dq_g * scale
            return drow_all

        drow_all = jnp.zeros((rows, LANES), F32)
        if single:
            drow_all = step(0, drow_all)
        else:
            dq_ref[...] = jnp.zeros(dq_ref.shape, F32)
            drow_all = lax.fori_loop(0, _window(mode, row0, 0, rows, win, lk)[1], step, drow_all)
        if has_kb:
            drow_ref[0, rs, :] = drow_all

    row_spec = functools.partial(_attn_specs, classes=classes, rows_block=tq, whole=False, together=together)
    in_specs = [row_spec(qa, width, qc), _attn_specs(ka, width, kc, classes, win, True, together),
                _attn_specs(va, width, vc, classes, win, True, together), row_spec(da, width, dc), row_spec(lse, LANES, 0),
                row_spec(delta, LANES, 0)]
    args = [qa, ka, va, da, lse, delta]
    assert single or dq_dtype == F32
    out_shape = [jax.ShapeDtypeStruct((n, lq, classes * width), dq_dtype), jax.ShapeDtypeStruct((n, lk, classes * width), F32),
                 jax.ShapeDtypeStruct((n, lk, classes * width), F32)]
    kv_spec = pl.BlockSpec((1, lk, together * width), lambda n_, r, i: (n_, 0, r))
    out_specs = [pl.BlockSpec((1, tq, together * width), lambda n_, r, i: (n_, i, r)), kv_spec, kv_spec]
    if has_kb:
        assert together == 1
        kb_spec = pl.BlockSpec((1,) + kb.shape[1:], lambda n_, r, i: (n_, 0, 0, 0))
        in_specs.append(kb_spec)
        args.append(kb)
        out_shape += [jax.ShapeDtypeStruct(kb.shape, F32), jax.ShapeDtypeStruct((n, lq, LANES), F32)]
        out_specs += [kb_spec, pl.BlockSpec((1, tq, LANES), lambda n_, r, i: (n_, i, 0))]
    return pl.pallas_call(
        body, grid=(n, classes // together, lq // tq), in_specs=in_specs, out_specs=out_specs, out_shape=out_shape, name=name,
        compiler_params=_params("parallel", "parallel", "arbitrary"),
    )(*args)


def _rope_tables():
    half = HEAD_DIM // 2
    inv = ROPE_THETA ** (-jnp.arange(half, dtype=F32) / half)
    ang = jnp.arange(SEQ, dtype=F32)[:, None] * inv[None, :]
    cos = jnp.tile(jnp.cos(ang), (1, 2 * ATT_W // HEAD_DIM))
    sin = jnp.tile(jnp.concatenate([-jnp.sin(ang), jnp.sin(ang)], axis=1), (1, ATT_W // HEAD_DIM))
    return cos, sin


def _rotate(x, cos, sin):
    width = x.shape[1]
    lane = lax.broadcasted_iota(jnp.int32, x.shape, 1)
    first_half = (lane % HEAD_DIM) < (HEAD_DIM // 2)
    swapped = jnp.where(first_half, pltpu.roll(x, width - HEAD_DIM // 2, axis=1), pltpu.roll(x, HEAD_DIM // 2, axis=1))
    return x * cos[:, :width] + swapped * sin[:, :width]


def _gelu(x):
    k = math.sqrt(2.0 / math.pi)
    t = jnp.tanh(k * (x + 0.044715 * x * x * x))
    return 0.5 * x * (1.0 + t), t


def _gelu_grad(x, t):
    k = math.sqrt(2.0 / math.pi)
    return 0.5 * (1.0 + t) + 0.5 * x * (1.0 - t * t) * k * (1.0 + 3.0 * 0.044715 * x * x)


def _shift_down(x, halo, s):
    ext = jnp.concatenate([halo, x], axis=0)
    return pltpu.roll(ext, s, axis=0)[8:, :]


def _shift_up(x, halo, s):
    rows = x.shape[0]
    ext = jnp.concatenate([x, halo], axis=0)
    return pltpu.roll(ext, rows + 8 - s, axis=0)[:rows, :]


def _lru_gates(u, halo, cw, cb, wa, ba, wi, bi, lam):
    taps = [_shift_down(u, halo, CONV_WIDTH - 1 - k) for k in range(CONV_WIDTH - 1)] + [u]
    uc = cb + sum(cw[k:k + 1, :] * taps[k] for k in range(CONV_WIDTH))
    ucb = uc.astype(BF16)
    r = jax.nn.sigmoid(jnp.dot(ucb, wa, preferred_element_type=F32) + ba)
    gi = jax.nn.sigmoid(jnp.dot(ucb, wi, preferred_element_type=F32) + bi)
    sp = jnp.maximum(-lam, 0.0) + jnp.log(1.0 + jnp.exp(-jnp.abs(lam)))
    log_a = -LRU_C * r * sp
    a = jnp.exp(log_a)
    x2 = 2.0 * log_a
    one_minus_a2 = jnp.where(x2 > -0.01, -x2 * (1.0 + x2 * (0.5 + x2 * (1.0 / 6.0 + x2 / 24.0))), 1.0 - jnp.exp(x2))
    mult = jnp.sqrt(one_minus_a2)
    return taps, uc, ucb, r, gi, sp, a, mult


def _lru_specs(nchunk, reverse):
    def chunk(b, c):
        return b * nchunk + ((nchunk - 1 - c) if reverse else c)

    def halo_before(b, c):
        return jnp.maximum(chunk(b, c) * (LRU_CHUNK // 8) - 1, 0)

    return chunk, halo_before


def _lru_fwd(zug, cw, cb, wa, ba, wi, bi, lam, *, name):
    total = zug.shape[0]
    nchunk = SEQ // LRU_CHUNK
    w = LRU_WIDTH
    chunk, halo_before = _lru_specs(nchunk, False)

    def body(u_ref, uh_ref, g_ref, cw_ref, cb_ref, wa_ref, ba_ref, wi_ref, bi_ref, lam_ref, y_ref, h_ref, a_sc, x_sc, carry_sc):
        c = pl.program_id(1)
        u = u_ref[...]
        halo = jnp.where(c > 0, uh_ref[...], 0.0)
        _, uc, _, _, gi, _, a, mult = _lru_gates(u, halo, cw_ref[...], cb_ref[...], wa_ref[...], ba_ref[...],
                                                 wi_ref[...], bi_ref[...], lam_ref[...])
        a_sc[...] = a
        x_sc[...] = mult * (gi * uc)

        @pl.when(c == 0)
        def _():
            carry_sc[...] = jnp.zeros(carry_sc.shape, F32)

        def tile_step(t, h):
            r0 = pl.multiple_of(t * 8, 8)
            at = a_sc[pl.ds(r0, 8), :]
            xt = x_sc[pl.ds(r0, 8), :]
            rows = []
            for j in range(8):
                h = at[j:j + 1, :] * h + xt[j:j + 1, :]
                rows.append(h)
            h_ref[pl.ds(r0, 8), :] = jnp.concatenate(rows, axis=0)
            return h

        h_last = lax.fori_loop(0, LRU_CHUNK // 8, tile_step, carry_sc[0:1, :])
        carry_sc[0:1, :] = h_last
        gel, _ = _gelu(g_ref[...])
        y_ref[...] = (h_ref[...] * gel).astype(BF16)

    small = lambda arr: pl.BlockSpec(arr.shape, lambda b, c: (0, 0))
    return pl.pallas_call(
        body, grid=(total // SEQ, nchunk),
        in_specs=[pl.BlockSpec((LRU_CHUNK, w), lambda b, c: (chunk(b, c), 0)),
                  pl.BlockSpec((8, w), lambda b, c: (halo_before(b, c), 0)),
                  pl.BlockSpec((LRU_CHUNK, w), lambda b, c: (chunk(b, c), 1)),
                  small(cw), small(cb), small(wa), small(ba), small(wi), small(bi), small(lam)],
        out_specs=[pl.BlockSpec((LRU_CHUNK, w), lambda b, c: (chunk(b, c), 0))] * 2,
        out_shape=[jax.ShapeDtypeStruct((total, w), BF16), jax.ShapeDtypeStruct((total, w), F32)],
        scratch_shapes=[pltpu.VMEM((LRU_CHUNK, w), F32), pltpu.VMEM((LRU_CHUNK, w), F32), pltpu.VMEM((8, w), F32)],
        name=name, compiler_params=_params("arbitrary", "arbitrary"),
    )(zug, zug, zug, cw, cb, wa, ba, wi, bi, lam)


def _lru_bwd(zug, hl, dy, cw, cb, wa, ba, wi, bi, lam, *, name):
    total = zug.shape[0]
    nchunk = SEQ // LRU_CHUNK
    w = LRU_WIDTH
    chunk, halo_before = _lru_specs(nchunk, True)
    tn = (((0,), (0,)), ((), ()))
    nt = (((1,), (1,)), ((), ()))

    def body(u_ref, uh_ref, g_ref, hl_ref, hh_ref, dy_ref, cw_ref, cb_ref, wa_ref, ba_ref, wi_ref, bi_ref, lam_ref,
             dz_ref, dcw_ref, dcb_ref, dwa_ref, dba_ref, dwi_ref, dbi_ref, dlam_ref,
             a_sc, d_sc, g_sc, gcarry_sc, acarry_sc, duc_sc):
        b, c = pl.program_id(0), pl.program_id(1)
        first_chunk = c == nchunk - 1

        @pl.when((b == 0) & (c == 0))
        def _():
            for ref in (dcw_ref, dcb_ref, dwa_ref, dba_ref, dwi_ref, dbi_ref, dlam_ref):
                ref[...] = jnp.zeros(ref.shape, F32)

        @pl.when(c == 0)
        def _():
            gcarry_sc[...] = jnp.zeros(gcarry_sc.shape, F32)
            acarry_sc[...] = jnp.zeros(acarry_sc.shape, F32)
            duc_sc[...] = jnp.zeros(duc_sc.shape, F32)

        u = u_ref[...]
        halo = jnp.where(first_chunk, 0.0, uh_ref[...])
        cw, wa, wi, lam = cw_ref[...], wa_ref[...], wi_ref[...], lam_ref[...]
        taps, uc, ucb, r, gi, sp, a, mult = _lru_gates(u, halo, cw, cb_ref[...], wa, ba_ref[...], wi, bi_ref[...], lam)
        gate = g_ref[...]
        gel, th = _gelu(gate)
        dy = dy_ref[...]
        hl = hl_ref[...]
        a_sc[...] = a
        d_sc[...] = dy * gel
        dgate = dy * hl * _gelu_grad(gate, th)

        def tile_step(t, carry):
            g_next, a_next = carry
            r0 = pl.multiple_of((LRU_CHUNK // 8 - 1 - t) * 8, 8)
            dt = d_sc[pl.ds(r0, 8), :]
            at = a_sc[pl.ds(r0, 8), :]
            rows = [None] * 8
            for j in reversed(range(8)):
                g_next = dt[j:j + 1, :] + a_next * g_next
                a_next = at[j:j + 1, :]
                rows[j] = g_next
            g_sc[pl.ds(r0, 8), :] = jnp.concatenate(rows, axis=0)
            return g_next, a_next

        g_last, a_last = lax.fori_loop(0, LRU_CHUNK // 8, tile_step, (gcarry_sc[0:1, :], acarry_sc[0:1, :]))
        gcarry_sc[0:1, :] = g_last
        acarry_sc[0:1, :] = a_last

        gs = g_sc[...]
        hl_halo = jnp.where(first_chunk, 0.0, hh_ref[...])
        da = gs * _shift_down(hl, hl_halo, 1)
        dmult = gs * gi * uc
        dgi = gs * mult * uc
        duc = gs * mult * gi
        dlog_a = da * a - dmult * a * a / mult
        dr = dlog_a * (-LRU_C * sp)
        dsp = jnp.sum(dlog_a * (-LRU_C * r), axis=0, keepdims=True)
        dlam_ref[...] += -dsp * jax.nn.sigmoid(-lam)
        dpa = dr * r * (1.0 - r)
        dpi = dgi * gi * (1.0 - gi)
        dpab, dpib = dpa.astype(BF16), dpi.astype(BF16)
        dba_ref[...] += jnp.sum(dpa, axis=0, keepdims=True)
        dbi_ref[...] += jnp.sum(dpi, axis=0, keepdims=True)
        dwa_ref[...] += lax.dot_general(ucb, dpab, tn, preferred_element_type=F32)
        dwi_ref[...] += lax.dot_general(ucb, dpib, tn, preferred_element_type=F32)
        duc = duc + lax.dot_general(dpab, wa, nt, preferred_element_type=F32)
        duc = duc + lax.dot_general(dpib, wi, nt, preferred_element_type=F32)
        dcb_ref[...] += jnp.sum(duc, axis=0, keepdims=True)
        dcw_ref[...] += jnp.concatenate([jnp.sum(duc * taps[k], axis=0, keepdims=True) for k in range(CONV_WIDTH)], axis=0)
        after = duc_sc[...]
        du = cw[CONV_WIDTH - 1:CONV_WIDTH, :] * duc
        for k in range(CONV_WIDTH - 1):
            du = du + cw[k:k + 1, :] * _shift_up(duc, after, CONV_WIDTH - 1 - k)
        duc_sc[...] = duc[0:8, :]
        dz_ref[:, 0:w] = du.astype(BF16)
        dz_ref[:, w:2 * w] = dgate.astype(BF16)

    small = lambda arr: pl.BlockSpec(arr.shape, lambda b, c: (0, 0))
    acc = lambda shape: pl.BlockSpec(shape, lambda b, c: (0, 0))
    chunk_spec = lambda cb_: pl.BlockSpec((LRU_CHUNK, w), lambda b, c: (chunk(b, c), cb_))
    halo_spec = pl.BlockSpec((8, w), lambda b, c: (halo_before(b, c), 0))
    acc_shapes = [(CONV_WIDTH, w), (1, w), (w, w), (1, w), (w, w), (1, w), (1, w)]
    return pl.pallas_call(
        body, grid=(total // SEQ, nchunk),
        in_specs=[chunk_spec(0), halo_spec, chunk_spec(1), chunk_spec(0), halo_spec, chunk_spec(0),
                  small(cw), small(cb), small(wa), small(ba), small(wi), small(bi), small(lam)],
        out_specs=[pl.BlockSpec((LRU_CHUNK, 2 * w), lambda b, c: (chunk(b, c), 0))] + [acc(s) for s in acc_shapes],
        out_shape=[jax.ShapeDtypeStruct((total, 2 * w), BF16)] + [jax.ShapeDtypeStruct(s, F32) for s in acc_shapes],
        scratch_shapes=[pltpu.VMEM((LRU_CHUNK, w), F32)] * 3 + [pltpu.VMEM((8, w), F32)] * 3,
        name=name, compiler_params=_params("arbitrary", "arbitrary"),
    )(zug, zug, zug, hl, hl, dy, cw, cb, wa, ba, wi, bi, lam)


def _block_diag(wb):
    eye = jnp.eye(LRU_BLOCKS, dtype=wb.dtype)
    return jnp.einsum("gcd,gh->gchd", wb, eye).reshape(LRU_WIDTH, LRU_WIDTH).astype(BF16)


def _diag_blocks(wd):
    blk = LRU_WIDTH // LRU_BLOCKS
    return jnp.stack([wd[g * blk:(g + 1) * blk, g * blk:(g + 1) * blk] for g in range(LRU_BLOCKS)])


FOX_SCAN = 256


def _fox_bias(fl, bf, *, name):
    nb = fl.shape[0]

    def body(fl_ref, bf_ref, c_ref):
        x = fl_ref[0] + bf_ref[...]
        logf = jnp.minimum(x, 0.0) - jnp.log(1.0 + jnp.exp(-jnp.abs(x)))
        upper = (lax.broadcasted_iota(jnp.int32, (FOX_SCAN, FOX_SCAN), 0)
                 <= lax.broadcasted_iota(jnp.int32, (FOX_SCAN, FOX_SCAN), 1)).astype(BF16)
        carry = jnp.zeros((LRU_BLOCKS, 1), F32)
        for j in range(SEQ // FOX_SCAN):
            blk = logf[:, j * FOX_SCAN:(j + 1) * FOX_SCAN]
            c_ref[0, :, j * FOX_SCAN:(j + 1) * FOX_SCAN] = _dot_exact(blk, upper) + carry
            carry = carry + jnp.sum(blk, axis=1, keepdims=True)

    return pl.pallas_call(
        body, grid=(nb,),
        in_specs=[pl.BlockSpec((1, 8, SEQ), lambda b: (b, 0, 0)), pl.BlockSpec((8, 1), lambda b: (0, 0))],
        out_specs=pl.BlockSpec((1, 8, SEQ), lambda b: (b, 0, 0)),
        out_shape=jax.ShapeDtypeStruct((nb, 8, SEQ), F32), name=name, compiler_params=_params("arbitrary"),
    )(fl, bf)


def _fox_bias_bwd(fl, bf, dc, *, name):
    nb = fl.shape[0]

    def body(fl_ref, bf_ref, dc_ref, dfl_ref, dbf_ref):
        @pl.when(pl.program_id(0) == 0)
        def _():
            dbf_ref[...] = jnp.zeros(dbf_ref.shape, F32)

        x = fl_ref[0] + bf_ref[...]
        dc_all = dc_ref[0]
        lower = (lax.broadcasted_iota(jnp.int32, (FOX_SCAN, FOX_SCAN), 0)
                 >= lax.broadcasted_iota(jnp.int32, (FOX_SCAN, FOX_SCAN), 1)).astype(BF16)
        carry = jnp.zeros((LRU_BLOCKS, 1), F32)
        total = jnp.zeros((LRU_BLOCKS, 1), F32)
        for j in reversed(range(SEQ // FOX_SCAN)):
            cols = slice(j * FOX_SCAN, (j + 1) * FOX_SCAN)
            blk = dc_all[:, cols]
            dlogf = _dot_exact(blk, lower) + carry
            carry = carry + jnp.sum(blk, axis=1, keepdims=True)
            dx = dlogf * jax.nn.sigmoid(-x[:, cols])
            dfl_ref[0, :, cols] = dx
            total = total + jnp.sum(dx, axis=1, keepdims=True)
        dbf_ref[...] += total

    return pl.pallas_call(
        body, grid=(nb,),
        in_specs=[pl.BlockSpec((1, 8, SEQ), lambda b: (b, 0, 0)), pl.BlockSpec((8, 1), lambda b: (0, 0)),
                  pl.BlockSpec((1, 8, SEQ), lambda b: (b, 0, 0))],
        out_specs=[pl.BlockSpec((1, 8, SEQ), lambda b: (b, 0, 0)), pl.BlockSpec((8, 1), lambda b: (0, 0))],
        out_shape=[jax.ShapeDtypeStruct((nb, 8, SEQ), F32), jax.ShapeDtypeStruct((8, 1), F32)],
        name=name, compiler_params=_params("arbitrary"),
    )(fl, bf, dc)


def _seq3(a, nb):
    return a.reshape(nb, a.shape[0] // nb, a.shape[1])


def _flat2(a):
    return a.reshape(a.shape[0] * a.shape[1], a.shape[2])


def _residual_out(y, w, h, next_gain, *, name):
    if next_gain is None:
        out, = _matmul(y, w, extras=(h,), epilogue=lambda acc, res: (acc + res,), name=name)
        return out, None

    def epilogue(acc, res, g):
        out = acc + res
        return out, out * _rstd(out) * g
    return _matmul(y, w, outs=(F32, BF16), extras=(h,), consts=(next_gain,), epilogue=epilogue, whole_rows=True, name=name)


def _mlp_fwd(h, hn, p, tag, next_gain):
    act, = _matmul(hn, p["w_up_t"], tb=True, outs=(BF16,), epilogue=lambda acc: (jnp.square(jnp.maximum(acc, 0.0)),),
                   name=f"{tag}_up")
    out, hn_next = _residual_out(act, p["w_down"], h, next_gain, name=f"{tag}_down")
    return out, hn_next, (h, hn, act)


def _mlp_bwd(dh, dhb, p, saved, tag):
    h, hn, act = saved
    dup, = _matmul(dhb, p["w_down"], tb=True, outs=(BF16,), extras=(act,),
                   epilogue=lambda acc, act: (acc * (2.0 * jnp.sqrt(act).astype(F32)),), name=f"{tag}_bwd_dup")
    dw_down, = _matmul(act, dhb, ta=True, outs=(BF16,),name=f"{tag}_bwd_dwdown")
    dw_up_t, = _matmul(dup, hn, ta=True, outs=(BF16,),name=f"{tag}_bwd_dwup")
    dh2, dh2b, dg = _norm_input_grad(dup, p["w_up_t"], h, dh, p["norm"], name=f"{tag}_bwd_dh")
    return dh2, dh2b, {"norm": dg, "w_up_t": dw_up_t, "w_down": dw_down}


def _xa_fwd(h, hn, mem, p, tag, nb, next_gain):
    mem_n = _rms_fwd(mem, p["mem_norm"], name=f"{tag}_memnorm")
    q, = _matmul(hn, p["w_q"], outs=(BF16,), name=f"{tag}_q")
    kv, = _matmul(mem_n, p["w_kv_t"], tb=True, outs=(BF16,), name=f"{tag}_kv")
    q3, kv3 = _seq3(q, nb), _seq3(kv, nb)
    o, lse, ob = _attn_fwd((q3, 0), (kv3, 0), (kv3, 1), heads=XA_HEADS, dh=XA_HEAD_DIM, mode="full", bf16_copy=True,
                           name=f"{tag}_attn")
    o, ob = _flat2(o), _flat2(ob)
    out, hn_next = _residual_out(ob, p["w_o"], h, next_gain, name=f"{tag}_o")
    return out, hn_next, (h, hn, mem_n, q3, kv3, o, ob, lse)


def _xa_bwd(dh, dhb, mem, p, saved, tag, nb):
    h, hn, mem_n, q3, kv3, o, ob, lse = saved
    dob, delta = _matmul(dhb, p["w_o"], tb=True, outs=(BF16, (F32, LANES)), extras=(o,), consts=(_head_expand(XA_HEADS, D_MODEL).T,),
                         epilogue=lambda acc, o, e: (acc, _dot_exact(acc * o, e)), name=f"{tag}_bwd_do")
    dw_o, = _matmul(ob, dhb, ta=True, outs=(BF16,),name=f"{tag}_bwd_dwo")
    dq, dk, dv = _attn_bwd((q3, 0), (kv3, 0), (kv3, 1), (_seq3(dob, nb), 0), lse, _seq3(delta, nb), heads=XA_HEADS,
                           dh=XA_HEAD_DIM, mode="full", dq_dtype=BF16, name=f"{tag}_bwd_attn")
    dqb = _flat2(dq)
    dkvb, = _rowwise(lambda a, b: jnp.concatenate([a, b], axis=1), [_flat2(dk), _flat2(dv)], [], [(2 * D_MODEL, BF16)],
                     name=f"{tag}_bwd_dkvcast")
    dw_q, = _matmul(hn, dqb, ta=True, outs=(BF16,),name=f"{tag}_bwd_dwq")
    dw_kv_t, = _matmul(dkvb, mem_n, ta=True, outs=(BF16,),name=f"{tag}_bwd_dwkv")
    dmem_n, = _matmul(dkvb, p["w_kv_t"], name=f"{tag}_bwd_dmemn")
    dg_mem, = _rowwise(lambda x, d, g: _rms_bwd_vals(x, d, g)[1], [mem, dmem_n], [p["mem_norm"]], [], [(1, D_MODEL)],
                       name=f"{tag}_bwd_memnorm")
    dh2, dh2b, dg = _norm_input_grad(dqb, p["w_q"], h, dh, p["norm"], tb=True, name=f"{tag}_bwd_dh")
    return dh2, dh2b, {"norm": dg, "mem_norm": dg_mem, "w_q": dw_q, "w_kv_t": dw_kv_t, "w_o": dw_o}


AB_MAIN = 2 * LRU_WIDTH + 3 * ATT_W


def _ab_fwd(h, hn, p, nb, next_gain):
    w_in_t = p["w_in_t"]
    zug, = _matmul(hn, w_in_t[:2 * LRU_WIDTH], tb=True, name="ab_in_ug")
    qkv, = _matmul(hn, w_in_t[2 * LRU_WIDTH:AB_MAIN], tb=True, outs=(BF16,), tn=768, name="ab_in_qkv")
    zf, = _matmul(hn, w_in_t[AB_MAIN:], tb=True, name="ab_in_f")
    fl = zf[:, :8].reshape(nb, SEQ, 8).transpose(0, 2, 1)
    cbias = _fox_bias(fl, p["b_f"], name="ab_fox_bias")
    kb = cbias.reshape(nb, 8, SEQ // ATT_CHUNK, ATT_CHUNK)
    y_a, hl = _lru_fwd(zug, p["conv_w"], p["conv_b"], p["w_a"], p["b_a"], p["w_i"], p["b_i"], p["lam"], name="ab_lru")
    qkv3 = _seq3(qkv, nb)
    o, lse = _attn_fwd((qkv3, 0), (qkv3, 1), (qkv3, 2), kb, heads=8, dh=HEAD_DIM, mode="causal", name="ab_fox")
    o = _flat2(o)
    y, = _rowwise(lambda a, b: jnp.concatenate([a, b.astype(BF16)], axis=1), [y_a, o], [], [(D_MODEL, BF16)], name="ab_ycat")
    out, hn_next = _residual_out(y, p["w_out"], h, next_gain, name="ab_out")
    return out, hn_next, (h, hn, zug, qkv3, fl, kb, hl, o, lse, y)


def _ab_bwd(dh, dhb, p, saved, nb):
    h, hn, zug, qkv3, fl, kb, hl, o, lse, y = saved
    dy, dyb, delta = _matmul(dhb, p["w_out"], tb=True, outs=(F32, BF16, (F32, LANES)), extras=(y,), consts=(_head_expand(8, ATT_W).T,),
                             epilogue=lambda acc, y, e: (acc, acc, _dot_exact(acc[:, ATT_W:] * y[:, ATT_W:].astype(F32), e)),
                             name="ab_bwd_dy")
    dw_out, = _matmul(y, dhb, ta=True, outs=(BF16,),name="ab_bwd_dwout")
    dzug, dcw, dcb, dwa, dba, dwi, dbi, dlam = _lru_bwd(zug, hl, dy, p["conv_w"], p["conv_b"], p["w_a"], p["b_a"],
                                                        p["w_i"], p["b_i"], p["lam"], name="ab_bwd_lru")
    dq, dk, dv, dkb, drow = _attn_bwd((qkv3, 0), (qkv3, 1), (qkv3, 2), (_seq3(dyb, nb), 1), lse, _seq3(delta, nb), kb,
                                      heads=8, dh=HEAD_DIM, mode="causal", name="ab_bwd_fox")
    dcum = dkb.reshape(nb, 8, SEQ) + drow[:, :, :8].transpose(0, 2, 1)
    dfl, dbf = _fox_bias_bwd(fl, p["b_f"], dcum, name="ab_bwd_fox_bias")
    dzf = jnp.pad(dfl.transpose(0, 2, 1).reshape(nb * SEQ, 8), ((0, 0), (0, LANES - 8)))
    dz, = _rowwise(lambda a, q, k, v, f: jnp.concatenate([a, q.astype(BF16), k.astype(BF16), v.astype(BF16), f.astype(BF16)], axis=1),
                   [dzug, _flat2(dq), _flat2(dk), _flat2(dv), dzf], [], [(AB_MAIN + LANES, BF16)], name="ab_bwd_dzcat")
    dw_in_t, = _matmul(dz, hn, ta=True, outs=(BF16,),tm=384, name="ab_bwd_dwin")
    dh2, dh2b, dg = _norm_input_grad(dz, p["w_in_t"], h, dh, p["norm"], name="ab_bwd_dh")
    grads = {"norm": dg, "w_in_t": dw_in_t[:AB_MAIN + 8], "conv_w": dcw, "conv_b": dcb, "w_a": _diag_blocks(dwa), "b_a": dba,
             "w_i": _diag_blocks(dwi), "b_i": dbi, "lam": dlam, "b_f": dbf.reshape(1, 8), "w_out": dw_out}
    return dh2, dh2b, grads


CD_IN = 3 * ATT_W + ATT_W + 2 * SWA_KW


def _gqa_share():
    src = jnp.arange(SWA_KW)[:, None]
    dst = jnp.arange(ATT_W)[None, :]
    per_kv = ATT_W // (SWA_KW // HEAD_DIM)
    return ((dst // per_kv == src // HEAD_DIM) & (dst % HEAD_DIM == src % HEAD_DIM)).astype(BF16)


def _cd_fwd(h, hn, p, nb, next_gain):
    z, = _matmul(hn, p["w_in_t"], tb=True, tn=1152, name="cd_in")
    cos, sin = _rope_tables()
    per_seq = SEQ // ROW_TILE
    table_map = lambda i: (i % per_seq, 0)

    def rope_fn(z, cos, sin, share):
        qc, kc, vc = z[:, 0:ATT_W], z[:, ATT_W:2 * ATT_W], z[:, 2 * ATT_W:3 * ATT_W]
        qd, kd, vd = z[:, 3 * ATT_W:4 * ATT_W], z[:, 4 * ATT_W:4 * ATT_W + SWA_KW], z[:, 4 * ATT_W + SWA_KW:]
        kd8 = jnp.dot(_rotate(kd, cos, sin).astype(BF16), share, preferred_element_type=F32)
        vd8 = jnp.dot(vd.astype(BF16), share, preferred_element_type=F32)
        qk = jnp.concatenate([_rotate(qc, cos, sin), _rotate(kc, cos, sin)], axis=1)
        return qk, vc, _rotate(qd, cos, sin), kd8, vd8, qk, qk, vc, vc
    dils = [dil for _, dil in DIL_PATTERN if dil > 1]
    outs = _rowwise(rope_fn, [z, cos, sin], [_gqa_share()],
                    [(2 * ATT_W, BF16)] + [(ATT_W, BF16)] * 4 + [(2 * ATT_W, BF16, d) for d in dils] + [(ATT_W, BF16, d) for d in dils],
                    name="cd_rope", row_maps=[None, table_map, table_map])
    qk, vc, qd, kd, vd = outs[:5]
    views = {1: (_seq3(qk, nb), _seq3(vc, nb))}
    for d, qk_d, vc_d in zip(dils, outs[5:5 + len(dils)], outs[5 + len(dils):]):
        views[d] = (_seq3(qk_d, nb), _seq3(vc_d, nb))
    in_classes = lambda a, width, d: _flat2(a) if d == 1 else ("classes", _flat2(a), width, d)
    branch = []
    for window, dil in DIL_PATTERN:
        qk_v, vc_v = views[dil]
        o_i, lse_i = _attn_fwd((qk_v, 0), (qk_v, 1), (vc_v, 0), classes=dil, heads=8, dh=HEAD_DIM, mode="band",
                               max_dist=window // dil, name=f"cd_dil{dil}")
        branch.append((in_classes(o_i, ATT_W, dil), in_classes(lse_i, LANES, dil)))
    expand = _head_expand(8, ATT_W)

    qd3, kd3, vd3 = _seq3(qd, nb), _seq3(kd, nb), _seq3(vd, nb)
    o_d, lse_band = _attn_fwd((qd3, 0), (kd3, 0), (vd3, 0), heads=8, dh=HEAD_DIM, mode="band", max_dist=SWA_WINDOW - 1,
                              name="cd_swa")

    def mix(o1, o2, o3, l1, l2, l3, o_band, l_band, e, sink):
        m = jnp.maximum(jnp.maximum(l1, l2), l3)
        lt = m + jnp.log(jnp.exp(l1 - m) + jnp.exp(l2 - m) + jnp.exp(l3 - m))
        y_c = sum(_dot_exact(jnp.exp(l - lt), e) * o_ for o_, l in ((o1, l1), (o2, l2), (o3, l3)))
        lt_d = jnp.maximum(l_band, sink) + jnp.log(1.0 + jnp.exp(-jnp.abs(l_band - sink)))
        y_d = o_band * _dot_exact(jnp.exp(l_band - lt_d), e)
        return jnp.concatenate([y_c, y_d], axis=1), y_c, lt, y_d, lt_d
    y, y_c, lse_c, y_d, lse_d = _rowwise(
        mix, [b[0] for b in branch] + [b[1] for b in branch] + [_flat2(o_d), _flat2(lse_band)], [expand, p["sink"]],
        [(D_MODEL, BF16), (ATT_W, F32), (LANES, F32), (ATT_W, F32), (LANES, F32)], name="cd_mix")
    out, hn_next = _residual_out(y, p["w_out"], h, next_gain, name="cd_out")
    return out, hn_next, (h, hn, views, qd3, kd3, vd3, y_c, lse_c, y_d, lse_d, y)


def _cd_bwd(dh, dhb, p, saved, nb):
    h, hn, views, qd3, kd3, vd3, y_c, lse_c, y_d, lse_d, y = saved
    dy, dyb = _matmul(dhb, p["w_out"], tb=True, outs=(F32, BF16), epilogue=lambda acc: (acc, acc), name="cd_bwd_dy")
    dw_out, = _matmul(y, dhb, ta=True, outs=(BF16,),name="cd_bwd_dwout")
    dyb3 = _seq3(dyb, nb)
    dils = [dil for _, dil in DIL_PATTERN if dil > 1]
    gather = _head_expand(8, ATT_W).T

    def prepare(do, o, lse, do_d, o_d, lse_d, e, sink):
        delta = _dot_exact(do * o, e)
        delta_d = _dot_exact(do_d * o_d, e)
        dsink = -jnp.sum(jnp.exp(sink - lse_d) * delta_d, axis=0, keepdims=True)
        return (delta,) + (do,) * len(dils) + (lse,) * len(dils) + (delta,) * len(dils) + (delta_d, dsink)
    outs = _rowwise(prepare, [(dy, ATT_W, 0), y_c, lse_c, (dy, ATT_W, 1), y_d, lse_d], [gather, p["sink"]],
                    [(LANES, F32)] + [(ATT_W, BF16, d) for d in dils] + [(LANES, F32, d) for d in dils] * 2 + [(LANES, F32)],
                    [(1, LANES)], name="cd_bwd_delta")
    delta_d, dsink = outs[-2:]
    seen = {1: ((dyb3, 0), _seq3(lse_c, nb), _seq3(outs[0], nb))}
    for j, d in enumerate(dils):
        seen[d] = ((_seq3(outs[1 + j], nb), 0), _seq3(outs[1 + len(dils) + j], nb), _seq3(outs[1 + 2 * len(dils) + j], nb))
    in_classes = lambda a, d: _flat2(a) if d == 1 else ("classes", _flat2(a), ATT_W, d)
    parts = []
    for window, dil in DIL_PATTERN:
        (qk_v, vc_v), (do_v, lse_v, delta_v) = views[dil], seen[dil]
        grads = _attn_bwd((qk_v, 0), (qk_v, 1), (vc_v, 0), do_v, lse_v, delta_v, classes=dil, heads=8, dh=HEAD_DIM, mode="band",
                          max_dist=window // dil, dq_dtype=BF16, name=f"cd_bwd_dil{dil}")
        parts.append([in_classes(a, dil) for a in grads])
    dqd, dkd, dvd = _attn_bwd((qd3, 0), (kd3, 0), (vd3, 0), (dyb3, 1), _seq3(lse_d, nb), _seq3(delta_d, nb), heads=8,
                              dh=HEAD_DIM, mode="band", max_dist=SWA_WINDOW - 1, dq_dtype=BF16, name="cd_bwd_swa")
    cos, sin = _rope_tables()
    per_seq = SEQ // ROW_TILE
    table_map = lambda i: (i % per_seq, 0)

    def unrope(q1, q2, q3, k1, k2, k3, v1, v2, v3, qd, kd8, vd8, cos, sin, gather):
        back = lambda x: _rotate(x.astype(F32), cos, -sin)
        kd, vd = _dot_exact(kd8, gather), _dot_exact(vd8, gather)
        return jnp.concatenate([back(q1 + q2 + q3), back(k1 + k2 + k3), v1 + v2 + v3, back(qd), back(kd), vd], axis=1)
    ins = [parts[b][t] for t in range(3) for b in range(3)] + [_flat2(dqd), _flat2(dkd), _flat2(dvd), cos, sin]
    dz, = _rowwise(unrope, ins, [_gqa_share().T], [(CD_IN, BF16)], name="cd_bwd_unrope",
                   row_maps=[None] * 12 + [table_map, table_map])
    dw_in_t, = _matmul(dz, hn, ta=True, outs=(BF16,),tm=384, name="cd_bwd_dwin")
    dh2, dh2b, dg = _norm_input_grad(dz, p["w_in_t"], h, dh, p["norm"], name="cd_bwd_dh")
    return dh2, dh2b, {"norm": dg, "w_in_t": dw_in_t, "sink": dsink[:, :8], "w_out": dw_out}


def _local_step(x, mem, target, w, complete=None, grads_ready=None):
    nb = x.shape[0]
    h = x.reshape(nb * SEQ, D_MODEL)
    mem2 = mem.reshape(nb * MEM_LEN, D_MODEL)
    tgt = target.reshape(nb * SEQ, D_MODEL)

    hn = _rms_fwd(h, w["ab"]["norm"], name="ab_norm")
    h, hn, s_ab = _ab_fwd(h, hn, w["ab"], nb, w["xa0"]["norm"])
    if complete is not None:
        complete(h)
    h, hn, s_xa0 = _xa_fwd(h, hn, mem2, w["xa0"], "xa0", nb, w["mlp0"]["norm"])
    h, hn, s_mlp0 = _mlp_fwd(h, hn, w["mlp0"], "mlp0", w["cd"]["norm"])
    h, hn, s_cd = _cd_fwd(h, hn, w["cd"], nb, w["xa1"]["norm"])
    h, hn, s_xa1 = _xa_fwd(h, hn, mem2, w["xa1"], "xa1", nb, w["mlp1"]["norm"])
    h, _, s_mlp1 = _mlp_fwd(h, hn, w["mlp1"], "mlp1", None)

    dh, dhb, loss, dg_final = _loss_and_grad(h, tgt, w["final_norm"], name="loss")
    grads = {"final_norm": dg_final}
    dh, dhb, grads["mlp1"] = _mlp_bwd(dh, dhb, w["mlp1"], s_mlp1, "mlp1")
    dh, dhb, grads["xa1"] = _xa_bwd(dh, dhb, mem2, w["xa1"], s_xa1, "xa1", nb)
    dh, dhb, grads["cd"] = _cd_bwd(dh, dhb, w["cd"], s_cd, nb)
    if grads_ready is not None:
        grads_ready(0, grads)
    dh, dhb, grads["mlp0"] = _mlp_bwd(dh, dhb, w["mlp0"], s_mlp0, "mlp0")
    dh, dhb, grads["xa0"] = _xa_bwd(dh, dhb, mem2, w["xa0"], s_xa0, "xa0", nb)
    if grads_ready is not None:
        grads_ready(1, grads)
    dh, dhb, grads["ab"] = _ab_bwd(dh, dhb, w["ab"], s_ab, nb)
    return loss, dh.reshape(nb, SEQ, D_MODEL), grads


ANY = pl.BlockSpec(memory_space=pl.ANY)


def _place():
    x, y, c = lax.axis_index("x"), lax.axis_index("y"), lax.axis_index("c")
    return x, y, c, [(1 - x, y), (x, 1 - y), (1 - x, 1 - y)]


def _gather_chips(shard):
    half = shard.shape[0] // 2

    def body(src, out, send_sems, recv_sems):
        x, y, c, chips = _place()
        sibling = (x, y, 1 - c)

        def rows(slot, core):
            return out.at[slot, pl.ds(core * half, half)]

        def copy(k, src_ref, dst_ref, to):
            return pltpu.make_async_remote_copy(src_ref=src_ref, dst_ref=dst_ref, send_sem=send_sems.at[k],
                                                recv_sem=recv_sems.at[k], device_id=to, device_id_type=MESH)

        first = [copy(k, src.at[pl.ds(c * half, half)], rows(2 * x + y, c), (px, py, c)) for k, (px, py) in enumerate(chips)]
        for cp in first:
            cp.start()
        passed = [copy(3 + k, rows(2 * px + py, c), rows(2 * px + py, c), sibling) for k, (px, py) in enumerate(chips)]
        for k, (px, py) in enumerate(chips):
            copy(k, src.at[pl.ds(c * half, half)], rows(2 * px + py, c), (px, py, c)).wait_recv()
            passed[k].start()
        for k, (px, py) in enumerate(chips):
            copy(3 + k, rows(2 * px + py, 1 - c), rows(2 * px + py, 1 - c), sibling).wait_recv()
        for cp in first + passed:
            cp.wait_send()

    return pl.pallas_call(
        body, out_shape=jax.ShapeDtypeStruct((N_CHIPS,) + shard.shape, shard.dtype), in_specs=[ANY], out_specs=ANY,
        scratch_shapes=[pltpu.SemaphoreType.DMA((6,)), pltpu.SemaphoreType.DMA((6,))], name="gather_chips",
    )(shard)


HBM = pl.BlockSpec(memory_space=pltpu.HBM)
SEM = pl.BlockSpec(memory_space=pltpu.SEMAPHORE)
SIDE_EFFECT = pltpu.SideEffectType.DATAFLOW_SIDE_EFFECTING


def _chip_copies(src, land, send_sems, recv_sems):
    half = src.shape[0] // 2
    x, y, c, chips = _place()

    def copy(k, slot, to):
        return pltpu.make_async_remote_copy(src_ref=src.at[pl.ds(c * half, half)], dst_ref=land.at[slot, pl.ds(c * half, half)],
                                            send_sem=send_sems[k], recv_sem=recv_sems[k], device_id=to, device_id_type=MESH)
    out = [copy(k, 2 * x + y, (px, py, c)) for k, (px, py) in enumerate(chips)]
    back = [copy(k, 2 * px + py, (px, py, c)) for k, (px, py) in enumerate(chips)]
    return out, back


def _gather_start(shard, after):
    def body(src, land, *rest):
        rest = rest[len(after):]
        sems, token = rest[:6], rest[8]
        out, _ = _chip_copies(src, land, sems[:3], sems[3:])
        for cp in out:
            cp.start()
        token[...] = jnp.zeros(token.shape, F32)

    sem = pltpu.SemaphoreType.DMA(())
    land_shape = (N_CHIPS,) + shard.shape
    return pl.pallas_call(
        body, name="gather_start",
        out_shape=(sem,) * 6 + (pltpu.HBM(shard.shape, shard.dtype), pltpu.HBM(land_shape, shard.dtype),
                                jax.ShapeDtypeStruct((8, LANES), F32)),
        in_specs=(HBM, HBM) + (pl.BlockSpec(memory_space=pl.ANY),) * len(after),
        out_specs=(SEM,) * 6 + (HBM, HBM, pl.BlockSpec(memory_space=pltpu.VMEM)),
        input_output_aliases={0: 6, 1: 7}, compiler_params=pltpu.CompilerParams(has_side_effects=SIDE_EFFECT),
    )(pltpu.with_memory_space_constraint(shard, pltpu.HBM),
      pltpu.with_memory_space_constraint(lax.empty(land_shape, shard.dtype), pltpu.HBM), *after)


def _gather_wait(started, after):
    sems, src_thru, land_thru = started[:6], started[6], started[7]

    def body(src, land, *rest):
        out, back = _chip_copies(src, land, rest[:3], rest[3:6])
        for cp in out:
            cp.wait_send()
        for cp in back:
            cp.wait_recv()

    return pl.pallas_call(
        body, name="gather_wait",
        out_shape=(pltpu.HBM(src_thru.shape, src_thru.dtype), pltpu.HBM(land_thru.shape, land_thru.dtype)),
        in_specs=(HBM, HBM) + (SEM,) * 6 + (pl.BlockSpec(memory_space=pl.ANY),), out_specs=(HBM, HBM),
        input_output_aliases={0: 0, 1: 1}, compiler_params=pltpu.CompilerParams(has_side_effects=SIDE_EFFECT),
    )(src_thru, land_thru, *sems, after)


def _gather_pass(land):
    half = land.shape[1] // 2

    def body(land_in, land_out, send_sems, recv_sems):
        x, y, c, chips = _place()

        def copy(k, slot, core):
            rows = land_out.at[slot, pl.ds(core * half, half)]
            return pltpu.make_async_remote_copy(src_ref=rows, dst_ref=rows, send_sem=send_sems.at[k], recv_sem=recv_sems.at[k],
                                                device_id=(x, y, 1 - c), device_id_type=MESH)
        sends = [copy(k, 2 * px + py, c) for k, (px, py) in enumerate(chips)]
        for cp in sends:
            cp.start()
        for k, (px, py) in enumerate(chips):
            copy(k, 2 * px + py, 1 - c).wait_recv()
        for cp in sends:
            cp.wait_send()

    return pl.pallas_call(
        body, out_shape=jax.ShapeDtypeStruct(land.shape, land.dtype), in_specs=[ANY], out_specs=ANY,
        scratch_shapes=[pltpu.SemaphoreType.DMA((3,)), pltpu.SemaphoreType.DMA((3,))], input_output_aliases={0: 0},
        name="gather_pass",
    )(land)


def _swap_cores(block, *, name, half_rows=None):
    shape = block.shape if half_rows is None else (block.shape[0], half_rows, block.shape[2])

    def body(src, out, send_sem, recv_sem):
        x, y, c, _ = _place()
        part = src if half_rows is None else src.at[:, pl.ds((1 - c) * half_rows, half_rows)]
        cp = pltpu.make_async_remote_copy(src_ref=part, dst_ref=out, send_sem=send_sem, recv_sem=recv_sem,
                                          device_id=(x, y, 1 - c), device_id_type=MESH)
        cp.start()
        cp.wait()

    return pl.pallas_call(
        body, out_shape=jax.ShapeDtypeStruct(shape, block.dtype), in_specs=[ANY], out_specs=ANY,
        scratch_shapes=[pltpu.SemaphoreType.DMA(()), pltpu.SemaphoreType.DMA(())], name=name,
    )(block)


def _scatter_copies(parts, land, send_sems, recv_sems):
    x, y, c, chips = _place()
    return [pltpu.make_async_remote_copy(src_ref=parts.at[2 * px + py], dst_ref=land.at[k], send_sem=send_sems[k],
                                         recv_sem=recv_sems[k], device_id=(px, py, c), device_id_type=MESH)
            for k, (px, py) in enumerate(chips)]


def _scatter_start(parts, tag):
    def body(src, land, *rest):
        for cp in _scatter_copies(src, land, rest[:3], rest[3:6]):
            cp.start()
        rest[8][...] = jnp.zeros(rest[8].shape, F32)

    sem = pltpu.SemaphoreType.DMA(())
    land_shape = (3,) + parts.shape[1:]
    return pl.pallas_call(
        body, name=f"scatter_start_{tag}",
        out_shape=(sem,) * 6 + (pltpu.HBM(parts.shape, parts.dtype), pltpu.HBM(land_shape, parts.dtype),
                                jax.ShapeDtypeStruct((8, LANES), F32)),
        in_specs=(HBM, HBM), out_specs=(SEM,) * 6 + (HBM, HBM, pl.BlockSpec(memory_space=pltpu.VMEM)),
        input_output_aliases={0: 6, 1: 7}, compiler_params=pltpu.CompilerParams(has_side_effects=SIDE_EFFECT),
    )(pltpu.with_memory_space_constraint(parts, pltpu.HBM),
      pltpu.with_memory_space_constraint(lax.empty(land_shape, parts.dtype), pltpu.HBM))


def _scatter_wait(started, after, tag):
    def body(src, land, *rest):
        for cp in _scatter_copies(src, land, rest[:3], rest[3:6]):
            cp.wait_send()
            cp.wait_recv()

    src_thru, land_thru = started[6], started[7]
    return pl.pallas_call(
        body, name=f"scatter_wait_{tag}",
        out_shape=(pltpu.HBM(src_thru.shape, src_thru.dtype), pltpu.HBM(land_thru.shape, land_thru.dtype)),
        in_specs=(HBM, HBM) + (SEM,) * 6 + (pl.BlockSpec(memory_space=pl.ANY),), out_specs=(HBM, HBM),
        input_output_aliases={0: 0, 1: 1}, compiler_params=pltpu.CompilerParams(has_side_effects=SIDE_EFFECT),
    )(src_thru, land_thru, *started[:6], after)[1]


def _sum_all_devices(vec):
    rows = vec.shape[0]

    def body(x_ref, total_ref, all_ref, send_sems, recv_sems, local_sem):
        x, y, c, chips = _place()
        me, sibling = (x, y, c), (x, y, 1 - c)

        def slot(px, py, pc):
            return all_ref.at[4 * px + 2 * py + pc]

        def copy(k, block, to, src=None):
            return pltpu.make_async_remote_copy(src_ref=slot(*block) if src is None else src, dst_ref=slot(*block),
                                                send_sem=send_sems.at[k], recv_sem=recv_sems.at[k], device_id=to,
                                                device_id_type=MESH)

        mine = pltpu.make_async_copy(x_ref, slot(*me), local_sem)
        mine.start()
        first = [copy(0, me, sibling, src=x_ref)] + [copy(1 + j, me, (*chip, c), src=x_ref) for j, chip in enumerate(chips)]
        for cp in first:
            cp.start()
        passed = [copy(4 + j, (*chip, c), sibling) for j, chip in enumerate(chips)]
        for j, chip in enumerate(chips):
            copy(1 + j, (*chip, c), me).wait_recv()
            passed[j].start()
        copy(0, sibling, me).wait_recv()
        for j, chip in enumerate(chips):
            copy(4 + j, (*chip, 1 - c), me).wait_recv()
        for cp in first + passed:
            cp.wait_send()
        mine.wait()
        acc = all_ref[0]
        for d in range(1, 8):
            acc = acc + all_ref[d]
        total_ref[...] = acc

    vmem = pl.BlockSpec(memory_space=pltpu.VMEM)
    return pl.pallas_call(
        body, out_shape=[jax.ShapeDtypeStruct((rows, LANES), F32), jax.ShapeDtypeStruct((8, rows, LANES), F32)],
        in_specs=[vmem], out_specs=[vmem, vmem],
        scratch_shapes=[pltpu.SemaphoreType.DMA((7,)), pltpu.SemaphoreType.DMA((7,)), pltpu.SemaphoreType.DMA(())],
        name="sum_all_devices", compiler_params=pltpu.CompilerParams(vmem_limit_bytes=VMEM_LIMIT_BYTES),
    )(vec)[0]


PACK_COLS = 1024
BIG = (("mlp_w_up", 2, 1024, True), ("mlp_w_down", 2, 1024, False), ("xa_w_kv", 2, 512, True), ("xa_w_q", 2, 256, False),
       ("xa_w_o", 2, 256, False), ("ab_w_out", 1, 256, False), ("cd_w_out", 1, 256, False), ("cd_w_in", 1, 576, True),
       ("ab_w_in", 1, 642, True))
ENTRIES = tuple((name, layer, rows, transposed) for name, layers, rows, transposed in BIG for layer in range(layers))
FIRST_ENTRIES = tuple(e for e in ENTRIES if e[0].startswith("ab_"))
LATER_ENTRIES = tuple(e for e in ENTRIES if not e[0].startswith("ab_"))


def _tile_rows(entry):
    return -(-entry[2] // 16) * 16


def _padded_rows(entries):
    return -(-sum(_tile_rows(e) for e in entries) // 32) * 32


SMALL = (("ab_norm", (1, 1024)), ("ab_conv_w", (1, 4, 512)), ("ab_conv_b", (1, 512)), ("lru_w_a", (1, 8, 64, 64)),
         ("lru_b_a", (1, 512)), ("lru_w_i", (1, 8, 64, 64)), ("lru_b_i", (1, 512)), ("lru_lambda", (1, 512)),
         ("fox_b_f", (1, 8)), ("cd_norm", (1, 1024)), ("cd_sink", (1, 8)), ("xa_norm", (2, 1024)),
         ("xa_mem_norm", (2, 1024)), ("mlp_norm", (2, 1024)), ("final_norm", (1024,)), ("loss", (1,)))
SPLIT_SMALL = ("ab_conv_w", "cd_norm")


def _pack_rows(parts, entries, dtype):
    lead = parts[0].shape[:-2]
    zeros = lambda rows: jnp.zeros(lead + (rows, PACK_COLS), dtype)
    pieces = [jnp.pad(p.astype(dtype), ((0, 0),) * len(lead) + ((0, _tile_rows(e) - e[2]), (0, 0))) for p, e in zip(parts, entries)]
    tail = _padded_rows(entries) - sum(_tile_rows(e) for e in entries)
    return jnp.concatenate(pieces + ([zeros(tail)] if tail else []), axis=len(lead))


def _unpack_rows(packed, entries):
    out, r0 = [], 0
    for e in entries:
        out.append(packed[..., r0:r0 + e[2], :])
        r0 += _tile_rows(e)
    return out


def _shard_rows(w, entries):
    return [(w[name][layer].T if transposed else w[name][layer]) for name, layer, _, transposed in entries]


def _shard_blocks(rows):
    out = {}
    for (name, layer, _, transposed), r in zip(ENTRIES, rows):
        out.setdefault(name, []).append(r.T if transposed else r)
    return {name: jnp.stack(layers) for name, layers in out.items()}


def _pack_small(vals):
    flat = jnp.concatenate([vals[name].astype(F32).reshape(-1) for name, _ in SMALL])
    size = -(-flat.shape[0] // (8 * LANES)) * (8 * LANES)
    return jnp.pad(flat, (0, size - flat.shape[0])).reshape(-1, LANES)


def _unpack_small(packed):
    flat, out, at = packed.reshape(-1), {}, 0
    for name, shape in SMALL:
        out[name] = flat[at:at + math.prod(shape)].reshape(shape)
        at += math.prod(shape)
    return out


def _chip_part(full, chip):
    width = full.shape[-1] // N_CHIPS
    return lax.dynamic_slice_in_dim(full, chip * width, width, axis=full.ndim - 1)


def _chip_embed(part, chip):
    full = jnp.zeros(part.shape[:-1] + (part.shape[-1] * N_CHIPS,), part.dtype)
    return lax.dynamic_update_slice_in_dim(full, part, chip * part.shape[-1], axis=part.ndim - 1)


SUBLAYER_KEYS = {"ab_w_in": ("ab", "w_in_t"), "ab_w_out": ("ab", "w_out"), "cd_w_in": ("cd", "w_in_t"), "cd_w_out": ("cd", "w_out"),
                 "xa_w_q": ("xa", "w_q"), "xa_w_kv": ("xa", "w_kv_t"), "xa_w_o": ("xa", "w_o"), "mlp_w_up": ("mlp", "w_up_t"),
                 "mlp_w_down": ("mlp", "w_down")}


def _sublayer(name, layer):
    block, leaf = SUBLAYER_KEYS[name]
    return (block if block in ("ab", "cd") else f"{block}{layer}"), leaf


def _set_big(params, full_rows):
    for (name, layer), rows in full_rows.items():
        block, leaf = _sublayer(name, layer)
        if name == "ab_w_in":
            rows = jnp.concatenate([rows, jnp.zeros((LANES - 8, PACK_COLS), rows.dtype)])
        params[block][leaf] = rows


def _build_params(full_rows, w):
    pad = lambda a: jnp.pad(a, ((0, 0), (0, LANES - a.shape[1])))
    params = {
        "ab": dict(norm=w["ab_norm"], conv_w=w["ab_conv_w"][0], conv_b=w["ab_conv_b"], w_a=_block_diag(w["lru_w_a"][0]),
                   b_a=w["lru_b_a"], w_i=_block_diag(w["lru_w_i"][0]), b_i=w["lru_b_i"], lam=w["lru_lambda"],
                   b_f=w["fox_b_f"].reshape(8, 1)),
        "cd": dict(norm=w["cd_norm"], sink=pad(w["cd_sink"])),
        "final_norm": w["final_norm"].reshape(1, D_MODEL),
    }
    for layer in range(2):
        params[f"xa{layer}"] = dict(norm=w["xa_norm"][layer:layer + 1], mem_norm=w["xa_mem_norm"][layer:layer + 1])
        params[f"mlp{layer}"] = dict(norm=w["mlp_norm"][layer:layer + 1])
    _set_big(params, full_rows)
    return params


def _grad_rows(g, entries=ENTRIES):
    out = []
    for name, layer, _, _ in entries:
        block, leaf = _sublayer(name, layer)
        out.append(g[block][leaf])
    return out


def _reduce_group(entry):
    name, layer = entry[0], entry[1]
    if name.startswith("ab_"):
        return 2
    return 0 if layer == 1 or name.startswith("cd_") else 1


REDUCE_GROUPS = tuple(tuple(e for e in ENTRIES if _reduce_group(e) == group) for group in range(3))


def _reduce_tile(half):
    return max(t for t in range(16, 1025, 16) if half % t == 0)


def _reduce_start(grads_by_chip, core, chip, tag):
    half = grads_by_chip.shape[1] // 2
    tile = _reduce_tile(half)
    steps = half // tile
    place = jnp.stack([core, chip]).astype(jnp.int32)
    got = _swap_cores(grads_by_chip, name=f"swap_cores_{tag}", half_rows=half)
    block = (1, tile, PACK_COLS)

    def sum_cores(place_ref, mine_ref, got_ref, f32_ref, bf16_ref):
        total = mine_ref[...].astype(F32) + got_ref[...].astype(F32)
        f32_ref[...] = total
        bf16_ref[...] = total.astype(BF16)

    pair32, pair16 = pl.pallas_call(
        sum_cores, name=f"sum_cores_{tag}",
        grid_spec=pltpu.PrefetchScalarGridSpec(
            num_scalar_prefetch=1, grid=(N_CHIPS, steps),
            in_specs=[pl.BlockSpec(block, lambda s, i, place_ref: (s, place_ref[0] * steps + i, 0)),
                      pl.BlockSpec(block, lambda s, i, place_ref: (s, i, 0))],
            out_specs=[pl.BlockSpec(block, lambda s, i, place_ref: (s, i, 0))] * 2),
        out_shape=[jax.ShapeDtypeStruct((N_CHIPS, half, PACK_COLS), F32), jax.ShapeDtypeStruct((N_CHIPS, half, PACK_COLS), BF16)],
        compiler_params=_params("arbitrary", "arbitrary"),
    )(place, grads_by_chip, got)
    return pair32, _scatter_start(pair16, tag), place


def _reduce_finish(state, core, tag, after):
    pair32, started, place = state
    half = pair32.shape[1]
    tile = _reduce_tile(half)
    landed = _scatter_wait(started, after, tag)

    def sum_chips(place_ref, own_ref, landed_ref, out_ref):
        out_ref[...] = own_ref[0] + landed_ref[0].astype(F32) + landed_ref[1].astype(F32) + landed_ref[2].astype(F32)

    done = pl.pallas_call(
        sum_chips, name=f"sum_chips_{tag}",
        grid_spec=pltpu.PrefetchScalarGridSpec(
            num_scalar_prefetch=1, grid=(half // tile,),
            in_specs=[pl.BlockSpec((1, tile, PACK_COLS), lambda i, place_ref: (place_ref[1], i, 0)),
                      pl.BlockSpec((3, tile, PACK_COLS), lambda i, place_ref: (0, i, 0))],
            out_specs=pl.BlockSpec((tile, PACK_COLS), lambda i, place_ref: (i, 0))),
        out_shape=jax.ShapeDtypeStruct((half, PACK_COLS), F32), compiler_params=_params("arbitrary"),
    )(place, pair32, landed)
    theirs = _swap_cores(done, name=f"share_halves_{tag}")
    return jnp.where(core == 0, jnp.concatenate([done, theirs]), jnp.concatenate([theirs, done]))


def kernel(x, mem, ab_norm, ab_w_in, ab_conv_w, ab_conv_b, lru_w_a, lru_b_a, lru_w_i, lru_b_i, lru_lambda, fox_b_f, ab_w_out, cd_norm, cd_w_in, cd_sink, cd_w_out, xa_norm, xa_mem_norm, xa_w_q, xa_w_kv, xa_w_o, mlp_norm, mlp_w_up, mlp_w_down, final_norm, loss_target, m_ab_norm, m_ab_w_in, m_ab_conv_w, m_ab_conv_b, m_lru_w_a, m_lru_b_a, m_lru_w_i, m_lru_b_i, m_lru_lambda, m_fox_b_f, m_ab_w_out, m_cd_norm, m_cd_w_in, m_cd_sink, m_cd_w_out, m_xa_norm, m_xa_mem_norm, m_xa_w_q, m_xa_w_kv, m_xa_w_o, m_mlp_norm, m_mlp_w_up, m_mlp_w_down, m_final_norm, v_ab_norm, v_ab_w_in, v_ab_conv_w, v_ab_conv_b, v_lru_w_a, v_lru_b_a, v_lru_w_i, v_lru_b_i, v_lru_lambda, v_fox_b_f, v_ab_w_out, v_cd_norm, v_cd_w_in, v_cd_sink, v_cd_w_out, v_xa_norm, v_xa_mem_norm, v_xa_w_q, v_xa_w_kv, v_xa_w_o, v_mlp_norm, v_mlp_w_up, v_mlp_w_down, v_final_norm):
    given = dict(locals())
    weight_names = [name for name, _ in SMALL if name != "loss"] + [name for name, _, _, _ in BIG]
    w = {name: given[name] for name in weight_names}
    chip = 2 * lax.axis_index("x") + lax.axis_index("y")
    core = lax.axis_index("c")

    slot = lax.broadcasted_iota(jnp.int32, (N_CHIPS, 1, 1), 0)

    def whole(entries, mine, gathered):
        return {(name, layer): jnp.where(slot == chip, own[None], got).reshape(N_CHIPS * rows, PACK_COLS)
                for (name, layer, rows, _), own, got in zip(entries, _unpack_rows(mine, entries), _unpack_rows(gathered, entries))}

    mine_first = _pack_rows(_shard_rows(w, FIRST_ENTRIES), FIRST_ENTRIES, BF16)
    mine_later = _pack_rows(_shard_rows(w, LATER_ENTRIES), LATER_ENTRIES, BF16)
    first = _gather_chips(mine_first)
    owner = (core == 0).astype(F32)
    quarters = {name: _chip_embed(w[name], chip) * owner for name in SPLIT_SMALL}
    zeros = {name: jnp.zeros(shape, F32) for name, shape in SMALL}
    small_packed = _sum_all_devices(_pack_small({**zeros, **quarters}))
    small_full = _unpack_small(small_packed)
    started = _gather_start(mine_later, after=(small_packed, first))
    params = _build_params(whole(FIRST_ENTRIES, mine_first, first),
                           {**w, "ab_conv_w": small_full["ab_conv_w"], "cd_norm": small_full["cd_norm"]})
    params["ab"]["norm"] = params["ab"]["norm"] + started[8][0, 0]

    def complete(h):
        mine, landed = _gather_wait(started, after=h)
        _set_big(params, whole(LATER_ENTRIES, mine, _gather_pass(landed)))

    reducing = {}

    def grads_ready(group, g):
        entries = REDUCE_GROUPS[group]
        by_chip = _pack_rows([r.reshape(N_CHIPS, e[2], PACK_COLS) for r, e in zip(_grad_rows(g, entries), entries)], entries, BF16)
        reducing[group] = _reduce_start(by_chip, core, chip, f"g{group}")
        if group < 2:
            block, leaf = ("mlp0", "w_down") if group == 0 else ("ab", "w_out")
            params[block][leaf] = params[block][leaf] + reducing[group][1][8][0, 0].astype(BF16)

    loss_part, grad_x, g = _local_step(x, mem, loss_target, params, complete, grads_ready)
    grads_ready(2, g)
    reduced, after = {}, grad_x
    for group, entries in enumerate(REDUCE_GROUPS):
        after = _reduce_finish(reducing[group], core, f"g{group}", after=after)
        reduced.update({(e[0], e[1]): r for e, r in zip(entries, _unpack_rows(after, entries))})
    grads = _shard_blocks([reduced[e[0], e[1]] for e in ENTRIES])
    pair = lambda key, leaf: jnp.stack([g[f"{key}0"][leaf], g[f"{key}1"][leaf]])

    small_local = {"ab_norm": g["ab"]["norm"], "ab_conv_w": g["ab"]["conv_w"], "ab_conv_b": g["ab"]["conv_b"],
                   "lru_w_a": g["ab"]["w_a"], "lru_b_a": g["ab"]["b_a"], "lru_w_i": g["ab"]["w_i"], "lru_b_i": g["ab"]["b_i"],
                   "lru_lambda": g["ab"]["lam"], "fox_b_f": g["ab"]["b_f"], "cd_norm": g["cd"]["norm"], "cd_sink": g["cd"]["sink"],
                   "xa_norm": pair("xa", "norm"), "xa_mem_norm": pair("xa", "mem_norm"), "mlp_norm": pair("mlp", "norm"),
                   "final_norm": g["final_norm"], "loss": loss_part[0, :1]}
    small_sum_packed = _sum_all_devices(_pack_small(small_local))
    small_sum = _unpack_small(small_sum_packed)
    loss = small_sum["loss"][0]

    delta, new_m, new_v = {}, {}, {}
    for name, _, _, _ in BIG:
        shape = w[name].shape
        flat = lambda a: a.reshape(-1, shape[-1])
        d, m2, v2 = _adamw(flat(w[name]), flat(grads[name]), flat(given["m_" + name]), flat(given["v_" + name]), name=f"adamw_{name}")
        delta[name], new_m[name], new_v[name] = d.reshape(shape), m2.reshape(shape), v2.reshape(shape)

    def small_state(prefix):
        vals = {name: (given[prefix + name] if name != "loss" else jnp.zeros((1,), F32)) for name, _ in SMALL}
        for name in SPLIT_SMALL:
            vals[name] = _chip_embed(vals[name], chip)
        return _pack_small(vals)
    packed = _adamw(small_state(""), small_sum_packed, small_state("m_"), small_state("v_"), name="adamw_small")
    for store, vals in zip((delta, new_m, new_v), packed):
        for name, val in _unpack_small(vals).items():
            store[name] = _chip_part(val, chip) if name in SPLIT_SMALL else val
    for name in SPLIT_SMALL:
        small_sum[name] = _chip_part(small_sum[name], chip)
    grads.update({name: small_sum[name] for name, _ in SMALL})

    order = ["ab_norm", "ab_w_in", "ab_conv_w", "ab_conv_b", "lru_w_a", "lru_b_a", "lru_w_i", "lru_b_i", "lru_lambda", "fox_b_f",
             "ab_w_out", "cd_norm", "cd_w_in", "cd_sink", "cd_w_out", "xa_norm", "xa_mem_norm", "xa_w_q", "xa_w_kv", "xa_w_o",
             "mlp_norm", "mlp_w_up", "mlp_w_down", "final_norm"]
    return (loss, grad_x, *[grads[n] for n in order], *[delta[n] for n in order], *[new_m[n] for n in order],
            *[new_v[n] for n in order])
```

```python
import functools
import math

import jax
import jax.numpy as jnp
from jax import lax
from jax.experimental import pallas as pl
from jax.experimental.pallas import tpu as pltpu

F32 = jnp.float32
BF16 = jnp.bfloat16
MESH = pl.DeviceIdType.MESH

D_MODEL = 1024
SEQ = 2048
HEAD_DIM = 64
LRU_WIDTH = 512
LRU_BLOCKS = 8
LRU_C = 8.0
CONV_WIDTH = 4
ATT_W = 512
SWA_KW = 128
SWA_WINDOW = 128
DIL_PATTERN = ((128, 1), (512, 4), (2048, 16))
MEM_LEN = 256
XA_HEADS = 4
XA_HEAD_DIM = 256
D_FF = 4096
ROPE_THETA = 10000.0
EPS = 1e-6
N_CHIPS = 4
LANES = 128

ADAM_LR, ADAM_B1, ADAM_B2, ADAM_EPS, ADAM_WD, ADAM_STEP = 0.001, 0.9, 0.999, 1e-08, 0.01, 10

VMEM_LIMIT_BYTES = 56 * 1024 * 1024
MASKED = -1e30
ROW_TILE = 512
LRU_CHUNK = 512
ATT_BLOCK = 128
BAND_ROWS = 256
ATT_CHUNK = 512
CAUSAL_ROWS = 256
CLASS_ROWS = 512


def _params(*sem):
    return pltpu.CompilerParams(dimension_semantics=sem, vmem_limit_bytes=VMEM_LIMIT_BYTES)


def _rowwise(fn, rows, consts=(), out_rows=(), out_accs=(), *, name, tile=ROW_TILE, row_maps=None):
    rows = [r if isinstance(r, tuple) else (r, r.shape[1], 0) for r in rows]
    consts = list(consts)
    nr, nc, no = len(rows), len(consts), len(out_rows)
    dealt_in = [r[3] if isinstance(r[0], str) else None for r in rows]
    dealt_out = [o[2] if len(o) == 3 else None for o in out_rows]
    total = next(r[0].shape[0] for r in rows if not isinstance(r[0], str))
    tile = min(tile, total)
    assert total % tile == 0
    n = total // tile
    n_acc = len(out_accs)

    def body(*refs):
        scratch = list(refs[nr + nc + no + n_acc:])
        vals = []
        for ref, d, row in zip(refs[:nr], dealt_in, rows):
            if d is None:
                vals.append(ref[...])
                continue
            sc, width = scratch.pop(0), row[2]
            for r in range(d):
                for c in range(width // LANES):
                    lanes = slice(r * width + c * LANES, r * width + (c + 1) * LANES)
                    sc.at[c][pl.ds(r, tile // d, stride=d), :] = ref[:, lanes].astype(F32)
            vals.append(jnp.concatenate([sc[c] for c in range(width // LANES)], axis=1))
        outs = fn(*vals, *[r[...] for r in refs[nr:nr + nc]])
        outs = tuple(outs) if isinstance(outs, (tuple, list)) else (outs,)
        for ref, val, d, spec in zip(refs[nr + nc:nr + nc + no], outs[:no], dealt_out, out_rows):
            if d is None:
                ref[...] = val.astype(ref.dtype)
                continue
            sc, width = scratch.pop(0), spec[0]
            for c in range(width // LANES):
                sc[c] = val[:, c * LANES:(c + 1) * LANES].astype(F32)
            for r in range(d):
                for c in range(width // LANES):
                    lanes = slice(r * width + c * LANES, r * width + (c + 1) * LANES)
                    ref[:, lanes] = sc.at[c][pl.ds(r, tile // d, stride=d), :].astype(ref.dtype)
        for ref, val in zip(refs[nr + nc + no:nr + nc + no + n_acc], outs[no:]):
            @pl.when(pl.program_id(0) == 0)
            def _(ref=ref):
                ref[...] = jnp.zeros(ref.shape, ref.dtype)
            ref[...] += val

    in_specs, args, scratch_shapes = [], [], []
    for idx, row in enumerate(rows):
        if isinstance(row[0], str):
            _, arr, width, d = row
            in_specs.append(pl.BlockSpec((tile // d, d * width), lambda i: (i, 0)))
            scratch_shapes.append(pltpu.VMEM((width // LANES, tile, LANES), F32))
        elif row_maps is not None and row_maps[idx] is not None:
            arr, width, _ = row
            in_specs.append(pl.BlockSpec((tile, width), row_maps[idx]))
        else:
            arr, width, cb = row
            in_specs.append(pl.BlockSpec((tile, width), functools.partial(lambda i, cb: (i, cb), cb=cb)))
        args.append(arr)
    in_specs += [pl.BlockSpec(c.shape, lambda i: (0, 0)) for c in consts]
    out_shape, out_specs = [], []
    for spec, d in zip(out_rows, dealt_out):
        w, dt = spec[0], spec[1]
        if d is None:
            out_shape.append(jax.ShapeDtypeStruct((total, w), dt))
            out_specs.append(pl.BlockSpec((tile, w), lambda i: (i, 0)))
        else:
            out_shape.append(jax.ShapeDtypeStruct((total // d, d * w), dt))
            out_specs.append(pl.BlockSpec((tile // d, d * w), lambda i: (i, 0)))
            scratch_shapes.append(pltpu.VMEM((w // LANES, tile, LANES), F32))
    out_shape += [jax.ShapeDtypeStruct(s, F32) for s in out_accs]
    out_specs += [pl.BlockSpec(s, lambda i: (0, 0)) for s in out_accs]
    return pl.pallas_call(
        body, grid=(n,), in_specs=in_specs, out_specs=out_specs, out_shape=out_shape, scratch_shapes=scratch_shapes, name=name,
        compiler_params=_params("arbitrary"),
    )(*args, *consts)


MATMUL_VMEM_BYTES = 40 * 1024 * 1024


def _matmul_tiles(m, n, k, tm, tn, out_bytes):
    tm, tn, tk = min(tm, m), min(tn, n), k

    def need():
        return 2 * (2 * tk * (tm + tn) + tm * tn * out_bytes) + (0 if tk == k else 4 * tm * tn)

    while need() > MATMUL_VMEM_BYTES:
        if tm >= tn and tm % 256 == 0:
            tm //= 2
        elif tn % 256 == 0:
            tn //= 2
        else:
            tk //= 2
    return tm, tn, tk


def _matmul(a, b, *, ta=False, tb=False, outs=(F32,), epilogue=None, extras=(), consts=(), sums=(), whole_rows=False,
            tm=1024, tn=1024, name):
    m, k = (a.shape[1], a.shape[0]) if ta else a.shape
    n = b.shape[0] if tb else b.shape[1]
    assert (b.shape[1] if tb else b.shape[0]) == k
    outs = [o if isinstance(o, tuple) else (o, None) for o in outs]
    out_bytes = sum(jnp.dtype(dt).itemsize for dt, w in outs if w is None) + sum(e.dtype.itemsize for e in extras)
    tm, tn, tk = _matmul_tiles(m, n, k, tm, tn, out_bytes)
    assert m % tm == 0 and n % tn == 0 and k % tk == 0, (name, m, n, k)
    nk = k // tk
    ne, nc, no = len(extras), len(consts), len(outs)
    assert tn == n or not (sums or whole_rows or any(w is not None for _, w in outs)), name
    dims = (((0 if ta else 1,), (1 if tb else 0,)), ((), ()))

    def body(*refs):
        a_ref, b_ref = refs[:2]
        e_refs, o_refs = refs[2:2 + ne + nc], refs[2 + ne + nc:2 + ne + nc + no]
        s_refs = refs[2 + ne + nc + no:2 + ne + nc + no + len(sums)]

        def finish(acc):
            vals = (acc,) if epilogue is None else epilogue(acc, *[e[...] for e in e_refs])
            for ref, val in zip(o_refs, vals[:no]):
                ref[...] = val.astype(ref.dtype)
            for ref, val in zip(s_refs, vals[no:]):
                @pl.when(pl.program_id(0) == 0)
                def _(ref=ref, val=val):
                    ref[...] = val

                @pl.when(pl.program_id(0) > 0)
                def _(ref=ref, val=val):
                    ref[...] += val

        prod = lax.dot_general(a_ref[...], b_ref[...], dims, preferred_element_type=F32)
        if nk == 1:
            finish(prod)
        else:
            acc_ref = refs[-1]
            step = pl.program_id(2)

            @pl.when(step == 0)
            def _():
                acc_ref[...] = prod

            @pl.when(step > 0)
            def _():
                acc_ref[...] += prod

            @pl.when(step == nk - 1)
            def _():
                finish(acc_ref[...])

    a_spec = pl.BlockSpec((tk, tm), lambda i, j, s: (s, i)) if ta else pl.BlockSpec((tm, tk), lambda i, j, s: (i, s))
    b_spec = pl.BlockSpec((tn, tk), lambda i, j, s: (j, s)) if tb else pl.BlockSpec((tk, tn), lambda i, j, s: (s, j))
    tile_spec = pl.BlockSpec((tm, tn), lambda i, j, s: (i, j))
    whole = lambda shape: pl.BlockSpec(shape, lambda i, j, s: (0, 0))
    return pl.pallas_call(
        body, grid=(m // tm, n // tn, nk),
        in_specs=[a_spec, b_spec] + [tile_spec] * ne + [whole(c.shape) for c in consts],
        out_specs=[tile_spec if w is None else pl.BlockSpec((tm, w), lambda i, j, s: (i, 0)) for _, w in outs]
        + [whole(shape) for shape in sums],
        out_shape=[jax.ShapeDtypeStruct((m, n if w is None else w), dt) for dt, w in outs]
        + [jax.ShapeDtypeStruct(shape, F32) for shape in sums],
        scratch_shapes=[pltpu.VMEM((tm, tn), F32)] if nk > 1 else [],
        name=name, compiler_params=_params("arbitrary" if sums else "parallel", "parallel", "arbitrary"),
    )(a, b, *extras, *consts)


def _split3(x):
    x1 = x.astype(BF16)
    r1 = x - x1.astype(F32)
    x2 = r1.astype(BF16)
    x3 = (r1 - x2.astype(F32)).astype(BF16)
    return x1, x2, x3


def _dot_exact(x, e):
    return sum(jnp.dot(p, e, preferred_element_type=F32) for p in _split3(x))


def _head_expand(heads, width):
    dh = width // heads
    rows = jnp.arange(LANES)[:, None]
    cols = jnp.arange(width)[None, :]
    return (cols // dh == rows).astype(BF16)


def _rstd(x):
    return lax.rsqrt(jnp.mean(x * x, axis=-1, keepdims=True) + EPS)


def _rms_fwd(x, gain, *, name):
    return _rowwise(lambda x, g: x * _rstd(x) * g, [x], [gain], [(x.shape[1], BF16)], name=name)[0]


def _rms_bwd_vals(x, dhn, gain):
    r = _rstd(x)
    xh = x * r
    dxh = dhn * gain
    dx = r * (dxh - xh * jnp.mean(dxh * xh, axis=-1, keepdims=True))
    return dx, jnp.sum(dhn * xh, axis=0, keepdims=True)


def _norm_input_grad(dz, w, x, dres, gain, *, tb=False, name):
    def epilogue(dhn, x, dres, g):
        dx, dg = _rms_bwd_vals(x, dhn, g)
        dh = dres + dx
        return dh, dh, dg
    return _matmul(dz, w, tb=tb, outs=(F32, BF16), extras=(x, dres), consts=(gain,), sums=((1, x.shape[1]),), epilogue=epilogue,
                   name=name)


def _loss_and_grad(h, target, gain, *, name):
    def fn(h, tgt, g):
        r = _rstd(h)
        err = h * r * g - tgt
        loss = 0.5 * jnp.sum(jnp.mean(err * err, axis=-1, keepdims=True), axis=0, keepdims=True)
        dx, dg = _rms_bwd_vals(h, err * (1.0 / D_MODEL), g)
        return dx, dx, jnp.broadcast_to(loss, (1, LANES)), dg
    d = h.shape[1]
    return _rowwise(fn, [h, target], [gain], [(d, F32), (d, BF16)], [(1, LANES), (1, d)], name=name)


def _adamw(w, g, m, v, *, name):
    rows, cols = w.shape
    tile = rows
    for cand in (512, 256, 128, 64, 32, 16, 8):
        if rows % cand == 0 and rows > cand:
            tile = cand
            break
    bc1 = 1.0 - ADAM_B1 ** ADAM_STEP
    bc2 = 1.0 - ADAM_B2 ** ADAM_STEP

    def fn(w, g, m, v):
        m2 = ADAM_B1 * m + (1.0 - ADAM_B1) * g
        v2 = ADAM_B2 * v + (1.0 - ADAM_B2) * (g * g)
        delta = -ADAM_LR * ((m2 / bc1) / (jnp.sqrt(v2 / bc2) + ADAM_EPS) + ADAM_WD * w)
        return delta, m2, v2
    return _rowwise(fn, [w, g, m, v], [], [(cols, F32)] * 3, name=name, tile=tile)


def _attn_specs(arr, width, cb, classes, rows_block, whole, together=1):
    ncols = arr.shape[2] // (classes * width)
    lanes, at = (width, lambda r: r * ncols + cb) if together == 1 else (together * ncols * width, lambda r: r)
    if whole:
        return pl.BlockSpec((1, arr.shape[1], lanes), lambda n, r, i: (n, 0, at(r)))
    return pl.BlockSpec((1, rows_block, lanes), lambda n, r, i: (n, i, at(r)))


def _class_window(refs, spans, together, cl):
    if together == 1:
        return refs
    return [ref.at[:, :, pl.ds((cl * ncols + cb) * width, width)] for ref, (ncols, cb, width) in zip(refs, spans)]


def _attn_tiles(mode, lq, lk, backward=False):
    if mode == "full":
        return min(lq, 256), min(lq, 256), lk
    if mode == "band":
        tq = min(BAND_ROWS, lq)
        rows = tq if backward else ATT_BLOCK
        return tq, rows, min(rows + ATT_BLOCK, lk)
    return CAUSAL_ROWS, CAUSAL_ROWS, ATT_CHUNK


def _window(mode, row0, j, rows, win, lk):
    if mode == "causal":
        return pl.multiple_of(j * win, win), row0 // win + 1
    if mode == "band":
        return pl.multiple_of(jnp.clip(row0 + rows - win, 0, lk - win), ATT_BLOCK), 1
    return 0, 1


def _allowed(mode, row0, start, rows, win, max_dist):
    if mode == "full":
        return None
    dist = (row0 + lax.broadcasted_iota(jnp.int32, (rows, win), 0)) - (start + lax.broadcasted_iota(jnp.int32, (rows, win), 1))
    ok = dist >= 0
    if max_dist is not None:
        ok = ok & (dist <= max_dist)
    return ok


def _head_groups(heads, dh):
    gw = max(LANES, dh)
    return gw, gw // dh, heads // (gw // dh)


def _own_lanes(rows, gw, dh, u):
    return lax.broadcasted_iota(jnp.int32, (rows, gw), 1) // dh == u


def _only_head(x, own, per, u):
    return x if per == 1 else jnp.where(own[u], x, jnp.zeros_like(x))


def _step_scores(qz, kw, kb_row, ok):
    s = lax.dot_general(qz, kw, (((1,), (1,)), ((), ())), preferred_element_type=F32)
    if kb_row is not None:
        s = s - kb_row
    if ok is not None:
        s = jnp.where(ok, s, MASKED)
    return s


def _attn_fwd(q, k, v, kb=None, *, classes=1, heads, dh, mode, max_dist=None, bf16_copy=False, name):
    (qa, qc), (ka, kc), (va, vc) = q, k, v
    n, lq, lk = qa.shape[0], qa.shape[1], ka.shape[1]
    width = heads * dh
    tq, rows, win = _attn_tiles(mode, lq, lk)
    gw, per, groups = _head_groups(heads, dh)
    scale = dh ** -0.5
    has_kb = kb is not None
    single = mode != "causal"
    together = min(classes, max(1, CLASS_ROWS // lq))
    ncols = lambda arr: arr.shape[2] // (classes * width)
    spans = [(ncols(qa), qc, width), (ncols(ka), kc, width), (ncols(va), vc, width), (1, 0, width), (1, 0, LANES), (1, 0, width)]

    def body(*refs):
        for cl in range(together):
            for sub in range(tq // rows):
                part(_class_window(refs, spans, together, cl), pl.program_id(2) * tq + sub * rows, slice(sub * rows, (sub + 1) * rows))

    def part(refs, row0, rs):
        q_ref, k_ref, v_ref = refs[:3]
        kb_ref = refs[3] if has_kb else None
        n_in = 4 if has_kb else 3
        o_ref, lse_ref = refs[n_in:n_in + 2]
        o16_ref = refs[n_in + 2] if bf16_copy else None
        lane = lax.broadcasted_iota(jnp.int32, (rows, LANES), 1)
        own = [_own_lanes(rows, gw, dh, u) for u in range(per)]

        def step(j, carry):
            m_in, l_in = carry
            m_out, l_out = m_in, l_in
            start, _ = _window(mode, row0, j, rows, win, lk)
            ok = _allowed(mode, row0, start, rows, win, max_dist)
            for g in range(groups):
                cols = slice(g * gw, (g + 1) * gw)
                qg = q_ref[0, rs, cols] * scale
                kw = k_ref[0, pl.ds(start, win), cols]
                vw = v_ref[0, pl.ds(start, win), cols]
                alpha_g = new_g = None
                for u in range(per):
                    h = g * per + u
                    kb_row = kb_ref[0, h, pl.ds(j, 1), :] if has_kb else None
                    s = _step_scores(_only_head(qg, own, per, u), kw, kb_row, ok)
                    m_new = jnp.max(s, axis=1, keepdims=True)
                    if not single:
                        m_old = m_in[:, h:h + 1]
                        m_new = jnp.maximum(m_old, m_new)
                        alpha = jnp.exp(m_old - m_new)
                        alpha_g = alpha if u == 0 else jnp.where(own[u], alpha, alpha_g)
                    p = jnp.exp(s - m_new)
                    l_new = jnp.sum(p, axis=1, keepdims=True)
                    r = jnp.dot(p.astype(BF16), vw, preferred_element_type=F32)
                    if single:
                        r = r * (1.0 / l_new)
                    else:
                        l_new = alpha * l_in[:, h:h + 1] + l_new
                    new_g = r if u == 0 else jnp.where(own[u], r, new_g)
                    m_out = jnp.where(lane == h, m_new, m_out)
                    l_out = jnp.where(lane == h, l_new, l_out)
                o_ref[0, rs, cols] = new_g if single else alpha_g * o_ref[0, rs, cols] + new_g
                if bf16_copy:
                    o16_ref[0, rs, cols] = new_g.astype(BF16)
            return m_out, l_out

        carry = (jnp.full((rows, LANES), MASKED, F32), jnp.zeros((rows, LANES), F32))
        if single:
            m_all, l_all = step(0, carry)
            l_all = jnp.where(lane < heads, l_all, 1.0)
        else:
            o_ref[...] = jnp.zeros(o_ref.shape, F32)
            m_all, l_all = lax.fori_loop(0, _window(mode, row0, 0, rows, win, lk)[1], step, carry)
            l_all = jnp.where(lane < heads, l_all, 1.0)
            inv = 1.0 / l_all
            for g in range(groups):
                cols = slice(g * gw, (g + 1) * gw)
                inv_g = inv[:, g * per:g * per + 1]
                for u in range(1, per):
                    inv_g = jnp.where(own[u], inv[:, g * per + u:g * per + u + 1], inv_g)
                o_ref[0, rs, cols] = o_ref[0, rs, cols] * inv_g
        lse_ref[0, rs, :] = jnp.where(lane < heads, m_all + jnp.log(l_all), 0.0)

    in_specs = [_attn_specs(qa, width, qc, classes, tq, False, together), _attn_specs(ka, width, kc, classes, win, True, together),
                _attn_specs(va, width, vc, classes, win, True, together)]
    args = [qa, ka, va]
    if has_kb:
        assert together == 1
        in_specs.append(pl.BlockSpec((1,) + kb.shape[1:], lambda n_, r, i: (n_, 0, 0, 0)))
        args.append(kb)
    out_shape = [jax.ShapeDtypeStruct((n, lq, classes * width), F32), jax.ShapeDtypeStruct((n, lq, classes * LANES), F32)]
    out_specs = [pl.BlockSpec((1, tq, together * width), lambda n_, r, i: (n_, i, r)),
                 pl.BlockSpec((1, tq, together * LANES), lambda n_, r, i: (n_, i, r))]
    if bf16_copy:
        assert single
        out_shape.append(jax.ShapeDtypeStruct((n, lq, classes * width), BF16))
        out_specs.append(out_specs[0])
    return pl.pallas_call(
        body, grid=(n, classes // together, lq // tq), in_specs=in_specs, out_specs=out_specs, out_shape=out_shape, name=name,
        compiler_params=_params("parallel", "parallel", "arbitrary"),
    )(*args)


def _attn_bwd(q, k, v, do, lse, delta, kb=None, *, classes=1, heads, dh, mode, max_dist=None, dq_dtype=F32, name):
    (qa, qc), (ka, kc), (va, vc), (da, dc) = q, k, v, do
    n, lq, lk = qa.shape[0], qa.shape[1], ka.shape[1]
    width = heads * dh
    tq, rows, win = _attn_tiles(mode, lq, lk, backward=True)
    gw, per, groups = _head_groups(heads, dh)
    scale = dh ** -0.5
    has_kb = kb is not None
    single = mode != "causal"
    nt = (((1,), (1,)), ((), ()))
    tn = (((0,), (0,)), ((), ()))
    together = min(classes, max(1, CLASS_ROWS // lq))
    ncols = lambda arr: arr.shape[2] // (classes * width)
    spans = [(ncols(qa), qc, width), (ncols(ka), kc, width), (ncols(va), vc, width), (ncols(da), dc, width), (1, 0, LANES),
             (1, 0, LANES), (1, 0, width), (1, 0, width), (1, 0, width)]

    def body(*refs):
        n_in = 7 if has_kb else 6
        dk_ref, dv_ref = refs[n_in + 1:n_in + 3]

        @pl.when(pl.program_id(2) == 0)
        def _():
            dk_ref[...] = jnp.zeros(dk_ref.shape, F32)
            dv_ref[...] = jnp.zeros(dv_ref.shape, F32)
            if has_kb:
                refs[n_in + 3][...] = jnp.zeros(refs[n_in + 3].shape, F32)

        for cl in range(together):
            for sub in range(tq // rows):
                part(_class_window(refs, spans, together, cl), pl.program_id(2) * tq + sub * rows, slice(sub * rows, (sub + 1) * rows))

    def part(refs, row0, rs):
        q_ref, k_ref, v_ref, do_ref, lse_ref, dl_ref = refs[:6]
        kb_ref = refs[6] if has_kb else None
        n_in = 7 if has_kb else 6
        dq_ref, dk_ref, dv_ref = refs[n_in:n_in + 3]
        dkb_ref, drow_ref = refs[n_in + 3:n_in + 5] if has_kb else (None, None)
        lse_all = lse_ref[0, rs, :]
        dl_all = dl_ref[0, rs, :]
        lane = lax.broadcasted_iota(jnp.int32, (rows, LANES), 1)
        own = [_own_lanes(rows, gw, dh, u) for u in range(per)]

        def step(j, drow_all):
            start, _ = _window(mode, row0, j, rows, win, lk)
            ok = _allowed(mode, row0, start, rows, win, max_dist)
            for g in range(groups):
                cols = slice(g * gw, (g + 1) * gw)
                qg = q_ref[0, rs, cols] * scale
                dog = do_ref[0, rs, cols]
                kw = k_ref[0, pl.ds(start, win), cols]
                vw = v_ref[0, pl.ds(start, win), cols]
                dq_g = dk_w = dv_w = None
                for u in range(per):
                    h = g * per + u
                    qz, doz = _only_head(qg, own, per, u), _only_head(dog, own, per, u)
                    kb_row = kb_ref[0, h, pl.ds(j, 1), :] if has_kb else None
                    p = jnp.exp(_step_scores(qz, kw, kb_row, ok) - lse_all[:, h:h + 1])
                    dp = lax.dot_general(doz, vw, nt, preferred_element_type=F32)
                    ds = p * (dp - dl_all[:, h:h + 1])
                    dsb = ds.astype(BF16)
                    r = jnp.dot(dsb, kw, preferred_element_type=F32)
                    dq_g = r if u == 0 else jnp.where(own[u], r, dq_g)
                    dk_u = lax.dot_general(dsb, qz, tn, preferred_element_type=F32)
                    dv_u = lax.dot_general(p.astype(BF16), doz, tn, preferred_element_type=F32)
                    dk_w = dk_u if u == 0 else dk_w + dk_u
                    dv_w = dv_u if u == 0 else dv_w + dv_u
                    if has_kb:
                        dkb_ref[0, h, pl.ds(j, 1), :] += -jnp.sum(ds, axis=0, keepdims=True)
                        drow_all = drow_all + jnp.where(lane == h, jnp.sum(ds, axis=1, keepdims=True), 0.0)
                dk_ref[0, pl.ds(start, win), cols] += dk_w
                dv_ref[0, pl.ds(start, win), cols] += dv_w
                if single:
                    dq_ref[0, rs, cols] = (dq_g * scale).astype(dq_ref.dtype)
                else:
                    dq_ref[0, rs, cols] += dq_g * scale
            return drow_all

        drow_all = jnp.zeros((rows, LANES), F32)
        if single:
            drow_all = step(0, drow_all)
        else:
            dq_ref[...] = jnp.zeros(dq_ref.shape, F32)
            drow_all = lax.fori_loop(0, _window(mode, row0, 0, rows, win, lk)[1], step, drow_all)
        if has_kb:
            drow_ref[0, rs, :] = drow_all

    row_spec = functools.partial(_attn_specs, classes=classes, rows_block=tq, whole=False, together=together)
    in_specs = [row_spec(qa, width, qc), _attn_specs(ka, width, kc, classes, win, True, together),
                _attn_specs(va, width, vc, classes, win, True, together), row_spec(da, width, dc), row_spec(lse, LANES, 0),
                row_spec(delta, LANES, 0)]
    args = [qa, ka, va, da, lse, delta]
    assert single or dq_dtype == F32
    out_shape = [jax.ShapeDtypeStruct((n, lq, classes * width), dq_dtype), jax.ShapeDtypeStruct((n, lk, classes * width), F32),
                 jax.ShapeDtypeStruct((n, lk, classes * width), F32)]
    kv_spec = pl.BlockSpec((1, lk, together * width), lambda n_, r, i: (n_, 0, r))
    out_specs = [pl.BlockSpec((1, tq, together * width), lambda n_, r, i: (n_, i, r)), kv_spec, kv_spec]
    if has_kb:
        assert together == 1
        kb_spec = pl.BlockSpec((1,) + kb.shape[1:], lambda n_, r, i: (n_, 0, 0, 0))
        in_specs.append(kb_spec)
        args.append(kb)
        out_shape += [jax.ShapeDtypeStruct(kb.shape, F32), jax.ShapeDtypeStruct((n, lq, LANES), F32)]
        out_specs += [kb_spec, pl.BlockSpec((1, tq, LANES), lambda n_, r, i: (n_, i, 0))]
    return pl.pallas_call(
        body, grid=(n, classes // together, lq // tq), in_specs=in_specs, out_specs=out_specs, out_shape=out_shape, name=name,
        compiler_params=_params("parallel", "parallel", "arbitrary"),
    )(*args)


def _rope_tables():
    half = HEAD_DIM // 2
    inv = ROPE_THETA ** (-jnp.arange(half, dtype=F32) / half)
    ang = jnp.arange(SEQ, dtype=F32)[:, None] * inv[None, :]
    cos = jnp.tile(jnp.cos(ang), (1, 2 * ATT_W // HEAD_DIM))
    sin = jnp.tile(jnp.concatenate([-jnp.sin(ang), jnp.sin(ang)], axis=1), (1, ATT_W // HEAD_DIM))
    return cos, sin


def _rotate(x, cos, sin):
    width = x.shape[1]
    lane = lax.broadcasted_iota(jnp.int32, x.shape, 1)
    first_half = (lane % HEAD_DIM) < (HEAD_DIM // 2)
    swapped = jnp.where(first_half, pltpu.roll(x, width - HEAD_DIM // 2, axis=1), pltpu.roll(x, HEAD_DIM // 2, axis=1))
    return x * cos[:, :width] + swapped * sin[:, :width]


def _gelu(x):
    k = math.sqrt(2.0 / math.pi)
    t = jnp.tanh(k * (x + 0.044715 * x * x * x))
    return 0.5 * x * (1.0 + t), t


def _gelu_grad(x, t):
    k = math.sqrt(2.0 / math.pi)
    return 0.5 * (1.0 + t) + 0.5 * x * (1.0 - t * t) * k * (1.0 + 3.0 * 0.044715 * x * x)


def _shift_down(x, halo, s):
    ext = jnp.concatenate([halo, x], axis=0)
    return pltpu.roll(ext, s, axis=0)[8:, :]


def _shift_up(x, halo, s):
    rows = x.shape[0]
    ext = jnp.concatenate([x, halo], axis=0)
    return pltpu.roll(ext, rows + 8 - s, axis=0)[:rows, :]


def _lru_gates(u, halo, cw, cb, wa, ba, wi, bi, lam):
    taps = [_shift_down(u, halo, CONV_WIDTH - 1 - k) for k in range(CONV_WIDTH - 1)] + [u]
    uc = cb + sum(cw[k:k + 1, :] * taps[k] for k in range(CONV_WIDTH))
    ucb = uc.astype(BF16)
    r = jax.nn.sigmoid(jnp.dot(ucb, wa, preferred_element_type=F32) + ba)
    gi = jax.nn.sigmoid(jnp.dot(ucb, wi, preferred_element_type=F32) + bi)
    sp = jnp.maximum(-lam, 0.0) + jnp.log(1.0 + jnp.exp(-jnp.abs(lam)))
    log_a = -LRU_C * r * sp
    a = jnp.exp(log_a)
    x2 = 2.0 * log_a
    one_minus_a2 = jnp.where(x2 > -0.01, -x2 * (1.0 + x2 * (0.5 + x2 * (1.0 / 6.0 + x2 / 24.0))), 1.0 - jnp.exp(x2))
    mult = jnp.sqrt(one_minus_a2)
    return taps, uc, ucb, r, gi, sp, a, mult


def _lru_specs(nchunk, reverse):
    def chunk(b, c):
        return b * nchunk + ((nchunk - 1 - c) if reverse else c)

    def halo_before(b, c):
        return jnp.maximum(chunk(b, c) * (LRU_CHUNK // 8) - 1, 0)

    return chunk, halo_before


def _lru_fwd(zug, cw, cb, wa, ba, wi, bi, lam, *, name):
    total = zug.shape[0]
    nchunk = SEQ // LRU_CHUNK
    w = LRU_WIDTH
    chunk, halo_before = _lru_specs(nchunk, False)

    def body(u_ref, uh_ref, g_ref, cw_ref, cb_ref, wa_ref, ba_ref, wi_ref, bi_ref, lam_ref, y_ref, h_ref, a_sc, x_sc, carry_sc):
        c = pl.program_id(1)
        u = u_ref[...]
        halo = jnp.where(c > 0, uh_ref[...], 0.0)
        _, uc, _, _, gi, _, a, mult = _lru_gates(u, halo, cw_ref[...], cb_ref[...], wa_ref[...], ba_ref[...],
                                                 wi_ref[...], bi_ref[...], lam_ref[...])
        a_sc[...] = a
        x_sc[...] = mult * (gi * uc)

        @pl.when(c == 0)
        def _():
            carry_sc[...] = jnp.zeros(carry_sc.shape, F32)

        def tile_step(t, h):
            r0 = pl.multiple_of(t * 8, 8)
            at = a_sc[pl.ds(r0, 8), :]
            xt = x_sc[pl.ds(r0, 8), :]
            rows = []
            for j in range(8):
                h = at[j:j + 1, :] * h + xt[j:j + 1, :]
                rows.append(h)
            h_ref[pl.ds(r0, 8), :] = jnp.concatenate(rows, axis=0)
            return h

        h_last = lax.fori_loop(0, LRU_CHUNK // 8, tile_step, carry_sc[0:1, :])
        carry_sc[0:1, :] = h_last
        gel, _ = _gelu(g_ref[...])
        y_ref[...] = (h_ref[...] * gel).astype(BF16)

    small = lambda arr: pl.BlockSpec(arr.shape, lambda b, c: (0, 0))
    return pl.pallas_call(
        body, grid=(total // SEQ, nchunk),
        in_specs=[pl.BlockSpec((LRU_CHUNK, w), lambda b, c: (chunk(b, c), 0)),
                  pl.BlockSpec((8, w), lambda b, c: (halo_before(b, c), 0)),
                  pl.BlockSpec((LRU_CHUNK, w), lambda b, c: (chunk(b, c), 1)),
                  small(cw), small(cb), small(wa), small(ba), small(wi), small(bi), small(lam)],
        out_specs=[pl.BlockSpec((LRU_CHUNK, w), lambda b, c: (chunk(b, c), 0))] * 2,
        out_shape=[jax.ShapeDtypeStruct((total, w), BF16), jax.ShapeDtypeStruct((total, w), F32)],
        scratch_shapes=[pltpu.VMEM((LRU_CHUNK, w), F32), pltpu.VMEM((LRU_CHUNK, w), F32), pltpu.VMEM((8, w), F32)],
        name=name, compiler_params=_params("arbitrary", "arbitrary"),
    )(zug, zug, zug, cw, cb, wa, ba, wi, bi, lam)


def _lru_bwd(zug, hl, dy, cw, cb, wa, ba, wi, bi, lam, *, name):
    total = zug.shape[0]
    nchunk = SEQ // LRU_CHUNK
    w = LRU_WIDTH
    chunk, halo_before = _lru_specs(nchunk, True)
    tn = (((0,), (0,)), ((), ()))
    nt = (((1,), (1,)), ((), ()))

    def body(u_ref, uh_ref, g_ref, hl_ref, hh_ref, dy_ref, cw_ref, cb_ref, wa_ref, ba_ref, wi_ref, bi_ref, lam_ref,
             dz_ref, dcw_ref, dcb_ref, dwa_ref, dba_ref, dwi_ref, dbi_ref, dlam_ref,
             a_sc, d_sc, g_sc, gcarry_sc, acarry_sc, duc_sc):
        b, c = pl.program_id(0), pl.program_id(1)
        first_chunk = c == nchunk - 1

        @pl.when((b == 0) & (c == 0))
        def _():
            for ref in (dcw_ref, dcb_ref, dwa_ref, dba_ref, dwi_ref, dbi_ref, dlam_ref):
                ref[...] = jnp.zeros(ref.shape, F32)

        @pl.when(c == 0)
        def _():
            gcarry_sc[...] = jnp.zeros(gcarry_sc.shape, F32)
            acarry_sc[...] = jnp.zeros(acarry_sc.shape, F32)
            duc_sc[...] = jnp.zeros(duc_sc.shape, F32)

        u = u_ref[...]
        halo = jnp.where(first_chunk, 0.0, uh_ref[...])
        cw, wa, wi, lam = cw_ref[...], wa_ref[...], wi_ref[...], lam_ref[...]
        taps, uc, ucb, r, gi, sp, a, mult = _lru_gates(u, halo, cw, cb_ref[...], wa, ba_ref[...], wi, bi_ref[...], lam)
        gate = g_ref[...]
        gel, th = _gelu(gate)
        dy = dy_ref[...]
        hl = hl_ref[...]
        a_sc[...] = a
        d_sc[...] = dy * gel
        dgate = dy * hl * _gelu_grad(gate, th)

        def tile_step(t, carry):
            g_next, a_next = carry
            r0 = pl.multiple_of((LRU_CHUNK // 8 - 1 - t) * 8, 8)
            dt = d_sc[pl.ds(r0, 8), :]
            at = a_sc[pl.ds(r0, 8), :]
            rows = [None] * 8
            for j in reversed(range(8)):
                g_next = dt[j:j + 1, :] + a_next * g_next
                a_next = at[j:j + 1, :]
                rows[j] = g_next
            g_sc[pl.ds(r0, 8), :] = jnp.concatenate(rows, axis=0)
            return g_next, a_next

        g_last, a_last = lax.fori_loop(0, LRU_CHUNK // 8, tile_step, (gcarry_sc[0:1, :], acarry_sc[0:1, :]))
        gcarry_sc[0:1, :] = g_last
        acarry_sc[0:1, :] = a_last

        gs = g_sc[...]
        hl_halo = jnp.where(first_chunk, 0.0, hh_ref[...])
        da = gs * _shift_down(hl, hl_halo, 1)
        dmult = gs * gi * uc
        dgi = gs * mult * uc
        duc = gs * mult * gi
        dlog_a = da * a - dmult * a * a / mult
        dr = dlog_a * (-LRU_C * sp)
        dsp = jnp.sum(dlog_a * (-LRU_C * r), axis=0, keepdims=True)
        dlam_ref[...] += -dsp * jax.nn.sigmoid(-lam)
        dpa = dr * r * (1.0 - r)
        dpi = dgi * gi * (1.0 - gi)
        dpab, dpib = dpa.astype(BF16), dpi.astype(BF16)
        dba_ref[...] += jnp.sum(dpa, axis=0, keepdims=True)
        dbi_ref[...] += jnp.sum(dpi, axis=0, keepdims=True)
        dwa_ref[...] += lax.dot_general(ucb, dpab, tn, preferred_element_type=F32)
        dwi_ref[...] += lax.dot_general(ucb, dpib, tn, preferred_element_type=F32)
        duc = duc + lax.dot_general(dpab, wa, nt, preferred_element_type=F32)
        duc = duc + lax.dot_general(dpib, wi, nt, preferred_element_type=F32)
        dcb_ref[...] += jnp.sum(duc, axis=0, keepdims=True)
        dcw_ref[...] += jnp.concatenate([jnp.sum(duc * taps[k], axis=0, keepdims=True) for k in range(CONV_WIDTH)], axis=0)
        after = duc_sc[...]
        du = cw[CONV_WIDTH - 1:CONV_WIDTH, :] * duc
        for k in range(CONV_WIDTH - 1):
            du = du + cw[k:k + 1, :] * _shift_up(duc, after, CONV_WIDTH - 1 - k)
        duc_sc[...] = duc[0:8, :]
        dz_ref[:, 0:w] = du.astype(BF16)
        dz_ref[:, w:2 * w] = dgate.astype(BF16)

    small = lambda arr: pl.BlockSpec(arr.shape, lambda b, c: (0, 0))
    acc = lambda shape: pl.BlockSpec(shape, lambda b, c: (0, 0))
    chunk_spec = lambda cb_: pl.BlockSpec((LRU_CHUNK, w), lambda b, c: (chunk(b, c), cb_))
    halo_spec = pl.BlockSpec((8, w), lambda b, c: (halo_before(b, c), 0))
    acc_shapes = [(CONV_WIDTH, w), (1, w), (w, w), (1, w), (w, w), (1, w), (1, w)]
    return pl.pallas_call(
        body, grid=(total // SEQ, nchunk),
        in_specs=[chunk_spec(0), halo_spec, chunk_spec(1), chunk_spec(0), halo_spec, chunk_spec(0),
                  small(cw), small(cb), small(wa), small(ba), small(wi), small(bi), small(lam)],
        out_specs=[pl.BlockSpec((LRU_CHUNK, 2 * w), lambda b, c: (chunk(b, c), 0))] + [acc(s) for s in acc_shapes],
        out_shape=[jax.ShapeDtypeStruct((total, 2 * w), BF16)] + [jax.ShapeDtypeStruct(s, F32) for s in acc_shapes],
        scratch_shapes=[pltpu.VMEM((LRU_CHUNK, w), F32)] * 3 + [pltpu.VMEM((8, w), F32)] * 3,
        name=name, compiler_params=_params("arbitrary", "arbitrary"),
    )(zug, zug, zug, hl, hl, dy, cw, cb, wa, ba, wi, bi, lam)


def _block_diag(wb):
    eye = jnp.eye(LRU_BLOCKS, dtype=wb.dtype)
    return jnp.einsum("gcd,gh->gchd", wb, eye).reshape(LRU_WIDTH, LRU_WIDTH).astype(BF16)


def _diag_blocks(wd):
    blk = LRU_WIDTH // LRU_BLOCKS
    return jnp.stack([wd[g * blk:(g + 1) * blk, g * blk:(g + 1) * blk] for g in range(LRU_BLOCKS)])


FOX_SCAN = 256


def _fox_bias(fl, bf, *, name):
    nb = fl.shape[0]

    def body(fl_ref, bf_ref, c_ref):
        x = fl_ref[0] + bf_ref[...]
        logf = jnp.minimum(x, 0.0) - jnp.log(1.0 + jnp.exp(-jnp.abs(x)))
        upper = (lax.broadcasted_iota(jnp.int32, (FOX_SCAN, FOX_SCAN), 0)
                 <= lax.broadcasted_iota(jnp.int32, (FOX_SCAN, FOX_SCAN), 1)).astype(BF16)
        carry = jnp.zeros((LRU_BLOCKS, 1), F32)
        for j in range(SEQ // FOX_SCAN):
            blk = logf[:, j * FOX_SCAN:(j + 1) * FOX_SCAN]
            c_ref[0, :, j * FOX_SCAN:(j + 1) * FOX_SCAN] = _dot_exact(blk, upper) + carry
            carry = carry + jnp.sum(blk, axis=1, keepdims=True)

    return pl.pallas_call(
        body, grid=(nb,),
        in_specs=[pl.BlockSpec((1, 8, SEQ), lambda b: (b, 0, 0)), pl.BlockSpec((8, 1), lambda b: (0, 0))],
        out_specs=pl.BlockSpec((1, 8, SEQ), lambda b: (b, 0, 0)),
        out_shape=jax.ShapeDtypeStruct((nb, 8, SEQ), F32), name=name, compiler_params=_params("arbitrary"),
    )(fl, bf)


def _fox_bias_bwd(fl, bf, dc, *, name):
    nb = fl.shape[0]

    def body(fl_ref, bf_ref, dc_ref, dfl_ref, dbf_ref):
        @pl.when(pl.program_id(0) == 0)
        def _():
            dbf_ref[...] = jnp.zeros(dbf_ref.shape, F32)

        x = fl_ref[0] + bf_ref[...]
        dc_all = dc_ref[0]
        lower = (lax.broadcasted_iota(jnp.int32, (FOX_SCAN, FOX_SCAN), 0)
                 >= lax.broadcasted_iota(jnp.int32, (FOX_SCAN, FOX_SCAN), 1)).astype(BF16)
        carry = jnp.zeros((LRU_BLOCKS, 1), F32)
        total = jnp.zeros((LRU_BLOCKS, 1), F32)
        for j in reversed(range(SEQ // FOX_SCAN)):
            cols = slice(j * FOX_SCAN, (j + 1) * FOX_SCAN)
            blk = dc_all[:, cols]
            dlogf = _dot_exact(blk, lower) + carry
            carry = carry + jnp.sum(blk, axis=1, keepdims=True)
            dx = dlogf * jax.nn.sigmoid(-x[:, cols])
            dfl_ref[0, :, cols] = dx
            total = total + jnp.sum(dx, axis=1, keepdims=True)
        dbf_ref[...] += total

    return pl.pallas_call(
        body, grid=(nb,),
        in_specs=[pl.BlockSpec((1, 8, SEQ), lambda b: (b, 0, 0)), pl.BlockSpec((8, 1), lambda b: (0, 0)),
                  pl.BlockSpec((1, 8, SEQ), lambda b: (b, 0, 0))],
        out_specs=[pl.BlockSpec((1, 8, SEQ), lambda b: (b, 0, 0)), pl.BlockSpec((8, 1), lambda b: (0, 0))],
        out_shape=[jax.ShapeDtypeStruct((nb, 8, SEQ), F32), jax.ShapeDtypeStruct((8, 1), F32)],
        name=name, compiler_params=_params("arbitrary"),
    )(fl, bf, dc)


def _seq3(a, nb):
    return a.reshape(nb, a.shape[0] // nb, a.shape[1])


def _flat2(a):
    return a.reshape(a.shape[0] * a.shape[1], a.shape[2])


def _residual_out(y, w, h, next_gain, *, name):
    if next_gain is None:
        out, = _matmul(y, w, extras=(h,), epilogue=lambda acc, res: (acc + res,), name=name)
        return out, None

    def epilogue(acc, res, g):
        out = acc + res
        return out, out * _rstd(out) * g
    return _matmul(y, w, outs=(F32, BF16), extras=(h,), consts=(next_gain,), epilogue=epilogue, whole_rows=True, name=name)


def _mlp_fwd(h, hn, p, tag, next_gain):
    act, = _matmul(hn, p["w_up_t"], tb=True, outs=(BF16,), epilogue=lambda acc: (jnp.square(jnp.maximum(acc, 0.0)),),
                   name=f"{tag}_up")
    out, hn_next = _residual_out(act, p["w_down"], h, next_gain, name=f"{tag}_down")
    return out, hn_next, (h, hn, act)


def _mlp_bwd(dh, dhb, p, saved, tag):
    h, hn, act = saved
    dup, = _matmul(dhb, p["w_down"], tb=True, outs=(BF16,), extras=(act,),
                   epilogue=lambda acc, act: (acc * (2.0 * jnp.sqrt(act).astype(F32)),), name=f"{tag}_bwd_dup")
    dw_down, = _matmul(act, dhb, ta=True, outs=(BF16,),name=f"{tag}_bwd_dwdown")
    dw_up_t, = _matmul(dup, hn, ta=True, outs=(BF16,),name=f"{tag}_bwd_dwup")
    dh2, dh2b, dg = _norm_input_grad(dup, p["w_up_t"], h, dh, p["norm"], name=f"{tag}_bwd_dh")
    return dh2, dh2b, {"norm": dg, "w_up_t": dw_up_t, "w_down": dw_down}


def _xa_fwd(h, hn, mem, p, tag, nb, next_gain):
    mem_n = _rms_fwd(mem, p["mem_norm"], name=f"{tag}_memnorm")
    q, = _matmul(hn, p["w_q"], outs=(BF16,), name=f"{tag}_q")
    kv, = _matmul(mem_n, p["w_kv_t"], tb=True, outs=(BF16,), name=f"{tag}_kv")
    q3, kv3 = _seq3(q, nb), _seq3(kv, nb)
    o, lse, ob = _attn_fwd((q3, 0), (kv3, 0), (kv3, 1), heads=XA_HEADS, dh=XA_HEAD_DIM, mode="full", bf16_copy=True,
                           name=f"{tag}_attn")
    o, ob = _flat2(o), _flat2(ob)
    out, hn_next = _residual_out(ob, p["w_o"], h, next_gain, name=f"{tag}_o")
    return out, hn_next, (h, hn, mem_n, q3, kv3, o, ob, lse)


def _xa_bwd(dh, dhb, mem, p, saved, tag, nb):
    h, hn, mem_n, q3, kv3, o, ob, lse = saved
    dob, delta = _matmul(dhb, p["w_o"], tb=True, outs=(BF16, (F32, LANES)), extras=(o,), consts=(_head_expand(XA_HEADS, D_MODEL).T,),
                         epilogue=lambda acc, o, e: (acc, _dot_exact(acc * o, e)), name=f"{tag}_bwd_do")
    dw_o, = _matmul(ob, dhb, ta=True, outs=(BF16,),name=f"{tag}_bwd_dwo")
    dq, dk, dv = _attn_bwd((q3, 0), (kv3, 0), (kv3, 1), (_seq3(dob, nb), 0), lse, _seq3(delta, nb), heads=XA_HEADS,
                           dh=XA_HEAD_DIM, mode="full", dq_dtype=BF16, name=f"{tag}_bwd_attn")
    dqb = _flat2(dq)
    dkvb, = _rowwise(lambda a, b: jnp.concatenate([a, b], axis=1), [_flat2(dk), _flat2(dv)], [], [(2 * D_MODEL, BF16)],
                     name=f"{tag}_bwd_dkvcast")
    dw_q, = _matmul(hn, dqb, ta=True, outs=(BF16,),name=f"{tag}_bwd_dwq")
    dw_kv_t, = _matmul(dkvb, mem_n, ta=True, outs=(BF16,),name=f"{tag}_bwd_dwkv")
    dmem_n, = _matmul(dkvb, p["w_kv_t"], name=f"{tag}_bwd_dmemn")
    dg_mem, = _rowwise(lambda x, d, g: _rms_bwd_vals(x, d, g)[1], [mem, dmem_n], [p["mem_norm"]], [], [(1, D_MODEL)],
                       name=f"{tag}_bwd_memnorm")
    dh2, dh2b, dg = _norm_input_grad(dqb, p["w_q"], h, dh, p["norm"], tb=True, name=f"{tag}_bwd_dh")
    return dh2, dh2b, {"norm": dg, "mem_norm": dg_mem, "w_q": dw_q, "w_kv_t": dw_kv_t, "w_o": dw_o}


AB_MAIN = 2 * LRU_WIDTH + 3 * ATT_W


def _ab_fwd(h, hn, p, nb, next_gain, before_out=None):
    w_in_t = p["w_in_t"]
    zug, = _matmul(hn, w_in_t[:2 * LRU_WIDTH], tb=True, name="ab_in_ug")
    qkv, = _matmul(hn, w_in_t[2 * LRU_WIDTH:AB_MAIN], tb=True, outs=(BF16,), tn=768, name="ab_in_qkv")
    zf, = _matmul(hn, w_in_t[AB_MAIN:], tb=True, name="ab_in_f")
    fl = zf[:, :8].reshape(nb, SEQ, 8).transpose(0, 2, 1)
    cbias = _fox_bias(fl, p["b_f"], name="ab_fox_bias")
    kb = cbias.reshape(nb, 8, SEQ // ATT_CHUNK, ATT_CHUNK)
    y_a, hl = _lru_fwd(zug, p["conv_w"], p["conv_b"], p["w_a"], p["b_a"], p["w_i"], p["b_i"], p["lam"], name="ab_lru")
    qkv3 = _seq3(qkv, nb)
    o, lse = _attn_fwd((qkv3, 0), (qkv3, 1), (qkv3, 2), kb, heads=8, dh=HEAD_DIM, mode="causal", name="ab_fox")
    o = _flat2(o)
    y, = _rowwise(lambda a, b: jnp.concatenate([a, b.astype(BF16)], axis=1), [y_a, o], [], [(D_MODEL, BF16)], name="ab_ycat")
    if before_out is not None:
        before_out(y)
    out, hn_next = _residual_out(y, p["w_out"], h, next_gain, name="ab_out")
    return out, hn_next, (h, hn, zug, qkv3, fl, kb, hl, o, lse, y)


def _ab_bwd(dh, dhb, p, saved, nb):
    h, hn, zug, qkv3, fl, kb, hl, o, lse, y = saved
    dy, dyb, delta = _matmul(dhb, p["w_out"], tb=True, outs=(F32, BF16, (F32, LANES)), extras=(y,), consts=(_head_expand(8, ATT_W).T,),
                             epilogue=lambda acc, y, e: (acc, acc, _dot_exact(acc[:, ATT_W:] * y[:, ATT_W:].astype(F32), e)),
                             name="ab_bwd_dy")
    dw_out, = _matmul(y, dhb, ta=True, outs=(BF16,),name="ab_bwd_dwout")
    dzug, dcw, dcb, dwa, dba, dwi, dbi, dlam = _lru_bwd(zug, hl, dy, p["conv_w"], p["conv_b"], p["w_a"], p["b_a"],
                                                        p["w_i"], p["b_i"], p["lam"], name="ab_bwd_lru")
    dq, dk, dv, dkb, drow = _attn_bwd((qkv3, 0), (qkv3, 1), (qkv3, 2), (_seq3(dyb, nb), 1), lse, _seq3(delta, nb), kb,
                                      heads=8, dh=HEAD_DIM, mode="causal", name="ab_bwd_fox")
    dcum = dkb.reshape(nb, 8, SEQ) + drow[:, :, :8].transpose(0, 2, 1)
    dfl, dbf = _fox_bias_bwd(fl, p["b_f"], dcum, name="ab_bwd_fox_bias")
    dzf = jnp.pad(dfl.transpose(0, 2, 1).reshape(nb * SEQ, 8), ((0, 0), (0, LANES - 8)))
    dz, = _rowwise(lambda a, q, k, v, f: jnp.concatenate([a, q.astype(BF16), k.astype(BF16), v.astype(BF16), f.astype(BF16)], axis=1),
                   [dzug, _flat2(dq), _flat2(dk), _flat2(dv), dzf], [], [(AB_MAIN + LANES, BF16)], name="ab_bwd_dzcat")
    dw_in_t, = _matmul(dz, hn, ta=True, outs=(BF16,),tm=384, name="ab_bwd_dwin")
    dh2, dh2b, dg = _norm_input_grad(dz, p["w_in_t"], h, dh, p["norm"], name="ab_bwd_dh")
    grads = {"norm": dg, "w_in_t": dw_in_t[:AB_MAIN + 8], "conv_w": dcw, "conv_b": dcb, "w_a": _diag_blocks(dwa), "b_a": dba,
             "w_i": _diag_blocks(dwi), "b_i": dbi, "lam": dlam, "b_f": dbf.reshape(1, 8), "w_out": dw_out}
    return dh2, dh2b, grads


CD_IN = 3 * ATT_W + ATT_W + 2 * SWA_KW


def _gqa_share():
    src = jnp.arange(SWA_KW)[:, None]
    dst = jnp.arange(ATT_W)[None, :]
    per_kv = ATT_W // (SWA_KW // HEAD_DIM)
    return ((dst // per_kv == src // HEAD_DIM) & (dst % HEAD_DIM == src % HEAD_DIM)).astype(BF16)


def _cd_fwd(h, hn, p, nb, next_gain):
    z, = _matmul(hn, p["w_in_t"], tb=True, tn=1152, name="cd_in")
    cos, sin = _rope_tables()
    per_seq = SEQ // ROW_TILE
    table_map = lambda i: (i % per_seq, 0)

    def rope_fn(z, cos, sin, share):
        qc, kc, vc = z[:, 0:ATT_W], z[:, ATT_W:2 * ATT_W], z[:, 2 * ATT_W:3 * ATT_W]
        qd, kd, vd = z[:, 3 * ATT_W:4 * ATT_W], z[:, 4 * ATT_W:4 * ATT_W + SWA_KW], z[:, 4 * ATT_W + SWA_KW:]
        kd8 = jnp.dot(_rotate(kd, cos, sin).astype(BF16), share, preferred_element_type=F32)
        vd8 = jnp.dot(vd.astype(BF16), share, preferred_element_type=F32)
        qk = jnp.concatenate([_rotate(qc, cos, sin), _rotate(kc, cos, sin)], axis=1)
        return qk, vc, _rotate(qd, cos, sin), kd8, vd8, qk, qk, vc, vc
    dils = [dil for _, dil in DIL_PATTERN if dil > 1]
    outs = _rowwise(rope_fn, [z, cos, sin], [_gqa_share()],
                    [(2 * ATT_W, BF16)] + [(ATT_W, BF16)] * 4 + [(2 * ATT_W, BF16, d) for d in dils] + [(ATT_W, BF16, d) for d in dils],
                    name="cd_rope", row_maps=[None, table_map, table_map])
    qk, vc, qd, kd, vd = outs[:5]
    views = {1: (_seq3(qk, nb), _seq3(vc, nb))}
    for d, qk_d, vc_d in zip(dils, outs[5:5 + len(dils)], outs[5 + len(dils):]):
        views[d] = (_seq3(qk_d, nb), _seq3(vc_d, nb))
    in_classes = lambda a, width, d: _flat2(a) if d == 1 else ("classes", _flat2(a), width, d)
    branch = []
    for window, dil in DIL_PATTERN:
        qk_v, vc_v = views[dil]
        o_i, lse_i = _attn_fwd((qk_v, 0), (qk_v, 1), (vc_v, 0), classes=dil, heads=8, dh=HEAD_DIM, mode="band",
                               max_dist=window // dil, name=f"cd_dil{dil}")
        branch.append((in_classes(o_i, ATT_W, dil), in_classes(lse_i, LANES, dil)))
    expand = _head_expand(8, ATT_W)

    qd3, kd3, vd3 = _seq3(qd, nb), _seq3(kd, nb), _seq3(vd, nb)
    o_d, lse_band = _attn_fwd((qd3, 0), (kd3, 0), (vd3, 0), heads=8, dh=HEAD_DIM, mode="band", max_dist=SWA_WINDOW - 1,
                              name="cd_swa")

    def mix(o1, o2, o3, l1, l2, l3, o_band, l_band, e, sink):
        m = jnp.maximum(jnp.maximum(l1, l2), l3)
        lt = m + jnp.log(jnp.exp(l1 - m) + jnp.exp(l2 - m) + jnp.exp(l3 - m))
        y_c = sum(_dot_exact(jnp.exp(l - lt), e) * o_ for o_, l in ((o1, l1), (o2, l2), (o3, l3)))
        lt_d = jnp.maximum(l_band, sink) + jnp.log(1.0 + jnp.exp(-jnp.abs(l_band - sink)))
        y_d = o_band * _dot_exact(jnp.exp(l_band - lt_d), e)
        return jnp.concatenate([y_c, y_d], axis=1), y_c, lt, y_d, lt_d
    y, y_c, lse_c, y_d, lse_d = _rowwise(
        mix, [b[0] for b in branch] + [b[1] for b in branch] + [_flat2(o_d), _flat2(lse_band)], [expand, p["sink"]],
        [(D_MODEL, BF16), (ATT_W, F32), (LANES, F32), (ATT_W, F32), (LANES, F32)], name="cd_mix")
    out, hn_next = _residual_out(y, p["w_out"], h, next_gain, name="cd_out")
    return out, hn_next, (h, hn, views, qd3, kd3, vd3, y_c, lse_c, y_d, lse_d, y)


def _cd_bwd(dh, dhb, p, saved, nb):
    h, hn, views, qd3, kd3, vd3, y_c, lse_c, y_d, lse_d, y = saved
    dy, dyb = _matmul(dhb, p["w_out"], tb=True, outs=(F32, BF16), epilogue=lambda acc: (acc, acc), name="cd_bwd_dy")
    dw_out, = _matmul(y, dhb, ta=True, outs=(BF16,),name="cd_bwd_dwout")
    dyb3 = _seq3(dyb, nb)
    dils = [dil for _, dil in DIL_PATTERN if dil > 1]
    gather = _head_expand(8, ATT_W).T

    def prepare(do, o, lse, do_d, o_d, lse_d, e, sink):
        delta = _dot_exact(do * o, e)
        delta_d = _dot_exact(do_d * o_d, e)
        dsink = -jnp.sum(jnp.exp(sink - lse_d) * delta_d, axis=0, keepdims=True)
        return (delta,) + (do,) * len(dils) + (lse,) * len(dils) + (delta,) * len(dils) + (delta_d, dsink)
    outs = _rowwise(prepare, [(dy, ATT_W, 0), y_c, lse_c, (dy, ATT_W, 1), y_d, lse_d], [gather, p["sink"]],
                    [(LANES, F32)] + [(ATT_W, BF16, d) for d in dils] + [(LANES, F32, d) for d in dils] * 2 + [(LANES, F32)],
                    [(1, LANES)], name="cd_bwd_delta")
    delta_d, dsink = outs[-2:]
    seen = {1: ((dyb3, 0), _seq3(lse_c, nb), _seq3(outs[0], nb))}
    for j, d in enumerate(dils):
        seen[d] = ((_seq3(outs[1 + j], nb), 0), _seq3(outs[1 + len(dils) + j], nb), _seq3(outs[1 + 2 * len(dils) + j], nb))
    in_classes = lambda a, d: _flat2(a) if d == 1 else ("classes", _flat2(a), ATT_W, d)
    parts = []
    for window, dil in DIL_PATTERN:
        (qk_v, vc_v), (do_v, lse_v, delta_v) = views[dil], seen[dil]
        grads = _attn_bwd((qk_v, 0), (qk_v, 1), (vc_v, 0), do_v, lse_v, delta_v, classes=dil, heads=8, dh=HEAD_DIM, mode="band",
                          max_dist=window // dil, dq_dtype=BF16, name=f"cd_bwd_dil{dil}")
        parts.append([in_classes(a, dil) for a in grads])
    dqd, dkd, dvd = _attn_bwd((qd3, 0), (kd3, 0), (vd3, 0), (dyb3, 1), _seq3(lse_d, nb), _seq3(delta_d, nb), heads=8,
                              dh=HEAD_DIM, mode="band", max_dist=SWA_WINDOW - 1, dq_dtype=BF16, name="cd_bwd_swa")
    cos, sin = _rope_tables()
    per_seq = SEQ // ROW_TILE
    table_map = lambda i: (i % per_seq, 0)

    def unrope(q1, q2, q3, k1, k2, k3, v1, v2, v3, qd, kd8, vd8, cos, sin, gather):
        back = lambda x: _rotate(x.astype(F32), cos, -sin)
        kd, vd = _dot_exact(kd8, gather), _dot_exact(vd8, gather)
        return jnp.concatenate([back(q1 + q2 + q3), back(k1 + k2 + k3), v1 + v2 + v3, back(qd), back(kd), vd], axis=1)
    ins = [parts[b][t] for t in range(3) for b in range(3)] + [_flat2(dqd), _flat2(dkd), _flat2(dvd), cos, sin]
    dz, = _rowwise(unrope, ins, [_gqa_share().T], [(CD_IN, BF16)], name="cd_bwd_unrope",
                   row_maps=[None] * 12 + [table_map, table_map])
    dw_in_t, = _matmul(dz, hn, ta=True, outs=(BF16,),tm=384, name="cd_bwd_dwin")
    dh2, dh2b, dg = _norm_input_grad(dz, p["w_in_t"], h, dh, p["norm"], name="cd_bwd_dh")
    return dh2, dh2b, {"norm": dg, "w_in_t": dw_in_t, "sink": dsink[:, :8], "w_out": dw_out}


def _local_step(x, mem, target, w, complete=None, grads_ready=None):
    nb = x.shape[0]
    h = x.reshape(nb * SEQ, D_MODEL)
    mem2 = mem.reshape(nb * MEM_LEN, D_MODEL)
    tgt = target.reshape(nb * SEQ, D_MODEL)

    hn = _rms_fwd(h, w["ab"]["norm"], name="ab_norm")
    h, hn, s_ab = _ab_fwd(h, hn, w["ab"], nb, w["xa0"]["norm"], before_out=complete)
    h, hn, s_xa0 = _xa_fwd(h, hn, mem2, w["xa0"], "xa0", nb, w["mlp0"]["norm"])
    h, hn, s_mlp0 = _mlp_fwd(h, hn, w["mlp0"], "mlp0", w["cd"]["norm"])
    h, hn, s_cd = _cd_fwd(h, hn, w["cd"], nb, w["xa1"]["norm"])
    h, hn, s_xa1 = _xa_fwd(h, hn, mem2, w["xa1"], "xa1", nb, w["mlp1"]["norm"])
    h, _, s_mlp1 = _mlp_fwd(h, hn, w["mlp1"], "mlp1", None)

    dh, dhb, loss, dg_final = _loss_and_grad(h, tgt, w["final_norm"], name="loss")
    grads = {"final_norm": dg_final}
    dh, dhb, grads["mlp1"] = _mlp_bwd(dh, dhb, w["mlp1"], s_mlp1, "mlp1")
    dh, dhb, grads["xa1"] = _xa_bwd(dh, dhb, mem2, w["xa1"], s_xa1, "xa1", nb)
    dh, dhb, grads["cd"] = _cd_bwd(dh, dhb, w["cd"], s_cd, nb)
    if grads_ready is not None:
        grads_ready(0, grads)
    dh, dhb, grads["mlp0"] = _mlp_bwd(dh, dhb, w["mlp0"], s_mlp0, "mlp0")
    dh, dhb, grads["xa0"] = _xa_bwd(dh, dhb, mem2, w["xa0"], s_xa0, "xa0", nb)
    if grads_ready is not None:
        grads_ready(1, grads)
    dh, dhb, grads["ab"] = _ab_bwd(dh, dhb, w["ab"], s_ab, nb)
    return loss, dh.reshape(nb, SEQ, D_MODEL), grads


ANY = pl.BlockSpec(memory_space=pl.ANY)


def _place():
    x, y, c = lax.axis_index("x"), lax.axis_index("y"), lax.axis_index("c")
    return x, y, c, [(1 - x, y), (x, 1 - y), (1 - x, 1 - y)]


def _gather_chips(shard):
    half = shard.shape[0] // 2

    def body(src, out, send_sems, recv_sems):
        x, y, c, chips = _place()
        sibling = (x, y, 1 - c)

        def rows(slot, core):
            return out.at[slot, pl.ds(core * half, half)]

        def copy(k, src_ref, dst_ref, to):
            return pltpu.make_async_remote_copy(src_ref=src_ref, dst_ref=dst_ref, send_sem=send_sems.at[k],
                                                recv_sem=recv_sems.at[k], device_id=to, device_id_type=MESH)

        first = [copy(k, src.at[pl.ds(c * half, half)], rows(2 * x + y, c), (px, py, c)) for k, (px, py) in enumerate(chips)]
        for cp in first:
            cp.start()
        passed = [copy(3 + k, rows(2 * px + py, c), rows(2 * px + py, c), sibling) for k, (px, py) in enumerate(chips)]
        for k, (px, py) in enumerate(chips):
            copy(k, src.at[pl.ds(c * half, half)], rows(2 * px + py, c), (px, py, c)).wait_recv()
            passed[k].start()
        for k, (px, py) in enumerate(chips):
            copy(3 + k, rows(2 * px + py, 1 - c), rows(2 * px + py, 1 - c), sibling).wait_recv()
        for cp in first + passed:
            cp.wait_send()

    return pl.pallas_call(
        body, out_shape=jax.ShapeDtypeStruct((N_CHIPS,) + shard.shape, shard.dtype), in_specs=[ANY], out_specs=ANY,
        scratch_shapes=[pltpu.SemaphoreType.DMA((6,)), pltpu.SemaphoreType.DMA((6,))], name="gather_chips",
    )(shard)


HBM = pl.BlockSpec(memory_space=pltpu.HBM)
SEM = pl.BlockSpec(memory_space=pltpu.SEMAPHORE)
SIDE_EFFECT = pltpu.SideEffectType.DATAFLOW_SIDE_EFFECTING


def _chip_copies(src, land, send_sems, recv_sems):
    half = src.shape[0] // 2
    x, y, c, chips = _place()

    def copy(k, slot, to):
        return pltpu.make_async_remote_copy(src_ref=src.at[pl.ds(c * half, half)], dst_ref=land.at[slot, pl.ds(c * half, half)],
                                            send_sem=send_sems[k], recv_sem=recv_sems[k], device_id=to, device_id_type=MESH)
    out = [copy(k, 2 * x + y, (px, py, c)) for k, (px, py) in enumerate(chips)]
    back = [copy(k, 2 * px + py, (px, py, c)) for k, (px, py) in enumerate(chips)]
    return out, back


def _gather_start(shard, after):
    def body(src, land, *rest):
        rest = rest[len(after):]
        sems, token = rest[:6], rest[8]
        out, _ = _chip_copies(src, land, sems[:3], sems[3:])
        for cp in out:
            cp.start()
        token[...] = jnp.zeros(token.shape, F32)

    sem = pltpu.SemaphoreType.DMA(())
    land_shape = (N_CHIPS,) + shard.shape
    return pl.pallas_call(
        body, name="gather_start",
        out_shape=(sem,) * 6 + (pltpu.HBM(shard.shape, shard.dtype), pltpu.HBM(land_shape, shard.dtype),
                                jax.ShapeDtypeStruct((8, LANES), F32)),
        in_specs=(HBM, HBM) + (pl.BlockSpec(memory_space=pl.ANY),) * len(after),
        out_specs=(SEM,) * 6 + (HBM, HBM, pl.BlockSpec(memory_space=pltpu.VMEM)),
        input_output_aliases={0: 6, 1: 7}, compiler_params=pltpu.CompilerParams(has_side_effects=SIDE_EFFECT),
    )(pltpu.with_memory_space_constraint(shard, pltpu.HBM),
      pltpu.with_memory_space_constraint(lax.empty(land_shape, shard.dtype), pltpu.HBM), *after)


def _gather_wait(started, after):
    sems, src_thru, land_thru = started[:6], started[6], started[7]

    def body(src, land, *rest):
        out, back = _chip_copies(src, land, rest[:3], rest[3:6])
        for cp in out:
            cp.wait_send()
        for cp in back:
            cp.wait_recv()

    return pl.pallas_call(
        body, name="gather_wait",
        out_shape=(pltpu.HBM(src_thru.shape, src_thru.dtype), pltpu.HBM(land_thru.shape, land_thru.dtype)),
        in_specs=(HBM, HBM) + (SEM,) * 6 + (pl.BlockSpec(memory_space=pl.ANY),), out_specs=(HBM, HBM),
        input_output_aliases={0: 0, 1: 1}, compiler_params=pltpu.CompilerParams(has_side_effects=SIDE_EFFECT),
    )(src_thru, land_thru, *sems, after)


def _gather_pass(land):
    half = land.shape[1] // 2

    def body(land_in, land_out, send_sems, recv_sems):
        x, y, c, chips = _place()

        def copy(k, slot, core):
            rows = land_out.at[slot, pl.ds(core * half, half)]
            return pltpu.make_async_remote_copy(src_ref=rows, dst_ref=rows, send_sem=send_sems.at[k], recv_sem=recv_sems.at[k],
                                                device_id=(x, y, 1 - c), device_id_type=MESH)
        sends = [copy(k, 2 * px + py, c) for k, (px, py) in enumerate(chips)]
        for cp in sends:
            cp.start()
        for k, (px, py) in enumerate(chips):
            copy(k, 2 * px + py, 1 - c).wait_recv()
        for cp in sends:
            cp.wait_send()

    return pl.pallas_call(
        body, out_shape=jax.ShapeDtypeStruct(land.shape, land.dtype), in_specs=[ANY], out_specs=ANY,
        scratch_shapes=[pltpu.SemaphoreType.DMA((3,)), pltpu.SemaphoreType.DMA((3,))], input_output_aliases={0: 0},
        name="gather_pass",
    )(land)


def _swap_cores(block, *, name, half_rows=None):
    shape = block.shape if half_rows is None else (block.shape[0], half_rows, block.shape[2])

    def body(src, out, send_sem, recv_sem):
        x, y, c, _ = _place()
        part = src if half_rows is None else src.at[:, pl.ds((1 - c) * half_rows, half_rows)]
        cp = pltpu.make_async_remote_copy(src_ref=part, dst_ref=out, send_sem=send_sem, recv_sem=recv_sem,
                                          device_id=(x, y, 1 - c), device_id_type=MESH)
        cp.start()
        cp.wait()

    return pl.pallas_call(
        body, out_shape=jax.ShapeDtypeStruct(shape, block.dtype), in_specs=[ANY], out_specs=ANY,
        scratch_shapes=[pltpu.SemaphoreType.DMA(()), pltpu.SemaphoreType.DMA(())], name=name,
    )(block)


def _scatter_copies(parts, land, send_sems, recv_sems):
    x, y, c, chips = _place()
    return [pltpu.make_async_remote_copy(src_ref=parts.at[2 * px + py], dst_ref=land.at[k], send_sem=send_sems[k],
                                         recv_sem=recv_sems[k], device_id=(px, py, c), device_id_type=MESH)
            for k, (px, py) in enumerate(chips)]


def _scatter_start(parts, tag):
    def body(src, land, *rest):
        for cp in _scatter_copies(src, land, rest[:3], rest[3:6]):
            cp.start()
        rest[8][...] = jnp.zeros(rest[8].shape, F32)

    sem = pltpu.SemaphoreType.DMA(())
    land_shape = (3,) + parts.shape[1:]
    return pl.pallas_call(
        body, name=f"scatter_start_{tag}",
        out_shape=(sem,) * 6 + (pltpu.HBM(parts.shape, parts.dtype), pltpu.HBM(land_shape, parts.dtype),
                                jax.ShapeDtypeStruct((8, LANES), F32)),
        in_specs=(HBM, HBM), out_specs=(SEM,) * 6 + (HBM, HBM, pl.BlockSpec(memory_space=pltpu.VMEM)),
        input_output_aliases={0: 6, 1: 7}, compiler_params=pltpu.CompilerParams(has_side_effects=SIDE_EFFECT),
    )(pltpu.with_memory_space_constraint(parts, pltpu.HBM),
      pltpu.with_memory_space_constraint(lax.empty(land_shape, parts.dtype), pltpu.HBM))


def _scatter_wait(started, after, tag):
    def body(src, land, *rest):
        for cp in _scatter_copies(src, land, rest[:3], rest[3:6]):
            cp.wait_send()
            cp.wait_recv()

    src_thru, land_thru = started[6], started[7]
    return pl.pallas_call(
        body, name=f"scatter_wait_{tag}",
        out_shape=(pltpu.HBM(src_thru.shape, src_thru.dtype), pltpu.HBM(land_thru.shape, land_thru.dtype)),
        in_specs=(HBM, HBM) + (SEM,) * 6 + (pl.BlockSpec(memory_space=pl.ANY),), out_specs=(HBM, HBM),
        input_output_aliases={0: 0, 1: 1}, compiler_params=pltpu.CompilerParams(has_side_effects=SIDE_EFFECT),
    )(src_thru, land_thru, *started[:6], after)[1]


def _sum_all_devices(vec):
    rows = vec.shape[0]

    def body(x_ref, total_ref, all_ref, send_sems, recv_sems, local_sem):
        x, y, c, chips = _place()
        me, sibling = (x, y, c), (x, y, 1 - c)

        def slot(px, py, pc):
            return all_ref.at[4 * px + 2 * py + pc]

        def copy(k, block, to, src=None):
            return pltpu.make_async_remote_copy(src_ref=slot(*block) if src is None else src, dst_ref=slot(*block),
                                                send_sem=send_sems.at[k], recv_sem=recv_sems.at[k], device_id=to,
                                                device_id_type=MESH)

        mine = pltpu.make_async_copy(x_ref, slot(*me), local_sem)
        mine.start()
        first = [copy(0, me, sibling, src=x_ref)] + [copy(1 + j, me, (*chip, c), src=x_ref) for j, chip in enumerate(chips)]
        for cp in first:
            cp.start()
        passed = [copy(4 + j, (*chip, c), sibling) for j, chip in enumerate(chips)]
        for j, chip in enumerate(chips):
            copy(1 + j, (*chip, c), me).wait_recv()
            passed[j].start()
        copy(0, sibling, me).wait_recv()
        for j, chip in enumerate(chips):
            copy(4 + j, (*chip, 1 - c), me).wait_recv()
        for cp in first + passed:
            cp.wait_send()
        mine.wait()
        acc = all_ref[0]
        for d in range(1, 8):
            acc = acc + all_ref[d]
        total_ref[...] = acc

    vmem = pl.BlockSpec(memory_space=pltpu.VMEM)
    return pl.pallas_call(
        body, out_shape=[jax.ShapeDtypeStruct((rows, LANES), F32), jax.ShapeDtypeStruct((8, rows, LANES), F32)],
        in_specs=[vmem], out_specs=[vmem, vmem],
        scratch_shapes=[pltpu.SemaphoreType.DMA((7,)), pltpu.SemaphoreType.DMA((7,)), pltpu.SemaphoreType.DMA(())],
        name="sum_all_devices", compiler_params=pltpu.CompilerParams(vmem_limit_bytes=VMEM_LIMIT_BYTES),
    )(vec)[0]


PACK_COLS = 1024
BIG = (("mlp_w_up", 2, 1024, True), ("mlp_w_down", 2, 1024, False), ("xa_w_kv", 2, 512, True), ("xa_w_q", 2, 256, False),
       ("xa_w_o", 2, 256, False), ("ab_w_out", 1, 256, False), ("cd_w_out", 1, 256, False), ("cd_w_in", 1, 576, True),
       ("ab_w_in", 1, 642, True))
ENTRIES = tuple((name, layer, rows, transposed) for name, layers, rows, transposed in BIG for layer in range(layers))
FIRST_ENTRIES = tuple(e for e in ENTRIES if e[0] == "ab_w_in")
LATER_ENTRIES = tuple(e for e in ENTRIES if e[0] != "ab_w_in")


def _tile_rows(entry):
    return -(-entry[2] // 16) * 16


def _padded_rows(entries):
    return -(-sum(_tile_rows(e) for e in entries) // 32) * 32


SMALL = (("ab_norm", (1, 1024)), ("ab_conv_w", (1, 4, 512)), ("ab_conv_b", (1, 512)), ("lru_w_a", (1, 8, 64, 64)),
         ("lru_b_a", (1, 512)), ("lru_w_i", (1, 8, 64, 64)), ("lru_b_i", (1, 512)), ("lru_lambda", (1, 512)),
         ("fox_b_f", (1, 8)), ("cd_norm", (1, 1024)), ("cd_sink", (1, 8)), ("xa_norm", (2, 1024)),
         ("xa_mem_norm", (2, 1024)), ("mlp_norm", (2, 1024)), ("final_norm", (1024,)), ("loss", (1,)))
SPLIT_SMALL = ("ab_conv_w", "cd_norm")


def _pack_rows(parts, entries, dtype):
    lead = parts[0].shape[:-2]
    zeros = lambda rows: jnp.zeros(lead + (rows, PACK_COLS), dtype)
    pieces = [jnp.pad(p.astype(dtype), ((0, 0),) * len(lead) + ((0, _tile_rows(e) - e[2]), (0, 0))) for p, e in zip(parts, entries)]
    tail = _padded_rows(entries) - sum(_tile_rows(e) for e in entries)
    return jnp.concatenate(pieces + ([zeros(tail)] if tail else []), axis=len(lead))


def _unpack_rows(packed, entries):
    out, r0 = [], 0
    for e in entries:
        out.append(packed[..., r0:r0 + e[2], :])
        r0 += _tile_rows(e)
    return out


def _shard_rows(w, entries):
    return [(w[name][layer].T if transposed else w[name][layer]) for name, layer, _, transposed in entries]


def _shard_blocks(rows):
    out = {}
    for (name, layer, _, transposed), r in zip(ENTRIES, rows):
        out.setdefault(name, []).append(r.T if transposed else r)
    return {name: jnp.stack(layers) for name, layers in out.items()}


def _pack_small(vals):
    flat = jnp.concatenate([vals[name].astype(F32).reshape(-1) for name, _ in SMALL])
    size = -(-flat.shape[0] // (8 * LANES)) * (8 * LANES)
    return jnp.pad(flat, (0, size - flat.shape[0])).reshape(-1, LANES)


def _unpack_small(packed):
    flat, out, at = packed.reshape(-1), {}, 0
    for name, shape in SMALL:
        out[name] = flat[at:at + math.prod(shape)].reshape(shape)
        at += math.prod(shape)
    return out


def _chip_part(full, chip):
    width = full.shape[-1] // N_CHIPS
    return lax.dynamic_slice_in_dim(full, chip * width, width, axis=full.ndim - 1)


def _chip_embed(part, chip):
    full = jnp.zeros(part.shape[:-1] + (part.shape[-1] * N_CHIPS,), part.dtype)
    return lax.dynamic_update_slice_in_dim(full, part, chip * part.shape[-1], axis=part.ndim - 1)


SUBLAYER_KEYS = {"ab_w_in": ("ab", "w_in_t"), "ab_w_out": ("ab", "w_out"), "cd_w_in": ("cd", "w_in_t"), "cd_w_out": ("cd", "w_out"),
                 "xa_w_q": ("xa", "w_q"), "xa_w_kv": ("xa", "w_kv_t"), "xa_w_o": ("xa", "w_o"), "mlp_w_up": ("mlp", "w_up_t"),
                 "mlp_w_down": ("mlp", "w_down")}


def _sublayer(name, layer):
    block, leaf = SUBLAYER_KEYS[name]
    return (block if block in ("ab", "cd") else f"{block}{layer}"), leaf


def _set_big(params, full_rows):
    for (name, layer), rows in full_rows.items():
        block, leaf = _sublayer(name, layer)
        if name == "ab_w_in":
            rows = jnp.concatenate([rows, jnp.zeros((LANES - 8, PACK_COLS), rows.dtype)])
        params[block][leaf] = rows


def _build_params(full_rows, w):
    pad = lambda a: jnp.pad(a, ((0, 0), (0, LANES - a.shape[1])))
    params = {
        "ab": dict(norm=w["ab_norm"], conv_w=w["ab_conv_w"][0], conv_b=w["ab_conv_b"], w_a=_block_diag(w["lru_w_a"][0]),
                   b_a=w["lru_b_a"], w_i=_block_diag(w["lru_w_i"][0]), b_i=w["lru_b_i"], lam=w["lru_lambda"],
                   b_f=w["fox_b_f"].reshape(8, 1)),
        "cd": dict(norm=w["cd_norm"], sink=pad(w["cd_sink"])),
        "final_norm": w["final_norm"].reshape(1, D_MODEL),
    }
    for layer in range(2):
        params[f"xa{layer}"] = dict(norm=w["xa_norm"][layer:layer + 1], mem_norm=w["xa_mem_norm"][layer:layer + 1])
        params[f"mlp{layer}"] = dict(norm=w["mlp_norm"][layer:layer + 1])
    _set_big(params, full_rows)
    return params


def _grad_rows(g, entries=ENTRIES):
    out = []
    for name, layer, _, _ in entries:
        block, leaf = _sublayer(name, layer)
        out.append(g[block][leaf])
    return out


def _reduce_group(entry):
    name, layer = entry[0], entry[1]
    if name.startswith("ab_"):
        return 2
    return 0 if layer == 1 or name.startswith("cd_") else 1


REDUCE_GROUPS = tuple(tuple(e for e in ENTRIES if _reduce_group(e) == group) for group in range(3))


def _reduce_tile(half):
    return max(t for t in range(16, 1025, 16) if half % t == 0)


def _reduce_start(grads_by_chip, core, chip, tag):
    half = grads_by_chip.shape[1] // 2
    tile = _reduce_tile(half)
    steps = half // tile
    place = jnp.stack([core, chip]).astype(jnp.int32)
    got = _swap_cores(grads_by_chip, name=f"swap_cores_{tag}", half_rows=half)
    block = (1, tile, PACK_COLS)

    def sum_cores(place_ref, mine_ref, got_ref, f32_ref, bf16_ref):
        total = mine_ref[...].astype(F32) + got_ref[...].astype(F32)
        f32_ref[...] = total
        bf16_ref[...] = total.astype(BF16)

    pair32, pair16 = pl.pallas_call(
        sum_cores, name=f"sum_cores_{tag}",
        grid_spec=pltpu.PrefetchScalarGridSpec(
            num_scalar_prefetch=1, grid=(N_CHIPS, steps),
            in_specs=[pl.BlockSpec(block, lambda s, i, place_ref: (s, place_ref[0] * steps + i, 0)),
                      pl.BlockSpec(block, lambda s, i, place_ref: (s, i, 0))],
            out_specs=[pl.BlockSpec(block, lambda s, i, place_ref: (s, i, 0))] * 2),
        out_shape=[jax.ShapeDtypeStruct((N_CHIPS, half, PACK_COLS), F32), jax.ShapeDtypeStruct((N_CHIPS, half, PACK_COLS), BF16)],
        compiler_params=_params("arbitrary", "arbitrary"),
    )(place, grads_by_chip, got)
    return pair32, _scatter_start(pair16, tag), place


def _reduce_finish(state, core, tag, after):
    pair32, started, place = state
    half = pair32.shape[1]
    tile = _reduce_tile(half)
    landed = _scatter_wait(started, after, tag)

    def sum_chips(place_ref, own_ref, landed_ref, out_ref):
        out_ref[...] = own_ref[0] + landed_ref[0].astype(F32) + landed_ref[1].astype(F32) + landed_ref[2].astype(F32)

    done = pl.pallas_call(
        sum_chips, name=f"sum_chips_{tag}",
        grid_spec=pltpu.PrefetchScalarGridSpec(
            num_scalar_prefetch=1, grid=(half // tile,),
            in_specs=[pl.BlockSpec((1, tile, PACK_COLS), lambda i, place_ref: (place_ref[1], i, 0)),
                      pl.BlockSpec((3, tile, PACK_COLS), lambda i, place_ref: (0, i, 0))],
            out_specs=pl.BlockSpec((tile, PACK_COLS), lambda i, place_ref: (i, 0))),
        out_shape=jax.ShapeDtypeStruct((half, PACK_COLS), F32), compiler_params=_params("arbitrary"),
    )(place, pair32, landed)
    theirs = _swap_cores(done, name=f"share_halves_{tag}")
    return jnp.where(core == 0, jnp.concatenate([done, theirs]), jnp.concatenate([theirs, done]))


def kernel(x, mem, ab_norm, ab_w_in, ab_conv_w, ab_conv_b, lru_w_a, lru_b_a, lru_w_i, lru_b_i, lru_lambda, fox_b_f, ab_w_out, cd_norm, cd_w_in, cd_sink, cd_w_out, xa_norm, xa_mem_norm, xa_w_q, xa_w_kv, xa_w_o, mlp_norm, mlp_w_up, mlp_w_down, final_norm, loss_target, m_ab_norm, m_ab_w_in, m_ab_conv_w, m_ab_conv_b, m_lru_w_a, m_lru_b_a, m_lru_w_i, m_lru_b_i, m_lru_lambda, m_fox_b_f, m_ab_w_out, m_cd_norm, m_cd_w_in, m_cd_sink, m_cd_w_out, m_xa_norm, m_xa_mem_norm, m_xa_w_q, m_xa_w_kv, m_xa_w_o, m_mlp_norm, m_mlp_w_up, m_mlp_w_down, m_final_norm, v_ab_norm, v_ab_w_in, v_ab_conv_w, v_ab_conv_b, v_lru_w_a, v_lru_b_a, v_lru_w_i, v_lru_b_i, v_lru_lambda, v_fox_b_f, v_ab_w_out, v_cd_norm, v_cd_w_in, v_cd_sink, v_cd_w_out, v_xa_norm, v_xa_mem_norm, v_xa_w_q, v_xa_w_kv, v_xa_w_o, v_mlp_norm, v_mlp_w_up, v_mlp_w_down, v_final_norm):
    given = dict(locals())
    weight_names = [name for name, _ in SMALL if name != "loss"] + [name for name, _, _, _ in BIG]
    w = {name: given[name] for name in weight_names}
    chip = 2 * lax.axis_index("x") + lax.axis_index("y")
    core = lax.axis_index("c")

    slot = lax.broadcasted_iota(jnp.int32, (N_CHIPS, 1, 1), 0)

    def whole(entries, mine, gathered):
        return {(name, layer): jnp.where(slot == chip, own[None], got).reshape(N_CHIPS * rows, PACK_COLS)
                for (name, layer, rows, _), own, got in zip(entries, _unpack_rows(mine, entries), _unpack_rows(gathered, entries))}

    mine_first = _pack_rows(_shard_rows(w, FIRST_ENTRIES), FIRST_ENTRIES, BF16)
    mine_later = _pack_rows(_shard_rows(w, LATER_ENTRIES), LATER_ENTRIES, BF16)
    first = _gather_chips(mine_first)
    owner = (core == 0).astype(F32)
    quarters = {name: _chip_embed(w[name], chip) * owner for name in SPLIT_SMALL}
    zeros = {name: jnp.zeros(shape, F32) for name, shape in SMALL}
    small_packed = _sum_all_devices(_pack_small({**zeros, **quarters}))
    small_full = _unpack_small(small_packed)
    started = _gather_start(mine_later, after=(small_packed, first))
    params = _build_params(whole(FIRST_ENTRIES, mine_first, first),
                           {**w, "ab_conv_w": small_full["ab_conv_w"], "cd_norm": small_full["cd_norm"]})
    params["ab"]["norm"] = params["ab"]["norm"] + started[8][0, 0]

    def complete(h):
        mine, landed = _gather_wait(started, after=h)
        _set_big(params, whole(LATER_ENTRIES, mine, _gather_pass(landed)))

    reducing = {}

    def grads_ready(group, g):
        entries = REDUCE_GROUPS[group]
        by_chip = _pack_rows([r.reshape(N_CHIPS, e[2], PACK_COLS) for r, e in zip(_grad_rows(g, entries), entries)], entries, BF16)
        reducing[group] = _reduce_start(by_chip, core, chip, f"g{group}")
        if group < 2:
            block, leaf = ("mlp0", "w_down") if group == 0 else ("ab", "w_out")
            params[block][leaf] = params[block][leaf] + reducing[group][1][8][0, 0].astype(BF16)

    loss_part, grad_x, g = _local_step(x, mem, loss_target, params, complete, grads_ready)
    grads_ready(2, g)
    reduced, after = {}, reducing[2][1][8]
    for group, entries in enumerate(REDUCE_GROUPS):
        after = _reduce_finish(reducing[group], core, f"g{group}", after=after)
        reduced.update({(e[0], e[1]): r for e, r in zip(entries, _unpack_rows(after, entries))})
    grads = _shard_blocks([reduced[e[0], e[1]] for e in ENTRIES])
    pair = lambda key, leaf: jnp.stack([g[f"{key}0"][leaf], g[f"{key}1"][leaf]])

    small_local = {"ab_norm": g["ab"]["norm"], "ab_conv_w": g["ab"]["conv_w"], "ab_conv_b": g["ab"]["conv_b"],
                   "lru_w_a": g["ab"]["w_a"], "lru_b_a": g["ab"]["b_a"], "lru_w_i": g["ab"]["w_i"], "lru_b_i": g["ab"]["b_i"],
                   "lru_lambda": g["ab"]["lam"], "fox_b_f": g["ab"]["b_f"], "cd_norm": g["cd"]["norm"], "cd_sink": g["cd"]["sink"],
                   "xa_norm": pair("xa", "norm"), "xa_mem_norm": pair("xa", "mem_norm"), "mlp_norm": pair("mlp", "norm"),
                   "final_norm": g["final_norm"], "loss": loss_part[0, :1]}
    small_sum_packed = _sum_all_devices(_pack_small(small_local))
    small_sum = _unpack_small(small_sum_packed)
    loss = small_sum["loss"][0]

    delta, new_m, new_v = {}, {}, {}
    for name, _, _, _ in BIG:
        shape = w[name].shape
        flat = lambda a: a.reshape(-1, shape[-1])
        d, m2, v2 = _adamw(flat(w[name]), flat(grads[name]), flat(given["m_" + name]), flat(given["v_" + name]), name=f"adamw_{name}")
        delta[name], new_m[name], new_v[name] = d.reshape(shape), m2.reshape(shape), v2.reshape(shape)

    def small_state(prefix):
        vals = {name: (given[prefix + name] if name != "loss" else jnp.zeros((1,), F32)) for name, _ in SMALL}
        for name in SPLIT_SMALL:
            vals[name] = _chip_embed(vals[name], chip)
        return _pack_small(vals)
    packed = _adamw(small_state(""), small_sum_packed, small_state("m_"), small_state("v_"), name="adamw_small")
    for store, vals in zip((delta, new_m, new_v), packed):
        for name, val in _unpack_small(vals).items():
            store[name] = _chip_part(val, chip) if name in SPLIT_SMALL else val
    for name in SPLIT_SMALL:
        small_sum[name] = _chip_part(small_sum[name], chip)
    grads.update({name: small_sum[name] for name, _ in SMALL})

    order = ["ab_norm", "ab_w_in", "ab_conv_w", "ab_conv_b", "lru_w_a", "lru_b_a", "lru_w_i", "lru_b_i", "lru_lambda", "fox_b_f",
             "ab_w_out", "cd_norm", "cd_w_in", "cd_sink", "cd_w_out", "xa_norm", "xa_mem_norm", "xa_w_q", "xa_w_kv", "xa_w_o",
             "mlp_norm", "mlp_w_up", "mlp_w_down", "final_norm"]
    return (loss, grad_x, *[grads[n] for n in order], *[delta[n] for n in order], *[new_m[n] for n in order],
            *[new_v[n] for n in order])
```

```python
import functools
import math

import jax
import jax.numpy as jnp
from jax import lax
from jax.experimental import pallas as pl
from jax.experimental.pallas import tpu as pltpu

F32 = jnp.float32
BF16 = jnp.bfloat16
MESH = pl.DeviceIdType.MESH

D_MODEL = 1024
SEQ = 2048
HEAD_DIM = 64
LRU_WIDTH = 512
LRU_BLOCKS = 8
LRU_C = 8.0
CONV_WIDTH = 4
ATT_W = 512
SWA_KW = 128
SWA_WINDOW = 128
DIL_PATTERN = ((128, 1), (512, 4), (2048, 16))
MEM_LEN = 256
XA_HEADS = 4
XA_HEAD_DIM = 256
D_FF = 4096
ROPE_THETA = 10000.0
EPS = 1e-6
N_CHIPS = 4
LANES = 128

ADAM_LR, ADAM_B1, ADAM_B2, ADAM_EPS, ADAM_WD, ADAM_STEP = 0.001, 0.9, 0.999, 1e-08, 0.01, 10

VMEM_LIMIT_BYTES = 56 * 1024 * 1024
MASKED = -1e30
ROW_TILE = 512
LRU_CHUNK = 512
ATT_BLOCK = 128
BAND_ROWS = 256
ATT_CHUNK = 512
CAUSAL_ROWS = 256
CAUSAL_ROWS_BACKWARD = 512
CLASS_ROWS = 512


def _params(*sem):
    return pltpu.CompilerParams(dimension_semantics=sem, vmem_limit_bytes=VMEM_LIMIT_BYTES)


def _rowwise(fn, rows, consts=(), out_rows=(), out_accs=(), *, name, tile=ROW_TILE, row_maps=None):
    rows = [r if isinstance(r, tuple) else (r, r.shape[1], 0) for r in rows]
    consts = list(consts)
    nr, nc, no = len(rows), len(consts), len(out_rows)
    dealt_in = [r[3] if isinstance(r[0], str) else None for r in rows]
    dealt_out = [o[2] if len(o) == 3 else None for o in out_rows]
    total = next(r[0].shape[0] for r in rows if not isinstance(r[0], str))
    tile = min(tile, total)
    assert total % tile == 0
    n = total // tile
    n_acc = len(out_accs)

    def body(*refs):
        scratch = list(refs[nr + nc + no + n_acc:])
        vals = []
        for ref, d, row in zip(refs[:nr], dealt_in, rows):
            if d is None:
                vals.append(ref[...])
                continue
            sc, width = scratch.pop(0), row[2]
            for r in range(d):
                for c in range(width // LANES):
                    lanes = slice(r * width + c * LANES, r * width + (c + 1) * LANES)
                    sc.at[c][pl.ds(r, tile // d, stride=d), :] = ref[:, lanes].astype(F32)
            vals.append(jnp.concatenate([sc[c] for c in range(width // LANES)], axis=1))
        outs = fn(*vals, *[r[...] for r in refs[nr:nr + nc]])
        outs = tuple(outs) if isinstance(outs, (tuple, list)) else (outs,)
        for ref, val, d, spec in zip(refs[nr + nc:nr + nc + no], outs[:no], dealt_out, out_rows):
            if d is None:
                ref[...] = val.astype(ref.dtype)
                continue
            sc, width = scratch.pop(0), spec[0]
            for c in range(width // LANES):
                sc[c] = val[:, c * LANES:(c + 1) * LANES].astype(F32)
            for r in range(d):
                for c in range(width // LANES):
                    lanes = slice(r * width + c * LANES, r * width + (c + 1) * LANES)
                    ref[:, lanes] = sc.at[c][pl.ds(r, tile // d, stride=d), :].astype(ref.dtype)
        for ref, val in zip(refs[nr + nc + no:nr + nc + no + n_acc], outs[no:]):
            @pl.when(pl.program_id(0) == 0)
            def _(ref=ref):
                ref[...] = jnp.zeros(ref.shape, ref.dtype)
            ref[...] += val

    in_specs, args, scratch_shapes = [], [], []
    for idx, row in enumerate(rows):
        if isinstance(row[0], str):
            _, arr, width, d = row
            in_specs.append(pl.BlockSpec((tile // d, d * width), lambda i: (i, 0)))
            scratch_shapes.append(pltpu.VMEM((width // LANES, tile, LANES), F32))
        elif row_maps is not None and row_maps[idx] is not None:
            arr, width, _ = row
            in_specs.append(pl.BlockSpec((tile, width), row_maps[idx]))
        else:
            arr, width, cb = row
            in_specs.append(pl.BlockSpec((tile, width), functools.partial(lambda i, cb: (i, cb), cb=cb)))
        args.append(arr)
    in_specs += [pl.BlockSpec(c.shape, lambda i: (0, 0)) for c in consts]
    out_shape, out_specs = [], []
    for spec, d in zip(out_rows, dealt_out):
        w, dt = spec[0], spec[1]
        if d is None:
            out_shape.append(jax.ShapeDtypeStruct((total, w), dt))
            out_specs.append(pl.BlockSpec((tile, w), lambda i: (i, 0)))
        else:
            out_shape.append(jax.ShapeDtypeStruct((total // d, d * w), dt))
            out_specs.append(pl.BlockSpec((tile // d, d * w), lambda i: (i, 0)))
            scratch_shapes.append(pltpu.VMEM((w // LANES, tile, LANES), F32))
    out_shape += [jax.ShapeDtypeStruct(s, F32) for s in out_accs]
    out_specs += [pl.BlockSpec(s, lambda i: (0, 0)) for s in out_accs]
    return pl.pallas_call(
        body, grid=(n,), in_specs=in_specs, out_specs=out_specs, out_shape=out_shape, scratch_shapes=scratch_shapes, name=name,
        compiler_params=_params("arbitrary"),
    )(*args, *consts)


MATMUL_VMEM_BYTES = 40 * 1024 * 1024


def _matmul_tiles(m, n, k, tm, tn, out_bytes):
    tm, tn, tk = min(tm, m), min(tn, n), k

    def need():
        return 2 * (2 * tk * (tm + tn) + tm * tn * out_bytes) + (0 if tk == k else 4 * tm * tn)

    while need() > MATMUL_VMEM_BYTES:
        if tm >= tn and tm % 256 == 0:
            tm //= 2
        elif tn % 256 == 0:
            tn //= 2
        else:
            tk //= 2
    return tm, tn, tk


def _matmul(a, b, *, ta=False, tb=False, outs=(F32,), epilogue=None, extras=(), consts=(), sums=(), whole_rows=False,
            tm=1024, tn=1024, name):
    m, k = (a.shape[1], a.shape[0]) if ta else a.shape
    n = b.shape[0] if tb else b.shape[1]
    assert (b.shape[1] if tb else b.shape[0]) == k
    outs = [o if isinstance(o, tuple) else (o, None) for o in outs]
    out_bytes = sum(jnp.dtype(dt).itemsize for dt, w in outs if w is None) + sum(e.dtype.itemsize for e in extras)
    tm, tn, tk = _matmul_tiles(m, n, k, tm, tn, out_bytes)
    assert m % tm == 0 and n % tn == 0 and k % tk == 0, (name, m, n, k)
    nk = k // tk
    ne, nc, no = len(extras), len(consts), len(outs)
    assert tn == n or not (sums or whole_rows or any(w is not None for _, w in outs)), name
    dims = (((0 if ta else 1,), (1 if tb else 0,)), ((), ()))

    def body(*refs):
        a_ref, b_ref = refs[:2]
        e_refs, o_refs = refs[2:2 + ne + nc], refs[2 + ne + nc:2 + ne + nc + no]
        s_refs = refs[2 + ne + nc + no:2 + ne + nc + no + len(sums)]

        def finish(acc):
            vals = (acc,) if epilogue is None else epilogue(acc, *[e[...] for e in e_refs])
            for ref, val in zip(o_refs, vals[:no]):
                ref[...] = val.astype(ref.dtype)
            for ref, val in zip(s_refs, vals[no:]):
                @pl.when(pl.program_id(0) == 0)
                def _(ref=ref, val=val):
                    ref[...] = val

                @pl.when(pl.program_id(0) > 0)
                def _(ref=ref, val=val):
                    ref[...] += val

        prod = lax.dot_general(a_ref[...], b_ref[...], dims, preferred_element_type=F32)
        if nk == 1:
            finish(prod)
        else:
            acc_ref = refs[-1]
            step = pl.program_id(2)

            @pl.when(step == 0)
            def _():
                acc_ref[...] = prod

            @pl.when(step > 0)
            def _():
                acc_ref[...] += prod

            @pl.when(step == nk - 1)
            def _():
                finish(acc_ref[...])

    a_spec = pl.BlockSpec((tk, tm), lambda i, j, s: (s, i)) if ta else pl.BlockSpec((tm, tk), lambda i, j, s: (i, s))
    b_spec = pl.BlockSpec((tn, tk), lambda i, j, s: (j, s)) if tb else pl.BlockSpec((tk, tn), lambda i, j, s: (s, j))
    tile_spec = pl.BlockSpec((tm, tn), lambda i, j, s: (i, j))
    whole = lambda shape: pl.BlockSpec(shape, lambda i, j, s: (0, 0))
    return pl.pallas_call(
        body, grid=(m // tm, n // tn, nk),
        in_specs=[a_spec, b_spec] + [tile_spec] * ne + [whole(c.shape) for c in consts],
        out_specs=[tile_spec if w is None else pl.BlockSpec((tm, w), lambda i, j, s: (i, 0)) for _, w in outs]
        + [whole(shape) for shape in sums],
        out_shape=[jax.ShapeDtypeStruct((m, n if w is None else w), dt) for dt, w in outs]
        + [jax.ShapeDtypeStruct(shape, F32) for shape in sums],
        scratch_shapes=[pltpu.VMEM((tm, tn), F32)] if nk > 1 else [],
        name=name, compiler_params=_params("arbitrary" if sums else "parallel", "parallel", "arbitrary"),
    )(a, b, *extras, *consts)


def _split3(x):
    x1 = x.astype(BF16)
    r1 = x - x1.astype(F32)
    x2 = r1.astype(BF16)
    x3 = (r1 - x2.astype(F32)).astype(BF16)
    return x1, x2, x3


def _dot_exact(x, e):
    return sum(jnp.dot(p, e, preferred_element_type=F32) for p in _split3(x))


def _head_expand(heads, width):
    dh = width // heads
    rows = jnp.arange(LANES)[:, None]
    cols = jnp.arange(width)[None, :]
    return (cols // dh == rows).astype(BF16)


def _rstd(x):
    return lax.rsqrt(jnp.mean(x * x, axis=-1, keepdims=True) + EPS)


def _rms_fwd(x, gain, *, name):
    return _rowwise(lambda x, g: x * _rstd(x) * g, [x], [gain], [(x.shape[1], BF16)], name=name)[0]


def _rms_bwd_vals(x, dhn, gain):
    r = _rstd(x)
    xh = x * r
    dxh = dhn * gain
    dx = r * (dxh - xh * jnp.mean(dxh * xh, axis=-1, keepdims=True))
    return dx, jnp.sum(dhn * xh, axis=0, keepdims=True)


def _norm_input_grad(dz, w, x, dres, gain, *, tb=False, name):
    def epilogue(dhn, x, dres, g):
        dx, dg = _rms_bwd_vals(x, dhn, g)
        dh = dres + dx
        return dh, dh, dg
    return _matmul(dz, w, tb=tb, outs=(F32, BF16), extras=(x, dres), consts=(gain,), sums=((1, x.shape[1]),), epilogue=epilogue,
                   name=name)


def _loss_and_grad(h, target, gain, *, name):
    def fn(h, tgt, g):
        r = _rstd(h)
        err = h * r * g - tgt
        loss = 0.5 * jnp.sum(jnp.mean(err * err, axis=-1, keepdims=True), axis=0, keepdims=True)
        dx, dg = _rms_bwd_vals(h, err * (1.0 / D_MODEL), g)
        return dx, dx, jnp.broadcast_to(loss, (1, LANES)), dg
    d = h.shape[1]
    return _rowwise(fn, [h, target], [gain], [(d, F32), (d, BF16)], [(1, LANES), (1, d)], name=name)


def _adamw(w, g, m, v, *, name):
    rows, cols = w.shape
    tile = rows
    for cand in (512, 256, 128, 64, 32, 16, 8):
        if rows % cand == 0 and rows > cand:
            tile = cand
            break
    bc1 = 1.0 - ADAM_B1 ** ADAM_STEP
    bc2 = 1.0 - ADAM_B2 ** ADAM_STEP

    def fn(w, g, m, v):
        m2 = ADAM_B1 * m + (1.0 - ADAM_B1) * g
        v2 = ADAM_B2 * v + (1.0 - ADAM_B2) * (g * g)
        delta = -ADAM_LR * ((m2 / bc1) / (jnp.sqrt(v2 / bc2) + ADAM_EPS) + ADAM_WD * w)
        return delta, m2, v2
    return _rowwise(fn, [w, g, m, v], [], [(cols, F32)] * 3, name=name, tile=tile)


def _attn_specs(arr, width, cb, classes, rows_block, whole, together=1):
    ncols = arr.shape[2] // (classes * width)
    lanes, at = (width, lambda r: r * ncols + cb) if together == 1 else (together * ncols * width, lambda r: r)
    if whole:
        return pl.BlockSpec((1, arr.shape[1], lanes), lambda n, r, i: (n, 0, at(r)))
    return pl.BlockSpec((1, rows_block, lanes), lambda n, r, i: (n, i, at(r)))


def _class_window(refs, spans, together, cl):
    if together == 1:
        return refs
    return [ref.at[:, :, pl.ds((cl * ncols + cb) * width, width)] for ref, (ncols, cb, width) in zip(refs, spans)]


def _attn_tiles(mode, lq, lk, backward=False):
    if mode == "full":
        return min(lq, 256), min(lq, 256), lk
    if mode == "band":
        tq = min(BAND_ROWS, lq)
        rows = tq if backward else ATT_BLOCK
        return tq, rows, min(rows + ATT_BLOCK, lk)
    rows = CAUSAL_ROWS_BACKWARD if backward else CAUSAL_ROWS
    return rows, rows, ATT_CHUNK


def _window(mode, row0, j, rows, win, lk):
    if mode == "causal":
        return pl.multiple_of(j * win, win), row0 // win + 1
    if mode == "band":
        return pl.multiple_of(jnp.clip(row0 + rows - win, 0, lk - win), ATT_BLOCK), 1
    return 0, 1


def _allowed(mode, row0, start, rows, win, max_dist):
    if mode == "full":
        return None
    dist = (row0 + lax.broadcasted_iota(jnp.int32, (rows, win), 0)) - (start + lax.broadcasted_iota(jnp.int32, (rows, win), 1))
    ok = dist >= 0
    if max_dist is not None:
        ok = ok & (dist <= max_dist)
    return ok


def _head_groups(heads, dh):
    gw = max(LANES, dh)
    return gw, gw // dh, heads // (gw // dh)


def _own_lanes(rows, gw, dh, u):
    return lax.broadcasted_iota(jnp.int32, (rows, gw), 1) // dh == u


def _only_head(x, own, per, u):
    return x if per == 1 else jnp.where(own[u], x, jnp.zeros_like(x))


def _step_scores(qz, kw, kb_row, ok):
    s = lax.dot_general(qz, kw, (((1,), (1,)), ((), ())), preferred_element_type=F32)
    if kb_row is not None:
        s = s - kb_row
    if ok is not None:
        s = jnp.where(ok, s, MASKED)
    return s


def _attn_fwd(q, k, v, kb=None, *, classes=1, heads, dh, mode, max_dist=None, bf16_copy=False, name):
    (qa, qc), (ka, kc), (va, vc) = q, k, v
    n, lq, lk = qa.shape[0], qa.shape[1], ka.shape[1]
    width = heads * dh
    tq, rows, win = _attn_tiles(mode, lq, lk)
    gw, per, groups = _head_groups(heads, dh)
    scale = dh ** -0.5
    has_kb = kb is not None
    single = mode != "causal"
    together = min(classes, max(1, CLASS_ROWS // lq))
    ncols = lambda arr: arr.shape[2] // (classes * width)
    spans = [(ncols(qa), qc, width), (ncols(ka), kc, width), (ncols(va), vc, width), (1, 0, width), (1, 0, LANES), (1, 0, width)]

    def body(*refs):
        for cl in range(together):
            for sub in range(tq // rows):
                part(_class_window(refs, spans, together, cl), pl.program_id(2) * tq + sub * rows, slice(sub * rows, (sub + 1) * rows))

    def part(refs, row0, rs):
        q_ref, k_ref, v_ref = refs[:3]
        kb_ref = refs[3] if has_kb else None
        n_in = 4 if has_kb else 3
        o_ref, lse_ref = refs[n_in:n_in + 2]
        o16_ref = refs[n_in + 2] if bf16_copy else None
        lane = lax.broadcasted_iota(jnp.int32, (rows, LANES), 1)
        own = [_own_lanes(rows, gw, dh, u) for u in range(per)]

        def step(j, carry):
            m_in, l_in = carry
            m_out, l_out = m_in, l_in
            start, _ = _window(mode, row0, j, rows, win, lk)
            ok = _allowed(mode, row0, start, rows, win, max_dist)
            for g in range(groups):
                cols = slice(g * gw, (g + 1) * gw)
                qg = q_ref[0, rs, cols] * scale
                kw = k_ref[0, pl.ds(start, win), cols]
                vw = v_ref[0, pl.ds(start, win), cols]
                alpha_g = new_g = None
                for u in range(per):
                    h = g * per + u
                    kb_row = kb_ref[0, h, pl.ds(j, 1), :] if has_kb else None
                    s = _step_scores(_only_head(qg, own, per, u), kw, kb_row, ok)
                    m_new = jnp.max(s, axis=1, keepdims=True)
                    if not single:
                        m_old = m_in[:, h:h + 1]
                        m_new = jnp.maximum(m_old, m_new)
                        alpha = jnp.exp(m_old - m_new)
                        alpha_g = alpha if u == 0 else jnp.where(own[u], alpha, alpha_g)
                    p = jnp.exp(s - m_new)
                    l_new = jnp.sum(p, axis=1, keepdims=True)
                    r = jnp.dot(p.astype(BF16), vw, preferred_element_type=F32)
                    if single:
                        r = r * (1.0 / l_new)
                    else:
                        l_new = alpha * l_in[:, h:h + 1] + l_new
                    new_g = r if u == 0 else jnp.where(own[u], r, new_g)
                    m_out = jnp.where(lane == h, m_new, m_out)
                    l_out = jnp.where(lane == h, l_new, l_out)
                o_ref[0, rs, cols] = new_g if single else alpha_g * o_ref[0, rs, cols] + new_g
                if bf16_copy:
                    o16_ref[0, rs, cols] = new_g.astype(BF16)
            return m_out, l_out

        carry = (jnp.full((rows, LANES), MASKED, F32), jnp.zeros((rows, LANES), F32))
        if single:
            m_all, l_all = step(0, carry)
            l_all = jnp.where(lane < heads, l_all, 1.0)
        else:
            o_ref[...] = jnp.zeros(o_ref.shape, F32)
            m_all, l_all = lax.fori_loop(0, _window(mode, row0, 0, rows, win, lk)[1], step, carry)
            l_all = jnp.where(lane < heads, l_all, 1.0)
            inv = 1.0 / l_all
            for g in range(groups):
                cols = slice(g * gw, (g + 1) * gw)
                inv_g = inv[:, g * per:g * per + 1]
                for u in range(1, per):
                    inv_g = jnp.where(own[u], inv[:, g * per + u:g * per + u + 1], inv_g)
                o_ref[0, rs, cols] = o_ref[0, rs, cols] * inv_g
        lse_ref[0, rs, :] = jnp.where(lane < heads, m_all + jnp.log(l_all), 0.0)

    in_specs = [_attn_specs(qa, width, qc, classes, tq, False, together), _attn_specs(ka, width, kc, classes, win, True, together),
                _attn_specs(va, width, vc, classes, win, True, together)]
    args = [qa, ka, va]
    if has_kb:
        assert together == 1
        in_specs.append(pl.BlockSpec((1,) + kb.shape[1:], lambda n_, r, i: (n_, 0, 0, 0)))
        args.append(kb)
    out_shape = [jax.ShapeDtypeStruct((n, lq, classes * width), F32), jax.ShapeDtypeStruct((n, lq, classes * LANES), F32)]
    out_specs = [pl.BlockSpec((1, tq, together * width), lambda n_, r, i: (n_, i, r)),
                 pl.BlockSpec((1, tq, together * LANES), lambda n_, r, i: (n_, i, r))]
    if bf16_copy:
        assert single
        out_shape.append(jax.ShapeDtypeStruct((n, lq, classes * width), BF16))
        out_specs.append(out_specs[0])
    return pl.pallas_call(
        body, grid=(n, classes // together, lq // tq), in_specs=in_specs, out_specs=out_specs, out_shape=out_shape, name=name,
        compiler_params=_params("parallel", "parallel", "arbitrary"),
    )(*args)


def _attn_bwd(q, k, v, do, lse, delta, kb=None, *, classes=1, heads, dh, mode, max_dist=None, dq_dtype=F32, name):
    (qa, qc), (ka, kc), (va, vc), (da, dc) = q, k, v, do
    n, lq, lk = qa.shape[0], qa.shape[1], ka.shape[1]
    width = heads * dh
    tq, rows, win = _attn_tiles(mode, lq, lk, backward=True)
    gw, per, groups = _head_groups(heads, dh)
    scale = dh ** -0.5
    has_kb = kb is not None
    single = mode != "causal"
    nt = (((1,), (1,)), ((), ()))
    tn = (((0,), (0,)), ((), ()))
    together = min(classes, max(1, CLASS_ROWS // lq))
    ncols = lambda arr: arr.shape[2] // (classes * width)
    spans = [(ncols(qa), qc, width), (ncols(ka), kc, width), (ncols(va), vc, width), (ncols(da), dc, width), (1, 0, LANES),
             (1, 0, LANES), (1, 0, width), (1, 0, width), (1, 0, width)]

    def body(*refs):
        n_in = 7 if has_kb else 6
        dk_ref, dv_ref = refs[n_in + 1:n_in + 3]

        @pl.when(pl.program_id(2) == 0)
        def _():
            dk_ref[...] = jnp.zeros(dk_ref.shape, F32)
            dv_ref[...] = jnp.zeros(dv_ref.shape, F32)
            if has_kb:
                refs[n_in + 3][...] = jnp.zeros(refs[n_in + 3].shape, F32)

        for cl in range(together):
            for sub in range(tq // rows):
                part(_class_window(refs, spans, together, cl), pl.program_id(2) * tq + sub * rows, slice(sub * rows, (sub + 1) * rows))

    def part(refs, row0, rs):
        q_ref, k_ref, v_ref, do_ref, lse_ref, dl_ref = refs[:6]
        kb_ref = refs[6] if has_kb else None
        n_in = 7 if has_kb else 6
        dq_ref, dk_ref, dv_ref = refs[n_in:n_in + 3]
        dkb_ref, drow_ref = refs[n_in + 3:n_in + 5] if has_kb else (None, None)
        lse_all = lse_ref[0, rs, :]
        dl_all = dl_ref[0, rs, :]
        lane = lax.broadcasted_iota(jnp.int32, (rows, LANES), 1)
        own = [_own_lanes(rows, gw, dh, u) for u in range(per)]

        def step(j, drow_all):
            start, _ = _window(mode, row0, j, rows, win, lk)
            ok = _allowed(mode, row0, start, rows, win, max_dist)
            for g in range(groups):
                cols = slice(g * gw, (g + 1) * gw)
                qg = q_ref[0, rs, cols] * scale
                dog = do_ref[0, rs, cols]
                kw = k_ref[0, pl.ds(start, win), cols]
                vw = v_ref[0, pl.ds(start, win), cols]
                dq_g = dk_w = dv_w = None
                for u in range(per):
                    h = g * per + u
                    qz, doz = _only_head(qg, own, per, u), _only_head(dog, own, per, u)
                    kb_row = kb_ref[0, h, pl.ds(j, 1), :] if has_kb else None
                    p = jnp.exp(_step_scores(qz, kw, kb_row, ok) - lse_all[:, h:h + 1])
                    dp = lax.dot_general(doz, vw, nt, preferred_element_type=F32)
                    ds = p * (dp - dl_all[:, h:h + 1])
                    dsb = ds.astype(BF16)
                    r = jnp.dot(dsb, kw, preferred_element_type=F32)
                    dq_g = r if u == 0 else jnp.where(own[u], r, dq_g)
                    dk_u = lax.dot_general(dsb, qz, tn, preferred_element_type=F32)
                    dv_u = lax.dot_general(p.astype(BF16), doz, tn, preferred_element_type=F32)
                    dk_w = dk_u if u == 0 else dk_w + dk_u
                    dv_w = dv_u if u == 0 else dv_w + dv_u
                    if has_kb:
                        dkb_ref[0, h, pl.ds(j, 1), :] += -jnp.sum(ds, axis=0, keepdims=True)
                        drow_all = drow_all + jnp.where(lane == h, jnp.sum(ds, axis=1, keepdims=True), 0.0)
                dk_ref[0, pl.ds(start, win), cols] += dk_w
                dv_ref[0, pl.ds(start, win), cols] += dv_w
                if single:
                    dq_ref[0, rs, cols] = (dq_g * scale).astype(dq_ref.dtype)
                else:
                    dq_ref[0, rs, cols] += dq_g * scale
            return drow_all

        drow_all = jnp.zeros((rows, LANES), F32)
        if single:
            drow_all = step(0, drow_all)
        else:
            dq_ref[...] = jnp.zeros(dq_ref.shape, F32)
            drow_all = lax.fori_loop(0, _window(mode, row0, 0, rows, win, lk)[1], step, drow_all)
        if has_kb:
            drow_ref[0, rs, :] = drow_all

    row_spec = functools.partial(_attn_specs, classes=classes, rows_block=tq, whole=False, together=together)
    in_specs = [row_spec(qa, width, qc), _attn_specs(ka, width, kc, classes, win, True, together),
                _attn_specs(va, width, vc, classes, win, True, together), row_spec(da, width, dc), row_spec(lse, LANES, 0),
                row_spec(delta, LANES, 0)]
    args = [qa, ka, va, da, lse, delta]
    assert single or dq_dtype == F32
    out_shape = [jax.ShapeDtypeStruct((n, lq, classes * width), dq_dtype), jax.ShapeDtypeStruct((n, lk, classes * width), F32),
                 jax.ShapeDtypeStruct((n, lk, classes * width), F32)]
    kv_spec = pl.BlockSpec((1, lk, together * width), lambda n_, r, i: (n_, 0, r))
    out_specs = [pl.BlockSpec((1, tq, together * width), lambda n_, r, i: (n_, i, r)), kv_spec, kv_spec]
    if has_kb:
        assert together == 1
        kb_spec = pl.BlockSpec((1,) + kb.shape[1:], lambda n_, r, i: (n_, 0, 0, 0))
        in_specs.append(kb_spec)
        args.append(kb)
        out_shape += [jax.ShapeDtypeStruct(kb.shape, F32), jax.ShapeDtypeStruct((n, lq, LANES), F32)]
        out_specs += [kb_spec, pl.BlockSpec((1, tq, LANES), lambda n_, r, i: (n_, i, 0))]
    return pl.pallas_call(
        body, grid=(n, classes // together, lq // tq), in_specs=in_specs, out_specs=out_specs, out_shape=out_shape, name=name,
        compiler_params=_params("parallel", "parallel", "arbitrary"),
    )(*args)


def _rope_tables():
    half = HEAD_DIM // 2
    inv = ROPE_THETA ** (-jnp.arange(half, dtype=F32) / half)
    ang = jnp.arange(SEQ, dtype=F32)[:, None] * inv[None, :]
    cos = jnp.tile(jnp.cos(ang), (1, 2 * ATT_W // HEAD_DIM))
    sin = jnp.tile(jnp.concatenate([-jnp.sin(ang), jnp.sin(ang)], axis=1), (1, ATT_W // HEAD_DIM))
    return cos, sin


def _rotate(x, cos, sin):
    width = x.shape[1]
    lane = lax.broadcasted_iota(jnp.int32, x.shape, 1)
    first_half = (lane % HEAD_DIM) < (HEAD_DIM // 2)
    swapped = jnp.where(first_half, pltpu.roll(x, width - HEAD_DIM // 2, axis=1), pltpu.roll(x, HEAD_DIM // 2, axis=1))
    return x * cos[:, :width] + swapped * sin[:, :width]


def _gelu(x):
    k = math.sqrt(2.0 / math.pi)
    t = jnp.tanh(k * (x + 0.044715 * x * x * x))
    return 0.5 * x * (1.0 + t), t


def _gelu_grad(x, t):
    k = math.sqrt(2.0 / math.pi)
    return 0.5 * (1.0 + t) + 0.5 * x * (1.0 - t * t) * k * (1.0 + 3.0 * 0.044715 * x * x)


def _shift_down(x, halo, s):
    ext = jnp.concatenate([halo, x], axis=0)
    return pltpu.roll(ext, s, axis=0)[8:, :]


def _shift_up(x, halo, s):
    rows = x.shape[0]
    ext = jnp.concatenate([x, halo], axis=0)
    return pltpu.roll(ext, rows + 8 - s, axis=0)[:rows, :]


def _lru_gates(u, halo, cw, cb, wa, ba, wi, bi, lam):
    taps = [_shift_down(u, halo, CONV_WIDTH - 1 - k) for k in range(CONV_WIDTH - 1)] + [u]
    uc = cb + sum(cw[k:k + 1, :] * taps[k] for k in range(CONV_WIDTH))
    ucb = uc.astype(BF16)
    r = jax.nn.sigmoid(jnp.dot(ucb, wa, preferred_element_type=F32) + ba)
    gi = jax.nn.sigmoid(jnp.dot(ucb, wi, preferred_element_type=F32) + bi)
    sp = jnp.maximum(-lam, 0.0) + jnp.log(1.0 + jnp.exp(-jnp.abs(lam)))
    log_a = -LRU_C * r * sp
    a = jnp.exp(log_a)
    x2 = 2.0 * log_a
    one_minus_a2 = jnp.where(x2 > -0.01, -x2 * (1.0 + x2 * (0.5 + x2 * (1.0 / 6.0 + x2 / 24.0))), 1.0 - jnp.exp(x2))
    mult = jnp.sqrt(one_minus_a2)
    return taps, uc, ucb, r, gi, sp, a, mult


def _lru_specs(nchunk, reverse):
    def chunk(b, c):
        return b * nchunk + ((nchunk - 1 - c) if reverse else c)

    def halo_before(b, c):
        return jnp.maximum(chunk(b, c) * (LRU_CHUNK // 8) - 1, 0)

    return chunk, halo_before


def _lru_fwd(zug, cw, cb, wa, ba, wi, bi, lam, *, name):
    total = zug.shape[0]
    nchunk = SEQ // LRU_CHUNK
    w = LRU_WIDTH
    chunk, halo_before = _lru_specs(nchunk, False)

    def body(u_ref, uh_ref, g_ref, cw_ref, cb_ref, wa_ref, ba_ref, wi_ref, bi_ref, lam_ref, y_ref, h_ref, a_sc, x_sc, carry_sc):
        c = pl.program_id(1)
        u = u_ref[...]
        halo = jnp.where(c > 0, uh_ref[...], 0.0)
        _, uc, _, _, gi, _, a, mult = _lru_gates(u, halo, cw_ref[...], cb_ref[...], wa_ref[...], ba_ref[...],
                                                 wi_ref[...], bi_ref[...], lam_ref[...])
        a_sc[...] = a
        x_sc[...] = mult * (gi * uc)

        @pl.when(c == 0)
        def _():
            carry_sc[...] = jnp.zeros(carry_sc.shape, F32)

        def tile_step(t, h):
            r0 = pl.multiple_of(t * 8, 8)
            at = a_sc[pl.ds(r0, 8), :]
            xt = x_sc[pl.ds(r0, 8), :]
            rows = []
            for j in range(8):
                h = at[j:j + 1, :] * h + xt[j:j + 1, :]
                rows.append(h)
            h_ref[pl.ds(r0, 8), :] = jnp.concatenate(rows, axis=0)
            return h

        h_last = lax.fori_loop(0, LRU_CHUNK // 8, tile_step, carry_sc[0:1, :])
        carry_sc[0:1, :] = h_last
        gel, _ = _gelu(g_ref[...])
        y_ref[...] = (h_ref[...] * gel).astype(BF16)

    small = lambda arr: pl.BlockSpec(arr.shape, lambda b, c: (0, 0))
    return pl.pallas_call(
        body, grid=(total // SEQ, nchunk),
        in_specs=[pl.BlockSpec((LRU_CHUNK, w), lambda b, c: (chunk(b, c), 0)),
                  pl.BlockSpec((8, w), lambda b, c: (halo_before(b, c), 0)),
                  pl.BlockSpec((LRU_CHUNK, w), lambda b, c: (chunk(b, c), 1)),
                  small(cw), small(cb), small(wa), small(ba), small(wi), small(bi), small(lam)],
        out_specs=[pl.BlockSpec((LRU_CHUNK, w), lambda b, c: (chunk(b, c), 0))] * 2,
        out_shape=[jax.ShapeDtypeStruct((total, w), BF16), jax.ShapeDtypeStruct((total, w), F32)],
        scratch_shapes=[pltpu.VMEM((LRU_CHUNK, w), F32), pltpu.VMEM((LRU_CHUNK, w), F32), pltpu.VMEM((8, w), F32)],
        name=name, compiler_params=_params("arbitrary", "arbitrary"),
    )(zug, zug, zug, cw, cb, wa, ba, wi, bi, lam)


def _lru_bwd(zug, hl, dy, cw, cb, wa, ba, wi, bi, lam, *, name):
    total = zug.shape[0]
    nchunk = SEQ // LRU_CHUNK
    w = LRU_WIDTH
    chunk, halo_before = _lru_specs(nchunk, True)
    tn = (((0,), (0,)), ((), ()))
    nt = (((1,), (1,)), ((), ()))

    def body(u_ref, uh_ref, g_ref, hl_ref, hh_ref, dy_ref, cw_ref, cb_ref, wa_ref, ba_ref, wi_ref, bi_ref, lam_ref,
             dz_ref, dcw_ref, dcb_ref, dwa_ref, dba_ref, dwi_ref, dbi_ref, dlam_ref,
             a_sc, d_sc, g_sc, gcarry_sc, acarry_sc, duc_sc):
        b, c = pl.program_id(0), pl.program_id(1)
        first_chunk = c == nchunk - 1

        @pl.when((b == 0) & (c == 0))
        def _():
            for ref in (dcw_ref, dcb_ref, dwa_ref, dba_ref, dwi_ref, dbi_ref, dlam_ref):
                ref[...] = jnp.zeros(ref.shape, F32)

        @pl.when(c == 0)
        def _():
            gcarry_sc[...] = jnp.zeros(gcarry_sc.shape, F32)
            acarry_sc[...] = jnp.zeros(acarry_sc.shape, F32)
            duc_sc[...] = jnp.zeros(duc_sc.shape, F32)

        u = u_ref[...]
        halo = jnp.where(first_chunk, 0.0, uh_ref[...])
        cw, wa, wi, lam = cw_ref[...], wa_ref[...], wi_ref[...], lam_ref[...]
        taps, uc, ucb, r, gi, sp, a, mult = _lru_gates(u, halo, cw, cb_ref[...], wa, ba_ref[...], wi, bi_ref[...], lam)
        gate = g_ref[...]
        gel, th = _gelu(gate)
        dy = dy_ref[...]
        hl = hl_ref[...]
        a_sc[...] = a
        d_sc[...] = dy * gel
        dgate = dy * hl * _gelu_grad(gate, th)

        def tile_step(t, carry):
            g_next, a_next = carry
            r0 = pl.multiple_of((LRU_CHUNK // 8 - 1 - t) * 8, 8)
            dt = d_sc[pl.ds(r0, 8), :]
            at = a_sc[pl.ds(r0, 8), :]
            rows = [None] * 8
            for j in reversed(range(8)):
                g_next = dt[j:j + 1, :] + a_next * g_next
                a_next = at[j:j + 1, :]
                rows[j] = g_next
            g_sc[pl.ds(r0, 8), :] = jnp.concatenate(rows, axis=0)
            return g_next, a_next

        g_last, a_last = lax.fori_loop(0, LRU_CHUNK // 8, tile_step, (gcarry_sc[0:1, :], acarry_sc[0:1, :]))
        gcarry_sc[0:1, :] = g_last
        acarry_sc[0:1, :] = a_last

        gs = g_sc[...]
        hl_halo = jnp.where(first_chunk, 0.0, hh_ref[...])
        da = gs * _shift_down(hl, hl_halo, 1)
        dmult = gs * gi * uc
        dgi = gs * mult * uc
        duc = gs * mult * gi
        dlog_a = da * a - dmult * a * a / mult
        dr = dlog_a * (-LRU_C * sp)
        dsp = jnp.sum(dlog_a * (-LRU_C * r), axis=0, keepdims=True)
        dlam_ref[...] += -dsp * jax.nn.sigmoid(-lam)
        dpa = dr * r * (1.0 - r)
        dpi = dgi * gi * (1.0 - gi)
        dpab, dpib = dpa.astype(BF16), dpi.astype(BF16)
        dba_ref[...] += jnp.sum(dpa, axis=0, keepdims=True)
        dbi_ref[...] += jnp.sum(dpi, axis=0, keepdims=True)
        dwa_ref[...] += lax.dot_general(ucb, dpab, tn, preferred_element_type=F32)
        dwi_ref[...] += lax.dot_general(ucb, dpib, tn, preferred_element_type=F32)
        duc = duc + lax.dot_general(dpab, wa, nt, preferred_element_type=F32)
        duc = duc + lax.dot_general(dpib, wi, nt, preferred_element_type=F32)
        dcb_ref[...] += jnp.sum(duc, axis=0, keepdims=True)
        dcw_ref[...] += jnp.concatenate([jnp.sum(duc * taps[k], axis=0, keepdims=True) for k in range(CONV_WIDTH)], axis=0)
        after = duc_sc[...]
        du = cw[CONV_WIDTH - 1:CONV_WIDTH, :] * duc
        for k in range(CONV_WIDTH - 1):
            du = du + cw[k:k + 1, :] * _shift_up(duc, after, CONV_WIDTH - 1 - k)
        duc_sc[...] = duc[0:8, :]
        dz_ref[:, 0:w] = du.astype(BF16)
        dz_ref[:, w:2 * w] = dgate.astype(BF16)

    small = lambda arr: pl.BlockSpec(arr.shape, lambda b, c: (0, 0))
    acc = lambda shape: pl.BlockSpec(shape, lambda b, c: (0, 0))
    chunk_spec = lambda cb_: pl.BlockSpec((LRU_CHUNK, w), lambda b, c: (chunk(b, c), cb_))
    halo_spec = pl.BlockSpec((8, w), lambda b, c: (halo_before(b, c), 0))
    acc_shapes = [(CONV_WIDTH, w), (1, w), (w, w), (1, w), (w, w), (1, w), (1, w)]
    return pl.pallas_call(
        body, grid=(total // SEQ, nchunk),
        in_specs=[chunk_spec(0), halo_spec, chunk_spec(1), chunk_spec(0), halo_spec, chunk_spec(0),
                  small(cw), small(cb), small(wa), small(ba), small(wi), small(bi), small(lam)],
        out_specs=[pl.BlockSpec((LRU_CHUNK, 2 * w), lambda b, c: (chunk(b, c), 0))] + [acc(s) for s in acc_shapes],
        out_shape=[jax.ShapeDtypeStruct((total, 2 * w), BF16)] + [jax.ShapeDtypeStruct(s, F32) for s in acc_shapes],
        scratch_shapes=[pltpu.VMEM((LRU_CHUNK, w), F32)] * 3 + [pltpu.VMEM((8, w), F32)] * 3,
        name=name, compiler_params=_params("arbitrary", "arbitrary"),
    )(zug, zug, zug, hl, hl, dy, cw, cb, wa, ba, wi, bi, lam)


def _block_diag(wb):
    eye = jnp.eye(LRU_BLOCKS, dtype=wb.dtype)
    return jnp.einsum("gcd,gh->gchd", wb, eye).reshape(LRU_WIDTH, LRU_WIDTH).astype(BF16)


def _diag_blocks(wd):
    blk = LRU_WIDTH // LRU_BLOCKS
    return jnp.stack([wd[g * blk:(g + 1) * blk, g * blk:(g + 1) * blk] for g in range(LRU_BLOCKS)])


FOX_SCAN = 256


def _fox_bias(fl, bf, *, name):
    nb = fl.shape[0]

    def body(fl_ref, bf_ref, c_ref):
        x = fl_ref[0] + bf_ref[...]
        logf = jnp.minimum(x, 0.0) - jnp.log(1.0 + jnp.exp(-jnp.abs(x)))
        upper = (lax.broadcasted_iota(jnp.int32, (FOX_SCAN, FOX_SCAN), 0)
                 <= lax.broadcasted_iota(jnp.int32, (FOX_SCAN, FOX_SCAN), 1)).astype(BF16)
        carry = jnp.zeros((LRU_BLOCKS, 1), F32)
        for j in range(SEQ // FOX_SCAN):
            blk = logf[:, j * FOX_SCAN:(j + 1) * FOX_SCAN]
            c_ref[0, :, j * FOX_SCAN:(j + 1) * FOX_SCAN] = _dot_exact(blk, upper) + carry
            carry = carry + jnp.sum(blk, axis=1, keepdims=True)

    return pl.pallas_call(
        body, grid=(nb,),
        in_specs=[pl.BlockSpec((1, 8, SEQ), lambda b: (b, 0, 0)), pl.BlockSpec((8, 1), lambda b: (0, 0))],
        out_specs=pl.BlockSpec((1, 8, SEQ), lambda b: (b, 0, 0)),
        out_shape=jax.ShapeDtypeStruct((nb, 8, SEQ), F32), name=name, compiler_params=_params("arbitrary"),
    )(fl, bf)


def _fox_bias_bwd(fl, bf, dc, *, name):
    nb = fl.shape[0]

    def body(fl_ref, bf_ref, dc_ref, dfl_ref, dbf_ref):
        @pl.when(pl.program_id(0) == 0)
        def _():
            dbf_ref[...] = jnp.zeros(dbf_ref.shape, F32)

        x = fl_ref[0] + bf_ref[...]
        dc_all = dc_ref[0]
        lower = (lax.broadcasted_iota(jnp.int32, (FOX_SCAN, FOX_SCAN), 0)
                 >= lax.broadcasted_iota(jnp.int32, (FOX_SCAN, FOX_SCAN), 1)).astype(BF16)
        carry = jnp.zeros((LRU_BLOCKS, 1), F32)
        total = jnp.zeros((LRU_BLOCKS, 1), F32)
        for j in reversed(range(SEQ // FOX_SCAN)):
            cols = slice(j * FOX_SCAN, (j + 1) * FOX_SCAN)
            blk = dc_all[:, cols]
            dlogf = _dot_exact(blk, lower) + carry
            carry = carry + jnp.sum(blk, axis=1, keepdims=True)
            dx = dlogf * jax.nn.sigmoid(-x[:, cols])
            dfl_ref[0, :, cols] = dx
            total = total + jnp.sum(dx, axis=1, keepdims=True)
        dbf_ref[...] += total

    return pl.pallas_call(
        body, grid=(nb,),
        in_specs=[pl.BlockSpec((1, 8, SEQ), lambda b: (b, 0, 0)), pl.BlockSpec((8, 1), lambda b: (0, 0)),
                  pl.BlockSpec((1, 8, SEQ), lambda b: (b, 0, 0))],
        out_specs=[pl.BlockSpec((1, 8, SEQ), lambda b: (b, 0, 0)), pl.BlockSpec((8, 1), lambda b: (0, 0))],
        out_shape=[jax.ShapeDtypeStruct((nb, 8, SEQ), F32), jax.ShapeDtypeStruct((8, 1), F32)],
        name=name, compiler_params=_params("arbitrary"),
    )(fl, bf, dc)


def _seq3(a, nb):
    return a.reshape(nb, a.shape[0] // nb, a.shape[1])


def _flat2(a):
    return a.reshape(a.shape[0] * a.shape[1], a.shape[2])


def _residual_out(y, w, h, next_gain, *, name):
    if next_gain is None:
        out, = _matmul(y, w, extras=(h,), epilogue=lambda acc, res: (acc + res,), name=name)
        return out, None

    def epilogue(acc, res, g):
        out = acc + res
        return out, out * _rstd(out) * g
    return _matmul(y, w, outs=(F32, BF16), extras=(h,), consts=(next_gain,), epilogue=epilogue, whole_rows=True, name=name)


def _mlp_fwd(h, hn, p, tag, next_gain):
    act, = _matmul(hn, p["w_up_t"], tb=True, outs=(BF16,), epilogue=lambda acc: (jnp.square(jnp.maximum(acc, 0.0)),),
                   name=f"{tag}_up")
    out, hn_next = _residual_out(act, p["w_down"], h, next_gain, name=f"{tag}_down")
    return out, hn_next, (h, hn, act)


def _mlp_bwd(dh, dhb, p, saved, tag):
    h, hn, act = saved
    dup, = _matmul(dhb, p["w_down"], tb=True, outs=(BF16,), extras=(act,),
                   epilogue=lambda acc, act: (acc * (2.0 * jnp.sqrt(act).astype(F32)),), name=f"{tag}_bwd_dup")
    dw_down, = _matmul(act, dhb, ta=True, outs=(BF16,),name=f"{tag}_bwd_dwdown")
    dw_up_t, = _matmul(dup, hn, ta=True, outs=(BF16,),name=f"{tag}_bwd_dwup")
    dh2, dh2b, dg = _norm_input_grad(dup, p["w_up_t"], h, dh, p["norm"], name=f"{tag}_bwd_dh")
    return dh2, dh2b, {"norm": dg, "w_up_t": dw_up_t, "w_down": dw_down}


def _xa_fwd(h, hn, mem, p, tag, nb, next_gain):
    mem_n = _rms_fwd(mem, p["mem_norm"], name=f"{tag}_memnorm")
    q, = _matmul(hn, p["w_q"], outs=(BF16,), name=f"{tag}_q")
    kv, = _matmul(mem_n, p["w_kv_t"], tb=True, outs=(BF16,), name=f"{tag}_kv")
    q3, kv3 = _seq3(q, nb), _seq3(kv, nb)
    o, lse, ob = _attn_fwd((q3, 0), (kv3, 0), (kv3, 1), heads=XA_HEADS, dh=XA_HEAD_DIM, mode="full", bf16_copy=True,
                           name=f"{tag}_attn")
    o, ob = _flat2(o), _flat2(ob)
    out, hn_next = _residual_out(ob, p["w_o"], h, next_gain, name=f"{tag}_o")
    return out, hn_next, (h, hn, mem_n, q3, kv3, o, ob, lse)


def _xa_bwd(dh, dhb, mem, p, saved, tag, nb):
    h, hn, mem_n, q3, kv3, o, ob, lse = saved
    dob, delta = _matmul(dhb, p["w_o"], tb=True, outs=(BF16, (F32, LANES)), extras=(o,), consts=(_head_expand(XA_HEADS, D_MODEL).T,),
                         epilogue=lambda acc, o, e: (acc, _dot_exact(acc * o, e)), name=f"{tag}_bwd_do")
    dw_o, = _matmul(ob, dhb, ta=True, outs=(BF16,),name=f"{tag}_bwd_dwo")
    dq, dk, dv = _attn_bwd((q3, 0), (kv3, 0), (kv3, 1), (_seq3(dob, nb), 0), lse, _seq3(delta, nb), heads=XA_HEADS,
                           dh=XA_HEAD_DIM, mode="full", dq_dtype=BF16, name=f"{tag}_bwd_attn")
    dqb = _flat2(dq)
    dkvb, = _rowwise(lambda a, b: jnp.concatenate([a, b], axis=1), [_flat2(dk), _flat2(dv)], [], [(2 * D_MODEL, BF16)],
                     name=f"{tag}_bwd_dkvcast")
    dw_q, = _matmul(hn, dqb, ta=True, outs=(BF16,),name=f"{tag}_bwd_dwq")
    dw_kv_t, = _matmul(dkvb, mem_n, ta=True, outs=(BF16,),name=f"{tag}_bwd_dwkv")
    dmem_n, = _matmul(dkvb, p["w_kv_t"], name=f"{tag}_bwd_dmemn")
    dg_mem, = _rowwise(lambda x, d, g: _rms_bwd_vals(x, d, g)[1], [mem, dmem_n], [p["mem_norm"]], [], [(1, D_MODEL)],
                       name=f"{tag}_bwd_memnorm")
    dh2, dh2b, dg = _norm_input_grad(dqb, p["w_q"], h, dh, p["norm"], tb=True, name=f"{tag}_bwd_dh")
    return dh2, dh2b, {"norm": dg, "mem_norm": dg_mem, "w_q": dw_q, "w_kv_t": dw_kv_t, "w_o": dw_o}


AB_MAIN = 2 * LRU_WIDTH + 3 * ATT_W


def _ab_fwd(h, hn, p, nb, next_gain, before_out=None):
    w_in_t = p["w_in_t"]
    zug, = _matmul(hn, w_in_t[:2 * LRU_WIDTH], tb=True, name="ab_in_ug")
    qkv, = _matmul(hn, w_in_t[2 * LRU_WIDTH:AB_MAIN], tb=True, outs=(BF16,), tn=768, name="ab_in_qkv")
    zf, = _matmul(hn, w_in_t[AB_MAIN:], tb=True, name="ab_in_f")
    fl = zf[:, :8].reshape(nb, SEQ, 8).transpose(0, 2, 1)
    cbias = _fox_bias(fl, p["b_f"], name="ab_fox_bias")
    kb = cbias.reshape(nb, 8, SEQ // ATT_CHUNK, ATT_CHUNK)
    y_a, hl = _lru_fwd(zug, p["conv_w"], p["conv_b"], p["w_a"], p["b_a"], p["w_i"], p["b_i"], p["lam"], name="ab_lru")
    qkv3 = _seq3(qkv, nb)
    o, lse = _attn_fwd((qkv3, 0), (qkv3, 1), (qkv3, 2), kb, heads=8, dh=HEAD_DIM, mode="causal", name="ab_fox")
    o = _flat2(o)
    y, = _rowwise(lambda a, b: jnp.concatenate([a, b.astype(BF16)], axis=1), [y_a, o], [], [(D_MODEL, BF16)], name="ab_ycat")
    if before_out is not None:
        before_out(y)
    out, hn_next = _residual_out(y, p["w_out"], h, next_gain, name="ab_out")
    return out, hn_next, (h, hn, zug, qkv3, fl, kb, hl, o, lse, y)


def _ab_bwd(dh, dhb, p, saved, nb):
    h, hn, zug, qkv3, fl, kb, hl, o, lse, y = saved
    dy, dyb, delta = _matmul(dhb, p["w_out"], tb=True, outs=(F32, BF16, (F32, LANES)), extras=(y,), consts=(_head_expand(8, ATT_W).T,),
                             epilogue=lambda acc, y, e: (acc, acc, _dot_exact(acc[:, ATT_W:] * y[:, ATT_W:].astype(F32), e)),
                             name="ab_bwd_dy")
    dw_out, = _matmul(y, dhb, ta=True, outs=(BF16,),name="ab_bwd_dwout")
    dzug, dcw, dcb, dwa, dba, dwi, dbi, dlam = _lru_bwd(zug, hl, dy, p["conv_w"], p["conv_b"], p["w_a"], p["b_a"],
                                                        p["w_i"], p["b_i"], p["lam"], name="ab_bwd_lru")
    dq, dk, dv, dkb, drow = _attn_bwd((qkv3, 0), (qkv3, 1), (qkv3, 2), (_seq3(dyb, nb), 1), lse, _seq3(delta, nb), kb,
                                      heads=8, dh=HEAD_DIM, mode="causal", name="ab_bwd_fox")
    dcum = dkb.reshape(nb, 8, SEQ) + drow[:, :, :8].transpose(0, 2, 1)
    dfl, dbf = _fox_bias_bwd(fl, p["b_f"], dcum, name="ab_bwd_fox_bias")
    dzf = jnp.pad(dfl.transpose(0, 2, 1).reshape(nb * SEQ, 8), ((0, 0), (0, LANES - 8)))
    dz, = _rowwise(lambda a, q, k, v, f: jnp.concatenate([a, q.astype(BF16), k.astype(BF16), v.astype(BF16), f.astype(BF16)], axis=1),
                   [dzug, _flat2(dq), _flat2(dk), _flat2(dv), dzf], [], [(AB_MAIN + LANES, BF16)], name="ab_bwd_dzcat")
    dw_in_t, = _matmul(dz, hn, ta=True, outs=(BF16,),tm=384, name="ab_bwd_dwin")
    dh2, dh2b, dg = _norm_input_grad(dz, p["w_in_t"], h, dh, p["norm"], name="ab_bwd_dh")
    grads = {"norm": dg, "w_in_t": dw_in_t[:AB_MAIN + 8], "conv_w": dcw, "conv_b": dcb, "w_a": _diag_blocks(dwa), "b_a": dba,
             "w_i": _diag_blocks(dwi), "b_i": dbi, "lam": dlam, "b_f": dbf.reshape(1, 8), "w_out": dw_out}
    return dh2, dh2b, grads


CD_IN = 3 * ATT_W + ATT_W + 2 * SWA_KW


def _gqa_share():
    src = jnp.arange(SWA_KW)[:, None]
    dst = jnp.arange(ATT_W)[None, :]
    per_kv = ATT_W // (SWA_KW // HEAD_DIM)
    return ((dst // per_kv == src // HEAD_DIM) & (dst % HEAD_DIM == src % HEAD_DIM)).astype(BF16)


def _cd_fwd(h, hn, p, nb, next_gain):
    z, = _matmul(hn, p["w_in_t"], tb=True, tn=1152, name="cd_in")
    cos, sin = _rope_tables()
    per_seq = SEQ // ROW_TILE
    table_map = lambda i: (i % per_seq, 0)

    def rope_fn(z, cos, sin, share):
        qc, kc, vc = z[:, 0:ATT_W], z[:, ATT_W:2 * ATT_W], z[:, 2 * ATT_W:3 * ATT_W]
        qd, kd, vd = z[:, 3 * ATT_W:4 * ATT_W], z[:, 4 * ATT_W:4 * ATT_W + SWA_KW], z[:, 4 * ATT_W + SWA_KW:]
        kd8 = jnp.dot(_rotate(kd, cos, sin).astype(BF16), share, preferred_element_type=F32)
        vd8 = jnp.dot(vd.astype(BF16), share, preferred_element_type=F32)
        qk = jnp.concatenate([_rotate(qc, cos, sin), _rotate(kc, cos, sin)], axis=1)
        return qk, vc, _rotate(qd, cos, sin), kd8, vd8, qk, qk, vc, vc
    dils = [dil for _, dil in DIL_PATTERN if dil > 1]
    outs = _rowwise(rope_fn, [z, cos, sin], [_gqa_share()],
                    [(2 * ATT_W, BF16)] + [(ATT_W, BF16)] * 4 + [(2 * ATT_W, BF16, d) for d in dils] + [(ATT_W, BF16, d) for d in dils],
                    name="cd_rope", row_maps=[None, table_map, table_map])
    qk, vc, qd, kd, vd = outs[:5]
    views = {1: (_seq3(qk, nb), _seq3(vc, nb))}
    for d, qk_d, vc_d in zip(dils, outs[5:5 + len(dils)], outs[5 + len(dils):]):
        views[d] = (_seq3(qk_d, nb), _seq3(vc_d, nb))
    in_classes = lambda a, width, d: _flat2(a) if d == 1 else ("classes", _flat2(a), width, d)
    branch = []
    for window, dil in DIL_PATTERN:
        qk_v, vc_v = views[dil]
        o_i, lse_i = _attn_fwd((qk_v, 0), (qk_v, 1), (vc_v, 0), classes=dil, heads=8, dh=HEAD_DIM, mode="band",
                               max_dist=window // dil, name=f"cd_dil{dil}")
        branch.append((in_classes(o_i, ATT_W, dil), in_classes(lse_i, LANES, dil)))
    expand = _head_expand(8, ATT_W)

    qd3, kd3, vd3 = _seq3(qd, nb), _seq3(kd, nb), _seq3(vd, nb)
    o_d, lse_band = _attn_fwd((qd3, 0), (kd3, 0), (vd3, 0), heads=8, dh=HEAD_DIM, mode="band", max_dist=SWA_WINDOW - 1,
                              name="cd_swa")

    def mix(o1, o2, o3, l1, l2, l3, o_band, l_band, e, sink):
        m = jnp.maximum(jnp.maximum(l1, l2), l3)
        lt = m + jnp.log(jnp.exp(l1 - m) + jnp.exp(l2 - m) + jnp.exp(l3 - m))
        y_c = sum(_dot_exact(jnp.exp(l - lt), e) * o_ for o_, l in ((o1, l1), (o2, l2), (o3, l3)))
        lt_d = jnp.maximum(l_band, sink) + jnp.log(1.0 + jnp.exp(-jnp.abs(l_band - sink)))
        y_d = o_band * _dot_exact(jnp.exp(l_band - lt_d), e)
        return jnp.concatenate([y_c, y_d], axis=1), y_c, lt, y_d, lt_d
    y, y_c, lse_c, y_d, lse_d = _rowwise(
        mix, [b[0] for b in branch] + [b[1] for b in branch] + [_flat2(o_d), _flat2(lse_band)], [expand, p["sink"]],
        [(D_MODEL, BF16), (ATT_W, F32), (LANES, F32), (ATT_W, F32), (LANES, F32)], name="cd_mix")
    out, hn_next = _residual_out(y, p["w_out"], h, next_gain, name="cd_out")
    return out, hn_next, (h, hn, views, qd3, kd3, vd3, y_c, lse_c, y_d, lse_d, y)


def _cd_bwd(dh, dhb, p, saved, nb):
    h, hn, views, qd3, kd3, vd3, y_c, lse_c, y_d, lse_d, y = saved
    dy, dyb = _matmul(dhb, p["w_out"], tb=True, outs=(F32, BF16), epilogue=lambda acc: (acc, acc), name="cd_bwd_dy")
    dw_out, = _matmul(y, dhb, ta=True, outs=(BF16,),name="cd_bwd_dwout")
    dyb3 = _seq3(dyb, nb)
    dils = [dil for _, dil in DIL_PATTERN if dil > 1]
    gather = _head_expand(8, ATT_W).T

    def prepare(do, o, lse, do_d, o_d, lse_d, e, sink):
        delta = _dot_exact(do * o, e)
        delta_d = _dot_exact(do_d * o_d, e)
        dsink = -jnp.sum(jnp.exp(sink - lse_d) * delta_d, axis=0, keepdims=True)
        return (delta,) + (do,) * len(dils) + (lse,) * len(dils) + (delta,) * len(dils) + (delta_d, dsink)
    outs = _rowwise(prepare, [(dy, ATT_W, 0), y_c, lse_c, (dy, ATT_W, 1), y_d, lse_d], [gather, p["sink"]],
                    [(LANES, F32)] + [(ATT_W, BF16, d) for d in dils] + [(LANES, F32, d) for d in dils] * 2 + [(LANES, F32)],
                    [(1, LANES)], name="cd_bwd_delta")
    delta_d, dsink = outs[-2:]
    seen = {1: ((dyb3, 0), _seq3(lse_c, nb), _seq3(outs[0], nb))}
    for j, d in enumerate(dils):
        seen[d] = ((_seq3(outs[1 + j], nb), 0), _seq3(outs[1 + len(dils) + j], nb), _seq3(outs[1 + 2 * len(dils) + j], nb))
    in_classes = lambda a, d: _flat2(a) if d == 1 else ("classes", _flat2(a), ATT_W, d)
    parts = []
    for window, dil in DIL_PATTERN:
        (qk_v, vc_v), (do_v, lse_v, delta_v) = views[dil], seen[dil]
        grads = _attn_bwd((qk_v, 0), (qk_v, 1), (vc_v, 0), do_v, lse_v, delta_v, classes=dil, heads=8, dh=HEAD_DIM, mode="band",
                          max_dist=window // dil, dq_dtype=BF16, name=f"cd_bwd_dil{dil}")
        parts.append([in_classes(a, dil) for a in grads])
    dqd, dkd, dvd = _attn_bwd((qd3, 0), (kd3, 0), (vd3, 0), (dyb3, 1), _seq3(lse_d, nb), _seq3(delta_d, nb), heads=8,
                              dh=HEAD_DIM, mode="band", max_dist=SWA_WINDOW - 1, dq_dtype=BF16, name="cd_bwd_swa")
    cos, sin = _rope_tables()
    per_seq = SEQ // ROW_TILE
    table_map = lambda i: (i % per_seq, 0)

    def unrope(q1, q2, q3, k1, k2, k3, v1, v2, v3, qd, kd8, vd8, cos, sin, gather):
        back = lambda x: _rotate(x.astype(F32), cos, -sin)
        kd, vd = _dot_exact(kd8, gather), _dot_exact(vd8, gather)
        return jnp.concatenate([back(q1 + q2 + q3), back(k1 + k2 + k3), v1 + v2 + v3, back(qd), back(kd), vd], axis=1)
    ins = [parts[b][t] for t in range(3) for b in range(3)] + [_flat2(dqd), _flat2(dkd), _flat2(dvd), cos, sin]
    dz, = _rowwise(unrope, ins, [_gqa_share().T], [(CD_IN, BF16)], name="cd_bwd_unrope",
                   row_maps=[None] * 12 + [table_map, table_map])
    dw_in_t, = _matmul(dz, hn, ta=True, outs=(BF16,),tm=384, name="cd_bwd_dwin")
    dh2, dh2b, dg = _norm_input_grad(dz, p["w_in_t"], h, dh, p["norm"], name="cd_bwd_dh")
    return dh2, dh2b, {"norm": dg, "w_in_t": dw_in_t, "sink": dsink[:, :8], "w_out": dw_out}


def _local_step(x, mem, target, w, complete=None, grads_ready=None):
    nb = x.shape[0]
    h = x.reshape(nb * SEQ, D_MODEL)
    mem2 = mem.reshape(nb * MEM_LEN, D_MODEL)
    tgt = target.reshape(nb * SEQ, D_MODEL)

    hn = _rms_fwd(h, w["ab"]["norm"], name="ab_norm")
    h, hn, s_ab = _ab_fwd(h, hn, w["ab"], nb, w["xa0"]["norm"], before_out=complete)
    h, hn, s_xa0 = _xa_fwd(h, hn, mem2, w["xa0"], "xa0", nb, w["mlp0"]["norm"])
    h, hn, s_mlp0 = _mlp_fwd(h, hn, w["mlp0"], "mlp0", w["cd"]["norm"])
    h, hn, s_cd = _cd_fwd(h, hn, w["cd"], nb, w["xa1"]["norm"])
    h, hn, s_xa1 = _xa_fwd(h, hn, mem2, w["xa1"], "xa1", nb, w["mlp1"]["norm"])
    h, _, s_mlp1 = _mlp_fwd(h, hn, w["mlp1"], "mlp1", None)

    dh, dhb, loss, dg_final = _loss_and_grad(h, tgt, w["final_norm"], name="loss")
    grads = {"final_norm": dg_final}
    dh, dhb, grads["mlp1"] = _mlp_bwd(dh, dhb, w["mlp1"], s_mlp1, "mlp1")
    dh, dhb, grads["xa1"] = _xa_bwd(dh, dhb, mem2, w["xa1"], s_xa1, "xa1", nb)
    dh, dhb, grads["cd"] = _cd_bwd(dh, dhb, w["cd"], s_cd, nb)
    if grads_ready is not None:
        grads_ready(0, grads)
    dh, dhb, grads["mlp0"] = _mlp_bwd(dh, dhb, w["mlp0"], s_mlp0, "mlp0")
    dh, dhb, grads["xa0"] = _xa_bwd(dh, dhb, mem2, w["xa0"], s_xa0, "xa0", nb)
    if grads_ready is not None:
        grads_ready(1, grads)
    dh, dhb, grads["ab"] = _ab_bwd(dh, dhb, w["ab"], s_ab, nb)
    return loss, dh.reshape(nb, SEQ, D_MODEL), grads


ANY = pl.BlockSpec(memory_space=pl.ANY)


def _place():
    x, y, c = lax.axis_index("x"), lax.axis_index("y"), lax.axis_index("c")
    return x, y, c, [(1 - x, y), (x, 1 - y), (1 - x, 1 - y)]


def _gather_chips(shard):
    half = shard.shape[0] // 2

    def body(src, out, send_sems, recv_sems):
        x, y, c, chips = _place()
        sibling = (x, y, 1 - c)

        def rows(slot, core):
            return out.at[slot, pl.ds(core * half, half)]

        def copy(k, src_ref, dst_ref, to):
            return pltpu.make_async_remote_copy(src_ref=src_ref, dst_ref=dst_ref, send_sem=send_sems.at[k],
                                                recv_sem=recv_sems.at[k], device_id=to, device_id_type=MESH)

        first = [copy(k, src.at[pl.ds(c * half, half)], rows(2 * x + y, c), (px, py, c)) for k, (px, py) in enumerate(chips)]
        for cp in first:
            cp.start()
        passed = [copy(3 + k, rows(2 * px + py, c), rows(2 * px + py, c), sibling) for k, (px, py) in enumerate(chips)]
        for k, (px, py) in enumerate(chips):
            copy(k, src.at[pl.ds(c * half, half)], rows(2 * px + py, c), (px, py, c)).wait_recv()
            passed[k].start()
        for k, (px, py) in enumerate(chips):
            copy(3 + k, rows(2 * px + py, 1 - c), rows(2 * px + py, 1 - c), sibling).wait_recv()
        for cp in first + passed:
            cp.wait_send()

    return pl.pallas_call(
        body, out_shape=jax.ShapeDtypeStruct((N_CHIPS,) + shard.shape, shard.dtype), in_specs=[ANY], out_specs=ANY,
        scratch_shapes=[pltpu.SemaphoreType.DMA((6,)), pltpu.SemaphoreType.DMA((6,))], name="gather_chips",
    )(shard)


HBM = pl.BlockSpec(memory_space=pltpu.HBM)
SEM = pl.BlockSpec(memory_space=pltpu.SEMAPHORE)
SIDE_EFFECT = pltpu.SideEffectType.DATAFLOW_SIDE_EFFECTING


def _chip_copies(src, land, send_sems, recv_sems):
    half = src.shape[0] // 2
    x, y, c, chips = _place()

    def copy(k, slot, to):
        return pltpu.make_async_remote_copy(src_ref=src.at[pl.ds(c * half, half)], dst_ref=land.at[slot, pl.ds(c * half, half)],
                                            send_sem=send_sems[k], recv_sem=recv_sems[k], device_id=to, device_id_type=MESH)
    out = [copy(k, 2 * x + y, (px, py, c)) for k, (px, py) in enumerate(chips)]
    back = [copy(k, 2 * px + py, (px, py, c)) for k, (px, py) in enumerate(chips)]
    return out, back


def _gather_start(shard, after):
    def body(src, land, *rest):
        rest = rest[len(after):]
        sems, token = rest[:6], rest[8]
        out, _ = _chip_copies(src, land, sems[:3], sems[3:])
        for cp in out:
            cp.start()
        token[...] = jnp.zeros(token.shape, F32)

    sem = pltpu.SemaphoreType.DMA(())
    land_shape = (N_CHIPS,) + shard.shape
    return pl.pallas_call(
        body, name="gather_start",
        out_shape=(sem,) * 6 + (pltpu.HBM(shard.shape, shard.dtype), pltpu.HBM(land_shape, shard.dtype),
                                jax.ShapeDtypeStruct((8, LANES), F32)),
        in_specs=(HBM, HBM) + (pl.BlockSpec(memory_space=pl.ANY),) * len(after),
        out_specs=(SEM,) * 6 + (HBM, HBM, pl.BlockSpec(memory_space=pltpu.VMEM)),
        input_output_aliases={0: 6, 1: 7}, compiler_params=pltpu.CompilerParams(has_side_effects=SIDE_EFFECT),
    )(pltpu.with_memory_space_constraint(shard, pltpu.HBM),
      pltpu.with_memory_space_constraint(lax.empty(land_shape, shard.dtype), pltpu.HBM), *after)


def _gather_wait(started, after):
    sems, src_thru, land_thru = started[:6], started[6], started[7]

    def body(src, land, *rest):
        out, back = _chip_copies(src, land, rest[:3], rest[3:6])
        for cp in out:
            cp.wait_send()
        for cp in back:
            cp.wait_recv()

    return pl.pallas_call(
        body, name="gather_wait",
        out_shape=(pltpu.HBM(src_thru.shape, src_thru.dtype), pltpu.HBM(land_thru.shape, land_thru.dtype)),
        in_specs=(HBM, HBM) + (SEM,) * 6 + (pl.BlockSpec(memory_space=pl.ANY),), out_specs=(HBM, HBM),
        input_output_aliases={0: 0, 1: 1}, compiler_params=pltpu.CompilerParams(has_side_effects=SIDE_EFFECT),
    )(src_thru, land_thru, *sems, after)


def _gather_pass(land):
    half = land.shape[1] // 2

    def body(land_in, land_out, send_sems, recv_sems):
        x, y, c, chips = _place()

        def copy(k, slot, core):
            rows = land_out.at[slot, pl.ds(core * half, half)]
            return pltpu.make_async_remote_copy(src_ref=rows, dst_ref=rows, send_sem=send_sems.at[k], recv_sem=recv_sems.at[k],
                                                device_id=(x, y, 1 - c), device_id_type=MESH)
        sends = [copy(k, 2 * px + py, c) for k, (px, py) in enumerate(chips)]
        for cp in sends:
            cp.start()
        for k, (px, py) in enumerate(chips):
            copy(k, 2 * px + py, 1 - c).wait_recv()
        for cp in sends:
            cp.wait_send()

    return pl.pallas_call(
        body, out_shape=jax.ShapeDtypeStruct(land.shape, land.dtype), in_specs=[ANY], out_specs=ANY,
        scratch_shapes=[pltpu.SemaphoreType.DMA((3,)), pltpu.SemaphoreType.DMA((3,))], input_output_aliases={0: 0},
        name="gather_pass",
    )(land)


def _swap_cores(block, *, name, half_rows=None):
    shape = block.shape if half_rows is None else (block.shape[0], half_rows, block.shape[2])

    def body(src, out, send_sem, recv_sem):
        x, y, c, _ = _place()
        part = src if half_rows is None else src.at[:, pl.ds((1 - c) * half_rows, half_rows)]
        cp = pltpu.make_async_remote_copy(src_ref=part, dst_ref=out, send_sem=send_sem, recv_sem=recv_sem,
                                          device_id=(x, y, 1 - c), device_id_type=MESH)
        cp.start()
        cp.wait()

    return pl.pallas_call(
        body, out_shape=jax.ShapeDtypeStruct(shape, block.dtype), in_specs=[ANY], out_specs=ANY,
        scratch_shapes=[pltpu.SemaphoreType.DMA(()), pltpu.SemaphoreType.DMA(())], name=name,
    )(block)


def _scatter_copies(parts, land, send_sems, recv_sems):
    x, y, c, chips = _place()
    return [pltpu.make_async_remote_copy(src_ref=parts.at[2 * px + py], dst_ref=land.at[k], send_sem=send_sems[k],
                                         recv_sem=recv_sems[k], device_id=(px, py, c), device_id_type=MESH)
            for k, (px, py) in enumerate(chips)]


def _scatter_start(parts, tag):
    def body(src, land, *rest):
        for cp in _scatter_copies(src, land, rest[:3], rest[3:6]):
            cp.start()
        rest[8][...] = jnp.zeros(rest[8].shape, F32)

    sem = pltpu.SemaphoreType.DMA(())
    land_shape = (3,) + parts.shape[1:]
    return pl.pallas_call(
        body, name=f"scatter_start_{tag}",
        out_shape=(sem,) * 6 + (pltpu.HBM(parts.shape, parts.dtype), pltpu.HBM(land_shape, parts.dtype),
                                jax.ShapeDtypeStruct((8, LANES), F32)),
        in_specs=(HBM, HBM), out_specs=(SEM,) * 6 + (HBM, HBM, pl.BlockSpec(memory_space=pltpu.VMEM)),
        input_output_aliases={0: 6, 1: 7}, compiler_params=pltpu.CompilerParams(has_side_effects=SIDE_EFFECT),
    )(pltpu.with_memory_space_constraint(parts, pltpu.HBM),
      pltpu.with_memory_space_constraint(lax.empty(land_shape, parts.dtype), pltpu.HBM))


def _scatter_wait(started, after, tag):
    def body(src, land, *rest):
        for cp in _scatter_copies(src, land, rest[:3], rest[3:6]):
            cp.wait_send()
            cp.wait_recv()

    src_thru, land_thru = started[6], started[7]
    return pl.pallas_call(
        body, name=f"scatter_wait_{tag}",
        out_shape=(pltpu.HBM(src_thru.shape, src_thru.dtype), pltpu.HBM(land_thru.shape, land_thru.dtype)),
        in_specs=(HBM, HBM) + (SEM,) * 6 + (pl.BlockSpec(memory_space=pl.ANY),), out_specs=(HBM, HBM),
        input_output_aliases={0: 0, 1: 1}, compiler_params=pltpu.CompilerParams(has_side_effects=SIDE_EFFECT),
    )(src_thru, land_thru, *started[:6], after)[1]


def _sum_all_devices(vec):
    rows = vec.shape[0]

    def body(x_ref, total_ref, all_ref, send_sems, recv_sems, local_sem):
        x, y, c, chips = _place()
        me, sibling = (x, y, c), (x, y, 1 - c)

        def slot(px, py, pc):
            return all_ref.at[4 * px + 2 * py + pc]

        def copy(k, block, to, src=None):
            return pltpu.make_async_remote_copy(src_ref=slot(*block) if src is None else src, dst_ref=slot(*block),
                                                send_sem=send_sems.at[k], recv_sem=recv_sems.at[k], device_id=to,
                                                device_id_type=MESH)

        mine = pltpu.make_async_copy(x_ref, slot(*me), local_sem)
        mine.start()
        first = [copy(0, me, sibling, src=x_ref)] + [copy(1 + j, me, (*chip, c), src=x_ref) for j, chip in enumerate(chips)]
        for cp in first:
            cp.start()
        passed = [copy(4 + j, (*chip, c), sibling) for j, chip in enumerate(chips)]
        for j, chip in enumerate(chips):
            copy(1 + j, (*chip, c), me).wait_recv()
            passed[j].start()
        copy(0, sibling, me).wait_recv()
        for j, chip in enumerate(chips):
            copy(4 + j, (*chip, 1 - c), me).wait_recv()
        for cp in first + passed:
            cp.wait_send()
        mine.wait()
        acc = all_ref[0]
        for d in range(1, 8):
            acc = acc + all_ref[d]
        total_ref[...] = acc

    vmem = pl.BlockSpec(memory_space=pltpu.VMEM)
    return pl.pallas_call(
        body, out_shape=[jax.ShapeDtypeStruct((rows, LANES), F32), jax.ShapeDtypeStruct((8, rows, LANES), F32)],
        in_specs=[vmem], out_specs=[vmem, vmem],
        scratch_shapes=[pltpu.SemaphoreType.DMA((7,)), pltpu.SemaphoreType.DMA((7,)), pltpu.SemaphoreType.DMA(())],
        name="sum_all_devices", compiler_params=pltpu.CompilerParams(vmem_limit_bytes=VMEM_LIMIT_BYTES),
    )(vec)[0]


PACK_COLS = 1024
BIG = (("mlp_w_up", 2, 1024, True), ("mlp_w_down", 2, 1024, False), ("xa_w_kv", 2, 512, True), ("xa_w_q", 2, 256, False),
       ("xa_w_o", 2, 256, False), ("ab_w_out", 1, 256, False), ("cd_w_out", 1, 256, False), ("cd_w_in", 1, 576, True),
       ("ab_w_in", 1, 642, True))
ENTRIES = tuple((name, layer, rows, transposed) for name, layers, rows, transposed in BIG for layer in range(layers))
FIRST_ENTRIES = tuple(e for e in ENTRIES if e[0] == "ab_w_in")
LATER_ENTRIES = tuple(e for e in ENTRIES if e[0] != "ab_w_in")


def _tile_rows(entry):
    return -(-entry[2] // 16) * 16


def _padded_rows(entries):
    return -(-sum(_tile_rows(e) for e in entries) // 32) * 32


SMALL = (("ab_norm", (1, 1024)), ("ab_conv_w", (1, 4, 512)), ("ab_conv_b", (1, 512)), ("lru_w_a", (1, 8, 64, 64)),
         ("lru_b_a", (1, 512)), ("lru_w_i", (1, 8, 64, 64)), ("lru_b_i", (1, 512)), ("lru_lambda", (1, 512)),
         ("fox_b_f", (1, 8)), ("cd_norm", (1, 1024)), ("cd_sink", (1, 8)), ("xa_norm", (2, 1024)),
         ("xa_mem_norm", (2, 1024)), ("mlp_norm", (2, 1024)), ("final_norm", (1024,)), ("loss", (1,)))
SPLIT_SMALL = ("ab_conv_w", "cd_norm")


def _pack_rows(parts, entries, dtype):
    lead = parts[0].shape[:-2]
    zeros = lambda rows: jnp.zeros(lead + (rows, PACK_COLS), dtype)
    pieces = [jnp.pad(p.astype(dtype), ((0, 0),) * len(lead) + ((0, _tile_rows(e) - e[2]), (0, 0))) for p, e in zip(parts, entries)]
    tail = _padded_rows(entries) - sum(_tile_rows(e) for e in entries)
    return jnp.concatenate(pieces + ([zeros(tail)] if tail else []), axis=len(lead))


def _unpack_rows(packed, entries):
    out, r0 = [], 0
    for e in entries:
        out.append(packed[..., r0:r0 + e[2], :])
        r0 += _tile_rows(e)
    return out


def _shard_rows(w, entries):
    return [(w[name][layer].T if transposed else w[name][layer]) for name, layer, _, transposed in entries]


def _shard_blocks(rows):
    out = {}
    for (name, layer, _, transposed), r in zip(ENTRIES, rows):
        out.setdefault(name, []).append(r.T if transposed else r)
    return {name: jnp.stack(layers) for name, layers in out.items()}


def _pack_small(vals):
    flat = jnp.concatenate([vals[name].astype(F32).reshape(-1) for name, _ in SMALL])
    size = -(-flat.shape[0] // (8 * LANES)) * (8 * LANES)
    return jnp.pad(flat, (0, size - flat.shape[0])).reshape(-1, LANES)


def _unpack_small(packed):
    flat, out, at = packed.reshape(-1), {}, 0
    for name, shape in SMALL:
        out[name] = flat[at:at + math.prod(shape)].reshape(shape)
        at += math.prod(shape)
    return out


def _chip_part(full, chip):
    width = full.shape[-1] // N_CHIPS
    return lax.dynamic_slice_in_dim(full, chip * width, width, axis=full.ndim - 1)


def _chip_embed(part, chip):
    full = jnp.zeros(part.shape[:-1] + (part.shape[-1] * N_CHIPS,), part.dtype)
    return lax.dynamic_update_slice_in_dim(full, part, chip * part.shape[-1], axis=part.ndim - 1)


SUBLAYER_KEYS = {"ab_w_in": ("ab", "w_in_t"), "ab_w_out": ("ab", "w_out"), "cd_w_in": ("cd", "w_in_t"), "cd_w_out": ("cd", "w_out"),
                 "xa_w_q": ("xa", "w_q"), "xa_w_kv": ("xa", "w_kv_t"), "xa_w_o": ("xa", "w_o"), "mlp_w_up": ("mlp", "w_up_t"),
                 "mlp_w_down": ("mlp", "w_down")}


def _sublayer(name, layer):
    block, leaf = SUBLAYER_KEYS[name]
    return (block if block in ("ab", "cd") else f"{block}{layer}"), leaf


def _set_big(params, full_rows):
    for (name, layer), rows in full_rows.items():
        block, leaf = _sublayer(name, layer)
        if name == "ab_w_in":
            rows = jnp.concatenate([rows, jnp.zeros((LANES - 8, PACK_COLS), rows.dtype)])
        params[block][leaf] = rows


def _build_params(full_rows, w):
    pad = lambda a: jnp.pad(a, ((0, 0), (0, LANES - a.shape[1])))
    params = {
        "ab": dict(norm=w["ab_norm"], conv_w=w["ab_conv_w"][0], conv_b=w["ab_conv_b"], w_a=_block_diag(w["lru_w_a"][0]),
                   b_a=w["lru_b_a"], w_i=_block_diag(w["lru_w_i"][0]), b_i=w["lru_b_i"], lam=w["lru_lambda"],
                   b_f=w["fox_b_f"].reshape(8, 1)),
        "cd": dict(norm=w["cd_norm"], sink=pad(w["cd_sink"])),
        "final_norm": w["final_norm"].reshape(1, D_MODEL),
    }
    for layer in range(2):
        params[f"xa{layer}"] = dict(norm=w["xa_norm"][layer:layer + 1], mem_norm=w["xa_mem_norm"][layer:layer + 1])
        params[f"mlp{layer}"] = dict(norm=w["mlp_norm"][layer:layer + 1])
    _set_big(params, full_rows)
    return params


def _grad_rows(g, entries=ENTRIES):
    out = []
    for name, layer, _, _ in entries:
        block, leaf = _sublayer(name, layer)
        out.append(g[block][leaf])
    return out


def _reduce_group(entry):
    name, layer = entry[0], entry[1]
    if name.startswith("ab_"):
        return 2
    return 0 if layer == 1 or name.startswith("cd_") else 1


REDUCE_GROUPS = tuple(tuple(e for e in ENTRIES if _reduce_group(e) == group) for group in range(3))


def _reduce_tile(half):
    return max(t for t in range(16, 1025, 16) if half % t == 0)


def _reduce_start(grads_by_chip, core, chip, tag):
    half = grads_by_chip.shape[1] // 2
    tile = _reduce_tile(half)
    steps = half // tile
    place = jnp.stack([core, chip]).astype(jnp.int32)
    got = _swap_cores(grads_by_chip, name=f"swap_cores_{tag}", half_rows=half)
    block = (1, tile, PACK_COLS)

    def sum_cores(place_ref, mine_ref, got_ref, f32_ref, bf16_ref):
        total = mine_ref[...].astype(F32) + got_ref[...].astype(F32)
        f32_ref[...] = total
        bf16_ref[...] = total.astype(BF16)

    pair32, pair16 = pl.pallas_call(
        sum_cores, name=f"sum_cores_{tag}",
        grid_spec=pltpu.PrefetchScalarGridSpec(
            num_scalar_prefetch=1, grid=(N_CHIPS, steps),
            in_specs=[pl.BlockSpec(block, lambda s, i, place_ref: (s, place_ref[0] * steps + i, 0)),
                      pl.BlockSpec(block, lambda s, i, place_ref: (s, i, 0))],
            out_specs=[pl.BlockSpec(block, lambda s, i, place_ref: (s, i, 0))] * 2),
        out_shape=[jax.ShapeDtypeStruct((N_CHIPS, half, PACK_COLS), F32), jax.ShapeDtypeStruct((N_CHIPS, half, PACK_COLS), BF16)],
        compiler_params=_params("arbitrary", "arbitrary"),
    )(place, grads_by_chip, got)
    return pair32, _scatter_start(pair16, tag), place


def _reduce_finish(state, core, tag, after):
    pair32, started, place = state
    half = pair32.shape[1]
    tile = _reduce_tile(half)
    landed = _scatter_wait(started, after, tag)

    def sum_chips(place_ref, own_ref, landed_ref, out_ref):
        out_ref[...] = own_ref[0] + landed_ref[0].astype(F32) + landed_ref[1].astype(F32) + landed_ref[2].astype(F32)

    done = pl.pallas_call(
        sum_chips, name=f"sum_chips_{tag}",
        grid_spec=pltpu.PrefetchScalarGridSpec(
            num_scalar_prefetch=1, grid=(half // tile,),
            in_specs=[pl.BlockSpec((1, tile, PACK_COLS), lambda i, place_ref: (place_ref[1], i, 0)),
                      pl.BlockSpec((3, tile, PACK_COLS), lambda i, place_ref: (0, i, 0))],
            out_specs=pl.BlockSpec((tile, PACK_COLS), lambda i, place_ref: (i, 0))),
        out_shape=jax.ShapeDtypeStruct((half, PACK_COLS), F32), compiler_params=_params("arbitrary"),
    )(place, pair32, landed)
    theirs = _swap_cores(done, name=f"share_halves_{tag}")
    return jnp.where(core == 0, jnp.concatenate([done, theirs]), jnp.concatenate([theirs, done]))


def kernel(x, mem, ab_norm, ab_w_in, ab_conv_w, ab_conv_b, lru_w_a, lru_b_a, lru_w_i, lru_b_i, lru_lambda, fox_b_f, ab_w_out, cd_norm, cd_w_in, cd_sink, cd_w_out, xa_norm, xa_mem_norm, xa_w_q, xa_w_kv, xa_w_o, mlp_norm, mlp_w_up, mlp_w_down, final_norm, loss_target, m_ab_norm, m_ab_w_in, m_ab_conv_w, m_ab_conv_b, m_lru_w_a, m_lru_b_a, m_lru_w_i, m_lru_b_i, m_lru_lambda, m_fox_b_f, m_ab_w_out, m_cd_norm, m_cd_w_in, m_cd_sink, m_cd_w_out, m_xa_norm, m_xa_mem_norm, m_xa_w_q, m_xa_w_kv, m_xa_w_o, m_mlp_norm, m_mlp_w_up, m_mlp_w_down, m_final_norm, v_ab_norm, v_ab_w_in, v_ab_conv_w, v_ab_conv_b, v_lru_w_a, v_lru_b_a, v_lru_w_i, v_lru_b_i, v_lru_lambda, v_fox_b_f, v_ab_w_out, v_cd_norm, v_cd_w_in, v_cd_sink, v_cd_w_out, v_xa_norm, v_xa_mem_norm, v_xa_w_q, v_xa_w_kv, v_xa_w_o, v_mlp_norm, v_mlp_w_up, v_mlp_w_down, v_final_norm):
    given = dict(locals())
    weight_names = [name for name, _ in SMALL if name != "loss"] + [name for name, _, _, _ in BIG]
    w = {name: given[name] for name in weight_names}
    chip = 2 * lax.axis_index("x") + lax.axis_index("y")
    core = lax.axis_index("c")

    slot = lax.broadcasted_iota(jnp.int32, (N_CHIPS, 1, 1), 0)

    def whole(entries, mine, gathered):
        return {(name, layer): jnp.where(slot == chip, own[None], got).reshape(N_CHIPS * rows, PACK_COLS)
                for (name, layer, rows, _), own, got in zip(entries, _unpack_rows(mine, entries), _unpack_rows(gathered, entries))}

    mine_first = _pack_rows(_shard_rows(w, FIRST_ENTRIES), FIRST_ENTRIES, BF16)
    mine_later = _pack_rows(_shard_rows(w, LATER_ENTRIES), LATER_ENTRIES, BF16)
    first = _gather_chips(mine_first)
    owner = (core == 0).astype(F32)
    quarters = {name: _chip_embed(w[name], chip) * owner for name in SPLIT_SMALL}
    zeros = {name: jnp.zeros(shape, F32) for name, shape in SMALL}
    small_packed = _sum_all_devices(_pack_small({**zeros, **quarters}))
    small_full = _unpack_small(small_packed)
    started = _gather_start(mine_later, after=(small_packed, first))
    params = _build_params(whole(FIRST_ENTRIES, mine_first, first),
                           {**w, "ab_conv_w": small_full["ab_conv_w"], "cd_norm": small_full["cd_norm"]})
    params["ab"]["norm"] = params["ab"]["norm"] + started[8][0, 0]

    def complete(h):
        mine, landed = _gather_wait(started, after=h)
        _set_big(params, whole(LATER_ENTRIES, mine, _gather_pass(landed)))

    reducing = {}

    def grads_ready(group, g):
        entries = REDUCE_GROUPS[group]
        by_chip = _pack_rows([r.reshape(N_CHIPS, e[2], PACK_COLS) for r, e in zip(_grad_rows(g, entries), entries)], entries, BF16)
        reducing[group] = _reduce_start(by_chip, core, chip, f"g{group}")
        if group < 2:
            block, leaf = ("mlp0", "w_down") if group == 0 else ("ab", "w_out")
            params[block][leaf] = params[block][leaf] + reducing[group][1][8][0, 0].astype(BF16)

    loss_part, grad_x, g = _local_step(x, mem, loss_target, params, complete, grads_ready)
    grads_ready(2, g)
    reduced, after = {}, reducing[2][1][8]
    for group, entries in enumerate(REDUCE_GROUPS):
        after = _reduce_finish(reducing[group], core, f"g{group}", after=after)
        reduced.update({(e[0], e[1]): r for e, r in zip(entries, _unpack_rows(after, entries))})
    grads = _shard_blocks([reduced[e[0], e[1]] for e in ENTRIES])
    pair = lambda key, leaf: jnp.stack([g[f"{key}0"][leaf], g[f"{key}1"][leaf]])

    small_local = {"ab_norm": g["ab"]["norm"], "ab_conv_w": g["ab"]["conv_w"], "ab_conv_b": g["ab"]["conv_b"],
                   "lru_w_a": g["ab"]["w_a"], "lru_b_a": g["ab"]["b_a"], "lru_w_i": g["ab"]["w_i"], "lru_b_i": g["ab"]["b_i"],
                   "lru_lambda": g["ab"]["lam"], "fox_b_f": g["ab"]["b_f"], "cd_norm": g["cd"]["norm"], "cd_sink": g["cd"]["sink"],
                   "xa_norm": pair("xa", "norm"), "xa_mem_norm": pair("xa", "mem_norm"), "mlp_norm": pair("mlp", "norm"),
                   "final_norm": g["final_norm"], "loss": loss_part[0, :1]}
    small_sum_packed = _sum_all_devices(_pack_small(small_local))
    small_sum = _unpack_small(small_sum_packed)
    loss = small_sum["loss"][0]

    delta, new_m, new_v = {}, {}, {}
    for name, _, _, _ in BIG:
        shape = w[name].shape
        flat = lambda a: a.reshape(-1, shape[-1])
        d, m2, v2 = _adamw(flat(w[name]), flat(grads[name]), flat(given["m_" + name]), flat(given["v_" + name]), name=f"adamw_{name}")
        delta[name], new_m[name], new_v[name] = d.reshape(shape), m2.reshape(shape), v2.reshape(shape)

    def small_state(prefix):
        vals = {name: (given[prefix + name] if name != "loss" else jnp.zeros((1,), F32)) for name, _ in SMALL}
        for name in SPLIT_SMALL:
            vals[name] = _chip_embed(vals[name], chip)
        return _pack_small(vals)
    packed = _adamw(small_state(""), small_sum_packed, small_state("m_"), small_state("v_"), name="adamw_small")
    for store, vals in zip((delta, new_m, new_v), packed):
        for name, val in _unpack_small(vals).items():
            store[name] = _chip_part(val, chip) if name in SPLIT_SMALL else val
    for name in SPLIT_SMALL:
        small_sum[name] = _chip_part(small_sum[name], chip)
    grads.update({name: small_sum[name] for name, _ in SMALL})

    order = ["ab_norm", "ab_w_in", "ab_conv_w", "ab_conv_b", "lru_w_a", "lru_b_a", "lru_w_i", "lru_b_i", "lru_lambda", "fox_b_f",
             "ab_w_out", "cd_norm", "cd_w_in", "cd_sink", "cd_w_out", "xa_norm", "xa_mem_norm", "xa_w_q", "xa_w_kv", "xa_w_o",
             "mlp_norm", "mlp_w_up", "mlp_w_down", "final_norm"]
    return (loss, grad_x, *[grads[n] for n in order], *[delta[n] for n in order], *[new_m[n] for n in order],
            *[new_v[n] for n in order])
```

```python
import functools
import math

import jax
import jax.numpy as jnp
from jax import lax
from jax.experimental import pallas as pl
from jax.experimental.pallas import tpu as pltpu

F32 = jnp.float32
BF16 = jnp.bfloat16
MESH = pl.DeviceIdType.MESH

D_MODEL = 1024
SEQ = 2048
HEAD_DIM = 64
LRU_WIDTH = 512
LRU_BLOCKS = 8
LRU_C = 8.0
CONV_WIDTH = 4
ATT_W = 512
SWA_KW = 128
SWA_WINDOW = 128
DIL_PATTERN = ((128, 1), (512, 4), (2048, 16))
MEM_LEN = 256
XA_HEADS = 4
XA_HEAD_DIM = 256
D_FF = 4096
ROPE_THETA = 10000.0
EPS = 1e-6
N_CHIPS = 4
LANES = 128

ADAM_LR, ADAM_B1, ADAM_B2, ADAM_EPS, ADAM_WD, ADAM_STEP = 0.001, 0.9, 0.999, 1e-08, 0.01, 10

VMEM_LIMIT_BYTES = 56 * 1024 * 1024
MASKED = -1e30
ROW_TILE = 512
LRU_CHUNK = 512
ATT_BLOCK = 128
BAND_ROWS = 256
ATT_CHUNK = 512
CAUSAL_ROWS = 256
CAUSAL_ROWS_BACKWARD = 512
CLASS_ROWS = 512


def _params(*sem):
    return pltpu.CompilerParams(dimension_semantics=sem, vmem_limit_bytes=VMEM_LIMIT_BYTES)


def _rowwise(fn, rows, consts=(), out_rows=(), out_accs=(), *, name, tile=ROW_TILE, row_maps=None):
    rows = [r if isinstance(r, tuple) else (r, r.shape[1], 0) for r in rows]
    consts = list(consts)
    nr, nc, no = len(rows), len(consts), len(out_rows)
    dealt_in = [r[3] if isinstance(r[0], str) else None for r in rows]
    dealt_out = [o[2] if len(o) == 3 else None for o in out_rows]
    total = next(r[0].shape[0] for r in rows if not isinstance(r[0], str))
    tile = min(tile, total)
    assert total % tile == 0
    n = total // tile
    n_acc = len(out_accs)

    def body(*refs):
        scratch = list(refs[nr + nc + no + n_acc:])
        vals = []
        for ref, d, row in zip(refs[:nr], dealt_in, rows):
            if d is None:
                vals.append(ref[...])
                continue
            sc, width = scratch.pop(0), row[2]
            for r in range(d):
                for c in range(width // LANES):
                    lanes = slice(r * width + c * LANES, r * width + (c + 1) * LANES)
                    sc.at[c][pl.ds(r, tile // d, stride=d), :] = ref[:, lanes].astype(F32)
            vals.append(jnp.concatenate([sc[c] for c in range(width // LANES)], axis=1))
        outs = fn(*vals, *[r[...] for r in refs[nr:nr + nc]])
        outs = tuple(outs) if isinstance(outs, (tuple, list)) else (outs,)
        for ref, val, d, spec in zip(refs[nr + nc:nr + nc + no], outs[:no], dealt_out, out_rows):
            if d is None:
                ref[...] = val.astype(ref.dtype)
                continue
            sc, width = scratch.pop(0), spec[0]
            for c in range(width // LANES):
                sc[c] = val[:, c * LANES:(c + 1) * LANES].astype(F32)
            for r in range(d):
                for c in range(width // LANES):
                    lanes = slice(r * width + c * LANES, r * width + (c + 1) * LANES)
                    ref[:, lanes] = sc.at[c][pl.ds(r, tile // d, stride=d), :].astype(ref.dtype)
        for ref, val in zip(refs[nr + nc + no:nr + nc + no + n_acc], outs[no:]):
            @pl.when(pl.program_id(0) == 0)
            def _(ref=ref):
                ref[...] = jnp.zeros(ref.shape, ref.dtype)
            ref[...] += val

    in_specs, args, scratch_shapes = [], [], []
    for idx, row in enumerate(rows):
        if isinstance(row[0], str):
            _, arr, width, d = row
            in_specs.append(pl.BlockSpec((tile // d, d * width), lambda i: (i, 0)))
            scratch_shapes.append(pltpu.VMEM((width // LANES, tile, LANES), F32))
        elif row_maps is not None and row_maps[idx] is not None:
            arr, width, _ = row
            in_specs.append(pl.BlockSpec((tile, width), row_maps[idx]))
        else:
            arr, width, cb = row
            in_specs.append(pl.BlockSpec((tile, width), functools.partial(lambda i, cb: (i, cb), cb=cb)))
        args.append(arr)
    in_specs += [pl.BlockSpec(c.shape, lambda i: (0, 0)) for c in consts]
    out_shape, out_specs = [], []
    for spec, d in zip(out_rows, dealt_out):
        w, dt = spec[0], spec[1]
        if d is None:
            out_shape.append(jax.ShapeDtypeStruct((total, w), dt))
            out_specs.append(pl.BlockSpec((tile, w), lambda i: (i, 0)))
        else:
            out_shape.append(jax.ShapeDtypeStruct((total // d, d * w), dt))
            out_specs.append(pl.BlockSpec((tile // d, d * w), lambda i: (i, 0)))
            scratch_shapes.append(pltpu.VMEM((w // LANES, tile, LANES), F32))
    out_shape += [jax.ShapeDtypeStruct(s, F32) for s in out_accs]
    out_specs += [pl.BlockSpec(s, lambda i: (0, 0)) for s in out_accs]
    return pl.pallas_call(
        body, grid=(n,), in_specs=in_specs, out_specs=out_specs, out_shape=out_shape, scratch_shapes=scratch_shapes, name=name,
        compiler_params=_params("arbitrary"),
    )(*args, *consts)


MATMUL_VMEM_BYTES = 40 * 1024 * 1024


def _matmul_tiles(m, n, k, tm, tn, out_bytes):
    tm, tn, tk = min(tm, m), min(tn, n), k

    def need():
        return 2 * (2 * tk * (tm + tn) + tm * tn * out_bytes) + (0 if tk == k else 4 * tm * tn)

    while need() > MATMUL_VMEM_BYTES:
        if tm >= tn and tm % 256 == 0:
            tm //= 2
        elif tn % 256 == 0:
            tn //= 2
        else:
            tk //= 2
    return tm, tn, tk


def _matmul(a, b, *, ta=False, tb=False, outs=(F32,), epilogue=None, extras=(), consts=(), sums=(), whole_rows=False,
            tm=1024, tn=1024, name):
    m, k = (a.shape[1], a.shape[0]) if ta else a.shape
    n = b.shape[0] if tb else b.shape[1]
    assert (b.shape[1] if tb else b.shape[0]) == k
    outs = [o if isinstance(o, tuple) else (o, None) for o in outs]
    out_bytes = sum(jnp.dtype(dt).itemsize for dt, w in outs if w is None) + sum(e.dtype.itemsize for e in extras)
    tm, tn, tk = _matmul_tiles(m, n, k, tm, tn, out_bytes)
    assert m % tm == 0 and n % tn == 0 and k % tk == 0, (name, m, n, k)
    nk = k // tk
    ne, nc, no = len(extras), len(consts), len(outs)
    assert tn == n or not (sums or whole_rows or any(w is not None for _, w in outs)), name
    dims = (((0 if ta else 1,), (1 if tb else 0,)), ((), ()))

    def body(*refs):
        a_ref, b_ref = refs[:2]
        e_refs, o_refs = refs[2:2 + ne + nc], refs[2 + ne + nc:2 + ne + nc + no]
        s_refs = refs[2 + ne + nc + no:2 + ne + nc + no + len(sums)]

        def finish(acc):
            vals = (acc,) if epilogue is None else epilogue(acc, *[e[...] for e in e_refs])
            for ref, val in zip(o_refs, vals[:no]):
                ref[...] = val.astype(ref.dtype)
            for ref, val in zip(s_refs, vals[no:]):
                @pl.when(pl.program_id(0) == 0)
                def _(ref=ref, val=val):
                    ref[...] = val

                @pl.when(pl.program_id(0) > 0)
                def _(ref=ref, val=val):
                    ref[...] += val

        prod = lax.dot_general(a_ref[...], b_ref[...], dims, preferred_element_type=F32)
        if nk == 1:
            finish(prod)
        else:
            acc_ref = refs[-1]
            step = pl.program_id(2)

            @pl.when(step == 0)
            def _():
                acc_ref[...] = prod

            @pl.when(step > 0)
            def _():
                acc_ref[...] += prod

            @pl.when(step == nk - 1)
            def _():
                finish(acc_ref[...])

    a_spec = pl.BlockSpec((tk, tm), lambda i, j, s: (s, i)) if ta else pl.BlockSpec((tm, tk), lambda i, j, s: (i, s))
    b_spec = pl.BlockSpec((tn, tk), lambda i, j, s: (j, s)) if tb else pl.BlockSpec((tk, tn), lambda i, j, s: (s, j))
    tile_spec = pl.BlockSpec((tm, tn), lambda i, j, s: (i, j))
    whole = lambda shape: pl.BlockSpec(shape, lambda i, j, s: (0, 0))
    return pl.pallas_call(
        body, grid=(m // tm, n // tn, nk),
        in_specs=[a_spec, b_spec] + [tile_spec] * ne + [whole(c.shape) for c in consts],
        out_specs=[tile_spec if w is None else pl.BlockSpec((tm, w), lambda i, j, s: (i, 0)) for _, w in outs]
        + [whole(shape) for shape in sums],
        out_shape=[jax.ShapeDtypeStruct((m, n if w is None else w), dt) for dt, w in outs]
        + [jax.ShapeDtypeStruct(shape, F32) for shape in sums],
        scratch_shapes=[pltpu.VMEM((tm, tn), F32)] if nk > 1 else [],
        name=name, compiler_params=_params("arbitrary" if sums else "parallel", "parallel", "arbitrary"),
    )(a, b, *extras, *consts)


def _split3(x):
    x1 = x.astype(BF16)
    r1 = x - x1.astype(F32)
    x2 = r1.astype(BF16)
    x3 = (r1 - x2.astype(F32)).astype(BF16)
    return x1, x2, x3


def _dot_exact(x, e):
    return sum(jnp.dot(p, e, preferred_element_type=F32) for p in _split3(x))


def _head_expand(heads, width):
    dh = width // heads
    rows = jnp.arange(LANES)[:, None]
    cols = jnp.arange(width)[None, :]
    return (cols // dh == rows).astype(BF16)


def _rstd(x):
    return lax.rsqrt(jnp.mean(x * x, axis=-1, keepdims=True) + EPS)


def _rms_fwd(x, gain, *, name):
    return _rowwise(lambda x, g: x * _rstd(x) * g, [x], [gain], [(x.shape[1], BF16)], name=name)[0]


def _rms_bwd_vals(x, dhn, gain):
    r = _rstd(x)
    xh = x * r
    dxh = dhn * gain
    dx = r * (dxh - xh * jnp.mean(dxh * xh, axis=-1, keepdims=True))
    return dx, jnp.sum(dhn * xh, axis=0, keepdims=True)


def _norm_input_grad(dz, w, x, dres, gain, *, tb=False, name):
    def epilogue(dhn, x, dres, g):
        dx, dg = _rms_bwd_vals(x, dhn, g)
        dh = dres + dx
        return dh, dh, dg
    return _matmul(dz, w, tb=tb, outs=(F32, BF16), extras=(x, dres), consts=(gain,), sums=((1, x.shape[1]),), epilogue=epilogue,
                   name=name)


def _loss_and_grad(h, target, gain, *, name):
    def fn(h, tgt, g):
        r = _rstd(h)
        err = h * r * g - tgt
        loss = 0.5 * jnp.sum(jnp.mean(err * err, axis=-1, keepdims=True), axis=0, keepdims=True)
        dx, dg = _rms_bwd_vals(h, err * (1.0 / D_MODEL), g)
        return dx, dx, jnp.broadcast_to(loss, (1, LANES)), dg
    d = h.shape[1]
    return _rowwise(fn, [h, target], [gain], [(d, F32), (d, BF16)], [(1, LANES), (1, d)], name=name)


def _adamw(w, g, m, v, *, name):
    rows, cols = w.shape
    tile = rows
    for cand in (512, 256, 128, 64, 32, 16, 8):
        if rows % cand == 0 and rows > cand:
            tile = cand
            break
    bc1 = 1.0 - ADAM_B1 ** ADAM_STEP
    bc2 = 1.0 - ADAM_B2 ** ADAM_STEP

    def fn(w, g, m, v):
        m2 = ADAM_B1 * m + (1.0 - ADAM_B1) * g
        v2 = ADAM_B2 * v + (1.0 - ADAM_B2) * (g * g)
        delta = -ADAM_LR * ((m2 / bc1) / (jnp.sqrt(v2 / bc2) + ADAM_EPS) + ADAM_WD * w)
        return delta, m2, v2
    return _rowwise(fn, [w, g, m, v], [], [(cols, F32)] * 3, name=name, tile=tile)


def _attn_specs(arr, width, cb, classes, rows_block, whole, together=1):
    ncols = arr.shape[2] // (classes * width)
    lanes, at = (width, lambda r: r * ncols + cb) if together == 1 else (together * ncols * width, lambda r: r)
    if whole:
        return pl.BlockSpec((1, arr.shape[1], lanes), lambda n, r, i: (n, 0, at(r)))
    return pl.BlockSpec((1, rows_block, lanes), lambda n, r, i: (n, i, at(r)))


def _class_window(refs, spans, together, cl):
    if together == 1:
        return refs
    return [ref.at[:, :, pl.ds((cl * ncols + cb) * width, width)] for ref, (ncols, cb, width) in zip(refs, spans)]


def _attn_tiles(mode, lq, lk, backward=False):
    if mode == "full":
        rows = min(lq, CAUSAL_ROWS_BACKWARD if backward else CAUSAL_ROWS)
        return rows, rows, lk
    if mode == "band":
        tq = min(BAND_ROWS, lq)
        rows = tq if backward else ATT_BLOCK
        return tq, rows, min(rows + ATT_BLOCK, lk)
    rows = CAUSAL_ROWS_BACKWARD if backward else CAUSAL_ROWS
    return rows, rows, ATT_CHUNK


def _window(mode, row0, j, rows, win, lk):
    if mode == "causal":
        return pl.multiple_of(j * win, win), row0 // win + 1
    if mode == "band":
        return pl.multiple_of(jnp.clip(row0 + rows - win, 0, lk - win), ATT_BLOCK), 1
    return 0, 1


def _allowed(mode, row0, start, rows, win, max_dist):
    if mode == "full":
        return None
    dist = (row0 + lax.broadcasted_iota(jnp.int32, (rows, win), 0)) - (start + lax.broadcasted_iota(jnp.int32, (rows, win), 1))
    ok = dist >= 0
    if max_dist is not None:
        ok = ok & (dist <= max_dist)
    return ok


def _head_groups(heads, dh):
    gw = max(LANES, dh)
    return gw, gw // dh, heads // (gw // dh)


def _own_lanes(rows, gw, dh, u):
    return lax.broadcasted_iota(jnp.int32, (rows, gw), 1) // dh == u


def _only_head(x, own, per, u):
    return x if per == 1 else jnp.where(own[u], x, jnp.zeros_like(x))


def _step_scores(qz, kw, kb_row, ok):
    s = lax.dot_general(qz, kw, (((1,), (1,)), ((), ())), preferred_element_type=F32)
    if kb_row is not None:
        s = s - kb_row
    if ok is not None:
        s = jnp.where(ok, s, MASKED)
    return s


def _attn_fwd(q, k, v, kb=None, *, classes=1, heads, dh, mode, max_dist=None, bf16_copy=False, name):
    (qa, qc), (ka, kc), (va, vc) = q, k, v
    n, lq, lk = qa.shape[0], qa.shape[1], ka.shape[1]
    width = heads * dh
    tq, rows, win = _attn_tiles(mode, lq, lk)
    gw, per, groups = _head_groups(heads, dh)
    scale = dh ** -0.5
    has_kb = kb is not None
    single = mode != "causal"
    together = min(classes, max(1, CLASS_ROWS // lq))
    ncols = lambda arr: arr.shape[2] // (classes * width)
    spans = [(ncols(qa), qc, width), (ncols(ka), kc, width), (ncols(va), vc, width), (1, 0, width), (1, 0, LANES), (1, 0, width)]

    def body(*refs):
        for cl in range(together):
            for sub in range(tq // rows):
                part(_class_window(refs, spans, together, cl), pl.program_id(2) * tq + sub * rows, slice(sub * rows, (sub + 1) * rows))

    def part(refs, row0, rs):
        q_ref, k_ref, v_ref = refs[:3]
        kb_ref = refs[3] if has_kb else None
        n_in = 4 if has_kb else 3
        o_ref, lse_ref = refs[n_in:n_in + 2]
        o16_ref = refs[n_in + 2] if bf16_copy else None
        lane = lax.broadcasted_iota(jnp.int32, (rows, LANES), 1)
        own = [_own_lanes(rows, gw, dh, u) for u in range(per)]

        def step(j, carry, masked=True):
            m_in, l_in = carry
            m_out, l_out = m_in, l_in
            start, _ = _window(mode, row0, j, rows, win, lk)
            ok = _allowed(mode, row0, start, rows, win, max_dist) if masked else None
            for g in range(groups):
                cols = slice(g * gw, (g + 1) * gw)
                qg = q_ref[0, rs, cols] * scale
                kw = k_ref[0, pl.ds(start, win), cols]
                vw = v_ref[0, pl.ds(start, win), cols]
                alpha_g = new_g = None
                for u in range(per):
                    h = g * per + u
                    kb_row = kb_ref[0, h, pl.ds(j, 1), :] if has_kb else None
                    s = _step_scores(_only_head(qg, own, per, u), kw, kb_row, ok)
                    m_new = jnp.max(s, axis=1, keepdims=True)
                    if not single:
                        m_old = m_in[:, h:h + 1]
                        m_new = jnp.maximum(m_old, m_new)
                        alpha = jnp.exp(m_old - m_new)
                        alpha_g = alpha if u == 0 else jnp.where(own[u], alpha, alpha_g)
                    p = jnp.exp(s - m_new)
                    l_new = jnp.sum(p, axis=1, keepdims=True)
                    r = jnp.dot(p.astype(BF16), vw, preferred_element_type=F32)
                    if single:
                        r = r * (1.0 / l_new)
                    else:
                        l_new = alpha * l_in[:, h:h + 1] + l_new
                    new_g = r if u == 0 else jnp.where(own[u], r, new_g)
                    m_out = jnp.where(lane == h, m_new, m_out)
                    l_out = jnp.where(lane == h, l_new, l_out)
                o_ref[0, rs, cols] = new_g if single else alpha_g * o_ref[0, rs, cols] + new_g
                if bf16_copy:
                    o16_ref[0, rs, cols] = new_g.astype(BF16)
            return m_out, l_out

        carry = (jnp.full((rows, LANES), MASKED, F32), jnp.zeros((rows, LANES), F32))
        if single:
            m_all, l_all = step(0, carry)
            l_all = jnp.where(lane < heads, l_all, 1.0)
        else:
            o_ref[...] = jnp.zeros(o_ref.shape, F32)
            last = _window(mode, row0, 0, rows, win, lk)[1] - 1
            m_all, l_all = step(last, lax.fori_loop(0, last, functools.partial(step, masked=False), carry))
            l_all = jnp.where(lane < heads, l_all, 1.0)
            inv = 1.0 / l_all
            for g in range(groups):
                cols = slice(g * gw, (g + 1) * gw)
                inv_g = inv[:, g * per:g * per + 1]
                for u in range(1, per):
                    inv_g = jnp.where(own[u], inv[:, g * per + u:g * per + u + 1], inv_g)
                o_ref[0, rs, cols] = o_ref[0, rs, cols] * inv_g
        lse_ref[0, rs, :] = jnp.where(lane < heads, m_all + jnp.log(l_all), 0.0)

    in_specs = [_attn_specs(qa, width, qc, classes, tq, False, together), _attn_specs(ka, width, kc, classes, win, True, together),
                _attn_specs(va, width, vc, classes, win, True, together)]
    args = [qa, ka, va]
    if has_kb:
        assert together == 1
        in_specs.append(pl.BlockSpec((1,) + kb.shape[1:], lambda n_, r, i: (n_, 0, 0, 0)))
        args.append(kb)
    out_shape = [jax.ShapeDtypeStruct((n, lq, classes * width), F32), jax.ShapeDtypeStruct((n, lq, classes * LANES), F32)]
    out_specs = [pl.BlockSpec((1, tq, together * width), lambda n_, r, i: (n_, i, r)),
                 pl.BlockSpec((1, tq, together * LANES), lambda n_, r, i: (n_, i, r))]
    if bf16_copy:
        assert single
        out_shape.append(jax.ShapeDtypeStruct((n, lq, classes * width), BF16))
        out_specs.append(out_specs[0])
    return pl.pallas_call(
        body, grid=(n, classes // together, lq // tq), in_specs=in_specs, out_specs=out_specs, out_shape=out_shape, name=name,
        compiler_params=_params("parallel", "parallel", "arbitrary"),
    )(*args)


def _attn_bwd(q, k, v, do, lse, delta, kb=None, *, classes=1, heads, dh, mode, max_dist=None, dq_dtype=F32, name):
    (qa, qc), (ka, kc), (va, vc), (da, dc) = q, k, v, do
    n, lq, lk = qa.shape[0], qa.shape[1], ka.shape[1]
    width = heads * dh
    tq, rows, win = _attn_tiles(mode, lq, lk, backward=True)
    gw, per, groups = _head_groups(heads, dh)
    scale = dh ** -0.5
    has_kb = kb is not None
    single = mode != "causal"
    nt = (((1,), (1,)), ((), ()))
    tn = (((0,), (0,)), ((), ()))
    together = min(classes, max(1, CLASS_ROWS // lq))
    ncols = lambda arr: arr.shape[2] // (classes * width)
    spans = [(ncols(qa), qc, width), (ncols(ka), kc, width), (ncols(va), vc, width), (ncols(da), dc, width), (1, 0, LANES),
             (1, 0, LANES), (1, 0, width), (1, 0, width), (1, 0, width)]

    def body(*refs):
        n_in = 7 if has_kb else 6
        dk_ref, dv_ref = refs[n_in + 1:n_in + 3]

        @pl.when(pl.program_id(2) == 0)
        def _():
            dk_ref[...] = jnp.zeros(dk_ref.shape, F32)
            dv_ref[...] = jnp.zeros(dv_ref.shape, F32)
            if has_kb:
                refs[n_in + 3][...] = jnp.zeros(refs[n_in + 3].shape, F32)

        for cl in range(together):
            for sub in range(tq // rows):
                part(_class_window(refs, spans, together, cl), pl.program_id(2) * tq + sub * rows, slice(sub * rows, (sub + 1) * rows))

    def part(refs, row0, rs):
        q_ref, k_ref, v_ref, do_ref, lse_ref, dl_ref = refs[:6]
        kb_ref = refs[6] if has_kb else None
        n_in = 7 if has_kb else 6
        dq_ref, dk_ref, dv_ref = refs[n_in:n_in + 3]
        dkb_ref, drow_ref = refs[n_in + 3:n_in + 5] if has_kb else (None, None)
        lse_all = lse_ref[0, rs, :]
        dl_all = dl_ref[0, rs, :]
        lane = lax.broadcasted_iota(jnp.int32, (rows, LANES), 1)
        own = [_own_lanes(rows, gw, dh, u) for u in range(per)]

        def step(j, drow_all, masked=True):
            start, _ = _window(mode, row0, j, rows, win, lk)
            ok = _allowed(mode, row0, start, rows, win, max_dist) if masked else None
            for g in range(groups):
                cols = slice(g * gw, (g + 1) * gw)
                qg = q_ref[0, rs, cols] * scale
                dog = do_ref[0, rs, cols]
                kw = k_ref[0, pl.ds(start, win), cols]
                vw = v_ref[0, pl.ds(start, win), cols]
                dq_g = dk_w = dv_w = None
                for u in range(per):
                    h = g * per + u
                    qz, doz = _only_head(qg, own, per, u), _only_head(dog, own, per, u)
                    kb_row = kb_ref[0, h, pl.ds(j, 1), :] if has_kb else None
                    p = jnp.exp(_step_scores(qz, kw, kb_row, ok) - lse_all[:, h:h + 1])
                    dp = lax.dot_general(doz, vw, nt, preferred_element_type=F32)
                    ds = p * (dp - dl_all[:, h:h + 1])
                    dsb = ds.astype(BF16)
                    r = jnp.dot(dsb, kw, preferred_element_type=F32)
                    dq_g = r if u == 0 else jnp.where(own[u], r, dq_g)
                    dk_u = lax.dot_general(dsb, qz, tn, preferred_element_type=F32)
                    dv_u = lax.dot_general(p.astype(BF16), doz, tn, preferred_element_type=F32)
                    dk_w = dk_u if u == 0 else dk_w + dk_u
                    dv_w = dv_u if u == 0 else dv_w + dv_u
                    if has_kb:
                        dkb_ref[0, h, pl.ds(j, 1), :] += -jnp.sum(ds, axis=0, keepdims=True)
                        drow_all = drow_all + jnp.where(lane == h, jnp.sum(ds, axis=1, keepdims=True), 0.0)
                dk_ref[0, pl.ds(start, win), cols] += dk_w
                dv_ref[0, pl.ds(start, win), cols] += dv_w
                if single:
                    dq_ref[0, rs, cols] = (dq_g * scale).astype(dq_ref.dtype)
                else:
                    dq_ref[0, rs, cols] += dq_g * scale
            return drow_all

        drow_all = jnp.zeros((rows, LANES), F32)
        if single:
            drow_all = step(0, drow_all)
        else:
            dq_ref[...] = jnp.zeros(dq_ref.shape, F32)
            last = _window(mode, row0, 0, rows, win, lk)[1] - 1
            drow_all = step(last, lax.fori_loop(0, last, functools.partial(step, masked=False), drow_all))
        if has_kb:
            drow_ref[0, rs, :] = drow_all

    row_spec = functools.partial(_attn_specs, classes=classes, rows_block=tq, whole=False, together=together)
    in_specs = [row_spec(qa, width, qc), _attn_specs(ka, width, kc, classes, win, True, together),
                _attn_specs(va, width, vc, classes, win, True, together), row_spec(da, width, dc), row_spec(lse, LANES, 0),
                row_spec(delta, LANES, 0)]
    args = [qa, ka, va, da, lse, delta]
    assert single or dq_dtype == F32
    out_shape = [jax.ShapeDtypeStruct((n, lq, classes * width), dq_dtype), jax.ShapeDtypeStruct((n, lk, classes * width), F32),
                 jax.ShapeDtypeStruct((n, lk, classes * width), F32)]
    kv_spec = pl.BlockSpec((1, lk, together * width), lambda n_, r, i: (n_, 0, r))
    out_specs = [pl.BlockSpec((1, tq, together * width), lambda n_, r, i: (n_, i, r)), kv_spec, kv_spec]
    if has_kb:
        assert together == 1
        kb_spec = pl.BlockSpec((1,) + kb.shape[1:], lambda n_, r, i: (n_, 0, 0, 0))
        in_specs.append(kb_spec)
        args.append(kb)
        out_shape += [jax.ShapeDtypeStruct(kb.shape, F32), jax.ShapeDtypeStruct((n, lq, LANES), F32)]
        out_specs += [kb_spec, pl.BlockSpec((1, tq, LANES), lambda n_, r, i: (n_, i, 0))]
    return pl.pallas_call(
        body, grid=(n, classes // together, lq // tq), in_specs=in_specs, out_specs=out_specs, out_shape=out_shape, name=name,
        compiler_params=_params("parallel", "parallel", "arbitrary"),
    )(*args)


def _rope_tables():
    half = HEAD_DIM // 2
    inv = ROPE_THETA ** (-jnp.arange(half, dtype=F32) / half)
    ang = jnp.arange(SEQ, dtype=F32)[:, None] * inv[None, :]
    cos = jnp.tile(jnp.cos(ang), (1, 2 * ATT_W // HEAD_DIM))
    sin = jnp.tile(jnp.concatenate([-jnp.sin(ang), jnp.sin(ang)], axis=1), (1, ATT_W // HEAD_DIM))
    return cos, sin


def _rotate(x, cos, sin):
    width = x.shape[1]
    lane = lax.broadcasted_iota(jnp.int32, x.shape, 1)
    first_half = (lane % HEAD_DIM) < (HEAD_DIM // 2)
    swapped = jnp.where(first_half, pltpu.roll(x, width - HEAD_DIM // 2, axis=1), pltpu.roll(x, HEAD_DIM // 2, axis=1))
    return x * cos[:, :width] + swapped * sin[:, :width]


def _gelu(x):
    k = math.sqrt(2.0 / math.pi)
    t = jnp.tanh(k * (x + 0.044715 * x * x * x))
    return 0.5 * x * (1.0 + t), t


def _gelu_grad(x, t):
    k = math.sqrt(2.0 / math.pi)
    return 0.5 * (1.0 + t) + 0.5 * x * (1.0 - t * t) * k * (1.0 + 3.0 * 0.044715 * x * x)


def _shift_down(x, halo, s):
    ext = jnp.concatenate([halo, x], axis=0)
    return pltpu.roll(ext, s, axis=0)[8:, :]


def _shift_up(x, halo, s):
    rows = x.shape[0]
    ext = jnp.concatenate([x, halo], axis=0)
    return pltpu.roll(ext, rows + 8 - s, axis=0)[:rows, :]


def _lru_gates(u, halo, cw, cb, wa, ba, wi, bi, lam):
    taps = [_shift_down(u, halo, CONV_WIDTH - 1 - k) for k in range(CONV_WIDTH - 1)] + [u]
    uc = cb + sum(cw[k:k + 1, :] * taps[k] for k in range(CONV_WIDTH))
    ucb = uc.astype(BF16)
    r = jax.nn.sigmoid(jnp.dot(ucb, wa, preferred_element_type=F32) + ba)
    gi = jax.nn.sigmoid(jnp.dot(ucb, wi, preferred_element_type=F32) + bi)
    sp = jnp.maximum(-lam, 0.0) + jnp.log(1.0 + jnp.exp(-jnp.abs(lam)))
    log_a = -LRU_C * r * sp
    a = jnp.exp(log_a)
    x2 = 2.0 * log_a
    one_minus_a2 = jnp.where(x2 > -0.01, -x2 * (1.0 + x2 * (0.5 + x2 * (1.0 / 6.0 + x2 / 24.0))), 1.0 - jnp.exp(x2))
    mult = jnp.sqrt(one_minus_a2)
    return taps, uc, ucb, r, gi, sp, a, mult


def _lru_specs(nchunk, reverse):
    def chunk(b, c):
        return b * nchunk + ((nchunk - 1 - c) if reverse else c)

    def halo_before(b, c):
        return jnp.maximum(chunk(b, c) * (LRU_CHUNK // 8) - 1, 0)

    return chunk, halo_before


def _lru_fwd(zug, cw, cb, wa, ba, wi, bi, lam, *, name):
    total = zug.shape[0]
    nchunk = SEQ // LRU_CHUNK
    w = LRU_WIDTH
    chunk, halo_before = _lru_specs(nchunk, False)

    def body(u_ref, uh_ref, g_ref, cw_ref, cb_ref, wa_ref, ba_ref, wi_ref, bi_ref, lam_ref, y_ref, h_ref, a_sc, x_sc, carry_sc):
        c = pl.program_id(1)
        u = u_ref[...]
        halo = jnp.where(c > 0, uh_ref[...], 0.0)
        _, uc, _, _, gi, _, a, mult = _lru_gates(u, halo, cw_ref[...], cb_ref[...], wa_ref[...], ba_ref[...],
                                                 wi_ref[...], bi_ref[...], lam_ref[...])
        a_sc[...] = a
        x_sc[...] = mult * (gi * uc)

        @pl.when(c == 0)
        def _():
            carry_sc[...] = jnp.zeros(carry_sc.shape, F32)

        def tile_step(t, h):
            r0 = pl.multiple_of(t * 8, 8)
            at = a_sc[pl.ds(r0, 8), :]
            xt = x_sc[pl.ds(r0, 8), :]
            rows = []
            for j in range(8):
                h = at[j:j + 1, :] * h + xt[j:j + 1, :]
                rows.append(h)
            h_ref[pl.ds(r0, 8), :] = jnp.concatenate(rows, axis=0)
            return h

        h_last = lax.fori_loop(0, LRU_CHUNK // 8, tile_step, carry_sc[0:1, :])
        carry_sc[0:1, :] = h_last
        gel, _ = _gelu(g_ref[...])
        y_ref[...] = (h_ref[...] * gel).astype(BF16)

    small = lambda arr: pl.BlockSpec(arr.shape, lambda b, c: (0, 0))
    return pl.pallas_call(
        body, grid=(total // SEQ, nchunk),
        in_specs=[pl.BlockSpec((LRU_CHUNK, w), lambda b, c: (chunk(b, c), 0)),
                  pl.BlockSpec((8, w), lambda b, c: (halo_before(b, c), 0)),
                  pl.BlockSpec((LRU_CHUNK, w), lambda b, c: (chunk(b, c), 1)),
                  small(cw), small(cb), small(wa), small(ba), small(wi), small(bi), small(lam)],
        out_specs=[pl.BlockSpec((LRU_CHUNK, w), lambda b, c: (chunk(b, c), 0))] * 2,
        out_shape=[jax.ShapeDtypeStruct((total, w), BF16), jax.ShapeDtypeStruct((total, w), F32)],
        scratch_shapes=[pltpu.VMEM((LRU_CHUNK, w), F32), pltpu.VMEM((LRU_CHUNK, w), F32), pltpu.VMEM((8, w), F32)],
        name=name, compiler_params=_params("arbitrary", "arbitrary"),
    )(zug, zug, zug, cw, cb, wa, ba, wi, bi, lam)


def _lru_bwd(zug, hl, dy, cw, cb, wa, ba, wi, bi, lam, *, name):
    total = zug.shape[0]
    nchunk = SEQ // LRU_CHUNK
    w = LRU_WIDTH
    chunk, halo_before = _lru_specs(nchunk, True)
    tn = (((0,), (0,)), ((), ()))
    nt = (((1,), (1,)), ((), ()))

    def body(u_ref, uh_ref, g_ref, hl_ref, hh_ref, dy_ref, cw_ref, cb_ref, wa_ref, ba_ref, wi_ref, bi_ref, lam_ref,
             dz_ref, dcw_ref, dcb_ref, dwa_ref, dba_ref, dwi_ref, dbi_ref, dlam_ref,
             a_sc, d_sc, g_sc, gcarry_sc, acarry_sc, duc_sc):
        b, c = pl.program_id(0), pl.program_id(1)
        first_chunk = c == nchunk - 1

        @pl.when((b == 0) & (c == 0))
        def _():
            for ref in (dcw_ref, dcb_ref, dwa_ref, dba_ref, dwi_ref, dbi_ref, dlam_ref):
                ref[...] = jnp.zeros(ref.shape, F32)

        @pl.when(c == 0)
        def _():
            gcarry_sc[...] = jnp.zeros(gcarry_sc.shape, F32)
            acarry_sc[...] = jnp.zeros(acarry_sc.shape, F32)
            duc_sc[...] = jnp.zeros(duc_sc.shape, F32)

        u = u_ref[...]
        halo = jnp.where(first_chunk, 0.0, uh_ref[...])
        cw, wa, wi, lam = cw_ref[...], wa_ref[...], wi_ref[...], lam_ref[...]
        taps, uc, ucb, r, gi, sp, a, mult = _lru_gates(u, halo, cw, cb_ref[...], wa, ba_ref[...], wi, bi_ref[...], lam)
        gate = g_ref[...]
        gel, th = _gelu(gate)
        dy = dy_ref[...]
        hl = hl_ref[...]
        a_sc[...] = a
        d_sc[...] = dy * gel
        dgate = dy * hl * _gelu_grad(gate, th)

        def tile_step(t, carry):
            g_next, a_next = carry
            r0 = pl.multiple_of((LRU_CHUNK // 8 - 1 - t) * 8, 8)
            dt = d_sc[pl.ds(r0, 8), :]
            at = a_sc[pl.ds(r0, 8), :]
            rows = [None] * 8
            for j in reversed(range(8)):
                g_next = dt[j:j + 1, :] + a_next * g_next
                a_next = at[j:j + 1, :]
                rows[j] = g_next
            g_sc[pl.ds(r0, 8), :] = jnp.concatenate(rows, axis=0)
            return g_next, a_next

        g_last, a_last = lax.fori_loop(0, LRU_CHUNK // 8, tile_step, (gcarry_sc[0:1, :], acarry_sc[0:1, :]))
        gcarry_sc[0:1, :] = g_last
        acarry_sc[0:1, :] = a_last

        gs = g_sc[...]
        hl_halo = jnp.where(first_chunk, 0.0, hh_ref[...])
        da = gs * _shift_down(hl, hl_halo, 1)
        dmult = gs * gi * uc
        dgi = gs * mult * uc
        duc = gs * mult * gi
        dlog_a = da * a - dmult * a * a / mult
        dr = dlog_a * (-LRU_C * sp)
        dsp = jnp.sum(dlog_a * (-LRU_C * r), axis=0, keepdims=True)
        dlam_ref[...] += -dsp * jax.nn.sigmoid(-lam)
        dpa = dr * r * (1.0 - r)
        dpi = dgi * gi * (1.0 - gi)
        dpab, dpib = dpa.astype(BF16), dpi.astype(BF16)
        dba_ref[...] += jnp.sum(dpa, axis=0, keepdims=True)
        dbi_ref[...] += jnp.sum(dpi, axis=0, keepdims=True)
        dwa_ref[...] += lax.dot_general(ucb, dpab, tn, preferred_element_type=F32)
        dwi_ref[...] += lax.dot_general(ucb, dpib, tn, preferred_element_type=F32)
        duc = duc + lax.dot_general(dpab, wa, nt, preferred_element_type=F32)
        duc = duc + lax.dot_general(dpib, wi, nt, preferred_element_type=F32)
        dcb_ref[...] += jnp.sum(duc, axis=0, keepdims=True)
        dcw_ref[...] += jnp.concatenate([jnp.sum(duc * taps[k], axis=0, keepdims=True) for k in range(CONV_WIDTH)], axis=0)
        after = duc_sc[...]
        du = cw[CONV_WIDTH - 1:CONV_WIDTH, :] * duc
        for k in range(CONV_WIDTH - 1):
            du = du + cw[k:k + 1, :] * _shift_up(duc, after, CONV_WIDTH - 1 - k)
        duc_sc[...] = duc[0:8, :]
        dz_ref[:, 0:w] = du.astype(BF16)
        dz_ref[:, w:2 * w] = dgate.astype(BF16)

    small = lambda arr: pl.BlockSpec(arr.shape, lambda b, c: (0, 0))
    acc = lambda shape: pl.BlockSpec(shape, lambda b, c: (0, 0))
    chunk_spec = lambda cb_: pl.BlockSpec((LRU_CHUNK, w), lambda b, c: (chunk(b, c), cb_))
    halo_spec = pl.BlockSpec((8, w), lambda b, c: (halo_before(b, c), 0))
    acc_shapes = [(CONV_WIDTH, w), (1, w), (w, w), (1, w), (w, w), (1, w), (1, w)]
    return pl.pallas_call(
        body, grid=(total // SEQ, nchunk),
        in_specs=[chunk_spec(0), halo_spec, chunk_spec(1), chunk_spec(0), halo_spec, chunk_spec(0),
                  small(cw), small(cb), small(wa), small(ba), small(wi), small(bi), small(lam)],
        out_specs=[pl.BlockSpec((LRU_CHUNK, 2 * w), lambda b, c: (chunk(b, c), 0))] + [acc(s) for s in acc_shapes],
        out_shape=[jax.ShapeDtypeStruct((total, 2 * w), BF16)] + [jax.ShapeDtypeStruct(s, F32) for s in acc_shapes],
        scratch_shapes=[pltpu.VMEM((LRU_CHUNK, w), F32)] * 3 + [pltpu.VMEM((8, w), F32)] * 3,
        name=name, compiler_params=_params("arbitrary", "arbitrary"),
    )(zug, zug, zug, hl, hl, dy, cw, cb, wa, ba, wi, bi, lam)


def _block_diag(wb):
    eye = jnp.eye(LRU_BLOCKS, dtype=wb.dtype)
    return jnp.einsum("gcd,gh->gchd", wb, eye).reshape(LRU_WIDTH, LRU_WIDTH).astype(BF16)


def _diag_blocks(wd):
    blk = LRU_WIDTH // LRU_BLOCKS
    return jnp.stack([wd[g * blk:(g + 1) * blk, g * blk:(g + 1) * blk] for g in range(LRU_BLOCKS)])


FOX_SCAN = 256


def _fox_bias(fl, bf, *, name):
    nb = fl.shape[0]

    def body(fl_ref, bf_ref, c_ref):
        x = fl_ref[0] + bf_ref[...]
        logf = jnp.minimum(x, 0.0) - jnp.log(1.0 + jnp.exp(-jnp.abs(x)))
        upper = (lax.broadcasted_iota(jnp.int32, (FOX_SCAN, FOX_SCAN), 0)
                 <= lax.broadcasted_iota(jnp.int32, (FOX_SCAN, FOX_SCAN), 1)).astype(BF16)
        carry = jnp.zeros((LRU_BLOCKS, 1), F32)
        for j in range(SEQ // FOX_SCAN):
            blk = logf[:, j * FOX_SCAN:(j + 1) * FOX_SCAN]
            c_ref[0, :, j * FOX_SCAN:(j + 1) * FOX_SCAN] = _dot_exact(blk, upper) + carry
            carry = carry + jnp.sum(blk, axis=1, keepdims=True)

    return pl.pallas_call(
        body, grid=(nb,),
        in_specs=[pl.BlockSpec((1, 8, SEQ), lambda b: (b, 0, 0)), pl.BlockSpec((8, 1), lambda b: (0, 0))],
        out_specs=pl.BlockSpec((1, 8, SEQ), lambda b: (b, 0, 0)),
        out_shape=jax.ShapeDtypeStruct((nb, 8, SEQ), F32), name=name, compiler_params=_params("arbitrary"),
    )(fl, bf)


def _fox_bias_bwd(fl, bf, dc, *, name):
    nb = fl.shape[0]

    def body(fl_ref, bf_ref, dc_ref, dfl_ref, dbf_ref):
        @pl.when(pl.program_id(0) == 0)
        def _():
            dbf_ref[...] = jnp.zeros(dbf_ref.shape, F32)

        x = fl_ref[0] + bf_ref[...]
        dc_all = dc_ref[0]
        lower = (lax.broadcasted_iota(jnp.int32, (FOX_SCAN, FOX_SCAN), 0)
                 >= lax.broadcasted_iota(jnp.int32, (FOX_SCAN, FOX_SCAN), 1)).astype(BF16)
        carry = jnp.zeros((LRU_BLOCKS, 1), F32)
        total = jnp.zeros((LRU_BLOCKS, 1), F32)
        for j in reversed(range(SEQ // FOX_SCAN)):
            cols = slice(j * FOX_SCAN, (j + 1) * FOX_SCAN)
            blk = dc_all[:, cols]
            dlogf = _dot_exact(blk, lower) + carry
            carry = carry + jnp.sum(blk, axis=1, keepdims=True)
            dx = dlogf * jax.nn.sigmoid(-x[:, cols])
            dfl_ref[0, :, cols] = dx
            total = total + jnp.sum(dx, axis=1, keepdims=True)
        dbf_ref[...] += total

    return pl.pallas_call(
        body, grid=(nb,),
        in_specs=[pl.BlockSpec((1, 8, SEQ), lambda b: (b, 0, 0)), pl.BlockSpec((8, 1), lambda b: (0, 0)),
                  pl.BlockSpec((1, 8, SEQ), lambda b: (b, 0, 0))],
        out_specs=[pl.BlockSpec((1, 8, SEQ), lambda b: (b, 0, 0)), pl.BlockSpec((8, 1), lambda b: (0, 0))],
        out_shape=[jax.ShapeDtypeStruct((nb, 8, SEQ), F32), jax.ShapeDtypeStruct((8, 1), F32)],
        name=name, compiler_params=_params("arbitrary"),
    )(fl, bf, dc)


def _seq3(a, nb):
    return a.reshape(nb, a.shape[0] // nb, a.shape[1])


def _flat2(a):
    return a.reshape(a.shape[0] * a.shape[1], a.shape[2])


def _residual_out(y, w, h, next_gain, *, name):
    if next_gain is None:
        out, = _matmul(y, w, extras=(h,), epilogue=lambda acc, res: (acc + res,), name=name)
        return out, None

    def epilogue(acc, res, g):
        out = acc + res
        return out, out * _rstd(out) * g
    return _matmul(y, w, outs=(F32, BF16), extras=(h,), consts=(next_gain,), epilogue=epilogue, whole_rows=True, name=name)


def _mlp_fwd(h, hn, p, tag, next_gain):
    act, = _matmul(hn, p["w_up_t"], tb=True, outs=(BF16,), epilogue=lambda acc: (jnp.square(jnp.maximum(acc, 0.0)),),
                   name=f"{tag}_up")
    out, hn_next = _residual_out(act, p["w_down"], h, next_gain, name=f"{tag}_down")
    return out, hn_next, (h, hn, act)


def _mlp_bwd(dh, dhb, p, saved, tag):
    h, hn, act = saved
    dup, = _matmul(dhb, p["w_down"], tb=True, outs=(BF16,), extras=(act,),
                   epilogue=lambda acc, act: (acc * (2.0 * jnp.sqrt(act).astype(F32)),), name=f"{tag}_bwd_dup")
    dw_down, = _matmul(act, dhb, ta=True, outs=(BF16,),name=f"{tag}_bwd_dwdown")
    dw_up_t, = _matmul(dup, hn, ta=True, outs=(BF16,),name=f"{tag}_bwd_dwup")
    dh2, dh2b, dg = _norm_input_grad(dup, p["w_up_t"], h, dh, p["norm"], name=f"{tag}_bwd_dh")
    return dh2, dh2b, {"norm": dg, "w_up_t": dw_up_t, "w_down": dw_down}


def _xa_fwd(h, hn, mem, p, tag, nb, next_gain):
    mem_n = _rms_fwd(mem, p["mem_norm"], name=f"{tag}_memnorm")
    q, = _matmul(hn, p["w_q"], outs=(BF16,), name=f"{tag}_q")
    kv, = _matmul(mem_n, p["w_kv_t"], tb=True, outs=(BF16,), name=f"{tag}_kv")
    q3, kv3 = _seq3(q, nb), _seq3(kv, nb)
    o, lse, ob = _attn_fwd((q3, 0), (kv3, 0), (kv3, 1), heads=XA_HEADS, dh=XA_HEAD_DIM, mode="full", bf16_copy=True,
                           name=f"{tag}_attn")
    o, ob = _flat2(o), _flat2(ob)
    out, hn_next = _residual_out(ob, p["w_o"], h, next_gain, name=f"{tag}_o")
    return out, hn_next, (h, hn, mem_n, q3, kv3, o, ob, lse)


def _xa_bwd(dh, dhb, mem, p, saved, tag, nb):
    h, hn, mem_n, q3, kv3, o, ob, lse = saved
    dob, delta = _matmul(dhb, p["w_o"], tb=True, outs=(BF16, (F32, LANES)), extras=(o,), consts=(_head_expand(XA_HEADS, D_MODEL).T,),
                         epilogue=lambda acc, o, e: (acc, _dot_exact(acc * o, e)), name=f"{tag}_bwd_do")
    dw_o, = _matmul(ob, dhb, ta=True, outs=(BF16,),name=f"{tag}_bwd_dwo")
    dq, dk, dv = _attn_bwd((q3, 0), (kv3, 0), (kv3, 1), (_seq3(dob, nb), 0), lse, _seq3(delta, nb), heads=XA_HEADS,
                           dh=XA_HEAD_DIM, mode="full", dq_dtype=BF16, name=f"{tag}_bwd_attn")
    dqb = _flat2(dq)
    dkvb, = _rowwise(lambda a, b: jnp.concatenate([a, b], axis=1), [_flat2(dk), _flat2(dv)], [], [(2 * D_MODEL, BF16)],
                     name=f"{tag}_bwd_dkvcast")
    dw_q, = _matmul(hn, dqb, ta=True, outs=(BF16,),name=f"{tag}_bwd_dwq")
    dw_kv_t, = _matmul(dkvb, mem_n, ta=True, outs=(BF16,),name=f"{tag}_bwd_dwkv")
    dmem_n, = _matmul(dkvb, p["w_kv_t"], name=f"{tag}_bwd_dmemn")
    dg_mem, = _rowwise(lambda x, d, g: _rms_bwd_vals(x, d, g)[1], [mem, dmem_n], [p["mem_norm"]], [], [(1, D_MODEL)],
                       name=f"{tag}_bwd_memnorm")
    dh2, dh2b, dg = _norm_input_grad(dqb, p["w_q"], h, dh, p["norm"], tb=True, name=f"{tag}_bwd_dh")
    return dh2, dh2b, {"norm": dg, "mem_norm": dg_mem, "w_q": dw_q, "w_kv_t": dw_kv_t, "w_o": dw_o}


AB_MAIN = 2 * LRU_WIDTH + 3 * ATT_W


def _ab_fwd(h, hn, p, nb, next_gain, before_out=None):
    w_in_t = p["w_in_t"]
    zug, = _matmul(hn, w_in_t[:2 * LRU_WIDTH], tb=True, name="ab_in_ug")
    qkv, = _matmul(hn, w_in_t[2 * LRU_WIDTH:AB_MAIN], tb=True, outs=(BF16,), tn=768, name="ab_in_qkv")
    zf, = _matmul(hn, w_in_t[AB_MAIN:], tb=True, name="ab_in_f")
    fl = zf[:, :8].reshape(nb, SEQ, 8).transpose(0, 2, 1)
    cbias = _fox_bias(fl, p["b_f"], name="ab_fox_bias")
    kb = cbias.reshape(nb, 8, SEQ // ATT_CHUNK, ATT_CHUNK)
    y_a, hl = _lru_fwd(zug, p["conv_w"], p["conv_b"], p["w_a"], p["b_a"], p["w_i"], p["b_i"], p["lam"], name="ab_lru")
    qkv3 = _seq3(qkv, nb)
    o, lse = _attn_fwd((qkv3, 0), (qkv3, 1), (qkv3, 2), kb, heads=8, dh=HEAD_DIM, mode="causal", name="ab_fox")
    o = _flat2(o)
    y, = _rowwise(lambda a, b: jnp.concatenate([a, b.astype(BF16)], axis=1), [y_a, o], [], [(D_MODEL, BF16)], name="ab_ycat")
    if before_out is not None:
        before_out(y)
    out, hn_next = _residual_out(y, p["w_out"], h, next_gain, name="ab_out")
    return out, hn_next, (h, hn, zug, qkv3, fl, kb, hl, o, lse, y)


def _ab_bwd(dh, dhb, p, saved, nb):
    h, hn, zug, qkv3, fl, kb, hl, o, lse, y = saved
    dy, dyb, delta = _matmul(dhb, p["w_out"], tb=True, outs=(F32, BF16, (F32, LANES)), extras=(y,), consts=(_head_expand(8, ATT_W).T,),
                             epilogue=lambda acc, y, e: (acc, acc, _dot_exact(acc[:, ATT_W:] * y[:, ATT_W:].astype(F32), e)),
                             name="ab_bwd_dy")
    dw_out, = _matmul(y, dhb, ta=True, outs=(BF16,),name="ab_bwd_dwout")
    dzug, dcw, dcb, dwa, dba, dwi, dbi, dlam = _lru_bwd(zug, hl, dy, p["conv_w"], p["conv_b"], p["w_a"], p["b_a"],
                                                        p["w_i"], p["b_i"], p["lam"], name="ab_bwd_lru")
    dq, dk, dv, dkb, drow = _attn_bwd((qkv3, 0), (qkv3, 1), (qkv3, 2), (_seq3(dyb, nb), 1), lse, _seq3(delta, nb), kb,
                                      heads=8, dh=HEAD_DIM, mode="causal", name="ab_bwd_fox")
    dcum = dkb.reshape(nb, 8, SEQ) + drow[:, :, :8].transpose(0, 2, 1)
    dfl, dbf = _fox_bias_bwd(fl, p["b_f"], dcum, name="ab_bwd_fox_bias")
    dzf = jnp.pad(dfl.transpose(0, 2, 1).reshape(nb * SEQ, 8), ((0, 0), (0, LANES - 8)))
    dz, = _rowwise(lambda a, q, k, v, f: jnp.concatenate([a, q.astype(BF16), k.astype(BF16), v.astype(BF16), f.astype(BF16)], axis=1),
                   [dzug, _flat2(dq), _flat2(dk), _flat2(dv), dzf], [], [(AB_MAIN + LANES, BF16)], name="ab_bwd_dzcat")
    dw_in_t, = _matmul(dz, hn, ta=True, outs=(BF16,),tm=384, name="ab_bwd_dwin")
    dh2, dh2b, dg = _norm_input_grad(dz, p["w_in_t"], h, dh, p["norm"], name="ab_bwd_dh")
    grads = {"norm": dg, "w_in_t": dw_in_t[:AB_MAIN + 8], "conv_w": dcw, "conv_b": dcb, "w_a": _diag_blocks(dwa), "b_a": dba,
             "w_i": _diag_blocks(dwi), "b_i": dbi, "lam": dlam, "b_f": dbf.reshape(1, 8), "w_out": dw_out}
    return dh2, dh2b, grads


CD_IN = 3 * ATT_W + ATT_W + 2 * SWA_KW


def _gqa_share():
    src = jnp.arange(SWA_KW)[:, None]
    dst = jnp.arange(ATT_W)[None, :]
    per_kv = ATT_W // (SWA_KW // HEAD_DIM)
    return ((dst // per_kv == src // HEAD_DIM) & (dst % HEAD_DIM == src % HEAD_DIM)).astype(BF16)


def _cd_fwd(h, hn, p, nb, next_gain):
    z, = _matmul(hn, p["w_in_t"], tb=True, tn=1152, name="cd_in")
    cos, sin = _rope_tables()
    per_seq = SEQ // ROW_TILE
    table_map = lambda i: (i % per_seq, 0)

    def rope_fn(z, cos, sin, share):
        qc, kc, vc = z[:, 0:ATT_W], z[:, ATT_W:2 * ATT_W], z[:, 2 * ATT_W:3 * ATT_W]
        qd, kd, vd = z[:, 3 * ATT_W:4 * ATT_W], z[:, 4 * ATT_W:4 * ATT_W + SWA_KW], z[:, 4 * ATT_W + SWA_KW:]
        kd8 = jnp.dot(_rotate(kd, cos, sin).astype(BF16), share, preferred_element_type=F32)
        vd8 = jnp.dot(vd.astype(BF16), share, preferred_element_type=F32)
        qk = jnp.concatenate([_rotate(qc, cos, sin), _rotate(kc, cos, sin)], axis=1)
        return qk, vc, _rotate(qd, cos, sin), kd8, vd8, qk, qk, vc, vc
    dils = [dil for _, dil in DIL_PATTERN if dil > 1]
    outs = _rowwise(rope_fn, [z, cos, sin], [_gqa_share()],
                    [(2 * ATT_W, BF16)] + [(ATT_W, BF16)] * 4 + [(2 * ATT_W, BF16, d) for d in dils] + [(ATT_W, BF16, d) for d in dils],
                    name="cd_rope", row_maps=[None, table_map, table_map])
    qk, vc, qd, kd, vd = outs[:5]
    views = {1: (_seq3(qk, nb), _seq3(vc, nb))}
    for d, qk_d, vc_d in zip(dils, outs[5:5 + len(dils)], outs[5 + len(dils):]):
        views[d] = (_seq3(qk_d, nb), _seq3(vc_d, nb))
    in_classes = lambda a, width, d: _flat2(a) if d == 1 else ("classes", _flat2(a), width, d)
    branch = []
    for window, dil in DIL_PATTERN:
        qk_v, vc_v = views[dil]
        o_i, lse_i = _attn_fwd((qk_v, 0), (qk_v, 1), (vc_v, 0), classes=dil, heads=8, dh=HEAD_DIM, mode="band",
                               max_dist=window // dil, name=f"cd_dil{dil}")
        branch.append((in_classes(o_i, ATT_W, dil), in_classes(lse_i, LANES, dil)))
    expand = _head_expand(8, ATT_W)

    qd3, kd3, vd3 = _seq3(qd, nb), _seq3(kd, nb), _seq3(vd, nb)
    o_d, lse_band = _attn_fwd((qd3, 0), (kd3, 0), (vd3, 0), heads=8, dh=HEAD_DIM, mode="band", max_dist=SWA_WINDOW - 1,
                              name="cd_swa")

    def mix(o1, o2, o3, l1, l2, l3, o_band, l_band, e, sink):
        m = jnp.maximum(jnp.maximum(l1, l2), l3)
        lt = m + jnp.log(jnp.exp(l1 - m) + jnp.exp(l2 - m) + jnp.exp(l3 - m))
        y_c = sum(_dot_exact(jnp.exp(l - lt), e) * o_ for o_, l in ((o1, l1), (o2, l2), (o3, l3)))
        lt_d = jnp.maximum(l_band, sink) + jnp.log(1.0 + jnp.exp(-jnp.abs(l_band - sink)))
        y_d = o_band * _dot_exact(jnp.exp(l_band - lt_d), e)
        return jnp.concatenate([y_c, y_d], axis=1), y_c, lt, y_d, lt_d
    y, y_c, lse_c, y_d, lse_d = _rowwise(
        mix, [b[0] for b in branch] + [b[1] for b in branch] + [_flat2(o_d), _flat2(lse_band)], [expand, p["sink"]],
        [(D_MODEL, BF16), (ATT_W, F32), (LANES, F32), (ATT_W, F32), (LANES, F32)], name="cd_mix")
    out, hn_next = _residual_out(y, p["w_out"], h, next_gain, name="cd_out")
    return out, hn_next, (h, hn, views, qd3, kd3, vd3, y_c, lse_c, y_d, lse_d, y)


def _cd_bwd(dh, dhb, p, saved, nb):
    h, hn, views, qd3, kd3, vd3, y_c, lse_c, y_d, lse_d, y = saved
    dy, dyb = _matmul(dhb, p["w_out"], tb=True, outs=(F32, BF16), epilogue=lambda acc: (acc, acc), name="cd_bwd_dy")
    dw_out, = _matmul(y, dhb, ta=True, outs=(BF16,),name="cd_bwd_dwout")
    dyb3 = _seq3(dyb, nb)
    dils = [dil for _, dil in DIL_PATTERN if dil > 1]
    gather = _head_expand(8, ATT_W).T

    def prepare(do, o, lse, do_d, o_d, lse_d, e, sink):
        delta = _dot_exact(do * o, e)
        delta_d = _dot_exact(do_d * o_d, e)
        dsink = -jnp.sum(jnp.exp(sink - lse_d) * delta_d, axis=0, keepdims=True)
        return (delta,) + (do,) * len(dils) + (lse,) * len(dils) + (delta,) * len(dils) + (delta_d, dsink)
    outs = _rowwise(prepare, [(dy, ATT_W, 0), y_c, lse_c, (dy, ATT_W, 1), y_d, lse_d], [gather, p["sink"]],
                    [(LANES, F32)] + [(ATT_W, BF16, d) for d in dils] + [(LANES, F32, d) for d in dils] * 2 + [(LANES, F32)],
                    [(1, LANES)], name="cd_bwd_delta")
    delta_d, dsink = outs[-2:]
    seen = {1: ((dyb3, 0), _seq3(lse_c, nb), _seq3(outs[0], nb))}
    for j, d in enumerate(dils):
        seen[d] = ((_seq3(outs[1 + j], nb), 0), _seq3(outs[1 + len(dils) + j], nb), _seq3(outs[1 + 2 * len(dils) + j], nb))
    in_classes = lambda a, d: _flat2(a) if d == 1 else ("classes", _flat2(a), ATT_W, d)
    parts = []
    for window, dil in DIL_PATTERN:
        (qk_v, vc_v), (do_v, lse_v, delta_v) = views[dil], seen[dil]
        grads = _attn_bwd((qk_v, 0), (qk_v, 1), (vc_v, 0), do_v, lse_v, delta_v, classes=dil, heads=8, dh=HEAD_DIM, mode="band",
                          max_dist=window // dil, dq_dtype=BF16, name=f"cd_bwd_dil{dil}")
        parts.append([in_classes(a, dil) for a in grads])
    dqd, dkd, dvd = _attn_bwd((qd3, 0), (kd3, 0), (vd3, 0), (dyb3, 1), _seq3(lse_d, nb), _seq3(delta_d, nb), heads=8,
                              dh=HEAD_DIM, mode="band", max_dist=SWA_WINDOW - 1, dq_dtype=BF16, name="cd_bwd_swa")
    cos, sin = _rope_tables()
    per_seq = SEQ // ROW_TILE
    table_map = lambda i: (i % per_seq, 0)

    def unrope(q1, q2, q3, k1, k2, k3, v1, v2, v3, qd, kd8, vd8, cos, sin, gather):
        back = lambda x: _rotate(x.astype(F32), cos, -sin)
        kd, vd = _dot_exact(kd8, gather), _dot_exact(vd8, gather)
        return jnp.concatenate([back(q1 + q2 + q3), back(k1 + k2 + k3), v1 + v2 + v3, back(qd), back(kd), vd], axis=1)
    ins = [parts[b][t] for t in range(3) for b in range(3)] + [_flat2(dqd), _flat2(dkd), _flat2(dvd), cos, sin]
    dz, = _rowwise(unrope, ins, [_gqa_share().T], [(CD_IN, BF16)], name="cd_bwd_unrope",
                   row_maps=[None] * 12 + [table_map, table_map])
    dw_in_t, = _matmul(dz, hn, ta=True, outs=(BF16,),tm=384, name="cd_bwd_dwin")
    dh2, dh2b, dg = _norm_input_grad(dz, p["w_in_t"], h, dh, p["norm"], name="cd_bwd_dh")
    return dh2, dh2b, {"norm": dg, "w_in_t": dw_in_t, "sink": dsink[:, :8], "w_out": dw_out}


def _local_step(x, mem, target, w, complete=None, grads_ready=None):
    nb = x.shape[0]
    h = x.reshape(nb * SEQ, D_MODEL)
    mem2 = mem.reshape(nb * MEM_LEN, D_MODEL)
    tgt = target.reshape(nb * SEQ, D_MODEL)

    hn = _rms_fwd(h, w["ab"]["norm"], name="ab_norm")
    h, hn, s_ab = _ab_fwd(h, hn, w["ab"], nb, w["xa0"]["norm"], before_out=complete)
    h, hn, s_xa0 = _xa_fwd(h, hn, mem2, w["xa0"], "xa0", nb, w["mlp0"]["norm"])
    h, hn, s_mlp0 = _mlp_fwd(h, hn, w["mlp0"], "mlp0", w["cd"]["norm"])
    h, hn, s_cd = _cd_fwd(h, hn, w["cd"], nb, w["xa1"]["norm"])
    h, hn, s_xa1 = _xa_fwd(h, hn, mem2, w["xa1"], "xa1", nb, w["mlp1"]["norm"])
    h, _, s_mlp1 = _mlp_fwd(h, hn, w["mlp1"], "mlp1", None)

    dh, dhb, loss, dg_final = _loss_and_grad(h, tgt, w["final_norm"], name="loss")
    grads = {"final_norm": dg_final}
    dh, dhb, grads["mlp1"] = _mlp_bwd(dh, dhb, w["mlp1"], s_mlp1, "mlp1")
    dh, dhb, grads["xa1"] = _xa_bwd(dh, dhb, mem2, w["xa1"], s_xa1, "xa1", nb)
    dh, dhb, grads["cd"] = _cd_bwd(dh, dhb, w["cd"], s_cd, nb)
    if grads_ready is not None:
        grads_ready(0, grads)
    dh, dhb, grads["mlp0"] = _mlp_bwd(dh, dhb, w["mlp0"], s_mlp0, "mlp0")
    dh, dhb, grads["xa0"] = _xa_bwd(dh, dhb, mem2, w["xa0"], s_xa0, "xa0", nb)
    if grads_ready is not None:
        grads_ready(1, grads)
    dh, dhb, grads["ab"] = _ab_bwd(dh, dhb, w["ab"], s_ab, nb)
    return loss, dh.reshape(nb, SEQ, D_MODEL), grads


ANY = pl.BlockSpec(memory_space=pl.ANY)


def _place():
    x, y, c = lax.axis_index("x"), lax.axis_index("y"), lax.axis_index("c")
    return x, y, c, [(1 - x, y), (x, 1 - y), (1 - x, 1 - y)]


def _gather_chips(shard):
    half = shard.shape[0] // 2

    def body(src, out, send_sems, recv_sems):
        x, y, c, chips = _place()
        sibling = (x, y, 1 - c)

        def rows(slot, core):
            return out.at[slot, pl.ds(core * half, half)]

        def copy(k, src_ref, dst_ref, to):
            return pltpu.make_async_remote_copy(src_ref=src_ref, dst_ref=dst_ref, send_sem=send_sems.at[k],
                                                recv_sem=recv_sems.at[k], device_id=to, device_id_type=MESH)

        first = [copy(k, src.at[pl.ds(c * half, half)], rows(2 * x + y, c), (px, py, c)) for k, (px, py) in enumerate(chips)]
        for cp in first:
            cp.start()
        passed = [copy(3 + k, rows(2 * px + py, c), rows(2 * px + py, c), sibling) for k, (px, py) in enumerate(chips)]
        for k, (px, py) in enumerate(chips):
            copy(k, src.at[pl.ds(c * half, half)], rows(2 * px + py, c), (px, py, c)).wait_recv()
            passed[k].start()
        for k, (px, py) in enumerate(chips):
            copy(3 + k, rows(2 * px + py, 1 - c), rows(2 * px + py, 1 - c), sibling).wait_recv()
        for cp in first + passed:
            cp.wait_send()

    return pl.pallas_call(
        body, out_shape=jax.ShapeDtypeStruct((N_CHIPS,) + shard.shape, shard.dtype), in_specs=[ANY], out_specs=ANY,
        scratch_shapes=[pltpu.SemaphoreType.DMA((6,)), pltpu.SemaphoreType.DMA((6,))], name="gather_chips",
    )(shard)


HBM = pl.BlockSpec(memory_space=pltpu.HBM)
SEM = pl.BlockSpec(memory_space=pltpu.SEMAPHORE)
SIDE_EFFECT = pltpu.SideEffectType.DATAFLOW_SIDE_EFFECTING


def _chip_copies(src, land, send_sems, recv_sems):
    half = src.shape[0] // 2
    x, y, c, chips = _place()

    def copy(k, slot, to):
        return pltpu.make_async_remote_copy(src_ref=src.at[pl.ds(c * half, half)], dst_ref=land.at[slot, pl.ds(c * half, half)],
                                            send_sem=send_sems[k], recv_sem=recv_sems[k], device_id=to, device_id_type=MESH)
    out = [copy(k, 2 * x + y, (px, py, c)) for k, (px, py) in enumerate(chips)]
    back = [copy(k, 2 * px + py, (px, py, c)) for k, (px, py) in enumerate(chips)]
    return out, back


def _gather_start(shard, after):
    def body(src, land, *rest):
        rest = rest[len(after):]
        sems, token = rest[:6], rest[8]
        out, _ = _chip_copies(src, land, sems[:3], sems[3:])
        for cp in out:
            cp.start()
        token[...] = jnp.zeros(token.shape, F32)

    sem = pltpu.SemaphoreType.DMA(())
    land_shape = (N_CHIPS,) + shard.shape
    return pl.pallas_call(
        body, name="gather_start",
        out_shape=(sem,) * 6 + (pltpu.HBM(shard.shape, shard.dtype), pltpu.HBM(land_shape, shard.dtype),
                                jax.ShapeDtypeStruct((8, LANES), F32)),
        in_specs=(HBM, HBM) + (pl.BlockSpec(memory_space=pl.ANY),) * len(after),
        out_specs=(SEM,) * 6 + (HBM, HBM, pl.BlockSpec(memory_space=pltpu.VMEM)),
        input_output_aliases={0: 6, 1: 7}, compiler_params=pltpu.CompilerParams(has_side_effects=SIDE_EFFECT),
    )(pltpu.with_memory_space_constraint(shard, pltpu.HBM),
      pltpu.with_memory_space_constraint(lax.empty(land_shape, shard.dtype), pltpu.HBM), *after)


def _gather_wait(started, after):
    sems, src_thru, land_thru = started[:6], started[6], started[7]

    def body(src, land, *rest):
        out, back = _chip_copies(src, land, rest[:3], rest[3:6])
        for cp in out:
            cp.wait_send()
        for cp in back:
            cp.wait_recv()

    return pl.pallas_call(
        body, name="gather_wait",
        out_shape=(pltpu.HBM(src_thru.shape, src_thru.dtype), pltpu.HBM(land_thru.shape, land_thru.dtype)),
        in_specs=(HBM, HBM) + (SEM,) * 6 + (pl.BlockSpec(memory_space=pl.ANY),), out_specs=(HBM, HBM),
        input_output_aliases={0: 0, 1: 1}, compiler_params=pltpu.CompilerParams(has_side_effects=SIDE_EFFECT),
    )(src_thru, land_thru, *sems, after)


def _gather_pass(land):
    half = land.shape[1] // 2

    def body(land_in, land_out, send_sems, recv_sems):
        x, y, c, chips = _place()

        def copy(k, slot, core):
            rows = land_out.at[slot, pl.ds(core * half, half)]
            return pltpu.make_async_remote_copy(src_ref=rows, dst_ref=rows, send_sem=send_sems.at[k], recv_sem=recv_sems.at[k],
                                                device_id=(x, y, 1 - c), device_id_type=MESH)
        sends = [copy(k, 2 * px + py, c) for k, (px, py) in enumerate(chips)]
        for cp in sends:
            cp.start()
        for k, (px, py) in enumerate(chips):
            copy(k, 2 * px + py, 1 - c).wait_recv()
        for cp in sends:
            cp.wait_send()

    return pl.pallas_call(
        body, out_shape=jax.ShapeDtypeStruct(land.shape, land.dtype), in_specs=[ANY], out_specs=ANY,
        scratch_shapes=[pltpu.SemaphoreType.DMA((3,)), pltpu.SemaphoreType.DMA((3,))], input_output_aliases={0: 0},
        name="gather_pass",
    )(land)


def _swap_cores(block, *, name, half_rows=None):
    shape = block.shape if half_rows is None else (block.shape[0], half_rows, block.shape[2])

    def body(src, out, send_sem, recv_sem):
        x, y, c, _ = _place()
        part = src if half_rows is None else src.at[:, pl.ds((1 - c) * half_rows, half_rows)]
        cp = pltpu.make_async_remote_copy(src_ref=part, dst_ref=out, send_sem=send_sem, recv_sem=recv_sem,
                                          device_id=(x, y, 1 - c), device_id_type=MESH)
        cp.start()
        cp.wait()

    return pl.pallas_call(
        body, out_shape=jax.ShapeDtypeStruct(shape, block.dtype), in_specs=[ANY], out_specs=ANY,
        scratch_shapes=[pltpu.SemaphoreType.DMA(()), pltpu.SemaphoreType.DMA(())], name=name,
    )(block)


def _scatter_copies(parts, land, send_sems, recv_sems):
    x, y, c, chips = _place()
    return [pltpu.make_async_remote_copy(src_ref=parts.at[2 * px + py], dst_ref=land.at[k], send_sem=send_sems[k],
                                         recv_sem=recv_sems[k], device_id=(px, py, c), device_id_type=MESH)
            for k, (px, py) in enumerate(chips)]


def _scatter_start(parts, tag):
    def body(src, land, *rest):
        for cp in _scatter_copies(src, land, rest[:3], rest[3:6]):
            cp.start()
        rest[8][...] = jnp.zeros(rest[8].shape, F32)

    sem = pltpu.SemaphoreType.DMA(())
    land_shape = (3,) + parts.shape[1:]
    return pl.pallas_call(
        body, name=f"scatter_start_{tag}",
        out_shape=(sem,) * 6 + (pltpu.HBM(parts.shape, parts.dtype), pltpu.HBM(land_shape, parts.dtype),
                                jax.ShapeDtypeStruct((8, LANES), F32)),
        in_specs=(HBM, HBM), out_specs=(SEM,) * 6 + (HBM, HBM, pl.BlockSpec(memory_space=pltpu.VMEM)),
        input_output_aliases={0: 6, 1: 7}, compiler_params=pltpu.CompilerParams(has_side_effects=SIDE_EFFECT),
    )(pltpu.with_memory_space_constraint(parts, pltpu.HBM),
      pltpu.with_memory_space_constraint(lax.empty(land_shape, parts.dtype), pltpu.HBM))


def _scatter_wait(started, after, tag):
    def body(src, land, *rest):
        for cp in _scatter_copies(src, land, rest[:3], rest[3:6]):
            cp.wait_send()
            cp.wait_recv()

    src_thru, land_thru = started[6], started[7]
    return pl.pallas_call(
        body, name=f"scatter_wait_{tag}",
        out_shape=(pltpu.HBM(src_thru.shape, src_thru.dtype), pltpu.HBM(land_thru.shape, land_thru.dtype)),
        in_specs=(HBM, HBM) + (SEM,) * 6 + (pl.BlockSpec(memory_space=pl.ANY),), out_specs=(HBM, HBM),
        input_output_aliases={0: 0, 1: 1}, compiler_params=pltpu.CompilerParams(has_side_effects=SIDE_EFFECT),
    )(src_thru, land_thru, *started[:6], after)[1]


def _sum_all_devices(vec):
    rows = vec.shape[0]

    def body(x_ref, total_ref, all_ref, send_sems, recv_sems, local_sem):
        x, y, c, chips = _place()
        me, sibling = (x, y, c), (x, y, 1 - c)

        def slot(px, py, pc):
            return all_ref.at[4 * px + 2 * py + pc]

        def copy(k, block, to, src=None):
            return pltpu.make_async_remote_copy(src_ref=slot(*block) if src is None else src, dst_ref=slot(*block),
                                                send_sem=send_sems.at[k], recv_sem=recv_sems.at[k], device_id=to,
                                                device_id_type=MESH)

        mine = pltpu.make_async_copy(x_ref, slot(*me), local_sem)
        mine.start()
        first = [copy(0, me, sibling, src=x_ref)] + [copy(1 + j, me, (*chip, c), src=x_ref) for j, chip in enumerate(chips)]
        for cp in first:
            cp.start()
        passed = [copy(4 + j, (*chip, c), sibling) for j, chip in enumerate(chips)]
        for j, chip in enumerate(chips):
            copy(1 + j, (*chip, c), me).wait_recv()
            passed[j].start()
        copy(0, sibling, me).wait_recv()
        for j, chip in enumerate(chips):
            copy(4 + j, (*chip, 1 - c), me).wait_recv()
        for cp in first + passed:
            cp.wait_send()
        mine.wait()
        acc = all_ref[0]
        for d in range(1, 8):
            acc = acc + all_ref[d]
        total_ref[...] = acc

    vmem = pl.BlockSpec(memory_space=pltpu.VMEM)
    return pl.pallas_call(
        body, out_shape=[jax.ShapeDtypeStruct((rows, LANES), F32), jax.ShapeDtypeStruct((8, rows, LANES), F32)],
        in_specs=[vmem], out_specs=[vmem, vmem],
        scratch_shapes=[pltpu.SemaphoreType.DMA((7,)), pltpu.SemaphoreType.DMA((7,)), pltpu.SemaphoreType.DMA(())],
        name="sum_all_devices", compiler_params=pltpu.CompilerParams(vmem_limit_bytes=VMEM_LIMIT_BYTES),
    )(vec)[0]


PACK_COLS = 1024
BIG = (("mlp_w_up", 2, 1024, True), ("mlp_w_down", 2, 1024, False), ("xa_w_kv", 2, 512, True), ("xa_w_q", 2, 256, False),
       ("xa_w_o", 2, 256, False), ("ab_w_out", 1, 256, False), ("cd_w_out", 1, 256, False), ("cd_w_in", 1, 576, True),
       ("ab_w_in", 1, 642, True))
ENTRIES = tuple((name, layer, rows, transposed) for name, layers, rows, transposed in BIG for layer in range(layers))
FIRST_ENTRIES = tuple(e for e in ENTRIES if e[0] == "ab_w_in")
LATER_ENTRIES = tuple(e for e in ENTRIES if e[0] != "ab_w_in")


def _tile_rows(entry):
    return -(-entry[2] // 16) * 16


def _padded_rows(entries):
    return -(-sum(_tile_rows(e) for e in entries) // 32) * 32


SMALL = (("ab_norm", (1, 1024)), ("ab_conv_w", (1, 4, 512)), ("ab_conv_b", (1, 512)), ("lru_w_a", (1, 8, 64, 64)),
         ("lru_b_a", (1, 512)), ("lru_w_i", (1, 8, 64, 64)), ("lru_b_i", (1, 512)), ("lru_lambda", (1, 512)),
         ("fox_b_f", (1, 8)), ("cd_norm", (1, 1024)), ("cd_sink", (1, 8)), ("xa_norm", (2, 1024)),
         ("xa_mem_norm", (2, 1024)), ("mlp_norm", (2, 1024)), ("final_norm", (1024,)), ("loss", (1,)))
SPLIT_SMALL = ("ab_conv_w", "cd_norm")


def _pack_rows(parts, entries, dtype):
    lead = parts[0].shape[:-2]
    zeros = lambda rows: jnp.zeros(lead + (rows, PACK_COLS), dtype)
    pieces = [jnp.pad(p.astype(dtype), ((0, 0),) * len(lead) + ((0, _tile_rows(e) - e[2]), (0, 0))) for p, e in zip(parts, entries)]
    tail = _padded_rows(entries) - sum(_tile_rows(e) for e in entries)
    return jnp.concatenate(pieces + ([zeros(tail)] if tail else []), axis=len(lead))


def _unpack_rows(packed, entries):
    out, r0 = [], 0
    for e in entries:
        out.append(packed[..., r0:r0 + e[2], :])
        r0 += _tile_rows(e)
    return out


def _shard_rows(w, entries):
    return [(w[name][layer].T if transposed else w[name][layer]) for name, layer, _, transposed in entries]


def _shard_blocks(rows):
    out = {}
    for (name, layer, _, transposed), r in zip(ENTRIES, rows):
        out.setdefault(name, []).append(r.T if transposed else r)
    return {name: jnp.stack(layers) for name, layers in out.items()}


def _pack_small(vals):
    flat = jnp.concatenate([vals[name].astype(F32).reshape(-1) for name, _ in SMALL])
    size = -(-flat.shape[0] // (8 * LANES)) * (8 * LANES)
    return jnp.pad(flat, (0, size - flat.shape[0])).reshape(-1, LANES)


def _unpack_small(packed):
    flat, out, at = packed.reshape(-1), {}, 0
    for name, shape in SMALL:
        out[name] = flat[at:at + math.prod(shape)].reshape(shape)
        at += math.prod(shape)
    return out


def _chip_part(full, chip):
    width = full.shape[-1] // N_CHIPS
    return lax.dynamic_slice_in_dim(full, chip * width, width, axis=full.ndim - 1)


def _chip_embed(part, chip):
    full = jnp.zeros(part.shape[:-1] + (part.shape[-1] * N_CHIPS,), part.dtype)
    return lax.dynamic_update_slice_in_dim(full, part, chip * part.shape[-1], axis=part.ndim - 1)


SUBLAYER_KEYS = {"ab_w_in": ("ab", "w_in_t"), "ab_w_out": ("ab", "w_out"), "cd_w_in": ("cd", "w_in_t"), "cd_w_out": ("cd", "w_out"),
                 "xa_w_q": ("xa", "w_q"), "xa_w_kv": ("xa", "w_kv_t"), "xa_w_o": ("xa", "w_o"), "mlp_w_up": ("mlp", "w_up_t"),
                 "mlp_w_down": ("mlp", "w_down")}


def _sublayer(name, layer):
    block, leaf = SUBLAYER_KEYS[name]
    return (block if block in ("ab", "cd") else f"{block}{layer}"), leaf


def _set_big(params, full_rows):
    for (name, layer), rows in full_rows.items():
        block, leaf = _sublayer(name, layer)
        if name == "ab_w_in":
            rows = jnp.concatenate([rows, jnp.zeros((LANES - 8, PACK_COLS), rows.dtype)])
        params[block][leaf] = rows


def _build_params(full_rows, w):
    pad = lambda a: jnp.pad(a, ((0, 0), (0, LANES - a.shape[1])))
    params = {
        "ab": dict(norm=w["ab_norm"], conv_w=w["ab_conv_w"][0], conv_b=w["ab_conv_b"], w_a=_block_diag(w["lru_w_a"][0]),
                   b_a=w["lru_b_a"], w_i=_block_diag(w["lru_w_i"][0]), b_i=w["lru_b_i"], lam=w["lru_lambda"],
                   b_f=w["fox_b_f"].reshape(8, 1)),
        "cd": dict(norm=w["cd_norm"], sink=pad(w["cd_sink"])),
        "final_norm": w["final_norm"].reshape(1, D_MODEL),
    }
    for layer in range(2):
        params[f"xa{layer}"] = dict(norm=w["xa_norm"][layer:layer + 1], mem_norm=w["xa_mem_norm"][layer:layer + 1])
        params[f"mlp{layer}"] = dict(norm=w["mlp_norm"][layer:layer + 1])
    _set_big(params, full_rows)
    return params


def _grad_rows(g, entries=ENTRIES):
    out = []
    for name, layer, _, _ in entries:
        block, leaf = _sublayer(name, layer)
        out.append(g[block][leaf])
    return out


def _reduce_group(entry):
    name, layer = entry[0], entry[1]
    if name.startswith("ab_"):
        return 2
    return 0 if layer == 1 or name.startswith("cd_") else 1


REDUCE_GROUPS = tuple(tuple(e for e in ENTRIES if _reduce_group(e) == group) for group in range(3))


def _reduce_tile(half):
    return max(t for t in range(16, 1025, 16) if half % t == 0)


def _reduce_start(grads_by_chip, core, chip, tag):
    half = grads_by_chip.shape[1] // 2
    tile = _reduce_tile(half)
    steps = half // tile
    place = jnp.stack([core, chip]).astype(jnp.int32)
    got = _swap_cores(grads_by_chip, name=f"swap_cores_{tag}", half_rows=half)
    block = (1, tile, PACK_COLS)

    def sum_cores(place_ref, mine_ref, got_ref, f32_ref, bf16_ref):
        total = mine_ref[...].astype(F32) + got_ref[...].astype(F32)
        f32_ref[...] = total
        bf16_ref[...] = total.astype(BF16)

    pair32, pair16 = pl.pallas_call(
        sum_cores, name=f"sum_cores_{tag}",
        grid_spec=pltpu.PrefetchScalarGridSpec(
            num_scalar_prefetch=1, grid=(N_CHIPS, steps),
            in_specs=[pl.BlockSpec(block, lambda s, i, place_ref: (s, place_ref[0] * steps + i, 0)),
                      pl.BlockSpec(block, lambda s, i, place_ref: (s, i, 0))],
            out_specs=[pl.BlockSpec(block, lambda s, i, place_ref: (s, i, 0))] * 2),
        out_shape=[jax.ShapeDtypeStruct((N_CHIPS, half, PACK_COLS), F32), jax.ShapeDtypeStruct((N_CHIPS, half, PACK_COLS), BF16)],
        compiler_params=_params("arbitrary", "arbitrary"),
    )(place, grads_by_chip, got)
    return pair32, _scatter_start(pair16, tag), place


def _reduce_finish(state, core, tag, after):
    pair32, started, place = state
    half = pair32.shape[1]
    tile = _reduce_tile(half)
    landed = _scatter_wait(started, after, tag)

    def sum_chips(place_ref, own_ref, landed_ref, out_ref):
        out_ref[...] = own_ref[0] + landed_ref[0].astype(F32) + landed_ref[1].astype(F32) + landed_ref[2].astype(F32)

    done = pl.pallas_call(
        sum_chips, name=f"sum_chips_{tag}",
        grid_spec=pltpu.PrefetchScalarGridSpec(
            num_scalar_prefetch=1, grid=(half // tile,),
            in_specs=[pl.BlockSpec((1, tile, PACK_COLS), lambda i, place_ref: (place_ref[1], i, 0)),
                      pl.BlockSpec((3, tile, PACK_COLS), lambda i, place_ref: (0, i, 0))],
            out_specs=pl.BlockSpec((tile, PACK_COLS), lambda i, place_ref: (i, 0))),
        out_shape=jax.ShapeDtypeStruct((half, PACK_COLS), F32), compiler_params=_params("arbitrary"),
    )(place, pair32, landed)
    theirs = _swap_cores(done, name=f"share_halves_{tag}")
    return jnp.where(core == 0, jnp.concatenate([done, theirs]), jnp.concatenate([theirs, done]))


def kernel(x, mem, ab_norm, ab_w_in, ab_conv_w, ab_conv_b, lru_w_a, lru_b_a, lru_w_i, lru_b_i, lru_lambda, fox_b_f, ab_w_out, cd_norm, cd_w_in, cd_sink, cd_w_out, xa_norm, xa_mem_norm, xa_w_q, xa_w_kv, xa_w_o, mlp_norm, mlp_w_up, mlp_w_down, final_norm, loss_target, m_ab_norm, m_ab_w_in, m_ab_conv_w, m_ab_conv_b, m_lru_w_a, m_lru_b_a, m_lru_w_i, m_lru_b_i, m_lru_lambda, m_fox_b_f, m_ab_w_out, m_cd_norm, m_cd_w_in, m_cd_sink, m_cd_w_out, m_xa_norm, m_xa_mem_norm, m_xa_w_q, m_xa_w_kv, m_xa_w_o, m_mlp_norm, m_mlp_w_up, m_mlp_w_down, m_final_norm, v_ab_norm, v_ab_w_in, v_ab_conv_w, v_ab_conv_b, v_lru_w_a, v_lru_b_a, v_lru_w_i, v_lru_b_i, v_lru_lambda, v_fox_b_f, v_ab_w_out, v_cd_norm, v_cd_w_in, v_cd_sink, v_cd_w_out, v_xa_norm, v_xa_mem_norm, v_xa_w_q, v_xa_w_kv, v_xa_w_o, v_mlp_norm, v_mlp_w_up, v_mlp_w_down, v_final_norm):
    given = dict(locals())
    weight_names = [name for name, _ in SMALL if name != "loss"] + [name for name, _, _, _ in BIG]
    w = {name: given[name] for name in weight_names}
    chip = 2 * lax.axis_index("x") + lax.axis_index("y")
    core = lax.axis_index("c")

    slot = lax.broadcasted_iota(jnp.int32, (N_CHIPS, 1, 1), 0)

    def whole(entries, mine, gathered):
        return {(name, layer): jnp.where(slot == chip, own[None], got).reshape(N_CHIPS * rows, PACK_COLS)
                for (name, layer, rows, _), own, got in zip(entries, _unpack_rows(mine, entries), _unpack_rows(gathered, entries))}

    mine_first = _pack_rows(_shard_rows(w, FIRST_ENTRIES), FIRST_ENTRIES, BF16)
    mine_later = _pack_rows(_shard_rows(w, LATER_ENTRIES), LATER_ENTRIES, BF16)
    first = _gather_chips(mine_first)
    owner = (core == 0).astype(F32)
    quarters = {name: _chip_embed(w[name], chip) * owner for name in SPLIT_SMALL}
    zeros = {name: jnp.zeros(shape, F32) for name, shape in SMALL}
    small_packed = _sum_all_devices(_pack_small({**zeros, **quarters}))
    small_full = _unpack_small(small_packed)
    started = _gather_start(mine_later, after=(small_packed, first))
    params = _build_params(whole(FIRST_ENTRIES, mine_first, first),
                           {**w, "ab_conv_w": small_full["ab_conv_w"], "cd_norm": small_full["cd_norm"]})
    params["ab"]["norm"] = params["ab"]["norm"] + started[8][0, 0]

    def complete(h):
        mine, landed = _gather_wait(started, after=h)
        _set_big(params, whole(LATER_ENTRIES, mine, _gather_pass(landed)))

    reducing = {}

    def grads_ready(group, g):
        entries = REDUCE_GROUPS[group]
        by_chip = _pack_rows([r.reshape(N_CHIPS, e[2], PACK_COLS) for r, e in zip(_grad_rows(g, entries), entries)], entries, BF16)
        reducing[group] = _reduce_start(by_chip, core, chip, f"g{group}")
        if group < 2:
            block, leaf = ("mlp0", "w_down") if group == 0 else ("ab", "w_out")
            params[block][leaf] = params[block][leaf] + reducing[group][1][8][0, 0].astype(BF16)

    loss_part, grad_x, g = _local_step(x, mem, loss_target, params, complete, grads_ready)
    grads_ready(2, g)
    reduced, after = {}, reducing[2][1][8]
    for group, entries in enumerate(REDUCE_GROUPS):
        after = _reduce_finish(reducing[group], core, f"g{group}", after=after)
        reduced.update({(e[0], e[1]): r for e, r in zip(entries, _unpack_rows(after, entries))})
    grads = _shard_blocks([reduced[e[0], e[1]] for e in ENTRIES])
    pair = lambda key, leaf: jnp.stack([g[f"{key}0"][leaf], g[f"{key}1"][leaf]])

    small_local = {"ab_norm": g["ab"]["norm"], "ab_conv_w": g["ab"]["conv_w"], "ab_conv_b": g["ab"]["conv_b"],
                   "lru_w_a": g["ab"]["w_a"], "lru_b_a": g["ab"]["b_a"], "lru_w_i": g["ab"]["w_i"], "lru_b_i": g["ab"]["b_i"],
                   "lru_lambda": g["ab"]["lam"], "fox_b_f": g["ab"]["b_f"], "cd_norm": g["cd"]["norm"], "cd_sink": g["cd"]["sink"],
                   "xa_norm": pair("xa", "norm"), "xa_mem_norm": pair("xa", "mem_norm"), "mlp_norm": pair("mlp", "norm"),
                   "final_norm": g["final_norm"], "loss": loss_part[0, :1]}
    small_sum_packed = _sum_all_devices(_pack_small(small_local))
    small_sum = _unpack_small(small_sum_packed)
    loss = small_sum["loss"][0]

    delta, new_m, new_v = {}, {}, {}
    for name, _, _, _ in BIG:
        shape = w[name].shape
        flat = lambda a: a.reshape(-1, shape[-1])
        d, m2, v2 = _adamw(flat(w[name]), flat(grads[name]), flat(given["m_" + name]), flat(given["v_" + name]), name=f"adamw_{name}")
        delta[name], new_m[name], new_v[name] = d.reshape(shape), m2.reshape(shape), v2.reshape(shape)

    def small_state(prefix):
        vals = {name: (given[prefix + name] if name != "loss" else jnp.zeros((1,), F32)) for name, _ in SMALL}
        for name in SPLIT_SMALL:
            vals[name] = _chip_embed(vals[name], chip)
        return _pack_small(vals)
    packed = _adamw(small_state(""), small_sum_packed, small_state("m_"), small_state("v_"), name="adamw_small")
    for store, vals in zip((delta, new_m, new_v), packed):
        for name, val in _unpack_small(vals).items():
            store[name] = _chip_part(val, chip) if name in SPLIT_SMALL else val
    for name in SPLIT_SMALL:
        small_sum[name] = _chip_part(small_sum[name], chip)
    grads.update({name: small_sum[name] for name, _ in SMALL})

    order = ["ab_norm", "ab_w_in", "ab_conv_w", "ab_conv_b", "lru_w_a", "lru_b_a", "lru_w_i", "lru_b_i", "lru_lambda", "fox_b_f",
             "ab_w_out", "cd_norm", "cd_w_in", "cd_sink", "cd_w_out", "xa_norm", "xa_mem_norm", "xa_w_q", "xa_w_kv", "xa_w_o",
             "mlp_norm", "mlp_w_up", "mlp_w_down", "final_norm"]
    return (loss, grad_x, *[grads[n] for n in order], *[delta[n] for n in order], *[new_m[n] for n in order],
            *[new_v[n] for n in order])
```

```python
import functools
import math

import jax
import jax.numpy as jnp
from jax import lax
from jax.experimental import pallas as pl
from jax.experimental.pallas import tpu as pltpu

F32 = jnp.float32
BF16 = jnp.bfloat16
MESH = pl.DeviceIdType.MESH

D_MODEL = 1024
SEQ = 2048
HEAD_DIM = 64
LRU_WIDTH = 512
LRU_BLOCKS = 8
LRU_C = 8.0
CONV_WIDTH = 4
ATT_W = 512
SWA_KW = 128
SWA_WINDOW = 128
DIL_PATTERN = ((128, 1), (512, 4), (2048, 16))
MEM_LEN = 256
XA_HEADS = 4
XA_HEAD_DIM = 256
D_FF = 4096
ROPE_THETA = 10000.0
EPS = 1e-6
N_CHIPS = 4
LANES = 128

ADAM_LR, ADAM_B1, ADAM_B2, ADAM_EPS, ADAM_WD, ADAM_STEP = 0.001, 0.9, 0.999, 1e-08, 0.01, 10

VMEM_LIMIT_BYTES = 56 * 1024 * 1024
MASKED = -1e30
ROW_TILE = 512
LRU_CHUNK = 512
ATT_BLOCK = 128
BAND_ROWS = 256
FULL_ROWS = 512
ATT_CHUNK = 512
CAUSAL_ROWS = 256
CAUSAL_ROWS_BACKWARD = 512
CLASS_ROWS = 512


def _params(*sem):
    return pltpu.CompilerParams(dimension_semantics=sem, vmem_limit_bytes=VMEM_LIMIT_BYTES)


def _rowwise(fn, rows, consts=(), out_rows=(), out_accs=(), *, name, tile=ROW_TILE, row_maps=None):
    rows = [r if isinstance(r, tuple) else (r, r.shape[1], 0) for r in rows]
    consts = list(consts)
    nr, nc, no = len(rows), len(consts), len(out_rows)
    dealt_in = [r[3] if isinstance(r[0], str) else None for r in rows]
    dealt_out = [o[2] if len(o) == 3 else None for o in out_rows]
    total = next(r[0].shape[0] for r in rows if not isinstance(r[0], str))
    tile = min(tile, total)
    assert total % tile == 0
    n = total // tile
    n_acc = len(out_accs)

    def body(*refs):
        scratch = list(refs[nr + nc + no + n_acc:])
        vals = []
        for ref, d, row in zip(refs[:nr], dealt_in, rows):
            if d is None:
                vals.append(ref[...])
                continue
            sc, width = scratch.pop(0), row[2]
            for r in range(d):
                for c in range(width // LANES):
                    lanes = slice(r * width + c * LANES, r * width + (c + 1) * LANES)
                    sc.at[c][pl.ds(r, tile // d, stride=d), :] = ref[:, lanes].astype(F32)
            vals.append(jnp.concatenate([sc[c] for c in range(width // LANES)], axis=1))
        outs = fn(*vals, *[r[...] for r in refs[nr:nr + nc]])
        outs = tuple(outs) if isinstance(outs, (tuple, list)) else (outs,)
        for ref, val, d, spec in zip(refs[nr + nc:nr + nc + no], outs[:no], dealt_out, out_rows):
            if d is None:
                ref[...] = val.astype(ref.dtype)
                continue
            sc, width = scratch.pop(0), spec[0]
            for c in range(width // LANES):
                sc[c] = val[:, c * LANES:(c + 1) * LANES].astype(F32)
            for r in range(d):
                for c in range(width // LANES):
                    lanes = slice(r * width + c * LANES, r * width + (c + 1) * LANES)
                    ref[:, lanes] = sc.at[c][pl.ds(r, tile // d, stride=d), :].astype(ref.dtype)
        for ref, val in zip(refs[nr + nc + no:nr + nc + no + n_acc], outs[no:]):
            @pl.when(pl.program_id(0) == 0)
            def _(ref=ref):
                ref[...] = jnp.zeros(ref.shape, ref.dtype)
            ref[...] += val

    in_specs, args, scratch_shapes = [], [], []
    for idx, row in enumerate(rows):
        if isinstance(row[0], str):
            _, arr, width, d = row
            in_specs.append(pl.BlockSpec((tile // d, d * width), lambda i: (i, 0)))
            scratch_shapes.append(pltpu.VMEM((width // LANES, tile, LANES), F32))
        elif row_maps is not None and row_maps[idx] is not None:
            arr, width, _ = row
            in_specs.append(pl.BlockSpec((tile, width), row_maps[idx]))
        else:
            arr, width, cb = row
            in_specs.append(pl.BlockSpec((tile, width), functools.partial(lambda i, cb: (i, cb), cb=cb)))
        args.append(arr)
    in_specs += [pl.BlockSpec(c.shape, lambda i: (0, 0)) for c in consts]
    out_shape, out_specs = [], []
    for spec, d in zip(out_rows, dealt_out):
        w, dt = spec[0], spec[1]
        if d is None:
            out_shape.append(jax.ShapeDtypeStruct((total, w), dt))
            out_specs.append(pl.BlockSpec((tile, w), lambda i: (i, 0)))
        else:
            out_shape.append(jax.ShapeDtypeStruct((total // d, d * w), dt))
            out_specs.append(pl.BlockSpec((tile // d, d * w), lambda i: (i, 0)))
            scratch_shapes.append(pltpu.VMEM((w // LANES, tile, LANES), F32))
    out_shape += [jax.ShapeDtypeStruct(s, F32) for s in out_accs]
    out_specs += [pl.BlockSpec(s, lambda i: (0, 0)) for s in out_accs]
    return pl.pallas_call(
        body, grid=(n,), in_specs=in_specs, out_specs=out_specs, out_shape=out_shape, scratch_shapes=scratch_shapes, name=name,
        compiler_params=_params("arbitrary"),
    )(*args, *consts)


MATMUL_VMEM_BYTES = 40 * 1024 * 1024


def _matmul_tiles(m, n, k, tm, tn, out_bytes):
    tm, tn, tk = min(tm, m), min(tn, n), k

    def need():
        return 2 * (2 * tk * (tm + tn) + tm * tn * out_bytes) + (0 if tk == k else 4 * tm * tn)

    while need() > MATMUL_VMEM_BYTES:
        if tm >= tn and tm % 256 == 0:
            tm //= 2
        elif tn % 256 == 0:
            tn //= 2
        else:
            tk //= 2
    return tm, tn, tk


def _matmul(a, b, *, ta=False, tb=False, outs=(F32,), epilogue=None, extras=(), consts=(), sums=(), whole_rows=False,
            tm=1024, tn=1024, name):
    m, k = (a.shape[1], a.shape[0]) if ta else a.shape
    n = b.shape[0] if tb else b.shape[1]
    assert (b.shape[1] if tb else b.shape[0]) == k
    outs = [o if isinstance(o, tuple) else (o, None) for o in outs]
    out_bytes = sum(jnp.dtype(dt).itemsize for dt, w in outs if w is None) + sum(e.dtype.itemsize for e in extras)
    tm, tn, tk = _matmul_tiles(m, n, k, tm, tn, out_bytes)
    assert m % tm == 0 and n % tn == 0 and k % tk == 0, (name, m, n, k)
    nk = k // tk
    ne, nc, no = len(extras), len(consts), len(outs)
    assert tn == n or not (sums or whole_rows or any(w is not None for _, w in outs)), name
    dims = (((0 if ta else 1,), (1 if tb else 0,)), ((), ()))

    def body(*refs):
        a_ref, b_ref = refs[:2]
        e_refs, o_refs = refs[2:2 + ne + nc], refs[2 + ne + nc:2 + ne + nc + no]
        s_refs = refs[2 + ne + nc + no:2 + ne + nc + no + len(sums)]

        def finish(acc):
            vals = (acc,) if epilogue is None else epilogue(acc, *[e[...] for e in e_refs])
            for ref, val in zip(o_refs, vals[:no]):
                ref[...] = val.astype(ref.dtype)
            for ref, val in zip(s_refs, vals[no:]):
                @pl.when(pl.program_id(0) == 0)
                def _(ref=ref, val=val):
                    ref[...] = val

                @pl.when(pl.program_id(0) > 0)
                def _(ref=ref, val=val):
                    ref[...] += val

        prod = lax.dot_general(a_ref[...], b_ref[...], dims, preferred_element_type=F32)
        if nk == 1:
            finish(prod)
        else:
            acc_ref = refs[-1]
            step = pl.program_id(2)

            @pl.when(step == 0)
            def _():
                acc_ref[...] = prod

            @pl.when(step > 0)
            def _():
                acc_ref[...] += prod

            @pl.when(step == nk - 1)
            def _():
                finish(acc_ref[...])

    a_spec = pl.BlockSpec((tk, tm), lambda i, j, s: (s, i)) if ta else pl.BlockSpec((tm, tk), lambda i, j, s: (i, s))
    b_spec = pl.BlockSpec((tn, tk), lambda i, j, s: (j, s)) if tb else pl.BlockSpec((tk, tn), lambda i, j, s: (s, j))
    tile_spec = pl.BlockSpec((tm, tn), lambda i, j, s: (i, j))
    whole = lambda shape: pl.BlockSpec(shape, lambda i, j, s: (0, 0))
    return pl.pallas_call(
        body, grid=(m // tm, n // tn, nk),
        in_specs=[a_spec, b_spec] + [tile_spec] * ne + [whole(c.shape) for c in consts],
        out_specs=[tile_spec if w is None else pl.BlockSpec((tm, w), lambda i, j, s: (i, 0)) for _, w in outs]
        + [whole(shape) for shape in sums],
        out_shape=[jax.ShapeDtypeStruct((m, n if w is None else w), dt) for dt, w in outs]
        + [jax.ShapeDtypeStruct(shape, F32) for shape in sums],
        scratch_shapes=[pltpu.VMEM((tm, tn), F32)] if nk > 1 else [],
        name=name, compiler_params=_params("arbitrary" if sums else "parallel", "parallel", "arbitrary"),
    )(a, b, *extras, *consts)


def _split3(x):
    x1 = x.astype(BF16)
    r1 = x - x1.astype(F32)
    x2 = r1.astype(BF16)
    x3 = (r1 - x2.astype(F32)).astype(BF16)
    return x1, x2, x3


def _dot_exact(x, e):
    return sum(jnp.dot(p, e, preferred_element_type=F32) for p in _split3(x))


def _head_expand(heads, width):
    dh = width // heads
    rows = jnp.arange(LANES)[:, None]
    cols = jnp.arange(width)[None, :]
    return (cols // dh == rows).astype(BF16)


def _rstd(x):
    return lax.rsqrt(jnp.mean(x * x, axis=-1, keepdims=True) + EPS)


def _rms_fwd(x, gain, *, name):
    return _rowwise(lambda x, g: x * _rstd(x) * g, [x], [gain], [(x.shape[1], BF16)], name=name)[0]


def _rms_bwd_vals(x, dhn, gain):
    r = _rstd(x)
    xh = x * r
    dxh = dhn * gain
    dx = r * (dxh - xh * jnp.mean(dxh * xh, axis=-1, keepdims=True))
    return dx, jnp.sum(dhn * xh, axis=0, keepdims=True)


def _norm_input_grad(dz, w, x, dres, gain, *, tb=False, name):
    def epilogue(dhn, x, dres, g):
        dx, dg = _rms_bwd_vals(x, dhn, g)
        dh = dres + dx
        return dh, dh, dg
    return _matmul(dz, w, tb=tb, outs=(F32, BF16), extras=(x, dres), consts=(gain,), sums=((1, x.shape[1]),), epilogue=epilogue,
                   name=name)


def _loss_and_grad(h, target, gain, *, name):
    def fn(h, tgt, g):
        r = _rstd(h)
        err = h * r * g - tgt
        loss = 0.5 * jnp.sum(jnp.mean(err * err, axis=-1, keepdims=True), axis=0, keepdims=True)
        dx, dg = _rms_bwd_vals(h, err * (1.0 / D_MODEL), g)
        return dx, dx, jnp.broadcast_to(loss, (1, LANES)), dg
    d = h.shape[1]
    return _rowwise(fn, [h, target], [gain], [(d, F32), (d, BF16)], [(1, LANES), (1, d)], name=name)


def _adamw(w, g, m, v, *, name):
    rows, cols = w.shape
    tile = rows
    for cand in (512, 256, 128, 64, 32, 16, 8):
        if rows % cand == 0 and rows > cand:
            tile = cand
            break
    bc1 = 1.0 - ADAM_B1 ** ADAM_STEP
    bc2 = 1.0 - ADAM_B2 ** ADAM_STEP

    def fn(w, g, m, v):
        m2 = ADAM_B1 * m + (1.0 - ADAM_B1) * g
        v2 = ADAM_B2 * v + (1.0 - ADAM_B2) * (g * g)
        delta = -ADAM_LR * ((m2 / bc1) / (jnp.sqrt(v2 / bc2) + ADAM_EPS) + ADAM_WD * w)
        return delta, m2, v2
    return _rowwise(fn, [w, g, m, v], [], [(cols, F32)] * 3, name=name, tile=tile)


def _attn_specs(arr, width, cb, classes, rows_block, whole, together=1):
    ncols = arr.shape[2] // (classes * width)
    lanes, at = (width, lambda r: r * ncols + cb) if together == 1 else (together * ncols * width, lambda r: r)
    if whole:
        return pl.BlockSpec((1, arr.shape[1], lanes), lambda n, r, i: (n, 0, at(r)))
    return pl.BlockSpec((1, rows_block, lanes), lambda n, r, i: (n, i, at(r)))


def _class_window(refs, spans, together, cl):
    if together == 1:
        return refs
    return [ref.at[:, :, pl.ds((cl * ncols + cb) * width, width)] for ref, (ncols, cb, width) in zip(refs, spans)]


def _attn_tiles(mode, lq, lk, backward=False):
    if mode == "full":
        rows = min(lq, FULL_ROWS)
        return rows, rows, lk
    if mode == "band":
        tq = min(BAND_ROWS if backward else 2 * BAND_ROWS, lq)
        rows = tq if backward else ATT_BLOCK
        return tq, rows, min(rows + ATT_BLOCK, lk)
    rows = CAUSAL_ROWS_BACKWARD if backward else CAUSAL_ROWS
    return rows, rows, ATT_CHUNK


def _window(mode, row0, j, rows, win, lk):
    if mode == "causal":
        return pl.multiple_of(j * win, win), row0 // win + 1
    if mode == "band":
        return pl.multiple_of(jnp.clip(row0 + rows - win, 0, lk - win), ATT_BLOCK), 1
    return 0, 1


def _allowed(mode, row0, start, rows, win, max_dist):
    if mode == "full":
        return None
    dist = (row0 + lax.broadcasted_iota(jnp.int32, (rows, win), 0)) - (start + lax.broadcasted_iota(jnp.int32, (rows, win), 1))
    ok = dist >= 0
    if max_dist is not None:
        ok = ok & (dist <= max_dist)
    return ok


def _head_groups(heads, dh):
    gw = max(LANES, dh)
    return gw, gw // dh, heads // (gw // dh)


def _own_lanes(rows, gw, dh, u):
    return lax.broadcasted_iota(jnp.int32, (rows, gw), 1) // dh == u


def _only_head(x, own, per, u):
    return x if per == 1 else jnp.where(own[u], x, jnp.zeros_like(x))


def _step_scores(qz, kw, kb_row, ok):
    s = lax.dot_general(qz, kw, (((1,), (1,)), ((), ())), preferred_element_type=F32)
    if kb_row is not None:
        s = s - kb_row
    if ok is not None:
        s = jnp.where(ok, s, MASKED)
    return s


def _attn_fwd(q, k, v, kb=None, *, classes=1, heads, dh, mode, max_dist=None, bf16_copy=False, name):
    (qa, qc), (ka, kc), (va, vc) = q, k, v
    n, lq, lk = qa.shape[0], qa.shape[1], ka.shape[1]
    width = heads * dh
    tq, rows, win = _attn_tiles(mode, lq, lk)
    gw, per, groups = _head_groups(heads, dh)
    scale = dh ** -0.5
    has_kb = kb is not None
    single = mode != "causal"
    together = min(classes, max(1, CLASS_ROWS // lq))
    ncols = lambda arr: arr.shape[2] // (classes * width)
    spans = [(ncols(qa), qc, width), (ncols(ka), kc, width), (ncols(va), vc, width), (1, 0, width), (1, 0, LANES), (1, 0, width)]

    def body(*refs):
        for cl in range(together):
            for sub in range(tq // rows):
                part(_class_window(refs, spans, together, cl), pl.program_id(2) * tq + sub * rows, slice(sub * rows, (sub + 1) * rows))

    def part(refs, row0, rs):
        q_ref, k_ref, v_ref = refs[:3]
        kb_ref = refs[3] if has_kb else None
        n_in = 4 if has_kb else 3
        o_ref, lse_ref = refs[n_in:n_in + 2]
        o16_ref = refs[n_in + 2] if bf16_copy else None
        lane = lax.broadcasted_iota(jnp.int32, (rows, LANES), 1)
        own = [_own_lanes(rows, gw, dh, u) for u in range(per)]

        def step(j, carry, masked=True):
            m_in, l_in = carry
            m_out, l_out = m_in, l_in
            start, _ = _window(mode, row0, j, rows, win, lk)
            ok = _allowed(mode, row0, start, rows, win, max_dist) if masked else None
            for g in range(groups):
                cols = slice(g * gw, (g + 1) * gw)
                qg = q_ref[0, rs, cols] * scale
                kw = k_ref[0, pl.ds(start, win), cols]
                vw = v_ref[0, pl.ds(start, win), cols]
                alpha_g = new_g = None
                for u in range(per):
                    h = g * per + u
                    kb_row = kb_ref[0, h, pl.ds(j, 1), :] if has_kb else None
                    s = _step_scores(_only_head(qg, own, per, u), kw, kb_row, ok)
                    m_new = jnp.max(s, axis=1, keepdims=True)
                    if not single:
                        m_old = m_in[:, h:h + 1]
                        m_new = jnp.maximum(m_old, m_new)
                        alpha = jnp.exp(m_old - m_new)
                        alpha_g = alpha if u == 0 else jnp.where(own[u], alpha, alpha_g)
                    p = jnp.exp(s - m_new)
                    l_new = jnp.sum(p, axis=1, keepdims=True)
                    r = jnp.dot(p.astype(BF16), vw, preferred_element_type=F32)
                    if single:
                        r = r * (1.0 / l_new)
                    else:
                        l_new = alpha * l_in[:, h:h + 1] + l_new
                    new_g = r if u == 0 else jnp.where(own[u], r, new_g)
                    m_out = jnp.where(lane == h, m_new, m_out)
                    l_out = jnp.where(lane == h, l_new, l_out)
                o_ref[0, rs, cols] = new_g if single else alpha_g * o_ref[0, rs, cols] + new_g
                if bf16_copy:
                    o16_ref[0, rs, cols] = new_g.astype(BF16)
            return m_out, l_out

        carry = (jnp.full((rows, LANES), MASKED, F32), jnp.zeros((rows, LANES), F32))
        if single:
            m_all, l_all = step(0, carry)
            l_all = jnp.where(lane < heads, l_all, 1.0)
        else:
            o_ref[...] = jnp.zeros(o_ref.shape, F32)
            last = _window(mode, row0, 0, rows, win, lk)[1] - 1
            m_all, l_all = step(last, lax.fori_loop(0, last, functools.partial(step, masked=False), carry))
            l_all = jnp.where(lane < heads, l_all, 1.0)
            inv = 1.0 / l_all
            for g in range(groups):
                cols = slice(g * gw, (g + 1) * gw)
                inv_g = inv[:, g * per:g * per + 1]
                for u in range(1, per):
                    inv_g = jnp.where(own[u], inv[:, g * per + u:g * per + u + 1], inv_g)
                o_ref[0, rs, cols] = o_ref[0, rs, cols] * inv_g
        lse_ref[0, rs, :] = jnp.where(lane < heads, m_all + jnp.log(l_all), 0.0)

    in_specs = [_attn_specs(qa, width, qc, classes, tq, False, together), _attn_specs(ka, width, kc, classes, win, True, together),
                _attn_specs(va, width, vc, classes, win, True, together)]
    args = [qa, ka, va]
    if has_kb:
        assert together == 1
        in_specs.append(pl.BlockSpec((1,) + kb.shape[1:], lambda n_, r, i: (n_, 0, 0, 0)))
        args.append(kb)
    out_shape = [jax.ShapeDtypeStruct((n, lq, classes * width), F32), jax.ShapeDtypeStruct((n, lq, classes * LANES), F32)]
    out_specs = [pl.BlockSpec((1, tq, together * width), lambda n_, r, i: (n_, i, r)),
                 pl.BlockSpec((1, tq, together * LANES), lambda n_, r, i: (n_, i, r))]
    if bf16_copy:
        assert single
        out_shape.append(jax.ShapeDtypeStruct((n, lq, classes * width), BF16))
        out_specs.append(out_specs[0])
    return pl.pallas_call(
        body, grid=(n, classes // together, lq // tq), in_specs=in_specs, out_specs=out_specs, out_shape=out_shape, name=name,
        compiler_params=_params("parallel", "parallel", "arbitrary"),
    )(*args)


def _attn_bwd(q, k, v, do, lse, delta, kb=None, *, classes=1, heads, dh, mode, max_dist=None, dq_dtype=F32, name):
    (qa, qc), (ka, kc), (va, vc), (da, dc) = q, k, v, do
    n, lq, lk = qa.shape[0], qa.shape[1], ka.shape[1]
    width = heads * dh
    tq, rows, win = _attn_tiles(mode, lq, lk, backward=True)
    gw, per, groups = _head_groups(heads, dh)
    scale = dh ** -0.5
    has_kb = kb is not None
    single = mode != "causal"
    nt = (((1,), (1,)), ((), ()))
    tn = (((0,), (0,)), ((), ()))
    together = min(classes, max(1, CLASS_ROWS // lq))
    ncols = lambda arr: arr.shape[2] // (classes * width)
    spans = [(ncols(qa), qc, width), (ncols(ka), kc, width), (ncols(va), vc, width), (ncols(da), dc, width), (1, 0, LANES),
             (1, 0, LANES), (1, 0, width), (1, 0, width), (1, 0, width)]

    def body(*refs):
        n_in = 7 if has_kb else 6
        dk_ref, dv_ref = refs[n_in + 1:n_in + 3]

        @pl.when(pl.program_id(2) == 0)
        def _():
            dk_ref[...] = jnp.zeros(dk_ref.shape, F32)
            dv_ref[...] = jnp.zeros(dv_ref.shape, F32)
            if has_kb:
                refs[n_in + 3][...] = jnp.zeros(refs[n_in + 3].shape, F32)

        for cl in range(together):
            for sub in range(tq // rows):
                part(_class_window(refs, spans, together, cl), pl.program_id(2) * tq + sub * rows, slice(sub * rows, (sub + 1) * rows))

    def part(refs, row0, rs):
        q_ref, k_ref, v_ref, do_ref, lse_ref, dl_ref = refs[:6]
        kb_ref = refs[6] if has_kb else None
        n_in = 7 if has_kb else 6
        dq_ref, dk_ref, dv_ref = refs[n_in:n_in + 3]
        dkb_ref, drow_ref = refs[n_in + 3:n_in + 5] if has_kb else (None, None)
        lse_all = lse_ref[0, rs, :]
        dl_all = dl_ref[0, rs, :]
        lane = lax.broadcasted_iota(jnp.int32, (rows, LANES), 1)
        own = [_own_lanes(rows, gw, dh, u) for u in range(per)]

        def step(j, drow_all, masked=True):
            start, _ = _window(mode, row0, j, rows, win, lk)
            ok = _allowed(mode, row0, start, rows, win, max_dist) if masked else None
            for g in range(groups):
                cols = slice(g * gw, (g + 1) * gw)
                qg = q_ref[0, rs, cols] * scale
                dog = do_ref[0, rs, cols]
                kw = k_ref[0, pl.ds(start, win), cols]
                vw = v_ref[0, pl.ds(start, win), cols]
                dq_g = dk_w = dv_w = None
                for u in range(per):
                    h = g * per + u
                    qz, doz = _only_head(qg, own, per, u), _only_head(dog, own, per, u)
                    kb_row = kb_ref[0, h, pl.ds(j, 1), :] if has_kb else None
                    p = jnp.exp(_step_scores(qz, kw, kb_row, ok) - lse_all[:, h:h + 1])
                    dp = lax.dot_general(doz, vw, nt, preferred_element_type=F32)
                    ds = p * (dp - dl_all[:, h:h + 1])
                    dsb = ds.astype(BF16)
                    r = jnp.dot(dsb, kw, preferred_element_type=F32)
                    dq_g = r if u == 0 else jnp.where(own[u], r, dq_g)
                    dk_u = lax.dot_general(dsb, qz, tn, preferred_element_type=F32)
                    dv_u = lax.dot_general(p.astype(BF16), doz, tn, preferred_element_type=F32)
                    dk_w = dk_u if u == 0 else dk_w + dk_u
                    dv_w = dv_u if u == 0 else dv_w + dv_u
                    if has_kb:
                        dkb_ref[0, h, pl.ds(j, 1), :] += -jnp.sum(ds, axis=0, keepdims=True)
                        drow_all = drow_all + jnp.where(lane == h, jnp.sum(ds, axis=1, keepdims=True), 0.0)
                dk_ref[0, pl.ds(start, win), cols] += dk_w
                dv_ref[0, pl.ds(start, win), cols] += dv_w
                if single:
                    dq_ref[0, rs, cols] = (dq_g * scale).astype(dq_ref.dtype)
                else:
                    dq_ref[0, rs, cols] += dq_g * scale
            return drow_all

        drow_all = jnp.zeros((rows, LANES), F32)
        if single:
            drow_all = step(0, drow_all)
        else:
            dq_ref[...] = jnp.zeros(dq_ref.shape, F32)
            last = _window(mode, row0, 0, rows, win, lk)[1] - 1
            drow_all = step(last, lax.fori_loop(0, last, functools.partial(step, masked=False), drow_all))
        if has_kb:
            drow_ref[0, rs, :] = drow_all

    row_spec = functools.partial(_attn_specs, classes=classes, rows_block=tq, whole=False, together=together)
    in_specs = [row_spec(qa, width, qc), _attn_specs(ka, width, kc, classes, win, True, together),
                _attn_specs(va, width, vc, classes, win, True, together), row_spec(da, width, dc), row_spec(lse, LANES, 0),
                row_spec(delta, LANES, 0)]
    args = [qa, ka, va, da, lse, delta]
    assert single or dq_dtype == F32
    out_shape = [jax.ShapeDtypeStruct((n, lq, classes * width), dq_dtype), jax.ShapeDtypeStruct((n, lk, classes * width), F32),
                 jax.ShapeDtypeStruct((n, lk, classes * width), F32)]
    kv_spec = pl.BlockSpec((1, lk, together * width), lambda n_, r, i: (n_, 0, r))
    out_specs = [pl.BlockSpec((1, tq, together * width), lambda n_, r, i: (n_, i, r)), kv_spec, kv_spec]
    if has_kb:
        assert together == 1
        kb_spec = pl.BlockSpec((1,) + kb.shape[1:], lambda n_, r, i: (n_, 0, 0, 0))
        in_specs.append(kb_spec)
        args.append(kb)
        out_shape += [jax.ShapeDtypeStruct(kb.shape, F32), jax.ShapeDtypeStruct((n, lq, LANES), F32)]
        out_specs += [kb_spec, pl.BlockSpec((1, tq, LANES), lambda n_, r, i: (n_, i, 0))]
    return pl.pallas_call(
        body, grid=(n, classes // together, lq // tq), in_specs=in_specs, out_specs=out_specs, out_shape=out_shape, name=name,
        compiler_params=_params("parallel", "parallel", "arbitrary"),
    )(*args)


def _rope_tables():
    half = HEAD_DIM // 2
    inv = ROPE_THETA ** (-jnp.arange(half, dtype=F32) / half)
    ang = jnp.arange(SEQ, dtype=F32)[:, None] * inv[None, :]
    cos = jnp.tile(jnp.cos(ang), (1, 2 * ATT_W // HEAD_DIM))
    sin = jnp.tile(jnp.concatenate([-jnp.sin(ang), jnp.sin(ang)], axis=1), (1, ATT_W // HEAD_DIM))
    return cos, sin


def _rotate(x, cos, sin):
    width = x.shape[1]
    lane = lax.broadcasted_iota(jnp.int32, x.shape, 1)
    first_half = (lane % HEAD_DIM) < (HEAD_DIM // 2)
    swapped = jnp.where(first_half, pltpu.roll(x, width - HEAD_DIM // 2, axis=1), pltpu.roll(x, HEAD_DIM // 2, axis=1))
    return x * cos[:, :width] + swapped * sin[:, :width]


def _gelu(x):
    k = math.sqrt(2.0 / math.pi)
    t = jnp.tanh(k * (x + 0.044715 * x * x * x))
    return 0.5 * x * (1.0 + t), t


def _gelu_grad(x, t):
    k = math.sqrt(2.0 / math.pi)
    return 0.5 * (1.0 + t) + 0.5 * x * (1.0 - t * t) * k * (1.0 + 3.0 * 0.044715 * x * x)


def _shift_down(x, halo, s):
    ext = jnp.concatenate([halo, x], axis=0)
    return pltpu.roll(ext, s, axis=0)[8:, :]


def _shift_up(x, halo, s):
    rows = x.shape[0]
    ext = jnp.concatenate([x, halo], axis=0)
    return pltpu.roll(ext, rows + 8 - s, axis=0)[:rows, :]


def _lru_gates(u, halo, cw, cb, wa, ba, wi, bi, lam):
    taps = [_shift_down(u, halo, CONV_WIDTH - 1 - k) for k in range(CONV_WIDTH - 1)] + [u]
    uc = cb + sum(cw[k:k + 1, :] * taps[k] for k in range(CONV_WIDTH))
    ucb = uc.astype(BF16)
    r = jax.nn.sigmoid(jnp.dot(ucb, wa, preferred_element_type=F32) + ba)
    gi = jax.nn.sigmoid(jnp.dot(ucb, wi, preferred_element_type=F32) + bi)
    sp = jnp.maximum(-lam, 0.0) + jnp.log(1.0 + jnp.exp(-jnp.abs(lam)))
    log_a = -LRU_C * r * sp
    a = jnp.exp(log_a)
    x2 = 2.0 * log_a
    one_minus_a2 = jnp.where(x2 > -0.01, -x2 * (1.0 + x2 * (0.5 + x2 * (1.0 / 6.0 + x2 / 24.0))), 1.0 - jnp.exp(x2))
    mult = jnp.sqrt(one_minus_a2)
    return taps, uc, ucb, r, gi, sp, a, mult


def _lru_specs(nchunk, reverse):
    def chunk(b, c):
        return b * nchunk + ((nchunk - 1 - c) if reverse else c)

    def halo_before(b, c):
        return jnp.maximum(chunk(b, c) * (LRU_CHUNK // 8) - 1, 0)

    return chunk, halo_before


def _lru_fwd(zug, cw, cb, wa, ba, wi, bi, lam, *, name):
    total = zug.shape[0]
    nchunk = SEQ // LRU_CHUNK
    w = LRU_WIDTH
    chunk, halo_before = _lru_specs(nchunk, False)

    def body(u_ref, uh_ref, g_ref, cw_ref, cb_ref, wa_ref, ba_ref, wi_ref, bi_ref, lam_ref, y_ref, h_ref, a_sc, x_sc, carry_sc):
        c = pl.program_id(1)
        u = u_ref[...]
        halo = jnp.where(c > 0, uh_ref[...], 0.0)
        _, uc, _, _, gi, _, a, mult = _lru_gates(u, halo, cw_ref[...], cb_ref[...], wa_ref[...], ba_ref[...],
                                                 wi_ref[...], bi_ref[...], lam_ref[...])
        a_sc[...] = a
        x_sc[...] = mult * (gi * uc)

        @pl.when(c == 0)
        def _():
            carry_sc[...] = jnp.zeros(carry_sc.shape, F32)

        def tile_step(t, h):
            r0 = pl.multiple_of(t * 8, 8)
            at = a_sc[pl.ds(r0, 8), :]
            xt = x_sc[pl.ds(r0, 8), :]
            rows = []
            for j in range(8):
                h = at[j:j + 1, :] * h + xt[j:j + 1, :]
                rows.append(h)
            h_ref[pl.ds(r0, 8), :] = jnp.concatenate(rows, axis=0)
            return h

        h_last = lax.fori_loop(0, LRU_CHUNK // 8, tile_step, carry_sc[0:1, :])
        carry_sc[0:1, :] = h_last
        gel, _ = _gelu(g_ref[...])
        y_ref[...] = (h_ref[...] * gel).astype(BF16)

    small = lambda arr: pl.BlockSpec(arr.shape, lambda b, c: (0, 0))
    return pl.pallas_call(
        body, grid=(total // SEQ, nchunk),
        in_specs=[pl.BlockSpec((LRU_CHUNK, w), lambda b, c: (chunk(b, c), 0)),
                  pl.BlockSpec((8, w), lambda b, c: (halo_before(b, c), 0)),
                  pl.BlockSpec((LRU_CHUNK, w), lambda b, c: (chunk(b, c), 1)),
                  small(cw), small(cb), small(wa), small(ba), small(wi), small(bi), small(lam)],
        out_specs=[pl.BlockSpec((LRU_CHUNK, w), lambda b, c: (chunk(b, c), 0))] * 2,
        out_shape=[jax.ShapeDtypeStruct((total, w), BF16), jax.ShapeDtypeStruct((total, w), F32)],
        scratch_shapes=[pltpu.VMEM((LRU_CHUNK, w), F32), pltpu.VMEM((LRU_CHUNK, w), F32), pltpu.VMEM((8, w), F32)],
        name=name, compiler_params=_params("arbitrary", "arbitrary"),
    )(zug, zug, zug, cw, cb, wa, ba, wi, bi, lam)


def _lru_bwd(zug, hl, dy, cw, cb, wa, ba, wi, bi, lam, *, name):
    total = zug.shape[0]
    nchunk = SEQ // LRU_CHUNK
    w = LRU_WIDTH
    chunk, halo_before = _lru_specs(nchunk, True)
    tn = (((0,), (0,)), ((), ()))
    nt = (((1,), (1,)), ((), ()))

    def body(u_ref, uh_ref, g_ref, hl_ref, hh_ref, dy_ref, cw_ref, cb_ref, wa_ref, ba_ref, wi_ref, bi_ref, lam_ref,
             dz_ref, dcw_ref, dcb_ref, dwa_ref, dba_ref, dwi_ref, dbi_ref, dlam_ref,
             a_sc, d_sc, g_sc, gcarry_sc, acarry_sc, duc_sc):
        b, c = pl.program_id(0), pl.program_id(1)
        first_chunk = c == nchunk - 1

        @pl.when((b == 0) & (c == 0))
        def _():
            for ref in (dcw_ref, dcb_ref, dwa_ref, dba_ref, dwi_ref, dbi_ref, dlam_ref):
                ref[...] = jnp.zeros(ref.shape, F32)

        @pl.when(c == 0)
        def _():
            gcarry_sc[...] = jnp.zeros(gcarry_sc.shape, F32)
            acarry_sc[...] = jnp.zeros(acarry_sc.shape, F32)
            duc_sc[...] = jnp.zeros(duc_sc.shape, F32)

        u = u_ref[...]
        halo = jnp.where(first_chunk, 0.0, uh_ref[...])
        cw, wa, wi, lam = cw_ref[...], wa_ref[...], wi_ref[...], lam_ref[...]
        taps, uc, ucb, r, gi, sp, a, mult = _lru_gates(u, halo, cw, cb_ref[...], wa, ba_ref[...], wi, bi_ref[...], lam)
        gate = g_ref[...]
        gel, th = _gelu(gate)
        dy = dy_ref[...]
        hl = hl_ref[...]
        a_sc[...] = a
        d_sc[...] = dy * gel
        dgate = dy * hl * _gelu_grad(gate, th)

        def tile_step(t, carry):
            g_next, a_next = carry
            r0 = pl.multiple_of((LRU_CHUNK // 8 - 1 - t) * 8, 8)
            dt = d_sc[pl.ds(r0, 8), :]
            at = a_sc[pl.ds(r0, 8), :]
            rows = [None] * 8
            for j in reversed(range(8)):
                g_next = dt[j:j + 1, :] + a_next * g_next
                a_next = at[j:j + 1, :]
                rows[j] = g_next
            g_sc[pl.ds(r0, 8), :] = jnp.concatenate(rows, axis=0)
            return g_next, a_next

        g_last, a_last = lax.fori_loop(0, LRU_CHUNK // 8, tile_step, (gcarry_sc[0:1, :], acarry_sc[0:1, :]))
        gcarry_sc[0:1, :] = g_last
        acarry_sc[0:1, :] = a_last

        gs = g_sc[...]
        hl_halo = jnp.where(first_chunk, 0.0, hh_ref[...])
        da = gs * _shift_down(hl, hl_halo, 1)
        dmult = gs * gi * uc
        dgi = gs * mult * uc
        duc = gs * mult * gi
        dlog_a = da * a - dmult * a * a / mult
        dr = dlog_a * (-LRU_C * sp)
        dsp = jnp.sum(dlog_a * (-LRU_C * r), axis=0, keepdims=True)
        dlam_ref[...] += -dsp * jax.nn.sigmoid(-lam)
        dpa = dr * r * (1.0 - r)
        dpi = dgi * gi * (1.0 - gi)
        dpab, dpib = dpa.astype(BF16), dpi.astype(BF16)
        dba_ref[...] += jnp.sum(dpa, axis=0, keepdims=True)
        dbi_ref[...] += jnp.sum(dpi, axis=0, keepdims=True)
        dwa_ref[...] += lax.dot_general(ucb, dpab, tn, preferred_element_type=F32)
        dwi_ref[...] += lax.dot_general(ucb, dpib, tn, preferred_element_type=F32)
        duc = duc + lax.dot_general(dpab, wa, nt, preferred_element_type=F32)
        duc = duc + lax.dot_general(dpib, wi, nt, preferred_element_type=F32)
        dcb_ref[...] += jnp.sum(duc, axis=0, keepdims=True)
        dcw_ref[...] += jnp.concatenate([jnp.sum(duc * taps[k], axis=0, keepdims=True) for k in range(CONV_WIDTH)], axis=0)
        after = duc_sc[...]
        du = cw[CONV_WIDTH - 1:CONV_WIDTH, :] * duc
        for k in range(CONV_WIDTH - 1):
            du = du + cw[k:k + 1, :] * _shift_up(duc, after, CONV_WIDTH - 1 - k)
        duc_sc[...] = duc[0:8, :]
        dz_ref[:, 0:w] = du.astype(BF16)
        dz_ref[:, w:2 * w] = dgate.astype(BF16)

    small = lambda arr: pl.BlockSpec(arr.shape, lambda b, c: (0, 0))
    acc = lambda shape: pl.BlockSpec(shape, lambda b, c: (0, 0))
    chunk_spec = lambda cb_: pl.BlockSpec((LRU_CHUNK, w), lambda b, c: (chunk(b, c), cb_))
    halo_spec = pl.BlockSpec((8, w), lambda b, c: (halo_before(b, c), 0))
    acc_shapes = [(CONV_WIDTH, w), (1, w), (w, w), (1, w), (w, w), (1, w), (1, w)]
    return pl.pallas_call(
        body, grid=(total // SEQ, nchunk),
        in_specs=[chunk_spec(0), halo_spec, chunk_spec(1), chunk_spec(0), halo_spec, chunk_spec(0),
                  small(cw), small(cb), small(wa), small(ba), small(wi), small(bi), small(lam)],
        out_specs=[pl.BlockSpec((LRU_CHUNK, 2 * w), lambda b, c: (chunk(b, c), 0))] + [acc(s) for s in acc_shapes],
        out_shape=[jax.ShapeDtypeStruct((total, 2 * w), BF16)] + [jax.ShapeDtypeStruct(s, F32) for s in acc_shapes],
        scratch_shapes=[pltpu.VMEM((LRU_CHUNK, w), F32)] * 3 + [pltpu.VMEM((8, w), F32)] * 3,
        name=name, compiler_params=_params("arbitrary", "arbitrary"),
    )(zug, zug, zug, hl, hl, dy, cw, cb, wa, ba, wi, bi, lam)


def _block_diag(wb):
    eye = jnp.eye(LRU_BLOCKS, dtype=wb.dtype)
    return jnp.einsum("gcd,gh->gchd", wb, eye).reshape(LRU_WIDTH, LRU_WIDTH).astype(BF16)


def _diag_blocks(wd):
    blk = LRU_WIDTH // LRU_BLOCKS
    return jnp.stack([wd[g * blk:(g + 1) * blk, g * blk:(g + 1) * blk] for g in range(LRU_BLOCKS)])


FOX_SCAN = 256


def _fox_bias(fl, bf, *, name):
    nb = fl.shape[0]

    def body(fl_ref, bf_ref, c_ref):
        x = fl_ref[0] + bf_ref[...]
        logf = jnp.minimum(x, 0.0) - jnp.log(1.0 + jnp.exp(-jnp.abs(x)))
        upper = (lax.broadcasted_iota(jnp.int32, (FOX_SCAN, FOX_SCAN), 0)
                 <= lax.broadcasted_iota(jnp.int32, (FOX_SCAN, FOX_SCAN), 1)).astype(BF16)
        carry = jnp.zeros((LRU_BLOCKS, 1), F32)
        for j in range(SEQ // FOX_SCAN):
            blk = logf[:, j * FOX_SCAN:(j + 1) * FOX_SCAN]
            c_ref[0, :, j * FOX_SCAN:(j + 1) * FOX_SCAN] = _dot_exact(blk, upper) + carry
            carry = carry + jnp.sum(blk, axis=1, keepdims=True)

    return pl.pallas_call(
        body, grid=(nb,),
        in_specs=[pl.BlockSpec((1, 8, SEQ), lambda b: (b, 0, 0)), pl.BlockSpec((8, 1), lambda b: (0, 0))],
        out_specs=pl.BlockSpec((1, 8, SEQ), lambda b: (b, 0, 0)),
        out_shape=jax.ShapeDtypeStruct((nb, 8, SEQ), F32), name=name, compiler_params=_params("arbitrary"),
    )(fl, bf)


def _fox_bias_bwd(fl, bf, dc, *, name):
    nb = fl.shape[0]

    def body(fl_ref, bf_ref, dc_ref, dfl_ref, dbf_ref):
        @pl.when(pl.program_id(0) == 0)
        def _():
            dbf_ref[...] = jnp.zeros(dbf_ref.shape, F32)

        x = fl_ref[0] + bf_ref[...]
        dc_all = dc_ref[0]
        lower = (lax.broadcasted_iota(jnp.int32, (FOX_SCAN, FOX_SCAN), 0)
                 >= lax.broadcasted_iota(jnp.int32, (FOX_SCAN, FOX_SCAN), 1)).astype(BF16)
        carry = jnp.zeros((LRU_BLOCKS, 1), F32)
        total = jnp.zeros((LRU_BLOCKS, 1), F32)
        for j in reversed(range(SEQ // FOX_SCAN)):
            cols = slice(j * FOX_SCAN, (j + 1) * FOX_SCAN)
            blk = dc_all[:, cols]
            dlogf = _dot_exact(blk, lower) + carry
            carry = carry + jnp.sum(blk, axis=1, keepdims=True)
            dx = dlogf * jax.nn.sigmoid(-x[:, cols])
            dfl_ref[0, :, cols] = dx
            total = total + jnp.sum(dx, axis=1, keepdims=True)
        dbf_ref[...] += total

    return pl.pallas_call(
        body, grid=(nb,),
        in_specs=[pl.BlockSpec((1, 8, SEQ), lambda b: (b, 0, 0)), pl.BlockSpec((8, 1), lambda b: (0, 0)),
                  pl.BlockSpec((1, 8, SEQ), lambda b: (b, 0, 0))],
        out_specs=[pl.BlockSpec((1, 8, SEQ), lambda b: (b, 0, 0)), pl.BlockSpec((8, 1), lambda b: (0, 0))],
        out_shape=[jax.ShapeDtypeStruct((nb, 8, SEQ), F32), jax.ShapeDtypeStruct((8, 1), F32)],
        name=name, compiler_params=_params("arbitrary"),
    )(fl, bf, dc)


def _seq3(a, nb):
    return a.reshape(nb, a.shape[0] // nb, a.shape[1])


def _flat2(a):
    return a.reshape(a.shape[0] * a.shape[1], a.shape[2])


def _residual_out(y, w, h, next_gain, *, name):
    if next_gain is None:
        out, = _matmul(y, w, extras=(h,), epilogue=lambda acc, res: (acc + res,), name=name)
        return out, None

    def epilogue(acc, res, g):
        out = acc + res
        return out, out * _rstd(out) * g
    return _matmul(y, w, outs=(F32, BF16), extras=(h,), consts=(next_gain,), epilogue=epilogue, whole_rows=True, name=name)


def _mlp_fwd(h, hn, p, tag, next_gain):
    act, = _matmul(hn, p["w_up_t"], tb=True, outs=(BF16,), epilogue=lambda acc: (jnp.square(jnp.maximum(acc, 0.0)),),
                   name=f"{tag}_up")
    out, hn_next = _residual_out(act, p["w_down"], h, next_gain, name=f"{tag}_down")
    return out, hn_next, (h, hn, act)


def _mlp_bwd(dh, dhb, p, saved, tag):
    h, hn, act = saved
    dup, = _matmul(dhb, p["w_down"], tb=True, outs=(BF16,), extras=(act,),
                   epilogue=lambda acc, act: (acc * (2.0 * jnp.sqrt(act).astype(F32)),), name=f"{tag}_bwd_dup")
    dw_down, = _matmul(act, dhb, ta=True, outs=(BF16,),name=f"{tag}_bwd_dwdown")
    dw_up_t, = _matmul(dup, hn, ta=True, outs=(BF16,),name=f"{tag}_bwd_dwup")
    dh2, dh2b, dg = _norm_input_grad(dup, p["w_up_t"], h, dh, p["norm"], name=f"{tag}_bwd_dh")
    return dh2, dh2b, {"norm": dg, "w_up_t": dw_up_t, "w_down": dw_down}


def _xa_fwd(h, hn, mem, p, tag, nb, next_gain):
    mem_n = _rms_fwd(mem, p["mem_norm"], name=f"{tag}_memnorm")
    q, = _matmul(hn, p["w_q"], outs=(BF16,), name=f"{tag}_q")
    kv, = _matmul(mem_n, p["w_kv_t"], tb=True, outs=(BF16,), name=f"{tag}_kv")
    q3, kv3 = _seq3(q, nb), _seq3(kv, nb)
    o, lse, ob = _attn_fwd((q3, 0), (kv3, 0), (kv3, 1), heads=XA_HEADS, dh=XA_HEAD_DIM, mode="full", bf16_copy=True,
                           name=f"{tag}_attn")
    o, ob = _flat2(o), _flat2(ob)
    out, hn_next = _residual_out(ob, p["w_o"], h, next_gain, name=f"{tag}_o")
    return out, hn_next, (h, hn, mem_n, q3, kv3, o, ob, lse)


def _xa_bwd(dh, dhb, mem, p, saved, tag, nb):
    h, hn, mem_n, q3, kv3, o, ob, lse = saved
    dob, delta = _matmul(dhb, p["w_o"], tb=True, outs=(BF16, (F32, LANES)), extras=(o,), consts=(_head_expand(XA_HEADS, D_MODEL).T,),
                         epilogue=lambda acc, o, e: (acc, _dot_exact(acc * o, e)), name=f"{tag}_bwd_do")
    dw_o, = _matmul(ob, dhb, ta=True, outs=(BF16,),name=f"{tag}_bwd_dwo")
    dq, dk, dv = _attn_bwd((q3, 0), (kv3, 0), (kv3, 1), (_seq3(dob, nb), 0), lse, _seq3(delta, nb), heads=XA_HEADS,
                           dh=XA_HEAD_DIM, mode="full", dq_dtype=BF16, name=f"{tag}_bwd_attn")
    dqb = _flat2(dq)
    dkvb, = _rowwise(lambda a, b: jnp.concatenate([a, b], axis=1), [_flat2(dk), _flat2(dv)], [], [(2 * D_MODEL, BF16)],
                     name=f"{tag}_bwd_dkvcast")
    dw_q, = _matmul(hn, dqb, ta=True, outs=(BF16,),name=f"{tag}_bwd_dwq")
    dw_kv_t, = _matmul(dkvb, mem_n, ta=True, outs=(BF16,),name=f"{tag}_bwd_dwkv")
    dmem_n, = _matmul(dkvb, p["w_kv_t"], name=f"{tag}_bwd_dmemn")
    dg_mem, = _rowwise(lambda x, d, g: _rms_bwd_vals(x, d, g)[1], [mem, dmem_n], [p["mem_norm"]], [], [(1, D_MODEL)],
                       name=f"{tag}_bwd_memnorm")
    dh2, dh2b, dg = _norm_input_grad(dqb, p["w_q"], h, dh, p["norm"], tb=True, name=f"{tag}_bwd_dh")
    return dh2, dh2b, {"norm": dg, "mem_norm": dg_mem, "w_q": dw_q, "w_kv_t": dw_kv_t, "w_o": dw_o}


AB_MAIN = 2 * LRU_WIDTH + 3 * ATT_W


def _ab_fwd(h, hn, p, nb, next_gain, before_out=None):
    w_in_t = p["w_in_t"]
    zug, = _matmul(hn, w_in_t[:2 * LRU_WIDTH], tb=True, name="ab_in_ug")
    qkv, = _matmul(hn, w_in_t[2 * LRU_WIDTH:AB_MAIN], tb=True, outs=(BF16,), tn=768, name="ab_in_qkv")
    zf, = _matmul(hn, w_in_t[AB_MAIN:], tb=True, name="ab_in_f")
    fl = zf[:, :8].reshape(nb, SEQ, 8).transpose(0, 2, 1)
    cbias = _fox_bias(fl, p["b_f"], name="ab_fox_bias")
    kb = cbias.reshape(nb, 8, SEQ // ATT_CHUNK, ATT_CHUNK)
    y_a, hl = _lru_fwd(zug, p["conv_w"], p["conv_b"], p["w_a"], p["b_a"], p["w_i"], p["b_i"], p["lam"], name="ab_lru")
    qkv3 = _seq3(qkv, nb)
    o, lse = _attn_fwd((qkv3, 0), (qkv3, 1), (qkv3, 2), kb, heads=8, dh=HEAD_DIM, mode="causal", name="ab_fox")
    o = _flat2(o)
    y, = _rowwise(lambda a, b: jnp.concatenate([a, b.astype(BF16)], axis=1), [y_a, o], [], [(D_MODEL, BF16)], name="ab_ycat")
    if before_out is not None:
        before_out(y)
    out, hn_next = _residual_out(y, p["w_out"], h, next_gain, name="ab_out")
    return out, hn_next, (h, hn, zug, qkv3, fl, kb, hl, o, lse, y)


def _ab_bwd(dh, dhb, p, saved, nb):
    h, hn, zug, qkv3, fl, kb, hl, o, lse, y = saved
    dy, dyb, delta = _matmul(dhb, p["w_out"], tb=True, outs=(F32, BF16, (F32, LANES)), extras=(y,), consts=(_head_expand(8, ATT_W).T,),
                             epilogue=lambda acc, y, e: (acc, acc, _dot_exact(acc[:, ATT_W:] * y[:, ATT_W:].astype(F32), e)),
                             name="ab_bwd_dy")
    dw_out, = _matmul(y, dhb, ta=True, outs=(BF16,),name="ab_bwd_dwout")
    dzug, dcw, dcb, dwa, dba, dwi, dbi, dlam = _lru_bwd(zug, hl, dy, p["conv_w"], p["conv_b"], p["w_a"], p["b_a"],
                                                        p["w_i"], p["b_i"], p["lam"], name="ab_bwd_lru")
    dq, dk, dv, dkb, drow = _attn_bwd((qkv3, 0), (qkv3, 1), (qkv3, 2), (_seq3(dyb, nb), 1), lse, _seq3(delta, nb), kb,
                                      heads=8, dh=HEAD_DIM, mode="causal", name="ab_bwd_fox")
    dcum = dkb.reshape(nb, 8, SEQ) + drow[:, :, :8].transpose(0, 2, 1)
    dfl, dbf = _fox_bias_bwd(fl, p["b_f"], dcum, name="ab_bwd_fox_bias")
    dzf = jnp.pad(dfl.transpose(0, 2, 1).reshape(nb * SEQ, 8), ((0, 0), (0, LANES - 8)))
    dz, = _rowwise(lambda a, q, k, v, f: jnp.concatenate([a, q.astype(BF16), k.astype(BF16), v.astype(BF16), f.astype(BF16)], axis=1),
                   [dzug, _flat2(dq), _flat2(dk), _flat2(dv), dzf], [], [(AB_MAIN + LANES, BF16)], name="ab_bwd_dzcat")
    dw_in_t, = _matmul(dz, hn, ta=True, outs=(BF16,),tm=384, name="ab_bwd_dwin")
    dh2, dh2b, dg = _norm_input_grad(dz, p["w_in_t"], h, dh, p["norm"], name="ab_bwd_dh")
    grads = {"norm": dg, "w_in_t": dw_in_t[:AB_MAIN + 8], "conv_w": dcw, "conv_b": dcb, "w_a": _diag_blocks(dwa), "b_a": dba,
             "w_i": _diag_blocks(dwi), "b_i": dbi, "lam": dlam, "b_f": dbf.reshape(1, 8), "w_out": dw_out}
    return dh2, dh2b, grads


CD_IN = 3 * ATT_W + ATT_W + 2 * SWA_KW


def _gqa_share():
    src = jnp.arange(SWA_KW)[:, None]
    dst = jnp.arange(ATT_W)[None, :]
    per_kv = ATT_W // (SWA_KW // HEAD_DIM)
    return ((dst // per_kv == src // HEAD_DIM) & (dst % HEAD_DIM == src % HEAD_DIM)).astype(BF16)


def _cd_fwd(h, hn, p, nb, next_gain):
    z, = _matmul(hn, p["w_in_t"], tb=True, tn=1152, name="cd_in")
    cos, sin = _rope_tables()
    per_seq = SEQ // ROW_TILE
    table_map = lambda i: (i % per_seq, 0)

    def rope_fn(z, cos, sin, share):
        qc, kc, vc = z[:, 0:ATT_W], z[:, ATT_W:2 * ATT_W], z[:, 2 * ATT_W:3 * ATT_W]
        qd, kd, vd = z[:, 3 * ATT_W:4 * ATT_W], z[:, 4 * ATT_W:4 * ATT_W + SWA_KW], z[:, 4 * ATT_W + SWA_KW:]
        kd8 = jnp.dot(_rotate(kd, cos, sin).astype(BF16), share, preferred_element_type=F32)
        vd8 = jnp.dot(vd.astype(BF16), share, preferred_element_type=F32)
        qk = jnp.concatenate([_rotate(qc, cos, sin), _rotate(kc, cos, sin)], axis=1)
        return qk, vc, _rotate(qd, cos, sin), kd8, vd8, qk, qk, vc, vc
    dils = [dil for _, dil in DIL_PATTERN if dil > 1]
    outs = _rowwise(rope_fn, [z, cos, sin], [_gqa_share()],
                    [(2 * ATT_W, BF16)] + [(ATT_W, BF16)] * 4 + [(2 * ATT_W, BF16, d) for d in dils] + [(ATT_W, BF16, d) for d in dils],
                    name="cd_rope", row_maps=[None, table_map, table_map])
    qk, vc, qd, kd, vd = outs[:5]
    views = {1: (_seq3(qk, nb), _seq3(vc, nb))}
    for d, qk_d, vc_d in zip(dils, outs[5:5 + len(dils)], outs[5 + len(dils):]):
        views[d] = (_seq3(qk_d, nb), _seq3(vc_d, nb))
    in_classes = lambda a, width, d: _flat2(a) if d == 1 else ("classes", _flat2(a), width, d)
    branch = []
    for window, dil in DIL_PATTERN:
        qk_v, vc_v = views[dil]
        o_i, lse_i = _attn_fwd((qk_v, 0), (qk_v, 1), (vc_v, 0), classes=dil, heads=8, dh=HEAD_DIM, mode="band",
                               max_dist=window // dil, name=f"cd_dil{dil}")
        branch.append((in_classes(o_i, ATT_W, dil), in_classes(lse_i, LANES, dil)))
    expand = _head_expand(8, ATT_W)

    qd3, kd3, vd3 = _seq3(qd, nb), _seq3(kd, nb), _seq3(vd, nb)
    o_d, lse_band = _attn_fwd((qd3, 0), (kd3, 0), (vd3, 0), heads=8, dh=HEAD_DIM, mode="band", max_dist=SWA_WINDOW - 1,
                              name="cd_swa")

    def mix(o1, o2, o3, l1, l2, l3, o_band, l_band, e, sink):
        m = jnp.maximum(jnp.maximum(l1, l2), l3)
        lt = m + jnp.log(jnp.exp(l1 - m) + jnp.exp(l2 - m) + jnp.exp(l3 - m))
        y_c = sum(_dot_exact(jnp.exp(l - lt), e) * o_ for o_, l in ((o1, l1), (o2, l2), (o3, l3)))
        lt_d = jnp.maximum(l_band, sink) + jnp.log(1.0 + jnp.exp(-jnp.abs(l_band - sink)))
        y_d = o_band * _dot_exact(jnp.exp(l_band - lt_d), e)
        return jnp.concatenate([y_c, y_d], axis=1), y_c, lt, y_d, lt_d
    y, y_c, lse_c, y_d, lse_d = _rowwise(
        mix, [b[0] for b in branch] + [b[1] for b in branch] + [_flat2(o_d), _flat2(lse_band)], [expand, p["sink"]],
        [(D_MODEL, BF16), (ATT_W, F32), (LANES, F32), (ATT_W, F32), (LANES, F32)], name="cd_mix")
    out, hn_next = _residual_out(y, p["w_out"], h, next_gain, name="cd_out")
    return out, hn_next, (h, hn, views, qd3, kd3, vd3, y_c, lse_c, y_d, lse_d, y)


def _cd_bwd(dh, dhb, p, saved, nb):
    h, hn, views, qd3, kd3, vd3, y_c, lse_c, y_d, lse_d, y = saved
    dy, dyb = _matmul(dhb, p["w_out"], tb=True, outs=(F32, BF16), epilogue=lambda acc: (acc, acc), name="cd_bwd_dy")
    dw_out, = _matmul(y, dhb, ta=True, outs=(BF16,),name="cd_bwd_dwout")
    dyb3 = _seq3(dyb, nb)
    dils = [dil for _, dil in DIL_PATTERN if dil > 1]
    gather = _head_expand(8, ATT_W).T

    def prepare(do, o, lse, do_d, o_d, lse_d, e, sink):
        delta = _dot_exact(do * o, e)
        delta_d = _dot_exact(do_d * o_d, e)
        dsink = -jnp.sum(jnp.exp(sink - lse_d) * delta_d, axis=0, keepdims=True)
        return (delta,) + (do,) * len(dils) + (lse,) * len(dils) + (delta,) * len(dils) + (delta_d, dsink)
    outs = _rowwise(prepare, [(dy, ATT_W, 0), y_c, lse_c, (dy, ATT_W, 1), y_d, lse_d], [gather, p["sink"]],
                    [(LANES, F32)] + [(ATT_W, BF16, d) for d in dils] + [(LANES, F32, d) for d in dils] * 2 + [(LANES, F32)],
                    [(1, LANES)], name="cd_bwd_delta")
    delta_d, dsink = outs[-2:]
    seen = {1: ((dyb3, 0), _seq3(lse_c, nb), _seq3(outs[0], nb))}
    for j, d in enumerate(dils):
        seen[d] = ((_seq3(outs[1 + j], nb), 0), _seq3(outs[1 + len(dils) + j], nb), _seq3(outs[1 + 2 * len(dils) + j], nb))
    in_classes = lambda a, d: _flat2(a) if d == 1 else ("classes", _flat2(a), ATT_W, d)
    parts = []
    for window, dil in DIL_PATTERN:
        (qk_v, vc_v), (do_v, lse_v, delta_v) = views[dil], seen[dil]
        grads = _attn_bwd((qk_v, 0), (qk_v, 1), (vc_v, 0), do_v, lse_v, delta_v, classes=dil, heads=8, dh=HEAD_DIM, mode="band",
                          max_dist=window // dil, dq_dtype=BF16, name=f"cd_bwd_dil{dil}")
        parts.append([in_classes(a, dil) for a in grads])
    dqd, dkd, dvd = _attn_bwd((qd3, 0), (kd3, 0), (vd3, 0), (dyb3, 1), _seq3(lse_d, nb), _seq3(delta_d, nb), heads=8,
                              dh=HEAD_DIM, mode="band", max_dist=SWA_WINDOW - 1, dq_dtype=BF16, name="cd_bwd_swa")
    cos, sin = _rope_tables()
    per_seq = SEQ // ROW_TILE
    table_map = lambda i: (i % per_seq, 0)

    def unrope(q1, q2, q3, k1, k2, k3, v1, v2, v3, qd, kd8, vd8, cos, sin, gather):
        back = lambda x: _rotate(x.astype(F32), cos, -sin)
        kd, vd = _dot_exact(kd8, gather), _dot_exact(vd8, gather)
        return jnp.concatenate([back(q1 + q2 + q3), back(k1 + k2 + k3), v1 + v2 + v3, back(qd), back(kd), vd], axis=1)
    ins = [parts[b][t] for t in range(3) for b in range(3)] + [_flat2(dqd), _flat2(dkd), _flat2(dvd), cos, sin]
    dz, = _rowwise(unrope, ins, [_gqa_share().T], [(CD_IN, BF16)], name="cd_bwd_unrope",
                   row_maps=[None] * 12 + [table_map, table_map])
    dw_in_t, = _matmul(dz, hn, ta=True, outs=(BF16,),tm=384, name="cd_bwd_dwin")
    dh2, dh2b, dg = _norm_input_grad(dz, p["w_in_t"], h, dh, p["norm"], name="cd_bwd_dh")
    return dh2, dh2b, {"norm": dg, "w_in_t": dw_in_t, "sink": dsink[:, :8], "w_out": dw_out}


def _local_step(x, mem, target, w, complete=None, grads_ready=None):
    nb = x.shape[0]
    h = x.reshape(nb * SEQ, D_MODEL)
    mem2 = mem.reshape(nb * MEM_LEN, D_MODEL)
    tgt = target.reshape(nb * SEQ, D_MODEL)

    hn = _rms_fwd(h, w["ab"]["norm"], name="ab_norm")
    h, hn, s_ab = _ab_fwd(h, hn, w["ab"], nb, w["xa0"]["norm"], before_out=complete)
    h, hn, s_xa0 = _xa_fwd(h, hn, mem2, w["xa0"], "xa0", nb, w["mlp0"]["norm"])
    h, hn, s_mlp0 = _mlp_fwd(h, hn, w["mlp0"], "mlp0", w["cd"]["norm"])
    h, hn, s_cd = _cd_fwd(h, hn, w["cd"], nb, w["xa1"]["norm"])
    h, hn, s_xa1 = _xa_fwd(h, hn, mem2, w["xa1"], "xa1", nb, w["mlp1"]["norm"])
    h, _, s_mlp1 = _mlp_fwd(h, hn, w["mlp1"], "mlp1", None)

    dh, dhb, loss, dg_final = _loss_and_grad(h, tgt, w["final_norm"], name="loss")
    grads = {"final_norm": dg_final}
    dh, dhb, grads["mlp1"] = _mlp_bwd(dh, dhb, w["mlp1"], s_mlp1, "mlp1")
    dh, dhb, grads["xa1"] = _xa_bwd(dh, dhb, mem2, w["xa1"], s_xa1, "xa1", nb)
    dh, dhb, grads["cd"] = _cd_bwd(dh, dhb, w["cd"], s_cd, nb)
    if grads_ready is not None:
        grads_ready(0, grads)
    dh, dhb, grads["mlp0"] = _mlp_bwd(dh, dhb, w["mlp0"], s_mlp0, "mlp0")
    dh, dhb, grads["xa0"] = _xa_bwd(dh, dhb, mem2, w["xa0"], s_xa0, "xa0", nb)
    if grads_ready is not None:
        grads_ready(1, grads)
    dh, dhb, grads["ab"] = _ab_bwd(dh, dhb, w["ab"], s_ab, nb)
    return loss, dh.reshape(nb, SEQ, D_MODEL), grads


ANY = pl.BlockSpec(memory_space=pl.ANY)


def _place():
    x, y, c = lax.axis_index("x"), lax.axis_index("y"), lax.axis_index("c")
    return x, y, c, [(1 - x, y), (x, 1 - y), (1 - x, 1 - y)]


def _gather_chips(shard):
    half = shard.shape[0] // 2

    def body(src, out, send_sems, recv_sems):
        x, y, c, chips = _place()
        sibling = (x, y, 1 - c)

        def rows(slot, core):
            return out.at[slot, pl.ds(core * half, half)]

        def copy(k, src_ref, dst_ref, to):
            return pltpu.make_async_remote_copy(src_ref=src_ref, dst_ref=dst_ref, send_sem=send_sems.at[k],
                                                recv_sem=recv_sems.at[k], device_id=to, device_id_type=MESH)

        first = [copy(k, src.at[pl.ds(c * half, half)], rows(2 * x + y, c), (px, py, c)) for k, (px, py) in enumerate(chips)]
        for cp in first:
            cp.start()
        passed = [copy(3 + k, rows(2 * px + py, c), rows(2 * px + py, c), sibling) for k, (px, py) in enumerate(chips)]
        for k, (px, py) in enumerate(chips):
            copy(k, src.at[pl.ds(c * half, half)], rows(2 * px + py, c), (px, py, c)).wait_recv()
            passed[k].start()
        for k, (px, py) in enumerate(chips):
            copy(3 + k, rows(2 * px + py, 1 - c), rows(2 * px + py, 1 - c), sibling).wait_recv()
        for cp in first + passed:
            cp.wait_send()

    return pl.pallas_call(
        body, out_shape=jax.ShapeDtypeStruct((N_CHIPS,) + shard.shape, shard.dtype), in_specs=[ANY], out_specs=ANY,
        scratch_shapes=[pltpu.SemaphoreType.DMA((6,)), pltpu.SemaphoreType.DMA((6,))], name="gather_chips",
    )(shard)


HBM = pl.BlockSpec(memory_space=pltpu.HBM)
SEM = pl.BlockSpec(memory_space=pltpu.SEMAPHORE)
SIDE_EFFECT = pltpu.SideEffectType.DATAFLOW_SIDE_EFFECTING


def _chip_copies(src, land, send_sems, recv_sems):
    half = src.shape[0] // 2
    x, y, c, chips = _place()

    def copy(k, slot, to):
        return pltpu.make_async_remote_copy(src_ref=src.at[pl.ds(c * half, half)], dst_ref=land.at[slot, pl.ds(c * half, half)],
                                            send_sem=send_sems[k], recv_sem=recv_sems[k], device_id=to, device_id_type=MESH)
    out = [copy(k, 2 * x + y, (px, py, c)) for k, (px, py) in enumerate(chips)]
    back = [copy(k, 2 * px + py, (px, py, c)) for k, (px, py) in enumerate(chips)]
    return out, back


def _gather_start(shard, after):
    def body(src, land, *rest):
        rest = rest[len(after):]
        sems, token = rest[:6], rest[8]
        out, _ = _chip_copies(src, land, sems[:3], sems[3:])
        for cp in out:
            cp.start()
        token[...] = jnp.zeros(token.shape, F32)

    sem = pltpu.SemaphoreType.DMA(())
    land_shape = (N_CHIPS,) + shard.shape
    return pl.pallas_call(
        body, name="gather_start",
        out_shape=(sem,) * 6 + (pltpu.HBM(shard.shape, shard.dtype), pltpu.HBM(land_shape, shard.dtype),
                                jax.ShapeDtypeStruct((8, LANES), F32)),
        in_specs=(HBM, HBM) + (pl.BlockSpec(memory_space=pl.ANY),) * len(after),
        out_specs=(SEM,) * 6 + (HBM, HBM, pl.BlockSpec(memory_space=pltpu.VMEM)),
        input_output_aliases={0: 6, 1: 7}, compiler_params=pltpu.CompilerParams(has_side_effects=SIDE_EFFECT),
    )(pltpu.with_memory_space_constraint(shard, pltpu.HBM),
      pltpu.with_memory_space_constraint(lax.empty(land_shape, shard.dtype), pltpu.HBM), *after)


def _gather_wait(started, after):
    sems, src_thru, land_thru = started[:6], started[6], started[7]

    def body(src, land, *rest):
        out, back = _chip_copies(src, land, rest[:3], rest[3:6])
        for cp in out:
            cp.wait_send()
        for cp in back:
            cp.wait_recv()

    return pl.pallas_call(
        body, name="gather_wait",
        out_shape=(pltpu.HBM(src_thru.shape, src_thru.dtype), pltpu.HBM(land_thru.shape, land_thru.dtype)),
        in_specs=(HBM, HBM) + (SEM,) * 6 + (pl.BlockSpec(memory_space=pl.ANY),), out_specs=(HBM, HBM),
        input_output_aliases={0: 0, 1: 1}, compiler_params=pltpu.CompilerParams(has_side_effects=SIDE_EFFECT),
    )(src_thru, land_thru, *sems, after)


def _gather_pass(land):
    half = land.shape[1] // 2

    def body(land_in, land_out, send_sems, recv_sems):
        x, y, c, chips = _place()

        def copy(k, slot, core):
            rows = land_out.at[slot, pl.ds(core * half, half)]
            return pltpu.make_async_remote_copy(src_ref=rows, dst_ref=rows, send_sem=send_sems.at[k], recv_sem=recv_sems.at[k],
                                                device_id=(x, y, 1 - c), device_id_type=MESH)
        sends = [copy(k, 2 * px + py, c) for k, (px, py) in enumerate(chips)]
        for cp in sends:
            cp.start()
        for k, (px, py) in enumerate(chips):
            copy(k, 2 * px + py, 1 - c).wait_recv()
        for cp in sends:
            cp.wait_send()

    return pl.pallas_call(
        body, out_shape=jax.ShapeDtypeStruct(land.shape, land.dtype), in_specs=[ANY], out_specs=ANY,
        scratch_shapes=[pltpu.SemaphoreType.DMA((3,)), pltpu.SemaphoreType.DMA((3,))], input_output_aliases={0: 0},
        name="gather_pass",
    )(land)


def _swap_cores(block, *, name, half_rows=None):
    shape = block.shape if half_rows is None else (block.shape[0], half_rows, block.shape[2])

    def body(src, out, send_sem, recv_sem):
        x, y, c, _ = _place()
        part = src if half_rows is None else src.at[:, pl.ds((1 - c) * half_rows, half_rows)]
        cp = pltpu.make_async_remote_copy(src_ref=part, dst_ref=out, send_sem=send_sem, recv_sem=recv_sem,
                                          device_id=(x, y, 1 - c), device_id_type=MESH)
        cp.start()
        cp.wait()

    return pl.pallas_call(
        body, out_shape=jax.ShapeDtypeStruct(shape, block.dtype), in_specs=[ANY], out_specs=ANY,
        scratch_shapes=[pltpu.SemaphoreType.DMA(()), pltpu.SemaphoreType.DMA(())], name=name,
    )(block)


def _scatter_copies(parts, land, send_sems, recv_sems):
    x, y, c, chips = _place()
    return [pltpu.make_async_remote_copy(src_ref=parts.at[2 * px + py], dst_ref=land.at[k], send_sem=send_sems[k],
                                         recv_sem=recv_sems[k], device_id=(px, py, c), device_id_type=MESH)
            for k, (px, py) in enumerate(chips)]


def _scatter_start(parts, tag):
    def body(src, land, *rest):
        for cp in _scatter_copies(src, land, rest[:3], rest[3:6]):
            cp.start()
        rest[8][...] = jnp.zeros(rest[8].shape, F32)

    sem = pltpu.SemaphoreType.DMA(())
    land_shape = (3,) + parts.shape[1:]
    return pl.pallas_call(
        body, name=f"scatter_start_{tag}",
        out_shape=(sem,) * 6 + (pltpu.HBM(parts.shape, parts.dtype), pltpu.HBM(land_shape, parts.dtype),
                                jax.ShapeDtypeStruct((8, LANES), F32)),
        in_specs=(HBM, HBM), out_specs=(SEM,) * 6 + (HBM, HBM, pl.BlockSpec(memory_space=pltpu.VMEM)),
        input_output_aliases={0: 6, 1: 7}, compiler_params=pltpu.CompilerParams(has_side_effects=SIDE_EFFECT),
    )(pltpu.with_memory_space_constraint(parts, pltpu.HBM),
      pltpu.with_memory_space_constraint(lax.empty(land_shape, parts.dtype), pltpu.HBM))


def _scatter_wait(started, after, tag):
    def body(src, land, *rest):
        for cp in _scatter_copies(src, land, rest[:3], rest[3:6]):
            cp.wait_send()
            cp.wait_recv()

    src_thru, land_thru = started[6], started[7]
    return pl.pallas_call(
        body, name=f"scatter_wait_{tag}",
        out_shape=(pltpu.HBM(src_thru.shape, src_thru.dtype), pltpu.HBM(land_thru.shape, land_thru.dtype)),
        in_specs=(HBM, HBM) + (SEM,) * 6 + (pl.BlockSpec(memory_space=pl.ANY),), out_specs=(HBM, HBM),
        input_output_aliases={0: 0, 1: 1}, compiler_params=pltpu.CompilerParams(has_side_effects=SIDE_EFFECT),
    )(src_thru, land_thru, *started[:6], after)[1]


def _sum_all_devices(vec):
    rows = vec.shape[0]

    def body(x_ref, total_ref, all_ref, send_sems, recv_sems, local_sem):
        x, y, c, chips = _place()
        me, sibling = (x, y, c), (x, y, 1 - c)

        def slot(px, py, pc):
            return all_ref.at[4 * px + 2 * py + pc]

        def copy(k, block, to, src=None):
            return pltpu.make_async_remote_copy(src_ref=slot(*block) if src is None else src, dst_ref=slot(*block),
                                                send_sem=send_sems.at[k], recv_sem=recv_sems.at[k], device_id=to,
                                                device_id_type=MESH)

        mine = pltpu.make_async_copy(x_ref, slot(*me), local_sem)
        mine.start()
        first = [copy(0, me, sibling, src=x_ref)] + [copy(1 + j, me, (*chip, c), src=x_ref) for j, chip in enumerate(chips)]
        for cp in first:
            cp.start()
        passed = [copy(4 + j, (*chip, c), sibling) for j, chip in enumerate(chips)]
        for j, chip in enumerate(chips):
            copy(1 + j, (*chip, c), me).wait_recv()
            passed[j].start()
        copy(0, sibling, me).wait_recv()
        for j, chip in enumerate(chips):
            copy(4 + j, (*chip, 1 - c), me).wait_recv()
        for cp in first + passed:
            cp.wait_send()
        mine.wait()
        acc = all_ref[0]
        for d in range(1, 8):
            acc = acc + all_ref[d]
        total_ref[...] = acc

    vmem = pl.BlockSpec(memory_space=pltpu.VMEM)
    return pl.pallas_call(
        body, out_shape=[jax.ShapeDtypeStruct((rows, LANES), F32), jax.ShapeDtypeStruct((8, rows, LANES), F32)],
        in_specs=[vmem], out_specs=[vmem, vmem],
        scratch_shapes=[pltpu.SemaphoreType.DMA((7,)), pltpu.SemaphoreType.DMA((7,)), pltpu.SemaphoreType.DMA(())],
        name="sum_all_devices", compiler_params=pltpu.CompilerParams(vmem_limit_bytes=VMEM_LIMIT_BYTES),
    )(vec)[0]


PACK_COLS = 1024
BIG = (("mlp_w_up", 2, 1024, True), ("mlp_w_down", 2, 1024, False), ("xa_w_kv", 2, 512, True), ("xa_w_q", 2, 256, False),
       ("xa_w_o", 2, 256, False), ("ab_w_out", 1, 256, False), ("cd_w_out", 1, 256, False), ("cd_w_in", 1, 576, True),
       ("ab_w_in", 1, 642, True))
ENTRIES = tuple((name, layer, rows, transposed) for name, layers, rows, transposed in BIG for layer in range(layers))
FIRST_ENTRIES = tuple(e for e in ENTRIES if e[0] == "ab_w_in")
LATER_ENTRIES = tuple(e for e in ENTRIES if e[0] != "ab_w_in")


def _tile_rows(entry):
    return -(-entry[2] // 16) * 16


def _padded_rows(entries):
    return -(-sum(_tile_rows(e) for e in entries) // 32) * 32


SMALL = (("ab_norm", (1, 1024)), ("ab_conv_w", (1, 4, 512)), ("ab_conv_b", (1, 512)), ("lru_w_a", (1, 8, 64, 64)),
         ("lru_b_a", (1, 512)), ("lru_w_i", (1, 8, 64, 64)), ("lru_b_i", (1, 512)), ("lru_lambda", (1, 512)),
         ("fox_b_f", (1, 8)), ("cd_norm", (1, 1024)), ("cd_sink", (1, 8)), ("xa_norm", (2, 1024)),
         ("xa_mem_norm", (2, 1024)), ("mlp_norm", (2, 1024)), ("final_norm", (1024,)), ("loss", (1,)))
SPLIT_SMALL = ("ab_conv_w", "cd_norm")


def _pack_rows(parts, entries, dtype):
    lead = parts[0].shape[:-2]
    zeros = lambda rows: jnp.zeros(lead + (rows, PACK_COLS), dtype)
    pieces = [jnp.pad(p.astype(dtype), ((0, 0),) * len(lead) + ((0, _tile_rows(e) - e[2]), (0, 0))) for p, e in zip(parts, entries)]
    tail = _padded_rows(entries) - sum(_tile_rows(e) for e in entries)
    return jnp.concatenate(pieces + ([zeros(tail)] if tail else []), axis=len(lead))


def _unpack_rows(packed, entries):
    out, r0 = [], 0
    for e in entries:
        out.append(packed[..., r0:r0 + e[2], :])
        r0 += _tile_rows(e)
    return out


def _shard_rows(w, entries):
    return [(w[name][layer].T if transposed else w[name][layer]) for name, layer, _, transposed in entries]


def _shard_blocks(rows):
    out = {}
    for (name, layer, _, transposed), r in zip(ENTRIES, rows):
        out.setdefault(name, []).append(r.T if transposed else r)
    return {name: jnp.stack(layers) for name, layers in out.items()}


def _pack_small(vals):
    flat = jnp.concatenate([vals[name].astype(F32).reshape(-1) for name, _ in SMALL])
    size = -(-flat.shape[0] // (8 * LANES)) * (8 * LANES)
    return jnp.pad(flat, (0, size - flat.shape[0])).reshape(-1, LANES)


def _unpack_small(packed):
    flat, out, at = packed.reshape(-1), {}, 0
    for name, shape in SMALL:
        out[name] = flat[at:at + math.prod(shape)].reshape(shape)
        at += math.prod(shape)
    return out


def _chip_part(full, chip):
    width = full.shape[-1] // N_CHIPS
    return lax.dynamic_slice_in_dim(full, chip * width, width, axis=full.ndim - 1)


def _chip_embed(part, chip):
    full = jnp.zeros(part.shape[:-1] + (part.shape[-1] * N_CHIPS,), part.dtype)
    return lax.dynamic_update_slice_in_dim(full, part, chip * part.shape[-1], axis=part.ndim - 1)


SUBLAYER_KEYS = {"ab_w_in": ("ab", "w_in_t"), "ab_w_out": ("ab", "w_out"), "cd_w_in": ("cd", "w_in_t"), "cd_w_out": ("cd", "w_out"),
                 "xa_w_q": ("xa", "w_q"), "xa_w_kv": ("xa", "w_kv_t"), "xa_w_o": ("xa", "w_o"), "mlp_w_up": ("mlp", "w_up_t"),
                 "mlp_w_down": ("mlp", "w_down")}


def _sublayer(name, layer):
    block, leaf = SUBLAYER_KEYS[name]
    return (block if block in ("ab", "cd") else f"{block}{layer}"), leaf


def _set_big(params, full_rows):
    for (name, layer), rows in full_rows.items():
        block, leaf = _sublayer(name, layer)
        if name == "ab_w_in":
            rows = jnp.concatenate([rows, jnp.zeros((LANES - 8, PACK_COLS), rows.dtype)])
        params[block][leaf] = rows


def _build_params(full_rows, w):
    pad = lambda a: jnp.pad(a, ((0, 0), (0, LANES - a.shape[1])))
    params = {
        "ab": dict(norm=w["ab_norm"], conv_w=w["ab_conv_w"][0], conv_b=w["ab_conv_b"], w_a=_block_diag(w["lru_w_a"][0]),
                   b_a=w["lru_b_a"], w_i=_block_diag(w["lru_w_i"][0]), b_i=w["lru_b_i"], lam=w["lru_lambda"],
                   b_f=w["fox_b_f"].reshape(8, 1)),
        "cd": dict(norm=w["cd_norm"], sink=pad(w["cd_sink"])),
        "final_norm": w["final_norm"].reshape(1, D_MODEL),
    }
    for layer in range(2):
        params[f"xa{layer}"] = dict(norm=w["xa_norm"][layer:layer + 1], mem_norm=w["xa_mem_norm"][layer:layer + 1])
        params[f"mlp{layer}"] = dict(norm=w["mlp_norm"][layer:layer + 1])
    _set_big(params, full_rows)
    return params


def _grad_rows(g, entries=ENTRIES):
    out = []
    for name, layer, _, _ in entries:
        block, leaf = _sublayer(name, layer)
        out.append(g[block][leaf])
    return out


def _reduce_group(entry):
    name, layer = entry[0], entry[1]
    if name.startswith("ab_"):
        return 2
    return 0 if layer == 1 or name.startswith("cd_") else 1


REDUCE_GROUPS = tuple(tuple(e for e in ENTRIES if _reduce_group(e) == group) for group in range(3))


def _reduce_tile(half):
    return max(t for t in range(16, 1025, 16) if half % t == 0)


def _reduce_start(grads_by_chip, core, chip, tag):
    half = grads_by_chip.shape[1] // 2
    tile = _reduce_tile(half)
    steps = half // tile
    place = jnp.stack([core, chip]).astype(jnp.int32)
    got = _swap_cores(grads_by_chip, name=f"swap_cores_{tag}", half_rows=half)
    block = (1, tile, PACK_COLS)

    def sum_cores(place_ref, mine_ref, got_ref, f32_ref, bf16_ref):
        total = mine_ref[...].astype(F32) + got_ref[...].astype(F32)
        f32_ref[...] = total
        bf16_ref[...] = total.astype(BF16)

    pair32, pair16 = pl.pallas_call(
        sum_cores, name=f"sum_cores_{tag}",
        grid_spec=pltpu.PrefetchScalarGridSpec(
            num_scalar_prefetch=1, grid=(N_CHIPS, steps),
            in_specs=[pl.BlockSpec(block, lambda s, i, place_ref: (s, place_ref[0] * steps + i, 0)),
                      pl.BlockSpec(block, lambda s, i, place_ref: (s, i, 0))],
            out_specs=[pl.BlockSpec(block, lambda s, i, place_ref: (s, i, 0))] * 2),
        out_shape=[jax.ShapeDtypeStruct((N_CHIPS, half, PACK_COLS), F32), jax.ShapeDtypeStruct((N_CHIPS, half, PACK_COLS), BF16)],
        compiler_params=_params("arbitrary", "arbitrary"),
    )(place, grads_by_chip, got)
    return pair32, _scatter_start(pair16, tag), place


def _reduce_finish(state, core, tag, after):
    pair32, started, place = state
    half = pair32.shape[1]
    tile = _reduce_tile(half)
    landed = _scatter_wait(started, after, tag)

    def sum_chips(place_ref, own_ref, landed_ref, out_ref):
        out_ref[...] = own_ref[0] + landed_ref[0].astype(F32) + landed_ref[1].astype(F32) + landed_ref[2].astype(F32)

    done = pl.pallas_call(
        sum_chips, name=f"sum_chips_{tag}",
        grid_spec=pltpu.PrefetchScalarGridSpec(
            num_scalar_prefetch=1, grid=(half // tile,),
            in_specs=[pl.BlockSpec((1, tile, PACK_COLS), lambda i, place_ref: (place_ref[1], i, 0)),
                      pl.BlockSpec((3, tile, PACK_COLS), lambda i, place_ref: (0, i, 0))],
            out_specs=pl.BlockSpec((tile, PACK_COLS), lambda i, place_ref: (i, 0))),
        out_shape=jax.ShapeDtypeStruct((half, PACK_COLS), F32), compiler_params=_params("arbitrary"),
    )(place, pair32, landed)
    theirs = _swap_cores(done, name=f"share_halves_{tag}")
    return jnp.where(core == 0, jnp.concatenate([done, theirs]), jnp.concatenate([theirs, done]))


def kernel(x, mem, ab_norm, ab_w_in, ab_conv_w, ab_conv_b, lru_w_a, lru_b_a, lru_w_i, lru_b_i, lru_lambda, fox_b_f, ab_w_out, cd_norm, cd_w_in, cd_sink, cd_w_out, xa_norm, xa_mem_norm, xa_w_q, xa_w_kv, xa_w_o, mlp_norm, mlp_w_up, mlp_w_down, final_norm, loss_target, m_ab_norm, m_ab_w_in, m_ab_conv_w, m_ab_conv_b, m_lru_w_a, m_lru_b_a, m_lru_w_i, m_lru_b_i, m_lru_lambda, m_fox_b_f, m_ab_w_out, m_cd_norm, m_cd_w_in, m_cd_sink, m_cd_w_out, m_xa_norm, m_xa_mem_norm, m_xa_w_q, m_xa_w_kv, m_xa_w_o, m_mlp_norm, m_mlp_w_up, m_mlp_w_down, m_final_norm, v_ab_norm, v_ab_w_in, v_ab_conv_w, v_ab_conv_b, v_lru_w_a, v_lru_b_a, v_lru_w_i, v_lru_b_i, v_lru_lambda, v_fox_b_f, v_ab_w_out, v_cd_norm, v_cd_w_in, v_cd_sink, v_cd_w_out, v_xa_norm, v_xa_mem_norm, v_xa_w_q, v_xa_w_kv, v_xa_w_o, v_mlp_norm, v_mlp_w_up, v_mlp_w_down, v_final_norm):
    given = dict(locals())
    weight_names = [name for name, _ in SMALL if name != "loss"] + [name for name, _, _, _ in BIG]
    w = {name: given[name] for name in weight_names}
    chip = 2 * lax.axis_index("x") + lax.axis_index("y")
    core = lax.axis_index("c")

    slot = lax.broadcasted_iota(jnp.int32, (N_CHIPS, 1, 1), 0)

    def whole(entries, mine, gathered):
        return {(name, layer): jnp.where(slot == chip, own[None], got).reshape(N_CHIPS * rows, PACK_COLS)
                for (name, layer, rows, _), own, got in zip(entries, _unpack_rows(mine, entries), _unpack_rows(gathered, entries))}

    mine_first = _pack_rows(_shard_rows(w, FIRST_ENTRIES), FIRST_ENTRIES, BF16)
    mine_later = _pack_rows(_shard_rows(w, LATER_ENTRIES), LATER_ENTRIES, BF16)
    first = _gather_chips(mine_first)
    owner = (core == 0).astype(F32)
    quarters = {name: _chip_embed(w[name], chip) * owner for name in SPLIT_SMALL}
    zeros = {name: jnp.zeros(shape, F32) for name, shape in SMALL}
    small_packed = _sum_all_devices(_pack_small({**zeros, **quarters}))
    small_full = _unpack_small(small_packed)
    started = _gather_start(mine_later, after=(small_packed, first))
    params = _build_params(whole(FIRST_ENTRIES, mine_first, first),
                           {**w, "ab_conv_w": small_full["ab_conv_w"], "cd_norm": small_full["cd_norm"]})
    params["ab"]["norm"] = params["ab"]["norm"] + started[8][0, 0]

    def complete(h):
        mine, landed = _gather_wait(started, after=h)
        _set_big(params, whole(LATER_ENTRIES, mine, _gather_pass(landed)))

    reducing = {}

    def grads_ready(group, g):
        entries = REDUCE_GROUPS[group]
        by_chip = _pack_rows([r.reshape(N_CHIPS, e[2], PACK_COLS) for r, e in zip(_grad_rows(g, entries), entries)], entries, BF16)
        reducing[group] = _reduce_start(by_chip, core, chip, f"g{group}")
        if group < 2:
            block, leaf = ("mlp0", "w_down") if group == 0 else ("ab", "w_out")
            params[block][leaf] = params[block][leaf] + reducing[group][1][8][0, 0].astype(BF16)

    loss_part, grad_x, g = _local_step(x, mem, loss_target, params, complete, grads_ready)
    grads_ready(2, g)
    reduced, after = {}, reducing[2][1][8]
    for group, entries in enumerate(REDUCE_GROUPS):
        after = _reduce_finish(reducing[group], core, f"g{group}", after=after)
        reduced.update({(e[0], e[1]): r for e, r in zip(entries, _unpack_rows(after, entries))})
    grads = _shard_blocks([reduced[e[0], e[1]] for e in ENTRIES])
    pair = lambda key, leaf: jnp.stack([g[f"{key}0"][leaf], g[f"{key}1"][leaf]])

    small_local = {"ab_norm": g["ab"]["norm"], "ab_conv_w": g["ab"]["conv_w"], "ab_conv_b": g["ab"]["conv_b"],
                   "lru_w_a": g["ab"]["w_a"], "lru_b_a": g["ab"]["b_a"], "lru_w_i": g["ab"]["w_i"], "lru_b_i": g["ab"]["b_i"],
                   "lru_lambda": g["ab"]["lam"], "fox_b_f": g["ab"]["b_f"], "cd_norm": g["cd"]["norm"], "cd_sink": g["cd"]["sink"],
                   "xa_norm": pair("xa", "norm"), "xa_mem_norm": pair("xa", "mem_norm"), "mlp_norm": pair("mlp", "norm"),
                   "final_norm": g["final_norm"], "loss": loss_part[0, :1]}
    small_sum_packed = _sum_all_devices(_pack_small(small_local))
    small_sum = _unpack_small(small_sum_packed)
    loss = small_sum["loss"][0]

    delta, new_m, new_v = {}, {}, {}
    for name, _, _, _ in BIG:
        shape = w[name].shape
        flat = lambda a: a.reshape(-1, shape[-1])
        d, m2, v2 = _adamw(flat(w[name]), flat(grads[name]), flat(given["m_" + name]), flat(given["v_" + name]), name=f"adamw_{name}")
        delta[name], new_m[name], new_v[name] = d.reshape(shape), m2.reshape(shape), v2.reshape(shape)

    def small_state(prefix):
        vals = {name: (given[prefix + name] if name != "loss" else jnp.zeros((1,), F32)) for name, _ in SMALL}
        for name in SPLIT_SMALL:
            vals[name] = _chip_embed(vals[name], chip)
        return _pack_small(vals)
    packed = _adamw(small_state(""), small_sum_packed, small_state("m_"), small_state("v_"), name="adamw_small")
    for store, vals in zip((delta, new_m, new_v), packed):
        for name, val in _unpack_small(vals).items():
            store[name] = _chip_part(val, chip) if name in SPLIT_SMALL else val
    for name in SPLIT_SMALL:
        small_sum[name] = _chip_part(small_sum[name], chip)
    grads.update({name: small_sum[name] for name, _ in SMALL})

    order = ["ab_norm", "ab_w_in", "ab_conv_w", "ab_conv_b", "lru_w_a", "lru_b_a", "lru_w_i", "lru_b_i", "lru_lambda", "fox_b_f",
             "ab_w_out", "cd_norm", "cd_w_in", "cd_sink", "cd_w_out", "xa_norm", "xa_mem_norm", "xa_w_q", "xa_w_kv", "xa_w_o",
             "mlp_norm", "mlp_w_up", "mlp_w_down", "final_norm"]
    return (loss, grad_x, *[grads[n] for n in order], *[delta[n] for n in order], *[new_m[n] for n in order],
            *[new_v[n] for n in order])
```

```python
import functools
import math

import jax
import jax.numpy as jnp
from jax import lax
from jax.experimental import pallas as pl
from jax.experimental.pallas import tpu as pltpu

F32 = jnp.float32
BF16 = jnp.bfloat16
MESH = pl.DeviceIdType.MESH

D_MODEL = 1024
SEQ = 2048
HEAD_DIM = 64
LRU_WIDTH = 512
LRU_BLOCKS = 8
LRU_C = 8.0
CONV_WIDTH = 4
ATT_W = 512
SWA_KW = 128
SWA_WINDOW = 128
DIL_PATTERN = ((128, 1), (512, 4), (2048, 16))
MEM_LEN = 256
XA_HEADS = 4
XA_HEAD_DIM = 256
D_FF = 4096
ROPE_THETA = 10000.0
EPS = 1e-6
N_CHIPS = 4
LANES = 128

ADAM_LR, ADAM_B1, ADAM_B2, ADAM_EPS, ADAM_WD, ADAM_STEP = 0.001, 0.9, 0.999, 1e-08, 0.01, 10

VMEM_LIMIT_BYTES = 56 * 1024 * 1024
MASKED = -1e30
ROW_TILE = 512
LRU_CHUNK = 512
ATT_BLOCK = 128
BAND_ROWS = 256
FULL_ROWS = 512
ATT_CHUNK = 512
CAUSAL_ROWS = 256
CAUSAL_ROWS_BACKWARD = 512
CLASS_ROWS = 512


def _params(*sem):
    return pltpu.CompilerParams(dimension_semantics=sem, vmem_limit_bytes=VMEM_LIMIT_BYTES)


def _rowwise(fn, rows, consts=(), out_rows=(), out_accs=(), *, name, tile=ROW_TILE, row_maps=None):
    rows = [r if isinstance(r, tuple) else (r, r.shape[1], 0) for r in rows]
    consts = list(consts)
    nr, nc, no = len(rows), len(consts), len(out_rows)
    dealt_in = [r[3] if isinstance(r[0], str) else None for r in rows]
    dealt_out = [o[2] if len(o) == 3 else None for o in out_rows]
    total = next(r[0].shape[0] for r in rows if not isinstance(r[0], str))
    tile = min(tile, total)
    assert total % tile == 0
    n = total // tile
    n_acc = len(out_accs)

    def body(*refs):
        scratch = list(refs[nr + nc + no + n_acc:])
        vals = []
        for ref, d, row in zip(refs[:nr], dealt_in, rows):
            if d is None:
                vals.append(ref[...])
                continue
            sc, width = scratch.pop(0), row[2]
            for r in range(d):
                for c in range(width // LANES):
                    lanes = slice(r * width + c * LANES, r * width + (c + 1) * LANES)
                    sc.at[c][pl.ds(r, tile // d, stride=d), :] = ref[:, lanes].astype(F32)
            vals.append(jnp.concatenate([sc[c] for c in range(width // LANES)], axis=1))
        outs = fn(*vals, *[r[...] for r in refs[nr:nr + nc]])
        outs = tuple(outs) if isinstance(outs, (tuple, list)) else (outs,)
        for ref, val, d, spec in zip(refs[nr + nc:nr + nc + no], outs[:no], dealt_out, out_rows):
            if d is None:
                ref[...] = val.astype(ref.dtype)
                continue
            sc, width = scratch.pop(0), spec[0]
            for c in range(width // LANES):
                sc[c] = val[:, c * LANES:(c + 1) * LANES].astype(F32)
            for r in range(d):
                for c in range(width // LANES):
                    lanes = slice(r * width + c * LANES, r * width + (c + 1) * LANES)
                    ref[:, lanes] = sc.at[c][pl.ds(r, tile // d, stride=d), :].astype(ref.dtype)
        for ref, val in zip(refs[nr + nc + no:nr + nc + no + n_acc], outs[no:]):
            @pl.when(pl.program_id(0) == 0)
            def _(ref=ref):
                ref[...] = jnp.zeros(ref.shape, ref.dtype)
            ref[...] += val

    in_specs, args, scratch_shapes = [], [], []
    for idx, row in enumerate(rows):
        if isinstance(row[0], str):
            _, arr, width, d = row
            in_specs.append(pl.BlockSpec((tile // d, d * width), lambda i: (i, 0)))
            scratch_shapes.append(pltpu.VMEM((width // LANES, tile, LANES), F32))
        elif row_maps is not None and row_maps[idx] is not None:
            arr, width, _ = row
            in_specs.append(pl.BlockSpec((tile, width), row_maps[idx]))
        else:
            arr, width, cb = row
            in_specs.append(pl.BlockSpec((tile, width), functools.partial(lambda i, cb: (i, cb), cb=cb)))
        args.append(arr)
    in_specs += [pl.BlockSpec(c.shape, lambda i: (0, 0)) for c in consts]
    out_shape, out_specs = [], []
    for spec, d in zip(out_rows, dealt_out):
        w, dt = spec[0], spec[1]
        if d is None:
            out_shape.append(jax.ShapeDtypeStruct((total, w), dt))
            out_specs.append(pl.BlockSpec((tile, w), lambda i: (i, 0)))
        else:
            out_shape.append(jax.ShapeDtypeStruct((total // d, d * w), dt))
            out_specs.append(pl.BlockSpec((tile // d, d * w), lambda i: (i, 0)))
            scratch_shapes.append(pltpu.VMEM((w // LANES, tile, LANES), F32))
    out_shape += [jax.ShapeDtypeStruct(s, F32) for s in out_accs]
    out_specs += [pl.BlockSpec(s, lambda i: (0, 0)) for s in out_accs]
    return pl.pallas_call(
        body, grid=(n,), in_specs=in_specs, out_specs=out_specs, out_shape=out_shape, scratch_shapes=scratch_shapes, name=name,
        compiler_params=_params("arbitrary"),
    )(*args, *consts)


MATMUL_VMEM_BYTES = 40 * 1024 * 1024


def _matmul_tiles(m, n, k, tm, tn, out_bytes):
    tm, tn, tk = min(tm, m), min(tn, n), k

    def need():
        return 2 * (2 * tk * (tm + tn) + tm * tn * out_bytes) + (0 if tk == k else 4 * tm * tn)

    while need() > MATMUL_VMEM_BYTES:
        if tm >= tn and tm % 256 == 0:
            tm //= 2
        elif tn % 256 == 0:
            tn //= 2
        else:
            tk //= 2
    return tm, tn, tk


def _matmul(a, b, *, ta=False, tb=False, outs=(F32,), epilogue=None, extras=(), consts=(), sums=(), whole_rows=False,
            tm=1024, tn=1024, name):
    m, k = (a.shape[1], a.shape[0]) if ta else a.shape
    n = b.shape[0] if tb else b.shape[1]
    assert (b.shape[1] if tb else b.shape[0]) == k
    outs = [o if isinstance(o, tuple) else (o, None) for o in outs]
    out_bytes = sum(jnp.dtype(dt).itemsize for dt, w in outs if w is None) + sum(e.dtype.itemsize for e in extras)
    tm, tn, tk = _matmul_tiles(m, n, k, tm, tn, out_bytes)
    assert m % tm == 0 and n % tn == 0 and k % tk == 0, (name, m, n, k)
    nk = k // tk
    ne, nc, no = len(extras), len(consts), len(outs)
    assert tn == n or not (sums or whole_rows or any(w is not None for _, w in outs)), name
    dims = (((0 if ta else 1,), (1 if tb else 0,)), ((), ()))

    def body(*refs):
        a_ref, b_ref = refs[:2]
        e_refs, o_refs = refs[2:2 + ne + nc], refs[2 + ne + nc:2 + ne + nc + no]
        s_refs = refs[2 + ne + nc + no:2 + ne + nc + no + len(sums)]

        def finish(acc):
            vals = (acc,) if epilogue is None else epilogue(acc, *[e[...] for e in e_refs])
            for ref, val in zip(o_refs, vals[:no]):
                ref[...] = val.astype(ref.dtype)
            for ref, val in zip(s_refs, vals[no:]):
                @pl.when(pl.program_id(0) == 0)
                def _(ref=ref, val=val):
                    ref[...] = val

                @pl.when(pl.program_id(0) > 0)
                def _(ref=ref, val=val):
                    ref[...] += val

        prod = lax.dot_general(a_ref[...], b_ref[...], dims, preferred_element_type=F32)
        if nk == 1:
            finish(prod)
        else:
            acc_ref = refs[-1]
            step = pl.program_id(2)

            @pl.when(step == 0)
            def _():
                acc_ref[...] = prod

            @pl.when(step > 0)
            def _():
                acc_ref[...] += prod

            @pl.when(step == nk - 1)
            def _():
                finish(acc_ref[...])

    a_spec = pl.BlockSpec((tk, tm), lambda i, j, s: (s, i)) if ta else pl.BlockSpec((tm, tk), lambda i, j, s: (i, s))
    b_spec = pl.BlockSpec((tn, tk), lambda i, j, s: (j, s)) if tb else pl.BlockSpec((tk, tn), lambda i, j, s: (s, j))
    tile_spec = pl.BlockSpec((tm, tn), lambda i, j, s: (i, j))
    whole = lambda shape: pl.BlockSpec(shape, lambda i, j, s: (0, 0))
    return pl.pallas_call(
        body, grid=(m // tm, n // tn, nk),
        in_specs=[a_spec, b_spec] + [tile_spec] * ne + [whole(c.shape) for c in consts],
        out_specs=[tile_spec if w is None else pl.BlockSpec((tm, w), lambda i, j, s: (i, 0)) for _, w in outs]
        + [whole(shape) for shape in sums],
        out_shape=[jax.ShapeDtypeStruct((m, n if w is None else w), dt) for dt, w in outs]
        + [jax.ShapeDtypeStruct(shape, F32) for shape in sums],
        scratch_shapes=[pltpu.VMEM((tm, tn), F32)] if nk > 1 else [],
        name=name, compiler_params=_params("arbitrary" if sums else "parallel", "parallel", "arbitrary"),
    )(a, b, *extras, *consts)


def _split3(x):
    x1 = x.astype(BF16)
    r1 = x - x1.astype(F32)
    x2 = r1.astype(BF16)
    x3 = (r1 - x2.astype(F32)).astype(BF16)
    return x1, x2, x3


def _dot_exact(x, e):
    return sum(jnp.dot(p, e, preferred_element_type=F32) for p in _split3(x))


def _head_expand(heads, width):
    dh = width // heads
    rows = jnp.arange(LANES)[:, None]
    cols = jnp.arange(width)[None, :]
    return (cols // dh == rows).astype(BF16)


def _rstd(x):
    return lax.rsqrt(jnp.mean(x * x, axis=-1, keepdims=True) + EPS)


def _rms_fwd(x, gain, *, name):
    return _rowwise(lambda x, g: x * _rstd(x) * g, [x], [gain], [(x.shape[1], BF16)], name=name)[0]


def _rms_bwd_vals(x, dhn, gain):
    r = _rstd(x)
    xh = x * r
    dxh = dhn * gain
    dx = r * (dxh - xh * jnp.mean(dxh * xh, axis=-1, keepdims=True))
    return dx, jnp.sum(dhn * xh, axis=0, keepdims=True)


def _norm_input_grad(dz, w, x, dres, gain, *, tb=False, name):
    def epilogue(dhn, x, dres, g):
        dx, dg = _rms_bwd_vals(x, dhn, g)
        dh = dres + dx
        return dh, dh, dg
    return _matmul(dz, w, tb=tb, outs=(F32, BF16), extras=(x, dres), consts=(gain,), sums=((1, x.shape[1]),), epilogue=epilogue,
                   name=name)


def _loss_and_grad(h, target, gain, *, name):
    def fn(h, tgt, g):
        r = _rstd(h)
        err = h * r * g - tgt
        loss = 0.5 * jnp.sum(jnp.mean(err * err, axis=-1, keepdims=True), axis=0, keepdims=True)
        dx, dg = _rms_bwd_vals(h, err * (1.0 / D_MODEL), g)
        return dx, dx, jnp.broadcast_to(loss, (1, LANES)), dg
    d = h.shape[1]
    return _rowwise(fn, [h, target], [gain], [(d, F32), (d, BF16)], [(1, LANES), (1, d)], name=name)


def _adamw(w, g, m, v, *, name):
    rows, cols = w.shape
    tile = rows
    for cand in (512, 256, 128, 64, 32, 16, 8):
        if rows % cand == 0 and rows > cand:
            tile = cand
            break
    bc1 = 1.0 - ADAM_B1 ** ADAM_STEP
    bc2 = 1.0 - ADAM_B2 ** ADAM_STEP

    def fn(w, g, m, v):
        m2 = ADAM_B1 * m + (1.0 - ADAM_B1) * g
        v2 = ADAM_B2 * v + (1.0 - ADAM_B2) * (g * g)
        delta = -ADAM_LR * ((m2 / bc1) / (jnp.sqrt(v2 / bc2) + ADAM_EPS) + ADAM_WD * w)
        return delta, m2, v2
    return _rowwise(fn, [w, g, m, v], [], [(cols, F32)] * 3, name=name, tile=tile)


def _attn_specs(arr, width, cb, classes, rows_block, whole, together=1):
    ncols = arr.shape[2] // (classes * width)
    lanes, at = (width, lambda r: r * ncols + cb) if together == 1 else (together * ncols * width, lambda r: r)
    if whole:
        return pl.BlockSpec((1, arr.shape[1], lanes), lambda n, r, i: (n, 0, at(r)))
    return pl.BlockSpec((1, rows_block, lanes), lambda n, r, i: (n, i, at(r)))


def _class_window(refs, spans, together, cl):
    if together == 1:
        return refs
    return [ref.at[:, :, pl.ds((cl * ncols + cb) * width, width)] for ref, (ncols, cb, width) in zip(refs, spans)]


def _attn_tiles(mode, lq, lk, backward=False):
    if mode == "full":
        rows = min(lq, FULL_ROWS)
        return rows, rows, lk
    if mode == "band":
        tq = min(BAND_ROWS if backward else 2 * BAND_ROWS, lq)
        rows = tq if backward else ATT_BLOCK
        return tq, rows, min(rows + ATT_BLOCK, lk)
    rows = CAUSAL_ROWS_BACKWARD if backward else CAUSAL_ROWS
    return rows, rows, ATT_CHUNK


def _window(mode, row0, j, rows, win, lk):
    if mode == "causal":
        return pl.multiple_of(j * win, win), row0 // win + 1
    if mode == "band":
        return pl.multiple_of(jnp.clip(row0 + rows - win, 0, lk - win), ATT_BLOCK), 1
    return 0, 1


def _allowed(mode, row0, start, rows, win, max_dist):
    if mode == "full":
        return None
    dist = (row0 + lax.broadcasted_iota(jnp.int32, (rows, win), 0)) - (start + lax.broadcasted_iota(jnp.int32, (rows, win), 1))
    ok = dist >= 0
    if max_dist is not None:
        ok = ok & (dist <= max_dist)
    return ok


def _head_groups(heads, dh):
    gw = max(LANES, dh)
    return gw, gw // dh, heads // (gw // dh)


def _own_lanes(rows, gw, dh, u):
    return lax.broadcasted_iota(jnp.int32, (rows, gw), 1) // dh == u


def _only_head(x, own, per, u):
    return x if per == 1 else jnp.where(own[u], x, jnp.zeros_like(x))


def _step_scores(qz, kw, kb_row, ok):
    s = lax.dot_general(qz, kw, (((1,), (1,)), ((), ())), preferred_element_type=F32)
    if kb_row is not None:
        s = s - kb_row
    if ok is not None:
        s = jnp.where(ok, s, MASKED)
    return s


def _attn_fwd(q, k, v, kb=None, *, classes=1, heads, dh, mode, max_dist=None, bf16_copy=False, name):
    (qa, qc), (ka, kc), (va, vc) = q, k, v
    n, lq, lk = qa.shape[0], qa.shape[1], ka.shape[1]
    width = heads * dh
    tq, rows, win = _attn_tiles(mode, lq, lk)
    gw, per, groups = _head_groups(heads, dh)
    scale = dh ** -0.5
    has_kb = kb is not None
    single = mode != "causal"
    together = min(classes, max(1, CLASS_ROWS // lq))
    ncols = lambda arr: arr.shape[2] // (classes * width)
    spans = [(ncols(qa), qc, width), (ncols(ka), kc, width), (ncols(va), vc, width), (1, 0, width), (1, 0, LANES), (1, 0, width)]

    def body(*refs):
        for cl in range(together):
            for sub in range(tq // rows):
                part(_class_window(refs, spans, together, cl), pl.program_id(2) * tq + sub * rows, slice(sub * rows, (sub + 1) * rows))

    def part(refs, row0, rs):
        q_ref, k_ref, v_ref = refs[:3]
        kb_ref = refs[3] if has_kb else None
        n_in = 4 if has_kb else 3
        o_ref, lse_ref = refs[n_in:n_in + 2]
        o16_ref = refs[n_in + 2] if bf16_copy else None
        lane = lax.broadcasted_iota(jnp.int32, (rows, LANES), 1)
        own = [_own_lanes(rows, gw, dh, u) for u in range(per)]

        def step(j, carry, masked=True):
            m_in, l_in = carry
            m_out, l_out = m_in, l_in
            start, _ = _window(mode, row0, j, rows, win, lk)
            ok = _allowed(mode, row0, start, rows, win, max_dist) if masked else None
            for g in range(groups):
                cols = slice(g * gw, (g + 1) * gw)
                qg = q_ref[0, rs, cols] * scale
                kw = k_ref[0, pl.ds(start, win), cols]
                vw = v_ref[0, pl.ds(start, win), cols]
                alpha_g = new_g = None
                for u in range(per):
                    h = g * per + u
                    kb_row = kb_ref[0, h, pl.ds(j, 1), :] if has_kb else None
                    s = _step_scores(_only_head(qg, own, per, u), kw, kb_row, ok)
                    m_new = jnp.max(s, axis=1, keepdims=True)
                    if not single:
                        m_old = m_in[:, h:h + 1]
                        m_new = jnp.maximum(m_old, m_new)
                        alpha = jnp.exp(m_old - m_new)
                        alpha_g = alpha if u == 0 else jnp.where(own[u], alpha, alpha_g)
                    p = jnp.exp(s - m_new)
                    l_new = jnp.sum(p, axis=1, keepdims=True)
                    r = jnp.dot(p.astype(BF16), vw, preferred_element_type=F32)
                    if single:
                        r = r * (1.0 / l_new)
                    else:
                        l_new = alpha * l_in[:, h:h + 1] + l_new
                    new_g = r if u == 0 else jnp.where(own[u], r, new_g)
                    m_out = jnp.where(lane == h, m_new, m_out)
                    l_out = jnp.where(lane == h, l_new, l_out)
                o_ref[0, rs, cols] = new_g if single else alpha_g * o_ref[0, rs, cols] + new_g
                if bf16_copy:
                    o16_ref[0, rs, cols] = new_g.astype(BF16)
            return m_out, l_out

        carry = (jnp.full((rows, LANES), MASKED, F32), jnp.zeros((rows, LANES), F32))
        if single:
            m_all, l_all = step(0, carry)
            l_all = jnp.where(lane < heads, l_all, 1.0)
        else:
            o_ref[...] = jnp.zeros(o_ref.shape, F32)
            last = _window(mode, row0, 0, rows, win, lk)[1] - 1
            m_all, l_all = step(last, lax.fori_loop(0, last, functools.partial(step, masked=False), carry))
            l_all = jnp.where(lane < heads, l_all, 1.0)
            inv = 1.0 / l_all
            for g in range(groups):
                cols = slice(g * gw, (g + 1) * gw)
                inv_g = inv[:, g * per:g * per + 1]
                for u in range(1, per):
                    inv_g = jnp.where(own[u], inv[:, g * per + u:g * per + u + 1], inv_g)
                o_ref[0, rs, cols] = o_ref[0, rs, cols] * inv_g
        lse_ref[0, rs, :] = jnp.where(lane < heads, m_all + jnp.log(l_all), 0.0)

    in_specs = [_attn_specs(qa, width, qc, classes, tq, False, together), _attn_specs(ka, width, kc, classes, win, True, together),
                _attn_specs(va, width, vc, classes, win, True, together)]
    args = [qa, ka, va]
    if has_kb:
        assert together == 1
        in_specs.append(pl.BlockSpec((1,) + kb.shape[1:], lambda n_, r, i: (n_, 0, 0, 0)))
        args.append(kb)
    out_shape = [jax.ShapeDtypeStruct((n, lq, classes * width), F32), jax.ShapeDtypeStruct((n, lq, classes * LANES), F32)]
    out_specs = [pl.BlockSpec((1, tq, together * width), lambda n_, r, i: (n_, i, r)),
                 pl.BlockSpec((1, tq, together * LANES), lambda n_, r, i: (n_, i, r))]
    if bf16_copy:
        assert single
        out_shape.append(jax.ShapeDtypeStruct((n, lq, classes * width), BF16))
        out_specs.append(out_specs[0])
    return pl.pallas_call(
        body, grid=(n, classes // together, lq // tq), in_specs=in_specs, out_specs=out_specs, out_shape=out_shape, name=name,
        compiler_params=_params("parallel", "parallel", "arbitrary"),
    )(*args)


def _attn_bwd(q, k, v, do, lse, delta, kb=None, *, classes=1, heads, dh, mode, max_dist=None, dq_dtype=F32, name):
    (qa, qc), (ka, kc), (va, vc), (da, dc) = q, k, v, do
    n, lq, lk = qa.shape[0], qa.shape[1], ka.shape[1]
    width = heads * dh
    tq, rows, win = _attn_tiles(mode, lq, lk, backward=True)
    gw, per, groups = _head_groups(heads, dh)
    scale = dh ** -0.5
    has_kb = kb is not None
    single = mode != "causal"
    nt = (((1,), (1,)), ((), ()))
    tn = (((0,), (0,)), ((), ()))
    together = min(classes, max(1, CLASS_ROWS // lq))
    ncols = lambda arr: arr.shape[2] // (classes * width)
    spans = [(ncols(qa), qc, width), (ncols(ka), kc, width), (ncols(va), vc, width), (ncols(da), dc, width), (1, 0, LANES),
             (1, 0, LANES), (1, 0, width), (1, 0, width), (1, 0, width)]

    def body(*refs):
        n_in = 7 if has_kb else 6
        dk_ref, dv_ref = refs[n_in + 1:n_in + 3]

        @pl.when(pl.program_id(2) == 0)
        def _():
            dk_ref[...] = jnp.zeros(dk_ref.shape, F32)
            dv_ref[...] = jnp.zeros(dv_ref.shape, F32)
            if has_kb:
                refs[n_in + 3][...] = jnp.zeros(refs[n_in + 3].shape, F32)

        for cl in range(together):
            for sub in range(tq // rows):
                part(_class_window(refs, spans, together, cl), pl.program_id(2) * tq + sub * rows, slice(sub * rows, (sub + 1) * rows))

    def part(refs, row0, rs):
        q_ref, k_ref, v_ref, do_ref, lse_ref, dl_ref = refs[:6]
        kb_ref = refs[6] if has_kb else None
        n_in = 7 if has_kb else 6
        dq_ref, dk_ref, dv_ref = refs[n_in:n_in + 3]
        dkb_ref, drow_ref = refs[n_in + 3:n_in + 5] if has_kb else (None, None)
        lse_all = lse_ref[0, rs, :]
        dl_all = dl_ref[0, rs, :]
        lane = lax.broadcasted_iota(jnp.int32, (rows, LANES), 1)
        own = [_own_lanes(rows, gw, dh, u) for u in range(per)]

        def step(j, drow_all, masked=True):
            start, _ = _window(mode, row0, j, rows, win, lk)
            ok = _allowed(mode, row0, start, rows, win, max_dist) if masked else None
            for g in range(groups):
                cols = slice(g * gw, (g + 1) * gw)
                qg = q_ref[0, rs, cols] * scale
                dog = do_ref[0, rs, cols]
                kw = k_ref[0, pl.ds(start, win), cols]
                vw = v_ref[0, pl.ds(start, win), cols]
                dq_g = dk_w = dv_w = None
                for u in range(per):
                    h = g * per + u
                    qz, doz = _only_head(qg, own, per, u), _only_head(dog, own, per, u)
                    kb_row = kb_ref[0, h, pl.ds(j, 1), :] if has_kb else None
                    p = jnp.exp(_step_scores(qz, kw, kb_row, ok) - lse_all[:, h:h + 1])
                    dp = lax.dot_general(doz, vw, nt, preferred_element_type=F32)
                    ds = p * (dp - dl_all[:, h:h + 1])
                    dsb = ds.astype(BF16)
                    r = jnp.dot(dsb, kw, preferred_element_type=F32)
                    dq_g = r if u == 0 else jnp.where(own[u], r, dq_g)
                    dk_u = lax.dot_general(dsb, qz, tn, preferred_element_type=F32)
                    dv_u = lax.dot_general(p.astype(BF16), doz, tn, preferred_element_type=F32)
                    dk_w = dk_u if u == 0 else dk_w + dk_u
                    dv_w = dv_u if u == 0 else dv_w + dv_u
                    if has_kb:
                        dkb_ref[0, h, pl.ds(j, 1), :] += -jnp.sum(ds, axis=0, keepdims=True)
                        drow_all = drow_all + jnp.where(lane == h, jnp.sum(ds, axis=1, keepdims=True), 0.0)
                dk_ref[0, pl.ds(start, win), cols] += dk_w
                dv_ref[0, pl.ds(start, win), cols] += dv_w
                if single:
                    dq_ref[0, rs, cols] = (dq_g * scale).astype(dq_ref.dtype)
                else:
                    dq_ref[0, rs, cols] += dq_g * scale
            return drow_all

        drow_all = jnp.zeros((rows, LANES), F32)
        if single:
            drow_all = step(0, drow_all)
        else:
            dq_ref[...] = jnp.zeros(dq_ref.shape, F32)
            last = _window(mode, row0, 0, rows, win, lk)[1] - 1
            drow_all = step(last, lax.fori_loop(0, last, functools.partial(step, masked=False), drow_all))
        if has_kb:
            drow_ref[0, rs, :] = drow_all

    row_spec = functools.partial(_attn_specs, classes=classes, rows_block=tq, whole=False, together=together)
    in_specs = [row_spec(qa, width, qc), _attn_specs(ka, width, kc, classes, win, True, together),
                _attn_specs(va, width, vc, classes, win, True, together), row_spec(da, width, dc), row_spec(lse, LANES, 0),
                row_spec(delta, LANES, 0)]
    args = [qa, ka, va, da, lse, delta]
    assert single or dq_dtype == F32
    out_shape = [jax.ShapeDtypeStruct((n, lq, classes * width), dq_dtype), jax.ShapeDtypeStruct((n, lk, classes * width), F32),
                 jax.ShapeDtypeStruct((n, lk, classes * width), F32)]
    kv_spec = pl.BlockSpec((1, lk, together * width), lambda n_, r, i: (n_, 0, r))
    out_specs = [pl.BlockSpec((1, tq, together * width), lambda n_, r, i: (n_, i, r)), kv_spec, kv_spec]
    if has_kb:
        assert together == 1
        kb_spec = pl.BlockSpec((1,) + kb.shape[1:], lambda n_, r, i: (n_, 0, 0, 0))
        in_specs.append(kb_spec)
        args.append(kb)
        out_shape += [jax.ShapeDtypeStruct(kb.shape, F32), jax.ShapeDtypeStruct((n, lq, LANES), F32)]
        out_specs += [kb_spec, pl.BlockSpec((1, tq, LANES), lambda n_, r, i: (n_, i, 0))]
    return pl.pallas_call(
        body, grid=(n, classes // together, lq // tq), in_specs=in_specs, out_specs=out_specs, out_shape=out_shape, name=name,
        compiler_params=_params("parallel", "parallel", "arbitrary"),
    )(*args)


def _rope_tables():
    half = HEAD_DIM // 2
    inv = ROPE_THETA ** (-jnp.arange(half, dtype=F32) / half)
    ang = jnp.arange(SEQ, dtype=F32)[:, None] * inv[None, :]
    cos = jnp.tile(jnp.cos(ang), (1, 2 * ATT_W // HEAD_DIM))
    sin = jnp.tile(jnp.concatenate([-jnp.sin(ang), jnp.sin(ang)], axis=1), (1, ATT_W // HEAD_DIM))
    return cos, sin


def _rotate(x, cos, sin):
    width = x.shape[1]
    lane = lax.broadcasted_iota(jnp.int32, x.shape, 1)
    first_half = (lane % HEAD_DIM) < (HEAD_DIM // 2)
    swapped = jnp.where(first_half, pltpu.roll(x, width - HEAD_DIM // 2, axis=1), pltpu.roll(x, HEAD_DIM // 2, axis=1))
    return x * cos[:, :width] + swapped * sin[:, :width]


def _gelu(x):
    k = math.sqrt(2.0 / math.pi)
    t = jnp.tanh(k * (x + 0.044715 * x * x * x))
    return 0.5 * x * (1.0 + t), t


def _gelu_grad(x, t):
    k = math.sqrt(2.0 / math.pi)
    return 0.5 * (1.0 + t) + 0.5 * x * (1.0 - t * t) * k * (1.0 + 3.0 * 0.044715 * x * x)


def _shift_down(x, halo, s):
    ext = jnp.concatenate([halo, x], axis=0)
    return pltpu.roll(ext, s, axis=0)[8:, :]


def _shift_up(x, halo, s):
    rows = x.shape[0]
    ext = jnp.concatenate([x, halo], axis=0)
    return pltpu.roll(ext, rows + 8 - s, axis=0)[:rows, :]


def _lru_gates(u, halo, cw, cb, wa, ba, wi, bi, lam):
    taps = [_shift_down(u, halo, CONV_WIDTH - 1 - k) for k in range(CONV_WIDTH - 1)] + [u]
    uc = cb + sum(cw[k:k + 1, :] * taps[k] for k in range(CONV_WIDTH))
    ucb = uc.astype(BF16)
    r = jax.nn.sigmoid(jnp.dot(ucb, wa, preferred_element_type=F32) + ba)
    gi = jax.nn.sigmoid(jnp.dot(ucb, wi, preferred_element_type=F32) + bi)
    sp = jnp.maximum(-lam, 0.0) + jnp.log(1.0 + jnp.exp(-jnp.abs(lam)))
    log_a = -LRU_C * r * sp
    a = jnp.exp(log_a)
    x2 = 2.0 * log_a
    one_minus_a2 = jnp.where(x2 > -0.01, -x2 * (1.0 + x2 * (0.5 + x2 * (1.0 / 6.0 + x2 / 24.0))), 1.0 - jnp.exp(x2))
    mult = jnp.sqrt(one_minus_a2)
    return taps, uc, ucb, r, gi, sp, a, mult


def _lru_specs(nchunk, reverse):
    def chunk(b, c):
        return b * nchunk + ((nchunk - 1 - c) if reverse else c)

    def halo_before(b, c):
        return jnp.maximum(chunk(b, c) * (LRU_CHUNK // 8) - 1, 0)

    return chunk, halo_before


def _lru_fwd(zug, cw, cb, wa, ba, wi, bi, lam, *, name):
    total = zug.shape[0]
    nchunk = SEQ // LRU_CHUNK
    w = LRU_WIDTH
    chunk, halo_before = _lru_specs(nchunk, False)

    def body(u_ref, uh_ref, g_ref, cw_ref, cb_ref, wa_ref, ba_ref, wi_ref, bi_ref, lam_ref, y_ref, h_ref, a_sc, x_sc, carry_sc):
        c = pl.program_id(1)
        u = u_ref[...]
        halo = jnp.where(c > 0, uh_ref[...], 0.0)
        _, uc, _, _, gi, _, a, mult = _lru_gates(u, halo, cw_ref[...], cb_ref[...], wa_ref[...], ba_ref[...],
                                                 wi_ref[...], bi_ref[...], lam_ref[...])
        a_sc[...] = a
        x_sc[...] = mult * (gi * uc)

        @pl.when(c == 0)
        def _():
            carry_sc[...] = jnp.zeros(carry_sc.shape, F32)

        def tile_step(t, h):
            r0 = pl.multiple_of(t * 8, 8)
            at = a_sc[pl.ds(r0, 8), :]
            xt = x_sc[pl.ds(r0, 8), :]
            rows = []
            for j in range(8):
                h = at[j:j + 1, :] * h + xt[j:j + 1, :]
                rows.append(h)
            h_ref[pl.ds(r0, 8), :] = jnp.concatenate(rows, axis=0)
            return h

        h_last = lax.fori_loop(0, LRU_CHUNK // 8, tile_step, carry_sc[0:1, :])
        carry_sc[0:1, :] = h_last
        gel, _ = _gelu(g_ref[...])
        y_ref[...] = (h_ref[...] * gel).astype(BF16)

    small = lambda arr: pl.BlockSpec(arr.shape, lambda b, c: (0, 0))
    return pl.pallas_call(
        body, grid=(total // SEQ, nchunk),
        in_specs=[pl.BlockSpec((LRU_CHUNK, w), lambda b, c: (chunk(b, c), 0)),
                  pl.BlockSpec((8, w), lambda b, c: (halo_before(b, c), 0)),
                  pl.BlockSpec((LRU_CHUNK, w), lambda b, c: (chunk(b, c), 1)),
                  small(cw), small(cb), small(wa), small(ba), small(wi), small(bi), small(lam)],
        out_specs=[pl.BlockSpec((LRU_CHUNK, w), lambda b, c: (chunk(b, c), 0))] * 2,
        out_shape=[jax.ShapeDtypeStruct((total, w), BF16), jax.ShapeDtypeStruct((total, w), F32)],
        scratch_shapes=[pltpu.VMEM((LRU_CHUNK, w), F32), pltpu.VMEM((LRU_CHUNK, w), F32), pltpu.VMEM((8, w), F32)],
        name=name, compiler_params=_params("arbitrary", "arbitrary"),
    )(zug, zug, zug, cw, cb, wa, ba, wi, bi, lam)


def _lru_bwd(zug, hl, dy, cw, cb, wa, ba, wi, bi, lam, *, name):
    total = zug.shape[0]
    nchunk = SEQ // LRU_CHUNK
    w = LRU_WIDTH
    chunk, halo_before = _lru_specs(nchunk, True)
    tn = (((0,), (0,)), ((), ()))
    nt = (((1,), (1,)), ((), ()))

    def body(u_ref, uh_ref, g_ref, hl_ref, hh_ref, dy_ref, cw_ref, cb_ref, wa_ref, ba_ref, wi_ref, bi_ref, lam_ref,
             dz_ref, dcw_ref, dcb_ref, dwa_ref, dba_ref, dwi_ref, dbi_ref, dlam_ref,
             a_sc, d_sc, g_sc, gcarry_sc, acarry_sc, duc_sc):
        b, c = pl.program_id(0), pl.program_id(1)
        first_chunk = c == nchunk - 1

        @pl.when((b == 0) & (c == 0))
        def _():
            for ref in (dcw_ref, dcb_ref, dwa_ref, dba_ref, dwi_ref, dbi_ref, dlam_ref):
                ref[...] = jnp.zeros(ref.shape, F32)

        @pl.when(c == 0)
        def _():
            gcarry_sc[...] = jnp.zeros(gcarry_sc.shape, F32)
            acarry_sc[...] = jnp.zeros(acarry_sc.shape, F32)
            duc_sc[...] = jnp.zeros(duc_sc.shape, F32)

        u = u_ref[...]
        halo = jnp.where(first_chunk, 0.0, uh_ref[...])
        cw, wa, wi, lam = cw_ref[...], wa_ref[...], wi_ref[...], lam_ref[...]
        taps, uc, ucb, r, gi, sp, a, mult = _lru_gates(u, halo, cw, cb_ref[...], wa, ba_ref[...], wi, bi_ref[...], lam)
        gate = g_ref[...]
        gel, th = _gelu(gate)
        dy = dy_ref[...]
        hl = hl_ref[...]
        a_sc[...] = a
        d_sc[...] = dy * gel
        dgate = dy * hl * _gelu_grad(gate, th)

        def tile_step(t, carry):
            g_next, a_next = carry
            r0 = pl.multiple_of((LRU_CHUNK // 8 - 1 - t) * 8, 8)
            dt = d_sc[pl.ds(r0, 8), :]
            at = a_sc[pl.ds(r0, 8), :]
            rows = [None] * 8
            for j in reversed(range(8)):
                g_next = dt[j:j + 1, :] + a_next * g_next
                a_next = at[j:j + 1, :]
                rows[j] = g_next
            g_sc[pl.ds(r0, 8), :] = jnp.concatenate(rows, axis=0)
            return g_next, a_next

        g_last, a_last = lax.fori_loop(0, LRU_CHUNK // 8, tile_step, (gcarry_sc[0:1, :], acarry_sc[0:1, :]))
        gcarry_sc[0:1, :] = g_last
        acarry_sc[0:1, :] = a_last

        gs = g_sc[...]
        hl_halo = jnp.where(first_chunk, 0.0, hh_ref[...])
        da = gs * _shift_down(hl, hl_halo, 1)
        dmult = gs * gi * uc
        dgi = gs * mult * uc
        duc = gs * mult * gi
        dlog_a = da * a - dmult * a * a / mult
        dr = dlog_a * (-LRU_C * sp)
        dsp = jnp.sum(dlog_a * (-LRU_C * r), axis=0, keepdims=True)
        dlam_ref[...] += -dsp * jax.nn.sigmoid(-lam)
        dpa = dr * r * (1.0 - r)
        dpi = dgi * gi * (1.0 - gi)
        dpab, dpib = dpa.astype(BF16), dpi.astype(BF16)
        dba_ref[...] += jnp.sum(dpa, axis=0, keepdims=True)
        dbi_ref[...] += jnp.sum(dpi, axis=0, keepdims=True)
        dwa_ref[...] += lax.dot_general(ucb, dpab, tn, preferred_element_type=F32)
        dwi_ref[...] += lax.dot_general(ucb, dpib, tn, preferred_element_type=F32)
        duc = duc + lax.dot_general(dpab, wa, nt, preferred_element_type=F32)
        duc = duc + lax.dot_general(dpib, wi, nt, preferred_element_type=F32)
        dcb_ref[...] += jnp.sum(duc, axis=0, keepdims=True)
        dcw_ref[...] += jnp.concatenate([jnp.sum(duc * taps[k], axis=0, keepdims=True) for k in range(CONV_WIDTH)], axis=0)
        after = duc_sc[...]
        du = cw[CONV_WIDTH - 1:CONV_WIDTH, :] * duc
        for k in range(CONV_WIDTH - 1):
            du = du + cw[k:k + 1, :] * _shift_up(duc, after, CONV_WIDTH - 1 - k)
        duc_sc[...] = duc[0:8, :]
        dz_ref[:, 0:w] = du.astype(BF16)
        dz_ref[:, w:2 * w] = dgate.astype(BF16)

    small = lambda arr: pl.BlockSpec(arr.shape, lambda b, c: (0, 0))
    acc = lambda shape: pl.BlockSpec(shape, lambda b, c: (0, 0))
    chunk_spec = lambda cb_: pl.BlockSpec((LRU_CHUNK, w), lambda b, c: (chunk(b, c), cb_))
    halo_spec = pl.BlockSpec((8, w), lambda b, c: (halo_before(b, c), 0))
    acc_shapes = [(CONV_WIDTH, w), (1, w), (w, w), (1, w), (w, w), (1, w), (1, w)]
    return pl.pallas_call(
        body, grid=(total // SEQ, nchunk),
        in_specs=[chunk_spec(0), halo_spec, chunk_spec(1), chunk_spec(0), halo_spec, chunk_spec(0),
                  small(cw), small(cb), small(wa), small(ba), small(wi), small(bi), small(lam)],
        out_specs=[pl.BlockSpec((LRU_CHUNK, 2 * w), lambda b, c: (chunk(b, c), 0))] + [acc(s) for s in acc_shapes],
        out_shape=[jax.ShapeDtypeStruct((total, 2 * w), BF16)] + [jax.ShapeDtypeStruct(s, F32) for s in acc_shapes],
        scratch_shapes=[pltpu.VMEM((LRU_CHUNK, w), F32)] * 3 + [pltpu.VMEM((8, w), F32)] * 3,
        name=name, compiler_params=_params("arbitrary", "arbitrary"),
    )(zug, zug, zug, hl, hl, dy, cw, cb, wa, ba, wi, bi, lam)


def _block_diag(wb):
    eye = jnp.eye(LRU_BLOCKS, dtype=wb.dtype)
    return jnp.einsum("gcd,gh->gchd", wb, eye).reshape(LRU_WIDTH, LRU_WIDTH).astype(BF16)


def _diag_blocks(wd):
    blk = LRU_WIDTH // LRU_BLOCKS
    return jnp.stack([wd[g * blk:(g + 1) * blk, g * blk:(g + 1) * blk] for g in range(LRU_BLOCKS)])


FOX_SCAN = 256


def _fox_bias(fl, bf, *, name):
    nb = fl.shape[0]

    def body(fl_ref, bf_ref, c_ref):
        x = fl_ref[0] + bf_ref[...]
        logf = jnp.minimum(x, 0.0) - jnp.log(1.0 + jnp.exp(-jnp.abs(x)))
        upper = (lax.broadcasted_iota(jnp.int32, (FOX_SCAN, FOX_SCAN), 0)
                 <= lax.broadcasted_iota(jnp.int32, (FOX_SCAN, FOX_SCAN), 1)).astype(BF16)
        carry = jnp.zeros((LRU_BLOCKS, 1), F32)
        for j in range(SEQ // FOX_SCAN):
            blk = logf[:, j * FOX_SCAN:(j + 1) * FOX_SCAN]
            c_ref[0, :, j * FOX_SCAN:(j + 1) * FOX_SCAN] = _dot_exact(blk, upper) + carry
            carry = carry + jnp.sum(blk, axis=1, keepdims=True)

    return pl.pallas_call(
        body, grid=(nb,),
        in_specs=[pl.BlockSpec((1, 8, SEQ), lambda b: (b, 0, 0)), pl.BlockSpec((8, 1), lambda b: (0, 0))],
        out_specs=pl.BlockSpec((1, 8, SEQ), lambda b: (b, 0, 0)),
        out_shape=jax.ShapeDtypeStruct((nb, 8, SEQ), F32), name=name, compiler_params=_params("arbitrary"),
    )(fl, bf)


def _fox_bias_bwd(fl, bf, dc, *, name):
    nb = fl.shape[0]

    def body(fl_ref, bf_ref, dc_ref, dfl_ref, dbf_ref):
        @pl.when(pl.program_id(0) == 0)
        def _():
            dbf_ref[...] = jnp.zeros(dbf_ref.shape, F32)

        x = fl_ref[0] + bf_ref[...]
        dc_all = dc_ref[0]
        lower = (lax.broadcasted_iota(jnp.int32, (FOX_SCAN, FOX_SCAN), 0)
                 >= lax.broadcasted_iota(jnp.int32, (FOX_SCAN, FOX_SCAN), 1)).astype(BF16)
        carry = jnp.zeros((LRU_BLOCKS, 1), F32)
        total = jnp.zeros((LRU_BLOCKS, 1), F32)
        for j in reversed(range(SEQ // FOX_SCAN)):
            cols = slice(j * FOX_SCAN, (j + 1) * FOX_SCAN)
            blk = dc_all[:, cols]
            dlogf = _dot_exact(blk, lower) + carry
            carry = carry + jnp.sum(blk, axis=1, keepdims=True)
            dx = dlogf * jax.nn.sigmoid(-x[:, cols])
            dfl_ref[0, :, cols] = dx
            total = total + jnp.sum(dx, axis=1, keepdims=True)
        dbf_ref[...] += total

    return pl.pallas_call(
        body, grid=(nb,),
        in_specs=[pl.BlockSpec((1, 8, SEQ), lambda b: (b, 0, 0)), pl.BlockSpec((8, 1), lambda b: (0, 0)),
                  pl.BlockSpec((1, 8, SEQ), lambda b: (b, 0, 0))],
        out_specs=[pl.BlockSpec((1, 8, SEQ), lambda b: (b, 0, 0)), pl.BlockSpec((8, 1), lambda b: (0, 0))],
        out_shape=[jax.ShapeDtypeStruct((nb, 8, SEQ), F32), jax.ShapeDtypeStruct((8, 1), F32)],
        name=name, compiler_params=_params("arbitrary"),
    )(fl, bf, dc)


def _seq3(a, nb):
    return a.reshape(nb, a.shape[0] // nb, a.shape[1])


def _flat2(a):
    return a.reshape(a.shape[0] * a.shape[1], a.shape[2])


def _residual_out(y, w, h, next_gain, *, name):
    if next_gain is None:
        out, = _matmul(y, w, extras=(h,), epilogue=lambda acc, res: (acc + res,), name=name)
        return out, None

    def epilogue(acc, res, g):
        out = acc + res
        return out, out * _rstd(out) * g
    return _matmul(y, w, outs=(F32, BF16), extras=(h,), consts=(next_gain,), epilogue=epilogue, whole_rows=True, name=name)


def _mlp_fwd(h, hn, p, tag, next_gain):
    act, = _matmul(hn, p["w_up_t"], tb=True, outs=(BF16,), epilogue=lambda acc: (jnp.square(jnp.maximum(acc, 0.0)),),
                   name=f"{tag}_up")
    out, hn_next = _residual_out(act, p["w_down"], h, next_gain, name=f"{tag}_down")
    return out, hn_next, (h, hn, act)


def _mlp_bwd(dh, dhb, p, saved, tag):
    h, hn, act = saved
    dup, = _matmul(dhb, p["w_down"], tb=True, outs=(BF16,), extras=(act,),
                   epilogue=lambda acc, act: (acc * (2.0 * jnp.sqrt(act).astype(F32)),), name=f"{tag}_bwd_dup")
    dw_down, = _matmul(act, dhb, ta=True, outs=(BF16,),name=f"{tag}_bwd_dwdown")
    dw_up_t, = _matmul(dup, hn, ta=True, outs=(BF16,),name=f"{tag}_bwd_dwup")
    dh2, dh2b, dg = _norm_input_grad(dup, p["w_up_t"], h, dh, p["norm"], name=f"{tag}_bwd_dh")
    return dh2, dh2b, {"norm": dg, "w_up_t": dw_up_t, "w_down": dw_down}


def _xa_fwd(h, hn, mem, p, tag, nb, next_gain):
    mem_n = _rms_fwd(mem, p["mem_norm"], name=f"{tag}_memnorm")
    q, = _matmul(hn, p["w_q"], outs=(BF16,), name=f"{tag}_q")
    kv, = _matmul(mem_n, p["w_kv_t"], tb=True, outs=(BF16,), name=f"{tag}_kv")
    q3, kv3 = _seq3(q, nb), _seq3(kv, nb)
    o, lse, ob = _attn_fwd((q3, 0), (kv3, 0), (kv3, 1), heads=XA_HEADS, dh=XA_HEAD_DIM, mode="full", bf16_copy=True,
                           name=f"{tag}_attn")
    o, ob = _flat2(o), _flat2(ob)
    out, hn_next = _residual_out(ob, p["w_o"], h, next_gain, name=f"{tag}_o")
    return out, hn_next, (h, hn, mem_n, q3, kv3, o, ob, lse)


def _xa_bwd(dh, dhb, mem, p, saved, tag, nb):
    h, hn, mem_n, q3, kv3, o, ob, lse = saved
    dob, delta = _matmul(dhb, p["w_o"], tb=True, outs=(BF16, (F32, LANES)), extras=(o,), consts=(_head_expand(XA_HEADS, D_MODEL).T,),
                         epilogue=lambda acc, o, e: (acc, _dot_exact(acc * o, e)), name=f"{tag}_bwd_do")
    dw_o, = _matmul(ob, dhb, ta=True, outs=(BF16,),name=f"{tag}_bwd_dwo")
    dq, dk, dv = _attn_bwd((q3, 0), (kv3, 0), (kv3, 1), (_seq3(dob, nb), 0), lse, _seq3(delta, nb), heads=XA_HEADS,
                           dh=XA_HEAD_DIM, mode="full", dq_dtype=BF16, name=f"{tag}_bwd_attn")
    dqb = _flat2(dq)
    dkvb, = _rowwise(lambda a, b: jnp.concatenate([a, b], axis=1), [_flat2(dk), _flat2(dv)], [], [(2 * D_MODEL, BF16)],
                     name=f"{tag}_bwd_dkvcast")
    dw_q, = _matmul(hn, dqb, ta=True, outs=(BF16,),name=f"{tag}_bwd_dwq")
    dw_kv_t, = _matmul(dkvb, mem_n, ta=True, outs=(BF16,),name=f"{tag}_bwd_dwkv")
    dmem_n, = _matmul(dkvb, p["w_kv_t"], name=f"{tag}_bwd_dmemn")
    dg_mem, = _rowwise(lambda x, d, g: _rms_bwd_vals(x, d, g)[1], [mem, dmem_n], [p["mem_norm"]], [], [(1, D_MODEL)],
                       name=f"{tag}_bwd_memnorm")
    dh2, dh2b, dg = _norm_input_grad(dqb, p["w_q"], h, dh, p["norm"], tb=True, name=f"{tag}_bwd_dh")
    return dh2, dh2b, {"norm": dg, "mem_norm": dg_mem, "w_q": dw_q, "w_kv_t": dw_kv_t, "w_o": dw_o}


AB_MAIN = 2 * LRU_WIDTH + 3 * ATT_W


def _ab_fwd(h, hn, p, nb, next_gain, before_out=None):
    w_in_t = p["w_in_t"]
    zug, = _matmul(hn, w_in_t[:2 * LRU_WIDTH], tb=True, name="ab_in_ug")
    qkv, = _matmul(hn, w_in_t[2 * LRU_WIDTH:AB_MAIN], tb=True, outs=(BF16,), tn=768, name="ab_in_qkv")
    zf, = _matmul(hn, w_in_t[AB_MAIN:], tb=True, name="ab_in_f")
    fl = zf[:, :8].reshape(nb, SEQ, 8).transpose(0, 2, 1)
    cbias = _fox_bias(fl, p["b_f"], name="ab_fox_bias")
    kb = cbias.reshape(nb, 8, SEQ // ATT_CHUNK, ATT_CHUNK)
    y_a, hl = _lru_fwd(zug, p["conv_w"], p["conv_b"], p["w_a"], p["b_a"], p["w_i"], p["b_i"], p["lam"], name="ab_lru")
    qkv3 = _seq3(qkv, nb)
    o, lse = _attn_fwd((qkv3, 0), (qkv3, 1), (qkv3, 2), kb, heads=8, dh=HEAD_DIM, mode="causal", name="ab_fox")
    o = _flat2(o)
    y, = _rowwise(lambda a, b: jnp.concatenate([a, b.astype(BF16)], axis=1), [y_a, o], [], [(D_MODEL, BF16)], name="ab_ycat")
    if before_out is not None:
        before_out(y)
    out, hn_next = _residual_out(y, p["w_out"], h, next_gain, name="ab_out")
    return out, hn_next, (h, hn, zug, qkv3, fl, kb, hl, o, lse, y)


def _ab_bwd(dh, dhb, p, saved, nb):
    h, hn, zug, qkv3, fl, kb, hl, o, lse, y = saved
    dy, dyb, delta = _matmul(dhb, p["w_out"], tb=True, outs=(F32, BF16, (F32, LANES)), extras=(y,), consts=(_head_expand(8, ATT_W).T,),
                             epilogue=lambda acc, y, e: (acc, acc, _dot_exact(acc[:, ATT_W:] * y[:, ATT_W:].astype(F32), e)),
                             name="ab_bwd_dy")
    dw_out, = _matmul(y, dhb, ta=True, outs=(BF16,),name="ab_bwd_dwout")
    dzug, dcw, dcb, dwa, dba, dwi, dbi, dlam = _lru_bwd(zug, hl, dy, p["conv_w"], p["conv_b"], p["w_a"], p["b_a"],
                                                        p["w_i"], p["b_i"], p["lam"], name="ab_bwd_lru")
    dq, dk, dv, dkb, drow = _attn_bwd((qkv3, 0), (qkv3, 1), (qkv3, 2), (_seq3(dyb, nb), 1), lse, _seq3(delta, nb), kb,
                                      heads=8, dh=HEAD_DIM, mode="causal", name="ab_bwd_fox")
    dcum = dkb.reshape(nb, 8, SEQ) + drow[:, :, :8].transpose(0, 2, 1)
    dfl, dbf = _fox_bias_bwd(fl, p["b_f"], dcum, name="ab_bwd_fox_bias")
    dzf = jnp.pad(dfl.transpose(0, 2, 1).reshape(nb * SEQ, 8), ((0, 0), (0, LANES - 8)))
    dz, = _rowwise(lambda a, q, k, v, f: jnp.concatenate([a, q.astype(BF16), k.astype(BF16), v.astype(BF16), f.astype(BF16)], axis=1),
                   [dzug, _flat2(dq), _flat2(dk), _flat2(dv), dzf], [], [(AB_MAIN + LANES, BF16)], name="ab_bwd_dzcat")
    dw_in_t, = _matmul(dz, hn, ta=True, outs=(BF16,),tm=384, name="ab_bwd_dwin")
    dh2, dh2b, dg = _norm_input_grad(dz, p["w_in_t"], h, dh, p["norm"], name="ab_bwd_dh")
    grads = {"norm": dg, "w_in_t": dw_in_t[:AB_MAIN + 8], "conv_w": dcw, "conv_b": dcb, "w_a": _diag_blocks(dwa), "b_a": dba,
             "w_i": _diag_blocks(dwi), "b_i": dbi, "lam": dlam, "b_f": dbf.reshape(1, 8), "w_out": dw_out}
    return dh2, dh2b, grads


CD_IN = 3 * ATT_W + ATT_W + 2 * SWA_KW


def _gqa_share():
    src = jnp.arange(SWA_KW)[:, None]
    dst = jnp.arange(ATT_W)[None, :]
    per_kv = ATT_W // (SWA_KW // HEAD_DIM)
    return ((dst // per_kv == src // HEAD_DIM) & (dst % HEAD_DIM == src % HEAD_DIM)).astype(BF16)


def _cd_fwd(h, hn, p, nb, next_gain):
    z, = _matmul(hn, p["w_in_t"], tb=True, tn=1152, name="cd_in")
    cos, sin = _rope_tables()
    per_seq = SEQ // ROW_TILE
    table_map = lambda i: (i % per_seq, 0)

    def rope_fn(z, cos, sin, share):
        qc, kc, vc = z[:, 0:ATT_W], z[:, ATT_W:2 * ATT_W], z[:, 2 * ATT_W:3 * ATT_W]
        qd, kd, vd = z[:, 3 * ATT_W:4 * ATT_W], z[:, 4 * ATT_W:4 * ATT_W + SWA_KW], z[:, 4 * ATT_W + SWA_KW:]
        kd8 = jnp.dot(_rotate(kd, cos, sin).astype(BF16), share, preferred_element_type=F32)
        vd8 = jnp.dot(vd.astype(BF16), share, preferred_element_type=F32)
        qk = jnp.concatenate([_rotate(qc, cos, sin), _rotate(kc, cos, sin)], axis=1)
        return qk, vc, _rotate(qd, cos, sin), kd8, vd8, qk, qk, vc, vc
    dils = [dil for _, dil in DIL_PATTERN if dil > 1]
    outs = _rowwise(rope_fn, [z, cos, sin], [_gqa_share()],
                    [(2 * ATT_W, BF16)] + [(ATT_W, BF16)] * 4 + [(2 * ATT_W, BF16, d) for d in dils] + [(ATT_W, BF16, d) for d in dils],
                    name="cd_rope", row_maps=[None, table_map, table_map])
    qk, vc, qd, kd, vd = outs[:5]
    views = {1: (_seq3(qk, nb), _seq3(vc, nb))}
    for d, qk_d, vc_d in zip(dils, outs[5:5 + len(dils)], outs[5 + len(dils):]):
        views[d] = (_seq3(qk_d, nb), _seq3(vc_d, nb))
    in_classes = lambda a, width, d: _flat2(a) if d == 1 else ("classes", _flat2(a), width, d)
    branch = []
    for window, dil in DIL_PATTERN:
        qk_v, vc_v = views[dil]
        o_i, lse_i = _attn_fwd((qk_v, 0), (qk_v, 1), (vc_v, 0), classes=dil, heads=8, dh=HEAD_DIM, mode="band",
                               max_dist=window // dil, name=f"cd_dil{dil}")
        branch.append((in_classes(o_i, ATT_W, dil), in_classes(lse_i, LANES, dil)))
    expand = _head_expand(8, ATT_W)

    qd3, kd3, vd3 = _seq3(qd, nb), _seq3(kd, nb), _seq3(vd, nb)
    o_d, lse_band = _attn_fwd((qd3, 0), (kd3, 0), (vd3, 0), heads=8, dh=HEAD_DIM, mode="band", max_dist=SWA_WINDOW - 1,
                              name="cd_swa")

    def mix(o1, o2, o3, l1, l2, l3, o_band, l_band, e, sink):
        m = jnp.maximum(jnp.maximum(l1, l2), l3)
        lt = m + jnp.log(jnp.exp(l1 - m) + jnp.exp(l2 - m) + jnp.exp(l3 - m))
        y_c = sum(_dot_exact(jnp.exp(l - lt), e) * o_ for o_, l in ((o1, l1), (o2, l2), (o3, l3)))
        lt_d = jnp.maximum(l_band, sink) + jnp.log(1.0 + jnp.exp(-jnp.abs(l_band - sink)))
        y_d = o_band * _dot_exact(jnp.exp(l_band - lt_d), e)
        return jnp.concatenate([y_c, y_d], axis=1), y_c, lt, y_d, lt_d
    y, y_c, lse_c, y_d, lse_d = _rowwise(
        mix, [b[0] for b in branch] + [b[1] for b in branch] + [_flat2(o_d), _flat2(lse_band)], [expand, p["sink"]],
        [(D_MODEL, BF16), (ATT_W, F32), (LANES, F32), (ATT_W, F32), (LANES, F32)], name="cd_mix")
    out, hn_next = _residual_out(y, p["w_out"], h, next_gain, name="cd_out")
    return out, hn_next, (h, hn, views, qd3, kd3, vd3, y_c, lse_c, y_d, lse_d, y)


def _cd_bwd(dh, dhb, p, saved, nb):
    h, hn, views, qd3, kd3, vd3, y_c, lse_c, y_d, lse_d, y = saved
    dy, dyb = _matmul(dhb, p["w_out"], tb=True, outs=(F32, BF16), epilogue=lambda acc: (acc, acc), name="cd_bwd_dy")
    dw_out, = _matmul(y, dhb, ta=True, outs=(BF16,),name="cd_bwd_dwout")
    dyb3 = _seq3(dyb, nb)
    dils = [dil for _, dil in DIL_PATTERN if dil > 1]
    gather = _head_expand(8, ATT_W).T

    def prepare(do, o, lse, do_d, o_d, lse_d, e, sink):
        delta = _dot_exact(do * o, e)
        delta_d = _dot_exact(do_d * o_d, e)
        dsink = -jnp.sum(jnp.exp(sink - lse_d) * delta_d, axis=0, keepdims=True)
        return (delta,) + (do,) * len(dils) + (lse,) * len(dils) + (delta,) * len(dils) + (delta_d, dsink)
    outs = _rowwise(prepare, [(dy, ATT_W, 0), y_c, lse_c, (dy, ATT_W, 1), y_d, lse_d], [gather, p["sink"]],
                    [(LANES, F32)] + [(ATT_W, BF16, d) for d in dils] + [(LANES, F32, d) for d in dils] * 2 + [(LANES, F32)],
                    [(1, LANES)], name="cd_bwd_delta")
    delta_d, dsink = outs[-2:]
    seen = {1: ((dyb3, 0), _seq3(lse_c, nb), _seq3(outs[0], nb))}
    for j, d in enumerate(dils):
        seen[d] = ((_seq3(outs[1 + j], nb), 0), _seq3(outs[1 + len(dils) + j], nb), _seq3(outs[1 + 2 * len(dils) + j], nb))
    in_classes = lambda a, d: _flat2(a) if d == 1 else ("classes", _flat2(a), ATT_W, d)
    parts = []
    for window, dil in DIL_PATTERN:
        (qk_v, vc_v), (do_v, lse_v, delta_v) = views[dil], seen[dil]
        grads = _attn_bwd((qk_v, 0), (qk_v, 1), (vc_v, 0), do_v, lse_v, delta_v, classes=dil, heads=8, dh=HEAD_DIM, mode="band",
                          max_dist=window // dil, dq_dtype=BF16, name=f"cd_bwd_dil{dil}")
        parts.append([in_classes(a, dil) for a in grads])
    dqd, dkd, dvd = _attn_bwd((qd3, 0), (kd3, 0), (vd3, 0), (dyb3, 1), _seq3(lse_d, nb), _seq3(delta_d, nb), heads=8,
                              dh=HEAD_DIM, mode="band", max_dist=SWA_WINDOW - 1, dq_dtype=BF16, name="cd_bwd_swa")
    cos, sin = _rope_tables()
    per_seq = SEQ // ROW_TILE
    table_map = lambda i: (i % per_seq, 0)

    def unrope(q1, q2, q3, k1, k2, k3, v1, v2, v3, qd, kd8, vd8, cos, sin, gather):
        back = lambda x: _rotate(x.astype(F32), cos, -sin)
        kd, vd = _dot_exact(kd8, gather), _dot_exact(vd8, gather)
        return jnp.concatenate([back(q1 + q2 + q3), back(k1 + k2 + k3), v1 + v2 + v3, back(qd), back(kd), vd], axis=1)
    ins = [parts[b][t] for t in range(3) for b in range(3)] + [_flat2(dqd), _flat2(dkd), _flat2(dvd), cos, sin]
    dz, = _rowwise(unrope, ins, [_gqa_share().T], [(CD_IN, BF16)], name="cd_bwd_unrope",
                   row_maps=[None] * 12 + [table_map, table_map])
    dw_in_t, = _matmul(dz, hn, ta=True, outs=(BF16,),tm=384, name="cd_bwd_dwin")
    dh2, dh2b, dg = _norm_input_grad(dz, p["w_in_t"], h, dh, p["norm"], name="cd_bwd_dh")
    return dh2, dh2b, {"norm": dg, "w_in_t": dw_in_t, "sink": dsink[:, :8], "w_out": dw_out}


def _local_step(x, mem, target, w, complete=None, grads_ready=None):
    nb = x.shape[0]
    h = x.reshape(nb * SEQ, D_MODEL)
    mem2 = mem.reshape(nb * MEM_LEN, D_MODEL)
    tgt = target.reshape(nb * SEQ, D_MODEL)

    hn = _rms_fwd(h, w["ab"]["norm"], name="ab_norm")
    h, hn, s_ab = _ab_fwd(h, hn, w["ab"], nb, w["xa0"]["norm"], before_out=complete)
    h, hn, s_xa0 = _xa_fwd(h, hn, mem2, w["xa0"], "xa0", nb, w["mlp0"]["norm"])
    h, hn, s_mlp0 = _mlp_fwd(h, hn, w["mlp0"], "mlp0", w["cd"]["norm"])
    h, hn, s_cd = _cd_fwd(h, hn, w["cd"], nb, w["xa1"]["norm"])
    h, hn, s_xa1 = _xa_fwd(h, hn, mem2, w["xa1"], "xa1", nb, w["mlp1"]["norm"])
    h, _, s_mlp1 = _mlp_fwd(h, hn, w["mlp1"], "mlp1", None)

    dh, dhb, loss, dg_final = _loss_and_grad(h, tgt, w["final_norm"], name="loss")
    grads = {"final_norm": dg_final}
    dh, dhb, grads["mlp1"] = _mlp_bwd(dh, dhb, w["mlp1"], s_mlp1, "mlp1")
    dh, dhb, grads["xa1"] = _xa_bwd(dh, dhb, mem2, w["xa1"], s_xa1, "xa1", nb)
    dh, dhb, grads["cd"] = _cd_bwd(dh, dhb, w["cd"], s_cd, nb)
    if grads_ready is not None:
        grads_ready(0, grads)
    dh, dhb, grads["mlp0"] = _mlp_bwd(dh, dhb, w["mlp0"], s_mlp0, "mlp0")
    dh, dhb, grads["xa0"] = _xa_bwd(dh, dhb, mem2, w["xa0"], s_xa0, "xa0", nb)
    if grads_ready is not None:
        grads_ready(1, grads)
    dh, dhb, grads["ab"] = _ab_bwd(dh, dhb, w["ab"], s_ab, nb)
    return loss, dh.reshape(nb, SEQ, D_MODEL), grads


ANY = pl.BlockSpec(memory_space=pl.ANY)


def _place():
    x, y, c = lax.axis_index("x"), lax.axis_index("y"), lax.axis_index("c")
    return x, y, c, [(1 - x, y), (x, 1 - y), (1 - x, 1 - y)]


def _gather_chips(shard):
    half = shard.shape[0] // 2

    def body(src, out, send_sems, recv_sems):
        x, y, c, chips = _place()
        sibling = (x, y, 1 - c)

        def rows(slot, core):
            return out.at[slot, pl.ds(core * half, half)]

        def copy(k, src_ref, dst_ref, to):
            return pltpu.make_async_remote_copy(src_ref=src_ref, dst_ref=dst_ref, send_sem=send_sems.at[k],
                                                recv_sem=recv_sems.at[k], device_id=to, device_id_type=MESH)

        first = [copy(k, src.at[pl.ds(c * half, half)], rows(2 * x + y, c), (px, py, c)) for k, (px, py) in enumerate(chips)]
        for cp in first:
            cp.start()
        passed = [copy(3 + k, rows(2 * px + py, c), rows(2 * px + py, c), sibling) for k, (px, py) in enumerate(chips)]
        for k, (px, py) in enumerate(chips):
            copy(k, src.at[pl.ds(c * half, half)], rows(2 * px + py, c), (px, py, c)).wait_recv()
            passed[k].start()
        for k, (px, py) in enumerate(chips):
            copy(3 + k, rows(2 * px + py, 1 - c), rows(2 * px + py, 1 - c), sibling).wait_recv()
        for cp in first + passed:
            cp.wait_send()

    return pl.pallas_call(
        body, out_shape=jax.ShapeDtypeStruct((N_CHIPS,) + shard.shape, shard.dtype), in_specs=[ANY], out_specs=ANY,
        scratch_shapes=[pltpu.SemaphoreType.DMA((6,)), pltpu.SemaphoreType.DMA((6,))], name="gather_chips",
    )(shard)


HBM = pl.BlockSpec(memory_space=pltpu.HBM)
SEM = pl.BlockSpec(memory_space=pltpu.SEMAPHORE)
SIDE_EFFECT = pltpu.SideEffectType.DATAFLOW_SIDE_EFFECTING


def _chip_copies(src, land, send_sems, recv_sems):
    half = src.shape[0] // 2
    x, y, c, chips = _place()

    def copy(k, slot, to):
        return pltpu.make_async_remote_copy(src_ref=src.at[pl.ds(c * half, half)], dst_ref=land.at[slot, pl.ds(c * half, half)],
                                            send_sem=send_sems[k], recv_sem=recv_sems[k], device_id=to, device_id_type=MESH)
    out = [copy(k, 2 * x + y, (px, py, c)) for k, (px, py) in enumerate(chips)]
    back = [copy(k, 2 * px + py, (px, py, c)) for k, (px, py) in enumerate(chips)]
    return out, back


def _gather_start(shard, after):
    def body(src, land, *rest):
        rest = rest[len(after):]
        sems, token = rest[:6], rest[8]
        out, _ = _chip_copies(src, land, sems[:3], sems[3:])
        for cp in out:
            cp.start()
        token[...] = jnp.zeros(token.shape, F32)

    sem = pltpu.SemaphoreType.DMA(())
    land_shape = (N_CHIPS,) + shard.shape
    return pl.pallas_call(
        body, name="gather_start",
        out_shape=(sem,) * 6 + (pltpu.HBM(shard.shape, shard.dtype), pltpu.HBM(land_shape, shard.dtype),
                                jax.ShapeDtypeStruct((8, LANES), F32)),
        in_specs=(HBM, HBM) + (pl.BlockSpec(memory_space=pl.ANY),) * len(after),
        out_specs=(SEM,) * 6 + (HBM, HBM, pl.BlockSpec(memory_space=pltpu.VMEM)),
        input_output_aliases={0: 6, 1: 7}, compiler_params=pltpu.CompilerParams(has_side_effects=SIDE_EFFECT),
    )(pltpu.with_memory_space_constraint(shard, pltpu.HBM),
      pltpu.with_memory_space_constraint(lax.empty(land_shape, shard.dtype), pltpu.HBM), *after)


def _gather_wait(started, after):
    sems, src_thru, land_thru = started[:6], started[6], started[7]

    def body(src, land, *rest):
        out, back = _chip_copies(src, land, rest[:3], rest[3:6])
        for cp in out:
            cp.wait_send()
        for cp in back:
            cp.wait_recv()

    return pl.pallas_call(
        body, name="gather_wait",
        out_shape=(pltpu.HBM(src_thru.shape, src_thru.dtype), pltpu.HBM(land_thru.shape, land_thru.dtype)),
        in_specs=(HBM, HBM) + (SEM,) * 6 + (pl.BlockSpec(memory_space=pl.ANY),), out_specs=(HBM, HBM),
        input_output_aliases={0: 0, 1: 1}, compiler_params=pltpu.CompilerParams(has_side_effects=SIDE_EFFECT),
    )(src_thru, land_thru, *sems, after)


def _gather_pass(land):
    half = land.shape[1] // 2

    def body(land_in, land_out, send_sems, recv_sems):
        x, y, c, chips = _place()

        def copy(k, slot, core):
            rows = land_out.at[slot, pl.ds(core * half, half)]
            return pltpu.make_async_remote_copy(src_ref=rows, dst_ref=rows, send_sem=send_sems.at[k], recv_sem=recv_sems.at[k],
                                                device_id=(x, y, 1 - c), device_id_type=MESH)
        sends = [copy(k, 2 * px + py, c) for k, (px, py) in enumerate(chips)]
        for cp in sends:
            cp.start()
        for k, (px, py) in enumerate(chips):
            copy(k, 2 * px + py, 1 - c).wait_recv()
        for cp in sends:
            cp.wait_send()

    return pl.pallas_call(
        body, out_shape=jax.ShapeDtypeStruct(land.shape, land.dtype), in_specs=[ANY], out_specs=ANY,
        scratch_shapes=[pltpu.SemaphoreType.DMA((3,)), pltpu.SemaphoreType.DMA((3,))], input_output_aliases={0: 0},
        name="gather_pass",
    )(land)


def _swap_cores(block, *, name, half_rows=None):
    shape = block.shape if half_rows is None else (block.shape[0], half_rows, block.shape[2])

    def body(src, out, send_sem, recv_sem):
        x, y, c, _ = _place()
        part = src if half_rows is None else src.at[:, pl.ds((1 - c) * half_rows, half_rows)]
        cp = pltpu.make_async_remote_copy(src_ref=part, dst_ref=out, send_sem=send_sem, recv_sem=recv_sem,
                                          device_id=(x, y, 1 - c), device_id_type=MESH)
        cp.start()
        cp.wait()

    return pl.pallas_call(
        body, out_shape=jax.ShapeDtypeStruct(shape, block.dtype), in_specs=[ANY], out_specs=ANY,
        scratch_shapes=[pltpu.SemaphoreType.DMA(()), pltpu.SemaphoreType.DMA(())], name=name,
    )(block)


def _scatter_copies(parts, land, send_sems, recv_sems):
    x, y, c, chips = _place()
    return [pltpu.make_async_remote_copy(src_ref=parts.at[2 * px + py], dst_ref=land.at[k], send_sem=send_sems[k],
                                         recv_sem=recv_sems[k], device_id=(px, py, c), device_id_type=MESH)
            for k, (px, py) in enumerate(chips)]


def _scatter_start(parts, tag):
    def body(src, land, *rest):
        for cp in _scatter_copies(src, land, rest[:3], rest[3:6]):
            cp.start()
        rest[8][...] = jnp.zeros(rest[8].shape, F32)

    sem = pltpu.SemaphoreType.DMA(())
    land_shape = (3,) + parts.shape[1:]
    return pl.pallas_call(
        body, name=f"scatter_start_{tag}",
        out_shape=(sem,) * 6 + (pltpu.HBM(parts.shape, parts.dtype), pltpu.HBM(land_shape, parts.dtype),
                                jax.ShapeDtypeStruct((8, LANES), F32)),
        in_specs=(HBM, HBM), out_specs=(SEM,) * 6 + (HBM, HBM, pl.BlockSpec(memory_space=pltpu.VMEM)),
        input_output_aliases={0: 6, 1: 7}, compiler_params=pltpu.CompilerParams(has_side_effects=SIDE_EFFECT),
    )(pltpu.with_memory_space_constraint(parts, pltpu.HBM),
      pltpu.with_memory_space_constraint(lax.empty(land_shape, parts.dtype), pltpu.HBM))


def _scatter_wait(started, after, tag):
    def body(src, land, *rest):
        for cp in _scatter_copies(src, land, rest[:3], rest[3:6]):
            cp.wait_send()
            cp.wait_recv()

    src_thru, land_thru = started[6], started[7]
    return pl.pallas_call(
        body, name=f"scatter_wait_{tag}",
        out_shape=(pltpu.HBM(src_thru.shape, src_thru.dtype), pltpu.HBM(land_thru.shape, land_thru.dtype)),
        in_specs=(HBM, HBM) + (SEM,) * 6 + (pl.BlockSpec(memory_space=pl.ANY),), out_specs=(HBM, HBM),
        input_output_aliases={0: 0, 1: 1}, compiler_params=pltpu.CompilerParams(has_side_effects=SIDE_EFFECT),
    )(src_thru, land_thru, *started[:6], after)[1]


def _sum_all_devices(vec):
    rows = vec.shape[0]

    def body(x_ref, total_ref, all_ref, send_sems, recv_sems, local_sem):
        x, y, c, chips = _place()
        me, sibling = (x, y, c), (x, y, 1 - c)

        def slot(px, py, pc):
            return all_ref.at[4 * px + 2 * py + pc]

        def copy(k, block, to, src=None):
            return pltpu.make_async_remote_copy(src_ref=slot(*block) if src is None else src, dst_ref=slot(*block),
                                                send_sem=send_sems.at[k], recv_sem=recv_sems.at[k], device_id=to,
                                                device_id_type=MESH)

        mine = pltpu.make_async_copy(x_ref, slot(*me), local_sem)
        mine.start()
        first = [copy(0, me, sibling, src=x_ref)] + [copy(1 + j, me, (*chip, c), src=x_ref) for j, chip in enumerate(chips)]
        for cp in first:
            cp.start()
        passed = [copy(4 + j, (*chip, c), sibling) for j, chip in enumerate(chips)]
        for j, chip in enumerate(chips):
            copy(1 + j, (*chip, c), me).wait_recv()
            passed[j].start()
        copy(0, sibling, me).wait_recv()
        for j, chip in enumerate(chips):
            copy(4 + j, (*chip, 1 - c), me).wait_recv()
        for cp in first + passed:
            cp.wait_send()
        mine.wait()
        acc = all_ref[0]
        for d in range(1, 8):
            acc = acc + all_ref[d]
        total_ref[...] = acc

    vmem = pl.BlockSpec(memory_space=pltpu.VMEM)
    return pl.pallas_call(
        body, out_shape=[jax.ShapeDtypeStruct((rows, LANES), F32), jax.ShapeDtypeStruct((8, rows, LANES), F32)],
        in_specs=[vmem], out_specs=[vmem, vmem],
        scratch_shapes=[pltpu.SemaphoreType.DMA((7,)), pltpu.SemaphoreType.DMA((7,)), pltpu.SemaphoreType.DMA(())],
        name="sum_all_devices", compiler_params=pltpu.CompilerParams(vmem_limit_bytes=VMEM_LIMIT_BYTES),
    )(vec)[0]


PACK_COLS = 1024
BIG = (("mlp_w_up", 2, 1024, True), ("mlp_w_down", 2, 1024, False), ("xa_w_kv", 2, 512, True), ("xa_w_q", 2, 256, False),
       ("xa_w_o", 2, 256, False), ("ab_w_out", 1, 256, False), ("cd_w_out", 1, 256, False), ("cd_w_in", 1, 576, True),
       ("ab_w_in", 1, 642, True))
ENTRIES = tuple((name, layer, rows, transposed) for name, layers, rows, transposed in BIG for layer in range(layers))
FIRST_ENTRIES = tuple(e for e in ENTRIES if e[0] == "ab_w_in")
LATER_ENTRIES = tuple(e for e in ENTRIES if e[0] != "ab_w_in")


def _tile_rows(entry):
    return -(-entry[2] // 16) * 16


def _padded_rows(entries):
    return -(-sum(_tile_rows(e) for e in entries) // 32) * 32


SMALL = (("ab_norm", (1, 1024)), ("ab_conv_w", (1, 4, 512)), ("ab_conv_b", (1, 512)), ("lru_w_a", (1, 8, 64, 64)),
         ("lru_b_a", (1, 512)), ("lru_w_i", (1, 8, 64, 64)), ("lru_b_i", (1, 512)), ("lru_lambda", (1, 512)),
         ("fox_b_f", (1, 8)), ("cd_norm", (1, 1024)), ("cd_sink", (1, 8)), ("xa_norm", (2, 1024)),
         ("xa_mem_norm", (2, 1024)), ("mlp_norm", (2, 1024)), ("final_norm", (1024,)), ("loss", (1,)))
SPLIT_SMALL = ("ab_conv_w", "cd_norm")
SPLIT_SMALL_ROWS = 16
assert _padded_rows(FIRST_ENTRIES) - SPLIT_SMALL_ROWS >= sum(_tile_rows(e) for e in FIRST_ENTRIES)


def _pack_rows(parts, entries, dtype):
    lead = parts[0].shape[:-2]
    zeros = lambda rows: jnp.zeros(lead + (rows, PACK_COLS), dtype)
    pieces = [jnp.pad(p.astype(dtype), ((0, 0),) * len(lead) + ((0, _tile_rows(e) - e[2]), (0, 0))) for p, e in zip(parts, entries)]
    tail = _padded_rows(entries) - sum(_tile_rows(e) for e in entries)
    return jnp.concatenate(pieces + ([zeros(tail)] if tail else []), axis=len(lead))


def _unpack_rows(packed, entries):
    out, r0 = [], 0
    for e in entries:
        out.append(packed[..., r0:r0 + e[2], :])
        r0 += _tile_rows(e)
    return out


def _shard_rows(w, entries):
    return [(w[name][layer].T if transposed else w[name][layer]) for name, layer, _, transposed in entries]


def _shard_blocks(rows):
    out = {}
    for (name, layer, _, transposed), r in zip(ENTRIES, rows):
        out.setdefault(name, []).append(r.T if transposed else r)
    return {name: jnp.stack(layers) for name, layers in out.items()}


def _pack_small(vals):
    flat = jnp.concatenate([vals[name].astype(F32).reshape(-1) for name, _ in SMALL])
    size = -(-flat.shape[0] // (8 * LANES)) * (8 * LANES)
    return jnp.pad(flat, (0, size - flat.shape[0])).reshape(-1, LANES)


def _unpack_small(packed):
    flat, out, at = packed.reshape(-1), {}, 0
    for name, shape in SMALL:
        out[name] = flat[at:at + math.prod(shape)].reshape(shape)
        at += math.prod(shape)
    return out


def _chip_part(full, chip):
    width = full.shape[-1] // N_CHIPS
    return lax.dynamic_slice_in_dim(full, chip * width, width, axis=full.ndim - 1)


def _chip_embed(part, chip):
    full = jnp.zeros(part.shape[:-1] + (part.shape[-1] * N_CHIPS,), part.dtype)
    return lax.dynamic_update_slice_in_dim(full, part, chip * part.shape[-1], axis=part.ndim - 1)


SUBLAYER_KEYS = {"ab_w_in": ("ab", "w_in_t"), "ab_w_out": ("ab", "w_out"), "cd_w_in": ("cd", "w_in_t"), "cd_w_out": ("cd", "w_out"),
                 "xa_w_q": ("xa", "w_q"), "xa_w_kv": ("xa", "w_kv_t"), "xa_w_o": ("xa", "w_o"), "mlp_w_up": ("mlp", "w_up_t"),
                 "mlp_w_down": ("mlp", "w_down")}


def _sublayer(name, layer):
    block, leaf = SUBLAYER_KEYS[name]
    return (block if block in ("ab", "cd") else f"{block}{layer}"), leaf


def _set_big(params, full_rows):
    for (name, layer), rows in full_rows.items():
        block, leaf = _sublayer(name, layer)
        if name == "ab_w_in":
            rows = jnp.concatenate([rows, jnp.zeros((LANES - 8, PACK_COLS), rows.dtype)])
        params[block][leaf] = rows


def _build_params(full_rows, w):
    pad = lambda a: jnp.pad(a, ((0, 0), (0, LANES - a.shape[1])))
    params = {
        "ab": dict(norm=w["ab_norm"], conv_w=w["ab_conv_w"][0], conv_b=w["ab_conv_b"], w_a=_block_diag(w["lru_w_a"][0]),
                   b_a=w["lru_b_a"], w_i=_block_diag(w["lru_w_i"][0]), b_i=w["lru_b_i"], lam=w["lru_lambda"],
                   b_f=w["fox_b_f"].reshape(8, 1)),
        "cd": dict(norm=w["cd_norm"], sink=pad(w["cd_sink"])),
        "final_norm": w["final_norm"].reshape(1, D_MODEL),
    }
    for layer in range(2):
        params[f"xa{layer}"] = dict(norm=w["xa_norm"][layer:layer + 1], mem_norm=w["xa_mem_norm"][layer:layer + 1])
        params[f"mlp{layer}"] = dict(norm=w["mlp_norm"][layer:layer + 1])
    _set_big(params, full_rows)
    return params


def _grad_rows(g, entries=ENTRIES):
    out = []
    for name, layer, _, _ in entries:
        block, leaf = _sublayer(name, layer)
        out.append(g[block][leaf])
    return out


def _reduce_group(entry):
    name, layer = entry[0], entry[1]
    if name.startswith("ab_"):
        return 2
    return 0 if layer == 1 or name.startswith("cd_") else 1


REDUCE_GROUPS = tuple(tuple(e for e in ENTRIES if _reduce_group(e) == group) for group in range(3))


def _reduce_tile(half):
    return max(t for t in range(16, 1025, 16) if half % t == 0)


def _reduce_start(grads_by_chip, core, chip, tag):
    half = grads_by_chip.shape[1] // 2
    tile = _reduce_tile(half)
    steps = half // tile
    place = jnp.stack([core, chip]).astype(jnp.int32)
    got = _swap_cores(grads_by_chip, name=f"swap_cores_{tag}", half_rows=half)
    block = (1, tile, PACK_COLS)

    def sum_cores(place_ref, mine_ref, got_ref, f32_ref, bf16_ref):
        total = mine_ref[...].astype(F32) + got_ref[...].astype(F32)
        f32_ref[...] = total
        bf16_ref[...] = total.astype(BF16)

    pair32, pair16 = pl.pallas_call(
        sum_cores, name=f"sum_cores_{tag}",
        grid_spec=pltpu.PrefetchScalarGridSpec(
            num_scalar_prefetch=1, grid=(N_CHIPS, steps),
            in_specs=[pl.BlockSpec(block, lambda s, i, place_ref: (s, place_ref[0] * steps + i, 0)),
                      pl.BlockSpec(block, lambda s, i, place_ref: (s, i, 0))],
            out_specs=[pl.BlockSpec(block, lambda s, i, place_ref: (s, i, 0))] * 2),
        out_shape=[jax.ShapeDtypeStruct((N_CHIPS, half, PACK_COLS), F32), jax.ShapeDtypeStruct((N_CHIPS, half, PACK_COLS), BF16)],
        compiler_params=_params("arbitrary", "arbitrary"),
    )(place, grads_by_chip, got)
    return pair32, _scatter_start(pair16, tag), place


def _reduce_finish(state, core, tag, after):
    pair32, started, place = state
    half = pair32.shape[1]
    tile = _reduce_tile(half)
    landed = _scatter_wait(started, after, tag)

    def sum_chips(place_ref, own_ref, landed_ref, out_ref):
        out_ref[...] = own_ref[0] + landed_ref[0].astype(F32) + landed_ref[1].astype(F32) + landed_ref[2].astype(F32)

    done = pl.pallas_call(
        sum_chips, name=f"sum_chips_{tag}",
        grid_spec=pltpu.PrefetchScalarGridSpec(
            num_scalar_prefetch=1, grid=(half // tile,),
            in_specs=[pl.BlockSpec((1, tile, PACK_COLS), lambda i, place_ref: (place_ref[1], i, 0)),
                      pl.BlockSpec((3, tile, PACK_COLS), lambda i, place_ref: (0, i, 0))],
            out_specs=pl.BlockSpec((tile, PACK_COLS), lambda i, place_ref: (i, 0))),
        out_shape=jax.ShapeDtypeStruct((half, PACK_COLS), F32), compiler_params=_params("arbitrary"),
    )(place, pair32, landed)
    theirs = _swap_cores(done, name=f"share_halves_{tag}")
    return jnp.where(core == 0, jnp.concatenate([done, theirs]), jnp.concatenate([theirs, done]))


def kernel(x, mem, ab_norm, ab_w_in, ab_conv_w, ab_conv_b, lru_w_a, lru_b_a, lru_w_i, lru_b_i, lru_lambda, fox_b_f, ab_w_out, cd_norm, cd_w_in, cd_sink, cd_w_out, xa_norm, xa_mem_norm, xa_w_q, xa_w_kv, xa_w_o, mlp_norm, mlp_w_up, mlp_w_down, final_norm, loss_target, m_ab_norm, m_ab_w_in, m_ab_conv_w, m_ab_conv_b, m_lru_w_a, m_lru_b_a, m_lru_w_i, m_lru_b_i, m_lru_lambda, m_fox_b_f, m_ab_w_out, m_cd_norm, m_cd_w_in, m_cd_sink, m_cd_w_out, m_xa_norm, m_xa_mem_norm, m_xa_w_q, m_xa_w_kv, m_xa_w_o, m_mlp_norm, m_mlp_w_up, m_mlp_w_down, m_final_norm, v_ab_norm, v_ab_w_in, v_ab_conv_w, v_ab_conv_b, v_lru_w_a, v_lru_b_a, v_lru_w_i, v_lru_b_i, v_lru_lambda, v_fox_b_f, v_ab_w_out, v_cd_norm, v_cd_w_in, v_cd_sink, v_cd_w_out, v_xa_norm, v_xa_mem_norm, v_xa_w_q, v_xa_w_kv, v_xa_w_o, v_mlp_norm, v_mlp_w_up, v_mlp_w_down, v_final_norm):
    given = dict(locals())
    weight_names = [name for name, _ in SMALL if name != "loss"] + [name for name, _, _, _ in BIG]
    w = {name: given[name] for name in weight_names}
    chip = 2 * lax.axis_index("x") + lax.axis_index("y")
    core = lax.axis_index("c")

    slot = lax.broadcasted_iota(jnp.int32, (N_CHIPS, 1, 1), 0)

    def whole(entries, mine, gathered):
        return {(name, layer): jnp.where(slot == chip, own[None], got).reshape(N_CHIPS * rows, PACK_COLS)
                for (name, layer, rows, _), own, got in zip(entries, _unpack_rows(mine, entries), _unpack_rows(gathered, entries))}

    bits = lambda a: lax.bitcast_convert_type(a.reshape(-1), BF16).reshape(-1)
    spare = _padded_rows(FIRST_ENTRIES) - SPLIT_SMALL_ROWS
    small_rows = jnp.zeros((SPLIT_SMALL_ROWS, PACK_COLS), BF16)
    small_rows = small_rows.at[0].set(bits(w["ab_conv_w"])).at[1, :2 * w["cd_norm"].size].set(bits(w["cd_norm"]))
    mine_first = _pack_rows(_shard_rows(w, FIRST_ENTRIES), FIRST_ENTRIES, BF16).at[spare:].set(small_rows)
    mine_later = _pack_rows(_shard_rows(w, LATER_ENTRIES), LATER_ENTRIES, BF16)
    first = _gather_chips(mine_first)
    floats = lambda a: lax.bitcast_convert_type(a.reshape(a.shape[:-1] + (-1, 2)), F32)
    all_small = jnp.where(slot == chip, floats(mine_first[None, spare:]), floats(first[:, spare:]))
    taps, per_chip = w["ab_conv_w"].shape[1:]
    conv_w_full = all_small[:, 0].reshape(N_CHIPS, taps, per_chip).transpose(1, 0, 2).reshape(1, taps, N_CHIPS * per_chip)
    cd_norm_full = all_small[:, 1, :w["cd_norm"].size].reshape(1, -1)
    started = _gather_start(mine_later, after=(first,))
    params = _build_params(whole(FIRST_ENTRIES, mine_first, first), {**w, "ab_conv_w": conv_w_full, "cd_norm": cd_norm_full})
    params["ab"]["norm"] = params["ab"]["norm"] + started[8][0, 0]

    def complete(h):
        mine, landed = _gather_wait(started, after=h)
        _set_big(params, whole(LATER_ENTRIES, mine, _gather_pass(landed)))

    reducing = {}

    def grads_ready(group, g):
        entries = REDUCE_GROUPS[group]
        by_chip = _pack_rows([r.reshape(N_CHIPS, e[2], PACK_COLS) for r, e in zip(_grad_rows(g, entries), entries)], entries, BF16)
        reducing[group] = _reduce_start(by_chip, core, chip, f"g{group}")
        if group < 2:
            block, leaf = ("mlp0", "w_down") if group == 0 else ("ab", "w_out")
            params[block][leaf] = params[block][leaf] + reducing[group][1][8][0, 0].astype(BF16)

    loss_part, grad_x, g = _local_step(x, mem, loss_target, params, complete, grads_ready)
    grads_ready(2, g)
    reduced, after = {}, reducing[2][1][8]
    for group, entries in enumerate(REDUCE_GROUPS):
        after = _reduce_finish(reducing[group], core, f"g{group}", after=after)
        reduced.update({(e[0], e[1]): r for e, r in zip(entries, _unpack_rows(after, entries))})
    grads = _shard_blocks([reduced[e[0], e[1]] for e in ENTRIES])
    pair = lambda key, leaf: jnp.stack([g[f"{key}0"][leaf], g[f"{key}1"][leaf]])

    small_local = {"ab_norm": g["ab"]["norm"], "ab_conv_w": g["ab"]["conv_w"], "ab_conv_b": g["ab"]["conv_b"],
                   "lru_w_a": g["ab"]["w_a"], "lru_b_a": g["ab"]["b_a"], "lru_w_i": g["ab"]["w_i"], "lru_b_i": g["ab"]["b_i"],
                   "lru_lambda": g["ab"]["lam"], "fox_b_f": g["ab"]["b_f"], "cd_norm": g["cd"]["norm"], "cd_sink": g["cd"]["sink"],
                   "xa_norm": pair("xa", "norm"), "xa_mem_norm": pair("xa", "mem_norm"), "mlp_norm": pair("mlp", "norm"),
                   "final_norm": g["final_norm"], "loss": loss_part[0, :1]}
    small_sum_packed = _sum_all_devices(_pack_small(small_local))
    small_sum = _unpack_small(small_sum_packed)
    loss = small_sum["loss"][0]

    delta, new_m, new_v = {}, {}, {}
    for name, _, _, _ in BIG:
        shape = w[name].shape
        flat = lambda a: a.reshape(-1, shape[-1])
        d, m2, v2 = _adamw(flat(w[name]), flat(grads[name]), flat(given["m_" + name]), flat(given["v_" + name]), name=f"adamw_{name}")
        delta[name], new_m[name], new_v[name] = d.reshape(shape), m2.reshape(shape), v2.reshape(shape)

    def small_state(prefix):
        vals = {name: (given[prefix + name] if name != "loss" else jnp.zeros((1,), F32)) for name, _ in SMALL}
        for name in SPLIT_SMALL:
            vals[name] = _chip_embed(vals[name], chip)
        return _pack_small(vals)
    packed = _adamw(small_state(""), small_sum_packed, small_state("m_"), small_state("v_"), name="adamw_small")
    for store, vals in zip((delta, new_m, new_v), packed):
        for name, val in _unpack_small(vals).items():
            store[name] = _chip_part(val, chip) if name in SPLIT_SMALL else val
    for name in SPLIT_SMALL:
        small_sum[name] = _chip_part(small_sum[name], chip)
    grads.update({name: small_sum[name] for name, _ in SMALL})

    order = ["ab_norm", "ab_w_in", "ab_conv_w", "ab_conv_b", "lru_w_a", "lru_b_a", "lru_w_i", "lru_b_i", "lru_lambda", "fox_b_f",
             "ab_w_out", "cd_norm", "cd_w_in", "cd_sink", "cd_w_out", "xa_norm", "xa_mem_norm", "xa_w_q", "xa_w_kv", "xa_w_o",
             "mlp_norm", "mlp_w_up", "mlp_w_down", "final_norm"]
    return (loss, grad_x, *[grads[n] for n in order], *[delta[n] for n in order], *[new_m[n] for n in order],
            *[new_v[n] for n in order])
```

```python
import functools
import math

import jax
import jax.numpy as jnp
from jax import lax
from jax.experimental import pallas as pl
from jax.experimental.pallas import tpu as pltpu

F32 = jnp.float32
BF16 = jnp.bfloat16
MESH = pl.DeviceIdType.MESH

D_MODEL = 1024
SEQ = 2048
HEAD_DIM = 64
LRU_WIDTH = 512
LRU_BLOCKS = 8
LRU_C = 8.0
CONV_WIDTH = 4
ATT_W = 512
SWA_KW = 128
SWA_WINDOW = 128
DIL_PATTERN = ((128, 1), (512, 4), (2048, 16))
MEM_LEN = 256
XA_HEADS = 4
XA_HEAD_DIM = 256
D_FF = 4096
ROPE_THETA = 10000.0
EPS = 1e-6
N_CHIPS = 4
LANES = 128

ADAM_LR, ADAM_B1, ADAM_B2, ADAM_EPS, ADAM_WD, ADAM_STEP = 0.001, 0.9, 0.999, 1e-08, 0.01, 10

VMEM_LIMIT_BYTES = 56 * 1024 * 1024
MASKED = -1e30
ROW_TILE = 512
LRU_CHUNK = 512
ATT_BLOCK = 128
BAND_ROWS = 256
FULL_ROWS = 512
ATT_CHUNK = 512
CAUSAL_ROWS = 256
CAUSAL_ROWS_BACKWARD = 512
CLASS_ROWS = 512


def _params(*sem):
    return pltpu.CompilerParams(dimension_semantics=sem, vmem_limit_bytes=VMEM_LIMIT_BYTES)


def _rowwise(fn, rows, consts=(), out_rows=(), out_accs=(), *, name, tile=ROW_TILE, row_maps=None):
    rows = [r if isinstance(r, tuple) else (r, r.shape[1], 0) for r in rows]
    consts = list(consts)
    nr, nc, no = len(rows), len(consts), len(out_rows)
    dealt_in = [r[3] if isinstance(r[0], str) else None for r in rows]
    dealt_out = [o[2] if len(o) == 3 else None for o in out_rows]
    total = next(r[0].shape[0] for r in rows if not isinstance(r[0], str))
    tile = min(tile, total)
    assert total % tile == 0
    n = total // tile
    n_acc = len(out_accs)

    def body(*refs):
        scratch = list(refs[nr + nc + no + n_acc:])
        vals = []
        for ref, d, row in zip(refs[:nr], dealt_in, rows):
            if d is None:
                vals.append(ref[...])
                continue
            sc, width = scratch.pop(0), row[2]
            for r in range(d):
                for c in range(width // LANES):
                    lanes = slice(r * width + c * LANES, r * width + (c + 1) * LANES)
                    sc.at[c][pl.ds(r, tile // d, stride=d), :] = ref[:, lanes].astype(F32)
            vals.append(jnp.concatenate([sc[c] for c in range(width // LANES)], axis=1))
        outs = fn(*vals, *[r[...] for r in refs[nr:nr + nc]])
        outs = tuple(outs) if isinstance(outs, (tuple, list)) else (outs,)
        for ref, val, d, spec in zip(refs[nr + nc:nr + nc + no], outs[:no], dealt_out, out_rows):
            if d is None:
                ref[...] = val.astype(ref.dtype)
                continue
            sc, width = scratch.pop(0), spec[0]
            for c in range(width // LANES):
                sc[c] = val[:, c * LANES:(c + 1) * LANES].astype(F32)
            for r in range(d):
                for c in range(width // LANES):
                    lanes = slice(r * width + c * LANES, r * width + (c + 1) * LANES)
                    ref[:, lanes] = sc.at[c][pl.ds(r, tile // d, stride=d), :].astype(ref.dtype)
        for ref, val in zip(refs[nr + nc + no:nr + nc + no + n_acc], outs[no:]):
            @pl.when(pl.program_id(0) == 0)
            def _(ref=ref):
                ref[...] = jnp.zeros(ref.shape, ref.dtype)
            ref[...] += val

    in_specs, args, scratch_shapes = [], [], []
    for idx, row in enumerate(rows):
        if isinstance(row[0], str):
            _, arr, width, d = row
            in_specs.append(pl.BlockSpec((tile // d, d * width), lambda i: (i, 0)))
            scratch_shapes.append(pltpu.VMEM((width // LANES, tile, LANES), F32))
        elif row_maps is not None and row_maps[idx] is not None:
            arr, width, _ = row
            in_specs.append(pl.BlockSpec((tile, width), row_maps[idx]))
        else:
            arr, width, cb = row
            in_specs.append(pl.BlockSpec((tile, width), functools.partial(lambda i, cb: (i, cb), cb=cb)))
        args.append(arr)
    in_specs += [pl.BlockSpec(c.shape, lambda i: (0, 0)) for c in consts]
    out_shape, out_specs = [], []
    for spec, d in zip(out_rows, dealt_out):
        w, dt = spec[0], spec[1]
        if d is None:
            out_shape.append(jax.ShapeDtypeStruct((total, w), dt))
            out_specs.append(pl.BlockSpec((tile, w), lambda i: (i, 0)))
        else:
            out_shape.append(jax.ShapeDtypeStruct((total // d, d * w), dt))
            out_specs.append(pl.BlockSpec((tile // d, d * w), lambda i: (i, 0)))
            scratch_shapes.append(pltpu.VMEM((w // LANES, tile, LANES), F32))
    out_shape += [jax.ShapeDtypeStruct(s, F32) for s in out_accs]
    out_specs += [pl.BlockSpec(s, lambda i: (0, 0)) for s in out_accs]
    return pl.pallas_call(
        body, grid=(n,), in_specs=in_specs, out_specs=out_specs, out_shape=out_shape, scratch_shapes=scratch_shapes, name=name,
        compiler_params=_params("arbitrary"),
    )(*args, *consts)


MATMUL_VMEM_BYTES = 40 * 1024 * 1024


def _matmul_tiles(m, n, k, tm, tn, out_bytes):
    tm, tn, tk = min(tm, m), min(tn, n), k

    def need():
        return 2 * (2 * tk * (tm + tn) + tm * tn * out_bytes) + (0 if tk == k else 4 * tm * tn)

    while need() > MATMUL_VMEM_BYTES:
        if tm >= tn and tm % 256 == 0:
            tm //= 2
        elif tn % 256 == 0:
            tn //= 2
        else:
            tk //= 2
    return tm, tn, tk


def _matmul(a, b, *, ta=False, tb=False, outs=(F32,), epilogue=None, extras=(), consts=(), sums=(), whole_rows=False,
            tm=1024, tn=1024, name):
    m, k = (a.shape[1], a.shape[0]) if ta else a.shape
    n = b.shape[0] if tb else b.shape[1]
    assert (b.shape[1] if tb else b.shape[0]) == k
    outs = [o if isinstance(o, tuple) else (o, None) for o in outs]
    out_bytes = sum(jnp.dtype(dt).itemsize for dt, w in outs if w is None) + sum(e.dtype.itemsize for e in extras)
    tm, tn, tk = _matmul_tiles(m, n, k, tm, tn, out_bytes)
    assert m % tm == 0 and n % tn == 0 and k % tk == 0, (name, m, n, k)
    nk = k // tk
    ne, nc, no = len(extras), len(consts), len(outs)
    assert tn == n or not (sums or whole_rows or any(w is not None for _, w in outs)), name
    dims = (((0 if ta else 1,), (1 if tb else 0,)), ((), ()))

    def body(*refs):
        a_ref, b_ref = refs[:2]
        e_refs, o_refs = refs[2:2 + ne + nc], refs[2 + ne + nc:2 + ne + nc + no]
        s_refs = refs[2 + ne + nc + no:2 + ne + nc + no + len(sums)]

        def finish(acc):
            vals = (acc,) if epilogue is None else epilogue(acc, *[e[...] for e in e_refs])
            for ref, val in zip(o_refs, vals[:no]):
                ref[...] = val.astype(ref.dtype)
            for ref, val in zip(s_refs, vals[no:]):
                @pl.when(pl.program_id(0) == 0)
                def _(ref=ref, val=val):
                    ref[...] = val

                @pl.when(pl.program_id(0) > 0)
                def _(ref=ref, val=val):
                    ref[...] += val

        prod = lax.dot_general(a_ref[...], b_ref[...], dims, preferred_element_type=F32)
        if nk == 1:
            finish(prod)
        else:
            acc_ref = refs[-1]
            step = pl.program_id(2)

            @pl.when(step == 0)
            def _():
                acc_ref[...] = prod

            @pl.when(step > 0)
            def _():
                acc_ref[...] += prod

            @pl.when(step == nk - 1)
            def _():
                finish(acc_ref[...])

    a_spec = pl.BlockSpec((tk, tm), lambda i, j, s: (s, i)) if ta else pl.BlockSpec((tm, tk), lambda i, j, s: (i, s))
    b_spec = pl.BlockSpec((tn, tk), lambda i, j, s: (j, s)) if tb else pl.BlockSpec((tk, tn), lambda i, j, s: (s, j))
    tile_spec = pl.BlockSpec((tm, tn), lambda i, j, s: (i, j))
    whole = lambda shape: pl.BlockSpec(shape, lambda i, j, s: (0, 0))
    return pl.pallas_call(
        body, grid=(m // tm, n // tn, nk),
        in_specs=[a_spec, b_spec] + [tile_spec] * ne + [whole(c.shape) for c in consts],
        out_specs=[tile_spec if w is None else pl.BlockSpec((tm, w), lambda i, j, s: (i, 0)) for _, w in outs]
        + [whole(shape) for shape in sums],
        out_shape=[jax.ShapeDtypeStruct((m, n if w is None else w), dt) for dt, w in outs]
        + [jax.ShapeDtypeStruct(shape, F32) for shape in sums],
        scratch_shapes=[pltpu.VMEM((tm, tn), F32)] if nk > 1 else [],
        name=name, compiler_params=_params("arbitrary" if sums else "parallel", "parallel", "arbitrary"),
    )(a, b, *extras, *consts)


def _split(x, parts):
    out = []
    for _ in range(parts):
        out.append(x.astype(BF16))
        x = x - out[-1].astype(F32)
    return out


def _dot_exact(x, e, parts=3):
    return sum(jnp.dot(p, e, preferred_element_type=F32) for p in _split(x, parts))


def _head_expand(heads, width):
    dh = width // heads
    rows = jnp.arange(LANES)[:, None]
    cols = jnp.arange(width)[None, :]
    return (cols // dh == rows).astype(BF16)


def _rstd(x):
    return lax.rsqrt(jnp.mean(x * x, axis=-1, keepdims=True) + EPS)


def _rms_fwd(x, gain, *, name):
    return _rowwise(lambda x, g: x * _rstd(x) * g, [x], [gain], [(x.shape[1], BF16)], name=name)[0]


def _rms_bwd_vals(x, dhn, gain):
    r = _rstd(x)
    xh = x * r
    dxh = dhn * gain
    dx = r * (dxh - xh * jnp.mean(dxh * xh, axis=-1, keepdims=True))
    return dx, jnp.sum(dhn * xh, axis=0, keepdims=True)


def _norm_input_grad(dz, w, x, dres, gain, *, tb=False, name):
    def epilogue(dhn, x, dres, g):
        dx, dg = _rms_bwd_vals(x, dhn, g)
        dh = dres + dx
        return dh, dh, dg
    return _matmul(dz, w, tb=tb, outs=(F32, BF16), extras=(x, dres), consts=(gain,), sums=((1, x.shape[1]),), epilogue=epilogue,
                   name=name)


def _loss_and_grad(h, target, gain, *, name):
    def fn(h, tgt, g):
        r = _rstd(h)
        err = h * r * g - tgt
        loss = 0.5 * jnp.sum(jnp.mean(err * err, axis=-1, keepdims=True), axis=0, keepdims=True)
        dx, dg = _rms_bwd_vals(h, err * (1.0 / D_MODEL), g)
        return dx, dx, jnp.broadcast_to(loss, (1, LANES)), dg
    d = h.shape[1]
    return _rowwise(fn, [h, target], [gain], [(d, F32), (d, BF16)], [(1, LANES), (1, d)], name=name)


def _adamw(w, g, m, v, *, name):
    rows, cols = w.shape
    tile = rows
    for cand in (512, 256, 128, 64, 32, 16, 8):
        if rows % cand == 0 and rows > cand:
            tile = cand
            break
    bc1 = 1.0 - ADAM_B1 ** ADAM_STEP
    bc2 = 1.0 - ADAM_B2 ** ADAM_STEP

    def fn(w, g, m, v):
        m2 = ADAM_B1 * m + (1.0 - ADAM_B1) * g
        v2 = ADAM_B2 * v + (1.0 - ADAM_B2) * (g * g)
        delta = -ADAM_LR * ((m2 / bc1) / (jnp.sqrt(v2 / bc2) + ADAM_EPS) + ADAM_WD * w)
        return delta, m2, v2
    return _rowwise(fn, [w, g, m, v], [], [(cols, F32)] * 3, name=name, tile=tile)


def _attn_specs(arr, width, cb, classes, rows_block, whole, together=1):
    ncols = arr.shape[2] // (classes * width)
    lanes, at = (width, lambda r: r * ncols + cb) if together == 1 else (together * ncols * width, lambda r: r)
    if whole:
        return pl.BlockSpec((1, arr.shape[1], lanes), lambda n, r, i: (n, 0, at(r)))
    return pl.BlockSpec((1, rows_block, lanes), lambda n, r, i: (n, i, at(r)))


def _class_window(refs, spans, together, cl):
    if together == 1:
        return refs
    return [ref.at[:, :, pl.ds((cl * ncols + cb) * width, width)] for ref, (ncols, cb, width) in zip(refs, spans)]


def _attn_tiles(mode, lq, lk, backward=False):
    if mode == "full":
        rows = min(lq, FULL_ROWS)
        return rows, rows, lk
    if mode == "band":
        tq = min(BAND_ROWS if backward else 2 * BAND_ROWS, lq)
        rows = tq if backward else ATT_BLOCK
        return tq, rows, min(rows + ATT_BLOCK, lk)
    rows = CAUSAL_ROWS_BACKWARD if backward else CAUSAL_ROWS
    return rows, rows, ATT_CHUNK


def _window(mode, row0, j, rows, win, lk):
    if mode == "causal":
        return pl.multiple_of(j * win, win), row0 // win + 1
    if mode == "band":
        return pl.multiple_of(jnp.clip(row0 + rows - win, 0, lk - win), ATT_BLOCK), 1
    return 0, 1


def _allowed(mode, row0, start, rows, win, max_dist):
    if mode == "full":
        return None
    dist = (row0 + lax.broadcasted_iota(jnp.int32, (rows, win), 0)) - (start + lax.broadcasted_iota(jnp.int32, (rows, win), 1))
    ok = dist >= 0
    if max_dist is not None:
        ok = ok & (dist <= max_dist)
    return ok


def _head_groups(heads, dh):
    gw = max(LANES, dh)
    return gw, gw // dh, heads // (gw // dh)


def _own_lanes(rows, gw, dh, u):
    return lax.broadcasted_iota(jnp.int32, (rows, gw), 1) // dh == u


def _only_head(x, own, per, u):
    return x if per == 1 else jnp.where(own[u], x, jnp.zeros_like(x))


def _step_scores(qz, kw, kb_row, ok):
    s = lax.dot_general(qz, kw, (((1,), (1,)), ((), ())), preferred_element_type=F32)
    if kb_row is not None:
        s = s - kb_row
    if ok is not None:
        s = jnp.where(ok, s, MASKED)
    return s


def _attn_fwd(q, k, v, kb=None, *, classes=1, heads, dh, mode, max_dist=None, bf16_copy=False, name):
    (qa, qc), (ka, kc), (va, vc) = q, k, v
    n, lq, lk = qa.shape[0], qa.shape[1], ka.shape[1]
    width = heads * dh
    tq, rows, win = _attn_tiles(mode, lq, lk)
    gw, per, groups = _head_groups(heads, dh)
    scale = dh ** -0.5
    has_kb = kb is not None
    single = mode != "causal"
    together = min(classes, max(1, CLASS_ROWS // lq))
    ncols = lambda arr: arr.shape[2] // (classes * width)
    spans = [(ncols(qa), qc, width), (ncols(ka), kc, width), (ncols(va), vc, width), (1, 0, width), (1, 0, LANES), (1, 0, width)]

    def body(*refs):
        for cl in range(together):
            for sub in range(tq // rows):
                part(_class_window(refs, spans, together, cl), pl.program_id(2) * tq + sub * rows, slice(sub * rows, (sub + 1) * rows))

    def part(refs, row0, rs):
        q_ref, k_ref, v_ref = refs[:3]
        kb_ref = refs[3] if has_kb else None
        n_in = 4 if has_kb else 3
        o_ref, lse_ref = refs[n_in:n_in + 2]
        o16_ref = refs[n_in + 2] if bf16_copy else None
        lane = lax.broadcasted_iota(jnp.int32, (rows, LANES), 1)
        own = [_own_lanes(rows, gw, dh, u) for u in range(per)]

        def step(j, carry, masked=True):
            m_in, l_in = carry
            m_out, l_out = m_in, l_in
            start, _ = _window(mode, row0, j, rows, win, lk)
            ok = _allowed(mode, row0, start, rows, win, max_dist) if masked else None
            for g in range(groups):
                cols = slice(g * gw, (g + 1) * gw)
                qg = q_ref[0, rs, cols] * scale
                kw = k_ref[0, pl.ds(start, win), cols]
                vw = v_ref[0, pl.ds(start, win), cols]
                alpha_g = new_g = None
                for u in range(per):
                    h = g * per + u
                    kb_row = kb_ref[0, h, pl.ds(j, 1), :] if has_kb else None
                    s = _step_scores(_only_head(qg, own, per, u), kw, kb_row, ok)
                    m_new = jnp.max(s, axis=1, keepdims=True)
                    if not single:
                        m_old = m_in[:, h:h + 1]
                        m_new = jnp.maximum(m_old, m_new)
                        alpha = jnp.exp(m_old - m_new)
                        alpha_g = alpha if u == 0 else jnp.where(own[u], alpha, alpha_g)
                    p = jnp.exp(s - m_new)
                    l_new = jnp.sum(p, axis=1, keepdims=True)
                    r = jnp.dot(p.astype(BF16), vw, preferred_element_type=F32)
                    if single:
                        r = r * (1.0 / l_new)
                    else:
                        l_new = alpha * l_in[:, h:h + 1] + l_new
                    new_g = r if u == 0 else jnp.where(own[u], r, new_g)
                    m_out = jnp.where(lane == h, m_new, m_out)
                    l_out = jnp.where(lane == h, l_new, l_out)
                o_ref[0, rs, cols] = new_g if single else alpha_g * o_ref[0, rs, cols] + new_g
                if bf16_copy:
                    o16_ref[0, rs, cols] = new_g.astype(BF16)
            return m_out, l_out

        carry = (jnp.full((rows, LANES), MASKED, F32), jnp.zeros((rows, LANES), F32))
        if single:
            m_all, l_all = step(0, carry)
            l_all = jnp.where(lane < heads, l_all, 1.0)
        else:
            o_ref[...] = jnp.zeros(o_ref.shape, F32)
            last = _window(mode, row0, 0, rows, win, lk)[1] - 1
            m_all, l_all = step(last, lax.fori_loop(0, last, functools.partial(step, masked=False), carry))
            l_all = jnp.where(lane < heads, l_all, 1.0)
            inv = 1.0 / l_all
            for g in range(groups):
                cols = slice(g * gw, (g + 1) * gw)
                inv_g = inv[:, g * per:g * per + 1]
                for u in range(1, per):
                    inv_g = jnp.where(own[u], inv[:, g * per + u:g * per + u + 1], inv_g)
                o_ref[0, rs, cols] = o_ref[0, rs, cols] * inv_g
        lse_ref[0, rs, :] = jnp.where(lane < heads, m_all + jnp.log(l_all), 0.0)

    in_specs = [_attn_specs(qa, width, qc, classes, tq, False, together), _attn_specs(ka, width, kc, classes, win, True, together),
                _attn_specs(va, width, vc, classes, win, True, together)]
    args = [qa, ka, va]
    if has_kb:
        assert together == 1
        in_specs.append(pl.BlockSpec((1,) + kb.shape[1:], lambda n_, r, i: (n_, 0, 0, 0)))
        args.append(kb)
    out_shape = [jax.ShapeDtypeStruct((n, lq, classes * width), F32), jax.ShapeDtypeStruct((n, lq, classes * LANES), F32)]
    out_specs = [pl.BlockSpec((1, tq, together * width), lambda n_, r, i: (n_, i, r)),
                 pl.BlockSpec((1, tq, together * LANES), lambda n_, r, i: (n_, i, r))]
    if bf16_copy:
        assert single
        out_shape.append(jax.ShapeDtypeStruct((n, lq, classes * width), BF16))
        out_specs.append(out_specs[0])
    return pl.pallas_call(
        body, grid=(n, classes // together, lq // tq), in_specs=in_specs, out_specs=out_specs, out_shape=out_shape, name=name,
        compiler_params=_params("parallel", "parallel", "arbitrary"),
    )(*args)


def _attn_bwd(q, k, v, do, lse, delta, kb=None, *, classes=1, heads, dh, mode, max_dist=None, dq_dtype=F32, name):
    (qa, qc), (ka, kc), (va, vc), (da, dc) = q, k, v, do
    n, lq, lk = qa.shape[0], qa.shape[1], ka.shape[1]
    width = heads * dh
    tq, rows, win = _attn_tiles(mode, lq, lk, backward=True)
    gw, per, groups = _head_groups(heads, dh)
    scale = dh ** -0.5
    has_kb = kb is not None
    single = mode != "causal"
    nt = (((1,), (1,)), ((), ()))
    tn = (((0,), (0,)), ((), ()))
    together = min(classes, max(1, CLASS_ROWS // lq))
    ncols = lambda arr: arr.shape[2] // (classes * width)
    spans = [(ncols(qa), qc, width), (ncols(ka), kc, width), (ncols(va), vc, width), (ncols(da), dc, width), (1, 0, LANES),
             (1, 0, LANES), (1, 0, width), (1, 0, width), (1, 0, width)]

    def body(*refs):
        n_in = 7 if has_kb else 6
        dk_ref, dv_ref = refs[n_in + 1:n_in + 3]

        @pl.when(pl.program_id(2) == 0)
        def _():
            dk_ref[...] = jnp.zeros(dk_ref.shape, F32)
            dv_ref[...] = jnp.zeros(dv_ref.shape, F32)
            if has_kb:
                refs[n_in + 3][...] = jnp.zeros(refs[n_in + 3].shape, F32)

        for cl in range(together):
            for sub in range(tq // rows):
                part(_class_window(refs, spans, together, cl), pl.program_id(2) * tq + sub * rows, slice(sub * rows, (sub + 1) * rows))

    def part(refs, row0, rs):
        q_ref, k_ref, v_ref, do_ref, lse_ref, dl_ref = refs[:6]
        kb_ref = refs[6] if has_kb else None
        n_in = 7 if has_kb else 6
        dq_ref, dk_ref, dv_ref = refs[n_in:n_in + 3]
        dkb_ref, drow_ref = refs[n_in + 3:n_in + 5] if has_kb else (None, None)
        lse_all = lse_ref[0, rs, :]
        dl_all = dl_ref[0, rs, :]
        lane = lax.broadcasted_iota(jnp.int32, (rows, LANES), 1)
        own = [_own_lanes(rows, gw, dh, u) for u in range(per)]

        def step(j, drow_all, masked=True):
            start, _ = _window(mode, row0, j, rows, win, lk)
            ok = _allowed(mode, row0, start, rows, win, max_dist) if masked else None
            for g in range(groups):
                cols = slice(g * gw, (g + 1) * gw)
                qg = q_ref[0, rs, cols] * scale
                dog = do_ref[0, rs, cols]
                kw = k_ref[0, pl.ds(start, win), cols]
                vw = v_ref[0, pl.ds(start, win), cols]
                dq_g = dk_w = dv_w = None
                for u in range(per):
                    h = g * per + u
                    qz, doz = _only_head(qg, own, per, u), _only_head(dog, own, per, u)
                    kb_row = kb_ref[0, h, pl.ds(j, 1), :] if has_kb else None
                    p = jnp.exp(_step_scores(qz, kw, kb_row, ok) - lse_all[:, h:h + 1])
                    dp = lax.dot_general(doz, vw, nt, preferred_element_type=F32)
                    ds = p * (dp - dl_all[:, h:h + 1])
                    dsb = ds.astype(BF16)
                    r = jnp.dot(dsb, kw, preferred_element_type=F32)
                    dq_g = r if u == 0 else jnp.where(own[u], r, dq_g)
                    dk_u = lax.dot_general(dsb, qz, tn, preferred_element_type=F32)
                    dv_u = lax.dot_general(p.astype(BF16), doz, tn, preferred_element_type=F32)
                    dk_w = dk_u if u == 0 else dk_w + dk_u
                    dv_w = dv_u if u == 0 else dv_w + dv_u
                    if has_kb:
                        dkb_ref[0, h, pl.ds(j, 1), :] += -jnp.sum(ds, axis=0, keepdims=True)
                        drow_all = drow_all + jnp.where(lane == h, jnp.sum(ds, axis=1, keepdims=True), 0.0)
                dk_ref[0, pl.ds(start, win), cols] += dk_w
                dv_ref[0, pl.ds(start, win), cols] += dv_w
                if single:
                    dq_ref[0, rs, cols] = (dq_g * scale).astype(dq_ref.dtype)
                else:
                    dq_ref[0, rs, cols] += dq_g * scale
            return drow_all

        drow_all = jnp.zeros((rows, LANES), F32)
        if single:
            drow_all = step(0, drow_all)
        else:
            dq_ref[...] = jnp.zeros(dq_ref.shape, F32)
            last = _window(mode, row0, 0, rows, win, lk)[1] - 1
            drow_all = step(last, lax.fori_loop(0, last, functools.partial(step, masked=False), drow_all))
        if has_kb:
            drow_ref[0, rs, :] = drow_all

    row_spec = functools.partial(_attn_specs, classes=classes, rows_block=tq, whole=False, together=together)
    in_specs = [row_spec(qa, width, qc), _attn_specs(ka, width, kc, classes, win, True, together),
                _attn_specs(va, width, vc, classes, win, True, together), row_spec(da, width, dc), row_spec(lse, LANES, 0),
                row_spec(delta, LANES, 0)]
    args = [qa, ka, va, da, lse, delta]
    assert single or dq_dtype == F32
    out_shape = [jax.ShapeDtypeStruct((n, lq, classes * width), dq_dtype), jax.ShapeDtypeStruct((n, lk, classes * width), F32),
                 jax.ShapeDtypeStruct((n, lk, classes * width), F32)]
    kv_spec = pl.BlockSpec((1, lk, together * width), lambda n_, r, i: (n_, 0, r))
    out_specs = [pl.BlockSpec((1, tq, together * width), lambda n_, r, i: (n_, i, r)), kv_spec, kv_spec]
    if has_kb:
        assert together == 1
        kb_spec = pl.BlockSpec((1,) + kb.shape[1:], lambda n_, r, i: (n_, 0, 0, 0))
        in_specs.append(kb_spec)
        args.append(kb)
        out_shape += [jax.ShapeDtypeStruct(kb.shape, F32), jax.ShapeDtypeStruct((n, lq, LANES), F32)]
        out_specs += [kb_spec, pl.BlockSpec((1, tq, LANES), lambda n_, r, i: (n_, i, 0))]
    return pl.pallas_call(
        body, grid=(n, classes // together, lq // tq), in_specs=in_specs, out_specs=out_specs, out_shape=out_shape, name=name,
        compiler_params=_params("parallel", "parallel", "arbitrary"),
    )(*args)


def _rope_tables():
    half = HEAD_DIM // 2
    inv = ROPE_THETA ** (-jnp.arange(half, dtype=F32) / half)
    ang = jnp.arange(SEQ, dtype=F32)[:, None] * inv[None, :]
    cos = jnp.tile(jnp.cos(ang), (1, 2 * ATT_W // HEAD_DIM))
    sin = jnp.tile(jnp.concatenate([-jnp.sin(ang), jnp.sin(ang)], axis=1), (1, ATT_W // HEAD_DIM))
    return cos, sin


def _rotate(x, cos, sin):
    width = x.shape[1]
    lane = lax.broadcasted_iota(jnp.int32, x.shape, 1)
    first_half = (lane % HEAD_DIM) < (HEAD_DIM // 2)
    swapped = jnp.where(first_half, pltpu.roll(x, width - HEAD_DIM // 2, axis=1), pltpu.roll(x, HEAD_DIM // 2, axis=1))
    return x * cos[:, :width] + swapped * sin[:, :width]


def _gelu(x):
    k = math.sqrt(2.0 / math.pi)
    t = jnp.tanh(k * (x + 0.044715 * x * x * x))
    return 0.5 * x * (1.0 + t), t


def _gelu_grad(x, t):
    k = math.sqrt(2.0 / math.pi)
    return 0.5 * (1.0 + t) + 0.5 * x * (1.0 - t * t) * k * (1.0 + 3.0 * 0.044715 * x * x)


def _shift_down(x, halo, s):
    ext = jnp.concatenate([halo, x], axis=0)
    return pltpu.roll(ext, s, axis=0)[8:, :]


def _shift_up(x, halo, s):
    rows = x.shape[0]
    ext = jnp.concatenate([x, halo], axis=0)
    return pltpu.roll(ext, rows + 8 - s, axis=0)[:rows, :]


def _lru_gates(u, halo, cw, cb, wa, ba, wi, bi, lam):
    taps = [_shift_down(u, halo, CONV_WIDTH - 1 - k) for k in range(CONV_WIDTH - 1)] + [u]
    uc = cb + sum(cw[k:k + 1, :] * taps[k] for k in range(CONV_WIDTH))
    ucb = uc.astype(BF16)
    r = jax.nn.sigmoid(jnp.dot(ucb, wa, preferred_element_type=F32) + ba)
    gi = jax.nn.sigmoid(jnp.dot(ucb, wi, preferred_element_type=F32) + bi)
    sp = jnp.maximum(-lam, 0.0) + jnp.log(1.0 + jnp.exp(-jnp.abs(lam)))
    log_a = -LRU_C * r * sp
    a = jnp.exp(log_a)
    x2 = 2.0 * log_a
    one_minus_a2 = jnp.where(x2 > -0.01, -x2 * (1.0 + x2 * (0.5 + x2 * (1.0 / 6.0 + x2 / 24.0))), 1.0 - jnp.exp(x2))
    mult = jnp.sqrt(one_minus_a2)
    return taps, uc, ucb, r, gi, sp, a, mult


def _lru_specs(nchunk, reverse):
    def chunk(b, c):
        return b * nchunk + ((nchunk - 1 - c) if reverse else c)

    def halo_before(b, c):
        return jnp.maximum(chunk(b, c) * (LRU_CHUNK // 8) - 1, 0)

    return chunk, halo_before


def _lru_fwd(zug, cw, cb, wa, ba, wi, bi, lam, *, name):
    total = zug.shape[0]
    nchunk = SEQ // LRU_CHUNK
    w = LRU_WIDTH
    chunk, halo_before = _lru_specs(nchunk, False)

    def body(u_ref, uh_ref, g_ref, cw_ref, cb_ref, wa_ref, ba_ref, wi_ref, bi_ref, lam_ref, y_ref, h_ref, a_sc, x_sc, carry_sc):
        c = pl.program_id(1)
        u = u_ref[...]
        halo = jnp.where(c > 0, uh_ref[...], 0.0)
        _, uc, _, _, gi, _, a, mult = _lru_gates(u, halo, cw_ref[...], cb_ref[...], wa_ref[...], ba_ref[...],
                                                 wi_ref[...], bi_ref[...], lam_ref[...])
        a_sc[...] = a
        x_sc[...] = mult * (gi * uc)

        @pl.when(c == 0)
        def _():
            carry_sc[...] = jnp.zeros(carry_sc.shape, F32)

        def tile_step(t, h):
            r0 = pl.multiple_of(t * 8, 8)
            at = a_sc[pl.ds(r0, 8), :]
            xt = x_sc[pl.ds(r0, 8), :]
            rows = []
            for j in range(8):
                h = at[j:j + 1, :] * h + xt[j:j + 1, :]
                rows.append(h)
            h_ref[pl.ds(r0, 8), :] = jnp.concatenate(rows, axis=0)
            return h

        h_last = lax.fori_loop(0, LRU_CHUNK // 8, tile_step, carry_sc[0:1, :])
        carry_sc[0:1, :] = h_last
        gel, _ = _gelu(g_ref[...])
        y_ref[...] = (h_ref[...] * gel).astype(BF16)

    small = lambda arr: pl.BlockSpec(arr.shape, lambda b, c: (0, 0))
    return pl.pallas_call(
        body, grid=(total // SEQ, nchunk),
        in_specs=[pl.BlockSpec((LRU_CHUNK, w), lambda b, c: (chunk(b, c), 0)),
                  pl.BlockSpec((8, w), lambda b, c: (halo_before(b, c), 0)),
                  pl.BlockSpec((LRU_CHUNK, w), lambda b, c: (chunk(b, c), 1)),
                  small(cw), small(cb), small(wa), small(ba), small(wi), small(bi), small(lam)],
        out_specs=[pl.BlockSpec((LRU_CHUNK, w), lambda b, c: (chunk(b, c), 0))] * 2,
        out_shape=[jax.ShapeDtypeStruct((total, w), BF16), jax.ShapeDtypeStruct((total, w), F32)],
        scratch_shapes=[pltpu.VMEM((LRU_CHUNK, w), F32), pltpu.VMEM((LRU_CHUNK, w), F32), pltpu.VMEM((8, w), F32)],
        name=name, compiler_params=_params("arbitrary", "arbitrary"),
    )(zug, zug, zug, cw, cb, wa, ba, wi, bi, lam)


def _lru_bwd(zug, hl, dy, cw, cb, wa, ba, wi, bi, lam, *, name):
    total = zug.shape[0]
    nchunk = SEQ // LRU_CHUNK
    w = LRU_WIDTH
    chunk, halo_before = _lru_specs(nchunk, True)
    tn = (((0,), (0,)), ((), ()))
    nt = (((1,), (1,)), ((), ()))

    def body(u_ref, uh_ref, g_ref, hl_ref, hh_ref, dy_ref, cw_ref, cb_ref, wa_ref, ba_ref, wi_ref, bi_ref, lam_ref,
             dz_ref, dcw_ref, dcb_ref, dwa_ref, dba_ref, dwi_ref, dbi_ref, dlam_ref,
             a_sc, d_sc, g_sc, gcarry_sc, acarry_sc, duc_sc):
        b, c = pl.program_id(0), pl.program_id(1)
        first_chunk = c == nchunk - 1

        @pl.when((b == 0) & (c == 0))
        def _():
            for ref in (dcw_ref, dcb_ref, dwa_ref, dba_ref, dwi_ref, dbi_ref, dlam_ref):
                ref[...] = jnp.zeros(ref.shape, F32)

        @pl.when(c == 0)
        def _():
            gcarry_sc[...] = jnp.zeros(gcarry_sc.shape, F32)
            acarry_sc[...] = jnp.zeros(acarry_sc.shape, F32)
            duc_sc[...] = jnp.zeros(duc_sc.shape, F32)

        u = u_ref[...]
        halo = jnp.where(first_chunk, 0.0, uh_ref[...])
        cw, wa, wi, lam = cw_ref[...], wa_ref[...], wi_ref[...], lam_ref[...]
        taps, uc, ucb, r, gi, sp, a, mult = _lru_gates(u, halo, cw, cb_ref[...], wa, ba_ref[...], wi, bi_ref[...], lam)
        gate = g_ref[...]
        gel, th = _gelu(gate)
        dy = dy_ref[...]
        hl = hl_ref[...]
        a_sc[...] = a
        d_sc[...] = dy * gel
        dgate = dy * hl * _gelu_grad(gate, th)

        def tile_step(t, carry):
            g_next, a_next = carry
            r0 = pl.multiple_of((LRU_CHUNK // 8 - 1 - t) * 8, 8)
            dt = d_sc[pl.ds(r0, 8), :]
            at = a_sc[pl.ds(r0, 8), :]
            rows = [None] * 8
            for j in reversed(range(8)):
                g_next = dt[j:j + 1, :] + a_next * g_next
                a_next = at[j:j + 1, :]
                rows[j] = g_next
            g_sc[pl.ds(r0, 8), :] = jnp.concatenate(rows, axis=0)
            return g_next, a_next

        g_last, a_last = lax.fori_loop(0, LRU_CHUNK // 8, tile_step, (gcarry_sc[0:1, :], acarry_sc[0:1, :]))
        gcarry_sc[0:1, :] = g_last
        acarry_sc[0:1, :] = a_last

        gs = g_sc[...]
        hl_halo = jnp.where(first_chunk, 0.0, hh_ref[...])
        da = gs * _shift_down(hl, hl_halo, 1)
        dmult = gs * gi * uc
        dgi = gs * mult * uc
        duc = gs * mult * gi
        dlog_a = da * a - dmult * a * a / mult
        dr = dlog_a * (-LRU_C * sp)
        dsp = jnp.sum(dlog_a * (-LRU_C * r), axis=0, keepdims=True)
        dlam_ref[...] += -dsp * jax.nn.sigmoid(-lam)
        dpa = dr * r * (1.0 - r)
        dpi = dgi * gi * (1.0 - gi)
        dpab, dpib = dpa.astype(BF16), dpi.astype(BF16)
        dba_ref[...] += jnp.sum(dpa, axis=0, keepdims=True)
        dbi_ref[...] += jnp.sum(dpi, axis=0, keepdims=True)
        dwa_ref[...] += lax.dot_general(ucb, dpab, tn, preferred_element_type=F32)
        dwi_ref[...] += lax.dot_general(ucb, dpib, tn, preferred_element_type=F32)
        duc = duc + lax.dot_general(dpab, wa, nt, preferred_element_type=F32)
        duc = duc + lax.dot_general(dpib, wi, nt, preferred_element_type=F32)
        dcb_ref[...] += jnp.sum(duc, axis=0, keepdims=True)
        dcw_ref[...] += jnp.concatenate([jnp.sum(duc * taps[k], axis=0, keepdims=True) for k in range(CONV_WIDTH)], axis=0)
        after = duc_sc[...]
        du = cw[CONV_WIDTH - 1:CONV_WIDTH, :] * duc
        for k in range(CONV_WIDTH - 1):
            du = du + cw[k:k + 1, :] * _shift_up(duc, after, CONV_WIDTH - 1 - k)
        duc_sc[...] = duc[0:8, :]
        dz_ref[:, 0:w] = du.astype(BF16)
        dz_ref[:, w:2 * w] = dgate.astype(BF16)

    small = lambda arr: pl.BlockSpec(arr.shape, lambda b, c: (0, 0))
    acc = lambda shape: pl.BlockSpec(shape, lambda b, c: (0, 0))
    chunk_spec = lambda cb_: pl.BlockSpec((LRU_CHUNK, w), lambda b, c: (chunk(b, c), cb_))
    halo_spec = pl.BlockSpec((8, w), lambda b, c: (halo_before(b, c), 0))
    acc_shapes = [(CONV_WIDTH, w), (1, w), (w, w), (1, w), (w, w), (1, w), (1, w)]
    return pl.pallas_call(
        body, grid=(total // SEQ, nchunk),
        in_specs=[chunk_spec(0), halo_spec, chunk_spec(1), chunk_spec(0), halo_spec, chunk_spec(0),
                  small(cw), small(cb), small(wa), small(ba), small(wi), small(bi), small(lam)],
        out_specs=[pl.BlockSpec((LRU_CHUNK, 2 * w), lambda b, c: (chunk(b, c), 0))] + [acc(s) for s in acc_shapes],
        out_shape=[jax.ShapeDtypeStruct((total, 2 * w), BF16)] + [jax.ShapeDtypeStruct(s, F32) for s in acc_shapes],
        scratch_shapes=[pltpu.VMEM((LRU_CHUNK, w), F32)] * 3 + [pltpu.VMEM((8, w), F32)] * 3,
        name=name, compiler_params=_params("arbitrary", "arbitrary"),
    )(zug, zug, zug, hl, hl, dy, cw, cb, wa, ba, wi, bi, lam)


def _block_diag(wb):
    eye = jnp.eye(LRU_BLOCKS, dtype=wb.dtype)
    return jnp.einsum("gcd,gh->gchd", wb, eye).reshape(LRU_WIDTH, LRU_WIDTH).astype(BF16)


def _diag_blocks(wd):
    blk = LRU_WIDTH // LRU_BLOCKS
    return jnp.stack([wd[g * blk:(g + 1) * blk, g * blk:(g + 1) * blk] for g in range(LRU_BLOCKS)])


FOX_SCAN = 256


def _fox_bias(fl, bf, *, name):
    nb = fl.shape[0]

    def body(fl_ref, bf_ref, c_ref):
        x = fl_ref[0] + bf_ref[...]
        logf = jnp.minimum(x, 0.0) - jnp.log(1.0 + jnp.exp(-jnp.abs(x)))
        upper = (lax.broadcasted_iota(jnp.int32, (FOX_SCAN, FOX_SCAN), 0)
                 <= lax.broadcasted_iota(jnp.int32, (FOX_SCAN, FOX_SCAN), 1)).astype(BF16)
        carry = jnp.zeros((LRU_BLOCKS, 1), F32)
        for j in range(SEQ // FOX_SCAN):
            blk = logf[:, j * FOX_SCAN:(j + 1) * FOX_SCAN]
            c_ref[0, :, j * FOX_SCAN:(j + 1) * FOX_SCAN] = _dot_exact(blk, upper) + carry
            carry = carry + jnp.sum(blk, axis=1, keepdims=True)

    return pl.pallas_call(
        body, grid=(nb,),
        in_specs=[pl.BlockSpec((1, 8, SEQ), lambda b: (b, 0, 0)), pl.BlockSpec((8, 1), lambda b: (0, 0))],
        out_specs=pl.BlockSpec((1, 8, SEQ), lambda b: (b, 0, 0)),
        out_shape=jax.ShapeDtypeStruct((nb, 8, SEQ), F32), name=name, compiler_params=_params("arbitrary"),
    )(fl, bf)


def _fox_bias_bwd(fl, bf, dc, *, name):
    nb = fl.shape[0]

    def body(fl_ref, bf_ref, dc_ref, dfl_ref, dbf_ref):
        @pl.when(pl.program_id(0) == 0)
        def _():
            dbf_ref[...] = jnp.zeros(dbf_ref.shape, F32)

        x = fl_ref[0] + bf_ref[...]
        dc_all = dc_ref[0]
        lower = (lax.broadcasted_iota(jnp.int32, (FOX_SCAN, FOX_SCAN), 0)
                 >= lax.broadcasted_iota(jnp.int32, (FOX_SCAN, FOX_SCAN), 1)).astype(BF16)
        carry = jnp.zeros((LRU_BLOCKS, 1), F32)
        total = jnp.zeros((LRU_BLOCKS, 1), F32)
        for j in reversed(range(SEQ // FOX_SCAN)):
            cols = slice(j * FOX_SCAN, (j + 1) * FOX_SCAN)
            blk = dc_all[:, cols]
            dlogf = _dot_exact(blk, lower) + carry
            carry = carry + jnp.sum(blk, axis=1, keepdims=True)
            dx = dlogf * jax.nn.sigmoid(-x[:, cols])
            dfl_ref[0, :, cols] = dx
            total = total + jnp.sum(dx, axis=1, keepdims=True)
        dbf_ref[...] += total

    return pl.pallas_call(
        body, grid=(nb,),
        in_specs=[pl.BlockSpec((1, 8, SEQ), lambda b: (b, 0, 0)), pl.BlockSpec((8, 1), lambda b: (0, 0)),
                  pl.BlockSpec((1, 8, SEQ), lambda b: (b, 0, 0))],
        out_specs=[pl.BlockSpec((1, 8, SEQ), lambda b: (b, 0, 0)), pl.BlockSpec((8, 1), lambda b: (0, 0))],
        out_shape=[jax.ShapeDtypeStruct((nb, 8, SEQ), F32), jax.ShapeDtypeStruct((8, 1), F32)],
        name=name, compiler_params=_params("arbitrary"),
    )(fl, bf, dc)


def _seq3(a, nb):
    return a.reshape(nb, a.shape[0] // nb, a.shape[1])


def _flat2(a):
    return a.reshape(a.shape[0] * a.shape[1], a.shape[2])


def _residual_out(y, w, h, next_gain, *, name):
    if next_gain is None:
        out, = _matmul(y, w, extras=(h,), epilogue=lambda acc, res: (acc + res,), name=name)
        return out, None

    def epilogue(acc, res, g):
        out = acc + res
        return out, out * _rstd(out) * g
    return _matmul(y, w, outs=(F32, BF16), extras=(h,), consts=(next_gain,), epilogue=epilogue, whole_rows=True, name=name)


def _mlp_fwd(h, hn, p, tag, next_gain):
    act, = _matmul(hn, p["w_up_t"], tb=True, outs=(BF16,), epilogue=lambda acc: (jnp.square(jnp.maximum(acc, 0.0)),),
                   name=f"{tag}_up")
    out, hn_next = _residual_out(act, p["w_down"], h, next_gain, name=f"{tag}_down")
    return out, hn_next, (h, hn, act)


def _mlp_bwd(dh, dhb, p, saved, tag):
    h, hn, act = saved
    dup, = _matmul(dhb, p["w_down"], tb=True, outs=(BF16,), extras=(act,),
                   epilogue=lambda acc, act: (acc * (2.0 * jnp.sqrt(act).astype(F32)),), name=f"{tag}_bwd_dup")
    dw_down, = _matmul(act, dhb, ta=True, outs=(BF16,),name=f"{tag}_bwd_dwdown")
    dw_up_t, = _matmul(dup, hn, ta=True, outs=(BF16,),name=f"{tag}_bwd_dwup")
    dh2, dh2b, dg = _norm_input_grad(dup, p["w_up_t"], h, dh, p["norm"], name=f"{tag}_bwd_dh")
    return dh2, dh2b, {"norm": dg, "w_up_t": dw_up_t, "w_down": dw_down}


def _xa_fwd(h, hn, mem, p, tag, nb, next_gain):
    mem_n = _rms_fwd(mem, p["mem_norm"], name=f"{tag}_memnorm")
    q, = _matmul(hn, p["w_q"], outs=(BF16,), name=f"{tag}_q")
    kv, = _matmul(mem_n, p["w_kv_t"], tb=True, outs=(BF16,), name=f"{tag}_kv")
    q3, kv3 = _seq3(q, nb), _seq3(kv, nb)
    o, lse, ob = _attn_fwd((q3, 0), (kv3, 0), (kv3, 1), heads=XA_HEADS, dh=XA_HEAD_DIM, mode="full", bf16_copy=True,
                           name=f"{tag}_attn")
    o, ob = _flat2(o), _flat2(ob)
    out, hn_next = _residual_out(ob, p["w_o"], h, next_gain, name=f"{tag}_o")
    return out, hn_next, (h, hn, mem_n, q3, kv3, o, ob, lse)


def _xa_bwd(dh, dhb, mem, p, saved, tag, nb):
    h, hn, mem_n, q3, kv3, o, ob, lse = saved
    dob, delta = _matmul(dhb, p["w_o"], tb=True, outs=(BF16, (F32, LANES)), extras=(o,), consts=(_head_expand(XA_HEADS, D_MODEL).T,),
                         epilogue=lambda acc, o, e: (acc, _dot_exact(acc * o, e, parts=2)), name=f"{tag}_bwd_do")
    dw_o, = _matmul(ob, dhb, ta=True, outs=(BF16,),name=f"{tag}_bwd_dwo")
    dq, dk, dv = _attn_bwd((q3, 0), (kv3, 0), (kv3, 1), (_seq3(dob, nb), 0), lse, _seq3(delta, nb), heads=XA_HEADS,
                           dh=XA_HEAD_DIM, mode="full", dq_dtype=BF16, name=f"{tag}_bwd_attn")
    dqb = _flat2(dq)
    dkvb, = _rowwise(lambda a, b: jnp.concatenate([a, b], axis=1), [_flat2(dk), _flat2(dv)], [], [(2 * D_MODEL, BF16)],
                     name=f"{tag}_bwd_dkvcast")
    dw_q, = _matmul(hn, dqb, ta=True, outs=(BF16,),name=f"{tag}_bwd_dwq")
    dw_kv_t, = _matmul(dkvb, mem_n, ta=True, outs=(BF16,),name=f"{tag}_bwd_dwkv")
    dmem_n, = _matmul(dkvb, p["w_kv_t"], name=f"{tag}_bwd_dmemn")
    dg_mem, = _rowwise(lambda x, d, g: _rms_bwd_vals(x, d, g)[1], [mem, dmem_n], [p["mem_norm"]], [], [(1, D_MODEL)],
                       name=f"{tag}_bwd_memnorm")
    dh2, dh2b, dg = _norm_input_grad(dqb, p["w_q"], h, dh, p["norm"], tb=True, name=f"{tag}_bwd_dh")
    return dh2, dh2b, {"norm": dg, "mem_norm": dg_mem, "w_q": dw_q, "w_kv_t": dw_kv_t, "w_o": dw_o}


AB_MAIN = 2 * LRU_WIDTH + 3 * ATT_W


def _ab_fwd(h, hn, p, nb, next_gain, before_out=None):
    w_in_t = p["w_in_t"]
    zug, = _matmul(hn, w_in_t[:2 * LRU_WIDTH], tb=True, name="ab_in_ug")
    qkv, = _matmul(hn, w_in_t[2 * LRU_WIDTH:AB_MAIN], tb=True, outs=(BF16,), tn=1536, name="ab_in_qkv")
    zf, = _matmul(hn, w_in_t[AB_MAIN:], tb=True, name="ab_in_f")
    fl = zf[:, :8].reshape(nb, SEQ, 8).transpose(0, 2, 1)
    cbias = _fox_bias(fl, p["b_f"], name="ab_fox_bias")
    kb = cbias.reshape(nb, 8, SEQ // ATT_CHUNK, ATT_CHUNK)
    y_a, hl = _lru_fwd(zug, p["conv_w"], p["conv_b"], p["w_a"], p["b_a"], p["w_i"], p["b_i"], p["lam"], name="ab_lru")
    qkv3 = _seq3(qkv, nb)
    o, lse = _attn_fwd((qkv3, 0), (qkv3, 1), (qkv3, 2), kb, heads=8, dh=HEAD_DIM, mode="causal", name="ab_fox")
    o = _flat2(o)
    y, = _rowwise(lambda a, b: jnp.concatenate([a, b.astype(BF16)], axis=1), [y_a, o], [], [(D_MODEL, BF16)], name="ab_ycat")
    if before_out is not None:
        before_out(y)
    out, hn_next = _residual_out(y, p["w_out"], h, next_gain, name="ab_out")
    return out, hn_next, (h, hn, zug, qkv3, fl, kb, hl, o, lse, y)


def _ab_bwd(dh, dhb, p, saved, nb):
    h, hn, zug, qkv3, fl, kb, hl, o, lse, y = saved
    dy, dyb, delta = _matmul(dhb, p["w_out"], tb=True, outs=(F32, BF16, (F32, LANES)), extras=(y,), consts=(_head_expand(8, ATT_W).T,),
                             epilogue=lambda acc, y, e: (acc, acc, _dot_exact(acc[:, ATT_W:] * y[:, ATT_W:].astype(F32), e, parts=2)),
                             name="ab_bwd_dy")
    dw_out, = _matmul(y, dhb, ta=True, outs=(BF16,),name="ab_bwd_dwout")
    dzug, dcw, dcb, dwa, dba, dwi, dbi, dlam = _lru_bwd(zug, hl, dy, p["conv_w"], p["conv_b"], p["w_a"], p["b_a"],
                                                        p["w_i"], p["b_i"], p["lam"], name="ab_bwd_lru")
    dq, dk, dv, dkb, drow = _attn_bwd((qkv3, 0), (qkv3, 1), (qkv3, 2), (_seq3(dyb, nb), 1), lse, _seq3(delta, nb), kb,
                                      heads=8, dh=HEAD_DIM, mode="causal", name="ab_bwd_fox")
    dcum = dkb.reshape(nb, 8, SEQ) + drow[:, :, :8].transpose(0, 2, 1)
    dfl, dbf = _fox_bias_bwd(fl, p["b_f"], dcum, name="ab_bwd_fox_bias")
    dzf = jnp.pad(dfl.transpose(0, 2, 1).reshape(nb * SEQ, 8), ((0, 0), (0, LANES - 8)))
    dz, = _rowwise(lambda a, q, k, v, f: jnp.concatenate([a, q.astype(BF16), k.astype(BF16), v.astype(BF16), f.astype(BF16)], axis=1),
                   [dzug, _flat2(dq), _flat2(dk), _flat2(dv), dzf], [], [(AB_MAIN + LANES, BF16)], name="ab_bwd_dzcat")
    dw_in_t, = _matmul(dz, hn, ta=True, outs=(BF16,),tm=384, name="ab_bwd_dwin")
    dh2, dh2b, dg = _norm_input_grad(dz, p["w_in_t"], h, dh, p["norm"], name="ab_bwd_dh")
    grads = {"norm": dg, "w_in_t": dw_in_t[:AB_MAIN + 8], "conv_w": dcw, "conv_b": dcb, "w_a": _diag_blocks(dwa), "b_a": dba,
             "w_i": _diag_blocks(dwi), "b_i": dbi, "lam": dlam, "b_f": dbf.reshape(1, 8), "w_out": dw_out}
    return dh2, dh2b, grads


CD_IN = 3 * ATT_W + ATT_W + 2 * SWA_KW


def _gqa_share():
    src = jnp.arange(SWA_KW)[:, None]
    dst = jnp.arange(ATT_W)[None, :]
    per_kv = ATT_W // (SWA_KW // HEAD_DIM)
    return ((dst // per_kv == src // HEAD_DIM) & (dst % HEAD_DIM == src % HEAD_DIM)).astype(BF16)


def _cd_fwd(h, hn, p, nb, next_gain):
    z, = _matmul(hn, p["w_in_t"], tb=True, tn=2304, name="cd_in")
    cos, sin = _rope_tables()
    per_seq = SEQ // ROW_TILE
    table_map = lambda i: (i % per_seq, 0)

    def rope_fn(z, cos, sin, share):
        qc, kc, vc = z[:, 0:ATT_W], z[:, ATT_W:2 * ATT_W], z[:, 2 * ATT_W:3 * ATT_W]
        qd, kd, vd = z[:, 3 * ATT_W:4 * ATT_W], z[:, 4 * ATT_W:4 * ATT_W + SWA_KW], z[:, 4 * ATT_W + SWA_KW:]
        kd8 = jnp.dot(_rotate(kd, cos, sin).astype(BF16), share, preferred_element_type=F32)
        vd8 = jnp.dot(vd.astype(BF16), share, preferred_element_type=F32)
        qk = jnp.concatenate([_rotate(qc, cos, sin), _rotate(kc, cos, sin)], axis=1)
        return qk, vc, _rotate(qd, cos, sin), kd8, vd8, qk, qk, vc, vc
    dils = [dil for _, dil in DIL_PATTERN if dil > 1]
    outs = _rowwise(rope_fn, [z, cos, sin], [_gqa_share()],
                    [(2 * ATT_W, BF16)] + [(ATT_W, BF16)] * 4 + [(2 * ATT_W, BF16, d) for d in dils] + [(ATT_W, BF16, d) for d in dils],
                    name="cd_rope", row_maps=[None, table_map, table_map])
    qk, vc, qd, kd, vd = outs[:5]
    views = {1: (_seq3(qk, nb), _seq3(vc, nb))}
    for d, qk_d, vc_d in zip(dils, outs[5:5 + len(dils)], outs[5 + len(dils):]):
        views[d] = (_seq3(qk_d, nb), _seq3(vc_d, nb))
    in_classes = lambda a, width, d: _flat2(a) if d == 1 else ("classes", _flat2(a), width, d)
    branch = []
    for window, dil in DIL_PATTERN:
        qk_v, vc_v = views[dil]
        o_i, lse_i = _attn_fwd((qk_v, 0), (qk_v, 1), (vc_v, 0), classes=dil, heads=8, dh=HEAD_DIM, mode="band",
                               max_dist=window // dil, name=f"cd_dil{dil}")
        branch.append((in_classes(o_i, ATT_W, dil), in_classes(lse_i, LANES, dil)))
    expand = _head_expand(8, ATT_W)

    qd3, kd3, vd3 = _seq3(qd, nb), _seq3(kd, nb), _seq3(vd, nb)
    o_d, lse_band = _attn_fwd((qd3, 0), (kd3, 0), (vd3, 0), heads=8, dh=HEAD_DIM, mode="band", max_dist=SWA_WINDOW - 1,
                              name="cd_swa")

    def mix(o1, o2, o3, l1, l2, l3, o_band, l_band, e, sink):
        m = jnp.maximum(jnp.maximum(l1, l2), l3)
        lt = m + jnp.log(jnp.exp(l1 - m) + jnp.exp(l2 - m) + jnp.exp(l3 - m))
        y_c = sum(_dot_exact(jnp.exp(l - lt), e) * o_ for o_, l in ((o1, l1), (o2, l2), (o3, l3)))
        lt_d = jnp.maximum(l_band, sink) + jnp.log(1.0 + jnp.exp(-jnp.abs(l_band - sink)))
        y_d = o_band * _dot_exact(jnp.exp(l_band - lt_d), e)
        return jnp.concatenate([y_c, y_d], axis=1), y_c, lt, y_d, lt_d
    y, y_c, lse_c, y_d, lse_d = _rowwise(
        mix, [b[0] for b in branch] + [b[1] for b in branch] + [_flat2(o_d), _flat2(lse_band)], [expand, p["sink"]],
        [(D_MODEL, BF16), (ATT_W, F32), (LANES, F32), (ATT_W, F32), (LANES, F32)], name="cd_mix")
    out, hn_next = _residual_out(y, p["w_out"], h, next_gain, name="cd_out")
    return out, hn_next, (h, hn, views, qd3, kd3, vd3, y_c, lse_c, y_d, lse_d, y)


def _cd_bwd(dh, dhb, p, saved, nb):
    h, hn, views, qd3, kd3, vd3, y_c, lse_c, y_d, lse_d, y = saved
    dy, dyb = _matmul(dhb, p["w_out"], tb=True, outs=(F32, BF16), epilogue=lambda acc: (acc, acc), name="cd_bwd_dy")
    dw_out, = _matmul(y, dhb, ta=True, outs=(BF16,),name="cd_bwd_dwout")
    dyb3 = _seq3(dyb, nb)
    dils = [dil for _, dil in DIL_PATTERN if dil > 1]
    gather = _head_expand(8, ATT_W).T

    def prepare(do, o, lse, do_d, o_d, lse_d, e, sink):
        delta = _dot_exact(do * o, e, parts=2)
        delta_d = _dot_exact(do_d * o_d, e, parts=2)
        dsink = -jnp.sum(jnp.exp(sink - lse_d) * delta_d, axis=0, keepdims=True)
        return (delta,) + (do,) * len(dils) + (lse,) * len(dils) + (delta,) * len(dils) + (delta_d, dsink)
    outs = _rowwise(prepare, [(dy, ATT_W, 0), y_c, lse_c, (dy, ATT_W, 1), y_d, lse_d], [gather, p["sink"]],
                    [(LANES, F32)] + [(ATT_W, BF16, d) for d in dils] + [(LANES, F32, d) for d in dils] * 2 + [(LANES, F32)],
                    [(1, LANES)], name="cd_bwd_delta")
    delta_d, dsink = outs[-2:]
    seen = {1: ((dyb3, 0), _seq3(lse_c, nb), _seq3(outs[0], nb))}
    for j, d in enumerate(dils):
        seen[d] = ((_seq3(outs[1 + j], nb), 0), _seq3(outs[1 + len(dils) + j], nb), _seq3(outs[1 + 2 * len(dils) + j], nb))
    in_classes = lambda a, d: _flat2(a) if d == 1 else ("classes", _flat2(a), ATT_W, d)
    parts = []
    for window, dil in DIL_PATTERN:
        (qk_v, vc_v), (do_v, lse_v, delta_v) = views[dil], seen[dil]
        grads = _attn_bwd((qk_v, 0), (qk_v, 1), (vc_v, 0), do_v, lse_v, delta_v, classes=dil, heads=8, dh=HEAD_DIM, mode="band",
                          max_dist=window // dil, dq_dtype=BF16, name=f"cd_bwd_dil{dil}")
        parts.append([in_classes(a, dil) for a in grads])
    dqd, dkd, dvd = _attn_bwd((qd3, 0), (kd3, 0), (vd3, 0), (dyb3, 1), _seq3(lse_d, nb), _seq3(delta_d, nb), heads=8,
                              dh=HEAD_DIM, mode="band", max_dist=SWA_WINDOW - 1, dq_dtype=BF16, name="cd_bwd_swa")
    cos, sin = _rope_tables()
    per_seq = SEQ // ROW_TILE
    table_map = lambda i: (i % per_seq, 0)

    def unrope(q1, q2, q3, k1, k2, k3, v1, v2, v3, qd, kd8, vd8, cos, sin, gather):
        back = lambda x: _rotate(x.astype(F32), cos, -sin)
        kd, vd = _dot_exact(kd8, gather), _dot_exact(vd8, gather)
        return jnp.concatenate([back(q1 + q2 + q3), back(k1 + k2 + k3), v1 + v2 + v3, back(qd), back(kd), vd], axis=1)
    ins = [parts[b][t] for t in range(3) for b in range(3)] + [_flat2(dqd), _flat2(dkd), _flat2(dvd), cos, sin]
    dz, = _rowwise(unrope, ins, [_gqa_share().T], [(CD_IN, BF16)], name="cd_bwd_unrope",
                   row_maps=[None] * 12 + [table_map, table_map])
    dw_in_t, = _matmul(dz, hn, ta=True, outs=(BF16,),tm=384, name="cd_bwd_dwin")
    dh2, dh2b, dg = _norm_input_grad(dz, p["w_in_t"], h, dh, p["norm"], name="cd_bwd_dh")
    return dh2, dh2b, {"norm": dg, "w_in_t": dw_in_t, "sink": dsink[:, :8], "w_out": dw_out}


def _local_step(x, mem, target, w, complete=None, grads_ready=None):
    nb = x.shape[0]
    h = x.reshape(nb * SEQ, D_MODEL)
    mem2 = mem.reshape(nb * MEM_LEN, D_MODEL)
    tgt = target.reshape(nb * SEQ, D_MODEL)

    hn = _rms_fwd(h, w["ab"]["norm"], name="ab_norm")
    h, hn, s_ab = _ab_fwd(h, hn, w["ab"], nb, w["xa0"]["norm"], before_out=complete)
    h, hn, s_xa0 = _xa_fwd(h, hn, mem2, w["xa0"], "xa0", nb, w["mlp0"]["norm"])
    h, hn, s_mlp0 = _mlp_fwd(h, hn, w["mlp0"], "mlp0", w["cd"]["norm"])
    h, hn, s_cd = _cd_fwd(h, hn, w["cd"], nb, w["xa1"]["norm"])
    h, hn, s_xa1 = _xa_fwd(h, hn, mem2, w["xa1"], "xa1", nb, w["mlp1"]["norm"])
    h, _, s_mlp1 = _mlp_fwd(h, hn, w["mlp1"], "mlp1", None)

    dh, dhb, loss, dg_final = _loss_and_grad(h, tgt, w["final_norm"], name="loss")
    grads = {"final_norm": dg_final}
    dh, dhb, grads["mlp1"] = _mlp_bwd(dh, dhb, w["mlp1"], s_mlp1, "mlp1")
    dh, dhb, grads["xa1"] = _xa_bwd(dh, dhb, mem2, w["xa1"], s_xa1, "xa1", nb)
    dh, dhb, grads["cd"] = _cd_bwd(dh, dhb, w["cd"], s_cd, nb)
    if grads_ready is not None:
        grads_ready(0, grads)
    dh, dhb, grads["mlp0"] = _mlp_bwd(dh, dhb, w["mlp0"], s_mlp0, "mlp0")
    dh, dhb, grads["xa0"] = _xa_bwd(dh, dhb, mem2, w["xa0"], s_xa0, "xa0", nb)
    if grads_ready is not None:
        grads_ready(1, grads)
    dh, dhb, grads["ab"] = _ab_bwd(dh, dhb, w["ab"], s_ab, nb)
    return loss, dh.reshape(nb, SEQ, D_MODEL), grads


ANY = pl.BlockSpec(memory_space=pl.ANY)


def _place():
    x, y, c = lax.axis_index("x"), lax.axis_index("y"), lax.axis_index("c")
    return x, y, c, [(1 - x, y), (x, 1 - y), (1 - x, 1 - y)]


def _gather_chips(shard):
    half = shard.shape[0] // 2

    def body(src, out, send_sems, recv_sems):
        x, y, c, chips = _place()
        sibling = (x, y, 1 - c)

        def rows(slot, core):
            return out.at[slot, pl.ds(core * half, half)]

        def copy(k, src_ref, dst_ref, to):
            return pltpu.make_async_remote_copy(src_ref=src_ref, dst_ref=dst_ref, send_sem=send_sems.at[k],
                                                recv_sem=recv_sems.at[k], device_id=to, device_id_type=MESH)

        first = [copy(k, src.at[pl.ds(c * half, half)], rows(2 * x + y, c), (px, py, c)) for k, (px, py) in enumerate(chips)]
        for cp in first:
            cp.start()
        passed = [copy(3 + k, rows(2 * px + py, c), rows(2 * px + py, c), sibling) for k, (px, py) in enumerate(chips)]
        for k, (px, py) in enumerate(chips):
            copy(k, src.at[pl.ds(c * half, half)], rows(2 * px + py, c), (px, py, c)).wait_recv()
            passed[k].start()
        for k, (px, py) in enumerate(chips):
            copy(3 + k, rows(2 * px + py, 1 - c), rows(2 * px + py, 1 - c), sibling).wait_recv()
        for cp in first + passed:
            cp.wait_send()

    return pl.pallas_call(
        body, out_shape=jax.ShapeDtypeStruct((N_CHIPS,) + shard.shape, shard.dtype), in_specs=[ANY], out_specs=ANY,
        scratch_shapes=[pltpu.SemaphoreType.DMA((6,)), pltpu.SemaphoreType.DMA((6,))], name="gather_chips",
    )(shard)


HBM = pl.BlockSpec(memory_space=pltpu.HBM)
SEM = pl.BlockSpec(memory_space=pltpu.SEMAPHORE)
SIDE_EFFECT = pltpu.SideEffectType.DATAFLOW_SIDE_EFFECTING


def _chip_copies(src, land, send_sems, recv_sems):
    half = src.shape[0] // 2
    x, y, c, chips = _place()

    def copy(k, slot, to):
        return pltpu.make_async_remote_copy(src_ref=src.at[pl.ds(c * half, half)], dst_ref=land.at[slot, pl.ds(c * half, half)],
                                            send_sem=send_sems[k], recv_sem=recv_sems[k], device_id=to, device_id_type=MESH)
    out = [copy(k, 2 * x + y, (px, py, c)) for k, (px, py) in enumerate(chips)]
    back = [copy(k, 2 * px + py, (px, py, c)) for k, (px, py) in enumerate(chips)]
    return out, back


def _gather_start(shard, after):
    def body(src, land, *rest):
        rest = rest[len(after):]
        sems, token = rest[:6], rest[8]
        out, _ = _chip_copies(src, land, sems[:3], sems[3:])
        for cp in out:
            cp.start()
        token[...] = jnp.zeros(token.shape, F32)

    sem = pltpu.SemaphoreType.DMA(())
    land_shape = (N_CHIPS,) + shard.shape
    return pl.pallas_call(
        body, name="gather_start",
        out_shape=(sem,) * 6 + (pltpu.HBM(shard.shape, shard.dtype), pltpu.HBM(land_shape, shard.dtype),
                                jax.ShapeDtypeStruct((8, LANES), F32)),
        in_specs=(HBM, HBM) + (pl.BlockSpec(memory_space=pl.ANY),) * len(after),
        out_specs=(SEM,) * 6 + (HBM, HBM, pl.BlockSpec(memory_space=pltpu.VMEM)),
        input_output_aliases={0: 6, 1: 7}, compiler_params=pltpu.CompilerParams(has_side_effects=SIDE_EFFECT),
    )(pltpu.with_memory_space_constraint(shard, pltpu.HBM),
      pltpu.with_memory_space_constraint(lax.empty(land_shape, shard.dtype), pltpu.HBM), *after)


def _gather_wait(started, after):
    sems, src_thru, land_thru = started[:6], started[6], started[7]

    def body(src, land, *rest):
        out, back = _chip_copies(src, land, rest[:3], rest[3:6])
        for cp in out:
            cp.wait_send()
        for cp in back:
            cp.wait_recv()

    return pl.pallas_call(
        body, name="gather_wait",
        out_shape=(pltpu.HBM(src_thru.shape, src_thru.dtype), pltpu.HBM(land_thru.shape, land_thru.dtype)),
        in_specs=(HBM, HBM) + (SEM,) * 6 + (pl.BlockSpec(memory_space=pl.ANY),), out_specs=(HBM, HBM),
        input_output_aliases={0: 0, 1: 1}, compiler_params=pltpu.CompilerParams(has_side_effects=SIDE_EFFECT),
    )(src_thru, land_thru, *sems, after)


def _gather_pass(land):
    half = land.shape[1] // 2

    def body(land_in, land_out, send_sems, recv_sems):
        x, y, c, chips = _place()

        def copy(k, slot, core):
            rows = land_out.at[slot, pl.ds(core * half, half)]
            return pltpu.make_async_remote_copy(src_ref=rows, dst_ref=rows, send_sem=send_sems.at[k], recv_sem=recv_sems.at[k],
                                                device_id=(x, y, 1 - c), device_id_type=MESH)
        sends = [copy(k, 2 * px + py, c) for k, (px, py) in enumerate(chips)]
        for cp in sends:
            cp.start()
        for k, (px, py) in enumerate(chips):
            copy(k, 2 * px + py, 1 - c).wait_recv()
        for cp in sends:
            cp.wait_send()

    return pl.pallas_call(
        body, out_shape=jax.ShapeDtypeStruct(land.shape, land.dtype), in_specs=[ANY], out_specs=ANY,
        scratch_shapes=[pltpu.SemaphoreType.DMA((3,)), pltpu.SemaphoreType.DMA((3,))], input_output_aliases={0: 0},
        name="gather_pass",
    )(land)


def _swap_cores(block, *, name, half_rows=None):
    shape = block.shape if half_rows is None else (block.shape[0], half_rows, block.shape[2])

    def body(src, out, send_sem, recv_sem):
        x, y, c, _ = _place()
        part = src if half_rows is None else src.at[:, pl.ds((1 - c) * half_rows, half_rows)]
        cp = pltpu.make_async_remote_copy(src_ref=part, dst_ref=out, send_sem=send_sem, recv_sem=recv_sem,
                                          device_id=(x, y, 1 - c), device_id_type=MESH)
        cp.start()
        cp.wait()

    return pl.pallas_call(
        body, out_shape=jax.ShapeDtypeStruct(shape, block.dtype), in_specs=[ANY], out_specs=ANY,
        scratch_shapes=[pltpu.SemaphoreType.DMA(()), pltpu.SemaphoreType.DMA(())], name=name,
    )(block)


def _scatter_copies(parts, land, send_sems, recv_sems):
    x, y, c, chips = _place()
    return [pltpu.make_async_remote_copy(src_ref=parts.at[2 * px + py], dst_ref=land.at[k], send_sem=send_sems[k],
                                         recv_sem=recv_sems[k], device_id=(px, py, c), device_id_type=MESH)
            for k, (px, py) in enumerate(chips)]


def _scatter_start(parts, tag):
    def body(src, land, *rest):
        for cp in _scatter_copies(src, land, rest[:3], rest[3:6]):
            cp.start()
        rest[8][...] = jnp.zeros(rest[8].shape, F32)

    sem = pltpu.SemaphoreType.DMA(())
    land_shape = (3,) + parts.shape[1:]
    return pl.pallas_call(
        body, name=f"scatter_start_{tag}",
        out_shape=(sem,) * 6 + (pltpu.HBM(parts.shape, parts.dtype), pltpu.HBM(land_shape, parts.dtype),
                                jax.ShapeDtypeStruct((8, LANES), F32)),
        in_specs=(HBM, HBM), out_specs=(SEM,) * 6 + (HBM, HBM, pl.BlockSpec(memory_space=pltpu.VMEM)),
        input_output_aliases={0: 6, 1: 7}, compiler_params=pltpu.CompilerParams(has_side_effects=SIDE_EFFECT),
    )(pltpu.with_memory_space_constraint(parts, pltpu.HBM),
      pltpu.with_memory_space_constraint(lax.empty(land_shape, parts.dtype), pltpu.HBM))


def _scatter_wait(started, after, tag):
    def body(src, land, *rest):
        for cp in _scatter_copies(src, land, rest[:3], rest[3:6]):
            cp.wait_send()
            cp.wait_recv()

    src_thru, land_thru = started[6], started[7]
    return pl.pallas_call(
        body, name=f"scatter_wait_{tag}",
        out_shape=(pltpu.HBM(src_thru.shape, src_thru.dtype), pltpu.HBM(land_thru.shape, land_thru.dtype)),
        in_specs=(HBM, HBM) + (SEM,) * 6 + (pl.BlockSpec(memory_space=pl.ANY),), out_specs=(HBM, HBM),
        input_output_aliases={0: 0, 1: 1}, compiler_params=pltpu.CompilerParams(has_side_effects=SIDE_EFFECT),
    )(src_thru, land_thru, *started[:6], after)[1]


def _sum_all_devices(vec):
    rows = vec.shape[0]

    def body(x_ref, total_ref, all_ref, send_sems, recv_sems, local_sem):
        x, y, c, chips = _place()
        me, sibling = (x, y, c), (x, y, 1 - c)

        def slot(px, py, pc):
            return all_ref.at[4 * px + 2 * py + pc]

        def copy(k, block, to, src=None):
            return pltpu.make_async_remote_copy(src_ref=slot(*block) if src is None else src, dst_ref=slot(*block),
                                                send_sem=send_sems.at[k], recv_sem=recv_sems.at[k], device_id=to,
                                                device_id_type=MESH)

        mine = pltpu.make_async_copy(x_ref, slot(*me), local_sem)
        mine.start()
        first = [copy(0, me, sibling, src=x_ref)] + [copy(1 + j, me, (*chip, c), src=x_ref) for j, chip in enumerate(chips)]
        for cp in first:
            cp.start()
        passed = [copy(4 + j, (*chip, c), sibling) for j, chip in enumerate(chips)]
        for j, chip in enumerate(chips):
            copy(1 + j, (*chip, c), me).wait_recv()
            passed[j].start()
        copy(0, sibling, me).wait_recv()
        for j, chip in enumerate(chips):
            copy(4 + j, (*chip, 1 - c), me).wait_recv()
        for cp in first + passed:
            cp.wait_send()
        mine.wait()
        acc = all_ref[0]
        for d in range(1, 8):
            acc = acc + all_ref[d]
        total_ref[...] = acc

    vmem = pl.BlockSpec(memory_space=pltpu.VMEM)
    return pl.pallas_call(
        body, out_shape=[jax.ShapeDtypeStruct((rows, LANES), F32), jax.ShapeDtypeStruct((8, rows, LANES), F32)],
        in_specs=[vmem], out_specs=[vmem, vmem],
        scratch_shapes=[pltpu.SemaphoreType.DMA((7,)), pltpu.SemaphoreType.DMA((7,)), pltpu.SemaphoreType.DMA(())],
        name="sum_all_devices", compiler_params=pltpu.CompilerParams(vmem_limit_bytes=VMEM_LIMIT_BYTES),
    )(vec)[0]


PACK_COLS = 1024
BIG = (("mlp_w_up", 2, 1024, True), ("mlp_w_down", 2, 1024, False), ("xa_w_kv", 2, 512, True), ("xa_w_q", 2, 256, False),
       ("xa_w_o", 2, 256, False), ("ab_w_out", 1, 256, False), ("cd_w_out", 1, 256, False), ("cd_w_in", 1, 576, True),
       ("ab_w_in", 1, 642, True))
ENTRIES = tuple((name, layer, rows, transposed) for name, layers, rows, transposed in BIG for layer in range(layers))
FIRST_ENTRIES = tuple(e for e in ENTRIES if e[0] == "ab_w_in")
LATER_ENTRIES = tuple(e for e in ENTRIES if e[0] != "ab_w_in")


def _tile_rows(entry):
    return -(-entry[2] // 16) * 16


def _padded_rows(entries):
    return -(-sum(_tile_rows(e) for e in entries) // 32) * 32


SMALL = (("ab_norm", (1, 1024)), ("ab_conv_w", (1, 4, 512)), ("ab_conv_b", (1, 512)), ("lru_w_a", (1, 8, 64, 64)),
         ("lru_b_a", (1, 512)), ("lru_w_i", (1, 8, 64, 64)), ("lru_b_i", (1, 512)), ("lru_lambda", (1, 512)),
         ("fox_b_f", (1, 8)), ("cd_norm", (1, 1024)), ("cd_sink", (1, 8)), ("xa_norm", (2, 1024)),
         ("xa_mem_norm", (2, 1024)), ("mlp_norm", (2, 1024)), ("final_norm", (1024,)), ("loss", (1,)))
SPLIT_SMALL = ("ab_conv_w", "cd_norm")
SPLIT_SMALL_ROWS = 16
assert _padded_rows(FIRST_ENTRIES) - SPLIT_SMALL_ROWS >= sum(_tile_rows(e) for e in FIRST_ENTRIES)


def _pack_rows(parts, entries, dtype):
    lead = parts[0].shape[:-2]
    zeros = lambda rows: jnp.zeros(lead + (rows, PACK_COLS), dtype)
    pieces = [jnp.pad(p.astype(dtype), ((0, 0),) * len(lead) + ((0, _tile_rows(e) - e[2]), (0, 0))) for p, e in zip(parts, entries)]
    tail = _padded_rows(entries) - sum(_tile_rows(e) for e in entries)
    return jnp.concatenate(pieces + ([zeros(tail)] if tail else []), axis=len(lead))


def _unpack_rows(packed, entries):
    out, r0 = [], 0
    for e in entries:
        out.append(packed[..., r0:r0 + e[2], :])
        r0 += _tile_rows(e)
    return out


def _shard_rows(w, entries):
    return [(w[name][layer].T if transposed else w[name][layer]) for name, layer, _, transposed in entries]


def _shard_blocks(rows):
    out = {}
    for (name, layer, _, transposed), r in zip(ENTRIES, rows):
        out.setdefault(name, []).append(r.T if transposed else r)
    return {name: jnp.stack(layers) for name, layers in out.items()}


def _pack_small(vals):
    flat = jnp.concatenate([vals[name].astype(F32).reshape(-1) for name, _ in SMALL])
    size = -(-flat.shape[0] // (8 * LANES)) * (8 * LANES)
    return jnp.pad(flat, (0, size - flat.shape[0])).reshape(-1, LANES)


def _unpack_small(packed):
    flat, out, at = packed.reshape(-1), {}, 0
    for name, shape in SMALL:
        out[name] = flat[at:at + math.prod(shape)].reshape(shape)
        at += math.prod(shape)
    return out


def _chip_part(full, chip):
    width = full.shape[-1] // N_CHIPS
    return lax.dynamic_slice_in_dim(full, chip * width, width, axis=full.ndim - 1)


def _chip_embed(part, chip):
    full = jnp.zeros(part.shape[:-1] + (part.shape[-1] * N_CHIPS,), part.dtype)
    return lax.dynamic_update_slice_in_dim(full, part, chip * part.shape[-1], axis=part.ndim - 1)


SUBLAYER_KEYS = {"ab_w_in": ("ab", "w_in_t"), "ab_w_out": ("ab", "w_out"), "cd_w_in": ("cd", "w_in_t"), "cd_w_out": ("cd", "w_out"),
                 "xa_w_q": ("xa", "w_q"), "xa_w_kv": ("xa", "w_kv_t"), "xa_w_o": ("xa", "w_o"), "mlp_w_up": ("mlp", "w_up_t"),
                 "mlp_w_down": ("mlp", "w_down")}


def _sublayer(name, layer):
    block, leaf = SUBLAYER_KEYS[name]
    return (block if block in ("ab", "cd") else f"{block}{layer}"), leaf


def _set_big(params, full_rows):
    for (name, layer), rows in full_rows.items():
        block, leaf = _sublayer(name, layer)
        if name == "ab_w_in":
            rows = jnp.concatenate([rows, jnp.zeros((LANES - 8, PACK_COLS), rows.dtype)])
        params[block][leaf] = rows


def _build_params(full_rows, w):
    pad = lambda a: jnp.pad(a, ((0, 0), (0, LANES - a.shape[1])))
    params = {
        "ab": dict(norm=w["ab_norm"], conv_w=w["ab_conv_w"][0], conv_b=w["ab_conv_b"], w_a=_block_diag(w["lru_w_a"][0]),
                   b_a=w["lru_b_a"], w_i=_block_diag(w["lru_w_i"][0]), b_i=w["lru_b_i"], lam=w["lru_lambda"],
                   b_f=w["fox_b_f"].reshape(8, 1)),
        "cd": dict(norm=w["cd_norm"], sink=pad(w["cd_sink"])),
        "final_norm": w["final_norm"].reshape(1, D_MODEL),
    }
    for layer in range(2):
        params[f"xa{layer}"] = dict(norm=w["xa_norm"][layer:layer + 1], mem_norm=w["xa_mem_norm"][layer:layer + 1])
        params[f"mlp{layer}"] = dict(norm=w["mlp_norm"][layer:layer + 1])
    _set_big(params, full_rows)
    return params


def _grad_rows(g, entries=ENTRIES):
    out = []
    for name, layer, _, _ in entries:
        block, leaf = _sublayer(name, layer)
        out.append(g[block][leaf])
    return out


def _reduce_group(entry):
    name, layer = entry[0], entry[1]
    if name.startswith("ab_"):
        return 2
    return 0 if layer == 1 or name.startswith("cd_") else 1


REDUCE_GROUPS = tuple(tuple(e for e in ENTRIES if _reduce_group(e) == group) for group in range(3))


def _reduce_tile(half):
    return max(t for t in range(16, 1025, 16) if half % t == 0)


def _reduce_start(grads_by_chip, core, chip, tag):
    half = grads_by_chip.shape[1] // 2
    tile = _reduce_tile(half)
    steps = half // tile
    place = jnp.stack([core, chip]).astype(jnp.int32)
    got = _swap_cores(grads_by_chip, name=f"swap_cores_{tag}", half_rows=half)
    block = (1, tile, PACK_COLS)

    def sum_cores(place_ref, mine_ref, got_ref, f32_ref, bf16_ref):
        total = mine_ref[...].astype(F32) + got_ref[...].astype(F32)
        f32_ref[...] = total
        bf16_ref[...] = total.astype(BF16)

    pair32, pair16 = pl.pallas_call(
        sum_cores, name=f"sum_cores_{tag}",
        grid_spec=pltpu.PrefetchScalarGridSpec(
            num_scalar_prefetch=1, grid=(N_CHIPS, steps),
            in_specs=[pl.BlockSpec(block, lambda s, i, place_ref: (s, place_ref[0] * steps + i, 0)),
                      pl.BlockSpec(block, lambda s, i, place_ref: (s, i, 0))],
            out_specs=[pl.BlockSpec(block, lambda s, i, place_ref: (s, i, 0))] * 2),
        out_shape=[jax.ShapeDtypeStruct((N_CHIPS, half, PACK_COLS), F32), jax.ShapeDtypeStruct((N_CHIPS, half, PACK_COLS), BF16)],
        compiler_params=_params("arbitrary", "arbitrary"),
    )(place, grads_by_chip, got)
    return pair32, _scatter_start(pair16, tag), place


def _reduce_finish(state, core, tag, after):
    pair32, started, place = state
    half = pair32.shape[1]
    tile = _reduce_tile(half)
    landed = _scatter_wait(started, after, tag)

    def sum_chips(place_ref, own_ref, landed_ref, out_ref):
        out_ref[...] = own_ref[0] + landed_ref[0].astype(F32) + landed_ref[1].astype(F32) + landed_ref[2].astype(F32)

    done = pl.pallas_call(
        sum_chips, name=f"sum_chips_{tag}",
        grid_spec=pltpu.PrefetchScalarGridSpec(
            num_scalar_prefetch=1, grid=(half // tile,),
            in_specs=[pl.BlockSpec((1, tile, PACK_COLS), lambda i, place_ref: (place_ref[1], i, 0)),
                      pl.BlockSpec((3, tile, PACK_COLS), lambda i, place_ref: (0, i, 0))],
            out_specs=pl.BlockSpec((tile, PACK_COLS), lambda i, place_ref: (i, 0))),
        out_shape=jax.ShapeDtypeStruct((half, PACK_COLS), F32), compiler_params=_params("arbitrary"),
    )(place, pair32, landed)
    theirs = _swap_cores(done, name=f"share_halves_{tag}")
    return jnp.where(core == 0, jnp.concatenate([done, theirs]), jnp.concatenate([theirs, done]))


def kernel(x, mem, ab_norm, ab_w_in, ab_conv_w, ab_conv_b, lru_w_a, lru_b_a, lru_w_i, lru_b_i, lru_lambda, fox_b_f, ab_w_out, cd_norm, cd_w_in, cd_sink, cd_w_out, xa_norm, xa_mem_norm, xa_w_q, xa_w_kv, xa_w_o, mlp_norm, mlp_w_up, mlp_w_down, final_norm, loss_target, m_ab_norm, m_ab_w_in, m_ab_conv_w, m_ab_conv_b, m_lru_w_a, m_lru_b_a, m_lru_w_i, m_lru_b_i, m_lru_lambda, m_fox_b_f, m_ab_w_out, m_cd_norm, m_cd_w_in, m_cd_sink, m_cd_w_out, m_xa_norm, m_xa_mem_norm, m_xa_w_q, m_xa_w_kv, m_xa_w_o, m_mlp_norm, m_mlp_w_up, m_mlp_w_down, m_final_norm, v_ab_norm, v_ab_w_in, v_ab_conv_w, v_ab_conv_b, v_lru_w_a, v_lru_b_a, v_lru_w_i, v_lru_b_i, v_lru_lambda, v_fox_b_f, v_ab_w_out, v_cd_norm, v_cd_w_in, v_cd_sink, v_cd_w_out, v_xa_norm, v_xa_mem_norm, v_xa_w_q, v_xa_w_kv, v_xa_w_o, v_mlp_norm, v_mlp_w_up, v_mlp_w_down, v_final_norm):
    given = dict(locals())
    weight_names = [name for name, _ in SMALL if name != "loss"] + [name for name, _, _, _ in BIG]
    w = {name: given[name] for name in weight_names}
    chip = 2 * lax.axis_index("x") + lax.axis_index("y")
    core = lax.axis_index("c")

    slot = lax.broadcasted_iota(jnp.int32, (N_CHIPS, 1, 1), 0)

    def whole(entries, mine, gathered):
        return {(name, layer): jnp.where(slot == chip, own[None], got).reshape(N_CHIPS * rows, PACK_COLS)
                for (name, layer, rows, _), own, got in zip(entries, _unpack_rows(mine, entries), _unpack_rows(gathered, entries))}

    bits = lambda a: lax.bitcast_convert_type(a.reshape(-1), BF16).reshape(-1)
    spare = _padded_rows(FIRST_ENTRIES) - SPLIT_SMALL_ROWS
    small_rows = jnp.zeros((SPLIT_SMALL_ROWS, PACK_COLS), BF16)
    small_rows = small_rows.at[0].set(bits(w["ab_conv_w"])).at[1, :2 * w["cd_norm"].size].set(bits(w["cd_norm"]))
    mine_first = _pack_rows(_shard_rows(w, FIRST_ENTRIES), FIRST_ENTRIES, BF16).at[spare:].set(small_rows)
    mine_later = _pack_rows(_shard_rows(w, LATER_ENTRIES), LATER_ENTRIES, BF16)
    first = _gather_chips(mine_first)
    floats = lambda a: lax.bitcast_convert_type(a.reshape(a.shape[:-1] + (-1, 2)), F32)
    all_small = jnp.where(slot == chip, floats(mine_first[None, spare:]), floats(first[:, spare:]))
    taps, per_chip = w["ab_conv_w"].shape[1:]
    conv_w_full = all_small[:, 0].reshape(N_CHIPS, taps, per_chip).transpose(1, 0, 2).reshape(1, taps, N_CHIPS * per_chip)
    cd_norm_full = all_small[:, 1, :w["cd_norm"].size].reshape(1, -1)
    started = _gather_start(mine_later, after=(first,))
    params = _build_params(whole(FIRST_ENTRIES, mine_first, first), {**w, "ab_conv_w": conv_w_full, "cd_norm": cd_norm_full})
    params["ab"]["norm"] = params["ab"]["norm"] + started[8][0, 0]

    def complete(h):
        mine, landed = _gather_wait(started, after=h)
        _set_big(params, whole(LATER_ENTRIES, mine, _gather_pass(landed)))

    reducing = {}

    def grads_ready(group, g):
        entries = REDUCE_GROUPS[group]
        by_chip = _pack_rows([r.reshape(N_CHIPS, e[2], PACK_COLS) for r, e in zip(_grad_rows(g, entries), entries)], entries, BF16)
        reducing[group] = _reduce_start(by_chip, core, chip, f"g{group}")
        if group < 2:
            block, leaf = ("mlp0", "w_down") if group == 0 else ("ab", "w_out")
            params[block][leaf] = params[block][leaf] + reducing[group][1][8][0, 0].astype(BF16)

    loss_part, grad_x, g = _local_step(x, mem, loss_target, params, complete, grads_ready)
    grads_ready(2, g)
    reduced, after = {}, reducing[2][1][8]
    for group, entries in enumerate(REDUCE_GROUPS):
        after = _reduce_finish(reducing[group], core, f"g{group}", after=after)
        reduced.update({(e[0], e[1]): r for e, r in zip(entries, _unpack_rows(after, entries))})
    grads = _shard_blocks([reduced[e[0], e[1]] for e in ENTRIES])
    pair = lambda key, leaf: jnp.stack([g[f"{key}0"][leaf], g[f"{key}1"][leaf]])

    small_local = {"ab_norm": g["ab"]["norm"], "ab_conv_w": g["ab"]["conv_w"], "ab_conv_b": g["ab"]["conv_b"],
                   "lru_w_a": g["ab"]["w_a"], "lru_b_a": g["ab"]["b_a"], "lru_w_i": g["ab"]["w_i"], "lru_b_i": g["ab"]["b_i"],
                   "lru_lambda": g["ab"]["lam"], "fox_b_f": g["ab"]["b_f"], "cd_norm": g["cd"]["norm"], "cd_sink": g["cd"]["sink"],
                   "xa_norm": pair("xa", "norm"), "xa_mem_norm": pair("xa", "mem_norm"), "mlp_norm": pair("mlp", "norm"),
                   "final_norm": g["final_norm"], "loss": loss_part[0, :1]}
    small_sum_packed = _sum_all_devices(_pack_small(small_local))
    small_sum = _unpack_small(small_sum_packed)
    loss = small_sum["loss"][0]

    delta, new_m, new_v = {}, {}, {}
    for name, _, _, _ in BIG:
        shape = w[name].shape
        flat = lambda a: a.reshape(-1, shape[-1])
        d, m2, v2 = _adamw(flat(w[name]), flat(grads[name]), flat(given["m_" + name]), flat(given["v_" + name]), name=f"adamw_{name}")
        delta[name], new_m[name], new_v[name] = d.reshape(shape), m2.reshape(shape), v2.reshape(shape)

    def small_state(prefix):
        vals = {name: (given[prefix + name] if name != "loss" else jnp.zeros((1,), F32)) for name, _ in SMALL}
        for name in SPLIT_SMALL:
            vals[name] = _chip_embed(vals[name], chip)
        return _pack_small(vals)
    packed = _adamw(small_state(""), small_sum_packed, small_state("m_"), small_state("v_"), name="adamw_small")
    for store, vals in zip((delta, new_m, new_v), packed):
        for name, val in _unpack_small(vals).items():
            store[name] = _chip_part(val, chip) if name in SPLIT_SMALL else val
    for name in SPLIT_SMALL:
        small_sum[name] = _chip_part(small_sum[name], chip)
    grads.update({name: small_sum[name] for name, _ in SMALL})

    order = ["ab_norm", "ab_w_in", "ab_conv_w", "ab_conv_b", "lru_w_a", "lru_b_a", "lru_w_i", "lru_b_i", "lru_lambda", "fox_b_f",
             "ab_w_out", "cd_norm", "cd_w_in", "cd_sink", "cd_w_out", "xa_norm", "xa_mem_norm", "xa_w_q", "xa_w_kv", "xa_w_o",
             "mlp_norm", "mlp_w_up", "mlp_w_down", "final_norm"]
    return (loss, grad_x, *[grads[n] for n in order], *[delta[n] for n in order], *[new_m[n] for n in order],
            *[new_v[n] for n in order])
```

```python
import functools
import math

import jax
import jax.numpy as jnp
from jax import lax
from jax.experimental import pallas as pl
from jax.experimental.pallas import tpu as pltpu

F32 = jnp.float32
BF16 = jnp.bfloat16
MESH = pl.DeviceIdType.MESH

D_MODEL = 1024
SEQ = 2048
HEAD_DIM = 64
LRU_WIDTH = 512
LRU_BLOCKS = 8
LRU_C = 8.0
CONV_WIDTH = 4
ATT_W = 512
SWA_KW = 128
SWA_WINDOW = 128
DIL_PATTERN = ((128, 1), (512, 4), (2048, 16))
MEM_LEN = 256
XA_HEADS = 4
XA_HEAD_DIM = 256
D_FF = 4096
ROPE_THETA = 10000.0
EPS = 1e-6
N_CHIPS = 4
LANES = 128

ADAM_LR, ADAM_B1, ADAM_B2, ADAM_EPS, ADAM_WD, ADAM_STEP = 0.001, 0.9, 0.999, 1e-08, 0.01, 10

VMEM_LIMIT_BYTES = 56 * 1024 * 1024
MASKED = -1e30
ROW_TILE = 512
LRU_CHUNK = 512
ATT_BLOCK = 128
BAND_ROWS = 256
FULL_ROWS = 512
ATT_CHUNK = 512
CAUSAL_ROWS = 256
CAUSAL_ROWS_BACKWARD = 512
CLASS_ROWS = 512


def _params(*sem):
    return pltpu.CompilerParams(dimension_semantics=sem, vmem_limit_bytes=VMEM_LIMIT_BYTES)


def _rowwise(fn, rows, consts=(), out_rows=(), out_accs=(), *, name, tile=ROW_TILE, row_maps=None):
    rows = [r if isinstance(r, tuple) else (r, r.shape[1], 0) for r in rows]
    consts = list(consts)
    nr, nc, no = len(rows), len(consts), len(out_rows)
    dealt_in = [r[3] if isinstance(r[0], str) else None for r in rows]
    dealt_out = [o[2] if len(o) == 3 else None for o in out_rows]
    total = next(r[0].shape[0] for r in rows if not isinstance(r[0], str))
    tile = min(tile, total)
    assert total % tile == 0
    n = total // tile
    n_acc = len(out_accs)

    def body(*refs):
        scratch = list(refs[nr + nc + no + n_acc:])
        vals = []
        for ref, d, row in zip(refs[:nr], dealt_in, rows):
            if d is None:
                vals.append(ref[...])
                continue
            sc, width = scratch.pop(0), row[2]
            for r in range(d):
                for c in range(width // LANES):
                    lanes = slice(r * width + c * LANES, r * width + (c + 1) * LANES)
                    sc.at[c][pl.ds(r, tile // d, stride=d), :] = ref[:, lanes].astype(F32)
            vals.append(jnp.concatenate([sc[c] for c in range(width // LANES)], axis=1))
        outs = fn(*vals, *[r[...] for r in refs[nr:nr + nc]])
        outs = tuple(outs) if isinstance(outs, (tuple, list)) else (outs,)
        for ref, val, d, spec in zip(refs[nr + nc:nr + nc + no], outs[:no], dealt_out, out_rows):
            if d is None:
                ref[...] = val.astype(ref.dtype)
                continue
            sc, width = scratch.pop(0), spec[0]
            for c in range(width // LANES):
                sc[c] = val[:, c * LANES:(c + 1) * LANES].astype(F32)
            for r in range(d):
                for c in range(width // LANES):
                    lanes = slice(r * width + c * LANES, r * width + (c + 1) * LANES)
                    ref[:, lanes] = sc.at[c][pl.ds(r, tile // d, stride=d), :].astype(ref.dtype)
        for ref, val in zip(refs[nr + nc + no:nr + nc + no + n_acc], outs[no:]):
            @pl.when(pl.program_id(0) == 0)
            def _(ref=ref):
                ref[...] = jnp.zeros(ref.shape, ref.dtype)
            ref[...] += val

    in_specs, args, scratch_shapes = [], [], []
    for idx, row in enumerate(rows):
        if isinstance(row[0], str):
            _, arr, width, d = row
            in_specs.append(pl.BlockSpec((tile // d, d * width), lambda i: (i, 0)))
            scratch_shapes.append(pltpu.VMEM((width // LANES, tile, LANES), F32))
        elif row_maps is not None and row_maps[idx] is not None:
            arr, width, _ = row
            in_specs.append(pl.BlockSpec((tile, width), row_maps[idx]))
        else:
            arr, width, cb = row
            in_specs.append(pl.BlockSpec((tile, width), functools.partial(lambda i, cb: (i, cb), cb=cb)))
        args.append(arr)
    in_specs += [pl.BlockSpec(c.shape, lambda i: (0, 0)) for c in consts]
    out_shape, out_specs = [], []
    for spec, d in zip(out_rows, dealt_out):
        w, dt = spec[0], spec[1]
        if d is None:
            out_shape.append(jax.ShapeDtypeStruct((total, w), dt))
            out_specs.append(pl.BlockSpec((tile, w), lambda i: (i, 0)))
        else:
            out_shape.append(jax.ShapeDtypeStruct((total // d, d * w), dt))
            out_specs.append(pl.BlockSpec((tile // d, d * w), lambda i: (i, 0)))
            scratch_shapes.append(pltpu.VMEM((w // LANES, tile, LANES), F32))
    out_shape += [jax.ShapeDtypeStruct(s, F32) for s in out_accs]
    out_specs += [pl.BlockSpec(s, lambda i: (0, 0)) for s in out_accs]
    return pl.pallas_call(
        body, grid=(n,), in_specs=in_specs, out_specs=out_specs, out_shape=out_shape, scratch_shapes=scratch_shapes, name=name,
        compiler_params=_params("arbitrary"),
    )(*args, *consts)


MATMUL_VMEM_BYTES = 40 * 1024 * 1024


def _matmul_tiles(m, n, k, tm, tn, out_bytes):
    tm, tn, tk = min(tm, m), min(tn, n), k

    def need():
        return 2 * (2 * tk * (tm + tn) + tm * tn * out_bytes) + (0 if tk == k else 4 * tm * tn)

    while need() > MATMUL_VMEM_BYTES:
        if tm >= tn and tm % 256 == 0:
            tm //= 2
        elif tn % 256 == 0:
            tn //= 2
        else:
            tk //= 2
    return tm, tn, tk


def _matmul(a, b, *, ta=False, tb=False, outs=(F32,), epilogue=None, extras=(), consts=(), sums=(), whole_rows=False,
            tm=1024, tn=1024, name):
    m, k = (a.shape[1], a.shape[0]) if ta else a.shape
    n = b.shape[0] if tb else b.shape[1]
    assert (b.shape[1] if tb else b.shape[0]) == k
    outs = [o if isinstance(o, tuple) else (o, None) for o in outs]
    out_bytes = sum(jnp.dtype(dt).itemsize for dt, w in outs if w is None) + sum(e.dtype.itemsize for e in extras)
    tm, tn, tk = _matmul_tiles(m, n, k, tm, tn, out_bytes)
    assert m % tm == 0 and n % tn == 0 and k % tk == 0, (name, m, n, k)
    nk = k // tk
    ne, nc, no = len(extras), len(consts), len(outs)
    assert tn == n or not (sums or whole_rows or any(w is not None for _, w in outs)), name
    dims = (((0 if ta else 1,), (1 if tb else 0,)), ((), ()))

    def body(*refs):
        a_ref, b_ref = refs[:2]
        e_refs, o_refs = refs[2:2 + ne + nc], refs[2 + ne + nc:2 + ne + nc + no]
        s_refs = refs[2 + ne + nc + no:2 + ne + nc + no + len(sums)]

        def finish(acc):
            vals = (acc,) if epilogue is None else epilogue(acc, *[e[...] for e in e_refs])
            for ref, val in zip(o_refs, vals[:no]):
                ref[...] = val.astype(ref.dtype)
            for ref, val in zip(s_refs, vals[no:]):
                @pl.when(pl.program_id(0) == 0)
                def _(ref=ref, val=val):
                    ref[...] = val

                @pl.when(pl.program_id(0) > 0)
                def _(ref=ref, val=val):
                    ref[...] += val

        prod = lax.dot_general(a_ref[...], b_ref[...], dims, preferred_element_type=F32)
        if nk == 1:
            finish(prod)
        else:
            acc_ref = refs[-1]
            step = pl.program_id(2)

            @pl.when(step == 0)
            def _():
                acc_ref[...] = prod

            @pl.when(step > 0)
            def _():
                acc_ref[...] += prod

            @pl.when(step == nk - 1)
            def _():
                finish(acc_ref[...])

    a_spec = pl.BlockSpec((tk, tm), lambda i, j, s: (s, i)) if ta else pl.BlockSpec((tm, tk), lambda i, j, s: (i, s))
    b_spec = pl.BlockSpec((tn, tk), lambda i, j, s: (j, s)) if tb else pl.BlockSpec((tk, tn), lambda i, j, s: (s, j))
    tile_spec = pl.BlockSpec((tm, tn), lambda i, j, s: (i, j))
    whole = lambda shape: pl.BlockSpec(shape, lambda i, j, s: (0, 0))
    return pl.pallas_call(
        body, grid=(m // tm, n // tn, nk),
        in_specs=[a_spec, b_spec] + [tile_spec] * ne + [whole(c.shape) for c in consts],
        out_specs=[tile_spec if w is None else pl.BlockSpec((tm, w), lambda i, j, s: (i, 0)) for _, w in outs]
        + [whole(shape) for shape in sums],
        out_shape=[jax.ShapeDtypeStruct((m, n if w is None else w), dt) for dt, w in outs]
        + [jax.ShapeDtypeStruct(shape, F32) for shape in sums],
        scratch_shapes=[pltpu.VMEM((tm, tn), F32)] if nk > 1 else [],
        name=name, compiler_params=_params("arbitrary" if sums else "parallel", "parallel", "arbitrary"),
    )(a, b, *extras, *consts)


def _split(x, parts):
    out = []
    for _ in range(parts):
        out.append(x.astype(BF16))
        x = x - out[-1].astype(F32)
    return out


def _dot_exact(x, e, parts=3):
    return sum(jnp.dot(p, e, preferred_element_type=F32) for p in _split(x, parts))


def _head_expand(heads, width):
    dh = width // heads
    rows = jnp.arange(LANES)[:, None]
    cols = jnp.arange(width)[None, :]
    return (cols // dh == rows).astype(BF16)


def _rstd(x):
    return lax.rsqrt(jnp.mean(x * x, axis=-1, keepdims=True) + EPS)


def _rms_fwd(x, gain, *, name):
    return _rowwise(lambda x, g: x * _rstd(x) * g, [x], [gain], [(x.shape[1], BF16)], name=name)[0]


def _rms_bwd_vals(x, dhn, gain):
    r = _rstd(x)
    xh = x * r
    dxh = dhn * gain
    dx = r * (dxh - xh * jnp.mean(dxh * xh, axis=-1, keepdims=True))
    return dx, jnp.sum(dhn * xh, axis=0, keepdims=True)


def _norm_input_grad(dz, w, x, dres, gain, *, tb=False, name):
    def epilogue(dhn, x, dres, g):
        dx, dg = _rms_bwd_vals(x, dhn, g)
        dh = dres + dx
        return dh, dh, dg
    return _matmul(dz, w, tb=tb, outs=(F32, BF16), extras=(x, dres), consts=(gain,), sums=((1, x.shape[1]),), epilogue=epilogue,
                   name=name)


def _loss_and_grad(h, target, gain, *, name):
    def fn(h, tgt, g):
        r = _rstd(h)
        err = h * r * g - tgt
        loss = 0.5 * jnp.sum(jnp.mean(err * err, axis=-1, keepdims=True), axis=0, keepdims=True)
        dx, dg = _rms_bwd_vals(h, err * (1.0 / D_MODEL), g)
        return dx, dx, jnp.broadcast_to(loss, (1, LANES)), dg
    d = h.shape[1]
    return _rowwise(fn, [h, target], [gain], [(d, F32), (d, BF16)], [(1, LANES), (1, d)], name=name)


def _adamw(w, g, m, v, *, name):
    rows, cols = w.shape
    tile = rows
    for cand in (512, 256, 128, 64, 32, 16, 8):
        if rows % cand == 0 and rows > cand:
            tile = cand
            break
    bc1 = 1.0 - ADAM_B1 ** ADAM_STEP
    bc2 = 1.0 - ADAM_B2 ** ADAM_STEP

    def fn(w, g, m, v):
        m2 = ADAM_B1 * m + (1.0 - ADAM_B1) * g
        v2 = ADAM_B2 * v + (1.0 - ADAM_B2) * (g * g)
        delta = -ADAM_LR * ((m2 / bc1) / (jnp.sqrt(v2 / bc2) + ADAM_EPS) + ADAM_WD * w)
        return delta, m2, v2
    return _rowwise(fn, [w, g, m, v], [], [(cols, F32)] * 3, name=name, tile=tile)


def _attn_specs(arr, width, cb, classes, rows_block, whole, together=1):
    ncols = arr.shape[2] // (classes * width)
    lanes, at = (width, lambda r: r * ncols + cb) if together == 1 else (together * ncols * width, lambda r: r)
    if whole:
        return pl.BlockSpec((1, arr.shape[1], lanes), lambda n, r, i: (n, 0, at(r)))
    return pl.BlockSpec((1, rows_block, lanes), lambda n, r, i: (n, i, at(r)))


def _class_window(refs, spans, together, cl):
    if together == 1:
        return refs
    return [ref.at[:, :, pl.ds((cl * ncols + cb) * width, width)] for ref, (ncols, cb, width) in zip(refs, spans)]


def _attn_tiles(mode, lq, lk, backward=False):
    if mode == "full":
        rows = min(lq, FULL_ROWS)
        return rows, rows, lk
    if mode == "band":
        tq = min(BAND_ROWS if backward else 2 * BAND_ROWS, lq)
        rows = tq if backward else ATT_BLOCK
        return tq, rows, min(rows + ATT_BLOCK, lk)
    rows = CAUSAL_ROWS_BACKWARD if backward else CAUSAL_ROWS
    return rows, rows, ATT_CHUNK


def _window(mode, row0, j, rows, win, lk):
    if mode == "causal":
        return pl.multiple_of(j * win, win), row0 // win + 1
    if mode == "band":
        return pl.multiple_of(jnp.clip(row0 + rows - win, 0, lk - win), ATT_BLOCK), 1
    return 0, 1


def _allowed(mode, row0, start, rows, win, max_dist):
    if mode == "full":
        return None
    dist = (row0 + lax.broadcasted_iota(jnp.int32, (rows, win), 0)) - (start + lax.broadcasted_iota(jnp.int32, (rows, win), 1))
    ok = dist >= 0
    if max_dist is not None:
        ok = ok & (dist <= max_dist)
    return ok


def _head_groups(heads, dh):
    gw = max(LANES, dh)
    return gw, gw // dh, heads // (gw // dh)


def _own_lanes(rows, gw, dh, u):
    return lax.broadcasted_iota(jnp.int32, (rows, gw), 1) // dh == u


def _only_head(x, own, per, u):
    return x if per == 1 else jnp.where(own[u], x, jnp.zeros_like(x))


def _step_scores(qz, kw, kb_row, ok):
    s = lax.dot_general(qz, kw, (((1,), (1,)), ((), ())), preferred_element_type=F32)
    if kb_row is not None:
        s = s - kb_row
    if ok is not None:
        s = jnp.where(ok, s, MASKED)
    return s


def _attn_fwd(q, k, v, kb=None, *, classes=1, heads, dh, mode, max_dist=None, bf16_copy=False, name):
    (qa, qc), (ka, kc), (va, vc) = q, k, v
    n, lq, lk = qa.shape[0], qa.shape[1], ka.shape[1]
    width = heads * dh
    tq, rows, win = _attn_tiles(mode, lq, lk)
    gw, per, groups = _head_groups(heads, dh)
    scale = dh ** -0.5
    has_kb = kb is not None
    single = mode != "causal"
    together = min(classes, max(1, CLASS_ROWS // lq))
    ncols = lambda arr: arr.shape[2] // (classes * width)
    spans = [(ncols(qa), qc, width), (ncols(ka), kc, width), (ncols(va), vc, width), (1, 0, width), (1, 0, LANES), (1, 0, width)]

    def body(*refs):
        for cl in range(together):
            for sub in range(tq // rows):
                part(_class_window(refs, spans, together, cl), pl.program_id(2) * tq + sub * rows, slice(sub * rows, (sub + 1) * rows))

    def part(refs, row0, rs):
        q_ref, k_ref, v_ref = refs[:3]
        kb_ref = refs[3] if has_kb else None
        n_in = 4 if has_kb else 3
        o_ref, lse_ref = refs[n_in:n_in + 2]
        o16_ref = refs[n_in + 2] if bf16_copy else None
        lane = lax.broadcasted_iota(jnp.int32, (rows, LANES), 1)
        own = [_own_lanes(rows, gw, dh, u) for u in range(per)]

        def step(j, carry, masked=True):
            m_in, l_in = carry
            m_out, l_out = m_in, l_in
            start, _ = _window(mode, row0, j, rows, win, lk)
            ok = _allowed(mode, row0, start, rows, win, max_dist) if masked else None
            for g in range(groups):
                cols = slice(g * gw, (g + 1) * gw)
                qg = q_ref[0, rs, cols] * scale
                kw = k_ref[0, pl.ds(start, win), cols]
                vw = v_ref[0, pl.ds(start, win), cols]
                alpha_g = new_g = None
                for u in range(per):
                    h = g * per + u
                    kb_row = kb_ref[0, h, pl.ds(j, 1), :] if has_kb else None
                    s = _step_scores(_only_head(qg, own, per, u), kw, kb_row, ok)
                    m_new = jnp.max(s, axis=1, keepdims=True)
                    if not single:
                        m_old = m_in[:, h:h + 1]
                        m_new = jnp.maximum(m_old, m_new)
                        alpha = jnp.exp(m_old - m_new)
                        alpha_g = alpha if u == 0 else jnp.where(own[u], alpha, alpha_g)
                    p = jnp.exp(s - m_new)
                    l_new = jnp.sum(p, axis=1, keepdims=True)
                    r = jnp.dot(p.astype(BF16), vw, preferred_element_type=F32)
                    if single:
                        r = r * (1.0 / l_new)
                    else:
                        l_new = alpha * l_in[:, h:h + 1] + l_new
                    new_g = r if u == 0 else jnp.where(own[u], r, new_g)
                    m_out = jnp.where(lane == h, m_new, m_out)
                    l_out = jnp.where(lane == h, l_new, l_out)
                o_ref[0, rs, cols] = new_g if single else alpha_g * o_ref[0, rs, cols] + new_g
                if bf16_copy:
                    o16_ref[0, rs, cols] = new_g.astype(BF16)
            return m_out, l_out

        carry = (jnp.full((rows, LANES), MASKED, F32), jnp.zeros((rows, LANES), F32))
        if single:
            m_all, l_all = step(0, carry)
            l_all = jnp.where(lane < heads, l_all, 1.0)
        else:
            o_ref[...] = jnp.zeros(o_ref.shape, F32)
            last = _window(mode, row0, 0, rows, win, lk)[1] - 1
            m_all, l_all = step(last, lax.fori_loop(0, last, functools.partial(step, masked=False), carry))
            l_all = jnp.where(lane < heads, l_all, 1.0)
            inv = 1.0 / l_all
            for g in range(groups):
                cols = slice(g * gw, (g + 1) * gw)
                inv_g = inv[:, g * per:g * per + 1]
                for u in range(1, per):
                    inv_g = jnp.where(own[u], inv[:, g * per + u:g * per + u + 1], inv_g)
                o_ref[0, rs, cols] = o_ref[0, rs, cols] * inv_g
        lse_ref[0, rs, :] = jnp.where(lane < heads, m_all + jnp.log(l_all), 0.0)

    in_specs = [_attn_specs(qa, width, qc, classes, tq, False, together), _attn_specs(ka, width, kc, classes, win, True, together),
                _attn_specs(va, width, vc, classes, win, True, together)]
    args = [qa, ka, va]
    if has_kb:
        assert together == 1
        in_specs.append(pl.BlockSpec((1,) + kb.shape[1:], lambda n_, r, i: (n_, 0, 0, 0)))
        args.append(kb)
    out_shape = [jax.ShapeDtypeStruct((n, lq, classes * width), F32), jax.ShapeDtypeStruct((n, lq, classes * LANES), F32)]
    out_specs = [pl.BlockSpec((1, tq, together * width), lambda n_, r, i: (n_, i, r)),
                 pl.BlockSpec((1, tq, together * LANES), lambda n_, r, i: (n_, i, r))]
    if bf16_copy:
        assert single
        out_shape.append(jax.ShapeDtypeStruct((n, lq, classes * width), BF16))
        out_specs.append(out_specs[0])
    return pl.pallas_call(
        body, grid=(n, classes // together, lq // tq), in_specs=in_specs, out_specs=out_specs, out_shape=out_shape, name=name,
        compiler_params=_params("parallel", "parallel", "arbitrary"),
    )(*args)


def _attn_bwd(q, k, v, do, lse, delta, kb=None, *, classes=1, heads, dh, mode, max_dist=None, dq_dtype=F32, name):
    (qa, qc), (ka, kc), (va, vc), (da, dc) = q, k, v, do
    n, lq, lk = qa.shape[0], qa.shape[1], ka.shape[1]
    width = heads * dh
    tq, rows, win = _attn_tiles(mode, lq, lk, backward=True)
    gw, per, groups = _head_groups(heads, dh)
    scale = dh ** -0.5
    has_kb = kb is not None
    single = mode != "causal"
    nt = (((1,), (1,)), ((), ()))
    tn = (((0,), (0,)), ((), ()))
    together = min(classes, max(1, CLASS_ROWS // lq))
    ncols = lambda arr: arr.shape[2] // (classes * width)
    spans = [(ncols(qa), qc, width), (ncols(ka), kc, width), (ncols(va), vc, width), (ncols(da), dc, width), (1, 0, LANES),
             (1, 0, LANES), (1, 0, width), (1, 0, width), (1, 0, width)]

    def body(*refs):
        n_in = 7 if has_kb else 6
        dk_ref, dv_ref = refs[n_in + 1:n_in + 3]

        @pl.when(pl.program_id(2) == 0)
        def _():
            dk_ref[...] = jnp.zeros(dk_ref.shape, F32)
            dv_ref[...] = jnp.zeros(dv_ref.shape, F32)
            if has_kb:
                refs[n_in + 3][...] = jnp.zeros(refs[n_in + 3].shape, F32)

        for cl in range(together):
            for sub in range(tq // rows):
                part(_class_window(refs, spans, together, cl), pl.program_id(2) * tq + sub * rows, slice(sub * rows, (sub + 1) * rows))

    def part(refs, row0, rs):
        q_ref, k_ref, v_ref, do_ref, lse_ref, dl_ref = refs[:6]
        kb_ref = refs[6] if has_kb else None
        n_in = 7 if has_kb else 6
        dq_ref, dk_ref, dv_ref = refs[n_in:n_in + 3]
        dkb_ref, drow_ref = refs[n_in + 3:n_in + 5] if has_kb else (None, None)
        lse_all = lse_ref[0, rs, :]
        dl_all = dl_ref[0, rs, :]
        lane = lax.broadcasted_iota(jnp.int32, (rows, LANES), 1)
        own = [_own_lanes(rows, gw, dh, u) for u in range(per)]

        def step(j, drow_all, masked=True):
            start, _ = _window(mode, row0, j, rows, win, lk)
            ok = _allowed(mode, row0, start, rows, win, max_dist) if masked else None
            for g in range(groups):
                cols = slice(g * gw, (g + 1) * gw)
                qg = q_ref[0, rs, cols] * scale
                dog = do_ref[0, rs, cols]
                kw = k_ref[0, pl.ds(start, win), cols]
                vw = v_ref[0, pl.ds(start, win), cols]
                dq_g = dk_w = dv_w = None
                for u in range(per):
                    h = g * per + u
                    qz, doz = _only_head(qg, own, per, u), _only_head(dog, own, per, u)
                    kb_row = kb_ref[0, h, pl.ds(j, 1), :] if has_kb else None
                    p = jnp.exp(_step_scores(qz, kw, kb_row, ok) - lse_all[:, h:h + 1])
                    dp = lax.dot_general(doz, vw, nt, preferred_element_type=F32)
                    ds = p * (dp - dl_all[:, h:h + 1])
                    dsb = ds.astype(BF16)
                    r = jnp.dot(dsb, kw, preferred_element_type=F32)
                    dq_g = r if u == 0 else jnp.where(own[u], r, dq_g)
                    dk_u = lax.dot_general(dsb, qz, tn, preferred_element_type=F32)
                    dv_u = lax.dot_general(p.astype(BF16), doz, tn, preferred_element_type=F32)
                    dk_w = dk_u if u == 0 else dk_w + dk_u
                    dv_w = dv_u if u == 0 else dv_w + dv_u
                    if has_kb:
                        dkb_ref[0, h, pl.ds(j, 1), :] += -jnp.sum(ds, axis=0, keepdims=True)
                        drow_all = drow_all + jnp.where(lane == h, jnp.sum(ds, axis=1, keepdims=True), 0.0)
                dk_ref[0, pl.ds(start, win), cols] += dk_w
                dv_ref[0, pl.ds(start, win), cols] += dv_w
                if single:
                    dq_ref[0, rs, cols] = (dq_g * scale).astype(dq_ref.dtype)
                else:
                    dq_ref[0, rs, cols] += dq_g * scale
            return drow_all

        drow_all = jnp.zeros((rows, LANES), F32)
        if single:
            drow_all = step(0, drow_all)
        else:
            dq_ref[...] = jnp.zeros(dq_ref.shape, F32)
            last = _window(mode, row0, 0, rows, win, lk)[1] - 1
            drow_all = step(last, lax.fori_loop(0, last, functools.partial(step, masked=False), drow_all))
        if has_kb:
            drow_ref[0, rs, :] = drow_all

    row_spec = functools.partial(_attn_specs, classes=classes, rows_block=tq, whole=False, together=together)
    in_specs = [row_spec(qa, width, qc), _attn_specs(ka, width, kc, classes, win, True, together),
                _attn_specs(va, width, vc, classes, win, True, together), row_spec(da, width, dc), row_spec(lse, LANES, 0),
                row_spec(delta, LANES, 0)]
    args = [qa, ka, va, da, lse, delta]
    assert single or dq_dtype == F32
    out_shape = [jax.ShapeDtypeStruct((n, lq, classes * width), dq_dtype), jax.ShapeDtypeStruct((n, lk, classes * width), F32),
                 jax.ShapeDtypeStruct((n, lk, classes * width), F32)]
    kv_spec = pl.BlockSpec((1, lk, together * width), lambda n_, r, i: (n_, 0, r))
    out_specs = [pl.BlockSpec((1, tq, together * width), lambda n_, r, i: (n_, i, r)), kv_spec, kv_spec]
    if has_kb:
        assert together == 1
        kb_spec = pl.BlockSpec((1,) + kb.shape[1:], lambda n_, r, i: (n_, 0, 0, 0))
        in_specs.append(kb_spec)
        args.append(kb)
        out_shape += [jax.ShapeDtypeStruct(kb.shape, F32), jax.ShapeDtypeStruct((n, lq, LANES), F32)]
        out_specs += [kb_spec, pl.BlockSpec((1, tq, LANES), lambda n_, r, i: (n_, i, 0))]
    return pl.pallas_call(
        body, grid=(n, classes // together, lq // tq), in_specs=in_specs, out_specs=out_specs, out_shape=out_shape, name=name,
        compiler_params=_params("parallel", "parallel", "arbitrary"),
    )(*args)


def _rope_tables():
    half = HEAD_DIM // 2
    inv = ROPE_THETA ** (-jnp.arange(half, dtype=F32) / half)
    ang = jnp.arange(SEQ, dtype=F32)[:, None] * inv[None, :]
    cos = jnp.tile(jnp.cos(ang), (1, 2 * ATT_W // HEAD_DIM))
    sin = jnp.tile(jnp.concatenate([-jnp.sin(ang), jnp.sin(ang)], axis=1), (1, ATT_W // HEAD_DIM))
    return cos, sin


def _rotate(x, cos, sin):
    width = x.shape[1]
    lane = lax.broadcasted_iota(jnp.int32, x.shape, 1)
    first_half = (lane % HEAD_DIM) < (HEAD_DIM // 2)
    swapped = jnp.where(first_half, pltpu.roll(x, width - HEAD_DIM // 2, axis=1), pltpu.roll(x, HEAD_DIM // 2, axis=1))
    return x * cos[:, :width] + swapped * sin[:, :width]


def _gelu(x):
    k = math.sqrt(2.0 / math.pi)
    t = jnp.tanh(k * (x + 0.044715 * x * x * x))
    return 0.5 * x * (1.0 + t), t


def _gelu_grad(x, t):
    k = math.sqrt(2.0 / math.pi)
    return 0.5 * (1.0 + t) + 0.5 * x * (1.0 - t * t) * k * (1.0 + 3.0 * 0.044715 * x * x)


def _shift_down(x, halo, s):
    ext = jnp.concatenate([halo, x], axis=0)
    return pltpu.roll(ext, s, axis=0)[8:, :]


def _shift_up(x, halo, s):
    rows = x.shape[0]
    ext = jnp.concatenate([x, halo], axis=0)
    return pltpu.roll(ext, rows + 8 - s, axis=0)[:rows, :]


def _lru_gates(u, halo, cw, cb, wa, ba, wi, bi, lam):
    taps = [_shift_down(u, halo, CONV_WIDTH - 1 - k) for k in range(CONV_WIDTH - 1)] + [u]
    uc = cb + sum(cw[k:k + 1, :] * taps[k] for k in range(CONV_WIDTH))
    ucb = uc.astype(BF16)
    r = jax.nn.sigmoid(jnp.dot(ucb, wa, preferred_element_type=F32) + ba)
    gi = jax.nn.sigmoid(jnp.dot(ucb, wi, preferred_element_type=F32) + bi)
    sp = jnp.maximum(-lam, 0.0) + jnp.log(1.0 + jnp.exp(-jnp.abs(lam)))
    log_a = -LRU_C * r * sp
    a = jnp.exp(log_a)
    x2 = 2.0 * log_a
    one_minus_a2 = jnp.where(x2 > -0.01, -x2 * (1.0 + x2 * (0.5 + x2 * (1.0 / 6.0 + x2 / 24.0))), 1.0 - jnp.exp(x2))
    mult = jnp.sqrt(one_minus_a2)
    return taps, uc, ucb, r, gi, sp, a, mult


def _lru_specs(nchunk, reverse):
    def chunk(b, c):
        return b * nchunk + ((nchunk - 1 - c) if reverse else c)

    def halo_before(b, c):
        return jnp.maximum(chunk(b, c) * (LRU_CHUNK // 8) - 1, 0)

    return chunk, halo_before


def _lru_fwd(zug, cw, cb, wa, ba, wi, bi, lam, *, name):
    total = zug.shape[0]
    nchunk = SEQ // LRU_CHUNK
    w = LRU_WIDTH
    chunk, halo_before = _lru_specs(nchunk, False)

    def body(u_ref, uh_ref, g_ref, cw_ref, cb_ref, wa_ref, ba_ref, wi_ref, bi_ref, lam_ref, y_ref, h_ref, a_sc, x_sc, carry_sc):
        c = pl.program_id(1)
        u = u_ref[...]
        halo = jnp.where(c > 0, uh_ref[...], 0.0)
        _, uc, _, _, gi, _, a, mult = _lru_gates(u, halo, cw_ref[...], cb_ref[...], wa_ref[...], ba_ref[...],
                                                 wi_ref[...], bi_ref[...], lam_ref[...])
        a_sc[...] = a
        x_sc[...] = mult * (gi * uc)

        @pl.when(c == 0)
        def _():
            carry_sc[...] = jnp.zeros(carry_sc.shape, F32)

        def tile_step(t, h):
            r0 = pl.multiple_of(t * 8, 8)
            at = a_sc[pl.ds(r0, 8), :]
            xt = x_sc[pl.ds(r0, 8), :]
            rows = []
            for j in range(8):
                h = at[j:j + 1, :] * h + xt[j:j + 1, :]
                rows.append(h)
            h_ref[pl.ds(r0, 8), :] = jnp.concatenate(rows, axis=0)
            return h

        h_last = lax.fori_loop(0, LRU_CHUNK // 8, tile_step, carry_sc[0:1, :])
        carry_sc[0:1, :] = h_last
        gel, _ = _gelu(g_ref[...])
        y_ref[...] = (h_ref[...] * gel).astype(BF16)

    small = lambda arr: pl.BlockSpec(arr.shape, lambda b, c: (0, 0))
    return pl.pallas_call(
        body, grid=(total // SEQ, nchunk),
        in_specs=[pl.BlockSpec((LRU_CHUNK, w), lambda b, c: (chunk(b, c), 0)),
                  pl.BlockSpec((8, w), lambda b, c: (halo_before(b, c), 0)),
                  pl.BlockSpec((LRU_CHUNK, w), lambda b, c: (chunk(b, c), 1)),
                  small(cw), small(cb), small(wa), small(ba), small(wi), small(bi), small(lam)],
        out_specs=[pl.BlockSpec((LRU_CHUNK, w), lambda b, c: (chunk(b, c), 0))] * 2,
        out_shape=[jax.ShapeDtypeStruct((total, w), BF16), jax.ShapeDtypeStruct((total, w), F32)],
        scratch_shapes=[pltpu.VMEM((LRU_CHUNK, w), F32), pltpu.VMEM((LRU_CHUNK, w), F32), pltpu.VMEM((8, w), F32)],
        name=name, compiler_params=_params("arbitrary", "arbitrary"),
    )(zug, zug, zug, cw, cb, wa, ba, wi, bi, lam)


def _lru_bwd(zug, hl, dy, cw, cb, wa, ba, wi, bi, lam, *, name):
    total = zug.shape[0]
    nchunk = SEQ // LRU_CHUNK
    w = LRU_WIDTH
    chunk, halo_before = _lru_specs(nchunk, True)
    tn = (((0,), (0,)), ((), ()))
    nt = (((1,), (1,)), ((), ()))

    def body(u_ref, uh_ref, g_ref, hl_ref, hh_ref, dy_ref, cw_ref, cb_ref, wa_ref, ba_ref, wi_ref, bi_ref, lam_ref,
             dz_ref, dcw_ref, dcb_ref, dwa_ref, dba_ref, dwi_ref, dbi_ref, dlam_ref,
             a_sc, d_sc, g_sc, gcarry_sc, acarry_sc, duc_sc):
        b, c = pl.program_id(0), pl.program_id(1)
        first_chunk = c == nchunk - 1

        @pl.when((b == 0) & (c == 0))
        def _():
            for ref in (dcw_ref, dcb_ref, dwa_ref, dba_ref, dwi_ref, dbi_ref, dlam_ref):
                ref[...] = jnp.zeros(ref.shape, F32)

        @pl.when(c == 0)
        def _():
            gcarry_sc[...] = jnp.zeros(gcarry_sc.shape, F32)
            acarry_sc[...] = jnp.zeros(acarry_sc.shape, F32)
            duc_sc[...] = jnp.zeros(duc_sc.shape, F32)

        u = u_ref[...]
        halo = jnp.where(first_chunk, 0.0, uh_ref[...])
        cw, wa, wi, lam = cw_ref[...], wa_ref[...], wi_ref[...], lam_ref[...]
        taps, uc, ucb, r, gi, sp, a, mult = _lru_gates(u, halo, cw, cb_ref[...], wa, ba_ref[...], wi, bi_ref[...], lam)
        gate = g_ref[...]
        gel, th = _gelu(gate)
        dy = dy_ref[...]
        hl = hl_ref[...]
        a_sc[...] = a
        d_sc[...] = dy * gel
        dgate = dy * hl * _gelu_grad(gate, th)

        def tile_step(t, carry):
            g_next, a_next = carry
            r0 = pl.multiple_of((LRU_CHUNK // 8 - 1 - t) * 8, 8)
            dt = d_sc[pl.ds(r0, 8), :]
            at = a_sc[pl.ds(r0, 8), :]
            rows = [None] * 8
            for j in reversed(range(8)):
                g_next = dt[j:j + 1, :] + a_next * g_next
                a_next = at[j:j + 1, :]
                rows[j] = g_next
            g_sc[pl.ds(r0, 8), :] = jnp.concatenate(rows, axis=0)
            return g_next, a_next

        g_last, a_last = lax.fori_loop(0, LRU_CHUNK // 8, tile_step, (gcarry_sc[0:1, :], acarry_sc[0:1, :]))
        gcarry_sc[0:1, :] = g_last
        acarry_sc[0:1, :] = a_last

        gs = g_sc[...]
        hl_halo = jnp.where(first_chunk, 0.0, hh_ref[...])
        da = gs * _shift_down(hl, hl_halo, 1)
        dmult = gs * gi * uc
        dgi = gs * mult * uc
        duc = gs * mult * gi
        dlog_a = da * a - dmult * a * a / mult
        dr = dlog_a * (-LRU_C * sp)
        dsp = jnp.sum(dlog_a * (-LRU_C * r), axis=0, keepdims=True)
        dlam_ref[...] += -dsp * jax.nn.sigmoid(-lam)
        dpa = dr * r * (1.0 - r)
        dpi = dgi * gi * (1.0 - gi)
        dpab, dpib = dpa.astype(BF16), dpi.astype(BF16)
        dba_ref[...] += jnp.sum(dpa, axis=0, keepdims=True)
        dbi_ref[...] += jnp.sum(dpi, axis=0, keepdims=True)
        dwa_ref[...] += lax.dot_general(ucb, dpab, tn, preferred_element_type=F32)
        dwi_ref[...] += lax.dot_general(ucb, dpib, tn, preferred_element_type=F32)
        duc = duc + lax.dot_general(dpab, wa, nt, preferred_element_type=F32)
        duc = duc + lax.dot_general(dpib, wi, nt, preferred_element_type=F32)
        dcb_ref[...] += jnp.sum(duc, axis=0, keepdims=True)
        dcw_ref[...] += jnp.concatenate([jnp.sum(duc * taps[k], axis=0, keepdims=True) for k in range(CONV_WIDTH)], axis=0)
        after = duc_sc[...]
        du = cw[CONV_WIDTH - 1:CONV_WIDTH, :] * duc
        for k in range(CONV_WIDTH - 1):
            du = du + cw[k:k + 1, :] * _shift_up(duc, after, CONV_WIDTH - 1 - k)
        duc_sc[...] = duc[0:8, :]
        dz_ref[:, 0:w] = du.astype(BF16)
        dz_ref[:, w:2 * w] = dgate.astype(BF16)

    small = lambda arr: pl.BlockSpec(arr.shape, lambda b, c: (0, 0))
    acc = lambda shape: pl.BlockSpec(shape, lambda b, c: (0, 0))
    chunk_spec = lambda cb_: pl.BlockSpec((LRU_CHUNK, w), lambda b, c: (chunk(b, c), cb_))
    halo_spec = pl.BlockSpec((8, w), lambda b, c: (halo_before(b, c), 0))
    acc_shapes = [(CONV_WIDTH, w), (1, w), (w, w), (1, w), (w, w), (1, w), (1, w)]
    return pl.pallas_call(
        body, grid=(total // SEQ, nchunk),
        in_specs=[chunk_spec(0), halo_spec, chunk_spec(1), chunk_spec(0), halo_spec, chunk_spec(0),
                  small(cw), small(cb), small(wa), small(ba), small(wi), small(bi), small(lam)],
        out_specs=[pl.BlockSpec((LRU_CHUNK, 2 * w), lambda b, c: (chunk(b, c), 0))] + [acc(s) for s in acc_shapes],
        out_shape=[jax.ShapeDtypeStruct((total, 2 * w), BF16)] + [jax.ShapeDtypeStruct(s, F32) for s in acc_shapes],
        scratch_shapes=[pltpu.VMEM((LRU_CHUNK, w), F32)] * 3 + [pltpu.VMEM((8, w), F32)] * 3,
        name=name, compiler_params=_params("arbitrary", "arbitrary"),
    )(zug, zug, zug, hl, hl, dy, cw, cb, wa, ba, wi, bi, lam)


def _block_diag(wb):
    eye = jnp.eye(LRU_BLOCKS, dtype=wb.dtype)
    return jnp.einsum("gcd,gh->gchd", wb, eye).reshape(LRU_WIDTH, LRU_WIDTH).astype(BF16)


def _diag_blocks(wd):
    blk = LRU_WIDTH // LRU_BLOCKS
    return jnp.stack([wd[g * blk:(g + 1) * blk, g * blk:(g + 1) * blk] for g in range(LRU_BLOCKS)])


FOX_SCAN = 256


def _fox_bias(fl, bf, *, name):
    nb = fl.shape[0]

    def body(fl_ref, bf_ref, c_ref):
        x = fl_ref[0] + bf_ref[...]
        logf = jnp.minimum(x, 0.0) - jnp.log(1.0 + jnp.exp(-jnp.abs(x)))
        upper = (lax.broadcasted_iota(jnp.int32, (FOX_SCAN, FOX_SCAN), 0)
                 <= lax.broadcasted_iota(jnp.int32, (FOX_SCAN, FOX_SCAN), 1)).astype(BF16)
        carry = jnp.zeros((LRU_BLOCKS, 1), F32)
        for j in range(SEQ // FOX_SCAN):
            blk = logf[:, j * FOX_SCAN:(j + 1) * FOX_SCAN]
            c_ref[0, :, j * FOX_SCAN:(j + 1) * FOX_SCAN] = _dot_exact(blk, upper) + carry
            carry = carry + jnp.sum(blk, axis=1, keepdims=True)

    return pl.pallas_call(
        body, grid=(nb,),
        in_specs=[pl.BlockSpec((1, 8, SEQ), lambda b: (b, 0, 0)), pl.BlockSpec((8, 1), lambda b: (0, 0))],
        out_specs=pl.BlockSpec((1, 8, SEQ), lambda b: (b, 0, 0)),
        out_shape=jax.ShapeDtypeStruct((nb, 8, SEQ), F32), name=name, compiler_params=_params("arbitrary"),
    )(fl, bf)


def _fox_bias_bwd(fl, bf, dc, *, name):
    nb = fl.shape[0]

    def body(fl_ref, bf_ref, dc_ref, dfl_ref, dbf_ref):
        @pl.when(pl.program_id(0) == 0)
        def _():
            dbf_ref[...] = jnp.zeros(dbf_ref.shape, F32)

        x = fl_ref[0] + bf_ref[...]
        dc_all = dc_ref[0]
        lower = (lax.broadcasted_iota(jnp.int32, (FOX_SCAN, FOX_SCAN), 0)
                 >= lax.broadcasted_iota(jnp.int32, (FOX_SCAN, FOX_SCAN), 1)).astype(BF16)
        carry = jnp.zeros((LRU_BLOCKS, 1), F32)
        total = jnp.zeros((LRU_BLOCKS, 1), F32)
        for j in reversed(range(SEQ // FOX_SCAN)):
            cols = slice(j * FOX_SCAN, (j + 1) * FOX_SCAN)
            blk = dc_all[:, cols]
            dlogf = _dot_exact(blk, lower) + carry
            carry = carry + jnp.sum(blk, axis=1, keepdims=True)
            dx = dlogf * jax.nn.sigmoid(-x[:, cols])
            dfl_ref[0, :, cols] = dx
            total = total + jnp.sum(dx, axis=1, keepdims=True)
        dbf_ref[...] += total

    return pl.pallas_call(
        body, grid=(nb,),
        in_specs=[pl.BlockSpec((1, 8, SEQ), lambda b: (b, 0, 0)), pl.BlockSpec((8, 1), lambda b: (0, 0)),
                  pl.BlockSpec((1, 8, SEQ), lambda b: (b, 0, 0))],
        out_specs=[pl.BlockSpec((1, 8, SEQ), lambda b: (b, 0, 0)), pl.BlockSpec((8, 1), lambda b: (0, 0))],
        out_shape=[jax.ShapeDtypeStruct((nb, 8, SEQ), F32), jax.ShapeDtypeStruct((8, 1), F32)],
        name=name, compiler_params=_params("arbitrary"),
    )(fl, bf, dc)


def _seq3(a, nb):
    return a.reshape(nb, a.shape[0] // nb, a.shape[1])


def _flat2(a):
    return a.reshape(a.shape[0] * a.shape[1], a.shape[2])


def _residual_out(y, w, h, next_gain, *, name):
    if next_gain is None:
        out, = _matmul(y, w, extras=(h,), epilogue=lambda acc, res: (acc + res,), name=name)
        return out, None

    def epilogue(acc, res, g):
        out = acc + res
        return out, out * _rstd(out) * g
    return _matmul(y, w, outs=(F32, BF16), extras=(h,), consts=(next_gain,), epilogue=epilogue, whole_rows=True, name=name)


def _mlp_fwd(h, hn, p, tag, next_gain):
    act, = _matmul(hn, p["w_up_t"], tb=True, outs=(BF16,), epilogue=lambda acc: (jnp.square(jnp.maximum(acc, 0.0)),),
                   tn=D_FF // 2, name=f"{tag}_up")
    out, hn_next = _residual_out(act, p["w_down"], h, next_gain, name=f"{tag}_down")
    return out, hn_next, (h, hn, act)


def _mlp_bwd(dh, dhb, p, saved, tag):
    h, hn, act = saved
    dup, = _matmul(dhb, p["w_down"], tb=True, outs=(BF16,), extras=(act,),
                   epilogue=lambda acc, act: (acc * (2.0 * jnp.sqrt(act).astype(F32)),), tn=D_FF // 2, name=f"{tag}_bwd_dup")
    dw_down, = _matmul(act, dhb, ta=True, outs=(BF16,),name=f"{tag}_bwd_dwdown")
    dw_up_t, = _matmul(dup, hn, ta=True, outs=(BF16,),name=f"{tag}_bwd_dwup")
    dh2, dh2b, dg = _norm_input_grad(dup, p["w_up_t"], h, dh, p["norm"], name=f"{tag}_bwd_dh")
    return dh2, dh2b, {"norm": dg, "w_up_t": dw_up_t, "w_down": dw_down}


def _xa_fwd(h, hn, mem, p, tag, nb, next_gain):
    mem_n = _rms_fwd(mem, p["mem_norm"], name=f"{tag}_memnorm")
    q, = _matmul(hn, p["w_q"], outs=(BF16,), name=f"{tag}_q")
    kv, = _matmul(mem_n, p["w_kv_t"], tb=True, outs=(BF16,), tn=2 * D_MODEL, name=f"{tag}_kv")
    q3, kv3 = _seq3(q, nb), _seq3(kv, nb)
    o, lse, ob = _attn_fwd((q3, 0), (kv3, 0), (kv3, 1), heads=XA_HEADS, dh=XA_HEAD_DIM, mode="full", bf16_copy=True,
                           name=f"{tag}_attn")
    o, ob = _flat2(o), _flat2(ob)
    out, hn_next = _residual_out(ob, p["w_o"], h, next_gain, name=f"{tag}_o")
    return out, hn_next, (h, hn, mem_n, q3, kv3, o, ob, lse)


def _xa_bwd(dh, dhb, mem, p, saved, tag, nb):
    h, hn, mem_n, q3, kv3, o, ob, lse = saved
    dob, delta = _matmul(dhb, p["w_o"], tb=True, outs=(BF16, (F32, LANES)), extras=(o,), consts=(_head_expand(XA_HEADS, D_MODEL).T,),
                         epilogue=lambda acc, o, e: (acc, _dot_exact(acc * o, e, parts=2)), name=f"{tag}_bwd_do")
    dw_o, = _matmul(ob, dhb, ta=True, outs=(BF16,),name=f"{tag}_bwd_dwo")
    dq, dk, dv = _attn_bwd((q3, 0), (kv3, 0), (kv3, 1), (_seq3(dob, nb), 0), lse, _seq3(delta, nb), heads=XA_HEADS,
                           dh=XA_HEAD_DIM, mode="full", dq_dtype=BF16, name=f"{tag}_bwd_attn")
    dqb = _flat2(dq)
    dkvb, = _rowwise(lambda a, b: jnp.concatenate([a, b], axis=1), [_flat2(dk), _flat2(dv)], [], [(2 * D_MODEL, BF16)],
                     name=f"{tag}_bwd_dkvcast")
    dw_q, = _matmul(hn, dqb, ta=True, outs=(BF16,),name=f"{tag}_bwd_dwq")
    dw_kv_t, = _matmul(dkvb, mem_n, ta=True, outs=(BF16,),name=f"{tag}_bwd_dwkv")
    dmem_n, = _matmul(dkvb, p["w_kv_t"], name=f"{tag}_bwd_dmemn")
    dg_mem, = _rowwise(lambda x, d, g: _rms_bwd_vals(x, d, g)[1], [mem, dmem_n], [p["mem_norm"]], [], [(1, D_MODEL)],
                       name=f"{tag}_bwd_memnorm")
    dh2, dh2b, dg = _norm_input_grad(dqb, p["w_q"], h, dh, p["norm"], tb=True, name=f"{tag}_bwd_dh")
    return dh2, dh2b, {"norm": dg, "mem_norm": dg_mem, "w_q": dw_q, "w_kv_t": dw_kv_t, "w_o": dw_o}


AB_MAIN = 2 * LRU_WIDTH + 3 * ATT_W


def _ab_fwd(h, hn, p, nb, next_gain, before_out=None):
    w_in_t = p["w_in_t"]
    zug, = _matmul(hn, w_in_t[:2 * LRU_WIDTH], tb=True, name="ab_in_ug")
    qkv, = _matmul(hn, w_in_t[2 * LRU_WIDTH:AB_MAIN], tb=True, outs=(BF16,), tn=1536, name="ab_in_qkv")
    zf, = _matmul(hn, w_in_t[AB_MAIN:], tb=True, name="ab_in_f")
    fl = zf[:, :8].reshape(nb, SEQ, 8).transpose(0, 2, 1)
    cbias = _fox_bias(fl, p["b_f"], name="ab_fox_bias")
    kb = cbias.reshape(nb, 8, SEQ // ATT_CHUNK, ATT_CHUNK)
    y_a, hl = _lru_fwd(zug, p["conv_w"], p["conv_b"], p["w_a"], p["b_a"], p["w_i"], p["b_i"], p["lam"], name="ab_lru")
    qkv3 = _seq3(qkv, nb)
    o, lse = _attn_fwd((qkv3, 0), (qkv3, 1), (qkv3, 2), kb, heads=8, dh=HEAD_DIM, mode="causal", name="ab_fox")
    o = _flat2(o)
    y, = _rowwise(lambda a, b: jnp.concatenate([a, b.astype(BF16)], axis=1), [y_a, o], [], [(D_MODEL, BF16)], name="ab_ycat")
    if before_out is not None:
        before_out(y)
    out, hn_next = _residual_out(y, p["w_out"], h, next_gain, name="ab_out")
    return out, hn_next, (h, hn, zug, qkv3, fl, kb, hl, o, lse, y)


def _ab_bwd(dh, dhb, p, saved, nb):
    h, hn, zug, qkv3, fl, kb, hl, o, lse, y = saved
    dy, dyb, delta = _matmul(dhb, p["w_out"], tb=True, outs=(F32, BF16, (F32, LANES)), extras=(y,), consts=(_head_expand(8, ATT_W).T,),
                             epilogue=lambda acc, y, e: (acc, acc, _dot_exact(acc[:, ATT_W:] * y[:, ATT_W:].astype(F32), e, parts=2)),
                             name="ab_bwd_dy")
    dw_out, = _matmul(y, dhb, ta=True, outs=(BF16,),name="ab_bwd_dwout")
    dzug, dcw, dcb, dwa, dba, dwi, dbi, dlam = _lru_bwd(zug, hl, dy, p["conv_w"], p["conv_b"], p["w_a"], p["b_a"],
                                                        p["w_i"], p["b_i"], p["lam"], name="ab_bwd_lru")
    dq, dk, dv, dkb, drow = _attn_bwd((qkv3, 0), (qkv3, 1), (qkv3, 2), (_seq3(dyb, nb), 1), lse, _seq3(delta, nb), kb,
                                      heads=8, dh=HEAD_DIM, mode="causal", name="ab_bwd_fox")
    dcum = dkb.reshape(nb, 8, SEQ) + drow[:, :, :8].transpose(0, 2, 1)
    dfl, dbf = _fox_bias_bwd(fl, p["b_f"], dcum, name="ab_bwd_fox_bias")
    dzf = jnp.pad(dfl.transpose(0, 2, 1).reshape(nb * SEQ, 8), ((0, 0), (0, LANES - 8)))
    dz, = _rowwise(lambda a, q, k, v, f: jnp.concatenate([a, q.astype(BF16), k.astype(BF16), v.astype(BF16), f.astype(BF16)], axis=1),
                   [dzug, _flat2(dq), _flat2(dk), _flat2(dv), dzf], [], [(AB_MAIN + LANES, BF16)], name="ab_bwd_dzcat")
    dw_in_t, = _matmul(dz, hn, ta=True, outs=(BF16,),tm=384, name="ab_bwd_dwin")
    dh2, dh2b, dg = _norm_input_grad(dz, p["w_in_t"], h, dh, p["norm"], name="ab_bwd_dh")
    grads = {"norm": dg, "w_in_t": dw_in_t[:AB_MAIN + 8], "conv_w": dcw, "conv_b": dcb, "w_a": _diag_blocks(dwa), "b_a": dba,
             "w_i": _diag_blocks(dwi), "b_i": dbi, "lam": dlam, "b_f": dbf.reshape(1, 8), "w_out": dw_out}
    return dh2, dh2b, grads


CD_IN = 3 * ATT_W + ATT_W + 2 * SWA_KW


def _gqa_share():
    src = jnp.arange(SWA_KW)[:, None]
    dst = jnp.arange(ATT_W)[None, :]
    per_kv = ATT_W // (SWA_KW // HEAD_DIM)
    return ((dst // per_kv == src // HEAD_DIM) & (dst % HEAD_DIM == src % HEAD_DIM)).astype(BF16)


def _cd_fwd(h, hn, p, nb, next_gain):
    z, = _matmul(hn, p["w_in_t"], tb=True, tn=2304, name="cd_in")
    cos, sin = _rope_tables()
    per_seq = SEQ // ROW_TILE
    table_map = lambda i: (i % per_seq, 0)

    def rope_fn(z, cos, sin, share):
        qc, kc, vc = z[:, 0:ATT_W], z[:, ATT_W:2 * ATT_W], z[:, 2 * ATT_W:3 * ATT_W]
        qd, kd, vd = z[:, 3 * ATT_W:4 * ATT_W], z[:, 4 * ATT_W:4 * ATT_W + SWA_KW], z[:, 4 * ATT_W + SWA_KW:]
        kd8 = jnp.dot(_rotate(kd, cos, sin).astype(BF16), share, preferred_element_type=F32)
        vd8 = jnp.dot(vd.astype(BF16), share, preferred_element_type=F32)
        qk = jnp.concatenate([_rotate(qc, cos, sin), _rotate(kc, cos, sin)], axis=1)
        return qk, vc, _rotate(qd, cos, sin), kd8, vd8, qk, qk, vc, vc
    dils = [dil for _, dil in DIL_PATTERN if dil > 1]
    outs = _rowwise(rope_fn, [z, cos, sin], [_gqa_share()],
                    [(2 * ATT_W, BF16)] + [(ATT_W, BF16)] * 4 + [(2 * ATT_W, BF16, d) for d in dils] + [(ATT_W, BF16, d) for d in dils],
                    name="cd_rope", row_maps=[None, table_map, table_map])
    qk, vc, qd, kd, vd = outs[:5]
    views = {1: (_seq3(qk, nb), _seq3(vc, nb))}
    for d, qk_d, vc_d in zip(dils, outs[5:5 + len(dils)], outs[5 + len(dils):]):
        views[d] = (_seq3(qk_d, nb), _seq3(vc_d, nb))
    in_classes = lambda a, width, d: _flat2(a) if d == 1 else ("classes", _flat2(a), width, d)
    branch = []
    for window, dil in DIL_PATTERN:
        qk_v, vc_v = views[dil]
        o_i, lse_i = _attn_fwd((qk_v, 0), (qk_v, 1), (vc_v, 0), classes=dil, heads=8, dh=HEAD_DIM, mode="band",
                               max_dist=window // dil, name=f"cd_dil{dil}")
        branch.append((in_classes(o_i, ATT_W, dil), in_classes(lse_i, LANES, dil)))
    expand = _head_expand(8, ATT_W)

    qd3, kd3, vd3 = _seq3(qd, nb), _seq3(kd, nb), _seq3(vd, nb)
    o_d, lse_band = _attn_fwd((qd3, 0), (kd3, 0), (vd3, 0), heads=8, dh=HEAD_DIM, mode="band", max_dist=SWA_WINDOW - 1,
                              name="cd_swa")

    def mix(o1, o2, o3, l1, l2, l3, o_band, l_band, e, sink):
        m = jnp.maximum(jnp.maximum(l1, l2), l3)
        lt = m + jnp.log(jnp.exp(l1 - m) + jnp.exp(l2 - m) + jnp.exp(l3 - m))
        y_c = sum(_dot_exact(jnp.exp(l - lt), e) * o_ for o_, l in ((o1, l1), (o2, l2), (o3, l3)))
        lt_d = jnp.maximum(l_band, sink) + jnp.log(1.0 + jnp.exp(-jnp.abs(l_band - sink)))
        y_d = o_band * _dot_exact(jnp.exp(l_band - lt_d), e)
        return jnp.concatenate([y_c, y_d], axis=1), y_c, lt, y_d, lt_d
    y, y_c, lse_c, y_d, lse_d = _rowwise(
        mix, [b[0] for b in branch] + [b[1] for b in branch] + [_flat2(o_d), _flat2(lse_band)], [expand, p["sink"]],
        [(D_MODEL, BF16), (ATT_W, F32), (LANES, F32), (ATT_W, F32), (LANES, F32)], name="cd_mix")
    out, hn_next = _residual_out(y, p["w_out"], h, next_gain, name="cd_out")
    return out, hn_next, (h, hn, views, qd3, kd3, vd3, y_c, lse_c, y_d, lse_d, y)


def _cd_bwd(dh, dhb, p, saved, nb):
    h, hn, views, qd3, kd3, vd3, y_c, lse_c, y_d, lse_d, y = saved
    dy, dyb = _matmul(dhb, p["w_out"], tb=True, outs=(F32, BF16), epilogue=lambda acc: (acc, acc), name="cd_bwd_dy")
    dw_out, = _matmul(y, dhb, ta=True, outs=(BF16,),name="cd_bwd_dwout")
    dyb3 = _seq3(dyb, nb)
    dils = [dil for _, dil in DIL_PATTERN if dil > 1]
    gather = _head_expand(8, ATT_W).T

    def prepare(do, o, lse, do_d, o_d, lse_d, e, sink):
        delta = _dot_exact(do * o, e, parts=2)
        delta_d = _dot_exact(do_d * o_d, e, parts=2)
        dsink = -jnp.sum(jnp.exp(sink - lse_d) * delta_d, axis=0, keepdims=True)
        return (delta,) + (do,) * len(dils) + (lse,) * len(dils) + (delta,) * len(dils) + (delta_d, dsink)
    outs = _rowwise(prepare, [(dy, ATT_W, 0), y_c, lse_c, (dy, ATT_W, 1), y_d, lse_d], [gather, p["sink"]],
                    [(LANES, F32)] + [(ATT_W, BF16, d) for d in dils] + [(LANES, F32, d) for d in dils] * 2 + [(LANES, F32)],
                    [(1, LANES)], name="cd_bwd_delta")
    delta_d, dsink = outs[-2:]
    seen = {1: ((dyb3, 0), _seq3(lse_c, nb), _seq3(outs[0], nb))}
    for j, d in enumerate(dils):
        seen[d] = ((_seq3(outs[1 + j], nb), 0), _seq3(outs[1 + len(dils) + j], nb), _seq3(outs[1 + 2 * len(dils) + j], nb))
    in_classes = lambda a, d: _flat2(a) if d == 1 else ("classes", _flat2(a), ATT_W, d)
    parts = []
    for window, dil in DIL_PATTERN:
        (qk_v, vc_v), (do_v, lse_v, delta_v) = views[dil], seen[dil]
        grads = _attn_bwd((qk_v, 0), (qk_v, 1), (vc_v, 0), do_v, lse_v, delta_v, classes=dil, heads=8, dh=HEAD_DIM, mode="band",
                          max_dist=window // dil, dq_dtype=BF16, name=f"cd_bwd_dil{dil}")
        parts.append([in_classes(a, dil) for a in grads])
    dqd, dkd, dvd = _attn_bwd((qd3, 0), (kd3, 0), (vd3, 0), (dyb3, 1), _seq3(lse_d, nb), _seq3(delta_d, nb), heads=8,
                              dh=HEAD_DIM, mode="band", max_dist=SWA_WINDOW - 1, dq_dtype=BF16, name="cd_bwd_swa")
    cos, sin = _rope_tables()
    per_seq = SEQ // ROW_TILE
    table_map = lambda i: (i % per_seq, 0)

    def unrope(q1, q2, q3, k1, k2, k3, v1, v2, v3, qd, kd8, vd8, cos, sin, gather):
        back = lambda x: _rotate(x.astype(F32), cos, -sin)
        kd, vd = _dot_exact(kd8, gather), _dot_exact(vd8, gather)
        return jnp.concatenate([back(q1 + q2 + q3), back(k1 + k2 + k3), v1 + v2 + v3, back(qd), back(kd), vd], axis=1)
    ins = [parts[b][t] for t in range(3) for b in range(3)] + [_flat2(dqd), _flat2(dkd), _flat2(dvd), cos, sin]
    dz, = _rowwise(unrope, ins, [_gqa_share().T], [(CD_IN, BF16)], name="cd_bwd_unrope",
                   row_maps=[None] * 12 + [table_map, table_map])
    dw_in_t, = _matmul(dz, hn, ta=True, outs=(BF16,),tm=384, name="cd_bwd_dwin")
    dh2, dh2b, dg = _norm_input_grad(dz, p["w_in_t"], h, dh, p["norm"], name="cd_bwd_dh")
    return dh2, dh2b, {"norm": dg, "w_in_t": dw_in_t, "sink": dsink[:, :8], "w_out": dw_out}


def _local_step(x, mem, target, w, complete=None, grads_ready=None):
    nb = x.shape[0]
    h = x.reshape(nb * SEQ, D_MODEL)
    mem2 = mem.reshape(nb * MEM_LEN, D_MODEL)
    tgt = target.reshape(nb * SEQ, D_MODEL)

    hn = _rms_fwd(h, w["ab"]["norm"], name="ab_norm")
    h, hn, s_ab = _ab_fwd(h, hn, w["ab"], nb, w["xa0"]["norm"], before_out=complete)
    h, hn, s_xa0 = _xa_fwd(h, hn, mem2, w["xa0"], "xa0", nb, w["mlp0"]["norm"])
    h, hn, s_mlp0 = _mlp_fwd(h, hn, w["mlp0"], "mlp0", w["cd"]["norm"])
    h, hn, s_cd = _cd_fwd(h, hn, w["cd"], nb, w["xa1"]["norm"])
    h, hn, s_xa1 = _xa_fwd(h, hn, mem2, w["xa1"], "xa1", nb, w["mlp1"]["norm"])
    h, _, s_mlp1 = _mlp_fwd(h, hn, w["mlp1"], "mlp1", None)

    dh, dhb, loss, dg_final = _loss_and_grad(h, tgt, w["final_norm"], name="loss")
    grads = {"final_norm": dg_final}
    dh, dhb, grads["mlp1"] = _mlp_bwd(dh, dhb, w["mlp1"], s_mlp1, "mlp1")
    dh, dhb, grads["xa1"] = _xa_bwd(dh, dhb, mem2, w["xa1"], s_xa1, "xa1", nb)
    dh, dhb, grads["cd"] = _cd_bwd(dh, dhb, w["cd"], s_cd, nb)
    if grads_ready is not None:
        grads_ready(0, grads)
    dh, dhb, grads["mlp0"] = _mlp_bwd(dh, dhb, w["mlp0"], s_mlp0, "mlp0")
    dh, dhb, grads["xa0"] = _xa_bwd(dh, dhb, mem2, w["xa0"], s_xa0, "xa0", nb)
    if grads_ready is not None:
        grads_ready(1, grads)
    dh, dhb, grads["ab"] = _ab_bwd(dh, dhb, w["ab"], s_ab, nb)
    return loss, dh.reshape(nb, SEQ, D_MODEL), grads


ANY = pl.BlockSpec(memory_space=pl.ANY)


def _place():
    x, y, c = lax.axis_index("x"), lax.axis_index("y"), lax.axis_index("c")
    return x, y, c, [(1 - x, y), (x, 1 - y), (1 - x, 1 - y)]


def _gather_chips(shard):
    half = shard.shape[0] // 2

    def body(src, out, send_sems, recv_sems):
        x, y, c, chips = _place()
        sibling = (x, y, 1 - c)

        def rows(slot, core):
            return out.at[slot, pl.ds(core * half, half)]

        def copy(k, src_ref, dst_ref, to):
            return pltpu.make_async_remote_copy(src_ref=src_ref, dst_ref=dst_ref, send_sem=send_sems.at[k],
                                                recv_sem=recv_sems.at[k], device_id=to, device_id_type=MESH)

        first = [copy(k, src.at[pl.ds(c * half, half)], rows(2 * x + y, c), (px, py, c)) for k, (px, py) in enumerate(chips)]
        for cp in first:
            cp.start()
        passed = [copy(3 + k, rows(2 * px + py, c), rows(2 * px + py, c), sibling) for k, (px, py) in enumerate(chips)]
        for k, (px, py) in enumerate(chips):
            copy(k, src.at[pl.ds(c * half, half)], rows(2 * px + py, c), (px, py, c)).wait_recv()
            passed[k].start()
        for k, (px, py) in enumerate(chips):
            copy(3 + k, rows(2 * px + py, 1 - c), rows(2 * px + py, 1 - c), sibling).wait_recv()
        for cp in first + passed:
            cp.wait_send()

    return pl.pallas_call(
        body, out_shape=jax.ShapeDtypeStruct((N_CHIPS,) + shard.shape, shard.dtype), in_specs=[ANY], out_specs=ANY,
        scratch_shapes=[pltpu.SemaphoreType.DMA((6,)), pltpu.SemaphoreType.DMA((6,))], name="gather_chips",
    )(shard)


HBM = pl.BlockSpec(memory_space=pltpu.HBM)
SEM = pl.BlockSpec(memory_space=pltpu.SEMAPHORE)
SIDE_EFFECT = pltpu.SideEffectType.DATAFLOW_SIDE_EFFECTING


def _chip_copies(src, land, send_sems, recv_sems):
    half = src.shape[0] // 2
    x, y, c, chips = _place()

    def copy(k, slot, to):
        return pltpu.make_async_remote_copy(src_ref=src.at[pl.ds(c * half, half)], dst_ref=land.at[slot, pl.ds(c * half, half)],
                                            send_sem=send_sems[k], recv_sem=recv_sems[k], device_id=to, device_id_type=MESH)
    out = [copy(k, 2 * x + y, (px, py, c)) for k, (px, py) in enumerate(chips)]
    back = [copy(k, 2 * px + py, (px, py, c)) for k, (px, py) in enumerate(chips)]
    return out, back


def _gather_start(shard, after):
    def body(src, land, *rest):
        rest = rest[len(after):]
        sems, token = rest[:6], rest[8]
        out, _ = _chip_copies(src, land, sems[:3], sems[3:])
        for cp in out:
            cp.start()
        token[...] = jnp.zeros(token.shape, F32)

    sem = pltpu.SemaphoreType.DMA(())
    land_shape = (N_CHIPS,) + shard.shape
    return pl.pallas_call(
        body, name="gather_start",
        out_shape=(sem,) * 6 + (pltpu.HBM(shard.shape, shard.dtype), pltpu.HBM(land_shape, shard.dtype),
                                jax.ShapeDtypeStruct((8, LANES), F32)),
        in_specs=(HBM, HBM) + (pl.BlockSpec(memory_space=pl.ANY),) * len(after),
        out_specs=(SEM,) * 6 + (HBM, HBM, pl.BlockSpec(memory_space=pltpu.VMEM)),
        input_output_aliases={0: 6, 1: 7}, compiler_params=pltpu.CompilerParams(has_side_effects=SIDE_EFFECT),
    )(pltpu.with_memory_space_constraint(shard, pltpu.HBM),
      pltpu.with_memory_space_constraint(lax.empty(land_shape, shard.dtype), pltpu.HBM), *after)


def _gather_wait(started, after):
    sems, src_thru, land_thru = started[:6], started[6], started[7]

    def body(src, land, *rest):
        out, back = _chip_copies(src, land, rest[:3], rest[3:6])
        for cp in out:
            cp.wait_send()
        for cp in back:
            cp.wait_recv()

    return pl.pallas_call(
        body, name="gather_wait",
        out_shape=(pltpu.HBM(src_thru.shape, src_thru.dtype), pltpu.HBM(land_thru.shape, land_thru.dtype)),
        in_specs=(HBM, HBM) + (SEM,) * 6 + (pl.BlockSpec(memory_space=pl.ANY),), out_specs=(HBM, HBM),
        input_output_aliases={0: 0, 1: 1}, compiler_params=pltpu.CompilerParams(has_side_effects=SIDE_EFFECT),
    )(src_thru, land_thru, *sems, after)


def _gather_pass(land):
    half = land.shape[1] // 2

    def body(land_in, land_out, send_sems, recv_sems):
        x, y, c, chips = _place()

        def copy(k, slot, core):
            rows = land_out.at[slot, pl.ds(core * half, half)]
            return pltpu.make_async_remote_copy(src_ref=rows, dst_ref=rows, send_sem=send_sems.at[k], recv_sem=recv_sems.at[k],
                                                device_id=(x, y, 1 - c), device_id_type=MESH)
        sends = [copy(k, 2 * px + py, c) for k, (px, py) in enumerate(chips)]
        for cp in sends:
            cp.start()
        for k, (px, py) in enumerate(chips):
            copy(k, 2 * px + py, 1 - c).wait_recv()
        for cp in sends:
            cp.wait_send()

    return pl.pallas_call(
        body, out_shape=jax.ShapeDtypeStruct(land.shape, land.dtype), in_specs=[ANY], out_specs=ANY,
        scratch_shapes=[pltpu.SemaphoreType.DMA((3,)), pltpu.SemaphoreType.DMA((3,))], input_output_aliases={0: 0},
        name="gather_pass",
    )(land)


def _swap_cores(block, *, name, half_rows=None):
    shape = block.shape if half_rows is None else (block.shape[0], half_rows, block.shape[2])

    def body(src, out, send_sem, recv_sem):
        x, y, c, _ = _place()
        part = src if half_rows is None else src.at[:, pl.ds((1 - c) * half_rows, half_rows)]
        cp = pltpu.make_async_remote_copy(src_ref=part, dst_ref=out, send_sem=send_sem, recv_sem=recv_sem,
                                          device_id=(x, y, 1 - c), device_id_type=MESH)
        cp.start()
        cp.wait()

    return pl.pallas_call(
        body, out_shape=jax.ShapeDtypeStruct(shape, block.dtype), in_specs=[ANY], out_specs=ANY,
        scratch_shapes=[pltpu.SemaphoreType.DMA(()), pltpu.SemaphoreType.DMA(())], name=name,
    )(block)


def _scatter_copies(parts, land, send_sems, recv_sems):
    x, y, c, chips = _place()
    return [pltpu.make_async_remote_copy(src_ref=parts.at[2 * px + py], dst_ref=land.at[k], send_sem=send_sems[k],
                                         recv_sem=recv_sems[k], device_id=(px, py, c), device_id_type=MESH)
            for k, (px, py) in enumerate(chips)]


def _scatter_start(parts, tag):
    def body(src, land, *rest):
        for cp in _scatter_copies(src, land, rest[:3], rest[3:6]):
            cp.start()
        rest[8][...] = jnp.zeros(rest[8].shape, F32)

    sem = pltpu.SemaphoreType.DMA(())
    land_shape = (3,) + parts.shape[1:]
    return pl.pallas_call(
        body, name=f"scatter_start_{tag}",
        out_shape=(sem,) * 6 + (pltpu.HBM(parts.shape, parts.dtype), pltpu.HBM(land_shape, parts.dtype),
                                jax.ShapeDtypeStruct((8, LANES), F32)),
        in_specs=(HBM, HBM), out_specs=(SEM,) * 6 + (HBM, HBM, pl.BlockSpec(memory_space=pltpu.VMEM)),
        input_output_aliases={0: 6, 1: 7}, compiler_params=pltpu.CompilerParams(has_side_effects=SIDE_EFFECT),
    )(pltpu.with_memory_space_constraint(parts, pltpu.HBM),
      pltpu.with_memory_space_constraint(lax.empty(land_shape, parts.dtype), pltpu.HBM))


def _scatter_wait(started, after, tag):
    def body(src, land, *rest):
        for cp in _scatter_copies(src, land, rest[:3], rest[3:6]):
            cp.wait_send()
            cp.wait_recv()

    src_thru, land_thru = started[6], started[7]
    return pl.pallas_call(
        body, name=f"scatter_wait_{tag}",
        out_shape=(pltpu.HBM(src_thru.shape, src_thru.dtype), pltpu.HBM(land_thru.shape, land_thru.dtype)),
        in_specs=(HBM, HBM) + (SEM,) * 6 + (pl.BlockSpec(memory_space=pl.ANY),), out_specs=(HBM, HBM),
        input_output_aliases={0: 0, 1: 1}, compiler_params=pltpu.CompilerParams(has_side_effects=SIDE_EFFECT),
    )(src_thru, land_thru, *started[:6], after)[1]


def _sum_all_devices(vec):
    rows = vec.shape[0]

    def body(x_ref, total_ref, all_ref, send_sems, recv_sems, local_sem):
        x, y, c, chips = _place()
        me, sibling = (x, y, c), (x, y, 1 - c)

        def slot(px, py, pc):
            return all_ref.at[4 * px + 2 * py + pc]

        def copy(k, block, to, src=None):
            return pltpu.make_async_remote_copy(src_ref=slot(*block) if src is None else src, dst_ref=slot(*block),
                                                send_sem=send_sems.at[k], recv_sem=recv_sems.at[k], device_id=to,
                                                device_id_type=MESH)

        mine = pltpu.make_async_copy(x_ref, slot(*me), local_sem)
        mine.start()
        first = [copy(0, me, sibling, src=x_ref)] + [copy(1 + j, me, (*chip, c), src=x_ref) for j, chip in enumerate(chips)]
        for cp in first:
            cp.start()
        passed = [copy(4 + j, (*chip, c), sibling) for j, chip in enumerate(chips)]
        for j, chip in enumerate(chips):
            copy(1 + j, (*chip, c), me).wait_recv()
            passed[j].start()
        copy(0, sibling, me).wait_recv()
        for j, chip in enumerate(chips):
            copy(4 + j, (*chip, 1 - c), me).wait_recv()
        for cp in first + passed:
            cp.wait_send()
        mine.wait()
        acc = all_ref[0]
        for d in range(1, 8):
            acc = acc + all_ref[d]
        total_ref[...] = acc

    vmem = pl.BlockSpec(memory_space=pltpu.VMEM)
    return pl.pallas_call(
        body, out_shape=[jax.ShapeDtypeStruct((rows, LANES), F32), jax.ShapeDtypeStruct((8, rows, LANES), F32)],
        in_specs=[vmem], out_specs=[vmem, vmem],
        scratch_shapes=[pltpu.SemaphoreType.DMA((7,)), pltpu.SemaphoreType.DMA((7,)), pltpu.SemaphoreType.DMA(())],
        name="sum_all_devices", compiler_params=pltpu.CompilerParams(vmem_limit_bytes=VMEM_LIMIT_BYTES),
    )(vec)[0]


PACK_COLS = 1024
BIG = (("mlp_w_up", 2, 1024, True), ("mlp_w_down", 2, 1024, False), ("xa_w_kv", 2, 512, True), ("xa_w_q", 2, 256, False),
       ("xa_w_o", 2, 256, False), ("ab_w_out", 1, 256, False), ("cd_w_out", 1, 256, False), ("cd_w_in", 1, 576, True),
       ("ab_w_in", 1, 642, True))
ENTRIES = tuple((name, layer, rows, transposed) for name, layers, rows, transposed in BIG for layer in range(layers))
FIRST_ENTRIES = tuple(e for e in ENTRIES if e[0] == "ab_w_in")
LATER_ENTRIES = tuple(e for e in ENTRIES if e[0] != "ab_w_in")


def _tile_rows(entry):
    return -(-entry[2] // 16) * 16


def _padded_rows(entries):
    return -(-sum(_tile_rows(e) for e in entries) // 32) * 32


SMALL = (("ab_norm", (1, 1024)), ("ab_conv_w", (1, 4, 512)), ("ab_conv_b", (1, 512)), ("lru_w_a", (1, 8, 64, 64)),
         ("lru_b_a", (1, 512)), ("lru_w_i", (1, 8, 64, 64)), ("lru_b_i", (1, 512)), ("lru_lambda", (1, 512)),
         ("fox_b_f", (1, 8)), ("cd_norm", (1, 1024)), ("cd_sink", (1, 8)), ("xa_norm", (2, 1024)),
         ("xa_mem_norm", (2, 1024)), ("mlp_norm", (2, 1024)), ("final_norm", (1024,)), ("loss", (1,)))
SPLIT_SMALL = ("ab_conv_w", "cd_norm")
SPLIT_SMALL_ROWS = 16
assert _padded_rows(FIRST_ENTRIES) - SPLIT_SMALL_ROWS >= sum(_tile_rows(e) for e in FIRST_ENTRIES)


def _pack_rows(parts, entries, dtype):
    lead = parts[0].shape[:-2]
    zeros = lambda rows: jnp.zeros(lead + (rows, PACK_COLS), dtype)
    pieces = [jnp.pad(p.astype(dtype), ((0, 0),) * len(lead) + ((0, _tile_rows(e) - e[2]), (0, 0))) for p, e in zip(parts, entries)]
    tail = _padded_rows(entries) - sum(_tile_rows(e) for e in entries)
    return jnp.concatenate(pieces + ([zeros(tail)] if tail else []), axis=len(lead))


def _unpack_rows(packed, entries):
    out, r0 = [], 0
    for e in entries:
        out.append(packed[..., r0:r0 + e[2], :])
        r0 += _tile_rows(e)
    return out


def _shard_rows(w, entries):
    return [(w[name][layer].T if transposed else w[name][layer]) for name, layer, _, transposed in entries]


def _shard_blocks(rows):
    out = {}
    for (name, layer, _, transposed), r in zip(ENTRIES, rows):
        out.setdefault(name, []).append(r.T if transposed else r)
    return {name: jnp.stack(layers) for name, layers in out.items()}


def _pack_small(vals):
    flat = jnp.concatenate([vals[name].astype(F32).reshape(-1) for name, _ in SMALL])
    size = -(-flat.shape[0] // (8 * LANES)) * (8 * LANES)
    return jnp.pad(flat, (0, size - flat.shape[0])).reshape(-1, LANES)


def _unpack_small(packed):
    flat, out, at = packed.reshape(-1), {}, 0
    for name, shape in SMALL:
        out[name] = flat[at:at + math.prod(shape)].reshape(shape)
        at += math.prod(shape)
    return out


def _chip_part(full, chip):
    width = full.shape[-1] // N_CHIPS
    return lax.dynamic_slice_in_dim(full, chip * width, width, axis=full.ndim - 1)


def _chip_embed(part, chip):
    full = jnp.zeros(part.shape[:-1] + (part.shape[-1] * N_CHIPS,), part.dtype)
    return lax.dynamic_update_slice_in_dim(full, part, chip * part.shape[-1], axis=part.ndim - 1)


SUBLAYER_KEYS = {"ab_w_in": ("ab", "w_in_t"), "ab_w_out": ("ab", "w_out"), "cd_w_in": ("cd", "w_in_t"), "cd_w_out": ("cd", "w_out"),
                 "xa_w_q": ("xa", "w_q"), "xa_w_kv": ("xa", "w_kv_t"), "xa_w_o": ("xa", "w_o"), "mlp_w_up": ("mlp", "w_up_t"),
                 "mlp_w_down": ("mlp", "w_down")}


def _sublayer(name, layer):
    block, leaf = SUBLAYER_KEYS[name]
    return (block if block in ("ab", "cd") else f"{block}{layer}"), leaf


def _set_big(params, full_rows):
    for (name, layer), rows in full_rows.items():
        block, leaf = _sublayer(name, layer)
        if name == "ab_w_in":
            rows = jnp.concatenate([rows, jnp.zeros((LANES - 8, PACK_COLS), rows.dtype)])
        params[block][leaf] = rows


def _build_params(full_rows, w):
    pad = lambda a: jnp.pad(a, ((0, 0), (0, LANES - a.shape[1])))
    params = {
        "ab": dict(norm=w["ab_norm"], conv_w=w["ab_conv_w"][0], conv_b=w["ab_conv_b"], w_a=_block_diag(w["lru_w_a"][0]),
                   b_a=w["lru_b_a"], w_i=_block_diag(w["lru_w_i"][0]), b_i=w["lru_b_i"], lam=w["lru_lambda"],
                   b_f=w["fox_b_f"].reshape(8, 1)),
        "cd": dict(norm=w["cd_norm"], sink=pad(w["cd_sink"])),
        "final_norm": w["final_norm"].reshape(1, D_MODEL),
    }
    for layer in range(2):
        params[f"xa{layer}"] = dict(norm=w["xa_norm"][layer:layer + 1], mem_norm=w["xa_mem_norm"][layer:layer + 1])
        params[f"mlp{layer}"] = dict(norm=w["mlp_norm"][layer:layer + 1])
    _set_big(params, full_rows)
    return params


def _grad_rows(g, entries=ENTRIES):
    out = []
    for name, layer, _, _ in entries:
        block, leaf = _sublayer(name, layer)
        out.append(g[block][leaf])
    return out


def _reduce_group(entry):
    name, layer = entry[0], entry[1]
    if name.startswith("ab_"):
        return 2
    return 0 if layer == 1 or name.startswith("cd_") else 1


REDUCE_GROUPS = tuple(tuple(e for e in ENTRIES if _reduce_group(e) == group) for group in range(3))


def _reduce_tile(half):
    return max(t for t in range(16, 1025, 16) if half % t == 0)


def _reduce_start(grads_by_chip, core, chip, tag):
    half = grads_by_chip.shape[1] // 2
    tile = _reduce_tile(half)
    steps = half // tile
    place = jnp.stack([core, chip]).astype(jnp.int32)
    got = _swap_cores(grads_by_chip, name=f"swap_cores_{tag}", half_rows=half)
    block = (1, tile, PACK_COLS)

    def sum_cores(place_ref, mine_ref, got_ref, f32_ref, bf16_ref):
        total = mine_ref[...].astype(F32) + got_ref[...].astype(F32)
        f32_ref[...] = total
        bf16_ref[...] = total.astype(BF16)

    pair32, pair16 = pl.pallas_call(
        sum_cores, name=f"sum_cores_{tag}",
        grid_spec=pltpu.PrefetchScalarGridSpec(
            num_scalar_prefetch=1, grid=(N_CHIPS, steps),
            in_specs=[pl.BlockSpec(block, lambda s, i, place_ref: (s, place_ref[0] * steps + i, 0)),
                      pl.BlockSpec(block, lambda s, i, place_ref: (s, i, 0))],
            out_specs=[pl.BlockSpec(block, lambda s, i, place_ref: (s, i, 0))] * 2),
        out_shape=[jax.ShapeDtypeStruct((N_CHIPS, half, PACK_COLS), F32), jax.ShapeDtypeStruct((N_CHIPS, half, PACK_COLS), BF16)],
        compiler_params=_params("arbitrary", "arbitrary"),
    )(place, grads_by_chip, got)
    return pair32, _scatter_start(pair16, tag), place


def _reduce_finish(state, core, tag, after):
    pair32, started, place = state
    half = pair32.shape[1]
    tile = _reduce_tile(half)
    landed = _scatter_wait(started, after, tag)

    def sum_chips(place_ref, own_ref, landed_ref, out_ref):
        out_ref[...] = own_ref[0] + landed_ref[0].astype(F32) + landed_ref[1].astype(F32) + landed_ref[2].astype(F32)

    done = pl.pallas_call(
        sum_chips, name=f"sum_chips_{tag}",
        grid_spec=pltpu.PrefetchScalarGridSpec(
            num_scalar_prefetch=1, grid=(half // tile,),
            in_specs=[pl.BlockSpec((1, tile, PACK_COLS), lambda i, place_ref: (place_ref[1], i, 0)),
                      pl.BlockSpec((3, tile, PACK_COLS), lambda i, place_ref: (0, i, 0))],
            out_specs=pl.BlockSpec((tile, PACK_COLS), lambda i, place_ref: (i, 0))),
        out_shape=jax.ShapeDtypeStruct((half, PACK_COLS), F32), compiler_params=_params("arbitrary"),
    )(place, pair32, landed)
    theirs = _swap_cores(done, name=f"share_halves_{tag}")
    return jnp.where(core == 0, jnp.concatenate([done, theirs]), jnp.concatenate([theirs, done]))


def kernel(x, mem, ab_norm, ab_w_in, ab_conv_w, ab_conv_b, lru_w_a, lru_b_a, lru_w_i, lru_b_i, lru_lambda, fox_b_f, ab_w_out, cd_norm, cd_w_in, cd_sink, cd_w_out, xa_norm, xa_mem_norm, xa_w_q, xa_w_kv, xa_w_o, mlp_norm, mlp_w_up, mlp_w_down, final_norm, loss_target, m_ab_norm, m_ab_w_in, m_ab_conv_w, m_ab_conv_b, m_lru_w_a, m_lru_b_a, m_lru_w_i, m_lru_b_i, m_lru_lambda, m_fox_b_f, m_ab_w_out, m_cd_norm, m_cd_w_in, m_cd_sink, m_cd_w_out, m_xa_norm, m_xa_mem_norm, m_xa_w_q, m_xa_w_kv, m_xa_w_o, m_mlp_norm, m_mlp_w_up, m_mlp_w_down, m_final_norm, v_ab_norm, v_ab_w_in, v_ab_conv_w, v_ab_conv_b, v_lru_w_a, v_lru_b_a, v_lru_w_i, v_lru_b_i, v_lru_lambda, v_fox_b_f, v_ab_w_out, v_cd_norm, v_cd_w_in, v_cd_sink, v_cd_w_out, v_xa_norm, v_xa_mem_norm, v_xa_w_q, v_xa_w_kv, v_xa_w_o, v_mlp_norm, v_mlp_w_up, v_mlp_w_down, v_final_norm):
    given = dict(locals())
    weight_names = [name for name, _ in SMALL if name != "loss"] + [name for name, _, _, _ in BIG]
    w = {name: given[name] for name in weight_names}
    chip = 2 * lax.axis_index("x") + lax.axis_index("y")
    core = lax.axis_index("c")

    slot = lax.broadcasted_iota(jnp.int32, (N_CHIPS, 1, 1), 0)

    def whole(entries, mine, gathered):
        return {(name, layer): jnp.where(slot == chip, own[None], got).reshape(N_CHIPS * rows, PACK_COLS)
                for (name, layer, rows, _), own, got in zip(entries, _unpack_rows(mine, entries), _unpack_rows(gathered, entries))}

    bits = lambda a: lax.bitcast_convert_type(a.reshape(-1), BF16).reshape(-1)
    spare = _padded_rows(FIRST_ENTRIES) - SPLIT_SMALL_ROWS
    small_rows = jnp.zeros((SPLIT_SMALL_ROWS, PACK_COLS), BF16)
    small_rows = small_rows.at[0].set(bits(w["ab_conv_w"])).at[1, :2 * w["cd_norm"].size].set(bits(w["cd_norm"]))
    mine_first = _pack_rows(_shard_rows(w, FIRST_ENTRIES), FIRST_ENTRIES, BF16).at[spare:].set(small_rows)
    mine_later = _pack_rows(_shard_rows(w, LATER_ENTRIES), LATER_ENTRIES, BF16)
    first = _gather_chips(mine_first)
    floats = lambda a: lax.bitcast_convert_type(a.reshape(a.shape[:-1] + (-1, 2)), F32)
    all_small = jnp.where(slot == chip, floats(mine_first[None, spare:]), floats(first[:, spare:]))
    taps, per_chip = w["ab_conv_w"].shape[1:]
    conv_w_full = all_small[:, 0].reshape(N_CHIPS, taps, per_chip).transpose(1, 0, 2).reshape(1, taps, N_CHIPS * per_chip)
    cd_norm_full = all_small[:, 1, :w["cd_norm"].size].reshape(1, -1)
    started = _gather_start(mine_later, after=(first,))
    params = _build_params(whole(FIRST_ENTRIES, mine_first, first), {**w, "ab_conv_w": conv_w_full, "cd_norm": cd_norm_full})
    params["ab"]["norm"] = params["ab"]["norm"] + started[8][0, 0]

    def complete(h):
        mine, landed = _gather_wait(started, after=h)
        _set_big(params, whole(LATER_ENTRIES, mine, _gather_pass(landed)))

    reducing = {}

    def grads_ready(group, g):
        entries = REDUCE_GROUPS[group]
        by_chip = _pack_rows([r.reshape(N_CHIPS, e[2], PACK_COLS) for r, e in zip(_grad_rows(g, entries), entries)], entries, BF16)
        reducing[group] = _reduce_start(by_chip, core, chip, f"g{group}")
        if group < 2:
            block, leaf = ("mlp0", "w_down") if group == 0 else ("ab", "w_out")
            params[block][leaf] = params[block][leaf] + reducing[group][1][8][0, 0].astype(BF16)

    loss_part, grad_x, g = _local_step(x, mem, loss_target, params, complete, grads_ready)
    grads_ready(2, g)
    reduced, after = {}, reducing[2][1][8]
    for group, entries in enumerate(REDUCE_GROUPS):
        after = _reduce_finish(reducing[group], core, f"g{group}", after=after)
        reduced.update({(e[0], e[1]): r for e, r in zip(entries, _unpack_rows(after, entries))})
    grads = _shard_blocks([reduced[e[0], e[1]] for e in ENTRIES])
    pair = lambda key, leaf: jnp.stack([g[f"{key}0"][leaf], g[f"{key}1"][leaf]])

    small_local = {"ab_norm": g["ab"]["norm"], "ab_conv_w": g["ab"]["conv_w"], "ab_conv_b": g["ab"]["conv_b"],
                   "lru_w_a": g["ab"]["w_a"], "lru_b_a": g["ab"]["b_a"], "lru_w_i": g["ab"]["w_i"], "lru_b_i": g["ab"]["b_i"],
                   "lru_lambda": g["ab"]["lam"], "fox_b_f": g["ab"]["b_f"], "cd_norm": g["cd"]["norm"], "cd_sink": g["cd"]["sink"],
                   "xa_norm": pair("xa", "norm"), "xa_mem_norm": pair("xa", "mem_norm"), "mlp_norm": pair("mlp", "norm"),
                   "final_norm": g["final_norm"], "loss": loss_part[0, :1]}
    small_sum_packed = _sum_all_devices(_pack_small(small_local))
    small_sum = _unpack_small(small_sum_packed)
    loss = small_sum["loss"][0]

    delta, new_m, new_v = {}, {}, {}
    for name, _, _, _ in BIG:
        shape = w[name].shape
        flat = lambda a: a.reshape(-1, shape[-1])
        d, m2, v2 = _adamw(flat(w[name]), flat(grads[name]), flat(given["m_" + name]), flat(given["v_" + name]), name=f"adamw_{name}")
        delta[name], new_m[name], new_v[name] = d.reshape(shape), m2.reshape(shape), v2.reshape(shape)

    def small_state(prefix):
        vals = {name: (given[prefix + name] if name != "loss" else jnp.zeros((1,), F32)) for name, _ in SMALL}
        for name in SPLIT_SMALL:
            vals[name] = _chip_embed(vals[name], chip)
        return _pack_small(vals)
    packed = _adamw(small_state(""), small_sum_packed, small_state("m_"), small_state("v_"), name="adamw_small")
    for store, vals in zip((delta, new_m, new_v), packed):
        for name, val in _unpack_small(vals).items():
            store[name] = _chip_part(val, chip) if name in SPLIT_SMALL else val
    for name in SPLIT_SMALL:
        small_sum[name] = _chip_part(small_sum[name], chip)
    grads.update({name: small_sum[name] for name, _ in SMALL})

    order = ["ab_norm", "ab_w_in", "ab_conv_w", "ab_conv_b", "lru_w_a", "lru_b_a", "lru_w_i", "lru_b_i", "lru_lambda", "fox_b_f",
             "ab_w_out", "cd_norm", "cd_w_in", "cd_sink", "cd_w_out", "xa_norm", "xa_mem_norm", "xa_w_q", "xa_w_kv", "xa_w_o",
             "mlp_norm", "mlp_w_up", "mlp_w_down", "final_norm"]
    return (loss, grad_x, *[grads[n] for n in order], *[delta[n] for n in order], *[new_m[n] for n in order],
            *[new_v[n] for n in order])
```

```python
import functools
import math

import jax
import jax.numpy as jnp
from jax import lax
from jax.experimental import pallas as pl
from jax.experimental.pallas import tpu as pltpu

F32 = jnp.float32
BF16 = jnp.bfloat16
MESH = pl.DeviceIdType.MESH

D_MODEL = 1024
SEQ = 2048
HEAD_DIM = 64
LRU_WIDTH = 512
LRU_BLOCKS = 8
LRU_C = 8.0
CONV_WIDTH = 4
ATT_W = 512
SWA_KW = 128
SWA_WINDOW = 128
DIL_PATTERN = ((128, 1), (512, 4), (2048, 16))
MEM_LEN = 256
XA_HEADS = 4
XA_HEAD_DIM = 256
D_FF = 4096
ROPE_THETA = 10000.0
EPS = 1e-6
N_CHIPS = 4
LANES = 128

ADAM_LR, ADAM_B1, ADAM_B2, ADAM_EPS, ADAM_WD, ADAM_STEP = 0.001, 0.9, 0.999, 1e-08, 0.01, 10

VMEM_LIMIT_BYTES = 56 * 1024 * 1024
MASKED = -1e30
ROW_TILE = 512
LRU_CHUNK = 512
ATT_BLOCK = 128
BAND_ROWS = 256
FULL_ROWS = 1024
ATT_CHUNK = 512
CAUSAL_ROWS = 256
CAUSAL_ROWS_BACKWARD = 512
CLASS_ROWS = 512


def _params(*sem):
    return pltpu.CompilerParams(dimension_semantics=sem, vmem_limit_bytes=VMEM_LIMIT_BYTES)


def _rowwise(fn, rows, consts=(), out_rows=(), out_accs=(), *, name, tile=ROW_TILE, row_maps=None):
    rows = [r if isinstance(r, tuple) else (r, r.shape[1], 0) for r in rows]
    consts = list(consts)
    nr, nc, no = len(rows), len(consts), len(out_rows)
    dealt_in = [r[3] if isinstance(r[0], str) else None for r in rows]
    dealt_out = [o[2] if len(o) == 3 else None for o in out_rows]
    total = next(r[0].shape[0] for r in rows if not isinstance(r[0], str))
    tile = min(tile, total)
    assert total % tile == 0
    n = total // tile
    n_acc = len(out_accs)

    def body(*refs):
        scratch = list(refs[nr + nc + no + n_acc:])
        vals = []
        for ref, d, row in zip(refs[:nr], dealt_in, rows):
            if d is None:
                vals.append(ref[...])
                continue
            sc, width = scratch.pop(0), row[2]
            for r in range(d):
                for c in range(width // LANES):
                    lanes = slice(r * width + c * LANES, r * width + (c + 1) * LANES)
                    sc.at[c][pl.ds(r, tile // d, stride=d), :] = ref[:, lanes].astype(F32)
            vals.append(jnp.concatenate([sc[c] for c in range(width // LANES)], axis=1))
        outs = fn(*vals, *[r[...] for r in refs[nr:nr + nc]])
        outs = tuple(outs) if isinstance(outs, (tuple, list)) else (outs,)
        for ref, val, d, spec in zip(refs[nr + nc:nr + nc + no], outs[:no], dealt_out, out_rows):
            if d is None:
                ref[...] = val.astype(ref.dtype)
                continue
            sc, width = scratch.pop(0), spec[0]
            for c in range(width // LANES):
                sc[c] = val[:, c * LANES:(c + 1) * LANES].astype(F32)
            for r in range(d):
                for c in range(width // LANES):
                    lanes = slice(r * width + c * LANES, r * width + (c + 1) * LANES)
                    ref[:, lanes] = sc.at[c][pl.ds(r, tile // d, stride=d), :].astype(ref.dtype)
        for ref, val in zip(refs[nr + nc + no:nr + nc + no + n_acc], outs[no:]):
            @pl.when(pl.program_id(0) == 0)
            def _(ref=ref):
                ref[...] = jnp.zeros(ref.shape, ref.dtype)
            ref[...] += val

    in_specs, args, scratch_shapes = [], [], []
    for idx, row in enumerate(rows):
        if isinstance(row[0], str):
            _, arr, width, d = row
            in_specs.append(pl.BlockSpec((tile // d, d * width), lambda i: (i, 0)))
            scratch_shapes.append(pltpu.VMEM((width // LANES, tile, LANES), F32))
        elif row_maps is not None and row_maps[idx] is not None:
            arr, width, _ = row
            in_specs.append(pl.BlockSpec((tile, width), row_maps[idx]))
        else:
            arr, width, cb = row
            in_specs.append(pl.BlockSpec((tile, width), functools.partial(lambda i, cb: (i, cb), cb=cb)))
        args.append(arr)
    in_specs += [pl.BlockSpec(c.shape, lambda i: (0, 0)) for c in consts]
    out_shape, out_specs = [], []
    for spec, d in zip(out_rows, dealt_out):
        w, dt = spec[0], spec[1]
        if d is None:
            out_shape.append(jax.ShapeDtypeStruct((total, w), dt))
            out_specs.append(pl.BlockSpec((tile, w), lambda i: (i, 0)))
        else:
            out_shape.append(jax.ShapeDtypeStruct((total // d, d * w), dt))
            out_specs.append(pl.BlockSpec((tile // d, d * w), lambda i: (i, 0)))
            scratch_shapes.append(pltpu.VMEM((w // LANES, tile, LANES), F32))
    out_shape += [jax.ShapeDtypeStruct(s, F32) for s in out_accs]
    out_specs += [pl.BlockSpec(s, lambda i: (0, 0)) for s in out_accs]
    return pl.pallas_call(
        body, grid=(n,), in_specs=in_specs, out_specs=out_specs, out_shape=out_shape, scratch_shapes=scratch_shapes, name=name,
        compiler_params=_params("arbitrary"),
    )(*args, *consts)


MATMUL_VMEM_BYTES = 40 * 1024 * 1024


def _matmul_tiles(m, n, k, tm, tn, out_bytes):
    tm, tn, tk = min(tm, m), min(tn, n), k

    def need():
        return 2 * (2 * tk * (tm + tn) + tm * tn * out_bytes) + (0 if tk == k else 4 * tm * tn)

    while need() > MATMUL_VMEM_BYTES:
        if tm >= tn and tm % 256 == 0:
            tm //= 2
        elif tn % 256 == 0:
            tn //= 2
        else:
            tk //= 2
    return tm, tn, tk


def _matmul(a, b, *, ta=False, tb=False, outs=(F32,), epilogue=None, extras=(), consts=(), sums=(), whole_rows=False,
            tm=1024, tn=1024, name):
    m, k = (a.shape[1], a.shape[0]) if ta else a.shape
    n = b.shape[0] if tb else b.shape[1]
    assert (b.shape[1] if tb else b.shape[0]) == k
    outs = [o if isinstance(o, tuple) else (o, None) for o in outs]
    out_bytes = sum(jnp.dtype(dt).itemsize for dt, w in outs if w is None) + sum(e.dtype.itemsize for e in extras)
    tm, tn, tk = _matmul_tiles(m, n, k, tm, tn, out_bytes)
    assert m % tm == 0 and n % tn == 0 and k % tk == 0, (name, m, n, k)
    nk = k // tk
    ne, nc, no = len(extras), len(consts), len(outs)
    assert tn == n or not (sums or whole_rows or any(w is not None for _, w in outs)), name
    dims = (((0 if ta else 1,), (1 if tb else 0,)), ((), ()))

    def body(*refs):
        a_ref, b_ref = refs[:2]
        e_refs, o_refs = refs[2:2 + ne + nc], refs[2 + ne + nc:2 + ne + nc + no]
        s_refs = refs[2 + ne + nc + no:2 + ne + nc + no + len(sums)]

        def finish(acc):
            vals = (acc,) if epilogue is None else epilogue(acc, *[e[...] for e in e_refs])
            for ref, val in zip(o_refs, vals[:no]):
                ref[...] = val.astype(ref.dtype)
            for ref, val in zip(s_refs, vals[no:]):
                @pl.when(pl.program_id(0) == 0)
                def _(ref=ref, val=val):
                    ref[...] = val

                @pl.when(pl.program_id(0) > 0)
                def _(ref=ref, val=val):
                    ref[...] += val

        prod = lax.dot_general(a_ref[...], b_ref[...], dims, preferred_element_type=F32)
        if nk == 1:
            finish(prod)
        else:
            acc_ref = refs[-1]
            step = pl.program_id(2)

            @pl.when(step == 0)
            def _():
                acc_ref[...] = prod

            @pl.when(step > 0)
            def _():
                acc_ref[...] += prod

            @pl.when(step == nk - 1)
            def _():
                finish(acc_ref[...])

    a_spec = pl.BlockSpec((tk, tm), lambda i, j, s: (s, i)) if ta else pl.BlockSpec((tm, tk), lambda i, j, s: (i, s))
    b_spec = pl.BlockSpec((tn, tk), lambda i, j, s: (j, s)) if tb else pl.BlockSpec((tk, tn), lambda i, j, s: (s, j))
    tile_spec = pl.BlockSpec((tm, tn), lambda i, j, s: (i, j))
    whole = lambda shape: pl.BlockSpec(shape, lambda i, j, s: (0, 0))
    return pl.pallas_call(
        body, grid=(m // tm, n // tn, nk),
        in_specs=[a_spec, b_spec] + [tile_spec] * ne + [whole(c.shape) for c in consts],
        out_specs=[tile_spec if w is None else pl.BlockSpec((tm, w), lambda i, j, s: (i, 0)) for _, w in outs]
        + [whole(shape) for shape in sums],
        out_shape=[jax.ShapeDtypeStruct((m, n if w is None else w), dt) for dt, w in outs]
        + [jax.ShapeDtypeStruct(shape, F32) for shape in sums],
        scratch_shapes=[pltpu.VMEM((tm, tn), F32)] if nk > 1 else [],
        name=name, compiler_params=_params("arbitrary" if sums else "parallel", "parallel", "arbitrary"),
    )(a, b, *extras, *consts)


def _split(x, parts):
    out = []
    for _ in range(parts):
        out.append(x.astype(BF16))
        x = x - out[-1].astype(F32)
    return out


def _dot_exact(x, e, parts=3):
    return sum(jnp.dot(p, e, preferred_element_type=F32) for p in _split(x, parts))


def _head_expand(heads, width):
    dh = width // heads
    rows = jnp.arange(LANES)[:, None]
    cols = jnp.arange(width)[None, :]
    return (cols // dh == rows).astype(BF16)


def _rstd(x):
    return lax.rsqrt(jnp.mean(x * x, axis=-1, keepdims=True) + EPS)


def _rms_fwd(x, gain, *, name):
    return _rowwise(lambda x, g: x * _rstd(x) * g, [x], [gain], [(x.shape[1], BF16)], name=name)[0]


def _rms_bwd_vals(x, dhn, gain):
    r = _rstd(x)
    xh = x * r
    dxh = dhn * gain
    dx = r * (dxh - xh * jnp.mean(dxh * xh, axis=-1, keepdims=True))
    return dx, jnp.sum(dhn * xh, axis=0, keepdims=True)


def _norm_input_grad(dz, w, x, dres, gain, *, tb=False, name):
    def epilogue(dhn, x, dres, g):
        dx, dg = _rms_bwd_vals(x, dhn, g)
        dh = dres + dx
        return dh, dh, dg
    return _matmul(dz, w, tb=tb, outs=(F32, BF16), extras=(x, dres), consts=(gain,), sums=((1, x.shape[1]),), epilogue=epilogue,
                   name=name)


def _loss_and_grad(h, target, gain, *, name):
    def fn(h, tgt, g):
        r = _rstd(h)
        err = h * r * g - tgt
        loss = 0.5 * jnp.sum(jnp.mean(err * err, axis=-1, keepdims=True), axis=0, keepdims=True)
        dx, dg = _rms_bwd_vals(h, err * (1.0 / D_MODEL), g)
        return dx, dx, jnp.broadcast_to(loss, (1, LANES)), dg
    d = h.shape[1]
    return _rowwise(fn, [h, target], [gain], [(d, F32), (d, BF16)], [(1, LANES), (1, d)], name=name)


def _adamw(w, g, m, v, *, name):
    rows, cols = w.shape
    tile = rows
    for cand in (512, 256, 128, 64, 32, 16, 8):
        if rows % cand == 0 and rows > cand:
            tile = cand
            break
    bc1 = 1.0 - ADAM_B1 ** ADAM_STEP
    bc2 = 1.0 - ADAM_B2 ** ADAM_STEP

    def fn(w, g, m, v):
        m2 = ADAM_B1 * m + (1.0 - ADAM_B1) * g
        v2 = ADAM_B2 * v + (1.0 - ADAM_B2) * (g * g)
        delta = -ADAM_LR * ((m2 / bc1) / (jnp.sqrt(v2 / bc2) + ADAM_EPS) + ADAM_WD * w)
        return delta, m2, v2
    return _rowwise(fn, [w, g, m, v], [], [(cols, F32)] * 3, name=name, tile=tile)


def _attn_specs(arr, width, cb, classes, rows_block, whole, together=1):
    ncols = arr.shape[2] // (classes * width)
    lanes, at = (width, lambda r: r * ncols + cb) if together == 1 else (together * ncols * width, lambda r: r)
    if whole:
        return pl.BlockSpec((1, arr.shape[1], lanes), lambda n, r, i: (n, 0, at(r)))
    return pl.BlockSpec((1, rows_block, lanes), lambda n, r, i: (n, i, at(r)))


def _class_window(refs, spans, together, cl):
    if together == 1:
        return refs
    return [ref.at[:, :, pl.ds((cl * ncols + cb) * width, width)] for ref, (ncols, cb, width) in zip(refs, spans)]


def _attn_tiles(mode, lq, lk, backward=False):
    if mode == "full":
        rows = min(lq, FULL_ROWS)
        return rows, rows, lk
    if mode == "band":
        tq = min(BAND_ROWS if backward else 2 * BAND_ROWS, lq)
        rows = tq if backward else ATT_BLOCK
        return tq, rows, min(rows + ATT_BLOCK, lk)
    rows = CAUSAL_ROWS_BACKWARD if backward else CAUSAL_ROWS
    return rows, rows, ATT_CHUNK


def _window(mode, row0, j, rows, win, lk):
    if mode == "causal":
        return pl.multiple_of(j * win, win), row0 // win + 1
    if mode == "band":
        return pl.multiple_of(jnp.clip(row0 + rows - win, 0, lk - win), ATT_BLOCK), 1
    return 0, 1


def _allowed(mode, row0, start, rows, win, max_dist):
    if mode == "full":
        return None
    dist = (row0 + lax.broadcasted_iota(jnp.int32, (rows, win), 0)) - (start + lax.broadcasted_iota(jnp.int32, (rows, win), 1))
    ok = dist >= 0
    if max_dist is not None:
        ok = ok & (dist <= max_dist)
    return ok


def _head_groups(heads, dh):
    gw = max(LANES, dh)
    return gw, gw // dh, heads // (gw // dh)


def _own_lanes(rows, gw, dh, u):
    return lax.broadcasted_iota(jnp.int32, (rows, gw), 1) // dh == u


def _only_head(x, own, per, u):
    return x if per == 1 else jnp.where(own[u], x, jnp.zeros_like(x))


def _step_scores(qz, kw, kb_row, ok):
    s = lax.dot_general(qz, kw, (((1,), (1,)), ((), ())), preferred_element_type=F32)
    if kb_row is not None:
        s = s - kb_row
    if ok is not None:
        s = jnp.where(ok, s, MASKED)
    return s


def _attn_fwd(q, k, v, kb=None, *, classes=1, heads, dh, mode, max_dist=None, bf16_copy=False, name):
    (qa, qc), (ka, kc), (va, vc) = q, k, v
    n, lq, lk = qa.shape[0], qa.shape[1], ka.shape[1]
    width = heads * dh
    tq, rows, win = _attn_tiles(mode, lq, lk)
    gw, per, groups = _head_groups(heads, dh)
    scale = dh ** -0.5
    has_kb = kb is not None
    single = mode != "causal"
    together = min(classes, max(1, CLASS_ROWS // lq))
    ncols = lambda arr: arr.shape[2] // (classes * width)
    spans = [(ncols(qa), qc, width), (ncols(ka), kc, width), (ncols(va), vc, width), (1, 0, width), (1, 0, LANES), (1, 0, width)]

    def body(*refs):
        for cl in range(together):
            for sub in range(tq // rows):
                part(_class_window(refs, spans, together, cl), pl.program_id(2) * tq + sub * rows, slice(sub * rows, (sub + 1) * rows))

    def part(refs, row0, rs):
        q_ref, k_ref, v_ref = refs[:3]
        kb_ref = refs[3] if has_kb else None
        n_in = 4 if has_kb else 3
        o_ref, lse_ref = refs[n_in:n_in + 2]
        o16_ref = refs[n_in + 2] if bf16_copy else None
        lane = lax.broadcasted_iota(jnp.int32, (rows, LANES), 1)
        own = [_own_lanes(rows, gw, dh, u) for u in range(per)]

        def step(j, carry, masked=True):
            m_in, l_in = carry
            m_out, l_out = m_in, l_in
            start, _ = _window(mode, row0, j, rows, win, lk)
            ok = _allowed(mode, row0, start, rows, win, max_dist) if masked else None
            for g in range(groups):
                cols = slice(g * gw, (g + 1) * gw)
                qg = q_ref[0, rs, cols] * scale
                kw = k_ref[0, pl.ds(start, win), cols]
                vw = v_ref[0, pl.ds(start, win), cols]
                alpha_g = new_g = None
                for u in range(per):
                    h = g * per + u
                    kb_row = kb_ref[0, h, pl.ds(j, 1), :] if has_kb else None
                    s = _step_scores(_only_head(qg, own, per, u), kw, kb_row, ok)
                    m_new = jnp.max(s, axis=1, keepdims=True)
                    if not single:
                        m_old = m_in[:, h:h + 1]
                        m_new = jnp.maximum(m_old, m_new)
                        alpha = jnp.exp(m_old - m_new)
                        alpha_g = alpha if u == 0 else jnp.where(own[u], alpha, alpha_g)
                    p = jnp.exp(s - m_new)
                    l_new = jnp.sum(p, axis=1, keepdims=True)
                    r = jnp.dot(p.astype(BF16), vw, preferred_element_type=F32)
                    if single:
                        r = r * (1.0 / l_new)
                    else:
                        l_new = alpha * l_in[:, h:h + 1] + l_new
                    new_g = r if u == 0 else jnp.where(own[u], r, new_g)
                    m_out = jnp.where(lane == h, m_new, m_out)
                    l_out = jnp.where(lane == h, l_new, l_out)
                o_ref[0, rs, cols] = new_g if single else alpha_g * o_ref[0, rs, cols] + new_g
                if bf16_copy:
                    o16_ref[0, rs, cols] = new_g.astype(BF16)
            return m_out, l_out

        carry = (jnp.full((rows, LANES), MASKED, F32), jnp.zeros((rows, LANES), F32))
        if single:
            m_all, l_all = step(0, carry)
            l_all = jnp.where(lane < heads, l_all, 1.0)
        else:
            o_ref[...] = jnp.zeros(o_ref.shape, F32)
            last = _window(mode, row0, 0, rows, win, lk)[1] - 1
            m_all, l_all = step(last, lax.fori_loop(0, last, functools.partial(step, masked=False), carry))
            l_all = jnp.where(lane < heads, l_all, 1.0)
            inv = 1.0 / l_all
            for g in range(groups):
                cols = slice(g * gw, (g + 1) * gw)
                inv_g = inv[:, g * per:g * per + 1]
                for u in range(1, per):
                    inv_g = jnp.where(own[u], inv[:, g * per + u:g * per + u + 1], inv_g)
                o_ref[0, rs, cols] = o_ref[0, rs, cols] * inv_g
        lse_ref[0, rs, :] = jnp.where(lane < heads, m_all + jnp.log(l_all), 0.0)

    in_specs = [_attn_specs(qa, width, qc, classes, tq, False, together), _attn_specs(ka, width, kc, classes, win, True, together),
                _attn_specs(va, width, vc, classes, win, True, together)]
    args = [qa, ka, va]
    if has_kb:
        assert together == 1
        in_specs.append(pl.BlockSpec((1,) + kb.shape[1:], lambda n_, r, i: (n_, 0, 0, 0)))
        args.append(kb)
    out_shape = [jax.ShapeDtypeStruct((n, lq, classes * width), F32), jax.ShapeDtypeStruct((n, lq, classes * LANES), F32)]
    out_specs = [pl.BlockSpec((1, tq, together * width), lambda n_, r, i: (n_, i, r)),
                 pl.BlockSpec((1, tq, together * LANES), lambda n_, r, i: (n_, i, r))]
    if bf16_copy:
        assert single
        out_shape.append(jax.ShapeDtypeStruct((n, lq, classes * width), BF16))
        out_specs.append(out_specs[0])
    return pl.pallas_call(
        body, grid=(n, classes // together, lq // tq), in_specs=in_specs, out_specs=out_specs, out_shape=out_shape, name=name,
        compiler_params=_params("parallel", "parallel", "arbitrary"),
    )(*args)


def _attn_bwd(q, k, v, do, lse, delta, kb=None, *, classes=1, heads, dh, mode, max_dist=None, dq_dtype=F32, name):
    (qa, qc), (ka, kc), (va, vc), (da, dc) = q, k, v, do
    n, lq, lk = qa.shape[0], qa.shape[1], ka.shape[1]
    width = heads * dh
    tq, rows, win = _attn_tiles(mode, lq, lk, backward=True)
    gw, per, groups = _head_groups(heads, dh)
    scale = dh ** -0.5
    has_kb = kb is not None
    single = mode != "causal"
    nt = (((1,), (1,)), ((), ()))
    tn = (((0,), (0,)), ((), ()))
    together = min(classes, max(1, CLASS_ROWS // lq))
    ncols = lambda arr: arr.shape[2] // (classes * width)
    spans = [(ncols(qa), qc, width), (ncols(ka), kc, width), (ncols(va), vc, width), (ncols(da), dc, width), (1, 0, LANES),
             (1, 0, LANES), (1, 0, width), (1, 0, width), (1, 0, width)]

    def body(*refs):
        n_in = 7 if has_kb else 6
        dk_ref, dv_ref = refs[n_in + 1:n_in + 3]

        @pl.when(pl.program_id(2) == 0)
        def _():
            dk_ref[...] = jnp.zeros(dk_ref.shape, F32)
            dv_ref[...] = jnp.zeros(dv_ref.shape, F32)
            if has_kb:
                refs[n_in + 3][...] = jnp.zeros(refs[n_in + 3].shape, F32)

        for cl in range(together):
            for sub in range(tq // rows):
                part(_class_window(refs, spans, together, cl), pl.program_id(2) * tq + sub * rows, slice(sub * rows, (sub + 1) * rows))

    def part(refs, row0, rs):
        q_ref, k_ref, v_ref, do_ref, lse_ref, dl_ref = refs[:6]
        kb_ref = refs[6] if has_kb else None
        n_in = 7 if has_kb else 6
        dq_ref, dk_ref, dv_ref = refs[n_in:n_in + 3]
        dkb_ref, drow_ref = refs[n_in + 3:n_in + 5] if has_kb else (None, None)
        lse_all = lse_ref[0, rs, :]
        dl_all = dl_ref[0, rs, :]
        lane = lax.broadcasted_iota(jnp.int32, (rows, LANES), 1)
        own = [_own_lanes(rows, gw, dh, u) for u in range(per)]

        def step(j, drow_all, masked=True):
            start, _ = _window(mode, row0, j, rows, win, lk)
            ok = _allowed(mode, row0, start, rows, win, max_dist) if masked else None
            for g in range(groups):
                cols = slice(g * gw, (g + 1) * gw)
                qg = q_ref[0, rs, cols] * scale
                dog = do_ref[0, rs, cols]
                kw = k_ref[0, pl.ds(start, win), cols]
                vw = v_ref[0, pl.ds(start, win), cols]
                dq_g = dk_w = dv_w = None
                for u in range(per):
                    h = g * per + u
                    qz, doz = _only_head(qg, own, per, u), _only_head(dog, own, per, u)
                    kb_row = kb_ref[0, h, pl.ds(j, 1), :] if has_kb else None
                    p = jnp.exp(_step_scores(qz, kw, kb_row, ok) - lse_all[:, h:h + 1])
                    dp = lax.dot_general(doz, vw, nt, preferred_element_type=F32)
                    ds = p * (dp - dl_all[:, h:h + 1])
                    dsb = ds.astype(BF16)
                    r = jnp.dot(dsb, kw, preferred_element_type=F32)
                    dq_g = r if u == 0 else jnp.where(own[u], r, dq_g)
                    dk_u = lax.dot_general(dsb, qz, tn, preferred_element_type=F32)
                    dv_u = lax.dot_general(p.astype(BF16), doz, tn, preferred_element_type=F32)
                    dk_w = dk_u if u == 0 else dk_w + dk_u
                    dv_w = dv_u if u == 0 else dv_w + dv_u
                    if has_kb:
                        dkb_ref[0, h, pl.ds(j, 1), :] += -jnp.sum(ds, axis=0, keepdims=True)
                        drow_all = drow_all + jnp.where(lane == h, jnp.sum(ds, axis=1, keepdims=True), 0.0)
                dk_ref[0, pl.ds(start, win), cols] += dk_w
                dv_ref[0, pl.ds(start, win), cols] += dv_w
                if single:
                    dq_ref[0, rs, cols] = (dq_g * scale).astype(dq_ref.dtype)
                else:
                    dq_ref[0, rs, cols] += dq_g * scale
            return drow_all

        drow_all = jnp.zeros((rows, LANES), F32)
        if single:
            drow_all = step(0, drow_all)
        else:
            dq_ref[...] = jnp.zeros(dq_ref.shape, F32)
            last = _window(mode, row0, 0, rows, win, lk)[1] - 1
            drow_all = step(last, lax.fori_loop(0, last, functools.partial(step, masked=False), drow_all))
        if has_kb:
            drow_ref[0, rs, :] = drow_all

    row_spec = functools.partial(_attn_specs, classes=classes, rows_block=tq, whole=False, together=together)
    in_specs = [row_spec(qa, width, qc), _attn_specs(ka, width, kc, classes, win, True, together),
                _attn_specs(va, width, vc, classes, win, True, together), row_spec(da, width, dc), row_spec(lse, LANES, 0),
                row_spec(delta, LANES, 0)]
    args = [qa, ka, va, da, lse, delta]
    assert single or dq_dtype == F32
    out_shape = [jax.ShapeDtypeStruct((n, lq, classes * width), dq_dtype), jax.ShapeDtypeStruct((n, lk, classes * width), F32),
                 jax.ShapeDtypeStruct((n, lk, classes * width), F32)]
    kv_spec = pl.BlockSpec((1, lk, together * width), lambda n_, r, i: (n_, 0, r))
    out_specs = [pl.BlockSpec((1, tq, together * width), lambda n_, r, i: (n_, i, r)), kv_spec, kv_spec]
    if has_kb:
        assert together == 1
        kb_spec = pl.BlockSpec((1,) + kb.shape[1:], lambda n_, r, i: (n_, 0, 0, 0))
        in_specs.append(kb_spec)
        args.append(kb)
        out_shape += [jax.ShapeDtypeStruct(kb.shape, F32), jax.ShapeDtypeStruct((n, lq, LANES), F32)]
        out_specs += [kb_spec, pl.BlockSpec((1, tq, LANES), lambda n_, r, i: (n_, i, 0))]
    return pl.pallas_call(
        body, grid=(n, classes // together, lq // tq), in_specs=in_specs, out_specs=out_specs, out_shape=out_shape, name=name,
        compiler_params=_params("parallel", "parallel", "arbitrary"),
    )(*args)


def _rope_tables():
    half = HEAD_DIM // 2
    inv = ROPE_THETA ** (-jnp.arange(half, dtype=F32) / half)
    ang = jnp.arange(SEQ, dtype=F32)[:, None] * inv[None, :]
    cos = jnp.tile(jnp.cos(ang), (1, 2 * ATT_W // HEAD_DIM))
    sin = jnp.tile(jnp.concatenate([-jnp.sin(ang), jnp.sin(ang)], axis=1), (1, ATT_W // HEAD_DIM))
    return cos, sin


def _rotate(x, cos, sin):
    width = x.shape[1]
    lane = lax.broadcasted_iota(jnp.int32, x.shape, 1)
    first_half = (lane % HEAD_DIM) < (HEAD_DIM // 2)
    swapped = jnp.where(first_half, pltpu.roll(x, width - HEAD_DIM // 2, axis=1), pltpu.roll(x, HEAD_DIM // 2, axis=1))
    return x * cos[:, :width] + swapped * sin[:, :width]


def _gelu(x):
    k = math.sqrt(2.0 / math.pi)
    t = jnp.tanh(k * (x + 0.044715 * x * x * x))
    return 0.5 * x * (1.0 + t), t


def _gelu_grad(x, t):
    k = math.sqrt(2.0 / math.pi)
    return 0.5 * (1.0 + t) + 0.5 * x * (1.0 - t * t) * k * (1.0 + 3.0 * 0.044715 * x * x)


def _shift_down(x, halo, s):
    ext = jnp.concatenate([halo, x], axis=0)
    return pltpu.roll(ext, s, axis=0)[8:, :]


def _shift_up(x, halo, s):
    rows = x.shape[0]
    ext = jnp.concatenate([x, halo], axis=0)
    return pltpu.roll(ext, rows + 8 - s, axis=0)[:rows, :]


def _lru_gates(u, halo, cw, cb, wa, ba, wi, bi, lam):
    taps = [_shift_down(u, halo, CONV_WIDTH - 1 - k) for k in range(CONV_WIDTH - 1)] + [u]
    uc = cb + sum(cw[k:k + 1, :] * taps[k] for k in range(CONV_WIDTH))
    ucb = uc.astype(BF16)
    r = jax.nn.sigmoid(jnp.dot(ucb, wa, preferred_element_type=F32) + ba)
    gi = jax.nn.sigmoid(jnp.dot(ucb, wi, preferred_element_type=F32) + bi)
    sp = jnp.maximum(-lam, 0.0) + jnp.log(1.0 + jnp.exp(-jnp.abs(lam)))
    log_a = -LRU_C * r * sp
    a = jnp.exp(log_a)
    x2 = 2.0 * log_a
    one_minus_a2 = jnp.where(x2 > -0.01, -x2 * (1.0 + x2 * (0.5 + x2 * (1.0 / 6.0 + x2 / 24.0))), 1.0 - jnp.exp(x2))
    mult = jnp.sqrt(one_minus_a2)
    return taps, uc, ucb, r, gi, sp, a, mult


def _lru_specs(nchunk, reverse):
    def chunk(b, c):
        return b * nchunk + ((nchunk - 1 - c) if reverse else c)

    def halo_before(b, c):
        return jnp.maximum(chunk(b, c) * (LRU_CHUNK // 8) - 1, 0)

    return chunk, halo_before


def _lru_fwd(zug, cw, cb, wa, ba, wi, bi, lam, *, name):
    total = zug.shape[0]
    nchunk = SEQ // LRU_CHUNK
    w = LRU_WIDTH
    chunk, halo_before = _lru_specs(nchunk, False)

    def body(u_ref, uh_ref, g_ref, cw_ref, cb_ref, wa_ref, ba_ref, wi_ref, bi_ref, lam_ref, y_ref, h_ref, a_sc, x_sc, carry_sc):
        c = pl.program_id(1)
        u = u_ref[...]
        halo = jnp.where(c > 0, uh_ref[...], 0.0)
        _, uc, _, _, gi, _, a, mult = _lru_gates(u, halo, cw_ref[...], cb_ref[...], wa_ref[...], ba_ref[...],
                                                 wi_ref[...], bi_ref[...], lam_ref[...])
        a_sc[...] = a
        x_sc[...] = mult * (gi * uc)

        @pl.when(c == 0)
        def _():
            carry_sc[...] = jnp.zeros(carry_sc.shape, F32)

        def tile_step(t, h):
            r0 = pl.multiple_of(t * 8, 8)
            at = a_sc[pl.ds(r0, 8), :]
            xt = x_sc[pl.ds(r0, 8), :]
            rows = []
            for j in range(8):
                h = at[j:j + 1, :] * h + xt[j:j + 1, :]
                rows.append(h)
            h_ref[pl.ds(r0, 8), :] = jnp.concatenate(rows, axis=0)
            return h

        h_last = lax.fori_loop(0, LRU_CHUNK // 8, tile_step, carry_sc[0:1, :])
        carry_sc[0:1, :] = h_last
        gel, _ = _gelu(g_ref[...])
        y_ref[...] = (h_ref[...] * gel).astype(BF16)

    small = lambda arr: pl.BlockSpec(arr.shape, lambda b, c: (0, 0))
    return pl.pallas_call(
        body, grid=(total // SEQ, nchunk),
        in_specs=[pl.BlockSpec((LRU_CHUNK, w), lambda b, c: (chunk(b, c), 0)),
                  pl.BlockSpec((8, w), lambda b, c: (halo_before(b, c), 0)),
                  pl.BlockSpec((LRU_CHUNK, w), lambda b, c: (chunk(b, c), 1)),
                  small(cw), small(cb), small(wa), small(ba), small(wi), small(bi), small(lam)],
        out_specs=[pl.BlockSpec((LRU_CHUNK, w), lambda b, c: (chunk(b, c), 0))] * 2,
        out_shape=[jax.ShapeDtypeStruct((total, w), BF16), jax.ShapeDtypeStruct((total, w), F32)],
        scratch_shapes=[pltpu.VMEM((LRU_CHUNK, w), F32), pltpu.VMEM((LRU_CHUNK, w), F32), pltpu.VMEM((8, w), F32)],
        name=name, compiler_params=_params("arbitrary", "arbitrary"),
    )(zug, zug, zug, cw, cb, wa, ba, wi, bi, lam)


def _lru_bwd(zug, hl, dy, cw, cb, wa, ba, wi, bi, lam, *, name):
    total = zug.shape[0]
    nchunk = SEQ // LRU_CHUNK
    w = LRU_WIDTH
    chunk, halo_before = _lru_specs(nchunk, True)
    tn = (((0,), (0,)), ((), ()))
    nt = (((1,), (1,)), ((), ()))

    def body(u_ref, uh_ref, g_ref, hl_ref, hh_ref, dy_ref, cw_ref, cb_ref, wa_ref, ba_ref, wi_ref, bi_ref, lam_ref,
             dz_ref, dcw_ref, dcb_ref, dwa_ref, dba_ref, dwi_ref, dbi_ref, dlam_ref,
             a_sc, d_sc, g_sc, gcarry_sc, acarry_sc, duc_sc):
        b, c = pl.program_id(0), pl.program_id(1)
        first_chunk = c == nchunk - 1

        @pl.when((b == 0) & (c == 0))
        def _():
            for ref in (dcw_ref, dcb_ref, dwa_ref, dba_ref, dwi_ref, dbi_ref, dlam_ref):
                ref[...] = jnp.zeros(ref.shape, F32)

        @pl.when(c == 0)
        def _():
            gcarry_sc[...] = jnp.zeros(gcarry_sc.shape, F32)
            acarry_sc[...] = jnp.zeros(acarry_sc.shape, F32)
            duc_sc[...] = jnp.zeros(duc_sc.shape, F32)

        u = u_ref[...]
        halo = jnp.where(first_chunk, 0.0, uh_ref[...])
        cw, wa, wi, lam = cw_ref[...], wa_ref[...], wi_ref[...], lam_ref[...]
        taps, uc, ucb, r, gi, sp, a, mult = _lru_gates(u, halo, cw, cb_ref[...], wa, ba_ref[...], wi, bi_ref[...], lam)
        gate = g_ref[...]
        gel, th = _gelu(gate)
        dy = dy_ref[...]
        hl = hl_ref[...]
        a_sc[...] = a
        d_sc[...] = dy * gel
        dgate = dy * hl * _gelu_grad(gate, th)

        def tile_step(t, carry):
            g_next, a_next = carry
            r0 = pl.multiple_of((LRU_CHUNK // 8 - 1 - t) * 8, 8)
            dt = d_sc[pl.ds(r0, 8), :]
            at = a_sc[pl.ds(r0, 8), :]
            rows = [None] * 8
            for j in reversed(range(8)):
                g_next = dt[j:j + 1, :] + a_next * g_next
                a_next = at[j:j + 1, :]
                rows[j] = g_next
            g_sc[pl.ds(r0, 8), :] = jnp.concatenate(rows, axis=0)
            return g_next, a_next

        g_last, a_last = lax.fori_loop(0, LRU_CHUNK // 8, tile_step, (gcarry_sc[0:1, :], acarry_sc[0:1, :]))
        gcarry_sc[0:1, :] = g_last
        acarry_sc[0:1, :] = a_last

        gs = g_sc[...]
        hl_halo = jnp.where(first_chunk, 0.0, hh_ref[...])
        da = gs * _shift_down(hl, hl_halo, 1)
        dmult = gs * gi * uc
        dgi = gs * mult * uc
        duc = gs * mult * gi
        dlog_a = da * a - dmult * a * a / mult
        dr = dlog_a * (-LRU_C * sp)
        dsp = jnp.sum(dlog_a * (-LRU_C * r), axis=0, keepdims=True)
        dlam_ref[...] += -dsp * jax.nn.sigmoid(-lam)
        dpa = dr * r * (1.0 - r)
        dpi = dgi * gi * (1.0 - gi)
        dpab, dpib = dpa.astype(BF16), dpi.astype(BF16)
        dba_ref[...] += jnp.sum(dpa, axis=0, keepdims=True)
        dbi_ref[...] += jnp.sum(dpi, axis=0, keepdims=True)
        dwa_ref[...] += lax.dot_general(ucb, dpab, tn, preferred_element_type=F32)
        dwi_ref[...] += lax.dot_general(ucb, dpib, tn, preferred_element_type=F32)
        duc = duc + lax.dot_general(dpab, wa, nt, preferred_element_type=F32)
        duc = duc + lax.dot_general(dpib, wi, nt, preferred_element_type=F32)
        dcb_ref[...] += jnp.sum(duc, axis=0, keepdims=True)
        dcw_ref[...] += jnp.concatenate([jnp.sum(duc * taps[k], axis=0, keepdims=True) for k in range(CONV_WIDTH)], axis=0)
        after = duc_sc[...]
        du = cw[CONV_WIDTH - 1:CONV_WIDTH, :] * duc
        for k in range(CONV_WIDTH - 1):
            du = du + cw[k:k + 1, :] * _shift_up(duc, after, CONV_WIDTH - 1 - k)
        duc_sc[...] = duc[0:8, :]
        dz_ref[:, 0:w] = du.astype(BF16)
        dz_ref[:, w:2 * w] = dgate.astype(BF16)

    small = lambda arr: pl.BlockSpec(arr.shape, lambda b, c: (0, 0))
    acc = lambda shape: pl.BlockSpec(shape, lambda b, c: (0, 0))
    chunk_spec = lambda cb_: pl.BlockSpec((LRU_CHUNK, w), lambda b, c: (chunk(b, c), cb_))
    halo_spec = pl.BlockSpec((8, w), lambda b, c: (halo_before(b, c), 0))
    acc_shapes = [(CONV_WIDTH, w), (1, w), (w, w), (1, w), (w, w), (1, w), (1, w)]
    return pl.pallas_call(
        body, grid=(total // SEQ, nchunk),
        in_specs=[chunk_spec(0), halo_spec, chunk_spec(1), chunk_spec(0), halo_spec, chunk_spec(0),
                  small(cw), small(cb), small(wa), small(ba), small(wi), small(bi), small(lam)],
        out_specs=[pl.BlockSpec((LRU_CHUNK, 2 * w), lambda b, c: (chunk(b, c), 0))] + [acc(s) for s in acc_shapes],
        out_shape=[jax.ShapeDtypeStruct((total, 2 * w), BF16)] + [jax.ShapeDtypeStruct(s, F32) for s in acc_shapes],
        scratch_shapes=[pltpu.VMEM((LRU_CHUNK, w), F32)] * 3 + [pltpu.VMEM((8, w), F32)] * 3,
        name=name, compiler_params=_params("arbitrary", "arbitrary"),
    )(zug, zug, zug, hl, hl, dy, cw, cb, wa, ba, wi, bi, lam)


def _block_diag(wb):
    eye = jnp.eye(LRU_BLOCKS, dtype=wb.dtype)
    return jnp.einsum("gcd,gh->gchd", wb, eye).reshape(LRU_WIDTH, LRU_WIDTH).astype(BF16)


def _diag_blocks(wd):
    blk = LRU_WIDTH // LRU_BLOCKS
    return jnp.stack([wd[g * blk:(g + 1) * blk, g * blk:(g + 1) * blk] for g in range(LRU_BLOCKS)])


FOX_SCAN = 256


def _fox_bias(fl, bf, *, name):
    nb = fl.shape[0]

    def body(fl_ref, bf_ref, c_ref):
        x = fl_ref[0] + bf_ref[...]
        logf = jnp.minimum(x, 0.0) - jnp.log(1.0 + jnp.exp(-jnp.abs(x)))
        upper = (lax.broadcasted_iota(jnp.int32, (FOX_SCAN, FOX_SCAN), 0)
                 <= lax.broadcasted_iota(jnp.int32, (FOX_SCAN, FOX_SCAN), 1)).astype(BF16)
        carry = jnp.zeros((LRU_BLOCKS, 1), F32)
        for j in range(SEQ // FOX_SCAN):
            blk = logf[:, j * FOX_SCAN:(j + 1) * FOX_SCAN]
            c_ref[0, :, j * FOX_SCAN:(j + 1) * FOX_SCAN] = _dot_exact(blk, upper) + carry
            carry = carry + jnp.sum(blk, axis=1, keepdims=True)

    return pl.pallas_call(
        body, grid=(nb,),
        in_specs=[pl.BlockSpec((1, 8, SEQ), lambda b: (b, 0, 0)), pl.BlockSpec((8, 1), lambda b: (0, 0))],
        out_specs=pl.BlockSpec((1, 8, SEQ), lambda b: (b, 0, 0)),
        out_shape=jax.ShapeDtypeStruct((nb, 8, SEQ), F32), name=name, compiler_params=_params("arbitrary"),
    )(fl, bf)


def _fox_bias_bwd(fl, bf, dc, *, name):
    nb = fl.shape[0]

    def body(fl_ref, bf_ref, dc_ref, dfl_ref, dbf_ref):
        @pl.when(pl.program_id(0) == 0)
        def _():
            dbf_ref[...] = jnp.zeros(dbf_ref.shape, F32)

        x = fl_ref[0] + bf_ref[...]
        dc_all = dc_ref[0]
        lower = (lax.broadcasted_iota(jnp.int32, (FOX_SCAN, FOX_SCAN), 0)
                 >= lax.broadcasted_iota(jnp.int32, (FOX_SCAN, FOX_SCAN), 1)).astype(BF16)
        carry = jnp.zeros((LRU_BLOCKS, 1), F32)
        total = jnp.zeros((LRU_BLOCKS, 1), F32)
        for j in reversed(range(SEQ // FOX_SCAN)):
            cols = slice(j * FOX_SCAN, (j + 1) * FOX_SCAN)
            blk = dc_all[:, cols]
            dlogf = _dot_exact(blk, lower) + carry
            carry = carry + jnp.sum(blk, axis=1, keepdims=True)
            dx = dlogf * jax.nn.sigmoid(-x[:, cols])
            dfl_ref[0, :, cols] = dx
            total = total + jnp.sum(dx, axis=1, keepdims=True)
        dbf_ref[...] += total

    return pl.pallas_call(
        body, grid=(nb,),
        in_specs=[pl.BlockSpec((1, 8, SEQ), lambda b: (b, 0, 0)), pl.BlockSpec((8, 1), lambda b: (0, 0)),
                  pl.BlockSpec((1, 8, SEQ), lambda b: (b, 0, 0))],
        out_specs=[pl.BlockSpec((1, 8, SEQ), lambda b: (b, 0, 0)), pl.BlockSpec((8, 1), lambda b: (0, 0))],
        out_shape=[jax.ShapeDtypeStruct((nb, 8, SEQ), F32), jax.ShapeDtypeStruct((8, 1), F32)],
        name=name, compiler_params=_params("arbitrary"),
    )(fl, bf, dc)


def _seq3(a, nb):
    return a.reshape(nb, a.shape[0] // nb, a.shape[1])


def _flat2(a):
    return a.reshape(a.shape[0] * a.shape[1], a.shape[2])


def _residual_out(y, w, h, next_gain, *, name):
    if next_gain is None:
        out, = _matmul(y, w, extras=(h,), epilogue=lambda acc, res: (acc + res,), name=name)
        return out, None

    def epilogue(acc, res, g):
        out = acc + res
        return out, out * _rstd(out) * g
    return _matmul(y, w, outs=(F32, BF16), extras=(h,), consts=(next_gain,), epilogue=epilogue, whole_rows=True, name=name)


def _mlp_fwd(h, hn, p, tag, next_gain):
    act, = _matmul(hn, p["w_up_t"], tb=True, outs=(BF16,), epilogue=lambda acc: (jnp.square(jnp.maximum(acc, 0.0)),),
                   tn=D_FF // 2, name=f"{tag}_up")
    out, hn_next = _residual_out(act, p["w_down"], h, next_gain, name=f"{tag}_down")
    return out, hn_next, (h, hn, act)


def _mlp_bwd(dh, dhb, p, saved, tag):
    h, hn, act = saved
    dup, = _matmul(dhb, p["w_down"], tb=True, outs=(BF16,), extras=(act,),
                   epilogue=lambda acc, act: (acc * (2.0 * jnp.sqrt(act).astype(F32)),), tn=D_FF // 2, name=f"{tag}_bwd_dup")
    dw_down, = _matmul(act, dhb, ta=True, outs=(BF16,),name=f"{tag}_bwd_dwdown")
    dw_up_t, = _matmul(dup, hn, ta=True, outs=(BF16,),name=f"{tag}_bwd_dwup")
    dh2, dh2b, dg = _norm_input_grad(dup, p["w_up_t"], h, dh, p["norm"], name=f"{tag}_bwd_dh")
    return dh2, dh2b, {"norm": dg, "w_up_t": dw_up_t, "w_down": dw_down}


def _xa_fwd(h, hn, mem, p, tag, nb, next_gain):
    mem_n = _rms_fwd(mem, p["mem_norm"], name=f"{tag}_memnorm")
    q, = _matmul(hn, p["w_q"], outs=(BF16,), name=f"{tag}_q")
    kv, = _matmul(mem_n, p["w_kv_t"], tb=True, outs=(BF16,), tn=2 * D_MODEL, name=f"{tag}_kv")
    q3, kv3 = _seq3(q, nb), _seq3(kv, nb)
    o, lse, ob = _attn_fwd((q3, 0), (kv3, 0), (kv3, 1), heads=XA_HEADS, dh=XA_HEAD_DIM, mode="full", bf16_copy=True,
                           name=f"{tag}_attn")
    o, ob = _flat2(o), _flat2(ob)
    out, hn_next = _residual_out(ob, p["w_o"], h, next_gain, name=f"{tag}_o")
    return out, hn_next, (h, hn, mem_n, q3, kv3, o, ob, lse)


def _xa_bwd(dh, dhb, mem, p, saved, tag, nb):
    h, hn, mem_n, q3, kv3, o, ob, lse = saved
    dob, delta = _matmul(dhb, p["w_o"], tb=True, outs=(BF16, (F32, LANES)), extras=(o,), consts=(_head_expand(XA_HEADS, D_MODEL).T,),
                         epilogue=lambda acc, o, e: (acc, _dot_exact(acc * o, e, parts=2)), name=f"{tag}_bwd_do")
    dw_o, = _matmul(ob, dhb, ta=True, outs=(BF16,),name=f"{tag}_bwd_dwo")
    dq, dk, dv = _attn_bwd((q3, 0), (kv3, 0), (kv3, 1), (_seq3(dob, nb), 0), lse, _seq3(delta, nb), heads=XA_HEADS,
                           dh=XA_HEAD_DIM, mode="full", dq_dtype=BF16, name=f"{tag}_bwd_attn")
    dqb = _flat2(dq)
    dkvb, = _rowwise(lambda a, b: jnp.concatenate([a, b], axis=1), [_flat2(dk), _flat2(dv)], [], [(2 * D_MODEL, BF16)],
                     name=f"{tag}_bwd_dkvcast")
    dw_q, = _matmul(hn, dqb, ta=True, outs=(BF16,),name=f"{tag}_bwd_dwq")
    dw_kv_t, = _matmul(dkvb, mem_n, ta=True, outs=(BF16,),name=f"{tag}_bwd_dwkv")
    dmem_n, = _matmul(dkvb, p["w_kv_t"], name=f"{tag}_bwd_dmemn")
    dg_mem, = _rowwise(lambda x, d, g: _rms_bwd_vals(x, d, g)[1], [mem, dmem_n], [p["mem_norm"]], [], [(1, D_MODEL)],
                       name=f"{tag}_bwd_memnorm")
    dh2, dh2b, dg = _norm_input_grad(dqb, p["w_q"], h, dh, p["norm"], tb=True, name=f"{tag}_bwd_dh")
    return dh2, dh2b, {"norm": dg, "mem_norm": dg_mem, "w_q": dw_q, "w_kv_t": dw_kv_t, "w_o": dw_o}


AB_MAIN = 2 * LRU_WIDTH + 3 * ATT_W


def _ab_fwd(h, hn, p, nb, next_gain, before_out=None):
    w_in_t = p["w_in_t"]
    zug, = _matmul(hn, w_in_t[:2 * LRU_WIDTH], tb=True, name="ab_in_ug")
    qkv, = _matmul(hn, w_in_t[2 * LRU_WIDTH:AB_MAIN], tb=True, outs=(BF16,), tn=1536, name="ab_in_qkv")
    zf, = _matmul(hn, w_in_t[AB_MAIN:], tb=True, name="ab_in_f")
    fl = zf[:, :8].reshape(nb, SEQ, 8).transpose(0, 2, 1)
    cbias = _fox_bias(fl, p["b_f"], name="ab_fox_bias")
    kb = cbias.reshape(nb, 8, SEQ // ATT_CHUNK, ATT_CHUNK)
    y_a, hl = _lru_fwd(zug, p["conv_w"], p["conv_b"], p["w_a"], p["b_a"], p["w_i"], p["b_i"], p["lam"], name="ab_lru")
    qkv3 = _seq3(qkv, nb)
    o, lse = _attn_fwd((qkv3, 0), (qkv3, 1), (qkv3, 2), kb, heads=8, dh=HEAD_DIM, mode="causal", name="ab_fox")
    o = _flat2(o)
    y, = _rowwise(lambda a, b: jnp.concatenate([a, b.astype(BF16)], axis=1), [y_a, o], [], [(D_MODEL, BF16)], name="ab_ycat")
    if before_out is not None:
        before_out(y)
    out, hn_next = _residual_out(y, p["w_out"], h, next_gain, name="ab_out")
    return out, hn_next, (h, hn, zug, qkv3, fl, kb, hl, o, lse, y)


def _ab_bwd(dh, dhb, p, saved, nb):
    h, hn, zug, qkv3, fl, kb, hl, o, lse, y = saved
    dy, dyb, delta = _matmul(dhb, p["w_out"], tb=True, outs=(F32, BF16, (F32, LANES)), extras=(y,), consts=(_head_expand(8, ATT_W).T,),
                             epilogue=lambda acc, y, e: (acc, acc, _dot_exact(acc[:, ATT_W:] * y[:, ATT_W:].astype(F32), e, parts=2)),
                             name="ab_bwd_dy")
    dw_out, = _matmul(y, dhb, ta=True, outs=(BF16,),name="ab_bwd_dwout")
    dzug, dcw, dcb, dwa, dba, dwi, dbi, dlam = _lru_bwd(zug, hl, dy, p["conv_w"], p["conv_b"], p["w_a"], p["b_a"],
                                                        p["w_i"], p["b_i"], p["lam"], name="ab_bwd_lru")
    dq, dk, dv, dkb, drow = _attn_bwd((qkv3, 0), (qkv3, 1), (qkv3, 2), (_seq3(dyb, nb), 1), lse, _seq3(delta, nb), kb,
                                      heads=8, dh=HEAD_DIM, mode="causal", name="ab_bwd_fox")
    dcum = dkb.reshape(nb, 8, SEQ) + drow[:, :, :8].transpose(0, 2, 1)
    dfl, dbf = _fox_bias_bwd(fl, p["b_f"], dcum, name="ab_bwd_fox_bias")
    dzf = jnp.pad(dfl.transpose(0, 2, 1).reshape(nb * SEQ, 8), ((0, 0), (0, LANES - 8)))
    dz, = _rowwise(lambda a, q, k, v, f: jnp.concatenate([a, q.astype(BF16), k.astype(BF16), v.astype(BF16), f.astype(BF16)], axis=1),
                   [dzug, _flat2(dq), _flat2(dk), _flat2(dv), dzf], [], [(AB_MAIN + LANES, BF16)], name="ab_bwd_dzcat")
    dw_in_t, = _matmul(dz, hn, ta=True, outs=(BF16,),tm=384, name="ab_bwd_dwin")
    dh2, dh2b, dg = _norm_input_grad(dz, p["w_in_t"], h, dh, p["norm"], name="ab_bwd_dh")
    grads = {"norm": dg, "w_in_t": dw_in_t[:AB_MAIN + 8], "conv_w": dcw, "conv_b": dcb, "w_a": _diag_blocks(dwa), "b_a": dba,
             "w_i": _diag_blocks(dwi), "b_i": dbi, "lam": dlam, "b_f": dbf.reshape(1, 8), "w_out": dw_out}
    return dh2, dh2b, grads


CD_IN = 3 * ATT_W + ATT_W + 2 * SWA_KW


def _gqa_share():
    src = jnp.arange(SWA_KW)[:, None]
    dst = jnp.arange(ATT_W)[None, :]
    per_kv = ATT_W // (SWA_KW // HEAD_DIM)
    return ((dst // per_kv == src // HEAD_DIM) & (dst % HEAD_DIM == src % HEAD_DIM)).astype(BF16)


def _cd_fwd(h, hn, p, nb, next_gain):
    z, = _matmul(hn, p["w_in_t"], tb=True, tn=2304, name="cd_in")
    cos, sin = _rope_tables()
    per_seq = SEQ // ROW_TILE
    table_map = lambda i: (i % per_seq, 0)

    def rope_fn(z, cos, sin, share):
        qc, kc, vc = z[:, 0:ATT_W], z[:, ATT_W:2 * ATT_W], z[:, 2 * ATT_W:3 * ATT_W]
        qd, kd, vd = z[:, 3 * ATT_W:4 * ATT_W], z[:, 4 * ATT_W:4 * ATT_W + SWA_KW], z[:, 4 * ATT_W + SWA_KW:]
        kd8 = jnp.dot(_rotate(kd, cos, sin).astype(BF16), share, preferred_element_type=F32)
        vd8 = jnp.dot(vd.astype(BF16), share, preferred_element_type=F32)
        qk = jnp.concatenate([_rotate(qc, cos, sin), _rotate(kc, cos, sin)], axis=1)
        return qk, vc, _rotate(qd, cos, sin), kd8, vd8, qk, qk, vc, vc
    dils = [dil for _, dil in DIL_PATTERN if dil > 1]
    outs = _rowwise(rope_fn, [z, cos, sin], [_gqa_share()],
                    [(2 * ATT_W, BF16)] + [(ATT_W, BF16)] * 4 + [(2 * ATT_W, BF16, d) for d in dils] + [(ATT_W, BF16, d) for d in dils],
                    name="cd_rope", row_maps=[None, table_map, table_map])
    qk, vc, qd, kd, vd = outs[:5]
    views = {1: (_seq3(qk, nb), _seq3(vc, nb))}
    for d, qk_d, vc_d in zip(dils, outs[5:5 + len(dils)], outs[5 + len(dils):]):
        views[d] = (_seq3(qk_d, nb), _seq3(vc_d, nb))
    in_classes = lambda a, width, d: _flat2(a) if d == 1 else ("classes", _flat2(a), width, d)
    branch = []
    for window, dil in DIL_PATTERN:
        qk_v, vc_v = views[dil]
        o_i, lse_i = _attn_fwd((qk_v, 0), (qk_v, 1), (vc_v, 0), classes=dil, heads=8, dh=HEAD_DIM, mode="band",
                               max_dist=window // dil, name=f"cd_dil{dil}")
        branch.append((in_classes(o_i, ATT_W, dil), in_classes(lse_i, LANES, dil)))
    expand = _head_expand(8, ATT_W)

    qd3, kd3, vd3 = _seq3(qd, nb), _seq3(kd, nb), _seq3(vd, nb)
    o_d, lse_band = _attn_fwd((qd3, 0), (kd3, 0), (vd3, 0), heads=8, dh=HEAD_DIM, mode="band", max_dist=SWA_WINDOW - 1,
                              name="cd_swa")

    def mix(o1, o2, o3, l1, l2, l3, o_band, l_band, e, sink):
        m = jnp.maximum(jnp.maximum(l1, l2), l3)
        lt = m + jnp.log(jnp.exp(l1 - m) + jnp.exp(l2 - m) + jnp.exp(l3 - m))
        y_c = sum(_dot_exact(jnp.exp(l - lt), e) * o_ for o_, l in ((o1, l1), (o2, l2), (o3, l3)))
        lt_d = jnp.maximum(l_band, sink) + jnp.log(1.0 + jnp.exp(-jnp.abs(l_band - sink)))
        y_d = o_band * _dot_exact(jnp.exp(l_band - lt_d), e)
        return jnp.concatenate([y_c, y_d], axis=1), y_c, lt, y_d, lt_d
    y, y_c, lse_c, y_d, lse_d = _rowwise(
        mix, [b[0] for b in branch] + [b[1] for b in branch] + [_flat2(o_d), _flat2(lse_band)], [expand, p["sink"]],
        [(D_MODEL, BF16), (ATT_W, F32), (LANES, F32), (ATT_W, F32), (LANES, F32)], name="cd_mix")
    out, hn_next = _residual_out(y, p["w_out"], h, next_gain, name="cd_out")
    return out, hn_next, (h, hn, views, qd3, kd3, vd3, y_c, lse_c, y_d, lse_d, y)


def _cd_bwd(dh, dhb, p, saved, nb):
    h, hn, views, qd3, kd3, vd3, y_c, lse_c, y_d, lse_d, y = saved
    dy, dyb = _matmul(dhb, p["w_out"], tb=True, outs=(F32, BF16), epilogue=lambda acc: (acc, acc), name="cd_bwd_dy")
    dw_out, = _matmul(y, dhb, ta=True, outs=(BF16,),name="cd_bwd_dwout")
    dyb3 = _seq3(dyb, nb)
    dils = [dil for _, dil in DIL_PATTERN if dil > 1]
    gather = _head_expand(8, ATT_W).T

    def prepare(do, o, lse, do_d, o_d, lse_d, e, sink):
        delta = _dot_exact(do * o, e, parts=2)
        delta_d = _dot_exact(do_d * o_d, e, parts=2)
        dsink = -jnp.sum(jnp.exp(sink - lse_d) * delta_d, axis=0, keepdims=True)
        return (delta,) + (do,) * len(dils) + (lse,) * len(dils) + (delta,) * len(dils) + (delta_d, dsink)
    outs = _rowwise(prepare, [(dy, ATT_W, 0), y_c, lse_c, (dy, ATT_W, 1), y_d, lse_d], [gather, p["sink"]],
                    [(LANES, F32)] + [(ATT_W, BF16, d) for d in dils] + [(LANES, F32, d) for d in dils] * 2 + [(LANES, F32)],
                    [(1, LANES)], name="cd_bwd_delta")
    delta_d, dsink = outs[-2:]
    seen = {1: ((dyb3, 0), _seq3(lse_c, nb), _seq3(outs[0], nb))}
    for j, d in enumerate(dils):
        seen[d] = ((_seq3(outs[1 + j], nb), 0), _seq3(outs[1 + len(dils) + j], nb), _seq3(outs[1 + 2 * len(dils) + j], nb))
    in_classes = lambda a, d: _flat2(a) if d == 1 else ("classes", _flat2(a), ATT_W, d)
    parts = []
    for window, dil in DIL_PATTERN:
        (qk_v, vc_v), (do_v, lse_v, delta_v) = views[dil], seen[dil]
        grads = _attn_bwd((qk_v, 0), (qk_v, 1), (vc_v, 0), do_v, lse_v, delta_v, classes=dil, heads=8, dh=HEAD_DIM, mode="band",
                          max_dist=window // dil, dq_dtype=BF16, name=f"cd_bwd_dil{dil}")
        parts.append([in_classes(a, dil) for a in grads])
    dqd, dkd, dvd = _attn_bwd((qd3, 0), (kd3, 0), (vd3, 0), (dyb3, 1), _seq3(lse_d, nb), _seq3(delta_d, nb), heads=8,
                              dh=HEAD_DIM, mode="band", max_dist=SWA_WINDOW - 1, dq_dtype=BF16, name="cd_bwd_swa")
    cos, sin = _rope_tables()
    per_seq = SEQ // ROW_TILE
    table_map = lambda i: (i % per_seq, 0)

    def unrope(q1, q2, q3, k1, k2, k3, v1, v2, v3, qd, kd8, vd8, cos, sin, gather):
        back = lambda x: _rotate(x.astype(F32), cos, -sin)
        kd, vd = _dot_exact(kd8, gather), _dot_exact(vd8, gather)
        return jnp.concatenate([back(q1 + q2 + q3), back(k1 + k2 + k3), v1 + v2 + v3, back(qd), back(kd), vd], axis=1)
    ins = [parts[b][t] for t in range(3) for b in range(3)] + [_flat2(dqd), _flat2(dkd), _flat2(dvd), cos, sin]
    dz, = _rowwise(unrope, ins, [_gqa_share().T], [(CD_IN, BF16)], name="cd_bwd_unrope",
                   row_maps=[None] * 12 + [table_map, table_map])
    dw_in_t, = _matmul(dz, hn, ta=True, outs=(BF16,),tm=384, name="cd_bwd_dwin")
    dh2, dh2b, dg = _norm_input_grad(dz, p["w_in_t"], h, dh, p["norm"], name="cd_bwd_dh")
    return dh2, dh2b, {"norm": dg, "w_in_t": dw_in_t, "sink": dsink[:, :8], "w_out": dw_out}


def _local_step(x, mem, target, w, complete=None, grads_ready=None):
    nb = x.shape[0]
    h = x.reshape(nb * SEQ, D_MODEL)
    mem2 = mem.reshape(nb * MEM_LEN, D_MODEL)
    tgt = target.reshape(nb * SEQ, D_MODEL)

    hn = _rms_fwd(h, w["ab"]["norm"], name="ab_norm")
    h, hn, s_ab = _ab_fwd(h, hn, w["ab"], nb, w["xa0"]["norm"], before_out=complete)
    h, hn, s_xa0 = _xa_fwd(h, hn, mem2, w["xa0"], "xa0", nb, w["mlp0"]["norm"])
    h, hn, s_mlp0 = _mlp_fwd(h, hn, w["mlp0"], "mlp0", w["cd"]["norm"])
    h, hn, s_cd = _cd_fwd(h, hn, w["cd"], nb, w["xa1"]["norm"])
    h, hn, s_xa1 = _xa_fwd(h, hn, mem2, w["xa1"], "xa1", nb, w["mlp1"]["norm"])
    h, _, s_mlp1 = _mlp_fwd(h, hn, w["mlp1"], "mlp1", None)

    dh, dhb, loss, dg_final = _loss_and_grad(h, tgt, w["final_norm"], name="loss")
    grads = {"final_norm": dg_final}
    dh, dhb, grads["mlp1"] = _mlp_bwd(dh, dhb, w["mlp1"], s_mlp1, "mlp1")
    dh, dhb, grads["xa1"] = _xa_bwd(dh, dhb, mem2, w["xa1"], s_xa1, "xa1", nb)
    dh, dhb, grads["cd"] = _cd_bwd(dh, dhb, w["cd"], s_cd, nb)
    if grads_ready is not None:
        grads_ready(0, grads)
    dh, dhb, grads["mlp0"] = _mlp_bwd(dh, dhb, w["mlp0"], s_mlp0, "mlp0")
    dh, dhb, grads["xa0"] = _xa_bwd(dh, dhb, mem2, w["xa0"], s_xa0, "xa0", nb)
    if grads_ready is not None:
        grads_ready(1, grads)
    dh, dhb, grads["ab"] = _ab_bwd(dh, dhb, w["ab"], s_ab, nb)
    return loss, dh.reshape(nb, SEQ, D_MODEL), grads


ANY = pl.BlockSpec(memory_space=pl.ANY)


def _place():
    x, y, c = lax.axis_index("x"), lax.axis_index("y"), lax.axis_index("c")
    return x, y, c, [(1 - x, y), (x, 1 - y), (1 - x, 1 - y)]


def _gather_chips(shard):
    half = shard.shape[0] // 2

    def body(src, out, send_sems, recv_sems):
        x, y, c, chips = _place()
        sibling = (x, y, 1 - c)

        def rows(slot, core):
            return out.at[slot, pl.ds(core * half, half)]

        def copy(k, src_ref, dst_ref, to):
            return pltpu.make_async_remote_copy(src_ref=src_ref, dst_ref=dst_ref, send_sem=send_sems.at[k],
                                                recv_sem=recv_sems.at[k], device_id=to, device_id_type=MESH)

        first = [copy(k, src.at[pl.ds(c * half, half)], rows(2 * x + y, c), (px, py, c)) for k, (px, py) in enumerate(chips)]
        for cp in first:
            cp.start()
        passed = [copy(3 + k, rows(2 * px + py, c), rows(2 * px + py, c), sibling) for k, (px, py) in enumerate(chips)]
        for k, (px, py) in enumerate(chips):
            copy(k, src.at[pl.ds(c * half, half)], rows(2 * px + py, c), (px, py, c)).wait_recv()
            passed[k].start()
        for k, (px, py) in enumerate(chips):
            copy(3 + k, rows(2 * px + py, 1 - c), rows(2 * px + py, 1 - c), sibling).wait_recv()
        for cp in first + passed:
            cp.wait_send()

    return pl.pallas_call(
        body, out_shape=jax.ShapeDtypeStruct((N_CHIPS,) + shard.shape, shard.dtype), in_specs=[ANY], out_specs=ANY,
        scratch_shapes=[pltpu.SemaphoreType.DMA((6,)), pltpu.SemaphoreType.DMA((6,))], name="gather_chips",
    )(shard)


HBM = pl.BlockSpec(memory_space=pltpu.HBM)
SEM = pl.BlockSpec(memory_space=pltpu.SEMAPHORE)
SIDE_EFFECT = pltpu.SideEffectType.DATAFLOW_SIDE_EFFECTING


def _chip_copies(src, land, send_sems, recv_sems):
    half = src.shape[0] // 2
    x, y, c, chips = _place()

    def copy(k, slot, to):
        return pltpu.make_async_remote_copy(src_ref=src.at[pl.ds(c * half, half)], dst_ref=land.at[slot, pl.ds(c * half, half)],
                                            send_sem=send_sems[k], recv_sem=recv_sems[k], device_id=to, device_id_type=MESH)
    out = [copy(k, 2 * x + y, (px, py, c)) for k, (px, py) in enumerate(chips)]
    back = [copy(k, 2 * px + py, (px, py, c)) for k, (px, py) in enumerate(chips)]
    return out, back


def _gather_start(shard, after):
    def body(src, land, *rest):
        rest = rest[len(after):]
        sems, token = rest[:6], rest[8]
        out, _ = _chip_copies(src, land, sems[:3], sems[3:])
        for cp in out:
            cp.start()
        token[...] = jnp.zeros(token.shape, F32)

    sem = pltpu.SemaphoreType.DMA(())
    land_shape = (N_CHIPS,) + shard.shape
    return pl.pallas_call(
        body, name="gather_start",
        out_shape=(sem,) * 6 + (pltpu.HBM(shard.shape, shard.dtype), pltpu.HBM(land_shape, shard.dtype),
                                jax.ShapeDtypeStruct((8, LANES), F32)),
        in_specs=(HBM, HBM) + (pl.BlockSpec(memory_space=pl.ANY),) * len(after),
        out_specs=(SEM,) * 6 + (HBM, HBM, pl.BlockSpec(memory_space=pltpu.VMEM)),
        input_output_aliases={0: 6, 1: 7}, compiler_params=pltpu.CompilerParams(has_side_effects=SIDE_EFFECT),
    )(pltpu.with_memory_space_constraint(shard, pltpu.HBM),
      pltpu.with_memory_space_constraint(lax.empty(land_shape, shard.dtype), pltpu.HBM), *after)


def _gather_wait(started, after):
    sems, src_thru, land_thru = started[:6], started[6], started[7]

    def body(src, land, *rest):
        out, back = _chip_copies(src, land, rest[:3], rest[3:6])
        for cp in out:
            cp.wait_send()
        for cp in back:
            cp.wait_recv()

    return pl.pallas_call(
        body, name="gather_wait",
        out_shape=(pltpu.HBM(src_thru.shape, src_thru.dtype), pltpu.HBM(land_thru.shape, land_thru.dtype)),
        in_specs=(HBM, HBM) + (SEM,) * 6 + (pl.BlockSpec(memory_space=pl.ANY),), out_specs=(HBM, HBM),
        input_output_aliases={0: 0, 1: 1}, compiler_params=pltpu.CompilerParams(has_side_effects=SIDE_EFFECT),
    )(src_thru, land_thru, *sems, after)


def _gather_pass(land):
    half = land.shape[1] // 2

    def body(land_in, land_out, send_sems, recv_sems):
        x, y, c, chips = _place()

        def copy(k, slot, core):
            rows = land_out.at[slot, pl.ds(core * half, half)]
            return pltpu.make_async_remote_copy(src_ref=rows, dst_ref=rows, send_sem=send_sems.at[k], recv_sem=recv_sems.at[k],
                                                device_id=(x, y, 1 - c), device_id_type=MESH)
        sends = [copy(k, 2 * px + py, c) for k, (px, py) in enumerate(chips)]
        for cp in sends:
            cp.start()
        for k, (px, py) in enumerate(chips):
            copy(k, 2 * px + py, 1 - c).wait_recv()
        for cp in sends:
            cp.wait_send()

    return pl.pallas_call(
        body, out_shape=jax.ShapeDtypeStruct(land.shape, land.dtype), in_specs=[ANY], out_specs=ANY,
        scratch_shapes=[pltpu.SemaphoreType.DMA((3,)), pltpu.SemaphoreType.DMA((3,))], input_output_aliases={0: 0},
        name="gather_pass",
    )(land)


def _swap_cores(block, *, name, half_rows=None):
    shape = block.shape if half_rows is None else (block.shape[0], half_rows, block.shape[2])

    def body(src, out, send_sem, recv_sem):
        x, y, c, _ = _place()
        part = src if half_rows is None else src.at[:, pl.ds((1 - c) * half_rows, half_rows)]
        cp = pltpu.make_async_remote_copy(src_ref=part, dst_ref=out, send_sem=send_sem, recv_sem=recv_sem,
                                          device_id=(x, y, 1 - c), device_id_type=MESH)
        cp.start()
        cp.wait()

    return pl.pallas_call(
        body, out_shape=jax.ShapeDtypeStruct(shape, block.dtype), in_specs=[ANY], out_specs=ANY,
        scratch_shapes=[pltpu.SemaphoreType.DMA(()), pltpu.SemaphoreType.DMA(())], name=name,
    )(block)


def _scatter_copies(parts, land, send_sems, recv_sems):
    x, y, c, chips = _place()
    return [pltpu.make_async_remote_copy(src_ref=parts.at[2 * px + py], dst_ref=land.at[k], send_sem=send_sems[k],
                                         recv_sem=recv_sems[k], device_id=(px, py, c), device_id_type=MESH)
            for k, (px, py) in enumerate(chips)]


def _scatter_start(parts, tag):
    def body(src, land, *rest):
        for cp in _scatter_copies(src, land, rest[:3], rest[3:6]):
            cp.start()
        rest[8][...] = jnp.zeros(rest[8].shape, F32)

    sem = pltpu.SemaphoreType.DMA(())
    land_shape = (3,) + parts.shape[1:]
    return pl.pallas_call(
        body, name=f"scatter_start_{tag}",
        out_shape=(sem,) * 6 + (pltpu.HBM(parts.shape, parts.dtype), pltpu.HBM(land_shape, parts.dtype),
                                jax.ShapeDtypeStruct((8, LANES), F32)),
        in_specs=(HBM, HBM), out_specs=(SEM,) * 6 + (HBM, HBM, pl.BlockSpec(memory_space=pltpu.VMEM)),
        input_output_aliases={0: 6, 1: 7}, compiler_params=pltpu.CompilerParams(has_side_effects=SIDE_EFFECT),
    )(pltpu.with_memory_space_constraint(parts, pltpu.HBM),
      pltpu.with_memory_space_constraint(lax.empty(land_shape, parts.dtype), pltpu.HBM))


def _scatter_wait(started, after, tag):
    def body(src, land, *rest):
        for cp in _scatter_copies(src, land, rest[:3], rest[3:6]):
            cp.wait_send()
            cp.wait_recv()

    src_thru, land_thru = started[6], started[7]
    return pl.pallas_call(
        body, name=f"scatter_wait_{tag}",
        out_shape=(pltpu.HBM(src_thru.shape, src_thru.dtype), pltpu.HBM(land_thru.shape, land_thru.dtype)),
        in_specs=(HBM, HBM) + (SEM,) * 6 + (pl.BlockSpec(memory_space=pl.ANY),), out_specs=(HBM, HBM),
        input_output_aliases={0: 0, 1: 1}, compiler_params=pltpu.CompilerParams(has_side_effects=SIDE_EFFECT),
    )(src_thru, land_thru, *started[:6], after)[1]


def _sum_all_devices(vec):
    rows = vec.shape[0]

    def body(x_ref, total_ref, all_ref, send_sems, recv_sems, local_sem):
        x, y, c, chips = _place()
        me, sibling = (x, y, c), (x, y, 1 - c)

        def slot(px, py, pc):
            return all_ref.at[4 * px + 2 * py + pc]

        def copy(k, block, to, src=None):
            return pltpu.make_async_remote_copy(src_ref=slot(*block) if src is None else src, dst_ref=slot(*block),
                                                send_sem=send_sems.at[k], recv_sem=recv_sems.at[k], device_id=to,
                                                device_id_type=MESH)

        mine = pltpu.make_async_copy(x_ref, slot(*me), local_sem)
        mine.start()
        first = [copy(0, me, sibling, src=x_ref)] + [copy(1 + j, me, (*chip, c), src=x_ref) for j, chip in enumerate(chips)]
        for cp in first:
            cp.start()
        passed = [copy(4 + j, (*chip, c), sibling) for j, chip in enumerate(chips)]
        for j, chip in enumerate(chips):
            copy(1 + j, (*chip, c), me).wait_recv()
            passed[j].start()
        copy(0, sibling, me).wait_recv()
        for j, chip in enumerate(chips):
            copy(4 + j, (*chip, 1 - c), me).wait_recv()
        for cp in first + passed:
            cp.wait_send()
        mine.wait()
        acc = all_ref[0]
        for d in range(1, 8):
            acc = acc + all_ref[d]
        total_ref[...] = acc

    vmem = pl.BlockSpec(memory_space=pltpu.VMEM)
    return pl.pallas_call(
        body, out_shape=[jax.ShapeDtypeStruct((rows, LANES), F32), jax.ShapeDtypeStruct((8, rows, LANES), F32)],
        in_specs=[vmem], out_specs=[vmem, vmem],
        scratch_shapes=[pltpu.SemaphoreType.DMA((7,)), pltpu.SemaphoreType.DMA((7,)), pltpu.SemaphoreType.DMA(())],
        name="sum_all_devices", compiler_params=pltpu.CompilerParams(vmem_limit_bytes=VMEM_LIMIT_BYTES),
    )(vec)[0]


PACK_COLS = 1024
BIG = (("mlp_w_up", 2, 1024, True), ("mlp_w_down", 2, 1024, False), ("xa_w_kv", 2, 512, True), ("xa_w_q", 2, 256, False),
       ("xa_w_o", 2, 256, False), ("ab_w_out", 1, 256, False), ("cd_w_out", 1, 256, False), ("cd_w_in", 1, 576, True),
       ("ab_w_in", 1, 642, True))
ENTRIES = tuple((name, layer, rows, transposed) for name, layers, rows, transposed in BIG for layer in range(layers))
FIRST_ENTRIES = tuple(e for e in ENTRIES if e[0] == "ab_w_in")
LATER_ENTRIES = tuple(e for e in ENTRIES if e[0] != "ab_w_in")


def _tile_rows(entry):
    return -(-entry[2] // 16) * 16


def _padded_rows(entries):
    return -(-sum(_tile_rows(e) for e in entries) // 32) * 32


SMALL = (("ab_norm", (1, 1024)), ("ab_conv_w", (1, 4, 512)), ("ab_conv_b", (1, 512)), ("lru_w_a", (1, 8, 64, 64)),
         ("lru_b_a", (1, 512)), ("lru_w_i", (1, 8, 64, 64)), ("lru_b_i", (1, 512)), ("lru_lambda", (1, 512)),
         ("fox_b_f", (1, 8)), ("cd_norm", (1, 1024)), ("cd_sink", (1, 8)), ("xa_norm", (2, 1024)),
         ("xa_mem_norm", (2, 1024)), ("mlp_norm", (2, 1024)), ("final_norm", (1024,)), ("loss", (1,)))
SPLIT_SMALL = ("ab_conv_w", "cd_norm")
SPLIT_SMALL_ROWS = 16
assert _padded_rows(FIRST_ENTRIES) - SPLIT_SMALL_ROWS >= sum(_tile_rows(e) for e in FIRST_ENTRIES)


def _pack_rows(parts, entries, dtype):
    lead = parts[0].shape[:-2]
    zeros = lambda rows: jnp.zeros(lead + (rows, PACK_COLS), dtype)
    pieces = [jnp.pad(p.astype(dtype), ((0, 0),) * len(lead) + ((0, _tile_rows(e) - e[2]), (0, 0))) for p, e in zip(parts, entries)]
    tail = _padded_rows(entries) - sum(_tile_rows(e) for e in entries)
    return jnp.concatenate(pieces + ([zeros(tail)] if tail else []), axis=len(lead))


def _unpack_rows(packed, entries):
    out, r0 = [], 0
    for e in entries:
        out.append(packed[..., r0:r0 + e[2], :])
        r0 += _tile_rows(e)
    return out


def _shard_rows(w, entries):
    return [(w[name][layer].T if transposed else w[name][layer]) for name, layer, _, transposed in entries]


def _shard_blocks(rows):
    out = {}
    for (name, layer, _, transposed), r in zip(ENTRIES, rows):
        out.setdefault(name, []).append(r.T if transposed else r)
    return {name: jnp.stack(layers) for name, layers in out.items()}


def _pack_small(vals):
    flat = jnp.concatenate([vals[name].astype(F32).reshape(-1) for name, _ in SMALL])
    size = -(-flat.shape[0] // (8 * LANES)) * (8 * LANES)
    return jnp.pad(flat, (0, size - flat.shape[0])).reshape(-1, LANES)


def _unpack_small(packed):
    flat, out, at = packed.reshape(-1), {}, 0
    for name, shape in SMALL:
        out[name] = flat[at:at + math.prod(shape)].reshape(shape)
        at += math.prod(shape)
    return out


def _chip_part(full, chip):
    width = full.shape[-1] // N_CHIPS
    return lax.dynamic_slice_in_dim(full, chip * width, width, axis=full.ndim - 1)


def _chip_embed(part, chip):
    full = jnp.zeros(part.shape[:-1] + (part.shape[-1] * N_CHIPS,), part.dtype)
    return lax.dynamic_update_slice_in_dim(full, part, chip * part.shape[-1], axis=part.ndim - 1)


SUBLAYER_KEYS = {"ab_w_in": ("ab", "w_in_t"), "ab_w_out": ("ab", "w_out"), "cd_w_in": ("cd", "w_in_t"), "cd_w_out": ("cd", "w_out"),
                 "xa_w_q": ("xa", "w_q"), "xa_w_kv": ("xa", "w_kv_t"), "xa_w_o": ("xa", "w_o"), "mlp_w_up": ("mlp", "w_up_t"),
                 "mlp_w_down": ("mlp", "w_down")}


def _sublayer(name, layer):
    block, leaf = SUBLAYER_KEYS[name]
    return (block if block in ("ab", "cd") else f"{block}{layer}"), leaf


def _set_big(params, full_rows):
    for (name, layer), rows in full_rows.items():
        block, leaf = _sublayer(name, layer)
        if name == "ab_w_in":
            rows = jnp.concatenate([rows, jnp.zeros((LANES - 8, PACK_COLS), rows.dtype)])
        params[block][leaf] = rows


def _build_params(full_rows, w):
    pad = lambda a: jnp.pad(a, ((0, 0), (0, LANES - a.shape[1])))
    params = {
        "ab": dict(norm=w["ab_norm"], conv_w=w["ab_conv_w"][0], conv_b=w["ab_conv_b"], w_a=_block_diag(w["lru_w_a"][0]),
                   b_a=w["lru_b_a"], w_i=_block_diag(w["lru_w_i"][0]), b_i=w["lru_b_i"], lam=w["lru_lambda"],
                   b_f=w["fox_b_f"].reshape(8, 1)),
        "cd": dict(norm=w["cd_norm"], sink=pad(w["cd_sink"])),
        "final_norm": w["final_norm"].reshape(1, D_MODEL),
    }
    for layer in range(2):
        params[f"xa{layer}"] = dict(norm=w["xa_norm"][layer:layer + 1], mem_norm=w["xa_mem_norm"][layer:layer + 1])
        params[f"mlp{layer}"] = dict(norm=w["mlp_norm"][layer:layer + 1])
    _set_big(params, full_rows)
    return params


def _grad_rows(g, entries=ENTRIES):
    out = []
    for name, layer, _, _ in entries:
        block, leaf = _sublayer(name, layer)
        out.append(g[block][leaf])
    return out


def _reduce_group(entry):
    name, layer = entry[0], entry[1]
    if name.startswith("ab_"):
        return 2
    return 0 if layer == 1 or name.startswith("cd_") else 1


REDUCE_GROUPS = tuple(tuple(e for e in ENTRIES if _reduce_group(e) == group) for group in range(3))


def _reduce_tile(half):
    return max(t for t in range(16, 1025, 16) if half % t == 0)


def _reduce_start(grads_by_chip, core, chip, tag):
    half = grads_by_chip.shape[1] // 2
    tile = _reduce_tile(half)
    steps = half // tile
    place = jnp.stack([core, chip]).astype(jnp.int32)
    got = _swap_cores(grads_by_chip, name=f"swap_cores_{tag}", half_rows=half)
    block = (1, tile, PACK_COLS)

    def sum_cores(place_ref, mine_ref, got_ref, f32_ref, bf16_ref):
        total = mine_ref[...].astype(F32) + got_ref[...].astype(F32)
        f32_ref[...] = total
        bf16_ref[...] = total.astype(BF16)

    pair32, pair16 = pl.pallas_call(
        sum_cores, name=f"sum_cores_{tag}",
        grid_spec=pltpu.PrefetchScalarGridSpec(
            num_scalar_prefetch=1, grid=(N_CHIPS, steps),
            in_specs=[pl.BlockSpec(block, lambda s, i, place_ref: (s, place_ref[0] * steps + i, 0)),
                      pl.BlockSpec(block, lambda s, i, place_ref: (s, i, 0))],
            out_specs=[pl.BlockSpec(block, lambda s, i, place_ref: (s, i, 0))] * 2),
        out_shape=[jax.ShapeDtypeStruct((N_CHIPS, half, PACK_COLS), F32), jax.ShapeDtypeStruct((N_CHIPS, half, PACK_COLS), BF16)],
        compiler_params=_params("arbitrary", "arbitrary"),
    )(place, grads_by_chip, got)
    return pair32, _scatter_start(pair16, tag), place


def _reduce_finish(state, core, tag, after):
    pair32, started, place = state
    half = pair32.shape[1]
    tile = _reduce_tile(half)
    landed = _scatter_wait(started, after, tag)

    def sum_chips(place_ref, own_ref, landed_ref, out_ref):
        out_ref[...] = own_ref[0] + landed_ref[0].astype(F32) + landed_ref[1].astype(F32) + landed_ref[2].astype(F32)

    done = pl.pallas_call(
        sum_chips, name=f"sum_chips_{tag}",
        grid_spec=pltpu.PrefetchScalarGridSpec(
            num_scalar_prefetch=1, grid=(half // tile,),
            in_specs=[pl.BlockSpec((1, tile, PACK_COLS), lambda i, place_ref: (place_ref[1], i, 0)),
                      pl.BlockSpec((3, tile, PACK_COLS), lambda i, place_ref: (0, i, 0))],
            out_specs=pl.BlockSpec((tile, PACK_COLS), lambda i, place_ref: (i, 0))),
        out_shape=jax.ShapeDtypeStruct((half, PACK_COLS), F32), compiler_params=_params("arbitrary"),
    )(place, pair32, landed)
    theirs = _swap_cores(done, name=f"share_halves_{tag}")
    return jnp.where(core == 0, jnp.concatenate([done, theirs]), jnp.concatenate([theirs, done]))


def kernel(x, mem, ab_norm, ab_w_in, ab_conv_w, ab_conv_b, lru_w_a, lru_b_a, lru_w_i, lru_b_i, lru_lambda, fox_b_f, ab_w_out, cd_norm, cd_w_in, cd_sink, cd_w_out, xa_norm, xa_mem_norm, xa_w_q, xa_w_kv, xa_w_o, mlp_norm, mlp_w_up, mlp_w_down, final_norm, loss_target, m_ab_norm, m_ab_w_in, m_ab_conv_w, m_ab_conv_b, m_lru_w_a, m_lru_b_a, m_lru_w_i, m_lru_b_i, m_lru_lambda, m_fox_b_f, m_ab_w_out, m_cd_norm, m_cd_w_in, m_cd_sink, m_cd_w_out, m_xa_norm, m_xa_mem_norm, m_xa_w_q, m_xa_w_kv, m_xa_w_o, m_mlp_norm, m_mlp_w_up, m_mlp_w_down, m_final_norm, v_ab_norm, v_ab_w_in, v_ab_conv_w, v_ab_conv_b, v_lru_w_a, v_lru_b_a, v_lru_w_i, v_lru_b_i, v_lru_lambda, v_fox_b_f, v_ab_w_out, v_cd_norm, v_cd_w_in, v_cd_sink, v_cd_w_out, v_xa_norm, v_xa_mem_norm, v_xa_w_q, v_xa_w_kv, v_xa_w_o, v_mlp_norm, v_mlp_w_up, v_mlp_w_down, v_final_norm):
    given = dict(locals())
    weight_names = [name for name, _ in SMALL if name != "loss"] + [name for name, _, _, _ in BIG]
    w = {name: given[name] for name in weight_names}
    chip = 2 * lax.axis_index("x") + lax.axis_index("y")
    core = lax.axis_index("c")

    slot = lax.broadcasted_iota(jnp.int32, (N_CHIPS, 1, 1), 0)

    def whole(entries, mine, gathered):
        return {(name, layer): jnp.where(slot == chip, own[None], got).reshape(N_CHIPS * rows, PACK_COLS)
                for (name, layer, rows, _), own, got in zip(entries, _unpack_rows(mine, entries), _unpack_rows(gathered, entries))}

    bits = lambda a: lax.bitcast_convert_type(a.reshape(-1), BF16).reshape(-1)
    spare = _padded_rows(FIRST_ENTRIES) - SPLIT_SMALL_ROWS
    small_rows = jnp.zeros((SPLIT_SMALL_ROWS, PACK_COLS), BF16)
    small_rows = small_rows.at[0].set(bits(w["ab_conv_w"])).at[1, :2 * w["cd_norm"].size].set(bits(w["cd_norm"]))
    mine_first = _pack_rows(_shard_rows(w, FIRST_ENTRIES), FIRST_ENTRIES, BF16).at[spare:].set(small_rows)
    mine_later = _pack_rows(_shard_rows(w, LATER_ENTRIES), LATER_ENTRIES, BF16)
    first = _gather_chips(mine_first)
    floats = lambda a: lax.bitcast_convert_type(a.reshape(a.shape[:-1] + (-1, 2)), F32)
    all_small = jnp.where(slot == chip, floats(mine_first[None, spare:]), floats(first[:, spare:]))
    taps, per_chip = w["ab_conv_w"].shape[1:]
    conv_w_full = all_small[:, 0].reshape(N_CHIPS, taps, per_chip).transpose(1, 0, 2).reshape(1, taps, N_CHIPS * per_chip)
    cd_norm_full = all_small[:, 1, :w["cd_norm"].size].reshape(1, -1)
    started = _gather_start(mine_later, after=(first,))
    params = _build_params(whole(FIRST_ENTRIES, mine_first, first), {**w, "ab_conv_w": conv_w_full, "cd_norm": cd_norm_full})
    params["ab"]["norm"] = params["ab"]["norm"] + started[8][0, 0]

    def complete(h):
        mine, landed = _gather_wait(started, after=h)
        _set_big(params, whole(LATER_ENTRIES, mine, _gather_pass(landed)))

    reducing = {}

    def grads_ready(group, g):
        entries = REDUCE_GROUPS[group]
        by_chip = _pack_rows([r.reshape(N_CHIPS, e[2], PACK_COLS) for r, e in zip(_grad_rows(g, entries), entries)], entries, BF16)
        reducing[group] = _reduce_start(by_chip, core, chip, f"g{group}")
        if group < 2:
            block, leaf = ("mlp0", "w_down") if group == 0 else ("ab", "w_out")
            params[block][leaf] = params[block][leaf] + reducing[group][1][8][0, 0].astype(BF16)

    loss_part, grad_x, g = _local_step(x, mem, loss_target, params, complete, grads_ready)
    grads_ready(2, g)
    reduced, after = {}, reducing[2][1][8]
    for group, entries in enumerate(REDUCE_GROUPS):
        after = _reduce_finish(reducing[group], core, f"g{group}", after=after)
        reduced.update({(e[0], e[1]): r for e, r in zip(entries, _unpack_rows(after, entries))})
    grads = _shard_blocks([reduced[e[0], e[1]] for e in ENTRIES])
    pair = lambda key, leaf: jnp.stack([g[f"{key}0"][leaf], g[f"{key}1"][leaf]])

    small_local = {"ab_norm": g["ab"]["norm"], "ab_conv_w": g["ab"]["conv_w"], "ab_conv_b": g["ab"]["conv_b"],
                   "lru_w_a": g["ab"]["w_a"], "lru_b_a": g["ab"]["b_a"], "lru_w_i": g["ab"]["w_i"], "lru_b_i": g["ab"]["b_i"],
                   "lru_lambda": g["ab"]["lam"], "fox_b_f": g["ab"]["b_f"], "cd_norm": g["cd"]["norm"], "cd_sink": g["cd"]["sink"],
                   "xa_norm": pair("xa", "norm"), "xa_mem_norm": pair("xa", "mem_norm"), "mlp_norm": pair("mlp", "norm"),
                   "final_norm": g["final_norm"], "loss": loss_part[0, :1]}
    small_sum_packed = _sum_all_devices(_pack_small(small_local))
    small_sum = _unpack_small(small_sum_packed)
    loss = small_sum["loss"][0]

    delta, new_m, new_v = {}, {}, {}
    for name, _, _, _ in BIG:
        shape = w[name].shape
        flat = lambda a: a.reshape(-1, shape[-1])
        d, m2, v2 = _adamw(flat(w[name]), flat(grads[name]), flat(given["m_" + name]), flat(given["v_" + name]), name=f"adamw_{name}")
        delta[name], new_m[name], new_v[name] = d.reshape(shape), m2.reshape(shape), v2.reshape(shape)

    def small_state(prefix):
        vals = {name: (given[prefix + name] if name != "loss" else jnp.zeros((1,), F32)) for name, _ in SMALL}
        for name in SPLIT_SMALL:
            vals[name] = _chip_embed(vals[name], chip)
        return _pack_small(vals)
    packed = _adamw(small_state(""), small_sum_packed, small_state("m_"), small_state("v_"), name="adamw_small")
    for store, vals in zip((delta, new_m, new_v), packed):
        for name, val in _unpack_small(vals).items():
            store[name] = _chip_part(val, chip) if name in SPLIT_SMALL else val
    for name in SPLIT_SMALL:
        small_sum[name] = _chip_part(small_sum[name], chip)
    grads.update({name: small_sum[name] for name, _ in SMALL})

    order = ["ab_norm", "ab_w_in", "ab_conv_w", "ab_conv_b", "lru_w_a", "lru_b_a", "lru_w_i", "lru_b_i", "lru_lambda", "fox_b_f",
             "ab_w_out", "cd_norm", "cd_w_in", "cd_sink", "cd_w_out", "xa_norm", "xa_mem_norm", "xa_w_q", "xa_w_kv", "xa_w_o",
             "mlp_norm", "mlp_w_up", "mlp_w_down", "final_norm"]
    return (loss, grad_x, *[grads[n] for n in order], *[delta[n] for n in order], *[new_m[n] for n in order],
            *[new_v[n] for n in order])
```
